```python
import math
import jax, jax.numpy as jnp
from jax import lax
import numpy as np

D_MODEL = 1024
BATCH = 8
SEQ = 4096
DEPTH = 1

DN_HEADS = 4
DN_HEAD_DIM = 128
DN_WIDTH = DN_HEADS * DN_HEAD_DIM
DN_QKV_WIDTH = 3 * DN_WIDTH
DN_CONV = 4
DN_CHUNK = 64
SWA_HEADS = 8
SWA_KV_HEADS = 2
SWA_HEAD_DIM = 64
SWA_GROUP = SWA_HEADS // SWA_KV_HEADS
SWA_WIDTH = SWA_HEADS * SWA_HEAD_DIM
SWA_KV_WIDTH = SWA_KV_HEADS * SWA_HEAD_DIM
WINDOW = 128
SWA_BLOCK = 128
REL_BUCKETS = 32
REL_MAX_DIST = 128
N_BRANCHES = 2
D_FF = ((8 * D_MODEL // 3 + 255) // 256) * 256
D_IN = DN_QKV_WIDTH + DN_WIDTH + 2 * DN_HEADS + SWA_WIDTH + 2 * SWA_KV_WIDTH + N_BRANCHES * D_MODEL
EPS = 1e-6

kernel_name = "hybrid_gdn_swa_gated_merge"


def rms_norm(x, gain):
    xf = x.astype(jnp.float32)
    y = xf * lax.rsqrt(jnp.mean(xf * xf, axis=-1, keepdims=True) + EPS)
    return (y * gain.astype(jnp.float32)).astype(x.dtype)


def l2_norm(x):
    xf = x.astype(jnp.float32)
    return xf * lax.rsqrt(jnp.sum(xf * xf, axis=-1, keepdims=True) + EPS)


def causal_dwconv(x, w):
    k = w.shape[0]
    return lax.conv_general_dilated(
        x, w[:, None, :], window_strides=(1,), padding=((k - 1, 0),),
        dimension_numbers=("NWC", "WIO", "NWC"), feature_group_count=x.shape[-1])


def chunk_gated_delta_rule(q, k, v, g, beta):
    B, S, H, DK = q.shape
    DV = v.shape[-1]
    C = DN_CHUNK
    NC = S // C
    q = q * (DK ** -0.5)

    def to_chunks(t):
        return t.reshape(B, NC, C, H, t.shape[-1]).transpose(0, 3, 1, 2, 4)

    qc, kc, vc = to_chunks(q), to_chunks(k), to_chunks(v)
    gc = jnp.cumsum(g.reshape(B, NC, C, H).transpose(0, 3, 1, 2), axis=-1)
    bc = beta.reshape(B, NC, C, H).transpose(0, 3, 1, 2)[..., None]
    k_beta = kc * bc
    v_beta = vc * bc

    idx = jnp.arange(C)
    lower_incl = idx[:, None] >= idx[None, :]
    strict_lower = idx[:, None] > idx[None, :]
    decay = jnp.exp(jnp.where(lower_incl, gc[..., :, None] - gc[..., None, :], -jnp.inf))

    a_mat = jnp.where(strict_lower, jnp.einsum("bhncd,bhnsd->bhncs", k_beta, kc) * decay, 0.0)
    lhs = a_mat + jnp.eye(C, dtype=a_mat.dtype)
    rhs = jnp.concatenate([v_beta, k_beta * jnp.exp(gc)[..., None]], axis=-1)
    sol = lax.linalg.triangular_solve(lhs, rhs, left_side=True, lower=True, unit_diagonal=True)
    u, w = sol[..., :DV], sol[..., DV:]
    qk = jnp.einsum("bhncd,bhnsd->bhncs", qc, kc) * decay

    def step(state, inp):
        q_i, k_i, u_i, w_i, g_i, qk_i = inp
        v_new = u_i - jnp.einsum("bhck,bhkv->bhcv", w_i, state)
        o_i = (jnp.einsum("bhck,bhkv->bhcv", q_i * jnp.exp(g_i)[..., None], state)
               + jnp.einsum("bhcs,bhsv->bhcv", qk_i, v_new))
        g_last = g_i[..., -1]
        k_dec = k_i * jnp.exp(g_last[..., None] - g_i)[..., None]
        state = state * jnp.exp(g_last)[..., None, None] + jnp.einsum("bhck,bhcv->bhkv", k_dec, v_new)
        return state, o_i

    xs = tuple(jnp.moveaxis(t, 2, 0) for t in (qc, kc, u, w, gc, qk))
    state0 = jnp.zeros((B, H, DK, DV), jnp.float32)
    _, o = lax.scan(step, state0, xs)
    return o.transpose(1, 0, 3, 2, 4).reshape(B, S, H, DV)


def gated_deltanet_branch(qkv, z, b_raw, a_raw, conv_w, a_log, dt_bias, out_gain):
    B, S, _ = qkv.shape
    qkv = jax.nn.silu(causal_dwconv(qkv, conv_w))
    q, k, v = jnp.split(qkv, 3, axis=-1)
    q = l2_norm(q.reshape(B, S, DN_HEADS, DN_HEAD_DIM))
    k = l2_norm(k.reshape(B, S, DN_HEADS, DN_HEAD_DIM))
    v = v.reshape(B, S, DN_HEADS, DN_HEAD_DIM).astype(jnp.float32)
    beta = jax.nn.sigmoid(b_raw.astype(jnp.float32))
    g = -jnp.exp(a_log.astype(jnp.float32)) * jax.nn.softplus(a_raw.astype(jnp.float32) + dt_bias.astype(jnp.float32))
    o = chunk_gated_delta_rule(q, k, v, g, beta)
    o = rms_norm(o, out_gain) * jax.nn.silu(z.reshape(B, S, DN_HEADS, DN_HEAD_DIM).astype(jnp.float32))
    return o.reshape(B, S, DN_WIDTH).astype(qkv.dtype)


def t5_causal_bucket(dist):
    n = jnp.maximum(dist, 0)
    max_exact = REL_BUCKETS // 2
    nf = jnp.maximum(n, 1).astype(jnp.float32)
    large = max_exact + (jnp.log(nf / max_exact) / math.log(REL_MAX_DIST / max_exact)
                         * (REL_BUCKETS - max_exact)).astype(jnp.int32)
    large = jnp.minimum(large, REL_BUCKETS - 1)
    return jnp.where(n < max_exact, n, large)


def sliding_window_branch(q, k, v, q_gain, k_gain, sinks, rel_bias):
    B, S, _ = q.shape
    NB = S // SWA_BLOCK
    q = rms_norm(q.reshape(B, S, SWA_HEADS, SWA_HEAD_DIM), q_gain)
    k = rms_norm(k.reshape(B, S, SWA_KV_HEADS, SWA_HEAD_DIM), k_gain)
    v = v.reshape(B, S, SWA_KV_HEADS, SWA_HEAD_DIM)
    qb = q.reshape(B, NB, SWA_BLOCK, SWA_KV_HEADS, SWA_GROUP, SWA_HEAD_DIM).astype(jnp.float32)

    def band(t):
        tp = jnp.pad(t, ((0, 0), (SWA_BLOCK, 0), (0, 0), (0, 0)))
        tp = tp.reshape(B, NB + 1, SWA_BLOCK, SWA_KV_HEADS, SWA_HEAD_DIM)
        return jnp.concatenate([tp[:, :-1], tp[:, 1:]], axis=2).astype(jnp.float32)

    kb, vb = band(k), band(v)
    logits = jnp.einsum("bnqkgd,bnskd->bnkgqs", qb, kb) * (SWA_HEAD_DIM ** -0.5)

    qi = jnp.arange(SWA_BLOCK)[:, None]
    kj = jnp.arange(2 * SWA_BLOCK)[None, :]
    dist = SWA_BLOCK + qi - kj
    in_window = (dist >= 0) & (dist < WINDOW)
    key_pos = (jnp.arange(NB)[:, None] - 1) * SWA_BLOCK + jnp.arange(2 * SWA_BLOCK)[None, :]
    mask = in_window[None, :, :] & (key_pos >= 0)[:, None, :]

    bias = rel_bias.astype(jnp.float32)[t5_causal_bucket(dist)]
    bias = bias.transpose(2, 0, 1).reshape(SWA_KV_HEADS, SWA_GROUP, SWA_BLOCK, 2 * SWA_BLOCK)
    logits = jnp.where(mask[None, :, None, None], logits + bias, -jnp.inf)

    sink = sinks.astype(jnp.float32).reshape(SWA_KV_HEADS, SWA_GROUP)[None, None, :, :, None, None]
    m = jnp.maximum(jnp.max(logits, axis=-1, keepdims=True), sink)
    p = jnp.exp(logits - m)
    denom = jnp.sum(p, axis=-1, keepdims=True) + jnp.exp(sink - m)
    out = jnp.einsum("bnkgqs,bnskd->bnqkgd", p / denom, vb)
    return out.reshape(B, S, SWA_WIDTH).astype(q.dtype)


def split_combined(p):
    sizes = (DN_QKV_WIDTH, DN_WIDTH, DN_HEADS, DN_HEADS, SWA_WIDTH, SWA_KV_WIDTH, SWA_KV_WIDTH,
             N_BRANCHES * D_MODEL)
    return jnp.split(p, np.cumsum(sizes)[:-1].tolist(), axis=-1)


def _fwd_setup_inputs(seed: int = 0) -> dict:
    key = jax.random.key(seed)
    ks = jax.random.split(key, 20)
    f32 = jnp.float32
    L = DEPTH

    def nrm(k, shape, scale):
        return jax.random.normal(k, shape, f32) * scale

    dt = jnp.exp(jax.random.uniform(ks[5], (L, DN_HEADS), f32, math.log(1e-3), math.log(1e-1)))
    return {
        "x": nrm(ks[0], (BATCH, SEQ, D_MODEL), 1.0),
        "attn_norm": 1.0 + nrm(ks[1], (L, D_MODEL), 0.02),
        "w_in": nrm(ks[2], (L, D_MODEL, D_IN), D_MODEL ** -0.5),
        "dn_conv": nrm(ks[3], (L, DN_CONV, DN_QKV_WIDTH), DN_CONV ** -0.5),
        "dn_a_log": jnp.log(jax.random.uniform(ks[4], (L, DN_HEADS), f32, 1.0, 16.0)),
        "dn_dt_bias": dt + jnp.log(-jnp.expm1(-dt)),
        "dn_out_norm": 1.0 + nrm(ks[6], (L, DN_HEAD_DIM), 0.02),
        "swa_q_norm": 1.0 + nrm(ks[7], (L, SWA_HEAD_DIM), 0.02),
        "swa_k_norm": 1.0 + nrm(ks[8], (L, SWA_HEAD_DIM), 0.02),
        "swa_sinks": nrm(ks[9], (L, SWA_HEADS), 0.5),
        "rel_bias": nrm(ks[10], (REL_BUCKETS, SWA_HEADS), 0.1),
        "w_branch_dn": nrm(ks[11], (L, DN_WIDTH, D_MODEL), DN_WIDTH ** -0.5),
        "w_branch_swa": nrm(ks[12], (L, SWA_WIDTH, D_MODEL), SWA_WIDTH ** -0.5),
        "w_out": nrm(ks[13], (L, D_MODEL, D_MODEL), D_MODEL ** -0.5),
        "ffn_norm": 1.0 + nrm(ks[14], (L, D_MODEL), 0.02),
        "w_gate": nrm(ks[15], (L, D_MODEL, D_FF), D_MODEL ** -0.5),
        "w_up": nrm(ks[16], (L, D_MODEL, D_FF), D_MODEL ** -0.5),
        "w_down": nrm(ks[17], (L, D_FF, D_MODEL), D_FF ** -0.5),
    }


def _fwd_reference(x, attn_norm, w_in, dn_conv, dn_a_log, dn_dt_bias, dn_out_norm, swa_q_norm,
              swa_k_norm, swa_sinks, rel_bias, w_branch_dn, w_branch_swa, w_out, ffn_norm,
              w_gate, w_up, w_down):
    B, S, _ = x.shape
    for l in range(DEPTH):
        h = rms_norm(x, attn_norm[l])
        proj = h @ w_in[l]
        dn_qkv, dn_z, dn_b, dn_a, sq, sk, sv, gate_raw = split_combined(proj)
        y_dn = gated_deltanet_branch(dn_qkv, dn_z, dn_b, dn_a, dn_conv[l], dn_a_log[l],
                                     dn_dt_bias[l], dn_out_norm[l])
        y_swa = sliding_window_branch(sq, sk, sv, swa_q_norm[l], swa_k_norm[l], swa_sinks[l], rel_bias)
        gates = jax.nn.sigmoid(gate_raw.astype(jnp.float32)).astype(x.dtype)
        gates = gates.reshape(B, S, N_BRANCHES, D_MODEL)
        merged = gates[:, :, 0] * (y_dn @ w_branch_dn[l]) + gates[:, :, 1] * (y_swa @ w_branch_swa[l])
        x = x + merged @ w_out[l]
        h2 = rms_norm(x, ffn_norm[l])
        x = x + (jax.nn.silu(h2 @ w_gate[l]) * (h2 @ w_up[l])) @ w_down[l]
    return x


import jax as _jax
import jax.numpy as _jnp

TWIN_FORMAT = 'train_step'
FWD_PARAMS = ['x', 'attn_norm', 'w_in', 'dn_conv', 'dn_a_log', 'dn_dt_bias', 'dn_out_norm', 'swa_q_norm', 'swa_k_norm', 'swa_sinks', 'rel_bias', 'w_branch_dn', 'w_branch_swa', 'w_out', 'ffn_norm', 'w_gate', 'w_up', 'w_down']
TWIN_WEIGHTS = ['attn_norm', 'w_in', 'dn_conv', 'dn_a_log', 'dn_dt_bias', 'dn_out_norm', 'swa_q_norm', 'swa_k_norm', 'swa_sinks', 'rel_bias', 'w_branch_dn', 'w_branch_swa', 'w_out', 'ffn_norm', 'w_gate', 'w_up', 'w_down']
TWIN_DIFF_INPUT = 'x'
TWIN_INPUTS = ['x', 'attn_norm', 'w_in', 'dn_conv', 'dn_a_log', 'dn_dt_bias', 'dn_out_norm', 'swa_q_norm', 'swa_k_norm', 'swa_sinks', 'rel_bias', 'w_branch_dn', 'w_branch_swa', 'w_out', 'ffn_norm', 'w_gate', 'w_up', 'w_down', 'loss_target', 'm_attn_norm', 'm_w_in', 'm_dn_conv', 'm_dn_a_log', 'm_dn_dt_bias', 'm_dn_out_norm', 'm_swa_q_norm', 'm_swa_k_norm', 'm_swa_sinks', 'm_rel_bias', 'm_w_branch_dn', 'm_w_branch_swa', 'm_w_out', 'm_ffn_norm', 'm_w_gate', 'm_w_up', 'm_w_down', 'v_attn_norm', 'v_w_in', 'v_dn_conv', 'v_dn_a_log', 'v_dn_dt_bias', 'v_dn_out_norm', 'v_swa_q_norm', 'v_swa_k_norm', 'v_swa_sinks', 'v_rel_bias', 'v_w_branch_dn', 'v_w_branch_swa', 'v_w_out', 'v_ffn_norm', 'v_w_gate', 'v_w_up', 'v_w_down']
TWIN_OUTPUTS = ['loss', 'grad_x', 'grad_attn_norm', 'grad_w_in', 'grad_dn_conv', 'grad_dn_a_log', 'grad_dn_dt_bias', 'grad_dn_out_norm', 'grad_swa_q_norm', 'grad_swa_k_norm', 'grad_swa_sinks', 'grad_rel_bias', 'grad_w_branch_dn', 'grad_w_branch_swa', 'grad_w_out', 'grad_ffn_norm', 'grad_w_gate', 'grad_w_up', 'grad_w_down', 'delta_attn_norm', 'delta_w_in', 'delta_dn_conv', 'delta_dn_a_log', 'delta_dn_dt_bias', 'delta_dn_out_norm', 'delta_swa_q_norm', 'delta_swa_k_norm', 'delta_swa_sinks', 'delta_rel_bias', 'delta_w_branch_dn', 'delta_w_branch_swa', 'delta_w_out', 'delta_ffn_norm', 'delta_w_gate', 'delta_w_up', 'delta_w_down', 'new_m_attn_norm', 'new_m_w_in', 'new_m_dn_conv', 'new_m_dn_a_log', 'new_m_dn_dt_bias', 'new_m_dn_out_norm', 'new_m_swa_q_norm', 'new_m_swa_k_norm', 'new_m_swa_sinks', 'new_m_rel_bias', 'new_m_w_branch_dn', 'new_m_w_branch_swa', 'new_m_w_out', 'new_m_ffn_norm', 'new_m_w_gate', 'new_m_w_up', 'new_m_w_down', 'new_v_attn_norm', 'new_v_w_in', 'new_v_dn_conv', 'new_v_dn_a_log', 'new_v_dn_dt_bias', 'new_v_dn_out_norm', 'new_v_swa_q_norm', 'new_v_swa_k_norm', 'new_v_swa_sinks', 'new_v_rel_bias', 'new_v_w_branch_dn', 'new_v_w_branch_swa', 'new_v_w_out', 'new_v_ffn_norm', 'new_v_w_gate', 'new_v_w_up', 'new_v_w_down']
TWIN_LEAF_KINDS = {'loss': 'loss', 'grad_x': 'grad_x', 'grad_attn_norm': 'grad_w', 'grad_w_in': 'grad_w', 'grad_dn_conv': 'grad_w', 'grad_dn_a_log': 'grad_w', 'grad_dn_dt_bias': 'grad_w', 'grad_dn_out_norm': 'grad_w', 'grad_swa_q_norm': 'grad_w', 'grad_swa_k_norm': 'grad_w', 'grad_swa_sinks': 'grad_w', 'grad_rel_bias': 'grad_w', 'grad_w_branch_dn': 'grad_w', 'grad_w_branch_swa': 'grad_w', 'grad_w_out': 'grad_w', 'grad_ffn_norm': 'grad_w', 'grad_w_gate': 'grad_w', 'grad_w_up': 'grad_w', 'grad_w_down': 'grad_w', 'delta_attn_norm': 'delta_w', 'delta_w_in': 'delta_w', 'delta_dn_conv': 'delta_w', 'delta_dn_a_log': 'delta_w', 'delta_dn_dt_bias': 'delta_w', 'delta_dn_out_norm': 'delta_w', 'delta_swa_q_norm': 'delta_w', 'delta_swa_k_norm': 'delta_w', 'delta_swa_sinks': 'delta_w', 'delta_rel_bias': 'delta_w', 'delta_w_branch_dn': 'delta_w', 'delta_w_branch_swa': 'delta_w', 'delta_w_out': 'delta_w', 'delta_ffn_norm': 'delta_w', 'delta_w_gate': 'delta_w', 'delta_w_up': 'delta_w', 'delta_w_down': 'delta_w', 'new_m_attn_norm': 'new_m', 'new_m_w_in': 'new_m', 'new_m_dn_conv': 'new_m', 'new_m_dn_a_log': 'new_m', 'new_m_dn_dt_bias': 'new_m', 'new_m_dn_out_norm': 'new_m', 'new_m_swa_q_norm': 'new_m', 'new_m_swa_k_norm': 'new_m', 'new_m_swa_sinks': 'new_m', 'new_m_rel_bias': 'new_m', 'new_m_w_branch_dn': 'new_m', 'new_m_w_branch_swa': 'new_m', 'new_m_w_out': 'new_m', 'new_m_ffn_norm': 'new_m', 'new_m_w_gate': 'new_m', 'new_m_w_up': 'new_m', 'new_m_w_down': 'new_m', 'new_v_attn_norm': 'new_v', 'new_v_w_in': 'new_v', 'new_v_dn_conv': 'new_v', 'new_v_dn_a_log': 'new_v', 'new_v_dn_dt_bias': 'new_v', 'new_v_dn_out_norm': 'new_v', 'new_v_swa_q_norm': 'new_v', 'new_v_swa_k_norm': 'new_v', 'new_v_swa_sinks': 'new_v', 'new_v_rel_bias': 'new_v', 'new_v_w_branch_dn': 'new_v', 'new_v_w_branch_swa': 'new_v', 'new_v_w_out': 'new_v', 'new_v_ffn_norm': 'new_v', 'new_v_w_gate': 'new_v', 'new_v_w_up': 'new_v', 'new_v_w_down': 'new_v'}


def _forward(args):
    return _fwd_reference(*[args[k] for k in FWD_PARAMS])


def _output_shape():
    def fwd():
        inp = _fwd_setup_inputs(0)
        return _fwd_reference(*[inp[k] for k in FWD_PARAMS])
    out = _jax.eval_shape(fwd)
    return out.shape, out.dtype

N_MICROBATCH = 1
ADAM_LR = 0.001
ADAM_B1 = 0.9
ADAM_B2 = 0.999
ADAM_EPS = 1e-08
ADAM_WD = 0.01
ADAM_STEP = 10
PER_EXAMPLE_BATCH_AXIS = {'x': 0, 'loss_target': 0}
SHARED_INPUTS = []
_WEIGHT_DTYPES = {'attn_norm': _jnp.float32, 'w_in': _jnp.float32, 'dn_conv': _jnp.float32, 'dn_a_log': _jnp.float32, 'dn_dt_bias': _jnp.float32, 'dn_out_norm': _jnp.float32, 'swa_q_norm': _jnp.float32, 'swa_k_norm': _jnp.float32, 'swa_sinks': _jnp.float32, 'rel_bias': _jnp.float32, 'w_branch_dn': _jnp.float32, 'w_branch_swa': _jnp.float32, 'w_out': _jnp.float32, 'ffn_norm': _jnp.float32, 'w_gate': _jnp.float32, 'w_up': _jnp.float32, 'w_down': _jnp.float32}
MOMENT_SCALE = {'attn_norm': 4.391320e+00, 'w_in': 1.582964e-01, 'dn_conv': 3.246182e-01, 'dn_a_log': 2.595455e+01, 'dn_dt_bias': 2.438892e+01, 'dn_out_norm': 2.759541e+01, 'swa_q_norm': 2.071559e+00, 'swa_k_norm': 2.073363e+00, 'swa_sinks': 6.554591e-01, 'rel_bias': 9.243407e-02, 'w_branch_dn': 4.088931e-01, 'w_branch_swa': 4.877755e-02, 'w_out': 3.521001e-01, 'ffn_norm': 2.485603e+01, 'w_gate': 2.164896e-01, 'w_up': 1.851041e-01, 'w_down': 2.888060e-01}


def _to_microbatches(a, axis):
    t = _jnp.moveaxis(a, axis, 0)
    t = t.reshape((N_MICROBATCH, t.shape[0] // N_MICROBATCH) + t.shape[1:])
    return _jnp.moveaxis(t, 1, axis + 1)


def setup_inputs(seed: int = 0) -> dict:
    inp = _fwd_setup_inputs(seed)
    key = _jax.random.fold_in(_jax.random.key(seed), 7919)
    shape, _ = _output_shape()
    out = dict(inp)
    out["loss_target"] = _jax.random.normal(_jax.random.fold_in(key, 0), shape, _jnp.float32)
    for i, name in enumerate(TWIN_WEIGHTS):
        w = inp[name].astype(_jnp.float32)
        if MOMENT_SCALE is None:
            s = _jnp.sqrt(_jnp.mean(_jnp.square(w)) + 1e-30)
        else:
            s = MOMENT_SCALE[name]
        km, kv = _jax.random.split(_jax.random.fold_in(key, i + 1))
        out[name] = w
        out["m_" + name] = s * _jax.random.normal(km, w.shape, _jnp.float32)
        out["v_" + name] = (s * s) * _jax.random.uniform(kv, w.shape, _jnp.float32, 0.5, 1.5)
    if N_MICROBATCH > 1:
        for name, axis in PER_EXAMPLE_BATCH_AXIS.items():
            out[name] = _to_microbatches(out[name], axis)
    return {'x': out['x'], 'attn_norm': out['attn_norm'], 'w_in': out['w_in'], 'dn_conv': out['dn_conv'], 'dn_a_log': out['dn_a_log'], 'dn_dt_bias': out['dn_dt_bias'], 'dn_out_norm': out['dn_out_norm'], 'swa_q_norm': out['swa_q_norm'], 'swa_k_norm': out['swa_k_norm'], 'swa_sinks': out['swa_sinks'], 'rel_bias': out['rel_bias'], 'w_branch_dn': out['w_branch_dn'], 'w_branch_swa': out['w_branch_swa'], 'w_out': out['w_out'], 'ffn_norm': out['ffn_norm'], 'w_gate': out['w_gate'], 'w_up': out['w_up'], 'w_down': out['w_down'], 'loss_target': out['loss_target'], 'm_attn_norm': out['m_attn_norm'], 'm_w_in': out['m_w_in'], 'm_dn_conv': out['m_dn_conv'], 'm_dn_a_log': out['m_dn_a_log'], 'm_dn_dt_bias': out['m_dn_dt_bias'], 'm_dn_out_norm': out['m_dn_out_norm'], 'm_swa_q_norm': out['m_swa_q_norm'], 'm_swa_k_norm': out['m_swa_k_norm'], 'm_swa_sinks': out['m_swa_sinks'], 'm_rel_bias': out['m_rel_bias'], 'm_w_branch_dn': out['m_w_branch_dn'], 'm_w_branch_swa': out['m_w_branch_swa'], 'm_w_out': out['m_w_out'], 'm_ffn_norm': out['m_ffn_norm'], 'm_w_gate': out['m_w_gate'], 'm_w_up': out['m_w_up'], 'm_w_down': out['m_w_down'], 'v_attn_norm': out['v_attn_norm'], 'v_w_in': out['v_w_in'], 'v_dn_conv': out['v_dn_conv'], 'v_dn_a_log': out['v_dn_a_log'], 'v_dn_dt_bias': out['v_dn_dt_bias'], 'v_dn_out_norm': out['v_dn_out_norm'], 'v_swa_q_norm': out['v_swa_q_norm'], 'v_swa_k_norm': out['v_swa_k_norm'], 'v_swa_sinks': out['v_swa_sinks'], 'v_rel_bias': out['v_rel_bias'], 'v_w_branch_dn': out['v_w_branch_dn'], 'v_w_branch_swa': out['v_w_branch_swa'], 'v_w_out': out['v_w_out'], 'v_ffn_norm': out['v_ffn_norm'], 'v_w_gate': out['v_w_gate'], 'v_w_up': out['v_w_up'], 'v_w_down': out['v_w_down']}


def _loss(weights, diff, rest, loss_target):
    with _jax.named_scope("forward"):
        args = {**rest, TWIN_DIFF_INPUT: diff, **{k: w.astype(_WEIGHT_DTYPES[k]) for k, w in weights.items()}}
        y = _forward(args)
    with _jax.named_scope("loss_head"):
        err = _jnp.square(y.astype(_jnp.float32) - loss_target)
        return 0.5 * _jnp.sum(_jnp.mean(err, axis=-1)) if err.ndim else 0.5 * err


def _adamw(w, g, m, v):
    m = ADAM_B1 * m + (1.0 - ADAM_B1) * g
    v = ADAM_B2 * v + (1.0 - ADAM_B2) * _jnp.square(g)
    m_hat = m / (1.0 - ADAM_B1 ** ADAM_STEP)
    v_hat = v / (1.0 - ADAM_B2 ** ADAM_STEP)
    delta = -ADAM_LR * (m_hat / (_jnp.sqrt(v_hat) + ADAM_EPS) + ADAM_WD * w)
    return delta, m, v


def reference(x, attn_norm, w_in, dn_conv, dn_a_log, dn_dt_bias, dn_out_norm, swa_q_norm, swa_k_norm, swa_sinks, rel_bias, w_branch_dn, w_branch_swa, w_out, ffn_norm, w_gate, w_up, w_down, loss_target, m_attn_norm, m_w_in, m_dn_conv, m_dn_a_log, m_dn_dt_bias, m_dn_out_norm, m_swa_q_norm, m_swa_k_norm, m_swa_sinks, m_rel_bias, m_w_branch_dn, m_w_branch_swa, m_w_out, m_ffn_norm, m_w_gate, m_w_up, m_w_down, v_attn_norm, v_w_in, v_dn_conv, v_dn_a_log, v_dn_dt_bias, v_dn_out_norm, v_swa_q_norm, v_swa_k_norm, v_swa_sinks, v_rel_bias, v_w_branch_dn, v_w_branch_swa, v_w_out, v_ffn_norm, v_w_gate, v_w_up, v_w_down):
    given = dict(x=x, attn_norm=attn_norm, w_in=w_in, dn_conv=dn_conv, dn_a_log=dn_a_log, dn_dt_bias=dn_dt_bias, dn_out_norm=dn_out_norm, swa_q_norm=swa_q_norm, swa_k_norm=swa_k_norm, swa_sinks=swa_sinks, rel_bias=rel_bias, w_branch_dn=w_branch_dn, w_branch_swa=w_branch_swa, w_out=w_out, ffn_norm=ffn_norm, w_gate=w_gate, w_up=w_up, w_down=w_down, loss_target=loss_target, m_attn_norm=m_attn_norm, m_w_in=m_w_in, m_dn_conv=m_dn_conv, m_dn_a_log=m_dn_a_log, m_dn_dt_bias=m_dn_dt_bias, m_dn_out_norm=m_dn_out_norm, m_swa_q_norm=m_swa_q_norm, m_swa_k_norm=m_swa_k_norm, m_swa_sinks=m_swa_sinks, m_rel_bias=m_rel_bias, m_w_branch_dn=m_w_branch_dn, m_w_branch_swa=m_w_branch_swa, m_w_out=m_w_out, m_ffn_norm=m_ffn_norm, m_w_gate=m_w_gate, m_w_up=m_w_up, m_w_down=m_w_down, v_attn_norm=v_attn_norm, v_w_in=v_w_in, v_dn_conv=v_dn_conv, v_dn_a_log=v_dn_a_log, v_dn_dt_bias=v_dn_dt_bias, v_dn_out_norm=v_dn_out_norm, v_swa_q_norm=v_swa_q_norm, v_swa_k_norm=v_swa_k_norm, v_swa_sinks=v_swa_sinks, v_rel_bias=v_rel_bias, v_w_branch_dn=v_w_branch_dn, v_w_branch_swa=v_w_branch_swa, v_w_out=v_w_out, v_ffn_norm=v_ffn_norm, v_w_gate=v_w_gate, v_w_up=v_w_up, v_w_down=v_w_down)
    weights = {n: given[n] for n in TWIN_WEIGHTS}
    shared = {n: given[n] for n in SHARED_INPUTS}
    per_example = {n: given[n] for n in ['x']}
    grad_fn = _jax.value_and_grad(_loss, argnums=(0, 1))

    def one_microbatch(ex, loss_target):
        ex = dict(ex)
        diff = ex.pop(TWIN_DIFF_INPUT)
        return grad_fn(weights, diff, {**shared, **ex}, loss_target)

    if N_MICROBATCH == 1:
        loss, (grad_w, grad_x) = one_microbatch(per_example, given["loss_target"])
    else:
        def body(carry, xs):
            loss_sum, grad_sum = carry
            l_k, (gw_k, gx_k) = one_microbatch(xs[0], xs[1])
            with _jax.named_scope("update"):
                return (loss_sum + l_k, _jax.tree.map(_jnp.add, grad_sum, gw_k)), gx_k

        init = (_jnp.zeros((), _jnp.float32), _jax.tree.map(_jnp.zeros_like, weights))
        (loss, grad_w), grad_x = _jax.lax.scan(body, init, (per_example, given["loss_target"]))
    with _jax.named_scope("update"):
        delta_w, new_m, new_v = {}, {}, {}
        for n in TWIN_WEIGHTS:
            delta_w[n], new_m[n], new_v[n] = _adamw(weights[n], grad_w[n], given["m_" + n], given["v_" + n])
    return (loss, grad_x, *[grad_w[n] for n in TWIN_WEIGHTS], *[delta_w[n] for n in TWIN_WEIGHTS],
            *[new_m[n] for n in TWIN_WEIGHTS], *[new_v[n] for n in TWIN_WEIGHTS])
```

```python
import math

import numpy as np
import jax
import jax.numpy as jnp
from jax import lax
from jax.experimental import pallas as pl
from jax.experimental.pallas import tpu as pltpu

F32 = jnp.float32
BF16 = jnp.bfloat16
HI = lax.Precision.HIGHEST

D_MODEL = 1024
DN_HEADS = 4
DN_DIM = 128
DN_WIDTH = 512
DN_QKV = 1536
DN_CONV = 4
CHUNK = 64
SWA_HEADS = 8
SWA_KV = 2
SWA_GROUP = 4
SWA_DIM = 64
SWA_WIDTH = 512
SWA_KVW = 128
WINDOW = 128
BLOCK = 128
REL_BUCKETS = 32
REL_MAX_DIST = 128
D_FF = 2816
D_IN = 4872
EPS = 1e-6
N_DEV = 8

ADAM_LR = 0.001
ADAM_B1 = 0.9
ADAM_B2 = 0.999
ADAM_EPS = 1e-08
ADAM_WD = 0.01
ADAM_STEP = 10

P_GATE, P_QKV, P_Z, P_SQ, P_SK, P_SV, P_BA = 0, 2048, 3584, 4096, 4608, 4736, 4864
P_WIDTH = 4992
R_QKV, R_Z, R_B, R_A, R_SQ, R_SK, R_SV, R_GATE = 0, 1536, 2048, 2052, 2056, 2568, 2696, 2824

VMEM_LIMIT = 56 * 1024 * 1024
LANES = 128
MESH_ID = pl.DeviceIdType.MESH


def _params(sem=None):
    return pltpu.CompilerParams(dimension_semantics=sem, vmem_limit_bytes=VMEM_LIMIT)


def _pick(dim, target):
    if dim <= target:
        return dim
    t = target - target % LANES
    while t >= LANES:
        if dim % t == 0:
            return t
        t -= LANES
    return dim


_DIMS = {"nn": (((1,), (0,)), ((), ())), "nt": (((1,), (1,)), ((), ())), "tn": (((0,), (0,)), ((), ()))}


def _mm(a, b, mode, out_dtype, name, tm=512, tn=512, tk=1024):
    if mode == "nn":
        (M, K), (K2, N) = a.shape, b.shape
    elif mode == "nt":
        (M, K), (N, K2) = a.shape, b.shape
    else:
        (K, M), (K2, N) = a.shape, b.shape
    assert K == K2, (name, a.shape, b.shape)
    bm, bn, bk = _pick(M, tm), _pick(N, tn), _pick(K, tk)
    nk = K // bk
    dims = _DIMS[mode]

    def body(a_ref, b_ref, o_ref, acc_ref):
        k = pl.program_id(2)
        p = lax.dot_general(a_ref[...].astype(BF16), b_ref[...].astype(BF16), dims,
                            preferred_element_type=F32)

        @pl.when(k == 0)
        def _():
            acc_ref[...] = p

        @pl.when(k > 0)
        def _():
            acc_ref[...] += p

        @pl.when(k == nk - 1)
        def _():
            o_ref[...] = acc_ref[...].astype(out_dtype)

    if mode == "tn":
        a_spec = pl.BlockSpec((bk, bm), lambda i, j, k: (k, i))
    else:
        a_spec = pl.BlockSpec((bm, bk), lambda i, j, k: (i, k))
    if mode == "nt":
        b_spec = pl.BlockSpec((bn, bk), lambda i, j, k: (j, k))
    else:
        b_spec = pl.BlockSpec((bk, bn), lambda i, j, k: (k, j))
    return pl.pallas_call(
        body, grid=(M // bm, N // bn, nk), in_specs=[a_spec, b_spec],
        out_specs=pl.BlockSpec((bm, bn), lambda i, j, k: (i, j)),
        out_shape=jax.ShapeDtypeStruct((M, N), out_dtype),
        scratch_shapes=[pltpu.VMEM((bm, bn), F32)], name=name,
        compiler_params=_params(("parallel", "parallel", "arbitrary")),
    )(a, b)


def _rms(x, gain):
    return x * lax.rsqrt(jnp.mean(x * x, axis=-1, keepdims=True) + EPS) * gain


def _silu(x):
    return x * jax.nn.sigmoid(x)


def _act(g, u):
    return _silu(g) * u


def _merge(g0, g1, a_dn, a_swa):
    return jax.nn.sigmoid(g0) * a_dn + jax.nn.sigmoid(g1) * a_swa


def _dn_post(c, is_v, q_scale):
    a = _silu(c)
    rs = lax.rsqrt(jnp.sum(a * a, axis=-1, keepdims=True) + EPS) * q_scale
    return a * jnp.where(is_v, 1.0, rs)


def _dn_out(o, z, gain):
    return _rms(o, gain) * _silu(z)


def _tri_consts():
    ii = lax.broadcasted_iota(jnp.int32, (CHUNK, CHUNK), 0)
    jj = lax.broadcasted_iota(jnp.int32, (CHUNK, CHUNK), 1)
    return ii, jj


def _dot(a, b, dims=_DIMS["nn"], hi=False):
    if hi:
        return lax.dot_general(a, b, dims, precision=HI, preferred_element_type=F32)
    return lax.dot_general(a.astype(BF16), b.astype(BF16), dims, preferred_element_type=F32)


def _dn_prep(q, k, v, g, beta):
    ii, jj = _tri_consts()
    lane0 = (lax.broadcasted_iota(jnp.int32, (LANES, CHUNK), 0) == 0).astype(F32)
    g64 = _dot(g, lane0, hi=True)
    low = (ii >= jj).astype(F32)
    upp = (ii <= jj).astype(F32)
    ones = jnp.ones((CHUNK, CHUNK), F32)
    eye = (ii == jj).astype(F32)
    gc = _dot(low, g, hi=True)
    col = _dot(low, g64, hi=True)
    row = _dot(ones, g64 * upp, hi=True)
    decay = jnp.exp(jnp.where(ii >= jj, col - row, -jnp.inf))
    kb = k * beta
    vb = v * beta
    a = jnp.where(ii > jj, _dot(kb, k, _DIMS["nt"]) * decay, 0.0)
    t = eye - a
    p = _dot(a, a, hi=True)
    for level in range(5):
        t = t + _dot(t, p, hi=True)
        if level < 4:
            p = _dot(p, p, hi=True)
    u = _dot(t, vb, hi=True)
    w = _dot(t, kb * jnp.exp(gc), hi=True)
    qk = _dot(q, k, _DIMS["nt"]) * decay
    gl = jnp.sum(g, axis=0, keepdims=True)
    return u, w, q * jnp.exp(gc), k * jnp.exp(gl - gc), qk, jnp.exp(gl)


def _dn_step(s, u, w, qe, kd, qk, egl):
    v_new = u - _dot(w, s)
    o = _dot(qe, s) + _dot(qk, v_new)
    s_new = s * egl + _dot(kd, v_new, _DIMS["tn"])
    return s_new, o


def _swa_heads(qs, kband, vband, qg, kg, sinks, biases, mask):
    kn = _rms(kband, kg)
    outs = []
    for q, sink, bias in zip(qs, sinks, biases):
        qn = _rms(q, qg)
        logits = _dot(qn, kn, _DIMS["nt"]) * (SWA_DIM ** -0.5)
        logits = jnp.where(mask, logits + bias, -jnp.inf)
        m = jnp.maximum(jnp.max(logits, axis=-1, keepdims=True), sink)
        p = jnp.exp(logits - m)
        denom = jnp.sum(p, axis=-1, keepdims=True) + jnp.exp(sink - m)
        outs.append(_dot(p / denom, vband))
    return outs


def _adamw(w, g, m, v):
    m = ADAM_B1 * m + (1.0 - ADAM_B1) * g
    v = ADAM_B2 * v + (1.0 - ADAM_B2) * jnp.square(g)
    m_hat = m / (1.0 - ADAM_B1 ** ADAM_STEP)
    v_hat = v / (1.0 - ADAM_B2 ** ADAM_STEP)
    delta = -ADAM_LR * (m_hat / (jnp.sqrt(v_hat) + ADAM_EPS) + ADAM_WD * w)
    return delta, m, v


def _row(tm, c, cb=0):
    return pl.BlockSpec((tm, c), lambda i, cb=cb: (i, cb))


def _full(shape):
    nd = len(shape)
    return pl.BlockSpec(shape, lambda *_, nd=nd: (0,) * nd)


def _norm_fwd(x, gain, name, tm=512):
    S = x.shape[0]

    def body(x_ref, g_ref, h_ref):
        h_ref[...] = _rms(x_ref[...], g_ref[...]).astype(BF16)

    return pl.pallas_call(
        body, grid=(S // tm,), in_specs=[_row(tm, D_MODEL), _full((1, D_MODEL))],
        out_specs=_row(tm, D_MODEL), out_shape=jax.ShapeDtypeStruct((S, D_MODEL), BF16),
        name=name, compiler_params=_params(("parallel",)))(x, gain)


def _resid_norm_fwd(x, t, gain, name, tm=512):
    S = x.shape[0]

    def body(x_ref, t_ref, g_ref, x1_ref, h_ref):
        x1 = x_ref[...] + t_ref[...]
        x1_ref[...] = x1
        h_ref[...] = _rms(x1, g_ref[...]).astype(BF16)

    return pl.pallas_call(
        body, grid=(S // tm,), in_specs=[_row(tm, D_MODEL), _row(tm, D_MODEL), _full((1, D_MODEL))],
        out_specs=[_row(tm, D_MODEL), _row(tm, D_MODEL)],
        out_shape=[jax.ShapeDtypeStruct((S, D_MODEL), F32), jax.ShapeDtypeStruct((S, D_MODEL), BF16)],
        name=name, compiler_params=_params(("parallel",)))(x, t, gain)


def _norm_bwd(x, dh_list, dres, gain, name, tm=256):
    S = x.shape[0]
    n = len(dh_list)

    def body(*refs):
        x_ref, g_ref, r_ref = refs[0], refs[1], refs[2]
        dh_refs = refs[3:3 + n]
        dx_ref, dg_ref = refs[3 + n], refs[4 + n]
        dh = dh_refs[0][...].astype(F32)
        for r in dh_refs[1:]:
            dh = dh + r[...].astype(F32)
        _, vjp = jax.vjp(_rms, x_ref[...], g_ref[...])
        dx, dg = vjp(dh)
        dx_ref[...] = dx + r_ref[...]

        @pl.when(pl.program_id(0) == 0)
        def _():
            dg_ref[...] = jnp.zeros_like(dg_ref)

        dg_ref[...] += dg

    return pl.pallas_call(
        body, grid=(S // tm,),
        in_specs=[_row(tm, D_MODEL), _full((1, D_MODEL)), _row(tm, D_MODEL)] + [_row(tm, D_MODEL)] * n,
        out_specs=[_row(tm, D_MODEL), _full((1, D_MODEL))],
        out_shape=[jax.ShapeDtypeStruct((S, D_MODEL), F32), jax.ShapeDtypeStruct((1, D_MODEL), F32)],
        name=name, compiler_params=_params(("arbitrary",)))(x, gain, dres, *dh_list)


def _loss_fwd_bwd(x1, f, target, tm=512):
    S = x1.shape[0]

    def body(x_ref, f_ref, t_ref, dy_ref, l_ref):
        diff = x_ref[...] + f_ref[...] - t_ref[...]
        dy_ref[...] = diff * (1.0 / D_MODEL)

        @pl.when(pl.program_id(0) == 0)
        def _():
            l_ref[...] = jnp.zeros_like(l_ref)

        l_ref[...] += jnp.sum(jnp.mean(diff * diff, axis=-1, keepdims=True), axis=0, keepdims=True) * 0.5

    return pl.pallas_call(
        body, grid=(S // tm,), in_specs=[_row(tm, D_MODEL)] * 3,
        out_specs=[_row(tm, D_MODEL), _full((1, 1))],
        out_shape=[jax.ShapeDtypeStruct((S, D_MODEL), F32), jax.ShapeDtypeStruct((1, 1), F32)],
        name="loss_fwd_bwd", compiler_params=_params(("arbitrary",)))(x1, f, target)


def _act_fwd(g, u, tm=256):
    S = g.shape[0]

    def body(g_ref, u_ref, o_ref):
        o_ref[...] = _act(g_ref[...], u_ref[...]).astype(BF16)

    return pl.pallas_call(
        body, grid=(S // tm,), in_specs=[_row(tm, D_FF)] * 2, out_specs=_row(tm, D_FF),
        out_shape=jax.ShapeDtypeStruct((S, D_FF), BF16), name="act_fwd",
        compiler_params=_params(("parallel",)))(g, u)


def _act_bwd(g, u, dact, tm=256):
    S = g.shape[0]

    def body(g_ref, u_ref, d_ref, dg_ref, du_ref):
        _, vjp = jax.vjp(_act, g_ref[...], u_ref[...])
        dg, du = vjp(d_ref[...])
        dg_ref[...] = dg.astype(BF16)
        du_ref[...] = du.astype(BF16)

    return pl.pallas_call(
        body, grid=(S // tm,), in_specs=[_row(tm, D_FF)] * 3, out_specs=[_row(tm, D_FF)] * 2,
        out_shape=[jax.ShapeDtypeStruct((S, D_FF), BF16)] * 2, name="act_bwd",
        compiler_params=_params(("parallel",)))(g, u, dact)


def _merge_fwd(proj, a_dn, a_swa, tm=512, tc=512):
    S = proj.shape[0]
    nc = D_MODEL // tc

    def spec(off):
        return pl.BlockSpec((tm, tc), lambda i, j, off=off: (i, off + j))

    def body(g0_ref, g1_ref, ad_ref, as_ref, o_ref):
        o_ref[...] = _merge(g0_ref[...], g1_ref[...], ad_ref[...], as_ref[...]).astype(BF16)

    return pl.pallas_call(
        body, grid=(S // tm, nc), in_specs=[spec(P_GATE // tc), spec(P_GATE // tc + nc), spec(0), spec(0)],
        out_specs=spec(0), out_shape=jax.ShapeDtypeStruct((S, D_MODEL), BF16), name="merge_fwd",
        compiler_params=_params(("parallel", "parallel")))(proj, proj, a_dn, a_swa)


def _merge_bwd(proj, a_dn, a_swa, dmerged, tm=512, tc=512):
    S = proj.shape[0]
    nc = D_MODEL // tc

    def spec(off):
        return pl.BlockSpec((tm, tc), lambda i, j, off=off: (i, off + j))

    def body(g0_ref, g1_ref, ad_ref, as_ref, d_ref, dg0_ref, dg1_ref, dad_ref, das_ref):
        _, vjp = jax.vjp(_merge, g0_ref[...], g1_ref[...], ad_ref[...], as_ref[...])
        dg0, dg1, dad, das = vjp(d_ref[...])
        dg0_ref[...] = dg0.astype(BF16)
        dg1_ref[...] = dg1.astype(BF16)
        dad_ref[...] = dad.astype(BF16)
        das_ref[...] = das.astype(BF16)

    return pl.pallas_call(
        body, grid=(S // tm, nc),
        in_specs=[spec(P_GATE // tc), spec(P_GATE // tc + nc), spec(0), spec(0), spec(0)],
        out_specs=[spec(0)] * 4, out_shape=[jax.ShapeDtypeStruct((S, D_MODEL), BF16)] * 4,
        name="merge_bwd", compiler_params=_params(("parallel", "parallel")))(proj, proj, a_dn, a_swa, dmerged)


def _shift_down(x, s):
    row = lax.broadcasted_iota(jnp.int32, x.shape, 0)
    return jnp.where(row >= s, pltpu.roll(x, s, axis=0), 0.0)


def _shift_up(x, s):
    n = x.shape[0]
    row = lax.broadcasted_iota(jnp.int32, x.shape, 0)
    return jnp.where(row < n - s, pltpu.roll(x, n - s, axis=0), 0.0)


def _conv(x, w):
    out = w[DN_CONV - 1:DN_CONV] * x
    for s in range(1, DN_CONV):
        out = out + w[DN_CONV - 1 - s:DN_CONV - s] * _shift_down(x, s)
    return out


def _dn_conv_fwd(proj, conv_w):
    S = proj.shape[0]
    nb = DN_QKV // LANES

    def body(x_ref, w_ref, o_ref):
        j = pl.program_id(0)
        q_scale = jnp.where(j < DN_HEADS, DN_DIM ** -0.5, 1.0).astype(F32)
        o_ref[...] = _dn_post(_conv(x_ref[...], w_ref[...]), j >= 2 * DN_HEADS, q_scale)

    return pl.pallas_call(
        body, grid=(nb,),
        in_specs=[pl.BlockSpec((S, LANES), lambda j: (0, P_QKV // LANES + j)),
                  pl.BlockSpec((DN_CONV, LANES), lambda j: (0, j))],
        out_specs=pl.BlockSpec((S, LANES), lambda j: (0, j)),
        out_shape=jax.ShapeDtypeStruct((S, DN_QKV), F32), name="dn_conv_fwd",
        compiler_params=_params(("parallel",)))(proj, conv_w)


def _dn_conv_bwd(proj, conv_w, dqkvn):
    S = proj.shape[0]
    nb = DN_QKV // LANES

    def body(x_ref, w_ref, d_ref, dx_ref, dw_ref):
        j = pl.program_id(0)
        q_scale = jnp.where(j < DN_HEADS, DN_DIM ** -0.5, 1.0).astype(F32)
        x = x_ref[...]
        w = w_ref[...]
        _, vjp = jax.vjp(lambda c: _dn_post(c, j >= 2 * DN_HEADS, q_scale), _conv(x, w))
        (dc,) = vjp(d_ref[...])
        dx = w[DN_CONV - 1:DN_CONV] * dc
        dw_ref[DN_CONV - 1:DN_CONV, :] = jnp.sum(dc * x, axis=0, keepdims=True)
        for s in range(1, DN_CONV):
            dx = dx + w[DN_CONV - 1 - s:DN_CONV - s] * _shift_up(dc, s)
            dw_ref[DN_CONV - 1 - s:DN_CONV - s, :] = jnp.sum(dc * _shift_down(x, s), axis=0, keepdims=True)
        dx_ref[...] = dx.astype(BF16)

    return pl.pallas_call(
        body, grid=(nb,),
        in_specs=[pl.BlockSpec((S, LANES), lambda j: (0, P_QKV // LANES + j)),
                  pl.BlockSpec((DN_CONV, LANES), lambda j: (0, j)),
                  pl.BlockSpec((S, LANES), lambda j: (0, j))],
        out_specs=[pl.BlockSpec((S, LANES), lambda j: (0, j)), pl.BlockSpec((DN_CONV, LANES), lambda j: (0, j))],
        out_shape=[jax.ShapeDtypeStruct((S, DN_QKV), BF16), jax.ShapeDtypeStruct((DN_CONV, DN_QKV), F32)],
        name="dn_conv_bwd", compiler_params=_params(("parallel",)))(proj, conv_w, dqkvn)


def _expanders():
    eb = np.zeros((LANES, DN_WIDTH), np.float32)
    ea = np.zeros((LANES, DN_WIDTH), np.float32)
    for h in range(DN_HEADS):
        eb[h, h * DN_DIM:(h + 1) * DN_DIM] = 1.0
        ea[DN_HEADS + h, h * DN_DIM:(h + 1) * DN_DIM] = 1.0
    return jnp.asarray(eb), jnp.asarray(ea)


def _dn_gate_args(a_log, dt_bias):
    eb, ea = _expanders()
    alog = jnp.repeat(a_log.reshape(1, DN_HEADS), DN_DIM, axis=1)
    dtb = jnp.repeat(dt_bias.reshape(1, DN_HEADS), DN_DIM, axis=1)
    return eb, ea, alog, dtb


def _dn_gate_specs(tm):
    return [_row(tm, LANES, P_BA // LANES), _full((LANES, DN_WIDTH)), _full((LANES, DN_WIDTH)),
            _full((1, DN_WIDTH)), _full((1, DN_WIDTH))]


def _dn_gate_fn(ba, eb, ea, alog, dtb):
    beta = jax.nn.sigmoid(_dot(ba, eb, hi=True))
    g = -jnp.exp(alog) * jax.nn.softplus(_dot(ba, ea, hi=True) + dtb)
    return beta, g


def _dn_gate_fwd(proj, a_log, dt_bias, tm=512):
    S = proj.shape[0]
    args = _dn_gate_args(a_log, dt_bias)

    def body(ba_ref, eb_ref, ea_ref, al_ref, dt_ref, beta_ref, g_ref):
        beta, g = _dn_gate_fn(ba_ref[...], eb_ref[...], ea_ref[...], al_ref[...], dt_ref[...])
        beta_ref[...] = beta
        g_ref[...] = g

    return pl.pallas_call(
        body, grid=(S // tm,), in_specs=_dn_gate_specs(tm), out_specs=[_row(tm, DN_WIDTH), _row(tm, DN_WIDTH)],
        out_shape=[jax.ShapeDtypeStruct((S, DN_WIDTH), F32), jax.ShapeDtypeStruct((S, DN_WIDTH), F32)],
        name="dn_gate_fwd", compiler_params=_params(("parallel",)))(proj, *args)


def _dn_gate_bwd(proj, a_log, dt_bias, dbeta, dg, tm=512):
    S = proj.shape[0]
    args = _dn_gate_args(a_log, dt_bias)

    def body(ba_ref, eb_ref, ea_ref, al_ref, dt_ref, dbeta_ref, dg_ref, dba_ref, dal_ref, ddt_ref):
        eb, ea = eb_ref[...], ea_ref[...]
        _, vjp = jax.vjp(lambda ba, al, dt: _dn_gate_fn(ba, eb, ea, al, dt), ba_ref[...], al_ref[...], dt_ref[...])
        dba, dal, ddt = vjp((dbeta_ref[...], dg_ref[...]))
        dba_ref[...] = dba.astype(BF16)

        @pl.when(pl.program_id(0) == 0)
        def _():
            dal_ref[...] = jnp.zeros_like(dal_ref)
            ddt_ref[...] = jnp.zeros_like(ddt_ref)

        dal_ref[...] += dal
        ddt_ref[...] += ddt

    return pl.pallas_call(
        body, grid=(S // tm,), in_specs=_dn_gate_specs(tm) + [_row(tm, DN_WIDTH), _row(tm, DN_WIDTH)],
        out_specs=[_row(tm, LANES), _full((1, DN_WIDTH)), _full((1, DN_WIDTH))],
        out_shape=[jax.ShapeDtypeStruct((S, LANES), BF16), jax.ShapeDtypeStruct((1, DN_WIDTH), F32),
                   jax.ShapeDtypeStruct((1, DN_WIDTH), F32)],
        name="dn_gate_bwd", compiler_params=_params(("arbitrary",)))(proj, *args, dbeta, dg)


PREP_CHUNKS = 2


def _dn_prep_specs():
    rows = PREP_CHUNKS * CHUNK
    q = pl.BlockSpec((rows, LANES), lambda h, c: (c, h))
    k = pl.BlockSpec((rows, LANES), lambda h, c: (c, DN_HEADS + h))
    v = pl.BlockSpec((rows, LANES), lambda h, c: (c, 2 * DN_HEADS + h))
    qk = pl.BlockSpec((1, rows, CHUNK), lambda h, c: (h, c, 0))
    egl = pl.BlockSpec((1, PREP_CHUNKS, 1, LANES), lambda h, c: (h, c, 0, 0))
    return q, k, v, qk, egl


def _dn_prep_fwd(qkvn, g, beta):
    S = qkvn.shape[0]
    nc = S // CHUNK
    q, k, v, qks, egl = _dn_prep_specs()

    def body(q_ref, k_ref, v_ref, g_ref, b_ref, u_ref, w_ref, qe_ref, kd_ref, qk_ref, egl_ref):
        for i in range(PREP_CHUNKS):
            r = pl.ds(i * CHUNK, CHUNK)
            u, w, qe, kd, qk, e = _dn_prep(q_ref[r, :], k_ref[r, :], v_ref[r, :], g_ref[r, :], b_ref[r, :])
            u_ref[r, :] = u
            w_ref[r, :] = w
            qe_ref[r, :] = qe
            kd_ref[r, :] = kd
            qk_ref[0, r, :] = qk
            egl_ref[0, i] = e

    wide = jax.ShapeDtypeStruct((S, DN_WIDTH), F32)
    return pl.pallas_call(
        body, grid=(DN_HEADS, nc // PREP_CHUNKS), in_specs=[q, k, v, q, q],
        out_specs=[q, q, q, q, qks, egl],
        out_shape=[wide, wide, wide, wide, jax.ShapeDtypeStruct((DN_HEADS, S, CHUNK), F32),
                   jax.ShapeDtypeStruct((DN_HEADS, nc, 1, LANES), F32)],
        name="dn_prep_fwd", compiler_params=_params(("parallel", "parallel")))(qkvn, qkvn, qkvn, g, beta)


def _dn_prep_bwd(qkvn, g, beta, du, dw, dqe, dkd, dqk, degl):
    S = qkvn.shape[0]
    nc = S // CHUNK
    q, k, v, qks, egl = _dn_prep_specs()

    def body(q_ref, k_ref, v_ref, g_ref, b_ref, du_ref, dw_ref, dqe_ref, dkd_ref, dqk_ref, degl_ref,
             dq_ref, dk_ref, dv_ref, dg_ref, db_ref):
        for i in range(PREP_CHUNKS):
            r = pl.ds(i * CHUNK, CHUNK)
            _, vjp = jax.vjp(_dn_prep, q_ref[r, :], k_ref[r, :], v_ref[r, :], g_ref[r, :], b_ref[r, :])
            dq, dk, dv, dg, db = vjp((du_ref[r, :], dw_ref[r, :], dqe_ref[r, :], dkd_ref[r, :],
                                      dqk_ref[0, r, :], degl_ref[0, i]))
            dq_ref[r, :] = dq
            dk_ref[r, :] = dk
            dv_ref[r, :] = dv
            dg_ref[r, :] = dg
            db_ref[r, :] = db

    wide = jax.ShapeDtypeStruct((S, DN_WIDTH), F32)
    return pl.pallas_call(
        body, grid=(DN_HEADS, nc // PREP_CHUNKS), in_specs=[q, k, v, q, q, q, q, q, q, qks, egl],
        out_specs=[q] * 5, out_shape=[wide] * 5,
        name="dn_prep_bwd", compiler_params=_params(("parallel", "parallel")),
    )(qkvn, qkvn, qkvn, g, beta, du, dw, dqe, dkd, dqk, degl)


def _dn_scan_specs(nc, reverse):
    def cidx(c):
        return nc - 1 - c if reverse else c

    hc = pl.BlockSpec((CHUNK, LANES), lambda h, c: (cidx(c), h))
    qk = pl.BlockSpec((1, CHUNK, CHUNK), lambda h, c: (h, cidx(c), 0))
    egl = pl.BlockSpec((1, 1, 1, LANES), lambda h, c: (h, cidx(c), 0, 0))
    st = pl.BlockSpec((1, 1, DN_DIM, DN_DIM), lambda h, c: (h, cidx(c), 0, 0))
    return hc, qk, egl, st


def _dn_scan_fwd(u, w, qe, kd, qk, egl):
    S = u.shape[0]
    nc = S // CHUNK
    hc, qks, egls, st = _dn_scan_specs(nc, False)

    def body(u_ref, w_ref, qe_ref, kd_ref, qk_ref, egl_ref, o_ref, st_ref, s_scr):
        @pl.when(pl.program_id(1) == 0)
        def _():
            s_scr[...] = jnp.zeros_like(s_scr)

        s = s_scr[...]
        st_ref[0, 0] = s
        s_new, o = _dn_step(s, u_ref[...], w_ref[...], qe_ref[...], kd_ref[...], qk_ref[0], egl_ref[0, 0])
        o_ref[...] = o
        s_scr[...] = s_new

    return pl.pallas_call(
        body, grid=(DN_HEADS, nc), in_specs=[hc, hc, hc, hc, qks, egls], out_specs=[hc, st],
        out_shape=[jax.ShapeDtypeStruct((S, DN_WIDTH), F32), jax.ShapeDtypeStruct((DN_HEADS, nc, DN_DIM, DN_DIM), F32)],
        scratch_shapes=[pltpu.VMEM((DN_DIM, DN_DIM), F32)], name="dn_scan_fwd",
        compiler_params=_params(("parallel", "arbitrary")))(u, w, qe, kd, qk, egl)


def _dn_scan_bwd(u, w, qe, kd, qk, egl, states, do):
    S = u.shape[0]
    nc = S // CHUNK
    hc, qks, egls, st = _dn_scan_specs(nc, True)

    def body(u_ref, w_ref, qe_ref, kd_ref, qk_ref, egl_ref, st_ref, do_ref,
             du_ref, dw_ref, dqe_ref, dkd_ref, dqk_ref, degl_ref, ds_scr):
        @pl.when(pl.program_id(1) == 0)
        def _():
            ds_scr[...] = jnp.zeros_like(ds_scr)

        _, vjp = jax.vjp(_dn_step, st_ref[0, 0], u_ref[...], w_ref[...], qe_ref[...], kd_ref[...], qk_ref[0],
                         egl_ref[0, 0])
        ds, du, dw, dqe, dkd, dqk, degl = vjp((ds_scr[...], do_ref[...]))
        ds_scr[...] = ds
        du_ref[...] = du
        dw_ref[...] = dw
        dqe_ref[...] = dqe
        dkd_ref[...] = dkd
        dqk_ref[0] = dqk
        degl_ref[0, 0] = degl

    wide = jax.ShapeDtypeStruct((S, DN_WIDTH), F32)
    return pl.pallas_call(
        body, grid=(DN_HEADS, nc), in_specs=[hc, hc, hc, hc, qks, egls, st, hc],
        out_specs=[hc, hc, hc, hc, qks, egls],
        out_shape=[wide, wide, wide, wide, jax.ShapeDtypeStruct((DN_HEADS, S, CHUNK), F32),
                   jax.ShapeDtypeStruct((DN_HEADS, nc, 1, LANES), F32)],
        scratch_shapes=[pltpu.VMEM((DN_DIM, DN_DIM), F32)], name="dn_scan_bwd",
        compiler_params=_params(("parallel", "arbitrary")))(u, w, qe, kd, qk, egl, states, do)


def _dn_out_fwd(o, proj, gain, tm=512):
    S = o.shape[0]

    def body(o_ref, z_ref, g_ref, y_ref):
        y_ref[...] = _dn_out(o_ref[...], z_ref[...], g_ref[...]).astype(BF16)

    hs = pl.BlockSpec((tm, LANES), lambda i, h: (i, h))
    zs = pl.BlockSpec((tm, LANES), lambda i, h: (i, P_Z // LANES + h))
    return pl.pallas_call(
        body, grid=(S // tm, DN_HEADS), in_specs=[hs, zs, _full((1, DN_DIM))], out_specs=hs,
        out_shape=jax.ShapeDtypeStruct((S, DN_WIDTH), BF16), name="dn_out_fwd",
        compiler_params=_params(("parallel", "parallel")))(o, proj, gain)


def _dn_out_bwd(o, proj, gain, dy, tm=512):
    S = o.shape[0]

    def body(o_ref, z_ref, g_ref, dy_ref, do_ref, dz_ref, dg_ref):
        _, vjp = jax.vjp(_dn_out, o_ref[...], z_ref[...], g_ref[...])
        do, dz, dg = vjp(dy_ref[...])
        do_ref[...] = do
        dz_ref[...] = dz.astype(BF16)

        @pl.when((pl.program_id(0) == 0) & (pl.program_id(1) == 0))
        def _():
            dg_ref[...] = jnp.zeros_like(dg_ref)

        dg_ref[...] += dg

    hs = pl.BlockSpec((tm, LANES), lambda i, h: (i, h))
    zs = pl.BlockSpec((tm, LANES), lambda i, h: (i, P_Z // LANES + h))
    return pl.pallas_call(
        body, grid=(S // tm, DN_HEADS), in_specs=[hs, zs, _full((1, DN_DIM)), hs],
        out_specs=[hs, hs, _full((1, DN_DIM))],
        out_shape=[jax.ShapeDtypeStruct((S, DN_WIDTH), F32), jax.ShapeDtypeStruct((S, DN_WIDTH), BF16),
                   jax.ShapeDtypeStruct((1, DN_DIM), F32)],
        name="dn_out_bwd", compiler_params=_params(("arbitrary", "arbitrary")))(o, proj, gain, dy)


def _rel_buckets():
    qi = np.arange(BLOCK)[:, None]
    kj = np.arange(2 * BLOCK)[None, :]
    n = np.maximum(BLOCK + qi - kj, 0)
    max_exact = REL_BUCKETS // 2
    nf = np.maximum(n, 1).astype(np.float32)
    large = max_exact + (np.log(nf / np.float32(max_exact)) / np.float32(math.log(REL_MAX_DIST / max_exact))
                         * np.float32(REL_BUCKETS - max_exact)).astype(np.int32)
    large = np.minimum(large, REL_BUCKETS - 1)
    return np.where(n < max_exact, n, large).astype(np.int32)


def _bias_fwd(rel_bias):
    buckets = jnp.asarray(_rel_buckets())

    def body(rb_ref, bk_ref, o_ref):
        bk = bk_ref[...]
        for h in range(SWA_HEADS):
            acc = jnp.zeros((BLOCK, 2 * BLOCK), F32)
            for b in range(REL_BUCKETS):
                acc = jnp.where(bk == b, rb_ref[b, h], acc)
            o_ref[h] = acc

    return pl.pallas_call(
        body, in_specs=[pl.BlockSpec(memory_space=pltpu.SMEM), pl.BlockSpec(memory_space=pltpu.VMEM)],
        out_specs=pl.BlockSpec(memory_space=pltpu.VMEM),
        out_shape=jax.ShapeDtypeStruct((SWA_HEADS, BLOCK, 2 * BLOCK), F32), name="swa_bias_fwd",
        compiler_params=_params())(rel_bias, buckets)


def _bias_bwd(dbias):
    buckets = jnp.asarray(_rel_buckets())

    def body(d_ref, bk_ref, o_ref):
        bk = bk_ref[...]
        lane = lax.broadcasted_iota(jnp.int32, (1, LANES), 1)
        for h in range(SWA_HEADS):
            d = d_ref[h]
            row = jnp.zeros((1, LANES), F32)
            for b in range(REL_BUCKETS):
                part = jnp.sum(jnp.where(bk == b, d, 0.0), axis=1, keepdims=True)
                row = jnp.where(lane == b, jnp.sum(part, axis=0, keepdims=True), row)
            o_ref[h:h + 1, :] = row

    return pl.pallas_call(
        body, in_specs=[pl.BlockSpec(memory_space=pltpu.VMEM), pl.BlockSpec(memory_space=pltpu.VMEM)],
        out_specs=pl.BlockSpec(memory_space=pltpu.VMEM),
        out_shape=jax.ShapeDtypeStruct((SWA_HEADS, LANES), F32), name="swa_bias_bwd",
        compiler_params=_params())(dbias, buckets)


def _swa_mask(n):
    qi = lax.broadcasted_iota(jnp.int32, (BLOCK, 2 * BLOCK), 0)
    kj = lax.broadcasted_iota(jnp.int32, (BLOCK, 2 * BLOCK), 1)
    dist = BLOCK + qi - kj
    return (dist >= 0) & (dist < WINDOW) & ((n > 0) | (kj >= BLOCK))


def _swa_in_specs():
    q = pl.BlockSpec((BLOCK, SWA_WIDTH), lambda n: (n, P_SQ // SWA_WIDTH))
    kc = pl.BlockSpec((BLOCK, SWA_KVW), lambda n: (n, P_SK // SWA_KVW))
    kp = pl.BlockSpec((BLOCK, SWA_KVW), lambda n: (jnp.maximum(n - 1, 0), P_SK // SWA_KVW))
    vc = pl.BlockSpec((BLOCK, SWA_KVW), lambda n: (n, P_SV // SWA_KVW))
    vp = pl.BlockSpec((BLOCK, SWA_KVW), lambda n: (jnp.maximum(n - 1, 0), P_SV // SWA_KVW))
    small = [_full((1, SWA_DIM)), _full((1, SWA_DIM)), _full((1, SWA_HEADS)),
             _full((SWA_HEADS, BLOCK, 2 * BLOCK))]
    return [q, kp, kc, vp, vc] + small


def _swa_load(kv, q_ref, kp_ref, kc_ref, vp_ref, vc_ref, s_ref, bias_ref):
    cols = pl.ds(kv * SWA_DIM, SWA_DIM)
    heads = [kv * SWA_GROUP + g for g in range(SWA_GROUP)]
    qs = [q_ref[:, pl.ds(h * SWA_DIM, SWA_DIM)] for h in heads]
    kband = jnp.concatenate([kp_ref[:, cols], kc_ref[:, cols]], axis=0)
    vband = jnp.concatenate([vp_ref[:, cols], vc_ref[:, cols]], axis=0)
    sinks = [s_ref[:, pl.ds(h, 1)] for h in heads]
    biases = [bias_ref[h] for h in heads]
    return heads, qs, kband, vband, sinks, biases


def _swa_fwd(proj, q_gain, k_gain, sinks, bias):
    S = proj.shape[0]

    def body(q_ref, kp_ref, kc_ref, vp_ref, vc_ref, qg_ref, kg_ref, s_ref, bias_ref, y_ref):
        mask = _swa_mask(pl.program_id(0))
        for kv in range(SWA_KV):
            heads, qs, kband, vband, sk, bs = _swa_load(kv, q_ref, kp_ref, kc_ref, vp_ref, vc_ref, s_ref, bias_ref)
            outs = _swa_heads(qs, kband, vband, qg_ref[...], kg_ref[...], sk, bs, mask)
            for h, o in zip(heads, outs):
                y_ref[:, pl.ds(h * SWA_DIM, SWA_DIM)] = o.astype(BF16)

    return pl.pallas_call(
        body, grid=(S // BLOCK,), in_specs=_swa_in_specs(),
        out_specs=pl.BlockSpec((BLOCK, SWA_WIDTH), lambda n: (n, 0)),
        out_shape=jax.ShapeDtypeStruct((S, SWA_WIDTH), BF16), name="swa_fwd",
        compiler_params=_params(("parallel",)))(proj, proj, proj, proj, proj, q_gain, k_gain, sinks, bias)


def _swa_bwd(proj, q_gain, k_gain, sinks, bias, dy):
    S = proj.shape[0]

    def body(q_ref, kp_ref, kc_ref, vp_ref, vc_ref, qg_ref, kg_ref, s_ref, bias_ref, dy_ref,
             dq_ref, dk_ref, dv_ref, dqg_ref, dkg_ref, ds_ref, dbias_ref):
        n = pl.program_id(0)
        mask = _swa_mask(n)

        @pl.when(n == 0)
        def _():
            for r in (dk_ref, dv_ref, dqg_ref, dkg_ref, ds_ref, dbias_ref):
                r[...] = jnp.zeros_like(r)

        cur = pl.ds(pl.multiple_of(n * BLOCK, BLOCK), BLOCK)
        prev = pl.ds(pl.multiple_of(jnp.maximum(n - 1, 0) * BLOCK, BLOCK), BLOCK)
        for kv in range(SWA_KV):
            heads, qs, kband, vband, sk, bs = _swa_load(kv, q_ref, kp_ref, kc_ref, vp_ref, vc_ref, s_ref, bias_ref)
            _, vjp = jax.vjp(lambda qs, kb, vb, qg, kg, sk, bs: _swa_heads(qs, kb, vb, qg, kg, sk, bs, mask),
                             qs, kband, vband, qg_ref[...], kg_ref[...], sk, bs)
            dqs, dkb, dvb, dqg, dkg, dsk, dbs = vjp([dy_ref[:, pl.ds(h * SWA_DIM, SWA_DIM)] for h in heads])
            cols = pl.ds(kv * SWA_DIM, SWA_DIM)
            for h, dq, dsink, db in zip(heads, dqs, dsk, dbs):
                dq_ref[:, pl.ds(h * SWA_DIM, SWA_DIM)] = dq.astype(BF16)
                ds_ref[:, pl.ds(h, 1)] += dsink
                dbias_ref[h] += db
            dqg_ref[...] += dqg
            dkg_ref[...] += dkg
            dk_ref[cur, cols] += dkb[BLOCK:]
            dv_ref[cur, cols] += dvb[BLOCK:]

            @pl.when(n > 0)
            def _():
                dk_ref[prev, cols] += dkb[:BLOCK]
                dv_ref[prev, cols] += dvb[:BLOCK]

    return pl.pallas_call(
        body, grid=(S // BLOCK,),
        in_specs=_swa_in_specs() + [pl.BlockSpec((BLOCK, SWA_WIDTH), lambda n: (n, 0))],
        out_specs=[pl.BlockSpec((BLOCK, SWA_WIDTH), lambda n: (n, 0)), _full((S, SWA_KVW)), _full((S, SWA_KVW)),
                   _full((1, SWA_DIM)), _full((1, SWA_DIM)), _full((1, SWA_HEADS)),
                   _full((SWA_HEADS, BLOCK, 2 * BLOCK))],
        out_shape=[jax.ShapeDtypeStruct((S, SWA_WIDTH), BF16), jax.ShapeDtypeStruct((S, SWA_KVW), F32),
                   jax.ShapeDtypeStruct((S, SWA_KVW), F32), jax.ShapeDtypeStruct((1, SWA_DIM), F32),
                   jax.ShapeDtypeStruct((1, SWA_DIM), F32), jax.ShapeDtypeStruct((1, SWA_HEADS), F32),
                   jax.ShapeDtypeStruct((SWA_HEADS, BLOCK, 2 * BLOCK), F32)],
        name="swa_bwd", compiler_params=_params(("arbitrary",)),
    )(proj, proj, proj, proj, proj, q_gain, k_gain, sinks, bias, dy)


def _position():
    return lax.axis_index("x"), lax.axis_index("y"), lax.axis_index("c")


def _all_gather(shard):
    R, C = shard.shape

    def body(x_ref, out_ref, send_sems, recv_sems, local_sem):
        x, y, c = _position()
        me, sibling = (x, y, c), (x, y, 1 - c)
        chips = [(1 - x, y), (x, 1 - y), (1 - x, 1 - y)]

        def slot(px, py, pc):
            return out_ref.at[4 * px + 2 * py + pc]

        def copy(k, block, to, src=None):
            return pltpu.make_async_remote_copy(
                src_ref=slot(*block) if src is None else src, dst_ref=slot(*block),
                send_sem=send_sems.at[k], recv_sem=recv_sems.at[k], device_id=to, device_id_type=MESH_ID)

        mine = pltpu.make_async_copy(x_ref, slot(*me), local_sem)
        mine.start()
        first = [copy(0, me, sibling, src=x_ref)]
        first += [copy(1 + j, me, (*chip, c), src=x_ref) for j, chip in enumerate(chips)]
        for cp in first:
            cp.start()
        passed = [copy(4 + j, (*chip, c), sibling) for j, chip in enumerate(chips)]
        for j, chip in enumerate(chips):
            copy(1 + j, (*chip, c), me).wait_recv()
            passed[j].start()
        copy(0, sibling, me).wait_recv()
        for j, chip in enumerate(chips):
            copy(4 + j, (*chip, 1 - c), me).wait_recv()
        for cp in first + passed:
            cp.wait_send()
        mine.wait()

    return pl.pallas_call(
        body, in_specs=[pl.BlockSpec(memory_space=pl.ANY)], out_specs=pl.BlockSpec(memory_space=pl.ANY),
        out_shape=jax.ShapeDtypeStruct((N_DEV, R, C), shard.dtype),
        scratch_shapes=[pltpu.SemaphoreType.DMA((7,)), pltpu.SemaphoreType.DMA((7,)), pltpu.SemaphoreType.DMA],
        name="all_gather_weights")(shard)


def _exchange(big, small):
    def body(big_ref, small_ref, bout_ref, sout_ref, send_sems, recv_sems, local_sems):
        x, y, c = _position()
        me = 4 * x + 2 * y + c
        lb = pltpu.make_async_copy(big_ref.at[me], bout_ref.at[me], local_sems.at[0])
        ls = pltpu.make_async_copy(small_ref.at[me], sout_ref.at[me], local_sems.at[1])
        lb.start()
        ls.start()
        sends, recvs = [], []
        for k in range(1, N_DEV):
            px, py, pc = x ^ (k >> 2), y ^ ((k >> 1) & 1), c ^ (k & 1)
            peer = 4 * px + 2 * py + pc
            for t, (src, dst) in enumerate(((big_ref, bout_ref), (small_ref, sout_ref))):
                sems = dict(send_sem=send_sems.at[2 * (k - 1) + t], recv_sem=recv_sems.at[2 * (k - 1) + t],
                            device_id=(px, py, pc), device_id_type=MESH_ID)
                sends.append(pltpu.make_async_remote_copy(src_ref=src.at[peer], dst_ref=dst.at[me], **sems))
                recvs.append(pltpu.make_async_remote_copy(src_ref=src.at[me], dst_ref=dst.at[peer], **sems))
        for cp in sends:
            cp.start()
        for cp in recvs:
            cp.wait_recv()
        for cp in sends:
            cp.wait_send()
        lb.wait()
        ls.wait()

    return pl.pallas_call(
        body, in_specs=[pl.BlockSpec(memory_space=pl.ANY)] * 2, out_specs=[pl.BlockSpec(memory_space=pl.ANY)] * 2,
        out_shape=[jax.ShapeDtypeStruct(big.shape, big.dtype), jax.ShapeDtypeStruct(small.shape, small.dtype)],
        scratch_shapes=[pltpu.SemaphoreType.DMA((14,)), pltpu.SemaphoreType.DMA((14,)), pltpu.SemaphoreType.DMA((2,))],
        name="exchange_grads")(big, small)


def _adam_update(parts, w, m, v, name, tr=512):
    R = w.shape[0]
    tr = _pick_rows(R, tr)

    def body(p_ref, w_ref, m_ref, v_ref, g_ref, d_ref, nm_ref, nv_ref):
        g = p_ref[0].astype(F32)
        for i in range(1, N_DEV):
            g = g + p_ref[i].astype(F32)
        delta, nm, nv = _adamw(w_ref[...], g, m_ref[...], v_ref[...])
        g_ref[...] = g
        d_ref[...] = delta
        nm_ref[...] = nm
        nv_ref[...] = nv

    rs = pl.BlockSpec((tr, LANES), lambda i: (i, 0))
    return pl.pallas_call(
        body, grid=(R // tr,), in_specs=[pl.BlockSpec((N_DEV, tr, LANES), lambda i: (0, i, 0)), rs, rs, rs],
        out_specs=[rs] * 4, out_shape=[jax.ShapeDtypeStruct((R, LANES), F32)] * 4, name=name,
        compiler_params=_params(("parallel",)))(parts, w, m, v)


def _pick_rows(rows, target):
    if rows <= target:
        return rows
    t = target
    while t >= 16:
        if rows % t == 0:
            return t
        t -= 16
    return rows


BIG = ("w_in", "dn_conv", "w_branch_dn", "w_branch_swa", "w_out", "w_gate", "w_up", "w_down")
BIG_SHAPES = {"w_in": (D_MODEL, D_IN // N_DEV), "dn_conv": (DN_CONV, DN_QKV // N_DEV),
              "w_branch_dn": (DN_WIDTH, D_MODEL // N_DEV), "w_branch_swa": (SWA_WIDTH, D_MODEL // N_DEV),
              "w_out": (D_MODEL // N_DEV, D_MODEL), "w_gate": (D_MODEL, D_FF // N_DEV),
              "w_up": (D_MODEL, D_FF // N_DEV), "w_down": (D_FF // N_DEV, D_MODEL)}
BIG_SPLIT_AXIS = {"w_in": 1, "dn_conv": 1, "w_branch_dn": 1, "w_branch_swa": 1, "w_out": 0, "w_gate": 1,
                  "w_up": 1, "w_down": 0}
BIG_ROW_TILE = 496


def _big_rows():
    n = sum(r * c for r, c in BIG_SHAPES.values())
    rows = -(-n // LANES)
    return -(-rows // BIG_ROW_TILE) * BIG_ROW_TILE


def _pack(flat_parts, rows):
    flat = jnp.concatenate(flat_parts, axis=-1)
    pad = rows * LANES - flat.shape[-1]
    flat = jnp.pad(flat, [(0, 0)] * (flat.ndim - 1) + [(0, pad)])
    return flat.reshape(flat.shape[:-1] + (rows, LANES))


def _unpack(buf, shapes):
    lead = buf.shape[:-2]
    flat = buf.reshape(lead + (-1,))
    out, off = [], 0
    for shp in shapes:
        n = int(np.prod(shp))
        out.append(flat[..., off:off + n].reshape(lead + tuple(shp)))
        off += n
    return out


def _bf16_pieces(w):
    hi = w.astype(BF16)
    r1 = w - hi.astype(F32)
    mid = r1.astype(BF16)
    lo = (r1 - mid.astype(F32)).astype(BF16)
    return [hi, mid, lo]


def _join(blocks, axis):
    if axis == 0:
        return blocks.reshape(-1, blocks.shape[-1])
    return blocks.transpose(1, 0, 2).reshape(blocks.shape[1], -1)


def _split(full, axis):
    r, c = full.shape
    if axis == 0:
        return full.reshape(N_DEV, -1)
    return full.reshape(r, N_DEV, c // N_DEV).transpose(1, 0, 2).reshape(N_DEV, -1)


def _to_padded_cols(w_in):
    def seg(a, n):
        return w_in[:, a:a + n]

    ba = jnp.concatenate([seg(R_B, 4), seg(R_A, 4), jnp.zeros((w_in.shape[0], LANES - 8), w_in.dtype)], axis=1)
    return jnp.concatenate([seg(R_GATE, 2048), seg(R_QKV, DN_QKV), seg(R_Z, DN_WIDTH), seg(R_SQ, SWA_WIDTH),
                            seg(R_SK, SWA_KVW), seg(R_SV, SWA_KVW), ba], axis=1)


def _from_padded_cols(g):
    def seg(a, n):
        return g[:, a:a + n]

    return jnp.concatenate([seg(P_QKV, DN_QKV), seg(P_Z, DN_WIDTH), seg(P_BA, 8), seg(P_SQ, SWA_WIDTH),
                            seg(P_SK, SWA_KVW), seg(P_SV, SWA_KVW), seg(P_GATE, 2048)], axis=1)


SMALL = ("attn_norm", "ffn_norm", "rel_bias", "dn_out_norm", "swa_q_norm", "swa_k_norm", "dn_a_log",
         "dn_dt_bias", "swa_sinks")
SMALL_SHAPES = {"attn_norm": (1, D_MODEL), "ffn_norm": (1, D_MODEL), "rel_bias": (REL_BUCKETS, SWA_HEADS),
                "dn_out_norm": (1, DN_DIM), "swa_q_norm": (1, SWA_DIM), "swa_k_norm": (1, SWA_DIM),
                "dn_a_log": (1, DN_HEADS), "dn_dt_bias": (1, DN_HEADS), "swa_sinks": (1, SWA_HEADS)}
SMALL_ROWS = 24


def _pack_small(d):
    return _pack([d[n].reshape(-1) for n in SMALL], SMALL_ROWS)


def kernel(x, attn_norm, w_in, dn_conv, dn_a_log, dn_dt_bias, dn_out_norm, swa_q_norm, swa_k_norm, swa_sinks, rel_bias, w_branch_dn, w_branch_swa, w_out, ffn_norm, w_gate, w_up, w_down, loss_target, m_attn_norm, m_w_in, m_dn_conv, m_dn_a_log, m_dn_dt_bias, m_dn_out_norm, m_swa_q_norm, m_swa_k_norm, m_swa_sinks, m_rel_bias, m_w_branch_dn, m_w_branch_swa, m_w_out, m_ffn_norm, m_w_gate, m_w_up, m_w_down, v_attn_norm, v_w_in, v_dn_conv, v_dn_a_log, v_dn_dt_bias, v_dn_out_norm, v_swa_q_norm, v_swa_k_norm, v_swa_sinks, v_rel_bias, v_w_branch_dn, v_w_branch_swa, v_w_out, v_ffn_norm, v_w_gate, v_w_up, v_w_down):
    args = dict(locals())
    S = x.shape[1]
    xs = x.reshape(S, D_MODEL)
    target = loss_target.reshape(S, D_MODEL)
    rows = _big_rows()

    w_loc = {n: args[n].reshape(BIG_SHAPES[n]) for n in BIG}
    shard = _pack([w_loc[n].reshape(-1) for n in BIG], rows)
    conv_pieces = _bf16_pieces(w_loc["dn_conv"].reshape(-1))
    wire = [w_loc[n].reshape(-1).astype(BF16) for n in BIG if n != "dn_conv"] + conv_pieces
    wire_shapes = [BIG_SHAPES[n] for n in BIG if n != "dn_conv"] + [BIG_SHAPES["dn_conv"]] * 3
    n_wire = sum(int(np.prod(s)) for s in wire_shapes)
    gathered = _all_gather(_pack(wire, -(-n_wire // (16 * LANES)) * 16))
    blocks = _unpack(gathered, wire_shapes)
    W = {n: _join(b, BIG_SPLIT_AXIS[n]) for n, b in zip([n for n in BIG if n != "dn_conv"], blocks)}
    w_pad = _to_padded_cols(W["w_in"])
    conv_w = sum(_join(b, BIG_SPLIT_AXIS["dn_conv"]).astype(F32) for b in blocks[-3:])

    h = _norm_fwd(xs, attn_norm, "norm1_fwd")
    proj = _mm(h, w_pad, "nn", F32, "mm_in", tn=1664)
    qkvn = _dn_conv_fwd(proj, conv_w)
    beta, g = _dn_gate_fwd(proj, dn_a_log, dn_dt_bias)
    u, w, qe, kd, qk, egl = _dn_prep_fwd(qkvn, g, beta)
    o, states = _dn_scan_fwd(u, w, qe, kd, qk, egl)
    y_dn = _dn_out_fwd(o, proj, dn_out_norm)
    bias = _bias_fwd(rel_bias)
    y_swa = _swa_fwd(proj, swa_q_norm, swa_k_norm, swa_sinks, bias)
    a_dn = _mm(y_dn, W["w_branch_dn"], "nn", F32, "mm_branch_dn")
    a_swa = _mm(y_swa, W["w_branch_swa"], "nn", F32, "mm_branch_swa")
    merged = _merge_fwd(proj, a_dn, a_swa)
    t_out = _mm(merged, W["w_out"], "nn", F32, "mm_out")
    x1, h2 = _resid_norm_fwd(xs, t_out, ffn_norm, "norm2_fwd")
    gate = _mm(h2, W["w_gate"], "nn", F32, "mm_gate", tn=1408)
    up = _mm(h2, W["w_up"], "nn", F32, "mm_up", tn=1408)
    act = _act_fwd(gate, up)
    f = _mm(act, W["w_down"], "nn", F32, "mm_down", tk=1408)
    dy, loss_local = _loss_fwd_bwd(x1, f, target)

    dact = _mm(dy, W["w_down"], "nt", F32, "mm_dact", tn=1408)
    g_w_down = _mm(act, dy, "tn", F32, "mm_dw_down", tm=1408)
    dgate, dup = _act_bwd(gate, up, dact)
    dh2_g = _mm(dgate, W["w_gate"], "nt", F32, "mm_dh2_gate", tk=1408)
    dh2_u = _mm(dup, W["w_up"], "nt", F32, "mm_dh2_up", tk=1408)
    g_w_gate = _mm(h2, dgate, "tn", F32, "mm_dw_gate", tn=1408)
    g_w_up = _mm(h2, dup, "tn", F32, "mm_dw_up", tn=1408)
    dx1, g_ffn_norm = _norm_bwd(x1, [dh2_g, dh2_u], dy, ffn_norm, "norm2_bwd")
    dmerged = _mm(dx1, W["w_out"], "nt", F32, "mm_dmerged")
    g_w_out = _mm(merged, dx1, "tn", F32, "mm_dw_out")
    dg0, dg1, da_dn, da_swa = _merge_bwd(proj, a_dn, a_swa, dmerged)
    dy_dn = _mm(da_dn, W["w_branch_dn"], "nt", F32, "mm_dy_dn")
    dy_swa = _mm(da_swa, W["w_branch_swa"], "nt", F32, "mm_dy_swa")
    g_w_bdn = _mm(y_dn, da_dn, "tn", F32, "mm_dw_branch_dn")
    g_w_bswa = _mm(y_swa, da_swa, "tn", F32, "mm_dw_branch_swa")
    dsq, dsk, dsv, g_q_norm, g_k_norm, g_sinks, dbias = _swa_bwd(proj, swa_q_norm, swa_k_norm, swa_sinks, bias, dy_swa)
    g_rel_bias = _bias_bwd(dbias)[:, :REL_BUCKETS].T
    do, dz, g_out_norm = _dn_out_bwd(o, proj, dn_out_norm, dy_dn)
    du, dw, dqe, dkd, dqk, degl = _dn_scan_bwd(u, w, qe, kd, qk, egl, states, do)
    dq, dk, dv, dgd, dbeta = _dn_prep_bwd(qkvn, g, beta, du, dw, dqe, dkd, dqk, degl)
    dqkvn = jnp.concatenate([dq, dk, dv], axis=1)
    dba, dal, ddt = _dn_gate_bwd(proj, dn_a_log, dn_dt_bias, dbeta, dgd)
    g_a_log = dal.reshape(DN_HEADS, DN_DIM).sum(axis=1)
    g_dt_bias = ddt.reshape(DN_HEADS, DN_DIM).sum(axis=1)
    dqkv, g_conv = _dn_conv_bwd(proj, conv_w, dqkvn)
    dproj = jnp.concatenate([dg0, dg1, dqkv, dz, dsq, dsk.astype(BF16), dsv.astype(BF16), dba], axis=1)
    dh = _mm(dproj, w_pad, "nt", F32, "mm_dh", tk=1664)
    g_w_pad = _mm(h, dproj, "tn", F32, "mm_dw_in", tn=1664)
    dx, g_attn_norm = _norm_bwd(xs, [dh], dx1, attn_norm, "norm1_bwd")

    g_full = {"w_in": _from_padded_cols(g_w_pad), "dn_conv": g_conv, "w_branch_dn": g_w_bdn,
              "w_branch_swa": g_w_bswa, "w_out": g_w_out, "w_gate": g_w_gate, "w_up": g_w_up, "w_down": g_w_down}
    send_big = _pack([_split(g_full[n], BIG_SPLIT_AXIS[n]) for n in BIG], rows).astype(BF16)
    g_small = {"attn_norm": g_attn_norm, "ffn_norm": g_ffn_norm, "rel_bias": g_rel_bias, "dn_out_norm": g_out_norm,
               "swa_q_norm": g_q_norm, "swa_k_norm": g_k_norm, "dn_a_log": g_a_log, "dn_dt_bias": g_dt_bias,
               "swa_sinks": g_sinks}
    send_small = jnp.broadcast_to(_pack_small(g_small)[None], (N_DEV, SMALL_ROWS, LANES))
    recv_big, recv_small = _exchange(send_big, send_small)

    m_big = _pack([args["m_" + n].reshape(-1) for n in BIG], rows)
    v_big = _pack([args["v_" + n].reshape(-1) for n in BIG], rows)
    big_out = _adam_update(recv_big, shard, m_big, v_big, "adam_big", tr=BIG_ROW_TILE)
    small_out = _adam_update(recv_small, _pack_small({n: args[n] for n in SMALL}),
                             _pack_small({n: args["m_" + n] for n in SMALL}),
                             _pack_small({n: args["v_" + n] for n in SMALL}), "adam_small")

    names = ("attn_norm", "w_in", "dn_conv", "dn_a_log", "dn_dt_bias", "dn_out_norm", "swa_q_norm", "swa_k_norm",
             "swa_sinks", "rel_bias", "w_branch_dn", "w_branch_swa", "w_out", "ffn_norm", "w_gate", "w_up", "w_down")
    results = []
    for kind in range(4):
        big = dict(zip(BIG, _unpack(big_out[kind], [args[n].shape for n in BIG])))
        small = dict(zip(SMALL, _unpack(small_out[kind], [args[n].shape for n in SMALL])))
        results += [big[n] if n in big else small[n] for n in names]

    loss = lax.psum(loss_local[0, 0], ("x", "y", "c"))
    return (loss, dx.reshape(x.shape), *results)
```

```python
import math

import numpy as np
import jax
import jax.numpy as jnp
from jax import lax
from jax.experimental import pallas as pl
from jax.experimental.pallas import tpu as pltpu

F32 = jnp.float32
BF16 = jnp.bfloat16
HI = lax.Precision.HIGHEST

D_MODEL = 1024
DN_HEADS = 4
DN_DIM = 128
DN_WIDTH = 512
DN_QKV = 1536
DN_CONV = 4
CHUNK = 64
SWA_HEADS = 8
SWA_KV = 2
SWA_GROUP = 4
SWA_DIM = 64
SWA_WIDTH = 512
SWA_KVW = 128
WINDOW = 128
BLOCK = 128
REL_BUCKETS = 32
REL_MAX_DIST = 128
D_FF = 2816
D_IN = 4872
EPS = 1e-6
N_DEV = 8

ADAM_LR = 0.001
ADAM_B1 = 0.9
ADAM_B2 = 0.999
ADAM_EPS = 1e-08
ADAM_WD = 0.01
ADAM_STEP = 10

P_GATE, P_QKV, P_Z, P_SQ, P_SK, P_SV, P_BA = 0, 2048, 3584, 4096, 4608, 4736, 4864
P_WIDTH = 4992
R_QKV, R_Z, R_B, R_A, R_SQ, R_SK, R_SV, R_GATE = 0, 1536, 2048, 2052, 2056, 2568, 2696, 2824

VMEM_LIMIT = 56 * 1024 * 1024
LANES = 128
MESH_ID = pl.DeviceIdType.MESH


def _params(sem=None):
    return pltpu.CompilerParams(dimension_semantics=sem, vmem_limit_bytes=VMEM_LIMIT)


def _pick(dim, target):
    if dim <= target:
        return dim
    t = target - target % LANES
    while t >= LANES:
        if dim % t == 0:
            return t
        t -= LANES
    return dim


_DIMS = {"nn": (((1,), (0,)), ((), ())), "nt": (((1,), (1,)), ((), ())), "tn": (((0,), (0,)), ((), ()))}


def _mm(a, b, mode, out_dtype, name, tm=512, tn=512, tk=1024, b_blocks=False, out_blocks=False):
    if b_blocks:
        nb, br, c = b.shape
        b_shape = (br, nb * c)
    else:
        b_shape = b.shape
    if mode == "nn":
        (M, K), (K2, N) = a.shape, b_shape
    elif mode == "nt":
        (M, K), (N, K2) = a.shape, b_shape
    else:
        (K, M), (K2, N) = a.shape, b_shape
    assert K == K2, (name, a.shape, b.shape)
    bm, bn, bk = _pick(M, tm), _pick(N, tn), _pick(K, tk)
    if b_blocks and mode == "nt":
        bk = c
    elif b_blocks:
        bn = c
    if out_blocks:
        c_out = N // N_DEV
        bn = c_out
    nk = K // bk
    dims = _DIMS[mode]

    def product(a_ref, b_ref):
        return lax.dot_general(a_ref[...].astype(BF16), b_ref[...].astype(BF16), dims, preferred_element_type=F32)

    def body_one(a_ref, b_ref, o_ref):
        o_ref[...] = product(a_ref, b_ref).astype(out_dtype)

    def body_acc(a_ref, b_ref, o_ref, acc_ref):
        k = pl.program_id(2)
        p = product(a_ref, b_ref)

        @pl.when(k == 0)
        def _():
            acc_ref[...] = p

        @pl.when(k > 0)
        def _():
            acc_ref[...] += p

        @pl.when(k == nk - 1)
        def _():
            o_ref[...] = acc_ref[...].astype(out_dtype)

    if mode == "tn":
        a_spec = pl.BlockSpec((bk, bm), lambda i, j, k: (k, i))
    else:
        a_spec = pl.BlockSpec((bm, bk), lambda i, j, k: (i, k))
    if b_blocks and mode == "nt":
        b_spec = pl.BlockSpec((None, bn, c), lambda i, j, k: (k, j, 0))
    elif b_blocks:
        b_spec = pl.BlockSpec((None, bk, c), lambda i, j, k: (j, k, 0))
    elif mode == "nt":
        b_spec = pl.BlockSpec((bn, bk), lambda i, j, k: (j, k))
    else:
        b_spec = pl.BlockSpec((bk, bn), lambda i, j, k: (k, j))
    if out_blocks:
        out_spec = pl.BlockSpec((None, bm, c_out), lambda i, j, k: (j, i, 0))
        out_shape = jax.ShapeDtypeStruct((N_DEV, M, c_out), out_dtype)
    else:
        out_spec = pl.BlockSpec((bm, bn), lambda i, j, k: (i, j))
        out_shape = jax.ShapeDtypeStruct((M, N), out_dtype)
    return pl.pallas_call(
        body_one if nk == 1 else body_acc, grid=(M // bm, N // bn, nk), in_specs=[a_spec, b_spec],
        out_specs=out_spec, out_shape=out_shape,
        scratch_shapes=[] if nk == 1 else [pltpu.VMEM((bm, bn), F32)], name=name,
        compiler_params=_params(("parallel", "parallel", "arbitrary")),
    )(a, b)


def _rms(x, gain):
    return x * lax.rsqrt(jnp.mean(x * x, axis=-1, keepdims=True) + EPS) * gain


def _silu(x):
    return x * jax.nn.sigmoid(x)


def _act(g, u):
    return _silu(g) * u


def _merge(g0, g1, a_dn, a_swa):
    return jax.nn.sigmoid(g0) * a_dn + jax.nn.sigmoid(g1) * a_swa


def _dn_post(c, is_v, q_scale):
    a = _silu(c)
    rs = lax.rsqrt(jnp.sum(a * a, axis=-1, keepdims=True) + EPS) * q_scale
    return a * jnp.where(is_v, 1.0, rs)


def _dn_out(o, z, gain):
    return _rms(o, gain) * _silu(z)


def _dot(a, b, dims=_DIMS["nn"], hi=False):
    if hi:
        return lax.dot_general(a, b, dims, precision=HI, preferred_element_type=F32)
    return lax.dot_general(a.astype(BF16), b.astype(BF16), dims, preferred_element_type=F32)


def _pieces(x):
    hi = x.astype(BF16)
    r1 = x - hi.astype(F32)
    mid = r1.astype(BF16)
    return hi, mid, (r1 - mid.astype(F32)).astype(BF16)


def _sel_left_impl(m, x):
    mb = m.astype(BF16)
    hi, mid, lo = _pieces(x)
    return _dot(mb, hi) + (_dot(mb, mid) + _dot(mb, lo))


@jax.custom_vjp
def _sel_left(m, mt, x):
    return _sel_left_impl(m, x)


_sel_left.defvjp(lambda m, mt, x: (_sel_left_impl(m, x), (m, mt)),
                 lambda res, ct: (jnp.zeros_like(res[0]), jnp.zeros_like(res[1]), _sel_left_impl(res[1], ct)))


def _sel_right_impl(x, s):
    sb = s.astype(BF16)
    hi, mid, lo = _pieces(x)
    return _dot(hi, sb) + (_dot(mid, sb) + _dot(lo, sb))


@jax.custom_vjp
def _sel_right(x, s, st):
    return _sel_right_impl(x, s)


_sel_right.defvjp(lambda x, s, st: (_sel_right_impl(x, s), (s, st)),
                  lambda res, ct: (_sel_right_impl(ct, res[1]), jnp.zeros_like(res[0]), jnp.zeros_like(res[1])))


def _dot3_impl(a, b):
    a_hi, a_lo, _ = _pieces(a)
    b_hi, b_lo, _ = _pieces(b)
    return _dot(a_hi, b_hi) + (_dot(a_hi, b_lo) + _dot(a_lo, b_hi))


@jax.custom_vjp
def _dot3(a, b):
    return _dot3_impl(a, b)


_dot3.defvjp(lambda a, b: (_dot3_impl(a, b), (a, b)),
             lambda res, ct: (_dot(ct, res[1], _DIMS["nt"]), _dot(res[0], ct, _DIMS["tn"])))


def _inv_impl(a, eye, strict):
    t = eye - a
    p = _dot(a, a)
    for level in range(5):
        t = t + _dot(t, p)
        if level < 4:
            p = _dot(p, p)
    t = t + _dot3_impl(t, eye - _dot3_impl(eye + a, t))
    return jnp.where(strict > 0.5, t, eye)


@jax.custom_vjp
def _inv_unit_lower(a, eye, strict):
    return _inv_impl(a, eye, strict)


def _inv_bwd(res, ct):
    t, eye, strict = res
    da = -_dot(_dot(t, ct, _DIMS["tn"]), t, _DIMS["nt"])
    return da, jnp.zeros_like(eye), jnp.zeros_like(strict)


def _inv_fwd(a, eye, strict):
    t = _inv_impl(a, eye, strict)
    return t, (t, eye, strict)


_inv_unit_lower.defvjp(_inv_fwd, _inv_bwd)

GROUP = 4
GROUP_ROWS = GROUP * CHUNK


def _block_consts(n):
    ii = lax.broadcasted_iota(jnp.int32, (n, n), 0)
    jj = lax.broadcasted_iota(jnp.int32, (n, n), 1)
    shift = CHUNK.bit_length() - 1
    same = jnp.right_shift(ii, shift) == jnp.right_shift(jj, shift)
    return same & (ii >= jj), same & (ii <= jj), same & (ii > jj), same, ii == jj


def _lane0(n):
    s = (lax.broadcasted_iota(jnp.int32, (LANES, n), 0) == 0).astype(F32)
    st = (lax.broadcasted_iota(jnp.int32, (n, LANES), 1) == 0).astype(F32)
    return s, st


def _dn_group(q, k, v, g, beta):
    n = GROUP_ROWS
    low_b, upp_b, strict_b, same_b, eye_b = _block_consts(n)
    low, upp, same, eye = low_b.astype(F32), upp_b.astype(F32), same_b.astype(F32), eye_b.astype(F32)
    s, st = _lane0(n)
    gc = _sel_left(low, upp, g)
    gl = _sel_left(same, same, g)
    col = _sel_right(gc, s, st)
    decay = jnp.exp(jnp.where(low_b, col - col.T, -jnp.inf))
    kb = k * beta
    vb = v * beta
    a = jnp.where(strict_b, _dot(kb, k, _DIMS["nt"]) * decay, 0.0)
    t = _inv_unit_lower(a, eye, strict_b.astype(F32))
    u = _dot3(t, vb)
    w = _dot3(t, kb * jnp.exp(gc))
    return u, w, q * jnp.exp(gc), k * jnp.exp(gl - gc)


def _dn_chunk(q, k, g):
    ii = lax.broadcasted_iota(jnp.int32, (CHUNK, CHUNK), 0)
    jj = lax.broadcasted_iota(jnp.int32, (CHUNK, CHUNK), 1)
    low = (ii >= jj).astype(F32)
    upp = (ii <= jj).astype(F32)
    ones = jnp.ones((CHUNK, CHUNK), F32)
    eye = (ii == jj).astype(F32)
    s, st = _lane0(CHUNK)
    gc = _sel_left(low, upp, g)
    col = _sel_right(gc, s, st)
    row = _sel_left(ones, ones, col * eye)
    decay = jnp.exp(jnp.where(ii >= jj, col - row, -jnp.inf))
    qk = _dot(q, k, _DIMS["nt"]) * decay
    return qk, jnp.exp(jnp.sum(g, axis=0, keepdims=True))


def _dn_step(s, u, w, qe, kd, qk, egl):
    v_new = u - _dot(w, s)
    o = _dot(qe, s) + _dot(qk, v_new)
    s_new = s * egl + _dot(kd, v_new, _DIMS["tn"])
    return s_new, o


def _swa_heads(qs, kband, vband, qg, kg, sinks, biases, mask):
    kn = _rms(kband, kg)
    outs = []
    for q, sink, bias in zip(qs, sinks, biases):
        qn = _rms(q, qg)
        logits = _dot(qn, kn, _DIMS["nt"]) * (SWA_DIM ** -0.5)
        logits = jnp.where(mask, logits + bias, -jnp.inf)
        m = jnp.maximum(jnp.max(logits, axis=-1, keepdims=True), sink)
        p = jnp.exp(logits - m)
        denom = jnp.sum(p, axis=-1, keepdims=True) + jnp.exp(sink - m)
        outs.append(_dot(p / denom, vband))
    return outs


def _adamw(w, g, m, v):
    m = ADAM_B1 * m + (1.0 - ADAM_B1) * g
    v = ADAM_B2 * v + (1.0 - ADAM_B2) * jnp.square(g)
    m_hat = m / (1.0 - ADAM_B1 ** ADAM_STEP)
    v_hat = v / (1.0 - ADAM_B2 ** ADAM_STEP)
    delta = -ADAM_LR * (m_hat / (jnp.sqrt(v_hat) + ADAM_EPS) + ADAM_WD * w)
    return delta, m, v


def _row(tm, c, cb=0):
    return pl.BlockSpec((tm, c), lambda i, cb=cb: (i, cb))


def _full(shape):
    nd = len(shape)
    return pl.BlockSpec(shape, lambda *_, nd=nd: (0,) * nd)


def _norm_fwd(x, gain, name, tm=512):
    S = x.shape[0]

    def body(x_ref, g_ref, h_ref):
        h_ref[...] = _rms(x_ref[...], g_ref[...]).astype(BF16)

    return pl.pallas_call(
        body, grid=(S // tm,), in_specs=[_row(tm, D_MODEL), _full((1, D_MODEL))],
        out_specs=_row(tm, D_MODEL), out_shape=jax.ShapeDtypeStruct((S, D_MODEL), BF16),
        name=name, compiler_params=_params(("parallel",)))(x, gain)


def _resid_norm_fwd(x, t, gain, name, tm=512):
    S = x.shape[0]

    def body(x_ref, t_ref, g_ref, x1_ref, h_ref):
        x1 = x_ref[...] + t_ref[...]
        x1_ref[...] = x1
        h_ref[...] = _rms(x1, g_ref[...]).astype(BF16)

    return pl.pallas_call(
        body, grid=(S // tm,), in_specs=[_row(tm, D_MODEL), _row(tm, D_MODEL), _full((1, D_MODEL))],
        out_specs=[_row(tm, D_MODEL), _row(tm, D_MODEL)],
        out_shape=[jax.ShapeDtypeStruct((S, D_MODEL), F32), jax.ShapeDtypeStruct((S, D_MODEL), BF16)],
        name=name, compiler_params=_params(("parallel",)))(x, t, gain)


def _norm_bwd(x, dh_list, dres, gain, name, tm=256):
    S = x.shape[0]
    n = len(dh_list)

    def body(*refs):
        x_ref, g_ref, r_ref = refs[0], refs[1], refs[2]
        dh_refs = refs[3:3 + n]
        dx_ref, dg_ref = refs[3 + n], refs[4 + n]
        dh = dh_refs[0][...].astype(F32)
        for r in dh_refs[1:]:
            dh = dh + r[...].astype(F32)
        _, vjp = jax.vjp(_rms, x_ref[...], g_ref[...])
        dx, dg = vjp(dh)
        dx_ref[...] = dx + r_ref[...]

        @pl.when(pl.program_id(0) == 0)
        def _():
            dg_ref[...] = jnp.zeros_like(dg_ref)

        dg_ref[...] += dg

    return pl.pallas_call(
        body, grid=(S // tm,),
        in_specs=[_row(tm, D_MODEL), _full((1, D_MODEL)), _row(tm, D_MODEL)] + [_row(tm, D_MODEL)] * n,
        out_specs=[_row(tm, D_MODEL), _full((1, D_MODEL))],
        out_shape=[jax.ShapeDtypeStruct((S, D_MODEL), F32), jax.ShapeDtypeStruct((1, D_MODEL), F32)],
        name=name, compiler_params=_params(("arbitrary",)))(x, gain, dres, *dh_list)


def _loss_fwd_bwd(x1, f, target, tm=512):
    S = x1.shape[0]

    def body(x_ref, f_ref, t_ref, dy_ref, l_ref):
        diff = x_ref[...] + f_ref[...] - t_ref[...]
        dy_ref[...] = diff * (1.0 / D_MODEL)

        @pl.when(pl.program_id(0) == 0)
        def _():
            l_ref[...] = jnp.zeros_like(l_ref)

        l_ref[...] += jnp.sum(jnp.mean(diff * diff, axis=-1, keepdims=True), axis=0, keepdims=True) * 0.5

    return pl.pallas_call(
        body, grid=(S // tm,), in_specs=[_row(tm, D_MODEL)] * 3,
        out_specs=[_row(tm, D_MODEL), _full((1, 1))],
        out_shape=[jax.ShapeDtypeStruct((S, D_MODEL), F32), jax.ShapeDtypeStruct((1, 1), F32)],
        name="loss_fwd_bwd", compiler_params=_params(("arbitrary",)))(x1, f, target)


def _act_fwd(g, u, tm=256):
    S, width = g.shape

    def body(g_ref, u_ref, o_ref):
        o_ref[...] = _act(g_ref[...], u_ref[...]).astype(BF16)

    return pl.pallas_call(
        body, grid=(S // tm,), in_specs=[_row(tm, width)] * 2, out_specs=_row(tm, width),
        out_shape=jax.ShapeDtypeStruct((S, width), BF16), name="act_fwd",
        compiler_params=_params(("parallel",)))(g, u)


def _act_bwd(g, u, dact, tm=256):
    S, width = g.shape

    def body(g_ref, u_ref, d_ref, dg_ref, du_ref):
        _, vjp = jax.vjp(_act, g_ref[...], u_ref[...])
        dg, du = vjp(d_ref[...])
        dg_ref[...] = dg.astype(BF16)
        du_ref[...] = du.astype(BF16)

    return pl.pallas_call(
        body, grid=(S // tm,), in_specs=[_row(tm, width)] * 3, out_specs=[_row(tm, width)] * 2,
        out_shape=[jax.ShapeDtypeStruct((S, width), BF16)] * 2, name="act_bwd",
        compiler_params=_params(("parallel",)))(g, u, dact)


def _merge_fwd(proj, a_dn, a_swa, tm=512, tc=512):
    S = proj.shape[0]
    nc = D_MODEL // tc

    def spec(off):
        return pl.BlockSpec((tm, tc), lambda i, j, off=off: (i, off + j))

    def body(g0_ref, g1_ref, ad_ref, as_ref, o_ref):
        o_ref[...] = _merge(g0_ref[...], g1_ref[...], ad_ref[...], as_ref[...]).astype(BF16)

    return pl.pallas_call(
        body, grid=(S // tm, nc), in_specs=[spec(P_GATE // tc), spec(P_GATE // tc + nc), spec(0), spec(0)],
        out_specs=spec(0), out_shape=jax.ShapeDtypeStruct((S, D_MODEL), BF16), name="merge_fwd",
        compiler_params=_params(("parallel", "parallel")))(proj, proj, a_dn, a_swa)


def _merge_bwd(proj, a_dn, a_swa, dmerged, tm=512, tc=512):
    S = proj.shape[0]
    nc = D_MODEL // tc

    def spec(off):
        return pl.BlockSpec((tm, tc), lambda i, j, off=off: (i, off + j))

    def body(g0_ref, g1_ref, ad_ref, as_ref, d_ref, dg0_ref, dg1_ref, dad_ref, das_ref):
        _, vjp = jax.vjp(_merge, g0_ref[...], g1_ref[...], ad_ref[...], as_ref[...])
        dg0, dg1, dad, das = vjp(d_ref[...])
        dg0_ref[...] = dg0.astype(BF16)
        dg1_ref[...] = dg1.astype(BF16)
        dad_ref[...] = dad.astype(BF16)
        das_ref[...] = das.astype(BF16)

    return pl.pallas_call(
        body, grid=(S // tm, nc),
        in_specs=[spec(P_GATE // tc), spec(P_GATE // tc + nc), spec(0), spec(0), spec(0)],
        out_specs=[spec(0)] * 4, out_shape=[jax.ShapeDtypeStruct((S, D_MODEL), BF16)] * 4,
        name="merge_bwd", compiler_params=_params(("parallel", "parallel")))(proj, proj, a_dn, a_swa, dmerged)


def _shift_down(x, s):
    row = lax.broadcasted_iota(jnp.int32, x.shape, 0)
    return jnp.where(row >= s, pltpu.roll(x, s, axis=0), 0.0)


def _shift_up(x, s):
    n = x.shape[0]
    row = lax.broadcasted_iota(jnp.int32, x.shape, 0)
    return jnp.where(row < n - s, pltpu.roll(x, n - s, axis=0), 0.0)


def _conv(x, w):
    out = w[DN_CONV - 1:DN_CONV] * x
    for s in range(1, DN_CONV):
        out = out + w[DN_CONV - 1 - s:DN_CONV - s] * _shift_down(x, s)
    return out


def _dn_conv_fwd(proj, conv_w):
    S = proj.shape[0]
    nb = DN_QKV // LANES

    def body(x_ref, w_ref, o_ref):
        j = pl.program_id(0)
        q_scale = jnp.where(j < DN_HEADS, DN_DIM ** -0.5, 1.0).astype(F32)
        o_ref[...] = _dn_post(_conv(x_ref[...], w_ref[...]), j >= 2 * DN_HEADS, q_scale)

    return pl.pallas_call(
        body, grid=(nb,),
        in_specs=[pl.BlockSpec((S, LANES), lambda j: (0, P_QKV // LANES + j)),
                  pl.BlockSpec((DN_CONV, LANES), lambda j: (0, j))],
        out_specs=pl.BlockSpec((S, LANES), lambda j: (0, j)),
        out_shape=jax.ShapeDtypeStruct((S, DN_QKV), F32), name="dn_conv_fwd",
        compiler_params=_params(("parallel",)))(proj, conv_w)


def _dn_conv_bwd(proj, conv_w, dqkvn):
    S = proj.shape[0]
    nb = DN_QKV // LANES

    def body(x_ref, w_ref, d_ref, dx_ref, dw_ref):
        j = pl.program_id(0)
        q_scale = jnp.where(j < DN_HEADS, DN_DIM ** -0.5, 1.0).astype(F32)
        x = x_ref[...]
        w = w_ref[...]
        _, vjp = jax.vjp(lambda c: _dn_post(c, j >= 2 * DN_HEADS, q_scale), _conv(x, w))
        (dc,) = vjp(d_ref[...])
        dx = w[DN_CONV - 1:DN_CONV] * dc
        dw_ref[DN_CONV - 1:DN_CONV, :] = jnp.sum(dc * x, axis=0, keepdims=True)
        for s in range(1, DN_CONV):
            dx = dx + w[DN_CONV - 1 - s:DN_CONV - s] * _shift_up(dc, s)
            dw_ref[DN_CONV - 1 - s:DN_CONV - s, :] = jnp.sum(dc * _shift_down(x, s), axis=0, keepdims=True)
        dx_ref[...] = dx.astype(BF16)

    return pl.pallas_call(
        body, grid=(nb,),
        in_specs=[pl.BlockSpec((S, LANES), lambda j: (0, P_QKV // LANES + j)),
                  pl.BlockSpec((DN_CONV, LANES), lambda j: (0, j)),
                  pl.BlockSpec((S, LANES), lambda j: (0, j))],
        out_specs=[pl.BlockSpec((S, LANES), lambda j: (0, j)), pl.BlockSpec((DN_CONV, LANES), lambda j: (0, j))],
        out_shape=[jax.ShapeDtypeStruct((S, DN_QKV), BF16), jax.ShapeDtypeStruct((DN_CONV, DN_QKV), F32)],
        name="dn_conv_bwd", compiler_params=_params(("parallel",)))(proj, conv_w, dqkvn)


def _expanders():
    eb = np.zeros((LANES, DN_WIDTH), np.float32)
    ea = np.zeros((LANES, DN_WIDTH), np.float32)
    for h in range(DN_HEADS):
        eb[h, h * DN_DIM:(h + 1) * DN_DIM] = 1.0
        ea[DN_HEADS + h, h * DN_DIM:(h + 1) * DN_DIM] = 1.0
    return jnp.asarray(eb), jnp.asarray(ea)


def _dn_gate_args(a_log, dt_bias):
    eb, ea = _expanders()
    alog = jnp.repeat(a_log.reshape(1, DN_HEADS), DN_DIM, axis=1)
    dtb = jnp.repeat(dt_bias.reshape(1, DN_HEADS), DN_DIM, axis=1)
    return eb, ea, alog, dtb


def _dn_gate_specs(tm):
    return [_row(tm, LANES, P_BA // LANES), _full((LANES, DN_WIDTH)), _full((LANES, DN_WIDTH)),
            _full((1, DN_WIDTH)), _full((1, DN_WIDTH))]


def _dn_gate_fn(ba, eb, ea, alog, dtb):
    beta = jax.nn.sigmoid(_dot(ba, eb, hi=True))
    g = -jnp.exp(alog) * jax.nn.softplus(_dot(ba, ea, hi=True) + dtb)
    return beta, g


def _dn_gate_fwd(proj, a_log, dt_bias, tm=512):
    S = proj.shape[0]
    args = _dn_gate_args(a_log, dt_bias)

    def body(ba_ref, eb_ref, ea_ref, al_ref, dt_ref, beta_ref, g_ref):
        beta, g = _dn_gate_fn(ba_ref[...], eb_ref[...], ea_ref[...], al_ref[...], dt_ref[...])
        beta_ref[...] = beta
        g_ref[...] = g

    return pl.pallas_call(
        body, grid=(S // tm,), in_specs=_dn_gate_specs(tm), out_specs=[_row(tm, DN_WIDTH), _row(tm, DN_WIDTH)],
        out_shape=[jax.ShapeDtypeStruct((S, DN_WIDTH), F32), jax.ShapeDtypeStruct((S, DN_WIDTH), F32)],
        name="dn_gate_fwd", compiler_params=_params(("parallel",)))(proj, *args)


def _dn_gate_bwd(proj, a_log, dt_bias, dbeta, dg, tm=512):
    S = proj.shape[0]
    args = _dn_gate_args(a_log, dt_bias)

    def body(ba_ref, eb_ref, ea_ref, al_ref, dt_ref, dbeta_ref, dg_ref, dba_ref, dal_ref, ddt_ref):
        eb, ea = eb_ref[...], ea_ref[...]
        _, vjp = jax.vjp(lambda ba, al, dt: _dn_gate_fn(ba, eb, ea, al, dt), ba_ref[...], al_ref[...], dt_ref[...])
        dba, dal, ddt = vjp((dbeta_ref[...], dg_ref[...]))
        dba_ref[...] = dba.astype(BF16)

        @pl.when(pl.program_id(0) == 0)
        def _():
            dal_ref[...] = jnp.zeros_like(dal_ref)
            ddt_ref[...] = jnp.zeros_like(ddt_ref)

        dal_ref[...] += dal
        ddt_ref[...] += ddt

    return pl.pallas_call(
        body, grid=(S // tm,), in_specs=_dn_gate_specs(tm) + [_row(tm, DN_WIDTH), _row(tm, DN_WIDTH)],
        out_specs=[_row(tm, LANES), _full((1, DN_WIDTH)), _full((1, DN_WIDTH))],
        out_shape=[jax.ShapeDtypeStruct((S, LANES), BF16), jax.ShapeDtypeStruct((1, DN_WIDTH), F32),
                   jax.ShapeDtypeStruct((1, DN_WIDTH), F32)],
        name="dn_gate_bwd", compiler_params=_params(("arbitrary",)))(proj, *args, dbeta, dg)


PREP_CHUNKS = GROUP


def _dn_prep_specs():
    rows = PREP_CHUNKS * CHUNK
    q = pl.BlockSpec((rows, LANES), lambda h, c: (c, h))
    k = pl.BlockSpec((rows, LANES), lambda h, c: (c, DN_HEADS + h))
    v = pl.BlockSpec((rows, LANES), lambda h, c: (c, 2 * DN_HEADS + h))
    qk = pl.BlockSpec((1, rows, CHUNK), lambda h, c: (h, c, 0))
    egl = pl.BlockSpec((1, PREP_CHUNKS, 1, LANES), lambda h, c: (h, c, 0, 0))
    return q, k, v, qk, egl


def _dn_prep_fwd(qkvn, g, beta):
    S = qkvn.shape[0]
    nc = S // CHUNK
    q, k, v, qks, egl = _dn_prep_specs()

    def body(q_ref, k_ref, v_ref, g_ref, b_ref, u_ref, w_ref, qe_ref, kd_ref, qk_ref, egl_ref):
        u, w, qe, kd = _dn_group(q_ref[...], k_ref[...], v_ref[...], g_ref[...], b_ref[...])
        u_ref[...] = u
        w_ref[...] = w
        qe_ref[...] = qe
        kd_ref[...] = kd
        for i in range(PREP_CHUNKS):
            r = pl.ds(i * CHUNK, CHUNK)
            qk, e = _dn_chunk(q_ref[r, :], k_ref[r, :], g_ref[r, :])
            qk_ref[0, r, :] = qk
            egl_ref[0, i] = e

    wide = jax.ShapeDtypeStruct((S, DN_WIDTH), F32)
    return pl.pallas_call(
        body, grid=(DN_HEADS, nc // PREP_CHUNKS), in_specs=[q, k, v, q, q],
        out_specs=[q, q, q, q, qks, egl],
        out_shape=[wide, wide, wide, wide, jax.ShapeDtypeStruct((DN_HEADS, S, CHUNK), F32),
                   jax.ShapeDtypeStruct((DN_HEADS, nc, 1, LANES), F32)],
        name="dn_prep_fwd", compiler_params=_params(("parallel", "parallel")))(qkvn, qkvn, qkvn, g, beta)


def _dn_prep_bwd(qkvn, g, beta, du, dw, dqe, dkd, dqk, degl):
    S = qkvn.shape[0]
    nc = S // CHUNK
    q, k, v, qks, egl = _dn_prep_specs()

    def body(q_ref, k_ref, v_ref, g_ref, b_ref, du_ref, dw_ref, dqe_ref, dkd_ref, dqk_ref, degl_ref,
             dq_ref, dk_ref, dv_ref, dg_ref, db_ref):
        _, vjp = jax.vjp(_dn_group, q_ref[...], k_ref[...], v_ref[...], g_ref[...], b_ref[...])
        dq, dk, dv, dg, db = vjp((du_ref[...], dw_ref[...], dqe_ref[...], dkd_ref[...]))
        dq_ref[...] = dq
        dk_ref[...] = dk
        dv_ref[...] = dv
        dg_ref[...] = dg
        db_ref[...] = db
        for i in range(PREP_CHUNKS):
            r = pl.ds(i * CHUNK, CHUNK)
            _, vjp = jax.vjp(_dn_chunk, q_ref[r, :], k_ref[r, :], g_ref[r, :])
            dq, dk, dg = vjp((dqk_ref[0, r, :], degl_ref[0, i]))
            dq_ref[r, :] += dq
            dk_ref[r, :] += dk
            dg_ref[r, :] += dg

    wide = jax.ShapeDtypeStruct((S, DN_WIDTH), F32)
    return pl.pallas_call(
        body, grid=(DN_HEADS, nc // PREP_CHUNKS), in_specs=[q, k, v, q, q, q, q, q, q, qks, egl],
        out_specs=[q] * 5, out_shape=[wide] * 5,
        name="dn_prep_bwd", compiler_params=_params(("parallel", "parallel")),
    )(qkvn, qkvn, qkvn, g, beta, du, dw, dqe, dkd, dqk, degl)


def _dn_scan_specs(nc, reverse):
    def cidx(c):
        return nc - 1 - c if reverse else c

    hc = pl.BlockSpec((CHUNK, DN_WIDTH), lambda c: (cidx(c), 0))
    qk = pl.BlockSpec((DN_HEADS, CHUNK, CHUNK), lambda c: (0, cidx(c), 0))
    egl = pl.BlockSpec((DN_HEADS, 1, 1, LANES), lambda c: (0, cidx(c), 0, 0))
    st = pl.BlockSpec((DN_HEADS, 1, DN_DIM, DN_DIM), lambda c: (0, cidx(c), 0, 0))
    return hc, qk, egl, st


def _dn_scan_fwd(u, w, qe, kd, qk, egl):
    S = u.shape[0]
    nc = S // CHUNK
    hc, qks, egls, st = _dn_scan_specs(nc, False)

    def body(u_ref, w_ref, qe_ref, kd_ref, qk_ref, egl_ref, o_ref, st_ref, s_scr):
        @pl.when(pl.program_id(0) == 0)
        def _():
            s_scr[...] = jnp.zeros_like(s_scr)

        for h in range(DN_HEADS):
            cols = pl.ds(h * DN_DIM, DN_DIM)
            s = s_scr[h]
            st_ref[h, 0] = s
            s_new, o = _dn_step(s, u_ref[:, cols], w_ref[:, cols], qe_ref[:, cols], kd_ref[:, cols], qk_ref[h],
                                egl_ref[h, 0])
            o_ref[:, cols] = o
            s_scr[h] = s_new

    return pl.pallas_call(
        body, grid=(nc,), in_specs=[hc, hc, hc, hc, qks, egls], out_specs=[hc, st],
        out_shape=[jax.ShapeDtypeStruct((S, DN_WIDTH), F32), jax.ShapeDtypeStruct((DN_HEADS, nc, DN_DIM, DN_DIM), F32)],
        scratch_shapes=[pltpu.VMEM((DN_HEADS, DN_DIM, DN_DIM), F32)], name="dn_scan_fwd",
        compiler_params=_params(("arbitrary",)))(u, w, qe, kd, qk, egl)


def _dn_scan_bwd(u, w, qe, kd, qk, egl, states, do):
    S = u.shape[0]
    nc = S // CHUNK
    hc, qks, egls, st = _dn_scan_specs(nc, True)

    def body(u_ref, w_ref, qe_ref, kd_ref, qk_ref, egl_ref, st_ref, do_ref,
             du_ref, dw_ref, dqe_ref, dkd_ref, dqk_ref, degl_ref, ds_scr):
        @pl.when(pl.program_id(0) == 0)
        def _():
            ds_scr[...] = jnp.zeros_like(ds_scr)

        for h in range(DN_HEADS):
            cols = pl.ds(h * DN_DIM, DN_DIM)
            _, vjp = jax.vjp(_dn_step, st_ref[h, 0], u_ref[:, cols], w_ref[:, cols], qe_ref[:, cols],
                             kd_ref[:, cols], qk_ref[h], egl_ref[h, 0])
            ds, du, dw, dqe, dkd, dqk, degl = vjp((ds_scr[h], do_ref[:, cols]))
            ds_scr[h] = ds
            du_ref[:, cols] = du
            dw_ref[:, cols] = dw
            dqe_ref[:, cols] = dqe
            dkd_ref[:, cols] = dkd
            dqk_ref[h] = dqk
            degl_ref[h, 0] = degl

    wide = jax.ShapeDtypeStruct((S, DN_WIDTH), F32)
    return pl.pallas_call(
        body, grid=(nc,), in_specs=[hc, hc, hc, hc, qks, egls, st, hc],
        out_specs=[hc, hc, hc, hc, qks, egls],
        out_shape=[wide, wide, wide, wide, jax.ShapeDtypeStruct((DN_HEADS, S, CHUNK), F32),
                   jax.ShapeDtypeStruct((DN_HEADS, nc, 1, LANES), F32)],
        scratch_shapes=[pltpu.VMEM((DN_HEADS, DN_DIM, DN_DIM), F32)], name="dn_scan_bwd",
        compiler_params=_params(("arbitrary",)))(u, w, qe, kd, qk, egl, states, do)


def _dn_out_fwd(o, proj, gain, tm=512):
    S = o.shape[0]

    def body(o_ref, z_ref, g_ref, y_ref):
        y_ref[...] = _dn_out(o_ref[...], z_ref[...], g_ref[...]).astype(BF16)

    hs = pl.BlockSpec((tm, LANES), lambda i, h: (i, h))
    zs = pl.BlockSpec((tm, LANES), lambda i, h: (i, P_Z // LANES + h))
    return pl.pallas_call(
        body, grid=(S // tm, DN_HEADS), in_specs=[hs, zs, _full((1, DN_DIM))], out_specs=hs,
        out_shape=jax.ShapeDtypeStruct((S, DN_WIDTH), BF16), name="dn_out_fwd",
        compiler_params=_params(("parallel", "parallel")))(o, proj, gain)


def _dn_out_bwd(o, proj, gain, dy, tm=512):
    S = o.shape[0]

    def body(o_ref, z_ref, g_ref, dy_ref, do_ref, dz_ref, dg_ref):
        _, vjp = jax.vjp(_dn_out, o_ref[...], z_ref[...], g_ref[...])
        do, dz, dg = vjp(dy_ref[...])
        do_ref[...] = do
        dz_ref[...] = dz.astype(BF16)

        @pl.when((pl.program_id(0) == 0) & (pl.program_id(1) == 0))
        def _():
            dg_ref[...] = jnp.zeros_like(dg_ref)

        dg_ref[...] += dg

    hs = pl.BlockSpec((tm, LANES), lambda i, h: (i, h))
    zs = pl.BlockSpec((tm, LANES), lambda i, h: (i, P_Z // LANES + h))
    return pl.pallas_call(
        body, grid=(S // tm, DN_HEADS), in_specs=[hs, zs, _full((1, DN_DIM)), hs],
        out_specs=[hs, hs, _full((1, DN_DIM))],
        out_shape=[jax.ShapeDtypeStruct((S, DN_WIDTH), F32), jax.ShapeDtypeStruct((S, DN_WIDTH), BF16),
                   jax.ShapeDtypeStruct((1, DN_DIM), F32)],
        name="dn_out_bwd", compiler_params=_params(("arbitrary", "arbitrary")))(o, proj, gain, dy)


def _rel_buckets():
    qi = np.arange(BLOCK)[:, None]
    kj = np.arange(2 * BLOCK)[None, :]
    n = np.maximum(BLOCK + qi - kj, 0)
    max_exact = REL_BUCKETS // 2
    nf = np.maximum(n, 1).astype(np.float32)
    large = max_exact + (np.log(nf / np.float32(max_exact)) / np.float32(math.log(REL_MAX_DIST / max_exact))
                         * np.float32(REL_BUCKETS - max_exact)).astype(np.int32)
    large = np.minimum(large, REL_BUCKETS - 1)
    return np.where(n < max_exact, n, large).astype(np.int32)


def _bias_fwd(rel_bias):
    buckets = jnp.asarray(_rel_buckets())

    def body(rb_ref, bk_ref, o_ref):
        bk = bk_ref[...]
        for h in range(SWA_HEADS):
            acc = jnp.zeros((BLOCK, 2 * BLOCK), F32)
            for b in range(REL_BUCKETS):
                acc = jnp.where(bk == b, rb_ref[b, h], acc)
            o_ref[h] = acc

    return pl.pallas_call(
        body, in_specs=[pl.BlockSpec(memory_space=pltpu.SMEM), pl.BlockSpec(memory_space=pltpu.VMEM)],
        out_specs=pl.BlockSpec(memory_space=pltpu.VMEM),
        out_shape=jax.ShapeDtypeStruct((SWA_HEADS, BLOCK, 2 * BLOCK), F32), name="swa_bias_fwd",
        compiler_params=_params())(rel_bias, buckets)


def _bias_bwd(dbias):
    buckets = jnp.asarray(_rel_buckets())

    def body(d_ref, bk_ref, o_ref):
        bk = bk_ref[...]
        lane = lax.broadcasted_iota(jnp.int32, (1, LANES), 1)
        for h in range(SWA_HEADS):
            d = d_ref[h]
            row = jnp.zeros((1, LANES), F32)
            for b in range(REL_BUCKETS):
                part = jnp.sum(jnp.where(bk == b, d, 0.0), axis=1, keepdims=True)
                row = jnp.where(lane == b, jnp.sum(part, axis=0, keepdims=True), row)
            o_ref[h:h + 1, :] = row

    return pl.pallas_call(
        body, in_specs=[pl.BlockSpec(memory_space=pltpu.VMEM), pl.BlockSpec(memory_space=pltpu.VMEM)],
        out_specs=pl.BlockSpec(memory_space=pltpu.VMEM),
        out_shape=jax.ShapeDtypeStruct((SWA_HEADS, LANES), F32), name="swa_bias_bwd",
        compiler_params=_params())(dbias, buckets)


def _swa_mask(n):
    qi = lax.broadcasted_iota(jnp.int32, (BLOCK, 2 * BLOCK), 0)
    kj = lax.broadcasted_iota(jnp.int32, (BLOCK, 2 * BLOCK), 1)
    dist = BLOCK + qi - kj
    return (dist >= 0) & (dist < WINDOW) & ((n > 0) | (kj >= BLOCK))


def _swa_in_specs():
    q = pl.BlockSpec((BLOCK, SWA_WIDTH), lambda n: (n, P_SQ // SWA_WIDTH))
    kc = pl.BlockSpec((BLOCK, SWA_KVW), lambda n: (n, P_SK // SWA_KVW))
    kp = pl.BlockSpec((BLOCK, SWA_KVW), lambda n: (jnp.maximum(n - 1, 0), P_SK // SWA_KVW))
    vc = pl.BlockSpec((BLOCK, SWA_KVW), lambda n: (n, P_SV // SWA_KVW))
    vp = pl.BlockSpec((BLOCK, SWA_KVW), lambda n: (jnp.maximum(n - 1, 0), P_SV // SWA_KVW))
    small = [_full((1, SWA_DIM)), _full((1, SWA_DIM)), _full((1, SWA_HEADS)),
             _full((SWA_HEADS, BLOCK, 2 * BLOCK))]
    return [q, kp, kc, vp, vc] + small


def _swa_load(kv, q_ref, kp_ref, kc_ref, vp_ref, vc_ref, s_ref, bias_ref):
    cols = pl.ds(kv * SWA_DIM, SWA_DIM)
    heads = [kv * SWA_GROUP + g for g in range(SWA_GROUP)]
    qs = [q_ref[:, pl.ds(h * SWA_DIM, SWA_DIM)] for h in heads]
    kband = jnp.concatenate([kp_ref[:, cols], kc_ref[:, cols]], axis=0)
    vband = jnp.concatenate([vp_ref[:, cols], vc_ref[:, cols]], axis=0)
    sinks = [s_ref[:, pl.ds(h, 1)] for h in heads]
    biases = [bias_ref[h] for h in heads]
    return heads, qs, kband, vband, sinks, biases


def _swa_fwd(proj, q_gain, k_gain, sinks, bias):
    S = proj.shape[0]

    def body(q_ref, kp_ref, kc_ref, vp_ref, vc_ref, qg_ref, kg_ref, s_ref, bias_ref, y_ref):
        mask = _swa_mask(pl.program_id(0))
        for kv in range(SWA_KV):
            heads, qs, kband, vband, sk, bs = _swa_load(kv, q_ref, kp_ref, kc_ref, vp_ref, vc_ref, s_ref, bias_ref)
            outs = _swa_heads(qs, kband, vband, qg_ref[...], kg_ref[...], sk, bs, mask)
            for h, o in zip(heads, outs):
                y_ref[:, pl.ds(h * SWA_DIM, SWA_DIM)] = o.astype(BF16)

    return pl.pallas_call(
        body, grid=(S // BLOCK,), in_specs=_swa_in_specs(),
        out_specs=pl.BlockSpec((BLOCK, SWA_WIDTH), lambda n: (n, 0)),
        out_shape=jax.ShapeDtypeStruct((S, SWA_WIDTH), BF16), name="swa_fwd",
        compiler_params=_params(("parallel",)))(proj, proj, proj, proj, proj, q_gain, k_gain, sinks, bias)


def _swa_bwd(proj, q_gain, k_gain, sinks, bias, dy):
    S = proj.shape[0]

    def body(q_ref, kp_ref, kc_ref, vp_ref, vc_ref, qg_ref, kg_ref, s_ref, bias_ref, dy_ref,
             dq_ref, dk_ref, dv_ref, dqg_ref, dkg_ref, ds_ref, dbias_ref):
        n = pl.program_id(0)
        mask = _swa_mask(n)

        @pl.when(n == 0)
        def _():
            for r in (dk_ref, dv_ref, dqg_ref, dkg_ref, ds_ref, dbias_ref):
                r[...] = jnp.zeros_like(r)

        cur = pl.ds(pl.multiple_of(n * BLOCK, BLOCK), BLOCK)
        prev = pl.ds(pl.multiple_of(jnp.maximum(n - 1, 0) * BLOCK, BLOCK), BLOCK)
        for kv in range(SWA_KV):
            heads, qs, kband, vband, sk, bs = _swa_load(kv, q_ref, kp_ref, kc_ref, vp_ref, vc_ref, s_ref, bias_ref)
            _, vjp = jax.vjp(lambda qs, kb, vb, qg, kg, sk, bs: _swa_heads(qs, kb, vb, qg, kg, sk, bs, mask),
                             qs, kband, vband, qg_ref[...], kg_ref[...], sk, bs)
            dqs, dkb, dvb, dqg, dkg, dsk, dbs = vjp([dy_ref[:, pl.ds(h * SWA_DIM, SWA_DIM)] for h in heads])
            cols = pl.ds(kv * SWA_DIM, SWA_DIM)
            for h, dq, dsink, db in zip(heads, dqs, dsk, dbs):
                dq_ref[:, pl.ds(h * SWA_DIM, SWA_DIM)] = dq.astype(BF16)
                ds_ref[:, pl.ds(h, 1)] += dsink
                dbias_ref[h] += db
            dqg_ref[...] += dqg
            dkg_ref[...] += dkg
            dk_ref[cur, cols] += dkb[BLOCK:]
            dv_ref[cur, cols] += dvb[BLOCK:]

            @pl.when(n > 0)
            def _():
                dk_ref[prev, cols] += dkb[:BLOCK]
                dv_ref[prev, cols] += dvb[:BLOCK]

    return pl.pallas_call(
        body, grid=(S // BLOCK,),
        in_specs=_swa_in_specs() + [pl.BlockSpec((BLOCK, SWA_WIDTH), lambda n: (n, 0))],
        out_specs=[pl.BlockSpec((BLOCK, SWA_WIDTH), lambda n: (n, 0)), _full((S, SWA_KVW)), _full((S, SWA_KVW)),
                   _full((1, SWA_DIM)), _full((1, SWA_DIM)), _full((1, SWA_HEADS)),
                   _full((SWA_HEADS, BLOCK, 2 * BLOCK))],
        out_shape=[jax.ShapeDtypeStruct((S, SWA_WIDTH), BF16), jax.ShapeDtypeStruct((S, SWA_KVW), F32),
                   jax.ShapeDtypeStruct((S, SWA_KVW), F32), jax.ShapeDtypeStruct((1, SWA_DIM), F32),
                   jax.ShapeDtypeStruct((1, SWA_DIM), F32), jax.ShapeDtypeStruct((1, SWA_HEADS), F32),
                   jax.ShapeDtypeStruct((SWA_HEADS, BLOCK, 2 * BLOCK), F32)],
        name="swa_bwd", compiler_params=_params(("arbitrary",)),
    )(proj, proj, proj, proj, proj, q_gain, k_gain, sinks, bias, dy)


def _position():
    return lax.axis_index("x"), lax.axis_index("y"), lax.axis_index("c")


def _all_gather(shards):
    na = len(shards)

    def body(*refs):
        x_refs, out_refs = refs[:na], refs[na:2 * na]
        send_sems, recv_sems, local_sems = refs[2 * na:]
        x, y, c = _position()
        me, sibling = (x, y, c), (x, y, 1 - c)
        chips = [(1 - x, y), (x, 1 - y), (1 - x, 1 - y)]

        def copy(a, k, block, to, own=False):
            px, py, pc = block
            slot = out_refs[a].at[4 * px + 2 * py + pc]
            return pltpu.make_async_remote_copy(
                src_ref=x_refs[a] if own else slot, dst_ref=slot, send_sem=send_sems.at[7 * a + k],
                recv_sem=recv_sems.at[7 * a + k], device_id=to, device_id_type=MESH_ID)

        mine = [pltpu.make_async_copy(x_refs[a], out_refs[a].at[4 * x + 2 * y + c], local_sems.at[a])
                for a in range(na)]
        for cp in mine:
            cp.start()
        first = []
        for a in range(na):
            first.append(copy(a, 0, me, sibling, own=True))
            first += [copy(a, 1 + j, me, (*chip, c), own=True) for j, chip in enumerate(chips)]
        for cp in first:
            cp.start()
        passed = []
        for j, chip in enumerate(chips):
            for a in range(na):
                copy(a, 1 + j, (*chip, c), me).wait_recv()
                passed.append(copy(a, 4 + j, (*chip, c), sibling))
                passed[-1].start()
        for a in range(na):
            copy(a, 0, sibling, me).wait_recv()
            for j, chip in enumerate(chips):
                copy(a, 4 + j, (*chip, 1 - c), me).wait_recv()
        for cp in first + passed:
            cp.wait_send()
        for cp in mine:
            cp.wait()

    return pl.pallas_call(
        body, in_specs=[pl.BlockSpec(memory_space=pl.ANY)] * na, out_specs=[pl.BlockSpec(memory_space=pl.ANY)] * na,
        out_shape=[jax.ShapeDtypeStruct((N_DEV,) + s.shape, s.dtype) for s in shards],
        scratch_shapes=[pltpu.SemaphoreType.DMA((7 * na,)), pltpu.SemaphoreType.DMA((7 * na,)),
                        pltpu.SemaphoreType.DMA((na,))],
        name="all_gather_weights")(*shards)


def _exchange(blocks):
    na = len(blocks)

    def body(*refs):
        in_refs, out_refs = refs[:na], refs[na:2 * na]
        send_sems, recv_sems, local_sems = refs[2 * na:]
        x, y, c = _position()
        me = 4 * x + 2 * y + c
        local = [pltpu.make_async_copy(in_refs[a].at[me], out_refs[a].at[me], local_sems.at[a]) for a in range(na)]
        for cp in local:
            cp.start()
        sends, recvs = [], []
        for k in range(1, N_DEV):
            px, py, pc = x ^ (k >> 2), y ^ ((k >> 1) & 1), c ^ (k & 1)
            peer = 4 * px + 2 * py + pc
            for a in range(na):
                sems = dict(send_sem=send_sems.at[na * (k - 1) + a], recv_sem=recv_sems.at[na * (k - 1) + a],
                            device_id=(px, py, pc), device_id_type=MESH_ID)
                sends.append(pltpu.make_async_remote_copy(src_ref=in_refs[a].at[peer], dst_ref=out_refs[a].at[me], **sems))
                recvs.append(pltpu.make_async_remote_copy(src_ref=in_refs[a].at[me], dst_ref=out_refs[a].at[peer], **sems))
        for cp in sends:
            cp.start()
        for cp in recvs:
            cp.wait_recv()
        for cp in sends:
            cp.wait_send()
        for cp in local:
            cp.wait()

    return pl.pallas_call(
        body, in_specs=[pl.BlockSpec(memory_space=pl.ANY)] * na, out_specs=[pl.BlockSpec(memory_space=pl.ANY)] * na,
        out_shape=[jax.ShapeDtypeStruct(b.shape, b.dtype) for b in blocks],
        scratch_shapes=[pltpu.SemaphoreType.DMA((7 * na,)), pltpu.SemaphoreType.DMA((7 * na,)),
                        pltpu.SemaphoreType.DMA((na,))],
        name="exchange_grads")(*blocks)


def _adam_update(parts, w, m, v, name, tr=256):
    r, c = w.shape
    tr = _pick_rows(r, tr)
    cp = parts.shape[2]

    def body(p_ref, w_ref, m_ref, v_ref, g_ref, d_ref, nm_ref, nv_ref):
        g = p_ref[0, :, pl.ds(0, c)].astype(F32)
        for i in range(1, N_DEV):
            g = g + p_ref[i, :, pl.ds(0, c)].astype(F32)
        delta, nm, nv = _adamw(w_ref[...], g, m_ref[...], v_ref[...])
        g_ref[...] = g
        d_ref[...] = delta
        nm_ref[...] = nm
        nv_ref[...] = nv

    rs = pl.BlockSpec((tr, c), lambda i: (i, 0))
    return pl.pallas_call(
        body, grid=(r // tr,), in_specs=[pl.BlockSpec((N_DEV, tr, cp), lambda i: (0, i, 0)), rs, rs, rs],
        out_specs=[rs] * 4, out_shape=[jax.ShapeDtypeStruct((r, c), F32)] * 4, name=name,
        compiler_params=_params(("parallel",)))(parts, w, m, v)


def _pick_rows(rows, target):
    if rows <= target:
        return rows
    t = target
    while t >= 16:
        if rows % t == 0:
            return t
        t -= 16
    return rows


BIG = ("w_in", "w_branch_dn", "w_branch_swa", "w_out", "w_gate", "w_up", "w_down")
IN_SHARD, IN_WIRE = D_IN // N_DEV, 640
FF_SHARD, FF_WIRE = D_FF // N_DEV, 384
D_FFP = N_DEV * FF_WIRE
BIG_SHAPES = {"w_in": ((D_MODEL, IN_SHARD), (D_MODEL, IN_WIRE)),
              "w_branch_dn": ((DN_WIDTH, LANES), (DN_WIDTH, LANES)),
              "w_branch_swa": ((SWA_WIDTH, LANES), (SWA_WIDTH, LANES)),
              "w_out": ((LANES, D_MODEL), (LANES, D_MODEL)),
              "w_gate": ((D_MODEL, FF_SHARD), (D_MODEL, FF_WIRE)),
              "w_up": ((D_MODEL, FF_SHARD), (D_MODEL, FF_WIRE)),
              "w_down": ((FF_SHARD, D_MODEL), (FF_WIRE, D_MODEL))}
CONV_SHARD, CONV_WIRE = (DN_CONV, DN_QKV // N_DEV), (8, 256)


def _pad_to(a, shape):
    return jnp.pad(a, [(0, t - s) for s, t in zip(a.shape, shape)])


_IN_SEGS = ((R_GATE, 2048, P_GATE), (R_QKV, DN_QKV, P_QKV), (R_Z, DN_WIDTH, P_Z), (R_SQ, SWA_WIDTH, P_SQ),
            (R_SK, SWA_KVW, P_SK), (R_SV, SWA_KVW, P_SV), (R_B, 8, P_BA))


def _w_in_from_blocks(blocks):
    parts = []
    for rs, n, _ in _IN_SEGS:
        for dev in range(N_DEV):
            lo, hi = max(rs, IN_SHARD * dev), min(rs + n, IN_SHARD * (dev + 1))
            if lo < hi:
                parts.append(blocks[dev, :, lo - IN_SHARD * dev:hi - IN_SHARD * dev])
    parts.append(jnp.zeros((blocks.shape[1], P_WIDTH - P_BA - 8), blocks.dtype))
    return jnp.concatenate(parts, axis=1)


def _w_in_to_blocks(g):
    out = []
    for dev in range(N_DEV):
        parts = []
        for rs, n, ps in sorted(_IN_SEGS):
            lo, hi = max(rs, IN_SHARD * dev), min(rs + n, IN_SHARD * (dev + 1))
            if lo < hi:
                parts.append(g[:, ps + lo - rs:ps + hi - rs])
        parts.append(jnp.zeros((g.shape[0], IN_WIRE - IN_SHARD), g.dtype))
        out.append(jnp.concatenate(parts, axis=1))
    return jnp.stack(out)


def _pack(flat_parts, rows):
    flat = jnp.concatenate(flat_parts, axis=-1)
    pad = rows * LANES - flat.shape[-1]
    flat = jnp.pad(flat, [(0, 0)] * (flat.ndim - 1) + [(0, pad)])
    return flat.reshape(flat.shape[:-1] + (rows, LANES))


def _unpack(buf, shapes):
    lead = buf.shape[:-2]
    flat = buf.reshape(lead + (-1,))
    out, off = [], 0
    for shp in shapes:
        n = int(np.prod(shp))
        out.append(flat[..., off:off + n].reshape(lead + tuple(shp)))
        off += n
    return out


def _cols_join(blocks):
    return jnp.concatenate([blocks[d] for d in range(N_DEV)], axis=1)


def _cols_split(full):
    c = full.shape[1] // N_DEV
    return jnp.stack([full[:, d * c:(d + 1) * c] for d in range(N_DEV)])


SMALL = ("attn_norm", "ffn_norm", "rel_bias", "dn_out_norm", "swa_q_norm", "swa_k_norm", "dn_a_log",
         "dn_dt_bias", "swa_sinks")
SMALL_ROWS = 24
CONV_ROWS = DN_CONV * DN_QKV // LANES


def _pack_small(d):
    return _pack([d[n].reshape(-1) for n in SMALL], SMALL_ROWS)


def kernel(x, attn_norm, w_in, dn_conv, dn_a_log, dn_dt_bias, dn_out_norm, swa_q_norm, swa_k_norm, swa_sinks, rel_bias, w_branch_dn, w_branch_swa, w_out, ffn_norm, w_gate, w_up, w_down, loss_target, m_attn_norm, m_w_in, m_dn_conv, m_dn_a_log, m_dn_dt_bias, m_dn_out_norm, m_swa_q_norm, m_swa_k_norm, m_swa_sinks, m_rel_bias, m_w_branch_dn, m_w_branch_swa, m_w_out, m_ffn_norm, m_w_gate, m_w_up, m_w_down, v_attn_norm, v_w_in, v_dn_conv, v_dn_a_log, v_dn_dt_bias, v_dn_out_norm, v_swa_q_norm, v_swa_k_norm, v_swa_sinks, v_rel_bias, v_w_branch_dn, v_w_branch_swa, v_w_out, v_ffn_norm, v_w_gate, v_w_up, v_w_down):
    args = dict(locals())
    S = x.shape[1]
    xs = x.reshape(S, D_MODEL)
    target = loss_target.reshape(S, D_MODEL)

    w_loc = {n: args[n].reshape(BIG_SHAPES[n][0]) for n in BIG}
    conv_loc = dn_conv.reshape(CONV_SHARD)
    gathered = _all_gather([_pad_to(w_loc[n], BIG_SHAPES[n][1]).astype(BF16) for n in BIG]
                           + [_pad_to(conv_loc, CONV_WIRE)])
    G = dict(zip(BIG, gathered))
    w_pad = _w_in_from_blocks(G["w_in"])
    conv_w = jnp.concatenate([gathered[-1][d, :DN_CONV, :CONV_SHARD[1]] for d in range(N_DEV)], axis=1)
    w_bdn = _cols_join(G["w_branch_dn"])
    w_bswa = _cols_join(G["w_branch_swa"])
    w_o = G["w_out"].reshape(D_MODEL, D_MODEL)
    w_d = G["w_down"].reshape(D_FFP, D_MODEL)

    h = _norm_fwd(xs, attn_norm, "norm1_fwd")
    proj = _mm(h, w_pad, "nn", F32, "mm_in", tn=1664)
    qkvn = _dn_conv_fwd(proj, conv_w)
    beta, g = _dn_gate_fwd(proj, dn_a_log, dn_dt_bias)
    u, w, qe, kd, qk, egl = _dn_prep_fwd(qkvn, g, beta)
    o, states = _dn_scan_fwd(u, w, qe, kd, qk, egl)
    y_dn = _dn_out_fwd(o, proj, dn_out_norm)
    bias = _bias_fwd(rel_bias)
    y_swa = _swa_fwd(proj, swa_q_norm, swa_k_norm, swa_sinks, bias)
    a_dn = _mm(y_dn, w_bdn, "nn", F32, "mm_branch_dn")
    a_swa = _mm(y_swa, w_bswa, "nn", F32, "mm_branch_swa")
    merged = _merge_fwd(proj, a_dn, a_swa)
    t_out = _mm(merged, w_o, "nn", F32, "mm_out")
    x1, h2 = _resid_norm_fwd(xs, t_out, ffn_norm, "norm2_fwd")
    gate = _mm(h2, G["w_gate"], "nn", F32, "mm_gate", b_blocks=True)
    up = _mm(h2, G["w_up"], "nn", F32, "mm_up", b_blocks=True)
    act = _act_fwd(gate, up)
    f = _mm(act, w_d, "nn", F32, "mm_down")
    dy, loss_local = _loss_fwd_bwd(x1, f, target)

    dact = _mm(dy, w_d, "nt", F32, "mm_dact", tn=1536)
    g_w_down = _mm(act, dy, "tn", BF16, "mm_dw_down", tm=1536).reshape(N_DEV, FF_WIRE, D_MODEL)
    dgate, dup = _act_bwd(gate, up, dact)
    dh2_g = _mm(dgate, G["w_gate"], "nt", F32, "mm_dh2_gate", b_blocks=True)
    dh2_u = _mm(dup, G["w_up"], "nt", F32, "mm_dh2_up", b_blocks=True)
    g_w_gate = _mm(h2, dgate, "tn", BF16, "mm_dw_gate", out_blocks=True)
    g_w_up = _mm(h2, dup, "tn", BF16, "mm_dw_up", out_blocks=True)
    dx1, g_ffn_norm = _norm_bwd(x1, [dh2_g, dh2_u], dy, ffn_norm, "norm2_bwd")
    dmerged = _mm(dx1, w_o, "nt", F32, "mm_dmerged")
    g_w_out = _mm(merged, dx1, "tn", BF16, "mm_dw_out").reshape(N_DEV, LANES, D_MODEL)
    dg0, dg1, da_dn, da_swa = _merge_bwd(proj, a_dn, a_swa, dmerged)
    dy_dn = _mm(da_dn, w_bdn, "nt", F32, "mm_dy_dn")
    dy_swa = _mm(da_swa, w_bswa, "nt", F32, "mm_dy_swa")
    g_w_bdn = _cols_split(_mm(y_dn, da_dn, "tn", BF16, "mm_dw_branch_dn"))
    g_w_bswa = _cols_split(_mm(y_swa, da_swa, "tn", BF16, "mm_dw_branch_swa"))
    dsq, dsk, dsv, g_q_norm, g_k_norm, g_sinks, dbias = _swa_bwd(proj, swa_q_norm, swa_k_norm, swa_sinks, bias, dy_swa)
    g_rel_bias = _bias_bwd(dbias)[:, :REL_BUCKETS].T
    do, dz, g_out_norm = _dn_out_bwd(o, proj, dn_out_norm, dy_dn)
    du, dw, dqe, dkd, dqk, degl = _dn_scan_bwd(u, w, qe, kd, qk, egl, states, do)
    dq, dk, dv, dgd, dbeta = _dn_prep_bwd(qkvn, g, beta, du, dw, dqe, dkd, dqk, degl)
    dqkvn = jnp.concatenate([dq, dk, dv], axis=1)
    dba, dal, ddt = _dn_gate_bwd(proj, dn_a_log, dn_dt_bias, dbeta, dgd)
    g_a_log = dal.reshape(DN_HEADS, DN_DIM).sum(axis=1)
    g_dt_bias = ddt.reshape(DN_HEADS, DN_DIM).sum(axis=1)
    dqkv, g_conv = _dn_conv_bwd(proj, conv_w, dqkvn)
    dproj = jnp.concatenate([dg0, dg1, dqkv, dz, dsq, dsk.astype(BF16), dsv.astype(BF16), dba], axis=1)
    dh = _mm(dproj, w_pad, "nt", F32, "mm_dh", tk=1664)
    g_w_in = _w_in_to_blocks(_mm(h, dproj, "tn", BF16, "mm_dw_in", tn=1664))
    dx, g_attn_norm = _norm_bwd(xs, [dh], dx1, attn_norm, "norm1_bwd")

    g_send = {"w_in": g_w_in, "w_branch_dn": g_w_bdn, "w_branch_swa": g_w_bswa, "w_out": g_w_out,
              "w_gate": g_w_gate, "w_up": g_w_up, "w_down": g_w_down}
    g_small = {"attn_norm": g_attn_norm, "ffn_norm": g_ffn_norm, "rel_bias": g_rel_bias, "dn_out_norm": g_out_norm,
               "swa_q_norm": g_q_norm, "swa_k_norm": g_k_norm, "dn_a_log": g_a_log, "dn_dt_bias": g_dt_bias,
               "swa_sinks": g_sinks}
    small_rows = jnp.concatenate([_pack_small(g_small), g_conv.reshape(CONV_ROWS, LANES)], axis=0)
    send_small = jnp.broadcast_to(small_rows[None], (N_DEV,) + small_rows.shape)
    received = _exchange([g_send[n] for n in BIG] + [send_small])
    recv_small = received[-1]

    outs = {}
    for n, parts in zip(BIG, received):
        shp = BIG_SHAPES[n][0]
        outs[n] = _adam_update(parts, w_loc[n], args["m_" + n].reshape(shp), args["v_" + n].reshape(shp), "adam_" + n)
    me = 4 * lax.axis_index("x") + 2 * lax.axis_index("y") + lax.axis_index("c")
    conv_parts = lax.dynamic_slice_in_dim(recv_small[:, SMALL_ROWS:].reshape(N_DEV, DN_CONV, DN_QKV),
                                          me * CONV_SHARD[1], CONV_SHARD[1], axis=2)
    outs["dn_conv"] = _adam_update(conv_parts, conv_loc, m_dn_conv.reshape(CONV_SHARD), v_dn_conv.reshape(CONV_SHARD),
                                   "adam_dn_conv")
    small_out = _adam_update(recv_small[:, :SMALL_ROWS], _pack_small({n: args[n] for n in SMALL}),
                             _pack_small({n: args["m_" + n] for n in SMALL}),
                             _pack_small({n: args["v_" + n] for n in SMALL}), "adam_small")

    names = ("attn_norm", "w_in", "dn_conv", "dn_a_log", "dn_dt_bias", "dn_out_norm", "swa_q_norm", "swa_k_norm",
             "swa_sinks", "rel_bias", "w_branch_dn", "w_branch_swa", "w_out", "ffn_norm", "w_gate", "w_up", "w_down")
    results = []
    for kind in range(4):
        small = dict(zip(SMALL, _unpack(small_out[kind], [args[n].shape for n in SMALL])))
        results += [outs[n][kind].reshape(args[n].shape) if n in outs else small[n] for n in names]

    loss = lax.psum(loss_local[0, 0], ("x", "y", "c"))
    return (loss, dx.reshape(x.shape), *results)
```

```python
import math

import numpy as np
import jax
import jax.numpy as jnp
from jax import lax
from jax.experimental import pallas as pl
from jax.experimental.pallas import tpu as pltpu

F32 = jnp.float32
BF16 = jnp.bfloat16
HI = lax.Precision.HIGHEST

D_MODEL = 1024
DN_HEADS = 4
DN_DIM = 128
DN_WIDTH = 512
DN_QKV = 1536
DN_CONV = 4
CHUNK = 64
SWA_HEADS = 8
SWA_KV = 2
SWA_GROUP = 4
SWA_DIM = 64
SWA_WIDTH = 512
SWA_KVW = 128
WINDOW = 128
BLOCK = 128
REL_BUCKETS = 32
REL_MAX_DIST = 128
D_FF = 2816
D_IN = 4872
EPS = 1e-6
N_DEV = 8

ADAM_LR = 0.001
ADAM_B1 = 0.9
ADAM_B2 = 0.999
ADAM_EPS = 1e-08
ADAM_WD = 0.01
ADAM_STEP = 10

P_GATE, P_QKV, P_Z, P_SQ, P_SK, P_SV, P_BA = 0, 2048, 3584, 4096, 4608, 4736, 4864
P_WIDTH = 4992
R_QKV, R_Z, R_B, R_A, R_SQ, R_SK, R_SV, R_GATE = 0, 1536, 2048, 2052, 2056, 2568, 2696, 2824

VMEM_LIMIT = 56 * 1024 * 1024
LANES = 128
MESH_ID = pl.DeviceIdType.MESH


def _params(sem=None):
    return pltpu.CompilerParams(dimension_semantics=sem, vmem_limit_bytes=VMEM_LIMIT)


def _pick(dim, target):
    if dim <= target:
        return dim
    t = target - target % LANES
    while t >= LANES:
        if dim % t == 0:
            return t
        t -= LANES
    return dim


_DIMS = {"nn": (((1,), (0,)), ((), ())), "nt": (((1,), (1,)), ((), ())), "tn": (((0,), (0,)), ((), ()))}


def _mm(pairs, mode, out_dtype, name, bm, bn, j_outer=False):
    a0, b0 = pairs[0]
    if mode == "nn":
        (M, K), (K2, N) = a0.shape, b0.shape
    elif mode == "nt":
        (M, K), (N, K2) = a0.shape, b0.shape
    else:
        (K, M), (K2, N) = a0.shape, b0.shape
    bm, bn = min(bm, M), min(bn, N)
    assert K == K2 and M % bm == 0 and N % bn == 0, (name, a0.shape, b0.shape, bm, bn)
    dims = _DIMS[mode]
    n = len(pairs)

    def body(*refs):
        o_ref = refs[2 * n]
        acc = None
        for t in range(n):
            p = lax.dot_general(refs[2 * t][...].astype(BF16), refs[2 * t + 1][...].astype(BF16), dims,
                                preferred_element_type=F32)
            acc = p if acc is None else acc + p
        o_ref[...] = acc.astype(out_dtype)

    def ij(f):
        return (lambda j, i: f(i, j)) if j_outer else f

    a_spec = pl.BlockSpec((K, bm), ij(lambda i, j: (0, i))) if mode == "tn" else pl.BlockSpec((bm, K), ij(lambda i, j: (i, 0)))
    b_spec = pl.BlockSpec((bn, K), ij(lambda i, j: (j, 0))) if mode == "nt" else pl.BlockSpec((K, bn), ij(lambda i, j: (0, j)))
    grid = (N // bn, M // bm) if j_outer else (M // bm, N // bn)
    return pl.pallas_call(
        body, grid=grid, in_specs=[a_spec, b_spec] * n, out_specs=pl.BlockSpec((bm, bn), ij(lambda i, j: (i, j))),
        out_shape=jax.ShapeDtypeStruct((M, N), out_dtype), name=name,
        compiler_params=_params(("parallel", "parallel")),
    )(*[x for pair in pairs for x in pair])


def _rms(x, gain):
    return x * lax.rsqrt(jnp.mean(x * x, axis=-1, keepdims=True) + EPS) * gain


def _silu(x):
    return x * jax.nn.sigmoid(x)


def _act(g, u):
    return _silu(g) * u


def _merge(g0, g1, a_dn, a_swa):
    return jax.nn.sigmoid(g0) * a_dn + jax.nn.sigmoid(g1) * a_swa


def _dn_post(c, is_v, q_scale):
    a = _silu(c)
    rs = lax.rsqrt(jnp.sum(a * a, axis=-1, keepdims=True) + EPS) * q_scale
    return a * jnp.where(is_v, 1.0, rs)


def _dn_out(o, z, gain):
    return _rms(o, gain) * _silu(z)


def _dot(a, b, dims=_DIMS["nn"], hi=False):
    if hi:
        return lax.dot_general(a, b, dims, precision=HI, preferred_element_type=F32)
    return lax.dot_general(a.astype(BF16), b.astype(BF16), dims, preferred_element_type=F32)


def _pieces(x):
    hi = x.astype(BF16)
    r1 = x - hi.astype(F32)
    mid = r1.astype(BF16)
    return hi, mid, (r1 - mid.astype(F32)).astype(BF16)


def _sel_left_impl(m, x):
    mb = m.astype(BF16)
    hi, mid, lo = _pieces(x)
    return _dot(mb, hi) + (_dot(mb, mid) + _dot(mb, lo))


@jax.custom_vjp
def _sel_left(m, mt, x):
    return _sel_left_impl(m, x)


_sel_left.defvjp(lambda m, mt, x: (_sel_left_impl(m, x), (m, mt)),
                 lambda res, ct: (jnp.zeros_like(res[0]), jnp.zeros_like(res[1]), _sel_left_impl(res[1], ct)))


def _sel_right_impl(x, s):
    sb = s.astype(BF16)
    hi, mid, lo = _pieces(x)
    return _dot(hi, sb) + (_dot(mid, sb) + _dot(lo, sb))


@jax.custom_vjp
def _sel_right(x, s, st):
    return _sel_right_impl(x, s)


_sel_right.defvjp(lambda x, s, st: (_sel_right_impl(x, s), (s, st)),
                  lambda res, ct: (_sel_right_impl(ct, res[1]), jnp.zeros_like(res[0]), jnp.zeros_like(res[1])))


def _dot3_impl(a, b):
    a_hi, a_lo, _ = _pieces(a)
    b_hi, b_lo, _ = _pieces(b)
    return _dot(a_hi, b_hi) + (_dot(a_hi, b_lo) + _dot(a_lo, b_hi))


@jax.custom_vjp
def _dot3(a, b):
    return _dot3_impl(a, b)


_dot3.defvjp(lambda a, b: (_dot3_impl(a, b), (a, b)),
             lambda res, ct: (_dot(ct, res[1], _DIMS["nt"]), _dot(res[0], ct, _DIMS["tn"])))


def _inv_impl(a, eye, strict):
    t = eye - a
    p = _dot(a, a)
    for level in range(5):
        t = t + _dot(t, p)
        if level < 4:
            p = _dot(p, p)
    t = t + _dot3_impl(t, eye - _dot3_impl(eye + a, t))
    return jnp.where(strict > 0.5, t, eye)


@jax.custom_vjp
def _inv_unit_lower(a, eye, strict):
    return _inv_impl(a, eye, strict)


def _inv_bwd(res, ct):
    t, eye, strict = res
    da = -_dot(_dot(t, ct, _DIMS["tn"]), t, _DIMS["nt"])
    return da, jnp.zeros_like(eye), jnp.zeros_like(strict)


def _inv_fwd(a, eye, strict):
    t = _inv_impl(a, eye, strict)
    return t, (t, eye, strict)


_inv_unit_lower.defvjp(_inv_fwd, _inv_bwd)

GROUP = 4
GROUP_ROWS = GROUP * CHUNK


def _block_consts(n):
    ii = lax.broadcasted_iota(jnp.int32, (n, n), 0)
    jj = lax.broadcasted_iota(jnp.int32, (n, n), 1)
    shift = CHUNK.bit_length() - 1
    same = jnp.right_shift(ii, shift) == jnp.right_shift(jj, shift)
    return same & (ii >= jj), same & (ii <= jj), same & (ii > jj), same, ii == jj


def _lane0(n):
    s = (lax.broadcasted_iota(jnp.int32, (LANES, n), 0) == 0).astype(F32)
    st = (lax.broadcasted_iota(jnp.int32, (n, LANES), 1) == 0).astype(F32)
    return s, st


def _dn_group(q, k, v, g, beta):
    n = GROUP_ROWS
    low_b, upp_b, strict_b, same_b, eye_b = _block_consts(n)
    low, upp, same, eye = low_b.astype(F32), upp_b.astype(F32), same_b.astype(F32), eye_b.astype(F32)
    s, st = _lane0(n)
    gc = _sel_left(low, upp, g)
    gl = _sel_left(same, same, g)
    col = _sel_right(gc, s, st)
    decay = jnp.exp(jnp.where(low_b, col - col.T, -jnp.inf))
    kb = k * beta
    vb = v * beta
    a = jnp.where(strict_b, _dot(kb, k, _DIMS["nt"]) * decay, 0.0)
    t = _inv_unit_lower(a, eye, strict_b.astype(F32))
    u = _dot3(t, vb)
    w = _dot3(t, kb * jnp.exp(gc))
    return u, w, q * jnp.exp(gc), k * jnp.exp(gl - gc)


def _dn_chunk(q, k, g):
    ii = lax.broadcasted_iota(jnp.int32, (CHUNK, CHUNK), 0)
    jj = lax.broadcasted_iota(jnp.int32, (CHUNK, CHUNK), 1)
    low = (ii >= jj).astype(F32)
    upp = (ii <= jj).astype(F32)
    ones = jnp.ones((CHUNK, CHUNK), F32)
    eye = (ii == jj).astype(F32)
    s, st = _lane0(CHUNK)
    gc = _sel_left(low, upp, g)
    col = _sel_right(gc, s, st)
    row = _sel_left(ones, ones, col * eye)
    decay = jnp.exp(jnp.where(ii >= jj, col - row, -jnp.inf))
    qk = _dot(q, k, _DIMS["nt"]) * decay
    return qk, jnp.exp(jnp.sum(g, axis=0, keepdims=True))


def _dn_step(s, u, w, qe, kd, qk, egl):
    v_new = u - _dot(w, s)
    o = _dot(qe, s) + _dot(qk, v_new)
    s_new = s * egl + _dot(kd, v_new, _DIMS["tn"])
    return s_new, o


def _swa_heads(qs, kband, vband, qg, kg, sinks, biases, mask):
    kn = _rms(kband, kg)
    outs = []
    for q, sink, bias in zip(qs, sinks, biases):
        qn = _rms(q, qg)
        logits = _dot(qn, kn, _DIMS["nt"]) * (SWA_DIM ** -0.5)
        logits = jnp.where(mask, logits + bias, -jnp.inf)
        m = jnp.maximum(jnp.max(logits, axis=-1, keepdims=True), sink)
        p = jnp.exp(logits - m)
        denom = jnp.sum(p, axis=-1, keepdims=True) + jnp.exp(sink - m)
        outs.append(_dot(p / denom, vband))
    return outs


def _adamw(w, g, m, v):
    m = ADAM_B1 * m + (1.0 - ADAM_B1) * g
    v = ADAM_B2 * v + (1.0 - ADAM_B2) * jnp.square(g)
    m_hat = m / (1.0 - ADAM_B1 ** ADAM_STEP)
    v_hat = v / (1.0 - ADAM_B2 ** ADAM_STEP)
    delta = -ADAM_LR * (m_hat / (jnp.sqrt(v_hat) + ADAM_EPS) + ADAM_WD * w)
    return delta, m, v


def _row(tm, c, cb=0):
    return pl.BlockSpec((tm, c), lambda i, cb=cb: (i, cb))


def _full(shape):
    nd = len(shape)
    return pl.BlockSpec(shape, lambda *_, nd=nd: (0,) * nd)


def _norm_fwd(x, gain, name, tm=512):
    S = x.shape[0]

    def body(x_ref, g_ref, h_ref):
        h_ref[...] = _rms(x_ref[...], g_ref[...]).astype(BF16)

    return pl.pallas_call(
        body, grid=(S // tm,), in_specs=[_row(tm, D_MODEL), _full((1, D_MODEL))],
        out_specs=_row(tm, D_MODEL), out_shape=jax.ShapeDtypeStruct((S, D_MODEL), BF16),
        name=name, compiler_params=_params(("parallel",)))(x, gain)


def _resid_norm_fwd(x, t, gain, name, tm=512):
    S = x.shape[0]

    def body(x_ref, t_ref, g_ref, x1_ref, h_ref):
        x1 = x_ref[...] + t_ref[...]
        x1_ref[...] = x1
        h_ref[...] = _rms(x1, g_ref[...]).astype(BF16)

    return pl.pallas_call(
        body, grid=(S // tm,), in_specs=[_row(tm, D_MODEL), _row(tm, D_MODEL), _full((1, D_MODEL))],
        out_specs=[_row(tm, D_MODEL), _row(tm, D_MODEL)],
        out_shape=[jax.ShapeDtypeStruct((S, D_MODEL), F32), jax.ShapeDtypeStruct((S, D_MODEL), BF16)],
        name=name, compiler_params=_params(("parallel",)))(x, t, gain)


def _norm_bwd(x, dh_list, dres, gain, name, tm=256):
    S = x.shape[0]
    n = len(dh_list)

    def body(*refs):
        x_ref, g_ref, r_ref = refs[0], refs[1], refs[2]
        dh_refs = refs[3:3 + n]
        dx_ref, dxb_ref, dg_ref = refs[3 + n], refs[4 + n], refs[5 + n]
        dh = dh_refs[0][...].astype(F32)
        for r in dh_refs[1:]:
            dh = dh + r[...].astype(F32)
        _, vjp = jax.vjp(_rms, x_ref[...], g_ref[...])
        dx, dg = vjp(dh)
        dx = dx + r_ref[...]
        dx_ref[...] = dx
        dxb_ref[...] = dx.astype(BF16)

        @pl.when(pl.program_id(0) == 0)
        def _():
            dg_ref[...] = jnp.zeros_like(dg_ref)

        dg_ref[...] += dg

    return pl.pallas_call(
        body, grid=(S // tm,),
        in_specs=[_row(tm, D_MODEL), _full((1, D_MODEL)), _row(tm, D_MODEL)] + [_row(tm, D_MODEL)] * n,
        out_specs=[_row(tm, D_MODEL), _row(tm, D_MODEL), _full((1, D_MODEL))],
        out_shape=[jax.ShapeDtypeStruct((S, D_MODEL), F32), jax.ShapeDtypeStruct((S, D_MODEL), BF16),
                   jax.ShapeDtypeStruct((1, D_MODEL), F32)],
        name=name, compiler_params=_params(("arbitrary",)))(x, gain, dres, *dh_list)


def _loss_fwd_bwd(x1, f, target, tm=512):
    S = x1.shape[0]

    def body(x_ref, f_ref, t_ref, dy_ref, dyb_ref, l_ref):
        diff = x_ref[...] + f_ref[...] - t_ref[...]
        dy = diff * (1.0 / D_MODEL)
        dy_ref[...] = dy
        dyb_ref[...] = dy.astype(BF16)

        @pl.when(pl.program_id(0) == 0)
        def _():
            l_ref[...] = jnp.zeros_like(l_ref)

        l_ref[...] += jnp.sum(jnp.mean(diff * diff, axis=-1, keepdims=True), axis=0, keepdims=True) * 0.5

    return pl.pallas_call(
        body, grid=(S // tm,), in_specs=[_row(tm, D_MODEL)] * 3,
        out_specs=[_row(tm, D_MODEL), _row(tm, D_MODEL), _full((1, 1))],
        out_shape=[jax.ShapeDtypeStruct((S, D_MODEL), F32), jax.ShapeDtypeStruct((S, D_MODEL), BF16),
                   jax.ShapeDtypeStruct((1, 1), F32)],
        name="loss_fwd_bwd", compiler_params=_params(("arbitrary",)))(x1, f, target)


def _act_fwd(g, u, tm=256):
    S, width = g.shape

    def body(g_ref, u_ref, o_ref):
        o_ref[...] = _act(g_ref[...], u_ref[...]).astype(BF16)

    return pl.pallas_call(
        body, grid=(S // tm,), in_specs=[_row(tm, width)] * 2, out_specs=_row(tm, width),
        out_shape=jax.ShapeDtypeStruct((S, width), BF16), name="act_fwd",
        compiler_params=_params(("parallel",)))(g, u)


def _act_bwd(g, u, dact, tm=256):
    S, width = g.shape

    def body(g_ref, u_ref, d_ref, dg_ref, du_ref):
        _, vjp = jax.vjp(_act, g_ref[...], u_ref[...])
        dg, du = vjp(d_ref[...])
        dg_ref[...] = dg.astype(BF16)
        du_ref[...] = du.astype(BF16)

    return pl.pallas_call(
        body, grid=(S // tm,), in_specs=[_row(tm, width)] * 3, out_specs=[_row(tm, width)] * 2,
        out_shape=[jax.ShapeDtypeStruct((S, width), BF16)] * 2, name="act_bwd",
        compiler_params=_params(("parallel",)))(g, u, dact)


def _merge_fwd(proj, a_dn, a_swa, tm=512, tc=512):
    S = proj.shape[0]
    nc = D_MODEL // tc

    def spec(off):
        return pl.BlockSpec((tm, tc), lambda i, j, off=off: (i, off + j))

    def body(g0_ref, g1_ref, ad_ref, as_ref, o_ref):
        o_ref[...] = _merge(g0_ref[...], g1_ref[...], ad_ref[...], as_ref[...]).astype(BF16)

    return pl.pallas_call(
        body, grid=(S // tm, nc), in_specs=[spec(P_GATE // tc), spec(P_GATE // tc + nc), spec(0), spec(0)],
        out_specs=spec(0), out_shape=jax.ShapeDtypeStruct((S, D_MODEL), BF16), name="merge_fwd",
        compiler_params=_params(("parallel", "parallel")))(proj, proj, a_dn, a_swa)


def _merge_bwd(proj, a_dn, a_swa, dmerged, tm=512, tc=512):
    S = proj.shape[0]
    nc = D_MODEL // tc

    def spec(off):
        return pl.BlockSpec((tm, tc), lambda i, j, off=off: (i, off + j))

    def body(g0_ref, g1_ref, ad_ref, as_ref, d_ref, dg0_ref, dg1_ref, dad_ref, das_ref):
        _, vjp = jax.vjp(_merge, g0_ref[...], g1_ref[...], ad_ref[...], as_ref[...])
        dg0, dg1, dad, das = vjp(d_ref[...])
        dg0_ref[...] = dg0.astype(BF16)
        dg1_ref[...] = dg1.astype(BF16)
        dad_ref[...] = dad.astype(BF16)
        das_ref[...] = das.astype(BF16)

    return pl.pallas_call(
        body, grid=(S // tm, nc),
        in_specs=[spec(P_GATE // tc), spec(P_GATE // tc + nc), spec(0), spec(0), spec(0)],
        out_specs=[spec(0)] * 4, out_shape=[jax.ShapeDtypeStruct((S, D_MODEL), BF16)] * 4,
        name="merge_bwd", compiler_params=_params(("parallel", "parallel")))(proj, proj, a_dn, a_swa, dmerged)


def _shift_down(x, s):
    row = lax.broadcasted_iota(jnp.int32, x.shape, 0)
    return jnp.where(row >= s, pltpu.roll(x, s, axis=0), 0.0)


def _shift_up(x, s):
    n = x.shape[0]
    row = lax.broadcasted_iota(jnp.int32, x.shape, 0)
    return jnp.where(row < n - s, pltpu.roll(x, n - s, axis=0), 0.0)


def _conv(x, w):
    out = w[DN_CONV - 1:DN_CONV] * x
    for s in range(1, DN_CONV):
        out = out + w[DN_CONV - 1 - s:DN_CONV - s] * _shift_down(x, s)
    return out


def _dn_conv_fwd(proj, conv_w):
    S = proj.shape[0]
    nb = DN_QKV // LANES

    def body(x_ref, w_ref, o_ref):
        j = pl.program_id(0)
        q_scale = jnp.where(j < DN_HEADS, DN_DIM ** -0.5, 1.0).astype(F32)
        o_ref[...] = _dn_post(_conv(x_ref[...], w_ref[...]), j >= 2 * DN_HEADS, q_scale)

    return pl.pallas_call(
        body, grid=(nb,),
        in_specs=[pl.BlockSpec((S, LANES), lambda j: (0, P_QKV // LANES + j)),
                  pl.BlockSpec((DN_CONV, LANES), lambda j: (0, j))],
        out_specs=pl.BlockSpec((S, LANES), lambda j: (0, j)),
        out_shape=jax.ShapeDtypeStruct((S, DN_QKV), F32), name="dn_conv_fwd",
        compiler_params=_params(("parallel",)))(proj, conv_w)


def _dn_conv_bwd(proj, conv_w, dqkvn):
    S = proj.shape[0]
    nb = DN_QKV // LANES

    def body(x_ref, w_ref, d_ref, dx_ref, dw_ref):
        j = pl.program_id(0)
        q_scale = jnp.where(j < DN_HEADS, DN_DIM ** -0.5, 1.0).astype(F32)
        x = x_ref[...]
        w = w_ref[...]
        _, vjp = jax.vjp(lambda c: _dn_post(c, j >= 2 * DN_HEADS, q_scale), _conv(x, w))
        (dc,) = vjp(d_ref[...])
        dx = w[DN_CONV - 1:DN_CONV] * dc
        dw_ref[DN_CONV - 1:DN_CONV, :] = jnp.sum(dc * x, axis=0, keepdims=True)
        for s in range(1, DN_CONV):
            dx = dx + w[DN_CONV - 1 - s:DN_CONV - s] * _shift_up(dc, s)
            dw_ref[DN_CONV - 1 - s:DN_CONV - s, :] = jnp.sum(dc * _shift_down(x, s), axis=0, keepdims=True)
        dx_ref[...] = dx.astype(BF16)

    return pl.pallas_call(
        body, grid=(nb,),
        in_specs=[pl.BlockSpec((S, LANES), lambda j: (0, P_QKV // LANES + j)),
                  pl.BlockSpec((DN_CONV, LANES), lambda j: (0, j)),
                  pl.BlockSpec((S, LANES), lambda j: (0, j))],
        out_specs=[pl.BlockSpec((S, LANES), lambda j: (0, j)), pl.BlockSpec((DN_CONV, LANES), lambda j: (0, j))],
        out_shape=[jax.ShapeDtypeStruct((S, DN_QKV), BF16), jax.ShapeDtypeStruct((DN_CONV, DN_QKV), F32)],
        name="dn_conv_bwd", compiler_params=_params(("parallel",)))(proj, conv_w, dqkvn)


def _expanders():
    eb = np.zeros((LANES, DN_WIDTH), np.float32)
    ea = np.zeros((LANES, DN_WIDTH), np.float32)
    for h in range(DN_HEADS):
        eb[h, h * DN_DIM:(h + 1) * DN_DIM] = 1.0
        ea[DN_HEADS + h, h * DN_DIM:(h + 1) * DN_DIM] = 1.0
    return jnp.asarray(eb), jnp.asarray(ea)


def _dn_gate_args(a_log, dt_bias):
    eb, ea = _expanders()
    alog = jnp.repeat(a_log.reshape(1, DN_HEADS), DN_DIM, axis=1)
    dtb = jnp.repeat(dt_bias.reshape(1, DN_HEADS), DN_DIM, axis=1)
    return eb, ea, alog, dtb


def _dn_gate_specs(tm):
    return [_row(tm, LANES, P_BA // LANES), _full((LANES, DN_WIDTH)), _full((LANES, DN_WIDTH)),
            _full((1, DN_WIDTH)), _full((1, DN_WIDTH))]


def _dn_gate_fn(ba, eb, ea, alog, dtb):
    beta = jax.nn.sigmoid(_dot(ba, eb, hi=True))
    g = -jnp.exp(alog) * jax.nn.softplus(_dot(ba, ea, hi=True) + dtb)
    return beta, g


def _dn_gate_fwd(proj, a_log, dt_bias, tm=512):
    S = proj.shape[0]
    args = _dn_gate_args(a_log, dt_bias)

    def body(ba_ref, eb_ref, ea_ref, al_ref, dt_ref, beta_ref, g_ref):
        beta, g = _dn_gate_fn(ba_ref[...], eb_ref[...], ea_ref[...], al_ref[...], dt_ref[...])
        beta_ref[...] = beta
        g_ref[...] = g

    return pl.pallas_call(
        body, grid=(S // tm,), in_specs=_dn_gate_specs(tm), out_specs=[_row(tm, DN_WIDTH), _row(tm, DN_WIDTH)],
        out_shape=[jax.ShapeDtypeStruct((S, DN_WIDTH), F32), jax.ShapeDtypeStruct((S, DN_WIDTH), F32)],
        name="dn_gate_fwd", compiler_params=_params(("parallel",)))(proj, *args)


def _dn_gate_bwd(proj, a_log, dt_bias, dbeta, dg, tm=512):
    S = proj.shape[0]
    args = _dn_gate_args(a_log, dt_bias)

    def body(ba_ref, eb_ref, ea_ref, al_ref, dt_ref, dbeta_ref, dg_ref, dba_ref, dal_ref, ddt_ref):
        eb, ea = eb_ref[...], ea_ref[...]
        _, vjp = jax.vjp(lambda ba, al, dt: _dn_gate_fn(ba, eb, ea, al, dt), ba_ref[...], al_ref[...], dt_ref[...])
        dba, dal, ddt = vjp((dbeta_ref[...], dg_ref[...]))
        dba_ref[...] = dba.astype(BF16)

        @pl.when(pl.program_id(0) == 0)
        def _():
            dal_ref[...] = jnp.zeros_like(dal_ref)
            ddt_ref[...] = jnp.zeros_like(ddt_ref)

        dal_ref[...] += dal
        ddt_ref[...] += ddt

    return pl.pallas_call(
        body, grid=(S // tm,), in_specs=_dn_gate_specs(tm) + [_row(tm, DN_WIDTH), _row(tm, DN_WIDTH)],
        out_specs=[_row(tm, LANES), _full((1, DN_WIDTH)), _full((1, DN_WIDTH))],
        out_shape=[jax.ShapeDtypeStruct((S, LANES), BF16), jax.ShapeDtypeStruct((1, DN_WIDTH), F32),
                   jax.ShapeDtypeStruct((1, DN_WIDTH), F32)],
        name="dn_gate_bwd", compiler_params=_params(("arbitrary",)))(proj, *args, dbeta, dg)


PREP_GROUPS = 2
PREP_CHUNKS = GROUP * PREP_GROUPS


def _dn_prep_specs():
    rows = PREP_CHUNKS * CHUNK
    q = pl.BlockSpec((rows, LANES), lambda h, c: (c, h))
    k = pl.BlockSpec((rows, LANES), lambda h, c: (c, DN_HEADS + h))
    v = pl.BlockSpec((rows, LANES), lambda h, c: (c, 2 * DN_HEADS + h))
    qk = pl.BlockSpec((1, rows, CHUNK), lambda h, c: (h, c, 0))
    egl = pl.BlockSpec((1, PREP_CHUNKS, 1, LANES), lambda h, c: (h, c, 0, 0))
    return q, k, v, qk, egl


def _dn_prep_fwd(qkvn, g, beta):
    S = qkvn.shape[0]
    nc = S // CHUNK
    q, k, v, qks, egl = _dn_prep_specs()

    def body(q_ref, k_ref, v_ref, g_ref, b_ref, u_ref, w_ref, qe_ref, kd_ref, qk_ref, egl_ref):
        for gi in range(PREP_GROUPS):
            r = pl.ds(gi * GROUP_ROWS, GROUP_ROWS)
            u, w, qe, kd = _dn_group(q_ref[r, :], k_ref[r, :], v_ref[r, :], g_ref[r, :], b_ref[r, :])
            u_ref[r, :] = u
            w_ref[r, :] = w
            qe_ref[r, :] = qe
            kd_ref[r, :] = kd
        for i in range(PREP_CHUNKS):
            r = pl.ds(i * CHUNK, CHUNK)
            qk, e = _dn_chunk(q_ref[r, :], k_ref[r, :], g_ref[r, :])
            qk_ref[0, r, :] = qk
            egl_ref[0, i] = e

    wide = jax.ShapeDtypeStruct((S, DN_WIDTH), F32)
    return pl.pallas_call(
        body, grid=(DN_HEADS, nc // PREP_CHUNKS), in_specs=[q, k, v, q, q],
        out_specs=[q, q, q, q, qks, egl],
        out_shape=[wide, wide, wide, wide, jax.ShapeDtypeStruct((DN_HEADS, S, CHUNK), F32),
                   jax.ShapeDtypeStruct((DN_HEADS, nc, 1, LANES), F32)],
        name="dn_prep_fwd", compiler_params=_params(("parallel", "parallel")))(qkvn, qkvn, qkvn, g, beta)


def _dn_prep_bwd(qkvn, g, beta, du, dw, dqe, dkd, dqk, degl):
    S = qkvn.shape[0]
    nc = S // CHUNK
    q, k, v, qks, egl = _dn_prep_specs()

    def body(q_ref, k_ref, v_ref, g_ref, b_ref, du_ref, dw_ref, dqe_ref, dkd_ref, dqk_ref, degl_ref,
             dq_ref, dk_ref, dv_ref, dg_ref, db_ref):
        for gi in range(PREP_GROUPS):
            r = pl.ds(gi * GROUP_ROWS, GROUP_ROWS)
            _, vjp = jax.vjp(_dn_group, q_ref[r, :], k_ref[r, :], v_ref[r, :], g_ref[r, :], b_ref[r, :])
            dq, dk, dv, dg, db = vjp((du_ref[r, :], dw_ref[r, :], dqe_ref[r, :], dkd_ref[r, :]))
            dq_ref[r, :] = dq
            dk_ref[r, :] = dk
            dv_ref[r, :] = dv
            dg_ref[r, :] = dg
            db_ref[r, :] = db
        for i in range(PREP_CHUNKS):
            r = pl.ds(i * CHUNK, CHUNK)
            _, vjp = jax.vjp(_dn_chunk, q_ref[r, :], k_ref[r, :], g_ref[r, :])
            dq, dk, dg = vjp((dqk_ref[0, r, :], degl_ref[0, i]))
            dq_ref[r, :] += dq
            dk_ref[r, :] += dk
            dg_ref[r, :] += dg

    wide = jax.ShapeDtypeStruct((S, DN_WIDTH), F32)
    return pl.pallas_call(
        body, grid=(DN_HEADS, nc // PREP_CHUNKS), in_specs=[q, k, v, q, q, q, q, q, q, qks, egl],
        out_specs=[q] * 5, out_shape=[wide] * 5,
        name="dn_prep_bwd", compiler_params=_params(("parallel", "parallel")),
    )(qkvn, qkvn, qkvn, g, beta, du, dw, dqe, dkd, dqk, degl)


def _dn_scan_specs(nc, reverse):
    def cidx(c):
        return nc - 1 - c if reverse else c

    hc = pl.BlockSpec((CHUNK, DN_WIDTH), lambda c: (cidx(c), 0))
    qk = pl.BlockSpec((DN_HEADS, CHUNK, CHUNK), lambda c: (0, cidx(c), 0))
    egl = pl.BlockSpec((DN_HEADS, 1, 1, LANES), lambda c: (0, cidx(c), 0, 0))
    st = pl.BlockSpec((DN_HEADS, 1, DN_DIM, DN_DIM), lambda c: (0, cidx(c), 0, 0))
    return hc, qk, egl, st


def _dn_scan_fwd(u, w, qe, kd, qk, egl):
    S = u.shape[0]
    nc = S // CHUNK
    hc, qks, egls, st = _dn_scan_specs(nc, False)

    def body(u_ref, w_ref, qe_ref, kd_ref, qk_ref, egl_ref, o_ref, st_ref, s_scr):
        @pl.when(pl.program_id(0) == 0)
        def _():
            s_scr[...] = jnp.zeros_like(s_scr)

        for h in range(DN_HEADS):
            cols = pl.ds(h * DN_DIM, DN_DIM)
            s = s_scr[h]
            st_ref[h, 0] = s
            s_new, o = _dn_step(s, u_ref[:, cols], w_ref[:, cols], qe_ref[:, cols], kd_ref[:, cols], qk_ref[h],
                                egl_ref[h, 0])
            o_ref[:, cols] = o
            s_scr[h] = s_new

    return pl.pallas_call(
        body, grid=(nc,), in_specs=[hc, hc, hc, hc, qks, egls], out_specs=[hc, st],
        out_shape=[jax.ShapeDtypeStruct((S, DN_WIDTH), F32), jax.ShapeDtypeStruct((DN_HEADS, nc, DN_DIM, DN_DIM), F32)],
        scratch_shapes=[pltpu.VMEM((DN_HEADS, DN_DIM, DN_DIM), F32)], name="dn_scan_fwd",
        compiler_params=_params(("arbitrary",)))(u, w, qe, kd, qk, egl)


def _dn_scan_bwd(u, w, qe, kd, qk, egl, states, do):
    S = u.shape[0]
    nc = S // CHUNK
    hc, qks, egls, st = _dn_scan_specs(nc, True)

    def body(u_ref, w_ref, qe_ref, kd_ref, qk_ref, egl_ref, st_ref, do_ref,
             du_ref, dw_ref, dqe_ref, dkd_ref, dqk_ref, degl_ref, ds_scr):
        @pl.when(pl.program_id(0) == 0)
        def _():
            ds_scr[...] = jnp.zeros_like(ds_scr)

        for h in range(DN_HEADS):
            cols = pl.ds(h * DN_DIM, DN_DIM)
            _, vjp = jax.vjp(_dn_step, st_ref[h, 0], u_ref[:, cols], w_ref[:, cols], qe_ref[:, cols],
                             kd_ref[:, cols], qk_ref[h], egl_ref[h, 0])
            ds, du, dw, dqe, dkd, dqk, degl = vjp((ds_scr[h], do_ref[:, cols]))
            ds_scr[h] = ds
            du_ref[:, cols] = du
            dw_ref[:, cols] = dw
            dqe_ref[:, cols] = dqe
            dkd_ref[:, cols] = dkd
            dqk_ref[h] = dqk
            degl_ref[h, 0] = degl

    wide = jax.ShapeDtypeStruct((S, DN_WIDTH), F32)
    return pl.pallas_call(
        body, grid=(nc,), in_specs=[hc, hc, hc, hc, qks, egls, st, hc],
        out_specs=[hc, hc, hc, hc, qks, egls],
        out_shape=[wide, wide, wide, wide, jax.ShapeDtypeStruct((DN_HEADS, S, CHUNK), F32),
                   jax.ShapeDtypeStruct((DN_HEADS, nc, 1, LANES), F32)],
        scratch_shapes=[pltpu.VMEM((DN_HEADS, DN_DIM, DN_DIM), F32)], name="dn_scan_bwd",
        compiler_params=_params(("arbitrary",)))(u, w, qe, kd, qk, egl, states, do)


def _dn_out_fwd(o, proj, gain, tm=512):
    S = o.shape[0]

    def body(o_ref, z_ref, g_ref, y_ref):
        y_ref[...] = _dn_out(o_ref[...], z_ref[...], g_ref[...]).astype(BF16)

    hs = pl.BlockSpec((tm, LANES), lambda i, h: (i, h))
    zs = pl.BlockSpec((tm, LANES), lambda i, h: (i, P_Z // LANES + h))
    return pl.pallas_call(
        body, grid=(S // tm, DN_HEADS), in_specs=[hs, zs, _full((1, DN_DIM))], out_specs=hs,
        out_shape=jax.ShapeDtypeStruct((S, DN_WIDTH), BF16), name="dn_out_fwd",
        compiler_params=_params(("parallel", "parallel")))(o, proj, gain)


def _dn_out_bwd(o, proj, gain, dy, tm=512):
    S = o.shape[0]

    def body(o_ref, z_ref, g_ref, dy_ref, do_ref, dz_ref, dg_ref):
        _, vjp = jax.vjp(_dn_out, o_ref[...], z_ref[...], g_ref[...])
        do, dz, dg = vjp(dy_ref[...])
        do_ref[...] = do
        dz_ref[...] = dz.astype(BF16)

        @pl.when((pl.program_id(0) == 0) & (pl.program_id(1) == 0))
        def _():
            dg_ref[...] = jnp.zeros_like(dg_ref)

        dg_ref[...] += dg

    hs = pl.BlockSpec((tm, LANES), lambda i, h: (i, h))
    zs = pl.BlockSpec((tm, LANES), lambda i, h: (i, P_Z // LANES + h))
    return pl.pallas_call(
        body, grid=(S // tm, DN_HEADS), in_specs=[hs, zs, _full((1, DN_DIM)), hs],
        out_specs=[hs, hs, _full((1, DN_DIM))],
        out_shape=[jax.ShapeDtypeStruct((S, DN_WIDTH), F32), jax.ShapeDtypeStruct((S, DN_WIDTH), BF16),
                   jax.ShapeDtypeStruct((1, DN_DIM), F32)],
        name="dn_out_bwd", compiler_params=_params(("arbitrary", "arbitrary")))(o, proj, gain, dy)


def _rel_buckets():
    qi = np.arange(BLOCK)[:, None]
    kj = np.arange(2 * BLOCK)[None, :]
    n = np.maximum(BLOCK + qi - kj, 0)
    max_exact = REL_BUCKETS // 2
    nf = np.maximum(n, 1).astype(np.float32)
    large = max_exact + (np.log(nf / np.float32(max_exact)) / np.float32(math.log(REL_MAX_DIST / max_exact))
                         * np.float32(REL_BUCKETS - max_exact)).astype(np.int32)
    large = np.minimum(large, REL_BUCKETS - 1)
    return np.where(n < max_exact, n, large).astype(np.int32)


def _bias_fwd(rel_bias):
    buckets = jnp.asarray(_rel_buckets())

    def body(rb_ref, bk_ref, o_ref):
        bk = bk_ref[...]
        for h in range(SWA_HEADS):
            acc = jnp.zeros((BLOCK, 2 * BLOCK), F32)
            for b in range(REL_BUCKETS):
                acc = jnp.where(bk == b, rb_ref[b, h], acc)
            o_ref[h] = acc

    return pl.pallas_call(
        body, in_specs=[pl.BlockSpec(memory_space=pltpu.SMEM), pl.BlockSpec(memory_space=pltpu.VMEM)],
        out_specs=pl.BlockSpec(memory_space=pltpu.VMEM),
        out_shape=jax.ShapeDtypeStruct((SWA_HEADS, BLOCK, 2 * BLOCK), F32), name="swa_bias_fwd",
        compiler_params=_params())(rel_bias, buckets)


def _bias_bwd(dbias):
    buckets = jnp.asarray(_rel_buckets())

    def body(d_ref, bk_ref, o_ref):
        bk = bk_ref[...]
        lane = lax.broadcasted_iota(jnp.int32, (1, LANES), 1)
        for h in range(SWA_HEADS):
            d = d_ref[h]
            row = jnp.zeros((1, LANES), F32)
            for b in range(REL_BUCKETS):
                part = jnp.sum(jnp.where(bk == b, d, 0.0), axis=1, keepdims=True)
                row = jnp.where(lane == b, jnp.sum(part, axis=0, keepdims=True), row)
            o_ref[h:h + 1, :] = row

    return pl.pallas_call(
        body, in_specs=[pl.BlockSpec(memory_space=pltpu.VMEM), pl.BlockSpec(memory_space=pltpu.VMEM)],
        out_specs=pl.BlockSpec(memory_space=pltpu.VMEM),
        out_shape=jax.ShapeDtypeStruct((SWA_HEADS, LANES), F32), name="swa_bias_bwd",
        compiler_params=_params())(dbias, buckets)


def _swa_mask(n):
    qi = lax.broadcasted_iota(jnp.int32, (BLOCK, 2 * BLOCK), 0)
    kj = lax.broadcasted_iota(jnp.int32, (BLOCK, 2 * BLOCK), 1)
    dist = BLOCK + qi - kj
    return (dist >= 0) & (dist < WINDOW) & ((n > 0) | (kj >= BLOCK))


def _swa_in_specs():
    q = pl.BlockSpec((BLOCK, SWA_WIDTH), lambda n: (n, P_SQ // SWA_WIDTH))
    kc = pl.BlockSpec((BLOCK, SWA_KVW), lambda n: (n, P_SK // SWA_KVW))
    kp = pl.BlockSpec((BLOCK, SWA_KVW), lambda n: (jnp.maximum(n - 1, 0), P_SK // SWA_KVW))
    vc = pl.BlockSpec((BLOCK, SWA_KVW), lambda n: (n, P_SV // SWA_KVW))
    vp = pl.BlockSpec((BLOCK, SWA_KVW), lambda n: (jnp.maximum(n - 1, 0), P_SV // SWA_KVW))
    small = [_full((1, SWA_DIM)), _full((1, SWA_DIM)), _full((1, SWA_HEADS)),
             _full((SWA_HEADS, BLOCK, 2 * BLOCK))]
    return [q, kp, kc, vp, vc] + small


def _swa_load(kv, q_ref, kp_ref, kc_ref, vp_ref, vc_ref, s_ref, bias_ref):
    cols = pl.ds(kv * SWA_DIM, SWA_DIM)
    heads = [kv * SWA_GROUP + g for g in range(SWA_GROUP)]
    qs = [q_ref[:, pl.ds(h * SWA_DIM, SWA_DIM)] for h in heads]
    kband = jnp.concatenate([kp_ref[:, cols], kc_ref[:, cols]], axis=0)
    vband = jnp.concatenate([vp_ref[:, cols], vc_ref[:, cols]], axis=0)
    sinks = [s_ref[:, pl.ds(h, 1)] for h in heads]
    biases = [bias_ref[h] for h in heads]
    return heads, qs, kband, vband, sinks, biases


def _swa_fwd(proj, q_gain, k_gain, sinks, bias):
    S = proj.shape[0]

    def body(q_ref, kp_ref, kc_ref, vp_ref, vc_ref, qg_ref, kg_ref, s_ref, bias_ref, y_ref):
        mask = _swa_mask(pl.program_id(0))
        for kv in range(SWA_KV):
            heads, qs, kband, vband, sk, bs = _swa_load(kv, q_ref, kp_ref, kc_ref, vp_ref, vc_ref, s_ref, bias_ref)
            outs = _swa_heads(qs, kband, vband, qg_ref[...], kg_ref[...], sk, bs, mask)
            for h, o in zip(heads, outs):
                y_ref[:, pl.ds(h * SWA_DIM, SWA_DIM)] = o.astype(BF16)

    return pl.pallas_call(
        body, grid=(S // BLOCK,), in_specs=_swa_in_specs(),
        out_specs=pl.BlockSpec((BLOCK, SWA_WIDTH), lambda n: (n, 0)),
        out_shape=jax.ShapeDtypeStruct((S, SWA_WIDTH), BF16), name="swa_fwd",
        compiler_params=_params(("parallel",)))(proj, proj, proj, proj, proj, q_gain, k_gain, sinks, bias)


def _swa_bwd(proj, q_gain, k_gain, sinks, bias, dy):
    S = proj.shape[0]

    def body(q_ref, kp_ref, kc_ref, vp_ref, vc_ref, qg_ref, kg_ref, s_ref, bias_ref, dy_ref,
             dq_ref, dk_ref, dv_ref, dqg_ref, dkg_ref, ds_ref, dbias_ref):
        n = pl.program_id(0)
        mask = _swa_mask(n)

        @pl.when(n == 0)
        def _():
            for r in (dk_ref, dv_ref, dqg_ref, dkg_ref, ds_ref, dbias_ref):
                r[...] = jnp.zeros_like(r)

        cur = pl.ds(pl.multiple_of(n * BLOCK, BLOCK), BLOCK)
        prev = pl.ds(pl.multiple_of(jnp.maximum(n - 1, 0) * BLOCK, BLOCK), BLOCK)
        for kv in range(SWA_KV):
            heads, qs, kband, vband, sk, bs = _swa_load(kv, q_ref, kp_ref, kc_ref, vp_ref, vc_ref, s_ref, bias_ref)
            _, vjp = jax.vjp(lambda qs, kb, vb, qg, kg, sk, bs: _swa_heads(qs, kb, vb, qg, kg, sk, bs, mask),
                             qs, kband, vband, qg_ref[...], kg_ref[...], sk, bs)
            dqs, dkb, dvb, dqg, dkg, dsk, dbs = vjp([dy_ref[:, pl.ds(h * SWA_DIM, SWA_DIM)] for h in heads])
            cols = pl.ds(kv * SWA_DIM, SWA_DIM)
            for h, dq, dsink, db in zip(heads, dqs, dsk, dbs):
                dq_ref[:, pl.ds(h * SWA_DIM, SWA_DIM)] = dq.astype(BF16)
                ds_ref[:, pl.ds(h, 1)] += dsink
                dbias_ref[h] += db
            dqg_ref[...] += dqg
            dkg_ref[...] += dkg
            dk_ref[cur, cols] += dkb[BLOCK:]
            dv_ref[cur, cols] += dvb[BLOCK:]

            @pl.when(n > 0)
            def _():
                dk_ref[prev, cols] += dkb[:BLOCK]
                dv_ref[prev, cols] += dvb[:BLOCK]

    return pl.pallas_call(
        body, grid=(S // BLOCK,),
        in_specs=_swa_in_specs() + [pl.BlockSpec((BLOCK, SWA_WIDTH), lambda n: (n, 0))],
        out_specs=[pl.BlockSpec((BLOCK, SWA_WIDTH), lambda n: (n, 0)), _full((S, SWA_KVW)), _full((S, SWA_KVW)),
                   _full((1, SWA_DIM)), _full((1, SWA_DIM)), _full((1, SWA_HEADS)),
                   _full((SWA_HEADS, BLOCK, 2 * BLOCK))],
        out_shape=[jax.ShapeDtypeStruct((S, SWA_WIDTH), BF16), jax.ShapeDtypeStruct((S, SWA_KVW), F32),
                   jax.ShapeDtypeStruct((S, SWA_KVW), F32), jax.ShapeDtypeStruct((1, SWA_DIM), F32),
                   jax.ShapeDtypeStruct((1, SWA_DIM), F32), jax.ShapeDtypeStruct((1, SWA_HEADS), F32),
                   jax.ShapeDtypeStruct((SWA_HEADS, BLOCK, 2 * BLOCK), F32)],
        name="swa_bwd", compiler_params=_params(("arbitrary",)),
    )(proj, proj, proj, proj, proj, q_gain, k_gain, sinks, bias, dy)


def _position():
    return lax.axis_index("x"), lax.axis_index("y"), lax.axis_index("c")


def _all_gather(shards):
    na = len(shards)

    def body(*refs):
        x_refs, out_refs = refs[:na], refs[na:2 * na]
        send_sems, recv_sems, local_sems = refs[2 * na:]
        x, y, c = _position()
        me, sibling = (x, y, c), (x, y, 1 - c)
        chips = [(1 - x, y), (x, 1 - y), (1 - x, 1 - y)]

        def copy(a, k, block, to, own=False):
            px, py, pc = block
            slot = out_refs[a].at[4 * px + 2 * py + pc]
            return pltpu.make_async_remote_copy(
                src_ref=x_refs[a] if own else slot, dst_ref=slot, send_sem=send_sems.at[7 * a + k],
                recv_sem=recv_sems.at[7 * a + k], device_id=to, device_id_type=MESH_ID)

        mine = [pltpu.make_async_copy(x_refs[a], out_refs[a].at[4 * x + 2 * y + c], local_sems.at[a])
                for a in range(na)]
        for cp in mine:
            cp.start()
        first = []
        for a in range(na):
            first.append(copy(a, 0, me, sibling, own=True))
            first += [copy(a, 1 + j, me, (*chip, c), own=True) for j, chip in enumerate(chips)]
        for cp in first:
            cp.start()
        passed = []
        for j, chip in enumerate(chips):
            for a in range(na):
                copy(a, 1 + j, (*chip, c), me).wait_recv()
                passed.append(copy(a, 4 + j, (*chip, c), sibling))
                passed[-1].start()
        for a in range(na):
            copy(a, 0, sibling, me).wait_recv()
            for j, chip in enumerate(chips):
                copy(a, 4 + j, (*chip, 1 - c), me).wait_recv()
        for cp in first + passed:
            cp.wait_send()
        for cp in mine:
            cp.wait()

    return pl.pallas_call(
        body, in_specs=[pl.BlockSpec(memory_space=pl.ANY)] * na, out_specs=[pl.BlockSpec(memory_space=pl.ANY)] * na,
        out_shape=[jax.ShapeDtypeStruct((N_DEV,) + s.shape, s.dtype) for s in shards],
        scratch_shapes=[pltpu.SemaphoreType.DMA((7 * na,)), pltpu.SemaphoreType.DMA((7 * na,)),
                        pltpu.SemaphoreType.DMA((na,))],
        name="all_gather_weights")(*shards)


def _exchange(blocks):
    na = len(blocks)

    def body(*refs):
        in_refs, out_refs = refs[:na], refs[na:2 * na]
        send_sems, recv_sems, local_sems = refs[2 * na:]
        x, y, c = _position()
        me = 4 * x + 2 * y + c
        local = [pltpu.make_async_copy(in_refs[a].at[me], out_refs[a].at[me], local_sems.at[a]) for a in range(na)]
        for cp in local:
            cp.start()
        sends, recvs = [], []
        for k in range(1, N_DEV):
            px, py, pc = x ^ (k >> 2), y ^ ((k >> 1) & 1), c ^ (k & 1)
            peer = 4 * px + 2 * py + pc
            for a in range(na):
                sems = dict(send_sem=send_sems.at[na * (k - 1) + a], recv_sem=recv_sems.at[na * (k - 1) + a],
                            device_id=(px, py, pc), device_id_type=MESH_ID)
                sends.append(pltpu.make_async_remote_copy(src_ref=in_refs[a].at[peer], dst_ref=out_refs[a].at[me], **sems))
                recvs.append(pltpu.make_async_remote_copy(src_ref=in_refs[a].at[me], dst_ref=out_refs[a].at[peer], **sems))
        for cp in sends:
            cp.start()
        for cp in recvs:
            cp.wait_recv()
        for cp in sends:
            cp.wait_send()
        for cp in local:
            cp.wait()

    return pl.pallas_call(
        body, in_specs=[pl.BlockSpec(memory_space=pl.ANY)] * na, out_specs=[pl.BlockSpec(memory_space=pl.ANY)] * na,
        out_shape=[jax.ShapeDtypeStruct(b.shape, b.dtype) for b in blocks],
        scratch_shapes=[pltpu.SemaphoreType.DMA((7 * na,)), pltpu.SemaphoreType.DMA((7 * na,)),
                        pltpu.SemaphoreType.DMA((na,))],
        name="exchange_grads")(*blocks)


def _adam_update(parts, w, m, v, name, tr=256):
    r, c = w.shape
    tr = _pick_rows(r, tr)
    cp = parts.shape[2]

    def body(p_ref, w_ref, m_ref, v_ref, g_ref, d_ref, nm_ref, nv_ref):
        g = p_ref[0, :, pl.ds(0, c)].astype(F32)
        for i in range(1, N_DEV):
            g = g + p_ref[i, :, pl.ds(0, c)].astype(F32)
        delta, nm, nv = _adamw(w_ref[...], g, m_ref[...], v_ref[...])
        g_ref[...] = g
        d_ref[...] = delta
        nm_ref[...] = nm
        nv_ref[...] = nv

    rs = pl.BlockSpec((tr, c), lambda i: (i, 0))
    return pl.pallas_call(
        body, grid=(r // tr,), in_specs=[pl.BlockSpec((N_DEV, tr, cp), lambda i: (0, i, 0)), rs, rs, rs],
        out_specs=[rs] * 4, out_shape=[jax.ShapeDtypeStruct((r, c), F32)] * 4, name=name,
        compiler_params=_params(("parallel",)))(parts, w, m, v)


def _pick_rows(rows, target):
    if rows <= target:
        return rows
    t = target
    while t >= 16:
        if rows % t == 0:
            return t
        t -= 16
    return rows


BIG = ("w_in", "w_branch_dn", "w_branch_swa", "w_out", "w_gate", "w_up", "w_down")
IN_SHARD, IN_WIRE = D_IN // N_DEV, 640
FF_SHARD, FF_WIRE = D_FF // N_DEV, 384
D_FFP = N_DEV * FF_WIRE
BIG_SHAPES = {"w_in": ((D_MODEL, IN_SHARD), (D_MODEL, IN_WIRE)),
              "w_branch_dn": ((DN_WIDTH, LANES), (DN_WIDTH, LANES)),
              "w_branch_swa": ((SWA_WIDTH, LANES), (SWA_WIDTH, LANES)),
              "w_out": ((LANES, D_MODEL), (LANES, D_MODEL)),
              "w_gate": ((D_MODEL, FF_SHARD), (D_MODEL, FF_WIRE)),
              "w_up": ((D_MODEL, FF_SHARD), (D_MODEL, FF_WIRE)),
              "w_down": ((FF_SHARD, D_MODEL), (FF_WIRE, D_MODEL))}
CONV_SHARD, CONV_WIRE = (DN_CONV, DN_QKV // N_DEV), (8, 256)


def _pad_to(a, shape):
    return jnp.pad(a, [(0, t - s) for s, t in zip(a.shape, shape)])


_IN_SEGS = ((R_GATE, 2048, P_GATE), (R_QKV, DN_QKV, P_QKV), (R_Z, DN_WIDTH, P_Z), (R_SQ, SWA_WIDTH, P_SQ),
            (R_SK, SWA_KVW, P_SK), (R_SV, SWA_KVW, P_SV), (R_B, 8, P_BA))


def _w_in_from_blocks(blocks):
    parts = []
    for rs, n, _ in _IN_SEGS:
        for dev in range(N_DEV):
            lo, hi = max(rs, IN_SHARD * dev), min(rs + n, IN_SHARD * (dev + 1))
            if lo < hi:
                parts.append(blocks[dev, :, lo - IN_SHARD * dev:hi - IN_SHARD * dev])
    parts.append(jnp.zeros((blocks.shape[1], P_WIDTH - P_BA - 8), blocks.dtype))
    return jnp.concatenate(parts, axis=1)


def _w_in_to_blocks(g):
    out = []
    for dev in range(N_DEV):
        parts = []
        for rs, n, ps in sorted(_IN_SEGS):
            lo, hi = max(rs, IN_SHARD * dev), min(rs + n, IN_SHARD * (dev + 1))
            if lo < hi:
                parts.append(g[:, ps + lo - rs:ps + hi - rs])
        parts.append(jnp.zeros((g.shape[0], IN_WIRE - IN_SHARD), g.dtype))
        out.append(jnp.concatenate(parts, axis=1))
    return jnp.stack(out)


def _pack(flat_parts, rows):
    flat = jnp.concatenate(flat_parts, axis=-1)
    pad = rows * LANES - flat.shape[-1]
    flat = jnp.pad(flat, [(0, 0)] * (flat.ndim - 1) + [(0, pad)])
    return flat.reshape(flat.shape[:-1] + (rows, LANES))


def _unpack(buf, shapes):
    lead = buf.shape[:-2]
    flat = buf.reshape(lead + (-1,))
    out, off = [], 0
    for shp in shapes:
        n = int(np.prod(shp))
        out.append(flat[..., off:off + n].reshape(lead + tuple(shp)))
        off += n
    return out


def _cols_join(blocks):
    return jnp.concatenate([blocks[d] for d in range(N_DEV)], axis=1)


def _cols_split(full):
    c = full.shape[1] // N_DEV
    return jnp.stack([full[:, d * c:(d + 1) * c] for d in range(N_DEV)])


SMALL = ("attn_norm", "ffn_norm", "rel_bias", "dn_out_norm", "swa_q_norm", "swa_k_norm", "dn_a_log",
         "dn_dt_bias", "swa_sinks")
SMALL_ROWS = 24
CONV_ROWS = DN_CONV * DN_QKV // LANES


def _pack_small(d):
    return _pack([d[n].reshape(-1) for n in SMALL], SMALL_ROWS)


def kernel(x, attn_norm, w_in, dn_conv, dn_a_log, dn_dt_bias, dn_out_norm, swa_q_norm, swa_k_norm, swa_sinks, rel_bias, w_branch_dn, w_branch_swa, w_out, ffn_norm, w_gate, w_up, w_down, loss_target, m_attn_norm, m_w_in, m_dn_conv, m_dn_a_log, m_dn_dt_bias, m_dn_out_norm, m_swa_q_norm, m_swa_k_norm, m_swa_sinks, m_rel_bias, m_w_branch_dn, m_w_branch_swa, m_w_out, m_ffn_norm, m_w_gate, m_w_up, m_w_down, v_attn_norm, v_w_in, v_dn_conv, v_dn_a_log, v_dn_dt_bias, v_dn_out_norm, v_swa_q_norm, v_swa_k_norm, v_swa_sinks, v_rel_bias, v_w_branch_dn, v_w_branch_swa, v_w_out, v_ffn_norm, v_w_gate, v_w_up, v_w_down):
    args = dict(locals())
    S = x.shape[1]
    xs = x.reshape(S, D_MODEL)
    target = loss_target.reshape(S, D_MODEL)

    w_loc = {n: args[n].reshape(BIG_SHAPES[n][0]) for n in BIG}
    conv_loc = dn_conv.reshape(CONV_SHARD)
    gathered = _all_gather([_pad_to(w_loc[n], BIG_SHAPES[n][1]).astype(BF16) for n in BIG]
                           + [_pad_to(conv_loc, CONV_WIRE)])
    G = dict(zip(BIG, gathered))
    w_pad = _w_in_from_blocks(G["w_in"])
    conv_w = jnp.concatenate([gathered[-1][d, :DN_CONV, :CONV_SHARD[1]] for d in range(N_DEV)], axis=1)
    w_bdn = _cols_join(G["w_branch_dn"])
    w_bswa = _cols_join(G["w_branch_swa"])
    w_o = G["w_out"].reshape(D_MODEL, D_MODEL)
    w_d = G["w_down"].reshape(D_FFP, D_MODEL)
    w_g = _cols_join(G["w_gate"])
    w_u = _cols_join(G["w_up"])

    h = _norm_fwd(xs, attn_norm, "norm1_fwd")
    proj = _mm([(h, w_pad)], "nn", F32, "mm_in", 512, 1664, j_outer=True)
    qkvn = _dn_conv_fwd(proj, conv_w)
    beta, g = _dn_gate_fwd(proj, dn_a_log, dn_dt_bias)
    u, w, qe, kd, qk, egl = _dn_prep_fwd(qkvn, g, beta)
    o, states = _dn_scan_fwd(u, w, qe, kd, qk, egl)
    y_dn = _dn_out_fwd(o, proj, dn_out_norm)
    bias = _bias_fwd(rel_bias)
    y_swa = _swa_fwd(proj, swa_q_norm, swa_k_norm, swa_sinks, bias)
    a_dn = _mm([(y_dn, w_bdn)], "nn", F32, "mm_branch_dn", 1024, D_MODEL)
    a_swa = _mm([(y_swa, w_bswa)], "nn", F32, "mm_branch_swa", 1024, D_MODEL)
    merged = _merge_fwd(proj, a_dn, a_swa)
    t_out = _mm([(merged, w_o)], "nn", F32, "mm_out", 1024, D_MODEL)
    x1, h2 = _resid_norm_fwd(xs, t_out, ffn_norm, "norm2_fwd")
    gate = _mm([(h2, w_g)], "nn", F32, "mm_gate", 1024, 1024, j_outer=True)
    up = _mm([(h2, w_u)], "nn", F32, "mm_up", 1024, 1024, j_outer=True)
    act = _act_fwd(gate, up)
    f = _mm([(act, w_d)], "nn", F32, "mm_down", 512, D_MODEL)
    dy, dy_b, loss_local = _loss_fwd_bwd(x1, f, target)

    dact = _mm([(dy_b, w_d)], "nt", F32, "mm_dact", 1024, 1024, j_outer=True)
    g_w_down = _mm([(act, dy_b)], "tn", BF16, "mm_dw_down", 768, D_MODEL, j_outer=True)
    g_w_down = g_w_down.reshape(N_DEV, FF_WIRE, D_MODEL)
    dgate, dup = _act_bwd(gate, up, dact)
    dh2 = _mm([(dgate, w_g), (dup, w_u)], "nt", F32, "mm_dh2", 512, 512, j_outer=True)
    g_w_gate = _cols_split(_mm([(h2, dgate)], "tn", BF16, "mm_dw_gate", D_MODEL, 768))
    g_w_up = _cols_split(_mm([(h2, dup)], "tn", BF16, "mm_dw_up", D_MODEL, 768))
    dx1, dx1_b, g_ffn_norm = _norm_bwd(x1, [dh2], dy, ffn_norm, "norm2_bwd")
    dmerged = _mm([(dx1_b, w_o)], "nt", F32, "mm_dmerged", 1024, D_MODEL)
    g_w_out = _mm([(merged, dx1_b)], "tn", BF16, "mm_dw_out", 512, D_MODEL, j_outer=True)
    g_w_out = g_w_out.reshape(N_DEV, LANES, D_MODEL)
    dg0, dg1, da_dn, da_swa = _merge_bwd(proj, a_dn, a_swa, dmerged)
    dy_dn = _mm([(da_dn, w_bdn)], "nt", F32, "mm_dy_dn", 1024, DN_WIDTH)
    dy_swa = _mm([(da_swa, w_bswa)], "nt", F32, "mm_dy_swa", 1024, SWA_WIDTH)
    g_w_bdn = _cols_split(_mm([(y_dn, da_dn)], "tn", BF16, "mm_dw_branch_dn", DN_WIDTH, 512))
    g_w_bswa = _cols_split(_mm([(y_swa, da_swa)], "tn", BF16, "mm_dw_branch_swa", SWA_WIDTH, 512))
    dsq, dsk, dsv, g_q_norm, g_k_norm, g_sinks, dbias = _swa_bwd(proj, swa_q_norm, swa_k_norm, swa_sinks, bias, dy_swa)
    g_rel_bias = _bias_bwd(dbias)[:, :REL_BUCKETS].T
    do, dz, g_out_norm = _dn_out_bwd(o, proj, dn_out_norm, dy_dn)
    du, dw, dqe, dkd, dqk, degl = _dn_scan_bwd(u, w, qe, kd, qk, egl, states, do)
    dq, dk, dv, dgd, dbeta = _dn_prep_bwd(qkvn, g, beta, du, dw, dqe, dkd, dqk, degl)
    dqkvn = jnp.concatenate([dq, dk, dv], axis=1)
    dba, dal, ddt = _dn_gate_bwd(proj, dn_a_log, dn_dt_bias, dbeta, dgd)
    g_a_log = dal.reshape(DN_HEADS, DN_DIM).sum(axis=1)
    g_dt_bias = ddt.reshape(DN_HEADS, DN_DIM).sum(axis=1)
    dqkv, g_conv = _dn_conv_bwd(proj, conv_w, dqkvn)
    dproj = jnp.concatenate([dg0, dg1, dqkv, dz, dsq, dsk.astype(BF16), dsv.astype(BF16), dba], axis=1)
    dh = _mm([(dproj, w_pad)], "nt", F32, "mm_dh", 512, D_MODEL)
    g_w_in = _w_in_to_blocks(_mm([(h, dproj)], "tn", BF16, "mm_dw_in", 512, 1664, j_outer=True))
    dx, _, g_attn_norm = _norm_bwd(xs, [dh], dx1, attn_norm, "norm1_bwd")

    g_send = {"w_in": g_w_in, "w_branch_dn": g_w_bdn, "w_branch_swa": g_w_bswa, "w_out": g_w_out,
              "w_gate": g_w_gate, "w_up": g_w_up, "w_down": g_w_down}
    g_small = {"attn_norm": g_attn_norm, "ffn_norm": g_ffn_norm, "rel_bias": g_rel_bias, "dn_out_norm": g_out_norm,
               "swa_q_norm": g_q_norm, "swa_k_norm": g_k_norm, "dn_a_log": g_a_log, "dn_dt_bias": g_dt_bias,
               "swa_sinks": g_sinks}
    small_rows = jnp.concatenate([_pack_small(g_small), g_conv.reshape(CONV_ROWS, LANES)], axis=0)
    send_small = jnp.broadcast_to(small_rows[None], (N_DEV,) + small_rows.shape)
    received = _exchange([g_send[n] for n in BIG] + [send_small])
    recv_small = received[-1]

    outs = {}
    for n, parts in zip(BIG, received):
        shp = BIG_SHAPES[n][0]
        outs[n] = _adam_update(parts, w_loc[n], args["m_" + n].reshape(shp), args["v_" + n].reshape(shp), "adam_" + n)
    me = 4 * lax.axis_index("x") + 2 * lax.axis_index("y") + lax.axis_index("c")
    conv_parts = lax.dynamic_slice_in_dim(recv_small[:, SMALL_ROWS:].reshape(N_DEV, DN_CONV, DN_QKV),
                                          me * CONV_SHARD[1], CONV_SHARD[1], axis=2)
    outs["dn_conv"] = _adam_update(conv_parts, conv_loc, m_dn_conv.reshape(CONV_SHARD), v_dn_conv.reshape(CONV_SHARD),
                                   "adam_dn_conv")
    small_out = _adam_update(recv_small[:, :SMALL_ROWS], _pack_small({n: args[n] for n in SMALL}),
                             _pack_small({n: args["m_" + n] for n in SMALL}),
                             _pack_small({n: args["v_" + n] for n in SMALL}), "adam_small")

    names = ("attn_norm", "w_in", "dn_conv", "dn_a_log", "dn_dt_bias", "dn_out_norm", "swa_q_norm", "swa_k_norm",
             "swa_sinks", "rel_bias", "w_branch_dn", "w_branch_swa", "w_out", "ffn_norm", "w_gate", "w_up", "w_down")
    results = []
    for kind in range(4):
        small = dict(zip(SMALL, _unpack(small_out[kind], [args[n].shape for n in SMALL])))
        results += [outs[n][kind].reshape(args[n].shape) if n in outs else small[n] for n in names]

    loss = lax.psum(loss_local[0, 0], ("x", "y", "c"))
    return (loss, dx.reshape(x.shape), *results)
```

```python
import math

import numpy as np
import jax
import jax.numpy as jnp
from jax import lax
from jax.experimental import pallas as pl
from jax.experimental.pallas import tpu as pltpu

F32 = jnp.float32
BF16 = jnp.bfloat16
HI = lax.Precision.HIGHEST

D_MODEL = 1024
DN_HEADS = 4
DN_DIM = 128
DN_WIDTH = 512
DN_QKV = 1536
DN_CONV = 4
CHUNK = 64
SWA_HEADS = 8
SWA_KV = 2
SWA_GROUP = 4
SWA_DIM = 64
SWA_WIDTH = 512
SWA_KVW = 128
WINDOW = 128
BLOCK = 128
REL_BUCKETS = 32
REL_MAX_DIST = 128
D_FF = 2816
D_IN = 4872
EPS = 1e-6
N_DEV = 8

ADAM_LR = 0.001
ADAM_B1 = 0.9
ADAM_B2 = 0.999
ADAM_EPS = 1e-08
ADAM_WD = 0.01
ADAM_STEP = 10

P_GATE, P_QKV, P_Z, P_SQ, P_SK, P_SV, P_BA = 0, 2048, 3584, 4096, 4608, 4736, 4864
P_WIDTH = 4992
R_QKV, R_Z, R_B, R_A, R_SQ, R_SK, R_SV, R_GATE = 0, 1536, 2048, 2052, 2056, 2568, 2696, 2824

VMEM_LIMIT = 56 * 1024 * 1024
LANES = 128
MESH_ID = pl.DeviceIdType.MESH


def _params(sem=None):
    return pltpu.CompilerParams(dimension_semantics=sem, vmem_limit_bytes=VMEM_LIMIT)


def _pick(dim, target):
    if dim <= target:
        return dim
    t = target - target % LANES
    while t >= LANES:
        if dim % t == 0:
            return t
        t -= LANES
    return dim


_DIMS = {"nn": (((1,), (0,)), ((), ())), "nt": (((1,), (1,)), ((), ())), "tn": (((0,), (0,)), ((), ()))}


def _mm(pairs, mode, out_dtype, name, bm, bn, j_outer=False):
    a0, b0 = pairs[0]
    if mode == "nn":
        (M, K), (K2, N) = a0.shape, b0.shape
    elif mode == "nt":
        (M, K), (N, K2) = a0.shape, b0.shape
    else:
        (K, M), (K2, N) = a0.shape, b0.shape
    bm, bn = min(bm, M), min(bn, N)
    assert K == K2 and M % bm == 0 and N % bn == 0, (name, a0.shape, b0.shape, bm, bn)
    dims = _DIMS[mode]
    n = len(pairs)

    def body(*refs):
        o_ref = refs[2 * n]
        acc = None
        for t in range(n):
            p = lax.dot_general(refs[2 * t][...].astype(BF16), refs[2 * t + 1][...].astype(BF16), dims,
                                preferred_element_type=F32)
            acc = p if acc is None else acc + p
        o_ref[...] = acc.astype(out_dtype)

    def ij(f):
        return (lambda j, i: f(i, j)) if j_outer else f

    a_spec = pl.BlockSpec((K, bm), ij(lambda i, j: (0, i))) if mode == "tn" else pl.BlockSpec((bm, K), ij(lambda i, j: (i, 0)))
    b_spec = pl.BlockSpec((bn, K), ij(lambda i, j: (j, 0))) if mode == "nt" else pl.BlockSpec((K, bn), ij(lambda i, j: (0, j)))
    grid = (N // bn, M // bm) if j_outer else (M // bm, N // bn)
    return pl.pallas_call(
        body, grid=grid, in_specs=[a_spec, b_spec] * n, out_specs=pl.BlockSpec((bm, bn), ij(lambda i, j: (i, j))),
        out_shape=jax.ShapeDtypeStruct((M, N), out_dtype), name=name,
        compiler_params=_params(("parallel", "parallel")),
    )(*[x for pair in pairs for x in pair])


def _rms(x, gain):
    return x * lax.rsqrt(jnp.mean(x * x, axis=-1, keepdims=True) + EPS) * gain


def _silu(x):
    return x * jax.nn.sigmoid(x)


def _act(g, u):
    return _silu(g) * u


def _merge(g0, g1, a_dn, a_swa):
    return jax.nn.sigmoid(g0) * a_dn + jax.nn.sigmoid(g1) * a_swa


def _dn_post(c, is_v, q_scale):
    a = _silu(c)
    rs = lax.rsqrt(jnp.sum(a * a, axis=-1, keepdims=True) + EPS) * q_scale
    return a * jnp.where(is_v, 1.0, rs)


def _dn_out(o, z, gain):
    return _rms(o, gain) * _silu(z)


def _dot(a, b, dims=_DIMS["nn"], hi=False):
    if hi:
        return lax.dot_general(a, b, dims, precision=HI, preferred_element_type=F32)
    return lax.dot_general(a.astype(BF16), b.astype(BF16), dims, preferred_element_type=F32)


def _pieces(x):
    hi = x.astype(BF16)
    r1 = x - hi.astype(F32)
    mid = r1.astype(BF16)
    return hi, mid, (r1 - mid.astype(F32)).astype(BF16)


def _sel_left_impl(m, x):
    mb = m.astype(BF16)
    hi, mid, lo = _pieces(x)
    return _dot(mb, hi) + (_dot(mb, mid) + _dot(mb, lo))


@jax.custom_vjp
def _sel_left(m, mt, x):
    return _sel_left_impl(m, x)


_sel_left.defvjp(lambda m, mt, x: (_sel_left_impl(m, x), (m, mt)),
                 lambda res, ct: (jnp.zeros_like(res[0]), jnp.zeros_like(res[1]), _sel_left_impl(res[1], ct)))


def _sel_right_impl(x, s):
    sb = s.astype(BF16)
    hi, mid, lo = _pieces(x)
    return _dot(hi, sb) + (_dot(mid, sb) + _dot(lo, sb))


@jax.custom_vjp
def _sel_right(x, s, st):
    return _sel_right_impl(x, s)


_sel_right.defvjp(lambda x, s, st: (_sel_right_impl(x, s), (s, st)),
                  lambda res, ct: (_sel_right_impl(ct, res[1]), jnp.zeros_like(res[0]), jnp.zeros_like(res[1])))


def _dot3_impl(a, b):
    a_hi, a_lo, _ = _pieces(a)
    b_hi, b_lo, _ = _pieces(b)
    return _dot(a_hi, b_hi) + (_dot(a_hi, b_lo) + _dot(a_lo, b_hi))


@jax.custom_vjp
def _dot3(a, b):
    return _dot3_impl(a, b)


_dot3.defvjp(lambda a, b: (_dot3_impl(a, b), (a, b)),
             lambda res, ct: (_dot(ct, res[1], _DIMS["nt"]), _dot(res[0], ct, _DIMS["tn"])))


def _inv_impl(a, eye, strict):
    t = eye - a
    p = _dot(a, a)
    for level in range(5):
        t = t + _dot(t, p)
        if level < 4:
            p = _dot(p, p)
    t = t + _dot3_impl(t, eye - _dot3_impl(eye + a, t))
    return jnp.where(strict > 0.5, t, eye)


@jax.custom_vjp
def _inv_unit_lower(a, eye, strict):
    return _inv_impl(a, eye, strict)


def _inv_bwd(res, ct):
    t, eye, strict = res
    da = -_dot(_dot(t, ct, _DIMS["tn"]), t, _DIMS["nt"])
    return da, jnp.zeros_like(eye), jnp.zeros_like(strict)


def _inv_fwd(a, eye, strict):
    t = _inv_impl(a, eye, strict)
    return t, (t, eye, strict)


_inv_unit_lower.defvjp(_inv_fwd, _inv_bwd)

GROUP = 4
GROUP_ROWS = GROUP * CHUNK


def _block_consts(n):
    ii = lax.broadcasted_iota(jnp.int32, (n, n), 0)
    jj = lax.broadcasted_iota(jnp.int32, (n, n), 1)
    shift = CHUNK.bit_length() - 1
    same = jnp.right_shift(ii, shift) == jnp.right_shift(jj, shift)
    return same & (ii >= jj), same & (ii <= jj), same & (ii > jj), same, ii == jj


def _lane0(n):
    s = (lax.broadcasted_iota(jnp.int32, (LANES, n), 0) == 0).astype(F32)
    st = (lax.broadcasted_iota(jnp.int32, (n, LANES), 1) == 0).astype(F32)
    return s, st


def _dn_group(q, k, v, g, beta):
    n = GROUP_ROWS
    low_b, upp_b, strict_b, same_b, eye_b = _block_consts(n)
    low, upp, same, eye = low_b.astype(F32), upp_b.astype(F32), same_b.astype(F32), eye_b.astype(F32)
    s, st = _lane0(n)
    gc = _sel_left(low, upp, g)
    gl = _sel_left(same, same, g)
    col = _sel_right(gc, s, st)
    decay = jnp.exp(jnp.where(low_b, col - col.T, -jnp.inf))
    kb = k * beta
    vb = v * beta
    a = jnp.where(strict_b, _dot(kb, k, _DIMS["nt"]) * decay, 0.0)
    t = _inv_unit_lower(a, eye, strict_b.astype(F32))
    u = _dot3(t, vb)
    w = _dot3(t, kb * jnp.exp(gc))
    return u, w, q * jnp.exp(gc), k * jnp.exp(gl - gc)


def _dn_chunk(q, k, g):
    ii = lax.broadcasted_iota(jnp.int32, (CHUNK, CHUNK), 0)
    jj = lax.broadcasted_iota(jnp.int32, (CHUNK, CHUNK), 1)
    low = (ii >= jj).astype(F32)
    upp = (ii <= jj).astype(F32)
    ones = jnp.ones((CHUNK, CHUNK), F32)
    eye = (ii == jj).astype(F32)
    s, st = _lane0(CHUNK)
    gc = _sel_left(low, upp, g)
    col = _sel_right(gc, s, st)
    row = _sel_left(ones, ones, col * eye)
    decay = jnp.exp(jnp.where(ii >= jj, col - row, -jnp.inf))
    qk = _dot(q, k, _DIMS["nt"]) * decay
    return qk, jnp.exp(jnp.sum(g, axis=0, keepdims=True))


def _dn_step(s, u, w, qe, kd, qk, egl):
    v_new = u - _dot(w, s)
    o = _dot(qe, s) + _dot(qk, v_new)
    s_new = s * egl + _dot(kd, v_new, _DIMS["tn"])
    return s_new, o


def _swa_heads(qs, kband, vband, qg, kg, sinks, biases, mask):
    kn = _rms(kband, kg)
    outs = []
    for q, sink, bias in zip(qs, sinks, biases):
        qn = _rms(q, qg)
        logits = _dot(qn, kn, _DIMS["nt"]) * (SWA_DIM ** -0.5)
        logits = jnp.where(mask, logits + bias, -jnp.inf)
        m = jnp.maximum(jnp.max(logits, axis=-1, keepdims=True), sink)
        p = jnp.exp(logits - m)
        denom = jnp.sum(p, axis=-1, keepdims=True) + jnp.exp(sink - m)
        outs.append(_dot(p / denom, vband))
    return outs


def _adamw(w, g, m, v):
    m = ADAM_B1 * m + (1.0 - ADAM_B1) * g
    v = ADAM_B2 * v + (1.0 - ADAM_B2) * jnp.square(g)
    m_hat = m / (1.0 - ADAM_B1 ** ADAM_STEP)
    v_hat = v / (1.0 - ADAM_B2 ** ADAM_STEP)
    delta = -ADAM_LR * (m_hat / (jnp.sqrt(v_hat) + ADAM_EPS) + ADAM_WD * w)
    return delta, m, v


def _row(tm, c, cb=0):
    return pl.BlockSpec((tm, c), lambda i, cb=cb: (i, cb))


def _full(shape):
    nd = len(shape)
    return pl.BlockSpec(shape, lambda *_, nd=nd: (0,) * nd)


def _norm_fwd(x, gain, name, tm=512):
    S = x.shape[0]

    def body(x_ref, g_ref, h_ref):
        h_ref[...] = _rms(x_ref[...], g_ref[...]).astype(BF16)

    return pl.pallas_call(
        body, grid=(S // tm,), in_specs=[_row(tm, D_MODEL), _full((1, D_MODEL))],
        out_specs=_row(tm, D_MODEL), out_shape=jax.ShapeDtypeStruct((S, D_MODEL), BF16),
        name=name, compiler_params=_params(("parallel",)))(x, gain)


def _resid_norm_fwd(x, t, gain, name, tm=512):
    S = x.shape[0]

    def body(x_ref, t_ref, g_ref, x1_ref, h_ref):
        x1 = x_ref[...] + t_ref[...]
        x1_ref[...] = x1
        h_ref[...] = _rms(x1, g_ref[...]).astype(BF16)

    return pl.pallas_call(
        body, grid=(S // tm,), in_specs=[_row(tm, D_MODEL), _row(tm, D_MODEL), _full((1, D_MODEL))],
        out_specs=[_row(tm, D_MODEL), _row(tm, D_MODEL)],
        out_shape=[jax.ShapeDtypeStruct((S, D_MODEL), F32), jax.ShapeDtypeStruct((S, D_MODEL), BF16)],
        name=name, compiler_params=_params(("parallel",)))(x, t, gain)


def _norm_bwd(x, dh_list, dres, gain, name, tm=256):
    S = x.shape[0]
    n = len(dh_list)

    def body(*refs):
        x_ref, g_ref, r_ref = refs[0], refs[1], refs[2]
        dh_refs = refs[3:3 + n]
        dx_ref, dxb_ref, dg_ref = refs[3 + n], refs[4 + n], refs[5 + n]
        dh = dh_refs[0][...].astype(F32)
        for r in dh_refs[1:]:
            dh = dh + r[...].astype(F32)
        _, vjp = jax.vjp(_rms, x_ref[...], g_ref[...])
        dx, dg = vjp(dh)
        dx = dx + r_ref[...]
        dx_ref[...] = dx
        dxb_ref[...] = dx.astype(BF16)

        @pl.when(pl.program_id(0) == 0)
        def _():
            dg_ref[...] = jnp.zeros_like(dg_ref)

        dg_ref[...] += dg

    return pl.pallas_call(
        body, grid=(S // tm,),
        in_specs=[_row(tm, D_MODEL), _full((1, D_MODEL)), _row(tm, D_MODEL)] + [_row(tm, D_MODEL)] * n,
        out_specs=[_row(tm, D_MODEL), _row(tm, D_MODEL), _full((1, D_MODEL))],
        out_shape=[jax.ShapeDtypeStruct((S, D_MODEL), F32), jax.ShapeDtypeStruct((S, D_MODEL), BF16),
                   jax.ShapeDtypeStruct((1, D_MODEL), F32)],
        name=name, compiler_params=_params(("arbitrary",)))(x, gain, dres, *dh_list)


def _loss_fwd_bwd(x1, f, target, tm=512):
    S = x1.shape[0]

    def body(x_ref, f_ref, t_ref, dy_ref, dyb_ref, l_ref):
        diff = x_ref[...] + f_ref[...] - t_ref[...]
        dy = diff * (1.0 / D_MODEL)
        dy_ref[...] = dy
        dyb_ref[...] = dy.astype(BF16)

        @pl.when(pl.program_id(0) == 0)
        def _():
            l_ref[...] = jnp.zeros_like(l_ref)

        l_ref[...] += jnp.sum(jnp.mean(diff * diff, axis=-1, keepdims=True), axis=0, keepdims=True) * 0.5

    return pl.pallas_call(
        body, grid=(S // tm,), in_specs=[_row(tm, D_MODEL)] * 3,
        out_specs=[_row(tm, D_MODEL), _row(tm, D_MODEL), _full((1, 1))],
        out_shape=[jax.ShapeDtypeStruct((S, D_MODEL), F32), jax.ShapeDtypeStruct((S, D_MODEL), BF16),
                   jax.ShapeDtypeStruct((1, 1), F32)],
        name="loss_fwd_bwd", compiler_params=_params(("arbitrary",)))(x1, f, target)


def _act_fwd(g, u, tm=256):
    S, width = g.shape

    def body(g_ref, u_ref, o_ref):
        o_ref[...] = _act(g_ref[...], u_ref[...]).astype(BF16)

    return pl.pallas_call(
        body, grid=(S // tm,), in_specs=[_row(tm, width)] * 2, out_specs=_row(tm, width),
        out_shape=jax.ShapeDtypeStruct((S, width), BF16), name="act_fwd",
        compiler_params=_params(("parallel",)))(g, u)


def _act_bwd(g, u, dact, tm=256):
    S, width = g.shape

    def body(g_ref, u_ref, d_ref, dg_ref, du_ref):
        _, vjp = jax.vjp(_act, g_ref[...], u_ref[...])
        dg, du = vjp(d_ref[...])
        dg_ref[...] = dg.astype(BF16)
        du_ref[...] = du.astype(BF16)

    return pl.pallas_call(
        body, grid=(S // tm,), in_specs=[_row(tm, width)] * 3, out_specs=[_row(tm, width)] * 2,
        out_shape=[jax.ShapeDtypeStruct((S, width), BF16)] * 2, name="act_bwd",
        compiler_params=_params(("parallel",)))(g, u, dact)


def _merge_fwd(proj, a_dn, a_swa, tm=512, tc=512):
    S = proj.shape[0]
    nc = D_MODEL // tc

    def spec(off):
        return pl.BlockSpec((tm, tc), lambda i, j, off=off: (i, off + j))

    def body(g0_ref, g1_ref, ad_ref, as_ref, o_ref):
        o_ref[...] = _merge(g0_ref[...], g1_ref[...], ad_ref[...], as_ref[...]).astype(BF16)

    return pl.pallas_call(
        body, grid=(S // tm, nc), in_specs=[spec(P_GATE // tc), spec(P_GATE // tc + nc), spec(0), spec(0)],
        out_specs=spec(0), out_shape=jax.ShapeDtypeStruct((S, D_MODEL), BF16), name="merge_fwd",
        compiler_params=_params(("parallel", "parallel")))(proj, proj, a_dn, a_swa)


def _merge_bwd(proj, a_dn, a_swa, dmerged, tm=512, tc=512):
    S = proj.shape[0]
    nc = D_MODEL // tc

    def spec(off):
        return pl.BlockSpec((tm, tc), lambda i, j, off=off: (i, off + j))

    def body(g0_ref, g1_ref, ad_ref, as_ref, d_ref, dg0_ref, dg1_ref, dad_ref, das_ref):
        _, vjp = jax.vjp(_merge, g0_ref[...], g1_ref[...], ad_ref[...], as_ref[...])
        dg0, dg1, dad, das = vjp(d_ref[...])
        dg0_ref[...] = dg0.astype(BF16)
        dg1_ref[...] = dg1.astype(BF16)
        dad_ref[...] = dad.astype(BF16)
        das_ref[...] = das.astype(BF16)

    return pl.pallas_call(
        body, grid=(S // tm, nc),
        in_specs=[spec(P_GATE // tc), spec(P_GATE // tc + nc), spec(0), spec(0), spec(0)],
        out_specs=[spec(0)] * 4, out_shape=[jax.ShapeDtypeStruct((S, D_MODEL), BF16)] * 4,
        name="merge_bwd", compiler_params=_params(("parallel", "parallel")))(proj, proj, a_dn, a_swa, dmerged)


def _shift_down(x, s):
    row = lax.broadcasted_iota(jnp.int32, x.shape, 0)
    return jnp.where(row >= s, pltpu.roll(x, s, axis=0), 0.0)


def _shift_up(x, s):
    n = x.shape[0]
    row = lax.broadcasted_iota(jnp.int32, x.shape, 0)
    return jnp.where(row < n - s, pltpu.roll(x, n - s, axis=0), 0.0)


def _conv(x, w):
    out = w[DN_CONV - 1:DN_CONV] * x
    for s in range(1, DN_CONV):
        out = out + w[DN_CONV - 1 - s:DN_CONV - s] * _shift_down(x, s)
    return out


def _dn_conv_fwd(proj, conv_w):
    S = proj.shape[0]
    nb = DN_QKV // LANES

    def body(x_ref, w_ref, o_ref):
        j = pl.program_id(0)
        q_scale = jnp.where(j < DN_HEADS, DN_DIM ** -0.5, 1.0).astype(F32)
        o_ref[...] = _dn_post(_conv(x_ref[...], w_ref[...]), j >= 2 * DN_HEADS, q_scale)

    return pl.pallas_call(
        body, grid=(nb,),
        in_specs=[pl.BlockSpec((S, LANES), lambda j: (0, P_QKV // LANES + j)),
                  pl.BlockSpec((DN_CONV, LANES), lambda j: (0, j))],
        out_specs=pl.BlockSpec((S, LANES), lambda j: (0, j)),
        out_shape=jax.ShapeDtypeStruct((S, DN_QKV), F32), name="dn_conv_fwd",
        compiler_params=_params(("parallel",)))(proj, conv_w)


def _dn_conv_bwd(proj, conv_w, dqkvn):
    S = proj.shape[0]
    nb = DN_QKV // LANES

    def body(x_ref, w_ref, d_ref, dx_ref, dw_ref):
        j = pl.program_id(0)
        q_scale = jnp.where(j < DN_HEADS, DN_DIM ** -0.5, 1.0).astype(F32)
        x = x_ref[...]
        w = w_ref[...]
        _, vjp = jax.vjp(lambda c: _dn_post(c, j >= 2 * DN_HEADS, q_scale), _conv(x, w))
        (dc,) = vjp(d_ref[...])
        dx = w[DN_CONV - 1:DN_CONV] * dc
        dw_ref[DN_CONV - 1:DN_CONV, :] = jnp.sum(dc * x, axis=0, keepdims=True)
        for s in range(1, DN_CONV):
            dx = dx + w[DN_CONV - 1 - s:DN_CONV - s] * _shift_up(dc, s)
            dw_ref[DN_CONV - 1 - s:DN_CONV - s, :] = jnp.sum(dc * _shift_down(x, s), axis=0, keepdims=True)
        dx_ref[...] = dx.astype(BF16)

    return pl.pallas_call(
        body, grid=(nb,),
        in_specs=[pl.BlockSpec((S, LANES), lambda j: (0, P_QKV // LANES + j)),
                  pl.BlockSpec((DN_CONV, LANES), lambda j: (0, j)),
                  pl.BlockSpec((S, LANES), lambda j: (0, j))],
        out_specs=[pl.BlockSpec((S, LANES), lambda j: (0, j)), pl.BlockSpec((DN_CONV, LANES), lambda j: (0, j))],
        out_shape=[jax.ShapeDtypeStruct((S, DN_QKV), BF16), jax.ShapeDtypeStruct((DN_CONV, DN_QKV), F32)],
        name="dn_conv_bwd", compiler_params=_params(("parallel",)))(proj, conv_w, dqkvn)


def _expanders():
    eb = np.zeros((LANES, DN_WIDTH), np.float32)
    ea = np.zeros((LANES, DN_WIDTH), np.float32)
    for h in range(DN_HEADS):
        eb[h, h * DN_DIM:(h + 1) * DN_DIM] = 1.0
        ea[DN_HEADS + h, h * DN_DIM:(h + 1) * DN_DIM] = 1.0
    return jnp.asarray(eb), jnp.asarray(ea)


def _dn_gate_args(a_log, dt_bias):
    eb, ea = _expanders()
    alog = jnp.repeat(a_log.reshape(1, DN_HEADS), DN_DIM, axis=1)
    dtb = jnp.repeat(dt_bias.reshape(1, DN_HEADS), DN_DIM, axis=1)
    return eb, ea, alog, dtb


def _dn_gate_specs(tm):
    return [_row(tm, LANES, P_BA // LANES), _full((LANES, DN_WIDTH)), _full((LANES, DN_WIDTH)),
            _full((1, DN_WIDTH)), _full((1, DN_WIDTH))]


def _dn_gate_fn(ba, eb, ea, alog, dtb):
    beta = jax.nn.sigmoid(_dot(ba, eb, hi=True))
    g = -jnp.exp(alog) * jax.nn.softplus(_dot(ba, ea, hi=True) + dtb)
    return beta, g


def _dn_gate_fwd(proj, a_log, dt_bias, tm=512):
    S = proj.shape[0]
    args = _dn_gate_args(a_log, dt_bias)

    def body(ba_ref, eb_ref, ea_ref, al_ref, dt_ref, beta_ref, g_ref):
        beta, g = _dn_gate_fn(ba_ref[...], eb_ref[...], ea_ref[...], al_ref[...], dt_ref[...])
        beta_ref[...] = beta
        g_ref[...] = g

    return pl.pallas_call(
        body, grid=(S // tm,), in_specs=_dn_gate_specs(tm), out_specs=[_row(tm, DN_WIDTH), _row(tm, DN_WIDTH)],
        out_shape=[jax.ShapeDtypeStruct((S, DN_WIDTH), F32), jax.ShapeDtypeStruct((S, DN_WIDTH), F32)],
        name="dn_gate_fwd", compiler_params=_params(("parallel",)))(proj, *args)


def _dn_gate_bwd(proj, a_log, dt_bias, dbeta, dg, tm=512):
    S = proj.shape[0]
    args = _dn_gate_args(a_log, dt_bias)

    def body(ba_ref, eb_ref, ea_ref, al_ref, dt_ref, dbeta_ref, dg_ref, dba_ref, dal_ref, ddt_ref):
        eb, ea = eb_ref[...], ea_ref[...]
        _, vjp = jax.vjp(lambda ba, al, dt: _dn_gate_fn(ba, eb, ea, al, dt), ba_ref[...], al_ref[...], dt_ref[...])
        dba, dal, ddt = vjp((dbeta_ref[...], dg_ref[...]))
        dba_ref[...] = dba.astype(BF16)

        @pl.when(pl.program_id(0) == 0)
        def _():
            dal_ref[...] = jnp.zeros_like(dal_ref)
            ddt_ref[...] = jnp.zeros_like(ddt_ref)

        dal_ref[...] += dal
        ddt_ref[...] += ddt

    return pl.pallas_call(
        body, grid=(S // tm,), in_specs=_dn_gate_specs(tm) + [_row(tm, DN_WIDTH), _row(tm, DN_WIDTH)],
        out_specs=[_row(tm, LANES), _full((1, DN_WIDTH)), _full((1, DN_WIDTH))],
        out_shape=[jax.ShapeDtypeStruct((S, LANES), BF16), jax.ShapeDtypeStruct((1, DN_WIDTH), F32),
                   jax.ShapeDtypeStruct((1, DN_WIDTH), F32)],
        name="dn_gate_bwd", compiler_params=_params(("arbitrary",)))(proj, *args, dbeta, dg)


PREP_GROUPS = 2
PREP_CHUNKS = GROUP * PREP_GROUPS


def _dn_prep_specs():
    rows = PREP_CHUNKS * CHUNK
    q = pl.BlockSpec((rows, LANES), lambda h, c: (c, h))
    k = pl.BlockSpec((rows, LANES), lambda h, c: (c, DN_HEADS + h))
    v = pl.BlockSpec((rows, LANES), lambda h, c: (c, 2 * DN_HEADS + h))
    qk = pl.BlockSpec((1, rows, CHUNK), lambda h, c: (h, c, 0))
    egl = pl.BlockSpec((1, PREP_CHUNKS, 1, LANES), lambda h, c: (h, c, 0, 0))
    return q, k, v, qk, egl


def _dn_prep_fwd(qkvn, g, beta):
    S = qkvn.shape[0]
    nc = S // CHUNK
    q, k, v, qks, egl = _dn_prep_specs()

    def body(q_ref, k_ref, v_ref, g_ref, b_ref, u_ref, w_ref, qe_ref, kd_ref, qk_ref, egl_ref):
        for gi in range(PREP_GROUPS):
            r = pl.ds(gi * GROUP_ROWS, GROUP_ROWS)
            u, w, qe, kd = _dn_group(q_ref[r, :], k_ref[r, :], v_ref[r, :], g_ref[r, :], b_ref[r, :])
            u_ref[r, :] = u
            w_ref[r, :] = w
            qe_ref[r, :] = qe
            kd_ref[r, :] = kd
        for i in range(PREP_CHUNKS):
            r = pl.ds(i * CHUNK, CHUNK)
            qk, e = _dn_chunk(q_ref[r, :], k_ref[r, :], g_ref[r, :])
            qk_ref[0, r, :] = qk
            egl_ref[0, i] = e

    wide = jax.ShapeDtypeStruct((S, DN_WIDTH), F32)
    return pl.pallas_call(
        body, grid=(DN_HEADS, nc // PREP_CHUNKS), in_specs=[q, k, v, q, q],
        out_specs=[q, q, q, q, qks, egl],
        out_shape=[wide, wide, wide, wide, jax.ShapeDtypeStruct((DN_HEADS, S, CHUNK), F32),
                   jax.ShapeDtypeStruct((DN_HEADS, nc, 1, LANES), F32)],
        name="dn_prep_fwd", compiler_params=_params(("parallel", "parallel")))(qkvn, qkvn, qkvn, g, beta)


def _dn_prep_bwd(qkvn, g, beta, du, dw, dqe, dkd, dqk, degl):
    S = qkvn.shape[0]
    nc = S // CHUNK
    q, k, v, qks, egl = _dn_prep_specs()

    def body(q_ref, k_ref, v_ref, g_ref, b_ref, du_ref, dw_ref, dqe_ref, dkd_ref, dqk_ref, degl_ref,
             dq_ref, dk_ref, dv_ref, dg_ref, db_ref):
        for gi in range(PREP_GROUPS):
            r = pl.ds(gi * GROUP_ROWS, GROUP_ROWS)
            _, vjp = jax.vjp(_dn_group, q_ref[r, :], k_ref[r, :], v_ref[r, :], g_ref[r, :], b_ref[r, :])
            dq, dk, dv, dg, db = vjp((du_ref[r, :], dw_ref[r, :], dqe_ref[r, :], dkd_ref[r, :]))
            dq_ref[r, :] = dq
            dk_ref[r, :] = dk
            dv_ref[r, :] = dv
            dg_ref[r, :] = dg
            db_ref[r, :] = db
        for i in range(PREP_CHUNKS):
            r = pl.ds(i * CHUNK, CHUNK)
            _, vjp = jax.vjp(_dn_chunk, q_ref[r, :], k_ref[r, :], g_ref[r, :])
            dq, dk, dg = vjp((dqk_ref[0, r, :], degl_ref[0, i]))
            dq_ref[r, :] += dq
            dk_ref[r, :] += dk
            dg_ref[r, :] += dg

    wide = jax.ShapeDtypeStruct((S, DN_WIDTH), F32)
    return pl.pallas_call(
        body, grid=(DN_HEADS, nc // PREP_CHUNKS), in_specs=[q, k, v, q, q, q, q, q, q, qks, egl],
        out_specs=[q] * 5, out_shape=[wide] * 5,
        name="dn_prep_bwd", compiler_params=_params(("parallel", "parallel")),
    )(qkvn, qkvn, qkvn, g, beta, du, dw, dqe, dkd, dqk, degl)


def _dn_scan_specs(nc, reverse):
    def cidx(c):
        return nc - 1 - c if reverse else c

    hc = pl.BlockSpec((CHUNK, DN_WIDTH), lambda c: (cidx(c), 0))
    qk = pl.BlockSpec((DN_HEADS, CHUNK, CHUNK), lambda c: (0, cidx(c), 0))
    egl = pl.BlockSpec((DN_HEADS, 1, 1, LANES), lambda c: (0, cidx(c), 0, 0))
    st = pl.BlockSpec((DN_HEADS, 1, DN_DIM, DN_DIM), lambda c: (0, cidx(c), 0, 0))
    return hc, qk, egl, st


def _dn_scan_fwd(u, w, qe, kd, qk, egl):
    S = u.shape[0]
    nc = S // CHUNK
    hc, qks, egls, st = _dn_scan_specs(nc, False)

    def body(u_ref, w_ref, qe_ref, kd_ref, qk_ref, egl_ref, o_ref, st_ref, s_scr):
        @pl.when(pl.program_id(0) == 0)
        def _():
            s_scr[...] = jnp.zeros_like(s_scr)

        for h in range(DN_HEADS):
            cols = pl.ds(h * DN_DIM, DN_DIM)
            s = s_scr[h]
            st_ref[h, 0] = s
            s_new, o = _dn_step(s, u_ref[:, cols], w_ref[:, cols], qe_ref[:, cols], kd_ref[:, cols], qk_ref[h],
                                egl_ref[h, 0])
            o_ref[:, cols] = o
            s_scr[h] = s_new

    return pl.pallas_call(
        body, grid=(nc,), in_specs=[hc, hc, hc, hc, qks, egls], out_specs=[hc, st],
        out_shape=[jax.ShapeDtypeStruct((S, DN_WIDTH), F32), jax.ShapeDtypeStruct((DN_HEADS, nc, DN_DIM, DN_DIM), F32)],
        scratch_shapes=[pltpu.VMEM((DN_HEADS, DN_DIM, DN_DIM), F32)], name="dn_scan_fwd",
        compiler_params=_params(("arbitrary",)))(u, w, qe, kd, qk, egl)


def _dn_scan_bwd(u, w, qe, kd, qk, egl, states, do):
    S = u.shape[0]
    nc = S // CHUNK
    hc, qks, egls, st = _dn_scan_specs(nc, True)

    def body(u_ref, w_ref, qe_ref, kd_ref, qk_ref, egl_ref, st_ref, do_ref,
             du_ref, dw_ref, dqe_ref, dkd_ref, dqk_ref, degl_ref, ds_scr):
        @pl.when(pl.program_id(0) == 0)
        def _():
            ds_scr[...] = jnp.zeros_like(ds_scr)

        for h in range(DN_HEADS):
            cols = pl.ds(h * DN_DIM, DN_DIM)
            _, vjp = jax.vjp(_dn_step, st_ref[h, 0], u_ref[:, cols], w_ref[:, cols], qe_ref[:, cols],
                             kd_ref[:, cols], qk_ref[h], egl_ref[h, 0])
            ds, du, dw, dqe, dkd, dqk, degl = vjp((ds_scr[h], do_ref[:, cols]))
            ds_scr[h] = ds
            du_ref[:, cols] = du
            dw_ref[:, cols] = dw
            dqe_ref[:, cols] = dqe
            dkd_ref[:, cols] = dkd
            dqk_ref[h] = dqk
            degl_ref[h, 0] = degl

    wide = jax.ShapeDtypeStruct((S, DN_WIDTH), F32)
    return pl.pallas_call(
        body, grid=(nc,), in_specs=[hc, hc, hc, hc, qks, egls, st, hc],
        out_specs=[hc, hc, hc, hc, qks, egls],
        out_shape=[wide, wide, wide, wide, jax.ShapeDtypeStruct((DN_HEADS, S, CHUNK), F32),
                   jax.ShapeDtypeStruct((DN_HEADS, nc, 1, LANES), F32)],
        scratch_shapes=[pltpu.VMEM((DN_HEADS, DN_DIM, DN_DIM), F32)], name="dn_scan_bwd",
        compiler_params=_params(("arbitrary",)))(u, w, qe, kd, qk, egl, states, do)


def _dn_out_fwd(o, proj, gain, tm=512):
    S = o.shape[0]

    def body(o_ref, z_ref, g_ref, y_ref):
        y_ref[...] = _dn_out(o_ref[...], z_ref[...], g_ref[...]).astype(BF16)

    hs = pl.BlockSpec((tm, LANES), lambda i, h: (i, h))
    zs = pl.BlockSpec((tm, LANES), lambda i, h: (i, P_Z // LANES + h))
    return pl.pallas_call(
        body, grid=(S // tm, DN_HEADS), in_specs=[hs, zs, _full((1, DN_DIM))], out_specs=hs,
        out_shape=jax.ShapeDtypeStruct((S, DN_WIDTH), BF16), name="dn_out_fwd",
        compiler_params=_params(("parallel", "parallel")))(o, proj, gain)


def _dn_out_bwd(o, proj, gain, dy, tm=512):
    S = o.shape[0]

    def body(o_ref, z_ref, g_ref, dy_ref, do_ref, dz_ref, dg_ref):
        _, vjp = jax.vjp(_dn_out, o_ref[...], z_ref[...], g_ref[...])
        do, dz, dg = vjp(dy_ref[...])
        do_ref[...] = do
        dz_ref[...] = dz.astype(BF16)

        @pl.when((pl.program_id(0) == 0) & (pl.program_id(1) == 0))
        def _():
            dg_ref[...] = jnp.zeros_like(dg_ref)

        dg_ref[...] += dg

    hs = pl.BlockSpec((tm, LANES), lambda i, h: (i, h))
    zs = pl.BlockSpec((tm, LANES), lambda i, h: (i, P_Z // LANES + h))
    return pl.pallas_call(
        body, grid=(S // tm, DN_HEADS), in_specs=[hs, zs, _full((1, DN_DIM)), hs],
        out_specs=[hs, hs, _full((1, DN_DIM))],
        out_shape=[jax.ShapeDtypeStruct((S, DN_WIDTH), F32), jax.ShapeDtypeStruct((S, DN_WIDTH), BF16),
                   jax.ShapeDtypeStruct((1, DN_DIM), F32)],
        name="dn_out_bwd", compiler_params=_params(("arbitrary", "arbitrary")))(o, proj, gain, dy)


def _rel_buckets():
    qi = np.arange(BLOCK)[:, None]
    kj = np.arange(2 * BLOCK)[None, :]
    n = np.maximum(BLOCK + qi - kj, 0)
    max_exact = REL_BUCKETS // 2
    nf = np.maximum(n, 1).astype(np.float32)
    large = max_exact + (np.log(nf / np.float32(max_exact)) / np.float32(math.log(REL_MAX_DIST / max_exact))
                         * np.float32(REL_BUCKETS - max_exact)).astype(np.int32)
    large = np.minimum(large, REL_BUCKETS - 1)
    return np.where(n < max_exact, n, large).astype(np.int32)


def _bias_fwd(rel_bias):
    buckets = jnp.asarray(_rel_buckets())

    def body(rb_ref, bk_ref, o_ref):
        bk = bk_ref[...]
        for h in range(SWA_HEADS):
            acc = jnp.zeros((BLOCK, 2 * BLOCK), F32)
            for b in range(REL_BUCKETS):
                acc = jnp.where(bk == b, rb_ref[b, h], acc)
            o_ref[h] = acc

    return pl.pallas_call(
        body, in_specs=[pl.BlockSpec(memory_space=pltpu.SMEM), pl.BlockSpec(memory_space=pltpu.VMEM)],
        out_specs=pl.BlockSpec(memory_space=pltpu.VMEM),
        out_shape=jax.ShapeDtypeStruct((SWA_HEADS, BLOCK, 2 * BLOCK), F32), name="swa_bias_fwd",
        compiler_params=_params())(rel_bias, buckets)


def _bias_bwd(dbias):
    buckets = jnp.asarray(_rel_buckets())

    def body(d_ref, bk_ref, o_ref):
        bk = bk_ref[...]
        lane = lax.broadcasted_iota(jnp.int32, (1, LANES), 1)
        for h in range(SWA_HEADS):
            d = d_ref[h]
            row = jnp.zeros((1, LANES), F32)
            for b in range(REL_BUCKETS):
                part = jnp.sum(jnp.where(bk == b, d, 0.0), axis=1, keepdims=True)
                row = jnp.where(lane == b, jnp.sum(part, axis=0, keepdims=True), row)
            o_ref[h:h + 1, :] = row

    return pl.pallas_call(
        body, in_specs=[pl.BlockSpec(memory_space=pltpu.VMEM), pl.BlockSpec(memory_space=pltpu.VMEM)],
        out_specs=pl.BlockSpec(memory_space=pltpu.VMEM),
        out_shape=jax.ShapeDtypeStruct((SWA_HEADS, LANES), F32), name="swa_bias_bwd",
        compiler_params=_params())(dbias, buckets)


def _swa_mask(n):
    qi = lax.broadcasted_iota(jnp.int32, (BLOCK, 2 * BLOCK), 0)
    kj = lax.broadcasted_iota(jnp.int32, (BLOCK, 2 * BLOCK), 1)
    dist = BLOCK + qi - kj
    return (dist >= 0) & (dist < WINDOW) & ((n > 0) | (kj >= BLOCK))


def _swa_in_specs():
    q = pl.BlockSpec((BLOCK, SWA_WIDTH), lambda n: (n, P_SQ // SWA_WIDTH))
    kc = pl.BlockSpec((BLOCK, SWA_KVW), lambda n: (n, P_SK // SWA_KVW))
    kp = pl.BlockSpec((BLOCK, SWA_KVW), lambda n: (jnp.maximum(n - 1, 0), P_SK // SWA_KVW))
    vc = pl.BlockSpec((BLOCK, SWA_KVW), lambda n: (n, P_SV // SWA_KVW))
    vp = pl.BlockSpec((BLOCK, SWA_KVW), lambda n: (jnp.maximum(n - 1, 0), P_SV // SWA_KVW))
    small = [_full((1, SWA_DIM)), _full((1, SWA_DIM)), _full((1, SWA_HEADS)),
             _full((SWA_HEADS, BLOCK, 2 * BLOCK))]
    return [q, kp, kc, vp, vc] + small


def _swa_load(kv, q_ref, kp_ref, kc_ref, vp_ref, vc_ref, s_ref, bias_ref):
    cols = pl.ds(kv * SWA_DIM, SWA_DIM)
    heads = [kv * SWA_GROUP + g for g in range(SWA_GROUP)]
    qs = [q_ref[:, pl.ds(h * SWA_DIM, SWA_DIM)] for h in heads]
    kband = jnp.concatenate([kp_ref[:, cols], kc_ref[:, cols]], axis=0)
    vband = jnp.concatenate([vp_ref[:, cols], vc_ref[:, cols]], axis=0)
    sinks = [s_ref[:, pl.ds(h, 1)] for h in heads]
    biases = [bias_ref[h] for h in heads]
    return heads, qs, kband, vband, sinks, biases


def _swa_fwd(proj, q_gain, k_gain, sinks, bias):
    S = proj.shape[0]

    def body(q_ref, kp_ref, kc_ref, vp_ref, vc_ref, qg_ref, kg_ref, s_ref, bias_ref, y_ref):
        mask = _swa_mask(pl.program_id(0))
        for kv in range(SWA_KV):
            heads, qs, kband, vband, sk, bs = _swa_load(kv, q_ref, kp_ref, kc_ref, vp_ref, vc_ref, s_ref, bias_ref)
            outs = _swa_heads(qs, kband, vband, qg_ref[...], kg_ref[...], sk, bs, mask)
            for h, o in zip(heads, outs):
                y_ref[:, pl.ds(h * SWA_DIM, SWA_DIM)] = o.astype(BF16)

    return pl.pallas_call(
        body, grid=(S // BLOCK,), in_specs=_swa_in_specs(),
        out_specs=pl.BlockSpec((BLOCK, SWA_WIDTH), lambda n: (n, 0)),
        out_shape=jax.ShapeDtypeStruct((S, SWA_WIDTH), BF16), name="swa_fwd",
        compiler_params=_params(("parallel",)))(proj, proj, proj, proj, proj, q_gain, k_gain, sinks, bias)


def _swa_bwd(proj, q_gain, k_gain, sinks, bias, dy):
    S = proj.shape[0]

    def body(q_ref, kp_ref, kc_ref, vp_ref, vc_ref, qg_ref, kg_ref, s_ref, bias_ref, dy_ref,
             dq_ref, dk_ref, dv_ref, dqg_ref, dkg_ref, ds_ref, dbias_ref):
        n = pl.program_id(0)
        mask = _swa_mask(n)

        @pl.when(n == 0)
        def _():
            for r in (dk_ref, dv_ref, dqg_ref, dkg_ref, ds_ref, dbias_ref):
                r[...] = jnp.zeros_like(r)

        cur = pl.ds(pl.multiple_of(n * BLOCK, BLOCK), BLOCK)
        prev = pl.ds(pl.multiple_of(jnp.maximum(n - 1, 0) * BLOCK, BLOCK), BLOCK)
        for kv in range(SWA_KV):
            heads, qs, kband, vband, sk, bs = _swa_load(kv, q_ref, kp_ref, kc_ref, vp_ref, vc_ref, s_ref, bias_ref)
            _, vjp = jax.vjp(lambda qs, kb, vb, qg, kg, sk, bs: _swa_heads(qs, kb, vb, qg, kg, sk, bs, mask),
                             qs, kband, vband, qg_ref[...], kg_ref[...], sk, bs)
            dqs, dkb, dvb, dqg, dkg, dsk, dbs = vjp([dy_ref[:, pl.ds(h * SWA_DIM, SWA_DIM)] for h in heads])
            cols = pl.ds(kv * SWA_DIM, SWA_DIM)
            for h, dq, dsink, db in zip(heads, dqs, dsk, dbs):
                dq_ref[:, pl.ds(h * SWA_DIM, SWA_DIM)] = dq.astype(BF16)
                ds_ref[:, pl.ds(h, 1)] += dsink
                dbias_ref[h] += db
            dqg_ref[...] += dqg
            dkg_ref[...] += dkg
            dk_ref[cur, cols] += dkb[BLOCK:]
            dv_ref[cur, cols] += dvb[BLOCK:]

            @pl.when(n > 0)
            def _():
                dk_ref[prev, cols] += dkb[:BLOCK]
                dv_ref[prev, cols] += dvb[:BLOCK]

    return pl.pallas_call(
        body, grid=(S // BLOCK,),
        in_specs=_swa_in_specs() + [pl.BlockSpec((BLOCK, SWA_WIDTH), lambda n: (n, 0))],
        out_specs=[pl.BlockSpec((BLOCK, SWA_WIDTH), lambda n: (n, 0)), _full((S, SWA_KVW)), _full((S, SWA_KVW)),
                   _full((1, SWA_DIM)), _full((1, SWA_DIM)), _full((1, SWA_HEADS)),
                   _full((SWA_HEADS, BLOCK, 2 * BLOCK))],
        out_shape=[jax.ShapeDtypeStruct((S, SWA_WIDTH), BF16), jax.ShapeDtypeStruct((S, SWA_KVW), F32),
                   jax.ShapeDtypeStruct((S, SWA_KVW), F32), jax.ShapeDtypeStruct((1, SWA_DIM), F32),
                   jax.ShapeDtypeStruct((1, SWA_DIM), F32), jax.ShapeDtypeStruct((1, SWA_HEADS), F32),
                   jax.ShapeDtypeStruct((SWA_HEADS, BLOCK, 2 * BLOCK), F32)],
        name="swa_bwd", compiler_params=_params(("arbitrary",)),
    )(proj, proj, proj, proj, proj, q_gain, k_gain, sinks, bias, dy)


def _position():
    return lax.axis_index("x"), lax.axis_index("y"), lax.axis_index("c")


def _all_gather(shards):
    na = len(shards)

    def body(*refs):
        x_refs, out_refs = refs[:na], refs[na:2 * na]
        send_sems, recv_sems, local_sems = refs[2 * na:]
        x, y, c = _position()
        me, sibling = (x, y, c), (x, y, 1 - c)
        chips = [(1 - x, y), (x, 1 - y), (1 - x, 1 - y)]

        def copy(a, k, block, to, own=False):
            px, py, pc = block
            slot = out_refs[a].at[4 * px + 2 * py + pc]
            return pltpu.make_async_remote_copy(
                src_ref=x_refs[a] if own else slot, dst_ref=slot, send_sem=send_sems.at[7 * a + k],
                recv_sem=recv_sems.at[7 * a + k], device_id=to, device_id_type=MESH_ID)

        mine = [pltpu.make_async_copy(x_refs[a], out_refs[a].at[4 * x + 2 * y + c], local_sems.at[a])
                for a in range(na)]
        for cp in mine:
            cp.start()
        first = []
        for a in range(na):
            first.append(copy(a, 0, me, sibling, own=True))
            first += [copy(a, 1 + j, me, (*chip, c), own=True) for j, chip in enumerate(chips)]
        for cp in first:
            cp.start()
        passed = []
        for j, chip in enumerate(chips):
            for a in range(na):
                copy(a, 1 + j, (*chip, c), me).wait_recv()
                passed.append(copy(a, 4 + j, (*chip, c), sibling))
                passed[-1].start()
        for a in range(na):
            copy(a, 0, sibling, me).wait_recv()
            for j, chip in enumerate(chips):
                copy(a, 4 + j, (*chip, 1 - c), me).wait_recv()
        for cp in first + passed:
            cp.wait_send()
        for cp in mine:
            cp.wait()

    return pl.pallas_call(
        body, in_specs=[pl.BlockSpec(memory_space=pl.ANY)] * na, out_specs=[pl.BlockSpec(memory_space=pl.ANY)] * na,
        out_shape=[jax.ShapeDtypeStruct((N_DEV,) + s.shape, s.dtype) for s in shards],
        scratch_shapes=[pltpu.SemaphoreType.DMA((7 * na,)), pltpu.SemaphoreType.DMA((7 * na,)),
                        pltpu.SemaphoreType.DMA((na,))],
        name="all_gather_weights")(*shards)


def _exchange(blocks):
    na = len(blocks)

    def body(*refs):
        in_refs, out_refs = refs[:na], refs[na:2 * na]
        send_sems, recv_sems, local_sems = refs[2 * na:]
        x, y, c = _position()
        me = 4 * x + 2 * y + c
        local = [pltpu.make_async_copy(in_refs[a].at[me], out_refs[a].at[me], local_sems.at[a]) for a in range(na)]
        for cp in local:
            cp.start()
        sends, recvs = [], []
        for k in range(1, N_DEV):
            px, py, pc = x ^ (k >> 2), y ^ ((k >> 1) & 1), c ^ (k & 1)
            peer = 4 * px + 2 * py + pc
            for a in range(na):
                sems = dict(send_sem=send_sems.at[na * (k - 1) + a], recv_sem=recv_sems.at[na * (k - 1) + a],
                            device_id=(px, py, pc), device_id_type=MESH_ID)
                sends.append(pltpu.make_async_remote_copy(src_ref=in_refs[a].at[peer], dst_ref=out_refs[a].at[me], **sems))
                recvs.append(pltpu.make_async_remote_copy(src_ref=in_refs[a].at[me], dst_ref=out_refs[a].at[peer], **sems))
        for cp in sends:
            cp.start()
        for cp in recvs:
            cp.wait_recv()
        for cp in sends:
            cp.wait_send()
        for cp in local:
            cp.wait()

    return pl.pallas_call(
        body, in_specs=[pl.BlockSpec(memory_space=pl.ANY)] * na, out_specs=[pl.BlockSpec(memory_space=pl.ANY)] * na,
        out_shape=[jax.ShapeDtypeStruct(b.shape, b.dtype) for b in blocks],
        scratch_shapes=[pltpu.SemaphoreType.DMA((7 * na,)), pltpu.SemaphoreType.DMA((7 * na,)),
                        pltpu.SemaphoreType.DMA((na,))],
        name="exchange_grads")(*blocks)


_HBM = pl.BlockSpec(memory_space=pltpu.HBM)
_SEM = pl.BlockSpec(memory_space=pltpu.SEMAPHORE)
_DATAFLOW = pltpu.SideEffectType.DATAFLOW_SIDE_EFFECTING


def _peers(x, y, c):
    out = []
    for k in range(1, N_DEV):
        px, py, pc = x ^ (k >> 2), y ^ ((k >> 1) & 1), c ^ (k & 1)
        out.append(((px, py, pc), 4 * px + 2 * py + pc))
    return out


def _split_copies(src_refs, land_refs, send_sems, recv_sems, scatter):
    x, y, c = _position()
    me = 4 * x + 2 * y + c
    sends, recvs = [], []
    for k, (peer_id, peer) in enumerate(_peers(x, y, c)):
        for a, (src, land) in enumerate(zip(src_refs, land_refs)):
            sems = dict(send_sem=send_sems.at[7 * a + k], recv_sem=recv_sems.at[7 * a + k],
                        device_id=peer_id, device_id_type=MESH_ID)
            mine = src.at[peer] if scatter else src
            sends.append(pltpu.make_async_remote_copy(src_ref=mine, dst_ref=land.at[me], **sems))
            recvs.append(pltpu.make_async_remote_copy(src_ref=mine, dst_ref=land.at[peer], **sems))
    return sends, recvs


def _exchange_start(srcs, scatter, name):
    na = len(srcs)
    lands = [lax.empty(s.shape if scatter else (N_DEV,) + s.shape, s.dtype) for s in srcs]

    def body(*refs):
        src_refs, land_refs = refs[:na], refs[na:2 * na]
        send_sems, recv_sems = refs[2 * na], refs[2 * na + 1]
        token = refs[-1]
        sends, _ = _split_copies(src_refs, land_refs, send_sems, recv_sems, scatter)
        for cp in sends:
            cp.start()
        token[...] = jnp.zeros_like(token)

    hbm = lambda a: pltpu.HBM(a.shape, a.dtype)
    out = pl.pallas_call(
        body, name=name,
        out_shape=(pltpu.SemaphoreType.DMA((7 * na,)), pltpu.SemaphoreType.DMA((7 * na,)),
                   *[hbm(s) for s in srcs], *[hbm(l) for l in lands], jax.ShapeDtypeStruct((8, LANES), F32)),
        in_specs=[_HBM] * (2 * na),
        out_specs=(_SEM, _SEM, *[_HBM] * (2 * na), pl.BlockSpec(memory_space=pltpu.VMEM)),
        input_output_aliases={i: 2 + i for i in range(2 * na)},
        compiler_params=pltpu.CompilerParams(has_side_effects=_DATAFLOW),
    )(*[pltpu.with_memory_space_constraint(s, pltpu.HBM) for s in srcs],
      *[pltpu.with_memory_space_constraint(l, pltpu.HBM) for l in lands])
    return (out[0], out[1], list(out[2:2 + na]), list(out[2 + na:2 + 2 * na])), out[-1]


def _exchange_wait(handle, after, scatter, name):
    send_sems, recv_sems, srcs, lands = handle
    na = len(srcs)

    def body(*refs):
        src_refs, land_refs = refs[:na], refs[na:2 * na]
        s_sems, r_sems = refs[2 * na], refs[2 * na + 1]
        sends, recvs = _split_copies(src_refs, land_refs, s_sems, r_sems, scatter)
        for cp in sends:
            cp.wait_send()
        for cp in recvs:
            cp.wait_recv()

    hbm = lambda a: pltpu.HBM(a.shape, a.dtype)
    out = pl.pallas_call(
        body, name=name, out_shape=(*[hbm(s) for s in srcs], *[hbm(l) for l in lands]),
        in_specs=[_HBM] * (2 * na) + [_SEM, _SEM, pl.BlockSpec(memory_space=pl.ANY)],
        out_specs=tuple([_HBM] * (2 * na)), input_output_aliases={i: i for i in range(2 * na)},
        compiler_params=pltpu.CompilerParams(has_side_effects=_DATAFLOW),
    )(*srcs, *lands, send_sems, recv_sems, after)
    return list(out[:na]), list(out[na:])


def _own_slot(landed, own):
    me = 4 * lax.axis_index("x") + 2 * lax.axis_index("y") + lax.axis_index("c")
    return lax.dynamic_update_slice_in_dim(landed, own[None], me, axis=0)


def _adam_update(parts, w, m, v, name, tr=256):
    r, c = w.shape
    tr = _pick_rows(r, tr)
    cp = parts.shape[2]

    def body(p_ref, w_ref, m_ref, v_ref, g_ref, d_ref, nm_ref, nv_ref):
        g = p_ref[0, :, pl.ds(0, c)].astype(F32)
        for i in range(1, N_DEV):
            g = g + p_ref[i, :, pl.ds(0, c)].astype(F32)
        delta, nm, nv = _adamw(w_ref[...], g, m_ref[...], v_ref[...])
        g_ref[...] = g
        d_ref[...] = delta
        nm_ref[...] = nm
        nv_ref[...] = nv

    rs = pl.BlockSpec((tr, c), lambda i: (i, 0))
    return pl.pallas_call(
        body, grid=(r // tr,), in_specs=[pl.BlockSpec((N_DEV, tr, cp), lambda i: (0, i, 0)), rs, rs, rs],
        out_specs=[rs] * 4, out_shape=[jax.ShapeDtypeStruct((r, c), F32)] * 4, name=name,
        compiler_params=_params(("parallel",)))(parts, w, m, v)


def _pick_rows(rows, target):
    if rows <= target:
        return rows
    t = target
    while t >= 16:
        if rows % t == 0:
            return t
        t -= 16
    return rows


BIG = ("w_in", "w_branch_dn", "w_branch_swa", "w_out", "w_gate", "w_up", "w_down")
IN_SHARD, IN_WIRE = D_IN // N_DEV, 640
FF_SHARD, FF_WIRE = D_FF // N_DEV, 384
D_FFP = N_DEV * FF_WIRE
BIG_SHAPES = {"w_in": ((D_MODEL, IN_SHARD), (D_MODEL, IN_WIRE)),
              "w_branch_dn": ((DN_WIDTH, LANES), (DN_WIDTH, LANES)),
              "w_branch_swa": ((SWA_WIDTH, LANES), (SWA_WIDTH, LANES)),
              "w_out": ((LANES, D_MODEL), (LANES, D_MODEL)),
              "w_gate": ((D_MODEL, FF_SHARD), (D_MODEL, FF_WIRE)),
              "w_up": ((D_MODEL, FF_SHARD), (D_MODEL, FF_WIRE)),
              "w_down": ((FF_SHARD, D_MODEL), (FF_WIRE, D_MODEL))}
CONV_SHARD, CONV_WIRE = (DN_CONV, DN_QKV // N_DEV), (8, 256)


def _pad_to(a, shape):
    return jnp.pad(a, [(0, t - s) for s, t in zip(a.shape, shape)])


_IN_SEGS = ((R_GATE, 2048, P_GATE), (R_QKV, DN_QKV, P_QKV), (R_Z, DN_WIDTH, P_Z), (R_SQ, SWA_WIDTH, P_SQ),
            (R_SK, SWA_KVW, P_SK), (R_SV, SWA_KVW, P_SV), (R_B, 8, P_BA))


def _w_in_from_blocks(blocks):
    parts = []
    for rs, n, _ in _IN_SEGS:
        for dev in range(N_DEV):
            lo, hi = max(rs, IN_SHARD * dev), min(rs + n, IN_SHARD * (dev + 1))
            if lo < hi:
                parts.append(blocks[dev, :, lo - IN_SHARD * dev:hi - IN_SHARD * dev])
    parts.append(jnp.zeros((blocks.shape[1], P_WIDTH - P_BA - 8), blocks.dtype))
    return jnp.concatenate(parts, axis=1)


def _w_in_to_blocks(g):
    out = []
    for dev in range(N_DEV):
        parts = []
        for rs, n, ps in sorted(_IN_SEGS):
            lo, hi = max(rs, IN_SHARD * dev), min(rs + n, IN_SHARD * (dev + 1))
            if lo < hi:
                parts.append(g[:, ps + lo - rs:ps + hi - rs])
        parts.append(jnp.zeros((g.shape[0], IN_WIRE - IN_SHARD), g.dtype))
        out.append(jnp.concatenate(parts, axis=1))
    return jnp.stack(out)


def _pack(flat_parts, rows):
    flat = jnp.concatenate(flat_parts, axis=-1)
    pad = rows * LANES - flat.shape[-1]
    flat = jnp.pad(flat, [(0, 0)] * (flat.ndim - 1) + [(0, pad)])
    return flat.reshape(flat.shape[:-1] + (rows, LANES))


def _unpack(buf, shapes):
    lead = buf.shape[:-2]
    flat = buf.reshape(lead + (-1,))
    out, off = [], 0
    for shp in shapes:
        n = int(np.prod(shp))
        out.append(flat[..., off:off + n].reshape(lead + tuple(shp)))
        off += n
    return out


def _cols_join(blocks):
    return jnp.concatenate([blocks[d] for d in range(N_DEV)], axis=1)


def _cols_split(full):
    c = full.shape[1] // N_DEV
    return jnp.stack([full[:, d * c:(d + 1) * c] for d in range(N_DEV)])


SMALL = ("attn_norm", "ffn_norm", "rel_bias", "dn_out_norm", "swa_q_norm", "swa_k_norm", "dn_a_log",
         "dn_dt_bias", "swa_sinks")
SMALL_ROWS = 24
CONV_ROWS = DN_CONV * DN_QKV // LANES


def _pack_small(d):
    return _pack([d[n].reshape(-1) for n in SMALL], SMALL_ROWS)


def kernel(x, attn_norm, w_in, dn_conv, dn_a_log, dn_dt_bias, dn_out_norm, swa_q_norm, swa_k_norm, swa_sinks, rel_bias, w_branch_dn, w_branch_swa, w_out, ffn_norm, w_gate, w_up, w_down, loss_target, m_attn_norm, m_w_in, m_dn_conv, m_dn_a_log, m_dn_dt_bias, m_dn_out_norm, m_swa_q_norm, m_swa_k_norm, m_swa_sinks, m_rel_bias, m_w_branch_dn, m_w_branch_swa, m_w_out, m_ffn_norm, m_w_gate, m_w_up, m_w_down, v_attn_norm, v_w_in, v_dn_conv, v_dn_a_log, v_dn_dt_bias, v_dn_out_norm, v_swa_q_norm, v_swa_k_norm, v_swa_sinks, v_rel_bias, v_w_branch_dn, v_w_branch_swa, v_w_out, v_ffn_norm, v_w_gate, v_w_up, v_w_down):
    args = dict(locals())
    S = x.shape[1]
    xs = x.reshape(S, D_MODEL)
    target = loss_target.reshape(S, D_MODEL)

    w_loc = {n: args[n].reshape(BIG_SHAPES[n][0]) for n in BIG}
    conv_loc = dn_conv.reshape(CONV_SHARD)
    wire = {n: _pad_to(w_loc[n], BIG_SHAPES[n][1]).astype(BF16) for n in BIG}
    first = _all_gather([wire["w_in"], _pad_to(conv_loc, CONV_WIRE)])
    later = [n for n in BIG if n != "w_in"]
    rest_handle, rest_token = _exchange_start([wire[n] for n in later], False, "gather_rest_start")
    w_pad = _w_in_from_blocks(first[0])
    conv_w = jnp.concatenate([first[1][d, :DN_CONV, :CONV_SHARD[1]] for d in range(N_DEV)], axis=1)

    h = _norm_fwd(xs, attn_norm + rest_token[0, 0], "norm1_fwd")
    proj = _mm([(h, w_pad)], "nn", F32, "mm_in", 512, 1664, j_outer=True)
    qkvn = _dn_conv_fwd(proj, conv_w)
    beta, g = _dn_gate_fwd(proj, dn_a_log, dn_dt_bias)
    u, w, qe, kd, qk, egl = _dn_prep_fwd(qkvn, g, beta)
    o, states = _dn_scan_fwd(u, w, qe, kd, qk, egl)
    y_dn = _dn_out_fwd(o, proj, dn_out_norm)
    bias = _bias_fwd(rel_bias)
    y_swa = _swa_fwd(proj, swa_q_norm, swa_k_norm, swa_sinks, bias)
    rest_src, rest_land = _exchange_wait(rest_handle, y_swa, False, "gather_rest_wait")
    G = {n: _own_slot(land, src) for n, src, land in zip(later, rest_src, rest_land)}
    w_bdn = _cols_join(G["w_branch_dn"])
    w_bswa = _cols_join(G["w_branch_swa"])
    w_o = G["w_out"].reshape(D_MODEL, D_MODEL)
    w_d = G["w_down"].reshape(D_FFP, D_MODEL)
    w_g = _cols_join(G["w_gate"])
    w_u = _cols_join(G["w_up"])
    a_dn = _mm([(y_dn, w_bdn)], "nn", F32, "mm_branch_dn", 1024, D_MODEL)
    a_swa = _mm([(y_swa, w_bswa)], "nn", F32, "mm_branch_swa", 1024, D_MODEL)
    merged = _merge_fwd(proj, a_dn, a_swa)
    t_out = _mm([(merged, w_o)], "nn", F32, "mm_out", 1024, D_MODEL)
    x1, h2 = _resid_norm_fwd(xs, t_out, ffn_norm, "norm2_fwd")
    gate = _mm([(h2, w_g)], "nn", F32, "mm_gate", 1024, 1024, j_outer=True)
    up = _mm([(h2, w_u)], "nn", F32, "mm_up", 1024, 1024, j_outer=True)
    act = _act_fwd(gate, up)
    f = _mm([(act, w_d)], "nn", F32, "mm_down", 512, D_MODEL)
    dy, dy_b, loss_local = _loss_fwd_bwd(x1, f, target)

    dact = _mm([(dy_b, w_d)], "nt", F32, "mm_dact", 1024, 1024, j_outer=True)
    g_w_down = _mm([(act, dy_b)], "tn", BF16, "mm_dw_down", 768, D_MODEL, j_outer=True)
    g_w_down = g_w_down.reshape(N_DEV, FF_WIRE, D_MODEL)
    dgate, dup = _act_bwd(gate, up, dact)
    dh2 = _mm([(dgate, w_g), (dup, w_u)], "nt", F32, "mm_dh2", 512, 512, j_outer=True)
    g_w_gate = _cols_split(_mm([(h2, dgate)], "tn", BF16, "mm_dw_gate", D_MODEL, 768))
    g_w_up = _cols_split(_mm([(h2, dup)], "tn", BF16, "mm_dw_up", D_MODEL, 768))
    ffn_handle, ffn_token = _exchange_start([g_w_down, g_w_gate, g_w_up], True, "scatter_ffn_start")
    dx1, dx1_b, g_ffn_norm = _norm_bwd(x1, [dh2], dy, ffn_norm + ffn_token[0, 0], "norm2_bwd")
    dmerged = _mm([(dx1_b, w_o)], "nt", F32, "mm_dmerged", 1024, D_MODEL)
    g_w_out = _mm([(merged, dx1_b)], "tn", BF16, "mm_dw_out", 512, D_MODEL, j_outer=True)
    g_w_out = g_w_out.reshape(N_DEV, LANES, D_MODEL)
    dg0, dg1, da_dn, da_swa = _merge_bwd(proj, a_dn, a_swa, dmerged)
    dy_dn = _mm([(da_dn, w_bdn)], "nt", F32, "mm_dy_dn", 1024, DN_WIDTH)
    dy_swa = _mm([(da_swa, w_bswa)], "nt", F32, "mm_dy_swa", 1024, SWA_WIDTH)
    g_w_bdn = _cols_split(_mm([(y_dn, da_dn)], "tn", BF16, "mm_dw_branch_dn", DN_WIDTH, 512))
    g_w_bswa = _cols_split(_mm([(y_swa, da_swa)], "tn", BF16, "mm_dw_branch_swa", SWA_WIDTH, 512))
    dsq, dsk, dsv, g_q_norm, g_k_norm, g_sinks, dbias = _swa_bwd(proj, swa_q_norm, swa_k_norm, swa_sinks, bias, dy_swa)
    g_rel_bias = _bias_bwd(dbias)[:, :REL_BUCKETS].T
    mix_handle, mix_token = _exchange_start([g_w_out, g_w_bdn, g_w_bswa], True, "scatter_mix_start")
    do, dz, g_out_norm = _dn_out_bwd(o, proj, dn_out_norm + mix_token[0, 0], dy_dn)
    du, dw, dqe, dkd, dqk, degl = _dn_scan_bwd(u, w, qe, kd, qk, egl, states, do)
    dq, dk, dv, dgd, dbeta = _dn_prep_bwd(qkvn, g, beta, du, dw, dqe, dkd, dqk, degl)
    dqkvn = jnp.concatenate([dq, dk, dv], axis=1)
    dba, dal, ddt = _dn_gate_bwd(proj, dn_a_log, dn_dt_bias, dbeta, dgd)
    g_a_log = dal.reshape(DN_HEADS, DN_DIM).sum(axis=1)
    g_dt_bias = ddt.reshape(DN_HEADS, DN_DIM).sum(axis=1)
    dqkv, g_conv = _dn_conv_bwd(proj, conv_w, dqkvn)
    dproj = jnp.concatenate([dg0, dg1, dqkv, dz, dsq, dsk.astype(BF16), dsv.astype(BF16), dba], axis=1)
    dh = _mm([(dproj, w_pad)], "nt", F32, "mm_dh", 512, D_MODEL)
    g_w_in = _w_in_to_blocks(_mm([(h, dproj)], "tn", BF16, "mm_dw_in", 512, 1664, j_outer=True))
    dx, _, g_attn_norm = _norm_bwd(xs, [dh], dx1, attn_norm, "norm1_bwd")

    g_small = {"attn_norm": g_attn_norm, "ffn_norm": g_ffn_norm, "rel_bias": g_rel_bias, "dn_out_norm": g_out_norm,
               "swa_q_norm": g_q_norm, "swa_k_norm": g_k_norm, "dn_a_log": g_a_log, "dn_dt_bias": g_dt_bias,
               "swa_sinks": g_sinks}
    small_rows = jnp.concatenate([_pack_small(g_small), g_conv.reshape(CONV_ROWS, LANES)], axis=0)
    send_small = jnp.broadcast_to(small_rows[None], (N_DEV,) + small_rows.shape)
    parts_in, recv_small = _exchange([g_w_in, send_small])
    me = 4 * lax.axis_index("x") + 2 * lax.axis_index("y") + lax.axis_index("c")
    received = {"w_in": parts_in}
    for handle, group, name in ((ffn_handle, ("w_down", "w_gate", "w_up"), "scatter_ffn_wait"),
                                (mix_handle, ("w_out", "w_branch_dn", "w_branch_swa"), "scatter_mix_wait")):
        srcs, lands = _exchange_wait(handle, recv_small, True, name)
        for n, src, land in zip(group, srcs, lands):
            received[n] = _own_slot(land, lax.dynamic_index_in_dim(src, me, 0, keepdims=False))

    outs = {}
    for n in BIG:
        shp = BIG_SHAPES[n][0]
        outs[n] = _adam_update(received[n], w_loc[n], args["m_" + n].reshape(shp), args["v_" + n].reshape(shp),
                               "adam_" + n)
    conv_parts = lax.dynamic_slice_in_dim(recv_small[:, SMALL_ROWS:].reshape(N_DEV, DN_CONV, DN_QKV),
                                          me * CONV_SHARD[1], CONV_SHARD[1], axis=2)
    outs["dn_conv"] = _adam_update(conv_parts, conv_loc, m_dn_conv.reshape(CONV_SHARD), v_dn_conv.reshape(CONV_SHARD),
                                   "adam_dn_conv")
    small_out = _adam_update(recv_small[:, :SMALL_ROWS], _pack_small({n: args[n] for n in SMALL}),
                             _pack_small({n: args["m_" + n] for n in SMALL}),
                             _pack_small({n: args["v_" + n] for n in SMALL}), "adam_small")

    names = ("attn_norm", "w_in", "dn_conv", "dn_a_log", "dn_dt_bias", "dn_out_norm", "swa_q_norm", "swa_k_norm",
             "swa_sinks", "rel_bias", "w_branch_dn", "w_branch_swa", "w_out", "ffn_norm", "w_gate", "w_up", "w_down")
    results = []
    for kind in range(4):
        small = dict(zip(SMALL, _unpack(small_out[kind], [args[n].shape for n in SMALL])))
        results += [outs[n][kind].reshape(args[n].shape) if n in outs else small[n] for n in names]

    loss = lax.psum(loss_local[0, 0], ("x", "y", "c"))
    return (loss, dx.reshape(x.shape), *results)
```

```python
import math

import numpy as np
import jax
import jax.numpy as jnp
from jax import lax
from jax.experimental import pallas as pl
from jax.experimental.pallas import tpu as pltpu

F32 = jnp.float32
BF16 = jnp.bfloat16
HI = lax.Precision.HIGHEST

D_MODEL = 1024
DN_HEADS = 4
DN_DIM = 128
DN_WIDTH = 512
DN_QKV = 1536
DN_CONV = 4
CHUNK = 64
SWA_HEADS = 8
SWA_KV = 2
SWA_GROUP = 4
SWA_DIM = 64
SWA_WIDTH = 512
SWA_KVW = 128
WINDOW = 128
BLOCK = 128
REL_BUCKETS = 32
REL_MAX_DIST = 128
D_FF = 2816
D_IN = 4872
EPS = 1e-6
N_DEV = 8

ADAM_LR = 0.001
ADAM_B1 = 0.9
ADAM_B2 = 0.999
ADAM_EPS = 1e-08
ADAM_WD = 0.01
ADAM_STEP = 10

P_GATE, P_QKV, P_Z, P_SQ, P_SK, P_SV, P_BA = 0, 2048, 3584, 4096, 4608, 4736, 4864
P_WIDTH = 4992
R_QKV, R_Z, R_B, R_A, R_SQ, R_SK, R_SV, R_GATE = 0, 1536, 2048, 2052, 2056, 2568, 2696, 2824

VMEM_LIMIT = 56 * 1024 * 1024
LANES = 128
MESH_ID = pl.DeviceIdType.MESH


def _params(sem=None):
    return pltpu.CompilerParams(dimension_semantics=sem, vmem_limit_bytes=VMEM_LIMIT)


def _pick(dim, target):
    if dim <= target:
        return dim
    t = target - target % LANES
    while t >= LANES:
        if dim % t == 0:
            return t
        t -= LANES
    return dim


_DIMS = {"nn": (((1,), (0,)), ((), ())), "nt": (((1,), (1,)), ((), ())), "tn": (((0,), (0,)), ((), ()))}


def _mm(pairs, mode, out_dtype, name, bm, bn, j_outer=False):
    a0, b0 = pairs[0]
    if mode == "nn":
        (M, K), (K2, N) = a0.shape, b0.shape
    elif mode == "nt":
        (M, K), (N, K2) = a0.shape, b0.shape
    else:
        (K, M), (K2, N) = a0.shape, b0.shape
    bm, bn = min(bm, M), min(bn, N)
    assert K == K2 and M % bm == 0 and N % bn == 0, (name, a0.shape, b0.shape, bm, bn)
    dims = _DIMS[mode]
    n = len(pairs)

    def body(*refs):
        o_ref = refs[2 * n]
        acc = None
        for t in range(n):
            p = lax.dot_general(refs[2 * t][...].astype(BF16), refs[2 * t + 1][...].astype(BF16), dims,
                                preferred_element_type=F32)
            acc = p if acc is None else acc + p
        o_ref[...] = acc.astype(out_dtype)

    def ij(f):
        return (lambda j, i: f(i, j)) if j_outer else f

    a_spec = pl.BlockSpec((K, bm), ij(lambda i, j: (0, i))) if mode == "tn" else pl.BlockSpec((bm, K), ij(lambda i, j: (i, 0)))
    b_spec = pl.BlockSpec((bn, K), ij(lambda i, j: (j, 0))) if mode == "nt" else pl.BlockSpec((K, bn), ij(lambda i, j: (0, j)))
    grid = (N // bn, M // bm) if j_outer else (M // bm, N // bn)
    return pl.pallas_call(
        body, grid=grid, in_specs=[a_spec, b_spec] * n, out_specs=pl.BlockSpec((bm, bn), ij(lambda i, j: (i, j))),
        out_shape=jax.ShapeDtypeStruct((M, N), out_dtype), name=name,
        compiler_params=_params(("parallel", "parallel")),
    )(*[x for pair in pairs for x in pair])


def _rms(x, gain):
    return x * lax.rsqrt(jnp.mean(x * x, axis=-1, keepdims=True) + EPS) * gain


def _silu(x):
    return x * jax.nn.sigmoid(x)


def _act(g, u):
    return _silu(g) * u


def _merge(g0, g1, a_dn, a_swa):
    return jax.nn.sigmoid(g0) * a_dn + jax.nn.sigmoid(g1) * a_swa


def _dn_post(c, is_v, q_scale):
    a = _silu(c)
    rs = lax.rsqrt(jnp.sum(a * a, axis=-1, keepdims=True) + EPS) * q_scale
    return a * jnp.where(is_v, 1.0, rs)


def _dn_out(o, z, gain):
    return _rms(o, gain) * _silu(z)


def _dot(a, b, dims=_DIMS["nn"], hi=False):
    if a.ndim == 3 or b.ndim == 3:
        batch = a.shape[0] if a.ndim == 3 else b.shape[0]
        a = a if a.ndim == 3 else jnp.broadcast_to(a, (batch,) + a.shape)
        b = b if b.ndim == 3 else jnp.broadcast_to(b, (batch,) + b.shape)
        ((ca,), (cb,)), _ = dims
        dims = (((ca + 1,), (cb + 1,)), ((0,), (0,)))
    if hi:
        return lax.dot_general(a, b, dims, precision=HI, preferred_element_type=F32)
    return lax.dot_general(a.astype(BF16), b.astype(BF16), dims, preferred_element_type=F32)


def _pieces(x):
    hi = x.astype(BF16)
    r1 = x - hi.astype(F32)
    mid = r1.astype(BF16)
    return hi, mid, (r1 - mid.astype(F32)).astype(BF16)


def _sel_left_impl(m, x):
    mb = m.astype(BF16)
    hi, mid, lo = _pieces(x)
    return _dot(mb, hi) + (_dot(mb, mid) + _dot(mb, lo))


@jax.custom_vjp
def _sel_left(m, mt, x):
    return _sel_left_impl(m, x)


_sel_left.defvjp(lambda m, mt, x: (_sel_left_impl(m, x), (m, mt)),
                 lambda res, ct: (jnp.zeros_like(res[0]), jnp.zeros_like(res[1]), _sel_left_impl(res[1], ct)))


def _sel_right_impl(x, s):
    sb = s.astype(BF16)
    hi, mid, lo = _pieces(x)
    return _dot(hi, sb) + (_dot(mid, sb) + _dot(lo, sb))


@jax.custom_vjp
def _sel_right(x, s, st):
    return _sel_right_impl(x, s)


_sel_right.defvjp(lambda x, s, st: (_sel_right_impl(x, s), (s, st)),
                  lambda res, ct: (_sel_right_impl(ct, res[1]), jnp.zeros_like(res[0]), jnp.zeros_like(res[1])))


def _sel_nt_impl(s, x):
    sb = s.astype(BF16)
    hi, mid, lo = _pieces(x)
    return _dot(sb, hi, _DIMS["nt"]) + (_dot(sb, mid, _DIMS["nt"]) + _dot(sb, lo, _DIMS["nt"]))


def _sel_tn_impl(x, s):
    sb = s.astype(BF16)
    hi, mid, lo = _pieces(x)
    return _dot(hi, sb, _DIMS["tn"]) + (_dot(mid, sb, _DIMS["tn"]) + _dot(lo, sb, _DIMS["tn"]))


@jax.custom_vjp
def _sel_nt(s, x):
    return _sel_nt_impl(s, x)


_sel_nt.defvjp(lambda s, x: (_sel_nt_impl(s, x), s),
               lambda s, ct: (jnp.zeros_like(s), _sel_tn_impl(ct, s)))


def _dot3_impl(a, b):
    a_hi, a_lo, _ = _pieces(a)
    b_hi, b_lo, _ = _pieces(b)
    return _dot(a_hi, b_hi) + (_dot(a_hi, b_lo) + _dot(a_lo, b_hi))


@jax.custom_vjp
def _dot3(a, b):
    return _dot3_impl(a, b)


_dot3.defvjp(lambda a, b: (_dot3_impl(a, b), (a, b)),
             lambda res, ct: (_dot(ct, res[1], _DIMS["nt"]), _dot(res[0], ct, _DIMS["tn"])))


def _inv_impl(a, eye, strict):
    t = eye - a
    p = _dot(a, a)
    for level in range(5):
        t = t + _dot(t, p)
        if level < 4:
            p = _dot(p, p)
    t = t + _dot3_impl(t, eye - _dot3_impl(eye + a, t))
    return jnp.where(strict > 0.5, t, eye)


@jax.custom_vjp
def _inv_unit_lower(a, eye, strict):
    return _inv_impl(a, eye, strict)


def _inv_bwd(res, ct):
    t, eye, strict = res
    da = -_dot(_dot(t, ct, _DIMS["tn"]), t, _DIMS["nt"])
    return da, jnp.zeros_like(eye), jnp.zeros_like(strict)


def _inv_fwd(a, eye, strict):
    t = _inv_impl(a, eye, strict)
    return t, (t, eye, strict)


_inv_unit_lower.defvjp(_inv_fwd, _inv_bwd)

GROUP = 4
GROUP_ROWS = GROUP * CHUNK


def _block_consts(n):
    ii = lax.broadcasted_iota(jnp.int32, (n, n), 0)
    jj = lax.broadcasted_iota(jnp.int32, (n, n), 1)
    shift = CHUNK.bit_length() - 1
    same = jnp.right_shift(ii, shift) == jnp.right_shift(jj, shift)
    return same & (ii >= jj), same & (ii <= jj), same & (ii > jj), same, ii == jj


def _lane0(n):
    s = (lax.broadcasted_iota(jnp.int32, (LANES, n), 0) == 0).astype(F32)
    st = (lax.broadcasted_iota(jnp.int32, (n, LANES), 1) == 0).astype(F32)
    return s, st


def _dn_group(q, k, v, g, beta):
    n = GROUP_ROWS
    low_b, upp_b, strict_b, same_b, eye_b = _block_consts(n)
    low, upp, same, eye = low_b.astype(F32), upp_b.astype(F32), same_b.astype(F32), eye_b.astype(F32)
    s, st = _lane0(n)
    gc = _sel_left(low, upp, g)
    gl = _sel_left(same, same, g)
    col = _sel_right(gc, s, st)
    row = _sel_nt(st, gc)
    decay = jnp.exp(jnp.where(low_b, col - row, -jnp.inf))
    kb = k * beta
    vb = v * beta
    a = jnp.where(strict_b, _dot(kb, k, _DIMS["nt"]) * decay, 0.0)
    t = _inv_unit_lower(a, eye, strict_b.astype(F32))
    u = _dot3(t, vb)
    w = _dot3(t, kb * jnp.exp(gc))
    return u, w, q * jnp.exp(gc), k * jnp.exp(gl - gc)


def _dn_chunk(q, k, g):
    ii = lax.broadcasted_iota(jnp.int32, (CHUNK, CHUNK), 0)
    jj = lax.broadcasted_iota(jnp.int32, (CHUNK, CHUNK), 1)
    low = (ii >= jj).astype(F32)
    upp = (ii <= jj).astype(F32)
    s, st = _lane0(CHUNK)
    gc = _sel_left(low, upp, g)
    col = _sel_right(gc, s, st)
    row = _sel_nt(st, gc)
    decay = jnp.exp(jnp.where(ii >= jj, col - row, -jnp.inf))
    qk = _dot(q, k, _DIMS["nt"]) * decay
    return qk, jnp.exp(jnp.sum(g, axis=-2, keepdims=True))


def _dn_step(s, u, w, qe, kd, qk, egl):
    v_new = u - _dot(w, s)
    o = _dot(qe, s) + _dot(qk, v_new)
    s_new = s * egl + _dot(kd, v_new, _DIMS["tn"])
    return s_new, o


def _swa_heads(qs, kband, vband, qg, kg, sinks, biases, mask):
    kn = _rms(kband, kg)
    outs = []
    for q, sink, bias in zip(qs, sinks, biases):
        qn = _rms(q, qg)
        logits = _dot(qn, kn, _DIMS["nt"]) * (SWA_DIM ** -0.5)
        logits = jnp.where(mask, logits + bias, -jnp.inf)
        m = jnp.maximum(jnp.max(logits, axis=-1, keepdims=True), sink)
        p = jnp.exp(logits - m)
        denom = jnp.sum(p, axis=-1, keepdims=True) + jnp.exp(sink - m)
        outs.append(_dot(p / denom, vband))
    return outs


def _adamw(w, g, m, v):
    m = ADAM_B1 * m + (1.0 - ADAM_B1) * g
    v = ADAM_B2 * v + (1.0 - ADAM_B2) * jnp.square(g)
    m_hat = m / (1.0 - ADAM_B1 ** ADAM_STEP)
    v_hat = v / (1.0 - ADAM_B2 ** ADAM_STEP)
    delta = -ADAM_LR * (m_hat / (jnp.sqrt(v_hat) + ADAM_EPS) + ADAM_WD * w)
    return delta, m, v


def _row(tm, c, cb=0):
    return pl.BlockSpec((tm, c), lambda i, cb=cb: (i, cb))


def _full(shape):
    nd = len(shape)
    return pl.BlockSpec(shape, lambda *_, nd=nd: (0,) * nd)


def _norm_fwd(x, gain, name, tm=512):
    S = x.shape[0]

    def body(x_ref, g_ref, h_ref):
        h_ref[...] = _rms(x_ref[...], g_ref[...]).astype(BF16)

    return pl.pallas_call(
        body, grid=(S // tm,), in_specs=[_row(tm, D_MODEL), _full((1, D_MODEL))],
        out_specs=_row(tm, D_MODEL), out_shape=jax.ShapeDtypeStruct((S, D_MODEL), BF16),
        name=name, compiler_params=_params(("parallel",)))(x, gain)


def _resid_norm_fwd(x, t, gain, name, tm=512):
    S = x.shape[0]

    def body(x_ref, t_ref, g_ref, x1_ref, h_ref):
        x1 = x_ref[...] + t_ref[...]
        x1_ref[...] = x1
        h_ref[...] = _rms(x1, g_ref[...]).astype(BF16)

    return pl.pallas_call(
        body, grid=(S // tm,), in_specs=[_row(tm, D_MODEL), _row(tm, D_MODEL), _full((1, D_MODEL))],
        out_specs=[_row(tm, D_MODEL), _row(tm, D_MODEL)],
        out_shape=[jax.ShapeDtypeStruct((S, D_MODEL), F32), jax.ShapeDtypeStruct((S, D_MODEL), BF16)],
        name=name, compiler_params=_params(("parallel",)))(x, t, gain)


def _norm_bwd(x, dh_list, dres, gain, name, tm=256):
    S = x.shape[0]
    n = len(dh_list)

    def body(*refs):
        x_ref, g_ref, r_ref = refs[0], refs[1], refs[2]
        dh_refs = refs[3:3 + n]
        dx_ref, dxb_ref, dg_ref = refs[3 + n], refs[4 + n], refs[5 + n]
        dh = dh_refs[0][...].astype(F32)
        for r in dh_refs[1:]:
            dh = dh + r[...].astype(F32)
        _, vjp = jax.vjp(_rms, x_ref[...], g_ref[...])
        dx, dg = vjp(dh)
        dx = dx + r_ref[...]
        dx_ref[...] = dx
        dxb_ref[...] = dx.astype(BF16)

        @pl.when(pl.program_id(0) == 0)
        def _():
            dg_ref[...] = jnp.zeros_like(dg_ref)

        dg_ref[...] += dg

    return pl.pallas_call(
        body, grid=(S // tm,),
        in_specs=[_row(tm, D_MODEL), _full((1, D_MODEL)), _row(tm, D_MODEL)] + [_row(tm, D_MODEL)] * n,
        out_specs=[_row(tm, D_MODEL), _row(tm, D_MODEL), _full((1, D_MODEL))],
        out_shape=[jax.ShapeDtypeStruct((S, D_MODEL), F32), jax.ShapeDtypeStruct((S, D_MODEL), BF16),
                   jax.ShapeDtypeStruct((1, D_MODEL), F32)],
        name=name, compiler_params=_params(("arbitrary",)))(x, gain, dres, *dh_list)


def _loss_fwd_bwd(x1, f, target, tm=512):
    S = x1.shape[0]

    def body(x_ref, f_ref, t_ref, dy_ref, dyb_ref, l_ref):
        diff = x_ref[...] + f_ref[...] - t_ref[...]
        dy = diff * (1.0 / D_MODEL)
        dy_ref[...] = dy
        dyb_ref[...] = dy.astype(BF16)

        @pl.when(pl.program_id(0) == 0)
        def _():
            l_ref[...] = jnp.zeros_like(l_ref)

        l_ref[...] += jnp.sum(jnp.mean(diff * diff, axis=-1, keepdims=True), axis=0, keepdims=True) * 0.5

    return pl.pallas_call(
        body, grid=(S // tm,), in_specs=[_row(tm, D_MODEL)] * 3,
        out_specs=[_row(tm, D_MODEL), _row(tm, D_MODEL), _full((1, 1))],
        out_shape=[jax.ShapeDtypeStruct((S, D_MODEL), F32), jax.ShapeDtypeStruct((S, D_MODEL), BF16),
                   jax.ShapeDtypeStruct((1, 1), F32)],
        name="loss_fwd_bwd", compiler_params=_params(("arbitrary",)))(x1, f, target)


def _act_fwd(g, u, tm=256):
    S, width = g.shape

    def body(g_ref, u_ref, o_ref):
        o_ref[...] = _act(g_ref[...], u_ref[...]).astype(BF16)

    return pl.pallas_call(
        body, grid=(S // tm,), in_specs=[_row(tm, width)] * 2, out_specs=_row(tm, width),
        out_shape=jax.ShapeDtypeStruct((S, width), BF16), name="act_fwd",
        compiler_params=_params(("parallel",)))(g, u)


def _act_bwd(g, u, dact, tm=256):
    S, width = g.shape

    def body(g_ref, u_ref, d_ref, dg_ref, du_ref):
        _, vjp = jax.vjp(_act, g_ref[...], u_ref[...])
        dg, du = vjp(d_ref[...])
        dg_ref[...] = dg.astype(BF16)
        du_ref[...] = du.astype(BF16)

    return pl.pallas_call(
        body, grid=(S // tm,), in_specs=[_row(tm, width)] * 3, out_specs=[_row(tm, width)] * 2,
        out_shape=[jax.ShapeDtypeStruct((S, width), BF16)] * 2, name="act_bwd",
        compiler_params=_params(("parallel",)))(g, u, dact)


def _merge_fwd(proj, a_dn, a_swa, tm=512, tc=512):
    S = proj.shape[0]
    nc = D_MODEL // tc

    def spec(off):
        return pl.BlockSpec((tm, tc), lambda i, j, off=off: (i, off + j))

    def body(g0_ref, g1_ref, ad_ref, as_ref, o_ref):
        o_ref[...] = _merge(g0_ref[...], g1_ref[...], ad_ref[...], as_ref[...]).astype(BF16)

    return pl.pallas_call(
        body, grid=(S // tm, nc), in_specs=[spec(P_GATE // tc), spec(P_GATE // tc + nc), spec(0), spec(0)],
        out_specs=spec(0), out_shape=jax.ShapeDtypeStruct((S, D_MODEL), BF16), name="merge_fwd",
        compiler_params=_params(("parallel", "parallel")))(proj, proj, a_dn, a_swa)


def _merge_bwd(proj, a_dn, a_swa, dmerged, tm=512, tc=512):
    S = proj.shape[0]
    nc = D_MODEL // tc

    def spec(off):
        return pl.BlockSpec((tm, tc), lambda i, j, off=off: (i, off + j))

    def body(g0_ref, g1_ref, ad_ref, as_ref, d_ref, dg0_ref, dg1_ref, dad_ref, das_ref):
        _, vjp = jax.vjp(_merge, g0_ref[...], g1_ref[...], ad_ref[...], as_ref[...])
        dg0, dg1, dad, das = vjp(d_ref[...])
        dg0_ref[...] = dg0.astype(BF16)
        dg1_ref[...] = dg1.astype(BF16)
        dad_ref[...] = dad.astype(BF16)
        das_ref[...] = das.astype(BF16)

    return pl.pallas_call(
        body, grid=(S // tm, nc),
        in_specs=[spec(P_GATE // tc), spec(P_GATE // tc + nc), spec(0), spec(0), spec(0)],
        out_specs=[spec(0)] * 4, out_shape=[jax.ShapeDtypeStruct((S, D_MODEL), BF16)] * 4,
        name="merge_bwd", compiler_params=_params(("parallel", "parallel")))(proj, proj, a_dn, a_swa, dmerged)


def _shift_down(x, s):
    row = lax.broadcasted_iota(jnp.int32, x.shape, 0)
    return jnp.where(row >= s, pltpu.roll(x, s, axis=0), 0.0)


def _shift_up(x, s):
    n = x.shape[0]
    row = lax.broadcasted_iota(jnp.int32, x.shape, 0)
    return jnp.where(row < n - s, pltpu.roll(x, n - s, axis=0), 0.0)


def _conv(x, w):
    out = w[DN_CONV - 1:DN_CONV] * x
    for s in range(1, DN_CONV):
        out = out + w[DN_CONV - 1 - s:DN_CONV - s] * _shift_down(x, s)
    return out


def _dn_conv_fwd(proj, conv_w):
    S = proj.shape[0]
    nb = DN_QKV // LANES

    def body(x_ref, w_ref, o_ref):
        j = pl.program_id(0)
        q_scale = jnp.where(j < DN_HEADS, DN_DIM ** -0.5, 1.0).astype(F32)
        o_ref[...] = _dn_post(_conv(x_ref[...], w_ref[...]), j >= 2 * DN_HEADS, q_scale)

    return pl.pallas_call(
        body, grid=(nb,),
        in_specs=[pl.BlockSpec((S, LANES), lambda j: (0, P_QKV // LANES + j)),
                  pl.BlockSpec((DN_CONV, LANES), lambda j: (0, j))],
        out_specs=pl.BlockSpec((S, LANES), lambda j: (0, j)),
        out_shape=jax.ShapeDtypeStruct((S, DN_QKV), F32), name="dn_conv_fwd",
        compiler_params=_params(("parallel",)))(proj, conv_w)


def _dn_conv_bwd(proj, conv_w, dqkvn):
    S = proj.shape[0]
    nb = DN_QKV // LANES

    def body(x_ref, w_ref, d_ref, dx_ref, dw_ref):
        j = pl.program_id(0)
        q_scale = jnp.where(j < DN_HEADS, DN_DIM ** -0.5, 1.0).astype(F32)
        x = x_ref[...]
        w = w_ref[...]
        _, vjp = jax.vjp(lambda c: _dn_post(c, j >= 2 * DN_HEADS, q_scale), _conv(x, w))
        (dc,) = vjp(d_ref[...])
        dx = w[DN_CONV - 1:DN_CONV] * dc
        dw_ref[DN_CONV - 1:DN_CONV, :] = jnp.sum(dc * x, axis=0, keepdims=True)
        for s in range(1, DN_CONV):
            dx = dx + w[DN_CONV - 1 - s:DN_CONV - s] * _shift_up(dc, s)
            dw_ref[DN_CONV - 1 - s:DN_CONV - s, :] = jnp.sum(dc * _shift_down(x, s), axis=0, keepdims=True)
        dx_ref[...] = dx.astype(BF16)

    return pl.pallas_call(
        body, grid=(nb,),
        in_specs=[pl.BlockSpec((S, LANES), lambda j: (0, P_QKV // LANES + j)),
                  pl.BlockSpec((DN_CONV, LANES), lambda j: (0, j)),
                  pl.BlockSpec((S, LANES), lambda j: (0, j))],
        out_specs=[pl.BlockSpec((S, LANES), lambda j: (0, j)), pl.BlockSpec((DN_CONV, LANES), lambda j: (0, j))],
        out_shape=[jax.ShapeDtypeStruct((S, DN_QKV), BF16), jax.ShapeDtypeStruct((DN_CONV, DN_QKV), F32)],
        name="dn_conv_bwd", compiler_params=_params(("parallel",)))(proj, conv_w, dqkvn)


def _expanders():
    eb = np.zeros((LANES, DN_WIDTH), np.float32)
    ea = np.zeros((LANES, DN_WIDTH), np.float32)
    for h in range(DN_HEADS):
        eb[h, h * DN_DIM:(h + 1) * DN_DIM] = 1.0
        ea[DN_HEADS + h, h * DN_DIM:(h + 1) * DN_DIM] = 1.0
    return jnp.asarray(eb), jnp.asarray(ea)


def _dn_gate_args(a_log, dt_bias):
    eb, ea = _expanders()
    alog = jnp.repeat(a_log.reshape(1, DN_HEADS), DN_DIM, axis=1)
    dtb = jnp.repeat(dt_bias.reshape(1, DN_HEADS), DN_DIM, axis=1)
    return eb, ea, alog, dtb


def _dn_gate_specs(tm):
    return [_row(tm, LANES, P_BA // LANES), _full((LANES, DN_WIDTH)), _full((LANES, DN_WIDTH)),
            _full((1, DN_WIDTH)), _full((1, DN_WIDTH))]


def _dn_gate_fn(ba, eb, ea, alog, dtb):
    beta = jax.nn.sigmoid(_dot(ba, eb, hi=True))
    g = -jnp.exp(alog) * jax.nn.softplus(_dot(ba, ea, hi=True) + dtb)
    return beta, g


def _dn_gate_fwd(proj, a_log, dt_bias, tm=512):
    S = proj.shape[0]
    args = _dn_gate_args(a_log, dt_bias)

    def body(ba_ref, eb_ref, ea_ref, al_ref, dt_ref, beta_ref, g_ref):
        beta, g = _dn_gate_fn(ba_ref[...], eb_ref[...], ea_ref[...], al_ref[...], dt_ref[...])
        beta_ref[...] = beta
        g_ref[...] = g

    return pl.pallas_call(
        body, grid=(S // tm,), in_specs=_dn_gate_specs(tm), out_specs=[_row(tm, DN_WIDTH), _row(tm, DN_WIDTH)],
        out_shape=[jax.ShapeDtypeStruct((S, DN_WIDTH), F32), jax.ShapeDtypeStruct((S, DN_WIDTH), F32)],
        name="dn_gate_fwd", compiler_params=_params(("parallel",)))(proj, *args)


def _dn_gate_bwd(proj, a_log, dt_bias, dbeta, dg, tm=512):
    S = proj.shape[0]
    args = _dn_gate_args(a_log, dt_bias)

    def body(ba_ref, eb_ref, ea_ref, al_ref, dt_ref, dbeta_ref, dg_ref, dba_ref, dal_ref, ddt_ref):
        eb, ea = eb_ref[...], ea_ref[...]
        _, vjp = jax.vjp(lambda ba, al, dt: _dn_gate_fn(ba, eb, ea, al, dt), ba_ref[...], al_ref[...], dt_ref[...])
        dba, dal, ddt = vjp((dbeta_ref[...], dg_ref[...]))
        dba_ref[...] = dba.astype(BF16)

        @pl.when(pl.program_id(0) == 0)
        def _():
            dal_ref[...] = jnp.zeros_like(dal_ref)
            ddt_ref[...] = jnp.zeros_like(ddt_ref)

        dal_ref[...] += dal
        ddt_ref[...] += ddt

    return pl.pallas_call(
        body, grid=(S // tm,), in_specs=_dn_gate_specs(tm) + [_row(tm, DN_WIDTH), _row(tm, DN_WIDTH)],
        out_specs=[_row(tm, LANES), _full((1, DN_WIDTH)), _full((1, DN_WIDTH))],
        out_shape=[jax.ShapeDtypeStruct((S, LANES), BF16), jax.ShapeDtypeStruct((1, DN_WIDTH), F32),
                   jax.ShapeDtypeStruct((1, DN_WIDTH), F32)],
        name="dn_gate_bwd", compiler_params=_params(("arbitrary",)))(proj, *args, dbeta, dg)


PREP_GROUPS = 4
PREP_CHUNKS = GROUP * PREP_GROUPS


def _dn_prep_specs():
    rows = PREP_CHUNKS * CHUNK
    q = pl.BlockSpec((rows, LANES), lambda h, c: (c, h))
    k = pl.BlockSpec((rows, LANES), lambda h, c: (c, DN_HEADS + h))
    v = pl.BlockSpec((rows, LANES), lambda h, c: (c, 2 * DN_HEADS + h))
    qk = pl.BlockSpec((1, rows, CHUNK), lambda h, c: (h, c, 0))
    egl = pl.BlockSpec((1, PREP_CHUNKS, 1, LANES), lambda h, c: (h, c, 0, 0))
    return q, k, v, qk, egl


def _dn_prep_fwd(qkvn, g, beta):
    S = qkvn.shape[0]
    nc = S // CHUNK
    q, k, v, qks, egl = _dn_prep_specs()

    def body(q_ref, k_ref, v_ref, g_ref, b_ref, u_ref, w_ref, qe_ref, kd_ref, qk_ref, egl_ref):
        rows = PREP_CHUNKS * CHUNK
        grp = (PREP_GROUPS, GROUP_ROWS, LANES)
        chk = (PREP_CHUNKS, CHUNK, LANES)
        q, k, g = q_ref[...], k_ref[...], g_ref[...]
        u, w, qe, kd = _dn_group(q.reshape(grp), k.reshape(grp), v_ref[...].reshape(grp), g.reshape(grp),
                                 b_ref[...].reshape(grp))
        u_ref[...] = u.reshape(rows, LANES)
        w_ref[...] = w.reshape(rows, LANES)
        qe_ref[...] = qe.reshape(rows, LANES)
        kd_ref[...] = kd.reshape(rows, LANES)
        qk, e = _dn_chunk(q.reshape(chk), k.reshape(chk), g.reshape(chk))
        qk_ref[0] = qk.reshape(rows, CHUNK)
        egl_ref[0] = e

    wide = jax.ShapeDtypeStruct((S, DN_WIDTH), F32)
    return pl.pallas_call(
        body, grid=(DN_HEADS, nc // PREP_CHUNKS), in_specs=[q, k, v, q, q],
        out_specs=[q, q, q, q, qks, egl],
        out_shape=[wide, wide, wide, wide, jax.ShapeDtypeStruct((DN_HEADS, S, CHUNK), F32),
                   jax.ShapeDtypeStruct((DN_HEADS, nc, 1, LANES), F32)],
        name="dn_prep_fwd", compiler_params=_params(("parallel", "parallel")))(qkvn, qkvn, qkvn, g, beta)


def _dn_prep_bwd(qkvn, g, beta, du, dw, dqe, dkd, dqk, degl):
    S = qkvn.shape[0]
    nc = S // CHUNK
    q, k, v, qks, egl = _dn_prep_specs()

    def body(q_ref, k_ref, v_ref, g_ref, b_ref, du_ref, dw_ref, dqe_ref, dkd_ref, dqk_ref, degl_ref,
             dq_ref, dk_ref, dv_ref, dg_ref, db_ref):
        rows = PREP_CHUNKS * CHUNK
        grp = (PREP_GROUPS, GROUP_ROWS, LANES)
        chk = (PREP_CHUNKS, CHUNK, LANES)
        q, k, g = q_ref[...], k_ref[...], g_ref[...]
        _, vjp = jax.vjp(_dn_group, q.reshape(grp), k.reshape(grp), v_ref[...].reshape(grp), g.reshape(grp),
                         b_ref[...].reshape(grp))
        dq, dk, dv, dg, db = vjp((du_ref[...].reshape(grp), dw_ref[...].reshape(grp), dqe_ref[...].reshape(grp),
                                  dkd_ref[...].reshape(grp)))
        _, vjp = jax.vjp(_dn_chunk, q.reshape(chk), k.reshape(chk), g.reshape(chk))
        dq2, dk2, dg2 = vjp((dqk_ref[0].reshape(PREP_CHUNKS, CHUNK, CHUNK), degl_ref[0]))
        dq_ref[...] = dq.reshape(rows, LANES) + dq2.reshape(rows, LANES)
        dk_ref[...] = dk.reshape(rows, LANES) + dk2.reshape(rows, LANES)
        dg_ref[...] = dg.reshape(rows, LANES) + dg2.reshape(rows, LANES)
        dv_ref[...] = dv.reshape(rows, LANES)
        db_ref[...] = db.reshape(rows, LANES)

    wide = jax.ShapeDtypeStruct((S, DN_WIDTH), F32)
    return pl.pallas_call(
        body, grid=(DN_HEADS, nc // PREP_CHUNKS), in_specs=[q, k, v, q, q, q, q, q, q, qks, egl],
        out_specs=[q] * 5, out_shape=[wide] * 5,
        name="dn_prep_bwd", compiler_params=_params(("parallel", "parallel")),
    )(qkvn, qkvn, qkvn, g, beta, du, dw, dqe, dkd, dqk, degl)


def _dn_scan_specs(nc, reverse):
    def cidx(c):
        return nc - 1 - c if reverse else c

    hc = pl.BlockSpec((CHUNK, DN_WIDTH), lambda c: (cidx(c), 0))
    qk = pl.BlockSpec((DN_HEADS, CHUNK, CHUNK), lambda c: (0, cidx(c), 0))
    egl = pl.BlockSpec((DN_HEADS, 1, 1, LANES), lambda c: (0, cidx(c), 0, 0))
    st = pl.BlockSpec((DN_HEADS, 1, DN_DIM, DN_DIM), lambda c: (0, cidx(c), 0, 0))
    return hc, qk, egl, st


def _heads(ref):
    return jnp.stack([ref[:, pl.ds(h * DN_DIM, DN_DIM)] for h in range(DN_HEADS)])


def _dn_scan_fwd(u, w, qe, kd, qk, egl):
    S = u.shape[0]
    nc = S // CHUNK
    hc, qks, egls, st = _dn_scan_specs(nc, False)

    def body(u_ref, w_ref, qe_ref, kd_ref, qk_ref, egl_ref, o_ref, st_ref, s_scr):
        @pl.when(pl.program_id(0) == 0)
        def _():
            s_scr[...] = jnp.zeros_like(s_scr)

        s = s_scr[...]
        st_ref[:, 0] = s
        s_new, o = _dn_step(s, _heads(u_ref), _heads(w_ref), _heads(qe_ref), _heads(kd_ref), qk_ref[...],
                            egl_ref[:, 0])
        for h in range(DN_HEADS):
            o_ref[:, pl.ds(h * DN_DIM, DN_DIM)] = o[h]
        s_scr[...] = s_new

    return pl.pallas_call(
        body, grid=(nc,), in_specs=[hc, hc, hc, hc, qks, egls], out_specs=[hc, st],
        out_shape=[jax.ShapeDtypeStruct((S, DN_WIDTH), F32), jax.ShapeDtypeStruct((DN_HEADS, nc, DN_DIM, DN_DIM), F32)],
        scratch_shapes=[pltpu.VMEM((DN_HEADS, DN_DIM, DN_DIM), F32)], name="dn_scan_fwd",
        compiler_params=_params(("arbitrary",)))(u, w, qe, kd, qk, egl)


def _dn_scan_bwd(u, w, qe, kd, qk, egl, states, do):
    S = u.shape[0]
    nc = S // CHUNK
    hc, qks, egls, st = _dn_scan_specs(nc, True)

    def body(u_ref, w_ref, qe_ref, kd_ref, qk_ref, egl_ref, st_ref, do_ref,
             du_ref, dw_ref, dqe_ref, dkd_ref, dqk_ref, degl_ref, ds_scr):
        @pl.when(pl.program_id(0) == 0)
        def _():
            ds_scr[...] = jnp.zeros_like(ds_scr)

        _, vjp = jax.vjp(_dn_step, st_ref[:, 0], _heads(u_ref), _heads(w_ref), _heads(qe_ref), _heads(kd_ref),
                         qk_ref[...], egl_ref[:, 0])
        ds, du, dw, dqe, dkd, dqk, degl = vjp((ds_scr[...], _heads(do_ref)))
        ds_scr[...] = ds
        dqk_ref[...] = dqk
        degl_ref[:, 0] = degl
        for h in range(DN_HEADS):
            cols = pl.ds(h * DN_DIM, DN_DIM)
            du_ref[:, cols] = du[h]
            dw_ref[:, cols] = dw[h]
            dqe_ref[:, cols] = dqe[h]
            dkd_ref[:, cols] = dkd[h]

    wide = jax.ShapeDtypeStruct((S, DN_WIDTH), F32)
    return pl.pallas_call(
        body, grid=(nc,), in_specs=[hc, hc, hc, hc, qks, egls, st, hc],
        out_specs=[hc, hc, hc, hc, qks, egls],
        out_shape=[wide, wide, wide, wide, jax.ShapeDtypeStruct((DN_HEADS, S, CHUNK), F32),
                   jax.ShapeDtypeStruct((DN_HEADS, nc, 1, LANES), F32)],
        scratch_shapes=[pltpu.VMEM((DN_HEADS, DN_DIM, DN_DIM), F32)], name="dn_scan_bwd",
        compiler_params=_params(("arbitrary",)))(u, w, qe, kd, qk, egl, states, do)


def _dn_out_fwd(o, proj, gain, tm=512):
    S = o.shape[0]

    def body(o_ref, z_ref, g_ref, y_ref):
        y_ref[...] = _dn_out(o_ref[...], z_ref[...], g_ref[...]).astype(BF16)

    hs = pl.BlockSpec((tm, LANES), lambda i, h: (i, h))
    zs = pl.BlockSpec((tm, LANES), lambda i, h: (i, P_Z // LANES + h))
    return pl.pallas_call(
        body, grid=(S // tm, DN_HEADS), in_specs=[hs, zs, _full((1, DN_DIM))], out_specs=hs,
        out_shape=jax.ShapeDtypeStruct((S, DN_WIDTH), BF16), name="dn_out_fwd",
        compiler_params=_params(("parallel", "parallel")))(o, proj, gain)


def _dn_out_bwd(o, proj, gain, dy, tm=512):
    S = o.shape[0]

    def body(o_ref, z_ref, g_ref, dy_ref, do_ref, dz_ref, dg_ref):
        _, vjp = jax.vjp(_dn_out, o_ref[...], z_ref[...], g_ref[...])
        do, dz, dg = vjp(dy_ref[...])
        do_ref[...] = do
        dz_ref[...] = dz.astype(BF16)

        @pl.when((pl.program_id(0) == 0) & (pl.program_id(1) == 0))
        def _():
            dg_ref[...] = jnp.zeros_like(dg_ref)

        dg_ref[...] += dg

    hs = pl.BlockSpec((tm, LANES), lambda i, h: (i, h))
    zs = pl.BlockSpec((tm, LANES), lambda i, h: (i, P_Z // LANES + h))
    return pl.pallas_call(
        body, grid=(S // tm, DN_HEADS), in_specs=[hs, zs, _full((1, DN_DIM)), hs],
        out_specs=[hs, hs, _full((1, DN_DIM))],
        out_shape=[jax.ShapeDtypeStruct((S, DN_WIDTH), F32), jax.ShapeDtypeStruct((S, DN_WIDTH), BF16),
                   jax.ShapeDtypeStruct((1, DN_DIM), F32)],
        name="dn_out_bwd", compiler_params=_params(("arbitrary", "arbitrary")))(o, proj, gain, dy)


def _rel_buckets():
    qi = np.arange(BLOCK)[:, None]
    kj = np.arange(2 * BLOCK)[None, :]
    n = np.maximum(BLOCK + qi - kj, 0)
    max_exact = REL_BUCKETS // 2
    nf = np.maximum(n, 1).astype(np.float32)
    large = max_exact + (np.log(nf / np.float32(max_exact)) / np.float32(math.log(REL_MAX_DIST / max_exact))
                         * np.float32(REL_BUCKETS - max_exact)).astype(np.int32)
    large = np.minimum(large, REL_BUCKETS - 1)
    return np.where(n < max_exact, n, large).astype(np.int32)


def _bias_fwd(rel_bias):
    buckets = jnp.asarray(_rel_buckets())

    def body(rb_ref, bk_ref, o_ref):
        bk = bk_ref[...]
        for h in range(SWA_HEADS):
            acc = jnp.zeros((BLOCK, 2 * BLOCK), F32)
            for b in range(REL_BUCKETS):
                acc = jnp.where(bk == b, rb_ref[b, h], acc)
            o_ref[h] = acc

    return pl.pallas_call(
        body, in_specs=[pl.BlockSpec(memory_space=pltpu.SMEM), pl.BlockSpec(memory_space=pltpu.VMEM)],
        out_specs=pl.BlockSpec(memory_space=pltpu.VMEM),
        out_shape=jax.ShapeDtypeStruct((SWA_HEADS, BLOCK, 2 * BLOCK), F32), name="swa_bias_fwd",
        compiler_params=_params())(rel_bias, buckets)


def _bias_bwd(dbias):
    buckets = jnp.asarray(_rel_buckets())

    def body(d_ref, bk_ref, o_ref):
        bk = bk_ref[...]
        lane = lax.broadcasted_iota(jnp.int32, (1, LANES), 1)
        for h in range(SWA_HEADS):
            d = d_ref[h]
            row = jnp.zeros((1, LANES), F32)
            for b in range(REL_BUCKETS):
                part = jnp.sum(jnp.where(bk == b, d, 0.0), axis=1, keepdims=True)
                row = jnp.where(lane == b, jnp.sum(part, axis=0, keepdims=True), row)
            o_ref[h:h + 1, :] = row

    return pl.pallas_call(
        body, in_specs=[pl.BlockSpec(memory_space=pltpu.VMEM), pl.BlockSpec(memory_space=pltpu.VMEM)],
        out_specs=pl.BlockSpec(memory_space=pltpu.VMEM),
        out_shape=jax.ShapeDtypeStruct((SWA_HEADS, LANES), F32), name="swa_bias_bwd",
        compiler_params=_params())(dbias, buckets)


def _swa_mask(n):
    qi = lax.broadcasted_iota(jnp.int32, (BLOCK, 2 * BLOCK), 0)
    kj = lax.broadcasted_iota(jnp.int32, (BLOCK, 2 * BLOCK), 1)
    dist = BLOCK + qi - kj
    return (dist >= 0) & (dist < WINDOW) & ((n > 0) | (kj >= BLOCK))


def _swa_in_specs():
    q = pl.BlockSpec((BLOCK, SWA_WIDTH), lambda n: (n, P_SQ // SWA_WIDTH))
    kc = pl.BlockSpec((BLOCK, SWA_KVW), lambda n: (n, P_SK // SWA_KVW))
    kp = pl.BlockSpec((BLOCK, SWA_KVW), lambda n: (jnp.maximum(n - 1, 0), P_SK // SWA_KVW))
    vc = pl.BlockSpec((BLOCK, SWA_KVW), lambda n: (n, P_SV // SWA_KVW))
    vp = pl.BlockSpec((BLOCK, SWA_KVW), lambda n: (jnp.maximum(n - 1, 0), P_SV // SWA_KVW))
    small = [_full((1, SWA_DIM)), _full((1, SWA_DIM)), _full((1, SWA_HEADS)),
             _full((SWA_HEADS, BLOCK, 2 * BLOCK))]
    return [q, kp, kc, vp, vc] + small


def _swa_load(kv, q_ref, kp_ref, kc_ref, vp_ref, vc_ref, s_ref, bias_ref):
    cols = pl.ds(kv * SWA_DIM, SWA_DIM)
    heads = [kv * SWA_GROUP + g for g in range(SWA_GROUP)]
    qs = [q_ref[:, pl.ds(h * SWA_DIM, SWA_DIM)] for h in heads]
    kband = jnp.concatenate([kp_ref[:, cols], kc_ref[:, cols]], axis=0)
    vband = jnp.concatenate([vp_ref[:, cols], vc_ref[:, cols]], axis=0)
    sinks = [s_ref[:, pl.ds(h, 1)] for h in heads]
    biases = [bias_ref[h] for h in heads]
    return heads, qs, kband, vband, sinks, biases


def _swa_fwd(proj, q_gain, k_gain, sinks, bias):
    S = proj.shape[0]

    def body(q_ref, kp_ref, kc_ref, vp_ref, vc_ref, qg_ref, kg_ref, s_ref, bias_ref, y_ref):
        mask = _swa_mask(pl.program_id(0))
        for kv in range(SWA_KV):
            heads, qs, kband, vband, sk, bs = _swa_load(kv, q_ref, kp_ref, kc_ref, vp_ref, vc_ref, s_ref, bias_ref)
            outs = _swa_heads(qs, kband, vband, qg_ref[...], kg_ref[...], sk, bs, mask)
            for h, o in zip(heads, outs):
                y_ref[:, pl.ds(h * SWA_DIM, SWA_DIM)] = o.astype(BF16)

    return pl.pallas_call(
        body, grid=(S // BLOCK,), in_specs=_swa_in_specs(),
        out_specs=pl.BlockSpec((BLOCK, SWA_WIDTH), lambda n: (n, 0)),
        out_shape=jax.ShapeDtypeStruct((S, SWA_WIDTH), BF16), name="swa_fwd",
        compiler_params=_params(("parallel",)))(proj, proj, proj, proj, proj, q_gain, k_gain, sinks, bias)


def _swa_bwd(proj, q_gain, k_gain, sinks, bias, dy):
    S = proj.shape[0]

    def body(q_ref, kp_ref, kc_ref, vp_ref, vc_ref, qg_ref, kg_ref, s_ref, bias_ref, dy_ref,
             dq_ref, dk_ref, dv_ref, dqg_ref, dkg_ref, ds_ref, dbias_ref):
        n = pl.program_id(0)
        mask = _swa_mask(n)

        @pl.when(n == 0)
        def _():
            for r in (dk_ref, dv_ref, dqg_ref, dkg_ref, ds_ref, dbias_ref):
                r[...] = jnp.zeros_like(r)

        cur = pl.ds(pl.multiple_of(n * BLOCK, BLOCK), BLOCK)
        prev = pl.ds(pl.multiple_of(jnp.maximum(n - 1, 0) * BLOCK, BLOCK), BLOCK)
        for kv in range(SWA_KV):
            heads, qs, kband, vband, sk, bs = _swa_load(kv, q_ref, kp_ref, kc_ref, vp_ref, vc_ref, s_ref, bias_ref)
            _, vjp = jax.vjp(lambda qs, kb, vb, qg, kg, sk, bs: _swa_heads(qs, kb, vb, qg, kg, sk, bs, mask),
                             qs, kband, vband, qg_ref[...], kg_ref[...], sk, bs)
            dqs, dkb, dvb, dqg, dkg, dsk, dbs = vjp([dy_ref[:, pl.ds(h * SWA_DIM, SWA_DIM)] for h in heads])
            cols = pl.ds(kv * SWA_DIM, SWA_DIM)
            for h, dq, dsink, db in zip(heads, dqs, dsk, dbs):
                dq_ref[:, pl.ds(h * SWA_DIM, SWA_DIM)] = dq.astype(BF16)
                ds_ref[:, pl.ds(h, 1)] += dsink
                dbias_ref[h] += db
            dqg_ref[...] += dqg
            dkg_ref[...] += dkg
            dk_ref[cur, cols] += dkb[BLOCK:]
            dv_ref[cur, cols] += dvb[BLOCK:]

            @pl.when(n > 0)
            def _():
                dk_ref[prev, cols] += dkb[:BLOCK]
                dv_ref[prev, cols] += dvb[:BLOCK]

    return pl.pallas_call(
        body, grid=(S // BLOCK,),
        in_specs=_swa_in_specs() + [pl.BlockSpec((BLOCK, SWA_WIDTH), lambda n: (n, 0))],
        out_specs=[pl.BlockSpec((BLOCK, SWA_WIDTH), lambda n: (n, 0)), _full((S, SWA_KVW)), _full((S, SWA_KVW)),
                   _full((1, SWA_DIM)), _full((1, SWA_DIM)), _full((1, SWA_HEADS)),
                   _full((SWA_HEADS, BLOCK, 2 * BLOCK))],
        out_shape=[jax.ShapeDtypeStruct((S, SWA_WIDTH), BF16), jax.ShapeDtypeStruct((S, SWA_KVW), F32),
                   jax.ShapeDtypeStruct((S, SWA_KVW), F32), jax.ShapeDtypeStruct((1, SWA_DIM), F32),
                   jax.ShapeDtypeStruct((1, SWA_DIM), F32), jax.ShapeDtypeStruct((1, SWA_HEADS), F32),
                   jax.ShapeDtypeStruct((SWA_HEADS, BLOCK, 2 * BLOCK), F32)],
        name="swa_bwd", compiler_params=_params(("arbitrary",)),
    )(proj, proj, proj, proj, proj, q_gain, k_gain, sinks, bias, dy)


def _position():
    return lax.axis_index("x"), lax.axis_index("y"), lax.axis_index("c")


def _all_gather(shards):
    na = len(shards)

    def body(*refs):
        x_refs, out_refs = refs[:na], refs[na:2 * na]
        send_sems, recv_sems, local_sems = refs[2 * na:]
        x, y, c = _position()
        me, sibling = (x, y, c), (x, y, 1 - c)
        chips = [(1 - x, y), (x, 1 - y), (1 - x, 1 - y)]

        def copy(a, k, block, to, own=False):
            px, py, pc = block
            slot = out_refs[a].at[4 * px + 2 * py + pc]
            return pltpu.make_async_remote_copy(
                src_ref=x_refs[a] if own else slot, dst_ref=slot, send_sem=send_sems.at[7 * a + k],
                recv_sem=recv_sems.at[7 * a + k], device_id=to, device_id_type=MESH_ID)

        mine = [pltpu.make_async_copy(x_refs[a], out_refs[a].at[4 * x + 2 * y + c], local_sems.at[a])
                for a in range(na)]
        for cp in mine:
            cp.start()
        first = []
        for a in range(na):
            first.append(copy(a, 0, me, sibling, own=True))
            first += [copy(a, 1 + j, me, (*chip, c), own=True) for j, chip in enumerate(chips)]
        for cp in first:
            cp.start()
        passed = []
        for j, chip in enumerate(chips):
            for a in range(na):
                copy(a, 1 + j, (*chip, c), me).wait_recv()
                passed.append(copy(a, 4 + j, (*chip, c), sibling))
                passed[-1].start()
        for a in range(na):
            copy(a, 0, sibling, me).wait_recv()
            for j, chip in enumerate(chips):
                copy(a, 4 + j, (*chip, 1 - c), me).wait_recv()
        for cp in first + passed:
            cp.wait_send()
        for cp in mine:
            cp.wait()

    return pl.pallas_call(
        body, in_specs=[pl.BlockSpec(memory_space=pl.ANY)] * na, out_specs=[pl.BlockSpec(memory_space=pl.ANY)] * na,
        out_shape=[jax.ShapeDtypeStruct((N_DEV,) + s.shape, s.dtype) for s in shards],
        scratch_shapes=[pltpu.SemaphoreType.DMA((7 * na,)), pltpu.SemaphoreType.DMA((7 * na,)),
                        pltpu.SemaphoreType.DMA((na,))],
        name="all_gather_weights")(*shards)


def _exchange(blocks):
    na = len(blocks)

    def body(*refs):
        in_refs, out_refs = refs[:na], refs[na:2 * na]
        send_sems, recv_sems, local_sems = refs[2 * na:]
        x, y, c = _position()
        me = 4 * x + 2 * y + c
        local = [pltpu.make_async_copy(in_refs[a].at[me], out_refs[a].at[me], local_sems.at[a]) for a in range(na)]
        for cp in local:
            cp.start()
        sends, recvs = [], []
        for k in range(1, N_DEV):
            px, py, pc = x ^ (k >> 2), y ^ ((k >> 1) & 1), c ^ (k & 1)
            peer = 4 * px + 2 * py + pc
            for a in range(na):
                sems = dict(send_sem=send_sems.at[na * (k - 1) + a], recv_sem=recv_sems.at[na * (k - 1) + a],
                            device_id=(px, py, pc), device_id_type=MESH_ID)
                sends.append(pltpu.make_async_remote_copy(src_ref=in_refs[a].at[peer], dst_ref=out_refs[a].at[me], **sems))
                recvs.append(pltpu.make_async_remote_copy(src_ref=in_refs[a].at[me], dst_ref=out_refs[a].at[peer], **sems))
        for cp in sends:
            cp.start()
        for cp in recvs:
            cp.wait_recv()
        for cp in sends:
            cp.wait_send()
        for cp in local:
            cp.wait()

    return pl.pallas_call(
        body, in_specs=[pl.BlockSpec(memory_space=pl.ANY)] * na, out_specs=[pl.BlockSpec(memory_space=pl.ANY)] * na,
        out_shape=[jax.ShapeDtypeStruct(b.shape, b.dtype) for b in blocks],
        scratch_shapes=[pltpu.SemaphoreType.DMA((7 * na,)), pltpu.SemaphoreType.DMA((7 * na,)),
                        pltpu.SemaphoreType.DMA((na,))],
        name="exchange_grads")(*blocks)


_HBM = pl.BlockSpec(memory_space=pltpu.HBM)
_SEM = pl.BlockSpec(memory_space=pltpu.SEMAPHORE)
_DATAFLOW = pltpu.SideEffectType.DATAFLOW_SIDE_EFFECTING


def _peers(x, y, c):
    out = []
    for k in range(1, N_DEV):
        px, py, pc = x ^ (k >> 2), y ^ ((k >> 1) & 1), c ^ (k & 1)
        out.append(((px, py, pc), 4 * px + 2 * py + pc))
    return out


def _split_copies(src_refs, land_refs, send_sems, recv_sems, scatter):
    x, y, c = _position()
    me = 4 * x + 2 * y + c
    sends, recvs = [], []
    for k, (peer_id, peer) in enumerate(_peers(x, y, c)):
        for a, (src, land) in enumerate(zip(src_refs, land_refs)):
            sems = dict(send_sem=send_sems.at[7 * a + k], recv_sem=recv_sems.at[7 * a + k],
                        device_id=peer_id, device_id_type=MESH_ID)
            mine = src.at[peer] if scatter else src
            sends.append(pltpu.make_async_remote_copy(src_ref=mine, dst_ref=land.at[me], **sems))
            recvs.append(pltpu.make_async_remote_copy(src_ref=mine, dst_ref=land.at[peer], **sems))
    return sends, recvs


def _exchange_start(srcs, scatter, name, after=None):
    na = len(srcs)
    lands = [lax.empty(s.shape if scatter else (N_DEV,) + s.shape, s.dtype) for s in srcs]
    extra = [] if after is None else [after]

    def body(*refs):
        src_refs, land_refs = refs[:na], refs[na:2 * na]
        send_sems, recv_sems = refs[2 * na + len(extra)], refs[2 * na + len(extra) + 1]
        token = refs[-1]
        sends, _ = _split_copies(src_refs, land_refs, send_sems, recv_sems, scatter)
        for cp in sends:
            cp.start()
        token[...] = jnp.zeros_like(token)

    hbm = lambda a: pltpu.HBM(a.shape, a.dtype)
    out = pl.pallas_call(
        body, name=name,
        out_shape=(pltpu.SemaphoreType.DMA((7 * na,)), pltpu.SemaphoreType.DMA((7 * na,)),
                   *[hbm(s) for s in srcs], *[hbm(l) for l in lands], jax.ShapeDtypeStruct((8, LANES), F32)),
        in_specs=[_HBM] * (2 * na) + [pl.BlockSpec(memory_space=pl.ANY)] * len(extra),
        out_specs=(_SEM, _SEM, *[_HBM] * (2 * na), pl.BlockSpec(memory_space=pltpu.VMEM)),
        input_output_aliases={i: 2 + i for i in range(2 * na)},
        compiler_params=pltpu.CompilerParams(has_side_effects=_DATAFLOW),
    )(*[pltpu.with_memory_space_constraint(s, pltpu.HBM) for s in srcs],
      *[pltpu.with_memory_space_constraint(l, pltpu.HBM) for l in lands], *extra)
    return (out[0], out[1], list(out[2:2 + na]), list(out[2 + na:2 + 2 * na])), out[-1]


def _exchange_wait(handle, after, scatter, name):
    send_sems, recv_sems, srcs, lands = handle
    na = len(srcs)

    def body(*refs):
        src_refs, land_refs = refs[:na], refs[na:2 * na]
        s_sems, r_sems = refs[2 * na], refs[2 * na + 1]
        sends, recvs = _split_copies(src_refs, land_refs, s_sems, r_sems, scatter)
        for cp in sends:
            cp.wait_send()
        for cp in recvs:
            cp.wait_recv()

    hbm = lambda a: pltpu.HBM(a.shape, a.dtype)
    out = pl.pallas_call(
        body, name=name, out_shape=(*[hbm(s) for s in srcs], *[hbm(l) for l in lands]),
        in_specs=[_HBM] * (2 * na) + [_SEM, _SEM, pl.BlockSpec(memory_space=pl.ANY)],
        out_specs=tuple([_HBM] * (2 * na)), input_output_aliases={i: i for i in range(2 * na)},
        compiler_params=pltpu.CompilerParams(has_side_effects=_DATAFLOW),
    )(*srcs, *lands, send_sems, recv_sems, after)
    return list(out[:na]), list(out[na:])


def _own_slot(landed, own):
    me = 4 * lax.axis_index("x") + 2 * lax.axis_index("y") + lax.axis_index("c")
    return lax.dynamic_update_slice_in_dim(landed, own[None], me, axis=0)


def _adam_update(parts, w, m, v, name, tr=256):
    r, c = w.shape
    tr = _pick_rows(r, tr)
    cp = parts.shape[2]

    def body(p_ref, w_ref, m_ref, v_ref, g_ref, d_ref, nm_ref, nv_ref):
        g = p_ref[0, :, pl.ds(0, c)].astype(F32)
        for i in range(1, N_DEV):
            g = g + p_ref[i, :, pl.ds(0, c)].astype(F32)
        delta, nm, nv = _adamw(w_ref[...], g, m_ref[...], v_ref[...])
        g_ref[...] = g
        d_ref[...] = delta
        nm_ref[...] = nm
        nv_ref[...] = nv

    rs = pl.BlockSpec((tr, c), lambda i: (i, 0))
    return pl.pallas_call(
        body, grid=(r // tr,), in_specs=[pl.BlockSpec((N_DEV, tr, cp), lambda i: (0, i, 0)), rs, rs, rs],
        out_specs=[rs] * 4, out_shape=[jax.ShapeDtypeStruct((r, c), F32)] * 4, name=name,
        compiler_params=_params(("parallel",)))(parts, w, m, v)


def _pick_rows(rows, target):
    if rows <= target:
        return rows
    t = target
    while t >= 16:
        if rows % t == 0:
            return t
        t -= 16
    return rows


BIG = ("w_in", "w_branch_dn", "w_branch_swa", "w_out", "w_gate", "w_up", "w_down")
IN_SHARD, IN_WIRE = D_IN // N_DEV, 640
FF_SHARD, FF_WIRE = D_FF // N_DEV, 384
D_FFP = N_DEV * FF_WIRE
BIG_SHAPES = {"w_in": ((D_MODEL, IN_SHARD), (D_MODEL, IN_WIRE)),
              "w_branch_dn": ((DN_WIDTH, LANES), (DN_WIDTH, LANES)),
              "w_branch_swa": ((SWA_WIDTH, LANES), (SWA_WIDTH, LANES)),
              "w_out": ((LANES, D_MODEL), (LANES, D_MODEL)),
              "w_gate": ((D_MODEL, FF_SHARD), (D_MODEL, FF_WIRE)),
              "w_up": ((D_MODEL, FF_SHARD), (D_MODEL, FF_WIRE)),
              "w_down": ((FF_SHARD, D_MODEL), (FF_WIRE, D_MODEL))}
CONV_SHARD, CONV_WIRE = (DN_CONV, DN_QKV // N_DEV), (8, 256)


def _pad_to(a, shape):
    return jnp.pad(a, [(0, t - s) for s, t in zip(a.shape, shape)])


_IN_SEGS = ((R_GATE, 2048, P_GATE), (R_QKV, DN_QKV, P_QKV), (R_Z, DN_WIDTH, P_Z), (R_SQ, SWA_WIDTH, P_SQ),
            (R_SK, SWA_KVW, P_SK), (R_SV, SWA_KVW, P_SV), (R_B, 8, P_BA))


def _w_in_from_blocks(blocks):
    parts = []
    for rs, n, _ in _IN_SEGS:
        for dev in range(N_DEV):
            lo, hi = max(rs, IN_SHARD * dev), min(rs + n, IN_SHARD * (dev + 1))
            if lo < hi:
                parts.append(blocks[dev, :, lo - IN_SHARD * dev:hi - IN_SHARD * dev])
    parts.append(jnp.zeros((blocks.shape[1], P_WIDTH - P_BA - 8), blocks.dtype))
    return jnp.concatenate(parts, axis=1)


def _w_in_to_blocks(g):
    out = []
    for dev in range(N_DEV):
        parts = []
        for rs, n, ps in sorted(_IN_SEGS):
            lo, hi = max(rs, IN_SHARD * dev), min(rs + n, IN_SHARD * (dev + 1))
            if lo < hi:
                parts.append(g[:, ps + lo - rs:ps + hi - rs])
        parts.append(jnp.zeros((g.shape[0], IN_WIRE - IN_SHARD), g.dtype))
        out.append(jnp.concatenate(parts, axis=1))
    return jnp.stack(out)


def _pack(flat_parts, rows):
    flat = jnp.concatenate(flat_parts, axis=-1)
    pad = rows * LANES - flat.shape[-1]
    flat = jnp.pad(flat, [(0, 0)] * (flat.ndim - 1) + [(0, pad)])
    return flat.reshape(flat.shape[:-1] + (rows, LANES))


def _unpack(buf, shapes):
    lead = buf.shape[:-2]
    flat = buf.reshape(lead + (-1,))
    out, off = [], 0
    for shp in shapes:
        n = int(np.prod(shp))
        out.append(flat[..., off:off + n].reshape(lead + tuple(shp)))
        off += n
    return out


def _cols_join(blocks):
    return jnp.concatenate([blocks[d] for d in range(N_DEV)], axis=1)


def _cols_split(full):
    c = full.shape[1] // N_DEV
    return jnp.stack([full[:, d * c:(d + 1) * c] for d in range(N_DEV)])


SMALL = ("attn_norm", "ffn_norm", "rel_bias", "dn_out_norm", "swa_q_norm", "swa_k_norm", "dn_a_log",
         "dn_dt_bias", "swa_sinks")
SMALL_ROWS = 24
CONV_ROWS = DN_CONV * DN_QKV // LANES


def _pack_small(d):
    return _pack([d[n].reshape(-1) for n in SMALL], SMALL_ROWS)


def kernel(x, attn_norm, w_in, dn_conv, dn_a_log, dn_dt_bias, dn_out_norm, swa_q_norm, swa_k_norm, swa_sinks, rel_bias, w_branch_dn, w_branch_swa, w_out, ffn_norm, w_gate, w_up, w_down, loss_target, m_attn_norm, m_w_in, m_dn_conv, m_dn_a_log, m_dn_dt_bias, m_dn_out_norm, m_swa_q_norm, m_swa_k_norm, m_swa_sinks, m_rel_bias, m_w_branch_dn, m_w_branch_swa, m_w_out, m_ffn_norm, m_w_gate, m_w_up, m_w_down, v_attn_norm, v_w_in, v_dn_conv, v_dn_a_log, v_dn_dt_bias, v_dn_out_norm, v_swa_q_norm, v_swa_k_norm, v_swa_sinks, v_rel_bias, v_w_branch_dn, v_w_branch_swa, v_w_out, v_ffn_norm, v_w_gate, v_w_up, v_w_down):
    args = dict(locals())
    S = x.shape[1]
    xs = x.reshape(S, D_MODEL)
    target = loss_target.reshape(S, D_MODEL)

    w_loc = {n: args[n].reshape(BIG_SHAPES[n][0]) for n in BIG}
    conv_loc = dn_conv.reshape(CONV_SHARD)
    wire = {n: _pad_to(w_loc[n], BIG_SHAPES[n][1]).astype(BF16) for n in BIG}
    first = _all_gather([wire["w_in"], _pad_to(conv_loc, CONV_WIRE)])
    later = [n for n in BIG if n != "w_in"]
    rest_handle, rest_token = _exchange_start([wire[n] for n in later], False, "gather_rest_start", after=first[1])
    w_pad = _w_in_from_blocks(first[0])
    conv_w = jnp.concatenate([first[1][d, :DN_CONV, :CONV_SHARD[1]] for d in range(N_DEV)], axis=1)

    h = _norm_fwd(xs, attn_norm + rest_token[0, 0], "norm1_fwd")
    proj = _mm([(h, w_pad)], "nn", F32, "mm_in", 512, 1664, j_outer=True)
    qkvn = _dn_conv_fwd(proj, conv_w)
    beta, g = _dn_gate_fwd(proj, dn_a_log, dn_dt_bias)
    u, w, qe, kd, qk, egl = _dn_prep_fwd(qkvn, g, beta)
    o, states = _dn_scan_fwd(u, w, qe, kd, qk, egl)
    y_dn = _dn_out_fwd(o, proj, dn_out_norm)
    bias = _bias_fwd(rel_bias)
    y_swa = _swa_fwd(proj, swa_q_norm, swa_k_norm, swa_sinks, bias)
    rest_src, rest_land = _exchange_wait(rest_handle, y_swa, False, "gather_rest_wait")
    G = {n: _own_slot(land, src) for n, src, land in zip(later, rest_src, rest_land)}
    w_bdn = _cols_join(G["w_branch_dn"])
    w_bswa = _cols_join(G["w_branch_swa"])
    w_o = G["w_out"].reshape(D_MODEL, D_MODEL)
    w_d = G["w_down"].reshape(D_FFP, D_MODEL)
    w_g = _cols_join(G["w_gate"])
    w_u = _cols_join(G["w_up"])
    a_dn = _mm([(y_dn, w_bdn)], "nn", F32, "mm_branch_dn", 1024, D_MODEL)
    a_swa = _mm([(y_swa, w_bswa)], "nn", F32, "mm_branch_swa", 1024, D_MODEL)
    merged = _merge_fwd(proj, a_dn, a_swa)
    t_out = _mm([(merged, w_o)], "nn", F32, "mm_out", 1024, D_MODEL)
    x1, h2 = _resid_norm_fwd(xs, t_out, ffn_norm, "norm2_fwd")
    gate = _mm([(h2, w_g)], "nn", F32, "mm_gate", 1024, 1024, j_outer=True)
    up = _mm([(h2, w_u)], "nn", F32, "mm_up", 1024, 1024, j_outer=True)
    act = _act_fwd(gate, up)
    f = _mm([(act, w_d)], "nn", F32, "mm_down", 512, D_MODEL)
    dy, dy_b, loss_local = _loss_fwd_bwd(x1, f, target)

    dact = _mm([(dy_b, w_d)], "nt", F32, "mm_dact", 1024, 1024, j_outer=True)
    g_w_down = _mm([(act, dy_b)], "tn", BF16, "mm_dw_down", 768, D_MODEL, j_outer=True)
    g_w_down = g_w_down.reshape(N_DEV, FF_WIRE, D_MODEL)
    dgate, dup = _act_bwd(gate, up, dact)
    dh2 = _mm([(dgate, w_g), (dup, w_u)], "nt", F32, "mm_dh2", 512, 512, j_outer=True)
    g_w_gate = _cols_split(_mm([(h2, dgate)], "tn", BF16, "mm_dw_gate", D_MODEL, 768))
    g_w_up = _cols_split(_mm([(h2, dup)], "tn", BF16, "mm_dw_up", D_MODEL, 768))
    ffn_handle, ffn_token = _exchange_start([g_w_down, g_w_gate, g_w_up], True, "scatter_ffn_start")
    dx1, dx1_b, g_ffn_norm = _norm_bwd(x1, [dh2], dy, ffn_norm + ffn_token[0, 0], "norm2_bwd")
    dmerged = _mm([(dx1_b, w_o)], "nt", F32, "mm_dmerged", 1024, D_MODEL)
    g_w_out = _mm([(merged, dx1_b)], "tn", BF16, "mm_dw_out", 512, D_MODEL, j_outer=True)
    g_w_out = g_w_out.reshape(N_DEV, LANES, D_MODEL)
    dg0, dg1, da_dn, da_swa = _merge_bwd(proj, a_dn, a_swa, dmerged)
    dy_dn = _mm([(da_dn, w_bdn)], "nt", F32, "mm_dy_dn", 1024, DN_WIDTH)
    dy_swa = _mm([(da_swa, w_bswa)], "nt", F32, "mm_dy_swa", 1024, SWA_WIDTH)
    g_w_bdn = _cols_split(_mm([(y_dn, da_dn)], "tn", BF16, "mm_dw_branch_dn", DN_WIDTH, 512))
    g_w_bswa = _cols_split(_mm([(y_swa, da_swa)], "tn", BF16, "mm_dw_branch_swa", SWA_WIDTH, 512))
    dsq, dsk, dsv, g_q_norm, g_k_norm, g_sinks, dbias = _swa_bwd(proj, swa_q_norm, swa_k_norm, swa_sinks, bias, dy_swa)
    g_rel_bias = _bias_bwd(dbias)[:, :REL_BUCKETS].T
    mix_handle, mix_token = _exchange_start([g_w_out, g_w_bdn, g_w_bswa], True, "scatter_mix_start")
    do, dz, g_out_norm = _dn_out_bwd(o, proj, dn_out_norm + mix_token[0, 0], dy_dn)
    du, dw, dqe, dkd, dqk, degl = _dn_scan_bwd(u, w, qe, kd, qk, egl, states, do)
    dq, dk, dv, dgd, dbeta = _dn_prep_bwd(qkvn, g, beta, du, dw, dqe, dkd, dqk, degl)
    dqkvn = jnp.concatenate([dq, dk, dv], axis=1)
    dba, dal, ddt = _dn_gate_bwd(proj, dn_a_log, dn_dt_bias, dbeta, dgd)
    g_a_log = dal.reshape(DN_HEADS, DN_DIM).sum(axis=1)
    g_dt_bias = ddt.reshape(DN_HEADS, DN_DIM).sum(axis=1)
    dqkv, g_conv = _dn_conv_bwd(proj, conv_w, dqkvn)
    dproj = jnp.concatenate([dg0, dg1, dqkv, dz, dsq, dsk.astype(BF16), dsv.astype(BF16), dba], axis=1)
    g_w_in = _w_in_to_blocks(_mm([(h, dproj)], "tn", BF16, "mm_dw_in", 512, 1664, j_outer=True))
    in_handle, in_token = _exchange_start([g_w_in], True, "scatter_in_start")
    dh = _mm([(dproj, w_pad)], "nt", F32, "mm_dh", 512, D_MODEL)
    dx, _, g_attn_norm = _norm_bwd(xs, [dh], dx1, attn_norm + in_token[0, 0], "norm1_bwd")

    g_small = {"attn_norm": g_attn_norm, "ffn_norm": g_ffn_norm, "rel_bias": g_rel_bias, "dn_out_norm": g_out_norm,
               "swa_q_norm": g_q_norm, "swa_k_norm": g_k_norm, "dn_a_log": g_a_log, "dn_dt_bias": g_dt_bias,
               "swa_sinks": g_sinks}
    small_rows = jnp.concatenate([_pack_small(g_small), g_conv.reshape(CONV_ROWS, LANES)], axis=0)
    send_small = jnp.broadcast_to(small_rows[None], (N_DEV,) + small_rows.shape)
    (recv_small,) = _exchange([send_small])
    me = 4 * lax.axis_index("x") + 2 * lax.axis_index("y") + lax.axis_index("c")
    received = {}
    for handle, group, name in ((ffn_handle, ("w_down", "w_gate", "w_up"), "scatter_ffn_wait"),
                                (mix_handle, ("w_out", "w_branch_dn", "w_branch_swa"), "scatter_mix_wait"),
                                (in_handle, ("w_in",), "scatter_in_wait")):
        srcs, lands = _exchange_wait(handle, recv_small, True, name)
        for n, src, land in zip(group, srcs, lands):
            received[n] = _own_slot(land, lax.dynamic_index_in_dim(src, me, 0, keepdims=False))

    outs = {}
    for n in BIG:
        shp = BIG_SHAPES[n][0]
        outs[n] = _adam_update(received[n], w_loc[n], args["m_" + n].reshape(shp), args["v_" + n].reshape(shp),
                               "adam_" + n)
    conv_parts = lax.dynamic_slice_in_dim(recv_small[:, SMALL_ROWS:].reshape(N_DEV, DN_CONV, DN_QKV),
                                          me * CONV_SHARD[1], CONV_SHARD[1], axis=2)
    outs["dn_conv"] = _adam_update(conv_parts, conv_loc, m_dn_conv.reshape(CONV_SHARD), v_dn_conv.reshape(CONV_SHARD),
                                   "adam_dn_conv")
    small_out = _adam_update(recv_small[:, :SMALL_ROWS], _pack_small({n: args[n] for n in SMALL}),
                             _pack_small({n: args["m_" + n] for n in SMALL}),
                             _pack_small({n: args["v_" + n] for n in SMALL}), "adam_small")

    names = ("attn_norm", "w_in", "dn_conv", "dn_a_log", "dn_dt_bias", "dn_out_norm", "swa_q_norm", "swa_k_norm",
             "swa_sinks", "rel_bias", "w_branch_dn", "w_branch_swa", "w_out", "ffn_norm", "w_gate", "w_up", "w_down")
    results = []
    for kind in range(4):
        small = dict(zip(SMALL, _unpack(small_out[kind], [args[n].shape for n in SMALL])))
        results += [outs[n][kind].reshape(args[n].shape) if n in outs else small[n] for n in names]

    loss = lax.psum(loss_local[0, 0], ("x", "y", "c"))
    return (loss, dx.reshape(x.shape), *results)
```

```python
import math

import numpy as np
import jax
import jax.numpy as jnp
from jax import lax
from jax.experimental import pallas as pl
from jax.experimental.pallas import tpu as pltpu

F32 = jnp.float32
BF16 = jnp.bfloat16
HI = lax.Precision.HIGHEST

D_MODEL = 1024
DN_HEADS = 4
DN_DIM = 128
DN_WIDTH = 512
DN_QKV = 1536
DN_CONV = 4
CHUNK = 64
SWA_HEADS = 8
SWA_KV = 2
SWA_GROUP = 4
SWA_DIM = 64
SWA_WIDTH = 512
SWA_KVW = 128
WINDOW = 128
BLOCK = 128
REL_BUCKETS = 32
REL_MAX_DIST = 128
D_FF = 2816
D_IN = 4872
EPS = 1e-6
N_DEV = 8

ADAM_LR = 0.001
ADAM_B1 = 0.9
ADAM_B2 = 0.999
ADAM_EPS = 1e-08
ADAM_WD = 0.01
ADAM_STEP = 10

P_GATE, P_QKV, P_Z, P_SQ, P_SK, P_SV, P_BA = 0, 2048, 3584, 4096, 4608, 4736, 4864
P_WIDTH = 4992
R_QKV, R_Z, R_B, R_A, R_SQ, R_SK, R_SV, R_GATE = 0, 1536, 2048, 2052, 2056, 2568, 2696, 2824

VMEM_LIMIT = 56 * 1024 * 1024
LANES = 128
MESH_ID = pl.DeviceIdType.MESH


def _params(sem=None):
    return pltpu.CompilerParams(dimension_semantics=sem, vmem_limit_bytes=VMEM_LIMIT)


def _pick(dim, target):
    if dim <= target:
        return dim
    t = target - target % LANES
    while t >= LANES:
        if dim % t == 0:
            return t
        t -= LANES
    return dim


_DIMS = {"nn": (((1,), (0,)), ((), ())), "nt": (((1,), (1,)), ((), ())), "tn": (((0,), (0,)), ((), ()))}


def _mm(pairs, mode, out_dtype, name, bm, bn, j_outer=False):
    a0, b0 = pairs[0]
    if mode == "nn":
        (M, K), (K2, N) = a0.shape, b0.shape
    elif mode == "nt":
        (M, K), (N, K2) = a0.shape, b0.shape
    else:
        (K, M), (K2, N) = a0.shape, b0.shape
    bm, bn = min(bm, M), min(bn, N)
    assert K == K2 and M % bm == 0 and N % bn == 0, (name, a0.shape, b0.shape, bm, bn)
    dims = _DIMS[mode]
    n = len(pairs)

    def body(*refs):
        o_ref = refs[2 * n]
        acc = None
        for t in range(n):
            p = lax.dot_general(refs[2 * t][...].astype(BF16), refs[2 * t + 1][...].astype(BF16), dims,
                                preferred_element_type=F32)
            acc = p if acc is None else acc + p
        o_ref[...] = acc.astype(out_dtype)

    def ij(f):
        return (lambda j, i: f(i, j)) if j_outer else f

    a_spec = pl.BlockSpec((K, bm), ij(lambda i, j: (0, i))) if mode == "tn" else pl.BlockSpec((bm, K), ij(lambda i, j: (i, 0)))
    b_spec = pl.BlockSpec((bn, K), ij(lambda i, j: (j, 0))) if mode == "nt" else pl.BlockSpec((K, bn), ij(lambda i, j: (0, j)))
    grid = (N // bn, M // bm) if j_outer else (M // bm, N // bn)
    return pl.pallas_call(
        body, grid=grid, in_specs=[a_spec, b_spec] * n, out_specs=pl.BlockSpec((bm, bn), ij(lambda i, j: (i, j))),
        out_shape=jax.ShapeDtypeStruct((M, N), out_dtype), name=name,
        compiler_params=_params(("parallel", "parallel")),
    )(*[x for pair in pairs for x in pair])


def _rms(x, gain):
    return x * lax.rsqrt(jnp.mean(x * x, axis=-1, keepdims=True) + EPS) * gain


def _silu(x):
    return x * jax.nn.sigmoid(x)


def _act(g, u):
    return _silu(g) * u


def _merge(g0, g1, a_dn, a_swa):
    return jax.nn.sigmoid(g0) * a_dn + jax.nn.sigmoid(g1) * a_swa


def _dn_post(c, is_v, q_scale):
    a = _silu(c)
    rs = lax.rsqrt(jnp.sum(a * a, axis=-1, keepdims=True) + EPS) * q_scale
    return a * jnp.where(is_v, 1.0, rs)


def _dn_out(o, z, gain):
    return _rms(o, gain) * _silu(z)


def _dot(a, b, dims=_DIMS["nn"], hi=False):
    if a.ndim == 3 or b.ndim == 3:
        batch = a.shape[0] if a.ndim == 3 else b.shape[0]
        a = a if a.ndim == 3 else jnp.broadcast_to(a, (batch,) + a.shape)
        b = b if b.ndim == 3 else jnp.broadcast_to(b, (batch,) + b.shape)
        ((ca,), (cb,)), _ = dims
        dims = (((ca + 1,), (cb + 1,)), ((0,), (0,)))
    if hi:
        return lax.dot_general(a, b, dims, precision=HI, preferred_element_type=F32)
    return lax.dot_general(a.astype(BF16), b.astype(BF16), dims, preferred_element_type=F32)


def _pieces(x):
    hi = x.astype(BF16)
    r1 = x - hi.astype(F32)
    mid = r1.astype(BF16)
    return hi, mid, (r1 - mid.astype(F32)).astype(BF16)


def _sel_left_impl(m, x):
    mb = m.astype(BF16)
    hi, mid, lo = _pieces(x)
    return _dot(mb, hi) + (_dot(mb, mid) + _dot(mb, lo))


@jax.custom_vjp
def _sel_left(m, mt, x):
    return _sel_left_impl(m, x)


_sel_left.defvjp(lambda m, mt, x: (_sel_left_impl(m, x), (m, mt)),
                 lambda res, ct: (jnp.zeros_like(res[0]), jnp.zeros_like(res[1]), _sel_left_impl(res[1], ct)))


def _sel_right_impl(x, s):
    sb = s.astype(BF16)
    hi, mid, lo = _pieces(x)
    return _dot(hi, sb) + (_dot(mid, sb) + _dot(lo, sb))


@jax.custom_vjp
def _sel_right(x, s, st):
    return _sel_right_impl(x, s)


_sel_right.defvjp(lambda x, s, st: (_sel_right_impl(x, s), (s, st)),
                  lambda res, ct: (_sel_right_impl(ct, res[1]), jnp.zeros_like(res[0]), jnp.zeros_like(res[1])))


def _sel_nt_impl(s, x):
    sb = s.astype(BF16)
    hi, mid, lo = _pieces(x)
    return _dot(sb, hi, _DIMS["nt"]) + (_dot(sb, mid, _DIMS["nt"]) + _dot(sb, lo, _DIMS["nt"]))


def _sel_tn_impl(x, s):
    sb = s.astype(BF16)
    hi, mid, lo = _pieces(x)
    return _dot(hi, sb, _DIMS["tn"]) + (_dot(mid, sb, _DIMS["tn"]) + _dot(lo, sb, _DIMS["tn"]))


@jax.custom_vjp
def _sel_nt(s, x):
    return _sel_nt_impl(s, x)


_sel_nt.defvjp(lambda s, x: (_sel_nt_impl(s, x), s),
               lambda s, ct: (jnp.zeros_like(s), _sel_tn_impl(ct, s)))


def _dot3_impl(a, b):
    a_hi, a_lo, _ = _pieces(a)
    b_hi, b_lo, _ = _pieces(b)
    return _dot(a_hi, b_hi) + (_dot(a_hi, b_lo) + _dot(a_lo, b_hi))


@jax.custom_vjp
def _dot3(a, b):
    return _dot3_impl(a, b)


_dot3.defvjp(lambda a, b: (_dot3_impl(a, b), (a, b)),
             lambda res, ct: (_dot(ct, res[1], _DIMS["nt"]), _dot(res[0], ct, _DIMS["tn"])))


def _inv_impl(a, eye, strict):
    t = eye - a
    p = _dot(a, a)
    for level in range(5):
        t = t + _dot(t, p)
        if level < 4:
            p = _dot(p, p)
    t = t + _dot3_impl(t, eye - _dot3_impl(eye + a, t))
    return jnp.where(strict > 0.5, t, eye)


@jax.custom_vjp
def _inv_unit_lower(a, eye, strict):
    return _inv_impl(a, eye, strict)


def _inv_bwd(res, ct):
    t, eye, strict = res
    da = -_dot(_dot(t, ct, _DIMS["tn"]), t, _DIMS["nt"])
    return da, jnp.zeros_like(eye), jnp.zeros_like(strict)


def _inv_fwd(a, eye, strict):
    t = _inv_impl(a, eye, strict)
    return t, (t, eye, strict)


_inv_unit_lower.defvjp(_inv_fwd, _inv_bwd)

GROUP = 4
GROUP_ROWS = GROUP * CHUNK


def _block_consts(n):
    ii = lax.broadcasted_iota(jnp.int32, (n, n), 0)
    jj = lax.broadcasted_iota(jnp.int32, (n, n), 1)
    shift = CHUNK.bit_length() - 1
    same = jnp.right_shift(ii, shift) == jnp.right_shift(jj, shift)
    return same & (ii >= jj), same & (ii <= jj), same & (ii > jj), same, ii == jj


def _lane0(n):
    s = (lax.broadcasted_iota(jnp.int32, (LANES, n), 0) == 0).astype(F32)
    st = (lax.broadcasted_iota(jnp.int32, (n, LANES), 1) == 0).astype(F32)
    return s, st


def _dn_group(q, k, v, g, beta):
    n = GROUP_ROWS
    low_b, upp_b, strict_b, same_b, eye_b = _block_consts(n)
    low, upp, same, eye = low_b.astype(F32), upp_b.astype(F32), same_b.astype(F32), eye_b.astype(F32)
    s, st = _lane0(n)
    gc = _sel_left(low, upp, g)
    gl = _sel_left(same, same, g)
    col = _sel_right(gc, s, st)
    row = _sel_nt(st, gc)
    decay = jnp.exp(jnp.where(low_b, col - row, -jnp.inf))
    kb = k * beta
    vb = v * beta
    a = jnp.where(strict_b, _dot(kb, k, _DIMS["nt"]) * decay, 0.0)
    t = _inv_unit_lower(a, eye, strict_b.astype(F32))
    u = _dot3(t, vb)
    w = _dot3(t, kb * jnp.exp(gc))
    return u, w, q * jnp.exp(gc), k * jnp.exp(gl - gc)


def _dn_chunk(q, k, g):
    ii = lax.broadcasted_iota(jnp.int32, (CHUNK, CHUNK), 0)
    jj = lax.broadcasted_iota(jnp.int32, (CHUNK, CHUNK), 1)
    low = (ii >= jj).astype(F32)
    upp = (ii <= jj).astype(F32)
    s, st = _lane0(CHUNK)
    gc = _sel_left(low, upp, g)
    col = _sel_right(gc, s, st)
    row = _sel_nt(st, gc)
    decay = jnp.exp(jnp.where(ii >= jj, col - row, -jnp.inf))
    qk = _dot(q, k, _DIMS["nt"]) * decay
    return qk, jnp.exp(jnp.sum(g, axis=-2, keepdims=True))


def _dn_step(s, u, w, qe, kd, qk, egl):
    v_new = u - _dot(w, s)
    o = _dot(qe, s) + _dot(qk, v_new)
    s_new = s * egl + _dot(kd, v_new, _DIMS["tn"])
    return s_new, o


def _swa_block(q, kband, vband, qg, kg, sinks, bias, mask):
    kn = _rms(kband, kg)
    qn = _rms(q, qg)
    logits = _dot(qn, kn, _DIMS["nt"]) * (SWA_DIM ** -0.5)
    logits = jnp.where(mask, logits + bias, -jnp.inf)
    m = jnp.maximum(jnp.max(logits, axis=-1, keepdims=True), sinks)
    p = jnp.exp(logits - m)
    denom = jnp.sum(p, axis=-1, keepdims=True) + jnp.exp(sinks - m)
    return _dot(p / denom, vband)


def _adamw(w, g, m, v):
    m = ADAM_B1 * m + (1.0 - ADAM_B1) * g
    v = ADAM_B2 * v + (1.0 - ADAM_B2) * jnp.square(g)
    m_hat = m / (1.0 - ADAM_B1 ** ADAM_STEP)
    v_hat = v / (1.0 - ADAM_B2 ** ADAM_STEP)
    delta = -ADAM_LR * (m_hat / (jnp.sqrt(v_hat) + ADAM_EPS) + ADAM_WD * w)
    return delta, m, v


def _row(tm, c, cb=0):
    return pl.BlockSpec((tm, c), lambda i, cb=cb: (i, cb))


def _full(shape):
    nd = len(shape)
    return pl.BlockSpec(shape, lambda *_, nd=nd: (0,) * nd)


def _norm_fwd(x, gain, name, tm=512):
    S = x.shape[0]

    def body(x_ref, g_ref, h_ref):
        h_ref[...] = _rms(x_ref[...], g_ref[...]).astype(BF16)

    return pl.pallas_call(
        body, grid=(S // tm,), in_specs=[_row(tm, D_MODEL), _full((1, D_MODEL))],
        out_specs=_row(tm, D_MODEL), out_shape=jax.ShapeDtypeStruct((S, D_MODEL), BF16),
        name=name, compiler_params=_params(("parallel",)))(x, gain)


def _resid_norm_fwd(x, t, gain, name, tm=512):
    S = x.shape[0]

    def body(x_ref, t_ref, g_ref, x1_ref, h_ref):
        x1 = x_ref[...] + t_ref[...]
        x1_ref[...] = x1
        h_ref[...] = _rms(x1, g_ref[...]).astype(BF16)

    return pl.pallas_call(
        body, grid=(S // tm,), in_specs=[_row(tm, D_MODEL), _row(tm, D_MODEL), _full((1, D_MODEL))],
        out_specs=[_row(tm, D_MODEL), _row(tm, D_MODEL)],
        out_shape=[jax.ShapeDtypeStruct((S, D_MODEL), F32), jax.ShapeDtypeStruct((S, D_MODEL), BF16)],
        name=name, compiler_params=_params(("parallel",)))(x, t, gain)


def _norm_bwd(x, dh_list, dres, gain, name, tm=256):
    S = x.shape[0]
    n = len(dh_list)

    def body(*refs):
        x_ref, g_ref, r_ref = refs[0], refs[1], refs[2]
        dh_refs = refs[3:3 + n]
        dx_ref, dxb_ref, dg_ref = refs[3 + n], refs[4 + n], refs[5 + n]
        dh = dh_refs[0][...].astype(F32)
        for r in dh_refs[1:]:
            dh = dh + r[...].astype(F32)
        _, vjp = jax.vjp(_rms, x_ref[...], g_ref[...])
        dx, dg = vjp(dh)
        dx = dx + r_ref[...]
        dx_ref[...] = dx
        dxb_ref[...] = dx.astype(BF16)

        @pl.when(pl.program_id(0) == 0)
        def _():
            dg_ref[...] = jnp.zeros_like(dg_ref)

        dg_ref[...] += dg

    return pl.pallas_call(
        body, grid=(S // tm,),
        in_specs=[_row(tm, D_MODEL), _full((1, D_MODEL)), _row(tm, D_MODEL)] + [_row(tm, D_MODEL)] * n,
        out_specs=[_row(tm, D_MODEL), _row(tm, D_MODEL), _full((1, D_MODEL))],
        out_shape=[jax.ShapeDtypeStruct((S, D_MODEL), F32), jax.ShapeDtypeStruct((S, D_MODEL), BF16),
                   jax.ShapeDtypeStruct((1, D_MODEL), F32)],
        name=name, compiler_params=_params(("arbitrary",)))(x, gain, dres, *dh_list)


def _loss_fwd_bwd(x1, f, target, tm=512):
    S = x1.shape[0]

    def body(x_ref, f_ref, t_ref, dy_ref, dyb_ref, l_ref):
        diff = x_ref[...] + f_ref[...] - t_ref[...]
        dy = diff * (1.0 / D_MODEL)
        dy_ref[...] = dy
        dyb_ref[...] = dy.astype(BF16)

        @pl.when(pl.program_id(0) == 0)
        def _():
            l_ref[...] = jnp.zeros_like(l_ref)

        l_ref[...] += jnp.sum(jnp.mean(diff * diff, axis=-1, keepdims=True), axis=0, keepdims=True) * 0.5

    return pl.pallas_call(
        body, grid=(S // tm,), in_specs=[_row(tm, D_MODEL)] * 3,
        out_specs=[_row(tm, D_MODEL), _row(tm, D_MODEL), _full((1, 1))],
        out_shape=[jax.ShapeDtypeStruct((S, D_MODEL), F32), jax.ShapeDtypeStruct((S, D_MODEL), BF16),
                   jax.ShapeDtypeStruct((1, 1), F32)],
        name="loss_fwd_bwd", compiler_params=_params(("arbitrary",)))(x1, f, target)


def _act_fwd(g, u, tm=256):
    S, width = g.shape

    def body(g_ref, u_ref, o_ref):
        o_ref[...] = _act(g_ref[...], u_ref[...]).astype(BF16)

    return pl.pallas_call(
        body, grid=(S // tm,), in_specs=[_row(tm, width)] * 2, out_specs=_row(tm, width),
        out_shape=jax.ShapeDtypeStruct((S, width), BF16), name="act_fwd",
        compiler_params=_params(("parallel",)))(g, u)


def _act_bwd(g, u, dact, tm=256):
    S, width = g.shape

    def body(g_ref, u_ref, d_ref, dg_ref, du_ref):
        _, vjp = jax.vjp(_act, g_ref[...], u_ref[...])
        dg, du = vjp(d_ref[...])
        dg_ref[...] = dg.astype(BF16)
        du_ref[...] = du.astype(BF16)

    return pl.pallas_call(
        body, grid=(S // tm,), in_specs=[_row(tm, width)] * 3, out_specs=[_row(tm, width)] * 2,
        out_shape=[jax.ShapeDtypeStruct((S, width), BF16)] * 2, name="act_bwd",
        compiler_params=_params(("parallel",)))(g, u, dact)


def _merge_fwd(proj, a_dn, a_swa, tm=512, tc=512):
    S = proj.shape[0]
    nc = D_MODEL // tc

    def spec(off):
        return pl.BlockSpec((tm, tc), lambda i, j, off=off: (i, off + j))

    def body(g0_ref, g1_ref, ad_ref, as_ref, o_ref):
        o_ref[...] = _merge(g0_ref[...], g1_ref[...], ad_ref[...], as_ref[...]).astype(BF16)

    return pl.pallas_call(
        body, grid=(S // tm, nc), in_specs=[spec(P_GATE // tc), spec(P_GATE // tc + nc), spec(0), spec(0)],
        out_specs=spec(0), out_shape=jax.ShapeDtypeStruct((S, D_MODEL), BF16), name="merge_fwd",
        compiler_params=_params(("parallel", "parallel")))(proj, proj, a_dn, a_swa)


def _merge_bwd(proj, a_dn, a_swa, dmerged, tm=512, tc=512):
    S = proj.shape[0]
    nc = D_MODEL // tc

    def spec(off):
        return pl.BlockSpec((tm, tc), lambda i, j, off=off: (i, off + j))

    def body(g0_ref, g1_ref, ad_ref, as_ref, d_ref, dg0_ref, dg1_ref, dad_ref, das_ref):
        _, vjp = jax.vjp(_merge, g0_ref[...], g1_ref[...], ad_ref[...], as_ref[...])
        dg0, dg1, dad, das = vjp(d_ref[...])
        dg0_ref[...] = dg0.astype(BF16)
        dg1_ref[...] = dg1.astype(BF16)
        dad_ref[...] = dad.astype(BF16)
        das_ref[...] = das.astype(BF16)

    return pl.pallas_call(
        body, grid=(S // tm, nc),
        in_specs=[spec(P_GATE // tc), spec(P_GATE // tc + nc), spec(0), spec(0), spec(0)],
        out_specs=[spec(0)] * 4, out_shape=[jax.ShapeDtypeStruct((S, D_MODEL), BF16)] * 4,
        name="merge_bwd", compiler_params=_params(("parallel", "parallel")))(proj, proj, a_dn, a_swa, dmerged)


def _shift_down(x, s):
    row = lax.broadcasted_iota(jnp.int32, x.shape, 0)
    return jnp.where(row >= s, pltpu.roll(x, s, axis=0), 0.0)


def _shift_up(x, s):
    n = x.shape[0]
    row = lax.broadcasted_iota(jnp.int32, x.shape, 0)
    return jnp.where(row < n - s, pltpu.roll(x, n - s, axis=0), 0.0)


def _conv(x, w):
    out = w[DN_CONV - 1:DN_CONV] * x
    for s in range(1, DN_CONV):
        out = out + w[DN_CONV - 1 - s:DN_CONV - s] * _shift_down(x, s)
    return out


def _dn_conv_fwd(proj, conv_w):
    S = proj.shape[0]
    nb = DN_QKV // LANES

    def body(x_ref, w_ref, o_ref):
        j = pl.program_id(0)
        q_scale = jnp.where(j < DN_HEADS, DN_DIM ** -0.5, 1.0).astype(F32)
        o_ref[...] = _dn_post(_conv(x_ref[...], w_ref[...]), j >= 2 * DN_HEADS, q_scale)

    return pl.pallas_call(
        body, grid=(nb,),
        in_specs=[pl.BlockSpec((S, LANES), lambda j: (0, P_QKV // LANES + j)),
                  pl.BlockSpec((DN_CONV, LANES), lambda j: (0, j))],
        out_specs=pl.BlockSpec((S, LANES), lambda j: (0, j)),
        out_shape=jax.ShapeDtypeStruct((S, DN_QKV), F32), name="dn_conv_fwd",
        compiler_params=_params(("parallel",)))(proj, conv_w)


def _dn_conv_bwd(proj, conv_w, dqkvn):
    S = proj.shape[0]
    nb = DN_QKV // LANES

    def body(x_ref, w_ref, d_ref, dx_ref, dw_ref):
        j = pl.program_id(0)
        q_scale = jnp.where(j < DN_HEADS, DN_DIM ** -0.5, 1.0).astype(F32)
        x = x_ref[...]
        w = w_ref[...]
        _, vjp = jax.vjp(lambda c: _dn_post(c, j >= 2 * DN_HEADS, q_scale), _conv(x, w))
        (dc,) = vjp(d_ref[...])
        dx = w[DN_CONV - 1:DN_CONV] * dc
        dw_ref[DN_CONV - 1:DN_CONV, :] = jnp.sum(dc * x, axis=0, keepdims=True)
        for s in range(1, DN_CONV):
            dx = dx + w[DN_CONV - 1 - s:DN_CONV - s] * _shift_up(dc, s)
            dw_ref[DN_CONV - 1 - s:DN_CONV - s, :] = jnp.sum(dc * _shift_down(x, s), axis=0, keepdims=True)
        dx_ref[...] = dx.astype(BF16)

    return pl.pallas_call(
        body, grid=(nb,),
        in_specs=[pl.BlockSpec((S, LANES), lambda j: (0, P_QKV // LANES + j)),
                  pl.BlockSpec((DN_CONV, LANES), lambda j: (0, j)),
                  pl.BlockSpec((S, LANES), lambda j: (0, j))],
        out_specs=[pl.BlockSpec((S, LANES), lambda j: (0, j)), pl.BlockSpec((DN_CONV, LANES), lambda j: (0, j))],
        out_shape=[jax.ShapeDtypeStruct((S, DN_QKV), BF16), jax.ShapeDtypeStruct((DN_CONV, DN_QKV), F32)],
        name="dn_conv_bwd", compiler_params=_params(("parallel",)))(proj, conv_w, dqkvn)


def _expanders():
    eb = np.zeros((LANES, DN_WIDTH), np.float32)
    ea = np.zeros((LANES, DN_WIDTH), np.float32)
    for h in range(DN_HEADS):
        eb[h, h * DN_DIM:(h + 1) * DN_DIM] = 1.0
        ea[DN_HEADS + h, h * DN_DIM:(h + 1) * DN_DIM] = 1.0
    return jnp.asarray(eb), jnp.asarray(ea)


def _dn_gate_args(a_log, dt_bias):
    eb, ea = _expanders()
    alog = jnp.repeat(a_log.reshape(1, DN_HEADS), DN_DIM, axis=1)
    dtb = jnp.repeat(dt_bias.reshape(1, DN_HEADS), DN_DIM, axis=1)
    return eb, ea, alog, dtb


def _dn_gate_specs(tm):
    return [_row(tm, LANES, P_BA // LANES), _full((LANES, DN_WIDTH)), _full((LANES, DN_WIDTH)),
            _full((1, DN_WIDTH)), _full((1, DN_WIDTH))]


def _dn_gate_fn(ba, eb, ea, alog, dtb):
    beta = jax.nn.sigmoid(_dot(ba, eb, hi=True))
    g = -jnp.exp(alog) * jax.nn.softplus(_dot(ba, ea, hi=True) + dtb)
    return beta, g


def _dn_gate_fwd(proj, a_log, dt_bias, tm=512):
    S = proj.shape[0]
    args = _dn_gate_args(a_log, dt_bias)

    def body(ba_ref, eb_ref, ea_ref, al_ref, dt_ref, beta_ref, g_ref):
        beta, g = _dn_gate_fn(ba_ref[...], eb_ref[...], ea_ref[...], al_ref[...], dt_ref[...])
        beta_ref[...] = beta
        g_ref[...] = g

    return pl.pallas_call(
        body, grid=(S // tm,), in_specs=_dn_gate_specs(tm), out_specs=[_row(tm, DN_WIDTH), _row(tm, DN_WIDTH)],
        out_shape=[jax.ShapeDtypeStruct((S, DN_WIDTH), F32), jax.ShapeDtypeStruct((S, DN_WIDTH), F32)],
        name="dn_gate_fwd", compiler_params=_params(("parallel",)))(proj, *args)


def _dn_gate_bwd(proj, a_log, dt_bias, dbeta, dg, tm=512):
    S = proj.shape[0]
    args = _dn_gate_args(a_log, dt_bias)

    def body(ba_ref, eb_ref, ea_ref, al_ref, dt_ref, dbeta_ref, dg_ref, dba_ref, dal_ref, ddt_ref):
        eb, ea = eb_ref[...], ea_ref[...]
        _, vjp = jax.vjp(lambda ba, al, dt: _dn_gate_fn(ba, eb, ea, al, dt), ba_ref[...], al_ref[...], dt_ref[...])
        dba, dal, ddt = vjp((dbeta_ref[...], dg_ref[...]))
        dba_ref[...] = dba.astype(BF16)

        @pl.when(pl.program_id(0) == 0)
        def _():
            dal_ref[...] = jnp.zeros_like(dal_ref)
            ddt_ref[...] = jnp.zeros_like(ddt_ref)

        dal_ref[...] += dal
        ddt_ref[...] += ddt

    return pl.pallas_call(
        body, grid=(S // tm,), in_specs=_dn_gate_specs(tm) + [_row(tm, DN_WIDTH), _row(tm, DN_WIDTH)],
        out_specs=[_row(tm, LANES), _full((1, DN_WIDTH)), _full((1, DN_WIDTH))],
        out_shape=[jax.ShapeDtypeStruct((S, LANES), BF16), jax.ShapeDtypeStruct((1, DN_WIDTH), F32),
                   jax.ShapeDtypeStruct((1, DN_WIDTH), F32)],
        name="dn_gate_bwd", compiler_params=_params(("arbitrary",)))(proj, *args, dbeta, dg)


PREP_GROUPS = 4
PREP_CHUNKS = GROUP * PREP_GROUPS


def _dn_prep_specs():
    rows = PREP_CHUNKS * CHUNK
    q = pl.BlockSpec((rows, LANES), lambda h, c: (c, h))
    k = pl.BlockSpec((rows, LANES), lambda h, c: (c, DN_HEADS + h))
    v = pl.BlockSpec((rows, LANES), lambda h, c: (c, 2 * DN_HEADS + h))
    qk = pl.BlockSpec((1, rows, CHUNK), lambda h, c: (h, c, 0))
    egl = pl.BlockSpec((1, PREP_CHUNKS, 1, LANES), lambda h, c: (h, c, 0, 0))
    return q, k, v, qk, egl


def _dn_prep_fwd(qkvn, g, beta):
    S = qkvn.shape[0]
    nc = S // CHUNK
    q, k, v, qks, egl = _dn_prep_specs()

    def body(q_ref, k_ref, v_ref, g_ref, b_ref, u_ref, w_ref, qe_ref, kd_ref, qk_ref, egl_ref):
        rows = PREP_CHUNKS * CHUNK
        grp = (PREP_GROUPS, GROUP_ROWS, LANES)
        chk = (PREP_CHUNKS, CHUNK, LANES)
        q, k, g = q_ref[...], k_ref[...], g_ref[...]
        u, w, qe, kd = _dn_group(q.reshape(grp), k.reshape(grp), v_ref[...].reshape(grp), g.reshape(grp),
                                 b_ref[...].reshape(grp))
        u_ref[...] = u.reshape(rows, LANES)
        w_ref[...] = w.reshape(rows, LANES)
        qe_ref[...] = qe.reshape(rows, LANES)
        kd_ref[...] = kd.reshape(rows, LANES)
        qk, e = _dn_chunk(q.reshape(chk), k.reshape(chk), g.reshape(chk))
        qk_ref[0] = qk.reshape(rows, CHUNK)
        egl_ref[0] = e

    wide = jax.ShapeDtypeStruct((S, DN_WIDTH), F32)
    return pl.pallas_call(
        body, grid=(DN_HEADS, nc // PREP_CHUNKS), in_specs=[q, k, v, q, q],
        out_specs=[q, q, q, q, qks, egl],
        out_shape=[wide, wide, wide, wide, jax.ShapeDtypeStruct((DN_HEADS, S, CHUNK), F32),
                   jax.ShapeDtypeStruct((DN_HEADS, nc, 1, LANES), F32)],
        name="dn_prep_fwd", compiler_params=_params(("parallel", "parallel")))(qkvn, qkvn, qkvn, g, beta)


def _dn_prep_bwd(qkvn, g, beta, du, dw, dqe, dkd, dqk, degl):
    S = qkvn.shape[0]
    nc = S // CHUNK
    q, k, v, qks, egl = _dn_prep_specs()

    def body(q_ref, k_ref, v_ref, g_ref, b_ref, du_ref, dw_ref, dqe_ref, dkd_ref, dqk_ref, degl_ref,
             dq_ref, dk_ref, dv_ref, dg_ref, db_ref):
        rows = PREP_CHUNKS * CHUNK
        grp = (PREP_GROUPS, GROUP_ROWS, LANES)
        chk = (PREP_CHUNKS, CHUNK, LANES)
        q, k, g = q_ref[...], k_ref[...], g_ref[...]
        _, vjp = jax.vjp(_dn_group, q.reshape(grp), k.reshape(grp), v_ref[...].reshape(grp), g.reshape(grp),
                         b_ref[...].reshape(grp))
        dq, dk, dv, dg, db = vjp((du_ref[...].reshape(grp), dw_ref[...].reshape(grp), dqe_ref[...].reshape(grp),
                                  dkd_ref[...].reshape(grp)))
        _, vjp = jax.vjp(_dn_chunk, q.reshape(chk), k.reshape(chk), g.reshape(chk))
        dq2, dk2, dg2 = vjp((dqk_ref[0].reshape(PREP_CHUNKS, CHUNK, CHUNK), degl_ref[0]))
        dq_ref[...] = dq.reshape(rows, LANES) + dq2.reshape(rows, LANES)
        dk_ref[...] = dk.reshape(rows, LANES) + dk2.reshape(rows, LANES)
        dg_ref[...] = dg.reshape(rows, LANES) + dg2.reshape(rows, LANES)
        dv_ref[...] = dv.reshape(rows, LANES)
        db_ref[...] = db.reshape(rows, LANES)

    wide = jax.ShapeDtypeStruct((S, DN_WIDTH), F32)
    return pl.pallas_call(
        body, grid=(DN_HEADS, nc // PREP_CHUNKS), in_specs=[q, k, v, q, q, q, q, q, q, qks, egl],
        out_specs=[q] * 5, out_shape=[wide] * 5,
        name="dn_prep_bwd", compiler_params=_params(("parallel", "parallel")),
    )(qkvn, qkvn, qkvn, g, beta, du, dw, dqe, dkd, dqk, degl)


def _dn_scan_specs(nc, reverse):
    def cidx(c):
        return nc - 1 - c if reverse else c

    hc = pl.BlockSpec((CHUNK, DN_WIDTH), lambda c: (cidx(c), 0))
    qk = pl.BlockSpec((DN_HEADS, CHUNK, CHUNK), lambda c: (0, cidx(c), 0))
    egl = pl.BlockSpec((DN_HEADS, 1, 1, LANES), lambda c: (0, cidx(c), 0, 0))
    st = pl.BlockSpec((DN_HEADS, 1, DN_DIM, DN_DIM), lambda c: (0, cidx(c), 0, 0))
    return hc, qk, egl, st


def _heads(ref):
    return jnp.stack([ref[:, pl.ds(h * DN_DIM, DN_DIM)] for h in range(DN_HEADS)])


def _dn_scan_fwd(u, w, qe, kd, qk, egl):
    S = u.shape[0]
    nc = S // CHUNK
    hc, qks, egls, st = _dn_scan_specs(nc, False)

    def body(u_ref, w_ref, qe_ref, kd_ref, qk_ref, egl_ref, o_ref, st_ref, s_scr):
        @pl.when(pl.program_id(0) == 0)
        def _():
            s_scr[...] = jnp.zeros_like(s_scr)

        s = s_scr[...]
        st_ref[:, 0] = s
        s_new, o = _dn_step(s, _heads(u_ref), _heads(w_ref), _heads(qe_ref), _heads(kd_ref), qk_ref[...],
                            egl_ref[:, 0])
        for h in range(DN_HEADS):
            o_ref[:, pl.ds(h * DN_DIM, DN_DIM)] = o[h]
        s_scr[...] = s_new

    return pl.pallas_call(
        body, grid=(nc,), in_specs=[hc, hc, hc, hc, qks, egls], out_specs=[hc, st],
        out_shape=[jax.ShapeDtypeStruct((S, DN_WIDTH), F32), jax.ShapeDtypeStruct((DN_HEADS, nc, DN_DIM, DN_DIM), F32)],
        scratch_shapes=[pltpu.VMEM((DN_HEADS, DN_DIM, DN_DIM), F32)], name="dn_scan_fwd",
        compiler_params=_params(("arbitrary",)))(u, w, qe, kd, qk, egl)


def _dn_scan_bwd(u, w, qe, kd, qk, egl, states, do):
    S = u.shape[0]
    nc = S // CHUNK
    hc, qks, egls, st = _dn_scan_specs(nc, True)

    def body(u_ref, w_ref, qe_ref, kd_ref, qk_ref, egl_ref, st_ref, do_ref,
             du_ref, dw_ref, dqe_ref, dkd_ref, dqk_ref, degl_ref, ds_scr):
        @pl.when(pl.program_id(0) == 0)
        def _():
            ds_scr[...] = jnp.zeros_like(ds_scr)

        _, vjp = jax.vjp(_dn_step, st_ref[:, 0], _heads(u_ref), _heads(w_ref), _heads(qe_ref), _heads(kd_ref),
                         qk_ref[...], egl_ref[:, 0])
        ds, du, dw, dqe, dkd, dqk, degl = vjp((ds_scr[...], _heads(do_ref)))
        ds_scr[...] = ds
        dqk_ref[...] = dqk
        degl_ref[:, 0] = degl
        for h in range(DN_HEADS):
            cols = pl.ds(h * DN_DIM, DN_DIM)
            du_ref[:, cols] = du[h]
            dw_ref[:, cols] = dw[h]
            dqe_ref[:, cols] = dqe[h]
            dkd_ref[:, cols] = dkd[h]

    wide = jax.ShapeDtypeStruct((S, DN_WIDTH), F32)
    return pl.pallas_call(
        body, grid=(nc,), in_specs=[hc, hc, hc, hc, qks, egls, st, hc],
        out_specs=[hc, hc, hc, hc, qks, egls],
        out_shape=[wide, wide, wide, wide, jax.ShapeDtypeStruct((DN_HEADS, S, CHUNK), F32),
                   jax.ShapeDtypeStruct((DN_HEADS, nc, 1, LANES), F32)],
        scratch_shapes=[pltpu.VMEM((DN_HEADS, DN_DIM, DN_DIM), F32)], name="dn_scan_bwd",
        compiler_params=_params(("arbitrary",)))(u, w, qe, kd, qk, egl, states, do)


def _dn_out_fwd(o, proj, gain, tm=512):
    S = o.shape[0]

    def body(o_ref, z_ref, g_ref, y_ref):
        y_ref[...] = _dn_out(o_ref[...], z_ref[...], g_ref[...]).astype(BF16)

    hs = pl.BlockSpec((tm, LANES), lambda i, h: (i, h))
    zs = pl.BlockSpec((tm, LANES), lambda i, h: (i, P_Z // LANES + h))
    return pl.pallas_call(
        body, grid=(S // tm, DN_HEADS), in_specs=[hs, zs, _full((1, DN_DIM))], out_specs=hs,
        out_shape=jax.ShapeDtypeStruct((S, DN_WIDTH), BF16), name="dn_out_fwd",
        compiler_params=_params(("parallel", "parallel")))(o, proj, gain)


def _dn_out_bwd(o, proj, gain, dy, tm=512):
    S = o.shape[0]

    def body(o_ref, z_ref, g_ref, dy_ref, do_ref, dz_ref, dg_ref):
        _, vjp = jax.vjp(_dn_out, o_ref[...], z_ref[...], g_ref[...])
        do, dz, dg = vjp(dy_ref[...])
        do_ref[...] = do
        dz_ref[...] = dz.astype(BF16)

        @pl.when((pl.program_id(0) == 0) & (pl.program_id(1) == 0))
        def _():
            dg_ref[...] = jnp.zeros_like(dg_ref)

        dg_ref[...] += dg

    hs = pl.BlockSpec((tm, LANES), lambda i, h: (i, h))
    zs = pl.BlockSpec((tm, LANES), lambda i, h: (i, P_Z // LANES + h))
    return pl.pallas_call(
        body, grid=(S // tm, DN_HEADS), in_specs=[hs, zs, _full((1, DN_DIM)), hs],
        out_specs=[hs, hs, _full((1, DN_DIM))],
        out_shape=[jax.ShapeDtypeStruct((S, DN_WIDTH), F32), jax.ShapeDtypeStruct((S, DN_WIDTH), BF16),
                   jax.ShapeDtypeStruct((1, DN_DIM), F32)],
        name="dn_out_bwd", compiler_params=_params(("arbitrary", "arbitrary")))(o, proj, gain, dy)


def _rel_buckets():
    qi = np.arange(BLOCK)[:, None]
    kj = np.arange(2 * BLOCK)[None, :]
    n = np.maximum(BLOCK + qi - kj, 0)
    max_exact = REL_BUCKETS // 2
    nf = np.maximum(n, 1).astype(np.float32)
    large = max_exact + (np.log(nf / np.float32(max_exact)) / np.float32(math.log(REL_MAX_DIST / max_exact))
                         * np.float32(REL_BUCKETS - max_exact)).astype(np.int32)
    large = np.minimum(large, REL_BUCKETS - 1)
    return np.where(n < max_exact, n, large).astype(np.int32)


def _bias_fwd(rel_bias):
    buckets = jnp.asarray(_rel_buckets())

    def body(rb_ref, bk_ref, o_ref):
        bk = bk_ref[...]
        for h in range(SWA_HEADS):
            acc = jnp.zeros((BLOCK, 2 * BLOCK), F32)
            for b in range(REL_BUCKETS):
                acc = jnp.where(bk == b, rb_ref[b, h], acc)
            o_ref[h] = acc

    return pl.pallas_call(
        body, in_specs=[pl.BlockSpec(memory_space=pltpu.SMEM), pl.BlockSpec(memory_space=pltpu.VMEM)],
        out_specs=pl.BlockSpec(memory_space=pltpu.VMEM),
        out_shape=jax.ShapeDtypeStruct((SWA_HEADS, BLOCK, 2 * BLOCK), F32), name="swa_bias_fwd",
        compiler_params=_params())(rel_bias, buckets)


def _bias_bwd(dbias):
    buckets = jnp.asarray(_rel_buckets())

    def body(d_ref, bk_ref, o_ref):
        bk = bk_ref[...]
        lane = lax.broadcasted_iota(jnp.int32, (1, LANES), 1)
        for h in range(SWA_HEADS):
            d = d_ref[h]
            row = jnp.zeros((1, LANES), F32)
            for b in range(REL_BUCKETS):
                part = jnp.sum(jnp.where(bk == b, d, 0.0), axis=1, keepdims=True)
                row = jnp.where(lane == b, jnp.sum(part, axis=0, keepdims=True), row)
            o_ref[h:h + 1, :] = row

    return pl.pallas_call(
        body, in_specs=[pl.BlockSpec(memory_space=pltpu.VMEM), pl.BlockSpec(memory_space=pltpu.VMEM)],
        out_specs=pl.BlockSpec(memory_space=pltpu.VMEM),
        out_shape=jax.ShapeDtypeStruct((SWA_HEADS, LANES), F32), name="swa_bias_bwd",
        compiler_params=_params())(dbias, buckets)


def _swa_mask(n):
    qi = lax.broadcasted_iota(jnp.int32, (BLOCK, 2 * BLOCK), 0)
    kj = lax.broadcasted_iota(jnp.int32, (BLOCK, 2 * BLOCK), 1)
    dist = BLOCK + qi - kj
    return (dist >= 0) & (dist < WINDOW) & ((n > 0) | (kj >= BLOCK))


def _swa_in_specs():
    q = pl.BlockSpec((BLOCK, SWA_WIDTH), lambda n: (n, P_SQ // SWA_WIDTH))
    kc = pl.BlockSpec((BLOCK, SWA_KVW), lambda n: (n, P_SK // SWA_KVW))
    kp = pl.BlockSpec((BLOCK, SWA_KVW), lambda n: (jnp.maximum(n - 1, 0), P_SK // SWA_KVW))
    vc = pl.BlockSpec((BLOCK, SWA_KVW), lambda n: (n, P_SV // SWA_KVW))
    vp = pl.BlockSpec((BLOCK, SWA_KVW), lambda n: (jnp.maximum(n - 1, 0), P_SV // SWA_KVW))
    small = [_full((1, SWA_DIM)), _full((1, SWA_DIM)), _full((1, SWA_HEADS)),
             _full((SWA_HEADS, BLOCK, 2 * BLOCK))]
    return [q, kp, kc, vp, vc] + small


def _swa_load(q_ref, kp_ref, kc_ref, vp_ref, vc_ref, s_ref):
    q = jnp.stack([q_ref[:, pl.ds(h * SWA_DIM, SWA_DIM)] for h in range(SWA_HEADS)])
    kbands, vbands = [], []
    for kv in range(SWA_KV):
        cols = pl.ds(kv * SWA_DIM, SWA_DIM)
        kbands += [jnp.concatenate([kp_ref[:, cols], kc_ref[:, cols]], axis=0)] * SWA_GROUP
        vbands += [jnp.concatenate([vp_ref[:, cols], vc_ref[:, cols]], axis=0)] * SWA_GROUP
    sinks = jnp.stack([s_ref[:, pl.ds(h, 1)] for h in range(SWA_HEADS)])
    return q, jnp.stack(kbands), jnp.stack(vbands), sinks


def _swa_fwd(proj, q_gain, k_gain, sinks, bias):
    S = proj.shape[0]

    def body(q_ref, kp_ref, kc_ref, vp_ref, vc_ref, qg_ref, kg_ref, s_ref, bias_ref, y_ref):
        mask = _swa_mask(pl.program_id(0))
        q, kband, vband, sk = _swa_load(q_ref, kp_ref, kc_ref, vp_ref, vc_ref, s_ref)
        out = _swa_block(q, kband, vband, qg_ref[...], kg_ref[...], sk, bias_ref[...], mask)
        for h in range(SWA_HEADS):
            y_ref[:, pl.ds(h * SWA_DIM, SWA_DIM)] = out[h].astype(BF16)

    return pl.pallas_call(
        body, grid=(S // BLOCK,), in_specs=_swa_in_specs(),
        out_specs=pl.BlockSpec((BLOCK, SWA_WIDTH), lambda n: (n, 0)),
        out_shape=jax.ShapeDtypeStruct((S, SWA_WIDTH), BF16), name="swa_fwd",
        compiler_params=_params(("parallel",)))(proj, proj, proj, proj, proj, q_gain, k_gain, sinks, bias)


def _swa_bwd(proj, q_gain, k_gain, sinks, bias, dy):
    S = proj.shape[0]

    def body(q_ref, kp_ref, kc_ref, vp_ref, vc_ref, qg_ref, kg_ref, s_ref, bias_ref, dy_ref,
             dq_ref, dk_ref, dv_ref, dqg_ref, dkg_ref, ds_ref, dbias_ref):
        n = pl.program_id(0)
        mask = _swa_mask(n)

        @pl.when(n == 0)
        def _():
            for r in (dk_ref, dv_ref, dqg_ref, dkg_ref, ds_ref, dbias_ref):
                r[...] = jnp.zeros_like(r)

        cur = pl.ds(pl.multiple_of(n * BLOCK, BLOCK), BLOCK)
        prev = pl.ds(pl.multiple_of(jnp.maximum(n - 1, 0) * BLOCK, BLOCK), BLOCK)
        q, kband, vband, sk = _swa_load(q_ref, kp_ref, kc_ref, vp_ref, vc_ref, s_ref)
        _, vjp = jax.vjp(lambda q, kb, vb, qg, kg, sk, bs: _swa_block(q, kb, vb, qg, kg, sk, bs, mask),
                         q, kband, vband, qg_ref[...], kg_ref[...], sk, bias_ref[...])
        dy = jnp.stack([dy_ref[:, pl.ds(h * SWA_DIM, SWA_DIM)] for h in range(SWA_HEADS)])
        dq, dkb, dvb, dqg, dkg, dsk, dbs = vjp(dy)
        for h in range(SWA_HEADS):
            dq_ref[:, pl.ds(h * SWA_DIM, SWA_DIM)] = dq[h].astype(BF16)
            ds_ref[:, pl.ds(h, 1)] += dsk[h]
        dbias_ref[...] += dbs
        dqg_ref[...] += dqg
        dkg_ref[...] += dkg
        for kv in range(SWA_KV):
            cols = pl.ds(kv * SWA_DIM, SWA_DIM)
            group = range(kv * SWA_GROUP, (kv + 1) * SWA_GROUP)
            dk_kv = sum(dkb[h] for h in group)
            dv_kv = sum(dvb[h] for h in group)
            dk_ref[cur, cols] += dk_kv[BLOCK:]
            dv_ref[cur, cols] += dv_kv[BLOCK:]

            @pl.when(n > 0)
            def _(cols=cols, dk_kv=dk_kv, dv_kv=dv_kv):
                dk_ref[prev, cols] += dk_kv[:BLOCK]
                dv_ref[prev, cols] += dv_kv[:BLOCK]

    return pl.pallas_call(
        body, grid=(S // BLOCK,),
        in_specs=_swa_in_specs() + [pl.BlockSpec((BLOCK, SWA_WIDTH), lambda n: (n, 0))],
        out_specs=[pl.BlockSpec((BLOCK, SWA_WIDTH), lambda n: (n, 0)), _full((S, SWA_KVW)), _full((S, SWA_KVW)),
                   _full((1, SWA_DIM)), _full((1, SWA_DIM)), _full((1, SWA_HEADS)),
                   _full((SWA_HEADS, BLOCK, 2 * BLOCK))],
        out_shape=[jax.ShapeDtypeStruct((S, SWA_WIDTH), BF16), jax.ShapeDtypeStruct((S, SWA_KVW), F32),
                   jax.ShapeDtypeStruct((S, SWA_KVW), F32), jax.ShapeDtypeStruct((1, SWA_DIM), F32),
                   jax.ShapeDtypeStruct((1, SWA_DIM), F32), jax.ShapeDtypeStruct((1, SWA_HEADS), F32),
                   jax.ShapeDtypeStruct((SWA_HEADS, BLOCK, 2 * BLOCK), F32)],
        name="swa_bwd", compiler_params=_params(("arbitrary",)),
    )(proj, proj, proj, proj, proj, q_gain, k_gain, sinks, bias, dy)


def _position():
    return lax.axis_index("x"), lax.axis_index("y"), lax.axis_index("c")


def _all_gather(shards, name="all_gather_weights"):
    na = len(shards)

    def body(*refs):
        x_refs, out_refs = refs[:na], refs[na:2 * na]
        send_sems, recv_sems, local_sems = refs[2 * na:]
        x, y, c = _position()
        me, sibling = (x, y, c), (x, y, 1 - c)
        chips = [(1 - x, y), (x, 1 - y), (1 - x, 1 - y)]

        def copy(a, k, block, to, own=False):
            px, py, pc = block
            slot = out_refs[a].at[4 * px + 2 * py + pc]
            return pltpu.make_async_remote_copy(
                src_ref=x_refs[a] if own else slot, dst_ref=slot, send_sem=send_sems.at[7 * a + k],
                recv_sem=recv_sems.at[7 * a + k], device_id=to, device_id_type=MESH_ID)

        mine = [pltpu.make_async_copy(x_refs[a], out_refs[a].at[4 * x + 2 * y + c], local_sems.at[a])
                for a in range(na)]
        for cp in mine:
            cp.start()
        first = []
        for a in range(na):
            first.append(copy(a, 0, me, sibling, own=True))
            first += [copy(a, 1 + j, me, (*chip, c), own=True) for j, chip in enumerate(chips)]
        for cp in first:
            cp.start()
        passed = []
        for j, chip in enumerate(chips):
            for a in range(na):
                copy(a, 1 + j, (*chip, c), me).wait_recv()
                passed.append(copy(a, 4 + j, (*chip, c), sibling))
                passed[-1].start()
        for a in range(na):
            copy(a, 0, sibling, me).wait_recv()
            for j, chip in enumerate(chips):
                copy(a, 4 + j, (*chip, 1 - c), me).wait_recv()
        for cp in first + passed:
            cp.wait_send()
        for cp in mine:
            cp.wait()

    return pl.pallas_call(
        body, in_specs=[pl.BlockSpec(memory_space=pl.ANY)] * na, out_specs=[pl.BlockSpec(memory_space=pl.ANY)] * na,
        out_shape=[jax.ShapeDtypeStruct((N_DEV,) + s.shape, s.dtype) for s in shards],
        scratch_shapes=[pltpu.SemaphoreType.DMA((7 * na,)), pltpu.SemaphoreType.DMA((7 * na,)),
                        pltpu.SemaphoreType.DMA((na,))],
        name=name)(*shards)


def _exchange(blocks):
    na = len(blocks)

    def body(*refs):
        in_refs, out_refs = refs[:na], refs[na:2 * na]
        send_sems, recv_sems, local_sems = refs[2 * na:]
        x, y, c = _position()
        me = 4 * x + 2 * y + c
        local = [pltpu.make_async_copy(in_refs[a].at[me], out_refs[a].at[me], local_sems.at[a]) for a in range(na)]
        for cp in local:
            cp.start()
        sends, recvs = [], []
        for k in range(1, N_DEV):
            px, py, pc = x ^ (k >> 2), y ^ ((k >> 1) & 1), c ^ (k & 1)
            peer = 4 * px + 2 * py + pc
            for a in range(na):
                sems = dict(send_sem=send_sems.at[na * (k - 1) + a], recv_sem=recv_sems.at[na * (k - 1) + a],
                            device_id=(px, py, pc), device_id_type=MESH_ID)
                sends.append(pltpu.make_async_remote_copy(src_ref=in_refs[a].at[peer], dst_ref=out_refs[a].at[me], **sems))
                recvs.append(pltpu.make_async_remote_copy(src_ref=in_refs[a].at[me], dst_ref=out_refs[a].at[peer], **sems))
        for cp in sends:
            cp.start()
        for cp in recvs:
            cp.wait_recv()
        for cp in sends:
            cp.wait_send()
        for cp in local:
            cp.wait()

    return pl.pallas_call(
        body, in_specs=[pl.BlockSpec(memory_space=pl.ANY)] * na, out_specs=[pl.BlockSpec(memory_space=pl.ANY)] * na,
        out_shape=[jax.ShapeDtypeStruct(b.shape, b.dtype) for b in blocks],
        scratch_shapes=[pltpu.SemaphoreType.DMA((7 * na,)), pltpu.SemaphoreType.DMA((7 * na,)),
                        pltpu.SemaphoreType.DMA((na,))],
        name="exchange_grads")(*blocks)


_HBM = pl.BlockSpec(memory_space=pltpu.HBM)
_SEM = pl.BlockSpec(memory_space=pltpu.SEMAPHORE)
_DATAFLOW = pltpu.SideEffectType.DATAFLOW_SIDE_EFFECTING


def _peers(x, y, c):
    out = []
    for k in range(1, N_DEV):
        px, py, pc = x ^ (k >> 2), y ^ ((k >> 1) & 1), c ^ (k & 1)
        out.append(((px, py, pc), 4 * px + 2 * py + pc))
    return out


def _split_copies(src_refs, land_refs, send_sems, recv_sems, scatter):
    x, y, c = _position()
    me = 4 * x + 2 * y + c
    sends, recvs = [], []
    for k, (peer_id, peer) in enumerate(_peers(x, y, c)):
        for a, (src, land) in enumerate(zip(src_refs, land_refs)):
            sems = dict(send_sem=send_sems.at[7 * a + k], recv_sem=recv_sems.at[7 * a + k],
                        device_id=peer_id, device_id_type=MESH_ID)
            mine = src.at[peer] if scatter else src
            sends.append(pltpu.make_async_remote_copy(src_ref=mine, dst_ref=land.at[me], **sems))
            recvs.append(pltpu.make_async_remote_copy(src_ref=mine, dst_ref=land.at[peer], **sems))
    return sends, recvs


def _exchange_start(srcs, scatter, name, after=None):
    na = len(srcs)
    lands = [lax.empty(s.shape if scatter else (N_DEV,) + s.shape, s.dtype) for s in srcs]
    extra = [] if after is None else [after]

    def body(*refs):
        src_refs, land_refs = refs[:na], refs[na:2 * na]
        send_sems, recv_sems = refs[2 * na + len(extra)], refs[2 * na + len(extra) + 1]
        token = refs[-1]
        sends, _ = _split_copies(src_refs, land_refs, send_sems, recv_sems, scatter)
        for cp in sends:
            cp.start()
        token[...] = jnp.zeros_like(token)

    hbm = lambda a: pltpu.HBM(a.shape, a.dtype)
    out = pl.pallas_call(
        body, name=name,
        out_shape=(pltpu.SemaphoreType.DMA((7 * na,)), pltpu.SemaphoreType.DMA((7 * na,)),
                   *[hbm(s) for s in srcs], *[hbm(l) for l in lands], jax.ShapeDtypeStruct((8, LANES), F32)),
        in_specs=[_HBM] * (2 * na) + [pl.BlockSpec(memory_space=pl.ANY)] * len(extra),
        out_specs=(_SEM, _SEM, *[_HBM] * (2 * na), pl.BlockSpec(memory_space=pltpu.VMEM)),
        input_output_aliases={i: 2 + i for i in range(2 * na)},
        compiler_params=pltpu.CompilerParams(has_side_effects=_DATAFLOW),
    )(*[pltpu.with_memory_space_constraint(s, pltpu.HBM) for s in srcs],
      *[pltpu.with_memory_space_constraint(l, pltpu.HBM) for l in lands], *extra)
    return (out[0], out[1], list(out[2:2 + na]), list(out[2 + na:2 + 2 * na])), out[-1]


def _exchange_wait(handle, after, scatter, name):
    send_sems, recv_sems, srcs, lands = handle
    na = len(srcs)

    def body(*refs):
        src_refs, land_refs = refs[:na], refs[na:2 * na]
        s_sems, r_sems = refs[2 * na], refs[2 * na + 1]
        sends, recvs = _split_copies(src_refs, land_refs, s_sems, r_sems, scatter)
        for cp in sends:
            cp.wait_send()
        for cp in recvs:
            cp.wait_recv()

    hbm = lambda a: pltpu.HBM(a.shape, a.dtype)
    out = pl.pallas_call(
        body, name=name, out_shape=(*[hbm(s) for s in srcs], *[hbm(l) for l in lands]),
        in_specs=[_HBM] * (2 * na) + [_SEM, _SEM, pl.BlockSpec(memory_space=pl.ANY)],
        out_specs=tuple([_HBM] * (2 * na)), input_output_aliases={i: i for i in range(2 * na)},
        compiler_params=pltpu.CompilerParams(has_side_effects=_DATAFLOW),
    )(*srcs, *lands, send_sems, recv_sems, after)
    return list(out[:na]), list(out[na:])


def _own_slot(landed, own):
    me = 4 * lax.axis_index("x") + 2 * lax.axis_index("y") + lax.axis_index("c")
    return lax.dynamic_update_slice_in_dim(landed, own[None], me, axis=0)


def _adam_update(parts, w, m, v, name, tr=256):
    r, c = w.shape
    tr = _pick_rows(r, tr)
    cp = parts.shape[2]

    def body(p_ref, w_ref, m_ref, v_ref, g_ref, d_ref, nm_ref, nv_ref):
        g = p_ref[0, :, pl.ds(0, c)].astype(F32)
        for i in range(1, N_DEV):
            g = g + p_ref[i, :, pl.ds(0, c)].astype(F32)
        delta, nm, nv = _adamw(w_ref[...], g, m_ref[...], v_ref[...])
        g_ref[...] = g
        d_ref[...] = delta
        nm_ref[...] = nm
        nv_ref[...] = nv

    rs = pl.BlockSpec((tr, c), lambda i: (i, 0))
    return pl.pallas_call(
        body, grid=(r // tr,), in_specs=[pl.BlockSpec((N_DEV, tr, cp), lambda i: (0, i, 0)), rs, rs, rs],
        out_specs=[rs] * 4, out_shape=[jax.ShapeDtypeStruct((r, c), F32)] * 4, name=name,
        compiler_params=_params(("parallel",)))(parts, w, m, v)


def _pick_rows(rows, target):
    if rows <= target:
        return rows
    t = target
    while t >= 16:
        if rows % t == 0:
            return t
        t -= 16
    return rows


BIG = ("w_in", "w_branch_dn", "w_branch_swa", "w_out", "w_gate", "w_up", "w_down")
IN_SHARD, IN_WIRE = D_IN // N_DEV, 640
FF_SHARD, FF_WIRE = D_FF // N_DEV, 384
D_FFP = N_DEV * FF_WIRE
BIG_SHAPES = {"w_in": ((D_MODEL, IN_SHARD), (D_MODEL, IN_WIRE)),
              "w_branch_dn": ((DN_WIDTH, LANES), (DN_WIDTH, LANES)),
              "w_branch_swa": ((SWA_WIDTH, LANES), (SWA_WIDTH, LANES)),
              "w_out": ((LANES, D_MODEL), (LANES, D_MODEL)),
              "w_gate": ((D_MODEL, FF_SHARD), (D_MODEL, FF_WIRE)),
              "w_up": ((D_MODEL, FF_SHARD), (D_MODEL, FF_WIRE)),
              "w_down": ((FF_SHARD, D_MODEL), (FF_WIRE, D_MODEL))}
CONV_SHARD, CONV_WIRE = (DN_CONV, DN_QKV // N_DEV), (8, 256)


def _pad_to(a, shape):
    return jnp.pad(a, [(0, t - s) for s, t in zip(a.shape, shape)])


_IN_SEGS = ((R_GATE, 2048, P_GATE), (R_QKV, DN_QKV, P_QKV), (R_Z, DN_WIDTH, P_Z), (R_SQ, SWA_WIDTH, P_SQ),
            (R_SK, SWA_KVW, P_SK), (R_SV, SWA_KVW, P_SV), (R_B, 8, P_BA))


def _w_in_from_blocks(blocks):
    parts = []
    for rs, n, _ in _IN_SEGS:
        for dev in range(N_DEV):
            lo, hi = max(rs, IN_SHARD * dev), min(rs + n, IN_SHARD * (dev + 1))
            if lo < hi:
                parts.append(blocks[dev, :, lo - IN_SHARD * dev:hi - IN_SHARD * dev])
    parts.append(jnp.zeros((blocks.shape[1], P_WIDTH - P_BA - 8), blocks.dtype))
    return jnp.concatenate(parts, axis=1)


def _w_in_to_blocks(g):
    out = []
    for dev in range(N_DEV):
        parts = []
        for rs, n, ps in sorted(_IN_SEGS):
            lo, hi = max(rs, IN_SHARD * dev), min(rs + n, IN_SHARD * (dev + 1))
            if lo < hi:
                parts.append(g[:, ps + lo - rs:ps + hi - rs])
        parts.append(jnp.zeros((g.shape[0], IN_WIRE - IN_SHARD), g.dtype))
        out.append(jnp.concatenate(parts, axis=1))
    return jnp.stack(out)


def _cols_join(blocks):
    return jnp.concatenate([blocks[d] for d in range(N_DEV)], axis=1)


def _cols_split(full):
    c = full.shape[1] // N_DEV
    return jnp.stack([full[:, d * c:(d + 1) * c] for d in range(N_DEV)])


SMALL = {"attn_norm": (0, (1, D_MODEL)), "ffn_norm": (1, (1, D_MODEL)), "dn_out_norm": (2, (1, DN_DIM)),
         "swa_q_norm": (3, (1, SWA_DIM)), "swa_k_norm": (4, (1, SWA_DIM)), "dn_a_log": (5, (1, DN_HEADS)),
         "dn_dt_bias": (6, (1, DN_HEADS)), "swa_sinks": (7, (1, SWA_HEADS)), "rel_bias": (8, (REL_BUCKETS, SWA_HEADS))}
SMALL_SHEET = (48, D_MODEL)


def _small_pack(grads):
    names = list(SMALL)

    def body(*refs):
        o_ref = refs[-1]
        o_ref[...] = jnp.zeros_like(o_ref)
        for n, ref in zip(names, refs):
            r0, (nr, nc) = SMALL[n]
            o_ref[r0:r0 + nr, 0:nc] = ref[...]

    return pl.pallas_call(
        body, in_specs=[pl.BlockSpec(memory_space=pltpu.VMEM)] * len(names),
        out_specs=pl.BlockSpec(memory_space=pltpu.VMEM), out_shape=jax.ShapeDtypeStruct(SMALL_SHEET, F32),
        name="small_pack", compiler_params=_params())(*[grads[n].reshape(SMALL[n][1]) for n in names])


def _small_update(sheets, w, m, v):
    names = list(SMALL)
    k = len(names)

    def body(*refs):
        p_ref = refs[0]
        ins, outs = refs[1:1 + 3 * k], refs[1 + 3 * k:]
        for t, n in enumerate(names):
            r0, (nr, nc) = SMALL[n]
            g = p_ref[0, r0:r0 + nr, 0:nc]
            for i in range(1, N_DEV):
                g = g + p_ref[i, r0:r0 + nr, 0:nc]
            delta, nm, nv = _adamw(ins[t][...], g, ins[k + t][...], ins[2 * k + t][...])
            for kind, val in enumerate((g, delta, nm, nv)):
                outs[kind * k + t][...] = val

    shapes = [jax.ShapeDtypeStruct(SMALL[n][1], F32) for n in names]
    vm = pl.BlockSpec(memory_space=pltpu.VMEM)
    res = pl.pallas_call(
        body, in_specs=[vm] * (1 + 3 * k), out_specs=[vm] * (4 * k), out_shape=shapes * 4, name="adam_small",
        compiler_params=_params(),
    )(sheets, *[d[n].reshape(SMALL[n][1]) for d in (w, m, v) for n in names])
    return {n: tuple(res[kind * k + t] for kind in range(4)) for t, n in enumerate(names)}


def kernel(x, attn_norm, w_in, dn_conv, dn_a_log, dn_dt_bias, dn_out_norm, swa_q_norm, swa_k_norm, swa_sinks, rel_bias, w_branch_dn, w_branch_swa, w_out, ffn_norm, w_gate, w_up, w_down, loss_target, m_attn_norm, m_w_in, m_dn_conv, m_dn_a_log, m_dn_dt_bias, m_dn_out_norm, m_swa_q_norm, m_swa_k_norm, m_swa_sinks, m_rel_bias, m_w_branch_dn, m_w_branch_swa, m_w_out, m_ffn_norm, m_w_gate, m_w_up, m_w_down, v_attn_norm, v_w_in, v_dn_conv, v_dn_a_log, v_dn_dt_bias, v_dn_out_norm, v_swa_q_norm, v_swa_k_norm, v_swa_sinks, v_rel_bias, v_w_branch_dn, v_w_branch_swa, v_w_out, v_ffn_norm, v_w_gate, v_w_up, v_w_down):
    args = dict(locals())
    S = x.shape[1]
    xs = x.reshape(S, D_MODEL)
    target = loss_target.reshape(S, D_MODEL)

    w_loc = {n: args[n].reshape(BIG_SHAPES[n][0]) for n in BIG}
    conv_loc = dn_conv.reshape(CONV_SHARD)
    wire = {n: _pad_to(w_loc[n], BIG_SHAPES[n][1]).astype(BF16) for n in BIG}
    first = _all_gather([wire["w_in"], _pad_to(conv_loc, CONV_WIRE)])
    later = [n for n in BIG if n != "w_in"]
    rest_handle, rest_token = _exchange_start([wire[n] for n in later], False, "gather_rest_start", after=first[1])
    w_pad = _w_in_from_blocks(first[0])
    conv_w = jnp.concatenate([first[1][d, :DN_CONV, :CONV_SHARD[1]] for d in range(N_DEV)], axis=1)

    h = _norm_fwd(xs, attn_norm + rest_token[0, 0], "norm1_fwd")
    proj = _mm([(h, w_pad)], "nn", F32, "mm_in", 512, 1664, j_outer=True)
    qkvn = _dn_conv_fwd(proj, conv_w)
    beta, g = _dn_gate_fwd(proj, dn_a_log, dn_dt_bias)
    u, w, qe, kd, qk, egl = _dn_prep_fwd(qkvn, g, beta)
    o, states = _dn_scan_fwd(u, w, qe, kd, qk, egl)
    y_dn = _dn_out_fwd(o, proj, dn_out_norm)
    bias = _bias_fwd(rel_bias)
    y_swa = _swa_fwd(proj, swa_q_norm, swa_k_norm, swa_sinks, bias)
    rest_src, rest_land = _exchange_wait(rest_handle, y_swa, False, "gather_rest_wait")
    G = {n: _own_slot(land, src) for n, src, land in zip(later, rest_src, rest_land)}
    w_bdn = _cols_join(G["w_branch_dn"])
    w_bswa = _cols_join(G["w_branch_swa"])
    w_o = G["w_out"].reshape(D_MODEL, D_MODEL)
    w_d = G["w_down"].reshape(D_FFP, D_MODEL)
    w_g = _cols_join(G["w_gate"])
    w_u = _cols_join(G["w_up"])
    a_dn = _mm([(y_dn, w_bdn)], "nn", F32, "mm_branch_dn", 1024, D_MODEL)
    a_swa = _mm([(y_swa, w_bswa)], "nn", F32, "mm_branch_swa", 1024, D_MODEL)
    merged = _merge_fwd(proj, a_dn, a_swa)
    t_out = _mm([(merged, w_o)], "nn", F32, "mm_out", 1024, D_MODEL)
    x1, h2 = _resid_norm_fwd(xs, t_out, ffn_norm, "norm2_fwd")
    gate = _mm([(h2, w_g)], "nn", F32, "mm_gate", 1024, 1024, j_outer=True)
    up = _mm([(h2, w_u)], "nn", F32, "mm_up", 1024, 1024, j_outer=True)
    act = _act_fwd(gate, up)
    f = _mm([(act, w_d)], "nn", F32, "mm_down", 512, D_MODEL)
    dy, dy_b, loss_local = _loss_fwd_bwd(x1, f, target)

    dact = _mm([(dy_b, w_d)], "nt", F32, "mm_dact", 1024, 1024, j_outer=True)
    g_w_down = _mm([(act, dy_b)], "tn", BF16, "mm_dw_down", 768, D_MODEL, j_outer=True)
    g_w_down = g_w_down.reshape(N_DEV, FF_WIRE, D_MODEL)
    dgate, dup = _act_bwd(gate, up, dact)
    dh2 = _mm([(dgate, w_g), (dup, w_u)], "nt", F32, "mm_dh2", 512, 512, j_outer=True)
    g_w_gate = _cols_split(_mm([(h2, dgate)], "tn", BF16, "mm_dw_gate", D_MODEL, 768))
    g_w_up = _cols_split(_mm([(h2, dup)], "tn", BF16, "mm_dw_up", D_MODEL, 768))
    ffn_handle, ffn_token = _exchange_start([g_w_down, g_w_gate, g_w_up], True, "scatter_ffn_start")
    dx1, dx1_b, g_ffn_norm = _norm_bwd(x1, [dh2], dy, ffn_norm + ffn_token[0, 0], "norm2_bwd")
    dmerged = _mm([(dx1_b, w_o)], "nt", F32, "mm_dmerged", 1024, D_MODEL)
    g_w_out = _mm([(merged, dx1_b)], "tn", BF16, "mm_dw_out", 512, D_MODEL, j_outer=True)
    g_w_out = g_w_out.reshape(N_DEV, LANES, D_MODEL)
    dg0, dg1, da_dn, da_swa = _merge_bwd(proj, a_dn, a_swa, dmerged)
    dy_dn = _mm([(da_dn, w_bdn)], "nt", F32, "mm_dy_dn", 1024, DN_WIDTH)
    dy_swa = _mm([(da_swa, w_bswa)], "nt", F32, "mm_dy_swa", 1024, SWA_WIDTH)
    g_w_bdn = _cols_split(_mm([(y_dn, da_dn)], "tn", BF16, "mm_dw_branch_dn", DN_WIDTH, 512))
    g_w_bswa = _cols_split(_mm([(y_swa, da_swa)], "tn", BF16, "mm_dw_branch_swa", SWA_WIDTH, 512))
    dsq, dsk, dsv, g_q_norm, g_k_norm, g_sinks, dbias = _swa_bwd(proj, swa_q_norm, swa_k_norm, swa_sinks, bias, dy_swa)
    g_rel_bias = _bias_bwd(dbias)[:, :REL_BUCKETS].T
    mix_handle, mix_token = _exchange_start([g_w_out, g_w_bdn, g_w_bswa], True, "scatter_mix_start")
    do, dz, g_out_norm = _dn_out_bwd(o, proj, dn_out_norm + mix_token[0, 0], dy_dn)
    du, dw, dqe, dkd, dqk, degl = _dn_scan_bwd(u, w, qe, kd, qk, egl, states, do)
    dq, dk, dv, dgd, dbeta = _dn_prep_bwd(qkvn, g, beta, du, dw, dqe, dkd, dqk, degl)
    dqkvn = jnp.concatenate([dq, dk, dv], axis=1)
    dba, dal, ddt = _dn_gate_bwd(proj, dn_a_log, dn_dt_bias, dbeta, dgd)
    g_a_log = dal.reshape(DN_HEADS, DN_DIM).sum(axis=1)
    g_dt_bias = ddt.reshape(DN_HEADS, DN_DIM).sum(axis=1)
    dqkv, g_conv = _dn_conv_bwd(proj, conv_w, dqkvn)
    dproj = jnp.concatenate([dg0, dg1, dqkv, dz, dsq, dsk.astype(BF16), dsv.astype(BF16), dba], axis=1)
    g_w_in = _w_in_to_blocks(_mm([(h, dproj)], "tn", BF16, "mm_dw_in", 512, 1664, j_outer=True))
    in_handle, in_token = _exchange_start([g_w_in], True, "scatter_in_start")
    dh = _mm([(dproj, w_pad)], "nt", F32, "mm_dh", 512, D_MODEL)
    dx, _, g_attn_norm = _norm_bwd(xs, [dh], dx1, attn_norm + in_token[0, 0], "norm1_bwd")

    g_small = {"attn_norm": g_attn_norm, "ffn_norm": g_ffn_norm, "rel_bias": g_rel_bias, "dn_out_norm": g_out_norm,
               "swa_q_norm": g_q_norm, "swa_k_norm": g_k_norm, "dn_a_log": g_a_log, "dn_dt_bias": g_dt_bias,
               "swa_sinks": g_sinks}
    sheets, conv_all = _all_gather([_small_pack(g_small), _pad_to(g_conv, (8, DN_QKV))], name="all_gather_small")
    me = 4 * lax.axis_index("x") + 2 * lax.axis_index("y") + lax.axis_index("c")
    recv_small = sheets
    received = {}
    for handle, group, name in ((ffn_handle, ("w_down", "w_gate", "w_up"), "scatter_ffn_wait"),
                                (mix_handle, ("w_out", "w_branch_dn", "w_branch_swa"), "scatter_mix_wait"),
                                (in_handle, ("w_in",), "scatter_in_wait")):
        srcs, lands = _exchange_wait(handle, recv_small, True, name)
        for n, src, land in zip(group, srcs, lands):
            received[n] = _own_slot(land, lax.dynamic_index_in_dim(src, me, 0, keepdims=False))

    outs = {}
    for n in BIG:
        shp = BIG_SHAPES[n][0]
        outs[n] = _adam_update(received[n], w_loc[n], args["m_" + n].reshape(shp), args["v_" + n].reshape(shp),
                               "adam_" + n)
    conv_parts = lax.dynamic_slice(conv_all, (0, 0, me * CONV_SHARD[1]), (N_DEV,) + CONV_SHARD)
    outs["dn_conv"] = _adam_update(conv_parts, conv_loc, m_dn_conv.reshape(CONV_SHARD), v_dn_conv.reshape(CONV_SHARD),
                                   "adam_dn_conv")
    outs.update(_small_update(sheets, {n: args[n] for n in SMALL}, {n: args["m_" + n] for n in SMALL},
                              {n: args["v_" + n] for n in SMALL}))

    names = ("attn_norm", "w_in", "dn_conv", "dn_a_log", "dn_dt_bias", "dn_out_norm", "swa_q_norm", "swa_k_norm",
             "swa_sinks", "rel_bias", "w_branch_dn", "w_branch_swa", "w_out", "ffn_norm", "w_gate", "w_up", "w_down")
    results = []
    for kind in range(4):
        results += [outs[n][kind].reshape(args[n].shape) for n in names]

    loss = lax.psum(loss_local[0, 0], ("x", "y", "c"))
    return (loss, dx.reshape(x.shape), *results)
```

```python
import math

import numpy as np
import jax
import jax.numpy as jnp
from jax import lax
from jax.experimental import pallas as pl
from jax.experimental.pallas import tpu as pltpu

F32 = jnp.float32
BF16 = jnp.bfloat16
HI = lax.Precision.HIGHEST

D_MODEL = 1024
DN_HEADS = 4
DN_DIM = 128
DN_WIDTH = 512
DN_QKV = 1536
DN_CONV = 4
CHUNK = 64
SWA_HEADS = 8
SWA_KV = 2
SWA_GROUP = 4
SWA_DIM = 64
SWA_WIDTH = 512
SWA_KVW = 128
WINDOW = 128
BLOCK = 128
REL_BUCKETS = 32
REL_MAX_DIST = 128
D_FF = 2816
D_IN = 4872
EPS = 1e-6
N_DEV = 8

ADAM_LR = 0.001
ADAM_B1 = 0.9
ADAM_B2 = 0.999
ADAM_EPS = 1e-08
ADAM_WD = 0.01
ADAM_STEP = 10

P_GATE, P_QKV, P_Z, P_SQ, P_SK, P_SV, P_BA = 0, 2048, 3584, 4096, 4608, 4736, 4864
P_WIDTH = 4992
R_QKV, R_Z, R_B, R_A, R_SQ, R_SK, R_SV, R_GATE = 0, 1536, 2048, 2052, 2056, 2568, 2696, 2824

VMEM_LIMIT = 56 * 1024 * 1024
LANES = 128
MESH_ID = pl.DeviceIdType.MESH


def _params(sem=None):
    return pltpu.CompilerParams(dimension_semantics=sem, vmem_limit_bytes=VMEM_LIMIT)


def _pick(dim, target):
    if dim <= target:
        return dim
    t = target - target % LANES
    while t >= LANES:
        if dim % t == 0:
            return t
        t -= LANES
    return dim


_DIMS = {"nn": (((1,), (0,)), ((), ())), "nt": (((1,), (1,)), ((), ())), "tn": (((0,), (0,)), ((), ()))}


def _mm(pairs, mode, out_dtype, name, bm, bn, j_outer=False, b_blocks=False, out_blocks=False):
    a0, b0 = pairs[0]
    cb = b0.shape[2] if b_blocks else None
    b_shape = (b0.shape[1], N_DEV * cb) if b_blocks else b0.shape
    if mode == "nn":
        (M, K), (K2, N) = a0.shape, b_shape
    elif mode == "nt":
        (M, K), (N, K2) = a0.shape, b_shape
    else:
        (K, M), (K2, N) = a0.shape, b_shape
    bm, bn = min(bm, M), min(bn, N)
    assert K == K2 and M % bm == 0 and N % bn == 0, (name, a0.shape, b0.shape, bm, bn)
    co = N // N_DEV
    assert not out_blocks or bn % co == 0
    dims = _DIMS[mode]
    n = len(pairs)

    def body(*refs):
        o_ref = refs[2 * n]
        acc = None
        for t in range(n):
            b_ref = refs[2 * t + 1]
            b = jnp.concatenate([b_ref[d] for d in range(b_ref.shape[0])], axis=1) if b_blocks else b_ref[...]
            p = lax.dot_general(refs[2 * t][...].astype(BF16), b.astype(BF16), dims, preferred_element_type=F32)
            acc = p if acc is None else acc + p
        if out_blocks:
            for d in range(bn // co):
                o_ref[d] = acc[:, d * co:(d + 1) * co].astype(out_dtype)
        else:
            o_ref[...] = acc.astype(out_dtype)

    def ij(f):
        return (lambda j, i: f(i, j)) if j_outer else f

    a_spec = pl.BlockSpec((K, bm), ij(lambda i, j: (0, i))) if mode == "tn" else pl.BlockSpec((bm, K), ij(lambda i, j: (i, 0)))
    if b_blocks and mode == "nt":
        b_spec = pl.BlockSpec((N_DEV, bn, cb), ij(lambda i, j: (0, j, 0)))
    elif b_blocks:
        b_spec = pl.BlockSpec((bn // cb, K, cb), ij(lambda i, j: (j, 0, 0)))
    elif mode == "nt":
        b_spec = pl.BlockSpec((bn, K), ij(lambda i, j: (j, 0)))
    else:
        b_spec = pl.BlockSpec((K, bn), ij(lambda i, j: (0, j)))
    if out_blocks:
        out_spec = pl.BlockSpec((bn // co, bm, co), ij(lambda i, j: (j, i, 0)))
        out_shape = jax.ShapeDtypeStruct((N_DEV, M, co), out_dtype)
    else:
        out_spec = pl.BlockSpec((bm, bn), ij(lambda i, j: (i, j)))
        out_shape = jax.ShapeDtypeStruct((M, N), out_dtype)
    grid = (N // bn, M // bm) if j_outer else (M // bm, N // bn)
    return pl.pallas_call(
        body, grid=grid, in_specs=[a_spec, b_spec] * n, out_specs=out_spec, out_shape=out_shape, name=name,
        compiler_params=_params(("parallel", "parallel")),
    )(*[x for pair in pairs for x in pair])


def _rms(x, gain):
    return x * lax.rsqrt(jnp.mean(x * x, axis=-1, keepdims=True) + EPS) * gain


def _silu(x):
    return x * jax.nn.sigmoid(x)


def _act(g, u):
    return _silu(g) * u


def _merge(g0, g1, a_dn, a_swa):
    return jax.nn.sigmoid(g0) * a_dn + jax.nn.sigmoid(g1) * a_swa


def _dn_post(c, is_v, q_scale):
    a = _silu(c)
    rs = lax.rsqrt(jnp.sum(a * a, axis=-1, keepdims=True) + EPS) * q_scale
    return a * jnp.where(is_v, 1.0, rs)


def _dn_out(o, z, gain):
    return _rms(o, gain) * _silu(z)


def _dot(a, b, dims=_DIMS["nn"], hi=False):
    if a.ndim == 3 or b.ndim == 3:
        batch = a.shape[0] if a.ndim == 3 else b.shape[0]
        a = a if a.ndim == 3 else jnp.broadcast_to(a, (batch,) + a.shape)
        b = b if b.ndim == 3 else jnp.broadcast_to(b, (batch,) + b.shape)
        ((ca,), (cb,)), _ = dims
        dims = (((ca + 1,), (cb + 1,)), ((0,), (0,)))
    if hi:
        return lax.dot_general(a, b, dims, precision=HI, preferred_element_type=F32)
    return lax.dot_general(a.astype(BF16), b.astype(BF16), dims, preferred_element_type=F32)


def _pieces(x):
    hi = x.astype(BF16)
    r1 = x - hi.astype(F32)
    mid = r1.astype(BF16)
    return hi, mid, (r1 - mid.astype(F32)).astype(BF16)


def _sel_left_impl(m, x):
    mb = m.astype(BF16)
    hi, mid, lo = _pieces(x)
    return _dot(mb, hi) + (_dot(mb, mid) + _dot(mb, lo))


@jax.custom_vjp
def _sel_left(m, mt, x):
    return _sel_left_impl(m, x)


_sel_left.defvjp(lambda m, mt, x: (_sel_left_impl(m, x), (m, mt)),
                 lambda res, ct: (jnp.zeros_like(res[0]), jnp.zeros_like(res[1]), _sel_left_impl(res[1], ct)))


def _sel_right_impl(x, s):
    sb = s.astype(BF16)
    hi, mid, lo = _pieces(x)
    return _dot(hi, sb) + (_dot(mid, sb) + _dot(lo, sb))


@jax.custom_vjp
def _sel_right(x, s, st):
    return _sel_right_impl(x, s)


_sel_right.defvjp(lambda x, s, st: (_sel_right_impl(x, s), (s, st)),
                  lambda res, ct: (_sel_right_impl(ct, res[1]), jnp.zeros_like(res[0]), jnp.zeros_like(res[1])))


def _sel_nt_impl(s, x):
    sb = s.astype(BF16)
    hi, mid, lo = _pieces(x)
    return _dot(sb, hi, _DIMS["nt"]) + (_dot(sb, mid, _DIMS["nt"]) + _dot(sb, lo, _DIMS["nt"]))


def _sel_tn_impl(x, s):
    sb = s.astype(BF16)
    hi, mid, lo = _pieces(x)
    return _dot(hi, sb, _DIMS["tn"]) + (_dot(mid, sb, _DIMS["tn"]) + _dot(lo, sb, _DIMS["tn"]))


@jax.custom_vjp
def _sel_nt(s, x):
    return _sel_nt_impl(s, x)


_sel_nt.defvjp(lambda s, x: (_sel_nt_impl(s, x), s),
               lambda s, ct: (jnp.zeros_like(s), _sel_tn_impl(ct, s)))


def _dot3_impl(a, b):
    a_hi, a_lo, _ = _pieces(a)
    b_hi, b_lo, _ = _pieces(b)
    return _dot(a_hi, b_hi) + (_dot(a_hi, b_lo) + _dot(a_lo, b_hi))


@jax.custom_vjp
def _dot3(a, b):
    return _dot3_impl(a, b)


_dot3.defvjp(lambda a, b: (_dot3_impl(a, b), (a, b)),
             lambda res, ct: (_dot(ct, res[1], _DIMS["nt"]), _dot(res[0], ct, _DIMS["tn"])))


def _inv_impl(a, eye, strict):
    t = eye - a
    p = _dot(a, a)
    for level in range(5):
        t = t + _dot(t, p)
        if level < 4:
            p = _dot(p, p)
    t = t + _dot3_impl(t, eye - _dot3_impl(eye + a, t))
    return jnp.where(strict > 0.5, t, eye)


@jax.custom_vjp
def _inv_unit_lower(a, eye, strict):
    return _inv_impl(a, eye, strict)


def _inv_bwd(res, ct):
    t, eye, strict = res
    da = -_dot(_dot(t, ct, _DIMS["tn"]), t, _DIMS["nt"])
    return da, jnp.zeros_like(eye), jnp.zeros_like(strict)


def _inv_fwd(a, eye, strict):
    t = _inv_impl(a, eye, strict)
    return t, (t, eye, strict)


_inv_unit_lower.defvjp(_inv_fwd, _inv_bwd)

GROUP = 4
GROUP_ROWS = GROUP * CHUNK


def _block_consts(n):
    ii = lax.broadcasted_iota(jnp.int32, (n, n), 0)
    jj = lax.broadcasted_iota(jnp.int32, (n, n), 1)
    shift = CHUNK.bit_length() - 1
    same = jnp.right_shift(ii, shift) == jnp.right_shift(jj, shift)
    return same & (ii >= jj), same & (ii <= jj), same & (ii > jj), same, ii == jj


def _lane0(n):
    s = (lax.broadcasted_iota(jnp.int32, (LANES, n), 0) == 0).astype(F32)
    st = (lax.broadcasted_iota(jnp.int32, (n, LANES), 1) == 0).astype(F32)
    return s, st


def _dn_group(q, k, v, g, beta):
    n = GROUP_ROWS
    low_b, upp_b, strict_b, same_b, eye_b = _block_consts(n)
    low, upp, same, eye = low_b.astype(F32), upp_b.astype(F32), same_b.astype(F32), eye_b.astype(F32)
    s, st = _lane0(n)
    gc = _sel_left(low, upp, g)
    gl = _sel_left(same, same, g)
    col = _sel_right(gc, s, st)
    row = _sel_nt(st, gc)
    decay = jnp.exp(jnp.where(low_b, col - row, -jnp.inf))
    kb = k * beta
    vb = v * beta
    a = jnp.where(strict_b, _dot(kb, k, _DIMS["nt"]) * decay, 0.0)
    t = _inv_unit_lower(a, eye, strict_b.astype(F32))
    u = _dot3(t, vb)
    w = _dot3(t, kb * jnp.exp(gc))
    return u, w, q * jnp.exp(gc), k * jnp.exp(gl - gc)


def _dn_chunk(q, k, g):
    ii = lax.broadcasted_iota(jnp.int32, (CHUNK, CHUNK), 0)
    jj = lax.broadcasted_iota(jnp.int32, (CHUNK, CHUNK), 1)
    low = (ii >= jj).astype(F32)
    upp = (ii <= jj).astype(F32)
    s, st = _lane0(CHUNK)
    gc = _sel_left(low, upp, g)
    col = _sel_right(gc, s, st)
    row = _sel_nt(st, gc)
    decay = jnp.exp(jnp.where(ii >= jj, col - row, -jnp.inf))
    qk = _dot(q, k, _DIMS["nt"]) * decay
    return qk, jnp.exp(jnp.sum(g, axis=-2, keepdims=True))


def _dn_step(s, u, w, qe, kd, qk, egl):
    v_new = u - _dot(w, s)
    o = _dot(qe, s) + _dot(qk, v_new)
    s_new = s * egl + _dot(kd, v_new, _DIMS["tn"])
    return s_new, o


def _swa_block(q, kband, vband, qg, kg, sinks, bias, mask):
    kn = _rms(kband, kg)
    qn = _rms(q, qg)
    logits = _dot(qn, kn, _DIMS["nt"]) * (SWA_DIM ** -0.5)
    logits = jnp.where(mask, logits + bias, -jnp.inf)
    m = jnp.maximum(jnp.max(logits, axis=-1, keepdims=True), sinks)
    p = jnp.exp(logits - m)
    denom = jnp.sum(p, axis=-1, keepdims=True) + jnp.exp(sinks - m)
    return _dot(p / denom, vband)


def _adamw(w, g, m, v):
    m = ADAM_B1 * m + (1.0 - ADAM_B1) * g
    v = ADAM_B2 * v + (1.0 - ADAM_B2) * jnp.square(g)
    m_hat = m / (1.0 - ADAM_B1 ** ADAM_STEP)
    v_hat = v / (1.0 - ADAM_B2 ** ADAM_STEP)
    delta = -ADAM_LR * (m_hat / (jnp.sqrt(v_hat) + ADAM_EPS) + ADAM_WD * w)
    return delta, m, v


def _row(tm, c, cb=0):
    return pl.BlockSpec((tm, c), lambda i, cb=cb: (i, cb))


def _full(shape):
    nd = len(shape)
    return pl.BlockSpec(shape, lambda *_, nd=nd: (0,) * nd)


def _norm_fwd(x, gain, name, tm=512):
    S = x.shape[0]

    def body(x_ref, g_ref, h_ref):
        h_ref[...] = _rms(x_ref[...], g_ref[...]).astype(BF16)

    return pl.pallas_call(
        body, grid=(S // tm,), in_specs=[_row(tm, D_MODEL), _full((1, D_MODEL))],
        out_specs=_row(tm, D_MODEL), out_shape=jax.ShapeDtypeStruct((S, D_MODEL), BF16),
        name=name, compiler_params=_params(("parallel",)))(x, gain)


def _resid_norm_fwd(x, t, gain, name, tm=512):
    S = x.shape[0]

    def body(x_ref, t_ref, g_ref, x1_ref, h_ref):
        x1 = x_ref[...] + t_ref[...]
        x1_ref[...] = x1
        h_ref[...] = _rms(x1, g_ref[...]).astype(BF16)

    return pl.pallas_call(
        body, grid=(S // tm,), in_specs=[_row(tm, D_MODEL), _row(tm, D_MODEL), _full((1, D_MODEL))],
        out_specs=[_row(tm, D_MODEL), _row(tm, D_MODEL)],
        out_shape=[jax.ShapeDtypeStruct((S, D_MODEL), F32), jax.ShapeDtypeStruct((S, D_MODEL), BF16)],
        name=name, compiler_params=_params(("parallel",)))(x, t, gain)


def _norm_bwd(x, dh_list, dres, gain, name, tm=256):
    S = x.shape[0]
    n = len(dh_list)

    def body(*refs):
        x_ref, g_ref, r_ref = refs[0], refs[1], refs[2]
        dh_refs = refs[3:3 + n]
        dx_ref, dxb_ref, dg_ref = refs[3 + n], refs[4 + n], refs[5 + n]
        dh = dh_refs[0][...].astype(F32)
        for r in dh_refs[1:]:
            dh = dh + r[...].astype(F32)
        _, vjp = jax.vjp(_rms, x_ref[...], g_ref[...])
        dx, dg = vjp(dh)
        dx = dx + r_ref[...]
        dx_ref[...] = dx
        dxb_ref[...] = dx.astype(BF16)

        @pl.when(pl.program_id(0) == 0)
        def _():
            dg_ref[...] = jnp.zeros_like(dg_ref)

        dg_ref[...] += dg

    return pl.pallas_call(
        body, grid=(S // tm,),
        in_specs=[_row(tm, D_MODEL), _full((1, D_MODEL)), _row(tm, D_MODEL)] + [_row(tm, D_MODEL)] * n,
        out_specs=[_row(tm, D_MODEL), _row(tm, D_MODEL), _full((1, D_MODEL))],
        out_shape=[jax.ShapeDtypeStruct((S, D_MODEL), F32), jax.ShapeDtypeStruct((S, D_MODEL), BF16),
                   jax.ShapeDtypeStruct((1, D_MODEL), F32)],
        name=name, compiler_params=_params(("arbitrary",)))(x, gain, dres, *dh_list)


def _loss_fwd_bwd(x1, f, target, tm=512):
    S = x1.shape[0]

    def body(x_ref, f_ref, t_ref, dy_ref, dyb_ref, l_ref):
        diff = x_ref[...] + f_ref[...] - t_ref[...]
        dy = diff * (1.0 / D_MODEL)
        dy_ref[...] = dy
        dyb_ref[...] = dy.astype(BF16)

        @pl.when(pl.program_id(0) == 0)
        def _():
            l_ref[...] = jnp.zeros_like(l_ref)

        l_ref[...] += jnp.sum(jnp.mean(diff * diff, axis=-1, keepdims=True), axis=0, keepdims=True) * 0.5

    return pl.pallas_call(
        body, grid=(S // tm,), in_specs=[_row(tm, D_MODEL)] * 3,
        out_specs=[_row(tm, D_MODEL), _row(tm, D_MODEL), _full((1, 1))],
        out_shape=[jax.ShapeDtypeStruct((S, D_MODEL), F32), jax.ShapeDtypeStruct((S, D_MODEL), BF16),
                   jax.ShapeDtypeStruct((1, 1), F32)],
        name="loss_fwd_bwd", compiler_params=_params(("arbitrary",)))(x1, f, target)


def _act_fwd(g, u, tm=256):
    S, width = g.shape

    def body(g_ref, u_ref, o_ref):
        o_ref[...] = _act(g_ref[...], u_ref[...]).astype(BF16)

    return pl.pallas_call(
        body, grid=(S // tm,), in_specs=[_row(tm, width)] * 2, out_specs=_row(tm, width),
        out_shape=jax.ShapeDtypeStruct((S, width), BF16), name="act_fwd",
        compiler_params=_params(("parallel",)))(g, u)


def _act_bwd(g, u, dact, tm=256):
    S, width = g.shape

    def body(g_ref, u_ref, d_ref, dg_ref, du_ref):
        _, vjp = jax.vjp(_act, g_ref[...], u_ref[...])
        dg, du = vjp(d_ref[...])
        dg_ref[...] = dg.astype(BF16)
        du_ref[...] = du.astype(BF16)

    return pl.pallas_call(
        body, grid=(S // tm,), in_specs=[_row(tm, width)] * 3, out_specs=[_row(tm, width)] * 2,
        out_shape=[jax.ShapeDtypeStruct((S, width), BF16)] * 2, name="act_bwd",
        compiler_params=_params(("parallel",)))(g, u, dact)


def _merge_fwd(proj, a_dn, a_swa, tm=512, tc=512):
    S = proj.shape[0]
    nc = D_MODEL // tc

    def spec(off):
        return pl.BlockSpec((tm, tc), lambda i, j, off=off: (i, off + j))

    def body(g0_ref, g1_ref, ad_ref, as_ref, o_ref):
        o_ref[...] = _merge(g0_ref[...], g1_ref[...], ad_ref[...], as_ref[...]).astype(BF16)

    return pl.pallas_call(
        body, grid=(S // tm, nc), in_specs=[spec(P_GATE // tc), spec(P_GATE // tc + nc), spec(0), spec(0)],
        out_specs=spec(0), out_shape=jax.ShapeDtypeStruct((S, D_MODEL), BF16), name="merge_fwd",
        compiler_params=_params(("parallel", "parallel")))(proj, proj, a_dn, a_swa)


def _merge_bwd(proj, a_dn, a_swa, dmerged, tm=512, tc=512):
    S = proj.shape[0]
    nc = D_MODEL // tc

    def spec(off):
        return pl.BlockSpec((tm, tc), lambda i, j, off=off: (i, off + j))

    def body(g0_ref, g1_ref, ad_ref, as_ref, d_ref, dg0_ref, dg1_ref, dad_ref, das_ref):
        _, vjp = jax.vjp(_merge, g0_ref[...], g1_ref[...], ad_ref[...], as_ref[...])
        dg0, dg1, dad, das = vjp(d_ref[...])
        dg0_ref[...] = dg0.astype(BF16)
        dg1_ref[...] = dg1.astype(BF16)
        dad_ref[...] = dad.astype(BF16)
        das_ref[...] = das.astype(BF16)

    return pl.pallas_call(
        body, grid=(S // tm, nc),
        in_specs=[spec(P_GATE // tc), spec(P_GATE // tc + nc), spec(0), spec(0), spec(0)],
        out_specs=[spec(0)] * 4, out_shape=[jax.ShapeDtypeStruct((S, D_MODEL), BF16)] * 4,
        name="merge_bwd", compiler_params=_params(("parallel", "parallel")))(proj, proj, a_dn, a_swa, dmerged)


def _shift_down(x, s):
    row = lax.broadcasted_iota(jnp.int32, x.shape, 0)
    return jnp.where(row >= s, pltpu.roll(x, s, axis=0), 0.0)


def _shift_up(x, s):
    n = x.shape[0]
    row = lax.broadcasted_iota(jnp.int32, x.shape, 0)
    return jnp.where(row < n - s, pltpu.roll(x, n - s, axis=0), 0.0)


def _conv(x, w):
    out = w[DN_CONV - 1:DN_CONV] * x
    for s in range(1, DN_CONV):
        out = out + w[DN_CONV - 1 - s:DN_CONV - s] * _shift_down(x, s)
    return out


def _dn_conv_fwd(proj, conv_w):
    S = proj.shape[0]
    nb = DN_QKV // LANES

    def body(x_ref, w_ref, o_ref):
        j = pl.program_id(0)
        q_scale = jnp.where(j < DN_HEADS, DN_DIM ** -0.5, 1.0).astype(F32)
        o_ref[...] = _dn_post(_conv(x_ref[...], w_ref[...]), j >= 2 * DN_HEADS, q_scale)

    return pl.pallas_call(
        body, grid=(nb,),
        in_specs=[pl.BlockSpec((S, LANES), lambda j: (0, P_QKV // LANES + j)),
                  pl.BlockSpec((DN_CONV, LANES), lambda j: (0, j))],
        out_specs=pl.BlockSpec((S, LANES), lambda j: (0, j)),
        out_shape=jax.ShapeDtypeStruct((S, DN_QKV), F32), name="dn_conv_fwd",
        compiler_params=_params(("parallel",)))(proj, conv_w)


def _dn_conv_bwd(proj, conv_w, dqkvn):
    S = proj.shape[0]
    nb = DN_QKV // LANES

    def body(x_ref, w_ref, d_ref, dx_ref, dw_ref):
        j = pl.program_id(0)
        q_scale = jnp.where(j < DN_HEADS, DN_DIM ** -0.5, 1.0).astype(F32)
        x = x_ref[...]
        w = w_ref[...]
        _, vjp = jax.vjp(lambda c: _dn_post(c, j >= 2 * DN_HEADS, q_scale), _conv(x, w))
        (dc,) = vjp(d_ref[0])
        dx = w[DN_CONV - 1:DN_CONV] * dc
        dw_ref[DN_CONV - 1:DN_CONV, :] = jnp.sum(dc * x, axis=0, keepdims=True)
        for s in range(1, DN_CONV):
            dx = dx + w[DN_CONV - 1 - s:DN_CONV - s] * _shift_up(dc, s)
            dw_ref[DN_CONV - 1 - s:DN_CONV - s, :] = jnp.sum(dc * _shift_down(x, s), axis=0, keepdims=True)
        dx_ref[...] = dx.astype(BF16)

    return pl.pallas_call(
        body, grid=(nb,),
        in_specs=[pl.BlockSpec((S, LANES), lambda j: (0, P_QKV // LANES + j)),
                  pl.BlockSpec((DN_CONV, LANES), lambda j: (0, j)),
                  pl.BlockSpec((1, S, LANES), lambda j: (lax.div(j, DN_HEADS), 0, lax.rem(j, DN_HEADS)))],
        out_specs=[pl.BlockSpec((S, LANES), lambda j: (0, j)), pl.BlockSpec((DN_CONV, LANES), lambda j: (0, j))],
        out_shape=[jax.ShapeDtypeStruct((S, DN_QKV), BF16), jax.ShapeDtypeStruct((DN_CONV, DN_QKV), F32)],
        name="dn_conv_bwd", compiler_params=_params(("parallel",)))(proj, conv_w, dqkvn)


def _expanders():
    eb = np.zeros((LANES, DN_WIDTH), np.float32)
    ea = np.zeros((LANES, DN_WIDTH), np.float32)
    for h in range(DN_HEADS):
        eb[h, h * DN_DIM:(h + 1) * DN_DIM] = 1.0
        ea[DN_HEADS + h, h * DN_DIM:(h + 1) * DN_DIM] = 1.0
    return jnp.asarray(eb), jnp.asarray(ea), jnp.asarray(eb.T), jnp.asarray(ea.T)


def _dn_gate_args(a_log, dt_bias):
    alog = jnp.repeat(a_log.reshape(1, DN_HEADS), DN_DIM, axis=1)
    dtb = jnp.repeat(dt_bias.reshape(1, DN_HEADS), DN_DIM, axis=1)
    return _expanders() + (alog, dtb)


def _dn_gate_specs(tm):
    return [_row(tm, LANES, P_BA // LANES), _full((LANES, DN_WIDTH)), _full((LANES, DN_WIDTH)),
            _full((DN_WIDTH, LANES)), _full((DN_WIDTH, LANES)), _full((1, DN_WIDTH)), _full((1, DN_WIDTH))]


def _dn_gate_fn(ba, eb, ea, ebt, eat, alog, dtb):
    beta = jax.nn.sigmoid(_sel_right(ba, eb, ebt))
    g = -jnp.exp(alog) * jax.nn.softplus(_sel_right(ba, ea, eat) + dtb)
    return beta, g


def _dn_gate_fwd(proj, a_log, dt_bias, tm=512):
    S = proj.shape[0]
    args = _dn_gate_args(a_log, dt_bias)

    def body(ba_ref, eb_ref, ea_ref, ebt_ref, eat_ref, al_ref, dt_ref, beta_ref, g_ref):
        beta, g = _dn_gate_fn(ba_ref[...], eb_ref[...], ea_ref[...], ebt_ref[...], eat_ref[...], al_ref[...],
                              dt_ref[...])
        beta_ref[...] = beta
        g_ref[...] = g

    return pl.pallas_call(
        body, grid=(S // tm,), in_specs=_dn_gate_specs(tm), out_specs=[_row(tm, DN_WIDTH), _row(tm, DN_WIDTH)],
        out_shape=[jax.ShapeDtypeStruct((S, DN_WIDTH), F32), jax.ShapeDtypeStruct((S, DN_WIDTH), F32)],
        name="dn_gate_fwd", compiler_params=_params(("parallel",)))(proj, *args)


def _dn_gate_bwd(proj, a_log, dt_bias, dbeta, dg, tm=512):
    S = proj.shape[0]
    args = _dn_gate_args(a_log, dt_bias)

    def body(ba_ref, eb_ref, ea_ref, ebt_ref, eat_ref, al_ref, dt_ref, dbeta_ref, dg_ref, dba_ref, dal_ref, ddt_ref):
        eb, ea, ebt, eat = eb_ref[...], ea_ref[...], ebt_ref[...], eat_ref[...]
        _, vjp = jax.vjp(lambda ba, al, dt: _dn_gate_fn(ba, eb, ea, ebt, eat, al, dt), ba_ref[...], al_ref[...],
                         dt_ref[...])
        dba, dal, ddt = vjp((dbeta_ref[...], dg_ref[...]))
        dba_ref[...] = dba.astype(BF16)

        @pl.when(pl.program_id(0) == 0)
        def _():
            dal_ref[...] = jnp.zeros_like(dal_ref)
            ddt_ref[...] = jnp.zeros_like(ddt_ref)

        dal_ref[...] += dal
        ddt_ref[...] += ddt

    return pl.pallas_call(
        body, grid=(S // tm,), in_specs=_dn_gate_specs(tm) + [_row(tm, DN_WIDTH), _row(tm, DN_WIDTH)],
        out_specs=[_row(tm, LANES), _full((1, DN_WIDTH)), _full((1, DN_WIDTH))],
        out_shape=[jax.ShapeDtypeStruct((S, LANES), BF16), jax.ShapeDtypeStruct((1, DN_WIDTH), F32),
                   jax.ShapeDtypeStruct((1, DN_WIDTH), F32)],
        name="dn_gate_bwd", compiler_params=_params(("arbitrary",)))(proj, *args, dbeta, dg)


PREP_GROUPS = 4
PREP_CHUNKS = GROUP * PREP_GROUPS


def _dn_prep_specs():
    rows = PREP_CHUNKS * CHUNK
    q = pl.BlockSpec((rows, LANES), lambda h, c: (c, h))
    k = pl.BlockSpec((rows, LANES), lambda h, c: (c, DN_HEADS + h))
    v = pl.BlockSpec((rows, LANES), lambda h, c: (c, 2 * DN_HEADS + h))
    qk = pl.BlockSpec((1, rows, CHUNK), lambda h, c: (h, c, 0))
    egl = pl.BlockSpec((1, PREP_CHUNKS, 1, LANES), lambda h, c: (h, c, 0, 0))
    return q, k, v, qk, egl


def _dn_prep_fwd(qkvn, g, beta):
    S = qkvn.shape[0]
    nc = S // CHUNK
    q, k, v, qks, egl = _dn_prep_specs()

    def body(q_ref, k_ref, v_ref, g_ref, b_ref, u_ref, w_ref, qe_ref, kd_ref, qk_ref, egl_ref):
        rows = PREP_CHUNKS * CHUNK
        grp = (PREP_GROUPS, GROUP_ROWS, LANES)
        chk = (PREP_CHUNKS, CHUNK, LANES)
        q, k, g = q_ref[...], k_ref[...], g_ref[...]
        u, w, qe, kd = _dn_group(q.reshape(grp), k.reshape(grp), v_ref[...].reshape(grp), g.reshape(grp),
                                 b_ref[...].reshape(grp))
        u_ref[...] = u.reshape(rows, LANES)
        w_ref[...] = w.reshape(rows, LANES)
        qe_ref[...] = qe.reshape(rows, LANES)
        kd_ref[...] = kd.reshape(rows, LANES)
        qk, e = _dn_chunk(q.reshape(chk), k.reshape(chk), g.reshape(chk))
        qk_ref[0] = qk.reshape(rows, CHUNK)
        egl_ref[0] = e

    wide = jax.ShapeDtypeStruct((S, DN_WIDTH), F32)
    return pl.pallas_call(
        body, grid=(DN_HEADS, nc // PREP_CHUNKS), in_specs=[q, k, v, q, q],
        out_specs=[q, q, q, q, qks, egl],
        out_shape=[wide, wide, wide, wide, jax.ShapeDtypeStruct((DN_HEADS, S, CHUNK), F32),
                   jax.ShapeDtypeStruct((DN_HEADS, nc, 1, LANES), F32)],
        name="dn_prep_fwd", compiler_params=_params(("parallel", "parallel")))(qkvn, qkvn, qkvn, g, beta)


def _dn_prep_bwd(qkvn, g, beta, du, dw, dqe, dkd, dqk, degl):
    S = qkvn.shape[0]
    nc = S // CHUNK
    q, k, v, qks, egl = _dn_prep_specs()

    def body(q_ref, k_ref, v_ref, g_ref, b_ref, du_ref, dw_ref, dqe_ref, dkd_ref, dqk_ref, degl_ref,
             dqkv_ref, dg_ref, db_ref):
        rows = PREP_CHUNKS * CHUNK
        grp = (PREP_GROUPS, GROUP_ROWS, LANES)
        chk = (PREP_CHUNKS, CHUNK, LANES)
        q, k, g = q_ref[...], k_ref[...], g_ref[...]
        _, vjp = jax.vjp(_dn_group, q.reshape(grp), k.reshape(grp), v_ref[...].reshape(grp), g.reshape(grp),
                         b_ref[...].reshape(grp))
        dq, dk, dv, dg, db = vjp((du_ref[...].reshape(grp), dw_ref[...].reshape(grp), dqe_ref[...].reshape(grp),
                                  dkd_ref[...].reshape(grp)))
        _, vjp = jax.vjp(_dn_chunk, q.reshape(chk), k.reshape(chk), g.reshape(chk))
        dq2, dk2, dg2 = vjp((dqk_ref[0].reshape(PREP_CHUNKS, CHUNK, CHUNK), degl_ref[0]))
        dqkv_ref[0] = dq.reshape(rows, LANES) + dq2.reshape(rows, LANES)
        dqkv_ref[1] = dk.reshape(rows, LANES) + dk2.reshape(rows, LANES)
        dqkv_ref[2] = dv.reshape(rows, LANES)
        dg_ref[...] = dg.reshape(rows, LANES) + dg2.reshape(rows, LANES)
        db_ref[...] = db.reshape(rows, LANES)

    wide = jax.ShapeDtypeStruct((S, DN_WIDTH), F32)
    rows = PREP_CHUNKS * CHUNK
    return pl.pallas_call(
        body, grid=(DN_HEADS, nc // PREP_CHUNKS), in_specs=[q, k, v, q, q, q, q, q, q, qks, egl],
        out_specs=[pl.BlockSpec((3, rows, LANES), lambda h, c: (0, c, h)), q, q],
        out_shape=[jax.ShapeDtypeStruct((3, S, DN_WIDTH), F32), wide, wide],
        name="dn_prep_bwd", compiler_params=_params(("parallel", "parallel")),
    )(qkvn, qkvn, qkvn, g, beta, du, dw, dqe, dkd, dqk, degl)


def _dn_scan_specs(nc, reverse):
    def cidx(c):
        return nc - 1 - c if reverse else c

    hc = pl.BlockSpec((CHUNK, DN_WIDTH), lambda c: (cidx(c), 0))
    qk = pl.BlockSpec((DN_HEADS, CHUNK, CHUNK), lambda c: (0, cidx(c), 0))
    egl = pl.BlockSpec((DN_HEADS, 1, 1, LANES), lambda c: (0, cidx(c), 0, 0))
    st = pl.BlockSpec((DN_HEADS, 1, DN_DIM, DN_DIM), lambda c: (0, cidx(c), 0, 0))
    return hc, qk, egl, st


def _heads(ref):
    return jnp.stack([ref[:, pl.ds(h * DN_DIM, DN_DIM)] for h in range(DN_HEADS)])


def _dn_scan_fwd(u, w, qe, kd, qk, egl):
    S = u.shape[0]
    nc = S // CHUNK
    hc, qks, egls, st = _dn_scan_specs(nc, False)

    def body(u_ref, w_ref, qe_ref, kd_ref, qk_ref, egl_ref, o_ref, st_ref, s_scr):
        @pl.when(pl.program_id(0) == 0)
        def _():
            s_scr[...] = jnp.zeros_like(s_scr)

        s = s_scr[...]
        st_ref[:, 0] = s
        s_new, o = _dn_step(s, _heads(u_ref), _heads(w_ref), _heads(qe_ref), _heads(kd_ref), qk_ref[...],
                            egl_ref[:, 0])
        for h in range(DN_HEADS):
            o_ref[:, pl.ds(h * DN_DIM, DN_DIM)] = o[h]
        s_scr[...] = s_new

    return pl.pallas_call(
        body, grid=(nc,), in_specs=[hc, hc, hc, hc, qks, egls], out_specs=[hc, st],
        out_shape=[jax.ShapeDtypeStruct((S, DN_WIDTH), F32), jax.ShapeDtypeStruct((DN_HEADS, nc, DN_DIM, DN_DIM), F32)],
        scratch_shapes=[pltpu.VMEM((DN_HEADS, DN_DIM, DN_DIM), F32)], name="dn_scan_fwd",
        compiler_params=_params(("arbitrary",)))(u, w, qe, kd, qk, egl)


def _dn_scan_bwd(u, w, qe, kd, qk, egl, states, do):
    S = u.shape[0]
    nc = S // CHUNK
    hc, qks, egls, st = _dn_scan_specs(nc, True)

    def body(u_ref, w_ref, qe_ref, kd_ref, qk_ref, egl_ref, st_ref, do_ref,
             du_ref, dw_ref, dqe_ref, dkd_ref, dqk_ref, degl_ref, ds_scr):
        @pl.when(pl.program_id(0) == 0)
        def _():
            ds_scr[...] = jnp.zeros_like(ds_scr)

        _, vjp = jax.vjp(_dn_step, st_ref[:, 0], _heads(u_ref), _heads(w_ref), _heads(qe_ref), _heads(kd_ref),
                         qk_ref[...], egl_ref[:, 0])
        ds, du, dw, dqe, dkd, dqk, degl = vjp((ds_scr[...], _heads(do_ref)))
        ds_scr[...] = ds
        dqk_ref[...] = dqk
        degl_ref[:, 0] = degl
        for h in range(DN_HEADS):
            cols = pl.ds(h * DN_DIM, DN_DIM)
            du_ref[:, cols] = du[h]
            dw_ref[:, cols] = dw[h]
            dqe_ref[:, cols] = dqe[h]
            dkd_ref[:, cols] = dkd[h]

    wide = jax.ShapeDtypeStruct((S, DN_WIDTH), F32)
    return pl.pallas_call(
        body, grid=(nc,), in_specs=[hc, hc, hc, hc, qks, egls, st, hc],
        out_specs=[hc, hc, hc, hc, qks, egls],
        out_shape=[wide, wide, wide, wide, jax.ShapeDtypeStruct((DN_HEADS, S, CHUNK), F32),
                   jax.ShapeDtypeStruct((DN_HEADS, nc, 1, LANES), F32)],
        scratch_shapes=[pltpu.VMEM((DN_HEADS, DN_DIM, DN_DIM), F32)], name="dn_scan_bwd",
        compiler_params=_params(("arbitrary",)))(u, w, qe, kd, qk, egl, states, do)


def _dn_out_fwd(o, proj, gain, tm=512):
    S = o.shape[0]

    def body(o_ref, z_ref, g_ref, y_ref):
        y_ref[...] = _dn_out(o_ref[...], z_ref[...], g_ref[...]).astype(BF16)

    hs = pl.BlockSpec((tm, LANES), lambda i, h: (i, h))
    zs = pl.BlockSpec((tm, LANES), lambda i, h: (i, P_Z // LANES + h))
    return pl.pallas_call(
        body, grid=(S // tm, DN_HEADS), in_specs=[hs, zs, _full((1, DN_DIM))], out_specs=hs,
        out_shape=jax.ShapeDtypeStruct((S, DN_WIDTH), BF16), name="dn_out_fwd",
        compiler_params=_params(("parallel", "parallel")))(o, proj, gain)


def _dn_out_bwd(o, proj, gain, dy, tm=512):
    S = o.shape[0]

    def body(o_ref, z_ref, g_ref, dy_ref, do_ref, dz_ref, dg_ref):
        _, vjp = jax.vjp(_dn_out, o_ref[...], z_ref[...], g_ref[...])
        do, dz, dg = vjp(dy_ref[...])
        do_ref[...] = do
        dz_ref[...] = dz.astype(BF16)

        @pl.when((pl.program_id(0) == 0) & (pl.program_id(1) == 0))
        def _():
            dg_ref[...] = jnp.zeros_like(dg_ref)

        dg_ref[...] += dg

    hs = pl.BlockSpec((tm, LANES), lambda i, h: (i, h))
    zs = pl.BlockSpec((tm, LANES), lambda i, h: (i, P_Z // LANES + h))
    return pl.pallas_call(
        body, grid=(S // tm, DN_HEADS), in_specs=[hs, zs, _full((1, DN_DIM)), hs],
        out_specs=[hs, hs, _full((1, DN_DIM))],
        out_shape=[jax.ShapeDtypeStruct((S, DN_WIDTH), F32), jax.ShapeDtypeStruct((S, DN_WIDTH), BF16),
                   jax.ShapeDtypeStruct((1, DN_DIM), F32)],
        name="dn_out_bwd", compiler_params=_params(("arbitrary", "arbitrary")))(o, proj, gain, dy)


def _rel_buckets():
    qi = np.arange(BLOCK)[:, None]
    kj = np.arange(2 * BLOCK)[None, :]
    n = np.maximum(BLOCK + qi - kj, 0)
    max_exact = REL_BUCKETS // 2
    nf = np.maximum(n, 1).astype(np.float32)
    large = max_exact + (np.log(nf / np.float32(max_exact)) / np.float32(math.log(REL_MAX_DIST / max_exact))
                         * np.float32(REL_BUCKETS - max_exact)).astype(np.int32)
    large = np.minimum(large, REL_BUCKETS - 1)
    return np.where(n < max_exact, n, large).astype(np.int32)


def _bias_fwd(rel_bias):
    buckets = jnp.asarray(_rel_buckets())

    def body(rb_ref, bk_ref, o_ref):
        bk = bk_ref[...]
        for h in range(SWA_HEADS):
            acc = jnp.zeros((BLOCK, 2 * BLOCK), F32)
            for b in range(REL_BUCKETS):
                acc = jnp.where(bk == b, rb_ref[b, h], acc)
            o_ref[h] = acc

    return pl.pallas_call(
        body, in_specs=[pl.BlockSpec(memory_space=pltpu.SMEM), pl.BlockSpec(memory_space=pltpu.VMEM)],
        out_specs=pl.BlockSpec(memory_space=pltpu.VMEM),
        out_shape=jax.ShapeDtypeStruct((SWA_HEADS, BLOCK, 2 * BLOCK), F32), name="swa_bias_fwd",
        compiler_params=_params())(rel_bias, buckets)


def _bias_bwd(dbias):
    buckets = jnp.asarray(_rel_buckets())

    def body(d_ref, bk_ref, o_ref):
        bk = bk_ref[...]
        lane = lax.broadcasted_iota(jnp.int32, (1, LANES), 1)
        for h in range(SWA_HEADS):
            d = d_ref[h]
            row = jnp.zeros((1, LANES), F32)
            for b in range(REL_BUCKETS):
                part = jnp.sum(jnp.where(bk == b, d, 0.0), axis=1, keepdims=True)
                row = jnp.where(lane == b, jnp.sum(part, axis=0, keepdims=True), row)
            o_ref[h:h + 1, :] = row

    return pl.pallas_call(
        body, in_specs=[pl.BlockSpec(memory_space=pltpu.VMEM), pl.BlockSpec(memory_space=pltpu.VMEM)],
        out_specs=pl.BlockSpec(memory_space=pltpu.VMEM),
        out_shape=jax.ShapeDtypeStruct((SWA_HEADS, LANES), F32), name="swa_bias_bwd",
        compiler_params=_params())(dbias, buckets)


def _swa_mask(n):
    qi = lax.broadcasted_iota(jnp.int32, (BLOCK, 2 * BLOCK), 0)
    kj = lax.broadcasted_iota(jnp.int32, (BLOCK, 2 * BLOCK), 1)
    dist = BLOCK + qi - kj
    return (dist >= 0) & (dist < WINDOW) & ((n > 0) | (kj >= BLOCK))


def _swa_in_specs():
    q = pl.BlockSpec((BLOCK, SWA_WIDTH), lambda n: (n, P_SQ // SWA_WIDTH))
    kc = pl.BlockSpec((BLOCK, SWA_KVW), lambda n: (n, P_SK // SWA_KVW))
    kp = pl.BlockSpec((BLOCK, SWA_KVW), lambda n: (jnp.maximum(n - 1, 0), P_SK // SWA_KVW))
    vc = pl.BlockSpec((BLOCK, SWA_KVW), lambda n: (n, P_SV // SWA_KVW))
    vp = pl.BlockSpec((BLOCK, SWA_KVW), lambda n: (jnp.maximum(n - 1, 0), P_SV // SWA_KVW))
    small = [_full((1, SWA_DIM)), _full((1, SWA_DIM)), _full((1, SWA_HEADS)),
             _full((SWA_HEADS, BLOCK, 2 * BLOCK))]
    return [q, kp, kc, vp, vc] + small


def _swa_load(q_ref, kp_ref, kc_ref, vp_ref, vc_ref, s_ref):
    q = jnp.stack([q_ref[:, pl.ds(h * SWA_DIM, SWA_DIM)] for h in range(SWA_HEADS)])
    kbands, vbands = [], []
    for kv in range(SWA_KV):
        cols = pl.ds(kv * SWA_DIM, SWA_DIM)
        kbands += [jnp.concatenate([kp_ref[:, cols], kc_ref[:, cols]], axis=0)] * SWA_GROUP
        vbands += [jnp.concatenate([vp_ref[:, cols], vc_ref[:, cols]], axis=0)] * SWA_GROUP
    sinks = jnp.stack([s_ref[:, pl.ds(h, 1)] for h in range(SWA_HEADS)])
    return q, jnp.stack(kbands), jnp.stack(vbands), sinks


def _swa_fwd(proj, q_gain, k_gain, sinks, bias):
    S = proj.shape[0]

    def body(q_ref, kp_ref, kc_ref, vp_ref, vc_ref, qg_ref, kg_ref, s_ref, bias_ref, y_ref):
        mask = _swa_mask(pl.program_id(0))
        q, kband, vband, sk = _swa_load(q_ref, kp_ref, kc_ref, vp_ref, vc_ref, s_ref)
        out = _swa_block(q, kband, vband, qg_ref[...], kg_ref[...], sk, bias_ref[...], mask)
        for h in range(SWA_HEADS):
            y_ref[:, pl.ds(h * SWA_DIM, SWA_DIM)] = out[h].astype(BF16)

    return pl.pallas_call(
        body, grid=(S // BLOCK,), in_specs=_swa_in_specs(),
        out_specs=pl.BlockSpec((BLOCK, SWA_WIDTH), lambda n: (n, 0)),
        out_shape=jax.ShapeDtypeStruct((S, SWA_WIDTH), BF16), name="swa_fwd",
        compiler_params=_params(("parallel",)))(proj, proj, proj, proj, proj, q_gain, k_gain, sinks, bias)


def _swa_bwd(proj, q_gain, k_gain, sinks, bias, dy):
    S = proj.shape[0]

    def body(q_ref, kp_ref, kc_ref, vp_ref, vc_ref, qg_ref, kg_ref, s_ref, bias_ref, dy_ref,
             dq_ref, dk_ref, dv_ref, dqg_ref, dkg_ref, ds_ref, dbias_ref):
        n = pl.program_id(0)
        mask = _swa_mask(n)

        @pl.when(n == 0)
        def _():
            for r in (dk_ref, dv_ref, dqg_ref, dkg_ref, ds_ref, dbias_ref):
                r[...] = jnp.zeros_like(r)

        cur = pl.ds(pl.multiple_of(n * BLOCK, BLOCK), BLOCK)
        prev = pl.ds(pl.multiple_of(jnp.maximum(n - 1, 0) * BLOCK, BLOCK), BLOCK)
        q, kband, vband, sk = _swa_load(q_ref, kp_ref, kc_ref, vp_ref, vc_ref, s_ref)
        _, vjp = jax.vjp(lambda q, kb, vb, qg, kg, sk, bs: _swa_block(q, kb, vb, qg, kg, sk, bs, mask),
                         q, kband, vband, qg_ref[...], kg_ref[...], sk, bias_ref[...])
        dy = jnp.stack([dy_ref[:, pl.ds(h * SWA_DIM, SWA_DIM)] for h in range(SWA_HEADS)])
        dq, dkb, dvb, dqg, dkg, dsk, dbs = vjp(dy)
        for h in range(SWA_HEADS):
            dq_ref[:, pl.ds(h * SWA_DIM, SWA_DIM)] = dq[h].astype(BF16)
            ds_ref[:, pl.ds(h, 1)] += dsk[h]
        dbias_ref[...] += dbs
        dqg_ref[...] += dqg
        dkg_ref[...] += dkg
        for kv in range(SWA_KV):
            cols = pl.ds(kv * SWA_DIM, SWA_DIM)
            group = range(kv * SWA_GROUP, (kv + 1) * SWA_GROUP)
            dk_kv = sum(dkb[h] for h in group)
            dv_kv = sum(dvb[h] for h in group)
            dk_ref[cur, cols] += dk_kv[BLOCK:]
            dv_ref[cur, cols] += dv_kv[BLOCK:]

            @pl.when(n > 0)
            def _(cols=cols, dk_kv=dk_kv, dv_kv=dv_kv):
                dk_ref[prev, cols] += dk_kv[:BLOCK]
                dv_ref[prev, cols] += dv_kv[:BLOCK]

    return pl.pallas_call(
        body, grid=(S // BLOCK,),
        in_specs=_swa_in_specs() + [pl.BlockSpec((BLOCK, SWA_WIDTH), lambda n: (n, 0))],
        out_specs=[pl.BlockSpec((BLOCK, SWA_WIDTH), lambda n: (n, 0)), _full((S, SWA_KVW)), _full((S, SWA_KVW)),
                   _full((1, SWA_DIM)), _full((1, SWA_DIM)), _full((1, SWA_HEADS)),
                   _full((SWA_HEADS, BLOCK, 2 * BLOCK))],
        out_shape=[jax.ShapeDtypeStruct((S, SWA_WIDTH), BF16), jax.ShapeDtypeStruct((S, SWA_KVW), F32),
                   jax.ShapeDtypeStruct((S, SWA_KVW), F32), jax.ShapeDtypeStruct((1, SWA_DIM), F32),
                   jax.ShapeDtypeStruct((1, SWA_DIM), F32), jax.ShapeDtypeStruct((1, SWA_HEADS), F32),
                   jax.ShapeDtypeStruct((SWA_HEADS, BLOCK, 2 * BLOCK), F32)],
        name="swa_bwd", compiler_params=_params(("arbitrary",)),
    )(proj, proj, proj, proj, proj, q_gain, k_gain, sinks, bias, dy)


def _position():
    return lax.axis_index("x"), lax.axis_index("y"), lax.axis_index("c")


def _all_gather(shards, name="all_gather_weights"):
    na = len(shards)

    def body(*refs):
        x_refs, out_refs = refs[:na], refs[na:2 * na]
        send_sems, recv_sems, local_sems = refs[2 * na:]
        x, y, c = _position()
        me, sibling = (x, y, c), (x, y, 1 - c)
        chips = [(1 - x, y), (x, 1 - y), (1 - x, 1 - y)]

        def copy(a, k, block, to, own=False):
            px, py, pc = block
            slot = out_refs[a].at[4 * px + 2 * py + pc]
            return pltpu.make_async_remote_copy(
                src_ref=x_refs[a] if own else slot, dst_ref=slot, send_sem=send_sems.at[7 * a + k],
                recv_sem=recv_sems.at[7 * a + k], device_id=to, device_id_type=MESH_ID)

        mine = [pltpu.make_async_copy(x_refs[a], out_refs[a].at[4 * x + 2 * y + c], local_sems.at[a])
                for a in range(na)]
        for cp in mine:
            cp.start()
        first = []
        for a in range(na):
            first.append(copy(a, 0, me, sibling, own=True))
            first += [copy(a, 1 + j, me, (*chip, c), own=True) for j, chip in enumerate(chips)]
        for cp in first:
            cp.start()
        passed = []
        for j, chip in enumerate(chips):
            for a in range(na):
                copy(a, 1 + j, (*chip, c), me).wait_recv()
                passed.append(copy(a, 4 + j, (*chip, c), sibling))
                passed[-1].start()
        for a in range(na):
            copy(a, 0, sibling, me).wait_recv()
            for j, chip in enumerate(chips):
                copy(a, 4 + j, (*chip, 1 - c), me).wait_recv()
        for cp in first + passed:
            cp.wait_send()
        for cp in mine:
            cp.wait()

    return pl.pallas_call(
        body, in_specs=[pl.BlockSpec(memory_space=pl.ANY)] * na, out_specs=[pl.BlockSpec(memory_space=pl.ANY)] * na,
        out_shape=[jax.ShapeDtypeStruct((N_DEV,) + s.shape, s.dtype) for s in shards],
        scratch_shapes=[pltpu.SemaphoreType.DMA((7 * na,)), pltpu.SemaphoreType.DMA((7 * na,)),
                        pltpu.SemaphoreType.DMA((na,))],
        name=name)(*shards)


_HBM = pl.BlockSpec(memory_space=pltpu.HBM)
_SEM = pl.BlockSpec(memory_space=pltpu.SEMAPHORE)
_DATAFLOW = pltpu.SideEffectType.DATAFLOW_SIDE_EFFECTING


def _peers(x, y, c):
    out = []
    for k in range(1, N_DEV):
        px, py, pc = x ^ (k >> 2), y ^ ((k >> 1) & 1), c ^ (k & 1)
        out.append(((px, py, pc), 4 * px + 2 * py + pc))
    return out


def _split_copies(src_refs, land_refs, send_sems, recv_sems, scatter):
    x, y, c = _position()
    me = 4 * x + 2 * y + c
    sends, recvs = [], []
    for k, (peer_id, peer) in enumerate(_peers(x, y, c)):
        for a, (src, land) in enumerate(zip(src_refs, land_refs)):
            sems = dict(send_sem=send_sems.at[7 * a + k], recv_sem=recv_sems.at[7 * a + k],
                        device_id=peer_id, device_id_type=MESH_ID)
            mine = src.at[peer] if scatter else src
            sends.append(pltpu.make_async_remote_copy(src_ref=mine, dst_ref=land.at[me], **sems))
            recvs.append(pltpu.make_async_remote_copy(src_ref=mine, dst_ref=land.at[peer], **sems))
    return sends, recvs


def _exchange_start(srcs, scatter, name, after=None):
    na = len(srcs)
    lands = [lax.empty(s.shape if scatter else (N_DEV,) + s.shape, s.dtype) for s in srcs]
    extra = [] if after is None else [after]

    def body(*refs):
        src_refs, land_refs = refs[:na], refs[na:2 * na]
        send_sems, recv_sems = refs[2 * na + len(extra)], refs[2 * na + len(extra) + 1]
        token = refs[-1]
        sends, _ = _split_copies(src_refs, land_refs, send_sems, recv_sems, scatter)
        for cp in sends:
            cp.start()
        token[...] = jnp.zeros_like(token)

    hbm = lambda a: pltpu.HBM(a.shape, a.dtype)
    out = pl.pallas_call(
        body, name=name,
        out_shape=(pltpu.SemaphoreType.DMA((7 * na,)), pltpu.SemaphoreType.DMA((7 * na,)),
                   *[hbm(s) for s in srcs], *[hbm(l) for l in lands], jax.ShapeDtypeStruct((8, LANES), F32)),
        in_specs=[_HBM] * (2 * na) + [pl.BlockSpec(memory_space=pl.ANY)] * len(extra),
        out_specs=(_SEM, _SEM, *[_HBM] * (2 * na), pl.BlockSpec(memory_space=pltpu.VMEM)),
        input_output_aliases={i: 2 + i for i in range(2 * na)},
        compiler_params=pltpu.CompilerParams(has_side_effects=_DATAFLOW),
    )(*[pltpu.with_memory_space_constraint(s, pltpu.HBM) for s in srcs],
      *[pltpu.with_memory_space_constraint(l, pltpu.HBM) for l in lands], *extra)
    return (out[0], out[1], list(out[2:2 + na]), list(out[2 + na:2 + 2 * na])), out[-1]


def _exchange_wait(handle, after, scatter, name):
    send_sems, recv_sems, srcs, lands = handle
    na = len(srcs)

    def body(*refs):
        src_refs, land_refs = refs[:na], refs[na:2 * na]
        s_sems, r_sems = refs[2 * na], refs[2 * na + 1]
        sends, recvs = _split_copies(src_refs, land_refs, s_sems, r_sems, scatter)
        for cp in sends:
            cp.wait_send()
        for cp in recvs:
            cp.wait_recv()

    hbm = lambda a: pltpu.HBM(a.shape, a.dtype)
    out = pl.pallas_call(
        body, name=name, out_shape=(*[hbm(s) for s in srcs], *[hbm(l) for l in lands]),
        in_specs=[_HBM] * (2 * na) + [_SEM, _SEM, pl.BlockSpec(memory_space=pl.ANY)],
        out_specs=tuple([_HBM] * (2 * na)), input_output_aliases={i: i for i in range(2 * na)},
        compiler_params=pltpu.CompilerParams(has_side_effects=_DATAFLOW),
    )(*srcs, *lands, send_sems, recv_sems, after)
    return list(out[:na]), list(out[na:])


def _own_slot(landed, own):
    me = 4 * lax.axis_index("x") + 2 * lax.axis_index("y") + lax.axis_index("c")
    return lax.dynamic_update_slice_in_dim(landed, own[None], me, axis=0)


def _adam_update(parts, w, m, v, name, tr=256):
    _, r, c = w.shape
    tr = _pick_rows(r, tr)
    cp = parts.shape[2]

    def body(p_ref, w_ref, m_ref, v_ref, g_ref, d_ref, nm_ref, nv_ref):
        g = p_ref[0, :, pl.ds(0, c)].astype(F32)
        for i in range(1, N_DEV):
            g = g + p_ref[i, :, pl.ds(0, c)].astype(F32)
        delta, nm, nv = _adamw(w_ref[0], g, m_ref[0], v_ref[0])
        g_ref[0] = g
        d_ref[0] = delta
        nm_ref[0] = nm
        nv_ref[0] = nv

    rs = pl.BlockSpec((1, tr, c), lambda i: (0, i, 0))
    return pl.pallas_call(
        body, grid=(r // tr,), in_specs=[pl.BlockSpec((N_DEV, tr, cp), lambda i: (0, i, 0)), rs, rs, rs],
        out_specs=[rs] * 4, out_shape=[jax.ShapeDtypeStruct((1, r, c), F32)] * 4, name=name,
        compiler_params=_params(("parallel",)))(parts, w, m, v)


def _pick_rows(rows, target):
    if rows <= target:
        return rows
    t = target
    while t >= 16:
        if rows % t == 0:
            return t
        t -= 16
    return rows


BIG = ("w_in", "w_branch_dn", "w_branch_swa", "w_out", "w_gate", "w_up", "w_down")
IN_SHARD, IN_WIRE = D_IN // N_DEV, 640
FF_SHARD, FF_WIRE = D_FF // N_DEV, 384
D_FFP = N_DEV * FF_WIRE
BIG_SHAPES = {"w_in": ((D_MODEL, IN_SHARD), (D_MODEL, IN_WIRE)),
              "w_branch_dn": ((DN_WIDTH, LANES), (DN_WIDTH, LANES)),
              "w_branch_swa": ((SWA_WIDTH, LANES), (SWA_WIDTH, LANES)),
              "w_out": ((LANES, D_MODEL), (LANES, D_MODEL)),
              "w_gate": ((D_MODEL, FF_SHARD), (D_MODEL, FF_WIRE)),
              "w_up": ((D_MODEL, FF_SHARD), (D_MODEL, FF_WIRE)),
              "w_down": ((FF_SHARD, D_MODEL), (FF_WIRE, D_MODEL))}
CONV_SHARD, CONV_WIRE = (DN_CONV, DN_QKV // N_DEV), (8, 256)


def _pad_to(a, shape):
    return jnp.pad(a, [(0, t - s) for s, t in zip(a.shape, shape)])


_IN_SEGS = ((R_GATE, 2048, P_GATE), (R_QKV, DN_QKV, P_QKV), (R_Z, DN_WIDTH, P_Z), (R_SQ, SWA_WIDTH, P_SQ),
            (R_SK, SWA_KVW, P_SK), (R_SV, SWA_KVW, P_SV), (R_B, 8, P_BA))


def _w_in_from_blocks(blocks):
    parts = []
    for rs, n, _ in _IN_SEGS:
        for dev in range(N_DEV):
            lo, hi = max(rs, IN_SHARD * dev), min(rs + n, IN_SHARD * (dev + 1))
            if lo < hi:
                parts.append(blocks[dev, :, lo - IN_SHARD * dev:hi - IN_SHARD * dev])
    parts.append(jnp.zeros((blocks.shape[1], P_WIDTH - P_BA - 8), blocks.dtype))
    return jnp.concatenate(parts, axis=1)


def _w_in_to_blocks(g):
    out = []
    for dev in range(N_DEV):
        parts = []
        for rs, n, ps in sorted(_IN_SEGS):
            lo, hi = max(rs, IN_SHARD * dev), min(rs + n, IN_SHARD * (dev + 1))
            if lo < hi:
                parts.append(g[:, ps + lo - rs:ps + hi - rs])
        parts.append(jnp.zeros((g.shape[0], IN_WIRE - IN_SHARD), g.dtype))
        out.append(jnp.concatenate(parts, axis=1))
    return jnp.stack(out)


SMALL = {"attn_norm": (0, (1, D_MODEL)), "ffn_norm": (1, (1, D_MODEL)), "dn_out_norm": (2, (1, DN_DIM)),
         "swa_q_norm": (3, (1, SWA_DIM)), "swa_k_norm": (4, (1, SWA_DIM)), "dn_a_log": (5, (1, DN_HEADS)),
         "dn_dt_bias": (6, (1, DN_HEADS)), "swa_sinks": (7, (1, SWA_HEADS)), "rel_bias": (8, (REL_BUCKETS, SWA_HEADS))}
SMALL_SHEET = (48, D_MODEL)


def _small_pack(grads):
    names = list(SMALL)

    def body(*refs):
        o_ref = refs[-1]
        o_ref[...] = jnp.zeros_like(o_ref)
        for n, ref in zip(names, refs):
            r0, (nr, nc) = SMALL[n]
            o_ref[r0:r0 + nr, 0:nc] = ref[...]

    return pl.pallas_call(
        body, in_specs=[pl.BlockSpec(memory_space=pltpu.VMEM)] * len(names),
        out_specs=pl.BlockSpec(memory_space=pltpu.VMEM), out_shape=jax.ShapeDtypeStruct(SMALL_SHEET, F32),
        name="small_pack", compiler_params=_params())(*[grads[n].reshape(SMALL[n][1]) for n in names])


def _small_update(sheets, w, m, v):
    names = list(SMALL)
    k = len(names)

    def body(*refs):
        p_ref = refs[0]
        ins, outs = refs[1:1 + 3 * k], refs[1 + 3 * k:]
        for t, n in enumerate(names):
            r0, (nr, nc) = SMALL[n]
            g = p_ref[0, r0:r0 + nr, 0:nc]
            for i in range(1, N_DEV):
                g = g + p_ref[i, r0:r0 + nr, 0:nc]
            delta, nm, nv = _adamw(ins[t][...], g, ins[k + t][...], ins[2 * k + t][...])
            for kind, val in enumerate((g, delta, nm, nv)):
                outs[kind * k + t][...] = val

    shapes = [jax.ShapeDtypeStruct(SMALL[n][1], F32) for n in names]
    vm = pl.BlockSpec(memory_space=pltpu.VMEM)
    res = pl.pallas_call(
        body, in_specs=[vm] * (1 + 3 * k), out_specs=[vm] * (4 * k), out_shape=shapes * 4, name="adam_small",
        compiler_params=_params(),
    )(sheets, *[d[n].reshape(SMALL[n][1]) for d in (w, m, v) for n in names])
    return {n: tuple(res[kind * k + t] for kind in range(4)) for t, n in enumerate(names)}


def kernel(x, attn_norm, w_in, dn_conv, dn_a_log, dn_dt_bias, dn_out_norm, swa_q_norm, swa_k_norm, swa_sinks, rel_bias, w_branch_dn, w_branch_swa, w_out, ffn_norm, w_gate, w_up, w_down, loss_target, m_attn_norm, m_w_in, m_dn_conv, m_dn_a_log, m_dn_dt_bias, m_dn_out_norm, m_swa_q_norm, m_swa_k_norm, m_swa_sinks, m_rel_bias, m_w_branch_dn, m_w_branch_swa, m_w_out, m_ffn_norm, m_w_gate, m_w_up, m_w_down, v_attn_norm, v_w_in, v_dn_conv, v_dn_a_log, v_dn_dt_bias, v_dn_out_norm, v_swa_q_norm, v_swa_k_norm, v_swa_sinks, v_rel_bias, v_w_branch_dn, v_w_branch_swa, v_w_out, v_ffn_norm, v_w_gate, v_w_up, v_w_down):
    args = dict(locals())
    S = x.shape[1]
    xs = x.reshape(S, D_MODEL)
    target = loss_target.reshape(S, D_MODEL)

    w_loc = {n: args[n].reshape(BIG_SHAPES[n][0]) for n in BIG}
    conv_loc = dn_conv.reshape(CONV_SHARD)
    wire = {n: _pad_to(w_loc[n], BIG_SHAPES[n][1]).astype(BF16) for n in BIG}
    first = _all_gather([wire["w_in"], _pad_to(conv_loc, CONV_WIRE)])
    later = [n for n in BIG if n != "w_in"]
    rest_handle, rest_token = _exchange_start([wire[n] for n in later], False, "gather_rest_start", after=first[1])
    w_pad = _w_in_from_blocks(first[0])
    conv_w = jnp.concatenate([first[1][d, :DN_CONV, :CONV_SHARD[1]] for d in range(N_DEV)], axis=1)

    h = _norm_fwd(xs, attn_norm + rest_token[0, 0], "norm1_fwd")
    proj = _mm([(h, w_pad)], "nn", F32, "mm_in", 512, 1664, j_outer=True)
    qkvn = _dn_conv_fwd(proj, conv_w)
    beta, g = _dn_gate_fwd(proj, dn_a_log, dn_dt_bias)
    u, w, qe, kd, qk, egl = _dn_prep_fwd(qkvn, g, beta)
    o, states = _dn_scan_fwd(u, w, qe, kd, qk, egl)
    y_dn = _dn_out_fwd(o, proj, dn_out_norm)
    bias = _bias_fwd(rel_bias)
    y_swa = _swa_fwd(proj, swa_q_norm, swa_k_norm, swa_sinks, bias)
    rest_src, rest_land = _exchange_wait(rest_handle, y_swa, False, "gather_rest_wait")
    G = {n: _own_slot(land, src) for n, src, land in zip(later, rest_src, rest_land)}
    w_bdn, w_bswa, w_g, w_u = G["w_branch_dn"], G["w_branch_swa"], G["w_gate"], G["w_up"]
    w_o = G["w_out"].reshape(D_MODEL, D_MODEL)
    w_d = G["w_down"].reshape(D_FFP, D_MODEL)
    a_dn = _mm([(y_dn, w_bdn)], "nn", F32, "mm_branch_dn", 1024, D_MODEL, b_blocks=True)
    a_swa = _mm([(y_swa, w_bswa)], "nn", F32, "mm_branch_swa", 1024, D_MODEL, b_blocks=True)
    merged = _merge_fwd(proj, a_dn, a_swa)
    t_out = _mm([(merged, w_o)], "nn", F32, "mm_out", 1024, D_MODEL)
    x1, h2 = _resid_norm_fwd(xs, t_out, ffn_norm, "norm2_fwd")
    gate = _mm([(h2, w_g)], "nn", F32, "mm_gate", 1024, 768, j_outer=True, b_blocks=True)
    up = _mm([(h2, w_u)], "nn", F32, "mm_up", 1024, 768, j_outer=True, b_blocks=True)
    act = _act_fwd(gate, up)
    f = _mm([(act, w_d)], "nn", F32, "mm_down", 512, D_MODEL)
    dy, dy_b, loss_local = _loss_fwd_bwd(x1, f, target)

    dact = _mm([(dy_b, w_d)], "nt", F32, "mm_dact", 1024, 1024, j_outer=True)
    g_w_down = _mm([(act, dy_b)], "tn", BF16, "mm_dw_down", 768, D_MODEL, j_outer=True)
    g_w_down = g_w_down.reshape(N_DEV, FF_WIRE, D_MODEL)
    dgate, dup = _act_bwd(gate, up, dact)
    dh2 = _mm([(dgate, w_g), (dup, w_u)], "nt", F32, "mm_dh2", 512, 512, j_outer=True, b_blocks=True)
    g_w_gate = _mm([(h2, dgate)], "tn", BF16, "mm_dw_gate", D_MODEL, 768, out_blocks=True)
    g_w_up = _mm([(h2, dup)], "tn", BF16, "mm_dw_up", D_MODEL, 768, out_blocks=True)
    ffn_handle, ffn_token = _exchange_start([g_w_down, g_w_gate, g_w_up], True, "scatter_ffn_start")
    dx1, dx1_b, g_ffn_norm = _norm_bwd(x1, [dh2], dy, ffn_norm + ffn_token[0, 0], "norm2_bwd")
    dmerged = _mm([(dx1_b, w_o)], "nt", F32, "mm_dmerged", 1024, D_MODEL)
    g_w_out = _mm([(merged, dx1_b)], "tn", BF16, "mm_dw_out", 512, D_MODEL, j_outer=True)
    g_w_out = g_w_out.reshape(N_DEV, LANES, D_MODEL)
    dg0, dg1, da_dn, da_swa = _merge_bwd(proj, a_dn, a_swa, dmerged)
    dy_dn = _mm([(da_dn, w_bdn)], "nt", F32, "mm_dy_dn", 1024, DN_WIDTH, b_blocks=True)
    dy_swa = _mm([(da_swa, w_bswa)], "nt", F32, "mm_dy_swa", 1024, SWA_WIDTH, b_blocks=True)
    g_w_bdn = _mm([(y_dn, da_dn)], "tn", BF16, "mm_dw_branch_dn", DN_WIDTH, 512, out_blocks=True)
    g_w_bswa = _mm([(y_swa, da_swa)], "tn", BF16, "mm_dw_branch_swa", SWA_WIDTH, 512, out_blocks=True)
    dsq, dsk, dsv, g_q_norm, g_k_norm, g_sinks, dbias = _swa_bwd(proj, swa_q_norm, swa_k_norm, swa_sinks, bias, dy_swa)
    g_rel_bias = _bias_bwd(dbias)[:, :REL_BUCKETS].T
    mix_handle, mix_token = _exchange_start([g_w_out, g_w_bdn, g_w_bswa], True, "scatter_mix_start")
    do, dz, g_out_norm = _dn_out_bwd(o, proj, dn_out_norm + mix_token[0, 0], dy_dn)
    du, dw, dqe, dkd, dqk, degl = _dn_scan_bwd(u, w, qe, kd, qk, egl, states, do)
    dqkvn, dgd, dbeta = _dn_prep_bwd(qkvn, g, beta, du, dw, dqe, dkd, dqk, degl)
    dba, dal, ddt = _dn_gate_bwd(proj, dn_a_log, dn_dt_bias, dbeta, dgd)
    g_a_log = dal.reshape(DN_HEADS, DN_DIM).sum(axis=1)
    g_dt_bias = ddt.reshape(DN_HEADS, DN_DIM).sum(axis=1)
    dqkv, g_conv = _dn_conv_bwd(proj, conv_w, dqkvn)
    dproj = jnp.concatenate([dg0, dg1, dqkv, dz, dsq, dsk.astype(BF16), dsv.astype(BF16), dba], axis=1)
    g_w_in = _w_in_to_blocks(_mm([(h, dproj)], "tn", BF16, "mm_dw_in", 512, 1664, j_outer=True))
    in_handle, in_token = _exchange_start([g_w_in], True, "scatter_in_start")
    dh = _mm([(dproj, w_pad)], "nt", F32, "mm_dh", 512, D_MODEL)
    dx, _, g_attn_norm = _norm_bwd(xs, [dh], dx1, attn_norm + in_token[0, 0], "norm1_bwd")

    g_small = {"attn_norm": g_attn_norm, "ffn_norm": g_ffn_norm, "rel_bias": g_rel_bias, "dn_out_norm": g_out_norm,
               "swa_q_norm": g_q_norm, "swa_k_norm": g_k_norm, "dn_a_log": g_a_log, "dn_dt_bias": g_dt_bias,
               "swa_sinks": g_sinks}
    sheets, conv_all = _all_gather([_small_pack(g_small), _pad_to(g_conv, (8, DN_QKV))], name="all_gather_small")
    me = 4 * lax.axis_index("x") + 2 * lax.axis_index("y") + lax.axis_index("c")
    recv_small = sheets
    received = {}
    for handle, group, name in ((ffn_handle, ("w_down", "w_gate", "w_up"), "scatter_ffn_wait"),
                                (mix_handle, ("w_out", "w_branch_dn", "w_branch_swa"), "scatter_mix_wait"),
                                (in_handle, ("w_in",), "scatter_in_wait")):
        srcs, lands = _exchange_wait(handle, recv_small, True, name)
        for n, src, land in zip(group, srcs, lands):
            received[n] = _own_slot(land, lax.dynamic_index_in_dim(src, me, 0, keepdims=False))

    outs = {}
    for n in BIG:
        outs[n] = _adam_update(received[n], args[n], args["m_" + n], args["v_" + n], "adam_" + n)
    conv_parts = lax.dynamic_slice(conv_all, (0, 0, me * CONV_SHARD[1]), (N_DEV,) + CONV_SHARD)
    outs["dn_conv"] = _adam_update(conv_parts, dn_conv, m_dn_conv, v_dn_conv, "adam_dn_conv")
    outs.update(_small_update(sheets, {n: args[n] for n in SMALL}, {n: args["m_" + n] for n in SMALL},
                              {n: args["v_" + n] for n in SMALL}))

    names = ("attn_norm", "w_in", "dn_conv", "dn_a_log", "dn_dt_bias", "dn_out_norm", "swa_q_norm", "swa_k_norm",
             "swa_sinks", "rel_bias", "w_branch_dn", "w_branch_swa", "w_out", "ffn_norm", "w_gate", "w_up", "w_down")
    results = []
    for kind in range(4):
        results += [outs[n][kind].reshape(args[n].shape) for n in names]

    loss = lax.psum(loss_local[0, 0], ("x", "y", "c"))
    return (loss, dx.reshape(x.shape), *results)
```

```python
import math

import numpy as np
import jax
import jax.numpy as jnp
from jax import lax
from jax.experimental import pallas as pl
from jax.experimental.pallas import tpu as pltpu

F32 = jnp.float32
BF16 = jnp.bfloat16
HI = lax.Precision.HIGHEST

D_MODEL = 1024
DN_HEADS = 4
DN_DIM = 128
DN_WIDTH = 512
DN_QKV = 1536
DN_CONV = 4
CHUNK = 64
SWA_HEADS = 8
SWA_KV = 2
SWA_GROUP = 4
SWA_DIM = 64
SWA_WIDTH = 512
SWA_KVW = 128
WINDOW = 128
BLOCK = 128
REL_BUCKETS = 32
REL_MAX_DIST = 128
D_FF = 2816
D_IN = 4872
EPS = 1e-6
N_DEV = 8

ADAM_LR = 0.001
ADAM_B1 = 0.9
ADAM_B2 = 0.999
ADAM_EPS = 1e-08
ADAM_WD = 0.01
ADAM_STEP = 10

P_GATE, P_QKV, P_Z, P_SQ, P_SK, P_SV, P_BA = 0, 2048, 3584, 4096, 4608, 4736, 4864
P_WIDTH = 4992
R_QKV, R_Z, R_B, R_A, R_SQ, R_SK, R_SV, R_GATE = 0, 1536, 2048, 2052, 2056, 2568, 2696, 2824

VMEM_LIMIT = 56 * 1024 * 1024
LANES = 128
MESH_ID = pl.DeviceIdType.MESH


def _params(sem=None):
    return pltpu.CompilerParams(dimension_semantics=sem, vmem_limit_bytes=VMEM_LIMIT)


def _pick(dim, target):
    if dim <= target:
        return dim
    t = target - target % LANES
    while t >= LANES:
        if dim % t == 0:
            return t
        t -= LANES
    return dim


_DIMS = {"nn": (((1,), (0,)), ((), ())), "nt": (((1,), (1,)), ((), ())), "tn": (((0,), (0,)), ((), ()))}


def _mm(pairs, mode, out_dtype, name, bm, bn, j_outer=False, b_blocks=False, out_blocks=False):
    a0, b0 = pairs[0]
    cb = b0.shape[2] if b_blocks else None
    b_shape = (b0.shape[1], N_DEV * cb) if b_blocks else b0.shape
    if mode == "nn":
        (M, K), (K2, N) = a0.shape, b_shape
    elif mode == "nt":
        (M, K), (N, K2) = a0.shape, b_shape
    else:
        (K, M), (K2, N) = a0.shape, b_shape
    bm, bn = min(bm, M), min(bn, N)
    assert K == K2 and M % bm == 0 and N % bn == 0, (name, a0.shape, b0.shape, bm, bn)
    co = N // N_DEV
    assert not out_blocks or bn % co == 0
    dims = _DIMS[mode]
    n = len(pairs)

    def body(*refs):
        o_ref = refs[2 * n]
        acc = None
        for t in range(n):
            b_ref = refs[2 * t + 1]
            b = jnp.concatenate([b_ref[d] for d in range(b_ref.shape[0])], axis=1) if b_blocks else b_ref[...]
            p = lax.dot_general(refs[2 * t][...].astype(BF16), b.astype(BF16), dims, preferred_element_type=F32)
            acc = p if acc is None else acc + p
        if out_blocks:
            for d in range(bn // co):
                o_ref[d] = acc[:, d * co:(d + 1) * co].astype(out_dtype)
        else:
            o_ref[...] = acc.astype(out_dtype)

    def ij(f):
        return (lambda j, i: f(i, j)) if j_outer else f

    a_spec = pl.BlockSpec((K, bm), ij(lambda i, j: (0, i))) if mode == "tn" else pl.BlockSpec((bm, K), ij(lambda i, j: (i, 0)))
    if b_blocks and mode == "nt":
        b_spec = pl.BlockSpec((N_DEV, bn, cb), ij(lambda i, j: (0, j, 0)))
    elif b_blocks:
        b_spec = pl.BlockSpec((bn // cb, K, cb), ij(lambda i, j: (j, 0, 0)))
    elif mode == "nt":
        b_spec = pl.BlockSpec((bn, K), ij(lambda i, j: (j, 0)))
    else:
        b_spec = pl.BlockSpec((K, bn), ij(lambda i, j: (0, j)))
    if out_blocks:
        out_spec = pl.BlockSpec((bn // co, bm, co), ij(lambda i, j: (j, i, 0)))
        out_shape = jax.ShapeDtypeStruct((N_DEV, M, co), out_dtype)
    else:
        out_spec = pl.BlockSpec((bm, bn), ij(lambda i, j: (i, j)))
        out_shape = jax.ShapeDtypeStruct((M, N), out_dtype)
    grid = (N // bn, M // bm) if j_outer else (M // bm, N // bn)
    return pl.pallas_call(
        body, grid=grid, in_specs=[a_spec, b_spec] * n, out_specs=out_spec, out_shape=out_shape, name=name,
        compiler_params=_params(("parallel", "parallel")),
    )(*[x for pair in pairs for x in pair])


def _mm_fused(pairs, mode, name, bm, bn, epilogue, extras, out_dtypes, j_outer=False, b_blocks=False,
              with_sum=False):
    a0, b0 = pairs[0]
    cb = b0.shape[2] if b_blocks else None
    b_shape = (b0.shape[1], N_DEV * cb) if b_blocks else b0.shape
    if mode == "nn":
        (M, K), (K2, N) = a0.shape, b_shape
    else:
        (M, K), (N, K2) = a0.shape, b_shape
    bm, bn = min(bm, M), min(bn, N)
    assert mode in ("nn", "nt") and K == K2 and M % bm == 0 and N % bn == 0, (name, a0.shape, b0.shape)
    dims = _DIMS[mode]
    n, ne, no = len(pairs), len(extras), len(out_dtypes)

    def body(*refs):
        prods = []
        for t in range(n):
            b_ref = refs[2 * t + 1]
            b = jnp.concatenate([b_ref[d] for d in range(b_ref.shape[0])], axis=1) if b_blocks else b_ref[...]
            prods.append(lax.dot_general(refs[2 * t][...].astype(BF16), b.astype(BF16), dims,
                                         preferred_element_type=F32))
        results = epilogue(prods, [r[...] for r in refs[2 * n:2 * n + ne]])
        out_refs = refs[2 * n + ne:]
        for o_ref, val, dt in zip(out_refs, results, out_dtypes):
            o_ref[...] = val.astype(dt)
        if with_sum:
            s_ref = out_refs[no]

            @pl.when((pl.program_id(0) == 0) & (pl.program_id(1) == 0))
            def _():
                s_ref[...] = jnp.zeros_like(s_ref)

            s_ref[...] += results[no]

    def ij(f):
        return (lambda j, i: f(i, j)) if j_outer else f

    a_spec = pl.BlockSpec((bm, K), ij(lambda i, j: (i, 0)))
    if b_blocks and mode == "nt":
        b_spec = pl.BlockSpec((N_DEV, bn, cb), ij(lambda i, j: (0, j, 0)))
    elif b_blocks:
        b_spec = pl.BlockSpec((bn // cb, K, cb), ij(lambda i, j: (j, 0, 0)))
    elif mode == "nt":
        b_spec = pl.BlockSpec((bn, K), ij(lambda i, j: (j, 0)))
    else:
        b_spec = pl.BlockSpec((K, bn), ij(lambda i, j: (0, j)))
    e_specs = [pl.BlockSpec((1, bn), ij(lambda i, j: (0, j))) if first is None
               else pl.BlockSpec((bm, bn), ij(lambda i, j, first=first: (i, first + j))) for _, first in extras]
    tile = pl.BlockSpec((bm, bn), ij(lambda i, j: (i, j)))
    out_specs = [tile] * no + ([_full((1, 1))] if with_sum else [])
    out_shape = [jax.ShapeDtypeStruct((M, N), dt) for dt in out_dtypes]
    out_shape += [jax.ShapeDtypeStruct((1, 1), F32)] if with_sum else []
    grid = (N // bn, M // bm) if j_outer else (M // bm, N // bn)
    sem = ("arbitrary", "arbitrary") if with_sum else ("parallel", "parallel")
    return pl.pallas_call(
        body, grid=grid, in_specs=[a_spec, b_spec] * n + e_specs, out_specs=out_specs, out_shape=out_shape,
        name=name, compiler_params=_params(sem),
    )(*[x for pair in pairs for x in pair], *[arr for arr, _ in extras])


def _rms(x, gain):
    return x * lax.rsqrt(jnp.mean(x * x, axis=-1, keepdims=True) + EPS) * gain


def _silu(x):
    return x * jax.nn.sigmoid(x)


def _act(g, u):
    return _silu(g) * u


def _merge(g0, g1, a_dn, a_swa):
    return jax.nn.sigmoid(g0) * a_dn + jax.nn.sigmoid(g1) * a_swa


def _dn_post(c, is_v, q_scale):
    a = _silu(c)
    rs = lax.rsqrt(jnp.sum(a * a, axis=-1, keepdims=True) + EPS) * q_scale
    return a * jnp.where(is_v, 1.0, rs)


def _dn_out(o, z, gain):
    return _rms(o, gain) * _silu(z)


def _dot(a, b, dims=_DIMS["nn"], hi=False):
    if a.ndim == 3 or b.ndim == 3:
        batch = a.shape[0] if a.ndim == 3 else b.shape[0]
        a = a if a.ndim == 3 else jnp.broadcast_to(a, (batch,) + a.shape)
        b = b if b.ndim == 3 else jnp.broadcast_to(b, (batch,) + b.shape)
        ((ca,), (cb,)), _ = dims
        dims = (((ca + 1,), (cb + 1,)), ((0,), (0,)))
    if hi:
        return lax.dot_general(a, b, dims, precision=HI, preferred_element_type=F32)
    return lax.dot_general(a.astype(BF16), b.astype(BF16), dims, preferred_element_type=F32)


def _pieces(x):
    hi = x.astype(BF16)
    r1 = x - hi.astype(F32)
    mid = r1.astype(BF16)
    return hi, mid, (r1 - mid.astype(F32)).astype(BF16)


def _sel_left_impl(m, x):
    mb = m.astype(BF16)
    hi, mid, lo = _pieces(x)
    return _dot(mb, hi) + (_dot(mb, mid) + _dot(mb, lo))


@jax.custom_vjp
def _sel_left(m, mt, x):
    return _sel_left_impl(m, x)


_sel_left.defvjp(lambda m, mt, x: (_sel_left_impl(m, x), (m, mt)),
                 lambda res, ct: (jnp.zeros_like(res[0]), jnp.zeros_like(res[1]), _sel_left_impl(res[1], ct)))


def _sel_right_impl(x, s):
    sb = s.astype(BF16)
    hi, mid, lo = _pieces(x)
    return _dot(hi, sb) + (_dot(mid, sb) + _dot(lo, sb))


@jax.custom_vjp
def _sel_right(x, s, st):
    return _sel_right_impl(x, s)


_sel_right.defvjp(lambda x, s, st: (_sel_right_impl(x, s), (s, st)),
                  lambda res, ct: (_sel_right_impl(ct, res[1]), jnp.zeros_like(res[0]), jnp.zeros_like(res[1])))


def _sel_nt_impl(s, x):
    sb = s.astype(BF16)
    hi, mid, lo = _pieces(x)
    return _dot(sb, hi, _DIMS["nt"]) + (_dot(sb, mid, _DIMS["nt"]) + _dot(sb, lo, _DIMS["nt"]))


def _sel_tn_impl(x, s):
    sb = s.astype(BF16)
    hi, mid, lo = _pieces(x)
    return _dot(hi, sb, _DIMS["tn"]) + (_dot(mid, sb, _DIMS["tn"]) + _dot(lo, sb, _DIMS["tn"]))


@jax.custom_vjp
def _sel_nt(s, x):
    return _sel_nt_impl(s, x)


_sel_nt.defvjp(lambda s, x: (_sel_nt_impl(s, x), s),
               lambda s, ct: (jnp.zeros_like(s), _sel_tn_impl(ct, s)))


def _dot3_impl(a, b):
    a_hi, a_lo, _ = _pieces(a)
    b_hi, b_lo, _ = _pieces(b)
    return _dot(a_hi, b_hi) + (_dot(a_hi, b_lo) + _dot(a_lo, b_hi))


@jax.custom_vjp
def _dot3(a, b):
    return _dot3_impl(a, b)


_dot3.defvjp(lambda a, b: (_dot3_impl(a, b), (a, b)),
             lambda res, ct: (_dot(ct, res[1], _DIMS["nt"]), _dot(res[0], ct, _DIMS["tn"])))


def _inv_impl(a, eye, strict):
    t = eye - a
    p = _dot(a, a)
    for level in range(5):
        t = t + _dot(t, p)
        if level < 4:
            p = _dot(p, p)
    t = t + _dot3_impl(t, eye - _dot3_impl(eye + a, t))
    return jnp.where(strict > 0.5, t, eye)


@jax.custom_vjp
def _inv_unit_lower(a, eye, strict):
    return _inv_impl(a, eye, strict)


def _inv_bwd(res, ct):
    t, eye, strict = res
    da = -_dot(_dot(t, ct, _DIMS["tn"]), t, _DIMS["nt"])
    return da, jnp.zeros_like(eye), jnp.zeros_like(strict)


def _inv_fwd(a, eye, strict):
    t = _inv_impl(a, eye, strict)
    return t, (t, eye, strict)


_inv_unit_lower.defvjp(_inv_fwd, _inv_bwd)

GROUP = 4
GROUP_ROWS = GROUP * CHUNK


def _block_consts(n):
    ii = lax.broadcasted_iota(jnp.int32, (n, n), 0)
    jj = lax.broadcasted_iota(jnp.int32, (n, n), 1)
    shift = CHUNK.bit_length() - 1
    same = jnp.right_shift(ii, shift) == jnp.right_shift(jj, shift)
    return same & (ii >= jj), same & (ii <= jj), same & (ii > jj), same, ii == jj


def _lane0(n):
    s = (lax.broadcasted_iota(jnp.int32, (LANES, n), 0) == 0).astype(F32)
    st = (lax.broadcasted_iota(jnp.int32, (n, LANES), 1) == 0).astype(F32)
    return s, st


def _dn_group(q, k, v, g, beta):
    n = GROUP_ROWS
    low_b, upp_b, strict_b, same_b, eye_b = _block_consts(n)
    low, upp, same, eye = low_b.astype(F32), upp_b.astype(F32), same_b.astype(F32), eye_b.astype(F32)
    s, st = _lane0(n)
    gc = _sel_left(low, upp, g)
    gl = _sel_left(same, same, g)
    col = _sel_right(gc, s, st)
    row = _sel_nt(st, gc)
    decay = jnp.exp(jnp.where(low_b, col - row, -jnp.inf))
    kb = k * beta
    vb = v * beta
    a = jnp.where(strict_b, _dot(kb, k, _DIMS["nt"]) * decay, 0.0)
    t = _inv_unit_lower(a, eye, strict_b.astype(F32))
    u = _dot3(t, vb)
    w = _dot3(t, kb * jnp.exp(gc))
    return u, w, q * jnp.exp(gc), k * jnp.exp(gl - gc)


def _dn_chunk(q, k, g):
    ii = lax.broadcasted_iota(jnp.int32, (CHUNK, CHUNK), 0)
    jj = lax.broadcasted_iota(jnp.int32, (CHUNK, CHUNK), 1)
    low = (ii >= jj).astype(F32)
    upp = (ii <= jj).astype(F32)
    s, st = _lane0(CHUNK)
    gc = _sel_left(low, upp, g)
    col = _sel_right(gc, s, st)
    row = _sel_nt(st, gc)
    decay = jnp.exp(jnp.where(ii >= jj, col - row, -jnp.inf))
    qk = _dot(q, k, _DIMS["nt"]) * decay
    return qk, jnp.exp(jnp.sum(g, axis=-2, keepdims=True))


def _dn_step(s, u, w, qe, kd, qk, egl):
    v_new = u - _dot(w, s)
    o = _dot(qe, s) + _dot(qk, v_new)
    s_new = s * egl + _dot(kd, v_new, _DIMS["tn"])
    return s_new, o


def _swa_block(q, kband, vband, qg, kg, sinks, bias, mask):
    kn = _rms(kband, kg)
    qn = _rms(q, qg)
    logits = _dot(qn, kn, _DIMS["nt"]) * (SWA_DIM ** -0.5)
    logits = jnp.where(mask, logits + bias, -jnp.inf)
    m = jnp.maximum(jnp.max(logits, axis=-1, keepdims=True), sinks)
    p = jnp.exp(logits - m)
    denom = jnp.sum(p, axis=-1, keepdims=True) + jnp.exp(sinks - m)
    return _dot(p / denom, vband)


def _adamw(w, g, m, v):
    m = ADAM_B1 * m + (1.0 - ADAM_B1) * g
    v = ADAM_B2 * v + (1.0 - ADAM_B2) * jnp.square(g)
    m_hat = m / (1.0 - ADAM_B1 ** ADAM_STEP)
    v_hat = v / (1.0 - ADAM_B2 ** ADAM_STEP)
    delta = -ADAM_LR * (m_hat / (jnp.sqrt(v_hat) + ADAM_EPS) + ADAM_WD * w)
    return delta, m, v


def _row(tm, c, cb=0):
    return pl.BlockSpec((tm, c), lambda i, cb=cb: (i, cb))


def _full(shape):
    nd = len(shape)
    return pl.BlockSpec(shape, lambda *_, nd=nd: (0,) * nd)


def _norm_fwd(x, gain, name, tm=512):
    S = x.shape[0]

    def body(x_ref, g_ref, h_ref):
        h_ref[...] = _rms(x_ref[...], g_ref[...]).astype(BF16)

    return pl.pallas_call(
        body, grid=(S // tm,), in_specs=[_row(tm, D_MODEL), _full((1, D_MODEL))],
        out_specs=_row(tm, D_MODEL), out_shape=jax.ShapeDtypeStruct((S, D_MODEL), BF16),
        name=name, compiler_params=_params(("parallel",)))(x, gain)


def _norm_bwd(x, dh_list, dres, gain, name, tm=256):
    S = x.shape[0]
    n = len(dh_list)

    def body(*refs):
        x_ref, g_ref, r_ref = refs[0], refs[1], refs[2]
        dh_refs = refs[3:3 + n]
        dx_ref, dxb_ref, dg_ref = refs[3 + n], refs[4 + n], refs[5 + n]
        dh = dh_refs[0][...].astype(F32)
        for r in dh_refs[1:]:
            dh = dh + r[...].astype(F32)
        _, vjp = jax.vjp(_rms, x_ref[...], g_ref[...])
        dx, dg = vjp(dh)
        dx = dx + r_ref[...]
        dx_ref[...] = dx
        dxb_ref[...] = dx.astype(BF16)

        @pl.when(pl.program_id(0) == 0)
        def _():
            dg_ref[...] = jnp.zeros_like(dg_ref)

        dg_ref[...] += dg

    return pl.pallas_call(
        body, grid=(S // tm,),
        in_specs=[_row(tm, D_MODEL), _full((1, D_MODEL)), _row(tm, D_MODEL)] + [_row(tm, D_MODEL)] * n,
        out_specs=[_row(tm, D_MODEL), _row(tm, D_MODEL), _full((1, D_MODEL))],
        out_shape=[jax.ShapeDtypeStruct((S, D_MODEL), F32), jax.ShapeDtypeStruct((S, D_MODEL), BF16),
                   jax.ShapeDtypeStruct((1, D_MODEL), F32)],
        name=name, compiler_params=_params(("arbitrary",)))(x, gain, dres, *dh_list)


def _shift_down(x, s):
    row = lax.broadcasted_iota(jnp.int32, x.shape, 0)
    return jnp.where(row >= s, pltpu.roll(x, s, axis=0), 0.0)


def _shift_up(x, s):
    n = x.shape[0]
    row = lax.broadcasted_iota(jnp.int32, x.shape, 0)
    return jnp.where(row < n - s, pltpu.roll(x, n - s, axis=0), 0.0)


def _conv(x, w):
    out = w[DN_CONV - 1:DN_CONV] * x
    for s in range(1, DN_CONV):
        out = out + w[DN_CONV - 1 - s:DN_CONV - s] * _shift_down(x, s)
    return out


def _dn_conv_fwd(proj, conv_w):
    S = proj.shape[0]
    nb = DN_QKV // LANES

    def body(x_ref, w_ref, o_ref):
        j = pl.program_id(0)
        q_scale = jnp.where(j < DN_HEADS, DN_DIM ** -0.5, 1.0).astype(F32)
        o_ref[...] = _dn_post(_conv(x_ref[...], w_ref[...]), j >= 2 * DN_HEADS, q_scale)

    return pl.pallas_call(
        body, grid=(nb,),
        in_specs=[pl.BlockSpec((S, LANES), lambda j: (0, P_QKV // LANES + j)),
                  pl.BlockSpec((DN_CONV, LANES), lambda j: (0, j))],
        out_specs=pl.BlockSpec((S, LANES), lambda j: (0, j)),
        out_shape=jax.ShapeDtypeStruct((S, DN_QKV), F32), name="dn_conv_fwd",
        compiler_params=_params(("parallel",)))(proj, conv_w)


def _dn_conv_bwd(proj, conv_w, dqkvn):
    S = proj.shape[0]
    nb = DN_QKV // LANES

    def body(x_ref, w_ref, d_ref, dx_ref, dw_ref):
        j = pl.program_id(0)
        q_scale = jnp.where(j < DN_HEADS, DN_DIM ** -0.5, 1.0).astype(F32)
        x = x_ref[...]
        w = w_ref[...]
        _, vjp = jax.vjp(lambda c: _dn_post(c, j >= 2 * DN_HEADS, q_scale), _conv(x, w))
        (dc,) = vjp(d_ref[0])
        dx = w[DN_CONV - 1:DN_CONV] * dc
        dw_ref[DN_CONV - 1:DN_CONV, :] = jnp.sum(dc * x, axis=0, keepdims=True)
        for s in range(1, DN_CONV):
            dx = dx + w[DN_CONV - 1 - s:DN_CONV - s] * _shift_up(dc, s)
            dw_ref[DN_CONV - 1 - s:DN_CONV - s, :] = jnp.sum(dc * _shift_down(x, s), axis=0, keepdims=True)
        dx_ref[...] = dx.astype(BF16)

    return pl.pallas_call(
        body, grid=(nb,),
        in_specs=[pl.BlockSpec((S, LANES), lambda j: (0, P_QKV // LANES + j)),
                  pl.BlockSpec((DN_CONV, LANES), lambda j: (0, j)),
                  pl.BlockSpec((1, S, LANES), lambda j: (lax.div(j, DN_HEADS), 0, lax.rem(j, DN_HEADS)))],
        out_specs=[pl.BlockSpec((S, LANES), lambda j: (0, j)), pl.BlockSpec((DN_CONV, LANES), lambda j: (0, j))],
        out_shape=[jax.ShapeDtypeStruct((S, DN_QKV), BF16), jax.ShapeDtypeStruct((DN_CONV, DN_QKV), F32)],
        name="dn_conv_bwd", compiler_params=_params(("parallel",)))(proj, conv_w, dqkvn)


def _expanders():
    eb = np.zeros((LANES, DN_WIDTH), np.float32)
    ea = np.zeros((LANES, DN_WIDTH), np.float32)
    for h in range(DN_HEADS):
        eb[h, h * DN_DIM:(h + 1) * DN_DIM] = 1.0
        ea[DN_HEADS + h, h * DN_DIM:(h + 1) * DN_DIM] = 1.0
    return jnp.asarray(eb), jnp.asarray(ea), jnp.asarray(eb.T), jnp.asarray(ea.T)


def _dn_gate_args(a_log, dt_bias):
    alog = jnp.repeat(a_log.reshape(1, DN_HEADS), DN_DIM, axis=1)
    dtb = jnp.repeat(dt_bias.reshape(1, DN_HEADS), DN_DIM, axis=1)
    return _expanders() + (alog, dtb)


def _dn_gate_specs(tm):
    return [_row(tm, LANES, P_BA // LANES), _full((LANES, DN_WIDTH)), _full((LANES, DN_WIDTH)),
            _full((DN_WIDTH, LANES)), _full((DN_WIDTH, LANES)), _full((1, DN_WIDTH)), _full((1, DN_WIDTH))]


def _dn_gate_fn(ba, eb, ea, ebt, eat, alog, dtb):
    beta = jax.nn.sigmoid(_sel_right(ba, eb, ebt))
    g = -jnp.exp(alog) * jax.nn.softplus(_sel_right(ba, ea, eat) + dtb)
    return beta, g


def _dn_gate_fwd(proj, a_log, dt_bias, tm=512):
    S = proj.shape[0]
    args = _dn_gate_args(a_log, dt_bias)

    def body(ba_ref, eb_ref, ea_ref, ebt_ref, eat_ref, al_ref, dt_ref, beta_ref, g_ref):
        beta, g = _dn_gate_fn(ba_ref[...], eb_ref[...], ea_ref[...], ebt_ref[...], eat_ref[...], al_ref[...],
                              dt_ref[...])
        beta_ref[...] = beta
        g_ref[...] = g

    return pl.pallas_call(
        body, grid=(S // tm,), in_specs=_dn_gate_specs(tm), out_specs=[_row(tm, DN_WIDTH), _row(tm, DN_WIDTH)],
        out_shape=[jax.ShapeDtypeStruct((S, DN_WIDTH), F32), jax.ShapeDtypeStruct((S, DN_WIDTH), F32)],
        name="dn_gate_fwd", compiler_params=_params(("parallel",)))(proj, *args)


def _dn_gate_bwd(proj, a_log, dt_bias, dbeta, dg, tm=512):
    S = proj.shape[0]
    args = _dn_gate_args(a_log, dt_bias)

    def body(ba_ref, eb_ref, ea_ref, ebt_ref, eat_ref, al_ref, dt_ref, dbeta_ref, dg_ref, dba_ref, dal_ref, ddt_ref):
        eb, ea, ebt, eat = eb_ref[...], ea_ref[...], ebt_ref[...], eat_ref[...]
        _, vjp = jax.vjp(lambda ba, al, dt: _dn_gate_fn(ba, eb, ea, ebt, eat, al, dt), ba_ref[...], al_ref[...],
                         dt_ref[...])
        dba, dal, ddt = vjp((dbeta_ref[...], dg_ref[...]))
        dba_ref[...] = dba.astype(BF16)

        @pl.when(pl.program_id(0) == 0)
        def _():
            dal_ref[...] = jnp.zeros_like(dal_ref)
            ddt_ref[...] = jnp.zeros_like(ddt_ref)

        dal_ref[...] += dal
        ddt_ref[...] += ddt

    return pl.pallas_call(
        body, grid=(S // tm,), in_specs=_dn_gate_specs(tm) + [_row(tm, DN_WIDTH), _row(tm, DN_WIDTH)],
        out_specs=[_row(tm, LANES), _full((1, DN_WIDTH)), _full((1, DN_WIDTH))],
        out_shape=[jax.ShapeDtypeStruct((S, LANES), BF16), jax.ShapeDtypeStruct((1, DN_WIDTH), F32),
                   jax.ShapeDtypeStruct((1, DN_WIDTH), F32)],
        name="dn_gate_bwd", compiler_params=_params(("arbitrary",)))(proj, *args, dbeta, dg)


PREP_GROUPS = 4
PREP_CHUNKS = GROUP * PREP_GROUPS


def _dn_prep_specs():
    rows = PREP_CHUNKS * CHUNK
    q = pl.BlockSpec((rows, LANES), lambda h, c: (c, h))
    k = pl.BlockSpec((rows, LANES), lambda h, c: (c, DN_HEADS + h))
    v = pl.BlockSpec((rows, LANES), lambda h, c: (c, 2 * DN_HEADS + h))
    qk = pl.BlockSpec((1, rows, CHUNK), lambda h, c: (h, c, 0))
    egl = pl.BlockSpec((1, PREP_CHUNKS, 1, LANES), lambda h, c: (h, c, 0, 0))
    return q, k, v, qk, egl


def _dn_prep_fwd(qkvn, g, beta):
    S = qkvn.shape[0]
    nc = S // CHUNK
    q, k, v, qks, egl = _dn_prep_specs()

    def body(q_ref, k_ref, v_ref, g_ref, b_ref, u_ref, w_ref, qe_ref, kd_ref, qk_ref, egl_ref):
        rows = PREP_CHUNKS * CHUNK
        grp = (PREP_GROUPS, GROUP_ROWS, LANES)
        chk = (PREP_CHUNKS, CHUNK, LANES)
        q, k, g = q_ref[...], k_ref[...], g_ref[...]
        u, w, qe, kd = _dn_group(q.reshape(grp), k.reshape(grp), v_ref[...].reshape(grp), g.reshape(grp),
                                 b_ref[...].reshape(grp))
        u_ref[...] = u.reshape(rows, LANES)
        w_ref[...] = w.reshape(rows, LANES)
        qe_ref[...] = qe.reshape(rows, LANES)
        kd_ref[...] = kd.reshape(rows, LANES)
        qk, e = _dn_chunk(q.reshape(chk), k.reshape(chk), g.reshape(chk))
        qk_ref[0] = qk.reshape(rows, CHUNK)
        egl_ref[0] = e

    wide = jax.ShapeDtypeStruct((S, DN_WIDTH), F32)
    return pl.pallas_call(
        body, grid=(DN_HEADS, nc // PREP_CHUNKS), in_specs=[q, k, v, q, q],
        out_specs=[q, q, q, q, qks, egl],
        out_shape=[wide, wide, wide, wide, jax.ShapeDtypeStruct((DN_HEADS, S, CHUNK), F32),
                   jax.ShapeDtypeStruct((DN_HEADS, nc, 1, LANES), F32)],
        name="dn_prep_fwd", compiler_params=_params(("parallel", "parallel")))(qkvn, qkvn, qkvn, g, beta)


def _dn_prep_bwd(qkvn, g, beta, du, dw, dqe, dkd, dqk, degl):
    S = qkvn.shape[0]
    nc = S // CHUNK
    q, k, v, qks, egl = _dn_prep_specs()

    def body(q_ref, k_ref, v_ref, g_ref, b_ref, du_ref, dw_ref, dqe_ref, dkd_ref, dqk_ref, degl_ref,
             dqkv_ref, dg_ref, db_ref):
        rows = PREP_CHUNKS * CHUNK
        grp = (PREP_GROUPS, GROUP_ROWS, LANES)
        chk = (PREP_CHUNKS, CHUNK, LANES)
        q, k, g = q_ref[...], k_ref[...], g_ref[...]
        _, vjp = jax.vjp(_dn_group, q.reshape(grp), k.reshape(grp), v_ref[...].reshape(grp), g.reshape(grp),
                         b_ref[...].reshape(grp))
        dq, dk, dv, dg, db = vjp((du_ref[...].reshape(grp), dw_ref[...].reshape(grp), dqe_ref[...].reshape(grp),
                                  dkd_ref[...].reshape(grp)))
        _, vjp = jax.vjp(_dn_chunk, q.reshape(chk), k.reshape(chk), g.reshape(chk))
        dq2, dk2, dg2 = vjp((dqk_ref[0].reshape(PREP_CHUNKS, CHUNK, CHUNK), degl_ref[0]))
        dqkv_ref[0] = dq.reshape(rows, LANES) + dq2.reshape(rows, LANES)
        dqkv_ref[1] = dk.reshape(rows, LANES) + dk2.reshape(rows, LANES)
        dqkv_ref[2] = dv.reshape(rows, LANES)
        dg_ref[...] = dg.reshape(rows, LANES) + dg2.reshape(rows, LANES)
        db_ref[...] = db.reshape(rows, LANES)

    wide = jax.ShapeDtypeStruct((S, DN_WIDTH), F32)
    rows = PREP_CHUNKS * CHUNK
    return pl.pallas_call(
        body, grid=(DN_HEADS, nc // PREP_CHUNKS), in_specs=[q, k, v, q, q, q, q, q, q, qks, egl],
        out_specs=[pl.BlockSpec((3, rows, LANES), lambda h, c: (0, c, h)), q, q],
        out_shape=[jax.ShapeDtypeStruct((3, S, DN_WIDTH), F32), wide, wide],
        name="dn_prep_bwd", compiler_params=_params(("parallel", "parallel")),
    )(qkvn, qkvn, qkvn, g, beta, du, dw, dqe, dkd, dqk, degl)


def _dn_scan_specs(nc, reverse):
    def cidx(c):
        return nc - 1 - c if reverse else c

    hc = pl.BlockSpec((CHUNK, DN_WIDTH), lambda c: (cidx(c), 0))
    qk = pl.BlockSpec((DN_HEADS, CHUNK, CHUNK), lambda c: (0, cidx(c), 0))
    egl = pl.BlockSpec((DN_HEADS, 1, 1, LANES), lambda c: (0, cidx(c), 0, 0))
    st = pl.BlockSpec((DN_HEADS, 1, DN_DIM, DN_DIM), lambda c: (0, cidx(c), 0, 0))
    return hc, qk, egl, st


def _heads(ref):
    return jnp.stack([ref[:, pl.ds(h * DN_DIM, DN_DIM)] for h in range(DN_HEADS)])


def _dn_scan_fwd(u, w, qe, kd, qk, egl):
    S = u.shape[0]
    nc = S // CHUNK
    hc, qks, egls, st = _dn_scan_specs(nc, False)

    def body(u_ref, w_ref, qe_ref, kd_ref, qk_ref, egl_ref, o_ref, st_ref, s_scr):
        @pl.when(pl.program_id(0) == 0)
        def _():
            s_scr[...] = jnp.zeros_like(s_scr)

        s = s_scr[...]
        st_ref[:, 0] = s
        s_new, o = _dn_step(s, _heads(u_ref), _heads(w_ref), _heads(qe_ref), _heads(kd_ref), qk_ref[...],
                            egl_ref[:, 0])
        for h in range(DN_HEADS):
            o_ref[:, pl.ds(h * DN_DIM, DN_DIM)] = o[h]
        s_scr[...] = s_new

    return pl.pallas_call(
        body, grid=(nc,), in_specs=[hc, hc, hc, hc, qks, egls], out_specs=[hc, st],
        out_shape=[jax.ShapeDtypeStruct((S, DN_WIDTH), F32), jax.ShapeDtypeStruct((DN_HEADS, nc, DN_DIM, DN_DIM), F32)],
        scratch_shapes=[pltpu.VMEM((DN_HEADS, DN_DIM, DN_DIM), F32)], name="dn_scan_fwd",
        compiler_params=_params(("arbitrary",)))(u, w, qe, kd, qk, egl)


def _dn_scan_bwd(u, w, qe, kd, qk, egl, states, do):
    S = u.shape[0]
    nc = S // CHUNK
    hc, qks, egls, st = _dn_scan_specs(nc, True)

    def body(u_ref, w_ref, qe_ref, kd_ref, qk_ref, egl_ref, st_ref, do_ref,
             du_ref, dw_ref, dqe_ref, dkd_ref, dqk_ref, degl_ref, ds_scr):
        @pl.when(pl.program_id(0) == 0)
        def _():
            ds_scr[...] = jnp.zeros_like(ds_scr)

        _, vjp = jax.vjp(_dn_step, st_ref[:, 0], _heads(u_ref), _heads(w_ref), _heads(qe_ref), _heads(kd_ref),
                         qk_ref[...], egl_ref[:, 0])
        ds, du, dw, dqe, dkd, dqk, degl = vjp((ds_scr[...], _heads(do_ref)))
        ds_scr[...] = ds
        dqk_ref[...] = dqk
        degl_ref[:, 0] = degl
        for h in range(DN_HEADS):
            cols = pl.ds(h * DN_DIM, DN_DIM)
            du_ref[:, cols] = du[h]
            dw_ref[:, cols] = dw[h]
            dqe_ref[:, cols] = dqe[h]
            dkd_ref[:, cols] = dkd[h]

    wide = jax.ShapeDtypeStruct((S, DN_WIDTH), F32)
    return pl.pallas_call(
        body, grid=(nc,), in_specs=[hc, hc, hc, hc, qks, egls, st, hc],
        out_specs=[hc, hc, hc, hc, qks, egls],
        out_shape=[wide, wide, wide, wide, jax.ShapeDtypeStruct((DN_HEADS, S, CHUNK), F32),
                   jax.ShapeDtypeStruct((DN_HEADS, nc, 1, LANES), F32)],
        scratch_shapes=[pltpu.VMEM((DN_HEADS, DN_DIM, DN_DIM), F32)], name="dn_scan_bwd",
        compiler_params=_params(("arbitrary",)))(u, w, qe, kd, qk, egl, states, do)


def _dn_out_fwd(o, proj, gain, tm=512):
    S = o.shape[0]

    def body(o_ref, z_ref, g_ref, y_ref):
        y_ref[...] = _dn_out(o_ref[...], z_ref[...], g_ref[...]).astype(BF16)

    hs = pl.BlockSpec((tm, LANES), lambda i, h: (i, h))
    zs = pl.BlockSpec((tm, LANES), lambda i, h: (i, P_Z // LANES + h))
    return pl.pallas_call(
        body, grid=(S // tm, DN_HEADS), in_specs=[hs, zs, _full((1, DN_DIM))], out_specs=hs,
        out_shape=jax.ShapeDtypeStruct((S, DN_WIDTH), BF16), name="dn_out_fwd",
        compiler_params=_params(("parallel", "parallel")))(o, proj, gain)


def _dn_out_bwd(o, proj, gain, dy, tm=512):
    S = o.shape[0]

    def body(o_ref, z_ref, g_ref, dy_ref, do_ref, dz_ref, dg_ref):
        _, vjp = jax.vjp(_dn_out, o_ref[...], z_ref[...], g_ref[...])
        do, dz, dg = vjp(dy_ref[...])
        do_ref[...] = do
        dz_ref[...] = dz.astype(BF16)

        @pl.when((pl.program_id(0) == 0) & (pl.program_id(1) == 0))
        def _():
            dg_ref[...] = jnp.zeros_like(dg_ref)

        dg_ref[...] += dg

    hs = pl.BlockSpec((tm, LANES), lambda i, h: (i, h))
    zs = pl.BlockSpec((tm, LANES), lambda i, h: (i, P_Z // LANES + h))
    return pl.pallas_call(
        body, grid=(S // tm, DN_HEADS), in_specs=[hs, zs, _full((1, DN_DIM)), hs],
        out_specs=[hs, hs, _full((1, DN_DIM))],
        out_shape=[jax.ShapeDtypeStruct((S, DN_WIDTH), F32), jax.ShapeDtypeStruct((S, DN_WIDTH), BF16),
                   jax.ShapeDtypeStruct((1, DN_DIM), F32)],
        name="dn_out_bwd", compiler_params=_params(("arbitrary", "arbitrary")))(o, proj, gain, dy)


def _rel_buckets():
    qi = np.arange(BLOCK)[:, None]
    kj = np.arange(2 * BLOCK)[None, :]
    n = np.maximum(BLOCK + qi - kj, 0)
    max_exact = REL_BUCKETS // 2
    nf = np.maximum(n, 1).astype(np.float32)
    large = max_exact + (np.log(nf / np.float32(max_exact)) / np.float32(math.log(REL_MAX_DIST / max_exact))
                         * np.float32(REL_BUCKETS - max_exact)).astype(np.int32)
    large = np.minimum(large, REL_BUCKETS - 1)
    return np.where(n < max_exact, n, large).astype(np.int32)


def _bias_fwd(rel_bias):
    buckets = jnp.asarray(_rel_buckets())

    def body(rb_ref, bk_ref, o_ref):
        bk = bk_ref[...]
        for h in range(SWA_HEADS):
            acc = jnp.zeros((BLOCK, 2 * BLOCK), F32)
            for b in range(REL_BUCKETS):
                acc = jnp.where(bk == b, rb_ref[b, h], acc)
            o_ref[h] = acc

    return pl.pallas_call(
        body, in_specs=[pl.BlockSpec(memory_space=pltpu.SMEM), pl.BlockSpec(memory_space=pltpu.VMEM)],
        out_specs=pl.BlockSpec(memory_space=pltpu.VMEM),
        out_shape=jax.ShapeDtypeStruct((SWA_HEADS, BLOCK, 2 * BLOCK), F32), name="swa_bias_fwd",
        compiler_params=_params())(rel_bias, buckets)


def _bias_bwd(dbias):
    buckets = jnp.asarray(_rel_buckets())

    def body(d_ref, bk_ref, o_ref):
        bk = bk_ref[...]
        lane = lax.broadcasted_iota(jnp.int32, (1, LANES), 1)
        for h in range(SWA_HEADS):
            d = d_ref[h]
            row = jnp.zeros((1, LANES), F32)
            for b in range(REL_BUCKETS):
                part = jnp.sum(jnp.where(bk == b, d, 0.0), axis=1, keepdims=True)
                row = jnp.where(lane == b, jnp.sum(part, axis=0, keepdims=True), row)
            o_ref[h:h + 1, :] = row

    return pl.pallas_call(
        body, in_specs=[pl.BlockSpec(memory_space=pltpu.VMEM), pl.BlockSpec(memory_space=pltpu.VMEM)],
        out_specs=pl.BlockSpec(memory_space=pltpu.VMEM),
        out_shape=jax.ShapeDtypeStruct((SWA_HEADS, LANES), F32), name="swa_bias_bwd",
        compiler_params=_params())(dbias, buckets)


def _swa_mask(n):
    qi = lax.broadcasted_iota(jnp.int32, (BLOCK, 2 * BLOCK), 0)
    kj = lax.broadcasted_iota(jnp.int32, (BLOCK, 2 * BLOCK), 1)
    dist = BLOCK + qi - kj
    return (dist >= 0) & (dist < WINDOW) & ((n > 0) | (kj >= BLOCK))


def _swa_in_specs():
    q = pl.BlockSpec((BLOCK, SWA_WIDTH), lambda n: (n, P_SQ // SWA_WIDTH))
    kc = pl.BlockSpec((BLOCK, SWA_KVW), lambda n: (n, P_SK // SWA_KVW))
    kp = pl.BlockSpec((BLOCK, SWA_KVW), lambda n: (jnp.maximum(n - 1, 0), P_SK // SWA_KVW))
    vc = pl.BlockSpec((BLOCK, SWA_KVW), lambda n: (n, P_SV // SWA_KVW))
    vp = pl.BlockSpec((BLOCK, SWA_KVW), lambda n: (jnp.maximum(n - 1, 0), P_SV // SWA_KVW))
    small = [_full((1, SWA_DIM)), _full((1, SWA_DIM)), _full((1, SWA_HEADS)),
             _full((SWA_HEADS, BLOCK, 2 * BLOCK))]
    return [q, kp, kc, vp, vc] + small


def _swa_load(q_ref, kp_ref, kc_ref, vp_ref, vc_ref, s_ref):
    q = jnp.stack([q_ref[:, pl.ds(h * SWA_DIM, SWA_DIM)] for h in range(SWA_HEADS)])
    kbands, vbands = [], []
    for kv in range(SWA_KV):
        cols = pl.ds(kv * SWA_DIM, SWA_DIM)
        kbands += [jnp.concatenate([kp_ref[:, cols], kc_ref[:, cols]], axis=0)] * SWA_GROUP
        vbands += [jnp.concatenate([vp_ref[:, cols], vc_ref[:, cols]], axis=0)] * SWA_GROUP
    sinks = jnp.stack([s_ref[:, pl.ds(h, 1)] for h in range(SWA_HEADS)])
    return q, jnp.stack(kbands), jnp.stack(vbands), sinks


def _swa_fwd(proj, q_gain, k_gain, sinks, bias):
    S = proj.shape[0]

    def body(q_ref, kp_ref, kc_ref, vp_ref, vc_ref, qg_ref, kg_ref, s_ref, bias_ref, y_ref):
        mask = _swa_mask(pl.program_id(0))
        q, kband, vband, sk = _swa_load(q_ref, kp_ref, kc_ref, vp_ref, vc_ref, s_ref)
        out = _swa_block(q, kband, vband, qg_ref[...], kg_ref[...], sk, bias_ref[...], mask)
        for h in range(SWA_HEADS):
            y_ref[:, pl.ds(h * SWA_DIM, SWA_DIM)] = out[h].astype(BF16)

    return pl.pallas_call(
        body, grid=(S // BLOCK,), in_specs=_swa_in_specs(),
        out_specs=pl.BlockSpec((BLOCK, SWA_WIDTH), lambda n: (n, 0)),
        out_shape=jax.ShapeDtypeStruct((S, SWA_WIDTH), BF16), name="swa_fwd",
        compiler_params=_params(("parallel",)))(proj, proj, proj, proj, proj, q_gain, k_gain, sinks, bias)


def _swa_bwd(proj, q_gain, k_gain, sinks, bias, dy):
    S = proj.shape[0]

    def body(q_ref, kp_ref, kc_ref, vp_ref, vc_ref, qg_ref, kg_ref, s_ref, bias_ref, dy_ref,
             dq_ref, dk_ref, dv_ref, dqg_ref, dkg_ref, ds_ref, dbias_ref):
        n = pl.program_id(0)
        mask = _swa_mask(n)

        @pl.when(n == 0)
        def _():
            for r in (dk_ref, dv_ref, dqg_ref, dkg_ref, ds_ref, dbias_ref):
                r[...] = jnp.zeros_like(r)

        cur = pl.ds(pl.multiple_of(n * BLOCK, BLOCK), BLOCK)
        prev = pl.ds(pl.multiple_of(jnp.maximum(n - 1, 0) * BLOCK, BLOCK), BLOCK)
        q, kband, vband, sk = _swa_load(q_ref, kp_ref, kc_ref, vp_ref, vc_ref, s_ref)
        _, vjp = jax.vjp(lambda q, kb, vb, qg, kg, sk, bs: _swa_block(q, kb, vb, qg, kg, sk, bs, mask),
                         q, kband, vband, qg_ref[...], kg_ref[...], sk, bias_ref[...])
        dy = jnp.stack([dy_ref[:, pl.ds(h * SWA_DIM, SWA_DIM)] for h in range(SWA_HEADS)])
        dq, dkb, dvb, dqg, dkg, dsk, dbs = vjp(dy)
        for h in range(SWA_HEADS):
            dq_ref[:, pl.ds(h * SWA_DIM, SWA_DIM)] = dq[h].astype(BF16)
            ds_ref[:, pl.ds(h, 1)] += dsk[h]
        dbias_ref[...] += dbs
        dqg_ref[...] += dqg
        dkg_ref[...] += dkg
        for kv in range(SWA_KV):
            cols = pl.ds(kv * SWA_DIM, SWA_DIM)
            group = range(kv * SWA_GROUP, (kv + 1) * SWA_GROUP)
            dk_kv = sum(dkb[h] for h in group)
            dv_kv = sum(dvb[h] for h in group)
            dk_ref[cur, cols] += dk_kv[BLOCK:]
            dv_ref[cur, cols] += dv_kv[BLOCK:]

            @pl.when(n > 0)
            def _(cols=cols, dk_kv=dk_kv, dv_kv=dv_kv):
                dk_ref[prev, cols] += dk_kv[:BLOCK]
                dv_ref[prev, cols] += dv_kv[:BLOCK]

    return pl.pallas_call(
        body, grid=(S // BLOCK,),
        in_specs=_swa_in_specs() + [pl.BlockSpec((BLOCK, SWA_WIDTH), lambda n: (n, 0))],
        out_specs=[pl.BlockSpec((BLOCK, SWA_WIDTH), lambda n: (n, 0)), _full((S, SWA_KVW)), _full((S, SWA_KVW)),
                   _full((1, SWA_DIM)), _full((1, SWA_DIM)), _full((1, SWA_HEADS)),
                   _full((SWA_HEADS, BLOCK, 2 * BLOCK))],
        out_shape=[jax.ShapeDtypeStruct((S, SWA_WIDTH), BF16), jax.ShapeDtypeStruct((S, SWA_KVW), F32),
                   jax.ShapeDtypeStruct((S, SWA_KVW), F32), jax.ShapeDtypeStruct((1, SWA_DIM), F32),
                   jax.ShapeDtypeStruct((1, SWA_DIM), F32), jax.ShapeDtypeStruct((1, SWA_HEADS), F32),
                   jax.ShapeDtypeStruct((SWA_HEADS, BLOCK, 2 * BLOCK), F32)],
        name="swa_bwd", compiler_params=_params(("arbitrary",)),
    )(proj, proj, proj, proj, proj, q_gain, k_gain, sinks, bias, dy)


def _position():
    return lax.axis_index("x"), lax.axis_index("y"), lax.axis_index("c")


def _all_gather(shards, name="all_gather_weights"):
    na = len(shards)

    def body(*refs):
        x_refs, out_refs = refs[:na], refs[na:2 * na]
        send_sems, recv_sems, local_sems = refs[2 * na:]
        x, y, c = _position()
        me, sibling = (x, y, c), (x, y, 1 - c)
        chips = [(1 - x, y), (x, 1 - y), (1 - x, 1 - y)]

        def copy(a, k, block, to, own=False):
            px, py, pc = block
            slot = out_refs[a].at[4 * px + 2 * py + pc]
            return pltpu.make_async_remote_copy(
                src_ref=x_refs[a] if own else slot, dst_ref=slot, send_sem=send_sems.at[7 * a + k],
                recv_sem=recv_sems.at[7 * a + k], device_id=to, device_id_type=MESH_ID)

        mine = [pltpu.make_async_copy(x_refs[a], out_refs[a].at[4 * x + 2 * y + c], local_sems.at[a])
                for a in range(na)]
        for cp in mine:
            cp.start()
        first = []
        for a in range(na):
            first.append(copy(a, 0, me, sibling, own=True))
            first += [copy(a, 1 + j, me, (*chip, c), own=True) for j, chip in enumerate(chips)]
        for cp in first:
            cp.start()
        passed = []
        for j, chip in enumerate(chips):
            for a in range(na):
                copy(a, 1 + j, (*chip, c), me).wait_recv()
                passed.append(copy(a, 4 + j, (*chip, c), sibling))
                passed[-1].start()
        for a in range(na):
            copy(a, 0, sibling, me).wait_recv()
            for j, chip in enumerate(chips):
                copy(a, 4 + j, (*chip, 1 - c), me).wait_recv()
        for cp in first + passed:
            cp.wait_send()
        for cp in mine:
            cp.wait()

    return pl.pallas_call(
        body, in_specs=[pl.BlockSpec(memory_space=pl.ANY)] * na, out_specs=[pl.BlockSpec(memory_space=pl.ANY)] * na,
        out_shape=[jax.ShapeDtypeStruct((N_DEV,) + s.shape, s.dtype) for s in shards],
        scratch_shapes=[pltpu.SemaphoreType.DMA((7 * na,)), pltpu.SemaphoreType.DMA((7 * na,)),
                        pltpu.SemaphoreType.DMA((na,))],
        name=name)(*shards)


_HBM = pl.BlockSpec(memory_space=pltpu.HBM)
_SEM = pl.BlockSpec(memory_space=pltpu.SEMAPHORE)
_DATAFLOW = pltpu.SideEffectType.DATAFLOW_SIDE_EFFECTING


def _peers(x, y, c):
    out = []
    for k in range(1, N_DEV):
        px, py, pc = x ^ (k >> 2), y ^ ((k >> 1) & 1), c ^ (k & 1)
        out.append(((px, py, pc), 4 * px + 2 * py + pc))
    return out


def _split_copies(src_refs, land_refs, send_sems, recv_sems, scatter):
    x, y, c = _position()
    me = 4 * x + 2 * y + c
    sends, recvs = [], []
    for k, (peer_id, peer) in enumerate(_peers(x, y, c)):
        for a, (src, land) in enumerate(zip(src_refs, land_refs)):
            sems = dict(send_sem=send_sems.at[7 * a + k], recv_sem=recv_sems.at[7 * a + k],
                        device_id=peer_id, device_id_type=MESH_ID)
            mine = src.at[peer] if scatter else src
            sends.append(pltpu.make_async_remote_copy(src_ref=mine, dst_ref=land.at[me], **sems))
            recvs.append(pltpu.make_async_remote_copy(src_ref=mine, dst_ref=land.at[peer], **sems))
    return sends, recvs


def _exchange_start(srcs, scatter, name, after=None):
    na = len(srcs)
    lands = [lax.empty(s.shape if scatter else (N_DEV,) + s.shape, s.dtype) for s in srcs]
    extra = [] if after is None else [after]

    def body(*refs):
        src_refs, land_refs = refs[:na], refs[na:2 * na]
        send_sems, recv_sems = refs[2 * na + len(extra)], refs[2 * na + len(extra) + 1]
        token = refs[-1]
        sends, _ = _split_copies(src_refs, land_refs, send_sems, recv_sems, scatter)
        for cp in sends:
            cp.start()
        token[...] = jnp.zeros_like(token)

    hbm = lambda a: pltpu.HBM(a.shape, a.dtype)
    out = pl.pallas_call(
        body, name=name,
        out_shape=(pltpu.SemaphoreType.DMA((7 * na,)), pltpu.SemaphoreType.DMA((7 * na,)),
                   *[hbm(s) for s in srcs], *[hbm(l) for l in lands], jax.ShapeDtypeStruct((8, LANES), F32)),
        in_specs=[_HBM] * (2 * na) + [pl.BlockSpec(memory_space=pl.ANY)] * len(extra),
        out_specs=(_SEM, _SEM, *[_HBM] * (2 * na), pl.BlockSpec(memory_space=pltpu.VMEM)),
        input_output_aliases={i: 2 + i for i in range(2 * na)},
        compiler_params=pltpu.CompilerParams(has_side_effects=_DATAFLOW),
    )(*[pltpu.with_memory_space_constraint(s, pltpu.HBM) for s in srcs],
      *[pltpu.with_memory_space_constraint(l, pltpu.HBM) for l in lands], *extra)
    return (out[0], out[1], list(out[2:2 + na]), list(out[2 + na:2 + 2 * na])), out[-1]


def _exchange_wait(handle, after, scatter, name):
    send_sems, recv_sems, srcs, lands = handle
    na = len(srcs)

    def body(*refs):
        src_refs, land_refs = refs[:na], refs[na:2 * na]
        s_sems, r_sems = refs[2 * na], refs[2 * na + 1]
        sends, recvs = _split_copies(src_refs, land_refs, s_sems, r_sems, scatter)
        for cp in sends:
            cp.wait_send()
        for cp in recvs:
            cp.wait_recv()

    hbm = lambda a: pltpu.HBM(a.shape, a.dtype)
    out = pl.pallas_call(
        body, name=name, out_shape=(*[hbm(s) for s in srcs], *[hbm(l) for l in lands]),
        in_specs=[_HBM] * (2 * na) + [_SEM, _SEM, pl.BlockSpec(memory_space=pl.ANY)],
        out_specs=tuple([_HBM] * (2 * na)), input_output_aliases={i: i for i in range(2 * na)},
        compiler_params=pltpu.CompilerParams(has_side_effects=_DATAFLOW),
    )(*srcs, *lands, send_sems, recv_sems, after)
    return list(out[:na]), list(out[na:])


def _own_slot(landed, own):
    me = 4 * lax.axis_index("x") + 2 * lax.axis_index("y") + lax.axis_index("c")
    return lax.dynamic_update_slice_in_dim(landed, own[None], me, axis=0)


def _adam_update(parts, w, m, v, name, tr=256):
    _, r, c = w.shape
    tr = _pick_rows(r, tr)
    cp = parts.shape[2]

    def body(p_ref, w_ref, m_ref, v_ref, g_ref, d_ref, nm_ref, nv_ref):
        g = p_ref[0, :, pl.ds(0, c)].astype(F32)
        for i in range(1, N_DEV):
            g = g + p_ref[i, :, pl.ds(0, c)].astype(F32)
        delta, nm, nv = _adamw(w_ref[0], g, m_ref[0], v_ref[0])
        g_ref[0] = g
        d_ref[0] = delta
        nm_ref[0] = nm
        nv_ref[0] = nv

    rs = pl.BlockSpec((1, tr, c), lambda i: (0, i, 0))
    return pl.pallas_call(
        body, grid=(r // tr,), in_specs=[pl.BlockSpec((N_DEV, tr, cp), lambda i: (0, i, 0)), rs, rs, rs],
        out_specs=[rs] * 4, out_shape=[jax.ShapeDtypeStruct((1, r, c), F32)] * 4, name=name,
        compiler_params=_params(("parallel",)))(parts, w, m, v)


def _pick_rows(rows, target):
    if rows <= target:
        return rows
    t = target
    while t >= 16:
        if rows % t == 0:
            return t
        t -= 16
    return rows


BIG = ("w_in", "w_branch_dn", "w_branch_swa", "w_out", "w_gate", "w_up", "w_down")
IN_SHARD, IN_WIRE = D_IN // N_DEV, 640
FF_SHARD, FF_WIRE = D_FF // N_DEV, 384
D_FFP = N_DEV * FF_WIRE
BIG_SHAPES = {"w_in": ((D_MODEL, IN_SHARD), (D_MODEL, IN_WIRE)),
              "w_branch_dn": ((DN_WIDTH, LANES), (DN_WIDTH, LANES)),
              "w_branch_swa": ((SWA_WIDTH, LANES), (SWA_WIDTH, LANES)),
              "w_out": ((LANES, D_MODEL), (LANES, D_MODEL)),
              "w_gate": ((D_MODEL, FF_SHARD), (D_MODEL, FF_WIRE)),
              "w_up": ((D_MODEL, FF_SHARD), (D_MODEL, FF_WIRE)),
              "w_down": ((FF_SHARD, D_MODEL), (FF_WIRE, D_MODEL))}
CONV_SHARD, CONV_WIRE = (DN_CONV, DN_QKV // N_DEV), (8, 256)


def _pad_to(a, shape):
    return jnp.pad(a, [(0, t - s) for s, t in zip(a.shape, shape)])


_IN_SEGS = ((R_GATE, 2048, P_GATE), (R_QKV, DN_QKV, P_QKV), (R_Z, DN_WIDTH, P_Z), (R_SQ, SWA_WIDTH, P_SQ),
            (R_SK, SWA_KVW, P_SK), (R_SV, SWA_KVW, P_SV), (R_B, 8, P_BA))


def _w_in_from_blocks(blocks):
    parts = []
    for rs, n, _ in _IN_SEGS:
        for dev in range(N_DEV):
            lo, hi = max(rs, IN_SHARD * dev), min(rs + n, IN_SHARD * (dev + 1))
            if lo < hi:
                parts.append(blocks[dev, :, lo - IN_SHARD * dev:hi - IN_SHARD * dev])
    parts.append(jnp.zeros((blocks.shape[1], P_WIDTH - P_BA - 8), blocks.dtype))
    return jnp.concatenate(parts, axis=1)


def _w_in_to_blocks(g):
    out = []
    for dev in range(N_DEV):
        parts = []
        for rs, n, ps in sorted(_IN_SEGS):
            lo, hi = max(rs, IN_SHARD * dev), min(rs + n, IN_SHARD * (dev + 1))
            if lo < hi:
                parts.append(g[:, ps + lo - rs:ps + hi - rs])
        parts.append(jnp.zeros((g.shape[0], IN_WIRE - IN_SHARD), g.dtype))
        out.append(jnp.concatenate(parts, axis=1))
    return jnp.stack(out)


SMALL = {"attn_norm": (0, (1, D_MODEL)), "ffn_norm": (1, (1, D_MODEL)), "dn_out_norm": (2, (1, DN_DIM)),
         "swa_q_norm": (3, (1, SWA_DIM)), "swa_k_norm": (4, (1, SWA_DIM)), "dn_a_log": (5, (1, DN_HEADS)),
         "dn_dt_bias": (6, (1, DN_HEADS)), "swa_sinks": (7, (1, SWA_HEADS)), "rel_bias": (8, (REL_BUCKETS, SWA_HEADS))}
SMALL_SHEET = (48, D_MODEL)


def _small_pack(grads):
    names = list(SMALL)

    def body(*refs):
        o_ref = refs[-1]
        o_ref[...] = jnp.zeros_like(o_ref)
        for n, ref in zip(names, refs):
            r0, (nr, nc) = SMALL[n]
            o_ref[r0:r0 + nr, 0:nc] = ref[...]

    return pl.pallas_call(
        body, in_specs=[pl.BlockSpec(memory_space=pltpu.VMEM)] * len(names),
        out_specs=pl.BlockSpec(memory_space=pltpu.VMEM), out_shape=jax.ShapeDtypeStruct(SMALL_SHEET, F32),
        name="small_pack", compiler_params=_params())(*[grads[n].reshape(SMALL[n][1]) for n in names])


def _small_update(sheets, w, m, v):
    names = list(SMALL)
    k = len(names)

    def body(*refs):
        p_ref = refs[0]
        ins, outs = refs[1:1 + 3 * k], refs[1 + 3 * k:]
        for t, n in enumerate(names):
            r0, (nr, nc) = SMALL[n]
            g = p_ref[0, r0:r0 + nr, 0:nc]
            for i in range(1, N_DEV):
                g = g + p_ref[i, r0:r0 + nr, 0:nc]
            delta, nm, nv = _adamw(ins[t][...], g, ins[k + t][...], ins[2 * k + t][...])
            for kind, val in enumerate((g, delta, nm, nv)):
                outs[kind * k + t][...] = val

    shapes = [jax.ShapeDtypeStruct(SMALL[n][1], F32) for n in names]
    vm = pl.BlockSpec(memory_space=pltpu.VMEM)
    res = pl.pallas_call(
        body, in_specs=[vm] * (1 + 3 * k), out_specs=[vm] * (4 * k), out_shape=shapes * 4, name="adam_small",
        compiler_params=_params(),
    )(sheets, *[d[n].reshape(SMALL[n][1]) for d in (w, m, v) for n in names])
    return {n: tuple(res[kind * k + t] for kind in range(4)) for t, n in enumerate(names)}


def kernel(x, attn_norm, w_in, dn_conv, dn_a_log, dn_dt_bias, dn_out_norm, swa_q_norm, swa_k_norm, swa_sinks, rel_bias, w_branch_dn, w_branch_swa, w_out, ffn_norm, w_gate, w_up, w_down, loss_target, m_attn_norm, m_w_in, m_dn_conv, m_dn_a_log, m_dn_dt_bias, m_dn_out_norm, m_swa_q_norm, m_swa_k_norm, m_swa_sinks, m_rel_bias, m_w_branch_dn, m_w_branch_swa, m_w_out, m_ffn_norm, m_w_gate, m_w_up, m_w_down, v_attn_norm, v_w_in, v_dn_conv, v_dn_a_log, v_dn_dt_bias, v_dn_out_norm, v_swa_q_norm, v_swa_k_norm, v_swa_sinks, v_rel_bias, v_w_branch_dn, v_w_branch_swa, v_w_out, v_ffn_norm, v_w_gate, v_w_up, v_w_down):
    args = dict(locals())
    S = x.shape[1]
    xs = x.reshape(S, D_MODEL)
    target = loss_target.reshape(S, D_MODEL)

    w_loc = {n: args[n].reshape(BIG_SHAPES[n][0]) for n in BIG}
    conv_loc = dn_conv.reshape(CONV_SHARD)
    wire = {n: _pad_to(w_loc[n], BIG_SHAPES[n][1]).astype(BF16) for n in BIG}
    first = _all_gather([wire["w_in"], _pad_to(conv_loc, CONV_WIRE)])
    later = [n for n in BIG if n != "w_in"]
    rest_handle, rest_token = _exchange_start([wire[n] for n in later], False, "gather_rest_start", after=first[1])
    w_pad = _w_in_from_blocks(first[0])
    conv_w = jnp.concatenate([first[1][d, :DN_CONV, :CONV_SHARD[1]] for d in range(N_DEV)], axis=1)

    h = _norm_fwd(xs, attn_norm + rest_token[0, 0], "norm1_fwd")
    proj = _mm([(h, w_pad)], "nn", F32, "mm_in", 512, 1664, j_outer=True)
    qkvn = _dn_conv_fwd(proj, conv_w)
    beta, g = _dn_gate_fwd(proj, dn_a_log, dn_dt_bias)
    u, w, qe, kd, qk, egl = _dn_prep_fwd(qkvn, g, beta)
    o, states = _dn_scan_fwd(u, w, qe, kd, qk, egl)
    y_dn = _dn_out_fwd(o, proj, dn_out_norm)
    bias = _bias_fwd(rel_bias)
    y_swa = _swa_fwd(proj, swa_q_norm, swa_k_norm, swa_sinks, bias)
    rest_src, rest_land = _exchange_wait(rest_handle, y_swa, False, "gather_rest_wait")
    G = {n: _own_slot(land, src) for n, src, land in zip(later, rest_src, rest_land)}
    w_bdn, w_bswa, w_g, w_u = G["w_branch_dn"], G["w_branch_swa"], G["w_gate"], G["w_up"]
    w_o = G["w_out"].reshape(D_MODEL, D_MODEL)
    w_d = G["w_down"].reshape(D_FFP, D_MODEL)
    gates = [(proj, P_GATE // 512), (proj, (P_GATE + D_MODEL) // 512)]
    a_dn, a_swa, merged = _mm_fused(
        [(y_dn, w_bdn), (y_swa, w_bswa)], "nn", "mm_branch_merge", 1024, 512,
        lambda p, e: (p[0], p[1], _merge(e[0], e[1], p[0], p[1])), gates, (F32, F32, BF16), b_blocks=True)

    def resid_norm(p, e):
        x1 = e[0] + p[0]
        return x1, _rms(x1, e[1])

    x1, h2 = _mm_fused([(merged, w_o)], "nn", "mm_out_norm", 512, D_MODEL, resid_norm,
                       [(xs, 0), (ffn_norm, None)], (F32, BF16))
    gate, up, act = _mm_fused([(h2, w_g), (h2, w_u)], "nn", "mm_gate_up_act", 1024, 768,
                              lambda p, e: (p[0], p[1], _act(p[0], p[1])), [], (F32, F32, BF16),
                              j_outer=True, b_blocks=True)

    def loss_head(p, e):
        diff = e[0] + p[0] - e[1]
        dy = diff * (1.0 / D_MODEL)
        part = jnp.sum(jnp.mean(diff * diff, axis=-1, keepdims=True), axis=0, keepdims=True) * 0.5
        return dy, dy, part

    dy, dy_b, loss_local = _mm_fused([(act, w_d)], "nn", "mm_down_loss", 512, D_MODEL, loss_head,
                                     [(x1, 0), (target, 0)], (F32, BF16), with_sum=True)

    def act_bwd(p, e):
        _, vjp = jax.vjp(_act, e[0], e[1])
        return vjp(p[0])

    dgate, dup = _mm_fused([(dy_b, w_d)], "nt", "mm_dact_act", 1024, 768, act_bwd, [(gate, 0), (up, 0)],
                           (BF16, BF16), j_outer=True)
    g_w_down = _mm([(act, dy_b)], "tn", BF16, "mm_dw_down", 768, D_MODEL, j_outer=True)
    g_w_down = g_w_down.reshape(N_DEV, FF_WIRE, D_MODEL)
    dh2 = _mm([(dgate, w_g), (dup, w_u)], "nt", F32, "mm_dh2", 512, 512, j_outer=True, b_blocks=True)
    g_w_gate = _mm([(h2, dgate)], "tn", BF16, "mm_dw_gate", D_MODEL, 768, out_blocks=True)
    g_w_up = _mm([(h2, dup)], "tn", BF16, "mm_dw_up", D_MODEL, 768, out_blocks=True)
    ffn_handle, ffn_token = _exchange_start([g_w_down, g_w_gate, g_w_up], True, "scatter_ffn_start")
    dx1, dx1_b, g_ffn_norm = _norm_bwd(x1, [dh2], dy, ffn_norm + ffn_token[0, 0], "norm2_bwd")
    def merge_bwd(p, e):
        _, vjp = jax.vjp(_merge, *e)
        return vjp(p[0])

    dg0, dg1, da_dn, da_swa = _mm_fused([(dx1_b, w_o)], "nt", "mm_dmerged_merge", 1024, 512, merge_bwd,
                                        gates + [(a_dn, 0), (a_swa, 0)], (BF16,) * 4, j_outer=True)
    g_w_out = _mm([(merged, dx1_b)], "tn", BF16, "mm_dw_out", 512, D_MODEL, j_outer=True)
    g_w_out = g_w_out.reshape(N_DEV, LANES, D_MODEL)
    dy_dn = _mm([(da_dn, w_bdn)], "nt", F32, "mm_dy_dn", 1024, DN_WIDTH, b_blocks=True)
    dy_swa = _mm([(da_swa, w_bswa)], "nt", F32, "mm_dy_swa", 1024, SWA_WIDTH, b_blocks=True)
    g_w_bdn = _mm([(y_dn, da_dn)], "tn", BF16, "mm_dw_branch_dn", DN_WIDTH, 512, out_blocks=True)
    g_w_bswa = _mm([(y_swa, da_swa)], "tn", BF16, "mm_dw_branch_swa", SWA_WIDTH, 512, out_blocks=True)
    dsq, dsk, dsv, g_q_norm, g_k_norm, g_sinks, dbias = _swa_bwd(proj, swa_q_norm, swa_k_norm, swa_sinks, bias, dy_swa)
    g_rel_bias = _bias_bwd(dbias)[:, :REL_BUCKETS].T
    mix_handle, mix_token = _exchange_start([g_w_out, g_w_bdn, g_w_bswa], True, "scatter_mix_start")
    do, dz, g_out_norm = _dn_out_bwd(o, proj, dn_out_norm + mix_token[0, 0], dy_dn)
    du, dw, dqe, dkd, dqk, degl = _dn_scan_bwd(u, w, qe, kd, qk, egl, states, do)
    dqkvn, dgd, dbeta = _dn_prep_bwd(qkvn, g, beta, du, dw, dqe, dkd, dqk, degl)
    dba, dal, ddt = _dn_gate_bwd(proj, dn_a_log, dn_dt_bias, dbeta, dgd)
    g_a_log = dal.reshape(DN_HEADS, DN_DIM).sum(axis=1)
    g_dt_bias = ddt.reshape(DN_HEADS, DN_DIM).sum(axis=1)
    dqkv, g_conv = _dn_conv_bwd(proj, conv_w, dqkvn)
    dproj = jnp.concatenate([dg0, dg1, dqkv, dz, dsq, dsk.astype(BF16), dsv.astype(BF16), dba], axis=1)
    g_w_in = _w_in_to_blocks(_mm([(h, dproj)], "tn", BF16, "mm_dw_in", 512, 1664, j_outer=True))
    in_handle, in_token = _exchange_start([g_w_in], True, "scatter_in_start")
    dh = _mm([(dproj, w_pad)], "nt", F32, "mm_dh", 512, D_MODEL)
    dx, _, g_attn_norm = _norm_bwd(xs, [dh], dx1, attn_norm + in_token[0, 0], "norm1_bwd")

    g_small = {"attn_norm": g_attn_norm, "ffn_norm": g_ffn_norm, "rel_bias": g_rel_bias, "dn_out_norm": g_out_norm,
               "swa_q_norm": g_q_norm, "swa_k_norm": g_k_norm, "dn_a_log": g_a_log, "dn_dt_bias": g_dt_bias,
               "swa_sinks": g_sinks}
    sheets, conv_all = _all_gather([_small_pack(g_small), _pad_to(g_conv, (8, DN_QKV))], name="all_gather_small")
    me = 4 * lax.axis_index("x") + 2 * lax.axis_index("y") + lax.axis_index("c")
    recv_small = sheets
    received = {}
    for handle, group, name in ((ffn_handle, ("w_down", "w_gate", "w_up"), "scatter_ffn_wait"),
                                (mix_handle, ("w_out", "w_branch_dn", "w_branch_swa"), "scatter_mix_wait"),
                                (in_handle, ("w_in",), "scatter_in_wait")):
        srcs, lands = _exchange_wait(handle, recv_small, True, name)
        for n, src, land in zip(group, srcs, lands):
            received[n] = _own_slot(land, lax.dynamic_index_in_dim(src, me, 0, keepdims=False))

    outs = {}
    for n in BIG:
        outs[n] = _adam_update(received[n], args[n], args["m_" + n], args["v_" + n], "adam_" + n)
    conv_parts = lax.dynamic_slice(conv_all, (0, 0, me * CONV_SHARD[1]), (N_DEV,) + CONV_SHARD)
    outs["dn_conv"] = _adam_update(conv_parts, dn_conv, m_dn_conv, v_dn_conv, "adam_dn_conv")
    outs.update(_small_update(sheets, {n: args[n] for n in SMALL}, {n: args["m_" + n] for n in SMALL},
                              {n: args["v_" + n] for n in SMALL}))

    names = ("attn_norm", "w_in", "dn_conv", "dn_a_log", "dn_dt_bias", "dn_out_norm", "swa_q_norm", "swa_k_norm",
             "swa_sinks", "rel_bias", "w_branch_dn", "w_branch_swa", "w_out", "ffn_norm", "w_gate", "w_up", "w_down")
    results = []
    for kind in range(4):
        results += [outs[n][kind].reshape(args[n].shape) for n in names]

    loss = lax.psum(loss_local[0, 0], ("x", "y", "c"))
    return (loss, dx.reshape(x.shape), *results)
```

```python
import math

import numpy as np
import jax
import jax.numpy as jnp
from jax import lax
from jax.experimental import pallas as pl
from jax.experimental.pallas import tpu as pltpu

F32 = jnp.float32
BF16 = jnp.bfloat16
HI = lax.Precision.HIGHEST

D_MODEL = 1024
DN_HEADS = 4
DN_DIM = 128
DN_WIDTH = 512
DN_QKV = 1536
DN_CONV = 4
CHUNK = 64
SWA_HEADS = 8
SWA_KV = 2
SWA_GROUP = 4
SWA_DIM = 64
SWA_WIDTH = 512
SWA_KVW = 128
WINDOW = 128
BLOCK = 128
REL_BUCKETS = 32
REL_MAX_DIST = 128
D_FF = 2816
D_IN = 4872
EPS = 1e-6
N_DEV = 8

ADAM_LR = 0.001
ADAM_B1 = 0.9
ADAM_B2 = 0.999
ADAM_EPS = 1e-08
ADAM_WD = 0.01
ADAM_STEP = 10

P_GATE, P_QKV, P_Z, P_SQ, P_SK, P_SV, P_BA = 0, 2048, 3584, 4096, 4608, 4736, 4864
P_WIDTH = 4992
R_QKV, R_Z, R_B, R_A, R_SQ, R_SK, R_SV, R_GATE = 0, 1536, 2048, 2052, 2056, 2568, 2696, 2824

VMEM_LIMIT = 56 * 1024 * 1024
LANES = 128
MESH_ID = pl.DeviceIdType.MESH


def _params(sem=None):
    return pltpu.CompilerParams(dimension_semantics=sem, vmem_limit_bytes=VMEM_LIMIT)


def _pick(dim, target):
    if dim <= target:
        return dim
    t = target - target % LANES
    while t >= LANES:
        if dim % t == 0:
            return t
        t -= LANES
    return dim


_DIMS = {"nn": (((1,), (0,)), ((), ())), "nt": (((1,), (1,)), ((), ())), "tn": (((0,), (0,)), ((), ()))}


def _tile_product(a_ref, b_ref, mode, b_blocks):
    a = a_ref[...].astype(BF16)
    if not b_blocks:
        return lax.dot_general(a, b_ref[...].astype(BF16), _DIMS[mode], preferred_element_type=F32)
    nb, _, c = b_ref.shape
    if mode == "nt" and c >= 2 * LANES:
        return sum(lax.dot_general(a[:, d * c:(d + 1) * c], b_ref[d].astype(BF16), _DIMS[mode],
                                   preferred_element_type=F32) for d in range(nb))
    b = jnp.concatenate([b_ref[d] for d in range(nb)], axis=1)
    return lax.dot_general(a, b.astype(BF16), _DIMS[mode], preferred_element_type=F32)


def _mm(pairs, mode, out_dtype, name, bm, bn, j_outer=False, b_blocks=False, out_blocks=False):
    a0, b0 = pairs[0]
    cb = b0.shape[2] if b_blocks else None
    b_shape = (b0.shape[1], N_DEV * cb) if b_blocks else b0.shape
    if mode == "nn":
        (M, K), (K2, N) = a0.shape, b_shape
    elif mode == "nt":
        (M, K), (N, K2) = a0.shape, b_shape
    else:
        (K, M), (K2, N) = a0.shape, b_shape
    bm, bn = min(bm, M), min(bn, N)
    assert K == K2 and M % bm == 0 and N % bn == 0, (name, a0.shape, b0.shape, bm, bn)
    co = N // N_DEV
    assert not out_blocks or bn % co == 0
    dims = _DIMS[mode]
    n = len(pairs)

    def body(*refs):
        o_ref = refs[2 * n]
        acc = None
        for t in range(n):
            p = _tile_product(refs[2 * t], refs[2 * t + 1], mode, b_blocks)
            acc = p if acc is None else acc + p
        if out_blocks:
            for d in range(bn // co):
                o_ref[d] = acc[:, d * co:(d + 1) * co].astype(out_dtype)
        else:
            o_ref[...] = acc.astype(out_dtype)

    def ij(f):
        return (lambda j, i: f(i, j)) if j_outer else f

    a_spec = pl.BlockSpec((K, bm), ij(lambda i, j: (0, i))) if mode == "tn" else pl.BlockSpec((bm, K), ij(lambda i, j: (i, 0)))
    if b_blocks and mode == "nt":
        b_spec = pl.BlockSpec((N_DEV, bn, cb), ij(lambda i, j: (0, j, 0)))
    elif b_blocks:
        b_spec = pl.BlockSpec((bn // cb, K, cb), ij(lambda i, j: (j, 0, 0)))
    elif mode == "nt":
        b_spec = pl.BlockSpec((bn, K), ij(lambda i, j: (j, 0)))
    else:
        b_spec = pl.BlockSpec((K, bn), ij(lambda i, j: (0, j)))
    if out_blocks:
        out_spec = pl.BlockSpec((bn // co, bm, co), ij(lambda i, j: (j, i, 0)))
        out_shape = jax.ShapeDtypeStruct((N_DEV, M, co), out_dtype)
    else:
        out_spec = pl.BlockSpec((bm, bn), ij(lambda i, j: (i, j)))
        out_shape = jax.ShapeDtypeStruct((M, N), out_dtype)
    grid = (N // bn, M // bm) if j_outer else (M // bm, N // bn)
    return pl.pallas_call(
        body, grid=grid, in_specs=[a_spec, b_spec] * n, out_specs=out_spec, out_shape=out_shape, name=name,
        compiler_params=_params(("parallel", "parallel")),
    )(*[x for pair in pairs for x in pair])


def _mm_fused(pairs, mode, name, bm, bn, epilogue, extras, out_dtypes, j_outer=False, b_blocks=False,
              sum_shape=None, wide_first=None):
    a0, b0 = pairs[0]
    cb = b0.shape[2] if b_blocks else None
    b_shape = (b0.shape[1], N_DEV * cb) if b_blocks else b0.shape
    if mode == "nn":
        (M, K), (K2, N) = a0.shape, b_shape
    else:
        (M, K), (N, K2) = a0.shape, b_shape
    bm, bn = min(bm, M), min(bn, N)
    assert mode in ("nn", "nt") and K == K2 and M % bm == 0 and N % bn == 0, (name, a0.shape, b0.shape)
    dims = _DIMS[mode]
    n, ne, no = len(pairs), len(extras), len(out_dtypes)

    def body(*refs):
        prods = [_tile_product(refs[2 * t], refs[2 * t + 1], mode, b_blocks) for t in range(n)]
        results = epilogue(prods, [r[...] for r in refs[2 * n:2 * n + ne]])
        out_refs = refs[2 * n + ne:]
        for o_ref, val, dt in zip(out_refs, results, out_dtypes):
            o_ref[...] = val.astype(dt)
        if sum_shape is not None:
            s_ref = out_refs[no]

            @pl.when((pl.program_id(0) == 0) & (pl.program_id(1) == 0))
            def _():
                s_ref[...] = jnp.zeros_like(s_ref)

            s_ref[...] += results[no]

    def ij(f):
        return (lambda j, i: f(i, j)) if j_outer else f

    a_spec = pl.BlockSpec((bm, K), ij(lambda i, j: (i, 0)))
    if b_blocks and mode == "nt":
        b_spec = pl.BlockSpec((N_DEV, bn, cb), ij(lambda i, j: (0, j, 0)))
    elif b_blocks:
        b_spec = pl.BlockSpec((bn // cb, K, cb), ij(lambda i, j: (j, 0, 0)))
    elif mode == "nt":
        b_spec = pl.BlockSpec((bn, K), ij(lambda i, j: (j, 0)))
    else:
        b_spec = pl.BlockSpec((K, bn), ij(lambda i, j: (0, j)))
    e_specs = [pl.BlockSpec((1, bn), ij(lambda i, j: (0, j))) if first is None
               else pl.BlockSpec((bm, bn), ij(lambda i, j, first=first: (i, first + j))) for _, first in extras]
    tile = pl.BlockSpec((bm, bn), ij(lambda i, j: (i, j)))
    out_specs = [tile] * no
    out_shape = [jax.ShapeDtypeStruct((M, N), dt) for dt in out_dtypes]
    if wide_first is not None:
        assert bn == N
        out_specs[0] = pl.BlockSpec((bm, wide_first[1]), ij(lambda i, j: (i, 0)))
        out_shape[0] = jax.ShapeDtypeStruct((M, wide_first[0]), out_dtypes[0])
    if sum_shape is not None:
        assert sum_shape[1] in (1, bn) and (sum_shape[1] == 1 or bn == N)
        out_specs.append(_full(sum_shape))
        out_shape.append(jax.ShapeDtypeStruct(sum_shape, F32))
    grid = (N // bn, M // bm) if j_outer else (M // bm, N // bn)
    sem = ("arbitrary", "arbitrary") if sum_shape is not None else ("parallel", "parallel")
    return pl.pallas_call(
        body, grid=grid, in_specs=[a_spec, b_spec] * n + e_specs, out_specs=out_specs, out_shape=out_shape,
        name=name, compiler_params=_params(sem),
    )(*[x for pair in pairs for x in pair], *[arr for arr, _ in extras])


def _rms(x, gain):
    return x * lax.rsqrt(jnp.mean(x * x, axis=-1, keepdims=True) + EPS) * gain


def _silu(x):
    return x * jax.nn.sigmoid(x)


def _act(g, u):
    return _silu(g) * u


def _merge(g0, g1, a_dn, a_swa):
    return jax.nn.sigmoid(g0) * a_dn + jax.nn.sigmoid(g1) * a_swa


def _dn_post(c, is_v, q_scale):
    a = _silu(c)
    rs = lax.rsqrt(jnp.sum(a * a, axis=-1, keepdims=True) + EPS) * q_scale
    return a * jnp.where(is_v, 1.0, rs)


def _dn_out(o, z, gain):
    return _rms(o, gain) * _silu(z)


def _dot(a, b, dims=_DIMS["nn"], hi=False):
    if a.ndim == 3 or b.ndim == 3:
        batch = a.shape[0] if a.ndim == 3 else b.shape[0]
        a = a if a.ndim == 3 else jnp.broadcast_to(a, (batch,) + a.shape)
        b = b if b.ndim == 3 else jnp.broadcast_to(b, (batch,) + b.shape)
        ((ca,), (cb,)), _ = dims
        dims = (((ca + 1,), (cb + 1,)), ((0,), (0,)))
    if hi:
        return lax.dot_general(a, b, dims, precision=HI, preferred_element_type=F32)
    return lax.dot_general(a.astype(BF16), b.astype(BF16), dims, preferred_element_type=F32)


def _pieces(x):
    hi = x.astype(BF16)
    r1 = x - hi.astype(F32)
    mid = r1.astype(BF16)
    return hi, mid, (r1 - mid.astype(F32)).astype(BF16)


def _sel_left_impl(m, x):
    mb = m.astype(BF16)
    hi, mid, lo = _pieces(x)
    return _dot(mb, hi) + (_dot(mb, mid) + _dot(mb, lo))


@jax.custom_vjp
def _sel_left(m, mt, x):
    return _sel_left_impl(m, x)


_sel_left.defvjp(lambda m, mt, x: (_sel_left_impl(m, x), (m, mt)),
                 lambda res, ct: (jnp.zeros_like(res[0]), jnp.zeros_like(res[1]), _sel_left_impl(res[1], ct)))


def _sel_right_impl(x, s):
    sb = s.astype(BF16)
    hi, mid, lo = _pieces(x)
    return _dot(hi, sb) + (_dot(mid, sb) + _dot(lo, sb))


@jax.custom_vjp
def _sel_right(x, s, st):
    return _sel_right_impl(x, s)


_sel_right.defvjp(lambda x, s, st: (_sel_right_impl(x, s), (s, st)),
                  lambda res, ct: (_sel_right_impl(ct, res[1]), jnp.zeros_like(res[0]), jnp.zeros_like(res[1])))


def _sel_nt_impl(s, x):
    sb = s.astype(BF16)
    hi, mid, lo = _pieces(x)
    return _dot(sb, hi, _DIMS["nt"]) + (_dot(sb, mid, _DIMS["nt"]) + _dot(sb, lo, _DIMS["nt"]))


def _sel_tn_impl(x, s):
    sb = s.astype(BF16)
    hi, mid, lo = _pieces(x)
    return _dot(hi, sb, _DIMS["tn"]) + (_dot(mid, sb, _DIMS["tn"]) + _dot(lo, sb, _DIMS["tn"]))


@jax.custom_vjp
def _sel_nt(s, x):
    return _sel_nt_impl(s, x)


_sel_nt.defvjp(lambda s, x: (_sel_nt_impl(s, x), s),
               lambda s, ct: (jnp.zeros_like(s), _sel_tn_impl(ct, s)))


def _dot3_impl(a, b):
    a_hi, a_lo, _ = _pieces(a)
    b_hi, b_lo, _ = _pieces(b)
    return _dot(a_hi, b_hi) + (_dot(a_hi, b_lo) + _dot(a_lo, b_hi))


@jax.custom_vjp
def _dot3(a, b):
    return _dot3_impl(a, b)


_dot3.defvjp(lambda a, b: (_dot3_impl(a, b), (a, b)),
             lambda res, ct: (_dot(ct, res[1], _DIMS["nt"]), _dot(res[0], ct, _DIMS["tn"])))


def _inv_impl(a, eye, strict):
    t = eye - a
    p = _dot(a, a)
    for level in range(5):
        t = t + _dot(t, p)
        if level < 4:
            p = _dot(p, p)
    t = t + _dot3_impl(t, eye - _dot3_impl(eye + a, t))
    return jnp.where(strict > 0.5, t, eye)


@jax.custom_vjp
def _inv_unit_lower(a, eye, strict):
    return _inv_impl(a, eye, strict)


def _inv_bwd(res, ct):
    t, eye, strict = res
    da = -_dot(_dot(t, ct, _DIMS["tn"]), t, _DIMS["nt"])
    return da, jnp.zeros_like(eye), jnp.zeros_like(strict)


def _inv_fwd(a, eye, strict):
    t = _inv_impl(a, eye, strict)
    return t, (t, eye, strict)


_inv_unit_lower.defvjp(_inv_fwd, _inv_bwd)

GROUP = 4
GROUP_ROWS = GROUP * CHUNK


def _block_consts(n):
    ii = lax.broadcasted_iota(jnp.int32, (n, n), 0)
    jj = lax.broadcasted_iota(jnp.int32, (n, n), 1)
    shift = CHUNK.bit_length() - 1
    same = jnp.right_shift(ii, shift) == jnp.right_shift(jj, shift)
    return same & (ii >= jj), same & (ii <= jj), same & (ii > jj), same, ii == jj


def _lane0(n):
    s = (lax.broadcasted_iota(jnp.int32, (LANES, n), 0) == 0).astype(F32)
    st = (lax.broadcasted_iota(jnp.int32, (n, LANES), 1) == 0).astype(F32)
    return s, st


def _dn_group(q, k, v, g, beta):
    n = GROUP_ROWS
    low_b, upp_b, strict_b, same_b, eye_b = _block_consts(n)
    low, upp, same, eye = low_b.astype(F32), upp_b.astype(F32), same_b.astype(F32), eye_b.astype(F32)
    s, st = _lane0(n)
    gc = _sel_left(low, upp, g)
    gl = _sel_left(same, same, g)
    col = _sel_right(gc, s, st)
    row = _sel_nt(st, gc)
    decay = jnp.exp(jnp.where(low_b, col - row, -jnp.inf))
    kb = k * beta
    vb = v * beta
    a = jnp.where(strict_b, _dot(kb, k, _DIMS["nt"]) * decay, 0.0)
    t = _inv_unit_lower(a, eye, strict_b.astype(F32))
    u = _dot3(t, vb)
    w = _dot3(t, kb * jnp.exp(gc))
    return u, w, q * jnp.exp(gc), k * jnp.exp(gl - gc)


def _dn_chunk(q, k, g):
    ii = lax.broadcasted_iota(jnp.int32, (CHUNK, CHUNK), 0)
    jj = lax.broadcasted_iota(jnp.int32, (CHUNK, CHUNK), 1)
    low = (ii >= jj).astype(F32)
    upp = (ii <= jj).astype(F32)
    s, st = _lane0(CHUNK)
    gc = _sel_left(low, upp, g)
    col = _sel_right(gc, s, st)
    row = _sel_nt(st, gc)
    decay = jnp.exp(jnp.where(ii >= jj, col - row, -jnp.inf))
    qk = _dot(q, k, _DIMS["nt"]) * decay
    return qk, jnp.exp(jnp.sum(g, axis=-2, keepdims=True))


def _dn_step(s, u, w, qe, kd, qk, egl):
    v_new = u - _dot(w, s)
    o = _dot(qe, s) + _dot(qk, v_new)
    s_new = s * egl + _dot(kd, v_new, _DIMS["tn"])
    return s_new, o


def _swa_block(q, kband, vband, qg, kg, sinks, bias, mask):
    kn = _rms(kband, kg)
    qn = _rms(q, qg)
    logits = _dot(qn, kn, _DIMS["nt"]) * (SWA_DIM ** -0.5)
    logits = jnp.where(mask, logits + bias, -jnp.inf)
    m = jnp.maximum(jnp.max(logits, axis=-1, keepdims=True), sinks)
    p = jnp.exp(logits - m)
    denom = jnp.sum(p, axis=-1, keepdims=True) + jnp.exp(sinks - m)
    return _dot(p / denom, vband)


def _adamw(w, g, m, v):
    m = ADAM_B1 * m + (1.0 - ADAM_B1) * g
    v = ADAM_B2 * v + (1.0 - ADAM_B2) * jnp.square(g)
    m_hat = m / (1.0 - ADAM_B1 ** ADAM_STEP)
    v_hat = v / (1.0 - ADAM_B2 ** ADAM_STEP)
    delta = -ADAM_LR * (m_hat / (jnp.sqrt(v_hat) + ADAM_EPS) + ADAM_WD * w)
    return delta, m, v


def _row(tm, c, cb=0):
    return pl.BlockSpec((tm, c), lambda i, cb=cb: (i, cb))


def _full(shape):
    nd = len(shape)
    return pl.BlockSpec(shape, lambda *_, nd=nd: (0,) * nd)


def _norm_fwd(x, gain, name, tm=512):
    S = x.shape[0]

    def body(x_ref, g_ref, h_ref):
        h_ref[...] = _rms(x_ref[...], g_ref[...]).astype(BF16)

    return pl.pallas_call(
        body, grid=(S // tm,), in_specs=[_row(tm, D_MODEL), _full((1, D_MODEL))],
        out_specs=_row(tm, D_MODEL), out_shape=jax.ShapeDtypeStruct((S, D_MODEL), BF16),
        name=name, compiler_params=_params(("parallel",)))(x, gain)


def _shift_down(x, s):
    row = lax.broadcasted_iota(jnp.int32, x.shape, 0)
    return jnp.where(row >= s, pltpu.roll(x, s, axis=0), 0.0)


def _shift_up(x, s):
    n = x.shape[0]
    row = lax.broadcasted_iota(jnp.int32, x.shape, 0)
    return jnp.where(row < n - s, pltpu.roll(x, n - s, axis=0), 0.0)


def _conv(x, w):
    out = w[DN_CONV - 1:DN_CONV] * x
    for s in range(1, DN_CONV):
        out = out + w[DN_CONV - 1 - s:DN_CONV - s] * _shift_down(x, s)
    return out


def _dn_conv_fwd(proj, conv_w):
    S = proj.shape[0]
    nb = DN_QKV // LANES

    def body(x_ref, w_ref, o_ref):
        j = pl.program_id(0)
        q_scale = jnp.where(j < DN_HEADS, DN_DIM ** -0.5, 1.0).astype(F32)
        o_ref[...] = _dn_post(_conv(x_ref[...], w_ref[...]), j >= 2 * DN_HEADS, q_scale)

    return pl.pallas_call(
        body, grid=(nb,),
        in_specs=[pl.BlockSpec((S, LANES), lambda j: (0, P_QKV // LANES + j)),
                  pl.BlockSpec((DN_CONV, LANES), lambda j: (0, j))],
        out_specs=pl.BlockSpec((S, LANES), lambda j: (0, j)),
        out_shape=jax.ShapeDtypeStruct((S, DN_QKV), F32), name="dn_conv_fwd",
        compiler_params=_params(("parallel",)))(proj, conv_w)


def _dn_conv_bwd(proj, conv_w, dqkvn, dproj):
    S = proj.shape[0]
    nb = DN_QKV // LANES

    def body(x_ref, w_ref, d_ref, _, dx_ref, dw_ref):
        j = pl.program_id(0)
        q_scale = jnp.where(j < DN_HEADS, DN_DIM ** -0.5, 1.0).astype(F32)
        x = x_ref[...]
        w = w_ref[...]
        _, vjp = jax.vjp(lambda c: _dn_post(c, j >= 2 * DN_HEADS, q_scale), _conv(x, w))
        (dc,) = vjp(d_ref[0])
        dx = w[DN_CONV - 1:DN_CONV] * dc
        dw_ref[DN_CONV - 1:DN_CONV, :] = jnp.sum(dc * x, axis=0, keepdims=True)
        for s in range(1, DN_CONV):
            dx = dx + w[DN_CONV - 1 - s:DN_CONV - s] * _shift_up(dc, s)
            dw_ref[DN_CONV - 1 - s:DN_CONV - s, :] = jnp.sum(dc * _shift_down(x, s), axis=0, keepdims=True)
        dx_ref[...] = dx.astype(BF16)

    return pl.pallas_call(
        body, grid=(nb,),
        in_specs=[pl.BlockSpec((S, LANES), lambda j: (0, P_QKV // LANES + j)),
                  pl.BlockSpec((DN_CONV, LANES), lambda j: (0, j)),
                  pl.BlockSpec((1, S, LANES), lambda j: (lax.div(j, DN_HEADS), 0, lax.rem(j, DN_HEADS))),
                  pl.BlockSpec(memory_space=pl.ANY)],
        out_specs=[pl.BlockSpec((S, LANES), lambda j: (0, P_QKV // LANES + j)),
                   pl.BlockSpec((DN_CONV, LANES), lambda j: (0, j))],
        out_shape=[jax.ShapeDtypeStruct(dproj.shape, dproj.dtype), jax.ShapeDtypeStruct((DN_CONV, DN_QKV), F32)],
        input_output_aliases={3: 0},
        name="dn_conv_bwd", compiler_params=_params(("parallel",)))(proj, conv_w, dqkvn, dproj)


def _expanders():
    eb = np.zeros((LANES, DN_WIDTH), np.float32)
    ea = np.zeros((LANES, DN_WIDTH), np.float32)
    for h in range(DN_HEADS):
        eb[h, h * DN_DIM:(h + 1) * DN_DIM] = 1.0
        ea[DN_HEADS + h, h * DN_DIM:(h + 1) * DN_DIM] = 1.0
    return jnp.asarray(eb), jnp.asarray(ea), jnp.asarray(eb.T), jnp.asarray(ea.T)


def _dn_gate_args(a_log, dt_bias):
    alog = jnp.repeat(a_log.reshape(1, DN_HEADS), DN_DIM, axis=1)
    dtb = jnp.repeat(dt_bias.reshape(1, DN_HEADS), DN_DIM, axis=1)
    return _expanders() + (alog, dtb)


def _dn_gate_specs(tm):
    return [_row(tm, LANES, P_BA // LANES), _full((LANES, DN_WIDTH)), _full((LANES, DN_WIDTH)),
            _full((DN_WIDTH, LANES)), _full((DN_WIDTH, LANES)), _full((1, DN_WIDTH)), _full((1, DN_WIDTH))]


def _dn_gate_fn(ba, eb, ea, ebt, eat, alog, dtb):
    beta = jax.nn.sigmoid(_sel_right(ba, eb, ebt))
    g = -jnp.exp(alog) * jax.nn.softplus(_sel_right(ba, ea, eat) + dtb)
    return beta, g


def _dn_gate_fwd(proj, a_log, dt_bias, tm=512):
    S = proj.shape[0]
    args = _dn_gate_args(a_log, dt_bias)

    def body(ba_ref, eb_ref, ea_ref, ebt_ref, eat_ref, al_ref, dt_ref, beta_ref, g_ref):
        beta, g = _dn_gate_fn(ba_ref[...], eb_ref[...], ea_ref[...], ebt_ref[...], eat_ref[...], al_ref[...],
                              dt_ref[...])
        beta_ref[...] = beta
        g_ref[...] = g

    return pl.pallas_call(
        body, grid=(S // tm,), in_specs=_dn_gate_specs(tm), out_specs=[_row(tm, DN_WIDTH), _row(tm, DN_WIDTH)],
        out_shape=[jax.ShapeDtypeStruct((S, DN_WIDTH), F32), jax.ShapeDtypeStruct((S, DN_WIDTH), F32)],
        name="dn_gate_fwd", compiler_params=_params(("parallel",)))(proj, *args)


def _dn_gate_bwd(proj, a_log, dt_bias, dbeta, dg, dproj, tm=512):
    S = proj.shape[0]
    args = _dn_gate_args(a_log, dt_bias)

    def body(ba_ref, eb_ref, ea_ref, ebt_ref, eat_ref, al_ref, dt_ref, dbeta_ref, dg_ref, _, dba_ref, dal_ref,
             ddt_ref):
        eb, ea, ebt, eat = eb_ref[...], ea_ref[...], ebt_ref[...], eat_ref[...]
        _, vjp = jax.vjp(lambda ba, al, dt: _dn_gate_fn(ba, eb, ea, ebt, eat, al, dt), ba_ref[...], al_ref[...],
                         dt_ref[...])
        dba, dal, ddt = vjp((dbeta_ref[...], dg_ref[...]))
        dba_ref[...] = dba.astype(BF16)

        @pl.when(pl.program_id(0) == 0)
        def _():
            dal_ref[...] = jnp.zeros_like(dal_ref)
            ddt_ref[...] = jnp.zeros_like(ddt_ref)

        dal_ref[...] += dal
        ddt_ref[...] += ddt

    return pl.pallas_call(
        body, grid=(S // tm,),
        in_specs=_dn_gate_specs(tm) + [_row(tm, DN_WIDTH), _row(tm, DN_WIDTH), pl.BlockSpec(memory_space=pl.ANY)],
        out_specs=[_row(tm, LANES, P_BA // LANES), _full((1, DN_WIDTH)), _full((1, DN_WIDTH))],
        out_shape=[jax.ShapeDtypeStruct(dproj.shape, dproj.dtype), jax.ShapeDtypeStruct((1, DN_WIDTH), F32),
                   jax.ShapeDtypeStruct((1, DN_WIDTH), F32)],
        input_output_aliases={len(args) + 3: 0},
        name="dn_gate_bwd", compiler_params=_params(("arbitrary",)))(proj, *args, dbeta, dg, dproj)


PREP_GROUPS = 4
PREP_CHUNKS = GROUP * PREP_GROUPS


def _dn_prep_specs():
    rows = PREP_CHUNKS * CHUNK
    q = pl.BlockSpec((rows, LANES), lambda h, c: (c, h))
    k = pl.BlockSpec((rows, LANES), lambda h, c: (c, DN_HEADS + h))
    v = pl.BlockSpec((rows, LANES), lambda h, c: (c, 2 * DN_HEADS + h))
    qk = pl.BlockSpec((1, rows, CHUNK), lambda h, c: (h, c, 0))
    egl = pl.BlockSpec((1, PREP_CHUNKS, 1, LANES), lambda h, c: (h, c, 0, 0))
    return q, k, v, qk, egl


def _dn_prep_fwd(qkvn, g, beta):
    S = qkvn.shape[0]
    nc = S // CHUNK
    q, k, v, qks, egl = _dn_prep_specs()

    def body(q_ref, k_ref, v_ref, g_ref, b_ref, u_ref, w_ref, qe_ref, kd_ref, qk_ref, egl_ref):
        rows = PREP_CHUNKS * CHUNK
        grp = (PREP_GROUPS, GROUP_ROWS, LANES)
        chk = (PREP_CHUNKS, CHUNK, LANES)
        q, k, g = q_ref[...], k_ref[...], g_ref[...]
        u, w, qe, kd = _dn_group(q.reshape(grp), k.reshape(grp), v_ref[...].reshape(grp), g.reshape(grp),
                                 b_ref[...].reshape(grp))
        u_ref[...] = u.reshape(rows, LANES)
        w_ref[...] = w.reshape(rows, LANES)
        qe_ref[...] = qe.reshape(rows, LANES)
        kd_ref[...] = kd.reshape(rows, LANES)
        qk, e = _dn_chunk(q.reshape(chk), k.reshape(chk), g.reshape(chk))
        qk_ref[0] = qk.reshape(rows, CHUNK)
        egl_ref[0] = e

    wide = jax.ShapeDtypeStruct((S, DN_WIDTH), F32)
    return pl.pallas_call(
        body, grid=(DN_HEADS, nc // PREP_CHUNKS), in_specs=[q, k, v, q, q],
        out_specs=[q, q, q, q, qks, egl],
        out_shape=[wide, wide, wide, wide, jax.ShapeDtypeStruct((DN_HEADS, S, CHUNK), F32),
                   jax.ShapeDtypeStruct((DN_HEADS, nc, 1, LANES), F32)],
        name="dn_prep_fwd", compiler_params=_params(("parallel", "parallel")))(qkvn, qkvn, qkvn, g, beta)


def _dn_prep_bwd(qkvn, g, beta, du, dw, dqe, dkd, dqk, degl):
    S = qkvn.shape[0]
    nc = S // CHUNK
    q, k, v, qks, egl = _dn_prep_specs()

    def body(q_ref, k_ref, v_ref, g_ref, b_ref, du_ref, dw_ref, dqe_ref, dkd_ref, dqk_ref, degl_ref,
             dqkv_ref, dg_ref, db_ref):
        rows = PREP_CHUNKS * CHUNK
        grp = (PREP_GROUPS, GROUP_ROWS, LANES)
        chk = (PREP_CHUNKS, CHUNK, LANES)
        q, k, g = q_ref[...], k_ref[...], g_ref[...]
        _, vjp = jax.vjp(_dn_group, q.reshape(grp), k.reshape(grp), v_ref[...].reshape(grp), g.reshape(grp),
                         b_ref[...].reshape(grp))
        dq, dk, dv, dg, db = vjp((du_ref[...].reshape(grp), dw_ref[...].reshape(grp), dqe_ref[...].reshape(grp),
                                  dkd_ref[...].reshape(grp)))
        _, vjp = jax.vjp(_dn_chunk, q.reshape(chk), k.reshape(chk), g.reshape(chk))
        dq2, dk2, dg2 = vjp((dqk_ref[0].reshape(PREP_CHUNKS, CHUNK, CHUNK), degl_ref[0]))
        dqkv_ref[0] = dq.reshape(rows, LANES) + dq2.reshape(rows, LANES)
        dqkv_ref[1] = dk.reshape(rows, LANES) + dk2.reshape(rows, LANES)
        dqkv_ref[2] = dv.reshape(rows, LANES)
        dg_ref[...] = dg.reshape(rows, LANES) + dg2.reshape(rows, LANES)
        db_ref[...] = db.reshape(rows, LANES)

    wide = jax.ShapeDtypeStruct((S, DN_WIDTH), F32)
    rows = PREP_CHUNKS * CHUNK
    return pl.pallas_call(
        body, grid=(DN_HEADS, nc // PREP_CHUNKS), in_specs=[q, k, v, q, q, q, q, q, q, qks, egl],
        out_specs=[pl.BlockSpec((3, rows, LANES), lambda h, c: (0, c, h)), q, q],
        out_shape=[jax.ShapeDtypeStruct((3, S, DN_WIDTH), F32), wide, wide],
        name="dn_prep_bwd", compiler_params=_params(("parallel", "parallel")),
    )(qkvn, qkvn, qkvn, g, beta, du, dw, dqe, dkd, dqk, degl)


def _dn_scan_specs(nc, reverse):
    def cidx(c):
        return nc - 1 - c if reverse else c

    hc = pl.BlockSpec((CHUNK, DN_WIDTH), lambda c: (cidx(c), 0))
    qk = pl.BlockSpec((DN_HEADS, CHUNK, CHUNK), lambda c: (0, cidx(c), 0))
    egl = pl.BlockSpec((DN_HEADS, 1, 1, LANES), lambda c: (0, cidx(c), 0, 0))
    st = pl.BlockSpec((DN_HEADS, 1, DN_DIM, DN_DIM), lambda c: (0, cidx(c), 0, 0))
    return hc, qk, egl, st


def _heads(ref):
    return jnp.stack([ref[:, pl.ds(h * DN_DIM, DN_DIM)] for h in range(DN_HEADS)])


def _dn_scan_fwd(u, w, qe, kd, qk, egl):
    S = u.shape[0]
    nc = S // CHUNK
    hc, qks, egls, st = _dn_scan_specs(nc, False)

    def body(u_ref, w_ref, qe_ref, kd_ref, qk_ref, egl_ref, o_ref, st_ref, s_scr):
        @pl.when(pl.program_id(0) == 0)
        def _():
            s_scr[...] = jnp.zeros_like(s_scr)

        s = s_scr[...]
        st_ref[:, 0] = s
        s_new, o = _dn_step(s, _heads(u_ref), _heads(w_ref), _heads(qe_ref), _heads(kd_ref), qk_ref[...],
                            egl_ref[:, 0])
        for h in range(DN_HEADS):
            o_ref[:, pl.ds(h * DN_DIM, DN_DIM)] = o[h]
        s_scr[...] = s_new

    return pl.pallas_call(
        body, grid=(nc,), in_specs=[hc, hc, hc, hc, qks, egls], out_specs=[hc, st],
        out_shape=[jax.ShapeDtypeStruct((S, DN_WIDTH), F32), jax.ShapeDtypeStruct((DN_HEADS, nc, DN_DIM, DN_DIM), F32)],
        scratch_shapes=[pltpu.VMEM((DN_HEADS, DN_DIM, DN_DIM), F32)], name="dn_scan_fwd",
        compiler_params=_params(("arbitrary",)))(u, w, qe, kd, qk, egl)


def _dn_scan_bwd(u, w, qe, kd, qk, egl, states, do):
    S = u.shape[0]
    nc = S // CHUNK
    hc, qks, egls, st = _dn_scan_specs(nc, True)

    def body(u_ref, w_ref, qe_ref, kd_ref, qk_ref, egl_ref, st_ref, do_ref,
             du_ref, dw_ref, dqe_ref, dkd_ref, dqk_ref, degl_ref, ds_scr):
        @pl.when(pl.program_id(0) == 0)
        def _():
            ds_scr[...] = jnp.zeros_like(ds_scr)

        _, vjp = jax.vjp(_dn_step, st_ref[:, 0], _heads(u_ref), _heads(w_ref), _heads(qe_ref), _heads(kd_ref),
                         qk_ref[...], egl_ref[:, 0])
        ds, du, dw, dqe, dkd, dqk, degl = vjp((ds_scr[...], _heads(do_ref)))
        ds_scr[...] = ds
        dqk_ref[...] = dqk
        degl_ref[:, 0] = degl
        for h in range(DN_HEADS):
            cols = pl.ds(h * DN_DIM, DN_DIM)
            du_ref[:, cols] = du[h]
            dw_ref[:, cols] = dw[h]
            dqe_ref[:, cols] = dqe[h]
            dkd_ref[:, cols] = dkd[h]

    wide = jax.ShapeDtypeStruct((S, DN_WIDTH), F32)
    return pl.pallas_call(
        body, grid=(nc,), in_specs=[hc, hc, hc, hc, qks, egls, st, hc],
        out_specs=[hc, hc, hc, hc, qks, egls],
        out_shape=[wide, wide, wide, wide, jax.ShapeDtypeStruct((DN_HEADS, S, CHUNK), F32),
                   jax.ShapeDtypeStruct((DN_HEADS, nc, 1, LANES), F32)],
        scratch_shapes=[pltpu.VMEM((DN_HEADS, DN_DIM, DN_DIM), F32)], name="dn_scan_bwd",
        compiler_params=_params(("arbitrary",)))(u, w, qe, kd, qk, egl, states, do)


def _dn_out_fwd(o, proj, gain, tm=512):
    S = o.shape[0]

    def body(o_ref, z_ref, g_ref, y_ref):
        y_ref[...] = _dn_out(o_ref[...], z_ref[...], g_ref[...]).astype(BF16)

    hs = pl.BlockSpec((tm, LANES), lambda i, h: (i, h))
    zs = pl.BlockSpec((tm, LANES), lambda i, h: (i, P_Z // LANES + h))
    return pl.pallas_call(
        body, grid=(S // tm, DN_HEADS), in_specs=[hs, zs, _full((1, DN_DIM))], out_specs=hs,
        out_shape=jax.ShapeDtypeStruct((S, DN_WIDTH), BF16), name="dn_out_fwd",
        compiler_params=_params(("parallel", "parallel")))(o, proj, gain)


_ANY = pl.BlockSpec(memory_space=pl.ANY)


def _dn_out_bwd(o, proj, gain, dy, dproj, tm=512):
    S = o.shape[0]

    def body(o_ref, z_ref, g_ref, dy_ref, _, do_ref, dz_ref, dg_ref):
        _, vjp = jax.vjp(_dn_out, o_ref[...], z_ref[...], g_ref[...])
        do, dz, dg = vjp(dy_ref[...])
        do_ref[...] = do
        dz_ref[...] = dz.astype(BF16)

        @pl.when((pl.program_id(0) == 0) & (pl.program_id(1) == 0))
        def _():
            dg_ref[...] = jnp.zeros_like(dg_ref)

        dg_ref[...] += dg

    hs = pl.BlockSpec((tm, LANES), lambda i, h: (i, h))
    zs = pl.BlockSpec((tm, LANES), lambda i, h: (i, P_Z // LANES + h))
    return pl.pallas_call(
        body, grid=(S // tm, DN_HEADS), in_specs=[hs, zs, _full((1, DN_DIM)), hs, _ANY],
        out_specs=[hs, zs, _full((1, DN_DIM))],
        out_shape=[jax.ShapeDtypeStruct((S, DN_WIDTH), F32), jax.ShapeDtypeStruct(dproj.shape, dproj.dtype),
                   jax.ShapeDtypeStruct((1, DN_DIM), F32)],
        input_output_aliases={4: 1},
        name="dn_out_bwd", compiler_params=_params(("arbitrary", "arbitrary")))(o, proj, gain, dy, dproj)


def _rel_buckets():
    qi = np.arange(BLOCK)[:, None]
    kj = np.arange(2 * BLOCK)[None, :]
    n = np.maximum(BLOCK + qi - kj, 0)
    max_exact = REL_BUCKETS // 2
    nf = np.maximum(n, 1).astype(np.float32)
    large = max_exact + (np.log(nf / np.float32(max_exact)) / np.float32(math.log(REL_MAX_DIST / max_exact))
                         * np.float32(REL_BUCKETS - max_exact)).astype(np.int32)
    large = np.minimum(large, REL_BUCKETS - 1)
    return np.where(n < max_exact, n, large).astype(np.int32)


def _bias_fwd(rel_bias):
    buckets = jnp.asarray(_rel_buckets())

    def body(rb_ref, bk_ref, o_ref):
        bk = bk_ref[...]
        for h in range(SWA_HEADS):
            acc = jnp.zeros((BLOCK, 2 * BLOCK), F32)
            for b in range(REL_BUCKETS):
                acc = jnp.where(bk == b, rb_ref[b, h], acc)
            o_ref[h] = acc

    return pl.pallas_call(
        body, in_specs=[pl.BlockSpec(memory_space=pltpu.SMEM), pl.BlockSpec(memory_space=pltpu.VMEM)],
        out_specs=pl.BlockSpec(memory_space=pltpu.VMEM),
        out_shape=jax.ShapeDtypeStruct((SWA_HEADS, BLOCK, 2 * BLOCK), F32), name="swa_bias_fwd",
        compiler_params=_params())(rel_bias, buckets)


def _bias_bwd(dbias):
    buckets = jnp.asarray(_rel_buckets())

    def body(d_ref, bk_ref, o_ref):
        bk = bk_ref[...]
        lane = lax.broadcasted_iota(jnp.int32, (1, LANES), 1)
        for h in range(SWA_HEADS):
            d = d_ref[h]
            row = jnp.zeros((1, LANES), F32)
            for b in range(REL_BUCKETS):
                part = jnp.sum(jnp.where(bk == b, d, 0.0), axis=1, keepdims=True)
                row = jnp.where(lane == b, jnp.sum(part, axis=0, keepdims=True), row)
            o_ref[h:h + 1, :] = row

    return pl.pallas_call(
        body, in_specs=[pl.BlockSpec(memory_space=pltpu.VMEM), pl.BlockSpec(memory_space=pltpu.VMEM)],
        out_specs=pl.BlockSpec(memory_space=pltpu.VMEM),
        out_shape=jax.ShapeDtypeStruct((SWA_HEADS, LANES), F32), name="swa_bias_bwd",
        compiler_params=_params())(dbias, buckets)


def _swa_mask(n):
    qi = lax.broadcasted_iota(jnp.int32, (BLOCK, 2 * BLOCK), 0)
    kj = lax.broadcasted_iota(jnp.int32, (BLOCK, 2 * BLOCK), 1)
    dist = BLOCK + qi - kj
    return (dist >= 0) & (dist < WINDOW) & ((n > 0) | (kj >= BLOCK))


def _swa_in_specs():
    q = pl.BlockSpec((BLOCK, SWA_WIDTH), lambda n: (n, P_SQ // SWA_WIDTH))
    kc = pl.BlockSpec((BLOCK, SWA_KVW), lambda n: (n, P_SK // SWA_KVW))
    kp = pl.BlockSpec((BLOCK, SWA_KVW), lambda n: (jnp.maximum(n - 1, 0), P_SK // SWA_KVW))
    vc = pl.BlockSpec((BLOCK, SWA_KVW), lambda n: (n, P_SV // SWA_KVW))
    vp = pl.BlockSpec((BLOCK, SWA_KVW), lambda n: (jnp.maximum(n - 1, 0), P_SV // SWA_KVW))
    small = [_full((1, SWA_DIM)), _full((1, SWA_DIM)), _full((1, SWA_HEADS)),
             _full((SWA_HEADS, BLOCK, 2 * BLOCK))]
    return [q, kp, kc, vp, vc] + small


def _swa_load(q_ref, kp_ref, kc_ref, vp_ref, vc_ref, s_ref):
    q = jnp.stack([q_ref[:, pl.ds(h * SWA_DIM, SWA_DIM)] for h in range(SWA_HEADS)])
    kbands, vbands = [], []
    for kv in range(SWA_KV):
        cols = pl.ds(kv * SWA_DIM, SWA_DIM)
        kbands += [jnp.concatenate([kp_ref[:, cols], kc_ref[:, cols]], axis=0)] * SWA_GROUP
        vbands += [jnp.concatenate([vp_ref[:, cols], vc_ref[:, cols]], axis=0)] * SWA_GROUP
    sinks = jnp.stack([s_ref[:, pl.ds(h, 1)] for h in range(SWA_HEADS)])
    return q, jnp.stack(kbands), jnp.stack(vbands), sinks


def _swa_fwd(proj, q_gain, k_gain, sinks, bias):
    S = proj.shape[0]

    def body(q_ref, kp_ref, kc_ref, vp_ref, vc_ref, qg_ref, kg_ref, s_ref, bias_ref, y_ref):
        mask = _swa_mask(pl.program_id(0))
        q, kband, vband, sk = _swa_load(q_ref, kp_ref, kc_ref, vp_ref, vc_ref, s_ref)
        out = _swa_block(q, kband, vband, qg_ref[...], kg_ref[...], sk, bias_ref[...], mask)
        for h in range(SWA_HEADS):
            y_ref[:, pl.ds(h * SWA_DIM, SWA_DIM)] = out[h].astype(BF16)

    return pl.pallas_call(
        body, grid=(S // BLOCK,), in_specs=_swa_in_specs(),
        out_specs=pl.BlockSpec((BLOCK, SWA_WIDTH), lambda n: (n, 0)),
        out_shape=jax.ShapeDtypeStruct((S, SWA_WIDTH), BF16), name="swa_fwd",
        compiler_params=_params(("parallel",)))(proj, proj, proj, proj, proj, q_gain, k_gain, sinks, bias)


def _swa_bwd(proj, q_gain, k_gain, sinks, bias, dy, dproj):
    S = proj.shape[0]

    def body(q_ref, kp_ref, kc_ref, vp_ref, vc_ref, qg_ref, kg_ref, s_ref, bias_ref, dy_ref, _,
             dq_ref, dk_ref, dv_ref, dqg_ref, dkg_ref, ds_ref, dbias_ref):
        n = pl.program_id(0)
        mask = _swa_mask(n)

        @pl.when(n == 0)
        def _():
            for r in (dk_ref, dv_ref, dqg_ref, dkg_ref, ds_ref, dbias_ref):
                r[...] = jnp.zeros_like(r)

        cur = pl.ds(pl.multiple_of(n * BLOCK, BLOCK), BLOCK)
        prev = pl.ds(pl.multiple_of(jnp.maximum(n - 1, 0) * BLOCK, BLOCK), BLOCK)
        q, kband, vband, sk = _swa_load(q_ref, kp_ref, kc_ref, vp_ref, vc_ref, s_ref)
        _, vjp = jax.vjp(lambda q, kb, vb, qg, kg, sk, bs: _swa_block(q, kb, vb, qg, kg, sk, bs, mask),
                         q, kband, vband, qg_ref[...], kg_ref[...], sk, bias_ref[...])
        dy = jnp.stack([dy_ref[:, pl.ds(h * SWA_DIM, SWA_DIM)] for h in range(SWA_HEADS)])
        dq, dkb, dvb, dqg, dkg, dsk, dbs = vjp(dy)
        for h in range(SWA_HEADS):
            dq_ref[:, pl.ds(h * SWA_DIM, SWA_DIM)] = dq[h].astype(BF16)
            ds_ref[:, pl.ds(h, 1)] += dsk[h]
        dbias_ref[...] += dbs
        dqg_ref[...] += dqg
        dkg_ref[...] += dkg
        for kv in range(SWA_KV):
            cols = pl.ds(kv * SWA_DIM, SWA_DIM)
            group = range(kv * SWA_GROUP, (kv + 1) * SWA_GROUP)
            dk_kv = sum(dkb[h] for h in group)
            dv_kv = sum(dvb[h] for h in group)
            dk_ref[cur, cols] += dk_kv[BLOCK:]
            dv_ref[cur, cols] += dv_kv[BLOCK:]

            @pl.when(n > 0)
            def _(cols=cols, dk_kv=dk_kv, dv_kv=dv_kv):
                dk_ref[prev, cols] += dk_kv[:BLOCK]
                dv_ref[prev, cols] += dv_kv[:BLOCK]

    return pl.pallas_call(
        body, grid=(S // BLOCK,),
        in_specs=_swa_in_specs() + [pl.BlockSpec((BLOCK, SWA_WIDTH), lambda n: (n, 0)),
                                    pl.BlockSpec(memory_space=pl.ANY)],
        out_specs=[pl.BlockSpec((BLOCK, SWA_WIDTH), lambda n: (n, P_SQ // SWA_WIDTH)), _full((S, SWA_KVW)),
                   _full((S, SWA_KVW)), _full((1, SWA_DIM)), _full((1, SWA_DIM)), _full((1, SWA_HEADS)),
                   _full((SWA_HEADS, BLOCK, 2 * BLOCK))],
        out_shape=[jax.ShapeDtypeStruct(dproj.shape, dproj.dtype), jax.ShapeDtypeStruct((S, SWA_KVW), F32),
                   jax.ShapeDtypeStruct((S, SWA_KVW), F32), jax.ShapeDtypeStruct((1, SWA_DIM), F32),
                   jax.ShapeDtypeStruct((1, SWA_DIM), F32), jax.ShapeDtypeStruct((1, SWA_HEADS), F32),
                   jax.ShapeDtypeStruct((SWA_HEADS, BLOCK, 2 * BLOCK), F32)],
        input_output_aliases={10: 0},
        name="swa_bwd", compiler_params=_params(("arbitrary",)),
    )(proj, proj, proj, proj, proj, q_gain, k_gain, sinks, bias, dy, dproj)


def _kv_into(dproj, dk, dv, tm=512):
    S = dk.shape[0]

    def body(dk_ref, dv_ref, _, o_ref):
        o_ref[:, :SWA_KVW] = dk_ref[...].astype(BF16)
        o_ref[:, SWA_KVW:] = dv_ref[...].astype(BF16)

    return pl.pallas_call(
        body, grid=(S // tm,), in_specs=[_row(tm, SWA_KVW), _row(tm, SWA_KVW), pl.BlockSpec(memory_space=pl.ANY)],
        out_specs=_row(tm, 2 * SWA_KVW, P_SK // (2 * SWA_KVW)),
        out_shape=jax.ShapeDtypeStruct(dproj.shape, dproj.dtype), input_output_aliases={2: 0},
        name="swa_kv_into", compiler_params=_params(("parallel",)))(dk, dv, dproj)


def _position():
    return lax.axis_index("x"), lax.axis_index("y"), lax.axis_index("c")


def _all_gather(shards, name="all_gather_weights"):
    na = len(shards)

    def body(*refs):
        x_refs, out_refs = refs[:na], refs[na:2 * na]
        send_sems, recv_sems, local_sems = refs[2 * na:]
        x, y, c = _position()
        me, sibling = (x, y, c), (x, y, 1 - c)
        chips = [(1 - x, y), (x, 1 - y), (1 - x, 1 - y)]

        def copy(a, k, block, to, own=False):
            px, py, pc = block
            slot = out_refs[a].at[4 * px + 2 * py + pc]
            return pltpu.make_async_remote_copy(
                src_ref=x_refs[a] if own else slot, dst_ref=slot, send_sem=send_sems.at[7 * a + k],
                recv_sem=recv_sems.at[7 * a + k], device_id=to, device_id_type=MESH_ID)

        mine = [pltpu.make_async_copy(x_refs[a], out_refs[a].at[4 * x + 2 * y + c], local_sems.at[a])
                for a in range(na)]
        for cp in mine:
            cp.start()
        first = []
        for a in range(na):
            first.append(copy(a, 0, me, sibling, own=True))
            first += [copy(a, 1 + j, me, (*chip, c), own=True) for j, chip in enumerate(chips)]
        for cp in first:
            cp.start()
        passed = []
        for j, chip in enumerate(chips):
            for a in range(na):
                copy(a, 1 + j, (*chip, c), me).wait_recv()
                passed.append(copy(a, 4 + j, (*chip, c), sibling))
                passed[-1].start()
        for a in range(na):
            copy(a, 0, sibling, me).wait_recv()
            for j, chip in enumerate(chips):
                copy(a, 4 + j, (*chip, 1 - c), me).wait_recv()
        for cp in first + passed:
            cp.wait_send()
        for cp in mine:
            cp.wait()

    return pl.pallas_call(
        body, in_specs=[pl.BlockSpec(memory_space=pl.ANY)] * na, out_specs=[pl.BlockSpec(memory_space=pl.ANY)] * na,
        out_shape=[jax.ShapeDtypeStruct((N_DEV,) + s.shape, s.dtype) for s in shards],
        scratch_shapes=[pltpu.SemaphoreType.DMA((7 * na,)), pltpu.SemaphoreType.DMA((7 * na,)),
                        pltpu.SemaphoreType.DMA((na,))],
        name=name)(*shards)


_HBM = pl.BlockSpec(memory_space=pltpu.HBM)
_SEM = pl.BlockSpec(memory_space=pltpu.SEMAPHORE)
_DATAFLOW = pltpu.SideEffectType.DATAFLOW_SIDE_EFFECTING


def _peers(x, y, c):
    out = []
    for k in range(1, N_DEV):
        px, py, pc = x ^ (k >> 2), y ^ ((k >> 1) & 1), c ^ (k & 1)
        out.append(((px, py, pc), 4 * px + 2 * py + pc))
    return out


def _split_copies(src_refs, land_refs, send_sems, recv_sems, scatter):
    x, y, c = _position()
    me = 4 * x + 2 * y + c
    sends, recvs = [], []
    for k, (peer_id, peer) in enumerate(_peers(x, y, c)):
        for a, (src, land) in enumerate(zip(src_refs, land_refs)):
            sems = dict(send_sem=send_sems.at[7 * a + k], recv_sem=recv_sems.at[7 * a + k],
                        device_id=peer_id, device_id_type=MESH_ID)
            mine = src.at[peer] if scatter else src
            sends.append(pltpu.make_async_remote_copy(src_ref=mine, dst_ref=land.at[me], **sems))
            recvs.append(pltpu.make_async_remote_copy(src_ref=mine, dst_ref=land.at[peer], **sems))
    return sends, recvs


def _all_gather_direct(shards, name):
    na = len(shards)

    def body(*refs):
        x_refs, out_refs = refs[:na], refs[na:2 * na]
        send_sems, recv_sems, local_sems = refs[2 * na:]
        x, y, c = _position()
        me = 4 * x + 2 * y + c
        local = [pltpu.make_async_copy(x_refs[a], out_refs[a].at[me], local_sems.at[a]) for a in range(na)]
        sends, recvs = _split_copies(x_refs, out_refs, send_sems, recv_sems, False)
        for cp in local + sends:
            cp.start()
        for cp in recvs:
            cp.wait_recv()
        for cp in sends:
            cp.wait_send()
        for cp in local:
            cp.wait()

    return pl.pallas_call(
        body, in_specs=[pl.BlockSpec(memory_space=pl.ANY)] * na, out_specs=[pl.BlockSpec(memory_space=pl.ANY)] * na,
        out_shape=[jax.ShapeDtypeStruct((N_DEV,) + s.shape, s.dtype) for s in shards],
        scratch_shapes=[pltpu.SemaphoreType.DMA((7 * na,)), pltpu.SemaphoreType.DMA((7 * na,)),
                        pltpu.SemaphoreType.DMA((na,))],
        name=name)(*shards)


def _exchange_start(srcs, scatter, name, after=None):
    na = len(srcs)
    lands = [lax.empty(s.shape if scatter else (N_DEV,) + s.shape, s.dtype) for s in srcs]
    extra = [] if after is None else [after]

    def body(*refs):
        src_refs, land_refs = refs[:na], refs[na:2 * na]
        send_sems, recv_sems = refs[2 * na + len(extra)], refs[2 * na + len(extra) + 1]
        token = refs[-1]
        sends, _ = _split_copies(src_refs, land_refs, send_sems, recv_sems, scatter)
        for cp in sends:
            cp.start()
        token[...] = jnp.zeros_like(token)

    hbm = lambda a: pltpu.HBM(a.shape, a.dtype)
    out = pl.pallas_call(
        body, name=name,
        out_shape=(pltpu.SemaphoreType.DMA((7 * na,)), pltpu.SemaphoreType.DMA((7 * na,)),
                   *[hbm(s) for s in srcs], *[hbm(l) for l in lands], jax.ShapeDtypeStruct((8, LANES), F32)),
        in_specs=[_HBM] * (2 * na) + [pl.BlockSpec(memory_space=pl.ANY)] * len(extra),
        out_specs=(_SEM, _SEM, *[_HBM] * (2 * na), pl.BlockSpec(memory_space=pltpu.VMEM)),
        input_output_aliases={i: 2 + i for i in range(2 * na)},
        compiler_params=pltpu.CompilerParams(has_side_effects=_DATAFLOW),
    )(*[pltpu.with_memory_space_constraint(s, pltpu.HBM) for s in srcs],
      *[pltpu.with_memory_space_constraint(l, pltpu.HBM) for l in lands], *extra)
    return (out[0], out[1], list(out[2:2 + na]), list(out[2 + na:2 + 2 * na])), out[-1]


def _exchange_wait(handle, after, scatter, name):
    send_sems, recv_sems, srcs, lands = handle
    na = len(srcs)

    def body(*refs):
        src_refs, land_refs = refs[:na], refs[na:2 * na]
        s_sems, r_sems = refs[2 * na], refs[2 * na + 1]
        sends, recvs = _split_copies(src_refs, land_refs, s_sems, r_sems, scatter)
        for cp in sends:
            cp.wait_send()
        for cp in recvs:
            cp.wait_recv()

    hbm = lambda a: pltpu.HBM(a.shape, a.dtype)
    out = pl.pallas_call(
        body, name=name, out_shape=(*[hbm(s) for s in srcs], *[hbm(l) for l in lands]),
        in_specs=[_HBM] * (2 * na) + [_SEM, _SEM, pl.BlockSpec(memory_space=pl.ANY)],
        out_specs=tuple([_HBM] * (2 * na)), input_output_aliases={i: i for i in range(2 * na)},
        compiler_params=pltpu.CompilerParams(has_side_effects=_DATAFLOW),
    )(*srcs, *lands, send_sems, recv_sems, after)
    return list(out[:na]), list(out[na:])


def _own_slot(landed, own):
    me = 4 * lax.axis_index("x") + 2 * lax.axis_index("y") + lax.axis_index("c")
    return lax.dynamic_update_slice_in_dim(landed, own[None], me, axis=0)


def _adam_update(parts, w, m, v, name, tr=256):
    _, r, c = w.shape
    tr = _pick_rows(r, tr)
    cp = parts.shape[2]

    def body(p_ref, w_ref, m_ref, v_ref, g_ref, d_ref, nm_ref, nv_ref):
        g = p_ref[0, :, pl.ds(0, c)].astype(F32)
        for i in range(1, N_DEV):
            g = g + p_ref[i, :, pl.ds(0, c)].astype(F32)
        delta, nm, nv = _adamw(w_ref[0], g, m_ref[0], v_ref[0])
        g_ref[0] = g
        d_ref[0] = delta
        nm_ref[0] = nm
        nv_ref[0] = nv

    rs = pl.BlockSpec((1, tr, c), lambda i: (0, i, 0))
    return pl.pallas_call(
        body, grid=(r // tr,), in_specs=[pl.BlockSpec((N_DEV, tr, cp), lambda i: (0, i, 0)), rs, rs, rs],
        out_specs=[rs] * 4, out_shape=[jax.ShapeDtypeStruct((1, r, c), F32)] * 4, name=name,
        compiler_params=_params(("parallel",)))(parts, w, m, v)


def _pick_rows(rows, target):
    if rows <= target:
        return rows
    t = target
    while t >= 16:
        if rows % t == 0:
            return t
        t -= 16
    return rows


BIG = ("w_in", "w_branch_dn", "w_branch_swa", "w_out", "w_gate", "w_up", "w_down")
IN_SHARD, IN_WIRE = D_IN // N_DEV, 640
FF_SHARD, FF_WIRE = D_FF // N_DEV, 384
D_FFP = N_DEV * FF_WIRE
BIG_SHAPES = {"w_in": ((D_MODEL, IN_SHARD), (D_MODEL, IN_WIRE)),
              "w_branch_dn": ((DN_WIDTH, LANES), (DN_WIDTH, LANES)),
              "w_branch_swa": ((SWA_WIDTH, LANES), (SWA_WIDTH, LANES)),
              "w_out": ((LANES, D_MODEL), (LANES, D_MODEL)),
              "w_gate": ((D_MODEL, FF_SHARD), (D_MODEL, FF_WIRE)),
              "w_up": ((D_MODEL, FF_SHARD), (D_MODEL, FF_WIRE)),
              "w_down": ((FF_SHARD, D_MODEL), (FF_WIRE, D_MODEL))}
CONV_SHARD, CONV_WIRE = (DN_CONV, DN_QKV // N_DEV), (8, 256)


def _pad_to(a, shape):
    return jnp.pad(a, [(0, t - s) for s, t in zip(a.shape, shape)])


_IN_SEGS = ((R_GATE, 2048, P_GATE), (R_QKV, DN_QKV, P_QKV), (R_Z, DN_WIDTH, P_Z), (R_SQ, SWA_WIDTH, P_SQ),
            (R_SK, SWA_KVW, P_SK), (R_SV, SWA_KVW, P_SV), (R_B, 8, P_BA))


def _w_in_from_blocks(blocks):
    parts = []
    for rs, n, _ in _IN_SEGS:
        for dev in range(N_DEV):
            lo, hi = max(rs, IN_SHARD * dev), min(rs + n, IN_SHARD * (dev + 1))
            if lo < hi:
                parts.append(blocks[dev, :, lo - IN_SHARD * dev:hi - IN_SHARD * dev])
    parts.append(jnp.zeros((blocks.shape[1], P_WIDTH - P_BA - 8), blocks.dtype))
    return jnp.concatenate(parts, axis=1)


def _w_in_to_blocks(g):
    out = []
    for dev in range(N_DEV):
        parts = []
        for rs, n, ps in sorted(_IN_SEGS):
            lo, hi = max(rs, IN_SHARD * dev), min(rs + n, IN_SHARD * (dev + 1))
            if lo < hi:
                parts.append(g[:, ps + lo - rs:ps + hi - rs])
        parts.append(jnp.zeros((g.shape[0], IN_WIRE - IN_SHARD), g.dtype))
        out.append(jnp.concatenate(parts, axis=1))
    return jnp.stack(out)


SMALL = {"attn_norm": (0, (1, D_MODEL)), "ffn_norm": (1, (1, D_MODEL)), "dn_out_norm": (2, (1, DN_DIM)),
         "swa_q_norm": (3, (1, SWA_DIM)), "swa_k_norm": (4, (1, SWA_DIM)), "dn_a_log": (5, (1, DN_HEADS)),
         "dn_dt_bias": (6, (1, DN_HEADS)), "swa_sinks": (7, (1, SWA_HEADS)), "rel_bias": (8, (REL_BUCKETS, SWA_HEADS))}
SMALL_SHEET = (48, D_MODEL)


def _small_pack(grads):
    names = list(SMALL)

    def body(*refs):
        o_ref = refs[-1]
        o_ref[...] = jnp.zeros_like(o_ref)
        for n, ref in zip(names, refs):
            r0, (nr, nc) = SMALL[n]
            o_ref[r0:r0 + nr, 0:nc] = ref[...]

    return pl.pallas_call(
        body, in_specs=[pl.BlockSpec(memory_space=pltpu.VMEM)] * len(names),
        out_specs=pl.BlockSpec(memory_space=pltpu.VMEM), out_shape=jax.ShapeDtypeStruct(SMALL_SHEET, F32),
        name="small_pack", compiler_params=_params())(*[grads[n].reshape(SMALL[n][1]) for n in names])


def _small_update(sheets, w, m, v):
    names = list(SMALL)
    k = len(names)

    def body(*refs):
        p_ref = refs[0]
        ins, outs = refs[1:1 + 3 * k], refs[1 + 3 * k:]
        for t, n in enumerate(names):
            r0, (nr, nc) = SMALL[n]
            g = p_ref[0, r0:r0 + nr, 0:nc]
            for i in range(1, N_DEV):
                g = g + p_ref[i, r0:r0 + nr, 0:nc]
            delta, nm, nv = _adamw(ins[t][...], g, ins[k + t][...], ins[2 * k + t][...])
            for kind, val in enumerate((g, delta, nm, nv)):
                outs[kind * k + t][...] = val

    shapes = [jax.ShapeDtypeStruct(SMALL[n][1], F32) for n in names]
    vm = pl.BlockSpec(memory_space=pltpu.VMEM)
    res = pl.pallas_call(
        body, in_specs=[vm] * (1 + 3 * k), out_specs=[vm] * (4 * k), out_shape=shapes * 4, name="adam_small",
        compiler_params=_params(),
    )(sheets, *[d[n].reshape(SMALL[n][1]) for d in (w, m, v) for n in names])
    return {n: tuple(res[kind * k + t] for kind in range(4)) for t, n in enumerate(names)}


def kernel(x, attn_norm, w_in, dn_conv, dn_a_log, dn_dt_bias, dn_out_norm, swa_q_norm, swa_k_norm, swa_sinks, rel_bias, w_branch_dn, w_branch_swa, w_out, ffn_norm, w_gate, w_up, w_down, loss_target, m_attn_norm, m_w_in, m_dn_conv, m_dn_a_log, m_dn_dt_bias, m_dn_out_norm, m_swa_q_norm, m_swa_k_norm, m_swa_sinks, m_rel_bias, m_w_branch_dn, m_w_branch_swa, m_w_out, m_ffn_norm, m_w_gate, m_w_up, m_w_down, v_attn_norm, v_w_in, v_dn_conv, v_dn_a_log, v_dn_dt_bias, v_dn_out_norm, v_swa_q_norm, v_swa_k_norm, v_swa_sinks, v_rel_bias, v_w_branch_dn, v_w_branch_swa, v_w_out, v_ffn_norm, v_w_gate, v_w_up, v_w_down):
    args = dict(locals())
    S = x.shape[1]
    xs = x.reshape(S, D_MODEL)
    target = loss_target.reshape(S, D_MODEL)

    w_loc = {n: args[n].reshape(BIG_SHAPES[n][0]) for n in BIG}
    conv_loc = dn_conv.reshape(CONV_SHARD)
    wire = {n: _pad_to(w_loc[n], BIG_SHAPES[n][1]).astype(BF16) for n in BIG}
    first = _all_gather([wire["w_in"], _pad_to(conv_loc, CONV_WIRE)])
    later = [n for n in BIG if n != "w_in"]
    rest_handle, rest_token = _exchange_start([wire[n] for n in later], False, "gather_rest_start", after=first[1])
    w_pad = _w_in_from_blocks(first[0])
    conv_w = jnp.concatenate([first[1][d, :DN_CONV, :CONV_SHARD[1]] for d in range(N_DEV)], axis=1)

    h = _norm_fwd(xs, attn_norm + rest_token[0, 0], "norm1_fwd")
    proj = _mm([(h, w_pad)], "nn", F32, "mm_in", 512, 1664, j_outer=True)
    qkvn = _dn_conv_fwd(proj, conv_w)
    beta, g = _dn_gate_fwd(proj, dn_a_log, dn_dt_bias)
    u, w, qe, kd, qk, egl = _dn_prep_fwd(qkvn, g, beta)
    o, states = _dn_scan_fwd(u, w, qe, kd, qk, egl)
    y_dn = _dn_out_fwd(o, proj, dn_out_norm)
    bias = _bias_fwd(rel_bias)
    y_swa = _swa_fwd(proj, swa_q_norm, swa_k_norm, swa_sinks, bias)
    rest_src, rest_land = _exchange_wait(rest_handle, y_swa, False, "gather_rest_wait")
    G = {n: _own_slot(land, src) for n, src, land in zip(later, rest_src, rest_land)}
    w_bdn, w_bswa, w_g, w_u = G["w_branch_dn"], G["w_branch_swa"], G["w_gate"], G["w_up"]
    w_o = G["w_out"].reshape(D_MODEL, D_MODEL)
    w_d = G["w_down"].reshape(D_FFP, D_MODEL)
    gates = [(proj, P_GATE // 512), (proj, (P_GATE + D_MODEL) // 512)]
    a_dn, a_swa, merged = _mm_fused(
        [(y_dn, w_bdn), (y_swa, w_bswa)], "nn", "mm_branch_merge", 1024, 512,
        lambda p, e: (p[0], p[1], _merge(e[0], e[1], p[0], p[1])), gates, (F32, F32, BF16), b_blocks=True)

    def resid_norm(p, e):
        x1 = e[0] + p[0]
        return x1, _rms(x1, e[1])

    x1, h2 = _mm_fused([(merged, w_o)], "nn", "mm_out_norm", 512, D_MODEL, resid_norm,
                       [(xs, 0), (ffn_norm, None)], (F32, BF16))
    gate, up, act = _mm_fused([(h2, w_g), (h2, w_u)], "nn", "mm_gate_up_act", 1024, 768,
                              lambda p, e: (p[0], p[1], _act(p[0], p[1])), [], (F32, F32, BF16),
                              j_outer=True, b_blocks=True)

    def loss_head(p, e):
        diff = e[0] + p[0] - e[1]
        dy = diff * (1.0 / D_MODEL)
        part = jnp.sum(jnp.mean(diff * diff, axis=-1, keepdims=True), axis=0, keepdims=True) * 0.5
        return dy, dy, part

    dy, dy_b, loss_local = _mm_fused([(act, w_d)], "nn", "mm_down_loss", 512, D_MODEL, loss_head,
                                     [(x1, 0), (target, 0)], (F32, BF16), sum_shape=(1, 1))

    def act_bwd(p, e):
        _, vjp = jax.vjp(_act, e[0], e[1])
        return vjp(p[0])

    dgate, dup = _mm_fused([(dy_b, w_d)], "nt", "mm_dact_act", 1024, 768, act_bwd, [(gate, 0), (up, 0)],
                           (BF16, BF16), j_outer=True)
    g_w_down = _mm([(act, dy_b)], "tn", BF16, "mm_dw_down", 768, D_MODEL, j_outer=True)
    g_w_down = g_w_down.reshape(N_DEV, FF_WIRE, D_MODEL)
    g_w_gate = _mm([(h2, dgate)], "tn", BF16, "mm_dw_gate", D_MODEL, 768, out_blocks=True)
    g_w_up = _mm([(h2, dup)], "tn", BF16, "mm_dw_up", D_MODEL, 768, out_blocks=True)
    ffn_handle, ffn_token = _exchange_start([g_w_down, g_w_gate, g_w_up], True, "scatter_ffn_start")

    def norm_bwd(p, e):
        _, vjp = jax.vjp(_rms, e[0], e[2])
        dx, dgain = vjp(sum(p))
        dx = dx + e[1]
        return dx, dx, dgain

    dx1, dx1_b, g_ffn_norm = _mm_fused(
        [(dgate, w_g), (dup, w_u)], "nt", "mm_dh2_norm", 256, D_MODEL, norm_bwd,
        [(x1, 0), (dy, 0), (ffn_norm + ffn_token[0, 0], None)], (F32, BF16), b_blocks=True, sum_shape=(1, D_MODEL))
    def merge_bwd(p, e):
        _, vjp = jax.vjp(_merge, *e)
        dg0, dg1, da_dn, da_swa = vjp(p[0])
        return jnp.concatenate([dg0, dg1], axis=1), da_dn, da_swa

    dproj, da_dn, da_swa = _mm_fused(
        [(dx1_b, w_o)], "nt", "mm_dmerged_merge", 512, D_MODEL, merge_bwd,
        [(proj, P_GATE // D_MODEL), (proj, P_GATE // D_MODEL + 1), (a_dn, 0), (a_swa, 0)], (BF16,) * 3,
        wide_first=(P_WIDTH, 2 * D_MODEL))
    g_w_out = _mm([(merged, dx1_b)], "tn", BF16, "mm_dw_out", 512, D_MODEL, j_outer=True)
    g_w_out = g_w_out.reshape(N_DEV, LANES, D_MODEL)
    dy_dn = _mm([(da_dn, w_bdn)], "nt", F32, "mm_dy_dn", 1024, DN_WIDTH, b_blocks=True)
    dy_swa = _mm([(da_swa, w_bswa)], "nt", F32, "mm_dy_swa", 1024, SWA_WIDTH, b_blocks=True)
    g_w_bdn = _mm([(y_dn, da_dn)], "tn", BF16, "mm_dw_branch_dn", DN_WIDTH, 512, out_blocks=True)
    g_w_bswa = _mm([(y_swa, da_swa)], "tn", BF16, "mm_dw_branch_swa", SWA_WIDTH, 512, out_blocks=True)
    dproj, dsk, dsv, g_q_norm, g_k_norm, g_sinks, dbias = _swa_bwd(proj, swa_q_norm, swa_k_norm, swa_sinks, bias,
                                                                   dy_swa, dproj)
    dproj = _kv_into(dproj, dsk, dsv)
    g_rel_bias = _bias_bwd(dbias)[:, :REL_BUCKETS].T
    mix_handle, mix_token = _exchange_start([g_w_out, g_w_bdn, g_w_bswa], True, "scatter_mix_start")
    do, dproj, g_out_norm = _dn_out_bwd(o, proj, dn_out_norm + mix_token[0, 0], dy_dn, dproj)
    du, dw, dqe, dkd, dqk, degl = _dn_scan_bwd(u, w, qe, kd, qk, egl, states, do)
    dqkvn, dgd, dbeta = _dn_prep_bwd(qkvn, g, beta, du, dw, dqe, dkd, dqk, degl)
    dproj, dal, ddt = _dn_gate_bwd(proj, dn_a_log, dn_dt_bias, dbeta, dgd, dproj)
    g_a_log = dal.reshape(DN_HEADS, DN_DIM).sum(axis=1)
    g_dt_bias = ddt.reshape(DN_HEADS, DN_DIM).sum(axis=1)
    dproj, g_conv = _dn_conv_bwd(proj, conv_w, dqkvn, dproj)
    g_w_in = _w_in_to_blocks(_mm([(h, dproj)], "tn", BF16, "mm_dw_in", 512, 1664, j_outer=True))
    in_handle, in_token = _exchange_start([g_w_in], True, "scatter_in_start")
    dx, g_attn_norm = _mm_fused(
        [(dproj, w_pad)], "nt", "mm_dh_norm", 256, D_MODEL, lambda p, e: norm_bwd(p, e)[1:],
        [(xs, 0), (dx1, 0), (attn_norm + in_token[0, 0], None)], (F32,), sum_shape=(1, D_MODEL))

    g_small = {"attn_norm": g_attn_norm, "ffn_norm": g_ffn_norm, "rel_bias": g_rel_bias, "dn_out_norm": g_out_norm,
               "swa_q_norm": g_q_norm, "swa_k_norm": g_k_norm, "dn_a_log": g_a_log, "dn_dt_bias": g_dt_bias,
               "swa_sinks": g_sinks}
    sheets, conv_all = _all_gather_direct([_small_pack(g_small), _pad_to(g_conv, (8, DN_QKV))], "all_gather_small")
    me = 4 * lax.axis_index("x") + 2 * lax.axis_index("y") + lax.axis_index("c")
    recv_small = sheets
    received = {}
    for handle, group, name in ((ffn_handle, ("w_down", "w_gate", "w_up"), "scatter_ffn_wait"),
                                (mix_handle, ("w_out", "w_branch_dn", "w_branch_swa"), "scatter_mix_wait"),
                                (in_handle, ("w_in",), "scatter_in_wait")):
        srcs, lands = _exchange_wait(handle, recv_small, True, name)
        for n, src, land in zip(group, srcs, lands):
            received[n] = _own_slot(land, lax.dynamic_index_in_dim(src, me, 0, keepdims=False))

    outs = {}
    for n in BIG:
        outs[n] = _adam_update(received[n], args[n], args["m_" + n], args["v_" + n], "adam_" + n)
    conv_parts = lax.dynamic_slice(conv_all, (0, 0, me * CONV_SHARD[1]), (N_DEV,) + CONV_SHARD)
    outs["dn_conv"] = _adam_update(conv_parts, dn_conv, m_dn_conv, v_dn_conv, "adam_dn_conv")
    outs.update(_small_update(sheets, {n: args[n] for n in SMALL}, {n: args["m_" + n] for n in SMALL},
                              {n: args["v_" + n] for n in SMALL}))

    names = ("attn_norm", "w_in", "dn_conv", "dn_a_log", "dn_dt_bias", "dn_out_norm", "swa_q_norm", "swa_k_norm",
             "swa_sinks", "rel_bias", "w_branch_dn", "w_branch_swa", "w_out", "ffn_norm", "w_gate", "w_up", "w_down")
    results = []
    for kind in range(4):
        results += [outs[n][kind].reshape(args[n].shape) for n in names]

    loss = lax.psum(loss_local[0, 0], ("x", "y", "c"))
    return (loss, dx.reshape(x.shape), *results)
```

```python
import math

import numpy as np
import jax
import jax.numpy as jnp
from jax import lax
from jax.experimental import pallas as pl
from jax.experimental.pallas import tpu as pltpu

F32 = jnp.float32
BF16 = jnp.bfloat16
HI = lax.Precision.HIGHEST

D_MODEL = 1024
DN_HEADS = 4
DN_DIM = 128
DN_WIDTH = 512
DN_QKV = 1536
DN_CONV = 4
CHUNK = 64
SWA_HEADS = 8
SWA_KV = 2
SWA_GROUP = 4
SWA_DIM = 64
SWA_WIDTH = 512
SWA_KVW = 128
WINDOW = 128
BLOCK = 128
REL_BUCKETS = 32
REL_MAX_DIST = 128
D_FF = 2816
D_IN = 4872
EPS = 1e-6
N_DEV = 8

ADAM_LR = 0.001
ADAM_B1 = 0.9
ADAM_B2 = 0.999
ADAM_EPS = 1e-08
ADAM_WD = 0.01
ADAM_STEP = 10

P_GATE, P_QKV, P_Z, P_SQ, P_SK, P_SV, P_BA = 0, 2048, 3584, 4096, 4608, 4736, 4864
P_WIDTH = 4992
R_QKV, R_Z, R_B, R_A, R_SQ, R_SK, R_SV, R_GATE = 0, 1536, 2048, 2052, 2056, 2568, 2696, 2824

VMEM_LIMIT = 56 * 1024 * 1024
LANES = 128
MESH_ID = pl.DeviceIdType.MESH


def _params(sem=None):
    return pltpu.CompilerParams(dimension_semantics=sem, vmem_limit_bytes=VMEM_LIMIT)


def _pick(dim, target):
    if dim <= target:
        return dim
    t = target - target % LANES
    while t >= LANES:
        if dim % t == 0:
            return t
        t -= LANES
    return dim


_DIMS = {"nn": (((1,), (0,)), ((), ())), "nt": (((1,), (1,)), ((), ())), "tn": (((0,), (0,)), ((), ()))}


def _tile_product(a_ref, b_ref, mode, b_blocks):
    a = a_ref[...].astype(BF16)
    if not b_blocks:
        return lax.dot_general(a, b_ref[...].astype(BF16), _DIMS[mode], preferred_element_type=F32)
    nb, _, c = b_ref.shape
    if mode == "nt" and c >= 2 * LANES:
        return sum(lax.dot_general(a[:, d * c:(d + 1) * c], b_ref[d].astype(BF16), _DIMS[mode],
                                   preferred_element_type=F32) for d in range(nb))
    b = jnp.concatenate([b_ref[d] for d in range(nb)], axis=1)
    return lax.dot_general(a, b.astype(BF16), _DIMS[mode], preferred_element_type=F32)


def _mm(pairs, mode, out_dtype, name, bm, bn, j_outer=False, b_blocks=False, out_blocks=False):
    a0, b0 = pairs[0]
    cb = b0.shape[2] if b_blocks else None
    b_shape = (b0.shape[1], N_DEV * cb) if b_blocks else b0.shape
    if mode == "nn":
        (M, K), (K2, N) = a0.shape, b_shape
    elif mode == "nt":
        (M, K), (N, K2) = a0.shape, b_shape
    else:
        (K, M), (K2, N) = a0.shape, b_shape
    bm, bn = min(bm, M), min(bn, N)
    assert K == K2 and M % bm == 0 and N % bn == 0, (name, a0.shape, b0.shape, bm, bn)
    co = N // N_DEV
    assert not out_blocks or bn % co == 0
    dims = _DIMS[mode]
    n = len(pairs)

    def body(*refs):
        o_ref = refs[2 * n]
        acc = None
        for t in range(n):
            p = _tile_product(refs[2 * t], refs[2 * t + 1], mode, b_blocks)
            acc = p if acc is None else acc + p
        if out_blocks:
            for d in range(bn // co):
                o_ref[d] = acc[:, d * co:(d + 1) * co].astype(out_dtype)
        else:
            o_ref[...] = acc.astype(out_dtype)

    def ij(f):
        return (lambda j, i: f(i, j)) if j_outer else f

    a_spec = pl.BlockSpec((K, bm), ij(lambda i, j: (0, i))) if mode == "tn" else pl.BlockSpec((bm, K), ij(lambda i, j: (i, 0)))
    if b_blocks and mode == "nt":
        b_spec = pl.BlockSpec((N_DEV, bn, cb), ij(lambda i, j: (0, j, 0)))
    elif b_blocks:
        b_spec = pl.BlockSpec((bn // cb, K, cb), ij(lambda i, j: (j, 0, 0)))
    elif mode == "nt":
        b_spec = pl.BlockSpec((bn, K), ij(lambda i, j: (j, 0)))
    else:
        b_spec = pl.BlockSpec((K, bn), ij(lambda i, j: (0, j)))
    if out_blocks:
        out_spec = pl.BlockSpec((bn // co, bm, co), ij(lambda i, j: (j, i, 0)))
        out_shape = jax.ShapeDtypeStruct((N_DEV, M, co), out_dtype)
    else:
        out_spec = pl.BlockSpec((bm, bn), ij(lambda i, j: (i, j)))
        out_shape = jax.ShapeDtypeStruct((M, N), out_dtype)
    grid = (N // bn, M // bm) if j_outer else (M // bm, N // bn)
    return pl.pallas_call(
        body, grid=grid, in_specs=[a_spec, b_spec] * n, out_specs=out_spec, out_shape=out_shape, name=name,
        compiler_params=_params(("parallel", "parallel")),
    )(*[x for pair in pairs for x in pair])


def _mm_fused(pairs, mode, name, bm, bn, epilogue, extras, out_dtypes, j_outer=False, b_blocks=False,
              sum_shape=None, wide_first=None):
    a0, b0 = pairs[0]
    cb = b0.shape[2] if b_blocks else None
    b_shape = (b0.shape[1], N_DEV * cb) if b_blocks else b0.shape
    if mode == "nn":
        (M, K), (K2, N) = a0.shape, b_shape
    else:
        (M, K), (N, K2) = a0.shape, b_shape
    bm, bn = min(bm, M), min(bn, N)
    assert mode in ("nn", "nt") and K == K2 and M % bm == 0 and N % bn == 0, (name, a0.shape, b0.shape)
    dims = _DIMS[mode]
    n, ne, no = len(pairs), len(extras), len(out_dtypes)

    def body(*refs):
        prods = [_tile_product(refs[2 * t], refs[2 * t + 1], mode, b_blocks) for t in range(n)]
        results = epilogue(prods, [r[...] for r in refs[2 * n:2 * n + ne]])
        out_refs = refs[2 * n + ne:]
        for o_ref, val, dt in zip(out_refs, results, out_dtypes):
            o_ref[...] = val.astype(dt)
        if sum_shape is not None:
            s_ref = out_refs[no]

            @pl.when((pl.program_id(0) == 0) & (pl.program_id(1) == 0))
            def _():
                s_ref[...] = jnp.zeros_like(s_ref)

            s_ref[...] += results[no]

    def ij(f):
        return (lambda j, i: f(i, j)) if j_outer else f

    a_spec = pl.BlockSpec((bm, K), ij(lambda i, j: (i, 0)))
    if b_blocks and mode == "nt":
        b_spec = pl.BlockSpec((N_DEV, bn, cb), ij(lambda i, j: (0, j, 0)))
    elif b_blocks:
        b_spec = pl.BlockSpec((bn // cb, K, cb), ij(lambda i, j: (j, 0, 0)))
    elif mode == "nt":
        b_spec = pl.BlockSpec((bn, K), ij(lambda i, j: (j, 0)))
    else:
        b_spec = pl.BlockSpec((K, bn), ij(lambda i, j: (0, j)))
    e_specs = [pl.BlockSpec((1, bn), ij(lambda i, j: (0, j))) if first is None
               else pl.BlockSpec((bm, bn), ij(lambda i, j, first=first: (i, first + j))) for _, first in extras]
    tile = pl.BlockSpec((bm, bn), ij(lambda i, j: (i, j)))
    out_specs = [tile] * no
    out_shape = [jax.ShapeDtypeStruct((M, N), dt) for dt in out_dtypes]
    if wide_first is not None:
        assert bn == N
        out_specs[0] = pl.BlockSpec((bm, wide_first[1]), ij(lambda i, j: (i, 0)))
        out_shape[0] = jax.ShapeDtypeStruct((M, wide_first[0]), out_dtypes[0])
    if sum_shape is not None:
        assert sum_shape[1] in (1, bn) and (sum_shape[1] == 1 or bn == N)
        out_specs.append(_full(sum_shape))
        out_shape.append(jax.ShapeDtypeStruct(sum_shape, F32))
    grid = (N // bn, M // bm) if j_outer else (M // bm, N // bn)
    sem = ("arbitrary", "arbitrary") if sum_shape is not None else ("parallel", "parallel")
    return pl.pallas_call(
        body, grid=grid, in_specs=[a_spec, b_spec] * n + e_specs, out_specs=out_specs, out_shape=out_shape,
        name=name, compiler_params=_params(sem),
    )(*[x for pair in pairs for x in pair], *[arr for arr, _ in extras])


def _rms(x, gain):
    return x * lax.rsqrt(jnp.mean(x * x, axis=-1, keepdims=True) + EPS) * gain


def _silu(x):
    return x * jax.nn.sigmoid(x)


def _act(g, u):
    return _silu(g) * u


def _merge(g0, g1, a_dn, a_swa):
    return jax.nn.sigmoid(g0) * a_dn + jax.nn.sigmoid(g1) * a_swa


def _dn_post(c, is_v, q_scale):
    a = _silu(c)
    rs = lax.rsqrt(jnp.sum(a * a, axis=-1, keepdims=True) + EPS) * q_scale
    return a * jnp.where(is_v, 1.0, rs)


def _dn_out(o, z, gain):
    return _rms(o, gain) * _silu(z)


def _dot(a, b, dims=_DIMS["nn"], hi=False):
    if a.ndim == 3 or b.ndim == 3:
        batch = a.shape[0] if a.ndim == 3 else b.shape[0]
        a = a if a.ndim == 3 else jnp.broadcast_to(a, (batch,) + a.shape)
        b = b if b.ndim == 3 else jnp.broadcast_to(b, (batch,) + b.shape)
        ((ca,), (cb,)), _ = dims
        dims = (((ca + 1,), (cb + 1,)), ((0,), (0,)))
    if hi:
        return lax.dot_general(a, b, dims, precision=HI, preferred_element_type=F32)
    return lax.dot_general(a.astype(BF16), b.astype(BF16), dims, preferred_element_type=F32)


def _pieces(x):
    hi = x.astype(BF16)
    r1 = x - hi.astype(F32)
    mid = r1.astype(BF16)
    return hi, mid, (r1 - mid.astype(F32)).astype(BF16)


def _sel_left_impl(m, x):
    mb = m.astype(BF16)
    hi, mid, lo = _pieces(x)
    return _dot(mb, hi) + (_dot(mb, mid) + _dot(mb, lo))


@jax.custom_vjp
def _sel_left(m, mt, x):
    return _sel_left_impl(m, x)


_sel_left.defvjp(lambda m, mt, x: (_sel_left_impl(m, x), (m, mt)),
                 lambda res, ct: (jnp.zeros_like(res[0]), jnp.zeros_like(res[1]), _sel_left_impl(res[1], ct)))


def _sel_right_impl(x, s):
    sb = s.astype(BF16)
    hi, mid, lo = _pieces(x)
    return _dot(hi, sb) + (_dot(mid, sb) + _dot(lo, sb))


@jax.custom_vjp
def _sel_right(x, s, st):
    return _sel_right_impl(x, s)


_sel_right.defvjp(lambda x, s, st: (_sel_right_impl(x, s), (s, st)),
                  lambda res, ct: (_sel_right_impl(ct, res[1]), jnp.zeros_like(res[0]), jnp.zeros_like(res[1])))


def _sel_nt_impl(s, x):
    sb = s.astype(BF16)
    hi, mid, lo = _pieces(x)
    return _dot(sb, hi, _DIMS["nt"]) + (_dot(sb, mid, _DIMS["nt"]) + _dot(sb, lo, _DIMS["nt"]))


def _sel_tn_impl(x, s):
    sb = s.astype(BF16)
    hi, mid, lo = _pieces(x)
    return _dot(hi, sb, _DIMS["tn"]) + (_dot(mid, sb, _DIMS["tn"]) + _dot(lo, sb, _DIMS["tn"]))


@jax.custom_vjp
def _sel_nt(s, x):
    return _sel_nt_impl(s, x)


_sel_nt.defvjp(lambda s, x: (_sel_nt_impl(s, x), s),
               lambda s, ct: (jnp.zeros_like(s), _sel_tn_impl(ct, s)))


def _dot3_impl(a, b):
    a_hi, a_lo, _ = _pieces(a)
    b_hi, b_lo, _ = _pieces(b)
    return _dot(a_hi, b_hi) + (_dot(a_hi, b_lo) + _dot(a_lo, b_hi))


@jax.custom_vjp
def _dot3(a, b):
    return _dot3_impl(a, b)


_dot3.defvjp(lambda a, b: (_dot3_impl(a, b), (a, b)),
             lambda res, ct: (_dot(ct, res[1], _DIMS["nt"]), _dot(res[0], ct, _DIMS["tn"])))


def _inv_impl(a, eye, strict):
    t = eye - a
    p = _dot(a, a)
    for level in range(5):
        t = t + _dot(t, p)
        if level < 4:
            p = _dot(p, p)
    t = t + _dot(t, eye - t - _dot3_impl(a, t))
    return jnp.where(strict > 0.5, t, eye)


@jax.custom_vjp
def _inv_given(a, t):
    return t.astype(F32)


_inv_given.defvjp(lambda a, t: (t.astype(F32), t),
                  lambda t, ct: (-_dot(_dot(t, ct, _DIMS["tn"]), t, _DIMS["nt"]), jnp.zeros_like(t)))


@jax.custom_vjp
def _lanes_join(a, b):
    return jnp.concatenate([a, b], axis=-1)


_lanes_join.defvjp(lambda a, b: (jnp.concatenate([a, b], axis=-1), None),
                   lambda _, ct: (ct[..., :ct.shape[-1] // 2], ct[..., ct.shape[-1] // 2:]))


@jax.custom_vjp
def _lanes_halves(y):
    h = y.shape[-1] // 2
    return y[..., :h], y[..., h:]


_lanes_halves.defvjp(lambda y: ((y[..., :y.shape[-1] // 2], y[..., y.shape[-1] // 2:]), None),
                     lambda _, ct: (jnp.concatenate(ct, axis=-1),))

GROUP = 4
GROUP_ROWS = GROUP * CHUNK


def _block_consts(n):
    ii = lax.broadcasted_iota(jnp.int32, (n, n), 0)
    jj = lax.broadcasted_iota(jnp.int32, (n, n), 1)
    shift = CHUNK.bit_length() - 1
    same = jnp.right_shift(ii, shift) == jnp.right_shift(jj, shift)
    return same & (ii >= jj), same & (ii <= jj), same & (ii > jj), same, ii == jj


def _lane0(n):
    s = (lax.broadcasted_iota(jnp.int32, (LANES, n), 0) == 0).astype(F32)
    st = (lax.broadcasted_iota(jnp.int32, (n, LANES), 1) == 0).astype(F32)
    return s, st


def _dn_group(q, k, v, g, beta, t_saved=None):
    n = GROUP_ROWS
    low_b, upp_b, strict_b, _, eye_b = _block_consts(n)
    low, upp, eye = low_b.astype(F32), upp_b.astype(F32), eye_b.astype(F32)
    s, st = _lane0(n)
    gc = _sel_left(low, upp, g)
    per_chunk = (g.shape[0], GROUP, CHUNK, LANES)
    gl = jnp.broadcast_to(jnp.sum(g.reshape(per_chunk), axis=2, keepdims=True), per_chunk).reshape(g.shape)
    col = _sel_right(gc, s, st)
    row = _sel_nt(st, gc)
    decay = jnp.exp(jnp.where(low_b, col - row, -jnp.inf))
    kb = k * beta
    vb = v * beta
    a = jnp.where(strict_b, _dot(kb, k, _DIMS["nt"]) * decay, 0.0)
    t = _inv_impl(a, eye, strict_b.astype(F32)) if t_saved is None else _inv_given(a, t_saved)
    u, w = _lanes_halves(_dot3(t, _lanes_join(vb, kb * jnp.exp(gc))))
    return u, w, q * jnp.exp(gc), k * jnp.exp(gl - gc), t


def _dn_chunk(q, k, g):
    ii = lax.broadcasted_iota(jnp.int32, (CHUNK, CHUNK), 0)
    jj = lax.broadcasted_iota(jnp.int32, (CHUNK, CHUNK), 1)
    low = (ii >= jj).astype(F32)
    upp = (ii <= jj).astype(F32)
    s, st = _lane0(CHUNK)
    gc = _sel_left(low, upp, g)
    col = _sel_right(gc, s, st)
    row = _sel_nt(st, gc)
    decay = jnp.exp(jnp.where(ii >= jj, col - row, -jnp.inf))
    qk = _dot(q, k, _DIMS["nt"]) * decay
    return qk, jnp.exp(jnp.sum(g, axis=-2, keepdims=True))


def _dn_step(s, u, w, qe, kd, qk, egl):
    v_new = u - _dot(w, s)
    o = _dot(qe, s) + _dot(qk, v_new)
    s_new = s * egl + _dot(kd, v_new, _DIMS["tn"])
    return s_new, o


def _swa_block(q, kband, vband, qg, kg, sinks, bias, mask):
    kn = _rms(kband, kg)
    qn = _rms(q, qg)
    logits = _dot(qn, kn, _DIMS["nt"]) * (SWA_DIM ** -0.5)
    logits = jnp.where(mask, logits + bias, -jnp.inf)
    m = jnp.maximum(jnp.max(logits, axis=-1, keepdims=True), sinks)
    p = jnp.exp(logits - m)
    denom = jnp.sum(p, axis=-1, keepdims=True) + jnp.exp(sinks - m)
    return _dot(p / denom, vband)


def _adamw(w, g, m, v):
    m = ADAM_B1 * m + (1.0 - ADAM_B1) * g
    v = ADAM_B2 * v + (1.0 - ADAM_B2) * jnp.square(g)
    m_hat = m / (1.0 - ADAM_B1 ** ADAM_STEP)
    v_hat = v / (1.0 - ADAM_B2 ** ADAM_STEP)
    delta = -ADAM_LR * (m_hat / (jnp.sqrt(v_hat) + ADAM_EPS) + ADAM_WD * w)
    return delta, m, v


def _row(tm, c, cb=0):
    return pl.BlockSpec((tm, c), lambda i, cb=cb: (i, cb))


def _full(shape):
    nd = len(shape)
    return pl.BlockSpec(shape, lambda *_, nd=nd: (0,) * nd)


def _norm_fwd(x, gain, name, tm=512):
    S = x.shape[0]

    def body(x_ref, g_ref, h_ref):
        h_ref[...] = _rms(x_ref[...], g_ref[...]).astype(BF16)

    return pl.pallas_call(
        body, grid=(S // tm,), in_specs=[_row(tm, D_MODEL), _full((1, D_MODEL))],
        out_specs=_row(tm, D_MODEL), out_shape=jax.ShapeDtypeStruct((S, D_MODEL), BF16),
        name=name, compiler_params=_params(("parallel",)))(x, gain)


def _shift_down(x, s):
    row = lax.broadcasted_iota(jnp.int32, x.shape, 0)
    return jnp.where(row >= s, pltpu.roll(x, s, axis=0), 0.0)


def _shift_up(x, s):
    n = x.shape[0]
    row = lax.broadcasted_iota(jnp.int32, x.shape, 0)
    return jnp.where(row < n - s, pltpu.roll(x, n - s, axis=0), 0.0)


def _conv(x, w):
    out = w[DN_CONV - 1:DN_CONV] * x
    for s in range(1, DN_CONV):
        out = out + w[DN_CONV - 1 - s:DN_CONV - s] * _shift_down(x, s)
    return out


def _dn_conv_fwd(proj, conv_w):
    S = proj.shape[0]
    nb = DN_QKV // LANES

    def body(x_ref, w_ref, o_ref):
        j = pl.program_id(0)
        q_scale = jnp.where(j < DN_HEADS, DN_DIM ** -0.5, 1.0).astype(F32)
        o_ref[...] = _dn_post(_conv(x_ref[...], w_ref[...]), j >= 2 * DN_HEADS, q_scale)

    return pl.pallas_call(
        body, grid=(nb,),
        in_specs=[pl.BlockSpec((S, LANES), lambda j: (0, P_QKV // LANES + j)),
                  pl.BlockSpec((DN_CONV, LANES), lambda j: (0, j))],
        out_specs=pl.BlockSpec((S, LANES), lambda j: (0, j)),
        out_shape=jax.ShapeDtypeStruct((S, DN_QKV), F32), name="dn_conv_fwd",
        compiler_params=_params(("parallel",)))(proj, conv_w)


def _dn_conv_bwd(proj, conv_w, dqkvn, dproj):
    S = proj.shape[0]
    nb = DN_QKV // LANES

    def body(x_ref, w_ref, d_ref, _, dx_ref, dw_ref):
        j = pl.program_id(0)
        q_scale = jnp.where(j < DN_HEADS, DN_DIM ** -0.5, 1.0).astype(F32)
        x = x_ref[...]
        w = w_ref[...]
        _, vjp = jax.vjp(lambda c: _dn_post(c, j >= 2 * DN_HEADS, q_scale), _conv(x, w))
        (dc,) = vjp(d_ref[0])
        dx = w[DN_CONV - 1:DN_CONV] * dc
        dw_ref[DN_CONV - 1:DN_CONV, :] = jnp.sum(dc * x, axis=0, keepdims=True)
        for s in range(1, DN_CONV):
            dx = dx + w[DN_CONV - 1 - s:DN_CONV - s] * _shift_up(dc, s)
            dw_ref[DN_CONV - 1 - s:DN_CONV - s, :] = jnp.sum(dc * _shift_down(x, s), axis=0, keepdims=True)
        dx_ref[...] = dx.astype(BF16)

    return pl.pallas_call(
        body, grid=(nb,),
        in_specs=[pl.BlockSpec((S, LANES), lambda j: (0, P_QKV // LANES + j)),
                  pl.BlockSpec((DN_CONV, LANES), lambda j: (0, j)),
                  pl.BlockSpec((1, S, LANES), lambda j: (lax.div(j, DN_HEADS), 0, lax.rem(j, DN_HEADS))),
                  pl.BlockSpec(memory_space=pl.ANY)],
        out_specs=[pl.BlockSpec((S, LANES), lambda j: (0, P_QKV // LANES + j)),
                   pl.BlockSpec((DN_CONV, LANES), lambda j: (0, j))],
        out_shape=[jax.ShapeDtypeStruct(dproj.shape, dproj.dtype), jax.ShapeDtypeStruct((DN_CONV, DN_QKV), F32)],
        input_output_aliases={3: 0},
        name="dn_conv_bwd", compiler_params=_params(("parallel",)))(proj, conv_w, dqkvn, dproj)


def _expanders():
    eb = np.zeros((LANES, DN_WIDTH), np.float32)
    ea = np.zeros((LANES, DN_WIDTH), np.float32)
    for h in range(DN_HEADS):
        eb[h, h * DN_DIM:(h + 1) * DN_DIM] = 1.0
        ea[DN_HEADS + h, h * DN_DIM:(h + 1) * DN_DIM] = 1.0
    return jnp.asarray(eb), jnp.asarray(ea), jnp.asarray(eb.T), jnp.asarray(ea.T)


def _dn_gate_args(a_log, dt_bias):
    alog = jnp.repeat(a_log.reshape(1, DN_HEADS), DN_DIM, axis=1)
    dtb = jnp.repeat(dt_bias.reshape(1, DN_HEADS), DN_DIM, axis=1)
    return _expanders() + (alog, dtb)


def _dn_gate_specs(tm):
    return [_row(tm, LANES, P_BA // LANES), _full((LANES, DN_WIDTH)), _full((LANES, DN_WIDTH)),
            _full((DN_WIDTH, LANES)), _full((DN_WIDTH, LANES)), _full((1, DN_WIDTH)), _full((1, DN_WIDTH))]


def _dn_gate_fn(ba, eb, ea, ebt, eat, alog, dtb):
    beta = jax.nn.sigmoid(_sel_right(ba, eb, ebt))
    g = -jnp.exp(alog) * jax.nn.softplus(_sel_right(ba, ea, eat) + dtb)
    return beta, g


def _dn_gate_fwd(proj, a_log, dt_bias, tm=512):
    S = proj.shape[0]
    args = _dn_gate_args(a_log, dt_bias)

    def body(ba_ref, eb_ref, ea_ref, ebt_ref, eat_ref, al_ref, dt_ref, beta_ref, g_ref):
        beta, g = _dn_gate_fn(ba_ref[...], eb_ref[...], ea_ref[...], ebt_ref[...], eat_ref[...], al_ref[...],
                              dt_ref[...])
        beta_ref[...] = beta
        g_ref[...] = g

    return pl.pallas_call(
        body, grid=(S // tm,), in_specs=_dn_gate_specs(tm), out_specs=[_row(tm, DN_WIDTH), _row(tm, DN_WIDTH)],
        out_shape=[jax.ShapeDtypeStruct((S, DN_WIDTH), F32), jax.ShapeDtypeStruct((S, DN_WIDTH), F32)],
        name="dn_gate_fwd", compiler_params=_params(("parallel",)))(proj, *args)


def _dn_gate_bwd(proj, a_log, dt_bias, dbeta, dg, dproj, tm=512):
    S = proj.shape[0]
    args = _dn_gate_args(a_log, dt_bias)

    def body(ba_ref, eb_ref, ea_ref, ebt_ref, eat_ref, al_ref, dt_ref, dbeta_ref, dg_ref, _, dba_ref, dal_ref,
             ddt_ref):
        eb, ea, ebt, eat = eb_ref[...], ea_ref[...], ebt_ref[...], eat_ref[...]
        _, vjp = jax.vjp(lambda ba, al, dt: _dn_gate_fn(ba, eb, ea, ebt, eat, al, dt), ba_ref[...], al_ref[...],
                         dt_ref[...])
        dba, dal, ddt = vjp((dbeta_ref[...], dg_ref[...]))
        dba_ref[...] = dba.astype(BF16)

        @pl.when(pl.program_id(0) == 0)
        def _():
            dal_ref[...] = jnp.zeros_like(dal_ref)
            ddt_ref[...] = jnp.zeros_like(ddt_ref)

        dal_ref[...] += dal
        ddt_ref[...] += ddt

    return pl.pallas_call(
        body, grid=(S // tm,),
        in_specs=_dn_gate_specs(tm) + [_row(tm, DN_WIDTH), _row(tm, DN_WIDTH), pl.BlockSpec(memory_space=pl.ANY)],
        out_specs=[_row(tm, LANES, P_BA // LANES), _full((1, DN_WIDTH)), _full((1, DN_WIDTH))],
        out_shape=[jax.ShapeDtypeStruct(dproj.shape, dproj.dtype), jax.ShapeDtypeStruct((1, DN_WIDTH), F32),
                   jax.ShapeDtypeStruct((1, DN_WIDTH), F32)],
        input_output_aliases={len(args) + 3: 0},
        name="dn_gate_bwd", compiler_params=_params(("arbitrary",)))(proj, *args, dbeta, dg, dproj)


PREP_GROUPS = 4
PREP_CHUNKS = GROUP * PREP_GROUPS


def _dn_prep_specs():
    rows = PREP_CHUNKS * CHUNK
    q = pl.BlockSpec((rows, LANES), lambda h, c: (c, h))
    k = pl.BlockSpec((rows, LANES), lambda h, c: (c, DN_HEADS + h))
    v = pl.BlockSpec((rows, LANES), lambda h, c: (c, 2 * DN_HEADS + h))
    qk = pl.BlockSpec((1, rows, CHUNK), lambda h, c: (h, c, 0))
    egl = pl.BlockSpec((1, PREP_CHUNKS, 1, LANES), lambda h, c: (h, c, 0, 0))
    return q, k, v, qk, egl


def _dn_prep_fwd(qkvn, g, beta):
    S = qkvn.shape[0]
    nc = S // CHUNK
    q, k, v, qks, egl = _dn_prep_specs()

    def body(q_ref, k_ref, v_ref, g_ref, b_ref, u_ref, w_ref, qe_ref, kd_ref, qk_ref, egl_ref, t_ref):
        rows = PREP_CHUNKS * CHUNK
        grp = (PREP_GROUPS, GROUP_ROWS, LANES)
        chk = (PREP_CHUNKS, CHUNK, LANES)
        q, k, g = q_ref[...], k_ref[...], g_ref[...]
        u, w, qe, kd, t = _dn_group(q.reshape(grp), k.reshape(grp), v_ref[...].reshape(grp), g.reshape(grp),
                                    b_ref[...].reshape(grp))
        u_ref[...] = u.reshape(rows, LANES)
        w_ref[...] = w.reshape(rows, LANES)
        qe_ref[...] = qe.reshape(rows, LANES)
        kd_ref[...] = kd.reshape(rows, LANES)
        t_ref[0] = t.reshape(rows, GROUP_ROWS).astype(BF16)
        qk, e = _dn_chunk(q.reshape(chk), k.reshape(chk), g.reshape(chk))
        qk_ref[0] = qk.reshape(rows, CHUNK)
        egl_ref[0] = e

    wide = jax.ShapeDtypeStruct((S, DN_WIDTH), F32)
    return pl.pallas_call(
        body, grid=(DN_HEADS, nc // PREP_CHUNKS), in_specs=[q, k, v, q, q],
        out_specs=[q, q, q, q, qks, egl, _dn_tinv_spec()],
        out_shape=[wide, wide, wide, wide, jax.ShapeDtypeStruct((DN_HEADS, S, CHUNK), F32),
                   jax.ShapeDtypeStruct((DN_HEADS, nc, 1, LANES), F32),
                   jax.ShapeDtypeStruct((DN_HEADS, S, GROUP_ROWS), BF16)],
        name="dn_prep_fwd", compiler_params=_params(("parallel", "parallel")))(qkvn, qkvn, qkvn, g, beta)


def _dn_tinv_spec():
    return pl.BlockSpec((1, PREP_CHUNKS * CHUNK, GROUP_ROWS), lambda h, c: (h, c, 0))


def _dn_prep_bwd(qkvn, g, beta, tinv, du, dw, dqe, dkd, dqk, degl):
    S = qkvn.shape[0]
    nc = S // CHUNK
    q, k, v, qks, egl = _dn_prep_specs()

    def body(q_ref, k_ref, v_ref, g_ref, b_ref, t_ref, du_ref, dw_ref, dqe_ref, dkd_ref, dqk_ref, degl_ref,
             dqkv_ref, dg_ref, db_ref):
        rows = PREP_CHUNKS * CHUNK
        grp = (PREP_GROUPS, GROUP_ROWS, LANES)
        chk = (PREP_CHUNKS, CHUNK, LANES)
        q, k, g = q_ref[...], k_ref[...], g_ref[...]
        t_saved = t_ref[0].reshape(PREP_GROUPS, GROUP_ROWS, GROUP_ROWS)
        _, vjp = jax.vjp(lambda *x: _dn_group(*x, t_saved=t_saved)[:4], q.reshape(grp), k.reshape(grp),
                         v_ref[...].reshape(grp), g.reshape(grp), b_ref[...].reshape(grp))
        dq, dk, dv, dg, db = vjp((du_ref[...].reshape(grp), dw_ref[...].reshape(grp), dqe_ref[...].reshape(grp),
                                  dkd_ref[...].reshape(grp)))
        _, vjp = jax.vjp(_dn_chunk, q.reshape(chk), k.reshape(chk), g.reshape(chk))
        dq2, dk2, dg2 = vjp((dqk_ref[0].reshape(PREP_CHUNKS, CHUNK, CHUNK), degl_ref[0]))
        dqkv_ref[0] = dq.reshape(rows, LANES) + dq2.reshape(rows, LANES)
        dqkv_ref[1] = dk.reshape(rows, LANES) + dk2.reshape(rows, LANES)
        dqkv_ref[2] = dv.reshape(rows, LANES)
        dg_ref[...] = dg.reshape(rows, LANES) + dg2.reshape(rows, LANES)
        db_ref[...] = db.reshape(rows, LANES)

    wide = jax.ShapeDtypeStruct((S, DN_WIDTH), F32)
    rows = PREP_CHUNKS * CHUNK
    return pl.pallas_call(
        body, grid=(DN_HEADS, nc // PREP_CHUNKS), in_specs=[q, k, v, q, q, _dn_tinv_spec(), q, q, q, q, qks, egl],
        out_specs=[pl.BlockSpec((3, rows, LANES), lambda h, c: (0, c, h)), q, q],
        out_shape=[jax.ShapeDtypeStruct((3, S, DN_WIDTH), F32), wide, wide],
        name="dn_prep_bwd", compiler_params=_params(("parallel", "parallel")),
    )(qkvn, qkvn, qkvn, g, beta, tinv, du, dw, dqe, dkd, dqk, degl)


def _dn_scan_specs(nc, reverse):
    def cidx(c):
        return nc - 1 - c if reverse else c

    hc = pl.BlockSpec((CHUNK, DN_WIDTH), lambda c: (cidx(c), 0))
    qk = pl.BlockSpec((DN_HEADS, CHUNK, CHUNK), lambda c: (0, cidx(c), 0))
    egl = pl.BlockSpec((DN_HEADS, 1, 1, LANES), lambda c: (0, cidx(c), 0, 0))
    st = pl.BlockSpec((DN_HEADS, 1, DN_DIM, DN_DIM), lambda c: (0, cidx(c), 0, 0))
    return hc, qk, egl, st


def _heads(ref):
    return jnp.stack([ref[:, pl.ds(h * DN_DIM, DN_DIM)] for h in range(DN_HEADS)])


def _dn_scan_fwd(u, w, qe, kd, qk, egl):
    S = u.shape[0]
    nc = S // CHUNK
    hc, qks, egls, st = _dn_scan_specs(nc, False)

    def body(u_ref, w_ref, qe_ref, kd_ref, qk_ref, egl_ref, o_ref, st_ref, s_scr):
        @pl.when(pl.program_id(0) == 0)
        def _():
            s_scr[...] = jnp.zeros_like(s_scr)

        s = s_scr[...]
        st_ref[:, 0] = s
        s_new, o = _dn_step(s, _heads(u_ref), _heads(w_ref), _heads(qe_ref), _heads(kd_ref), qk_ref[...],
                            egl_ref[:, 0])
        for h in range(DN_HEADS):
            o_ref[:, pl.ds(h * DN_DIM, DN_DIM)] = o[h]
        s_scr[...] = s_new

    return pl.pallas_call(
        body, grid=(nc,), in_specs=[hc, hc, hc, hc, qks, egls], out_specs=[hc, st],
        out_shape=[jax.ShapeDtypeStruct((S, DN_WIDTH), F32), jax.ShapeDtypeStruct((DN_HEADS, nc, DN_DIM, DN_DIM), F32)],
        scratch_shapes=[pltpu.VMEM((DN_HEADS, DN_DIM, DN_DIM), F32)], name="dn_scan_fwd",
        compiler_params=_params(("arbitrary",)))(u, w, qe, kd, qk, egl)


def _dn_scan_bwd(u, w, qe, kd, qk, egl, states, do):
    S = u.shape[0]
    nc = S // CHUNK
    hc, qks, egls, st = _dn_scan_specs(nc, True)

    def body(u_ref, w_ref, qe_ref, kd_ref, qk_ref, egl_ref, st_ref, do_ref,
             du_ref, dw_ref, dqe_ref, dkd_ref, dqk_ref, degl_ref, ds_scr):
        @pl.when(pl.program_id(0) == 0)
        def _():
            ds_scr[...] = jnp.zeros_like(ds_scr)

        _, vjp = jax.vjp(_dn_step, st_ref[:, 0], _heads(u_ref), _heads(w_ref), _heads(qe_ref), _heads(kd_ref),
                         qk_ref[...], egl_ref[:, 0])
        ds, du, dw, dqe, dkd, dqk, degl = vjp((ds_scr[...], _heads(do_ref)))
        ds_scr[...] = ds
        dqk_ref[...] = dqk
        degl_ref[:, 0] = degl
        for h in range(DN_HEADS):
            cols = pl.ds(h * DN_DIM, DN_DIM)
            du_ref[:, cols] = du[h]
            dw_ref[:, cols] = dw[h]
            dqe_ref[:, cols] = dqe[h]
            dkd_ref[:, cols] = dkd[h]

    wide = jax.ShapeDtypeStruct((S, DN_WIDTH), F32)
    return pl.pallas_call(
        body, grid=(nc,), in_specs=[hc, hc, hc, hc, qks, egls, st, hc],
        out_specs=[hc, hc, hc, hc, qks, egls],
        out_shape=[wide, wide, wide, wide, jax.ShapeDtypeStruct((DN_HEADS, S, CHUNK), F32),
                   jax.ShapeDtypeStruct((DN_HEADS, nc, 1, LANES), F32)],
        scratch_shapes=[pltpu.VMEM((DN_HEADS, DN_DIM, DN_DIM), F32)], name="dn_scan_bwd",
        compiler_params=_params(("arbitrary",)))(u, w, qe, kd, qk, egl, states, do)


def _dn_out_fwd(o, proj, gain, tm=512):
    S = o.shape[0]

    def body(o_ref, z_ref, g_ref, y_ref):
        y_ref[...] = _dn_out(o_ref[...], z_ref[...], g_ref[...]).astype(BF16)

    hs = pl.BlockSpec((tm, LANES), lambda i, h: (i, h))
    zs = pl.BlockSpec((tm, LANES), lambda i, h: (i, P_Z // LANES + h))
    return pl.pallas_call(
        body, grid=(S // tm, DN_HEADS), in_specs=[hs, zs, _full((1, DN_DIM))], out_specs=hs,
        out_shape=jax.ShapeDtypeStruct((S, DN_WIDTH), BF16), name="dn_out_fwd",
        compiler_params=_params(("parallel", "parallel")))(o, proj, gain)


_ANY = pl.BlockSpec(memory_space=pl.ANY)


def _dn_out_bwd(o, proj, gain, dy, dproj, tm=512):
    S = o.shape[0]

    def body(o_ref, z_ref, g_ref, dy_ref, _, do_ref, dz_ref, dg_ref):
        _, vjp = jax.vjp(_dn_out, o_ref[...], z_ref[...], g_ref[...])
        do, dz, dg = vjp(dy_ref[...])
        do_ref[...] = do
        dz_ref[...] = dz.astype(BF16)

        @pl.when((pl.program_id(0) == 0) & (pl.program_id(1) == 0))
        def _():
            dg_ref[...] = jnp.zeros_like(dg_ref)

        dg_ref[...] += dg

    hs = pl.BlockSpec((tm, LANES), lambda i, h: (i, h))
    zs = pl.BlockSpec((tm, LANES), lambda i, h: (i, P_Z // LANES + h))
    return pl.pallas_call(
        body, grid=(S // tm, DN_HEADS), in_specs=[hs, zs, _full((1, DN_DIM)), hs, _ANY],
        out_specs=[hs, zs, _full((1, DN_DIM))],
        out_shape=[jax.ShapeDtypeStruct((S, DN_WIDTH), F32), jax.ShapeDtypeStruct(dproj.shape, dproj.dtype),
                   jax.ShapeDtypeStruct((1, DN_DIM), F32)],
        input_output_aliases={4: 1},
        name="dn_out_bwd", compiler_params=_params(("arbitrary", "arbitrary")))(o, proj, gain, dy, dproj)


def _rel_buckets():
    qi = np.arange(BLOCK)[:, None]
    kj = np.arange(2 * BLOCK)[None, :]
    n = np.maximum(BLOCK + qi - kj, 0)
    max_exact = REL_BUCKETS // 2
    nf = np.maximum(n, 1).astype(np.float32)
    large = max_exact + (np.log(nf / np.float32(max_exact)) / np.float32(math.log(REL_MAX_DIST / max_exact))
                         * np.float32(REL_BUCKETS - max_exact)).astype(np.int32)
    large = np.minimum(large, REL_BUCKETS - 1)
    return np.where(n < max_exact, n, large).astype(np.int32)


def _bias_fwd(rel_bias):
    buckets = jnp.asarray(_rel_buckets())

    def body(rb_ref, bk_ref, o_ref):
        bk = bk_ref[...]
        for h in range(SWA_HEADS):
            acc = jnp.zeros((BLOCK, 2 * BLOCK), F32)
            for b in range(REL_BUCKETS):
                acc = jnp.where(bk == b, rb_ref[b, h], acc)
            o_ref[h] = acc

    return pl.pallas_call(
        body, in_specs=[pl.BlockSpec(memory_space=pltpu.SMEM), pl.BlockSpec(memory_space=pltpu.VMEM)],
        out_specs=pl.BlockSpec(memory_space=pltpu.VMEM),
        out_shape=jax.ShapeDtypeStruct((SWA_HEADS, BLOCK, 2 * BLOCK), F32), name="swa_bias_fwd",
        compiler_params=_params())(rel_bias, buckets)


def _bias_bwd(dbias):
    buckets = jnp.asarray(_rel_buckets())

    def body(d_ref, bk_ref, o_ref):
        bk = bk_ref[...]
        lane = lax.broadcasted_iota(jnp.int32, (1, LANES), 1)
        for h in range(SWA_HEADS):
            d = d_ref[h]
            row = jnp.zeros((1, LANES), F32)
            for b in range(REL_BUCKETS):
                part = jnp.sum(jnp.where(bk == b, d, 0.0), axis=1, keepdims=True)
                row = jnp.where(lane == b, jnp.sum(part, axis=0, keepdims=True), row)
            o_ref[h:h + 1, :] = row

    return pl.pallas_call(
        body, in_specs=[pl.BlockSpec(memory_space=pltpu.VMEM), pl.BlockSpec(memory_space=pltpu.VMEM)],
        out_specs=pl.BlockSpec(memory_space=pltpu.VMEM),
        out_shape=jax.ShapeDtypeStruct((SWA_HEADS, LANES), F32), name="swa_bias_bwd",
        compiler_params=_params())(dbias, buckets)


def _swa_mask(n):
    qi = lax.broadcasted_iota(jnp.int32, (BLOCK, 2 * BLOCK), 0)
    kj = lax.broadcasted_iota(jnp.int32, (BLOCK, 2 * BLOCK), 1)
    dist = BLOCK + qi - kj
    return (dist >= 0) & (dist < WINDOW) & ((n > 0) | (kj >= BLOCK))


def _swa_in_specs():
    q = pl.BlockSpec((BLOCK, SWA_WIDTH), lambda n: (n, P_SQ // SWA_WIDTH))
    kc = pl.BlockSpec((BLOCK, SWA_KVW), lambda n: (n, P_SK // SWA_KVW))
    kp = pl.BlockSpec((BLOCK, SWA_KVW), lambda n: (jnp.maximum(n - 1, 0), P_SK // SWA_KVW))
    vc = pl.BlockSpec((BLOCK, SWA_KVW), lambda n: (n, P_SV // SWA_KVW))
    vp = pl.BlockSpec((BLOCK, SWA_KVW), lambda n: (jnp.maximum(n - 1, 0), P_SV // SWA_KVW))
    small = [_full((1, SWA_DIM)), _full((1, SWA_DIM)), _full((1, SWA_HEADS)),
             _full((SWA_HEADS, BLOCK, 2 * BLOCK))]
    return [q, kp, kc, vp, vc] + small


def _swa_load(q_ref, kp_ref, kc_ref, vp_ref, vc_ref, s_ref):
    q = jnp.stack([q_ref[:, pl.ds(h * SWA_DIM, SWA_DIM)] for h in range(SWA_HEADS)])
    kbands, vbands = [], []
    for kv in range(SWA_KV):
        cols = pl.ds(kv * SWA_DIM, SWA_DIM)
        kbands += [jnp.concatenate([kp_ref[:, cols], kc_ref[:, cols]], axis=0)] * SWA_GROUP
        vbands += [jnp.concatenate([vp_ref[:, cols], vc_ref[:, cols]], axis=0)] * SWA_GROUP
    sinks = jnp.stack([s_ref[:, pl.ds(h, 1)] for h in range(SWA_HEADS)])
    return q, jnp.stack(kbands), jnp.stack(vbands), sinks


def _swa_fwd(proj, q_gain, k_gain, sinks, bias):
    S = proj.shape[0]

    def body(q_ref, kp_ref, kc_ref, vp_ref, vc_ref, qg_ref, kg_ref, s_ref, bias_ref, y_ref):
        mask = _swa_mask(pl.program_id(0))
        q, kband, vband, sk = _swa_load(q_ref, kp_ref, kc_ref, vp_ref, vc_ref, s_ref)
        out = _swa_block(q, kband, vband, qg_ref[...], kg_ref[...], sk, bias_ref[...], mask)
        for h in range(SWA_HEADS):
            y_ref[:, pl.ds(h * SWA_DIM, SWA_DIM)] = out[h].astype(BF16)

    return pl.pallas_call(
        body, grid=(S // BLOCK,), in_specs=_swa_in_specs(),
        out_specs=pl.BlockSpec((BLOCK, SWA_WIDTH), lambda n: (n, 0)),
        out_shape=jax.ShapeDtypeStruct((S, SWA_WIDTH), BF16), name="swa_fwd",
        compiler_params=_params(("parallel",)))(proj, proj, proj, proj, proj, q_gain, k_gain, sinks, bias)


def _swa_bwd(proj, q_gain, k_gain, sinks, bias, dy, dproj):
    S = proj.shape[0]

    def body(q_ref, kp_ref, kc_ref, vp_ref, vc_ref, qg_ref, kg_ref, s_ref, bias_ref, dy_ref, _,
             dq_ref, dk_ref, dv_ref, dqg_ref, dkg_ref, ds_ref, dbias_ref):
        n = pl.program_id(0)
        mask = _swa_mask(n)

        @pl.when(n == 0)
        def _():
            for r in (dk_ref, dv_ref, dqg_ref, dkg_ref, ds_ref, dbias_ref):
                r[...] = jnp.zeros_like(r)

        cur = pl.ds(pl.multiple_of(n * BLOCK, BLOCK), BLOCK)
        prev = pl.ds(pl.multiple_of(jnp.maximum(n - 1, 0) * BLOCK, BLOCK), BLOCK)
        q, kband, vband, sk = _swa_load(q_ref, kp_ref, kc_ref, vp_ref, vc_ref, s_ref)
        _, vjp = jax.vjp(lambda q, kb, vb, qg, kg, sk, bs: _swa_block(q, kb, vb, qg, kg, sk, bs, mask),
                         q, kband, vband, qg_ref[...], kg_ref[...], sk, bias_ref[...])
        dy = jnp.stack([dy_ref[:, pl.ds(h * SWA_DIM, SWA_DIM)] for h in range(SWA_HEADS)])
        dq, dkb, dvb, dqg, dkg, dsk, dbs = vjp(dy)
        for h in range(SWA_HEADS):
            dq_ref[:, pl.ds(h * SWA_DIM, SWA_DIM)] = dq[h].astype(BF16)
            ds_ref[:, pl.ds(h, 1)] += dsk[h]
        dbias_ref[...] += dbs
        dqg_ref[...] += dqg
        dkg_ref[...] += dkg
        for kv in range(SWA_KV):
            cols = pl.ds(kv * SWA_DIM, SWA_DIM)
            group = range(kv * SWA_GROUP, (kv + 1) * SWA_GROUP)
            dk_kv = sum(dkb[h] for h in group)
            dv_kv = sum(dvb[h] for h in group)
            dk_ref[cur, cols] += dk_kv[BLOCK:]
            dv_ref[cur, cols] += dv_kv[BLOCK:]

            @pl.when(n > 0)
            def _(cols=cols, dk_kv=dk_kv, dv_kv=dv_kv):
                dk_ref[prev, cols] += dk_kv[:BLOCK]
                dv_ref[prev, cols] += dv_kv[:BLOCK]

    return pl.pallas_call(
        body, grid=(S // BLOCK,),
        in_specs=_swa_in_specs() + [pl.BlockSpec((BLOCK, SWA_WIDTH), lambda n: (n, 0)),
                                    pl.BlockSpec(memory_space=pl.ANY)],
        out_specs=[pl.BlockSpec((BLOCK, SWA_WIDTH), lambda n: (n, P_SQ // SWA_WIDTH)), _full((S, SWA_KVW)),
                   _full((S, SWA_KVW)), _full((1, SWA_DIM)), _full((1, SWA_DIM)), _full((1, SWA_HEADS)),
                   _full((SWA_HEADS, BLOCK, 2 * BLOCK))],
        out_shape=[jax.ShapeDtypeStruct(dproj.shape, dproj.dtype), jax.ShapeDtypeStruct((S, SWA_KVW), F32),
                   jax.ShapeDtypeStruct((S, SWA_KVW), F32), jax.ShapeDtypeStruct((1, SWA_DIM), F32),
                   jax.ShapeDtypeStruct((1, SWA_DIM), F32), jax.ShapeDtypeStruct((1, SWA_HEADS), F32),
                   jax.ShapeDtypeStruct((SWA_HEADS, BLOCK, 2 * BLOCK), F32)],
        input_output_aliases={10: 0},
        name="swa_bwd", compiler_params=_params(("arbitrary",)),
    )(proj, proj, proj, proj, proj, q_gain, k_gain, sinks, bias, dy, dproj)


def _kv_into(dproj, dk, dv, tm=512):
    S = dk.shape[0]

    def body(dk_ref, dv_ref, _, o_ref):
        o_ref[:, :SWA_KVW] = dk_ref[...].astype(BF16)
        o_ref[:, SWA_KVW:] = dv_ref[...].astype(BF16)

    return pl.pallas_call(
        body, grid=(S // tm,), in_specs=[_row(tm, SWA_KVW), _row(tm, SWA_KVW), pl.BlockSpec(memory_space=pl.ANY)],
        out_specs=_row(tm, 2 * SWA_KVW, P_SK // (2 * SWA_KVW)),
        out_shape=jax.ShapeDtypeStruct(dproj.shape, dproj.dtype), input_output_aliases={2: 0},
        name="swa_kv_into", compiler_params=_params(("parallel",)))(dk, dv, dproj)


def _position():
    return lax.axis_index("x"), lax.axis_index("y"), lax.axis_index("c")


def _all_gather(shards, name="all_gather_weights"):
    na = len(shards)

    def body(*refs):
        x_refs, out_refs = refs[:na], refs[na:2 * na]
        send_sems, recv_sems, local_sems = refs[2 * na:]
        x, y, c = _position()
        me, sibling = (x, y, c), (x, y, 1 - c)
        chips = [(1 - x, y), (x, 1 - y), (1 - x, 1 - y)]

        def copy(a, k, block, to, own=False):
            px, py, pc = block
            slot = out_refs[a].at[4 * px + 2 * py + pc]
            return pltpu.make_async_remote_copy(
                src_ref=x_refs[a] if own else slot, dst_ref=slot, send_sem=send_sems.at[7 * a + k],
                recv_sem=recv_sems.at[7 * a + k], device_id=to, device_id_type=MESH_ID)

        mine = [pltpu.make_async_copy(x_refs[a], out_refs[a].at[4 * x + 2 * y + c], local_sems.at[a])
                for a in range(na)]
        for cp in mine:
            cp.start()
        first = []
        for a in range(na):
            first.append(copy(a, 0, me, sibling, own=True))
            first += [copy(a, 1 + j, me, (*chip, c), own=True) for j, chip in enumerate(chips)]
        for cp in first:
            cp.start()
        passed = []
        for j, chip in enumerate(chips):
            for a in range(na):
                copy(a, 1 + j, (*chip, c), me).wait_recv()
                passed.append(copy(a, 4 + j, (*chip, c), sibling))
                passed[-1].start()
        for a in range(na):
            copy(a, 0, sibling, me).wait_recv()
            for j, chip in enumerate(chips):
                copy(a, 4 + j, (*chip, 1 - c), me).wait_recv()
        for cp in first + passed:
            cp.wait_send()
        for cp in mine:
            cp.wait()

    return pl.pallas_call(
        body, in_specs=[pl.BlockSpec(memory_space=pl.ANY)] * na, out_specs=[pl.BlockSpec(memory_space=pl.ANY)] * na,
        out_shape=[jax.ShapeDtypeStruct((N_DEV,) + s.shape, s.dtype) for s in shards],
        scratch_shapes=[pltpu.SemaphoreType.DMA((7 * na,)), pltpu.SemaphoreType.DMA((7 * na,)),
                        pltpu.SemaphoreType.DMA((na,))],
        name=name)(*shards)


_HBM = pl.BlockSpec(memory_space=pltpu.HBM)
_SEM = pl.BlockSpec(memory_space=pltpu.SEMAPHORE)
_DATAFLOW = pltpu.SideEffectType.DATAFLOW_SIDE_EFFECTING


def _peers(x, y, c):
    out = []
    for k in range(1, N_DEV):
        px, py, pc = x ^ (k >> 2), y ^ ((k >> 1) & 1), c ^ (k & 1)
        out.append(((px, py, pc), 4 * px + 2 * py + pc))
    return out


def _split_copies(src_refs, land_refs, send_sems, recv_sems, scatter):
    x, y, c = _position()
    me = 4 * x + 2 * y + c
    sends, recvs = [], []
    for k, (peer_id, peer) in enumerate(_peers(x, y, c)):
        for a, (src, land) in enumerate(zip(src_refs, land_refs)):
            sems = dict(send_sem=send_sems.at[7 * a + k], recv_sem=recv_sems.at[7 * a + k],
                        device_id=peer_id, device_id_type=MESH_ID)
            mine = src.at[peer] if scatter else src
            sends.append(pltpu.make_async_remote_copy(src_ref=mine, dst_ref=land.at[me], **sems))
            recvs.append(pltpu.make_async_remote_copy(src_ref=mine, dst_ref=land.at[peer], **sems))
    return sends, recvs


def _all_gather_direct(shards, name, after):
    na = len(shards)

    def body(*refs):
        x_refs, out_refs = refs[:na], refs[na + 1:2 * na + 1]
        send_sems, recv_sems, local_sems = refs[2 * na + 1:]
        x, y, c = _position()
        me = 4 * x + 2 * y + c
        local = [pltpu.make_async_copy(x_refs[a], out_refs[a].at[me], local_sems.at[a]) for a in range(na)]
        sends, recvs = _split_copies(x_refs, out_refs, send_sems, recv_sems, False)
        for cp in local + sends:
            cp.start()
        for cp in recvs:
            cp.wait_recv()
        for cp in sends:
            cp.wait_send()
        for cp in local:
            cp.wait()

    return pl.pallas_call(
        body, in_specs=[pl.BlockSpec(memory_space=pl.ANY)] * (na + 1),
        out_specs=[pl.BlockSpec(memory_space=pl.ANY)] * na,
        out_shape=[jax.ShapeDtypeStruct((N_DEV,) + s.shape, s.dtype) for s in shards],
        scratch_shapes=[pltpu.SemaphoreType.DMA((7 * na,)), pltpu.SemaphoreType.DMA((7 * na,)),
                        pltpu.SemaphoreType.DMA((na,))],
        name=name)(*shards, after)


def _exchange_start(srcs, scatter, name, after=None):
    na = len(srcs)
    lands = [lax.empty(s.shape if scatter else (N_DEV,) + s.shape, s.dtype) for s in srcs]
    extra = [] if after is None else [after]

    def body(*refs):
        src_refs, land_refs = refs[:na], refs[na:2 * na]
        send_sems, recv_sems = refs[2 * na + len(extra)], refs[2 * na + len(extra) + 1]
        token = refs[-1]
        sends, _ = _split_copies(src_refs, land_refs, send_sems, recv_sems, scatter)
        for cp in sends:
            cp.start()
        token[...] = jnp.zeros_like(token)

    hbm = lambda a: pltpu.HBM(a.shape, a.dtype)
    out = pl.pallas_call(
        body, name=name,
        out_shape=(pltpu.SemaphoreType.DMA((7 * na,)), pltpu.SemaphoreType.DMA((7 * na,)),
                   *[hbm(s) for s in srcs], *[hbm(l) for l in lands], jax.ShapeDtypeStruct((8, LANES), F32)),
        in_specs=[_HBM] * (2 * na) + [pl.BlockSpec(memory_space=pl.ANY)] * len(extra),
        out_specs=(_SEM, _SEM, *[_HBM] * (2 * na), pl.BlockSpec(memory_space=pltpu.VMEM)),
        input_output_aliases={i: 2 + i for i in range(2 * na)},
        compiler_params=pltpu.CompilerParams(has_side_effects=_DATAFLOW),
    )(*[pltpu.with_memory_space_constraint(s, pltpu.HBM) for s in srcs],
      *[pltpu.with_memory_space_constraint(l, pltpu.HBM) for l in lands], *extra)
    return (out[0], out[1], list(out[2:2 + na]), list(out[2 + na:2 + 2 * na])), out[-1]


def _exchange_wait(handle, after, scatter, name):
    send_sems, recv_sems, srcs, lands = handle
    na = len(srcs)

    def body(*refs):
        src_refs, land_refs = refs[:na], refs[na:2 * na]
        s_sems, r_sems = refs[2 * na], refs[2 * na + 1]
        sends, recvs = _split_copies(src_refs, land_refs, s_sems, r_sems, scatter)
        for cp in sends:
            cp.wait_send()
        for cp in recvs:
            cp.wait_recv()

    hbm = lambda a: pltpu.HBM(a.shape, a.dtype)
    out = pl.pallas_call(
        body, name=name, out_shape=(*[hbm(s) for s in srcs], *[hbm(l) for l in lands]),
        in_specs=[_HBM] * (2 * na) + [_SEM, _SEM, pl.BlockSpec(memory_space=pl.ANY)],
        out_specs=tuple([_HBM] * (2 * na)), input_output_aliases={i: i for i in range(2 * na)},
        compiler_params=pltpu.CompilerParams(has_side_effects=_DATAFLOW),
    )(*srcs, *lands, send_sems, recv_sems, after)
    return list(out[:na]), list(out[na:])


def _own_slot(landed, own):
    me = 4 * lax.axis_index("x") + 2 * lax.axis_index("y") + lax.axis_index("c")
    return lax.dynamic_update_slice_in_dim(landed, own[None], me, axis=0)


def _adam_update(parts, w, m, v, name, tr=256):
    _, r, c = w.shape
    tr = _pick_rows(r, tr)
    cp = parts.shape[2]

    def body(p_ref, w_ref, m_ref, v_ref, g_ref, d_ref, nm_ref, nv_ref):
        g = p_ref[0, :, pl.ds(0, c)].astype(F32)
        for i in range(1, N_DEV):
            g = g + p_ref[i, :, pl.ds(0, c)].astype(F32)
        delta, nm, nv = _adamw(w_ref[0], g, m_ref[0], v_ref[0])
        g_ref[0] = g
        d_ref[0] = delta
        nm_ref[0] = nm
        nv_ref[0] = nv

    rs = pl.BlockSpec((1, tr, c), lambda i: (0, i, 0))
    return pl.pallas_call(
        body, grid=(r // tr,), in_specs=[pl.BlockSpec((N_DEV, tr, cp), lambda i: (0, i, 0)), rs, rs, rs],
        out_specs=[rs] * 4, out_shape=[jax.ShapeDtypeStruct((1, r, c), F32)] * 4, name=name,
        compiler_params=_params(("parallel",)))(parts, w, m, v)


def _pick_rows(rows, target):
    if rows <= target:
        return rows
    t = target
    while t >= 16:
        if rows % t == 0:
            return t
        t -= 16
    return rows


BIG = ("w_in", "w_branch_dn", "w_branch_swa", "w_out", "w_gate", "w_up", "w_down")
IN_SHARD, IN_WIRE = D_IN // N_DEV, 640
FF_SHARD, FF_WIRE = D_FF // N_DEV, 384
D_FFP = N_DEV * FF_WIRE
BIG_SHAPES = {"w_in": ((D_MODEL, IN_SHARD), (D_MODEL, IN_WIRE)),
              "w_branch_dn": ((DN_WIDTH, LANES), (DN_WIDTH, LANES)),
              "w_branch_swa": ((SWA_WIDTH, LANES), (SWA_WIDTH, LANES)),
              "w_out": ((LANES, D_MODEL), (LANES, D_MODEL)),
              "w_gate": ((D_MODEL, FF_SHARD), (D_MODEL, FF_WIRE)),
              "w_up": ((D_MODEL, FF_SHARD), (D_MODEL, FF_WIRE)),
              "w_down": ((FF_SHARD, D_MODEL), (FF_WIRE, D_MODEL))}
CONV_SHARD, CONV_WIRE = (DN_CONV, DN_QKV // N_DEV), (8, 256)


def _pad_to(a, shape):
    return jnp.pad(a, [(0, t - s) for s, t in zip(a.shape, shape)])


_IN_SEGS = ((R_GATE, 2048, P_GATE), (R_QKV, DN_QKV, P_QKV), (R_Z, DN_WIDTH, P_Z), (R_SQ, SWA_WIDTH, P_SQ),
            (R_SK, SWA_KVW, P_SK), (R_SV, SWA_KVW, P_SV), (R_B, 8, P_BA))


def _w_in_from_blocks(blocks):
    parts = []
    for rs, n, _ in _IN_SEGS:
        for dev in range(N_DEV):
            lo, hi = max(rs, IN_SHARD * dev), min(rs + n, IN_SHARD * (dev + 1))
            if lo < hi:
                parts.append(blocks[dev, :, lo - IN_SHARD * dev:hi - IN_SHARD * dev])
    parts.append(jnp.zeros((blocks.shape[1], P_WIDTH - P_BA - 8), blocks.dtype))
    return jnp.concatenate(parts, axis=1)


def _w_in_to_blocks(g):
    out = []
    for dev in range(N_DEV):
        parts = []
        for rs, n, ps in sorted(_IN_SEGS):
            lo, hi = max(rs, IN_SHARD * dev), min(rs + n, IN_SHARD * (dev + 1))
            if lo < hi:
                parts.append(g[:, ps + lo - rs:ps + hi - rs])
        parts.append(jnp.zeros((g.shape[0], IN_WIRE - IN_SHARD), g.dtype))
        out.append(jnp.concatenate(parts, axis=1))
    return jnp.stack(out)


SMALL = {"attn_norm": (0, (1, D_MODEL)), "ffn_norm": (1, (1, D_MODEL)), "dn_out_norm": (2, (1, DN_DIM)),
         "swa_q_norm": (3, (1, SWA_DIM)), "swa_k_norm": (4, (1, SWA_DIM)), "dn_a_log": (5, (1, DN_HEADS)),
         "dn_dt_bias": (6, (1, DN_HEADS)), "swa_sinks": (7, (1, SWA_HEADS)), "rel_bias": (8, (REL_BUCKETS, SWA_HEADS))}
SMALL_SHEET = (48, D_MODEL)


def _small_pack(grads):
    names = list(SMALL)

    def body(*refs):
        o_ref = refs[-1]
        o_ref[...] = jnp.zeros_like(o_ref)
        for n, ref in zip(names, refs):
            r0, (nr, nc) = SMALL[n]
            o_ref[r0:r0 + nr, 0:nc] = ref[...]

    return pl.pallas_call(
        body, in_specs=[pl.BlockSpec(memory_space=pltpu.VMEM)] * len(names),
        out_specs=pl.BlockSpec(memory_space=pltpu.VMEM), out_shape=jax.ShapeDtypeStruct(SMALL_SHEET, F32),
        name="small_pack", compiler_params=_params())(*[grads[n].reshape(SMALL[n][1]) for n in names])


def _small_update(sheets, w, m, v):
    names = list(SMALL)
    k = len(names)

    def body(*refs):
        p_ref = refs[0]
        ins, outs = refs[1:1 + 3 * k], refs[1 + 3 * k:]
        for t, n in enumerate(names):
            r0, (nr, nc) = SMALL[n]
            g = p_ref[0, r0:r0 + nr, 0:nc]
            for i in range(1, N_DEV):
                g = g + p_ref[i, r0:r0 + nr, 0:nc]
            delta, nm, nv = _adamw(ins[t][...], g, ins[k + t][...], ins[2 * k + t][...])
            for kind, val in enumerate((g, delta, nm, nv)):
                outs[kind * k + t][...] = val

    shapes = [jax.ShapeDtypeStruct(SMALL[n][1], F32) for n in names]
    vm = pl.BlockSpec(memory_space=pltpu.VMEM)
    res = pl.pallas_call(
        body, in_specs=[vm] * (1 + 3 * k), out_specs=[vm] * (4 * k), out_shape=shapes * 4, name="adam_small",
        compiler_params=_params(),
    )(sheets, *[d[n].reshape(SMALL[n][1]) for d in (w, m, v) for n in names])
    return {n: tuple(res[kind * k + t] for kind in range(4)) for t, n in enumerate(names)}


def kernel(x, attn_norm, w_in, dn_conv, dn_a_log, dn_dt_bias, dn_out_norm, swa_q_norm, swa_k_norm, swa_sinks, rel_bias, w_branch_dn, w_branch_swa, w_out, ffn_norm, w_gate, w_up, w_down, loss_target, m_attn_norm, m_w_in, m_dn_conv, m_dn_a_log, m_dn_dt_bias, m_dn_out_norm, m_swa_q_norm, m_swa_k_norm, m_swa_sinks, m_rel_bias, m_w_branch_dn, m_w_branch_swa, m_w_out, m_ffn_norm, m_w_gate, m_w_up, m_w_down, v_attn_norm, v_w_in, v_dn_conv, v_dn_a_log, v_dn_dt_bias, v_dn_out_norm, v_swa_q_norm, v_swa_k_norm, v_swa_sinks, v_rel_bias, v_w_branch_dn, v_w_branch_swa, v_w_out, v_ffn_norm, v_w_gate, v_w_up, v_w_down):
    args = dict(locals())
    S = x.shape[1]
    xs = x.reshape(S, D_MODEL)
    target = loss_target.reshape(S, D_MODEL)

    w_loc = {n: args[n].reshape(BIG_SHAPES[n][0]) for n in BIG}
    conv_loc = dn_conv.reshape(CONV_SHARD)
    wire = {n: _pad_to(w_loc[n], BIG_SHAPES[n][1]).astype(BF16) for n in BIG}
    first = _all_gather([wire["w_in"], _pad_to(conv_loc, CONV_WIRE)])
    later = [n for n in BIG if n != "w_in"]
    rest_handle, rest_token = _exchange_start([wire[n] for n in later], False, "gather_rest_start", after=first[1])
    w_pad = _w_in_from_blocks(first[0])
    conv_w = jnp.concatenate([first[1][d, :DN_CONV, :CONV_SHARD[1]] for d in range(N_DEV)], axis=1)

    h = _norm_fwd(xs, attn_norm + rest_token[0, 0], "norm1_fwd")
    proj = _mm([(h, w_pad)], "nn", F32, "mm_in", 512, 1664, j_outer=True)
    qkvn = _dn_conv_fwd(proj, conv_w)
    beta, g = _dn_gate_fwd(proj, dn_a_log, dn_dt_bias)
    u, w, qe, kd, qk, egl, tinv = _dn_prep_fwd(qkvn, g, beta)
    o, states = _dn_scan_fwd(u, w, qe, kd, qk, egl)
    y_dn = _dn_out_fwd(o, proj, dn_out_norm)
    bias = _bias_fwd(rel_bias)
    y_swa = _swa_fwd(proj, swa_q_norm, swa_k_norm, swa_sinks, bias)
    rest_src, rest_land = _exchange_wait(rest_handle, y_swa, False, "gather_rest_wait")
    G = {n: _own_slot(land, src) for n, src, land in zip(later, rest_src, rest_land)}
    w_bdn, w_bswa, w_g, w_u = G["w_branch_dn"], G["w_branch_swa"], G["w_gate"], G["w_up"]
    w_o = G["w_out"].reshape(D_MODEL, D_MODEL)
    w_d = G["w_down"].reshape(D_FFP, D_MODEL)
    gates = [(proj, P_GATE // 512), (proj, (P_GATE + D_MODEL) // 512)]
    a_dn, a_swa, merged = _mm_fused(
        [(y_dn, w_bdn), (y_swa, w_bswa)], "nn", "mm_branch_merge", 1024, 512,
        lambda p, e: (p[0], p[1], _merge(e[0], e[1], p[0], p[1])), gates, (F32, F32, BF16), b_blocks=True)

    def resid_norm(p, e):
        x1 = e[0] + p[0]
        return x1, _rms(x1, e[1])

    x1, h2 = _mm_fused([(merged, w_o)], "nn", "mm_out_norm", 512, D_MODEL, resid_norm,
                       [(xs, 0), (ffn_norm, None)], (F32, BF16))
    gate, up, act = _mm_fused([(h2, w_g), (h2, w_u)], "nn", "mm_gate_up_act", 1024, 768,
                              lambda p, e: (p[0], p[1], _act(p[0], p[1])), [], (F32, F32, BF16),
                              j_outer=True, b_blocks=True)

    def loss_head(p, e):
        diff = e[0] + p[0] - e[1]
        dy = diff * (1.0 / D_MODEL)
        part = jnp.sum(jnp.mean(diff * diff, axis=-1, keepdims=True), axis=0, keepdims=True) * 0.5
        return dy, dy, part

    dy, dy_b, loss_local = _mm_fused([(act, w_d)], "nn", "mm_down_loss", 512, D_MODEL, loss_head,
                                     [(x1, 0), (target, 0)], (F32, BF16), sum_shape=(1, 1))

    def act_bwd(p, e):
        _, vjp = jax.vjp(_act, e[0], e[1])
        return vjp(p[0])

    dgate, dup = _mm_fused([(dy_b, w_d)], "nt", "mm_dact_act", 1024, 768, act_bwd, [(gate, 0), (up, 0)],
                           (BF16, BF16), j_outer=True)
    g_w_down = _mm([(act, dy_b)], "tn", BF16, "mm_dw_down", 768, D_MODEL, j_outer=True)
    g_w_down = g_w_down.reshape(N_DEV, FF_WIRE, D_MODEL)
    g_w_gate = _mm([(h2, dgate)], "tn", BF16, "mm_dw_gate", D_MODEL, 768, out_blocks=True)
    g_w_up = _mm([(h2, dup)], "tn", BF16, "mm_dw_up", D_MODEL, 768, out_blocks=True)
    ffn_handle, ffn_token = _exchange_start([g_w_down, g_w_gate, g_w_up], True, "scatter_ffn_start")

    def norm_bwd(p, e):
        _, vjp = jax.vjp(_rms, e[0], e[2])
        dx, dgain = vjp(sum(p))
        dx = dx + e[1]
        return dx, dx, dgain

    dx1, dx1_b, g_ffn_norm = _mm_fused(
        [(dgate, w_g), (dup, w_u)], "nt", "mm_dh2_norm", 256, D_MODEL, norm_bwd,
        [(x1, 0), (dy, 0), (ffn_norm + ffn_token[0, 0], None)], (F32, BF16), b_blocks=True, sum_shape=(1, D_MODEL))
    def merge_bwd(p, e):
        _, vjp = jax.vjp(_merge, *e)
        dg0, dg1, da_dn, da_swa = vjp(p[0])
        return jnp.concatenate([dg0, dg1], axis=1), da_dn, da_swa

    dproj, da_dn, da_swa = _mm_fused(
        [(dx1_b, w_o)], "nt", "mm_dmerged_merge", 512, D_MODEL, merge_bwd,
        [(proj, P_GATE // D_MODEL), (proj, P_GATE // D_MODEL + 1), (a_dn, 0), (a_swa, 0)], (BF16,) * 3,
        wide_first=(P_WIDTH, 2 * D_MODEL))
    g_w_out = _mm([(merged, dx1_b)], "tn", BF16, "mm_dw_out", 512, D_MODEL, j_outer=True)
    g_w_out = g_w_out.reshape(N_DEV, LANES, D_MODEL)
    dy_dn = _mm([(da_dn, w_bdn)], "nt", F32, "mm_dy_dn", 1024, DN_WIDTH, b_blocks=True)
    dy_swa = _mm([(da_swa, w_bswa)], "nt", F32, "mm_dy_swa", 1024, SWA_WIDTH, b_blocks=True)
    g_w_bdn = _mm([(y_dn, da_dn)], "tn", BF16, "mm_dw_branch_dn", DN_WIDTH, 512, out_blocks=True)
    g_w_bswa = _mm([(y_swa, da_swa)], "tn", BF16, "mm_dw_branch_swa", SWA_WIDTH, 512, out_blocks=True)
    dproj, dsk, dsv, g_q_norm, g_k_norm, g_sinks, dbias = _swa_bwd(proj, swa_q_norm, swa_k_norm, swa_sinks, bias,
                                                                   dy_swa, dproj)
    dproj = _kv_into(dproj, dsk, dsv)
    g_rel_bias = _bias_bwd(dbias)[:, :REL_BUCKETS].T
    mix_handle, mix_token = _exchange_start([g_w_out, g_w_bdn, g_w_bswa], True, "scatter_mix_start")
    do, dproj, g_out_norm = _dn_out_bwd(o, proj, dn_out_norm + mix_token[0, 0], dy_dn, dproj)
    du, dw, dqe, dkd, dqk, degl = _dn_scan_bwd(u, w, qe, kd, qk, egl, states, do)
    dqkvn, dgd, dbeta = _dn_prep_bwd(qkvn, g, beta, tinv, du, dw, dqe, dkd, dqk, degl)
    dproj, dal, ddt = _dn_gate_bwd(proj, dn_a_log, dn_dt_bias, dbeta, dgd, dproj)
    g_a_log = dal.reshape(DN_HEADS, DN_DIM).sum(axis=1)
    g_dt_bias = ddt.reshape(DN_HEADS, DN_DIM).sum(axis=1)
    dproj, g_conv = _dn_conv_bwd(proj, conv_w, dqkvn, dproj)
    g_w_in = _w_in_to_blocks(_mm([(h, dproj)], "tn", BF16, "mm_dw_in", 512, 1664, j_outer=True))
    in_handle, in_token = _exchange_start([g_w_in], True, "scatter_in_start")
    dx, g_attn_norm = _mm_fused(
        [(dproj, w_pad)], "nt", "mm_dh_norm", 256, D_MODEL, lambda p, e: norm_bwd(p, e)[1:],
        [(xs, 0), (dx1, 0), (attn_norm + in_token[0, 0], None)], (F32,), sum_shape=(1, D_MODEL))

    g_small = {"attn_norm": g_attn_norm, "ffn_norm": g_ffn_norm, "rel_bias": g_rel_bias, "dn_out_norm": g_out_norm,
               "swa_q_norm": g_q_norm, "swa_k_norm": g_k_norm, "dn_a_log": g_a_log, "dn_dt_bias": g_dt_bias,
               "swa_sinks": g_sinks}
    me = 4 * lax.axis_index("x") + 2 * lax.axis_index("y") + lax.axis_index("c")
    outs = {}

    def finish(handle, group, name, after):
        srcs, lands = _exchange_wait(handle, after, True, name)
        for n, src, land in zip(group, srcs, lands):
            parts = _own_slot(land, lax.dynamic_index_in_dim(src, me, 0, keepdims=False))
            outs[n] = _adam_update(parts, args[n], args["m_" + n], args["v_" + n], "adam_" + n)

    finish(ffn_handle, ("w_down", "w_gate", "w_up"), "scatter_ffn_wait", dx)
    finish(mix_handle, ("w_out", "w_branch_dn", "w_branch_swa"), "scatter_mix_wait", dx)
    sheets, conv_all = _all_gather_direct([_small_pack(g_small), _pad_to(g_conv, (8, DN_QKV))], "all_gather_small",
                                          after=outs["w_up"][0])
    finish(in_handle, ("w_in",), "scatter_in_wait", sheets)
    conv_parts = lax.dynamic_slice(conv_all, (0, 0, me * CONV_SHARD[1]), (N_DEV,) + CONV_SHARD)
    outs["dn_conv"] = _adam_update(conv_parts, dn_conv, m_dn_conv, v_dn_conv, "adam_dn_conv")
    outs.update(_small_update(sheets, {n: args[n] for n in SMALL}, {n: args["m_" + n] for n in SMALL},
                              {n: args["v_" + n] for n in SMALL}))

    names = ("attn_norm", "w_in", "dn_conv", "dn_a_log", "dn_dt_bias", "dn_out_norm", "swa_q_norm", "swa_k_norm",
             "swa_sinks", "rel_bias", "w_branch_dn", "w_branch_swa", "w_out", "ffn_norm", "w_gate", "w_up", "w_down")
    results = []
    for kind in range(4):
        results += [outs[n][kind].reshape(args[n].shape) for n in names]

    loss = lax.psum(loss_local[0, 0], ("x", "y", "c"))
    return (loss, dx.reshape(x.shape), *results)
```

```python
import math

import numpy as np
import jax
import jax.numpy as jnp
from jax import lax
from jax.experimental import pallas as pl
from jax.experimental.pallas import tpu as pltpu

F32 = jnp.float32
BF16 = jnp.bfloat16
HI = lax.Precision.HIGHEST

D_MODEL = 1024
DN_HEADS = 4
DN_DIM = 128
DN_WIDTH = 512
DN_QKV = 1536
DN_CONV = 4
CHUNK = 64
SWA_HEADS = 8
SWA_KV = 2
SWA_GROUP = 4
SWA_DIM = 64
SWA_WIDTH = 512
SWA_KVW = 128
WINDOW = 128
BLOCK = 128
REL_BUCKETS = 32
REL_MAX_DIST = 128
D_FF = 2816
D_IN = 4872
EPS = 1e-6
N_DEV = 8

ADAM_LR = 0.001
ADAM_B1 = 0.9
ADAM_B2 = 0.999
ADAM_EPS = 1e-08
ADAM_WD = 0.01
ADAM_STEP = 10

P_GATE, P_QKV, P_Z, P_SQ, P_SK, P_SV, P_BA = 0, 2048, 3584, 4096, 4608, 4736, 4864
P_WIDTH = 4992
R_QKV, R_Z, R_B, R_A, R_SQ, R_SK, R_SV, R_GATE = 0, 1536, 2048, 2052, 2056, 2568, 2696, 2824

VMEM_LIMIT = 56 * 1024 * 1024
LANES = 128
MESH_ID = pl.DeviceIdType.MESH


def _params(sem=None):
    return pltpu.CompilerParams(dimension_semantics=sem, vmem_limit_bytes=VMEM_LIMIT)


def _pick(dim, target):
    if dim <= target:
        return dim
    t = target - target % LANES
    while t >= LANES:
        if dim % t == 0:
            return t
        t -= LANES
    return dim


_DIMS = {"nn": (((1,), (0,)), ((), ())), "nt": (((1,), (1,)), ((), ())), "tn": (((0,), (0,)), ((), ()))}


def _tile_product(a_ref, b_ref, mode, b_blocks):
    a = a_ref[...].astype(BF16)
    if not b_blocks:
        return lax.dot_general(a, b_ref[...].astype(BF16), _DIMS[mode], preferred_element_type=F32)
    nb, _, c = b_ref.shape
    if mode == "nt" and c >= 2 * LANES:
        return sum(lax.dot_general(a[:, d * c:(d + 1) * c], b_ref[d].astype(BF16), _DIMS[mode],
                                   preferred_element_type=F32) for d in range(nb))
    b = jnp.concatenate([b_ref[d] for d in range(nb)], axis=1)
    return lax.dot_general(a, b.astype(BF16), _DIMS[mode], preferred_element_type=F32)


def _mm(pairs, mode, out_dtype, name, bm, bn, j_outer=False, b_blocks=False, out_blocks=False):
    a0, b0 = pairs[0]
    cb = b0.shape[2] if b_blocks else None
    b_shape = (b0.shape[1], N_DEV * cb) if b_blocks else b0.shape
    if mode == "nn":
        (M, K), (K2, N) = a0.shape, b_shape
    elif mode == "nt":
        (M, K), (N, K2) = a0.shape, b_shape
    else:
        (K, M), (K2, N) = a0.shape, b_shape
    bm, bn = min(bm, M), min(bn, N)
    assert K == K2 and M % bm == 0 and N % bn == 0, (name, a0.shape, b0.shape, bm, bn)
    co = N // N_DEV
    assert not out_blocks or bn % co == 0
    dims = _DIMS[mode]
    n = len(pairs)

    def body(*refs):
        o_ref = refs[2 * n]
        acc = None
        for t in range(n):
            p = _tile_product(refs[2 * t], refs[2 * t + 1], mode, b_blocks)
            acc = p if acc is None else acc + p
        if out_blocks:
            for d in range(bn // co):
                o_ref[d] = acc[:, d * co:(d + 1) * co].astype(out_dtype)
        else:
            o_ref[...] = acc.astype(out_dtype)

    def ij(f):
        return (lambda j, i: f(i, j)) if j_outer else f

    a_spec = pl.BlockSpec((K, bm), ij(lambda i, j: (0, i))) if mode == "tn" else pl.BlockSpec((bm, K), ij(lambda i, j: (i, 0)))
    if b_blocks and mode == "nt":
        b_spec = pl.BlockSpec((N_DEV, bn, cb), ij(lambda i, j: (0, j, 0)))
    elif b_blocks:
        b_spec = pl.BlockSpec((bn // cb, K, cb), ij(lambda i, j: (j, 0, 0)))
    elif mode == "nt":
        b_spec = pl.BlockSpec((bn, K), ij(lambda i, j: (j, 0)))
    else:
        b_spec = pl.BlockSpec((K, bn), ij(lambda i, j: (0, j)))
    if out_blocks:
        out_spec = pl.BlockSpec((bn // co, bm, co), ij(lambda i, j: (j, i, 0)))
        out_shape = jax.ShapeDtypeStruct((N_DEV, M, co), out_dtype)
    else:
        out_spec = pl.BlockSpec((bm, bn), ij(lambda i, j: (i, j)))
        out_shape = jax.ShapeDtypeStruct((M, N), out_dtype)
    grid = (N // bn, M // bm) if j_outer else (M // bm, N // bn)
    return pl.pallas_call(
        body, grid=grid, in_specs=[a_spec, b_spec] * n, out_specs=out_spec, out_shape=out_shape, name=name,
        compiler_params=_params(("parallel", "parallel")),
    )(*[x for pair in pairs for x in pair])


def _mm_fused(pairs, mode, name, bm, bn, epilogue, extras, out_dtypes, j_outer=False, b_blocks=False,
              sum_shape=None, wide_first=None):
    a0, b0 = pairs[0]
    cb = b0.shape[2] if b_blocks else None
    b_shape = (b0.shape[1], N_DEV * cb) if b_blocks else b0.shape
    if mode == "nn":
        (M, K), (K2, N) = a0.shape, b_shape
    else:
        (M, K), (N, K2) = a0.shape, b_shape
    bm, bn = min(bm, M), min(bn, N)
    assert mode in ("nn", "nt") and K == K2 and M % bm == 0 and N % bn == 0, (name, a0.shape, b0.shape)
    dims = _DIMS[mode]
    n, ne, no = len(pairs), len(extras), len(out_dtypes)

    def body(*refs):
        prods = [_tile_product(refs[2 * t], refs[2 * t + 1], mode, b_blocks) for t in range(n)]
        results = epilogue(prods, [r[...] for r in refs[2 * n:2 * n + ne]])
        out_refs = refs[2 * n + ne:]
        for o_ref, val, dt in zip(out_refs, results, out_dtypes):
            o_ref[...] = val.astype(dt)
        if sum_shape is not None:
            s_ref = out_refs[no]

            @pl.when((pl.program_id(0) == 0) & (pl.program_id(1) == 0))
            def _():
                s_ref[...] = jnp.zeros_like(s_ref)

            s_ref[...] += results[no]

    def ij(f):
        return (lambda j, i: f(i, j)) if j_outer else f

    a_spec = pl.BlockSpec((bm, K), ij(lambda i, j: (i, 0)))
    once = dict(pipeline_mode=pl.Buffered(1)) if bn == N else {}
    if b_blocks and mode == "nt":
        b_spec = pl.BlockSpec((N_DEV, bn, cb), ij(lambda i, j: (0, j, 0)), **once)
    elif b_blocks:
        b_spec = pl.BlockSpec((bn // cb, K, cb), ij(lambda i, j: (j, 0, 0)), **once)
    elif mode == "nt":
        b_spec = pl.BlockSpec((bn, K), ij(lambda i, j: (j, 0)), **once)
    else:
        b_spec = pl.BlockSpec((K, bn), ij(lambda i, j: (0, j)), **once)
    e_specs = [pl.BlockSpec((1, bn), ij(lambda i, j: (0, j))) if first is None
               else pl.BlockSpec((bm, bn), ij(lambda i, j, first=first: (i, first + j))) for _, first in extras]
    tile = pl.BlockSpec((bm, bn), ij(lambda i, j: (i, j)))
    out_specs = [tile] * no
    out_shape = [jax.ShapeDtypeStruct((M, N), dt) for dt in out_dtypes]
    if wide_first is not None:
        assert bn == N
        out_specs[0] = pl.BlockSpec((bm, wide_first[1]), ij(lambda i, j: (i, 0)))
        out_shape[0] = jax.ShapeDtypeStruct((M, wide_first[0]), out_dtypes[0])
    if sum_shape is not None:
        assert sum_shape[1] in (1, bn) and (sum_shape[1] == 1 or bn == N)
        out_specs.append(_full(sum_shape))
        out_shape.append(jax.ShapeDtypeStruct(sum_shape, F32))
    grid = (N // bn, M // bm) if j_outer else (M // bm, N // bn)
    sem = ("arbitrary", "arbitrary") if sum_shape is not None else ("parallel", "parallel")
    return pl.pallas_call(
        body, grid=grid, in_specs=[a_spec, b_spec] * n + e_specs, out_specs=out_specs, out_shape=out_shape,
        name=name, compiler_params=_params(sem),
    )(*[x for pair in pairs for x in pair], *[arr for arr, _ in extras])


def _rms(x, gain):
    return x * lax.rsqrt(jnp.mean(x * x, axis=-1, keepdims=True) + EPS) * gain


def _silu(x):
    return x * jax.nn.sigmoid(x)


def _act(g, u):
    return _silu(g) * u


def _merge(g0, g1, a_dn, a_swa):
    return jax.nn.sigmoid(g0) * a_dn + jax.nn.sigmoid(g1) * a_swa


def _dn_post(c, is_v, q_scale):
    a = _silu(c)
    rs = lax.rsqrt(jnp.sum(a * a, axis=-1, keepdims=True) + EPS) * q_scale
    return a * jnp.where(is_v, 1.0, rs)


def _dn_out(o, z, gain):
    return _rms(o, gain) * _silu(z)


def _dot(a, b, dims=_DIMS["nn"], hi=False):
    if a.ndim == 3 or b.ndim == 3:
        batch = a.shape[0] if a.ndim == 3 else b.shape[0]
        a = a if a.ndim == 3 else jnp.broadcast_to(a, (batch,) + a.shape)
        b = b if b.ndim == 3 else jnp.broadcast_to(b, (batch,) + b.shape)
        ((ca,), (cb,)), _ = dims
        dims = (((ca + 1,), (cb + 1,)), ((0,), (0,)))
    if hi:
        return lax.dot_general(a, b, dims, precision=HI, preferred_element_type=F32)
    return lax.dot_general(a.astype(BF16), b.astype(BF16), dims, preferred_element_type=F32)


def _pieces(x):
    hi = x.astype(BF16)
    r1 = x - hi.astype(F32)
    mid = r1.astype(BF16)
    return hi, mid, (r1 - mid.astype(F32)).astype(BF16)


def _sel_left_impl(m, x):
    mb = m.astype(BF16)
    hi, mid, lo = _pieces(x)
    return _dot(mb, hi) + (_dot(mb, mid) + _dot(mb, lo))


@jax.custom_vjp
def _sel_left(m, mt, x):
    return _sel_left_impl(m, x)


_sel_left.defvjp(lambda m, mt, x: (_sel_left_impl(m, x), (m, mt)),
                 lambda res, ct: (jnp.zeros_like(res[0]), jnp.zeros_like(res[1]), _sel_left_impl(res[1], ct)))


def _sel_right_impl(x, s):
    sb = s.astype(BF16)
    hi, mid, lo = _pieces(x)
    return _dot(hi, sb) + (_dot(mid, sb) + _dot(lo, sb))


@jax.custom_vjp
def _sel_right(x, s, st):
    return _sel_right_impl(x, s)


_sel_right.defvjp(lambda x, s, st: (_sel_right_impl(x, s), (s, st)),
                  lambda res, ct: (_sel_right_impl(ct, res[1]), jnp.zeros_like(res[0]), jnp.zeros_like(res[1])))


def _sel_nt_impl(s, x):
    sb = s.astype(BF16)
    hi, mid, lo = _pieces(x)
    return _dot(sb, hi, _DIMS["nt"]) + (_dot(sb, mid, _DIMS["nt"]) + _dot(sb, lo, _DIMS["nt"]))


def _sel_tn_impl(x, s):
    sb = s.astype(BF16)
    hi, mid, lo = _pieces(x)
    return _dot(hi, sb, _DIMS["tn"]) + (_dot(mid, sb, _DIMS["tn"]) + _dot(lo, sb, _DIMS["tn"]))


@jax.custom_vjp
def _sel_nt(s, x):
    return _sel_nt_impl(s, x)


_sel_nt.defvjp(lambda s, x: (_sel_nt_impl(s, x), s),
               lambda s, ct: (jnp.zeros_like(s), _sel_tn_impl(ct, s)))


def _dot3_impl(a, b):
    a_hi, a_lo, _ = _pieces(a)
    b_hi, b_lo, _ = _pieces(b)
    return _dot(a_hi, b_hi) + (_dot(a_hi, b_lo) + _dot(a_lo, b_hi))


@jax.custom_vjp
def _dot3(a, b):
    return _dot3_impl(a, b)


_dot3.defvjp(lambda a, b: (_dot3_impl(a, b), (a, b)),
             lambda res, ct: (_dot(ct, res[1], _DIMS["nt"]), _dot(res[0], ct, _DIMS["tn"])))


def _inv_impl(a, eye, strict):
    t = eye - a
    p = _dot(a, a)
    for level in range(5):
        t = t + _dot(t, p)
        if level < 4:
            p = _dot(p, p)
    t = t + _dot(t, eye - t - _dot3_impl(a, t))
    return jnp.where(strict > 0.5, t, eye)


@jax.custom_vjp
def _inv_given(a, t):
    return t.astype(F32)


_inv_given.defvjp(lambda a, t: (t.astype(F32), t),
                  lambda t, ct: (-_dot(_dot(t, ct, _DIMS["tn"]), t, _DIMS["nt"]), jnp.zeros_like(t)))


@jax.custom_vjp
def _lanes_join(a, b):
    return jnp.concatenate([a, b], axis=-1)


_lanes_join.defvjp(lambda a, b: (jnp.concatenate([a, b], axis=-1), None),
                   lambda _, ct: (ct[..., :ct.shape[-1] // 2], ct[..., ct.shape[-1] // 2:]))


@jax.custom_vjp
def _lanes_halves(y):
    h = y.shape[-1] // 2
    return y[..., :h], y[..., h:]


_lanes_halves.defvjp(lambda y: ((y[..., :y.shape[-1] // 2], y[..., y.shape[-1] // 2:]), None),
                     lambda _, ct: (jnp.concatenate(ct, axis=-1),))

GROUP = 4
GROUP_ROWS = GROUP * CHUNK


def _block_consts(n):
    ii = lax.broadcasted_iota(jnp.int32, (n, n), 0)
    jj = lax.broadcasted_iota(jnp.int32, (n, n), 1)
    shift = CHUNK.bit_length() - 1
    same = jnp.right_shift(ii, shift) == jnp.right_shift(jj, shift)
    return same & (ii >= jj), same & (ii <= jj), same & (ii > jj), same, ii == jj


def _lane0(n):
    s = (lax.broadcasted_iota(jnp.int32, (LANES, n), 0) == 0).astype(F32)
    st = (lax.broadcasted_iota(jnp.int32, (n, LANES), 1) == 0).astype(F32)
    return s, st


def _dn_group(q, k, v, g, beta, t_saved=None):
    n = GROUP_ROWS
    low_b, upp_b, strict_b, _, eye_b = _block_consts(n)
    low, upp, eye = low_b.astype(F32), upp_b.astype(F32), eye_b.astype(F32)
    s, st = _lane0(n)
    gc = _sel_left(low, upp, g)
    per_chunk = (g.shape[0], GROUP, CHUNK, LANES)
    gl = jnp.broadcast_to(jnp.sum(g.reshape(per_chunk), axis=2, keepdims=True), per_chunk).reshape(g.shape)
    col = _sel_right(gc, s, st)
    row = _sel_nt(st, gc)
    decay = jnp.exp(jnp.where(low_b, col - row, -jnp.inf))
    kb = k * beta
    vb = v * beta
    a = jnp.where(strict_b, _dot(kb, k, _DIMS["nt"]) * decay, 0.0)
    t = _inv_impl(a, eye, strict_b.astype(F32)) if t_saved is None else _inv_given(a, t_saved)
    u, w = _lanes_halves(_dot3(t, _lanes_join(vb, kb * jnp.exp(gc))))
    return u, w, q * jnp.exp(gc), k * jnp.exp(gl - gc), t


def _dn_chunk(q, k, g):
    ii = lax.broadcasted_iota(jnp.int32, (CHUNK, CHUNK), 0)
    jj = lax.broadcasted_iota(jnp.int32, (CHUNK, CHUNK), 1)
    low = (ii >= jj).astype(F32)
    upp = (ii <= jj).astype(F32)
    s, st = _lane0(CHUNK)
    gc = _sel_left(low, upp, g)
    col = _sel_right(gc, s, st)
    row = _sel_nt(st, gc)
    decay = jnp.exp(jnp.where(ii >= jj, col - row, -jnp.inf))
    qk = _dot(q, k, _DIMS["nt"]) * decay
    return qk, jnp.exp(jnp.sum(g, axis=-2, keepdims=True))


def _dn_step(s, u, w, qe, kd, qk, egl):
    v_new = u - _dot(w, s)
    o = _dot(qe, s) + _dot(qk, v_new)
    s_new = s * egl + _dot(kd, v_new, _DIMS["tn"])
    return s_new, o


def _swa_block(q, kband, vband, qg, kg, sinks, bias, mask):
    kn = _rms(kband, kg)
    qn = _rms(q, qg)
    logits = _dot(qn, kn, _DIMS["nt"]) * (SWA_DIM ** -0.5)
    logits = jnp.where(mask, logits + bias, -jnp.inf)
    m = jnp.maximum(jnp.max(logits, axis=-1, keepdims=True), sinks)
    p = jnp.exp(logits - m)
    denom = jnp.sum(p, axis=-1, keepdims=True) + jnp.exp(sinks - m)
    return _dot(p / denom, vband)


def _adamw(w, g, m, v):
    m = ADAM_B1 * m + (1.0 - ADAM_B1) * g
    v = ADAM_B2 * v + (1.0 - ADAM_B2) * jnp.square(g)
    m_hat = m / (1.0 - ADAM_B1 ** ADAM_STEP)
    v_hat = v / (1.0 - ADAM_B2 ** ADAM_STEP)
    delta = -ADAM_LR * (m_hat / (jnp.sqrt(v_hat) + ADAM_EPS) + ADAM_WD * w)
    return delta, m, v


def _row(tm, c, cb=0):
    return pl.BlockSpec((tm, c), lambda i, cb=cb: (i, cb))


def _full(shape):
    nd = len(shape)
    return pl.BlockSpec(shape, lambda *_, nd=nd: (0,) * nd)


def _norm_fwd(x, gain, name, tm=512):
    S = x.shape[0]

    def body(x_ref, g_ref, h_ref):
        h_ref[...] = _rms(x_ref[...], g_ref[...]).astype(BF16)

    return pl.pallas_call(
        body, grid=(S // tm,), in_specs=[_row(tm, D_MODEL), _full((1, D_MODEL))],
        out_specs=_row(tm, D_MODEL), out_shape=jax.ShapeDtypeStruct((S, D_MODEL), BF16),
        name=name, compiler_params=_params(("parallel",)))(x, gain)


def _shift_down(x, s):
    row = lax.broadcasted_iota(jnp.int32, x.shape, 0)
    return jnp.where(row >= s, pltpu.roll(x, s, axis=0), 0.0)


def _shift_up(x, s):
    n = x.shape[0]
    row = lax.broadcasted_iota(jnp.int32, x.shape, 0)
    return jnp.where(row < n - s, pltpu.roll(x, n - s, axis=0), 0.0)


def _conv(x, w):
    out = w[DN_CONV - 1:DN_CONV] * x
    for s in range(1, DN_CONV):
        out = out + w[DN_CONV - 1 - s:DN_CONV - s] * _shift_down(x, s)
    return out


def _dn_conv_fwd(proj, conv_w):
    S = proj.shape[0]
    nb = DN_QKV // LANES

    def body(x_ref, w_ref, o_ref):
        j = pl.program_id(0)
        q_scale = jnp.where(j < DN_HEADS, DN_DIM ** -0.5, 1.0).astype(F32)
        o_ref[...] = _dn_post(_conv(x_ref[...], w_ref[...]), j >= 2 * DN_HEADS, q_scale)

    return pl.pallas_call(
        body, grid=(nb,),
        in_specs=[pl.BlockSpec((S, LANES), lambda j: (0, P_QKV // LANES + j)),
                  pl.BlockSpec((DN_CONV, LANES), lambda j: (0, j))],
        out_specs=pl.BlockSpec((S, LANES), lambda j: (0, j)),
        out_shape=jax.ShapeDtypeStruct((S, DN_QKV), F32), name="dn_conv_fwd",
        compiler_params=_params(("parallel",)))(proj, conv_w)


def _dn_conv_bwd(proj, conv_w, dqkvn, dproj):
    S = proj.shape[0]
    nb = DN_QKV // LANES

    def body(x_ref, w_ref, d_ref, _, dx_ref, dw_ref):
        j = pl.program_id(0)
        q_scale = jnp.where(j < DN_HEADS, DN_DIM ** -0.5, 1.0).astype(F32)
        x = x_ref[...]
        w = w_ref[...]
        _, vjp = jax.vjp(lambda c: _dn_post(c, j >= 2 * DN_HEADS, q_scale), _conv(x, w))
        (dc,) = vjp(d_ref[0])
        dx = w[DN_CONV - 1:DN_CONV] * dc
        dw_ref[DN_CONV - 1:DN_CONV, :] = jnp.sum(dc * x, axis=0, keepdims=True)
        for s in range(1, DN_CONV):
            dx = dx + w[DN_CONV - 1 - s:DN_CONV - s] * _shift_up(dc, s)
            dw_ref[DN_CONV - 1 - s:DN_CONV - s, :] = jnp.sum(dc * _shift_down(x, s), axis=0, keepdims=True)
        dx_ref[...] = dx.astype(BF16)

    return pl.pallas_call(
        body, grid=(nb,),
        in_specs=[pl.BlockSpec((S, LANES), lambda j: (0, P_QKV // LANES + j)),
                  pl.BlockSpec((DN_CONV, LANES), lambda j: (0, j)),
                  pl.BlockSpec((1, S, LANES), lambda j: (lax.div(j, DN_HEADS), 0, lax.rem(j, DN_HEADS))),
                  pl.BlockSpec(memory_space=pl.ANY)],
        out_specs=[pl.BlockSpec((S, LANES), lambda j: (0, P_QKV // LANES + j)),
                   pl.BlockSpec((DN_CONV, LANES), lambda j: (0, j))],
        out_shape=[jax.ShapeDtypeStruct(dproj.shape, dproj.dtype), jax.ShapeDtypeStruct((DN_CONV, DN_QKV), F32)],
        input_output_aliases={3: 0},
        name="dn_conv_bwd", compiler_params=_params(("parallel",)))(proj, conv_w, dqkvn, dproj)


def _expanders():
    eb = np.zeros((LANES, DN_WIDTH), np.float32)
    ea = np.zeros((LANES, DN_WIDTH), np.float32)
    for h in range(DN_HEADS):
        eb[h, h * DN_DIM:(h + 1) * DN_DIM] = 1.0
        ea[DN_HEADS + h, h * DN_DIM:(h + 1) * DN_DIM] = 1.0
    return jnp.asarray(eb), jnp.asarray(ea), jnp.asarray(eb.T), jnp.asarray(ea.T)


def _dn_gate_args(a_log, dt_bias):
    alog = jnp.repeat(a_log.reshape(1, DN_HEADS), DN_DIM, axis=1)
    dtb = jnp.repeat(dt_bias.reshape(1, DN_HEADS), DN_DIM, axis=1)
    return _expanders() + (alog, dtb)


def _dn_gate_specs(tm):
    return [_row(tm, LANES, P_BA // LANES), _full((LANES, DN_WIDTH)), _full((LANES, DN_WIDTH)),
            _full((DN_WIDTH, LANES)), _full((DN_WIDTH, LANES)), _full((1, DN_WIDTH)), _full((1, DN_WIDTH))]


def _dn_gate_fn(ba, eb, ea, ebt, eat, alog, dtb):
    beta = jax.nn.sigmoid(_sel_right(ba, eb, ebt))
    g = -jnp.exp(alog) * jax.nn.softplus(_sel_right(ba, ea, eat) + dtb)
    return beta, g


def _dn_gate_fwd(proj, a_log, dt_bias, tm=512):
    S = proj.shape[0]
    args = _dn_gate_args(a_log, dt_bias)

    def body(ba_ref, eb_ref, ea_ref, ebt_ref, eat_ref, al_ref, dt_ref, beta_ref, g_ref):
        beta, g = _dn_gate_fn(ba_ref[...], eb_ref[...], ea_ref[...], ebt_ref[...], eat_ref[...], al_ref[...],
                              dt_ref[...])
        beta_ref[...] = beta
        g_ref[...] = g

    return pl.pallas_call(
        body, grid=(S // tm,), in_specs=_dn_gate_specs(tm), out_specs=[_row(tm, DN_WIDTH), _row(tm, DN_WIDTH)],
        out_shape=[jax.ShapeDtypeStruct((S, DN_WIDTH), F32), jax.ShapeDtypeStruct((S, DN_WIDTH), F32)],
        name="dn_gate_fwd", compiler_params=_params(("parallel",)))(proj, *args)


def _dn_gate_bwd(proj, a_log, dt_bias, dbeta, dg, dproj, tm=512):
    S = proj.shape[0]
    args = _dn_gate_args(a_log, dt_bias)

    def body(ba_ref, eb_ref, ea_ref, ebt_ref, eat_ref, al_ref, dt_ref, dbeta_ref, dg_ref, _, dba_ref, dal_ref,
             ddt_ref):
        eb, ea, ebt, eat = eb_ref[...], ea_ref[...], ebt_ref[...], eat_ref[...]
        _, vjp = jax.vjp(lambda ba, al, dt: _dn_gate_fn(ba, eb, ea, ebt, eat, al, dt), ba_ref[...], al_ref[...],
                         dt_ref[...])
        dba, dal, ddt = vjp((dbeta_ref[...], dg_ref[...]))
        dba_ref[...] = dba.astype(BF16)

        @pl.when(pl.program_id(0) == 0)
        def _():
            dal_ref[...] = jnp.zeros_like(dal_ref)
            ddt_ref[...] = jnp.zeros_like(ddt_ref)

        dal_ref[...] += dal
        ddt_ref[...] += ddt

    return pl.pallas_call(
        body, grid=(S // tm,),
        in_specs=_dn_gate_specs(tm) + [_row(tm, DN_WIDTH), _row(tm, DN_WIDTH), pl.BlockSpec(memory_space=pl.ANY)],
        out_specs=[_row(tm, LANES, P_BA // LANES), _full((1, DN_WIDTH)), _full((1, DN_WIDTH))],
        out_shape=[jax.ShapeDtypeStruct(dproj.shape, dproj.dtype), jax.ShapeDtypeStruct((1, DN_WIDTH), F32),
                   jax.ShapeDtypeStruct((1, DN_WIDTH), F32)],
        input_output_aliases={len(args) + 3: 0},
        name="dn_gate_bwd", compiler_params=_params(("arbitrary",)))(proj, *args, dbeta, dg, dproj)


PREP_GROUPS = 4
PREP_CHUNKS = GROUP * PREP_GROUPS


def _dn_prep_specs():
    rows = PREP_CHUNKS * CHUNK
    q = pl.BlockSpec((rows, LANES), lambda h, c: (c, h))
    k = pl.BlockSpec((rows, LANES), lambda h, c: (c, DN_HEADS + h))
    v = pl.BlockSpec((rows, LANES), lambda h, c: (c, 2 * DN_HEADS + h))
    qk = pl.BlockSpec((1, rows, CHUNK), lambda h, c: (h, c, 0))
    egl = pl.BlockSpec((1, PREP_CHUNKS, 1, LANES), lambda h, c: (h, c, 0, 0))
    return q, k, v, qk, egl


def _dn_prep_fwd(qkvn, g, beta):
    S = qkvn.shape[0]
    nc = S // CHUNK
    q, k, v, qks, egl = _dn_prep_specs()

    def body(q_ref, k_ref, v_ref, g_ref, b_ref, u_ref, w_ref, qe_ref, kd_ref, qk_ref, egl_ref, t_ref):
        rows = PREP_CHUNKS * CHUNK
        grp = (PREP_GROUPS, GROUP_ROWS, LANES)
        chk = (PREP_CHUNKS, CHUNK, LANES)
        q, k, g = q_ref[...], k_ref[...], g_ref[...]
        u, w, qe, kd, t = _dn_group(q.reshape(grp), k.reshape(grp), v_ref[...].reshape(grp), g.reshape(grp),
                                    b_ref[...].reshape(grp))
        u_ref[...] = u.reshape(rows, LANES)
        w_ref[...] = w.reshape(rows, LANES)
        qe_ref[...] = qe.reshape(rows, LANES)
        kd_ref[...] = kd.reshape(rows, LANES)
        t_ref[0] = t.reshape(rows, GROUP_ROWS).astype(BF16)
        qk, e = _dn_chunk(q.reshape(chk), k.reshape(chk), g.reshape(chk))
        qk_ref[0] = qk.reshape(rows, CHUNK)
        egl_ref[0] = e

    wide = jax.ShapeDtypeStruct((S, DN_WIDTH), F32)
    return pl.pallas_call(
        body, grid=(DN_HEADS, nc // PREP_CHUNKS), in_specs=[q, k, v, q, q],
        out_specs=[q, q, q, q, qks, egl, _dn_tinv_spec()],
        out_shape=[wide, wide, wide, wide, jax.ShapeDtypeStruct((DN_HEADS, S, CHUNK), F32),
                   jax.ShapeDtypeStruct((DN_HEADS, nc, 1, LANES), F32),
                   jax.ShapeDtypeStruct((DN_HEADS, S, GROUP_ROWS), BF16)],
        name="dn_prep_fwd", compiler_params=_params(("parallel", "parallel")))(qkvn, qkvn, qkvn, g, beta)


def _dn_tinv_spec():
    return pl.BlockSpec((1, PREP_CHUNKS * CHUNK, GROUP_ROWS), lambda h, c: (h, c, 0))


def _dn_prep_bwd(qkvn, g, beta, tinv, du, dw, dqe, dkd, dqk, degl):
    S = qkvn.shape[0]
    nc = S // CHUNK
    q, k, v, qks, egl = _dn_prep_specs()

    def body(q_ref, k_ref, v_ref, g_ref, b_ref, t_ref, du_ref, dw_ref, dqe_ref, dkd_ref, dqk_ref, degl_ref,
             dqkv_ref, dg_ref, db_ref):
        rows = PREP_CHUNKS * CHUNK
        grp = (PREP_GROUPS, GROUP_ROWS, LANES)
        chk = (PREP_CHUNKS, CHUNK, LANES)
        q, k, g = q_ref[...], k_ref[...], g_ref[...]
        t_saved = t_ref[0].reshape(PREP_GROUPS, GROUP_ROWS, GROUP_ROWS)
        _, vjp = jax.vjp(lambda *x: _dn_group(*x, t_saved=t_saved)[:4], q.reshape(grp), k.reshape(grp),
                         v_ref[...].reshape(grp), g.reshape(grp), b_ref[...].reshape(grp))
        dq, dk, dv, dg, db = vjp((du_ref[...].reshape(grp), dw_ref[...].reshape(grp), dqe_ref[...].reshape(grp),
                                  dkd_ref[...].reshape(grp)))
        _, vjp = jax.vjp(_dn_chunk, q.reshape(chk), k.reshape(chk), g.reshape(chk))
        dq2, dk2, dg2 = vjp((dqk_ref[0].reshape(PREP_CHUNKS, CHUNK, CHUNK), degl_ref[0]))
        dqkv_ref[0] = dq.reshape(rows, LANES) + dq2.reshape(rows, LANES)
        dqkv_ref[1] = dk.reshape(rows, LANES) + dk2.reshape(rows, LANES)
        dqkv_ref[2] = dv.reshape(rows, LANES)
        dg_ref[...] = dg.reshape(rows, LANES) + dg2.reshape(rows, LANES)
        db_ref[...] = db.reshape(rows, LANES)

    wide = jax.ShapeDtypeStruct((S, DN_WIDTH), F32)
    rows = PREP_CHUNKS * CHUNK
    return pl.pallas_call(
        body, grid=(DN_HEADS, nc // PREP_CHUNKS), in_specs=[q, k, v, q, q, _dn_tinv_spec(), q, q, q, q, qks, egl],
        out_specs=[pl.BlockSpec((3, rows, LANES), lambda h, c: (0, c, h)), q, q],
        out_shape=[jax.ShapeDtypeStruct((3, S, DN_WIDTH), F32), wide, wide],
        name="dn_prep_bwd", compiler_params=_params(("parallel", "parallel")),
    )(qkvn, qkvn, qkvn, g, beta, tinv, du, dw, dqe, dkd, dqk, degl)


def _dn_scan_specs(nc, reverse):
    def cidx(c):
        return nc - 1 - c if reverse else c

    hc = pl.BlockSpec((CHUNK, DN_WIDTH), lambda c: (cidx(c), 0))
    qk = pl.BlockSpec((DN_HEADS, CHUNK, CHUNK), lambda c: (0, cidx(c), 0))
    egl = pl.BlockSpec((DN_HEADS, 1, 1, LANES), lambda c: (0, cidx(c), 0, 0))
    st = pl.BlockSpec((DN_HEADS, 1, DN_DIM, DN_DIM), lambda c: (0, cidx(c), 0, 0))
    return hc, qk, egl, st


def _heads(ref):
    return jnp.stack([ref[:, pl.ds(h * DN_DIM, DN_DIM)] for h in range(DN_HEADS)])


def _dn_scan_fwd(u, w, qe, kd, qk, egl):
    S = u.shape[0]
    nc = S // CHUNK
    hc, qks, egls, st = _dn_scan_specs(nc, False)

    def body(u_ref, w_ref, qe_ref, kd_ref, qk_ref, egl_ref, o_ref, st_ref, s_scr):
        @pl.when(pl.program_id(0) == 0)
        def _():
            s_scr[...] = jnp.zeros_like(s_scr)

        s = s_scr[...]
        st_ref[:, 0] = s
        s_new, o = _dn_step(s, _heads(u_ref), _heads(w_ref), _heads(qe_ref), _heads(kd_ref), qk_ref[...],
                            egl_ref[:, 0])
        for h in range(DN_HEADS):
            o_ref[:, pl.ds(h * DN_DIM, DN_DIM)] = o[h]
        s_scr[...] = s_new

    return pl.pallas_call(
        body, grid=(nc,), in_specs=[hc, hc, hc, hc, qks, egls], out_specs=[hc, st],
        out_shape=[jax.ShapeDtypeStruct((S, DN_WIDTH), F32), jax.ShapeDtypeStruct((DN_HEADS, nc, DN_DIM, DN_DIM), F32)],
        scratch_shapes=[pltpu.VMEM((DN_HEADS, DN_DIM, DN_DIM), F32)], name="dn_scan_fwd",
        compiler_params=_params(("arbitrary",)))(u, w, qe, kd, qk, egl)


def _dn_scan_bwd(u, w, qe, kd, qk, egl, states, do):
    S = u.shape[0]
    nc = S // CHUNK
    hc, qks, egls, st = _dn_scan_specs(nc, True)

    def body(u_ref, w_ref, qe_ref, kd_ref, qk_ref, egl_ref, st_ref, do_ref,
             du_ref, dw_ref, dqe_ref, dkd_ref, dqk_ref, degl_ref, ds_scr):
        @pl.when(pl.program_id(0) == 0)
        def _():
            ds_scr[...] = jnp.zeros_like(ds_scr)

        _, vjp = jax.vjp(_dn_step, st_ref[:, 0], _heads(u_ref), _heads(w_ref), _heads(qe_ref), _heads(kd_ref),
                         qk_ref[...], egl_ref[:, 0])
        ds, du, dw, dqe, dkd, dqk, degl = vjp((ds_scr[...], _heads(do_ref)))
        ds_scr[...] = ds
        dqk_ref[...] = dqk
        degl_ref[:, 0] = degl
        for h in range(DN_HEADS):
            cols = pl.ds(h * DN_DIM, DN_DIM)
            du_ref[:, cols] = du[h]
            dw_ref[:, cols] = dw[h]
            dqe_ref[:, cols] = dqe[h]
            dkd_ref[:, cols] = dkd[h]

    wide = jax.ShapeDtypeStruct((S, DN_WIDTH), F32)
    return pl.pallas_call(
        body, grid=(nc,), in_specs=[hc, hc, hc, hc, qks, egls, st, hc],
        out_specs=[hc, hc, hc, hc, qks, egls],
        out_shape=[wide, wide, wide, wide, jax.ShapeDtypeStruct((DN_HEADS, S, CHUNK), F32),
                   jax.ShapeDtypeStruct((DN_HEADS, nc, 1, LANES), F32)],
        scratch_shapes=[pltpu.VMEM((DN_HEADS, DN_DIM, DN_DIM), F32)], name="dn_scan_bwd",
        compiler_params=_params(("arbitrary",)))(u, w, qe, kd, qk, egl, states, do)


def _dn_out_fwd(o, proj, gain, tm=512):
    S = o.shape[0]

    def body(o_ref, z_ref, g_ref, y_ref):
        y_ref[...] = _dn_out(o_ref[...], z_ref[...], g_ref[...]).astype(BF16)

    hs = pl.BlockSpec((tm, LANES), lambda i, h: (i, h))
    zs = pl.BlockSpec((tm, LANES), lambda i, h: (i, P_Z // LANES + h))
    return pl.pallas_call(
        body, grid=(S // tm, DN_HEADS), in_specs=[hs, zs, _full((1, DN_DIM))], out_specs=hs,
        out_shape=jax.ShapeDtypeStruct((S, DN_WIDTH), BF16), name="dn_out_fwd",
        compiler_params=_params(("parallel", "parallel")))(o, proj, gain)


_ANY = pl.BlockSpec(memory_space=pl.ANY)


def _dn_out_bwd(o, proj, gain, dy, dproj, tm=512):
    S = o.shape[0]

    def body(o_ref, z_ref, g_ref, dy_ref, _, do_ref, dz_ref, dg_ref):
        _, vjp = jax.vjp(_dn_out, o_ref[...], z_ref[...], g_ref[...])
        do, dz, dg = vjp(dy_ref[...])
        do_ref[...] = do
        dz_ref[...] = dz.astype(BF16)

        @pl.when((pl.program_id(0) == 0) & (pl.program_id(1) == 0))
        def _():
            dg_ref[...] = jnp.zeros_like(dg_ref)

        dg_ref[...] += dg

    hs = pl.BlockSpec((tm, LANES), lambda i, h: (i, h))
    zs = pl.BlockSpec((tm, LANES), lambda i, h: (i, P_Z // LANES + h))
    return pl.pallas_call(
        body, grid=(S // tm, DN_HEADS), in_specs=[hs, zs, _full((1, DN_DIM)), hs, _ANY],
        out_specs=[hs, zs, _full((1, DN_DIM))],
        out_shape=[jax.ShapeDtypeStruct((S, DN_WIDTH), F32), jax.ShapeDtypeStruct(dproj.shape, dproj.dtype),
                   jax.ShapeDtypeStruct((1, DN_DIM), F32)],
        input_output_aliases={4: 1},
        name="dn_out_bwd", compiler_params=_params(("arbitrary", "arbitrary")))(o, proj, gain, dy, dproj)


def _rel_buckets():
    qi = np.arange(BLOCK)[:, None]
    kj = np.arange(2 * BLOCK)[None, :]
    n = np.maximum(BLOCK + qi - kj, 0)
    max_exact = REL_BUCKETS // 2
    nf = np.maximum(n, 1).astype(np.float32)
    large = max_exact + (np.log(nf / np.float32(max_exact)) / np.float32(math.log(REL_MAX_DIST / max_exact))
                         * np.float32(REL_BUCKETS - max_exact)).astype(np.int32)
    large = np.minimum(large, REL_BUCKETS - 1)
    return np.where(n < max_exact, n, large).astype(np.int32)


def _bias_fwd(rel_bias):
    buckets = jnp.asarray(_rel_buckets())

    def body(rb_ref, bk_ref, o_ref):
        bk = bk_ref[...]
        for h in range(SWA_HEADS):
            acc = jnp.zeros((BLOCK, 2 * BLOCK), F32)
            for b in range(REL_BUCKETS):
                acc = jnp.where(bk == b, rb_ref[b, h], acc)
            o_ref[h] = acc

    return pl.pallas_call(
        body, in_specs=[pl.BlockSpec(memory_space=pltpu.SMEM), pl.BlockSpec(memory_space=pltpu.VMEM)],
        out_specs=pl.BlockSpec(memory_space=pltpu.VMEM),
        out_shape=jax.ShapeDtypeStruct((SWA_HEADS, BLOCK, 2 * BLOCK), F32), name="swa_bias_fwd",
        compiler_params=_params())(rel_bias, buckets)


def _bias_bwd(dbias):
    buckets = jnp.asarray(_rel_buckets())

    def body(d_ref, bk_ref, o_ref):
        bk = bk_ref[...]
        lane = lax.broadcasted_iota(jnp.int32, (1, LANES), 1)
        for h in range(SWA_HEADS):
            d = d_ref[h]
            row = jnp.zeros((1, LANES), F32)
            for b in range(REL_BUCKETS):
                part = jnp.sum(jnp.where(bk == b, d, 0.0), axis=1, keepdims=True)
                row = jnp.where(lane == b, jnp.sum(part, axis=0, keepdims=True), row)
            o_ref[h:h + 1, :] = row

    return pl.pallas_call(
        body, in_specs=[pl.BlockSpec(memory_space=pltpu.VMEM), pl.BlockSpec(memory_space=pltpu.VMEM)],
        out_specs=pl.BlockSpec(memory_space=pltpu.VMEM),
        out_shape=jax.ShapeDtypeStruct((SWA_HEADS, LANES), F32), name="swa_bias_bwd",
        compiler_params=_params())(dbias, buckets)


def _swa_mask(n):
    qi = lax.broadcasted_iota(jnp.int32, (BLOCK, 2 * BLOCK), 0)
    kj = lax.broadcasted_iota(jnp.int32, (BLOCK, 2 * BLOCK), 1)
    dist = BLOCK + qi - kj
    return (dist >= 0) & (dist < WINDOW) & ((n > 0) | (kj >= BLOCK))


def _swa_in_specs():
    q = pl.BlockSpec((BLOCK, SWA_WIDTH), lambda n: (n, P_SQ // SWA_WIDTH))
    kc = pl.BlockSpec((BLOCK, SWA_KVW), lambda n: (n, P_SK // SWA_KVW))
    kp = pl.BlockSpec((BLOCK, SWA_KVW), lambda n: (jnp.maximum(n - 1, 0), P_SK // SWA_KVW))
    vc = pl.BlockSpec((BLOCK, SWA_KVW), lambda n: (n, P_SV // SWA_KVW))
    vp = pl.BlockSpec((BLOCK, SWA_KVW), lambda n: (jnp.maximum(n - 1, 0), P_SV // SWA_KVW))
    small = [_full((1, SWA_DIM)), _full((1, SWA_DIM)), _full((1, SWA_HEADS)),
             _full((SWA_HEADS, BLOCK, 2 * BLOCK))]
    return [q, kp, kc, vp, vc] + small


def _swa_load(q_ref, kp_ref, kc_ref, vp_ref, vc_ref, s_ref):
    q = jnp.stack([q_ref[:, pl.ds(h * SWA_DIM, SWA_DIM)] for h in range(SWA_HEADS)])
    kbands, vbands = [], []
    for kv in range(SWA_KV):
        cols = pl.ds(kv * SWA_DIM, SWA_DIM)
        kbands += [jnp.concatenate([kp_ref[:, cols], kc_ref[:, cols]], axis=0)] * SWA_GROUP
        vbands += [jnp.concatenate([vp_ref[:, cols], vc_ref[:, cols]], axis=0)] * SWA_GROUP
    sinks = jnp.stack([s_ref[:, pl.ds(h, 1)] for h in range(SWA_HEADS)])
    return q, jnp.stack(kbands), jnp.stack(vbands), sinks


def _swa_fwd(proj, q_gain, k_gain, sinks, bias):
    S = proj.shape[0]

    def body(q_ref, kp_ref, kc_ref, vp_ref, vc_ref, qg_ref, kg_ref, s_ref, bias_ref, y_ref):
        mask = _swa_mask(pl.program_id(0))
        q, kband, vband, sk = _swa_load(q_ref, kp_ref, kc_ref, vp_ref, vc_ref, s_ref)
        out = _swa_block(q, kband, vband, qg_ref[...], kg_ref[...], sk, bias_ref[...], mask)
        for h in range(SWA_HEADS):
            y_ref[:, pl.ds(h * SWA_DIM, SWA_DIM)] = out[h].astype(BF16)

    return pl.pallas_call(
        body, grid=(S // BLOCK,), in_specs=_swa_in_specs(),
        out_specs=pl.BlockSpec((BLOCK, SWA_WIDTH), lambda n: (n, 0)),
        out_shape=jax.ShapeDtypeStruct((S, SWA_WIDTH), BF16), name="swa_fwd",
        compiler_params=_params(("parallel",)))(proj, proj, proj, proj, proj, q_gain, k_gain, sinks, bias)


def _swa_bwd(proj, q_gain, k_gain, sinks, bias, dy, dproj):
    S = proj.shape[0]

    def body(q_ref, kp_ref, kc_ref, vp_ref, vc_ref, qg_ref, kg_ref, s_ref, bias_ref, dy_ref, _,
             dq_ref, dk_ref, dv_ref, dqg_ref, dkg_ref, ds_ref, dbias_ref):
        n = pl.program_id(0)
        mask = _swa_mask(n)

        @pl.when(n == 0)
        def _():
            for r in (dk_ref, dv_ref, dqg_ref, dkg_ref, ds_ref, dbias_ref):
                r[...] = jnp.zeros_like(r)

        cur = pl.ds(pl.multiple_of(n * BLOCK, BLOCK), BLOCK)
        prev = pl.ds(pl.multiple_of(jnp.maximum(n - 1, 0) * BLOCK, BLOCK), BLOCK)
        q, kband, vband, sk = _swa_load(q_ref, kp_ref, kc_ref, vp_ref, vc_ref, s_ref)
        _, vjp = jax.vjp(lambda q, kb, vb, qg, kg, sk, bs: _swa_block(q, kb, vb, qg, kg, sk, bs, mask),
                         q, kband, vband, qg_ref[...], kg_ref[...], sk, bias_ref[...])
        dy = jnp.stack([dy_ref[:, pl.ds(h * SWA_DIM, SWA_DIM)] for h in range(SWA_HEADS)])
        dq, dkb, dvb, dqg, dkg, dsk, dbs = vjp(dy)
        for h in range(SWA_HEADS):
            dq_ref[:, pl.ds(h * SWA_DIM, SWA_DIM)] = dq[h].astype(BF16)
            ds_ref[:, pl.ds(h, 1)] += dsk[h]
        dbias_ref[...] += dbs
        dqg_ref[...] += dqg
        dkg_ref[...] += dkg
        for kv in range(SWA_KV):
            cols = pl.ds(kv * SWA_DIM, SWA_DIM)
            group = range(kv * SWA_GROUP, (kv + 1) * SWA_GROUP)
            dk_kv = sum(dkb[h] for h in group)
            dv_kv = sum(dvb[h] for h in group)
            dk_ref[cur, cols] += dk_kv[BLOCK:]
            dv_ref[cur, cols] += dv_kv[BLOCK:]

            @pl.when(n > 0)
            def _(cols=cols, dk_kv=dk_kv, dv_kv=dv_kv):
                dk_ref[prev, cols] += dk_kv[:BLOCK]
                dv_ref[prev, cols] += dv_kv[:BLOCK]

    return pl.pallas_call(
        body, grid=(S // BLOCK,),
        in_specs=_swa_in_specs() + [pl.BlockSpec((BLOCK, SWA_WIDTH), lambda n: (n, 0)),
                                    pl.BlockSpec(memory_space=pl.ANY)],
        out_specs=[pl.BlockSpec((BLOCK, SWA_WIDTH), lambda n: (n, P_SQ // SWA_WIDTH)), _full((S, SWA_KVW)),
                   _full((S, SWA_KVW)), _full((1, SWA_DIM)), _full((1, SWA_DIM)), _full((1, SWA_HEADS)),
                   _full((SWA_HEADS, BLOCK, 2 * BLOCK))],
        out_shape=[jax.ShapeDtypeStruct(dproj.shape, dproj.dtype), jax.ShapeDtypeStruct((S, SWA_KVW), F32),
                   jax.ShapeDtypeStruct((S, SWA_KVW), F32), jax.ShapeDtypeStruct((1, SWA_DIM), F32),
                   jax.ShapeDtypeStruct((1, SWA_DIM), F32), jax.ShapeDtypeStruct((1, SWA_HEADS), F32),
                   jax.ShapeDtypeStruct((SWA_HEADS, BLOCK, 2 * BLOCK), F32)],
        input_output_aliases={10: 0},
        name="swa_bwd", compiler_params=_params(("arbitrary",)),
    )(proj, proj, proj, proj, proj, q_gain, k_gain, sinks, bias, dy, dproj)


def _kv_into(dproj, dk, dv, tm=512):
    S = dk.shape[0]

    def body(dk_ref, dv_ref, _, o_ref):
        o_ref[:, :SWA_KVW] = dk_ref[...].astype(BF16)
        o_ref[:, SWA_KVW:] = dv_ref[...].astype(BF16)

    return pl.pallas_call(
        body, grid=(S // tm,), in_specs=[_row(tm, SWA_KVW), _row(tm, SWA_KVW), pl.BlockSpec(memory_space=pl.ANY)],
        out_specs=_row(tm, 2 * SWA_KVW, P_SK // (2 * SWA_KVW)),
        out_shape=jax.ShapeDtypeStruct(dproj.shape, dproj.dtype), input_output_aliases={2: 0},
        name="swa_kv_into", compiler_params=_params(("parallel",)))(dk, dv, dproj)


def _position():
    return lax.axis_index("x"), lax.axis_index("y"), lax.axis_index("c")


def _all_gather(shards, name="all_gather_weights"):
    na = len(shards)

    def body(*refs):
        x_refs, out_refs = refs[:na], refs[na:2 * na]
        send_sems, recv_sems, local_sems = refs[2 * na:]
        x, y, c = _position()
        me, sibling = (x, y, c), (x, y, 1 - c)
        chips = [(1 - x, y), (x, 1 - y), (1 - x, 1 - y)]

        def copy(a, k, block, to, own=False):
            px, py, pc = block
            slot = out_refs[a].at[4 * px + 2 * py + pc]
            return pltpu.make_async_remote_copy(
                src_ref=x_refs[a] if own else slot, dst_ref=slot, send_sem=send_sems.at[7 * a + k],
                recv_sem=recv_sems.at[7 * a + k], device_id=to, device_id_type=MESH_ID)

        mine = [pltpu.make_async_copy(x_refs[a], out_refs[a].at[4 * x + 2 * y + c], local_sems.at[a])
                for a in range(na)]
        for cp in mine:
            cp.start()
        first = []
        for a in range(na):
            first.append(copy(a, 0, me, sibling, own=True))
            first += [copy(a, 1 + j, me, (*chip, c), own=True) for j, chip in enumerate(chips)]
        for cp in first:
            cp.start()
        passed = []
        for j, chip in enumerate(chips):
            for a in range(na):
                copy(a, 1 + j, (*chip, c), me).wait_recv()
                passed.append(copy(a, 4 + j, (*chip, c), sibling))
                passed[-1].start()
        for a in range(na):
            copy(a, 0, sibling, me).wait_recv()
            for j, chip in enumerate(chips):
                copy(a, 4 + j, (*chip, 1 - c), me).wait_recv()
        for cp in first + passed:
            cp.wait_send()
        for cp in mine:
            cp.wait()

    return pl.pallas_call(
        body, in_specs=[pl.BlockSpec(memory_space=pl.ANY)] * na, out_specs=[pl.BlockSpec(memory_space=pl.ANY)] * na,
        out_shape=[jax.ShapeDtypeStruct((N_DEV,) + s.shape, s.dtype) for s in shards],
        scratch_shapes=[pltpu.SemaphoreType.DMA((7 * na,)), pltpu.SemaphoreType.DMA((7 * na,)),
                        pltpu.SemaphoreType.DMA((na,))],
        name=name)(*shards)


_HBM = pl.BlockSpec(memory_space=pltpu.HBM)
_SEM = pl.BlockSpec(memory_space=pltpu.SEMAPHORE)
_DATAFLOW = pltpu.SideEffectType.DATAFLOW_SIDE_EFFECTING


def _peers(x, y, c):
    out = []
    for k in range(1, N_DEV):
        px, py, pc = x ^ (k >> 2), y ^ ((k >> 1) & 1), c ^ (k & 1)
        out.append(((px, py, pc), 4 * px + 2 * py + pc))
    return out


def _split_copies(src_refs, land_refs, send_sems, recv_sems, scatter):
    x, y, c = _position()
    me = 4 * x + 2 * y + c
    sends, recvs = [], []
    for k, (peer_id, peer) in enumerate(_peers(x, y, c)):
        for a, (src, land) in enumerate(zip(src_refs, land_refs)):
            sems = dict(send_sem=send_sems.at[7 * a + k], recv_sem=recv_sems.at[7 * a + k],
                        device_id=peer_id, device_id_type=MESH_ID)
            mine = src.at[peer] if scatter else src
            sends.append(pltpu.make_async_remote_copy(src_ref=mine, dst_ref=land.at[me], **sems))
            recvs.append(pltpu.make_async_remote_copy(src_ref=mine, dst_ref=land.at[peer], **sems))
    return sends, recvs


def _all_gather_direct(shards, name, after):
    na = len(shards)

    def body(*refs):
        x_refs, out_refs = refs[:na], refs[na + 1:2 * na + 1]
        send_sems, recv_sems, local_sems = refs[2 * na + 1:]
        x, y, c = _position()
        me = 4 * x + 2 * y + c
        local = [pltpu.make_async_copy(x_refs[a], out_refs[a].at[me], local_sems.at[a]) for a in range(na)]
        sends, recvs = _split_copies(x_refs, out_refs, send_sems, recv_sems, False)
        for cp in local + sends:
            cp.start()
        for cp in recvs:
            cp.wait_recv()
        for cp in sends:
            cp.wait_send()
        for cp in local:
            cp.wait()

    return pl.pallas_call(
        body, in_specs=[pl.BlockSpec(memory_space=pl.ANY)] * (na + 1),
        out_specs=[pl.BlockSpec(memory_space=pl.ANY)] * na,
        out_shape=[jax.ShapeDtypeStruct((N_DEV,) + s.shape, s.dtype) for s in shards],
        scratch_shapes=[pltpu.SemaphoreType.DMA((7 * na,)), pltpu.SemaphoreType.DMA((7 * na,)),
                        pltpu.SemaphoreType.DMA((na,))],
        name=name)(*shards, after)


def _exchange_start(srcs, scatter, name, after=None):
    na = len(srcs)
    lands = [lax.empty(s.shape if scatter else (N_DEV,) + s.shape, s.dtype) for s in srcs]
    extra = [] if after is None else [after]

    def body(*refs):
        src_refs, land_refs = refs[:na], refs[na:2 * na]
        send_sems, recv_sems = refs[2 * na + len(extra)], refs[2 * na + len(extra) + 1]
        token = refs[-1]
        sends, _ = _split_copies(src_refs, land_refs, send_sems, recv_sems, scatter)
        for cp in sends:
            cp.start()
        token[...] = jnp.zeros_like(token)

    hbm = lambda a: pltpu.HBM(a.shape, a.dtype)
    out = pl.pallas_call(
        body, name=name,
        out_shape=(pltpu.SemaphoreType.DMA((7 * na,)), pltpu.SemaphoreType.DMA((7 * na,)),
                   *[hbm(s) for s in srcs], *[hbm(l) for l in lands], jax.ShapeDtypeStruct((8, LANES), F32)),
        in_specs=[_HBM] * (2 * na) + [pl.BlockSpec(memory_space=pl.ANY)] * len(extra),
        out_specs=(_SEM, _SEM, *[_HBM] * (2 * na), pl.BlockSpec(memory_space=pltpu.VMEM)),
        input_output_aliases={i: 2 + i for i in range(2 * na)},
        compiler_params=pltpu.CompilerParams(has_side_effects=_DATAFLOW),
    )(*[pltpu.with_memory_space_constraint(s, pltpu.HBM) for s in srcs],
      *[pltpu.with_memory_space_constraint(l, pltpu.HBM) for l in lands], *extra)
    return (out[0], out[1], list(out[2:2 + na]), list(out[2 + na:2 + 2 * na])), out[-1]


def _exchange_wait(handle, after, scatter, name):
    send_sems, recv_sems, srcs, lands = handle
    na = len(srcs)

    def body(*refs):
        src_refs, land_refs = refs[:na], refs[na:2 * na]
        s_sems, r_sems = refs[2 * na], refs[2 * na + 1]
        sends, recvs = _split_copies(src_refs, land_refs, s_sems, r_sems, scatter)
        for cp in sends:
            cp.wait_send()
        for cp in recvs:
            cp.wait_recv()

    hbm = lambda a: pltpu.HBM(a.shape, a.dtype)
    out = pl.pallas_call(
        body, name=name, out_shape=(*[hbm(s) for s in srcs], *[hbm(l) for l in lands]),
        in_specs=[_HBM] * (2 * na) + [_SEM, _SEM, pl.BlockSpec(memory_space=pl.ANY)],
        out_specs=tuple([_HBM] * (2 * na)), input_output_aliases={i: i for i in range(2 * na)},
        compiler_params=pltpu.CompilerParams(has_side_effects=_DATAFLOW),
    )(*srcs, *lands, send_sems, recv_sems, after)
    return list(out[:na]), list(out[na:])


def _own_slot(landed, own):
    me = 4 * lax.axis_index("x") + 2 * lax.axis_index("y") + lax.axis_index("c")
    return lax.dynamic_update_slice_in_dim(landed, own[None], me, axis=0)


def _adam_update(parts, w, m, v, name, tr=256):
    _, r, c = w.shape
    tr = _pick_rows(r, tr)
    cp = parts.shape[2]

    def body(p_ref, w_ref, m_ref, v_ref, g_ref, d_ref, nm_ref, nv_ref):
        g = p_ref[0, :, pl.ds(0, c)].astype(F32)
        for i in range(1, N_DEV):
            g = g + p_ref[i, :, pl.ds(0, c)].astype(F32)
        delta, nm, nv = _adamw(w_ref[0], g, m_ref[0], v_ref[0])
        g_ref[0] = g
        d_ref[0] = delta
        nm_ref[0] = nm
        nv_ref[0] = nv

    rs = pl.BlockSpec((1, tr, c), lambda i: (0, i, 0))
    return pl.pallas_call(
        body, grid=(r // tr,), in_specs=[pl.BlockSpec((N_DEV, tr, cp), lambda i: (0, i, 0)), rs, rs, rs],
        out_specs=[rs] * 4, out_shape=[jax.ShapeDtypeStruct((1, r, c), F32)] * 4, name=name,
        compiler_params=_params(("parallel",)))(parts, w, m, v)


def _pick_rows(rows, target):
    if rows <= target:
        return rows
    t = target
    while t >= 16:
        if rows % t == 0:
            return t
        t -= 16
    return rows


BIG = ("w_in", "w_branch_dn", "w_branch_swa", "w_out", "w_gate", "w_up", "w_down")
IN_SHARD, IN_WIRE = D_IN // N_DEV, 640
FF_SHARD, FF_WIRE = D_FF // N_DEV, 384
D_FFP = N_DEV * FF_WIRE
BIG_SHAPES = {"w_in": ((D_MODEL, IN_SHARD), (D_MODEL, IN_WIRE)),
              "w_branch_dn": ((DN_WIDTH, LANES), (DN_WIDTH, LANES)),
              "w_branch_swa": ((SWA_WIDTH, LANES), (SWA_WIDTH, LANES)),
              "w_out": ((LANES, D_MODEL), (LANES, D_MODEL)),
              "w_gate": ((D_MODEL, FF_SHARD), (D_MODEL, FF_WIRE)),
              "w_up": ((D_MODEL, FF_SHARD), (D_MODEL, FF_WIRE)),
              "w_down": ((FF_SHARD, D_MODEL), (FF_WIRE, D_MODEL))}
CONV_SHARD, CONV_WIRE = (DN_CONV, DN_QKV // N_DEV), (8, 256)


def _pad_to(a, shape):
    return jnp.pad(a, [(0, t - s) for s, t in zip(a.shape, shape)])


_IN_SEGS = ((R_GATE, 2048, P_GATE), (R_QKV, DN_QKV, P_QKV), (R_Z, DN_WIDTH, P_Z), (R_SQ, SWA_WIDTH, P_SQ),
            (R_SK, SWA_KVW, P_SK), (R_SV, SWA_KVW, P_SV), (R_B, 8, P_BA))


def _w_in_from_blocks(blocks):
    parts = []
    for rs, n, _ in _IN_SEGS:
        for dev in range(N_DEV):
            lo, hi = max(rs, IN_SHARD * dev), min(rs + n, IN_SHARD * (dev + 1))
            if lo < hi:
                parts.append(blocks[dev, :, lo - IN_SHARD * dev:hi - IN_SHARD * dev])
    parts.append(jnp.zeros((blocks.shape[1], P_WIDTH - P_BA - 8), blocks.dtype))
    return jnp.concatenate(parts, axis=1)


def _w_in_to_blocks(g):
    out = []
    for dev in range(N_DEV):
        parts = []
        for rs, n, ps in sorted(_IN_SEGS):
            lo, hi = max(rs, IN_SHARD * dev), min(rs + n, IN_SHARD * (dev + 1))
            if lo < hi:
                parts.append(g[:, ps + lo - rs:ps + hi - rs])
        parts.append(jnp.zeros((g.shape[0], IN_WIRE - IN_SHARD), g.dtype))
        out.append(jnp.concatenate(parts, axis=1))
    return jnp.stack(out)


SMALL = {"attn_norm": (0, (1, D_MODEL)), "ffn_norm": (1, (1, D_MODEL)), "dn_out_norm": (2, (1, DN_DIM)),
         "swa_q_norm": (3, (1, SWA_DIM)), "swa_k_norm": (4, (1, SWA_DIM)), "dn_a_log": (5, (1, DN_HEADS)),
         "dn_dt_bias": (6, (1, DN_HEADS)), "swa_sinks": (7, (1, SWA_HEADS)), "rel_bias": (8, (REL_BUCKETS, SWA_HEADS))}
SMALL_SHEET = (48, D_MODEL)


def _small_pack(grads):
    names = list(SMALL)

    def body(*refs):
        o_ref = refs[-1]
        o_ref[...] = jnp.zeros_like(o_ref)
        for n, ref in zip(names, refs):
            r0, (nr, nc) = SMALL[n]
            o_ref[r0:r0 + nr, 0:nc] = ref[...]

    return pl.pallas_call(
        body, in_specs=[pl.BlockSpec(memory_space=pltpu.VMEM)] * len(names),
        out_specs=pl.BlockSpec(memory_space=pltpu.VMEM), out_shape=jax.ShapeDtypeStruct(SMALL_SHEET, F32),
        name="small_pack", compiler_params=_params())(*[grads[n].reshape(SMALL[n][1]) for n in names])


def _small_update(sheets, w, m, v):
    names = list(SMALL)
    k = len(names)

    def body(*refs):
        p_ref = refs[0]
        ins, outs = refs[1:1 + 3 * k], refs[1 + 3 * k:]
        for t, n in enumerate(names):
            r0, (nr, nc) = SMALL[n]
            g = p_ref[0, r0:r0 + nr, 0:nc]
            for i in range(1, N_DEV):
                g = g + p_ref[i, r0:r0 + nr, 0:nc]
            delta, nm, nv = _adamw(ins[t][...], g, ins[k + t][...], ins[2 * k + t][...])
            for kind, val in enumerate((g, delta, nm, nv)):
                outs[kind * k + t][...] = val

    shapes = [jax.ShapeDtypeStruct(SMALL[n][1], F32) for n in names]
    vm = pl.BlockSpec(memory_space=pltpu.VMEM)
    res = pl.pallas_call(
        body, in_specs=[vm] * (1 + 3 * k), out_specs=[vm] * (4 * k), out_shape=shapes * 4, name="adam_small",
        compiler_params=_params(),
    )(sheets, *[d[n].reshape(SMALL[n][1]) for d in (w, m, v) for n in names])
    return {n: tuple(res[kind * k + t] for kind in range(4)) for t, n in enumerate(names)}


def kernel(x, attn_norm, w_in, dn_conv, dn_a_log, dn_dt_bias, dn_out_norm, swa_q_norm, swa_k_norm, swa_sinks, rel_bias, w_branch_dn, w_branch_swa, w_out, ffn_norm, w_gate, w_up, w_down, loss_target, m_attn_norm, m_w_in, m_dn_conv, m_dn_a_log, m_dn_dt_bias, m_dn_out_norm, m_swa_q_norm, m_swa_k_norm, m_swa_sinks, m_rel_bias, m_w_branch_dn, m_w_branch_swa, m_w_out, m_ffn_norm, m_w_gate, m_w_up, m_w_down, v_attn_norm, v_w_in, v_dn_conv, v_dn_a_log, v_dn_dt_bias, v_dn_out_norm, v_swa_q_norm, v_swa_k_norm, v_swa_sinks, v_rel_bias, v_w_branch_dn, v_w_branch_swa, v_w_out, v_ffn_norm, v_w_gate, v_w_up, v_w_down):
    args = dict(locals())
    S = x.shape[1]
    xs = x.reshape(S, D_MODEL)
    target = loss_target.reshape(S, D_MODEL)

    w_loc = {n: args[n].reshape(BIG_SHAPES[n][0]) for n in BIG}
    conv_loc = dn_conv.reshape(CONV_SHARD)
    wire = {n: _pad_to(w_loc[n], BIG_SHAPES[n][1]).astype(BF16) for n in BIG}
    first = _all_gather([wire["w_in"], _pad_to(conv_loc, CONV_WIRE)])
    later = [n for n in BIG if n != "w_in"]
    rest_handle, rest_token = _exchange_start([wire[n] for n in later], False, "gather_rest_start", after=first[1])
    w_pad = _w_in_from_blocks(first[0])
    conv_w = jnp.concatenate([first[1][d, :DN_CONV, :CONV_SHARD[1]] for d in range(N_DEV)], axis=1)

    h = _norm_fwd(xs, attn_norm + rest_token[0, 0], "norm1_fwd")
    proj = _mm([(h, w_pad)], "nn", F32, "mm_in", 512, 1664, j_outer=True)
    qkvn = _dn_conv_fwd(proj, conv_w)
    beta, g = _dn_gate_fwd(proj, dn_a_log, dn_dt_bias)
    u, w, qe, kd, qk, egl, tinv = _dn_prep_fwd(qkvn, g, beta)
    o, states = _dn_scan_fwd(u, w, qe, kd, qk, egl)
    y_dn = _dn_out_fwd(o, proj, dn_out_norm)
    bias = _bias_fwd(rel_bias)
    y_swa = _swa_fwd(proj, swa_q_norm, swa_k_norm, swa_sinks, bias)
    rest_src, rest_land = _exchange_wait(rest_handle, y_swa, False, "gather_rest_wait")
    G = {n: _own_slot(land, src) for n, src, land in zip(later, rest_src, rest_land)}
    w_bdn, w_bswa, w_g, w_u = G["w_branch_dn"], G["w_branch_swa"], G["w_gate"], G["w_up"]
    w_o = G["w_out"].reshape(D_MODEL, D_MODEL)
    w_d = G["w_down"].reshape(D_FFP, D_MODEL)
    gates = [(proj, P_GATE // 512), (proj, (P_GATE + D_MODEL) // 512)]
    a_dn, a_swa, merged = _mm_fused(
        [(y_dn, w_bdn), (y_swa, w_bswa)], "nn", "mm_branch_merge", 1024, 512,
        lambda p, e: (p[0], p[1], _merge(e[0], e[1], p[0], p[1])), gates, (F32, F32, BF16), b_blocks=True)

    def resid_norm(p, e):
        x1 = e[0] + p[0]
        return x1, _rms(x1, e[1])

    x1, h2 = _mm_fused([(merged, w_o)], "nn", "mm_out_norm", 512, D_MODEL, resid_norm,
                       [(xs, 0), (ffn_norm, None)], (F32, BF16))
    gate, up, act = _mm_fused([(h2, w_g), (h2, w_u)], "nn", "mm_gate_up_act", 1024, 768,
                              lambda p, e: (p[0], p[1], _act(p[0], p[1])), [], (BF16, BF16, BF16),
                              j_outer=True, b_blocks=True)

    def loss_head(p, e):
        diff = e[0] + p[0] - e[1]
        dy = diff * (1.0 / D_MODEL)
        part = jnp.sum(jnp.mean(diff * diff, axis=-1, keepdims=True), axis=0, keepdims=True) * 0.5
        return dy, dy, part

    dy, dy_b, loss_local = _mm_fused([(act, w_d)], "nn", "mm_down_loss", 512, D_MODEL, loss_head,
                                     [(x1, 0), (target, 0)], (F32, BF16), sum_shape=(1, 1))

    def act_bwd(p, e):
        _, vjp = jax.vjp(_act, e[0].astype(F32), e[1].astype(F32))
        return vjp(p[0])

    dgate, dup = _mm_fused([(dy_b, w_d)], "nt", "mm_dact_act", 1024, 768, act_bwd, [(gate, 0), (up, 0)],
                           (BF16, BF16), j_outer=True)
    g_w_down = _mm([(act, dy_b)], "tn", BF16, "mm_dw_down", 768, D_MODEL, j_outer=True)
    g_w_down = g_w_down.reshape(N_DEV, FF_WIRE, D_MODEL)
    g_w_gate = _mm([(h2, dgate)], "tn", BF16, "mm_dw_gate", D_MODEL, 768, out_blocks=True)
    g_w_up = _mm([(h2, dup)], "tn", BF16, "mm_dw_up", D_MODEL, 768, out_blocks=True)
    ffn_handle, ffn_token = _exchange_start([g_w_down, g_w_gate, g_w_up], True, "scatter_ffn_start")

    def norm_bwd(p, e):
        _, vjp = jax.vjp(_rms, e[0], e[2])
        dx, dgain = vjp(sum(p))
        dx = dx + e[1]
        return dx, dx, dgain

    dx1, dx1_b, g_ffn_norm = _mm_fused(
        [(dgate, w_g), (dup, w_u)], "nt", "mm_dh2_norm", 512, D_MODEL, norm_bwd,
        [(x1, 0), (dy, 0), (ffn_norm + ffn_token[0, 0], None)], (F32, BF16), b_blocks=True, sum_shape=(1, D_MODEL))
    def merge_bwd(p, e):
        _, vjp = jax.vjp(_merge, *e)
        dg0, dg1, da_dn, da_swa = vjp(p[0])
        return jnp.concatenate([dg0, dg1], axis=1), da_dn, da_swa

    dproj, da_dn, da_swa = _mm_fused(
        [(dx1_b, w_o)], "nt", "mm_dmerged_merge", 512, D_MODEL, merge_bwd,
        [(proj, P_GATE // D_MODEL), (proj, P_GATE // D_MODEL + 1), (a_dn, 0), (a_swa, 0)], (BF16,) * 3,
        wide_first=(P_WIDTH, 2 * D_MODEL))
    g_w_out = _mm([(merged, dx1_b)], "tn", BF16, "mm_dw_out", 512, D_MODEL, j_outer=True)
    g_w_out = g_w_out.reshape(N_DEV, LANES, D_MODEL)
    dy_dn = _mm([(da_dn, w_bdn)], "nt", F32, "mm_dy_dn", 1024, DN_WIDTH, b_blocks=True)
    dy_swa = _mm([(da_swa, w_bswa)], "nt", F32, "mm_dy_swa", 1024, SWA_WIDTH, b_blocks=True)
    g_w_bdn = _mm([(y_dn, da_dn)], "tn", BF16, "mm_dw_branch_dn", DN_WIDTH, 512, out_blocks=True)
    g_w_bswa = _mm([(y_swa, da_swa)], "tn", BF16, "mm_dw_branch_swa", SWA_WIDTH, 512, out_blocks=True)
    dproj, dsk, dsv, g_q_norm, g_k_norm, g_sinks, dbias = _swa_bwd(proj, swa_q_norm, swa_k_norm, swa_sinks, bias,
                                                                   dy_swa, dproj)
    dproj = _kv_into(dproj, dsk, dsv)
    g_rel_bias = _bias_bwd(dbias)[:, :REL_BUCKETS].T
    mix_handle, mix_token = _exchange_start([g_w_out, g_w_bdn, g_w_bswa], True, "scatter_mix_start")
    do, dproj, g_out_norm = _dn_out_bwd(o, proj, dn_out_norm + mix_token[0, 0], dy_dn, dproj)
    du, dw, dqe, dkd, dqk, degl = _dn_scan_bwd(u, w, qe, kd, qk, egl, states, do)
    dqkvn, dgd, dbeta = _dn_prep_bwd(qkvn, g, beta, tinv, du, dw, dqe, dkd, dqk, degl)
    dproj, dal, ddt = _dn_gate_bwd(proj, dn_a_log, dn_dt_bias, dbeta, dgd, dproj)
    g_a_log = dal.reshape(DN_HEADS, DN_DIM).sum(axis=1)
    g_dt_bias = ddt.reshape(DN_HEADS, DN_DIM).sum(axis=1)
    dproj, g_conv = _dn_conv_bwd(proj, conv_w, dqkvn, dproj)
    g_w_in = _w_in_to_blocks(_mm([(h, dproj)], "tn", BF16, "mm_dw_in", 512, 1664, j_outer=True))
    in_handle, in_token = _exchange_start([g_w_in], True, "scatter_in_start")
    dx, g_attn_norm = _mm_fused(
        [(dproj, w_pad)], "nt", "mm_dh_norm", 512, D_MODEL, lambda p, e: norm_bwd(p, e)[1:],
        [(xs, 0), (dx1, 0), (attn_norm + in_token[0, 0], None)], (F32,), sum_shape=(1, D_MODEL))

    g_small = {"attn_norm": g_attn_norm, "ffn_norm": g_ffn_norm, "rel_bias": g_rel_bias, "dn_out_norm": g_out_norm,
               "swa_q_norm": g_q_norm, "swa_k_norm": g_k_norm, "dn_a_log": g_a_log, "dn_dt_bias": g_dt_bias,
               "swa_sinks": g_sinks}
    me = 4 * lax.axis_index("x") + 2 * lax.axis_index("y") + lax.axis_index("c")
    outs = {}

    def finish(handle, group, name, after):
        srcs, lands = _exchange_wait(handle, after, True, name)
        for n, src, land in zip(group, srcs, lands):
            parts = _own_slot(land, lax.dynamic_index_in_dim(src, me, 0, keepdims=False))
            outs[n] = _adam_update(parts, args[n], args["m_" + n], args["v_" + n], "adam_" + n)

    finish(ffn_handle, ("w_down", "w_gate", "w_up"), "scatter_ffn_wait", dx)
    finish(mix_handle, ("w_out", "w_branch_dn", "w_branch_swa"), "scatter_mix_wait", dx)
    sheets, conv_all = _all_gather_direct([_small_pack(g_small), _pad_to(g_conv, (8, DN_QKV))], "all_gather_small",
                                          after=outs["w_up"][0])
    finish(in_handle, ("w_in",), "scatter_in_wait", sheets)
    conv_parts = lax.dynamic_slice(conv_all, (0, 0, me * CONV_SHARD[1]), (N_DEV,) + CONV_SHARD)
    outs["dn_conv"] = _adam_update(conv_parts, dn_conv, m_dn_conv, v_dn_conv, "adam_dn_conv")
    outs.update(_small_update(sheets, {n: args[n] for n in SMALL}, {n: args["m_" + n] for n in SMALL},
                              {n: args["v_" + n] for n in SMALL}))

    names = ("attn_norm", "w_in", "dn_conv", "dn_a_log", "dn_dt_bias", "dn_out_norm", "swa_q_norm", "swa_k_norm",
             "swa_sinks", "rel_bias", "w_branch_dn", "w_branch_swa", "w_out", "ffn_norm", "w_gate", "w_up", "w_down")
    results = []
    for kind in range(4):
        results += [outs[n][kind].reshape(args[n].shape) for n in names]

    loss = lax.psum(loss_local[0, 0], ("x", "y", "c"))
    return (loss, dx.reshape(x.shape), *results)
```

```python
import math

import numpy as np
import jax
import jax.numpy as jnp
from jax import lax
from jax.experimental import pallas as pl
from jax.experimental.pallas import tpu as pltpu

F32 = jnp.float32
BF16 = jnp.bfloat16
HI = lax.Precision.HIGHEST

D_MODEL = 1024
DN_HEADS = 4
DN_DIM = 128
DN_WIDTH = 512
DN_QKV = 1536
DN_CONV = 4
CHUNK = 64
SWA_HEADS = 8
SWA_KV = 2
SWA_GROUP = 4
SWA_DIM = 64
SWA_WIDTH = 512
SWA_KVW = 128
WINDOW = 128
BLOCK = 128
REL_BUCKETS = 32
REL_MAX_DIST = 128
D_FF = 2816
D_IN = 4872
EPS = 1e-6
N_DEV = 8

ADAM_LR = 0.001
ADAM_B1 = 0.9
ADAM_B2 = 0.999
ADAM_EPS = 1e-08
ADAM_WD = 0.01
ADAM_STEP = 10

P_GATE, P_QKV, P_Z, P_SQ, P_SK, P_SV, P_BA = 0, 2048, 3584, 4096, 4608, 4736, 4864
P_WIDTH = 4992
R_QKV, R_Z, R_B, R_A, R_SQ, R_SK, R_SV, R_GATE = 0, 1536, 2048, 2052, 2056, 2568, 2696, 2824

VMEM_LIMIT = 56 * 1024 * 1024
LANES = 128
MESH_ID = pl.DeviceIdType.MESH


def _params(sem=None):
    return pltpu.CompilerParams(dimension_semantics=sem, vmem_limit_bytes=VMEM_LIMIT)


def _pick(dim, target):
    if dim <= target:
        return dim
    t = target - target % LANES
    while t >= LANES:
        if dim % t == 0:
            return t
        t -= LANES
    return dim


_DIMS = {"nn": (((1,), (0,)), ((), ())), "nt": (((1,), (1,)), ((), ())), "tn": (((0,), (0,)), ((), ()))}


def _tile_product(a_ref, b_ref, mode, b_blocks):
    a = a_ref[...].astype(BF16)
    b = jnp.concatenate([b_ref[d] for d in range(b_ref.shape[0])], axis=1) if b_blocks else b_ref[...]
    return lax.dot_general(a, b.astype(BF16), _DIMS[mode], preferred_element_type=F32)


def _mm(pairs, mode, out_dtype, name, bm, bn, j_outer=False, b_blocks=False, out_blocks=False):
    a0, b0 = pairs[0]
    cb = b0.shape[2] if b_blocks else None
    b_shape = (b0.shape[1], N_DEV * cb) if b_blocks else b0.shape
    if mode == "nn":
        (M, K), (K2, N) = a0.shape, b_shape
    elif mode == "nt":
        (M, K), (N, K2) = a0.shape, b_shape
    else:
        (K, M), (K2, N) = a0.shape, b_shape
    bm, bn = min(bm, M), min(bn, N)
    assert K == K2 and M % bm == 0 and N % bn == 0, (name, a0.shape, b0.shape, bm, bn)
    co = N // N_DEV
    assert not out_blocks or bn % co == 0
    dims = _DIMS[mode]
    n = len(pairs)

    def body(*refs):
        o_ref = refs[2 * n]
        acc = None
        for t in range(n):
            p = _tile_product(refs[2 * t], refs[2 * t + 1], mode, b_blocks)
            acc = p if acc is None else acc + p
        if out_blocks:
            for d in range(bn // co):
                o_ref[d] = acc[:, d * co:(d + 1) * co].astype(out_dtype)
        else:
            o_ref[...] = acc.astype(out_dtype)

    def ij(f):
        return (lambda j, i: f(i, j)) if j_outer else f

    a_spec = pl.BlockSpec((K, bm), ij(lambda i, j: (0, i))) if mode == "tn" else pl.BlockSpec((bm, K), ij(lambda i, j: (i, 0)))
    if b_blocks and mode == "nt":
        b_spec = pl.BlockSpec((N_DEV, bn, cb), ij(lambda i, j: (0, j, 0)))
    elif b_blocks:
        b_spec = pl.BlockSpec((bn // cb, K, cb), ij(lambda i, j: (j, 0, 0)))
    elif mode == "nt":
        b_spec = pl.BlockSpec((bn, K), ij(lambda i, j: (j, 0)))
    else:
        b_spec = pl.BlockSpec((K, bn), ij(lambda i, j: (0, j)))
    if out_blocks:
        out_spec = pl.BlockSpec((bn // co, bm, co), ij(lambda i, j: (j, i, 0)))
        out_shape = jax.ShapeDtypeStruct((N_DEV, M, co), out_dtype)
    else:
        out_spec = pl.BlockSpec((bm, bn), ij(lambda i, j: (i, j)))
        out_shape = jax.ShapeDtypeStruct((M, N), out_dtype)
    grid = (N // bn, M // bm) if j_outer else (M // bm, N // bn)
    return pl.pallas_call(
        body, grid=grid, in_specs=[a_spec, b_spec] * n, out_specs=out_spec, out_shape=out_shape, name=name,
        compiler_params=_params(("parallel", "parallel")),
    )(*[x for pair in pairs for x in pair])


def _mm_fused(pairs, mode, name, bm, bn, epilogue, extras, out_dtypes, j_outer=False, b_blocks=False,
              sum_shape=None, wide_first=None):
    a0, b0 = pairs[0]
    cb = b0.shape[2] if b_blocks else None
    b_shape = (b0.shape[1], N_DEV * cb) if b_blocks else b0.shape
    if mode == "nn":
        (M, K), (K2, N) = a0.shape, b_shape
    else:
        (M, K), (N, K2) = a0.shape, b_shape
    bm, bn = min(bm, M), min(bn, N)
    assert mode in ("nn", "nt") and K == K2 and M % bm == 0 and N % bn == 0, (name, a0.shape, b0.shape)
    dims = _DIMS[mode]
    n, ne, no = len(pairs), len(extras), len(out_dtypes)

    def body(*refs):
        prods = [_tile_product(refs[2 * t], refs[2 * t + 1], mode, b_blocks) for t in range(n)]
        results = epilogue(prods, [r[...] for r in refs[2 * n:2 * n + ne]])
        out_refs = refs[2 * n + ne:]
        for o_ref, val, dt in zip(out_refs, results, out_dtypes):
            o_ref[...] = val.astype(dt)
        if sum_shape is not None:
            s_ref = out_refs[no]

            @pl.when((pl.program_id(0) == 0) & (pl.program_id(1) == 0))
            def _():
                s_ref[...] = jnp.zeros_like(s_ref)

            s_ref[...] += results[no]

    def ij(f):
        return (lambda j, i: f(i, j)) if j_outer else f

    a_spec = pl.BlockSpec((bm, K), ij(lambda i, j: (i, 0)))
    once = dict(pipeline_mode=pl.Buffered(1)) if bn == N else {}
    if b_blocks and mode == "nt":
        b_spec = pl.BlockSpec((N_DEV, bn, cb), ij(lambda i, j: (0, j, 0)), **once)
    elif b_blocks:
        b_spec = pl.BlockSpec((bn // cb, K, cb), ij(lambda i, j: (j, 0, 0)), **once)
    elif mode == "nt":
        b_spec = pl.BlockSpec((bn, K), ij(lambda i, j: (j, 0)), **once)
    else:
        b_spec = pl.BlockSpec((K, bn), ij(lambda i, j: (0, j)), **once)
    e_specs = [pl.BlockSpec((1, bn), ij(lambda i, j: (0, j))) if first is None
               else pl.BlockSpec((bm, bn), ij(lambda i, j, first=first: (i, first + j))) for _, first in extras]
    tile = pl.BlockSpec((bm, bn), ij(lambda i, j: (i, j)))
    out_specs = [tile] * no
    out_shape = [jax.ShapeDtypeStruct((M, N), dt) for dt in out_dtypes]
    if wide_first is not None:
        assert bn == N
        out_specs[0] = pl.BlockSpec((bm, wide_first[1]), ij(lambda i, j: (i, 0)))
        out_shape[0] = jax.ShapeDtypeStruct((M, wide_first[0]), out_dtypes[0])
    if sum_shape is not None:
        assert sum_shape[1] in (1, bn) and (sum_shape[1] == 1 or bn == N)
        out_specs.append(_full(sum_shape))
        out_shape.append(jax.ShapeDtypeStruct(sum_shape, F32))
    grid = (N // bn, M // bm) if j_outer else (M // bm, N // bn)
    sem = ("arbitrary", "arbitrary") if sum_shape is not None else ("parallel", "parallel")
    return pl.pallas_call(
        body, grid=grid, in_specs=[a_spec, b_spec] * n + e_specs, out_specs=out_specs, out_shape=out_shape,
        name=name, compiler_params=_params(sem),
    )(*[x for pair in pairs for x in pair], *[arr for arr, _ in extras])


def _rms(x, gain):
    return x * lax.rsqrt(jnp.mean(x * x, axis=-1, keepdims=True) + EPS) * gain


def _silu(x):
    return x * jax.nn.sigmoid(x)


def _act(g, u):
    return _silu(g) * u


def _merge(g0, g1, a_dn, a_swa):
    return jax.nn.sigmoid(g0) * a_dn + jax.nn.sigmoid(g1) * a_swa


def _dn_post(c, is_v, q_scale):
    a = _silu(c)
    rs = lax.rsqrt(jnp.sum(a * a, axis=-1, keepdims=True) + EPS) * q_scale
    return a * jnp.where(is_v, 1.0, rs)


def _dn_out(o, z, gain):
    return _rms(o, gain) * _silu(z)


def _dot(a, b, dims=_DIMS["nn"], hi=False):
    if a.ndim == 3 or b.ndim == 3:
        batch = a.shape[0] if a.ndim == 3 else b.shape[0]
        a = a if a.ndim == 3 else jnp.broadcast_to(a, (batch,) + a.shape)
        b = b if b.ndim == 3 else jnp.broadcast_to(b, (batch,) + b.shape)
        ((ca,), (cb,)), _ = dims
        dims = (((ca + 1,), (cb + 1,)), ((0,), (0,)))
    if hi:
        return lax.dot_general(a, b, dims, precision=HI, preferred_element_type=F32)
    return lax.dot_general(a.astype(BF16), b.astype(BF16), dims, preferred_element_type=F32)


def _pieces(x):
    hi = x.astype(BF16)
    r1 = x - hi.astype(F32)
    mid = r1.astype(BF16)
    return hi, mid, (r1 - mid.astype(F32)).astype(BF16)


def _sel_left_impl(m, x):
    mb = m.astype(BF16)
    hi, mid, lo = _pieces(x)
    return _dot(mb, hi) + (_dot(mb, mid) + _dot(mb, lo))


@jax.custom_vjp
def _sel_left(m, mt, x):
    return _sel_left_impl(m, x)


_sel_left.defvjp(lambda m, mt, x: (_sel_left_impl(m, x), (m, mt)),
                 lambda res, ct: (jnp.zeros_like(res[0]), jnp.zeros_like(res[1]), _sel_left_impl(res[1], ct)))


def _sel_right_impl(x, s):
    sb = s.astype(BF16)
    hi, mid, lo = _pieces(x)
    return _dot(hi, sb) + (_dot(mid, sb) + _dot(lo, sb))


@jax.custom_vjp
def _sel_right(x, s, st):
    return _sel_right_impl(x, s)


_sel_right.defvjp(lambda x, s, st: (_sel_right_impl(x, s), (s, st)),
                  lambda res, ct: (_sel_right_impl(ct, res[1]), jnp.zeros_like(res[0]), jnp.zeros_like(res[1])))


def _sel_nt_impl(s, x):
    sb = s.astype(BF16)
    hi, mid, lo = _pieces(x)
    return _dot(sb, hi, _DIMS["nt"]) + (_dot(sb, mid, _DIMS["nt"]) + _dot(sb, lo, _DIMS["nt"]))


def _sel_tn_impl(x, s):
    sb = s.astype(BF16)
    hi, mid, lo = _pieces(x)
    return _dot(hi, sb, _DIMS["tn"]) + (_dot(mid, sb, _DIMS["tn"]) + _dot(lo, sb, _DIMS["tn"]))


@jax.custom_vjp
def _sel_nt(s, x):
    return _sel_nt_impl(s, x)


_sel_nt.defvjp(lambda s, x: (_sel_nt_impl(s, x), s),
               lambda s, ct: (jnp.zeros_like(s), _sel_tn_impl(ct, s)))


def _dot3_impl(a, b):
    a_hi, a_lo, _ = _pieces(a)
    b_hi, b_lo, _ = _pieces(b)
    return _dot(a_hi, b_hi) + (_dot(a_hi, b_lo) + _dot(a_lo, b_hi))


@jax.custom_vjp
def _dot3(a, b):
    return _dot3_impl(a, b)


_dot3.defvjp(lambda a, b: (_dot3_impl(a, b), (a, b)),
             lambda res, ct: (_dot(ct, res[1], _DIMS["nt"]), _dot(res[0], ct, _DIMS["tn"])))


def _inv_impl(a, eye, strict):
    t = eye - a
    p = _dot(a, a)
    for level in range(5):
        t = t + _dot(t, p)
        if level < 4:
            p = _dot(p, p)
    t = t + _dot(t, eye - t - _dot3_impl(a, t))
    return jnp.where(strict > 0.5, t, eye)


@jax.custom_vjp
def _inv_given(a, t):
    return t.astype(F32)


_inv_given.defvjp(lambda a, t: (t.astype(F32), t),
                  lambda t, ct: (-_dot(_dot(t, ct, _DIMS["tn"]), t, _DIMS["nt"]), jnp.zeros_like(t)))


@jax.custom_vjp
def _lanes_join(a, b):
    return jnp.concatenate([a, b], axis=-1)


_lanes_join.defvjp(lambda a, b: (jnp.concatenate([a, b], axis=-1), None),
                   lambda _, ct: (ct[..., :ct.shape[-1] // 2], ct[..., ct.shape[-1] // 2:]))


@jax.custom_vjp
def _lanes_halves(y):
    h = y.shape[-1] // 2
    return y[..., :h], y[..., h:]


_lanes_halves.defvjp(lambda y: ((y[..., :y.shape[-1] // 2], y[..., y.shape[-1] // 2:]), None),
                     lambda _, ct: (jnp.concatenate(ct, axis=-1),))

GROUP = 4
GROUP_ROWS = GROUP * CHUNK


def _block_consts(n):
    ii = lax.broadcasted_iota(jnp.int32, (n, n), 0)
    jj = lax.broadcasted_iota(jnp.int32, (n, n), 1)
    shift = CHUNK.bit_length() - 1
    same = jnp.right_shift(ii, shift) == jnp.right_shift(jj, shift)
    return same & (ii >= jj), same & (ii <= jj), same & (ii > jj), same, ii == jj


def _lane0(n):
    s = (lax.broadcasted_iota(jnp.int32, (LANES, n), 0) == 0).astype(F32)
    st = (lax.broadcasted_iota(jnp.int32, (n, LANES), 1) == 0).astype(F32)
    return s, st


def _dn_group(q, k, v, g, beta, t_saved=None):
    n = GROUP_ROWS
    low_b, upp_b, strict_b, _, eye_b = _block_consts(n)
    low, upp, eye = low_b.astype(F32), upp_b.astype(F32), eye_b.astype(F32)
    s, st = _lane0(n)
    gc = _sel_left(low, upp, g)
    per_chunk = (g.shape[0], GROUP, CHUNK, LANES)
    gl = jnp.broadcast_to(jnp.sum(g.reshape(per_chunk), axis=2, keepdims=True), per_chunk).reshape(g.shape)
    col = _sel_right(gc, s, st)
    row = _sel_nt(st, gc)
    decay = jnp.exp(jnp.where(low_b, col - row, -jnp.inf))
    kb = k * beta
    vb = v * beta
    a = jnp.where(strict_b, _dot(kb, k, _DIMS["nt"]) * decay, 0.0)
    t = _inv_impl(a, eye, strict_b.astype(F32)) if t_saved is None else _inv_given(a, t_saved)
    u, w = _lanes_halves(_dot3(t, _lanes_join(vb, kb * jnp.exp(gc))))
    return u, w, q * jnp.exp(gc), k * jnp.exp(gl - gc), t


def _dn_chunk(q, k, g):
    ii = lax.broadcasted_iota(jnp.int32, (CHUNK, CHUNK), 0)
    jj = lax.broadcasted_iota(jnp.int32, (CHUNK, CHUNK), 1)
    low = (ii >= jj).astype(F32)
    upp = (ii <= jj).astype(F32)
    s, st = _lane0(CHUNK)
    gc = _sel_left(low, upp, g)
    col = _sel_right(gc, s, st)
    row = _sel_nt(st, gc)
    decay = jnp.exp(jnp.where(ii >= jj, col - row, -jnp.inf))
    qk = _dot(q, k, _DIMS["nt"]) * decay
    return qk, jnp.exp(jnp.sum(g, axis=-2, keepdims=True))


def _dn_fold(u, w, qe, kd, qk):
    return qe - _dot(qk, w), _dot(qk, u), _dot(kd, w, _DIMS["tn"]), _dot(kd, u, _DIMS["tn"])


def _dn_prep_all(q, k, v, g, beta, t_saved=None):
    u, w, qe, kd, t = _dn_group(q, k, v, g, beta, t_saved)
    chk = (q.shape[0] * GROUP, CHUNK, LANES)
    qk, egl = _dn_chunk(q.reshape(chk), k.reshape(chk), g.reshape(chk))
    qp, op, mm, nn = _dn_fold(u.reshape(chk), w.reshape(chk), qe.reshape(chk), kd.reshape(chk), qk)
    return qp, op, mm, nn, egl, t


def _dn_step(s, qp, op, mm, nn, egl):
    o = _dot(qp, s) + op
    s_new = s * egl - _dot(mm, s) + nn
    return s_new, o


def _swa_block(q, kband, vband, qg, kg, sinks, bias, mask):
    kn = _rms(kband, kg)
    qn = _rms(q, qg)
    logits = _dot(qn, kn, _DIMS["nt"]) * (SWA_DIM ** -0.5)
    logits = jnp.where(mask, logits + bias, -jnp.inf)
    m = jnp.maximum(jnp.max(logits, axis=-1, keepdims=True), sinks)
    p = jnp.exp(logits - m)
    denom = jnp.sum(p, axis=-1, keepdims=True) + jnp.exp(sinks - m)
    return _dot(p / denom, vband)


def _adamw(w, g, m, v):
    m = ADAM_B1 * m + (1.0 - ADAM_B1) * g
    v = ADAM_B2 * v + (1.0 - ADAM_B2) * jnp.square(g)
    m_hat = m / (1.0 - ADAM_B1 ** ADAM_STEP)
    v_hat = v / (1.0 - ADAM_B2 ** ADAM_STEP)
    delta = -ADAM_LR * (m_hat / (jnp.sqrt(v_hat) + ADAM_EPS) + ADAM_WD * w)
    return delta, m, v


def _row(tm, c, cb=0):
    return pl.BlockSpec((tm, c), lambda i, cb=cb: (i, cb))


def _full(shape):
    nd = len(shape)
    return pl.BlockSpec(shape, lambda *_, nd=nd: (0,) * nd)


def _norm_fwd(x, gain, name, tm=512):
    S = x.shape[0]

    def body(x_ref, g_ref, h_ref):
        h_ref[...] = _rms(x_ref[...], g_ref[...]).astype(BF16)

    return pl.pallas_call(
        body, grid=(S // tm,), in_specs=[_row(tm, D_MODEL), _full((1, D_MODEL))],
        out_specs=_row(tm, D_MODEL), out_shape=jax.ShapeDtypeStruct((S, D_MODEL), BF16),
        name=name, compiler_params=_params(("parallel",)))(x, gain)


def _shift_down(x, s):
    row = lax.broadcasted_iota(jnp.int32, x.shape, 0)
    return jnp.where(row >= s, pltpu.roll(x, s, axis=0), 0.0)


def _shift_up(x, s):
    n = x.shape[0]
    row = lax.broadcasted_iota(jnp.int32, x.shape, 0)
    return jnp.where(row < n - s, pltpu.roll(x, n - s, axis=0), 0.0)


def _conv(x, w):
    out = w[DN_CONV - 1:DN_CONV] * x
    for s in range(1, DN_CONV):
        out = out + w[DN_CONV - 1 - s:DN_CONV - s] * _shift_down(x, s)
    return out


def _dn_conv_fwd(proj, conv_w):
    S = proj.shape[0]
    nb = DN_QKV // LANES

    def body(x_ref, w_ref, o_ref):
        j = pl.program_id(0)
        q_scale = jnp.where(j < DN_HEADS, DN_DIM ** -0.5, 1.0).astype(F32)
        o_ref[...] = _dn_post(_conv(x_ref[...], w_ref[...]), j >= 2 * DN_HEADS, q_scale)

    return pl.pallas_call(
        body, grid=(nb,),
        in_specs=[pl.BlockSpec((S, LANES), lambda j: (0, P_QKV // LANES + j)),
                  pl.BlockSpec((DN_CONV, LANES), lambda j: (0, j))],
        out_specs=pl.BlockSpec((S, LANES), lambda j: (0, j)),
        out_shape=jax.ShapeDtypeStruct((S, DN_QKV), F32), name="dn_conv_fwd",
        compiler_params=_params(("parallel",)))(proj, conv_w)


def _dn_conv_bwd(proj, conv_w, dqkvn, dproj):
    S = proj.shape[0]
    nb = DN_QKV // LANES

    def body(x_ref, w_ref, d_ref, _, dx_ref, dw_ref):
        j = pl.program_id(0)
        q_scale = jnp.where(j < DN_HEADS, DN_DIM ** -0.5, 1.0).astype(F32)
        x = x_ref[...]
        w = w_ref[...]
        _, vjp = jax.vjp(lambda c: _dn_post(c, j >= 2 * DN_HEADS, q_scale), _conv(x, w))
        (dc,) = vjp(d_ref[0])
        dx = w[DN_CONV - 1:DN_CONV] * dc
        dw_ref[DN_CONV - 1:DN_CONV, :] = jnp.sum(dc * x, axis=0, keepdims=True)
        for s in range(1, DN_CONV):
            dx = dx + w[DN_CONV - 1 - s:DN_CONV - s] * _shift_up(dc, s)
            dw_ref[DN_CONV - 1 - s:DN_CONV - s, :] = jnp.sum(dc * _shift_down(x, s), axis=0, keepdims=True)
        dx_ref[...] = dx.astype(BF16)

    return pl.pallas_call(
        body, grid=(nb,),
        in_specs=[pl.BlockSpec((S, LANES), lambda j: (0, P_QKV // LANES + j)),
                  pl.BlockSpec((DN_CONV, LANES), lambda j: (0, j)),
                  pl.BlockSpec((1, S, LANES), lambda j: (lax.div(j, DN_HEADS), 0, lax.rem(j, DN_HEADS))),
                  pl.BlockSpec(memory_space=pl.ANY)],
        out_specs=[pl.BlockSpec((S, LANES), lambda j: (0, P_QKV // LANES + j)),
                   pl.BlockSpec((DN_CONV, LANES), lambda j: (0, j))],
        out_shape=[jax.ShapeDtypeStruct(dproj.shape, dproj.dtype), jax.ShapeDtypeStruct((DN_CONV, DN_QKV), F32)],
        input_output_aliases={3: 0},
        name="dn_conv_bwd", compiler_params=_params(("parallel",)))(proj, conv_w, dqkvn, dproj)


def _expanders():
    eb = np.zeros((LANES, DN_WIDTH), np.float32)
    ea = np.zeros((LANES, DN_WIDTH), np.float32)
    for h in range(DN_HEADS):
        eb[h, h * DN_DIM:(h + 1) * DN_DIM] = 1.0
        ea[DN_HEADS + h, h * DN_DIM:(h + 1) * DN_DIM] = 1.0
    return jnp.asarray(eb), jnp.asarray(ea), jnp.asarray(eb.T), jnp.asarray(ea.T)


def _dn_gate_args(a_log, dt_bias):
    alog = jnp.repeat(a_log.reshape(1, DN_HEADS), DN_DIM, axis=1)
    dtb = jnp.repeat(dt_bias.reshape(1, DN_HEADS), DN_DIM, axis=1)
    return _expanders() + (alog, dtb)


def _dn_gate_specs(tm):
    return [_row(tm, LANES, P_BA // LANES), _full((LANES, DN_WIDTH)), _full((LANES, DN_WIDTH)),
            _full((DN_WIDTH, LANES)), _full((DN_WIDTH, LANES)), _full((1, DN_WIDTH)), _full((1, DN_WIDTH))]


def _dn_gate_fn(ba, eb, ea, ebt, eat, alog, dtb):
    beta = jax.nn.sigmoid(_sel_right(ba, eb, ebt))
    g = -jnp.exp(alog) * jax.nn.softplus(_sel_right(ba, ea, eat) + dtb)
    return beta, g


def _dn_gate_fwd(proj, a_log, dt_bias, tm=512):
    S = proj.shape[0]
    args = _dn_gate_args(a_log, dt_bias)

    def body(ba_ref, eb_ref, ea_ref, ebt_ref, eat_ref, al_ref, dt_ref, beta_ref, g_ref):
        beta, g = _dn_gate_fn(ba_ref[...], eb_ref[...], ea_ref[...], ebt_ref[...], eat_ref[...], al_ref[...],
                              dt_ref[...])
        beta_ref[...] = beta
        g_ref[...] = g

    return pl.pallas_call(
        body, grid=(S // tm,), in_specs=_dn_gate_specs(tm), out_specs=[_row(tm, DN_WIDTH), _row(tm, DN_WIDTH)],
        out_shape=[jax.ShapeDtypeStruct((S, DN_WIDTH), F32), jax.ShapeDtypeStruct((S, DN_WIDTH), F32)],
        name="dn_gate_fwd", compiler_params=_params(("parallel",)))(proj, *args)


def _dn_gate_bwd(proj, a_log, dt_bias, dbeta, dg, dproj, tm=512):
    S = proj.shape[0]
    args = _dn_gate_args(a_log, dt_bias)

    def body(ba_ref, eb_ref, ea_ref, ebt_ref, eat_ref, al_ref, dt_ref, dbeta_ref, dg_ref, _, dba_ref, dal_ref,
             ddt_ref):
        eb, ea, ebt, eat = eb_ref[...], ea_ref[...], ebt_ref[...], eat_ref[...]
        _, vjp = jax.vjp(lambda ba, al, dt: _dn_gate_fn(ba, eb, ea, ebt, eat, al, dt), ba_ref[...], al_ref[...],
                         dt_ref[...])
        dba, dal, ddt = vjp((dbeta_ref[...], dg_ref[...]))
        dba_ref[...] = dba.astype(BF16)

        @pl.when(pl.program_id(0) == 0)
        def _():
            dal_ref[...] = jnp.zeros_like(dal_ref)
            ddt_ref[...] = jnp.zeros_like(ddt_ref)

        dal_ref[...] += dal
        ddt_ref[...] += ddt

    return pl.pallas_call(
        body, grid=(S // tm,),
        in_specs=_dn_gate_specs(tm) + [_row(tm, DN_WIDTH), _row(tm, DN_WIDTH), pl.BlockSpec(memory_space=pl.ANY)],
        out_specs=[_row(tm, LANES, P_BA // LANES), _full((1, DN_WIDTH)), _full((1, DN_WIDTH))],
        out_shape=[jax.ShapeDtypeStruct(dproj.shape, dproj.dtype), jax.ShapeDtypeStruct((1, DN_WIDTH), F32),
                   jax.ShapeDtypeStruct((1, DN_WIDTH), F32)],
        input_output_aliases={len(args) + 3: 0},
        name="dn_gate_bwd", compiler_params=_params(("arbitrary",)))(proj, *args, dbeta, dg, dproj)


PREP_GROUPS = 4
PREP_CHUNKS = GROUP * PREP_GROUPS


def _dn_prep_specs():
    rows = PREP_CHUNKS * CHUNK
    q = pl.BlockSpec((rows, LANES), lambda h, c: (c, h))
    k = pl.BlockSpec((rows, LANES), lambda h, c: (c, DN_HEADS + h))
    v = pl.BlockSpec((rows, LANES), lambda h, c: (c, 2 * DN_HEADS + h))
    sq = pl.BlockSpec((1, PREP_CHUNKS, DN_DIM, DN_DIM), lambda h, c: (h, c, 0, 0))
    egl = pl.BlockSpec((1, PREP_CHUNKS, 1, LANES), lambda h, c: (h, c, 0, 0))
    tinv = pl.BlockSpec((1, rows, GROUP_ROWS), lambda h, c: (h, c, 0))
    return q, k, v, sq, egl, tinv


def _dn_prep_shapes(S):
    nc = S // CHUNK
    wide = jax.ShapeDtypeStruct((S, DN_WIDTH), F32)
    square = jax.ShapeDtypeStruct((DN_HEADS, nc, DN_DIM, DN_DIM), F32)
    return [wide, wide, square, square, jax.ShapeDtypeStruct((DN_HEADS, nc, 1, LANES), F32)]


def _dn_prep_fwd(qkvn, g, beta):
    S = qkvn.shape[0]
    q, k, v, sq, egl, tinv = _dn_prep_specs()

    def body(q_ref, k_ref, v_ref, g_ref, b_ref, qp_ref, op_ref, mm_ref, nn_ref, egl_ref, t_ref):
        rows = PREP_CHUNKS * CHUNK
        grp = (PREP_GROUPS, GROUP_ROWS, LANES)
        qp, op, mm, nn, e, t = _dn_prep_all(q_ref[...].reshape(grp), k_ref[...].reshape(grp), v_ref[...].reshape(grp),
                                            g_ref[...].reshape(grp), b_ref[...].reshape(grp))
        qp_ref[...] = qp.reshape(rows, LANES)
        op_ref[...] = op.reshape(rows, LANES)
        mm_ref[0] = mm
        nn_ref[0] = nn
        egl_ref[0] = e
        t_ref[0] = t.reshape(rows, GROUP_ROWS).astype(BF16)

    return pl.pallas_call(
        body, grid=(DN_HEADS, S // (PREP_CHUNKS * CHUNK)), in_specs=[q, k, v, q, q],
        out_specs=[q, q, sq, sq, egl, tinv],
        out_shape=_dn_prep_shapes(S) + [jax.ShapeDtypeStruct((DN_HEADS, S, GROUP_ROWS), BF16)],
        name="dn_prep_fwd", compiler_params=_params(("parallel", "parallel")))(qkvn, qkvn, qkvn, g, beta)


def _dn_prep_bwd(qkvn, g, beta, tinv, dqp, dop, dmm, dnn, degl):
    S = qkvn.shape[0]
    q, k, v, sq, egl, tinv_spec = _dn_prep_specs()

    def body(q_ref, k_ref, v_ref, g_ref, b_ref, t_ref, dqp_ref, dop_ref, dmm_ref, dnn_ref, degl_ref,
             dqkv_ref, dg_ref, db_ref):
        rows = PREP_CHUNKS * CHUNK
        grp = (PREP_GROUPS, GROUP_ROWS, LANES)
        chk = (PREP_CHUNKS, CHUNK, LANES)
        t_saved = t_ref[0].reshape(PREP_GROUPS, GROUP_ROWS, GROUP_ROWS)
        _, vjp = jax.vjp(lambda *x: _dn_prep_all(*x, t_saved=t_saved)[:5], q_ref[...].reshape(grp),
                         k_ref[...].reshape(grp), v_ref[...].reshape(grp), g_ref[...].reshape(grp),
                         b_ref[...].reshape(grp))
        dq, dk, dv, dg, db = vjp((dqp_ref[...].reshape(chk), dop_ref[...].reshape(chk), dmm_ref[0], dnn_ref[0],
                                  degl_ref[0]))
        dqkv_ref[0] = dq.reshape(rows, LANES)
        dqkv_ref[1] = dk.reshape(rows, LANES)
        dqkv_ref[2] = dv.reshape(rows, LANES)
        dg_ref[...] = dg.reshape(rows, LANES)
        db_ref[...] = db.reshape(rows, LANES)

    wide = jax.ShapeDtypeStruct((S, DN_WIDTH), F32)
    rows = PREP_CHUNKS * CHUNK
    return pl.pallas_call(
        body, grid=(DN_HEADS, S // rows), in_specs=[q, k, v, q, q, tinv_spec, q, q, sq, sq, egl],
        out_specs=[pl.BlockSpec((3, rows, LANES), lambda h, c: (0, c, h)), q, q],
        out_shape=[jax.ShapeDtypeStruct((3, S, DN_WIDTH), F32), wide, wide],
        name="dn_prep_bwd", compiler_params=_params(("parallel", "parallel")),
    )(qkvn, qkvn, qkvn, g, beta, tinv, dqp, dop, dmm, dnn, degl)


def _dn_scan_specs(nc, reverse):
    def cidx(c):
        return nc - 1 - c if reverse else c

    hc = pl.BlockSpec((CHUNK, DN_WIDTH), lambda c: (cidx(c), 0))
    egl = pl.BlockSpec((DN_HEADS, 1, 1, LANES), lambda c: (0, cidx(c), 0, 0))
    sq = pl.BlockSpec((DN_HEADS, 1, DN_DIM, DN_DIM), lambda c: (0, cidx(c), 0, 0))
    return hc, egl, sq


def _heads(ref):
    return jnp.stack([ref[:, pl.ds(h * DN_DIM, DN_DIM)] for h in range(DN_HEADS)])


def _dn_scan_fwd(qp, op, mm, nn, egl):
    S = qp.shape[0]
    nc = S // CHUNK
    hc, egls, sq = _dn_scan_specs(nc, False)

    def body(qp_ref, op_ref, mm_ref, nn_ref, egl_ref, o_ref, st_ref, s_scr):
        @pl.when(pl.program_id(0) == 0)
        def _():
            s_scr[...] = jnp.zeros_like(s_scr)

        s = s_scr[...]
        st_ref[:, 0] = s
        s_new, o = _dn_step(s, _heads(qp_ref), _heads(op_ref), mm_ref[:, 0], nn_ref[:, 0], egl_ref[:, 0])
        for h in range(DN_HEADS):
            o_ref[:, pl.ds(h * DN_DIM, DN_DIM)] = o[h]
        s_scr[...] = s_new

    return pl.pallas_call(
        body, grid=(nc,), in_specs=[hc, hc, sq, sq, egls], out_specs=[hc, sq],
        out_shape=[jax.ShapeDtypeStruct((S, DN_WIDTH), F32), jax.ShapeDtypeStruct((DN_HEADS, nc, DN_DIM, DN_DIM), F32)],
        scratch_shapes=[pltpu.VMEM((DN_HEADS, DN_DIM, DN_DIM), F32)], name="dn_scan_fwd",
        compiler_params=_params(("arbitrary",)))(qp, op, mm, nn, egl)


def _dn_scan_bwd(qp, op, mm, nn, egl, states, do):
    S = qp.shape[0]
    nc = S // CHUNK
    hc, egls, sq = _dn_scan_specs(nc, True)

    def body(qp_ref, op_ref, mm_ref, nn_ref, egl_ref, st_ref, do_ref, dqp_ref, dmm_ref, dnn_ref, degl_ref, ds_scr):
        @pl.when(pl.program_id(0) == 0)
        def _():
            ds_scr[...] = jnp.zeros_like(ds_scr)

        _, vjp = jax.vjp(_dn_step, st_ref[:, 0], _heads(qp_ref), _heads(op_ref), mm_ref[:, 0], nn_ref[:, 0],
                         egl_ref[:, 0])
        ds, dqp, _, dmm, dnn, degl = vjp((ds_scr[...], _heads(do_ref)))
        ds_scr[...] = ds
        dmm_ref[:, 0] = dmm
        dnn_ref[:, 0] = dnn
        degl_ref[:, 0] = degl
        for h in range(DN_HEADS):
            dqp_ref[:, pl.ds(h * DN_DIM, DN_DIM)] = dqp[h]

    shapes = _dn_prep_shapes(S)
    return pl.pallas_call(
        body, grid=(nc,), in_specs=[hc, hc, sq, sq, egls, sq, hc], out_specs=[hc, sq, sq, egls],
        out_shape=[shapes[0], shapes[2], shapes[3], shapes[4]],
        scratch_shapes=[pltpu.VMEM((DN_HEADS, DN_DIM, DN_DIM), F32)], name="dn_scan_bwd",
        compiler_params=_params(("arbitrary",)))(qp, op, mm, nn, egl, states, do)


def _dn_out_fwd(o, proj, gain, tm=512):
    S = o.shape[0]

    def body(o_ref, z_ref, g_ref, y_ref):
        y_ref[...] = _dn_out(o_ref[...], z_ref[...], g_ref[...]).astype(BF16)

    hs = pl.BlockSpec((tm, LANES), lambda i, h: (i, h))
    zs = pl.BlockSpec((tm, LANES), lambda i, h: (i, P_Z // LANES + h))
    return pl.pallas_call(
        body, grid=(S // tm, DN_HEADS), in_specs=[hs, zs, _full((1, DN_DIM))], out_specs=hs,
        out_shape=jax.ShapeDtypeStruct((S, DN_WIDTH), BF16), name="dn_out_fwd",
        compiler_params=_params(("parallel", "parallel")))(o, proj, gain)


_ANY = pl.BlockSpec(memory_space=pl.ANY)


def _dn_out_bwd(o, proj, gain, dy, dproj, tm=512):
    S = o.shape[0]

    def body(o_ref, z_ref, g_ref, dy_ref, _, do_ref, dz_ref, dg_ref):
        _, vjp = jax.vjp(_dn_out, o_ref[...], z_ref[...], g_ref[...])
        do, dz, dg = vjp(dy_ref[...])
        do_ref[...] = do
        dz_ref[...] = dz.astype(BF16)

        @pl.when((pl.program_id(0) == 0) & (pl.program_id(1) == 0))
        def _():
            dg_ref[...] = jnp.zeros_like(dg_ref)

        dg_ref[...] += dg

    hs = pl.BlockSpec((tm, LANES), lambda i, h: (i, h))
    zs = pl.BlockSpec((tm, LANES), lambda i, h: (i, P_Z // LANES + h))
    return pl.pallas_call(
        body, grid=(S // tm, DN_HEADS), in_specs=[hs, zs, _full((1, DN_DIM)), hs, _ANY],
        out_specs=[hs, zs, _full((1, DN_DIM))],
        out_shape=[jax.ShapeDtypeStruct((S, DN_WIDTH), F32), jax.ShapeDtypeStruct(dproj.shape, dproj.dtype),
                   jax.ShapeDtypeStruct((1, DN_DIM), F32)],
        input_output_aliases={4: 1},
        name="dn_out_bwd", compiler_params=_params(("arbitrary", "arbitrary")))(o, proj, gain, dy, dproj)


def _rel_buckets():
    qi = np.arange(BLOCK)[:, None]
    kj = np.arange(2 * BLOCK)[None, :]
    n = np.maximum(BLOCK + qi - kj, 0)
    max_exact = REL_BUCKETS // 2
    nf = np.maximum(n, 1).astype(np.float32)
    large = max_exact + (np.log(nf / np.float32(max_exact)) / np.float32(math.log(REL_MAX_DIST / max_exact))
                         * np.float32(REL_BUCKETS - max_exact)).astype(np.int32)
    large = np.minimum(large, REL_BUCKETS - 1)
    return np.where(n < max_exact, n, large).astype(np.int32)


def _bias_fwd(rel_bias):
    buckets = jnp.asarray(_rel_buckets())

    def body(rb_ref, bk_ref, o_ref):
        bk = bk_ref[...]
        for h in range(SWA_HEADS):
            acc = jnp.zeros((BLOCK, 2 * BLOCK), F32)
            for b in range(REL_BUCKETS):
                acc = jnp.where(bk == b, rb_ref[b, h], acc)
            o_ref[h] = acc

    return pl.pallas_call(
        body, in_specs=[pl.BlockSpec(memory_space=pltpu.SMEM), pl.BlockSpec(memory_space=pltpu.VMEM)],
        out_specs=pl.BlockSpec(memory_space=pltpu.VMEM),
        out_shape=jax.ShapeDtypeStruct((SWA_HEADS, BLOCK, 2 * BLOCK), F32), name="swa_bias_fwd",
        compiler_params=_params())(rel_bias, buckets)


def _bias_bwd(dbias):
    buckets = jnp.asarray(_rel_buckets())

    def body(d_ref, bk_ref, o_ref):
        bk = bk_ref[...]
        lane = lax.broadcasted_iota(jnp.int32, (1, LANES), 1)
        for h in range(SWA_HEADS):
            d = d_ref[h]
            row = jnp.zeros((1, LANES), F32)
            for b in range(REL_BUCKETS):
                part = jnp.sum(jnp.where(bk == b, d, 0.0), axis=1, keepdims=True)
                row = jnp.where(lane == b, jnp.sum(part, axis=0, keepdims=True), row)
            o_ref[h:h + 1, :] = row

    return pl.pallas_call(
        body, in_specs=[pl.BlockSpec(memory_space=pltpu.VMEM), pl.BlockSpec(memory_space=pltpu.VMEM)],
        out_specs=pl.BlockSpec(memory_space=pltpu.VMEM),
        out_shape=jax.ShapeDtypeStruct((SWA_HEADS, LANES), F32), name="swa_bias_bwd",
        compiler_params=_params())(dbias, buckets)


def _swa_mask(n):
    qi = lax.broadcasted_iota(jnp.int32, (BLOCK, 2 * BLOCK), 0)
    kj = lax.broadcasted_iota(jnp.int32, (BLOCK, 2 * BLOCK), 1)
    dist = BLOCK + qi - kj
    return (dist >= 0) & (dist < WINDOW) & ((n > 0) | (kj >= BLOCK))


def _swa_in_specs():
    q = pl.BlockSpec((BLOCK, SWA_WIDTH), lambda n: (n, P_SQ // SWA_WIDTH))
    kc = pl.BlockSpec((BLOCK, SWA_KVW), lambda n: (n, P_SK // SWA_KVW))
    kp = pl.BlockSpec((BLOCK, SWA_KVW), lambda n: (jnp.maximum(n - 1, 0), P_SK // SWA_KVW))
    vc = pl.BlockSpec((BLOCK, SWA_KVW), lambda n: (n, P_SV // SWA_KVW))
    vp = pl.BlockSpec((BLOCK, SWA_KVW), lambda n: (jnp.maximum(n - 1, 0), P_SV // SWA_KVW))
    small = [_full((1, SWA_DIM)), _full((1, SWA_DIM)), _full((1, SWA_HEADS)),
             _full((SWA_HEADS, BLOCK, 2 * BLOCK))]
    return [q, kp, kc, vp, vc] + small


def _swa_load(q_ref, kp_ref, kc_ref, vp_ref, vc_ref, s_ref):
    q = jnp.stack([q_ref[:, pl.ds(h * SWA_DIM, SWA_DIM)] for h in range(SWA_HEADS)])
    kbands, vbands = [], []
    for kv in range(SWA_KV):
        cols = pl.ds(kv * SWA_DIM, SWA_DIM)
        kbands += [jnp.concatenate([kp_ref[:, cols], kc_ref[:, cols]], axis=0)] * SWA_GROUP
        vbands += [jnp.concatenate([vp_ref[:, cols], vc_ref[:, cols]], axis=0)] * SWA_GROUP
    sinks = jnp.stack([s_ref[:, pl.ds(h, 1)] for h in range(SWA_HEADS)])
    return q, jnp.stack(kbands), jnp.stack(vbands), sinks


def _swa_fwd(proj, q_gain, k_gain, sinks, bias):
    S = proj.shape[0]

    def body(q_ref, kp_ref, kc_ref, vp_ref, vc_ref, qg_ref, kg_ref, s_ref, bias_ref, y_ref):
        mask = _swa_mask(pl.program_id(0))
        q, kband, vband, sk = _swa_load(q_ref, kp_ref, kc_ref, vp_ref, vc_ref, s_ref)
        out = _swa_block(q, kband, vband, qg_ref[...], kg_ref[...], sk, bias_ref[...], mask)
        for h in range(SWA_HEADS):
            y_ref[:, pl.ds(h * SWA_DIM, SWA_DIM)] = out[h].astype(BF16)

    return pl.pallas_call(
        body, grid=(S // BLOCK,), in_specs=_swa_in_specs(),
        out_specs=pl.BlockSpec((BLOCK, SWA_WIDTH), lambda n: (n, 0)),
        out_shape=jax.ShapeDtypeStruct((S, SWA_WIDTH), BF16), name="swa_fwd",
        compiler_params=_params(("parallel",)))(proj, proj, proj, proj, proj, q_gain, k_gain, sinks, bias)


def _swa_bwd(proj, q_gain, k_gain, sinks, bias, dy, dproj):
    S = proj.shape[0]

    def body(q_ref, kp_ref, kc_ref, vp_ref, vc_ref, qg_ref, kg_ref, s_ref, bias_ref, dy_ref, _,
             dq_ref, dk_ref, dv_ref, dqg_ref, dkg_ref, ds_ref, dbias_ref):
        n = pl.program_id(0)
        mask = _swa_mask(n)

        @pl.when(n == 0)
        def _():
            for r in (dk_ref, dv_ref, dqg_ref, dkg_ref, ds_ref, dbias_ref):
                r[...] = jnp.zeros_like(r)

        cur = pl.ds(pl.multiple_of(n * BLOCK, BLOCK), BLOCK)
        prev = pl.ds(pl.multiple_of(jnp.maximum(n - 1, 0) * BLOCK, BLOCK), BLOCK)
        q, kband, vband, sk = _swa_load(q_ref, kp_ref, kc_ref, vp_ref, vc_ref, s_ref)
        _, vjp = jax.vjp(lambda q, kb, vb, qg, kg, sk, bs: _swa_block(q, kb, vb, qg, kg, sk, bs, mask),
                         q, kband, vband, qg_ref[...], kg_ref[...], sk, bias_ref[...])
        dy = jnp.stack([dy_ref[:, pl.ds(h * SWA_DIM, SWA_DIM)] for h in range(SWA_HEADS)])
        dq, dkb, dvb, dqg, dkg, dsk, dbs = vjp(dy)
        for h in range(SWA_HEADS):
            dq_ref[:, pl.ds(h * SWA_DIM, SWA_DIM)] = dq[h].astype(BF16)
            ds_ref[:, pl.ds(h, 1)] += dsk[h]
        dbias_ref[...] += dbs
        dqg_ref[...] += dqg
        dkg_ref[...] += dkg
        for kv in range(SWA_KV):
            cols = pl.ds(kv * SWA_DIM, SWA_DIM)
            group = range(kv * SWA_GROUP, (kv + 1) * SWA_GROUP)
            dk_kv = sum(dkb[h] for h in group)
            dv_kv = sum(dvb[h] for h in group)
            dk_ref[cur, cols] += dk_kv[BLOCK:]
            dv_ref[cur, cols] += dv_kv[BLOCK:]

            @pl.when(n > 0)
            def _(cols=cols, dk_kv=dk_kv, dv_kv=dv_kv):
                dk_ref[prev, cols] += dk_kv[:BLOCK]
                dv_ref[prev, cols] += dv_kv[:BLOCK]

    return pl.pallas_call(
        body, grid=(S // BLOCK,),
        in_specs=_swa_in_specs() + [pl.BlockSpec((BLOCK, SWA_WIDTH), lambda n: (n, 0)),
                                    pl.BlockSpec(memory_space=pl.ANY)],
        out_specs=[pl.BlockSpec((BLOCK, SWA_WIDTH), lambda n: (n, P_SQ // SWA_WIDTH)), _full((S, SWA_KVW)),
                   _full((S, SWA_KVW)), _full((1, SWA_DIM)), _full((1, SWA_DIM)), _full((1, SWA_HEADS)),
                   _full((SWA_HEADS, BLOCK, 2 * BLOCK))],
        out_shape=[jax.ShapeDtypeStruct(dproj.shape, dproj.dtype), jax.ShapeDtypeStruct((S, SWA_KVW), F32),
                   jax.ShapeDtypeStruct((S, SWA_KVW), F32), jax.ShapeDtypeStruct((1, SWA_DIM), F32),
                   jax.ShapeDtypeStruct((1, SWA_DIM), F32), jax.ShapeDtypeStruct((1, SWA_HEADS), F32),
                   jax.ShapeDtypeStruct((SWA_HEADS, BLOCK, 2 * BLOCK), F32)],
        input_output_aliases={10: 0},
        name="swa_bwd", compiler_params=_params(("arbitrary",)),
    )(proj, proj, proj, proj, proj, q_gain, k_gain, sinks, bias, dy, dproj)


def _kv_into(dproj, dk, dv, tm=512):
    S = dk.shape[0]

    def body(dk_ref, dv_ref, _, o_ref):
        o_ref[:, :SWA_KVW] = dk_ref[...].astype(BF16)
        o_ref[:, SWA_KVW:] = dv_ref[...].astype(BF16)

    return pl.pallas_call(
        body, grid=(S // tm,), in_specs=[_row(tm, SWA_KVW), _row(tm, SWA_KVW), pl.BlockSpec(memory_space=pl.ANY)],
        out_specs=_row(tm, 2 * SWA_KVW, P_SK // (2 * SWA_KVW)),
        out_shape=jax.ShapeDtypeStruct(dproj.shape, dproj.dtype), input_output_aliases={2: 0},
        name="swa_kv_into", compiler_params=_params(("parallel",)))(dk, dv, dproj)


def _position():
    return lax.axis_index("x"), lax.axis_index("y"), lax.axis_index("c")


def _all_gather(shards, name="all_gather_weights"):
    na = len(shards)

    def body(*refs):
        x_refs, out_refs = refs[:na], refs[na:2 * na]
        send_sems, recv_sems, local_sems = refs[2 * na:]
        x, y, c = _position()
        me, sibling = (x, y, c), (x, y, 1 - c)
        chips = [(1 - x, y), (x, 1 - y), (1 - x, 1 - y)]

        def copy(a, k, block, to, own=False):
            px, py, pc = block
            slot = out_refs[a].at[4 * px + 2 * py + pc]
            return pltpu.make_async_remote_copy(
                src_ref=x_refs[a] if own else slot, dst_ref=slot, send_sem=send_sems.at[7 * a + k],
                recv_sem=recv_sems.at[7 * a + k], device_id=to, device_id_type=MESH_ID)

        mine = [pltpu.make_async_copy(x_refs[a], out_refs[a].at[4 * x + 2 * y + c], local_sems.at[a])
                for a in range(na)]
        for cp in mine:
            cp.start()
        first = []
        for a in range(na):
            first.append(copy(a, 0, me, sibling, own=True))
            first += [copy(a, 1 + j, me, (*chip, c), own=True) for j, chip in enumerate(chips)]
        for cp in first:
            cp.start()
        passed = []
        for j, chip in enumerate(chips):
            for a in range(na):
                copy(a, 1 + j, (*chip, c), me).wait_recv()
                passed.append(copy(a, 4 + j, (*chip, c), sibling))
                passed[-1].start()
        for a in range(na):
            copy(a, 0, sibling, me).wait_recv()
            for j, chip in enumerate(chips):
                copy(a, 4 + j, (*chip, 1 - c), me).wait_recv()
        for cp in first + passed:
            cp.wait_send()
        for cp in mine:
            cp.wait()

    return pl.pallas_call(
        body, in_specs=[pl.BlockSpec(memory_space=pl.ANY)] * na, out_specs=[pl.BlockSpec(memory_space=pl.ANY)] * na,
        out_shape=[jax.ShapeDtypeStruct((N_DEV,) + s.shape, s.dtype) for s in shards],
        scratch_shapes=[pltpu.SemaphoreType.DMA((7 * na,)), pltpu.SemaphoreType.DMA((7 * na,)),
                        pltpu.SemaphoreType.DMA((na,))],
        name=name)(*shards)


_HBM = pl.BlockSpec(memory_space=pltpu.HBM)
_SEM = pl.BlockSpec(memory_space=pltpu.SEMAPHORE)
_DATAFLOW = pltpu.SideEffectType.DATAFLOW_SIDE_EFFECTING


def _peers(x, y, c):
    out = []
    for k in range(1, N_DEV):
        px, py, pc = x ^ (k >> 2), y ^ ((k >> 1) & 1), c ^ (k & 1)
        out.append(((px, py, pc), 4 * px + 2 * py + pc))
    return out


def _split_copies(src_refs, land_refs, send_sems, recv_sems, scatter):
    x, y, c = _position()
    me = 4 * x + 2 * y + c
    sends, recvs = [], []
    for k, (peer_id, peer) in enumerate(_peers(x, y, c)):
        for a, (src, land) in enumerate(zip(src_refs, land_refs)):
            sems = dict(send_sem=send_sems.at[7 * a + k], recv_sem=recv_sems.at[7 * a + k],
                        device_id=peer_id, device_id_type=MESH_ID)
            mine = src.at[peer] if scatter else src
            sends.append(pltpu.make_async_remote_copy(src_ref=mine, dst_ref=land.at[me], **sems))
            recvs.append(pltpu.make_async_remote_copy(src_ref=mine, dst_ref=land.at[peer], **sems))
    return sends, recvs


def _all_gather_direct(shards, name, after):
    na = len(shards)

    def body(*refs):
        x_refs, out_refs = refs[:na], refs[na + 1:2 * na + 1]
        send_sems, recv_sems, local_sems = refs[2 * na + 1:]
        x, y, c = _position()
        me = 4 * x + 2 * y + c
        local = [pltpu.make_async_copy(x_refs[a], out_refs[a].at[me], local_sems.at[a]) for a in range(na)]
        sends, recvs = _split_copies(x_refs, out_refs, send_sems, recv_sems, False)
        for cp in local + sends:
            cp.start()
        for cp in recvs:
            cp.wait_recv()
        for cp in sends:
            cp.wait_send()
        for cp in local:
            cp.wait()

    return pl.pallas_call(
        body, in_specs=[pl.BlockSpec(memory_space=pl.ANY)] * (na + 1),
        out_specs=[pl.BlockSpec(memory_space=pl.ANY)] * na,
        out_shape=[jax.ShapeDtypeStruct((N_DEV,) + s.shape, s.dtype) for s in shards],
        scratch_shapes=[pltpu.SemaphoreType.DMA((7 * na,)), pltpu.SemaphoreType.DMA((7 * na,)),
                        pltpu.SemaphoreType.DMA((na,))],
        name=name)(*shards, after)


def _exchange_start(srcs, scatter, name, after=None):
    na = len(srcs)
    lands = [lax.empty(s.shape if scatter else (N_DEV,) + s.shape, s.dtype) for s in srcs]
    extra = [] if after is None else [after]

    def body(*refs):
        src_refs, land_refs = refs[:na], refs[na:2 * na]
        send_sems, recv_sems = refs[2 * na + len(extra)], refs[2 * na + len(extra) + 1]
        token = refs[-1]
        sends, _ = _split_copies(src_refs, land_refs, send_sems, recv_sems, scatter)
        for cp in sends:
            cp.start()
        token[...] = jnp.zeros_like(token)

    hbm = lambda a: pltpu.HBM(a.shape, a.dtype)
    out = pl.pallas_call(
        body, name=name,
        out_shape=(pltpu.SemaphoreType.DMA((7 * na,)), pltpu.SemaphoreType.DMA((7 * na,)),
                   *[hbm(s) for s in srcs], *[hbm(l) for l in lands], jax.ShapeDtypeStruct((8, LANES), F32)),
        in_specs=[_HBM] * (2 * na) + [pl.BlockSpec(memory_space=pl.ANY)] * len(extra),
        out_specs=(_SEM, _SEM, *[_HBM] * (2 * na), pl.BlockSpec(memory_space=pltpu.VMEM)),
        input_output_aliases={i: 2 + i for i in range(2 * na)},
        compiler_params=pltpu.CompilerParams(has_side_effects=_DATAFLOW),
    )(*[pltpu.with_memory_space_constraint(s, pltpu.HBM) for s in srcs],
      *[pltpu.with_memory_space_constraint(l, pltpu.HBM) for l in lands], *extra)
    return (out[0], out[1], list(out[2:2 + na]), list(out[2 + na:2 + 2 * na])), out[-1]


def _exchange_wait(handle, after, scatter, name):
    send_sems, recv_sems, srcs, lands = handle
    na = len(srcs)

    def body(*refs):
        src_refs, land_refs = refs[:na], refs[na:2 * na]
        s_sems, r_sems = refs[2 * na], refs[2 * na + 1]
        sends, recvs = _split_copies(src_refs, land_refs, s_sems, r_sems, scatter)
        for cp in sends:
            cp.wait_send()
        for cp in recvs:
            cp.wait_recv()

    hbm = lambda a: pltpu.HBM(a.shape, a.dtype)
    out = pl.pallas_call(
        body, name=name, out_shape=(*[hbm(s) for s in srcs], *[hbm(l) for l in lands]),
        in_specs=[_HBM] * (2 * na) + [_SEM, _SEM, pl.BlockSpec(memory_space=pl.ANY)],
        out_specs=tuple([_HBM] * (2 * na)), input_output_aliases={i: i for i in range(2 * na)},
        compiler_params=pltpu.CompilerParams(has_side_effects=_DATAFLOW),
    )(*srcs, *lands, send_sems, recv_sems, after)
    return list(out[:na]), list(out[na:])


def _own_slot(landed, own):
    me = 4 * lax.axis_index("x") + 2 * lax.axis_index("y") + lax.axis_index("c")
    return lax.dynamic_update_slice_in_dim(landed, own[None], me, axis=0)


def _adam_update(parts, w, m, v, name, tr=256):
    _, r, c = w.shape
    tr = _pick_rows(r, tr)
    cp = parts.shape[2]

    def body(p_ref, w_ref, m_ref, v_ref, g_ref, d_ref, nm_ref, nv_ref):
        g = p_ref[0, :, pl.ds(0, c)].astype(F32)
        for i in range(1, N_DEV):
            g = g + p_ref[i, :, pl.ds(0, c)].astype(F32)
        delta, nm, nv = _adamw(w_ref[0], g, m_ref[0], v_ref[0])
        g_ref[0] = g
        d_ref[0] = delta
        nm_ref[0] = nm
        nv_ref[0] = nv

    rs = pl.BlockSpec((1, tr, c), lambda i: (0, i, 0))
    return pl.pallas_call(
        body, grid=(r // tr,), in_specs=[pl.BlockSpec((N_DEV, tr, cp), lambda i: (0, i, 0)), rs, rs, rs],
        out_specs=[rs] * 4, out_shape=[jax.ShapeDtypeStruct((1, r, c), F32)] * 4, name=name,
        compiler_params=_params(("parallel",)))(parts, w, m, v)


def _pick_rows(rows, target):
    if rows <= target:
        return rows
    t = target
    while t >= 16:
        if rows % t == 0:
            return t
        t -= 16
    return rows


BIG = ("w_in", "w_branch_dn", "w_branch_swa", "w_out", "w_gate", "w_up", "w_down")
IN_SHARD, IN_WIRE = D_IN // N_DEV, 640
FF_SHARD, FF_WIRE = D_FF // N_DEV, 384
D_FFP = N_DEV * FF_WIRE
BIG_SHAPES = {"w_in": ((D_MODEL, IN_SHARD), (D_MODEL, IN_WIRE)),
              "w_branch_dn": ((DN_WIDTH, LANES), (DN_WIDTH, LANES)),
              "w_branch_swa": ((SWA_WIDTH, LANES), (SWA_WIDTH, LANES)),
              "w_out": ((LANES, D_MODEL), (LANES, D_MODEL)),
              "w_gate": ((D_MODEL, FF_SHARD), (D_MODEL, FF_WIRE)),
              "w_up": ((D_MODEL, FF_SHARD), (D_MODEL, FF_WIRE)),
              "w_down": ((FF_SHARD, D_MODEL), (FF_WIRE, D_MODEL))}
CONV_SHARD, CONV_WIRE = (DN_CONV, DN_QKV // N_DEV), (8, 256)


def _pad_to(a, shape):
    return jnp.pad(a, [(0, t - s) for s, t in zip(a.shape, shape)])


_IN_SEGS = ((R_GATE, 2048, P_GATE), (R_QKV, DN_QKV, P_QKV), (R_Z, DN_WIDTH, P_Z), (R_SQ, SWA_WIDTH, P_SQ),
            (R_SK, SWA_KVW, P_SK), (R_SV, SWA_KVW, P_SV), (R_B, 8, P_BA))


def _w_in_from_blocks(blocks):
    parts = []
    for rs, n, _ in _IN_SEGS:
        for dev in range(N_DEV):
            lo, hi = max(rs, IN_SHARD * dev), min(rs + n, IN_SHARD * (dev + 1))
            if lo < hi:
                parts.append(blocks[dev, :, lo - IN_SHARD * dev:hi - IN_SHARD * dev])
    parts.append(jnp.zeros((blocks.shape[1], P_WIDTH - P_BA - 8), blocks.dtype))
    return jnp.concatenate(parts, axis=1)


def _w_in_to_blocks(g):
    out = []
    for dev in range(N_DEV):
        parts = []
        for rs, n, ps in sorted(_IN_SEGS):
            lo, hi = max(rs, IN_SHARD * dev), min(rs + n, IN_SHARD * (dev + 1))
            if lo < hi:
                parts.append(g[:, ps + lo - rs:ps + hi - rs])
        parts.append(jnp.zeros((g.shape[0], IN_WIRE - IN_SHARD), g.dtype))
        out.append(jnp.concatenate(parts, axis=1))
    return jnp.stack(out)


SMALL = {"attn_norm": (0, (1, D_MODEL)), "ffn_norm": (1, (1, D_MODEL)), "dn_out_norm": (2, (1, DN_DIM)),
         "swa_q_norm": (3, (1, SWA_DIM)), "swa_k_norm": (4, (1, SWA_DIM)), "dn_a_log": (5, (1, DN_HEADS)),
         "dn_dt_bias": (6, (1, DN_HEADS)), "swa_sinks": (7, (1, SWA_HEADS)), "rel_bias": (8, (REL_BUCKETS, SWA_HEADS))}
SMALL_SHEET = (48, D_MODEL)


def _small_pack(grads):
    names = list(SMALL)

    def body(*refs):
        o_ref = refs[-1]
        o_ref[...] = jnp.zeros_like(o_ref)
        for n, ref in zip(names, refs):
            r0, (nr, nc) = SMALL[n]
            o_ref[r0:r0 + nr, 0:nc] = ref[...]

    return pl.pallas_call(
        body, in_specs=[pl.BlockSpec(memory_space=pltpu.VMEM)] * len(names),
        out_specs=pl.BlockSpec(memory_space=pltpu.VMEM), out_shape=jax.ShapeDtypeStruct(SMALL_SHEET, F32),
        name="small_pack", compiler_params=_params())(*[grads[n].reshape(SMALL[n][1]) for n in names])


def _small_update(sheets, w, m, v):
    names = list(SMALL)
    k = len(names)

    def body(*refs):
        p_ref = refs[0]
        ins, outs = refs[1:1 + 3 * k], refs[1 + 3 * k:]
        for t, n in enumerate(names):
            r0, (nr, nc) = SMALL[n]
            g = p_ref[0, r0:r0 + nr, 0:nc]
            for i in range(1, N_DEV):
                g = g + p_ref[i, r0:r0 + nr, 0:nc]
            delta, nm, nv = _adamw(ins[t][...], g, ins[k + t][...], ins[2 * k + t][...])
            for kind, val in enumerate((g, delta, nm, nv)):
                outs[kind * k + t][...] = val

    shapes = [jax.ShapeDtypeStruct(SMALL[n][1], F32) for n in names]
    vm = pl.BlockSpec(memory_space=pltpu.VMEM)
    res = pl.pallas_call(
        body, in_specs=[vm] * (1 + 3 * k), out_specs=[vm] * (4 * k), out_shape=shapes * 4, name="adam_small",
        compiler_params=_params(),
    )(sheets, *[d[n].reshape(SMALL[n][1]) for d in (w, m, v) for n in names])
    return {n: tuple(res[kind * k + t] for kind in range(4)) for t, n in enumerate(names)}


def kernel(x, attn_norm, w_in, dn_conv, dn_a_log, dn_dt_bias, dn_out_norm, swa_q_norm, swa_k_norm, swa_sinks, rel_bias, w_branch_dn, w_branch_swa, w_out, ffn_norm, w_gate, w_up, w_down, loss_target, m_attn_norm, m_w_in, m_dn_conv, m_dn_a_log, m_dn_dt_bias, m_dn_out_norm, m_swa_q_norm, m_swa_k_norm, m_swa_sinks, m_rel_bias, m_w_branch_dn, m_w_branch_swa, m_w_out, m_ffn_norm, m_w_gate, m_w_up, m_w_down, v_attn_norm, v_w_in, v_dn_conv, v_dn_a_log, v_dn_dt_bias, v_dn_out_norm, v_swa_q_norm, v_swa_k_norm, v_swa_sinks, v_rel_bias, v_w_branch_dn, v_w_branch_swa, v_w_out, v_ffn_norm, v_w_gate, v_w_up, v_w_down):
    args = dict(locals())
    S = x.shape[1]
    xs = x.reshape(S, D_MODEL)
    target = loss_target.reshape(S, D_MODEL)

    w_loc = {n: args[n].reshape(BIG_SHAPES[n][0]) for n in BIG}
    conv_loc = dn_conv.reshape(CONV_SHARD)
    wire = {n: _pad_to(w_loc[n], BIG_SHAPES[n][1]).astype(BF16) for n in BIG}
    first = _all_gather([wire["w_in"], _pad_to(conv_loc, CONV_WIRE)])
    later = [n for n in BIG if n != "w_in"]
    rest_handle, rest_token = _exchange_start([wire[n] for n in later], False, "gather_rest_start", after=first[1])
    w_pad = _w_in_from_blocks(first[0])
    conv_w = jnp.concatenate([first[1][d, :DN_CONV, :CONV_SHARD[1]] for d in range(N_DEV)], axis=1)

    h = _norm_fwd(xs, attn_norm + rest_token[0, 0], "norm1_fwd")
    proj = _mm([(h, w_pad)], "nn", F32, "mm_in", 512, 1664, j_outer=True)
    qkvn = _dn_conv_fwd(proj, conv_w)
    beta, g = _dn_gate_fwd(proj, dn_a_log, dn_dt_bias)
    qp, op, mm, nn, egl, tinv = _dn_prep_fwd(qkvn, g, beta)
    o, states = _dn_scan_fwd(qp, op, mm, nn, egl)
    y_dn = _dn_out_fwd(o, proj, dn_out_norm)
    bias = _bias_fwd(rel_bias)
    y_swa = _swa_fwd(proj, swa_q_norm, swa_k_norm, swa_sinks, bias)
    rest_src, rest_land = _exchange_wait(rest_handle, y_swa, False, "gather_rest_wait")
    G = {n: _own_slot(land, src) for n, src, land in zip(later, rest_src, rest_land)}
    w_bdn, w_bswa, w_g, w_u = G["w_branch_dn"], G["w_branch_swa"], G["w_gate"], G["w_up"]
    w_o = G["w_out"].reshape(D_MODEL, D_MODEL)
    w_d = G["w_down"].reshape(D_FFP, D_MODEL)
    gates = [(proj, P_GATE // 512), (proj, (P_GATE + D_MODEL) // 512)]
    a_dn, a_swa, merged = _mm_fused(
        [(y_dn, w_bdn), (y_swa, w_bswa)], "nn", "mm_branch_merge", 1024, 512,
        lambda p, e: (p[0], p[1], _merge(e[0], e[1], p[0], p[1])), gates, (F32, F32, BF16), b_blocks=True)

    def resid_norm(p, e):
        x1 = e[0] + p[0]
        return x1, _rms(x1, e[1])

    x1, h2 = _mm_fused([(merged, w_o)], "nn", "mm_out_norm", 512, D_MODEL, resid_norm,
                       [(xs, 0), (ffn_norm, None)], (F32, BF16))
    gate, up, act = _mm_fused([(h2, w_g), (h2, w_u)], "nn", "mm_gate_up_act", 1024, 768,
                              lambda p, e: (p[0], p[1], _act(p[0], p[1])), [], (F32, F32, BF16),
                              j_outer=True, b_blocks=True)

    def loss_head(p, e):
        diff = e[0] + p[0] - e[1]
        dy = diff * (1.0 / D_MODEL)
        part = jnp.sum(jnp.mean(diff * diff, axis=-1, keepdims=True), axis=0, keepdims=True) * 0.5
        return dy, dy, part

    dy, dy_b, loss_local = _mm_fused([(act, w_d)], "nn", "mm_down_loss", 512, D_MODEL, loss_head,
                                     [(x1, 0), (target, 0)], (F32, BF16), sum_shape=(1, 1))

    def act_bwd(p, e):
        _, vjp = jax.vjp(_act, e[0], e[1])
        return vjp(p[0])

    dgate, dup = _mm_fused([(dy_b, w_d)], "nt", "mm_dact_act", 1024, 768, act_bwd, [(gate, 0), (up, 0)],
                           (BF16, BF16), j_outer=True)
    g_w_down = _mm([(act, dy_b)], "tn", BF16, "mm_dw_down", 768, D_MODEL, j_outer=True)
    g_w_down = g_w_down.reshape(N_DEV, FF_WIRE, D_MODEL)
    g_w_gate = _mm([(h2, dgate)], "tn", BF16, "mm_dw_gate", D_MODEL, 768, out_blocks=True)
    g_w_up = _mm([(h2, dup)], "tn", BF16, "mm_dw_up", D_MODEL, 768, out_blocks=True)
    ffn_handle, ffn_token = _exchange_start([g_w_down, g_w_gate, g_w_up], True, "scatter_ffn_start")

    def norm_bwd(p, e):
        _, vjp = jax.vjp(_rms, e[0], e[2])
        dx, dgain = vjp(sum(p))
        dx = dx + e[1]
        return dx, dx, dgain

    dx1, dx1_b, g_ffn_norm = _mm_fused(
        [(dgate, w_g), (dup, w_u)], "nt", "mm_dh2_norm", 256, D_MODEL, norm_bwd,
        [(x1, 0), (dy, 0), (ffn_norm + ffn_token[0, 0], None)], (F32, BF16), b_blocks=True, sum_shape=(1, D_MODEL))
    def merge_bwd(p, e):
        _, vjp = jax.vjp(_merge, *e)
        dg0, dg1, da_dn, da_swa = vjp(p[0])
        return jnp.concatenate([dg0, dg1], axis=1), da_dn, da_swa

    dproj, da_dn, da_swa = _mm_fused(
        [(dx1_b, w_o)], "nt", "mm_dmerged_merge", 512, D_MODEL, merge_bwd,
        [(proj, P_GATE // D_MODEL), (proj, P_GATE // D_MODEL + 1), (a_dn, 0), (a_swa, 0)], (BF16,) * 3,
        wide_first=(P_WIDTH, 2 * D_MODEL))
    g_w_out = _mm([(merged, dx1_b)], "tn", BF16, "mm_dw_out", 512, D_MODEL, j_outer=True)
    g_w_out = g_w_out.reshape(N_DEV, LANES, D_MODEL)
    dy_dn = _mm([(da_dn, w_bdn)], "nt", F32, "mm_dy_dn", 1024, DN_WIDTH, b_blocks=True)
    dy_swa = _mm([(da_swa, w_bswa)], "nt", F32, "mm_dy_swa", 1024, SWA_WIDTH, b_blocks=True)
    g_w_bdn = _mm([(y_dn, da_dn)], "tn", BF16, "mm_dw_branch_dn", DN_WIDTH, 512, out_blocks=True)
    g_w_bswa = _mm([(y_swa, da_swa)], "tn", BF16, "mm_dw_branch_swa", SWA_WIDTH, 512, out_blocks=True)
    dproj, dsk, dsv, g_q_norm, g_k_norm, g_sinks, dbias = _swa_bwd(proj, swa_q_norm, swa_k_norm, swa_sinks, bias,
                                                                   dy_swa, dproj)
    dproj = _kv_into(dproj, dsk, dsv)
    g_rel_bias = _bias_bwd(dbias)[:, :REL_BUCKETS].T
    mix_handle, mix_token = _exchange_start([g_w_out, g_w_bdn, g_w_bswa], True, "scatter_mix_start")
    do, dproj, g_out_norm = _dn_out_bwd(o, proj, dn_out_norm + mix_token[0, 0], dy_dn, dproj)
    dqp, dmm, dnn, degl = _dn_scan_bwd(qp, op, mm, nn, egl, states, do)
    dqkvn, dgd, dbeta = _dn_prep_bwd(qkvn, g, beta, tinv, dqp, do, dmm, dnn, degl)
    dproj, dal, ddt = _dn_gate_bwd(proj, dn_a_log, dn_dt_bias, dbeta, dgd, dproj)
    g_a_log = dal.reshape(DN_HEADS, DN_DIM).sum(axis=1)
    g_dt_bias = ddt.reshape(DN_HEADS, DN_DIM).sum(axis=1)
    dproj, g_conv = _dn_conv_bwd(proj, conv_w, dqkvn, dproj)
    g_w_in = _w_in_to_blocks(_mm([(h, dproj)], "tn", BF16, "mm_dw_in", 512, 1664, j_outer=True))
    in_handle, in_token = _exchange_start([g_w_in], True, "scatter_in_start")
    dx, g_attn_norm = _mm_fused(
        [(dproj, w_pad)], "nt", "mm_dh_norm", 512, D_MODEL, lambda p, e: norm_bwd(p, e)[1:],
        [(xs, 0), (dx1, 0), (attn_norm + in_token[0, 0], None)], (F32,), sum_shape=(1, D_MODEL))

    g_small = {"attn_norm": g_attn_norm, "ffn_norm": g_ffn_norm, "rel_bias": g_rel_bias, "dn_out_norm": g_out_norm,
               "swa_q_norm": g_q_norm, "swa_k_norm": g_k_norm, "dn_a_log": g_a_log, "dn_dt_bias": g_dt_bias,
               "swa_sinks": g_sinks}
    me = 4 * lax.axis_index("x") + 2 * lax.axis_index("y") + lax.axis_index("c")
    outs = {}

    def finish(handle, group, name, after):
        srcs, lands = _exchange_wait(handle, after, True, name)
        for n, src, land in zip(group, srcs, lands):
            parts = _own_slot(land, lax.dynamic_index_in_dim(src, me, 0, keepdims=False))
            outs[n] = _adam_update(parts, args[n], args["m_" + n], args["v_" + n], "adam_" + n)

    finish(ffn_handle, ("w_down", "w_gate", "w_up"), "scatter_ffn_wait", dx)
    finish(mix_handle, ("w_out", "w_branch_dn", "w_branch_swa"), "scatter_mix_wait", dx)
    sheets, conv_all = _all_gather_direct([_small_pack(g_small), _pad_to(g_conv, (8, DN_QKV))], "all_gather_small",
                                          after=outs["w_up"][0])
    finish(in_handle, ("w_in",), "scatter_in_wait", sheets)
    conv_parts = lax.dynamic_slice(conv_all, (0, 0, me * CONV_SHARD[1]), (N_DEV,) + CONV_SHARD)
    outs["dn_conv"] = _adam_update(conv_parts, dn_conv, m_dn_conv, v_dn_conv, "adam_dn_conv")
    outs.update(_small_update(sheets, {n: args[n] for n in SMALL}, {n: args["m_" + n] for n in SMALL},
                              {n: args["v_" + n] for n in SMALL}))

    names = ("attn_norm", "w_in", "dn_conv", "dn_a_log", "dn_dt_bias", "dn_out_norm", "swa_q_norm", "swa_k_norm",
             "swa_sinks", "rel_bias", "w_branch_dn", "w_branch_swa", "w_out", "ffn_norm", "w_gate", "w_up", "w_down")
    results = []
    for kind in range(4):
        results += [outs[n][kind].reshape(args[n].shape) for n in names]

    loss = lax.psum(loss_local[0, 0], ("x", "y", "c"))
    return (loss, dx.reshape(x.shape), *results)
```

```python
import math

import numpy as np
import jax
import jax.numpy as jnp
from jax import lax
from jax.experimental import pallas as pl
from jax.experimental.pallas import tpu as pltpu

F32 = jnp.float32
BF16 = jnp.bfloat16
HI = lax.Precision.HIGHEST

D_MODEL = 1024
DN_HEADS = 4
DN_DIM = 128
DN_WIDTH = 512
DN_QKV = 1536
DN_CONV = 4
CHUNK = 64
SWA_HEADS = 8
SWA_KV = 2
SWA_GROUP = 4
SWA_DIM = 64
SWA_WIDTH = 512
SWA_KVW = 128
WINDOW = 128
BLOCK = 128
REL_BUCKETS = 32
REL_MAX_DIST = 128
D_FF = 2816
D_IN = 4872
EPS = 1e-6
N_DEV = 8

ADAM_LR = 0.001
ADAM_B1 = 0.9
ADAM_B2 = 0.999
ADAM_EPS = 1e-08
ADAM_WD = 0.01
ADAM_STEP = 10

P_GATE, P_QKV, P_Z, P_SQ, P_SK, P_SV, P_BA = 0, 2048, 3584, 4096, 4608, 4736, 4864
P_WIDTH = 4992
R_QKV, R_Z, R_B, R_A, R_SQ, R_SK, R_SV, R_GATE = 0, 1536, 2048, 2052, 2056, 2568, 2696, 2824

VMEM_LIMIT = 56 * 1024 * 1024
LANES = 128
MESH_ID = pl.DeviceIdType.MESH


def _params(sem=None):
    return pltpu.CompilerParams(dimension_semantics=sem, vmem_limit_bytes=VMEM_LIMIT)


def _pick(dim, target):
    if dim <= target:
        return dim
    t = target - target % LANES
    while t >= LANES:
        if dim % t == 0:
            return t
        t -= LANES
    return dim


_DIMS = {"nn": (((1,), (0,)), ((), ())), "nt": (((1,), (1,)), ((), ())), "tn": (((0,), (0,)), ((), ()))}


def _tile_product(a_ref, b_ref, mode, b_blocks):
    a = a_ref[...].astype(BF16)
    b = jnp.concatenate([b_ref[d] for d in range(b_ref.shape[0])], axis=1) if b_blocks else b_ref[...]
    return lax.dot_general(a, b.astype(BF16), _DIMS[mode], preferred_element_type=F32)


def _mm(pairs, mode, out_dtype, name, bm, bn, j_outer=False, b_blocks=False, out_blocks=False):
    a0, b0 = pairs[0]
    cb = b0.shape[2] if b_blocks else None
    b_shape = (b0.shape[1], N_DEV * cb) if b_blocks else b0.shape
    if mode == "nn":
        (M, K), (K2, N) = a0.shape, b_shape
    elif mode == "nt":
        (M, K), (N, K2) = a0.shape, b_shape
    else:
        (K, M), (K2, N) = a0.shape, b_shape
    bm, bn = min(bm, M), min(bn, N)
    assert K == K2 and M % bm == 0 and N % bn == 0, (name, a0.shape, b0.shape, bm, bn)
    co = N // N_DEV
    assert not out_blocks or bn % co == 0
    dims = _DIMS[mode]
    n = len(pairs)

    def body(*refs):
        o_ref = refs[2 * n]
        acc = None
        for t in range(n):
            p = _tile_product(refs[2 * t], refs[2 * t + 1], mode, b_blocks)
            acc = p if acc is None else acc + p
        if out_blocks:
            for d in range(bn // co):
                o_ref[d] = acc[:, d * co:(d + 1) * co].astype(out_dtype)
        else:
            o_ref[...] = acc.astype(out_dtype)

    def ij(f):
        return (lambda j, i: f(i, j)) if j_outer else f

    a_spec = pl.BlockSpec((K, bm), ij(lambda i, j: (0, i))) if mode == "tn" else pl.BlockSpec((bm, K), ij(lambda i, j: (i, 0)))
    if b_blocks and mode == "nt":
        b_spec = pl.BlockSpec((N_DEV, bn, cb), ij(lambda i, j: (0, j, 0)))
    elif b_blocks:
        b_spec = pl.BlockSpec((bn // cb, K, cb), ij(lambda i, j: (j, 0, 0)))
    elif mode == "nt":
        b_spec = pl.BlockSpec((bn, K), ij(lambda i, j: (j, 0)))
    else:
        b_spec = pl.BlockSpec((K, bn), ij(lambda i, j: (0, j)))
    if out_blocks:
        out_spec = pl.BlockSpec((bn // co, bm, co), ij(lambda i, j: (j, i, 0)))
        out_shape = jax.ShapeDtypeStruct((N_DEV, M, co), out_dtype)
    else:
        out_spec = pl.BlockSpec((bm, bn), ij(lambda i, j: (i, j)))
        out_shape = jax.ShapeDtypeStruct((M, N), out_dtype)
    grid = (N // bn, M // bm) if j_outer else (M // bm, N // bn)
    return pl.pallas_call(
        body, grid=grid, in_specs=[a_spec, b_spec] * n, out_specs=out_spec, out_shape=out_shape, name=name,
        compiler_params=_params(("parallel", "parallel")),
    )(*[x for pair in pairs for x in pair])


def _mm_fused(pairs, mode, name, bm, bn, epilogue, extras, out_dtypes, j_outer=False, b_blocks=False,
              sum_shape=None, wide_first=None):
    a0, b0 = pairs[0]
    cb = b0.shape[2] if b_blocks else None
    b_shape = (b0.shape[1], N_DEV * cb) if b_blocks else b0.shape
    if mode == "nn":
        (M, K), (K2, N) = a0.shape, b_shape
    else:
        (M, K), (N, K2) = a0.shape, b_shape
    bm, bn = min(bm, M), min(bn, N)
    assert mode in ("nn", "nt") and K == K2 and M % bm == 0 and N % bn == 0, (name, a0.shape, b0.shape)
    dims = _DIMS[mode]
    n, ne, no = len(pairs), len(extras), len(out_dtypes)

    def body(*refs):
        prods = [_tile_product(refs[2 * t], refs[2 * t + 1], mode, b_blocks) for t in range(n)]
        results = epilogue(prods, [r[...] for r in refs[2 * n:2 * n + ne]])
        out_refs = refs[2 * n + ne:]
        for o_ref, val, dt in zip(out_refs, results, out_dtypes):
            o_ref[...] = val.astype(dt)
        if sum_shape is not None:
            s_ref = out_refs[no]

            @pl.when((pl.program_id(0) == 0) & (pl.program_id(1) == 0))
            def _():
                s_ref[...] = jnp.zeros_like(s_ref)

            s_ref[...] += results[no]

    def ij(f):
        return (lambda j, i: f(i, j)) if j_outer else f

    a_spec = pl.BlockSpec((bm, K), ij(lambda i, j: (i, 0)))
    once = dict(pipeline_mode=pl.Buffered(1)) if bn == N else {}
    if b_blocks and mode == "nt":
        b_spec = pl.BlockSpec((N_DEV, bn, cb), ij(lambda i, j: (0, j, 0)), **once)
    elif b_blocks:
        b_spec = pl.BlockSpec((bn // cb, K, cb), ij(lambda i, j: (j, 0, 0)), **once)
    elif mode == "nt":
        b_spec = pl.BlockSpec((bn, K), ij(lambda i, j: (j, 0)), **once)
    else:
        b_spec = pl.BlockSpec((K, bn), ij(lambda i, j: (0, j)), **once)
    e_specs = [pl.BlockSpec((1, bn), ij(lambda i, j: (0, j))) if first is None
               else pl.BlockSpec((bm, bn), ij(lambda i, j, first=first: (i, first + j))) for _, first in extras]
    tile = pl.BlockSpec((bm, bn), ij(lambda i, j: (i, j)))
    out_specs = [tile] * no
    out_shape = [jax.ShapeDtypeStruct((M, N), dt) for dt in out_dtypes]
    if wide_first is not None:
        assert bn == N
        out_specs[0] = pl.BlockSpec((bm, wide_first[1]), ij(lambda i, j: (i, 0)))
        out_shape[0] = jax.ShapeDtypeStruct((M, wide_first[0]), out_dtypes[0])
    if sum_shape is not None:
        assert sum_shape[1] in (1, bn) and (sum_shape[1] == 1 or bn == N)
        out_specs.append(_full(sum_shape))
        out_shape.append(jax.ShapeDtypeStruct(sum_shape, F32))
    grid = (N // bn, M // bm) if j_outer else (M // bm, N // bn)
    sem = ("arbitrary", "arbitrary") if sum_shape is not None else ("parallel", "parallel")
    return pl.pallas_call(
        body, grid=grid, in_specs=[a_spec, b_spec] * n + e_specs, out_specs=out_specs, out_shape=out_shape,
        name=name, compiler_params=_params(sem),
    )(*[x for pair in pairs for x in pair], *[arr for arr, _ in extras])


def _rms(x, gain):
    return x * lax.rsqrt(jnp.mean(x * x, axis=-1, keepdims=True) + EPS) * gain


def _silu(x):
    return x * jax.nn.sigmoid(x)


def _act(g, u):
    return _silu(g) * u


def _merge(g0, g1, a_dn, a_swa):
    return jax.nn.sigmoid(g0) * a_dn + jax.nn.sigmoid(g1) * a_swa


def _dn_post(c, is_v, q_scale):
    a = _silu(c)
    rs = lax.rsqrt(jnp.sum(a * a, axis=-1, keepdims=True) + EPS) * q_scale
    return a * jnp.where(is_v, 1.0, rs)


def _dn_out(o, z, gain):
    return _rms(o, gain) * _silu(z)


def _dot(a, b, dims=_DIMS["nn"], hi=False):
    if a.ndim == 3 or b.ndim == 3:
        batch = a.shape[0] if a.ndim == 3 else b.shape[0]
        a = a if a.ndim == 3 else jnp.broadcast_to(a, (batch,) + a.shape)
        b = b if b.ndim == 3 else jnp.broadcast_to(b, (batch,) + b.shape)
        ((ca,), (cb,)), _ = dims
        dims = (((ca + 1,), (cb + 1,)), ((0,), (0,)))
    if hi:
        return lax.dot_general(a, b, dims, precision=HI, preferred_element_type=F32)
    return lax.dot_general(a.astype(BF16), b.astype(BF16), dims, preferred_element_type=F32)


def _pieces(x):
    hi = x.astype(BF16)
    r1 = x - hi.astype(F32)
    mid = r1.astype(BF16)
    return hi, mid, (r1 - mid.astype(F32)).astype(BF16)


def _sel_left_impl(m, x):
    mb = m.astype(BF16)
    hi, mid, lo = _pieces(x)
    return _dot(mb, hi) + (_dot(mb, mid) + _dot(mb, lo))


@jax.custom_vjp
def _sel_left(m, mt, x):
    return _sel_left_impl(m, x)


_sel_left.defvjp(lambda m, mt, x: (_sel_left_impl(m, x), (m, mt)),
                 lambda res, ct: (jnp.zeros_like(res[0]), jnp.zeros_like(res[1]), _sel_left_impl(res[1], ct)))


def _sel_right_impl(x, s):
    sb = s.astype(BF16)
    hi, mid, lo = _pieces(x)
    return _dot(hi, sb) + (_dot(mid, sb) + _dot(lo, sb))


@jax.custom_vjp
def _sel_right(x, s, st):
    return _sel_right_impl(x, s)


_sel_right.defvjp(lambda x, s, st: (_sel_right_impl(x, s), (s, st)),
                  lambda res, ct: (_sel_right_impl(ct, res[1]), jnp.zeros_like(res[0]), jnp.zeros_like(res[1])))


def _sel_nt_impl(s, x):
    sb = s.astype(BF16)
    hi, mid, lo = _pieces(x)
    return _dot(sb, hi, _DIMS["nt"]) + (_dot(sb, mid, _DIMS["nt"]) + _dot(sb, lo, _DIMS["nt"]))


def _sel_tn_impl(x, s):
    sb = s.astype(BF16)
    hi, mid, lo = _pieces(x)
    return _dot(hi, sb, _DIMS["tn"]) + (_dot(mid, sb, _DIMS["tn"]) + _dot(lo, sb, _DIMS["tn"]))


@jax.custom_vjp
def _sel_nt(s, x):
    return _sel_nt_impl(s, x)


_sel_nt.defvjp(lambda s, x: (_sel_nt_impl(s, x), s),
               lambda s, ct: (jnp.zeros_like(s), _sel_tn_impl(ct, s)))


def _dot3_impl(a, b):
    a_hi, a_lo, _ = _pieces(a)
    b_hi, b_lo, _ = _pieces(b)
    return _dot(a_hi, b_hi) + (_dot(a_hi, b_lo) + _dot(a_lo, b_hi))


@jax.custom_vjp
def _dot3(a, b):
    return _dot3_impl(a, b)


_dot3.defvjp(lambda a, b: (_dot3_impl(a, b), (a, b)),
             lambda res, ct: (_dot(ct, res[1], _DIMS["nt"]), _dot(res[0], ct, _DIMS["tn"])))


def _inv_impl(a, eye, strict):
    t = eye - a
    p = _dot(a, a)
    for level in range(5):
        t = t + _dot(t, p)
        if level < 4:
            p = _dot(p, p)
    t = t + _dot(t, eye - t - _dot3_impl(a, t))
    return jnp.where(strict > 0.5, t, eye)


@jax.custom_vjp
def _inv_given(a, t):
    return t.astype(F32)


_inv_given.defvjp(lambda a, t: (t.astype(F32), t),
                  lambda t, ct: (-_dot(_dot(t, ct, _DIMS["tn"]), t, _DIMS["nt"]), jnp.zeros_like(t)))


@jax.custom_vjp
def _lanes_join(a, b):
    return jnp.concatenate([a, b], axis=-1)


_lanes_join.defvjp(lambda a, b: (jnp.concatenate([a, b], axis=-1), None),
                   lambda _, ct: (ct[..., :ct.shape[-1] // 2], ct[..., ct.shape[-1] // 2:]))


@jax.custom_vjp
def _lanes_halves(y):
    h = y.shape[-1] // 2
    return y[..., :h], y[..., h:]


_lanes_halves.defvjp(lambda y: ((y[..., :y.shape[-1] // 2], y[..., y.shape[-1] // 2:]), None),
                     lambda _, ct: (jnp.concatenate(ct, axis=-1),))

GROUP = 4
GROUP_ROWS = GROUP * CHUNK


def _block_consts(n):
    ii = lax.broadcasted_iota(jnp.int32, (n, n), 0)
    jj = lax.broadcasted_iota(jnp.int32, (n, n), 1)
    shift = CHUNK.bit_length() - 1
    same = jnp.right_shift(ii, shift) == jnp.right_shift(jj, shift)
    return same & (ii >= jj), same & (ii <= jj), same & (ii > jj), same, ii == jj


def _lane0(n):
    s = (lax.broadcasted_iota(jnp.int32, (LANES, n), 0) == 0).astype(F32)
    st = (lax.broadcasted_iota(jnp.int32, (n, LANES), 1) == 0).astype(F32)
    return s, st


def _dn_group(q, k, v, g, beta, t_saved=None):
    n = GROUP_ROWS
    low_b, upp_b, strict_b, _, eye_b = _block_consts(n)
    low, upp, eye = low_b.astype(F32), upp_b.astype(F32), eye_b.astype(F32)
    s, st = _lane0(n)
    gc = _sel_left(low, upp, g)
    per_chunk = (g.shape[0], GROUP, CHUNK, LANES)
    gl = jnp.broadcast_to(jnp.sum(g.reshape(per_chunk), axis=2, keepdims=True), per_chunk).reshape(g.shape)
    col = _sel_right(gc, s, st)
    row = _sel_nt(st, gc)
    decay = jnp.exp(jnp.where(low_b, col - row, -jnp.inf))
    kb = k * beta
    vb = v * beta
    a = jnp.where(strict_b, _dot(kb, k, _DIMS["nt"]) * decay, 0.0)
    t = _inv_impl(a, eye, strict_b.astype(F32)) if t_saved is None else _inv_given(a, t_saved)
    u, w = _lanes_halves(_dot3(t, _lanes_join(vb, kb * jnp.exp(gc))))
    return u, w, q * jnp.exp(gc), k * jnp.exp(gl - gc), t


def _dn_chunk(q, k, g):
    ii = lax.broadcasted_iota(jnp.int32, (CHUNK, CHUNK), 0)
    jj = lax.broadcasted_iota(jnp.int32, (CHUNK, CHUNK), 1)
    low = (ii >= jj).astype(F32)
    upp = (ii <= jj).astype(F32)
    s, st = _lane0(CHUNK)
    gc = _sel_left(low, upp, g)
    col = _sel_right(gc, s, st)
    row = _sel_nt(st, gc)
    decay = jnp.exp(jnp.where(ii >= jj, col - row, -jnp.inf))
    qk = _dot(q, k, _DIMS["nt"]) * decay
    return qk, jnp.exp(jnp.sum(g, axis=-2, keepdims=True))


def _dn_step(s, u, w, qe, kd, qk, egl):
    v_new = u - _dot(w, s)
    o = _dot(qe, s) + _dot(qk, v_new)
    s_new = s * egl + _dot(kd, v_new, _DIMS["tn"])
    return s_new, o


def _swa_block(q, kband, vband, qg, kg, sinks, bias, mask):
    kn = _rms(kband, kg)
    qn = _rms(q, qg)
    logits = _dot(qn, kn, _DIMS["nt"]) * (SWA_DIM ** -0.5)
    logits = jnp.where(mask, logits + bias, -jnp.inf)
    m = jnp.maximum(jnp.max(logits, axis=-1, keepdims=True), sinks)
    p = jnp.exp(logits - m)
    denom = jnp.sum(p, axis=-1, keepdims=True) + jnp.exp(sinks - m)
    return _dot(p / denom, vband)


def _adamw(w, g, m, v):
    m = ADAM_B1 * m + (1.0 - ADAM_B1) * g
    v = ADAM_B2 * v + (1.0 - ADAM_B2) * jnp.square(g)
    m_hat = m / (1.0 - ADAM_B1 ** ADAM_STEP)
    v_hat = v / (1.0 - ADAM_B2 ** ADAM_STEP)
    delta = -ADAM_LR * (m_hat / (jnp.sqrt(v_hat) + ADAM_EPS) + ADAM_WD * w)
    return delta, m, v


def _row(tm, c, cb=0):
    return pl.BlockSpec((tm, c), lambda i, cb=cb: (i, cb))


def _full(shape):
    nd = len(shape)
    return pl.BlockSpec(shape, lambda *_, nd=nd: (0,) * nd)


def _norm_fwd(x, gain, name, tm=512):
    S = x.shape[0]

    def body(x_ref, g_ref, h_ref):
        h_ref[...] = _rms(x_ref[...], g_ref[...]).astype(BF16)

    return pl.pallas_call(
        body, grid=(S // tm,), in_specs=[_row(tm, D_MODEL), _full((1, D_MODEL))],
        out_specs=_row(tm, D_MODEL), out_shape=jax.ShapeDtypeStruct((S, D_MODEL), BF16),
        name=name, compiler_params=_params(("parallel",)))(x, gain)


def _shift_down(x, s):
    row = lax.broadcasted_iota(jnp.int32, x.shape, 0)
    return jnp.where(row >= s, pltpu.roll(x, s, axis=0), 0.0)


def _shift_up(x, s):
    n = x.shape[0]
    row = lax.broadcasted_iota(jnp.int32, x.shape, 0)
    return jnp.where(row < n - s, pltpu.roll(x, n - s, axis=0), 0.0)


def _conv(x, w):
    out = w[DN_CONV - 1:DN_CONV] * x
    for s in range(1, DN_CONV):
        out = out + w[DN_CONV - 1 - s:DN_CONV - s] * _shift_down(x, s)
    return out


def _dn_conv_fwd(proj, conv_w):
    S = proj.shape[0]
    nb = DN_QKV // LANES

    def body(x_ref, w_ref, o_ref):
        j = pl.program_id(0)
        q_scale = jnp.where(j < DN_HEADS, DN_DIM ** -0.5, 1.0).astype(F32)
        o_ref[...] = _dn_post(_conv(x_ref[...], w_ref[...]), j >= 2 * DN_HEADS, q_scale)

    return pl.pallas_call(
        body, grid=(nb,),
        in_specs=[pl.BlockSpec((S, LANES), lambda j: (0, P_QKV // LANES + j)),
                  pl.BlockSpec((DN_CONV, LANES), lambda j: (0, j))],
        out_specs=pl.BlockSpec((S, LANES), lambda j: (0, j)),
        out_shape=jax.ShapeDtypeStruct((S, DN_QKV), F32), name="dn_conv_fwd",
        compiler_params=_params(("parallel",)))(proj, conv_w)


def _dn_conv_bwd(proj, conv_w, dqkvn, dproj):
    S = proj.shape[0]
    nb = DN_QKV // LANES

    def body(x_ref, w_ref, d_ref, _, dx_ref, dw_ref):
        j = pl.program_id(0)
        q_scale = jnp.where(j < DN_HEADS, DN_DIM ** -0.5, 1.0).astype(F32)
        x = x_ref[...]
        w = w_ref[...]
        _, vjp = jax.vjp(lambda c: _dn_post(c, j >= 2 * DN_HEADS, q_scale), _conv(x, w))
        (dc,) = vjp(d_ref[0])
        dx = w[DN_CONV - 1:DN_CONV] * dc
        dw_ref[DN_CONV - 1:DN_CONV, :] = jnp.sum(dc * x, axis=0, keepdims=True)
        for s in range(1, DN_CONV):
            dx = dx + w[DN_CONV - 1 - s:DN_CONV - s] * _shift_up(dc, s)
            dw_ref[DN_CONV - 1 - s:DN_CONV - s, :] = jnp.sum(dc * _shift_down(x, s), axis=0, keepdims=True)
        dx_ref[...] = dx.astype(BF16)

    return pl.pallas_call(
        body, grid=(nb,),
        in_specs=[pl.BlockSpec((S, LANES), lambda j: (0, P_QKV // LANES + j)),
                  pl.BlockSpec((DN_CONV, LANES), lambda j: (0, j)),
                  pl.BlockSpec((1, S, LANES), lambda j: (lax.div(j, DN_HEADS), 0, lax.rem(j, DN_HEADS))),
                  pl.BlockSpec(memory_space=pl.ANY)],
        out_specs=[pl.BlockSpec((S, LANES), lambda j: (0, P_QKV // LANES + j)),
                   pl.BlockSpec((DN_CONV, LANES), lambda j: (0, j))],
        out_shape=[jax.ShapeDtypeStruct(dproj.shape, dproj.dtype), jax.ShapeDtypeStruct((DN_CONV, DN_QKV), F32)],
        input_output_aliases={3: 0},
        name="dn_conv_bwd", compiler_params=_params(("parallel",)))(proj, conv_w, dqkvn, dproj)


def _expanders():
    eb = np.zeros((LANES, DN_WIDTH), np.float32)
    ea = np.zeros((LANES, DN_WIDTH), np.float32)
    for h in range(DN_HEADS):
        eb[h, h * DN_DIM:(h + 1) * DN_DIM] = 1.0
        ea[DN_HEADS + h, h * DN_DIM:(h + 1) * DN_DIM] = 1.0
    return jnp.asarray(eb), jnp.asarray(ea), jnp.asarray(eb.T), jnp.asarray(ea.T)


def _dn_gate_args(a_log, dt_bias):
    alog = jnp.repeat(a_log.reshape(1, DN_HEADS), DN_DIM, axis=1)
    dtb = jnp.repeat(dt_bias.reshape(1, DN_HEADS), DN_DIM, axis=1)
    return _expanders() + (alog, dtb)


def _dn_gate_specs(tm):
    return [_row(tm, LANES, P_BA // LANES), _full((LANES, DN_WIDTH)), _full((LANES, DN_WIDTH)),
            _full((DN_WIDTH, LANES)), _full((DN_WIDTH, LANES)), _full((1, DN_WIDTH)), _full((1, DN_WIDTH))]


def _dn_gate_fn(ba, eb, ea, ebt, eat, alog, dtb):
    beta = jax.nn.sigmoid(_sel_right(ba, eb, ebt))
    g = -jnp.exp(alog) * jax.nn.softplus(_sel_right(ba, ea, eat) + dtb)
    return beta, g


def _dn_gate_fwd(proj, a_log, dt_bias, tm=512):
    S = proj.shape[0]
    args = _dn_gate_args(a_log, dt_bias)

    def body(ba_ref, eb_ref, ea_ref, ebt_ref, eat_ref, al_ref, dt_ref, beta_ref, g_ref):
        beta, g = _dn_gate_fn(ba_ref[...], eb_ref[...], ea_ref[...], ebt_ref[...], eat_ref[...], al_ref[...],
                              dt_ref[...])
        beta_ref[...] = beta
        g_ref[...] = g

    return pl.pallas_call(
        body, grid=(S // tm,), in_specs=_dn_gate_specs(tm), out_specs=[_row(tm, DN_WIDTH), _row(tm, DN_WIDTH)],
        out_shape=[jax.ShapeDtypeStruct((S, DN_WIDTH), F32), jax.ShapeDtypeStruct((S, DN_WIDTH), F32)],
        name="dn_gate_fwd", compiler_params=_params(("parallel",)))(proj, *args)


def _dn_gate_bwd(proj, a_log, dt_bias, dbeta, dg, dproj, tm=512):
    S = proj.shape[0]
    args = _dn_gate_args(a_log, dt_bias)

    def body(ba_ref, eb_ref, ea_ref, ebt_ref, eat_ref, al_ref, dt_ref, dbeta_ref, dg_ref, _, dba_ref, dal_ref,
             ddt_ref):
        eb, ea, ebt, eat = eb_ref[...], ea_ref[...], ebt_ref[...], eat_ref[...]
        _, vjp = jax.vjp(lambda ba, al, dt: _dn_gate_fn(ba, eb, ea, ebt, eat, al, dt), ba_ref[...], al_ref[...],
                         dt_ref[...])
        dba, dal, ddt = vjp((dbeta_ref[...], dg_ref[...]))
        dba_ref[...] = dba.astype(BF16)

        @pl.when(pl.program_id(0) == 0)
        def _():
            dal_ref[...] = jnp.zeros_like(dal_ref)
            ddt_ref[...] = jnp.zeros_like(ddt_ref)

        dal_ref[...] += dal
        ddt_ref[...] += ddt

    return pl.pallas_call(
        body, grid=(S // tm,),
        in_specs=_dn_gate_specs(tm) + [_row(tm, DN_WIDTH), _row(tm, DN_WIDTH), pl.BlockSpec(memory_space=pl.ANY)],
        out_specs=[_row(tm, LANES, P_BA // LANES), _full((1, DN_WIDTH)), _full((1, DN_WIDTH))],
        out_shape=[jax.ShapeDtypeStruct(dproj.shape, dproj.dtype), jax.ShapeDtypeStruct((1, DN_WIDTH), F32),
                   jax.ShapeDtypeStruct((1, DN_WIDTH), F32)],
        input_output_aliases={len(args) + 3: 0},
        name="dn_gate_bwd", compiler_params=_params(("arbitrary",)))(proj, *args, dbeta, dg, dproj)


PREP_GROUPS = 4
PREP_CHUNKS = GROUP * PREP_GROUPS


def _dn_prep_specs():
    rows = PREP_CHUNKS * CHUNK
    q = pl.BlockSpec((rows, LANES), lambda h, c: (c, h))
    k = pl.BlockSpec((rows, LANES), lambda h, c: (c, DN_HEADS + h))
    v = pl.BlockSpec((rows, LANES), lambda h, c: (c, 2 * DN_HEADS + h))
    qk = pl.BlockSpec((1, rows, CHUNK), lambda h, c: (h, c, 0))
    egl = pl.BlockSpec((1, PREP_CHUNKS, 1, LANES), lambda h, c: (h, c, 0, 0))
    return q, k, v, qk, egl


def _dn_prep_fwd(qkvn, g, beta):
    S = qkvn.shape[0]
    nc = S // CHUNK
    q, k, v, qks, egl = _dn_prep_specs()

    def body(q_ref, k_ref, v_ref, g_ref, b_ref, u_ref, w_ref, qe_ref, kd_ref, qk_ref, egl_ref, t_ref):
        rows = PREP_CHUNKS * CHUNK
        grp = (PREP_GROUPS, GROUP_ROWS, LANES)
        chk = (PREP_CHUNKS, CHUNK, LANES)
        q, k, g = q_ref[...], k_ref[...], g_ref[...]
        u, w, qe, kd, t = _dn_group(q.reshape(grp), k.reshape(grp), v_ref[...].reshape(grp), g.reshape(grp),
                                    b_ref[...].reshape(grp))
        u_ref[...] = u.reshape(rows, LANES)
        w_ref[...] = w.reshape(rows, LANES)
        qe_ref[...] = qe.reshape(rows, LANES)
        kd_ref[...] = kd.reshape(rows, LANES)
        t_ref[0] = t.reshape(rows, GROUP_ROWS).astype(BF16)
        qk, e = _dn_chunk(q.reshape(chk), k.reshape(chk), g.reshape(chk))
        qk_ref[0] = qk.reshape(rows, CHUNK)
        egl_ref[0] = e

    wide = jax.ShapeDtypeStruct((S, DN_WIDTH), F32)
    return pl.pallas_call(
        body, grid=(DN_HEADS, nc // PREP_CHUNKS), in_specs=[q, k, v, q, q],
        out_specs=[q, q, q, q, qks, egl, _dn_tinv_spec()],
        out_shape=[wide, wide, wide, wide, jax.ShapeDtypeStruct((DN_HEADS, S, CHUNK), F32),
                   jax.ShapeDtypeStruct((DN_HEADS, nc, 1, LANES), F32),
                   jax.ShapeDtypeStruct((DN_HEADS, S, GROUP_ROWS), BF16)],
        name="dn_prep_fwd", compiler_params=_params(("parallel", "parallel")))(qkvn, qkvn, qkvn, g, beta)


def _dn_tinv_spec():
    return pl.BlockSpec((1, PREP_CHUNKS * CHUNK, GROUP_ROWS), lambda h, c: (h, c, 0))


def _dn_prep_bwd(qkvn, g, beta, tinv, du, dw, dqe, dkd, dqk, degl):
    S = qkvn.shape[0]
    nc = S // CHUNK
    q, k, v, qks, egl = _dn_prep_specs()

    def body(q_ref, k_ref, v_ref, g_ref, b_ref, t_ref, du_ref, dw_ref, dqe_ref, dkd_ref, dqk_ref, degl_ref,
             dqkv_ref, dg_ref, db_ref):
        rows = PREP_CHUNKS * CHUNK
        grp = (PREP_GROUPS, GROUP_ROWS, LANES)
        chk = (PREP_CHUNKS, CHUNK, LANES)
        q, k, g = q_ref[...], k_ref[...], g_ref[...]
        t_saved = t_ref[0].reshape(PREP_GROUPS, GROUP_ROWS, GROUP_ROWS)
        _, vjp = jax.vjp(lambda *x: _dn_group(*x, t_saved=t_saved)[:4], q.reshape(grp), k.reshape(grp),
                         v_ref[...].reshape(grp), g.reshape(grp), b_ref[...].reshape(grp))
        dq, dk, dv, dg, db = vjp((du_ref[...].reshape(grp), dw_ref[...].reshape(grp), dqe_ref[...].reshape(grp),
                                  dkd_ref[...].reshape(grp)))
        _, vjp = jax.vjp(_dn_chunk, q.reshape(chk), k.reshape(chk), g.reshape(chk))
        dq2, dk2, dg2 = vjp((dqk_ref[0].reshape(PREP_CHUNKS, CHUNK, CHUNK), degl_ref[0]))
        dqkv_ref[0] = dq.reshape(rows, LANES) + dq2.reshape(rows, LANES)
        dqkv_ref[1] = dk.reshape(rows, LANES) + dk2.reshape(rows, LANES)
        dqkv_ref[2] = dv.reshape(rows, LANES)
        dg_ref[...] = dg.reshape(rows, LANES) + dg2.reshape(rows, LANES)
        db_ref[...] = db.reshape(rows, LANES)

    wide = jax.ShapeDtypeStruct((S, DN_WIDTH), F32)
    rows = PREP_CHUNKS * CHUNK
    return pl.pallas_call(
        body, grid=(DN_HEADS, nc // PREP_CHUNKS), in_specs=[q, k, v, q, q, _dn_tinv_spec(), q, q, q, q, qks, egl],
        out_specs=[pl.BlockSpec((3, rows, LANES), lambda h, c: (0, c, h)), q, q],
        out_shape=[jax.ShapeDtypeStruct((3, S, DN_WIDTH), F32), wide, wide],
        name="dn_prep_bwd", compiler_params=_params(("parallel", "parallel")),
    )(qkvn, qkvn, qkvn, g, beta, tinv, du, dw, dqe, dkd, dqk, degl)


SCAN_CHUNKS = 4


def _dn_scan_specs(nc, reverse):
    nb = nc // SCAN_CHUNKS

    def cidx(c):
        return nb - 1 - c if reverse else c

    hc = pl.BlockSpec((SCAN_CHUNKS * CHUNK, DN_WIDTH), lambda c: (cidx(c), 0))
    qk = pl.BlockSpec((DN_HEADS, SCAN_CHUNKS * CHUNK, CHUNK), lambda c: (0, cidx(c), 0))
    egl = pl.BlockSpec((DN_HEADS, SCAN_CHUNKS, 1, LANES), lambda c: (0, cidx(c), 0, 0))
    st = pl.BlockSpec((DN_HEADS, SCAN_CHUNKS, DN_DIM, DN_DIM), lambda c: (0, cidx(c), 0, 0))
    return hc, qk, egl, st


def _heads(ref, i):
    return jnp.stack([ref[pl.ds(i * CHUNK, CHUNK), pl.ds(h * DN_DIM, DN_DIM)] for h in range(DN_HEADS)])


def _dn_scan_fwd(u, w, qe, kd, qk, egl):
    S = u.shape[0]
    nc = S // CHUNK
    hc, qks, egls, st = _dn_scan_specs(nc, False)

    def body(u_ref, w_ref, qe_ref, kd_ref, qk_ref, egl_ref, o_ref, st_ref, s_scr):
        @pl.when(pl.program_id(0) == 0)
        def _():
            s_scr[...] = jnp.zeros_like(s_scr)

        s = s_scr[...]
        for i in range(SCAN_CHUNKS):
            rows = pl.ds(i * CHUNK, CHUNK)
            st_ref[:, i] = s
            s, o = _dn_step(s, _heads(u_ref, i), _heads(w_ref, i), _heads(qe_ref, i), _heads(kd_ref, i),
                            qk_ref[:, rows, :], egl_ref[:, i])
            for h in range(DN_HEADS):
                o_ref[rows, pl.ds(h * DN_DIM, DN_DIM)] = o[h]
        s_scr[...] = s

    return pl.pallas_call(
        body, grid=(nc // SCAN_CHUNKS,), in_specs=[hc, hc, hc, hc, qks, egls], out_specs=[hc, st],
        out_shape=[jax.ShapeDtypeStruct((S, DN_WIDTH), F32), jax.ShapeDtypeStruct((DN_HEADS, nc, DN_DIM, DN_DIM), F32)],
        scratch_shapes=[pltpu.VMEM((DN_HEADS, DN_DIM, DN_DIM), F32)], name="dn_scan_fwd",
        compiler_params=_params(("arbitrary",)))(u, w, qe, kd, qk, egl)


def _dn_scan_bwd(u, w, qe, kd, qk, egl, states, do):
    S = u.shape[0]
    nc = S // CHUNK
    hc, qks, egls, st = _dn_scan_specs(nc, True)

    def body(u_ref, w_ref, qe_ref, kd_ref, qk_ref, egl_ref, st_ref, do_ref,
             du_ref, dw_ref, dqe_ref, dkd_ref, dqk_ref, degl_ref, ds_scr):
        @pl.when(pl.program_id(0) == 0)
        def _():
            ds_scr[...] = jnp.zeros_like(ds_scr)

        ds = ds_scr[...]
        for i in reversed(range(SCAN_CHUNKS)):
            rows = pl.ds(i * CHUNK, CHUNK)
            _, vjp = jax.vjp(_dn_step, st_ref[:, i], _heads(u_ref, i), _heads(w_ref, i), _heads(qe_ref, i),
                             _heads(kd_ref, i), qk_ref[:, rows, :], egl_ref[:, i])
            ds, du, dw, dqe, dkd, dqk, degl = vjp((ds, _heads(do_ref, i)))
            dqk_ref[:, rows, :] = dqk
            degl_ref[:, i] = degl
            for h in range(DN_HEADS):
                cols = pl.ds(h * DN_DIM, DN_DIM)
                du_ref[rows, cols] = du[h]
                dw_ref[rows, cols] = dw[h]
                dqe_ref[rows, cols] = dqe[h]
                dkd_ref[rows, cols] = dkd[h]
        ds_scr[...] = ds

    wide = jax.ShapeDtypeStruct((S, DN_WIDTH), F32)
    return pl.pallas_call(
        body, grid=(nc // SCAN_CHUNKS,), in_specs=[hc, hc, hc, hc, qks, egls, st, hc],
        out_specs=[hc, hc, hc, hc, qks, egls],
        out_shape=[wide, wide, wide, wide, jax.ShapeDtypeStruct((DN_HEADS, S, CHUNK), F32),
                   jax.ShapeDtypeStruct((DN_HEADS, nc, 1, LANES), F32)],
        scratch_shapes=[pltpu.VMEM((DN_HEADS, DN_DIM, DN_DIM), F32)], name="dn_scan_bwd",
        compiler_params=_params(("arbitrary",)))(u, w, qe, kd, qk, egl, states, do)


def _dn_out_fwd(o, proj, gain, tm=512):
    S = o.shape[0]

    def body(o_ref, z_ref, g_ref, y_ref):
        y_ref[...] = _dn_out(o_ref[...], z_ref[...], g_ref[...]).astype(BF16)

    hs = pl.BlockSpec((tm, LANES), lambda i, h: (i, h))
    zs = pl.BlockSpec((tm, LANES), lambda i, h: (i, P_Z // LANES + h))
    return pl.pallas_call(
        body, grid=(S // tm, DN_HEADS), in_specs=[hs, zs, _full((1, DN_DIM))], out_specs=hs,
        out_shape=jax.ShapeDtypeStruct((S, DN_WIDTH), BF16), name="dn_out_fwd",
        compiler_params=_params(("parallel", "parallel")))(o, proj, gain)


_ANY = pl.BlockSpec(memory_space=pl.ANY)


def _dn_out_bwd(o, proj, gain, dy, dproj, tm=512):
    S = o.shape[0]

    def body(o_ref, z_ref, g_ref, dy_ref, _, do_ref, dz_ref, dg_ref):
        _, vjp = jax.vjp(_dn_out, o_ref[...], z_ref[...], g_ref[...])
        do, dz, dg = vjp(dy_ref[...])
        do_ref[...] = do
        dz_ref[...] = dz.astype(BF16)

        @pl.when((pl.program_id(0) == 0) & (pl.program_id(1) == 0))
        def _():
            dg_ref[...] = jnp.zeros_like(dg_ref)

        dg_ref[...] += dg

    hs = pl.BlockSpec((tm, LANES), lambda i, h: (i, h))
    zs = pl.BlockSpec((tm, LANES), lambda i, h: (i, P_Z // LANES + h))
    return pl.pallas_call(
        body, grid=(S // tm, DN_HEADS), in_specs=[hs, zs, _full((1, DN_DIM)), hs, _ANY],
        out_specs=[hs, zs, _full((1, DN_DIM))],
        out_shape=[jax.ShapeDtypeStruct((S, DN_WIDTH), F32), jax.ShapeDtypeStruct(dproj.shape, dproj.dtype),
                   jax.ShapeDtypeStruct((1, DN_DIM), F32)],
        input_output_aliases={4: 1},
        name="dn_out_bwd", compiler_params=_params(("arbitrary", "arbitrary")))(o, proj, gain, dy, dproj)


def _rel_buckets():
    qi = np.arange(BLOCK)[:, None]
    kj = np.arange(2 * BLOCK)[None, :]
    n = np.maximum(BLOCK + qi - kj, 0)
    max_exact = REL_BUCKETS // 2
    nf = np.maximum(n, 1).astype(np.float32)
    large = max_exact + (np.log(nf / np.float32(max_exact)) / np.float32(math.log(REL_MAX_DIST / max_exact))
                         * np.float32(REL_BUCKETS - max_exact)).astype(np.int32)
    large = np.minimum(large, REL_BUCKETS - 1)
    return np.where(n < max_exact, n, large).astype(np.int32)


def _bias_fwd(rel_bias):
    buckets = jnp.asarray(_rel_buckets())

    def body(rb_ref, bk_ref, o_ref):
        bk = bk_ref[...]
        for h in range(SWA_HEADS):
            acc = jnp.zeros((BLOCK, 2 * BLOCK), F32)
            for b in range(REL_BUCKETS):
                acc = jnp.where(bk == b, rb_ref[b, h], acc)
            o_ref[h] = acc

    return pl.pallas_call(
        body, in_specs=[pl.BlockSpec(memory_space=pltpu.SMEM), pl.BlockSpec(memory_space=pltpu.VMEM)],
        out_specs=pl.BlockSpec(memory_space=pltpu.VMEM),
        out_shape=jax.ShapeDtypeStruct((SWA_HEADS, BLOCK, 2 * BLOCK), F32), name="swa_bias_fwd",
        compiler_params=_params())(rel_bias, buckets)


def _bias_bwd(dbias):
    buckets = jnp.asarray(_rel_buckets())

    def body(d_ref, bk_ref, o_ref):
        bk = bk_ref[...]
        lane = lax.broadcasted_iota(jnp.int32, (1, LANES), 1)
        for h in range(SWA_HEADS):
            d = d_ref[h]
            row = jnp.zeros((1, LANES), F32)
            for b in range(REL_BUCKETS):
                part = jnp.sum(jnp.where(bk == b, d, 0.0), axis=1, keepdims=True)
                row = jnp.where(lane == b, jnp.sum(part, axis=0, keepdims=True), row)
            o_ref[h:h + 1, :] = row

    return pl.pallas_call(
        body, in_specs=[pl.BlockSpec(memory_space=pltpu.VMEM), pl.BlockSpec(memory_space=pltpu.VMEM)],
        out_specs=pl.BlockSpec(memory_space=pltpu.VMEM),
        out_shape=jax.ShapeDtypeStruct((SWA_HEADS, LANES), F32), name="swa_bias_bwd",
        compiler_params=_params())(dbias, buckets)


def _swa_mask(n):
    qi = lax.broadcasted_iota(jnp.int32, (BLOCK, 2 * BLOCK), 0)
    kj = lax.broadcasted_iota(jnp.int32, (BLOCK, 2 * BLOCK), 1)
    dist = BLOCK + qi - kj
    return (dist >= 0) & (dist < WINDOW) & ((n > 0) | (kj >= BLOCK))


def _swa_in_specs():
    q = pl.BlockSpec((BLOCK, SWA_WIDTH), lambda n: (n, P_SQ // SWA_WIDTH))
    kc = pl.BlockSpec((BLOCK, SWA_KVW), lambda n: (n, P_SK // SWA_KVW))
    kp = pl.BlockSpec((BLOCK, SWA_KVW), lambda n: (jnp.maximum(n - 1, 0), P_SK // SWA_KVW))
    vc = pl.BlockSpec((BLOCK, SWA_KVW), lambda n: (n, P_SV // SWA_KVW))
    vp = pl.BlockSpec((BLOCK, SWA_KVW), lambda n: (jnp.maximum(n - 1, 0), P_SV // SWA_KVW))
    small = [_full((1, SWA_DIM)), _full((1, SWA_DIM)), _full((1, SWA_HEADS)),
             _full((SWA_HEADS, BLOCK, 2 * BLOCK))]
    return [q, kp, kc, vp, vc] + small


def _swa_load(q_ref, kp_ref, kc_ref, vp_ref, vc_ref, s_ref):
    q = jnp.stack([q_ref[:, pl.ds(h * SWA_DIM, SWA_DIM)] for h in range(SWA_HEADS)])
    kbands, vbands = [], []
    for kv in range(SWA_KV):
        cols = pl.ds(kv * SWA_DIM, SWA_DIM)
        kbands += [jnp.concatenate([kp_ref[:, cols], kc_ref[:, cols]], axis=0)] * SWA_GROUP
        vbands += [jnp.concatenate([vp_ref[:, cols], vc_ref[:, cols]], axis=0)] * SWA_GROUP
    sinks = jnp.stack([s_ref[:, pl.ds(h, 1)] for h in range(SWA_HEADS)])
    return q, jnp.stack(kbands), jnp.stack(vbands), sinks


def _swa_fwd(proj, q_gain, k_gain, sinks, bias):
    S = proj.shape[0]

    def body(q_ref, kp_ref, kc_ref, vp_ref, vc_ref, qg_ref, kg_ref, s_ref, bias_ref, y_ref):
        mask = _swa_mask(pl.program_id(0))
        q, kband, vband, sk = _swa_load(q_ref, kp_ref, kc_ref, vp_ref, vc_ref, s_ref)
        out = _swa_block(q, kband, vband, qg_ref[...], kg_ref[...], sk, bias_ref[...], mask)
        for h in range(SWA_HEADS):
            y_ref[:, pl.ds(h * SWA_DIM, SWA_DIM)] = out[h].astype(BF16)

    return pl.pallas_call(
        body, grid=(S // BLOCK,), in_specs=_swa_in_specs(),
        out_specs=pl.BlockSpec((BLOCK, SWA_WIDTH), lambda n: (n, 0)),
        out_shape=jax.ShapeDtypeStruct((S, SWA_WIDTH), BF16), name="swa_fwd",
        compiler_params=_params(("parallel",)))(proj, proj, proj, proj, proj, q_gain, k_gain, sinks, bias)


def _swa_bwd(proj, q_gain, k_gain, sinks, bias, dy, dproj):
    S = proj.shape[0]

    def body(q_ref, kp_ref, kc_ref, vp_ref, vc_ref, qg_ref, kg_ref, s_ref, bias_ref, dy_ref, _,
             dq_ref, dk_ref, dv_ref, dqg_ref, dkg_ref, ds_ref, dbias_ref):
        n = pl.program_id(0)
        mask = _swa_mask(n)

        @pl.when(n == 0)
        def _():
            for r in (dk_ref, dv_ref, dqg_ref, dkg_ref, ds_ref, dbias_ref):
                r[...] = jnp.zeros_like(r)

        cur = pl.ds(pl.multiple_of(n * BLOCK, BLOCK), BLOCK)
        prev = pl.ds(pl.multiple_of(jnp.maximum(n - 1, 0) * BLOCK, BLOCK), BLOCK)
        q, kband, vband, sk = _swa_load(q_ref, kp_ref, kc_ref, vp_ref, vc_ref, s_ref)
        _, vjp = jax.vjp(lambda q, kb, vb, qg, kg, sk, bs: _swa_block(q, kb, vb, qg, kg, sk, bs, mask),
                         q, kband, vband, qg_ref[...], kg_ref[...], sk, bias_ref[...])
        dy = jnp.stack([dy_ref[:, pl.ds(h * SWA_DIM, SWA_DIM)] for h in range(SWA_HEADS)])
        dq, dkb, dvb, dqg, dkg, dsk, dbs = vjp(dy)
        for h in range(SWA_HEADS):
            dq_ref[:, pl.ds(h * SWA_DIM, SWA_DIM)] = dq[h].astype(BF16)
            ds_ref[:, pl.ds(h, 1)] += dsk[h]
        dbias_ref[...] += dbs
        dqg_ref[...] += dqg
        dkg_ref[...] += dkg
        for kv in range(SWA_KV):
            cols = pl.ds(kv * SWA_DIM, SWA_DIM)
            group = range(kv * SWA_GROUP, (kv + 1) * SWA_GROUP)
            dk_kv = sum(dkb[h] for h in group)
            dv_kv = sum(dvb[h] for h in group)
            dk_ref[cur, cols] += dk_kv[BLOCK:]
            dv_ref[cur, cols] += dv_kv[BLOCK:]

            @pl.when(n > 0)
            def _(cols=cols, dk_kv=dk_kv, dv_kv=dv_kv):
                dk_ref[prev, cols] += dk_kv[:BLOCK]
                dv_ref[prev, cols] += dv_kv[:BLOCK]

    return pl.pallas_call(
        body, grid=(S // BLOCK,),
        in_specs=_swa_in_specs() + [pl.BlockSpec((BLOCK, SWA_WIDTH), lambda n: (n, 0)),
                                    pl.BlockSpec(memory_space=pl.ANY)],
        out_specs=[pl.BlockSpec((BLOCK, SWA_WIDTH), lambda n: (n, P_SQ // SWA_WIDTH)), _full((S, SWA_KVW)),
                   _full((S, SWA_KVW)), _full((1, SWA_DIM)), _full((1, SWA_DIM)), _full((1, SWA_HEADS)),
                   _full((SWA_HEADS, BLOCK, 2 * BLOCK))],
        out_shape=[jax.ShapeDtypeStruct(dproj.shape, dproj.dtype), jax.ShapeDtypeStruct((S, SWA_KVW), F32),
                   jax.ShapeDtypeStruct((S, SWA_KVW), F32), jax.ShapeDtypeStruct((1, SWA_DIM), F32),
                   jax.ShapeDtypeStruct((1, SWA_DIM), F32), jax.ShapeDtypeStruct((1, SWA_HEADS), F32),
                   jax.ShapeDtypeStruct((SWA_HEADS, BLOCK, 2 * BLOCK), F32)],
        input_output_aliases={10: 0},
        name="swa_bwd", compiler_params=_params(("arbitrary",)),
    )(proj, proj, proj, proj, proj, q_gain, k_gain, sinks, bias, dy, dproj)


def _kv_into(dproj, dk, dv, tm=512):
    S = dk.shape[0]

    def body(dk_ref, dv_ref, _, o_ref):
        o_ref[:, :SWA_KVW] = dk_ref[...].astype(BF16)
        o_ref[:, SWA_KVW:] = dv_ref[...].astype(BF16)

    return pl.pallas_call(
        body, grid=(S // tm,), in_specs=[_row(tm, SWA_KVW), _row(tm, SWA_KVW), pl.BlockSpec(memory_space=pl.ANY)],
        out_specs=_row(tm, 2 * SWA_KVW, P_SK // (2 * SWA_KVW)),
        out_shape=jax.ShapeDtypeStruct(dproj.shape, dproj.dtype), input_output_aliases={2: 0},
        name="swa_kv_into", compiler_params=_params(("parallel",)))(dk, dv, dproj)


def _position():
    return lax.axis_index("x"), lax.axis_index("y"), lax.axis_index("c")


def _all_gather(shards, name="all_gather_weights"):
    na = len(shards)

    def body(*refs):
        x_refs, out_refs = refs[:na], refs[na:2 * na]
        send_sems, recv_sems, local_sems = refs[2 * na:]
        x, y, c = _position()
        me, sibling = (x, y, c), (x, y, 1 - c)
        chips = [(1 - x, y), (x, 1 - y), (1 - x, 1 - y)]

        def copy(a, k, block, to, own=False):
            px, py, pc = block
            slot = out_refs[a].at[4 * px + 2 * py + pc]
            return pltpu.make_async_remote_copy(
                src_ref=x_refs[a] if own else slot, dst_ref=slot, send_sem=send_sems.at[7 * a + k],
                recv_sem=recv_sems.at[7 * a + k], device_id=to, device_id_type=MESH_ID)

        mine = [pltpu.make_async_copy(x_refs[a], out_refs[a].at[4 * x + 2 * y + c], local_sems.at[a])
                for a in range(na)]
        for cp in mine:
            cp.start()
        first = []
        for a in range(na):
            first.append(copy(a, 0, me, sibling, own=True))
            first += [copy(a, 1 + j, me, (*chip, c), own=True) for j, chip in enumerate(chips)]
        for cp in first:
            cp.start()
        passed = []
        for j, chip in enumerate(chips):
            for a in range(na):
                copy(a, 1 + j, (*chip, c), me).wait_recv()
                passed.append(copy(a, 4 + j, (*chip, c), sibling))
                passed[-1].start()
        for a in range(na):
            copy(a, 0, sibling, me).wait_recv()
            for j, chip in enumerate(chips):
                copy(a, 4 + j, (*chip, 1 - c), me).wait_recv()
        for cp in first + passed:
            cp.wait_send()
        for cp in mine:
            cp.wait()

    return pl.pallas_call(
        body, in_specs=[pl.BlockSpec(memory_space=pl.ANY)] * na, out_specs=[pl.BlockSpec(memory_space=pl.ANY)] * na,
        out_shape=[jax.ShapeDtypeStruct((N_DEV,) + s.shape, s.dtype) for s in shards],
        scratch_shapes=[pltpu.SemaphoreType.DMA((7 * na,)), pltpu.SemaphoreType.DMA((7 * na,)),
                        pltpu.SemaphoreType.DMA((na,))],
        name=name)(*shards)


_HBM = pl.BlockSpec(memory_space=pltpu.HBM)
_SEM = pl.BlockSpec(memory_space=pltpu.SEMAPHORE)
_DATAFLOW = pltpu.SideEffectType.DATAFLOW_SIDE_EFFECTING


def _peers(x, y, c):
    out = []
    for k in range(1, N_DEV):
        px, py, pc = x ^ (k >> 2), y ^ ((k >> 1) & 1), c ^ (k & 1)
        out.append(((px, py, pc), 4 * px + 2 * py + pc))
    return out


def _split_copies(src_refs, land_refs, send_sems, recv_sems, scatter):
    x, y, c = _position()
    me = 4 * x + 2 * y + c
    sends, recvs = [], []
    for k, (peer_id, peer) in enumerate(_peers(x, y, c)):
        for a, (src, land) in enumerate(zip(src_refs, land_refs)):
            sems = dict(send_sem=send_sems.at[7 * a + k], recv_sem=recv_sems.at[7 * a + k],
                        device_id=peer_id, device_id_type=MESH_ID)
            mine = src.at[peer] if scatter else src
            sends.append(pltpu.make_async_remote_copy(src_ref=mine, dst_ref=land.at[me], **sems))
            recvs.append(pltpu.make_async_remote_copy(src_ref=mine, dst_ref=land.at[peer], **sems))
    return sends, recvs


def _all_gather_direct(shards, name, after):
    na = len(shards)

    def body(*refs):
        x_refs, out_refs = refs[:na], refs[na + 1:2 * na + 1]
        send_sems, recv_sems, local_sems = refs[2 * na + 1:]
        x, y, c = _position()
        me = 4 * x + 2 * y + c
        local = [pltpu.make_async_copy(x_refs[a], out_refs[a].at[me], local_sems.at[a]) for a in range(na)]
        sends, recvs = _split_copies(x_refs, out_refs, send_sems, recv_sems, False)
        for cp in local + sends:
            cp.start()
        for cp in recvs:
            cp.wait_recv()
        for cp in sends:
            cp.wait_send()
        for cp in local:
            cp.wait()

    return pl.pallas_call(
        body, in_specs=[pl.BlockSpec(memory_space=pl.ANY)] * (na + 1),
        out_specs=[pl.BlockSpec(memory_space=pl.ANY)] * na,
        out_shape=[jax.ShapeDtypeStruct((N_DEV,) + s.shape, s.dtype) for s in shards],
        scratch_shapes=[pltpu.SemaphoreType.DMA((7 * na,)), pltpu.SemaphoreType.DMA((7 * na,)),
                        pltpu.SemaphoreType.DMA((na,))],
        name=name)(*shards, after)


def _exchange_start(srcs, scatter, name, after=None):
    na = len(srcs)
    lands = [lax.empty(s.shape if scatter else (N_DEV,) + s.shape, s.dtype) for s in srcs]
    extra = [] if after is None else [after]

    def body(*refs):
        src_refs, land_refs = refs[:na], refs[na:2 * na]
        send_sems, recv_sems = refs[2 * na + len(extra)], refs[2 * na + len(extra) + 1]
        token = refs[-1]
        sends, _ = _split_copies(src_refs, land_refs, send_sems, recv_sems, scatter)
        for cp in sends:
            cp.start()
        token[...] = jnp.zeros_like(token)

    hbm = lambda a: pltpu.HBM(a.shape, a.dtype)
    out = pl.pallas_call(
        body, name=name,
        out_shape=(pltpu.SemaphoreType.DMA((7 * na,)), pltpu.SemaphoreType.DMA((7 * na,)),
                   *[hbm(s) for s in srcs], *[hbm(l) for l in lands], jax.ShapeDtypeStruct((8, LANES), F32)),
        in_specs=[_HBM] * (2 * na) + [pl.BlockSpec(memory_space=pl.ANY)] * len(extra),
        out_specs=(_SEM, _SEM, *[_HBM] * (2 * na), pl.BlockSpec(memory_space=pltpu.VMEM)),
        input_output_aliases={i: 2 + i for i in range(2 * na)},
        compiler_params=pltpu.CompilerParams(has_side_effects=_DATAFLOW),
    )(*[pltpu.with_memory_space_constraint(s, pltpu.HBM) for s in srcs],
      *[pltpu.with_memory_space_constraint(l, pltpu.HBM) for l in lands], *extra)
    return (out[0], out[1], list(out[2:2 + na]), list(out[2 + na:2 + 2 * na])), out[-1]


def _exchange_wait(handle, after, scatter, name):
    send_sems, recv_sems, srcs, lands = handle
    na = len(srcs)

    def body(*refs):
        src_refs, land_refs = refs[:na], refs[na:2 * na]
        s_sems, r_sems = refs[2 * na], refs[2 * na + 1]
        sends, recvs = _split_copies(src_refs, land_refs, s_sems, r_sems, scatter)
        for cp in sends:
            cp.wait_send()
        for cp in recvs:
            cp.wait_recv()

    hbm = lambda a: pltpu.HBM(a.shape, a.dtype)
    out = pl.pallas_call(
        body, name=name, out_shape=(*[hbm(s) for s in srcs], *[hbm(l) for l in lands]),
        in_specs=[_HBM] * (2 * na) + [_SEM, _SEM, pl.BlockSpec(memory_space=pl.ANY)],
        out_specs=tuple([_HBM] * (2 * na)), input_output_aliases={i: i for i in range(2 * na)},
        compiler_params=pltpu.CompilerParams(has_side_effects=_DATAFLOW),
    )(*srcs, *lands, send_sems, recv_sems, after)
    return list(out[:na]), list(out[na:])


def _own_slot(landed, own):
    me = 4 * lax.axis_index("x") + 2 * lax.axis_index("y") + lax.axis_index("c")
    return lax.dynamic_update_slice_in_dim(landed, own[None], me, axis=0)


def _adam_update(parts, w, m, v, name, tr=256):
    _, r, c = w.shape
    tr = _pick_rows(r, tr)
    cp = parts.shape[2]

    def body(p_ref, w_ref, m_ref, v_ref, g_ref, d_ref, nm_ref, nv_ref):
        g = p_ref[0, :, pl.ds(0, c)].astype(F32)
        for i in range(1, N_DEV):
            g = g + p_ref[i, :, pl.ds(0, c)].astype(F32)
        delta, nm, nv = _adamw(w_ref[0], g, m_ref[0], v_ref[0])
        g_ref[0] = g
        d_ref[0] = delta
        nm_ref[0] = nm
        nv_ref[0] = nv

    rs = pl.BlockSpec((1, tr, c), lambda i: (0, i, 0))
    return pl.pallas_call(
        body, grid=(r // tr,), in_specs=[pl.BlockSpec((N_DEV, tr, cp), lambda i: (0, i, 0)), rs, rs, rs],
        out_specs=[rs] * 4, out_shape=[jax.ShapeDtypeStruct((1, r, c), F32)] * 4, name=name,
        compiler_params=_params(("parallel",)))(parts, w, m, v)


def _pick_rows(rows, target):
    if rows <= target:
        return rows
    t = target
    while t >= 16:
        if rows % t == 0:
            return t
        t -= 16
    return rows


BIG = ("w_in", "w_branch_dn", "w_branch_swa", "w_out", "w_gate", "w_up", "w_down")
IN_SHARD, IN_WIRE = D_IN // N_DEV, 640
FF_SHARD, FF_WIRE = D_FF // N_DEV, 384
D_FFP = N_DEV * FF_WIRE
BIG_SHAPES = {"w_in": ((D_MODEL, IN_SHARD), (D_MODEL, IN_WIRE)),
              "w_branch_dn": ((DN_WIDTH, LANES), (DN_WIDTH, LANES)),
              "w_branch_swa": ((SWA_WIDTH, LANES), (SWA_WIDTH, LANES)),
              "w_out": ((LANES, D_MODEL), (LANES, D_MODEL)),
              "w_gate": ((D_MODEL, FF_SHARD), (D_MODEL, FF_WIRE)),
              "w_up": ((D_MODEL, FF_SHARD), (D_MODEL, FF_WIRE)),
              "w_down": ((FF_SHARD, D_MODEL), (FF_WIRE, D_MODEL))}
CONV_SHARD, CONV_WIRE = (DN_CONV, DN_QKV // N_DEV), (8, 256)


def _pad_to(a, shape):
    return jnp.pad(a, [(0, t - s) for s, t in zip(a.shape, shape)])


_IN_SEGS = ((R_GATE, 2048, P_GATE), (R_QKV, DN_QKV, P_QKV), (R_Z, DN_WIDTH, P_Z), (R_SQ, SWA_WIDTH, P_SQ),
            (R_SK, SWA_KVW, P_SK), (R_SV, SWA_KVW, P_SV), (R_B, 8, P_BA))


def _w_in_from_blocks(blocks):
    parts = []
    for rs, n, _ in _IN_SEGS:
        for dev in range(N_DEV):
            lo, hi = max(rs, IN_SHARD * dev), min(rs + n, IN_SHARD * (dev + 1))
            if lo < hi:
                parts.append(blocks[dev, :, lo - IN_SHARD * dev:hi - IN_SHARD * dev])
    parts.append(jnp.zeros((blocks.shape[1], P_WIDTH - P_BA - 8), blocks.dtype))
    return jnp.concatenate(parts, axis=1)


def _w_in_to_blocks(g):
    out = []
    for dev in range(N_DEV):
        parts = []
        for rs, n, ps in sorted(_IN_SEGS):
            lo, hi = max(rs, IN_SHARD * dev), min(rs + n, IN_SHARD * (dev + 1))
            if lo < hi:
                parts.append(g[:, ps + lo - rs:ps + hi - rs])
        parts.append(jnp.zeros((g.shape[0], IN_WIRE - IN_SHARD), g.dtype))
        out.append(jnp.concatenate(parts, axis=1))
    return jnp.stack(out)


SMALL = {"attn_norm": (0, (1, D_MODEL)), "ffn_norm": (1, (1, D_MODEL)), "dn_out_norm": (2, (1, DN_DIM)),
         "swa_q_norm": (3, (1, SWA_DIM)), "swa_k_norm": (4, (1, SWA_DIM)), "dn_a_log": (5, (1, DN_HEADS)),
         "dn_dt_bias": (6, (1, DN_HEADS)), "swa_sinks": (7, (1, SWA_HEADS)), "rel_bias": (8, (REL_BUCKETS, SWA_HEADS))}
SMALL_SHEET = (48, D_MODEL)


def _small_pack(grads):
    names = list(SMALL)

    def body(*refs):
        o_ref = refs[-1]
        o_ref[...] = jnp.zeros_like(o_ref)
        for n, ref in zip(names, refs):
            r0, (nr, nc) = SMALL[n]
            o_ref[r0:r0 + nr, 0:nc] = ref[...]

    return pl.pallas_call(
        body, in_specs=[pl.BlockSpec(memory_space=pltpu.VMEM)] * len(names),
        out_specs=pl.BlockSpec(memory_space=pltpu.VMEM), out_shape=jax.ShapeDtypeStruct(SMALL_SHEET, F32),
        name="small_pack", compiler_params=_params())(*[grads[n].reshape(SMALL[n][1]) for n in names])


def _small_update(sheets, w, m, v):
    names = list(SMALL)
    k = len(names)

    def body(*refs):
        p_ref = refs[0]
        ins, outs = refs[1:1 + 3 * k], refs[1 + 3 * k:]
        for t, n in enumerate(names):
            r0, (nr, nc) = SMALL[n]
            g = p_ref[0, r0:r0 + nr, 0:nc]
            for i in range(1, N_DEV):
                g = g + p_ref[i, r0:r0 + nr, 0:nc]
            delta, nm, nv = _adamw(ins[t][...], g, ins[k + t][...], ins[2 * k + t][...])
            for kind, val in enumerate((g, delta, nm, nv)):
                outs[kind * k + t][...] = val

    shapes = [jax.ShapeDtypeStruct(SMALL[n][1], F32) for n in names]
    vm = pl.BlockSpec(memory_space=pltpu.VMEM)
    res = pl.pallas_call(
        body, in_specs=[vm] * (1 + 3 * k), out_specs=[vm] * (4 * k), out_shape=shapes * 4, name="adam_small",
        compiler_params=_params(),
    )(sheets, *[d[n].reshape(SMALL[n][1]) for d in (w, m, v) for n in names])
    return {n: tuple(res[kind * k + t] for kind in range(4)) for t, n in enumerate(names)}


def kernel(x, attn_norm, w_in, dn_conv, dn_a_log, dn_dt_bias, dn_out_norm, swa_q_norm, swa_k_norm, swa_sinks, rel_bias, w_branch_dn, w_branch_swa, w_out, ffn_norm, w_gate, w_up, w_down, loss_target, m_attn_norm, m_w_in, m_dn_conv, m_dn_a_log, m_dn_dt_bias, m_dn_out_norm, m_swa_q_norm, m_swa_k_norm, m_swa_sinks, m_rel_bias, m_w_branch_dn, m_w_branch_swa, m_w_out, m_ffn_norm, m_w_gate, m_w_up, m_w_down, v_attn_norm, v_w_in, v_dn_conv, v_dn_a_log, v_dn_dt_bias, v_dn_out_norm, v_swa_q_norm, v_swa_k_norm, v_swa_sinks, v_rel_bias, v_w_branch_dn, v_w_branch_swa, v_w_out, v_ffn_norm, v_w_gate, v_w_up, v_w_down):
    args = dict(locals())
    S = x.shape[1]
    xs = x.reshape(S, D_MODEL)
    target = loss_target.reshape(S, D_MODEL)

    w_loc = {n: args[n].reshape(BIG_SHAPES[n][0]) for n in BIG}
    conv_loc = dn_conv.reshape(CONV_SHARD)
    wire = {n: _pad_to(w_loc[n], BIG_SHAPES[n][1]).astype(BF16) for n in BIG}
    first = _all_gather([wire["w_in"], _pad_to(conv_loc, CONV_WIRE)])
    later = [n for n in BIG if n != "w_in"]
    rest_handle, rest_token = _exchange_start([wire[n] for n in later], False, "gather_rest_start", after=first[1])
    w_pad = _w_in_from_blocks(first[0])
    conv_w = jnp.concatenate([first[1][d, :DN_CONV, :CONV_SHARD[1]] for d in range(N_DEV)], axis=1)

    h = _norm_fwd(xs, attn_norm + rest_token[0, 0], "norm1_fwd")
    proj = _mm([(h, w_pad)], "nn", F32, "mm_in", 512, 1664, j_outer=True)
    qkvn = _dn_conv_fwd(proj, conv_w)
    beta, g = _dn_gate_fwd(proj, dn_a_log, dn_dt_bias)
    u, w, qe, kd, qk, egl, tinv = _dn_prep_fwd(qkvn, g, beta)
    o, states = _dn_scan_fwd(u, w, qe, kd, qk, egl)
    y_dn = _dn_out_fwd(o, proj, dn_out_norm)
    bias = _bias_fwd(rel_bias)
    y_swa = _swa_fwd(proj, swa_q_norm, swa_k_norm, swa_sinks, bias)
    rest_src, rest_land = _exchange_wait(rest_handle, y_swa, False, "gather_rest_wait")
    G = {n: _own_slot(land, src) for n, src, land in zip(later, rest_src, rest_land)}
    w_bdn, w_bswa, w_g, w_u = G["w_branch_dn"], G["w_branch_swa"], G["w_gate"], G["w_up"]
    w_o = G["w_out"].reshape(D_MODEL, D_MODEL)
    w_d = G["w_down"].reshape(D_FFP, D_MODEL)
    gates = [(proj, P_GATE // 512), (proj, (P_GATE + D_MODEL) // 512)]
    a_dn, a_swa, merged = _mm_fused(
        [(y_dn, w_bdn), (y_swa, w_bswa)], "nn", "mm_branch_merge", 1024, 512,
        lambda p, e: (p[0], p[1], _merge(e[0], e[1], p[0], p[1])), gates, (F32, F32, BF16), b_blocks=True)

    def resid_norm(p, e):
        x1 = e[0] + p[0]
        return x1, _rms(x1, e[1])

    x1, h2 = _mm_fused([(merged, w_o)], "nn", "mm_out_norm", 512, D_MODEL, resid_norm,
                       [(xs, 0), (ffn_norm, None)], (F32, BF16))
    gate, up, act = _mm_fused([(h2, w_g), (h2, w_u)], "nn", "mm_gate_up_act", 1024, 768,
                              lambda p, e: (p[0], p[1], _act(p[0], p[1])), [], (F32, F32, BF16),
                              j_outer=True, b_blocks=True)

    def loss_head(p, e):
        diff = e[0] + p[0] - e[1]
        dy = diff * (1.0 / D_MODEL)
        part = jnp.sum(jnp.mean(diff * diff, axis=-1, keepdims=True), axis=0, keepdims=True) * 0.5
        return dy, dy, part

    dy, dy_b, loss_local = _mm_fused([(act, w_d)], "nn", "mm_down_loss", 512, D_MODEL, loss_head,
                                     [(x1, 0), (target, 0)], (F32, BF16), sum_shape=(1, 1))

    def act_bwd(p, e):
        _, vjp = jax.vjp(_act, e[0], e[1])
        return vjp(p[0])

    dgate, dup = _mm_fused([(dy_b, w_d)], "nt", "mm_dact_act", 1024, 768, act_bwd, [(gate, 0), (up, 0)],
                           (BF16, BF16), j_outer=True)
    g_w_down = _mm([(act, dy_b)], "tn", BF16, "mm_dw_down", 768, D_MODEL, j_outer=True)
    g_w_down = g_w_down.reshape(N_DEV, FF_WIRE, D_MODEL)
    g_w_gate = _mm([(h2, dgate)], "tn", BF16, "mm_dw_gate", D_MODEL, 768, out_blocks=True)
    g_w_up = _mm([(h2, dup)], "tn", BF16, "mm_dw_up", D_MODEL, 768, out_blocks=True)
    ffn_handle, ffn_token = _exchange_start([g_w_down, g_w_gate, g_w_up], True, "scatter_ffn_start")

    def norm_bwd(p, e):
        _, vjp = jax.vjp(_rms, e[0], e[2])
        dx, dgain = vjp(sum(p))
        dx = dx + e[1]
        return dx, dx, dgain

    dx1, dx1_b, g_ffn_norm = _mm_fused(
        [(dgate, w_g), (dup, w_u)], "nt", "mm_dh2_norm", 256, D_MODEL, norm_bwd,
        [(x1, 0), (dy, 0), (ffn_norm + ffn_token[0, 0], None)], (F32, BF16), b_blocks=True, sum_shape=(1, D_MODEL))
    def merge_bwd(p, e):
        _, vjp = jax.vjp(_merge, *e)
        dg0, dg1, da_dn, da_swa = vjp(p[0])
        return jnp.concatenate([dg0, dg1], axis=1), da_dn, da_swa

    dproj, da_dn, da_swa = _mm_fused(
        [(dx1_b, w_o)], "nt", "mm_dmerged_merge", 512, D_MODEL, merge_bwd,
        [(proj, P_GATE // D_MODEL), (proj, P_GATE // D_MODEL + 1), (a_dn, 0), (a_swa, 0)], (BF16,) * 3,
        wide_first=(P_WIDTH, 2 * D_MODEL))
    g_w_out = _mm([(merged, dx1_b)], "tn", BF16, "mm_dw_out", 512, D_MODEL, j_outer=True)
    g_w_out = g_w_out.reshape(N_DEV, LANES, D_MODEL)
    dy_dn = _mm([(da_dn, w_bdn)], "nt", F32, "mm_dy_dn", 1024, DN_WIDTH, b_blocks=True)
    dy_swa = _mm([(da_swa, w_bswa)], "nt", F32, "mm_dy_swa", 1024, SWA_WIDTH, b_blocks=True)
    g_w_bdn = _mm([(y_dn, da_dn)], "tn", BF16, "mm_dw_branch_dn", DN_WIDTH, 512, out_blocks=True)
    g_w_bswa = _mm([(y_swa, da_swa)], "tn", BF16, "mm_dw_branch_swa", SWA_WIDTH, 512, out_blocks=True)
    dproj, dsk, dsv, g_q_norm, g_k_norm, g_sinks, dbias = _swa_bwd(proj, swa_q_norm, swa_k_norm, swa_sinks, bias,
                                                                   dy_swa, dproj)
    dproj = _kv_into(dproj, dsk, dsv)
    g_rel_bias = _bias_bwd(dbias)[:, :REL_BUCKETS].T
    mix_handle, mix_token = _exchange_start([g_w_out, g_w_bdn, g_w_bswa], True, "scatter_mix_start")
    do, dproj, g_out_norm = _dn_out_bwd(o, proj, dn_out_norm + mix_token[0, 0], dy_dn, dproj)
    du, dw, dqe, dkd, dqk, degl = _dn_scan_bwd(u, w, qe, kd, qk, egl, states, do)
    dqkvn, dgd, dbeta = _dn_prep_bwd(qkvn, g, beta, tinv, du, dw, dqe, dkd, dqk, degl)
    dproj, dal, ddt = _dn_gate_bwd(proj, dn_a_log, dn_dt_bias, dbeta, dgd, dproj)
    g_a_log = dal.reshape(DN_HEADS, DN_DIM).sum(axis=1)
    g_dt_bias = ddt.reshape(DN_HEADS, DN_DIM).sum(axis=1)
    dproj, g_conv = _dn_conv_bwd(proj, conv_w, dqkvn, dproj)
    g_w_in = _w_in_to_blocks(_mm([(h, dproj)], "tn", BF16, "mm_dw_in", 512, 1664, j_outer=True))
    in_handle, in_token = _exchange_start([g_w_in], True, "scatter_in_start")
    dx, g_attn_norm = _mm_fused(
        [(dproj, w_pad)], "nt", "mm_dh_norm", 512, D_MODEL, lambda p, e: norm_bwd(p, e)[1:],
        [(xs, 0), (dx1, 0), (attn_norm + in_token[0, 0], None)], (F32,), sum_shape=(1, D_MODEL))

    g_small = {"attn_norm": g_attn_norm, "ffn_norm": g_ffn_norm, "rel_bias": g_rel_bias, "dn_out_norm": g_out_norm,
               "swa_q_norm": g_q_norm, "swa_k_norm": g_k_norm, "dn_a_log": g_a_log, "dn_dt_bias": g_dt_bias,
               "swa_sinks": g_sinks}
    me = 4 * lax.axis_index("x") + 2 * lax.axis_index("y") + lax.axis_index("c")
    outs = {}

    def finish(handle, group, name, after):
        srcs, lands = _exchange_wait(handle, after, True, name)
        for n, src, land in zip(group, srcs, lands):
            parts = _own_slot(land, lax.dynamic_index_in_dim(src, me, 0, keepdims=False))
            outs[n] = _adam_update(parts, args[n], args["m_" + n], args["v_" + n], "adam_" + n)

    finish(ffn_handle, ("w_down", "w_gate", "w_up"), "scatter_ffn_wait", dx)
    finish(mix_handle, ("w_out", "w_branch_dn", "w_branch_swa"), "scatter_mix_wait", dx)
    sheets, conv_all = _all_gather_direct([_small_pack(g_small), _pad_to(g_conv, (8, DN_QKV))], "all_gather_small",
                                          after=outs["w_up"][0])
    finish(in_handle, ("w_in",), "scatter_in_wait", sheets)
    conv_parts = lax.dynamic_slice(conv_all, (0, 0, me * CONV_SHARD[1]), (N_DEV,) + CONV_SHARD)
    outs["dn_conv"] = _adam_update(conv_parts, dn_conv, m_dn_conv, v_dn_conv, "adam_dn_conv")
    outs.update(_small_update(sheets, {n: args[n] for n in SMALL}, {n: args["m_" + n] for n in SMALL},
                              {n: args["v_" + n] for n in SMALL}))

    names = ("attn_norm", "w_in", "dn_conv", "dn_a_log", "dn_dt_bias", "dn_out_norm", "swa_q_norm", "swa_k_norm",
             "swa_sinks", "rel_bias", "w_branch_dn", "w_branch_swa", "w_out", "ffn_norm", "w_gate", "w_up", "w_down")
    results = []
    for kind in range(4):
        results += [outs[n][kind].reshape(args[n].shape) for n in names]

    loss = lax.psum(loss_local[0, 0], ("x", "y", "c"))
    return (loss, dx.reshape(x.shape), *results)
```

```python
import math

import numpy as np
import jax
import jax.numpy as jnp
from jax import lax
from jax.experimental import pallas as pl
from jax.experimental.pallas import tpu as pltpu

F32 = jnp.float32
BF16 = jnp.bfloat16
HI = lax.Precision.HIGHEST

D_MODEL = 1024
DN_HEADS = 4
DN_DIM = 128
DN_WIDTH = 512
DN_QKV = 1536
DN_CONV = 4
CHUNK = 64
SWA_HEADS = 8
SWA_KV = 2
SWA_GROUP = 4
SWA_DIM = 64
SWA_WIDTH = 512
SWA_KVW = 128
WINDOW = 128
BLOCK = 128
REL_BUCKETS = 32
REL_MAX_DIST = 128
D_FF = 2816
D_IN = 4872
EPS = 1e-6
N_DEV = 8

ADAM_LR = 0.001
ADAM_B1 = 0.9
ADAM_B2 = 0.999
ADAM_EPS = 1e-08
ADAM_WD = 0.01
ADAM_STEP = 10

P_GATE, P_QKV, P_Z, P_SQ, P_SK, P_SV, P_BA = 0, 2048, 3584, 4096, 4608, 4736, 4864
P_WIDTH = 4992
R_QKV, R_Z, R_B, R_A, R_SQ, R_SK, R_SV, R_GATE = 0, 1536, 2048, 2052, 2056, 2568, 2696, 2824

VMEM_LIMIT = 56 * 1024 * 1024
LANES = 128
MESH_ID = pl.DeviceIdType.MESH


def _params(sem=None):
    return pltpu.CompilerParams(dimension_semantics=sem, vmem_limit_bytes=VMEM_LIMIT)


def _pick(dim, target):
    if dim <= target:
        return dim
    t = target - target % LANES
    while t >= LANES:
        if dim % t == 0:
            return t
        t -= LANES
    return dim


_DIMS = {"nn": (((1,), (0,)), ((), ())), "nt": (((1,), (1,)), ((), ())), "tn": (((0,), (0,)), ((), ()))}


def _tile_product(a_ref, b_ref, mode, b_blocks):
    a = a_ref[...].astype(BF16)
    b = jnp.concatenate([b_ref[d] for d in range(b_ref.shape[0])], axis=1) if b_blocks else b_ref[...]
    return lax.dot_general(a, b.astype(BF16), _DIMS[mode], preferred_element_type=F32)


def _mm(pairs, mode, out_dtype, name, bm, bn, j_outer=False, b_blocks=False, out_blocks=False):
    a0, b0 = pairs[0]
    cb = b0.shape[2] if b_blocks else None
    b_shape = (b0.shape[1], N_DEV * cb) if b_blocks else b0.shape
    if mode == "nn":
        (M, K), (K2, N) = a0.shape, b_shape
    elif mode == "nt":
        (M, K), (N, K2) = a0.shape, b_shape
    else:
        (K, M), (K2, N) = a0.shape, b_shape
    bm, bn = min(bm, M), min(bn, N)
    assert K == K2 and M % bm == 0 and N % bn == 0, (name, a0.shape, b0.shape, bm, bn)
    co = N // N_DEV
    assert not out_blocks or bn % co == 0
    dims = _DIMS[mode]
    n = len(pairs)

    def body(*refs):
        o_ref = refs[2 * n]
        acc = None
        for t in range(n):
            p = _tile_product(refs[2 * t], refs[2 * t + 1], mode, b_blocks)
            acc = p if acc is None else acc + p
        if out_blocks:
            for d in range(bn // co):
                o_ref[d] = acc[:, d * co:(d + 1) * co].astype(out_dtype)
        else:
            o_ref[...] = acc.astype(out_dtype)

    def ij(f):
        return (lambda j, i: f(i, j)) if j_outer else f

    a_spec = pl.BlockSpec((K, bm), ij(lambda i, j: (0, i))) if mode == "tn" else pl.BlockSpec((bm, K), ij(lambda i, j: (i, 0)))
    if b_blocks and mode == "nt":
        b_spec = pl.BlockSpec((N_DEV, bn, cb), ij(lambda i, j: (0, j, 0)))
    elif b_blocks:
        b_spec = pl.BlockSpec((bn // cb, K, cb), ij(lambda i, j: (j, 0, 0)))
    elif mode == "nt":
        b_spec = pl.BlockSpec((bn, K), ij(lambda i, j: (j, 0)))
    else:
        b_spec = pl.BlockSpec((K, bn), ij(lambda i, j: (0, j)))
    if out_blocks:
        out_spec = pl.BlockSpec((bn // co, bm, co), ij(lambda i, j: (j, i, 0)))
        out_shape = jax.ShapeDtypeStruct((N_DEV, M, co), out_dtype)
    else:
        out_spec = pl.BlockSpec((bm, bn), ij(lambda i, j: (i, j)))
        out_shape = jax.ShapeDtypeStruct((M, N), out_dtype)
    grid = (N // bn, M // bm) if j_outer else (M // bm, N // bn)
    return pl.pallas_call(
        body, grid=grid, in_specs=[a_spec, b_spec] * n, out_specs=out_spec, out_shape=out_shape, name=name,
        compiler_params=_params(("parallel", "parallel")),
    )(*[x for pair in pairs for x in pair])


def _mm_fused(pairs, mode, name, bm, bn, epilogue, extras, out_dtypes, j_outer=False, b_blocks=False,
              sum_shape=None, wide_first=None):
    a0, b0 = pairs[0]
    cb = b0.shape[2] if b_blocks else None
    b_shape = (b0.shape[1], N_DEV * cb) if b_blocks else b0.shape
    if mode == "nn":
        (M, K), (K2, N) = a0.shape, b_shape
    else:
        (M, K), (N, K2) = a0.shape, b_shape
    bm, bn = min(bm, M), min(bn, N)
    assert mode in ("nn", "nt") and K == K2 and M % bm == 0 and N % bn == 0, (name, a0.shape, b0.shape)
    dims = _DIMS[mode]
    n, ne, no = len(pairs), len(extras), len(out_dtypes)

    def body(*refs):
        prods = [_tile_product(refs[2 * t], refs[2 * t + 1], mode, b_blocks) for t in range(n)]
        results = epilogue(prods, [r[...] for r in refs[2 * n:2 * n + ne]])
        out_refs = refs[2 * n + ne:]
        for o_ref, val, dt in zip(out_refs, results, out_dtypes):
            o_ref[...] = val.astype(dt)
        if sum_shape is not None:
            s_ref = out_refs[no]

            @pl.when((pl.program_id(0) == 0) & (pl.program_id(1) == 0))
            def _():
                s_ref[...] = jnp.zeros_like(s_ref)

            s_ref[...] += results[no]

    def ij(f):
        return (lambda j, i: f(i, j)) if j_outer else f

    a_spec = pl.BlockSpec((bm, K), ij(lambda i, j: (i, 0)))
    once = dict(pipeline_mode=pl.Buffered(1)) if bn == N else {}
    if b_blocks and mode == "nt":
        b_spec = pl.BlockSpec((N_DEV, bn, cb), ij(lambda i, j: (0, j, 0)), **once)
    elif b_blocks:
        b_spec = pl.BlockSpec((bn // cb, K, cb), ij(lambda i, j: (j, 0, 0)), **once)
    elif mode == "nt":
        b_spec = pl.BlockSpec((bn, K), ij(lambda i, j: (j, 0)), **once)
    else:
        b_spec = pl.BlockSpec((K, bn), ij(lambda i, j: (0, j)), **once)
    e_specs = [pl.BlockSpec((1, bn), ij(lambda i, j: (0, j))) if first is None
               else pl.BlockSpec((bm, bn), ij(lambda i, j, first=first: (i, first + j))) for _, first in extras]
    tile = pl.BlockSpec((bm, bn), ij(lambda i, j: (i, j)))
    out_specs = [tile] * no
    out_shape = [jax.ShapeDtypeStruct((M, N), dt) for dt in out_dtypes]
    if wide_first is not None:
        assert bn == N
        out_specs[0] = pl.BlockSpec((bm, wide_first[1]), ij(lambda i, j: (i, 0)))
        out_shape[0] = jax.ShapeDtypeStruct((M, wide_first[0]), out_dtypes[0])
    if sum_shape is not None:
        assert sum_shape[1] in (1, bn) and (sum_shape[1] == 1 or bn == N)
        out_specs.append(_full(sum_shape))
        out_shape.append(jax.ShapeDtypeStruct(sum_shape, F32))
    grid = (N // bn, M // bm) if j_outer else (M // bm, N // bn)
    sem = ("arbitrary", "arbitrary") if sum_shape is not None else ("parallel", "parallel")
    return pl.pallas_call(
        body, grid=grid, in_specs=[a_spec, b_spec] * n + e_specs, out_specs=out_specs, out_shape=out_shape,
        name=name, compiler_params=_params(sem),
    )(*[x for pair in pairs for x in pair], *[arr for arr, _ in extras])


def _rms(x, gain):
    return x * lax.rsqrt(jnp.mean(x * x, axis=-1, keepdims=True) + EPS) * gain


def _silu(x):
    return x * jax.nn.sigmoid(x)


def _act(g, u):
    return _silu(g) * u


def _merge(g0, g1, a_dn, a_swa):
    return jax.nn.sigmoid(g0) * a_dn + jax.nn.sigmoid(g1) * a_swa


def _dn_post(c, is_v, q_scale):
    a = _silu(c)
    rs = lax.rsqrt(jnp.sum(a * a, axis=-1, keepdims=True) + EPS) * q_scale
    return a * jnp.where(is_v, 1.0, rs)


def _dn_out(o, z, gain):
    return _rms(o, gain) * _silu(z)


def _dot(a, b, dims=_DIMS["nn"], hi=False):
    if a.ndim == 3 or b.ndim == 3:
        batch = a.shape[0] if a.ndim == 3 else b.shape[0]
        a = a if a.ndim == 3 else jnp.broadcast_to(a, (batch,) + a.shape)
        b = b if b.ndim == 3 else jnp.broadcast_to(b, (batch,) + b.shape)
        ((ca,), (cb,)), _ = dims
        dims = (((ca + 1,), (cb + 1,)), ((0,), (0,)))
    if hi:
        return lax.dot_general(a, b, dims, precision=HI, preferred_element_type=F32)
    return lax.dot_general(a.astype(BF16), b.astype(BF16), dims, preferred_element_type=F32)


def _pieces(x):
    hi = x.astype(BF16)
    r1 = x - hi.astype(F32)
    mid = r1.astype(BF16)
    return hi, mid, (r1 - mid.astype(F32)).astype(BF16)


def _sel_left_impl(m, x):
    mb = m.astype(BF16)
    hi, mid, lo = _pieces(x)
    return _dot(mb, hi) + (_dot(mb, mid) + _dot(mb, lo))


@jax.custom_vjp
def _sel_left(m, mt, x):
    return _sel_left_impl(m, x)


_sel_left.defvjp(lambda m, mt, x: (_sel_left_impl(m, x), (m, mt)),
                 lambda res, ct: (jnp.zeros_like(res[0]), jnp.zeros_like(res[1]), _sel_left_impl(res[1], ct)))


def _sel_right_impl(x, s):
    sb = s.astype(BF16)
    hi, mid, lo = _pieces(x)
    return _dot(hi, sb) + (_dot(mid, sb) + _dot(lo, sb))


@jax.custom_vjp
def _sel_right(x, s, st):
    return _sel_right_impl(x, s)


_sel_right.defvjp(lambda x, s, st: (_sel_right_impl(x, s), (s, st)),
                  lambda res, ct: (_sel_right_impl(ct, res[1]), jnp.zeros_like(res[0]), jnp.zeros_like(res[1])))


def _sel_nt_impl(s, x):
    sb = s.astype(BF16)
    hi, mid, lo = _pieces(x)
    return _dot(sb, hi, _DIMS["nt"]) + (_dot(sb, mid, _DIMS["nt"]) + _dot(sb, lo, _DIMS["nt"]))


def _sel_tn_impl(x, s):
    sb = s.astype(BF16)
    hi, mid, lo = _pieces(x)
    return _dot(hi, sb, _DIMS["tn"]) + (_dot(mid, sb, _DIMS["tn"]) + _dot(lo, sb, _DIMS["tn"]))


@jax.custom_vjp
def _sel_nt(s, x):
    return _sel_nt_impl(s, x)


_sel_nt.defvjp(lambda s, x: (_sel_nt_impl(s, x), s),
               lambda s, ct: (jnp.zeros_like(s), _sel_tn_impl(ct, s)))


def _dot3_impl(a, b):
    a_hi, a_lo, _ = _pieces(a)
    b_hi, b_lo, _ = _pieces(b)
    return _dot(a_hi, b_hi) + (_dot(a_hi, b_lo) + _dot(a_lo, b_hi))


@jax.custom_vjp
def _dot3(a, b):
    return _dot3_impl(a, b)


_dot3.defvjp(lambda a, b: (_dot3_impl(a, b), (a, b)),
             lambda res, ct: (_dot(ct, res[1], _DIMS["nt"]), _dot(res[0], ct, _DIMS["tn"])))


def _inv_impl(a, eye, strict):
    t = eye - a
    p = _dot(a, a)
    for level in range(5):
        t = t + _dot(t, p)
        if level < 4:
            p = _dot(p, p)
    t = t + _dot(t, eye - t - _dot3_impl(a, t))
    return jnp.where(strict > 0.5, t, eye)


@jax.custom_vjp
def _inv_given(a, t):
    return t.astype(F32)


_inv_given.defvjp(lambda a, t: (t.astype(F32), t),
                  lambda t, ct: (-_dot(_dot(t, ct, _DIMS["tn"]), t, _DIMS["nt"]), jnp.zeros_like(t)))


@jax.custom_vjp
def _lanes_join(a, b):
    return jnp.concatenate([a, b], axis=-1)


_lanes_join.defvjp(lambda a, b: (jnp.concatenate([a, b], axis=-1), None),
                   lambda _, ct: (ct[..., :ct.shape[-1] // 2], ct[..., ct.shape[-1] // 2:]))


@jax.custom_vjp
def _lanes_halves(y):
    h = y.shape[-1] // 2
    return y[..., :h], y[..., h:]


_lanes_halves.defvjp(lambda y: ((y[..., :y.shape[-1] // 2], y[..., y.shape[-1] // 2:]), None),
                     lambda _, ct: (jnp.concatenate(ct, axis=-1),))

GROUP = 4
GROUP_ROWS = GROUP * CHUNK


def _block_consts(n):
    ii = lax.broadcasted_iota(jnp.int32, (n, n), 0)
    jj = lax.broadcasted_iota(jnp.int32, (n, n), 1)
    shift = CHUNK.bit_length() - 1
    same = jnp.right_shift(ii, shift) == jnp.right_shift(jj, shift)
    return same & (ii >= jj), same & (ii <= jj), same & (ii > jj), same, ii == jj


def _lane0(n):
    s = (lax.broadcasted_iota(jnp.int32, (LANES, n), 0) == 0).astype(F32)
    st = (lax.broadcasted_iota(jnp.int32, (n, LANES), 1) == 0).astype(F32)
    return s, st


def _dn_group(q, k, v, g, beta, t_saved=None):
    n = GROUP_ROWS
    low_b, upp_b, strict_b, _, eye_b = _block_consts(n)
    low, upp, eye = low_b.astype(F32), upp_b.astype(F32), eye_b.astype(F32)
    s, st = _lane0(n)
    gc = _sel_left(low, upp, g)
    per_chunk = (g.shape[0], GROUP, CHUNK, LANES)
    gl = jnp.broadcast_to(jnp.sum(g.reshape(per_chunk), axis=2, keepdims=True), per_chunk).reshape(g.shape)
    col = _sel_right(gc, s, st)
    row = _sel_nt(st, gc)
    decay = jnp.exp(jnp.where(low_b, col - row, -jnp.inf))
    kb = k * beta
    vb = v * beta
    a = jnp.where(strict_b, _dot(kb, k, _DIMS["nt"]) * decay, 0.0)
    t = _inv_impl(a, eye, strict_b.astype(F32)) if t_saved is None else _inv_given(a, t_saved)
    u, w = _lanes_halves(_dot3(t, _lanes_join(vb, kb * jnp.exp(gc))))
    return u, w, q * jnp.exp(gc), k * jnp.exp(gl - gc), t


def _dn_chunk(q, k, g):
    ii = lax.broadcasted_iota(jnp.int32, (CHUNK, CHUNK), 0)
    jj = lax.broadcasted_iota(jnp.int32, (CHUNK, CHUNK), 1)
    low = (ii >= jj).astype(F32)
    upp = (ii <= jj).astype(F32)
    s, st = _lane0(CHUNK)
    gc = _sel_left(low, upp, g)
    col = _sel_right(gc, s, st)
    row = _sel_nt(st, gc)
    decay = jnp.exp(jnp.where(ii >= jj, col - row, -jnp.inf))
    qk = _dot(q, k, _DIMS["nt"]) * decay
    return qk, jnp.exp(jnp.sum(g, axis=-2, keepdims=True))


def _dn_step(s, u, w, qe, kd, qk, egl):
    v_new = u - _dot(w, s)
    o = _dot(qe, s) + _dot(qk, v_new)
    s_new = s * egl + _dot(kd, v_new, _DIMS["tn"])
    return s_new, o


def _swa_block(q, kband, vband, qg, kg, sinks, bias, mask):
    kn = _rms(kband, kg)
    qn = _rms(q, qg)
    logits = _dot(qn, kn, _DIMS["nt"]) * (SWA_DIM ** -0.5)
    logits = jnp.where(mask, logits + bias, -jnp.inf)
    m = jnp.maximum(jnp.max(logits, axis=-1, keepdims=True), sinks)
    p = jnp.exp(logits - m)
    denom = jnp.sum(p, axis=-1, keepdims=True) + jnp.exp(sinks - m)
    return _dot(p / denom, vband)


def _adamw(w, g, m, v):
    m = ADAM_B1 * m + (1.0 - ADAM_B1) * g
    v = ADAM_B2 * v + (1.0 - ADAM_B2) * jnp.square(g)
    m_hat = m / (1.0 - ADAM_B1 ** ADAM_STEP)
    v_hat = v / (1.0 - ADAM_B2 ** ADAM_STEP)
    delta = -ADAM_LR * (m_hat / (jnp.sqrt(v_hat) + ADAM_EPS) + ADAM_WD * w)
    return delta, m, v


def _row(tm, c, cb=0):
    return pl.BlockSpec((tm, c), lambda i, cb=cb: (i, cb))


def _full(shape):
    nd = len(shape)
    return pl.BlockSpec(shape, lambda *_, nd=nd: (0,) * nd)


def _norm_fwd(x, gain, name, tm=512):
    S = x.shape[0]

    def body(x_ref, g_ref, h_ref):
        h_ref[...] = _rms(x_ref[...], g_ref[...]).astype(BF16)

    return pl.pallas_call(
        body, grid=(S // tm,), in_specs=[_row(tm, D_MODEL), _full((1, D_MODEL))],
        out_specs=_row(tm, D_MODEL), out_shape=jax.ShapeDtypeStruct((S, D_MODEL), BF16),
        name=name, compiler_params=_params(("parallel",)))(x, gain)


def _shift_down(x, s):
    row = lax.broadcasted_iota(jnp.int32, x.shape, 0)
    return jnp.where(row >= s, pltpu.roll(x, s, axis=0), 0.0)


def _shift_up(x, s):
    n = x.shape[0]
    row = lax.broadcasted_iota(jnp.int32, x.shape, 0)
    return jnp.where(row < n - s, pltpu.roll(x, n - s, axis=0), 0.0)


def _conv(x, w):
    out = w[DN_CONV - 1:DN_CONV] * x
    for s in range(1, DN_CONV):
        out = out + w[DN_CONV - 1 - s:DN_CONV - s] * _shift_down(x, s)
    return out


def _dn_conv_fwd(proj, conv_w):
    S = proj.shape[0]
    nb = DN_QKV // LANES

    def body(x_ref, w_ref, o_ref):
        j = pl.program_id(0)
        q_scale = jnp.where(j < DN_HEADS, DN_DIM ** -0.5, 1.0).astype(F32)
        o_ref[...] = _dn_post(_conv(x_ref[...], w_ref[...]), j >= 2 * DN_HEADS, q_scale)

    return pl.pallas_call(
        body, grid=(nb,),
        in_specs=[pl.BlockSpec((S, LANES), lambda j: (0, P_QKV // LANES + j)),
                  pl.BlockSpec((DN_CONV, LANES), lambda j: (0, j))],
        out_specs=pl.BlockSpec((S, LANES), lambda j: (0, j)),
        out_shape=jax.ShapeDtypeStruct((S, DN_QKV), F32), name="dn_conv_fwd",
        compiler_params=_params(("parallel",)))(proj, conv_w)


def _dn_conv_bwd(proj, conv_w, dqkvn, dproj):
    S = proj.shape[0]
    nb = DN_QKV // LANES

    def body(x_ref, w_ref, d_ref, _, dx_ref, dw_ref):
        j = pl.program_id(0)
        q_scale = jnp.where(j < DN_HEADS, DN_DIM ** -0.5, 1.0).astype(F32)
        x = x_ref[...]
        w = w_ref[...]
        _, vjp = jax.vjp(lambda c: _dn_post(c, j >= 2 * DN_HEADS, q_scale), _conv(x, w))
        (dc,) = vjp(d_ref[0])
        dx = w[DN_CONV - 1:DN_CONV] * dc
        dw_ref[DN_CONV - 1:DN_CONV, :] = jnp.sum(dc * x, axis=0, keepdims=True)
        for s in range(1, DN_CONV):
            dx = dx + w[DN_CONV - 1 - s:DN_CONV - s] * _shift_up(dc, s)
            dw_ref[DN_CONV - 1 - s:DN_CONV - s, :] = jnp.sum(dc * _shift_down(x, s), axis=0, keepdims=True)
        dx_ref[...] = dx.astype(BF16)

    return pl.pallas_call(
        body, grid=(nb,),
        in_specs=[pl.BlockSpec((S, LANES), lambda j: (0, P_QKV // LANES + j)),
                  pl.BlockSpec((DN_CONV, LANES), lambda j: (0, j)),
                  pl.BlockSpec((1, S, LANES), lambda j: (lax.div(j, DN_HEADS), 0, lax.rem(j, DN_HEADS))),
                  pl.BlockSpec(memory_space=pl.ANY)],
        out_specs=[pl.BlockSpec((S, LANES), lambda j: (0, P_QKV // LANES + j)),
                   pl.BlockSpec((DN_CONV, LANES), lambda j: (0, j))],
        out_shape=[jax.ShapeDtypeStruct(dproj.shape, dproj.dtype), jax.ShapeDtypeStruct((DN_CONV, DN_QKV), F32)],
        input_output_aliases={3: 0},
        name="dn_conv_bwd", compiler_params=_params(("parallel",)))(proj, conv_w, dqkvn, dproj)


def _expanders():
    eb = np.zeros((LANES, DN_WIDTH), np.float32)
    ea = np.zeros((LANES, DN_WIDTH), np.float32)
    for h in range(DN_HEADS):
        eb[h, h * DN_DIM:(h + 1) * DN_DIM] = 1.0
        ea[DN_HEADS + h, h * DN_DIM:(h + 1) * DN_DIM] = 1.0
    return jnp.asarray(eb), jnp.asarray(ea), jnp.asarray(eb.T), jnp.asarray(ea.T)


def _dn_gate_args(a_log, dt_bias):
    alog = jnp.repeat(a_log.reshape(1, DN_HEADS), DN_DIM, axis=1)
    dtb = jnp.repeat(dt_bias.reshape(1, DN_HEADS), DN_DIM, axis=1)
    return _expanders() + (alog, dtb)


def _dn_gate_specs(tm):
    return [_row(tm, LANES, P_BA // LANES), _full((LANES, DN_WIDTH)), _full((LANES, DN_WIDTH)),
            _full((DN_WIDTH, LANES)), _full((DN_WIDTH, LANES)), _full((1, DN_WIDTH)), _full((1, DN_WIDTH))]


def _dn_gate_fn(ba, eb, ea, ebt, eat, alog, dtb):
    beta = jax.nn.sigmoid(_sel_right(ba, eb, ebt))
    g = -jnp.exp(alog) * jax.nn.softplus(_sel_right(ba, ea, eat) + dtb)
    return beta, g


def _dn_gate_fwd(proj, a_log, dt_bias, tm=512):
    S = proj.shape[0]
    args = _dn_gate_args(a_log, dt_bias)

    def body(ba_ref, eb_ref, ea_ref, ebt_ref, eat_ref, al_ref, dt_ref, beta_ref, g_ref):
        beta, g = _dn_gate_fn(ba_ref[...], eb_ref[...], ea_ref[...], ebt_ref[...], eat_ref[...], al_ref[...],
                              dt_ref[...])
        beta_ref[...] = beta
        g_ref[...] = g

    return pl.pallas_call(
        body, grid=(S // tm,), in_specs=_dn_gate_specs(tm), out_specs=[_row(tm, DN_WIDTH), _row(tm, DN_WIDTH)],
        out_shape=[jax.ShapeDtypeStruct((S, DN_WIDTH), F32), jax.ShapeDtypeStruct((S, DN_WIDTH), F32)],
        name="dn_gate_fwd", compiler_params=_params(("parallel",)))(proj, *args)


def _dn_gate_bwd(proj, a_log, dt_bias, dbeta, dg, dproj, tm=512):
    S = proj.shape[0]
    args = _dn_gate_args(a_log, dt_bias)

    def body(ba_ref, eb_ref, ea_ref, ebt_ref, eat_ref, al_ref, dt_ref, dbeta_ref, dg_ref, _, dba_ref, dal_ref,
             ddt_ref):
        eb, ea, ebt, eat = eb_ref[...], ea_ref[...], ebt_ref[...], eat_ref[...]
        _, vjp = jax.vjp(lambda ba, al, dt: _dn_gate_fn(ba, eb, ea, ebt, eat, al, dt), ba_ref[...], al_ref[...],
                         dt_ref[...])
        dba, dal, ddt = vjp((dbeta_ref[...], dg_ref[...]))
        dba_ref[...] = dba.astype(BF16)

        @pl.when(pl.program_id(0) == 0)
        def _():
            dal_ref[...] = jnp.zeros_like(dal_ref)
            ddt_ref[...] = jnp.zeros_like(ddt_ref)

        dal_ref[...] += dal
        ddt_ref[...] += ddt

    return pl.pallas_call(
        body, grid=(S // tm,),
        in_specs=_dn_gate_specs(tm) + [_row(tm, DN_WIDTH), _row(tm, DN_WIDTH), pl.BlockSpec(memory_space=pl.ANY)],
        out_specs=[_row(tm, LANES, P_BA // LANES), _full((1, DN_WIDTH)), _full((1, DN_WIDTH))],
        out_shape=[jax.ShapeDtypeStruct(dproj.shape, dproj.dtype), jax.ShapeDtypeStruct((1, DN_WIDTH), F32),
                   jax.ShapeDtypeStruct((1, DN_WIDTH), F32)],
        input_output_aliases={len(args) + 3: 0},
        name="dn_gate_bwd", compiler_params=_params(("arbitrary",)))(proj, *args, dbeta, dg, dproj)


PREP_GROUPS = 4
PREP_CHUNKS = GROUP * PREP_GROUPS


def _dn_prep_specs():
    rows = PREP_CHUNKS * CHUNK
    q = pl.BlockSpec((rows, LANES), lambda h, c: (c, h))
    k = pl.BlockSpec((rows, LANES), lambda h, c: (c, DN_HEADS + h))
    v = pl.BlockSpec((rows, LANES), lambda h, c: (c, 2 * DN_HEADS + h))
    qk = pl.BlockSpec((1, rows, CHUNK), lambda h, c: (h, c, 0))
    egl = pl.BlockSpec((1, PREP_CHUNKS, 1, LANES), lambda h, c: (h, c, 0, 0))
    return q, k, v, qk, egl


def _dn_prep_fwd(qkvn, g, beta):
    S = qkvn.shape[0]
    nc = S // CHUNK
    q, k, v, qks, egl = _dn_prep_specs()

    def body(q_ref, k_ref, v_ref, g_ref, b_ref, u_ref, w_ref, qe_ref, kd_ref, qk_ref, egl_ref, t_ref):
        rows = PREP_CHUNKS * CHUNK
        grp = (PREP_GROUPS, GROUP_ROWS, LANES)
        chk = (PREP_CHUNKS, CHUNK, LANES)
        q, k, g = q_ref[...], k_ref[...], g_ref[...]
        u, w, qe, kd, t = _dn_group(q.reshape(grp), k.reshape(grp), v_ref[...].reshape(grp), g.reshape(grp),
                                    b_ref[...].reshape(grp))
        u_ref[...] = u.reshape(rows, LANES)
        w_ref[...] = w.reshape(rows, LANES)
        qe_ref[...] = qe.reshape(rows, LANES)
        kd_ref[...] = kd.reshape(rows, LANES)
        t_ref[0] = t.reshape(rows, GROUP_ROWS).astype(BF16)
        qk, e = _dn_chunk(q.reshape(chk), k.reshape(chk), g.reshape(chk))
        qk_ref[0] = qk.reshape(rows, CHUNK)
        egl_ref[0] = e

    wide = jax.ShapeDtypeStruct((S, DN_WIDTH), F32)
    return pl.pallas_call(
        body, grid=(DN_HEADS, nc // PREP_CHUNKS), in_specs=[q, k, v, q, q],
        out_specs=[q, q, q, q, qks, egl, _dn_tinv_spec()],
        out_shape=[wide, wide, wide, wide, jax.ShapeDtypeStruct((DN_HEADS, S, CHUNK), F32),
                   jax.ShapeDtypeStruct((DN_HEADS, nc, 1, LANES), F32),
                   jax.ShapeDtypeStruct((DN_HEADS, S, GROUP_ROWS), BF16)],
        name="dn_prep_fwd", compiler_params=_params(("parallel", "parallel")))(qkvn, qkvn, qkvn, g, beta)


def _dn_tinv_spec():
    return pl.BlockSpec((1, PREP_CHUNKS * CHUNK, GROUP_ROWS), lambda h, c: (h, c, 0))


def _dn_prep_bwd(qkvn, g, beta, tinv, du, dw, dqe, dkd, dqk, degl):
    S = qkvn.shape[0]
    nc = S // CHUNK
    q, k, v, qks, egl = _dn_prep_specs()

    def body(q_ref, k_ref, v_ref, g_ref, b_ref, t_ref, du_ref, dw_ref, dqe_ref, dkd_ref, dqk_ref, degl_ref,
             dqkv_ref, dg_ref, db_ref):
        rows = PREP_CHUNKS * CHUNK
        grp = (PREP_GROUPS, GROUP_ROWS, LANES)
        chk = (PREP_CHUNKS, CHUNK, LANES)
        q, k, g = q_ref[...], k_ref[...], g_ref[...]
        t_saved = t_ref[0].reshape(PREP_GROUPS, GROUP_ROWS, GROUP_ROWS)
        _, vjp = jax.vjp(lambda *x: _dn_group(*x, t_saved=t_saved)[:4], q.reshape(grp), k.reshape(grp),
                         v_ref[...].reshape(grp), g.reshape(grp), b_ref[...].reshape(grp))
        dq, dk, dv, dg, db = vjp((du_ref[...].reshape(grp), dw_ref[...].reshape(grp), dqe_ref[...].reshape(grp),
                                  dkd_ref[...].reshape(grp)))
        _, vjp = jax.vjp(_dn_chunk, q.reshape(chk), k.reshape(chk), g.reshape(chk))
        dq2, dk2, dg2 = vjp((dqk_ref[0].reshape(PREP_CHUNKS, CHUNK, CHUNK), degl_ref[0]))
        dqkv_ref[0] = dq.reshape(rows, LANES) + dq2.reshape(rows, LANES)
        dqkv_ref[1] = dk.reshape(rows, LANES) + dk2.reshape(rows, LANES)
        dqkv_ref[2] = dv.reshape(rows, LANES)
        dg_ref[...] = dg.reshape(rows, LANES) + dg2.reshape(rows, LANES)
        db_ref[...] = db.reshape(rows, LANES)

    wide = jax.ShapeDtypeStruct((S, DN_WIDTH), F32)
    rows = PREP_CHUNKS * CHUNK
    return pl.pallas_call(
        body, grid=(DN_HEADS, nc // PREP_CHUNKS), in_specs=[q, k, v, q, q, _dn_tinv_spec(), q, q, q, q, qks, egl],
        out_specs=[pl.BlockSpec((3, rows, LANES), lambda h, c: (0, c, h)), q, q],
        out_shape=[jax.ShapeDtypeStruct((3, S, DN_WIDTH), F32), wide, wide],
        name="dn_prep_bwd", compiler_params=_params(("parallel", "parallel")),
    )(qkvn, qkvn, qkvn, g, beta, tinv, du, dw, dqe, dkd, dqk, degl)


SCAN_CHUNKS = 8


def _dn_scan_specs(nc, reverse):
    nb = nc // SCAN_CHUNKS

    def cidx(c):
        return nb - 1 - c if reverse else c

    hc = pl.BlockSpec((SCAN_CHUNKS * CHUNK, DN_WIDTH), lambda c: (cidx(c), 0))
    qk = pl.BlockSpec((DN_HEADS, SCAN_CHUNKS * CHUNK, CHUNK), lambda c: (0, cidx(c), 0))
    egl = pl.BlockSpec((DN_HEADS, SCAN_CHUNKS, 1, LANES), lambda c: (0, cidx(c), 0, 0))
    st = pl.BlockSpec((DN_HEADS, SCAN_CHUNKS, DN_DIM, DN_DIM), lambda c: (0, cidx(c), 0, 0))
    return hc, qk, egl, st


def _heads(ref, i):
    return jnp.stack([ref[pl.ds(i * CHUNK, CHUNK), pl.ds(h * DN_DIM, DN_DIM)] for h in range(DN_HEADS)])


def _dn_scan_fwd(u, w, qe, kd, qk, egl):
    S = u.shape[0]
    nc = S // CHUNK
    hc, qks, egls, st = _dn_scan_specs(nc, False)

    def body(u_ref, w_ref, qe_ref, kd_ref, qk_ref, egl_ref, o_ref, st_ref, s_scr):
        @pl.when(pl.program_id(0) == 0)
        def _():
            s_scr[...] = jnp.zeros_like(s_scr)

        s = s_scr[...]
        for i in range(SCAN_CHUNKS):
            rows = pl.ds(i * CHUNK, CHUNK)
            st_ref[:, i] = s
            s, o = _dn_step(s, _heads(u_ref, i), _heads(w_ref, i), _heads(qe_ref, i), _heads(kd_ref, i),
                            qk_ref[:, rows, :], egl_ref[:, i])
            for h in range(DN_HEADS):
                o_ref[rows, pl.ds(h * DN_DIM, DN_DIM)] = o[h]
        s_scr[...] = s

    return pl.pallas_call(
        body, grid=(nc // SCAN_CHUNKS,), in_specs=[hc, hc, hc, hc, qks, egls], out_specs=[hc, st],
        out_shape=[jax.ShapeDtypeStruct((S, DN_WIDTH), F32), jax.ShapeDtypeStruct((DN_HEADS, nc, DN_DIM, DN_DIM), F32)],
        scratch_shapes=[pltpu.VMEM((DN_HEADS, DN_DIM, DN_DIM), F32)], name="dn_scan_fwd",
        compiler_params=_params(("arbitrary",)))(u, w, qe, kd, qk, egl)


def _dn_scan_bwd(u, w, qe, kd, qk, egl, states, do):
    S = u.shape[0]
    nc = S // CHUNK
    hc, qks, egls, st = _dn_scan_specs(nc, True)

    def body(u_ref, w_ref, qe_ref, kd_ref, qk_ref, egl_ref, st_ref, do_ref,
             du_ref, dw_ref, dqe_ref, dkd_ref, dqk_ref, degl_ref, ds_scr):
        @pl.when(pl.program_id(0) == 0)
        def _():
            ds_scr[...] = jnp.zeros_like(ds_scr)

        ds = ds_scr[...]
        for i in reversed(range(SCAN_CHUNKS)):
            rows = pl.ds(i * CHUNK, CHUNK)
            _, vjp = jax.vjp(_dn_step, st_ref[:, i], _heads(u_ref, i), _heads(w_ref, i), _heads(qe_ref, i),
                             _heads(kd_ref, i), qk_ref[:, rows, :], egl_ref[:, i])
            ds, du, dw, dqe, dkd, dqk, degl = vjp((ds, _heads(do_ref, i)))
            dqk_ref[:, rows, :] = dqk
            degl_ref[:, i] = degl
            for h in range(DN_HEADS):
                cols = pl.ds(h * DN_DIM, DN_DIM)
                du_ref[rows, cols] = du[h]
                dw_ref[rows, cols] = dw[h]
                dqe_ref[rows, cols] = dqe[h]
                dkd_ref[rows, cols] = dkd[h]
        ds_scr[...] = ds

    wide = jax.ShapeDtypeStruct((S, DN_WIDTH), F32)
    return pl.pallas_call(
        body, grid=(nc // SCAN_CHUNKS,), in_specs=[hc, hc, hc, hc, qks, egls, st, hc],
        out_specs=[hc, hc, hc, hc, qks, egls],
        out_shape=[wide, wide, wide, wide, jax.ShapeDtypeStruct((DN_HEADS, S, CHUNK), F32),
                   jax.ShapeDtypeStruct((DN_HEADS, nc, 1, LANES), F32)],
        scratch_shapes=[pltpu.VMEM((DN_HEADS, DN_DIM, DN_DIM), F32)], name="dn_scan_bwd",
        compiler_params=_params(("arbitrary",)))(u, w, qe, kd, qk, egl, states, do)


def _dn_out_fwd(o, proj, gain, tm=1024):
    S = o.shape[0]

    def body(o_ref, z_ref, g_ref, y_ref):
        y_ref[...] = _dn_out(o_ref[...], z_ref[...], g_ref[...]).astype(BF16)

    hs = pl.BlockSpec((tm, LANES), lambda i, h: (i, h))
    zs = pl.BlockSpec((tm, LANES), lambda i, h: (i, P_Z // LANES + h))
    return pl.pallas_call(
        body, grid=(S // tm, DN_HEADS), in_specs=[hs, zs, _full((1, DN_DIM))], out_specs=hs,
        out_shape=jax.ShapeDtypeStruct((S, DN_WIDTH), BF16), name="dn_out_fwd",
        compiler_params=_params(("parallel", "parallel")))(o, proj, gain)


_ANY = pl.BlockSpec(memory_space=pl.ANY)


def _dn_out_bwd(o, proj, gain, dy, dproj, tm=1024):
    S = o.shape[0]

    def body(o_ref, z_ref, g_ref, dy_ref, _, do_ref, dz_ref, dg_ref):
        _, vjp = jax.vjp(_dn_out, o_ref[...], z_ref[...], g_ref[...])
        do, dz, dg = vjp(dy_ref[...])
        do_ref[...] = do
        dz_ref[...] = dz.astype(BF16)

        @pl.when((pl.program_id(0) == 0) & (pl.program_id(1) == 0))
        def _():
            dg_ref[...] = jnp.zeros_like(dg_ref)

        dg_ref[...] += dg

    hs = pl.BlockSpec((tm, LANES), lambda i, h: (i, h))
    zs = pl.BlockSpec((tm, LANES), lambda i, h: (i, P_Z // LANES + h))
    return pl.pallas_call(
        body, grid=(S // tm, DN_HEADS), in_specs=[hs, zs, _full((1, DN_DIM)), hs, _ANY],
        out_specs=[hs, zs, _full((1, DN_DIM))],
        out_shape=[jax.ShapeDtypeStruct((S, DN_WIDTH), F32), jax.ShapeDtypeStruct(dproj.shape, dproj.dtype),
                   jax.ShapeDtypeStruct((1, DN_DIM), F32)],
        input_output_aliases={4: 1},
        name="dn_out_bwd", compiler_params=_params(("arbitrary", "arbitrary")))(o, proj, gain, dy, dproj)


def _rel_buckets():
    qi = np.arange(BLOCK)[:, None]
    kj = np.arange(2 * BLOCK)[None, :]
    n = np.maximum(BLOCK + qi - kj, 0)
    max_exact = REL_BUCKETS // 2
    nf = np.maximum(n, 1).astype(np.float32)
    large = max_exact + (np.log(nf / np.float32(max_exact)) / np.float32(math.log(REL_MAX_DIST / max_exact))
                         * np.float32(REL_BUCKETS - max_exact)).astype(np.int32)
    large = np.minimum(large, REL_BUCKETS - 1)
    return np.where(n < max_exact, n, large).astype(np.int32)


def _bias_fwd(rel_bias):
    buckets = jnp.asarray(_rel_buckets())

    def body(rb_ref, bk_ref, o_ref):
        bk = bk_ref[...]
        for h in range(SWA_HEADS):
            acc = jnp.zeros((BLOCK, 2 * BLOCK), F32)
            for b in range(REL_BUCKETS):
                acc = jnp.where(bk == b, rb_ref[b, h], acc)
            o_ref[h] = acc

    return pl.pallas_call(
        body, in_specs=[pl.BlockSpec(memory_space=pltpu.SMEM), pl.BlockSpec(memory_space=pltpu.VMEM)],
        out_specs=pl.BlockSpec(memory_space=pltpu.VMEM),
        out_shape=jax.ShapeDtypeStruct((SWA_HEADS, BLOCK, 2 * BLOCK), F32), name="swa_bias_fwd",
        compiler_params=_params())(rel_bias, buckets)


def _bias_bwd(dbias):
    buckets = jnp.asarray(_rel_buckets())

    def body(d_ref, bk_ref, o_ref):
        bk = bk_ref[...]
        lane = lax.broadcasted_iota(jnp.int32, (1, LANES), 1)
        for h in range(SWA_HEADS):
            d = d_ref[h]
            row = jnp.zeros((1, LANES), F32)
            for b in range(REL_BUCKETS):
                part = jnp.sum(jnp.where(bk == b, d, 0.0), axis=1, keepdims=True)
                row = jnp.where(lane == b, jnp.sum(part, axis=0, keepdims=True), row)
            o_ref[h:h + 1, :] = row

    return pl.pallas_call(
        body, in_specs=[pl.BlockSpec(memory_space=pltpu.VMEM), pl.BlockSpec(memory_space=pltpu.VMEM)],
        out_specs=pl.BlockSpec(memory_space=pltpu.VMEM),
        out_shape=jax.ShapeDtypeStruct((SWA_HEADS, LANES), F32), name="swa_bias_bwd",
        compiler_params=_params())(dbias, buckets)


def _swa_mask(n):
    qi = lax.broadcasted_iota(jnp.int32, (BLOCK, 2 * BLOCK), 0)
    kj = lax.broadcasted_iota(jnp.int32, (BLOCK, 2 * BLOCK), 1)
    dist = BLOCK + qi - kj
    return (dist >= 0) & (dist < WINDOW) & ((n > 0) | (kj >= BLOCK))


def _swa_in_specs():
    q = pl.BlockSpec((BLOCK, SWA_WIDTH), lambda n: (n, P_SQ // SWA_WIDTH))
    kc = pl.BlockSpec((BLOCK, SWA_KVW), lambda n: (n, P_SK // SWA_KVW))
    kp = pl.BlockSpec((BLOCK, SWA_KVW), lambda n: (jnp.maximum(n - 1, 0), P_SK // SWA_KVW))
    vc = pl.BlockSpec((BLOCK, SWA_KVW), lambda n: (n, P_SV // SWA_KVW))
    vp = pl.BlockSpec((BLOCK, SWA_KVW), lambda n: (jnp.maximum(n - 1, 0), P_SV // SWA_KVW))
    small = [_full((1, SWA_DIM)), _full((1, SWA_DIM)), _full((1, SWA_HEADS)),
             _full((SWA_HEADS, BLOCK, 2 * BLOCK))]
    return [q, kp, kc, vp, vc] + small


def _swa_load(q_ref, kp_ref, kc_ref, vp_ref, vc_ref, s_ref):
    q = jnp.stack([q_ref[:, pl.ds(h * SWA_DIM, SWA_DIM)] for h in range(SWA_HEADS)])
    kbands, vbands = [], []
    for kv in range(SWA_KV):
        cols = pl.ds(kv * SWA_DIM, SWA_DIM)
        kbands += [jnp.concatenate([kp_ref[:, cols], kc_ref[:, cols]], axis=0)] * SWA_GROUP
        vbands += [jnp.concatenate([vp_ref[:, cols], vc_ref[:, cols]], axis=0)] * SWA_GROUP
    sinks = jnp.stack([s_ref[:, pl.ds(h, 1)] for h in range(SWA_HEADS)])
    return q, jnp.stack(kbands), jnp.stack(vbands), sinks


def _swa_fwd(proj, q_gain, k_gain, sinks, bias):
    S = proj.shape[0]

    def body(q_ref, kp_ref, kc_ref, vp_ref, vc_ref, qg_ref, kg_ref, s_ref, bias_ref, y_ref):
        mask = _swa_mask(pl.program_id(0))
        q, kband, vband, sk = _swa_load(q_ref, kp_ref, kc_ref, vp_ref, vc_ref, s_ref)
        out = _swa_block(q, kband, vband, qg_ref[...], kg_ref[...], sk, bias_ref[...], mask)
        for h in range(SWA_HEADS):
            y_ref[:, pl.ds(h * SWA_DIM, SWA_DIM)] = out[h].astype(BF16)

    return pl.pallas_call(
        body, grid=(S // BLOCK,), in_specs=_swa_in_specs(),
        out_specs=pl.BlockSpec((BLOCK, SWA_WIDTH), lambda n: (n, 0)),
        out_shape=jax.ShapeDtypeStruct((S, SWA_WIDTH), BF16), name="swa_fwd",
        compiler_params=_params(("parallel",)))(proj, proj, proj, proj, proj, q_gain, k_gain, sinks, bias)


def _swa_bwd(proj, q_gain, k_gain, sinks, bias, dy, dproj):
    S = proj.shape[0]

    def body(q_ref, kp_ref, kc_ref, vp_ref, vc_ref, qg_ref, kg_ref, s_ref, bias_ref, dy_ref, _,
             dq_ref, dk_ref, dv_ref, dqg_ref, dkg_ref, ds_ref, dbias_ref):
        n = pl.program_id(0)
        mask = _swa_mask(n)

        @pl.when(n == 0)
        def _():
            for r in (dk_ref, dv_ref, dqg_ref, dkg_ref, ds_ref, dbias_ref):
                r[...] = jnp.zeros_like(r)

        cur = pl.ds(pl.multiple_of(n * BLOCK, BLOCK), BLOCK)
        prev = pl.ds(pl.multiple_of(jnp.maximum(n - 1, 0) * BLOCK, BLOCK), BLOCK)
        q, kband, vband, sk = _swa_load(q_ref, kp_ref, kc_ref, vp_ref, vc_ref, s_ref)
        _, vjp = jax.vjp(lambda q, kb, vb, qg, kg, sk, bs: _swa_block(q, kb, vb, qg, kg, sk, bs, mask),
                         q, kband, vband, qg_ref[...], kg_ref[...], sk, bias_ref[...])
        dy = jnp.stack([dy_ref[:, pl.ds(h * SWA_DIM, SWA_DIM)] for h in range(SWA_HEADS)])
        dq, dkb, dvb, dqg, dkg, dsk, dbs = vjp(dy)
        for h in range(SWA_HEADS):
            dq_ref[:, pl.ds(h * SWA_DIM, SWA_DIM)] = dq[h].astype(BF16)
            ds_ref[:, pl.ds(h, 1)] += dsk[h]
        dbias_ref[...] += dbs
        dqg_ref[...] += dqg
        dkg_ref[...] += dkg
        for kv in range(SWA_KV):
            cols = pl.ds(kv * SWA_DIM, SWA_DIM)
            group = range(kv * SWA_GROUP, (kv + 1) * SWA_GROUP)
            dk_kv = sum(dkb[h] for h in group)
            dv_kv = sum(dvb[h] for h in group)
            dk_ref[cur, cols] += dk_kv[BLOCK:]
            dv_ref[cur, cols] += dv_kv[BLOCK:]

            @pl.when(n > 0)
            def _(cols=cols, dk_kv=dk_kv, dv_kv=dv_kv):
                dk_ref[prev, cols] += dk_kv[:BLOCK]
                dv_ref[prev, cols] += dv_kv[:BLOCK]

    return pl.pallas_call(
        body, grid=(S // BLOCK,),
        in_specs=_swa_in_specs() + [pl.BlockSpec((BLOCK, SWA_WIDTH), lambda n: (n, 0)),
                                    pl.BlockSpec(memory_space=pl.ANY)],
        out_specs=[pl.BlockSpec((BLOCK, SWA_WIDTH), lambda n: (n, P_SQ // SWA_WIDTH)), _full((S, SWA_KVW)),
                   _full((S, SWA_KVW)), _full((1, SWA_DIM)), _full((1, SWA_DIM)), _full((1, SWA_HEADS)),
                   _full((SWA_HEADS, BLOCK, 2 * BLOCK))],
        out_shape=[jax.ShapeDtypeStruct(dproj.shape, dproj.dtype), jax.ShapeDtypeStruct((S, SWA_KVW), F32),
                   jax.ShapeDtypeStruct((S, SWA_KVW), F32), jax.ShapeDtypeStruct((1, SWA_DIM), F32),
                   jax.ShapeDtypeStruct((1, SWA_DIM), F32), jax.ShapeDtypeStruct((1, SWA_HEADS), F32),
                   jax.ShapeDtypeStruct((SWA_HEADS, BLOCK, 2 * BLOCK), F32)],
        input_output_aliases={10: 0},
        name="swa_bwd", compiler_params=_params(("arbitrary",)),
    )(proj, proj, proj, proj, proj, q_gain, k_gain, sinks, bias, dy, dproj)


def _kv_into(dproj, dk, dv, tm=512):
    S = dk.shape[0]

    def body(dk_ref, dv_ref, _, o_ref):
        o_ref[:, :SWA_KVW] = dk_ref[...].astype(BF16)
        o_ref[:, SWA_KVW:] = dv_ref[...].astype(BF16)

    return pl.pallas_call(
        body, grid=(S // tm,), in_specs=[_row(tm, SWA_KVW), _row(tm, SWA_KVW), pl.BlockSpec(memory_space=pl.ANY)],
        out_specs=_row(tm, 2 * SWA_KVW, P_SK // (2 * SWA_KVW)),
        out_shape=jax.ShapeDtypeStruct(dproj.shape, dproj.dtype), input_output_aliases={2: 0},
        name="swa_kv_into", compiler_params=_params(("parallel",)))(dk, dv, dproj)


def _position():
    return lax.axis_index("x"), lax.axis_index("y"), lax.axis_index("c")


def _all_gather(shards, name="all_gather_weights"):
    na = len(shards)

    def body(*refs):
        x_refs, out_refs = refs[:na], refs[na:2 * na]
        send_sems, recv_sems, local_sems = refs[2 * na:]
        x, y, c = _position()
        me, sibling = (x, y, c), (x, y, 1 - c)
        chips = [(1 - x, y), (x, 1 - y), (1 - x, 1 - y)]

        def copy(a, k, block, to, own=False):
            px, py, pc = block
            slot = out_refs[a].at[4 * px + 2 * py + pc]
            return pltpu.make_async_remote_copy(
                src_ref=x_refs[a] if own else slot, dst_ref=slot, send_sem=send_sems.at[7 * a + k],
                recv_sem=recv_sems.at[7 * a + k], device_id=to, device_id_type=MESH_ID)

        mine = [pltpu.make_async_copy(x_refs[a], out_refs[a].at[4 * x + 2 * y + c], local_sems.at[a])
                for a in range(na)]
        for cp in mine:
            cp.start()
        first = []
        for a in range(na):
            first.append(copy(a, 0, me, sibling, own=True))
            first += [copy(a, 1 + j, me, (*chip, c), own=True) for j, chip in enumerate(chips)]
        for cp in first:
            cp.start()
        passed = []
        for j, chip in enumerate(chips):
            for a in range(na):
                copy(a, 1 + j, (*chip, c), me).wait_recv()
                passed.append(copy(a, 4 + j, (*chip, c), sibling))
                passed[-1].start()
        for a in range(na):
            copy(a, 0, sibling, me).wait_recv()
            for j, chip in enumerate(chips):
                copy(a, 4 + j, (*chip, 1 - c), me).wait_recv()
        for cp in first + passed:
            cp.wait_send()
        for cp in mine:
            cp.wait()

    return pl.pallas_call(
        body, in_specs=[pl.BlockSpec(memory_space=pl.ANY)] * na, out_specs=[pl.BlockSpec(memory_space=pl.ANY)] * na,
        out_shape=[jax.ShapeDtypeStruct((N_DEV,) + s.shape, s.dtype) for s in shards],
        scratch_shapes=[pltpu.SemaphoreType.DMA((7 * na,)), pltpu.SemaphoreType.DMA((7 * na,)),
                        pltpu.SemaphoreType.DMA((na,))],
        name=name)(*shards)


_HBM = pl.BlockSpec(memory_space=pltpu.HBM)
_SEM = pl.BlockSpec(memory_space=pltpu.SEMAPHORE)
_DATAFLOW = pltpu.SideEffectType.DATAFLOW_SIDE_EFFECTING


def _peers(x, y, c):
    out = []
    for k in range(1, N_DEV):
        px, py, pc = x ^ (k >> 2), y ^ ((k >> 1) & 1), c ^ (k & 1)
        out.append(((px, py, pc), 4 * px + 2 * py + pc))
    return out


def _split_copies(src_refs, land_refs, send_sems, recv_sems, scatter):
    x, y, c = _position()
    me = 4 * x + 2 * y + c
    sends, recvs = [], []
    for k, (peer_id, peer) in enumerate(_peers(x, y, c)):
        for a, (src, land) in enumerate(zip(src_refs, land_refs)):
            sems = dict(send_sem=send_sems.at[7 * a + k], recv_sem=recv_sems.at[7 * a + k],
                        device_id=peer_id, device_id_type=MESH_ID)
            mine = src.at[peer] if scatter else src
            sends.append(pltpu.make_async_remote_copy(src_ref=mine, dst_ref=land.at[me], **sems))
            recvs.append(pltpu.make_async_remote_copy(src_ref=mine, dst_ref=land.at[peer], **sems))
    return sends, recvs


def _all_gather_direct(shards, name, after):
    na = len(shards)

    def body(*refs):
        x_refs, out_refs = refs[:na], refs[na + 1:2 * na + 1]
        send_sems, recv_sems, local_sems = refs[2 * na + 1:]
        x, y, c = _position()
        me = 4 * x + 2 * y + c
        local = [pltpu.make_async_copy(x_refs[a], out_refs[a].at[me], local_sems.at[a]) for a in range(na)]
        sends, recvs = _split_copies(x_refs, out_refs, send_sems, recv_sems, False)
        for cp in local + sends:
            cp.start()
        for cp in recvs:
            cp.wait_recv()
        for cp in sends:
            cp.wait_send()
        for cp in local:
            cp.wait()

    return pl.pallas_call(
        body, in_specs=[pl.BlockSpec(memory_space=pl.ANY)] * (na + 1),
        out_specs=[pl.BlockSpec(memory_space=pl.ANY)] * na,
        out_shape=[jax.ShapeDtypeStruct((N_DEV,) + s.shape, s.dtype) for s in shards],
        scratch_shapes=[pltpu.SemaphoreType.DMA((7 * na,)), pltpu.SemaphoreType.DMA((7 * na,)),
                        pltpu.SemaphoreType.DMA((na,))],
        name=name)(*shards, after)


def _exchange_start(srcs, scatter, name, after=None):
    na = len(srcs)
    lands = [lax.empty(s.shape if scatter else (N_DEV,) + s.shape, s.dtype) for s in srcs]
    extra = [] if after is None else [after]

    def body(*refs):
        src_refs, land_refs = refs[:na], refs[na:2 * na]
        send_sems, recv_sems = refs[2 * na + len(extra)], refs[2 * na + len(extra) + 1]
        token = refs[-1]
        sends, _ = _split_copies(src_refs, land_refs, send_sems, recv_sems, scatter)
        for cp in sends:
            cp.start()
        token[...] = jnp.zeros_like(token)

    hbm = lambda a: pltpu.HBM(a.shape, a.dtype)
    out = pl.pallas_call(
        body, name=name,
        out_shape=(pltpu.SemaphoreType.DMA((7 * na,)), pltpu.SemaphoreType.DMA((7 * na,)),
                   *[hbm(s) for s in srcs], *[hbm(l) for l in lands], jax.ShapeDtypeStruct((8, LANES), F32)),
        in_specs=[_HBM] * (2 * na) + [pl.BlockSpec(memory_space=pl.ANY)] * len(extra),
        out_specs=(_SEM, _SEM, *[_HBM] * (2 * na), pl.BlockSpec(memory_space=pltpu.VMEM)),
        input_output_aliases={i: 2 + i for i in range(2 * na)},
        compiler_params=pltpu.CompilerParams(has_side_effects=_DATAFLOW),
    )(*[pltpu.with_memory_space_constraint(s, pltpu.HBM) for s in srcs],
      *[pltpu.with_memory_space_constraint(l, pltpu.HBM) for l in lands], *extra)
    return (out[0], out[1], list(out[2:2 + na]), list(out[2 + na:2 + 2 * na])), out[-1]


def _exchange_wait(handle, after, scatter, name):
    send_sems, recv_sems, srcs, lands = handle
    na = len(srcs)

    def body(*refs):
        src_refs, land_refs = refs[:na], refs[na:2 * na]
        s_sems, r_sems = refs[2 * na], refs[2 * na + 1]
        sends, recvs = _split_copies(src_refs, land_refs, s_sems, r_sems, scatter)
        for cp in sends:
            cp.wait_send()
        for cp in recvs:
            cp.wait_recv()

    hbm = lambda a: pltpu.HBM(a.shape, a.dtype)
    out = pl.pallas_call(
        body, name=name, out_shape=(*[hbm(s) for s in srcs], *[hbm(l) for l in lands]),
        in_specs=[_HBM] * (2 * na) + [_SEM, _SEM, pl.BlockSpec(memory_space=pl.ANY)],
        out_specs=tuple([_HBM] * (2 * na)), input_output_aliases={i: i for i in range(2 * na)},
        compiler_params=pltpu.CompilerParams(has_side_effects=_DATAFLOW),
    )(*srcs, *lands, send_sems, recv_sems, after)
    return list(out[:na]), list(out[na:])


def _own_slot(landed, own):
    me = 4 * lax.axis_index("x") + 2 * lax.axis_index("y") + lax.axis_index("c")
    return lax.dynamic_update_slice_in_dim(landed, own[None], me, axis=0)


def _adam_update(parts, w, m, v, name, tr=256):
    _, r, c = w.shape
    tr = _pick_rows(r, tr)
    cp = parts.shape[2]

    def body(p_ref, w_ref, m_ref, v_ref, g_ref, d_ref, nm_ref, nv_ref):
        g = p_ref[0, :, pl.ds(0, c)].astype(F32)
        for i in range(1, N_DEV):
            g = g + p_ref[i, :, pl.ds(0, c)].astype(F32)
        delta, nm, nv = _adamw(w_ref[0], g, m_ref[0], v_ref[0])
        g_ref[0] = g
        d_ref[0] = delta
        nm_ref[0] = nm
        nv_ref[0] = nv

    rs = pl.BlockSpec((1, tr, c), lambda i: (0, i, 0))
    return pl.pallas_call(
        body, grid=(r // tr,), in_specs=[pl.BlockSpec((N_DEV, tr, cp), lambda i: (0, i, 0)), rs, rs, rs],
        out_specs=[rs] * 4, out_shape=[jax.ShapeDtypeStruct((1, r, c), F32)] * 4, name=name,
        compiler_params=_params(("parallel",)))(parts, w, m, v)


def _pick_rows(rows, target):
    if rows <= target:
        return rows
    t = target
    while t >= 16:
        if rows % t == 0:
            return t
        t -= 16
    return rows


BIG = ("w_in", "w_branch_dn", "w_branch_swa", "w_out", "w_gate", "w_up", "w_down")
IN_SHARD, IN_WIRE = D_IN // N_DEV, 640
FF_SHARD, FF_WIRE = D_FF // N_DEV, 384
D_FFP = N_DEV * FF_WIRE
BIG_SHAPES = {"w_in": ((D_MODEL, IN_SHARD), (D_MODEL, IN_WIRE)),
              "w_branch_dn": ((DN_WIDTH, LANES), (DN_WIDTH, LANES)),
              "w_branch_swa": ((SWA_WIDTH, LANES), (SWA_WIDTH, LANES)),
              "w_out": ((LANES, D_MODEL), (LANES, D_MODEL)),
              "w_gate": ((D_MODEL, FF_SHARD), (D_MODEL, FF_WIRE)),
              "w_up": ((D_MODEL, FF_SHARD), (D_MODEL, FF_WIRE)),
              "w_down": ((FF_SHARD, D_MODEL), (FF_WIRE, D_MODEL))}
CONV_SHARD, CONV_WIRE = (DN_CONV, DN_QKV // N_DEV), (8, 256)


def _pad_to(a, shape):
    return jnp.pad(a, [(0, t - s) for s, t in zip(a.shape, shape)])


_IN_SEGS = ((R_GATE, 2048, P_GATE), (R_QKV, DN_QKV, P_QKV), (R_Z, DN_WIDTH, P_Z), (R_SQ, SWA_WIDTH, P_SQ),
            (R_SK, SWA_KVW, P_SK), (R_SV, SWA_KVW, P_SV), (R_B, 8, P_BA))


def _w_in_from_blocks(blocks):
    parts = []
    for rs, n, _ in _IN_SEGS:
        for dev in range(N_DEV):
            lo, hi = max(rs, IN_SHARD * dev), min(rs + n, IN_SHARD * (dev + 1))
            if lo < hi:
                parts.append(blocks[dev, :, lo - IN_SHARD * dev:hi - IN_SHARD * dev])
    parts.append(jnp.zeros((blocks.shape[1], P_WIDTH - P_BA - 8), blocks.dtype))
    return jnp.concatenate(parts, axis=1)


def _w_in_to_blocks(g):
    out = []
    for dev in range(N_DEV):
        parts = []
        for rs, n, ps in sorted(_IN_SEGS):
            lo, hi = max(rs, IN_SHARD * dev), min(rs + n, IN_SHARD * (dev + 1))
            if lo < hi:
                parts.append(g[:, ps + lo - rs:ps + hi - rs])
        parts.append(jnp.zeros((g.shape[0], IN_WIRE - IN_SHARD), g.dtype))
        out.append(jnp.concatenate(parts, axis=1))
    return jnp.stack(out)


SMALL = {"attn_norm": (0, (1, D_MODEL)), "ffn_norm": (1, (1, D_MODEL)), "dn_out_norm": (2, (1, DN_DIM)),
         "swa_q_norm": (3, (1, SWA_DIM)), "swa_k_norm": (4, (1, SWA_DIM)), "dn_a_log": (5, (1, DN_HEADS)),
         "dn_dt_bias": (6, (1, DN_HEADS)), "swa_sinks": (7, (1, SWA_HEADS)), "rel_bias": (8, (REL_BUCKETS, SWA_HEADS))}
SMALL_SHEET = (48, D_MODEL)


LOSS_ROW = 40


def _small_pack(grads, loss_local):
    names = list(SMALL)

    def body(*refs):
        o_ref = refs[-1]
        o_ref[...] = jnp.zeros_like(o_ref)
        for n, ref in zip(names, refs):
            r0, (nr, nc) = SMALL[n]
            o_ref[r0:r0 + nr, 0:nc] = ref[...]
        o_ref[LOSS_ROW:LOSS_ROW + 1, 0:1] = refs[len(names)][...]

    return pl.pallas_call(
        body, in_specs=[pl.BlockSpec(memory_space=pltpu.VMEM)] * (len(names) + 1),
        out_specs=pl.BlockSpec(memory_space=pltpu.VMEM), out_shape=jax.ShapeDtypeStruct(SMALL_SHEET, F32),
        name="small_pack", compiler_params=_params())(*[grads[n].reshape(SMALL[n][1]) for n in names], loss_local)


def _small_update(sheets, w, m, v):
    names = list(SMALL)
    k = len(names)

    def body(*refs):
        p_ref = refs[0]
        ins, outs = refs[1:1 + 3 * k], refs[1 + 3 * k:]
        loss = p_ref[0, LOSS_ROW:LOSS_ROW + 1, 0:1]
        for i in range(1, N_DEV):
            loss = loss + p_ref[i, LOSS_ROW:LOSS_ROW + 1, 0:1]
        outs[4 * k][...] = loss
        for t, n in enumerate(names):
            r0, (nr, nc) = SMALL[n]
            g = p_ref[0, r0:r0 + nr, 0:nc]
            for i in range(1, N_DEV):
                g = g + p_ref[i, r0:r0 + nr, 0:nc]
            delta, nm, nv = _adamw(ins[t][...], g, ins[k + t][...], ins[2 * k + t][...])
            for kind, val in enumerate((g, delta, nm, nv)):
                outs[kind * k + t][...] = val

    shapes = [jax.ShapeDtypeStruct(SMALL[n][1], F32) for n in names]
    vm = pl.BlockSpec(memory_space=pltpu.VMEM)
    res = pl.pallas_call(
        body, in_specs=[vm] * (1 + 3 * k), out_specs=[vm] * (4 * k + 1),
        out_shape=shapes * 4 + [jax.ShapeDtypeStruct((1, 1), F32)], name="adam_small", compiler_params=_params(),
    )(sheets, *[d[n].reshape(SMALL[n][1]) for d in (w, m, v) for n in names])
    return {n: tuple(res[kind * k + t] for kind in range(4)) for t, n in enumerate(names)}, res[4 * k]


def kernel(x, attn_norm, w_in, dn_conv, dn_a_log, dn_dt_bias, dn_out_norm, swa_q_norm, swa_k_norm, swa_sinks, rel_bias, w_branch_dn, w_branch_swa, w_out, ffn_norm, w_gate, w_up, w_down, loss_target, m_attn_norm, m_w_in, m_dn_conv, m_dn_a_log, m_dn_dt_bias, m_dn_out_norm, m_swa_q_norm, m_swa_k_norm, m_swa_sinks, m_rel_bias, m_w_branch_dn, m_w_branch_swa, m_w_out, m_ffn_norm, m_w_gate, m_w_up, m_w_down, v_attn_norm, v_w_in, v_dn_conv, v_dn_a_log, v_dn_dt_bias, v_dn_out_norm, v_swa_q_norm, v_swa_k_norm, v_swa_sinks, v_rel_bias, v_w_branch_dn, v_w_branch_swa, v_w_out, v_ffn_norm, v_w_gate, v_w_up, v_w_down):
    args = dict(locals())
    S = x.shape[1]
    xs = x.reshape(S, D_MODEL)
    target = loss_target.reshape(S, D_MODEL)

    w_loc = {n: args[n].reshape(BIG_SHAPES[n][0]) for n in BIG}
    conv_loc = dn_conv.reshape(CONV_SHARD)
    wire = {n: _pad_to(w_loc[n], BIG_SHAPES[n][1]).astype(BF16) for n in BIG}
    first = _all_gather([wire["w_in"], _pad_to(conv_loc, CONV_WIRE)])
    later = [n for n in BIG if n != "w_in"]
    rest_handle, rest_token = _exchange_start([wire[n] for n in later], False, "gather_rest_start", after=first[1])
    w_pad = _w_in_from_blocks(first[0])
    conv_w = jnp.concatenate([first[1][d, :DN_CONV, :CONV_SHARD[1]] for d in range(N_DEV)], axis=1)

    h = _norm_fwd(xs, attn_norm + rest_token[0, 0], "norm1_fwd")
    proj = _mm([(h, w_pad)], "nn", F32, "mm_in", 1024, 1664, j_outer=True)
    qkvn = _dn_conv_fwd(proj, conv_w)
    beta, g = _dn_gate_fwd(proj, dn_a_log, dn_dt_bias)
    u, w, qe, kd, qk, egl, tinv = _dn_prep_fwd(qkvn, g, beta)
    o, states = _dn_scan_fwd(u, w, qe, kd, qk, egl)
    y_dn = _dn_out_fwd(o, proj, dn_out_norm)
    bias = _bias_fwd(rel_bias)
    y_swa = _swa_fwd(proj, swa_q_norm, swa_k_norm, swa_sinks, bias)
    rest_src, rest_land = _exchange_wait(rest_handle, y_swa, False, "gather_rest_wait")
    G = {n: _own_slot(land, src) for n, src, land in zip(later, rest_src, rest_land)}
    w_bdn, w_bswa, w_g, w_u = G["w_branch_dn"], G["w_branch_swa"], G["w_gate"], G["w_up"]
    w_o = G["w_out"].reshape(D_MODEL, D_MODEL)
    w_d = G["w_down"].reshape(D_FFP, D_MODEL)
    gates = [(proj, P_GATE // 512), (proj, (P_GATE + D_MODEL) // 512)]
    a_dn, a_swa, merged = _mm_fused(
        [(y_dn, w_bdn), (y_swa, w_bswa)], "nn", "mm_branch_merge", 1024, 512,
        lambda p, e: (p[0], p[1], _merge(e[0], e[1], p[0], p[1])), gates, (F32, F32, BF16), b_blocks=True)

    def resid_norm(p, e):
        x1 = e[0] + p[0]
        return x1, _rms(x1, e[1])

    x1, h2 = _mm_fused([(merged, w_o)], "nn", "mm_out_norm", 512, D_MODEL, resid_norm,
                       [(xs, 0), (ffn_norm, None)], (F32, BF16))
    gate, up, act = _mm_fused([(h2, w_g), (h2, w_u)], "nn", "mm_gate_up_act", 1024, 768,
                              lambda p, e: (p[0], p[1], _act(p[0], p[1])), [], (F32, F32, BF16),
                              j_outer=True, b_blocks=True)

    def loss_head(p, e):
        diff = e[0] + p[0] - e[1]
        dy = diff * (1.0 / D_MODEL)
        part = jnp.sum(jnp.mean(diff * diff, axis=-1, keepdims=True), axis=0, keepdims=True) * 0.5
        return dy, dy, part

    dy, dy_b, loss_local = _mm_fused([(act, w_d)], "nn", "mm_down_loss", 512, D_MODEL, loss_head,
                                     [(x1, 0), (target, 0)], (F32, BF16), sum_shape=(1, 1))

    def act_bwd(p, e):
        _, vjp = jax.vjp(_act, e[0], e[1])
        return vjp(p[0])

    dgate, dup = _mm_fused([(dy_b, w_d)], "nt", "mm_dact_act", 1024, 768, act_bwd, [(gate, 0), (up, 0)],
                           (BF16, BF16), j_outer=True)
    g_w_down = _mm([(act, dy_b)], "tn", BF16, "mm_dw_down", 768, D_MODEL, j_outer=True)
    g_w_down = g_w_down.reshape(N_DEV, FF_WIRE, D_MODEL)
    g_w_gate = _mm([(h2, dgate)], "tn", BF16, "mm_dw_gate", D_MODEL, 768, out_blocks=True)
    g_w_up = _mm([(h2, dup)], "tn", BF16, "mm_dw_up", D_MODEL, 768, out_blocks=True)
    ffn_handle, ffn_token = _exchange_start([g_w_down, g_w_gate, g_w_up], True, "scatter_ffn_start")

    def norm_bwd(p, e):
        _, vjp = jax.vjp(_rms, e[0], e[2])
        dx, dgain = vjp(sum(p))
        dx = dx + e[1]
        return dx, dx, dgain

    dx1, dx1_b, g_ffn_norm = _mm_fused(
        [(dgate, w_g), (dup, w_u)], "nt", "mm_dh2_norm", 256, D_MODEL, norm_bwd,
        [(x1, 0), (dy, 0), (ffn_norm + ffn_token[0, 0], None)], (F32, BF16), b_blocks=True, sum_shape=(1, D_MODEL))
    def merge_bwd(p, e):
        _, vjp = jax.vjp(_merge, *e)
        dg0, dg1, da_dn, da_swa = vjp(p[0])
        return jnp.concatenate([dg0, dg1], axis=1), da_dn, da_swa

    dproj, da_dn, da_swa = _mm_fused(
        [(dx1_b, w_o)], "nt", "mm_dmerged_merge", 512, D_MODEL, merge_bwd,
        [(proj, P_GATE // D_MODEL), (proj, P_GATE // D_MODEL + 1), (a_dn, 0), (a_swa, 0)], (BF16,) * 3,
        wide_first=(P_WIDTH, 2 * D_MODEL))
    g_w_out = _mm([(merged, dx1_b)], "tn", BF16, "mm_dw_out", 512, D_MODEL, j_outer=True)
    g_w_out = g_w_out.reshape(N_DEV, LANES, D_MODEL)
    dy_dn = _mm([(da_dn, w_bdn)], "nt", F32, "mm_dy_dn", 1024, DN_WIDTH, b_blocks=True)
    dy_swa = _mm([(da_swa, w_bswa)], "nt", F32, "mm_dy_swa", 1024, SWA_WIDTH, b_blocks=True)
    g_w_bdn = _mm([(y_dn, da_dn)], "tn", BF16, "mm_dw_branch_dn", DN_WIDTH, 512, out_blocks=True)
    g_w_bswa = _mm([(y_swa, da_swa)], "tn", BF16, "mm_dw_branch_swa", SWA_WIDTH, 512, out_blocks=True)
    dproj, dsk, dsv, g_q_norm, g_k_norm, g_sinks, dbias = _swa_bwd(proj, swa_q_norm, swa_k_norm, swa_sinks, bias,
                                                                   dy_swa, dproj)
    dproj = _kv_into(dproj, dsk, dsv)
    g_rel_bias = _bias_bwd(dbias)[:, :REL_BUCKETS].T
    mix_handle, mix_token = _exchange_start([g_w_out, g_w_bdn, g_w_bswa], True, "scatter_mix_start")
    do, dproj, g_out_norm = _dn_out_bwd(o, proj, dn_out_norm + mix_token[0, 0], dy_dn, dproj)
    du, dw, dqe, dkd, dqk, degl = _dn_scan_bwd(u, w, qe, kd, qk, egl, states, do)
    dqkvn, dgd, dbeta = _dn_prep_bwd(qkvn, g, beta, tinv, du, dw, dqe, dkd, dqk, degl)
    dproj, dal, ddt = _dn_gate_bwd(proj, dn_a_log, dn_dt_bias, dbeta, dgd, dproj)
    g_a_log = dal.reshape(DN_HEADS, DN_DIM).sum(axis=1)
    g_dt_bias = ddt.reshape(DN_HEADS, DN_DIM).sum(axis=1)
    dproj, g_conv = _dn_conv_bwd(proj, conv_w, dqkvn, dproj)
    g_w_in = _w_in_to_blocks(_mm([(h, dproj)], "tn", BF16, "mm_dw_in", 512, 1664, j_outer=True))
    in_handle, in_token = _exchange_start([g_w_in], True, "scatter_in_start")
    dx, g_attn_norm = _mm_fused(
        [(dproj, w_pad)], "nt", "mm_dh_norm", 512, D_MODEL, lambda p, e: norm_bwd(p, e)[1:],
        [(xs, 0), (dx1, 0), (attn_norm + in_token[0, 0], None)], (F32,), sum_shape=(1, D_MODEL))

    g_small = {"attn_norm": g_attn_norm, "ffn_norm": g_ffn_norm, "rel_bias": g_rel_bias, "dn_out_norm": g_out_norm,
               "swa_q_norm": g_q_norm, "swa_k_norm": g_k_norm, "dn_a_log": g_a_log, "dn_dt_bias": g_dt_bias,
               "swa_sinks": g_sinks}
    me = 4 * lax.axis_index("x") + 2 * lax.axis_index("y") + lax.axis_index("c")
    outs = {}

    def finish(handle, group, name, after):
        srcs, lands = _exchange_wait(handle, after, True, name)
        for n, src, land in zip(group, srcs, lands):
            parts = _own_slot(land, lax.dynamic_index_in_dim(src, me, 0, keepdims=False))
            outs[n] = _adam_update(parts, args[n], args["m_" + n], args["v_" + n], "adam_" + n)

    finish(ffn_handle, ("w_down", "w_gate", "w_up"), "scatter_ffn_wait", dx)
    finish(mix_handle, ("w_out", "w_branch_dn", "w_branch_swa"), "scatter_mix_wait", dx)
    sheets, conv_all = _all_gather_direct([_small_pack(g_small, loss_local), _pad_to(g_conv, (8, DN_QKV))],
                                          "all_gather_small",
                                          after=outs["w_up"][0])
    finish(in_handle, ("w_in",), "scatter_in_wait", sheets)
    conv_parts = lax.dynamic_slice(conv_all, (0, 0, me * CONV_SHARD[1]), (N_DEV,) + CONV_SHARD)
    outs["dn_conv"] = _adam_update(conv_parts, dn_conv, m_dn_conv, v_dn_conv, "adam_dn_conv")
    small_outs, loss = _small_update(sheets, {n: args[n] for n in SMALL}, {n: args["m_" + n] for n in SMALL},
                                     {n: args["v_" + n] for n in SMALL})
    outs.update(small_outs)

    names = ("attn_norm", "w_in", "dn_conv", "dn_a_log", "dn_dt_bias", "dn_out_norm", "swa_q_norm", "swa_k_norm",
             "swa_sinks", "rel_bias", "w_branch_dn", "w_branch_swa", "w_out", "ffn_norm", "w_gate", "w_up", "w_down")
    results = []
    for kind in range(4):
        results += [outs[n][kind].reshape(args[n].shape) for n in names]

    return (loss.reshape(()), dx.reshape(x.shape), *results)
```

```python
import math

import numpy as np
import jax
import jax.numpy as jnp
from jax import lax
from jax.experimental import pallas as pl
from jax.experimental.pallas import tpu as pltpu

F32 = jnp.float32
BF16 = jnp.bfloat16
HI = lax.Precision.HIGHEST

D_MODEL = 1024
DN_HEADS = 4
DN_DIM = 128
DN_WIDTH = 512
DN_QKV = 1536
DN_CONV = 4
CHUNK = 64
SWA_HEADS = 8
SWA_KV = 2
SWA_GROUP = 4
SWA_DIM = 64
SWA_WIDTH = 512
SWA_KVW = 128
WINDOW = 128
BLOCK = 128
REL_BUCKETS = 32
REL_MAX_DIST = 128
D_FF = 2816
D_IN = 4872
EPS = 1e-6
N_DEV = 8

ADAM_LR = 0.001
ADAM_B1 = 0.9
ADAM_B2 = 0.999
ADAM_EPS = 1e-08
ADAM_WD = 0.01
ADAM_STEP = 10

P_GATE, P_QKV, P_Z, P_SQ, P_SK, P_SV, P_BA = 0, 2048, 3584, 4096, 4608, 4736, 4864
P_WIDTH = 4992
R_QKV, R_Z, R_B, R_A, R_SQ, R_SK, R_SV, R_GATE = 0, 1536, 2048, 2052, 2056, 2568, 2696, 2824

VMEM_LIMIT = 56 * 1024 * 1024
LANES = 128
MESH_ID = pl.DeviceIdType.MESH


def _params(sem=None):
    return pltpu.CompilerParams(dimension_semantics=sem, vmem_limit_bytes=VMEM_LIMIT)


def _pick(dim, target):
    if dim <= target:
        return dim
    t = target - target % LANES
    while t >= LANES:
        if dim % t == 0:
            return t
        t -= LANES
    return dim


_DIMS = {"nn": (((1,), (0,)), ((), ())), "nt": (((1,), (1,)), ((), ())), "tn": (((0,), (0,)), ((), ()))}


def _tile_product(a_ref, b_ref, mode, b_blocks):
    a = a_ref[...].astype(BF16)
    b = jnp.concatenate([b_ref[d] for d in range(b_ref.shape[0])], axis=1) if b_blocks else b_ref[...]
    return lax.dot_general(a, b.astype(BF16), _DIMS[mode], preferred_element_type=F32)


def _mm(pairs, mode, out_dtype, name, bm, bn, j_outer=False, b_blocks=False, out_blocks=False):
    a0, b0 = pairs[0]
    cb = b0.shape[2] if b_blocks else None
    b_shape = (b0.shape[1], N_DEV * cb) if b_blocks else b0.shape
    if mode == "nn":
        (M, K), (K2, N) = a0.shape, b_shape
    elif mode == "nt":
        (M, K), (N, K2) = a0.shape, b_shape
    else:
        (K, M), (K2, N) = a0.shape, b_shape
    bm, bn = min(bm, M), min(bn, N)
    assert K == K2 and M % bm == 0 and N % bn == 0, (name, a0.shape, b0.shape, bm, bn)
    co = N // N_DEV
    assert not out_blocks or bn % co == 0
    dims = _DIMS[mode]
    n = len(pairs)

    def body(*refs):
        o_ref = refs[2 * n]
        acc = None
        for t in range(n):
            p = _tile_product(refs[2 * t], refs[2 * t + 1], mode, b_blocks)
            acc = p if acc is None else acc + p
        if out_blocks:
            for d in range(bn // co):
                o_ref[d] = acc[:, d * co:(d + 1) * co].astype(out_dtype)
        else:
            o_ref[...] = acc.astype(out_dtype)

    def ij(f):
        return (lambda j, i: f(i, j)) if j_outer else f

    a_spec = pl.BlockSpec((K, bm), ij(lambda i, j: (0, i))) if mode == "tn" else pl.BlockSpec((bm, K), ij(lambda i, j: (i, 0)))
    if b_blocks and mode == "nt":
        b_spec = pl.BlockSpec((N_DEV, bn, cb), ij(lambda i, j: (0, j, 0)))
    elif b_blocks:
        b_spec = pl.BlockSpec((bn // cb, K, cb), ij(lambda i, j: (j, 0, 0)))
    elif mode == "nt":
        b_spec = pl.BlockSpec((bn, K), ij(lambda i, j: (j, 0)))
    else:
        b_spec = pl.BlockSpec((K, bn), ij(lambda i, j: (0, j)))
    if out_blocks:
        out_spec = pl.BlockSpec((bn // co, bm, co), ij(lambda i, j: (j, i, 0)))
        out_shape = jax.ShapeDtypeStruct((N_DEV, M, co), out_dtype)
    else:
        out_spec = pl.BlockSpec((bm, bn), ij(lambda i, j: (i, j)))
        out_shape = jax.ShapeDtypeStruct((M, N), out_dtype)
    grid = (N // bn, M // bm) if j_outer else (M // bm, N // bn)
    return pl.pallas_call(
        body, grid=grid, in_specs=[a_spec, b_spec] * n, out_specs=out_spec, out_shape=out_shape, name=name,
        compiler_params=_params(("parallel", "parallel")),
    )(*[x for pair in pairs for x in pair])


def _mm_fused(pairs, mode, name, bm, bn, epilogue, extras, out_dtypes, j_outer=False, b_blocks=False,
              sum_shape=None, wide_first=None):
    a0, b0 = pairs[0]
    cb = b0.shape[2] if b_blocks else None
    b_shape = (b0.shape[1], N_DEV * cb) if b_blocks else b0.shape
    if mode == "nn":
        (M, K), (K2, N) = a0.shape, b_shape
    else:
        (M, K), (N, K2) = a0.shape, b_shape
    bm, bn = min(bm, M), min(bn, N)
    assert mode in ("nn", "nt") and K == K2 and M % bm == 0 and N % bn == 0, (name, a0.shape, b0.shape)
    dims = _DIMS[mode]
    n, ne, no = len(pairs), len(extras), len(out_dtypes)

    def body(*refs):
        prods = [_tile_product(refs[2 * t], refs[2 * t + 1], mode, b_blocks) for t in range(n)]
        results = epilogue(prods, [r[...] for r in refs[2 * n:2 * n + ne]])
        out_refs = refs[2 * n + ne:]
        for o_ref, val, dt in zip(out_refs, results, out_dtypes):
            o_ref[...] = val.astype(dt)
        if sum_shape is not None:
            s_ref = out_refs[no]

            @pl.when((pl.program_id(0) == 0) & (pl.program_id(1) == 0))
            def _():
                s_ref[...] = jnp.zeros_like(s_ref)

            s_ref[...] += results[no]

    def ij(f):
        return (lambda j, i: f(i, j)) if j_outer else f

    a_spec = pl.BlockSpec((bm, K), ij(lambda i, j: (i, 0)))
    once = dict(pipeline_mode=pl.Buffered(1)) if bn == N else {}
    if b_blocks and mode == "nt":
        b_spec = pl.BlockSpec((N_DEV, bn, cb), ij(lambda i, j: (0, j, 0)), **once)
    elif b_blocks:
        b_spec = pl.BlockSpec((bn // cb, K, cb), ij(lambda i, j: (j, 0, 0)), **once)
    elif mode == "nt":
        b_spec = pl.BlockSpec((bn, K), ij(lambda i, j: (j, 0)), **once)
    else:
        b_spec = pl.BlockSpec((K, bn), ij(lambda i, j: (0, j)), **once)
    e_specs = [pl.BlockSpec((1, bn), ij(lambda i, j: (0, j))) if first is None
               else pl.BlockSpec((bm, bn), ij(lambda i, j, first=first: (i, first + j))) for _, first in extras]
    tile = pl.BlockSpec((bm, bn), ij(lambda i, j: (i, j)))
    out_specs = [tile] * no
    out_shape = [jax.ShapeDtypeStruct((M, N), dt) for dt in out_dtypes]
    if wide_first is not None:
        assert bn == N
        out_specs[0] = pl.BlockSpec((bm, wide_first[1]), ij(lambda i, j: (i, 0)))
        out_shape[0] = jax.ShapeDtypeStruct((M, wide_first[0]), out_dtypes[0])
    if sum_shape is not None:
        assert sum_shape[1] in (1, bn) and (sum_shape[1] == 1 or bn == N)
        out_specs.append(_full(sum_shape))
        out_shape.append(jax.ShapeDtypeStruct(sum_shape, F32))
    grid = (N // bn, M // bm) if j_outer else (M // bm, N // bn)
    sem = ("arbitrary", "arbitrary") if sum_shape is not None else ("parallel", "parallel")
    return pl.pallas_call(
        body, grid=grid, in_specs=[a_spec, b_spec] * n + e_specs, out_specs=out_specs, out_shape=out_shape,
        name=name, compiler_params=_params(sem),
    )(*[x for pair in pairs for x in pair], *[arr for arr, _ in extras])


def _rms(x, gain):
    return x * lax.rsqrt(jnp.mean(x * x, axis=-1, keepdims=True) + EPS) * gain


def _silu(x):
    return x * jax.nn.sigmoid(x)


def _act(g, u):
    return _silu(g) * u


def _merge(g0, g1, a_dn, a_swa):
    return jax.nn.sigmoid(g0) * a_dn + jax.nn.sigmoid(g1) * a_swa


def _dn_post(c, is_v, q_scale):
    a = _silu(c)
    rs = lax.rsqrt(jnp.sum(a * a, axis=-1, keepdims=True) + EPS) * q_scale
    return a * jnp.where(is_v, 1.0, rs)


def _dn_out(o, z, gain):
    return _rms(o, gain) * _silu(z)


def _dot(a, b, dims=_DIMS["nn"], hi=False):
    if a.ndim == 3 or b.ndim == 3:
        batch = a.shape[0] if a.ndim == 3 else b.shape[0]
        a = a if a.ndim == 3 else jnp.broadcast_to(a, (batch,) + a.shape)
        b = b if b.ndim == 3 else jnp.broadcast_to(b, (batch,) + b.shape)
        ((ca,), (cb,)), _ = dims
        dims = (((ca + 1,), (cb + 1,)), ((0,), (0,)))
    if hi:
        return lax.dot_general(a, b, dims, precision=HI, preferred_element_type=F32)
    return lax.dot_general(a.astype(BF16), b.astype(BF16), dims, preferred_element_type=F32)


def _pieces(x):
    hi = x.astype(BF16)
    r1 = x - hi.astype(F32)
    mid = r1.astype(BF16)
    return hi, mid, (r1 - mid.astype(F32)).astype(BF16)


def _sel_left_impl(m, x):
    mb = m.astype(BF16)
    hi, mid, lo = _pieces(x)
    return _dot(mb, hi) + (_dot(mb, mid) + _dot(mb, lo))


@jax.custom_vjp
def _sel_left(m, mt, x):
    return _sel_left_impl(m, x)


_sel_left.defvjp(lambda m, mt, x: (_sel_left_impl(m, x), (m, mt)),
                 lambda res, ct: (jnp.zeros_like(res[0]), jnp.zeros_like(res[1]), _sel_left_impl(res[1], ct)))


def _sel_right_impl(x, s):
    sb = s.astype(BF16)
    hi, mid, lo = _pieces(x)
    return _dot(hi, sb) + (_dot(mid, sb) + _dot(lo, sb))


@jax.custom_vjp
def _sel_right(x, s, st):
    return _sel_right_impl(x, s)


_sel_right.defvjp(lambda x, s, st: (_sel_right_impl(x, s), (s, st)),
                  lambda res, ct: (_sel_right_impl(ct, res[1]), jnp.zeros_like(res[0]), jnp.zeros_like(res[1])))


def _sel_nt_impl(s, x):
    sb = s.astype(BF16)
    hi, mid, lo = _pieces(x)
    return _dot(sb, hi, _DIMS["nt"]) + (_dot(sb, mid, _DIMS["nt"]) + _dot(sb, lo, _DIMS["nt"]))


def _sel_tn_impl(x, s):
    sb = s.astype(BF16)
    hi, mid, lo = _pieces(x)
    return _dot(hi, sb, _DIMS["tn"]) + (_dot(mid, sb, _DIMS["tn"]) + _dot(lo, sb, _DIMS["tn"]))


@jax.custom_vjp
def _sel_nt(s, x):
    return _sel_nt_impl(s, x)


_sel_nt.defvjp(lambda s, x: (_sel_nt_impl(s, x), s),
               lambda s, ct: (jnp.zeros_like(s), _sel_tn_impl(ct, s)))


def _dot3_impl(a, b):
    a_hi, a_lo, _ = _pieces(a)
    b_hi, b_lo, _ = _pieces(b)
    return _dot(a_hi, b_hi) + (_dot(a_hi, b_lo) + _dot(a_lo, b_hi))


@jax.custom_vjp
def _dot3(a, b):
    return _dot3_impl(a, b)


_dot3.defvjp(lambda a, b: (_dot3_impl(a, b), (a, b)),
             lambda res, ct: (_dot(ct, res[1], _DIMS["nt"]), _dot(res[0], ct, _DIMS["tn"])))


def _inv_impl(a, eye, strict):
    t = eye - a
    p = _dot(a, a)
    for level in range(5):
        t = t + _dot(t, p)
        if level < 4:
            p = _dot(p, p)
    t = t + _dot(t, eye - t - _dot3_impl(a, t))
    return jnp.where(strict > 0.5, t, eye)


@jax.custom_vjp
def _inv_given(a, t):
    return t.astype(F32)


_inv_given.defvjp(lambda a, t: (t.astype(F32), t),
                  lambda t, ct: (-_dot(_dot(t, ct, _DIMS["tn"]), t, _DIMS["nt"]), jnp.zeros_like(t)))


@jax.custom_vjp
def _lanes_join(a, b):
    return jnp.concatenate([a, b], axis=-1)


_lanes_join.defvjp(lambda a, b: (jnp.concatenate([a, b], axis=-1), None),
                   lambda _, ct: (ct[..., :ct.shape[-1] // 2], ct[..., ct.shape[-1] // 2:]))


@jax.custom_vjp
def _lanes_halves(y):
    h = y.shape[-1] // 2
    return y[..., :h], y[..., h:]


_lanes_halves.defvjp(lambda y: ((y[..., :y.shape[-1] // 2], y[..., y.shape[-1] // 2:]), None),
                     lambda _, ct: (jnp.concatenate(ct, axis=-1),))

GROUP = 4
GROUP_ROWS = GROUP * CHUNK


def _block_consts(n):
    ii = lax.broadcasted_iota(jnp.int32, (n, n), 0)
    jj = lax.broadcasted_iota(jnp.int32, (n, n), 1)
    shift = CHUNK.bit_length() - 1
    same = jnp.right_shift(ii, shift) == jnp.right_shift(jj, shift)
    return same & (ii >= jj), same & (ii <= jj), same & (ii > jj), same, ii == jj


def _lane0(n):
    s = (lax.broadcasted_iota(jnp.int32, (LANES, n), 0) == 0).astype(F32)
    st = (lax.broadcasted_iota(jnp.int32, (n, LANES), 1) == 0).astype(F32)
    return s, st


def _dn_group(q, k, v, g, beta, t_saved=None):
    n = GROUP_ROWS
    low_b, upp_b, strict_b, _, eye_b = _block_consts(n)
    low, upp, eye = low_b.astype(F32), upp_b.astype(F32), eye_b.astype(F32)
    s, st = _lane0(n)
    gc = _sel_left(low, upp, g)
    per_chunk = (g.shape[0], GROUP, CHUNK, LANES)
    gl = jnp.broadcast_to(jnp.sum(g.reshape(per_chunk), axis=2, keepdims=True), per_chunk).reshape(g.shape)
    col = _sel_right(gc, s, st)
    row = _sel_nt(st, gc)
    decay = jnp.exp(jnp.where(low_b, col - row, -jnp.inf))
    kb = k * beta
    vb = v * beta
    a = jnp.where(strict_b, _dot(kb, k, _DIMS["nt"]) * decay, 0.0)
    t = _inv_impl(a, eye, strict_b.astype(F32)) if t_saved is None else _inv_given(a, t_saved)
    u, w = _lanes_halves(_dot3(t, _lanes_join(vb, kb * jnp.exp(gc))))
    return u, w, q * jnp.exp(gc), k * jnp.exp(gl - gc), t


def _dn_chunk(q, k, g):
    ii = lax.broadcasted_iota(jnp.int32, (CHUNK, CHUNK), 0)
    jj = lax.broadcasted_iota(jnp.int32, (CHUNK, CHUNK), 1)
    low = (ii >= jj).astype(F32)
    upp = (ii <= jj).astype(F32)
    s, st = _lane0(CHUNK)
    gc = _sel_left(low, upp, g)
    col = _sel_right(gc, s, st)
    row = _sel_nt(st, gc)
    decay = jnp.exp(jnp.where(ii >= jj, col - row, -jnp.inf))
    qk = _dot(q, k, _DIMS["nt"]) * decay
    return qk, jnp.exp(jnp.sum(g, axis=-2, keepdims=True))


def _dn_step(s, u, w, qe, kd, qk, egl):
    v_new = u - _dot(w, s)
    o = _dot(qe, s) + _dot(qk, v_new)
    s_new = s * egl + _dot(kd, v_new, _DIMS["tn"])
    return s_new, o


def _swa_block(q, kband, vband, qg, kg, sinks, bias, mask):
    kn = _rms(kband, kg)
    qn = _rms(q, qg)
    logits = _dot(qn, kn, _DIMS["nt"]) * (SWA_DIM ** -0.5)
    logits = jnp.where(mask, logits + bias, -jnp.inf)
    m = lax.stop_gradient(jnp.maximum(jnp.max(logits, axis=-1, keepdims=True), sinks))
    p = jnp.exp(logits - m)
    denom = jnp.sum(p, axis=-1, keepdims=True) + jnp.exp(sinks - m)
    return _dot(p * (1.0 / denom), vband)


def _adamw(w, g, m, v):
    m = ADAM_B1 * m + (1.0 - ADAM_B1) * g
    v = ADAM_B2 * v + (1.0 - ADAM_B2) * jnp.square(g)
    m_hat = m / (1.0 - ADAM_B1 ** ADAM_STEP)
    v_hat = v / (1.0 - ADAM_B2 ** ADAM_STEP)
    delta = -ADAM_LR * (m_hat / (jnp.sqrt(v_hat) + ADAM_EPS) + ADAM_WD * w)
    return delta, m, v


def _row(tm, c, cb=0):
    return pl.BlockSpec((tm, c), lambda i, cb=cb: (i, cb))


def _full(shape):
    nd = len(shape)
    return pl.BlockSpec(shape, lambda *_, nd=nd: (0,) * nd)


def _norm_fwd(x, gain, name, tm=1024):
    S = x.shape[0]

    def body(x_ref, g_ref, h_ref):
        h_ref[...] = _rms(x_ref[...], g_ref[...]).astype(BF16)

    return pl.pallas_call(
        body, grid=(S // tm,), in_specs=[_row(tm, D_MODEL), _full((1, D_MODEL))],
        out_specs=_row(tm, D_MODEL), out_shape=jax.ShapeDtypeStruct((S, D_MODEL), BF16),
        name=name, compiler_params=_params(("parallel",)))(x, gain)


def _shift_down(x, s):
    row = lax.broadcasted_iota(jnp.int32, x.shape, 0)
    return jnp.where(row >= s, pltpu.roll(x, s, axis=0), 0.0)


def _shift_up(x, s):
    n = x.shape[0]
    row = lax.broadcasted_iota(jnp.int32, x.shape, 0)
    return jnp.where(row < n - s, pltpu.roll(x, n - s, axis=0), 0.0)


def _conv(x, w):
    out = w[DN_CONV - 1:DN_CONV] * x
    for s in range(1, DN_CONV):
        out = out + w[DN_CONV - 1 - s:DN_CONV - s] * _shift_down(x, s)
    return out


def _dn_conv_fwd(proj, conv_w):
    S = proj.shape[0]
    nb = DN_QKV // LANES

    def body(x_ref, w_ref, o_ref):
        j = pl.program_id(0)
        q_scale = jnp.where(j < DN_HEADS, DN_DIM ** -0.5, 1.0).astype(F32)
        o_ref[...] = _dn_post(_conv(x_ref[...], w_ref[...]), j >= 2 * DN_HEADS, q_scale)

    return pl.pallas_call(
        body, grid=(nb,),
        in_specs=[pl.BlockSpec((S, LANES), lambda j: (0, P_QKV // LANES + j)),
                  pl.BlockSpec((DN_CONV, LANES), lambda j: (0, j))],
        out_specs=pl.BlockSpec((S, LANES), lambda j: (0, j)),
        out_shape=jax.ShapeDtypeStruct((S, DN_QKV), F32), name="dn_conv_fwd",
        compiler_params=_params(("parallel",)))(proj, conv_w)


def _dn_conv_bwd(proj, conv_w, dqkvn, dproj):
    S = proj.shape[0]
    nb = DN_QKV // LANES

    def body(x_ref, w_ref, d_ref, _, dx_ref, dw_ref):
        j = pl.program_id(0)
        q_scale = jnp.where(j < DN_HEADS, DN_DIM ** -0.5, 1.0).astype(F32)
        x = x_ref[...]
        w = w_ref[...]
        _, vjp = jax.vjp(lambda c: _dn_post(c, j >= 2 * DN_HEADS, q_scale), _conv(x, w))
        (dc,) = vjp(d_ref[0])
        dx = w[DN_CONV - 1:DN_CONV] * dc
        dw_ref[DN_CONV - 1:DN_CONV, :] = jnp.sum(dc * x, axis=0, keepdims=True)
        for s in range(1, DN_CONV):
            dx = dx + w[DN_CONV - 1 - s:DN_CONV - s] * _shift_up(dc, s)
            dw_ref[DN_CONV - 1 - s:DN_CONV - s, :] = jnp.sum(dc * _shift_down(x, s), axis=0, keepdims=True)
        dx_ref[...] = dx.astype(BF16)

    return pl.pallas_call(
        body, grid=(nb,),
        in_specs=[pl.BlockSpec((S, LANES), lambda j: (0, P_QKV // LANES + j)),
                  pl.BlockSpec((DN_CONV, LANES), lambda j: (0, j)),
                  pl.BlockSpec((1, S, LANES), lambda j: (lax.div(j, DN_HEADS), 0, lax.rem(j, DN_HEADS))),
                  pl.BlockSpec(memory_space=pl.ANY)],
        out_specs=[pl.BlockSpec((S, LANES), lambda j: (0, P_QKV // LANES + j)),
                   pl.BlockSpec((DN_CONV, LANES), lambda j: (0, j))],
        out_shape=[jax.ShapeDtypeStruct(dproj.shape, dproj.dtype), jax.ShapeDtypeStruct((DN_CONV, DN_QKV), F32)],
        input_output_aliases={3: 0},
        name="dn_conv_bwd", compiler_params=_params(("parallel",)))(proj, conv_w, dqkvn, dproj)


def _expanders():
    eb = np.zeros((LANES, DN_WIDTH), np.float32)
    ea = np.zeros((LANES, DN_WIDTH), np.float32)
    for h in range(DN_HEADS):
        eb[h, h * DN_DIM:(h + 1) * DN_DIM] = 1.0
        ea[DN_HEADS + h, h * DN_DIM:(h + 1) * DN_DIM] = 1.0
    return jnp.asarray(eb), jnp.asarray(ea), jnp.asarray(eb.T), jnp.asarray(ea.T)


def _dn_gate_args(a_log, dt_bias):
    alog = jnp.repeat(a_log.reshape(1, DN_HEADS), DN_DIM, axis=1)
    dtb = jnp.repeat(dt_bias.reshape(1, DN_HEADS), DN_DIM, axis=1)
    return _expanders() + (alog, dtb)


def _dn_gate_specs(tm):
    return [_row(tm, LANES, P_BA // LANES), _full((LANES, DN_WIDTH)), _full((LANES, DN_WIDTH)),
            _full((DN_WIDTH, LANES)), _full((DN_WIDTH, LANES)), _full((1, DN_WIDTH)), _full((1, DN_WIDTH))]


def _dn_gate_fn(ba, eb, ea, ebt, eat, alog, dtb):
    beta = jax.nn.sigmoid(_sel_right(ba, eb, ebt))
    g = -jnp.exp(alog) * jax.nn.softplus(_sel_right(ba, ea, eat) + dtb)
    return beta, g


def _dn_gate_fwd(proj, a_log, dt_bias, tm=1024):
    S = proj.shape[0]
    args = _dn_gate_args(a_log, dt_bias)

    def body(ba_ref, eb_ref, ea_ref, ebt_ref, eat_ref, al_ref, dt_ref, beta_ref, g_ref):
        beta, g = _dn_gate_fn(ba_ref[...], eb_ref[...], ea_ref[...], ebt_ref[...], eat_ref[...], al_ref[...],
                              dt_ref[...])
        beta_ref[...] = beta
        g_ref[...] = g

    return pl.pallas_call(
        body, grid=(S // tm,), in_specs=_dn_gate_specs(tm), out_specs=[_row(tm, DN_WIDTH), _row(tm, DN_WIDTH)],
        out_shape=[jax.ShapeDtypeStruct((S, DN_WIDTH), F32), jax.ShapeDtypeStruct((S, DN_WIDTH), F32)],
        name="dn_gate_fwd", compiler_params=_params(("parallel",)))(proj, *args)


def _dn_gate_bwd(proj, a_log, dt_bias, dbeta, dg, dproj, tm=1024):
    S = proj.shape[0]
    args = _dn_gate_args(a_log, dt_bias)

    def body(ba_ref, eb_ref, ea_ref, ebt_ref, eat_ref, al_ref, dt_ref, dbeta_ref, dg_ref, _, dba_ref, dal_ref,
             ddt_ref):
        eb, ea, ebt, eat = eb_ref[...], ea_ref[...], ebt_ref[...], eat_ref[...]
        _, vjp = jax.vjp(lambda ba, al, dt: _dn_gate_fn(ba, eb, ea, ebt, eat, al, dt), ba_ref[...], al_ref[...],
                         dt_ref[...])
        dba, dal, ddt = vjp((dbeta_ref[...], dg_ref[...]))
        dba_ref[...] = dba.astype(BF16)

        @pl.when(pl.program_id(0) == 0)
        def _():
            dal_ref[...] = jnp.zeros_like(dal_ref)
            ddt_ref[...] = jnp.zeros_like(ddt_ref)

        dal_ref[...] += dal
        ddt_ref[...] += ddt

    return pl.pallas_call(
        body, grid=(S // tm,),
        in_specs=_dn_gate_specs(tm) + [_row(tm, DN_WIDTH), _row(tm, DN_WIDTH), pl.BlockSpec(memory_space=pl.ANY)],
        out_specs=[_row(tm, LANES, P_BA // LANES), _full((1, DN_WIDTH)), _full((1, DN_WIDTH))],
        out_shape=[jax.ShapeDtypeStruct(dproj.shape, dproj.dtype), jax.ShapeDtypeStruct((1, DN_WIDTH), F32),
                   jax.ShapeDtypeStruct((1, DN_WIDTH), F32)],
        input_output_aliases={len(args) + 3: 0},
        name="dn_gate_bwd", compiler_params=_params(("arbitrary",)))(proj, *args, dbeta, dg, dproj)


PREP_GROUPS = 4
PREP_CHUNKS = GROUP * PREP_GROUPS


def _dn_prep_specs():
    rows = PREP_CHUNKS * CHUNK
    q = pl.BlockSpec((rows, LANES), lambda h, c: (c, h))
    k = pl.BlockSpec((rows, LANES), lambda h, c: (c, DN_HEADS + h))
    v = pl.BlockSpec((rows, LANES), lambda h, c: (c, 2 * DN_HEADS + h))
    qk = pl.BlockSpec((1, rows, CHUNK), lambda h, c: (h, c, 0))
    egl = pl.BlockSpec((1, PREP_CHUNKS, 1, LANES), lambda h, c: (h, c, 0, 0))
    return q, k, v, qk, egl


def _dn_prep_fwd(qkvn, g, beta):
    S = qkvn.shape[0]
    nc = S // CHUNK
    q, k, v, qks, egl = _dn_prep_specs()

    def body(q_ref, k_ref, v_ref, g_ref, b_ref, u_ref, w_ref, qe_ref, kd_ref, qk_ref, egl_ref, t_ref):
        rows = PREP_CHUNKS * CHUNK
        grp = (PREP_GROUPS, GROUP_ROWS, LANES)
        chk = (PREP_CHUNKS, CHUNK, LANES)
        q, k, g = q_ref[...], k_ref[...], g_ref[...]
        u, w, qe, kd, t = _dn_group(q.reshape(grp), k.reshape(grp), v_ref[...].reshape(grp), g.reshape(grp),
                                    b_ref[...].reshape(grp))
        u_ref[...] = u.reshape(rows, LANES)
        w_ref[...] = w.reshape(rows, LANES)
        qe_ref[...] = qe.reshape(rows, LANES)
        kd_ref[...] = kd.reshape(rows, LANES)
        t_ref[0] = t.reshape(rows, GROUP_ROWS).astype(BF16)
        qk, e = _dn_chunk(q.reshape(chk), k.reshape(chk), g.reshape(chk))
        qk_ref[0] = qk.reshape(rows, CHUNK)
        egl_ref[0] = e

    wide = jax.ShapeDtypeStruct((S, DN_WIDTH), F32)
    return pl.pallas_call(
        body, grid=(DN_HEADS, nc // PREP_CHUNKS), in_specs=[q, k, v, q, q],
        out_specs=[q, q, q, q, qks, egl, _dn_tinv_spec()],
        out_shape=[wide, wide, wide, wide, jax.ShapeDtypeStruct((DN_HEADS, S, CHUNK), F32),
                   jax.ShapeDtypeStruct((DN_HEADS, nc, 1, LANES), F32),
                   jax.ShapeDtypeStruct((DN_HEADS, S, GROUP_ROWS), BF16)],
        name="dn_prep_fwd", compiler_params=_params(("parallel", "parallel")))(qkvn, qkvn, qkvn, g, beta)


def _dn_tinv_spec():
    return pl.BlockSpec((1, PREP_CHUNKS * CHUNK, GROUP_ROWS), lambda h, c: (h, c, 0))


def _dn_prep_bwd(qkvn, g, beta, tinv, du, dw, dqe, dkd, dqk, degl):
    S = qkvn.shape[0]
    nc = S // CHUNK
    q, k, v, qks, egl = _dn_prep_specs()

    def body(q_ref, k_ref, v_ref, g_ref, b_ref, t_ref, du_ref, dw_ref, dqe_ref, dkd_ref, dqk_ref, degl_ref,
             dqkv_ref, dg_ref, db_ref):
        rows = PREP_CHUNKS * CHUNK
        grp = (PREP_GROUPS, GROUP_ROWS, LANES)
        chk = (PREP_CHUNKS, CHUNK, LANES)
        q, k, g = q_ref[...], k_ref[...], g_ref[...]
        t_saved = t_ref[0].reshape(PREP_GROUPS, GROUP_ROWS, GROUP_ROWS)
        _, vjp = jax.vjp(lambda *x: _dn_group(*x, t_saved=t_saved)[:4], q.reshape(grp), k.reshape(grp),
                         v_ref[...].reshape(grp), g.reshape(grp), b_ref[...].reshape(grp))
        dq, dk, dv, dg, db = vjp((du_ref[...].reshape(grp), dw_ref[...].reshape(grp), dqe_ref[...].reshape(grp),
                                  dkd_ref[...].reshape(grp)))
        _, vjp = jax.vjp(_dn_chunk, q.reshape(chk), k.reshape(chk), g.reshape(chk))
        dq2, dk2, dg2 = vjp((dqk_ref[0].reshape(PREP_CHUNKS, CHUNK, CHUNK), degl_ref[0]))
        dqkv_ref[0] = dq.reshape(rows, LANES) + dq2.reshape(rows, LANES)
        dqkv_ref[1] = dk.reshape(rows, LANES) + dk2.reshape(rows, LANES)
        dqkv_ref[2] = dv.reshape(rows, LANES)
        dg_ref[...] = dg.reshape(rows, LANES) + dg2.reshape(rows, LANES)
        db_ref[...] = db.reshape(rows, LANES)

    wide = jax.ShapeDtypeStruct((S, DN_WIDTH), F32)
    rows = PREP_CHUNKS * CHUNK
    return pl.pallas_call(
        body, grid=(DN_HEADS, nc // PREP_CHUNKS), in_specs=[q, k, v, q, q, _dn_tinv_spec(), q, q, q, q, qks, egl],
        out_specs=[pl.BlockSpec((3, rows, LANES), lambda h, c: (0, c, h)), q, q],
        out_shape=[jax.ShapeDtypeStruct((3, S, DN_WIDTH), F32), wide, wide],
        name="dn_prep_bwd", compiler_params=_params(("parallel", "parallel")),
    )(qkvn, qkvn, qkvn, g, beta, tinv, du, dw, dqe, dkd, dqk, degl)


SCAN_CHUNKS = 8


def _dn_scan_specs(nc, reverse):
    nb = nc // SCAN_CHUNKS

    def cidx(c):
        return nb - 1 - c if reverse else c

    hc = pl.BlockSpec((SCAN_CHUNKS * CHUNK, DN_WIDTH), lambda c: (cidx(c), 0))
    qk = pl.BlockSpec((DN_HEADS, SCAN_CHUNKS * CHUNK, CHUNK), lambda c: (0, cidx(c), 0))
    egl = pl.BlockSpec((DN_HEADS, SCAN_CHUNKS, 1, LANES), lambda c: (0, cidx(c), 0, 0))
    st = pl.BlockSpec((DN_HEADS, SCAN_CHUNKS, DN_DIM, DN_DIM), lambda c: (0, cidx(c), 0, 0))
    return hc, qk, egl, st


def _heads(ref, i):
    return jnp.stack([ref[pl.ds(i * CHUNK, CHUNK), pl.ds(h * DN_DIM, DN_DIM)] for h in range(DN_HEADS)])


def _dn_scan_fwd(u, w, qe, kd, qk, egl):
    S = u.shape[0]
    nc = S // CHUNK
    hc, qks, egls, st = _dn_scan_specs(nc, False)

    def body(u_ref, w_ref, qe_ref, kd_ref, qk_ref, egl_ref, o_ref, st_ref, s_scr):
        @pl.when(pl.program_id(0) == 0)
        def _():
            s_scr[...] = jnp.zeros_like(s_scr)

        s = s_scr[...]
        for i in range(SCAN_CHUNKS):
            rows = pl.ds(i * CHUNK, CHUNK)
            st_ref[:, i] = s
            s, o = _dn_step(s, _heads(u_ref, i), _heads(w_ref, i), _heads(qe_ref, i), _heads(kd_ref, i),
                            qk_ref[:, rows, :], egl_ref[:, i])
            for h in range(DN_HEADS):
                o_ref[rows, pl.ds(h * DN_DIM, DN_DIM)] = o[h]
        s_scr[...] = s

    return pl.pallas_call(
        body, grid=(nc // SCAN_CHUNKS,), in_specs=[hc, hc, hc, hc, qks, egls], out_specs=[hc, st],
        out_shape=[jax.ShapeDtypeStruct((S, DN_WIDTH), F32), jax.ShapeDtypeStruct((DN_HEADS, nc, DN_DIM, DN_DIM), F32)],
        scratch_shapes=[pltpu.VMEM((DN_HEADS, DN_DIM, DN_DIM), F32)], name="dn_scan_fwd",
        compiler_params=_params(("arbitrary",)))(u, w, qe, kd, qk, egl)


def _dn_scan_bwd(u, w, qe, kd, qk, egl, states, do):
    S = u.shape[0]
    nc = S // CHUNK
    hc, qks, egls, st = _dn_scan_specs(nc, True)

    def body(u_ref, w_ref, qe_ref, kd_ref, qk_ref, egl_ref, st_ref, do_ref,
             du_ref, dw_ref, dqe_ref, dkd_ref, dqk_ref, degl_ref, ds_scr):
        @pl.when(pl.program_id(0) == 0)
        def _():
            ds_scr[...] = jnp.zeros_like(ds_scr)

        ds = ds_scr[...]
        for i in reversed(range(SCAN_CHUNKS)):
            rows = pl.ds(i * CHUNK, CHUNK)
            _, vjp = jax.vjp(_dn_step, st_ref[:, i], _heads(u_ref, i), _heads(w_ref, i), _heads(qe_ref, i),
                             _heads(kd_ref, i), qk_ref[:, rows, :], egl_ref[:, i])
            ds, du, dw, dqe, dkd, dqk, degl = vjp((ds, _heads(do_ref, i)))
            dqk_ref[:, rows, :] = dqk
            degl_ref[:, i] = degl
            for h in range(DN_HEADS):
                cols = pl.ds(h * DN_DIM, DN_DIM)
                du_ref[rows, cols] = du[h]
                dw_ref[rows, cols] = dw[h]
                dqe_ref[rows, cols] = dqe[h]
                dkd_ref[rows, cols] = dkd[h]
        ds_scr[...] = ds

    wide = jax.ShapeDtypeStruct((S, DN_WIDTH), F32)
    return pl.pallas_call(
        body, grid=(nc // SCAN_CHUNKS,), in_specs=[hc, hc, hc, hc, qks, egls, st, hc],
        out_specs=[hc, hc, hc, hc, qks, egls],
        out_shape=[wide, wide, wide, wide, jax.ShapeDtypeStruct((DN_HEADS, S, CHUNK), F32),
                   jax.ShapeDtypeStruct((DN_HEADS, nc, 1, LANES), F32)],
        scratch_shapes=[pltpu.VMEM((DN_HEADS, DN_DIM, DN_DIM), F32)], name="dn_scan_bwd",
        compiler_params=_params(("arbitrary",)))(u, w, qe, kd, qk, egl, states, do)


def _dn_out_fwd(o, proj, gain, tm=1024):
    S = o.shape[0]

    def body(o_ref, z_ref, g_ref, y_ref):
        y_ref[...] = _dn_out(o_ref[...], z_ref[...], g_ref[...]).astype(BF16)

    hs = pl.BlockSpec((tm, LANES), lambda i, h: (i, h))
    zs = pl.BlockSpec((tm, LANES), lambda i, h: (i, P_Z // LANES + h))
    return pl.pallas_call(
        body, grid=(S // tm, DN_HEADS), in_specs=[hs, zs, _full((1, DN_DIM))], out_specs=hs,
        out_shape=jax.ShapeDtypeStruct((S, DN_WIDTH), BF16), name="dn_out_fwd",
        compiler_params=_params(("parallel", "parallel")))(o, proj, gain)


_ANY = pl.BlockSpec(memory_space=pl.ANY)


def _dn_out_bwd(o, proj, gain, dy, dproj, tm=1024):
    S = o.shape[0]

    def body(o_ref, z_ref, g_ref, dy_ref, _, do_ref, dz_ref, dg_ref):
        _, vjp = jax.vjp(_dn_out, o_ref[...], z_ref[...], g_ref[...])
        do, dz, dg = vjp(dy_ref[...])
        do_ref[...] = do
        dz_ref[...] = dz.astype(BF16)

        @pl.when((pl.program_id(0) == 0) & (pl.program_id(1) == 0))
        def _():
            dg_ref[...] = jnp.zeros_like(dg_ref)

        dg_ref[...] += dg

    hs = pl.BlockSpec((tm, LANES), lambda i, h: (i, h))
    zs = pl.BlockSpec((tm, LANES), lambda i, h: (i, P_Z // LANES + h))
    return pl.pallas_call(
        body, grid=(S // tm, DN_HEADS), in_specs=[hs, zs, _full((1, DN_DIM)), hs, _ANY],
        out_specs=[hs, zs, _full((1, DN_DIM))],
        out_shape=[jax.ShapeDtypeStruct((S, DN_WIDTH), F32), jax.ShapeDtypeStruct(dproj.shape, dproj.dtype),
                   jax.ShapeDtypeStruct((1, DN_DIM), F32)],
        input_output_aliases={4: 1},
        name="dn_out_bwd", compiler_params=_params(("arbitrary", "arbitrary")))(o, proj, gain, dy, dproj)


def _rel_buckets():
    qi = np.arange(BLOCK)[:, None]
    kj = np.arange(2 * BLOCK)[None, :]
    n = np.maximum(BLOCK + qi - kj, 0)
    max_exact = REL_BUCKETS // 2
    nf = np.maximum(n, 1).astype(np.float32)
    large = max_exact + (np.log(nf / np.float32(max_exact)) / np.float32(math.log(REL_MAX_DIST / max_exact))
                         * np.float32(REL_BUCKETS - max_exact)).astype(np.int32)
    large = np.minimum(large, REL_BUCKETS - 1)
    return np.where(n < max_exact, n, large).astype(np.int32)


def _bias_fwd(rel_bias):
    buckets = jnp.asarray(_rel_buckets())

    def body(rb_ref, bk_ref, o_ref):
        bk = bk_ref[...]
        for h in range(SWA_HEADS):
            acc = jnp.zeros((BLOCK, 2 * BLOCK), F32)
            for b in range(REL_BUCKETS):
                acc = jnp.where(bk == b, rb_ref[b, h], acc)
            o_ref[h] = acc

    return pl.pallas_call(
        body, in_specs=[pl.BlockSpec(memory_space=pltpu.SMEM), pl.BlockSpec(memory_space=pltpu.VMEM)],
        out_specs=pl.BlockSpec(memory_space=pltpu.VMEM),
        out_shape=jax.ShapeDtypeStruct((SWA_HEADS, BLOCK, 2 * BLOCK), F32), name="swa_bias_fwd",
        compiler_params=_params())(rel_bias, buckets)


def _bias_bwd(dbias):
    buckets = jnp.asarray(_rel_buckets())

    def body(d_ref, bk_ref, o_ref):
        bk = bk_ref[...]
        lane = lax.broadcasted_iota(jnp.int32, (1, LANES), 1)
        for h in range(SWA_HEADS):
            d = d_ref[h]
            row = jnp.zeros((1, LANES), F32)
            for b in range(REL_BUCKETS):
                part = jnp.sum(jnp.where(bk == b, d, 0.0), axis=1, keepdims=True)
                row = jnp.where(lane == b, jnp.sum(part, axis=0, keepdims=True), row)
            o_ref[h:h + 1, :] = row

    return pl.pallas_call(
        body, in_specs=[pl.BlockSpec(memory_space=pltpu.VMEM), pl.BlockSpec(memory_space=pltpu.VMEM)],
        out_specs=pl.BlockSpec(memory_space=pltpu.VMEM),
        out_shape=jax.ShapeDtypeStruct((SWA_HEADS, LANES), F32), name="swa_bias_bwd",
        compiler_params=_params())(dbias, buckets)


def _swa_mask(n):
    qi = lax.broadcasted_iota(jnp.int32, (BLOCK, 2 * BLOCK), 0)
    kj = lax.broadcasted_iota(jnp.int32, (BLOCK, 2 * BLOCK), 1)
    dist = BLOCK + qi - kj
    return (dist >= 0) & (dist < WINDOW) & ((n > 0) | (kj >= BLOCK))


def _swa_in_specs():
    q = pl.BlockSpec((BLOCK, SWA_WIDTH), lambda n: (n, P_SQ // SWA_WIDTH))
    kc = pl.BlockSpec((BLOCK, SWA_KVW), lambda n: (n, P_SK // SWA_KVW))
    kp = pl.BlockSpec((BLOCK, SWA_KVW), lambda n: (jnp.maximum(n - 1, 0), P_SK // SWA_KVW))
    vc = pl.BlockSpec((BLOCK, SWA_KVW), lambda n: (n, P_SV // SWA_KVW))
    vp = pl.BlockSpec((BLOCK, SWA_KVW), lambda n: (jnp.maximum(n - 1, 0), P_SV // SWA_KVW))
    small = [_full((1, SWA_DIM)), _full((1, SWA_DIM)), _full((1, SWA_HEADS)),
             _full((SWA_HEADS, BLOCK, 2 * BLOCK))]
    return [q, kp, kc, vp, vc] + small


def _swa_load(q_ref, kp_ref, kc_ref, vp_ref, vc_ref, s_ref):
    q = jnp.stack([q_ref[:, pl.ds(h * SWA_DIM, SWA_DIM)] for h in range(SWA_HEADS)])
    kbands, vbands = [], []
    for kv in range(SWA_KV):
        cols = pl.ds(kv * SWA_DIM, SWA_DIM)
        kbands += [jnp.concatenate([kp_ref[:, cols], kc_ref[:, cols]], axis=0)] * SWA_GROUP
        vbands += [jnp.concatenate([vp_ref[:, cols], vc_ref[:, cols]], axis=0)] * SWA_GROUP
    sinks = jnp.stack([s_ref[:, pl.ds(h, 1)] for h in range(SWA_HEADS)])
    return q, jnp.stack(kbands), jnp.stack(vbands), sinks


def _swa_fwd(proj, q_gain, k_gain, sinks, bias):
    S = proj.shape[0]

    def body(q_ref, kp_ref, kc_ref, vp_ref, vc_ref, qg_ref, kg_ref, s_ref, bias_ref, y_ref):
        mask = _swa_mask(pl.program_id(0))
        q, kband, vband, sk = _swa_load(q_ref, kp_ref, kc_ref, vp_ref, vc_ref, s_ref)
        out = _swa_block(q, kband, vband, qg_ref[...], kg_ref[...], sk, bias_ref[...], mask)
        for h in range(SWA_HEADS):
            y_ref[:, pl.ds(h * SWA_DIM, SWA_DIM)] = out[h].astype(BF16)

    return pl.pallas_call(
        body, grid=(S // BLOCK,), in_specs=_swa_in_specs(),
        out_specs=pl.BlockSpec((BLOCK, SWA_WIDTH), lambda n: (n, 0)),
        out_shape=jax.ShapeDtypeStruct((S, SWA_WIDTH), BF16), name="swa_fwd",
        compiler_params=_params(("parallel",)))(proj, proj, proj, proj, proj, q_gain, k_gain, sinks, bias)


def _swa_bwd(proj, q_gain, k_gain, sinks, bias, dy, dproj):
    S = proj.shape[0]

    def body(q_ref, kp_ref, kc_ref, vp_ref, vc_ref, qg_ref, kg_ref, s_ref, bias_ref, dy_ref, _,
             dq_ref, dk_ref, dv_ref, dqg_ref, dkg_ref, ds_ref, dbias_ref):
        n = pl.program_id(0)
        mask = _swa_mask(n)

        @pl.when(n == 0)
        def _():
            for r in (dk_ref, dv_ref, dqg_ref, dkg_ref, ds_ref, dbias_ref):
                r[...] = jnp.zeros_like(r)

        cur = pl.ds(pl.multiple_of(n * BLOCK, BLOCK), BLOCK)
        prev = pl.ds(pl.multiple_of(jnp.maximum(n - 1, 0) * BLOCK, BLOCK), BLOCK)
        q, kband, vband, sk = _swa_load(q_ref, kp_ref, kc_ref, vp_ref, vc_ref, s_ref)
        _, vjp = jax.vjp(lambda q, kb, vb, qg, kg, sk, bs: _swa_block(q, kb, vb, qg, kg, sk, bs, mask),
                         q, kband, vband, qg_ref[...], kg_ref[...], sk, bias_ref[...])
        dy = jnp.stack([dy_ref[:, pl.ds(h * SWA_DIM, SWA_DIM)] for h in range(SWA_HEADS)])
        dq, dkb, dvb, dqg, dkg, dsk, dbs = vjp(dy)
        for h in range(SWA_HEADS):
            dq_ref[:, pl.ds(h * SWA_DIM, SWA_DIM)] = dq[h].astype(BF16)
            ds_ref[:, pl.ds(h, 1)] += dsk[h]
        dbias_ref[...] += dbs
        dqg_ref[...] += dqg
        dkg_ref[...] += dkg
        for kv in range(SWA_KV):
            cols = pl.ds(kv * SWA_DIM, SWA_DIM)
            group = range(kv * SWA_GROUP, (kv + 1) * SWA_GROUP)
            dk_kv = sum(dkb[h] for h in group)
            dv_kv = sum(dvb[h] for h in group)
            dk_ref[cur, cols] += dk_kv[BLOCK:]
            dv_ref[cur, cols] += dv_kv[BLOCK:]

            @pl.when(n > 0)
            def _(cols=cols, dk_kv=dk_kv, dv_kv=dv_kv):
                dk_ref[prev, cols] += dk_kv[:BLOCK]
                dv_ref[prev, cols] += dv_kv[:BLOCK]

    return pl.pallas_call(
        body, grid=(S // BLOCK,),
        in_specs=_swa_in_specs() + [pl.BlockSpec((BLOCK, SWA_WIDTH), lambda n: (n, 0)),
                                    pl.BlockSpec(memory_space=pl.ANY)],
        out_specs=[pl.BlockSpec((BLOCK, SWA_WIDTH), lambda n: (n, P_SQ // SWA_WIDTH)), _full((S, SWA_KVW)),
                   _full((S, SWA_KVW)), _full((1, SWA_DIM)), _full((1, SWA_DIM)), _full((1, SWA_HEADS)),
                   _full((SWA_HEADS, BLOCK, 2 * BLOCK))],
        out_shape=[jax.ShapeDtypeStruct(dproj.shape, dproj.dtype), jax.ShapeDtypeStruct((S, SWA_KVW), F32),
                   jax.ShapeDtypeStruct((S, SWA_KVW), F32), jax.ShapeDtypeStruct((1, SWA_DIM), F32),
                   jax.ShapeDtypeStruct((1, SWA_DIM), F32), jax.ShapeDtypeStruct((1, SWA_HEADS), F32),
                   jax.ShapeDtypeStruct((SWA_HEADS, BLOCK, 2 * BLOCK), F32)],
        input_output_aliases={10: 0},
        name="swa_bwd", compiler_params=_params(("arbitrary",)),
    )(proj, proj, proj, proj, proj, q_gain, k_gain, sinks, bias, dy, dproj)


def _kv_into(dproj, dk, dv, tm=1024):
    S = dk.shape[0]

    def body(dk_ref, dv_ref, _, o_ref):
        o_ref[:, :SWA_KVW] = dk_ref[...].astype(BF16)
        o_ref[:, SWA_KVW:] = dv_ref[...].astype(BF16)

    return pl.pallas_call(
        body, grid=(S // tm,), in_specs=[_row(tm, SWA_KVW), _row(tm, SWA_KVW), pl.BlockSpec(memory_space=pl.ANY)],
        out_specs=_row(tm, 2 * SWA_KVW, P_SK // (2 * SWA_KVW)),
        out_shape=jax.ShapeDtypeStruct(dproj.shape, dproj.dtype), input_output_aliases={2: 0},
        name="swa_kv_into", compiler_params=_params(("parallel",)))(dk, dv, dproj)


def _position():
    return lax.axis_index("x"), lax.axis_index("y"), lax.axis_index("c")


def _all_gather(shards, name="all_gather_weights"):
    na = len(shards)

    def body(*refs):
        x_refs, out_refs = refs[:na], refs[na:2 * na]
        send_sems, recv_sems, local_sems = refs[2 * na:]
        x, y, c = _position()
        me, sibling = (x, y, c), (x, y, 1 - c)
        chips = [(1 - x, y), (x, 1 - y), (1 - x, 1 - y)]

        def copy(a, k, block, to, own=False):
            px, py, pc = block
            slot = out_refs[a].at[4 * px + 2 * py + pc]
            return pltpu.make_async_remote_copy(
                src_ref=x_refs[a] if own else slot, dst_ref=slot, send_sem=send_sems.at[7 * a + k],
                recv_sem=recv_sems.at[7 * a + k], device_id=to, device_id_type=MESH_ID)

        mine = [pltpu.make_async_copy(x_refs[a], out_refs[a].at[4 * x + 2 * y + c], local_sems.at[a])
                for a in range(na)]
        for cp in mine:
            cp.start()
        first = []
        for a in range(na):
            first.append(copy(a, 0, me, sibling, own=True))
            first += [copy(a, 1 + j, me, (*chip, c), own=True) for j, chip in enumerate(chips)]
        for cp in first:
            cp.start()
        passed = []
        for j, chip in enumerate(chips):
            for a in range(na):
                copy(a, 1 + j, (*chip, c), me).wait_recv()
                passed.append(copy(a, 4 + j, (*chip, c), sibling))
                passed[-1].start()
        for a in range(na):
            copy(a, 0, sibling, me).wait_recv()
            for j, chip in enumerate(chips):
                copy(a, 4 + j, (*chip, 1 - c), me).wait_recv()
        for cp in first + passed:
            cp.wait_send()
        for cp in mine:
            cp.wait()

    return pl.pallas_call(
        body, in_specs=[pl.BlockSpec(memory_space=pl.ANY)] * na, out_specs=[pl.BlockSpec(memory_space=pl.ANY)] * na,
        out_shape=[jax.ShapeDtypeStruct((N_DEV,) + s.shape, s.dtype) for s in shards],
        scratch_shapes=[pltpu.SemaphoreType.DMA((7 * na,)), pltpu.SemaphoreType.DMA((7 * na,)),
                        pltpu.SemaphoreType.DMA((na,))],
        name=name)(*shards)


_HBM = pl.BlockSpec(memory_space=pltpu.HBM)
_SEM = pl.BlockSpec(memory_space=pltpu.SEMAPHORE)
_DATAFLOW = pltpu.SideEffectType.DATAFLOW_SIDE_EFFECTING


def _peers(x, y, c):
    out = []
    for k in range(1, N_DEV):
        px, py, pc = x ^ (k >> 2), y ^ ((k >> 1) & 1), c ^ (k & 1)
        out.append(((px, py, pc), 4 * px + 2 * py + pc))
    return out


def _split_copies(src_refs, land_refs, send_sems, recv_sems, scatter):
    x, y, c = _position()
    me = 4 * x + 2 * y + c
    sends, recvs = [], []
    for k, (peer_id, peer) in enumerate(_peers(x, y, c)):
        for a, (src, land) in enumerate(zip(src_refs, land_refs)):
            sems = dict(send_sem=send_sems.at[7 * a + k], recv_sem=recv_sems.at[7 * a + k],
                        device_id=peer_id, device_id_type=MESH_ID)
            mine = src.at[peer] if scatter else src
            sends.append(pltpu.make_async_remote_copy(src_ref=mine, dst_ref=land.at[me], **sems))
            recvs.append(pltpu.make_async_remote_copy(src_ref=mine, dst_ref=land.at[peer], **sems))
    return sends, recvs


def _all_gather_direct(shards, name, after):
    na = len(shards)

    def body(*refs):
        x_refs, out_refs = refs[:na], refs[na + 1:2 * na + 1]
        send_sems, recv_sems, local_sems = refs[2 * na + 1:]
        x, y, c = _position()
        me = 4 * x + 2 * y + c
        local = [pltpu.make_async_copy(x_refs[a], out_refs[a].at[me], local_sems.at[a]) for a in range(na)]
        sends, recvs = _split_copies(x_refs, out_refs, send_sems, recv_sems, False)
        for cp in local + sends:
            cp.start()
        for cp in recvs:
            cp.wait_recv()
        for cp in sends:
            cp.wait_send()
        for cp in local:
            cp.wait()

    return pl.pallas_call(
        body, in_specs=[pl.BlockSpec(memory_space=pl.ANY)] * (na + 1),
        out_specs=[pl.BlockSpec(memory_space=pl.ANY)] * na,
        out_shape=[jax.ShapeDtypeStruct((N_DEV,) + s.shape, s.dtype) for s in shards],
        scratch_shapes=[pltpu.SemaphoreType.DMA((7 * na,)), pltpu.SemaphoreType.DMA((7 * na,)),
                        pltpu.SemaphoreType.DMA((na,))],
        name=name)(*shards, after)


def _exchange_start(srcs, scatter, name, after=None):
    na = len(srcs)
    lands = [lax.empty(s.shape if scatter else (N_DEV,) + s.shape, s.dtype) for s in srcs]
    extra = [] if after is None else [after]

    def body(*refs):
        src_refs, land_refs = refs[:na], refs[na:2 * na]
        send_sems, recv_sems = refs[2 * na + len(extra)], refs[2 * na + len(extra) + 1]
        token = refs[-1]
        sends, _ = _split_copies(src_refs, land_refs, send_sems, recv_sems, scatter)
        for cp in sends:
            cp.start()
        token[...] = jnp.zeros_like(token)

    hbm = lambda a: pltpu.HBM(a.shape, a.dtype)
    out = pl.pallas_call(
        body, name=name,
        out_shape=(pltpu.SemaphoreType.DMA((7 * na,)), pltpu.SemaphoreType.DMA((7 * na,)),
                   *[hbm(s) for s in srcs], *[hbm(l) for l in lands], jax.ShapeDtypeStruct((8, LANES), F32)),
        in_specs=[_HBM] * (2 * na) + [pl.BlockSpec(memory_space=pl.ANY)] * len(extra),
        out_specs=(_SEM, _SEM, *[_HBM] * (2 * na), pl.BlockSpec(memory_space=pltpu.VMEM)),
        input_output_aliases={i: 2 + i for i in range(2 * na)},
        compiler_params=pltpu.CompilerParams(has_side_effects=_DATAFLOW),
    )(*[pltpu.with_memory_space_constraint(s, pltpu.HBM) for s in srcs],
      *[pltpu.with_memory_space_constraint(l, pltpu.HBM) for l in lands], *extra)
    return (out[0], out[1], list(out[2:2 + na]), list(out[2 + na:2 + 2 * na])), out[-1]


def _exchange_wait(handle, after, scatter, name):
    send_sems, recv_sems, srcs, lands = handle
    na = len(srcs)

    def body(*refs):
        src_refs, land_refs = refs[:na], refs[na:2 * na]
        s_sems, r_sems = refs[2 * na], refs[2 * na + 1]
        sends, recvs = _split_copies(src_refs, land_refs, s_sems, r_sems, scatter)
        for cp in sends:
            cp.wait_send()
        for cp in recvs:
            cp.wait_recv()

    hbm = lambda a: pltpu.HBM(a.shape, a.dtype)
    out = pl.pallas_call(
        body, name=name, out_shape=(*[hbm(s) for s in srcs], *[hbm(l) for l in lands]),
        in_specs=[_HBM] * (2 * na) + [_SEM, _SEM, pl.BlockSpec(memory_space=pl.ANY)],
        out_specs=tuple([_HBM] * (2 * na)), input_output_aliases={i: i for i in range(2 * na)},
        compiler_params=pltpu.CompilerParams(has_side_effects=_DATAFLOW),
    )(*srcs, *lands, send_sems, recv_sems, after)
    return list(out[:na]), list(out[na:])


def _own_slot(landed, own):
    me = 4 * lax.axis_index("x") + 2 * lax.axis_index("y") + lax.axis_index("c")
    return lax.dynamic_update_slice_in_dim(landed, own[None], me, axis=0)


def _adam_update(parts, w, m, v, name, tr=256):
    _, r, c = w.shape
    tr = _pick_rows(r, tr)
    cp = parts.shape[2]

    def body(p_ref, w_ref, m_ref, v_ref, g_ref, d_ref, nm_ref, nv_ref):
        g = p_ref[0, :, pl.ds(0, c)].astype(F32)
        for i in range(1, N_DEV):
            g = g + p_ref[i, :, pl.ds(0, c)].astype(F32)
        delta, nm, nv = _adamw(w_ref[0], g, m_ref[0], v_ref[0])
        g_ref[0] = g
        d_ref[0] = delta
        nm_ref[0] = nm
        nv_ref[0] = nv

    rs = pl.BlockSpec((1, tr, c), lambda i: (0, i, 0))
    return pl.pallas_call(
        body, grid=(r // tr,), in_specs=[pl.BlockSpec((N_DEV, tr, cp), lambda i: (0, i, 0)), rs, rs, rs],
        out_specs=[rs] * 4, out_shape=[jax.ShapeDtypeStruct((1, r, c), F32)] * 4, name=name,
        compiler_params=_params(("parallel",)))(parts, w, m, v)


def _pick_rows(rows, target):
    if rows <= target:
        return rows
    t = target
    while t >= 16:
        if rows % t == 0:
            return t
        t -= 16
    return rows


BIG = ("w_in", "w_branch_dn", "w_branch_swa", "w_out", "w_gate", "w_up", "w_down")
IN_SHARD, IN_WIRE = D_IN // N_DEV, 640
FF_SHARD, FF_WIRE = D_FF // N_DEV, 384
D_FFP = N_DEV * FF_WIRE
BIG_SHAPES = {"w_in": ((D_MODEL, IN_SHARD), (D_MODEL, IN_WIRE)),
              "w_branch_dn": ((DN_WIDTH, LANES), (DN_WIDTH, LANES)),
              "w_branch_swa": ((SWA_WIDTH, LANES), (SWA_WIDTH, LANES)),
              "w_out": ((LANES, D_MODEL), (LANES, D_MODEL)),
              "w_gate": ((D_MODEL, FF_SHARD), (D_MODEL, FF_WIRE)),
              "w_up": ((D_MODEL, FF_SHARD), (D_MODEL, FF_WIRE)),
              "w_down": ((FF_SHARD, D_MODEL), (FF_WIRE, D_MODEL))}
CONV_SHARD, CONV_WIRE = (DN_CONV, DN_QKV // N_DEV), (8, 256)


def _pad_to(a, shape):
    return jnp.pad(a, [(0, t - s) for s, t in zip(a.shape, shape)])


_IN_SEGS = ((R_GATE, 2048, P_GATE), (R_QKV, DN_QKV, P_QKV), (R_Z, DN_WIDTH, P_Z), (R_SQ, SWA_WIDTH, P_SQ),
            (R_SK, SWA_KVW, P_SK), (R_SV, SWA_KVW, P_SV), (R_B, 8, P_BA))


def _w_in_from_blocks(blocks):
    parts = []
    for rs, n, _ in _IN_SEGS:
        for dev in range(N_DEV):
            lo, hi = max(rs, IN_SHARD * dev), min(rs + n, IN_SHARD * (dev + 1))
            if lo < hi:
                parts.append(blocks[dev, :, lo - IN_SHARD * dev:hi - IN_SHARD * dev])
    parts.append(jnp.zeros((blocks.shape[1], P_WIDTH - P_BA - 8), blocks.dtype))
    return jnp.concatenate(parts, axis=1)


def _w_in_to_blocks(g):
    out = []
    for dev in range(N_DEV):
        parts = []
        for rs, n, ps in sorted(_IN_SEGS):
            lo, hi = max(rs, IN_SHARD * dev), min(rs + n, IN_SHARD * (dev + 1))
            if lo < hi:
                parts.append(g[:, ps + lo - rs:ps + hi - rs])
        parts.append(jnp.zeros((g.shape[0], IN_WIRE - IN_SHARD), g.dtype))
        out.append(jnp.concatenate(parts, axis=1))
    return jnp.stack(out)


SMALL = {"attn_norm": (0, (1, D_MODEL)), "ffn_norm": (1, (1, D_MODEL)), "dn_out_norm": (2, (1, DN_DIM)),
         "swa_q_norm": (3, (1, SWA_DIM)), "swa_k_norm": (4, (1, SWA_DIM)), "dn_a_log": (5, (1, DN_HEADS)),
         "dn_dt_bias": (6, (1, DN_HEADS)), "swa_sinks": (7, (1, SWA_HEADS)), "rel_bias": (8, (REL_BUCKETS, SWA_HEADS))}
SMALL_SHEET = (48, D_MODEL)


LOSS_ROW = 40


def _small_pack(grads, loss_local):
    names = list(SMALL)

    def body(*refs):
        o_ref = refs[-1]
        o_ref[...] = jnp.zeros_like(o_ref)
        for n, ref in zip(names, refs):
            r0, (nr, nc) = SMALL[n]
            o_ref[r0:r0 + nr, 0:nc] = ref[...]
        o_ref[LOSS_ROW:LOSS_ROW + 1, 0:1] = refs[len(names)][...]

    return pl.pallas_call(
        body, in_specs=[pl.BlockSpec(memory_space=pltpu.VMEM)] * (len(names) + 1),
        out_specs=pl.BlockSpec(memory_space=pltpu.VMEM), out_shape=jax.ShapeDtypeStruct(SMALL_SHEET, F32),
        name="small_pack", compiler_params=_params())(*[grads[n].reshape(SMALL[n][1]) for n in names], loss_local)


def _small_update(sheets, w, m, v):
    names = list(SMALL)
    k = len(names)

    def body(*refs):
        p_ref = refs[0]
        ins, outs = refs[1:1 + 3 * k], refs[1 + 3 * k:]
        loss = p_ref[0, LOSS_ROW:LOSS_ROW + 1, 0:1]
        for i in range(1, N_DEV):
            loss = loss + p_ref[i, LOSS_ROW:LOSS_ROW + 1, 0:1]
        outs[4 * k][...] = loss
        for t, n in enumerate(names):
            r0, (nr, nc) = SMALL[n]
            g = p_ref[0, r0:r0 + nr, 0:nc]
            for i in range(1, N_DEV):
                g = g + p_ref[i, r0:r0 + nr, 0:nc]
            delta, nm, nv = _adamw(ins[t][...], g, ins[k + t][...], ins[2 * k + t][...])
            for kind, val in enumerate((g, delta, nm, nv)):
                outs[kind * k + t][...] = val

    shapes = [jax.ShapeDtypeStruct(SMALL[n][1], F32) for n in names]
    vm = pl.BlockSpec(memory_space=pltpu.VMEM)
    res = pl.pallas_call(
        body, in_specs=[vm] * (1 + 3 * k), out_specs=[vm] * (4 * k + 1),
        out_shape=shapes * 4 + [jax.ShapeDtypeStruct((1, 1), F32)], name="adam_small", compiler_params=_params(),
    )(sheets, *[d[n].reshape(SMALL[n][1]) for d in (w, m, v) for n in names])
    return {n: tuple(res[kind * k + t] for kind in range(4)) for t, n in enumerate(names)}, res[4 * k]


def kernel(x, attn_norm, w_in, dn_conv, dn_a_log, dn_dt_bias, dn_out_norm, swa_q_norm, swa_k_norm, swa_sinks, rel_bias, w_branch_dn, w_branch_swa, w_out, ffn_norm, w_gate, w_up, w_down, loss_target, m_attn_norm, m_w_in, m_dn_conv, m_dn_a_log, m_dn_dt_bias, m_dn_out_norm, m_swa_q_norm, m_swa_k_norm, m_swa_sinks, m_rel_bias, m_w_branch_dn, m_w_branch_swa, m_w_out, m_ffn_norm, m_w_gate, m_w_up, m_w_down, v_attn_norm, v_w_in, v_dn_conv, v_dn_a_log, v_dn_dt_bias, v_dn_out_norm, v_swa_q_norm, v_swa_k_norm, v_swa_sinks, v_rel_bias, v_w_branch_dn, v_w_branch_swa, v_w_out, v_ffn_norm, v_w_gate, v_w_up, v_w_down):
    args = dict(locals())
    S = x.shape[1]
    xs = x.reshape(S, D_MODEL)
    target = loss_target.reshape(S, D_MODEL)

    w_loc = {n: args[n].reshape(BIG_SHAPES[n][0]) for n in BIG}
    conv_loc = dn_conv.reshape(CONV_SHARD)
    wire = {n: _pad_to(w_loc[n], BIG_SHAPES[n][1]).astype(BF16) for n in BIG}
    first = _all_gather([wire["w_in"], _pad_to(conv_loc, CONV_WIRE)])
    later = [n for n in BIG if n != "w_in"]
    rest_handle, rest_token = _exchange_start([wire[n] for n in later], False, "gather_rest_start", after=first[1])
    w_pad = _w_in_from_blocks(first[0])
    conv_w = jnp.concatenate([first[1][d, :DN_CONV, :CONV_SHARD[1]] for d in range(N_DEV)], axis=1)

    h = _norm_fwd(xs, attn_norm + rest_token[0, 0], "norm1_fwd")
    proj = _mm([(h, w_pad)], "nn", F32, "mm_in", 1024, 1664, j_outer=True)
    qkvn = _dn_conv_fwd(proj, conv_w)
    beta, g = _dn_gate_fwd(proj, dn_a_log, dn_dt_bias)
    u, w, qe, kd, qk, egl, tinv = _dn_prep_fwd(qkvn, g, beta)
    o, states = _dn_scan_fwd(u, w, qe, kd, qk, egl)
    y_dn = _dn_out_fwd(o, proj, dn_out_norm)
    bias = _bias_fwd(rel_bias)
    y_swa = _swa_fwd(proj, swa_q_norm, swa_k_norm, swa_sinks, bias)
    rest_src, rest_land = _exchange_wait(rest_handle, y_swa, False, "gather_rest_wait")
    G = {n: _own_slot(land, src) for n, src, land in zip(later, rest_src, rest_land)}
    w_bdn, w_bswa, w_g, w_u = G["w_branch_dn"], G["w_branch_swa"], G["w_gate"], G["w_up"]
    w_o = G["w_out"].reshape(D_MODEL, D_MODEL)
    w_d = G["w_down"].reshape(D_FFP, D_MODEL)
    gates = [(proj, P_GATE // 512), (proj, (P_GATE + D_MODEL) // 512)]
    a_dn, a_swa, merged = _mm_fused(
        [(y_dn, w_bdn), (y_swa, w_bswa)], "nn", "mm_branch_merge", 1024, 512,
        lambda p, e: (p[0], p[1], _merge(e[0], e[1], p[0], p[1])), gates, (F32, F32, BF16), b_blocks=True)

    def resid_norm(p, e):
        x1 = e[0] + p[0]
        return x1, _rms(x1, e[1])

    x1, h2 = _mm_fused([(merged, w_o)], "nn", "mm_out_norm", 512, D_MODEL, resid_norm,
                       [(xs, 0), (ffn_norm, None)], (F32, BF16))
    gate, up, act = _mm_fused([(h2, w_g), (h2, w_u)], "nn", "mm_gate_up_act", 1024, 768,
                              lambda p, e: (p[0], p[1], _act(p[0], p[1])), [], (F32, F32, BF16),
                              j_outer=True, b_blocks=True)

    def loss_head(p, e):
        diff = e[0] + p[0] - e[1]
        dy = diff * (1.0 / D_MODEL)
        part = jnp.sum(jnp.mean(diff * diff, axis=-1, keepdims=True), axis=0, keepdims=True) * 0.5
        return dy, dy, part

    dy, dy_b, loss_local = _mm_fused([(act, w_d)], "nn", "mm_down_loss", 512, D_MODEL, loss_head,
                                     [(x1, 0), (target, 0)], (F32, BF16), sum_shape=(1, 1))

    def act_bwd(p, e):
        _, vjp = jax.vjp(_act, e[0], e[1])
        return vjp(p[0])

    dgate, dup = _mm_fused([(dy_b, w_d)], "nt", "mm_dact_act", 1024, 768, act_bwd, [(gate, 0), (up, 0)],
                           (BF16, BF16), j_outer=True)
    g_w_down = _mm([(act, dy_b)], "tn", BF16, "mm_dw_down", 768, D_MODEL, j_outer=True)
    g_w_down = g_w_down.reshape(N_DEV, FF_WIRE, D_MODEL)
    g_w_gate = _mm([(h2, dgate)], "tn", BF16, "mm_dw_gate", D_MODEL, 768, out_blocks=True)
    g_w_up = _mm([(h2, dup)], "tn", BF16, "mm_dw_up", D_MODEL, 768, out_blocks=True)
    ffn_handle, ffn_token = _exchange_start([g_w_down, g_w_gate, g_w_up], True, "scatter_ffn_start")

    def norm_bwd(p, e):
        _, vjp = jax.vjp(_rms, e[0], e[2])
        dx, dgain = vjp(sum(p))
        dx = dx + e[1]
        return dx, dx, dgain

    dx1, dx1_b, g_ffn_norm = _mm_fused(
        [(dgate, w_g), (dup, w_u)], "nt", "mm_dh2_norm", 256, D_MODEL, norm_bwd,
        [(x1, 0), (dy, 0), (ffn_norm + ffn_token[0, 0], None)], (F32, BF16), b_blocks=True, sum_shape=(1, D_MODEL))
    def merge_bwd(p, e):
        _, vjp = jax.vjp(_merge, *e)
        dg0, dg1, da_dn, da_swa = vjp(p[0])
        return jnp.concatenate([dg0, dg1], axis=1), da_dn, da_swa

    dproj, da_dn, da_swa = _mm_fused(
        [(dx1_b, w_o)], "nt", "mm_dmerged_merge", 512, D_MODEL, merge_bwd,
        [(proj, P_GATE // D_MODEL), (proj, P_GATE // D_MODEL + 1), (a_dn, 0), (a_swa, 0)], (BF16,) * 3,
        wide_first=(P_WIDTH, 2 * D_MODEL))
    g_w_out = _mm([(merged, dx1_b)], "tn", BF16, "mm_dw_out", 512, D_MODEL, j_outer=True)
    g_w_out = g_w_out.reshape(N_DEV, LANES, D_MODEL)
    dy_dn = _mm([(da_dn, w_bdn)], "nt", F32, "mm_dy_dn", 1024, DN_WIDTH, b_blocks=True)
    dy_swa = _mm([(da_swa, w_bswa)], "nt", F32, "mm_dy_swa", 1024, SWA_WIDTH, b_blocks=True)
    g_w_bdn = _mm([(y_dn, da_dn)], "tn", BF16, "mm_dw_branch_dn", DN_WIDTH, 512, out_blocks=True)
    g_w_bswa = _mm([(y_swa, da_swa)], "tn", BF16, "mm_dw_branch_swa", SWA_WIDTH, 512, out_blocks=True)
    dproj, dsk, dsv, g_q_norm, g_k_norm, g_sinks, dbias = _swa_bwd(proj, swa_q_norm, swa_k_norm, swa_sinks, bias,
                                                                   dy_swa, dproj)
    dproj = _kv_into(dproj, dsk, dsv)
    g_rel_bias = _bias_bwd(dbias)[:, :REL_BUCKETS].T
    mix_handle, mix_token = _exchange_start([g_w_out, g_w_bdn, g_w_bswa], True, "scatter_mix_start")
    do, dproj, g_out_norm = _dn_out_bwd(o, proj, dn_out_norm + mix_token[0, 0], dy_dn, dproj)
    du, dw, dqe, dkd, dqk, degl = _dn_scan_bwd(u, w, qe, kd, qk, egl, states, do)
    dqkvn, dgd, dbeta = _dn_prep_bwd(qkvn, g, beta, tinv, du, dw, dqe, dkd, dqk, degl)
    dproj, dal, ddt = _dn_gate_bwd(proj, dn_a_log, dn_dt_bias, dbeta, dgd, dproj)
    g_a_log = dal.reshape(DN_HEADS, DN_DIM).sum(axis=1)
    g_dt_bias = ddt.reshape(DN_HEADS, DN_DIM).sum(axis=1)
    dproj, g_conv = _dn_conv_bwd(proj, conv_w, dqkvn, dproj)
    g_w_in = _w_in_to_blocks(_mm([(h, dproj)], "tn", BF16, "mm_dw_in", 512, 1664, j_outer=True))
    in_handle, in_token = _exchange_start([g_w_in], True, "scatter_in_start")
    dx, g_attn_norm = _mm_fused(
        [(dproj, w_pad)], "nt", "mm_dh_norm", 512, D_MODEL, lambda p, e: norm_bwd(p, e)[1:],
        [(xs, 0), (dx1, 0), (attn_norm + in_token[0, 0], None)], (F32,), sum_shape=(1, D_MODEL))

    g_small = {"attn_norm": g_attn_norm, "ffn_norm": g_ffn_norm, "rel_bias": g_rel_bias, "dn_out_norm": g_out_norm,
               "swa_q_norm": g_q_norm, "swa_k_norm": g_k_norm, "dn_a_log": g_a_log, "dn_dt_bias": g_dt_bias,
               "swa_sinks": g_sinks}
    me = 4 * lax.axis_index("x") + 2 * lax.axis_index("y") + lax.axis_index("c")
    outs = {}

    def finish(handle, group, name, after):
        srcs, lands = _exchange_wait(handle, after, True, name)
        for n, src, land in zip(group, srcs, lands):
            parts = _own_slot(land, lax.dynamic_index_in_dim(src, me, 0, keepdims=False))
            outs[n] = _adam_update(parts, args[n], args["m_" + n], args["v_" + n], "adam_" + n)

    finish(ffn_handle, ("w_down", "w_gate", "w_up"), "scatter_ffn_wait", dx)
    finish(mix_handle, ("w_out", "w_branch_dn", "w_branch_swa"), "scatter_mix_wait", dx)
    sheets, conv_all = _all_gather_direct([_small_pack(g_small, loss_local), _pad_to(g_conv, (8, DN_QKV))],
                                          "all_gather_small",
                                          after=outs["w_up"][0])
    finish(in_handle, ("w_in",), "scatter_in_wait", sheets)
    conv_parts = lax.dynamic_slice(conv_all, (0, 0, me * CONV_SHARD[1]), (N_DEV,) + CONV_SHARD)
    outs["dn_conv"] = _adam_update(conv_parts, dn_conv, m_dn_conv, v_dn_conv, "adam_dn_conv")
    small_outs, loss = _small_update(sheets, {n: args[n] for n in SMALL}, {n: args["m_" + n] for n in SMALL},
                                     {n: args["v_" + n] for n in SMALL})
    outs.update(small_outs)

    names = ("attn_norm", "w_in", "dn_conv", "dn_a_log", "dn_dt_bias", "dn_out_norm", "swa_q_norm", "swa_k_norm",
             "swa_sinks", "rel_bias", "w_branch_dn", "w_branch_swa", "w_out", "ffn_norm", "w_gate", "w_up", "w_down")
    results = []
    for kind in range(4):
        results += [outs[n][kind].reshape(args[n].shape) for n in names]

    return (loss.reshape(()), dx.reshape(x.shape), *results)
```

```python
import math

import numpy as np
import jax
import jax.numpy as jnp
from jax import lax
from jax.experimental import pallas as pl
from jax.experimental.pallas import tpu as pltpu

F32 = jnp.float32
BF16 = jnp.bfloat16
HI = lax.Precision.HIGHEST

D_MODEL = 1024
DN_HEADS = 4
DN_DIM = 128
DN_WIDTH = 512
DN_QKV = 1536
DN_CONV = 4
CHUNK = 64
SWA_HEADS = 8
SWA_KV = 2
SWA_GROUP = 4
SWA_DIM = 64
SWA_WIDTH = 512
SWA_KVW = 128
WINDOW = 128
BLOCK = 128
REL_BUCKETS = 32
REL_MAX_DIST = 128
D_FF = 2816
D_IN = 4872
EPS = 1e-6
N_DEV = 8

ADAM_LR = 0.001
ADAM_B1 = 0.9
ADAM_B2 = 0.999
ADAM_EPS = 1e-08
ADAM_WD = 0.01
ADAM_STEP = 10

P_GATE, P_QKV, P_Z, P_SQ, P_SK, P_SV, P_BA = 0, 2048, 3584, 4096, 4608, 4736, 4864
P_WIDTH = 4992
R_QKV, R_Z, R_B, R_A, R_SQ, R_SK, R_SV, R_GATE = 0, 1536, 2048, 2052, 2056, 2568, 2696, 2824

VMEM_LIMIT = 56 * 1024 * 1024
LANES = 128
MESH_ID = pl.DeviceIdType.MESH


def _params(sem=None):
    return pltpu.CompilerParams(dimension_semantics=sem, vmem_limit_bytes=VMEM_LIMIT)


def _pick(dim, target):
    if dim <= target:
        return dim
    t = target - target % LANES
    while t >= LANES:
        if dim % t == 0:
            return t
        t -= LANES
    return dim


_DIMS = {"nn": (((1,), (0,)), ((), ())), "nt": (((1,), (1,)), ((), ())), "tn": (((0,), (0,)), ((), ()))}


def _tile_product(a_ref, b_ref, mode, b_blocks):
    a = a_ref[...].astype(BF16)
    b = jnp.concatenate([b_ref[d] for d in range(b_ref.shape[0])], axis=1) if b_blocks else b_ref[...]
    return lax.dot_general(a, b.astype(BF16), _DIMS[mode], preferred_element_type=F32)


def _mm(pairs, mode, out_dtype, name, bm, bn, j_outer=False, b_blocks=False, out_blocks=False):
    a0, b0 = pairs[0]
    cb = b0.shape[2] if b_blocks else None
    b_shape = (b0.shape[1], N_DEV * cb) if b_blocks else b0.shape
    if mode == "nn":
        (M, K), (K2, N) = a0.shape, b_shape
    elif mode == "nt":
        (M, K), (N, K2) = a0.shape, b_shape
    else:
        (K, M), (K2, N) = a0.shape, b_shape
    bm, bn = min(bm, M), min(bn, N)
    assert K == K2 and M % bm == 0 and N % bn == 0, (name, a0.shape, b0.shape, bm, bn)
    co = N // N_DEV
    assert not out_blocks or bn % co == 0
    dims = _DIMS[mode]
    n = len(pairs)

    def body(*refs):
        o_ref = refs[2 * n]
        acc = None
        for t in range(n):
            p = _tile_product(refs[2 * t], refs[2 * t + 1], mode, b_blocks)
            acc = p if acc is None else acc + p
        if out_blocks:
            for d in range(bn // co):
                o_ref[d] = acc[:, d * co:(d + 1) * co].astype(out_dtype)
        else:
            o_ref[...] = acc.astype(out_dtype)

    def ij(f):
        return (lambda j, i: f(i, j)) if j_outer else f

    a_spec = pl.BlockSpec((K, bm), ij(lambda i, j: (0, i))) if mode == "tn" else pl.BlockSpec((bm, K), ij(lambda i, j: (i, 0)))
    if b_blocks and mode == "nt":
        b_spec = pl.BlockSpec((N_DEV, bn, cb), ij(lambda i, j: (0, j, 0)))
    elif b_blocks:
        b_spec = pl.BlockSpec((bn // cb, K, cb), ij(lambda i, j: (j, 0, 0)))
    elif mode == "nt":
        b_spec = pl.BlockSpec((bn, K), ij(lambda i, j: (j, 0)))
    else:
        b_spec = pl.BlockSpec((K, bn), ij(lambda i, j: (0, j)))
    if out_blocks:
        out_spec = pl.BlockSpec((bn // co, bm, co), ij(lambda i, j: (j, i, 0)))
        out_shape = jax.ShapeDtypeStruct((N_DEV, M, co), out_dtype)
    else:
        out_spec = pl.BlockSpec((bm, bn), ij(lambda i, j: (i, j)))
        out_shape = jax.ShapeDtypeStruct((M, N), out_dtype)
    grid = (N // bn, M // bm) if j_outer else (M // bm, N // bn)
    return pl.pallas_call(
        body, grid=grid, in_specs=[a_spec, b_spec] * n, out_specs=out_spec, out_shape=out_shape, name=name,
        compiler_params=_params(("parallel", "parallel")),
    )(*[x for pair in pairs for x in pair])


def _mm_fused(pairs, mode, name, bm, bn, epilogue, extras, out_dtypes, j_outer=False, b_blocks=False,
              sum_shape=None, wide_first=None):
    a0, b0 = pairs[0]
    cb = b0.shape[2] if b_blocks else None
    b_shape = (b0.shape[1], N_DEV * cb) if b_blocks else b0.shape
    if mode == "nn":
        (M, K), (K2, N) = a0.shape, b_shape
    else:
        (M, K), (N, K2) = a0.shape, b_shape
    bm, bn = min(bm, M), min(bn, N)
    assert mode in ("nn", "nt") and K == K2 and M % bm == 0 and N % bn == 0, (name, a0.shape, b0.shape)
    dims = _DIMS[mode]
    n, ne, no = len(pairs), len(extras), len(out_dtypes)

    def body(*refs):
        prods = [_tile_product(refs[2 * t], refs[2 * t + 1], mode, b_blocks) for t in range(n)]
        results = epilogue(prods, [r[...] for r in refs[2 * n:2 * n + ne]])
        out_refs = refs[2 * n + ne:]
        for o_ref, val, dt in zip(out_refs, results, out_dtypes):
            o_ref[...] = val.astype(dt)
        if sum_shape is not None:
            s_ref = out_refs[no]

            @pl.when((pl.program_id(0) == 0) & (pl.program_id(1) == 0))
            def _():
                s_ref[...] = jnp.zeros_like(s_ref)

            s_ref[...] += results[no]

    def ij(f):
        return (lambda j, i: f(i, j)) if j_outer else f

    a_spec = pl.BlockSpec((bm, K), ij(lambda i, j: (i, 0)))
    once = dict(pipeline_mode=pl.Buffered(1)) if bn == N else {}
    if b_blocks and mode == "nt":
        b_spec = pl.BlockSpec((N_DEV, bn, cb), ij(lambda i, j: (0, j, 0)), **once)
    elif b_blocks:
        b_spec = pl.BlockSpec((bn // cb, K, cb), ij(lambda i, j: (j, 0, 0)), **once)
    elif mode == "nt":
        b_spec = pl.BlockSpec((bn, K), ij(lambda i, j: (j, 0)), **once)
    else:
        b_spec = pl.BlockSpec((K, bn), ij(lambda i, j: (0, j)), **once)
    e_specs = [pl.BlockSpec((1, bn), ij(lambda i, j: (0, j))) if first is None
               else pl.BlockSpec((bm, bn), ij(lambda i, j, first=first: (i, first + j))) for _, first in extras]
    tile = pl.BlockSpec((bm, bn), ij(lambda i, j: (i, j)))
    out_specs = [tile] * no
    out_shape = [jax.ShapeDtypeStruct((M, N), dt) for dt in out_dtypes]
    if wide_first is not None:
        assert bn == N
        out_specs[0] = pl.BlockSpec((bm, wide_first[1]), ij(lambda i, j: (i, 0)))
        out_shape[0] = jax.ShapeDtypeStruct((M, wide_first[0]), out_dtypes[0])
    if sum_shape is not None:
        assert sum_shape[1] in (1, bn) and (sum_shape[1] == 1 or bn == N)
        out_specs.append(_full(sum_shape))
        out_shape.append(jax.ShapeDtypeStruct(sum_shape, F32))
    grid = (N // bn, M // bm) if j_outer else (M // bm, N // bn)
    sem = ("arbitrary", "arbitrary") if sum_shape is not None else ("parallel", "parallel")
    return pl.pallas_call(
        body, grid=grid, in_specs=[a_spec, b_spec] * n + e_specs, out_specs=out_specs, out_shape=out_shape,
        name=name, compiler_params=_params(sem),
    )(*[x for pair in pairs for x in pair], *[arr for arr, _ in extras])


def _rms(x, gain):
    return x * lax.rsqrt(jnp.mean(x * x, axis=-1, keepdims=True) + EPS) * gain


def _silu(x):
    return x * jax.nn.sigmoid(x)


def _act(g, u):
    return _silu(g) * u


def _merge(g0, g1, a_dn, a_swa):
    return jax.nn.sigmoid(g0) * a_dn + jax.nn.sigmoid(g1) * a_swa


def _dn_post(c, is_v, q_scale):
    a = _silu(c)
    rs = lax.rsqrt(jnp.sum(a * a, axis=-1, keepdims=True) + EPS) * q_scale
    return a * jnp.where(is_v, 1.0, rs)


def _dn_out(o, z, gain):
    return _rms(o, gain) * _silu(z)


def _dot(a, b, dims=_DIMS["nn"], hi=False):
    if a.ndim == 3 or b.ndim == 3:
        batch = a.shape[0] if a.ndim == 3 else b.shape[0]
        a = a if a.ndim == 3 else jnp.broadcast_to(a, (batch,) + a.shape)
        b = b if b.ndim == 3 else jnp.broadcast_to(b, (batch,) + b.shape)
        ((ca,), (cb,)), _ = dims
        dims = (((ca + 1,), (cb + 1,)), ((0,), (0,)))
    if hi:
        return lax.dot_general(a, b, dims, precision=HI, preferred_element_type=F32)
    return lax.dot_general(a.astype(BF16), b.astype(BF16), dims, preferred_element_type=F32)


def _pieces(x):
    hi = x.astype(BF16)
    r1 = x - hi.astype(F32)
    mid = r1.astype(BF16)
    return hi, mid, (r1 - mid.astype(F32)).astype(BF16)


def _sel_left_impl(m, x):
    mb = m.astype(BF16)
    hi, mid, lo = _pieces(x)
    return _dot(mb, hi) + (_dot(mb, mid) + _dot(mb, lo))


@jax.custom_vjp
def _sel_left(m, mt, x):
    return _sel_left_impl(m, x)


_sel_left.defvjp(lambda m, mt, x: (_sel_left_impl(m, x), (m, mt)),
                 lambda res, ct: (jnp.zeros_like(res[0]), jnp.zeros_like(res[1]), _sel_left_impl(res[1], ct)))


def _sel_right_impl(x, s):
    sb = s.astype(BF16)
    hi, mid, lo = _pieces(x)
    return _dot(hi, sb) + (_dot(mid, sb) + _dot(lo, sb))


@jax.custom_vjp
def _sel_right(x, s, st):
    return _sel_right_impl(x, s)


_sel_right.defvjp(lambda x, s, st: (_sel_right_impl(x, s), (s, st)),
                  lambda res, ct: (_sel_right_impl(ct, res[1]), jnp.zeros_like(res[0]), jnp.zeros_like(res[1])))


def _sel_nt_impl(s, x):
    sb = s.astype(BF16)
    hi, mid, lo = _pieces(x)
    return _dot(sb, hi, _DIMS["nt"]) + (_dot(sb, mid, _DIMS["nt"]) + _dot(sb, lo, _DIMS["nt"]))


def _sel_tn_impl(x, s):
    sb = s.astype(BF16)
    hi, mid, lo = _pieces(x)
    return _dot(hi, sb, _DIMS["tn"]) + (_dot(mid, sb, _DIMS["tn"]) + _dot(lo, sb, _DIMS["tn"]))


@jax.custom_vjp
def _sel_nt(s, x):
    return _sel_nt_impl(s, x)


_sel_nt.defvjp(lambda s, x: (_sel_nt_impl(s, x), s),
               lambda s, ct: (jnp.zeros_like(s), _sel_tn_impl(ct, s)))


def _dot3_impl(a, b):
    a_hi, a_lo, _ = _pieces(a)
    b_hi, b_lo, _ = _pieces(b)
    return _dot(a_hi, b_hi) + (_dot(a_hi, b_lo) + _dot(a_lo, b_hi))


@jax.custom_vjp
def _dot3(a, b):
    return _dot3_impl(a, b)


_dot3.defvjp(lambda a, b: (_dot3_impl(a, b), (a, b)),
             lambda res, ct: (_dot(ct, res[1], _DIMS["nt"]), _dot(res[0], ct, _DIMS["tn"])))


def _inv_impl(a, eye, strict):
    t = eye - a
    p = _dot(a, a)
    for level in range(5):
        t = t + _dot(t, p)
        if level < 4:
            p = _dot(p, p)
    t = t + _dot(t, eye - t - _dot3_impl(a, t))
    return jnp.where(strict > 0.5, t, eye)


@jax.custom_vjp
def _inv_given(a, t):
    return t.astype(F32)


_inv_given.defvjp(lambda a, t: (t.astype(F32), t),
                  lambda t, ct: (-_dot(_dot(t, ct, _DIMS["tn"]), t, _DIMS["nt"]), jnp.zeros_like(t)))


@jax.custom_vjp
def _lanes_join(a, b):
    return jnp.concatenate([a, b], axis=-1)


_lanes_join.defvjp(lambda a, b: (jnp.concatenate([a, b], axis=-1), None),
                   lambda _, ct: (ct[..., :ct.shape[-1] // 2], ct[..., ct.shape[-1] // 2:]))


@jax.custom_vjp
def _lanes_halves(y):
    h = y.shape[-1] // 2
    return y[..., :h], y[..., h:]


_lanes_halves.defvjp(lambda y: ((y[..., :y.shape[-1] // 2], y[..., y.shape[-1] // 2:]), None),
                     lambda _, ct: (jnp.concatenate(ct, axis=-1),))

GROUP = 4
GROUP_ROWS = GROUP * CHUNK


def _block_consts(n):
    ii = lax.broadcasted_iota(jnp.int32, (n, n), 0)
    jj = lax.broadcasted_iota(jnp.int32, (n, n), 1)
    shift = CHUNK.bit_length() - 1
    same = jnp.right_shift(ii, shift) == jnp.right_shift(jj, shift)
    return same & (ii >= jj), same & (ii <= jj), same & (ii > jj), same, ii == jj


def _lane0(n):
    s = (lax.broadcasted_iota(jnp.int32, (LANES, n), 0) == 0).astype(F32)
    st = (lax.broadcasted_iota(jnp.int32, (n, LANES), 1) == 0).astype(F32)
    return s, st


def _dn_group(q, k, v, g, beta, t_saved=None):
    n = GROUP_ROWS
    low_b, upp_b, strict_b, _, eye_b = _block_consts(n)
    low, upp, eye = low_b.astype(F32), upp_b.astype(F32), eye_b.astype(F32)
    gc = _sel_left(low, upp, g)
    per_chunk = (g.shape[0], GROUP, CHUNK, LANES)
    gl = jnp.broadcast_to(jnp.sum(g.reshape(per_chunk), axis=2, keepdims=True), per_chunk).reshape(g.shape)
    s, st = _lane0(n)
    col = _sel_right(gc, s, st)
    row = _sel_nt(st, gc)
    decay = jnp.exp(jnp.where(low_b, col - row, -jnp.inf))
    kb = k * beta
    vb = v * beta
    a = jnp.where(strict_b, _dot(kb, k, _DIMS["nt"]) * decay, 0.0)
    t = _inv_impl(a, eye, strict_b.astype(F32)) if t_saved is None else _inv_given(a, t_saved)
    u, w = _lanes_halves(_dot3(t, _lanes_join(vb, kb * jnp.exp(gc))))
    return u, w, q * jnp.exp(gc), k * jnp.exp(gl - gc), t


def _dn_chunk(q, k, g):
    ii = lax.broadcasted_iota(jnp.int32, (CHUNK, CHUNK), 0)
    jj = lax.broadcasted_iota(jnp.int32, (CHUNK, CHUNK), 1)
    low = (ii >= jj).astype(F32)
    upp = (ii <= jj).astype(F32)
    s, st = _lane0(CHUNK)
    gc = _sel_left(low, upp, g)
    col = _sel_right(gc, s, st)
    row = _sel_nt(st, gc)
    decay = jnp.exp(jnp.where(ii >= jj, col - row, -jnp.inf))
    qk = _dot(q, k, _DIMS["nt"]) * decay
    return qk, jnp.exp(jnp.sum(g, axis=-2, keepdims=True))


def _dn_step(s, u, w, qe, kd, qk, egl):
    v_new = u - _dot(w, s)
    o = _dot(qe, s) + _dot(qk, v_new)
    s_new = s * egl + _dot(kd, v_new, _DIMS["tn"])
    return s_new, o


def _swa_block(q, kband, vband, qg, kg, sinks, band):
    kn = _rms(kband, kg)
    qn = _rms(q, qg) * (SWA_DIM ** -0.5)
    logits = _dot(qn, kn, _DIMS["nt"]) + band
    m = lax.stop_gradient(jnp.maximum(jnp.max(logits, axis=-1, keepdims=True), sinks))
    p = jnp.exp(logits - m)
    denom = jnp.sum(p, axis=-1, keepdims=True) + jnp.exp(sinks - m)
    return _dot(p * (1.0 / denom), vband)


def _adamw(w, g, m, v):
    m = ADAM_B1 * m + (1.0 - ADAM_B1) * g
    v = ADAM_B2 * v + (1.0 - ADAM_B2) * jnp.square(g)
    m_hat = m / (1.0 - ADAM_B1 ** ADAM_STEP)
    v_hat = v / (1.0 - ADAM_B2 ** ADAM_STEP)
    delta = -ADAM_LR * (m_hat / (jnp.sqrt(v_hat) + ADAM_EPS) + ADAM_WD * w)
    return delta, m, v


def _row(tm, c, cb=0):
    return pl.BlockSpec((tm, c), lambda i, cb=cb: (i, cb))


def _full(shape):
    nd = len(shape)
    return pl.BlockSpec(shape, lambda *_, nd=nd: (0,) * nd)


def _norm_fwd(x, gain, name, tm=1024):
    S = x.shape[0]

    def body(x_ref, g_ref, h_ref):
        h_ref[...] = _rms(x_ref[...], g_ref[...]).astype(BF16)

    return pl.pallas_call(
        body, grid=(S // tm,), in_specs=[_row(tm, D_MODEL), _full((1, D_MODEL))],
        out_specs=_row(tm, D_MODEL), out_shape=jax.ShapeDtypeStruct((S, D_MODEL), BF16),
        name=name, compiler_params=_params(("parallel",)))(x, gain)


def _shift_down(x, s):
    row = lax.broadcasted_iota(jnp.int32, x.shape, 0)
    return jnp.where(row >= s, pltpu.roll(x, s, axis=0), 0.0)


def _shift_up(x, s):
    n = x.shape[0]
    row = lax.broadcasted_iota(jnp.int32, x.shape, 0)
    return jnp.where(row < n - s, pltpu.roll(x, n - s, axis=0), 0.0)


def _conv(x, w):
    out = w[DN_CONV - 1:DN_CONV] * x
    for s in range(1, DN_CONV):
        out = out + w[DN_CONV - 1 - s:DN_CONV - s] * _shift_down(x, s)
    return out


def _dn_conv_fwd(proj, conv_w):
    S = proj.shape[0]
    nb = DN_QKV // LANES

    def body(x_ref, w_ref, o_ref):
        j = pl.program_id(0)
        q_scale = jnp.where(j < DN_HEADS, DN_DIM ** -0.5, 1.0).astype(F32)
        o_ref[...] = _dn_post(_conv(x_ref[...], w_ref[...]), j >= 2 * DN_HEADS, q_scale)

    return pl.pallas_call(
        body, grid=(nb,),
        in_specs=[pl.BlockSpec((S, LANES), lambda j: (0, P_QKV // LANES + j)),
                  pl.BlockSpec((DN_CONV, LANES), lambda j: (0, j))],
        out_specs=pl.BlockSpec((S, LANES), lambda j: (0, j)),
        out_shape=jax.ShapeDtypeStruct((S, DN_QKV), F32), name="dn_conv_fwd",
        compiler_params=_params(("parallel",)))(proj, conv_w)


def _dn_conv_bwd(proj, conv_w, dqkvn, dproj):
    S = proj.shape[0]
    nb = DN_QKV // LANES

    def body(x_ref, w_ref, d_ref, _, dx_ref, dw_ref):
        j = pl.program_id(0)
        q_scale = jnp.where(j < DN_HEADS, DN_DIM ** -0.5, 1.0).astype(F32)
        x = x_ref[...]
        w = w_ref[...]
        _, vjp = jax.vjp(lambda c: _dn_post(c, j >= 2 * DN_HEADS, q_scale), _conv(x, w))
        (dc,) = vjp(d_ref[0])
        dx = w[DN_CONV - 1:DN_CONV] * dc
        dw_ref[DN_CONV - 1:DN_CONV, :] = jnp.sum(dc * x, axis=0, keepdims=True)
        for s in range(1, DN_CONV):
            dx = dx + w[DN_CONV - 1 - s:DN_CONV - s] * _shift_up(dc, s)
            dw_ref[DN_CONV - 1 - s:DN_CONV - s, :] = jnp.sum(dc * _shift_down(x, s), axis=0, keepdims=True)
        dx_ref[...] = dx.astype(BF16)

    return pl.pallas_call(
        body, grid=(nb,),
        in_specs=[pl.BlockSpec((S, LANES), lambda j: (0, P_QKV // LANES + j)),
                  pl.BlockSpec((DN_CONV, LANES), lambda j: (0, j)),
                  pl.BlockSpec((1, S, LANES), lambda j: (lax.div(j, DN_HEADS), 0, lax.rem(j, DN_HEADS))),
                  pl.BlockSpec(memory_space=pl.ANY)],
        out_specs=[pl.BlockSpec((S, LANES), lambda j: (0, P_QKV // LANES + j)),
                   pl.BlockSpec((DN_CONV, LANES), lambda j: (0, j))],
        out_shape=[jax.ShapeDtypeStruct(dproj.shape, dproj.dtype), jax.ShapeDtypeStruct((DN_CONV, DN_QKV), F32)],
        input_output_aliases={3: 0},
        name="dn_conv_bwd", compiler_params=_params(("parallel",)))(proj, conv_w, dqkvn, dproj)


def _expanders():
    eb = np.zeros((LANES, DN_WIDTH), np.float32)
    ea = np.zeros((LANES, DN_WIDTH), np.float32)
    for h in range(DN_HEADS):
        eb[h, h * DN_DIM:(h + 1) * DN_DIM] = 1.0
        ea[DN_HEADS + h, h * DN_DIM:(h + 1) * DN_DIM] = 1.0
    return jnp.asarray(eb), jnp.asarray(ea), jnp.asarray(eb.T), jnp.asarray(ea.T)


def _dn_gate_args(a_log, dt_bias):
    alog = jnp.repeat(a_log.reshape(1, DN_HEADS), DN_DIM, axis=1)
    dtb = _pad_to(jnp.pad(dt_bias.reshape(1, DN_HEADS), ((0, 0), (DN_HEADS, 0))), (1, LANES))
    return _expanders() + (alog, dtb)


def _dn_gate_specs(tm):
    return [_row(tm, LANES, P_BA // LANES), _full((LANES, DN_WIDTH)), _full((LANES, DN_WIDTH)),
            _full((DN_WIDTH, LANES)), _full((DN_WIDTH, LANES)), _full((1, DN_WIDTH)), _full((1, LANES))]


def _dn_gate_fn(ba, eb, ea, ebt, eat, alog, dtb):
    beta = _sel_right(jax.nn.sigmoid(ba), eb, ebt)
    g = -jnp.exp(alog) * _sel_right(jax.nn.softplus(ba + dtb), ea, eat)
    return beta, g


def _dn_gate_fwd(proj, a_log, dt_bias, tm=1024):
    S = proj.shape[0]
    args = _dn_gate_args(a_log, dt_bias)

    def body(ba_ref, eb_ref, ea_ref, ebt_ref, eat_ref, al_ref, dt_ref, beta_ref, g_ref):
        beta, g = _dn_gate_fn(ba_ref[...], eb_ref[...], ea_ref[...], ebt_ref[...], eat_ref[...], al_ref[...],
                              dt_ref[...])
        beta_ref[...] = beta
        g_ref[...] = g

    return pl.pallas_call(
        body, grid=(S // tm,), in_specs=_dn_gate_specs(tm), out_specs=[_row(tm, DN_WIDTH), _row(tm, DN_WIDTH)],
        out_shape=[jax.ShapeDtypeStruct((S, DN_WIDTH), F32), jax.ShapeDtypeStruct((S, DN_WIDTH), F32)],
        name="dn_gate_fwd", compiler_params=_params(("parallel",)))(proj, *args)


def _dn_gate_bwd(proj, a_log, dt_bias, dbeta, dg, dproj, tm=1024):
    S = proj.shape[0]
    args = _dn_gate_args(a_log, dt_bias)

    def body(ba_ref, eb_ref, ea_ref, ebt_ref, eat_ref, al_ref, dt_ref, dbeta_ref, dg_ref, _, dba_ref, dal_ref,
             ddt_ref):
        eb, ea, ebt, eat = eb_ref[...], ea_ref[...], ebt_ref[...], eat_ref[...]
        _, vjp = jax.vjp(lambda ba, al, dt: _dn_gate_fn(ba, eb, ea, ebt, eat, al, dt), ba_ref[...], al_ref[...],
                         dt_ref[...])
        dba, dal, ddt = vjp((dbeta_ref[...], dg_ref[...]))
        dba_ref[...] = dba.astype(BF16)

        @pl.when(pl.program_id(0) == 0)
        def _():
            dal_ref[...] = jnp.zeros_like(dal_ref)
            ddt_ref[...] = jnp.zeros_like(ddt_ref)

        dal_ref[...] += dal
        ddt_ref[...] += ddt

    return pl.pallas_call(
        body, grid=(S // tm,),
        in_specs=_dn_gate_specs(tm) + [_row(tm, DN_WIDTH), _row(tm, DN_WIDTH), pl.BlockSpec(memory_space=pl.ANY)],
        out_specs=[_row(tm, LANES, P_BA // LANES), _full((1, DN_WIDTH)), _full((1, LANES))],
        out_shape=[jax.ShapeDtypeStruct(dproj.shape, dproj.dtype), jax.ShapeDtypeStruct((1, DN_WIDTH), F32),
                   jax.ShapeDtypeStruct((1, LANES), F32)],
        input_output_aliases={len(args) + 3: 0},
        name="dn_gate_bwd", compiler_params=_params(("arbitrary",)))(proj, *args, dbeta, dg, dproj)


PREP_GROUPS = 4
PREP_CHUNKS = GROUP * PREP_GROUPS


def _dn_prep_specs():
    rows = PREP_CHUNKS * CHUNK
    q = pl.BlockSpec((rows, LANES), lambda h, c: (c, h))
    k = pl.BlockSpec((rows, LANES), lambda h, c: (c, DN_HEADS + h))
    v = pl.BlockSpec((rows, LANES), lambda h, c: (c, 2 * DN_HEADS + h))
    qk = pl.BlockSpec((1, rows, CHUNK), lambda h, c: (h, c, 0))
    egl = pl.BlockSpec((1, PREP_CHUNKS, 1, LANES), lambda h, c: (h, c, 0, 0))
    return q, k, v, qk, egl


def _dn_prep_fwd(qkvn, g, beta):
    S = qkvn.shape[0]
    nc = S // CHUNK
    q, k, v, qks, egl = _dn_prep_specs()

    def body(q_ref, k_ref, v_ref, g_ref, b_ref, u_ref, w_ref, qe_ref, kd_ref, qk_ref, egl_ref, t_ref):
        rows = PREP_CHUNKS * CHUNK
        grp = (PREP_GROUPS, GROUP_ROWS, LANES)
        chk = (PREP_CHUNKS, CHUNK, LANES)
        q, k, g = q_ref[...], k_ref[...], g_ref[...]
        u, w, qe, kd, t = _dn_group(q.reshape(grp), k.reshape(grp), v_ref[...].reshape(grp), g.reshape(grp),
                                    b_ref[...].reshape(grp))
        u_ref[...] = u.reshape(rows, LANES)
        w_ref[...] = w.reshape(rows, LANES)
        qe_ref[...] = qe.reshape(rows, LANES)
        kd_ref[...] = kd.reshape(rows, LANES)
        t_ref[0] = t.reshape(rows, GROUP_ROWS).astype(BF16)
        qk, e = _dn_chunk(q.reshape(chk), k.reshape(chk), g.reshape(chk))
        qk_ref[0] = qk.reshape(rows, CHUNK)
        egl_ref[0] = e

    wide = jax.ShapeDtypeStruct((S, DN_WIDTH), F32)
    return pl.pallas_call(
        body, grid=(DN_HEADS, nc // PREP_CHUNKS), in_specs=[q, k, v, q, q],
        out_specs=[q, q, q, q, qks, egl, _dn_tinv_spec()],
        out_shape=[wide, wide, wide, wide, jax.ShapeDtypeStruct((DN_HEADS, S, CHUNK), F32),
                   jax.ShapeDtypeStruct((DN_HEADS, nc, 1, LANES), F32),
                   jax.ShapeDtypeStruct((DN_HEADS, S, GROUP_ROWS), BF16)],
        name="dn_prep_fwd", compiler_params=_params(("parallel", "parallel")))(qkvn, qkvn, qkvn, g, beta)


def _dn_tinv_spec():
    return pl.BlockSpec((1, PREP_CHUNKS * CHUNK, GROUP_ROWS), lambda h, c: (h, c, 0))


def _dn_prep_bwd(qkvn, g, beta, tinv, du, dw, dqe, dkd, dqk, degl):
    S = qkvn.shape[0]
    nc = S // CHUNK
    q, k, v, qks, egl = _dn_prep_specs()

    def body(q_ref, k_ref, v_ref, g_ref, b_ref, t_ref, du_ref, dw_ref, dqe_ref, dkd_ref, dqk_ref, degl_ref,
             dqkv_ref, dg_ref, db_ref):
        rows = PREP_CHUNKS * CHUNK
        grp = (PREP_GROUPS, GROUP_ROWS, LANES)
        chk = (PREP_CHUNKS, CHUNK, LANES)
        q, k, g = q_ref[...], k_ref[...], g_ref[...]
        t_saved = t_ref[0].reshape(PREP_GROUPS, GROUP_ROWS, GROUP_ROWS)
        _, vjp = jax.vjp(lambda *x: _dn_group(*x, t_saved=t_saved)[:4], q.reshape(grp), k.reshape(grp),
                         v_ref[...].reshape(grp), g.reshape(grp), b_ref[...].reshape(grp))
        dq, dk, dv, dg, db = vjp((du_ref[...].reshape(grp), dw_ref[...].reshape(grp), dqe_ref[...].reshape(grp),
                                  dkd_ref[...].reshape(grp)))
        _, vjp = jax.vjp(_dn_chunk, q.reshape(chk), k.reshape(chk), g.reshape(chk))
        dq2, dk2, dg2 = vjp((dqk_ref[0].reshape(PREP_CHUNKS, CHUNK, CHUNK), degl_ref[0]))
        dqkv_ref[0] = dq.reshape(rows, LANES) + dq2.reshape(rows, LANES)
        dqkv_ref[1] = dk.reshape(rows, LANES) + dk2.reshape(rows, LANES)
        dqkv_ref[2] = dv.reshape(rows, LANES)
        dg_ref[...] = dg.reshape(rows, LANES) + dg2.reshape(rows, LANES)
        db_ref[...] = db.reshape(rows, LANES)

    wide = jax.ShapeDtypeStruct((S, DN_WIDTH), F32)
    rows = PREP_CHUNKS * CHUNK
    return pl.pallas_call(
        body, grid=(DN_HEADS, nc // PREP_CHUNKS), in_specs=[q, k, v, q, q, _dn_tinv_spec(), q, q, q, q, qks, egl],
        out_specs=[pl.BlockSpec((3, rows, LANES), lambda h, c: (0, c, h)), q, q],
        out_shape=[jax.ShapeDtypeStruct((3, S, DN_WIDTH), F32), wide, wide],
        name="dn_prep_bwd", compiler_params=_params(("parallel", "parallel")),
    )(qkvn, qkvn, qkvn, g, beta, tinv, du, dw, dqe, dkd, dqk, degl)


SCAN_CHUNKS = 8


def _dn_scan_specs(nc, reverse):
    nb = nc // SCAN_CHUNKS

    def cidx(c):
        return nb - 1 - c if reverse else c

    hc = pl.BlockSpec((SCAN_CHUNKS * CHUNK, DN_WIDTH), lambda c: (cidx(c), 0))
    qk = pl.BlockSpec((DN_HEADS, SCAN_CHUNKS * CHUNK, CHUNK), lambda c: (0, cidx(c), 0))
    egl = pl.BlockSpec((DN_HEADS, SCAN_CHUNKS, 1, LANES), lambda c: (0, cidx(c), 0, 0))
    st = pl.BlockSpec((DN_HEADS, SCAN_CHUNKS, DN_DIM, DN_DIM), lambda c: (0, cidx(c), 0, 0))
    return hc, qk, egl, st


def _heads(ref, i):
    return jnp.stack([ref[pl.ds(i * CHUNK, CHUNK), pl.ds(h * DN_DIM, DN_DIM)] for h in range(DN_HEADS)])


def _dn_scan_fwd(u, w, qe, kd, qk, egl):
    S = u.shape[0]
    nc = S // CHUNK
    hc, qks, egls, st = _dn_scan_specs(nc, False)

    def body(u_ref, w_ref, qe_ref, kd_ref, qk_ref, egl_ref, o_ref, st_ref, s_scr):
        @pl.when(pl.program_id(0) == 0)
        def _():
            s_scr[...] = jnp.zeros_like(s_scr)

        s = s_scr[...]
        for i in range(SCAN_CHUNKS):
            rows = pl.ds(i * CHUNK, CHUNK)
            st_ref[:, i] = s
            s, o = _dn_step(s, _heads(u_ref, i), _heads(w_ref, i), _heads(qe_ref, i), _heads(kd_ref, i),
                            qk_ref[:, rows, :], egl_ref[:, i])
            for h in range(DN_HEADS):
                o_ref[rows, pl.ds(h * DN_DIM, DN_DIM)] = o[h]
        s_scr[...] = s

    return pl.pallas_call(
        body, grid=(nc // SCAN_CHUNKS,), in_specs=[hc, hc, hc, hc, qks, egls], out_specs=[hc, st],
        out_shape=[jax.ShapeDtypeStruct((S, DN_WIDTH), F32), jax.ShapeDtypeStruct((DN_HEADS, nc, DN_DIM, DN_DIM), F32)],
        scratch_shapes=[pltpu.VMEM((DN_HEADS, DN_DIM, DN_DIM), F32)], name="dn_scan_fwd",
        compiler_params=_params(("arbitrary",)))(u, w, qe, kd, qk, egl)


def _dn_scan_bwd(u, w, qe, kd, qk, egl, states, do):
    S = u.shape[0]
    nc = S // CHUNK
    hc, qks, egls, st = _dn_scan_specs(nc, True)

    def body(u_ref, w_ref, qe_ref, kd_ref, qk_ref, egl_ref, st_ref, do_ref,
             du_ref, dw_ref, dqe_ref, dkd_ref, dqk_ref, degl_ref, ds_scr):
        @pl.when(pl.program_id(0) == 0)
        def _():
            ds_scr[...] = jnp.zeros_like(ds_scr)

        ds = ds_scr[...]
        for i in reversed(range(SCAN_CHUNKS)):
            rows = pl.ds(i * CHUNK, CHUNK)
            _, vjp = jax.vjp(_dn_step, st_ref[:, i], _heads(u_ref, i), _heads(w_ref, i), _heads(qe_ref, i),
                             _heads(kd_ref, i), qk_ref[:, rows, :], egl_ref[:, i])
            ds, du, dw, dqe, dkd, dqk, degl = vjp((ds, _heads(do_ref, i)))
            dqk_ref[:, rows, :] = dqk
            degl_ref[:, i] = degl
            for h in range(DN_HEADS):
                cols = pl.ds(h * DN_DIM, DN_DIM)
                du_ref[rows, cols] = du[h]
                dw_ref[rows, cols] = dw[h]
                dqe_ref[rows, cols] = dqe[h]
                dkd_ref[rows, cols] = dkd[h]
        ds_scr[...] = ds

    wide = jax.ShapeDtypeStruct((S, DN_WIDTH), F32)
    return pl.pallas_call(
        body, grid=(nc // SCAN_CHUNKS,), in_specs=[hc, hc, hc, hc, qks, egls, st, hc],
        out_specs=[hc, hc, hc, hc, qks, egls],
        out_shape=[wide, wide, wide, wide, jax.ShapeDtypeStruct((DN_HEADS, S, CHUNK), F32),
                   jax.ShapeDtypeStruct((DN_HEADS, nc, 1, LANES), F32)],
        scratch_shapes=[pltpu.VMEM((DN_HEADS, DN_DIM, DN_DIM), F32)], name="dn_scan_bwd",
        compiler_params=_params(("arbitrary",)))(u, w, qe, kd, qk, egl, states, do)


def _dn_out_fwd(o, proj, gain, tm=1024):
    S = o.shape[0]

    def body(o_ref, z_ref, g_ref, y_ref):
        y_ref[...] = _dn_out(o_ref[...], z_ref[...], g_ref[...]).astype(BF16)

    hs = pl.BlockSpec((tm, LANES), lambda i, h: (i, h))
    zs = pl.BlockSpec((tm, LANES), lambda i, h: (i, P_Z // LANES + h))
    return pl.pallas_call(
        body, grid=(S // tm, DN_HEADS), in_specs=[hs, zs, _full((1, DN_DIM))], out_specs=hs,
        out_shape=jax.ShapeDtypeStruct((S, DN_WIDTH), BF16), name="dn_out_fwd",
        compiler_params=_params(("parallel", "parallel")))(o, proj, gain)


_ANY = pl.BlockSpec(memory_space=pl.ANY)


def _dn_out_bwd(o, proj, gain, dy, dproj, tm=1024):
    S = o.shape[0]

    def body(o_ref, z_ref, g_ref, dy_ref, _, do_ref, dz_ref, dg_ref):
        _, vjp = jax.vjp(_dn_out, o_ref[...], z_ref[...], g_ref[...])
        do, dz, dg = vjp(dy_ref[...])
        do_ref[...] = do
        dz_ref[...] = dz.astype(BF16)

        @pl.when((pl.program_id(0) == 0) & (pl.program_id(1) == 0))
        def _():
            dg_ref[...] = jnp.zeros_like(dg_ref)

        dg_ref[...] += dg

    hs = pl.BlockSpec((tm, LANES), lambda i, h: (i, h))
    zs = pl.BlockSpec((tm, LANES), lambda i, h: (i, P_Z // LANES + h))
    return pl.pallas_call(
        body, grid=(S // tm, DN_HEADS), in_specs=[hs, zs, _full((1, DN_DIM)), hs, _ANY],
        out_specs=[hs, zs, _full((1, DN_DIM))],
        out_shape=[jax.ShapeDtypeStruct((S, DN_WIDTH), F32), jax.ShapeDtypeStruct(dproj.shape, dproj.dtype),
                   jax.ShapeDtypeStruct((1, DN_DIM), F32)],
        input_output_aliases={4: 1},
        name="dn_out_bwd", compiler_params=_params(("arbitrary", "arbitrary")))(o, proj, gain, dy, dproj)


def _rel_buckets():
    qi = np.arange(BLOCK)[:, None]
    kj = np.arange(2 * BLOCK)[None, :]
    n = np.maximum(BLOCK + qi - kj, 0)
    max_exact = REL_BUCKETS // 2
    nf = np.maximum(n, 1).astype(np.float32)
    large = max_exact + (np.log(nf / np.float32(max_exact)) / np.float32(math.log(REL_MAX_DIST / max_exact))
                         * np.float32(REL_BUCKETS - max_exact)).astype(np.int32)
    large = np.minimum(large, REL_BUCKETS - 1)
    return np.where(n < max_exact, n, large).astype(np.int32)


def _bias_fwd(rel_bias):
    buckets = jnp.asarray(_rel_buckets())

    def body(rb_ref, bk_ref, o_ref):
        bk = bk_ref[...]
        for h in range(SWA_HEADS):
            acc = jnp.zeros((BLOCK, 2 * BLOCK), F32)
            for b in range(REL_BUCKETS):
                acc = jnp.where(bk == b, rb_ref[b, h], acc)
            for first in range(2):
                o_ref[first, h] = jnp.where(_swa_mask(1 - first), acc, -jnp.inf)

    return pl.pallas_call(
        body, in_specs=[pl.BlockSpec(memory_space=pltpu.SMEM), pl.BlockSpec(memory_space=pltpu.VMEM)],
        out_specs=pl.BlockSpec(memory_space=pltpu.VMEM),
        out_shape=jax.ShapeDtypeStruct((2, SWA_HEADS, BLOCK, 2 * BLOCK), F32), name="swa_bias_fwd",
        compiler_params=_params())(rel_bias, buckets)


def _bias_bwd(dbias):
    buckets = jnp.asarray(_rel_buckets())

    def body(d_ref, bk_ref, o_ref):
        bk = bk_ref[...]
        lane = lax.broadcasted_iota(jnp.int32, (1, LANES), 1)
        for h in range(SWA_HEADS):
            d = d_ref[h]
            row = jnp.zeros((1, LANES), F32)
            for b in range(REL_BUCKETS):
                part = jnp.sum(jnp.where(bk == b, d, 0.0), axis=1, keepdims=True)
                row = jnp.where(lane == b, jnp.sum(part, axis=0, keepdims=True), row)
            o_ref[h:h + 1, :] = row

    return pl.pallas_call(
        body, in_specs=[pl.BlockSpec(memory_space=pltpu.VMEM), pl.BlockSpec(memory_space=pltpu.VMEM)],
        out_specs=pl.BlockSpec(memory_space=pltpu.VMEM),
        out_shape=jax.ShapeDtypeStruct((SWA_HEADS, LANES), F32), name="swa_bias_bwd",
        compiler_params=_params())(dbias, buckets)


def _swa_mask(n):
    qi = lax.broadcasted_iota(jnp.int32, (BLOCK, 2 * BLOCK), 0)
    kj = lax.broadcasted_iota(jnp.int32, (BLOCK, 2 * BLOCK), 1)
    dist = BLOCK + qi - kj
    return (dist >= 0) & (dist < WINDOW) & ((n > 0) | (kj >= BLOCK))


def _swa_in_specs():
    q = pl.BlockSpec((BLOCK, SWA_WIDTH), lambda n: (n, P_SQ // SWA_WIDTH))
    kc = pl.BlockSpec((BLOCK, SWA_KVW), lambda n: (n, P_SK // SWA_KVW))
    kp = pl.BlockSpec((BLOCK, SWA_KVW), lambda n: (jnp.maximum(n - 1, 0), P_SK // SWA_KVW))
    vc = pl.BlockSpec((BLOCK, SWA_KVW), lambda n: (n, P_SV // SWA_KVW))
    vp = pl.BlockSpec((BLOCK, SWA_KVW), lambda n: (jnp.maximum(n - 1, 0), P_SV // SWA_KVW))
    band = pl.BlockSpec((None, SWA_HEADS, BLOCK, 2 * BLOCK), lambda n: (jnp.where(n == 0, 1, 0), 0, 0, 0))
    small = [_full((1, SWA_DIM)), _full((1, SWA_DIM)), _full((1, SWA_HEADS)), band]
    return [q, kp, kc, vp, vc] + small


def _swa_load(q_ref, kp_ref, kc_ref, vp_ref, vc_ref, s_ref):
    q = jnp.stack([q_ref[:, pl.ds(h * SWA_DIM, SWA_DIM)] for h in range(SWA_HEADS)])
    kbands, vbands = [], []
    for kv in range(SWA_KV):
        cols = pl.ds(kv * SWA_DIM, SWA_DIM)
        kbands += [jnp.concatenate([kp_ref[:, cols], kc_ref[:, cols]], axis=0)] * SWA_GROUP
        vbands += [jnp.concatenate([vp_ref[:, cols], vc_ref[:, cols]], axis=0)] * SWA_GROUP
    sinks = jnp.stack([s_ref[:, pl.ds(h, 1)] for h in range(SWA_HEADS)])
    return q, jnp.stack(kbands), jnp.stack(vbands), sinks


def _swa_fwd(proj, q_gain, k_gain, sinks, bias):
    S = proj.shape[0]

    def body(q_ref, kp_ref, kc_ref, vp_ref, vc_ref, qg_ref, kg_ref, s_ref, bias_ref, y_ref):
        q, kband, vband, sk = _swa_load(q_ref, kp_ref, kc_ref, vp_ref, vc_ref, s_ref)
        out = _swa_block(q, kband, vband, qg_ref[...], kg_ref[...], sk, bias_ref[...])
        for h in range(SWA_HEADS):
            y_ref[:, pl.ds(h * SWA_DIM, SWA_DIM)] = out[h].astype(BF16)

    return pl.pallas_call(
        body, grid=(S // BLOCK,), in_specs=_swa_in_specs(),
        out_specs=pl.BlockSpec((BLOCK, SWA_WIDTH), lambda n: (n, 0)),
        out_shape=jax.ShapeDtypeStruct((S, SWA_WIDTH), BF16), name="swa_fwd",
        compiler_params=_params(("parallel",)))(proj, proj, proj, proj, proj, q_gain, k_gain, sinks, bias)


def _swa_bwd(proj, q_gain, k_gain, sinks, bias, dy, dproj):
    S = proj.shape[0]

    def body(q_ref, kp_ref, kc_ref, vp_ref, vc_ref, qg_ref, kg_ref, s_ref, bias_ref, dy_ref, _,
             dq_ref, dk_ref, dv_ref, dqg_ref, dkg_ref, ds_ref, dbias_ref):
        n = pl.program_id(0)

        @pl.when(n == 0)
        def _():
            for r in (dk_ref, dv_ref, dqg_ref, dkg_ref, ds_ref, dbias_ref):
                r[...] = jnp.zeros_like(r)

        cur = pl.ds(pl.multiple_of(n * BLOCK, BLOCK), BLOCK)
        prev = pl.ds(pl.multiple_of(jnp.maximum(n - 1, 0) * BLOCK, BLOCK), BLOCK)
        q, kband, vband, sk = _swa_load(q_ref, kp_ref, kc_ref, vp_ref, vc_ref, s_ref)
        _, vjp = jax.vjp(_swa_block, q, kband, vband, qg_ref[...], kg_ref[...], sk, bias_ref[...])
        dy = jnp.stack([dy_ref[:, pl.ds(h * SWA_DIM, SWA_DIM)] for h in range(SWA_HEADS)])
        dq, dkb, dvb, dqg, dkg, dsk, dbs = vjp(dy)
        for h in range(SWA_HEADS):
            dq_ref[:, pl.ds(h * SWA_DIM, SWA_DIM)] = dq[h].astype(BF16)
            ds_ref[:, pl.ds(h, 1)] += dsk[h]
        dbias_ref[...] += dbs
        dqg_ref[...] += dqg
        dkg_ref[...] += dkg
        for kv in range(SWA_KV):
            cols = pl.ds(kv * SWA_DIM, SWA_DIM)
            group = range(kv * SWA_GROUP, (kv + 1) * SWA_GROUP)
            dk_kv = sum(dkb[h] for h in group)
            dv_kv = sum(dvb[h] for h in group)
            dk_ref[cur, cols] += dk_kv[BLOCK:]
            dv_ref[cur, cols] += dv_kv[BLOCK:]

            @pl.when(n > 0)
            def _(cols=cols, dk_kv=dk_kv, dv_kv=dv_kv):
                dk_ref[prev, cols] += dk_kv[:BLOCK]
                dv_ref[prev, cols] += dv_kv[:BLOCK]

    return pl.pallas_call(
        body, grid=(S // BLOCK,),
        in_specs=_swa_in_specs() + [pl.BlockSpec((BLOCK, SWA_WIDTH), lambda n: (n, 0)),
                                    pl.BlockSpec(memory_space=pl.ANY)],
        out_specs=[pl.BlockSpec((BLOCK, SWA_WIDTH), lambda n: (n, P_SQ // SWA_WIDTH)), _full((S, SWA_KVW)),
                   _full((S, SWA_KVW)), _full((1, SWA_DIM)), _full((1, SWA_DIM)), _full((1, SWA_HEADS)),
                   _full((SWA_HEADS, BLOCK, 2 * BLOCK))],
        out_shape=[jax.ShapeDtypeStruct(dproj.shape, dproj.dtype), jax.ShapeDtypeStruct((S, SWA_KVW), F32),
                   jax.ShapeDtypeStruct((S, SWA_KVW), F32), jax.ShapeDtypeStruct((1, SWA_DIM), F32),
                   jax.ShapeDtypeStruct((1, SWA_DIM), F32), jax.ShapeDtypeStruct((1, SWA_HEADS), F32),
                   jax.ShapeDtypeStruct((SWA_HEADS, BLOCK, 2 * BLOCK), F32)],
        input_output_aliases={10: 0},
        name="swa_bwd", compiler_params=_params(("arbitrary",)),
    )(proj, proj, proj, proj, proj, q_gain, k_gain, sinks, bias, dy, dproj)


def _kv_into(dproj, dk, dv, tm=1024):
    S = dk.shape[0]

    def body(dk_ref, dv_ref, _, o_ref):
        o_ref[:, :SWA_KVW] = dk_ref[...].astype(BF16)
        o_ref[:, SWA_KVW:] = dv_ref[...].astype(BF16)

    return pl.pallas_call(
        body, grid=(S // tm,), in_specs=[_row(tm, SWA_KVW), _row(tm, SWA_KVW), pl.BlockSpec(memory_space=pl.ANY)],
        out_specs=_row(tm, 2 * SWA_KVW, P_SK // (2 * SWA_KVW)),
        out_shape=jax.ShapeDtypeStruct(dproj.shape, dproj.dtype), input_output_aliases={2: 0},
        name="swa_kv_into", compiler_params=_params(("parallel",)))(dk, dv, dproj)


def _position():
    return lax.axis_index("x"), lax.axis_index("y"), lax.axis_index("c")


def _all_gather(shards, name="all_gather_weights"):
    na = len(shards)

    def body(*refs):
        x_refs, out_refs = refs[:na], refs[na:2 * na]
        send_sems, recv_sems, local_sems = refs[2 * na:]
        x, y, c = _position()
        me, sibling = (x, y, c), (x, y, 1 - c)
        chips = [(1 - x, y), (x, 1 - y), (1 - x, 1 - y)]

        def copy(a, k, block, to, own=False):
            px, py, pc = block
            slot = out_refs[a].at[4 * px + 2 * py + pc]
            return pltpu.make_async_remote_copy(
                src_ref=x_refs[a] if own else slot, dst_ref=slot, send_sem=send_sems.at[7 * a + k],
                recv_sem=recv_sems.at[7 * a + k], device_id=to, device_id_type=MESH_ID)

        mine = [pltpu.make_async_copy(x_refs[a], out_refs[a].at[4 * x + 2 * y + c], local_sems.at[a])
                for a in range(na)]
        for cp in mine:
            cp.start()
        first = []
        for a in range(na):
            first.append(copy(a, 0, me, sibling, own=True))
            first += [copy(a, 1 + j, me, (*chip, c), own=True) for j, chip in enumerate(chips)]
        for cp in first:
            cp.start()
        passed = []
        for j, chip in enumerate(chips):
            for a in range(na):
                copy(a, 1 + j, (*chip, c), me).wait_recv()
                passed.append(copy(a, 4 + j, (*chip, c), sibling))
                passed[-1].start()
        for a in range(na):
            copy(a, 0, sibling, me).wait_recv()
            for j, chip in enumerate(chips):
                copy(a, 4 + j, (*chip, 1 - c), me).wait_recv()
        for cp in first + passed:
            cp.wait_send()
        for cp in mine:
            cp.wait()

    return pl.pallas_call(
        body, in_specs=[pl.BlockSpec(memory_space=pl.ANY)] * na, out_specs=[pl.BlockSpec(memory_space=pl.ANY)] * na,
        out_shape=[jax.ShapeDtypeStruct((N_DEV,) + s.shape, s.dtype) for s in shards],
        scratch_shapes=[pltpu.SemaphoreType.DMA((7 * na,)), pltpu.SemaphoreType.DMA((7 * na,)),
                        pltpu.SemaphoreType.DMA((na,))],
        name=name)(*shards)


_HBM = pl.BlockSpec(memory_space=pltpu.HBM)
_SEM = pl.BlockSpec(memory_space=pltpu.SEMAPHORE)
_DATAFLOW = pltpu.SideEffectType.DATAFLOW_SIDE_EFFECTING


def _peers(x, y, c):
    out = []
    for k in range(1, N_DEV):
        px, py, pc = x ^ (k >> 2), y ^ ((k >> 1) & 1), c ^ (k & 1)
        out.append(((px, py, pc), 4 * px + 2 * py + pc))
    return out


def _split_copies(src_refs, land_refs, send_sems, recv_sems, scatter):
    x, y, c = _position()
    me = 4 * x + 2 * y + c
    sends, recvs = [], []
    for k, (peer_id, peer) in enumerate(_peers(x, y, c)):
        for a, (src, land) in enumerate(zip(src_refs, land_refs)):
            sems = dict(send_sem=send_sems.at[7 * a + k], recv_sem=recv_sems.at[7 * a + k],
                        device_id=peer_id, device_id_type=MESH_ID)
            mine = src.at[peer] if scatter else src
            sends.append(pltpu.make_async_remote_copy(src_ref=mine, dst_ref=land.at[me], **sems))
            recvs.append(pltpu.make_async_remote_copy(src_ref=mine, dst_ref=land.at[peer], **sems))
    return sends, recvs


def _all_gather_direct(shards, name, after):
    na = len(shards)

    def body(*refs):
        x_refs, out_refs = refs[:na], refs[na + 1:2 * na + 1]
        send_sems, recv_sems, local_sems = refs[2 * na + 1:]
        x, y, c = _position()
        me = 4 * x + 2 * y + c
        local = [pltpu.make_async_copy(x_refs[a], out_refs[a].at[me], local_sems.at[a]) for a in range(na)]
        sends, recvs = _split_copies(x_refs, out_refs, send_sems, recv_sems, False)
        for cp in local + sends:
            cp.start()
        for cp in recvs:
            cp.wait_recv()
        for cp in sends:
            cp.wait_send()
        for cp in local:
            cp.wait()

    return pl.pallas_call(
        body, in_specs=[pl.BlockSpec(memory_space=pl.ANY)] * (na + 1),
        out_specs=[pl.BlockSpec(memory_space=pl.ANY)] * na,
        out_shape=[jax.ShapeDtypeStruct((N_DEV,) + s.shape, s.dtype) for s in shards],
        scratch_shapes=[pltpu.SemaphoreType.DMA((7 * na,)), pltpu.SemaphoreType.DMA((7 * na,)),
                        pltpu.SemaphoreType.DMA((na,))],
        name=name)(*shards, after)


def _exchange_start(srcs, scatter, name, after=None):
    na = len(srcs)
    lands = [lax.empty(s.shape if scatter else (N_DEV,) + s.shape, s.dtype) for s in srcs]
    extra = [] if after is None else [after]

    def body(*refs):
        src_refs, land_refs = refs[:na], refs[na:2 * na]
        send_sems, recv_sems = refs[2 * na + len(extra)], refs[2 * na + len(extra) + 1]
        token = refs[-1]
        sends, _ = _split_copies(src_refs, land_refs, send_sems, recv_sems, scatter)
        for cp in sends:
            cp.start()
        token[...] = jnp.zeros_like(token)

    hbm = lambda a: pltpu.HBM(a.shape, a.dtype)
    out = pl.pallas_call(
        body, name=name,
        out_shape=(pltpu.SemaphoreType.DMA((7 * na,)), pltpu.SemaphoreType.DMA((7 * na,)),
                   *[hbm(s) for s in srcs], *[hbm(l) for l in lands], jax.ShapeDtypeStruct((8, LANES), F32)),
        in_specs=[_HBM] * (2 * na) + [pl.BlockSpec(memory_space=pl.ANY)] * len(extra),
        out_specs=(_SEM, _SEM, *[_HBM] * (2 * na), pl.BlockSpec(memory_space=pltpu.VMEM)),
        input_output_aliases={i: 2 + i for i in range(2 * na)},
        compiler_params=pltpu.CompilerParams(has_side_effects=_DATAFLOW),
    )(*[pltpu.with_memory_space_constraint(s, pltpu.HBM) for s in srcs],
      *[pltpu.with_memory_space_constraint(l, pltpu.HBM) for l in lands], *extra)
    return (out[0], out[1], list(out[2:2 + na]), list(out[2 + na:2 + 2 * na])), out[-1]


def _exchange_wait(handle, after, scatter, name):
    send_sems, recv_sems, srcs, lands = handle
    na = len(srcs)

    def body(*refs):
        src_refs, land_refs = refs[:na], refs[na:2 * na]
        s_sems, r_sems = refs[2 * na], refs[2 * na + 1]
        sends, recvs = _split_copies(src_refs, land_refs, s_sems, r_sems, scatter)
        for cp in sends:
            cp.wait_send()
        for cp in recvs:
            cp.wait_recv()

    hbm = lambda a: pltpu.HBM(a.shape, a.dtype)
    out = pl.pallas_call(
        body, name=name, out_shape=(*[hbm(s) for s in srcs], *[hbm(l) for l in lands]),
        in_specs=[_HBM] * (2 * na) + [_SEM, _SEM, pl.BlockSpec(memory_space=pl.ANY)],
        out_specs=tuple([_HBM] * (2 * na)), input_output_aliases={i: i for i in range(2 * na)},
        compiler_params=pltpu.CompilerParams(has_side_effects=_DATAFLOW),
    )(*srcs, *lands, send_sems, recv_sems, after)
    return list(out[:na]), list(out[na:])


def _own_slot(landed, own):
    me = 4 * lax.axis_index("x") + 2 * lax.axis_index("y") + lax.axis_index("c")
    return lax.dynamic_update_slice_in_dim(landed, own[None], me, axis=0)


def _adam_update(parts, w, m, v, name, tr=256):
    _, r, c = w.shape
    tr = _pick_rows(r, tr)
    cp = parts.shape[2]

    def body(p_ref, w_ref, m_ref, v_ref, g_ref, d_ref, nm_ref, nv_ref):
        g = p_ref[0, :, pl.ds(0, c)].astype(F32)
        for i in range(1, N_DEV):
            g = g + p_ref[i, :, pl.ds(0, c)].astype(F32)
        delta, nm, nv = _adamw(w_ref[0], g, m_ref[0], v_ref[0])
        g_ref[0] = g
        d_ref[0] = delta
        nm_ref[0] = nm
        nv_ref[0] = nv

    rs = pl.BlockSpec((1, tr, c), lambda i: (0, i, 0))
    return pl.pallas_call(
        body, grid=(r // tr,), in_specs=[pl.BlockSpec((N_DEV, tr, cp), lambda i: (0, i, 0)), rs, rs, rs],
        out_specs=[rs] * 4, out_shape=[jax.ShapeDtypeStruct((1, r, c), F32)] * 4, name=name,
        compiler_params=_params(("parallel",)))(parts, w, m, v)


def _pick_rows(rows, target):
    if rows <= target:
        return rows
    t = target
    while t >= 16:
        if rows % t == 0:
            return t
        t -= 16
    return rows


BIG = ("w_in", "w_branch_dn", "w_branch_swa", "w_out", "w_gate", "w_up", "w_down")
IN_SHARD, IN_WIRE = D_IN // N_DEV, 640
FF_SHARD, FF_WIRE = D_FF // N_DEV, 384
D_FFP = N_DEV * FF_WIRE
BIG_SHAPES = {"w_in": ((D_MODEL, IN_SHARD), (D_MODEL, IN_WIRE)),
              "w_branch_dn": ((DN_WIDTH, LANES), (DN_WIDTH, LANES)),
              "w_branch_swa": ((SWA_WIDTH, LANES), (SWA_WIDTH, LANES)),
              "w_out": ((LANES, D_MODEL), (LANES, D_MODEL)),
              "w_gate": ((D_MODEL, FF_SHARD), (D_MODEL, FF_WIRE)),
              "w_up": ((D_MODEL, FF_SHARD), (D_MODEL, FF_WIRE)),
              "w_down": ((FF_SHARD, D_MODEL), (FF_WIRE, D_MODEL))}
CONV_SHARD, CONV_WIRE = (DN_CONV, DN_QKV // N_DEV), (8, 256)


def _pad_to(a, shape):
    return jnp.pad(a, [(0, t - s) for s, t in zip(a.shape, shape)])


_IN_SEGS = ((R_GATE, 2048, P_GATE), (R_QKV, DN_QKV, P_QKV), (R_Z, DN_WIDTH, P_Z), (R_SQ, SWA_WIDTH, P_SQ),
            (R_SK, SWA_KVW, P_SK), (R_SV, SWA_KVW, P_SV), (R_B, 8, P_BA))


def _w_in_from_blocks(blocks):
    parts = []
    for rs, n, _ in _IN_SEGS:
        for dev in range(N_DEV):
            lo, hi = max(rs, IN_SHARD * dev), min(rs + n, IN_SHARD * (dev + 1))
            if lo < hi:
                parts.append(blocks[dev, :, lo - IN_SHARD * dev:hi - IN_SHARD * dev])
    parts.append(jnp.zeros((blocks.shape[1], P_WIDTH - P_BA - 8), blocks.dtype))
    return jnp.concatenate(parts, axis=1)


def _w_in_to_blocks(g):
    out = []
    for dev in range(N_DEV):
        parts = []
        for rs, n, ps in sorted(_IN_SEGS):
            lo, hi = max(rs, IN_SHARD * dev), min(rs + n, IN_SHARD * (dev + 1))
            if lo < hi:
                parts.append(g[:, ps + lo - rs:ps + hi - rs])
        parts.append(jnp.zeros((g.shape[0], IN_WIRE - IN_SHARD), g.dtype))
        out.append(jnp.concatenate(parts, axis=1))
    return jnp.stack(out)


SMALL = {"attn_norm": (0, (1, D_MODEL)), "ffn_norm": (1, (1, D_MODEL)), "dn_out_norm": (2, (1, DN_DIM)),
         "swa_q_norm": (3, (1, SWA_DIM)), "swa_k_norm": (4, (1, SWA_DIM)), "dn_a_log": (5, (1, DN_HEADS)),
         "dn_dt_bias": (6, (1, DN_HEADS)), "swa_sinks": (7, (1, SWA_HEADS)), "rel_bias": (8, (REL_BUCKETS, SWA_HEADS))}
SMALL_SHEET = (48, D_MODEL)


LOSS_ROW = 40


def _small_pack(grads, loss_local):
    names = list(SMALL)

    def body(*refs):
        o_ref = refs[-1]
        o_ref[...] = jnp.zeros_like(o_ref)
        for n, ref in zip(names, refs):
            r0, (nr, nc) = SMALL[n]
            o_ref[r0:r0 + nr, 0:nc] = ref[...]
        o_ref[LOSS_ROW:LOSS_ROW + 1, 0:1] = refs[len(names)][...]

    return pl.pallas_call(
        body, in_specs=[pl.BlockSpec(memory_space=pltpu.VMEM)] * (len(names) + 1),
        out_specs=pl.BlockSpec(memory_space=pltpu.VMEM), out_shape=jax.ShapeDtypeStruct(SMALL_SHEET, F32),
        name="small_pack", compiler_params=_params())(*[grads[n].reshape(SMALL[n][1]) for n in names], loss_local)


def _small_update(sheets, w, m, v):
    names = list(SMALL)
    k = len(names)

    def body(*refs):
        p_ref = refs[0]
        ins, outs = refs[1:1 + 3 * k], refs[1 + 3 * k:]
        loss = p_ref[0, LOSS_ROW:LOSS_ROW + 1, 0:1]
        for i in range(1, N_DEV):
            loss = loss + p_ref[i, LOSS_ROW:LOSS_ROW + 1, 0:1]
        outs[4 * k][...] = loss
        for t, n in enumerate(names):
            r0, (nr, nc) = SMALL[n]
            g = p_ref[0, r0:r0 + nr, 0:nc]
            for i in range(1, N_DEV):
                g = g + p_ref[i, r0:r0 + nr, 0:nc]
            delta, nm, nv = _adamw(ins[t][...], g, ins[k + t][...], ins[2 * k + t][...])
            for kind, val in enumerate((g, delta, nm, nv)):
                outs[kind * k + t][...] = val

    shapes = [jax.ShapeDtypeStruct(SMALL[n][1], F32) for n in names]
    vm = pl.BlockSpec(memory_space=pltpu.VMEM)
    res = pl.pallas_call(
        body, in_specs=[vm] * (1 + 3 * k), out_specs=[vm] * (4 * k + 1),
        out_shape=shapes * 4 + [jax.ShapeDtypeStruct((1, 1), F32)], name="adam_small", compiler_params=_params(),
    )(sheets, *[d[n].reshape(SMALL[n][1]) for d in (w, m, v) for n in names])
    return {n: tuple(res[kind * k + t] for kind in range(4)) for t, n in enumerate(names)}, res[4 * k]


def kernel(x, attn_norm, w_in, dn_conv, dn_a_log, dn_dt_bias, dn_out_norm, swa_q_norm, swa_k_norm, swa_sinks, rel_bias, w_branch_dn, w_branch_swa, w_out, ffn_norm, w_gate, w_up, w_down, loss_target, m_attn_norm, m_w_in, m_dn_conv, m_dn_a_log, m_dn_dt_bias, m_dn_out_norm, m_swa_q_norm, m_swa_k_norm, m_swa_sinks, m_rel_bias, m_w_branch_dn, m_w_branch_swa, m_w_out, m_ffn_norm, m_w_gate, m_w_up, m_w_down, v_attn_norm, v_w_in, v_dn_conv, v_dn_a_log, v_dn_dt_bias, v_dn_out_norm, v_swa_q_norm, v_swa_k_norm, v_swa_sinks, v_rel_bias, v_w_branch_dn, v_w_branch_swa, v_w_out, v_ffn_norm, v_w_gate, v_w_up, v_w_down):
    args = dict(locals())
    S = x.shape[1]
    xs = x.reshape(S, D_MODEL)
    target = loss_target.reshape(S, D_MODEL)

    w_loc = {n: args[n].reshape(BIG_SHAPES[n][0]) for n in BIG}
    conv_loc = dn_conv.reshape(CONV_SHARD)
    wire = {n: _pad_to(w_loc[n], BIG_SHAPES[n][1]).astype(BF16) for n in BIG}
    first = _all_gather([wire["w_in"], _pad_to(conv_loc, CONV_WIRE)])
    later = [n for n in BIG if n != "w_in"]
    rest_handle, rest_token = _exchange_start([wire[n] for n in later], False, "gather_rest_start", after=first[1])
    w_pad = _w_in_from_blocks(first[0])
    conv_w = jnp.concatenate([first[1][d, :DN_CONV, :CONV_SHARD[1]] for d in range(N_DEV)], axis=1)

    h = _norm_fwd(xs, attn_norm + rest_token[0, 0], "norm1_fwd")
    proj = _mm([(h, w_pad)], "nn", F32, "mm_in", 1024, 1664, j_outer=True)
    qkvn = _dn_conv_fwd(proj, conv_w)
    beta, g = _dn_gate_fwd(proj, dn_a_log, dn_dt_bias)
    u, w, qe, kd, qk, egl, tinv = _dn_prep_fwd(qkvn, g, beta)
    o, states = _dn_scan_fwd(u, w, qe, kd, qk, egl)
    y_dn = _dn_out_fwd(o, proj, dn_out_norm)
    bias = _bias_fwd(rel_bias)
    y_swa = _swa_fwd(proj, swa_q_norm, swa_k_norm, swa_sinks, bias)
    rest_src, rest_land = _exchange_wait(rest_handle, y_swa, False, "gather_rest_wait")
    G = {n: _own_slot(land, src) for n, src, land in zip(later, rest_src, rest_land)}
    w_bdn, w_bswa, w_g, w_u = G["w_branch_dn"], G["w_branch_swa"], G["w_gate"], G["w_up"]
    w_o = G["w_out"].reshape(D_MODEL, D_MODEL)
    w_d = G["w_down"].reshape(D_FFP, D_MODEL)
    gates = [(proj, P_GATE // 512), (proj, (P_GATE + D_MODEL) // 512)]
    a_dn, a_swa, merged = _mm_fused(
        [(y_dn, w_bdn), (y_swa, w_bswa)], "nn", "mm_branch_merge", 1024, 512,
        lambda p, e: (p[0], p[1], _merge(e[0], e[1], p[0], p[1])), gates, (F32, F32, BF16), b_blocks=True)

    def resid_norm(p, e):
        x1 = e[0] + p[0]
        return x1, _rms(x1, e[1])

    x1, h2 = _mm_fused([(merged, w_o)], "nn", "mm_out_norm", 512, D_MODEL, resid_norm,
                       [(xs, 0), (ffn_norm, None)], (F32, BF16))
    gate, up, act = _mm_fused([(h2, w_g), (h2, w_u)], "nn", "mm_gate_up_act", 1024, 768,
                              lambda p, e: (p[0], p[1], _act(p[0], p[1])), [], (F32, F32, BF16),
                              j_outer=True, b_blocks=True)

    def loss_head(p, e):
        diff = e[0] + p[0] - e[1]
        dy = diff * (1.0 / D_MODEL)
        part = jnp.sum(jnp.mean(diff * diff, axis=-1, keepdims=True), axis=0, keepdims=True) * 0.5
        return dy, dy, part

    dy, dy_b, loss_local = _mm_fused([(act, w_d)], "nn", "mm_down_loss", 512, D_MODEL, loss_head,
                                     [(x1, 0), (target, 0)], (F32, BF16), sum_shape=(1, 1))

    def act_bwd(p, e):
        _, vjp = jax.vjp(_act, e[0], e[1])
        return vjp(p[0])

    dgate, dup = _mm_fused([(dy_b, w_d)], "nt", "mm_dact_act", 1024, 768, act_bwd, [(gate, 0), (up, 0)],
                           (BF16, BF16), j_outer=True)
    g_w_down = _mm([(act, dy_b)], "tn", BF16, "mm_dw_down", 768, D_MODEL, j_outer=True)
    g_w_down = g_w_down.reshape(N_DEV, FF_WIRE, D_MODEL)
    g_w_gate = _mm([(h2, dgate)], "tn", BF16, "mm_dw_gate", D_MODEL, 768, out_blocks=True)
    g_w_up = _mm([(h2, dup)], "tn", BF16, "mm_dw_up", D_MODEL, 768, out_blocks=True)
    ffn_handle, ffn_token = _exchange_start([g_w_down, g_w_gate, g_w_up], True, "scatter_ffn_start")

    def norm_bwd(p, e):
        _, vjp = jax.vjp(_rms, e[0], e[2])
        dx, dgain = vjp(sum(p))
        dx = dx + e[1]
        return dx, dx, dgain

    dx1, dx1_b, g_ffn_norm = _mm_fused(
        [(dgate, w_g), (dup, w_u)], "nt", "mm_dh2_norm", 256, D_MODEL, norm_bwd,
        [(x1, 0), (dy, 0), (ffn_norm + ffn_token[0, 0], None)], (F32, BF16), b_blocks=True, sum_shape=(1, D_MODEL))
    def merge_bwd(p, e):
        _, vjp = jax.vjp(_merge, *e)
        dg0, dg1, da_dn, da_swa = vjp(p[0])
        return jnp.concatenate([dg0, dg1], axis=1), da_dn, da_swa

    dproj, da_dn, da_swa = _mm_fused(
        [(dx1_b, w_o)], "nt", "mm_dmerged_merge", 512, D_MODEL, merge_bwd,
        [(proj, P_GATE // D_MODEL), (proj, P_GATE // D_MODEL + 1), (a_dn, 0), (a_swa, 0)], (BF16,) * 3,
        wide_first=(P_WIDTH, 2 * D_MODEL))
    g_w_out = _mm([(merged, dx1_b)], "tn", BF16, "mm_dw_out", 512, D_MODEL, j_outer=True)
    g_w_out = g_w_out.reshape(N_DEV, LANES, D_MODEL)
    dy_dn = _mm([(da_dn, w_bdn)], "nt", F32, "mm_dy_dn", 1024, DN_WIDTH, b_blocks=True)
    dy_swa = _mm([(da_swa, w_bswa)], "nt", F32, "mm_dy_swa", 1024, SWA_WIDTH, b_blocks=True)
    g_w_bdn = _mm([(y_dn, da_dn)], "tn", BF16, "mm_dw_branch_dn", DN_WIDTH, 512, out_blocks=True)
    g_w_bswa = _mm([(y_swa, da_swa)], "tn", BF16, "mm_dw_branch_swa", SWA_WIDTH, 512, out_blocks=True)
    dproj, dsk, dsv, g_q_norm, g_k_norm, g_sinks, dbias = _swa_bwd(proj, swa_q_norm, swa_k_norm, swa_sinks, bias,
                                                                   dy_swa, dproj)
    dproj = _kv_into(dproj, dsk, dsv)
    g_rel_bias = _bias_bwd(dbias)[:, :REL_BUCKETS].T
    mix_handle, mix_token = _exchange_start([g_w_out, g_w_bdn, g_w_bswa], True, "scatter_mix_start")
    do, dproj, g_out_norm = _dn_out_bwd(o, proj, dn_out_norm + mix_token[0, 0], dy_dn, dproj)
    du, dw, dqe, dkd, dqk, degl = _dn_scan_bwd(u, w, qe, kd, qk, egl, states, do)
    dqkvn, dgd, dbeta = _dn_prep_bwd(qkvn, g, beta, tinv, du, dw, dqe, dkd, dqk, degl)
    dproj, dal, ddt = _dn_gate_bwd(proj, dn_a_log, dn_dt_bias, dbeta, dgd, dproj)
    g_a_log = dal.reshape(DN_HEADS, DN_DIM).sum(axis=1)
    g_dt_bias = ddt[0, DN_HEADS:2 * DN_HEADS]
    dproj, g_conv = _dn_conv_bwd(proj, conv_w, dqkvn, dproj)
    g_w_in = _w_in_to_blocks(_mm([(h, dproj)], "tn", BF16, "mm_dw_in", 512, 1664, j_outer=True))
    in_handle, in_token = _exchange_start([g_w_in], True, "scatter_in_start")
    dx, g_attn_norm = _mm_fused(
        [(dproj, w_pad)], "nt", "mm_dh_norm", 512, D_MODEL, lambda p, e: norm_bwd(p, e)[1:],
        [(xs, 0), (dx1, 0), (attn_norm + in_token[0, 0], None)], (F32,), sum_shape=(1, D_MODEL))

    g_small = {"attn_norm": g_attn_norm, "ffn_norm": g_ffn_norm, "rel_bias": g_rel_bias, "dn_out_norm": g_out_norm,
               "swa_q_norm": g_q_norm, "swa_k_norm": g_k_norm, "dn_a_log": g_a_log, "dn_dt_bias": g_dt_bias,
               "swa_sinks": g_sinks}
    me = 4 * lax.axis_index("x") + 2 * lax.axis_index("y") + lax.axis_index("c")
    outs = {}

    def finish(handle, group, name, after):
        srcs, lands = _exchange_wait(handle, after, True, name)
        for n, src, land in zip(group, srcs, lands):
            parts = _own_slot(land, lax.dynamic_index_in_dim(src, me, 0, keepdims=False))
            outs[n] = _adam_update(parts, args[n], args["m_" + n], args["v_" + n], "adam_" + n)

    finish(ffn_handle, ("w_down", "w_gate", "w_up"), "scatter_ffn_wait", dx)
    finish(mix_handle, ("w_out", "w_branch_dn", "w_branch_swa"), "scatter_mix_wait", dx)
    sheets, conv_all = _all_gather_direct([_small_pack(g_small, loss_local), _pad_to(g_conv, (8, DN_QKV))],
                                          "all_gather_small",
                                          after=outs["w_up"][0])
    finish(in_handle, ("w_in",), "scatter_in_wait", sheets)
    conv_parts = lax.dynamic_slice(conv_all, (0, 0, me * CONV_SHARD[1]), (N_DEV,) + CONV_SHARD)
    outs["dn_conv"] = _adam_update(conv_parts, dn_conv, m_dn_conv, v_dn_conv, "adam_dn_conv")
    small_outs, loss = _small_update(sheets, {n: args[n] for n in SMALL}, {n: args["m_" + n] for n in SMALL},
                                     {n: args["v_" + n] for n in SMALL})
    outs.update(small_outs)

    names = ("attn_norm", "w_in", "dn_conv", "dn_a_log", "dn_dt_bias", "dn_out_norm", "swa_q_norm", "swa_k_norm",
             "swa_sinks", "rel_bias", "w_branch_dn", "w_branch_swa", "w_out", "ffn_norm", "w_gate", "w_up", "w_down")
    results = []
    for kind in range(4):
        results += [outs[n][kind].reshape(args[n].shape) for n in names]

    return (loss.reshape(()), dx.reshape(x.shape), *results)
```

```python
import math

import numpy as np
import jax
import jax.numpy as jnp
from jax import lax
from jax.experimental import pallas as pl
from jax.experimental.pallas import tpu as pltpu

F32 = jnp.float32
BF16 = jnp.bfloat16
HI = lax.Precision.HIGHEST

D_MODEL = 1024
DN_HEADS = 4
DN_DIM = 128
DN_WIDTH = 512
DN_QKV = 1536
DN_CONV = 4
CHUNK = 64
SWA_HEADS = 8
SWA_KV = 2
SWA_GROUP = 4
SWA_DIM = 64
SWA_WIDTH = 512
SWA_KVW = 128
WINDOW = 128
BLOCK = 128
REL_BUCKETS = 32
REL_MAX_DIST = 128
D_FF = 2816
D_IN = 4872
EPS = 1e-6
N_DEV = 8

ADAM_LR = 0.001
ADAM_B1 = 0.9
ADAM_B2 = 0.999
ADAM_EPS = 1e-08
ADAM_WD = 0.01
ADAM_STEP = 10

P_GATE, P_QKV, P_Z, P_SQ, P_SK, P_SV, P_BA = 0, 2048, 3584, 4096, 4608, 4736, 4864
P_WIDTH = 4992
R_QKV, R_Z, R_B, R_A, R_SQ, R_SK, R_SV, R_GATE = 0, 1536, 2048, 2052, 2056, 2568, 2696, 2824

VMEM_LIMIT = 56 * 1024 * 1024
LANES = 128
MESH_ID = pl.DeviceIdType.MESH


def _params(sem=None):
    return pltpu.CompilerParams(dimension_semantics=sem, vmem_limit_bytes=VMEM_LIMIT)


def _pick(dim, target):
    if dim <= target:
        return dim
    t = target - target % LANES
    while t >= LANES:
        if dim % t == 0:
            return t
        t -= LANES
    return dim


_DIMS = {"nn": (((1,), (0,)), ((), ())), "nt": (((1,), (1,)), ((), ())), "tn": (((0,), (0,)), ((), ()))}


def _tile_product(a_ref, b_ref, mode, b_blocks):
    a = a_ref[...].astype(BF16)
    b = jnp.concatenate([b_ref[d] for d in range(b_ref.shape[0])], axis=1) if b_blocks else b_ref[...]
    return lax.dot_general(a, b.astype(BF16), _DIMS[mode], preferred_element_type=F32)


def _mm(pairs, mode, out_dtype, name, bm, bn, j_outer=False, b_blocks=False, out_blocks=False):
    a0, b0 = pairs[0]
    cb = b0.shape[2] if b_blocks else None
    b_shape = (b0.shape[1], N_DEV * cb) if b_blocks else b0.shape
    if mode == "nn":
        (M, K), (K2, N) = a0.shape, b_shape
    elif mode == "nt":
        (M, K), (N, K2) = a0.shape, b_shape
    else:
        (K, M), (K2, N) = a0.shape, b_shape
    bm, bn = min(bm, M), min(bn, N)
    assert K == K2 and M % bm == 0 and N % bn == 0, (name, a0.shape, b0.shape, bm, bn)
    co = N // N_DEV
    assert not out_blocks or bn % co == 0
    dims = _DIMS[mode]
    n = len(pairs)

    def body(*refs):
        o_ref = refs[2 * n]
        acc = None
        for t in range(n):
            p = _tile_product(refs[2 * t], refs[2 * t + 1], mode, b_blocks)
            acc = p if acc is None else acc + p
        if out_blocks:
            for d in range(bn // co):
                o_ref[d] = acc[:, d * co:(d + 1) * co].astype(out_dtype)
        else:
            o_ref[...] = acc.astype(out_dtype)

    def ij(f):
        return (lambda j, i: f(i, j)) if j_outer else f

    a_spec = pl.BlockSpec((K, bm), ij(lambda i, j: (0, i))) if mode == "tn" else pl.BlockSpec((bm, K), ij(lambda i, j: (i, 0)))
    if b_blocks and mode == "nt":
        b_spec = pl.BlockSpec((N_DEV, bn, cb), ij(lambda i, j: (0, j, 0)))
    elif b_blocks:
        b_spec = pl.BlockSpec((bn // cb, K, cb), ij(lambda i, j: (j, 0, 0)))
    elif mode == "nt":
        b_spec = pl.BlockSpec((bn, K), ij(lambda i, j: (j, 0)))
    else:
        b_spec = pl.BlockSpec((K, bn), ij(lambda i, j: (0, j)))
    if out_blocks:
        out_spec = pl.BlockSpec((bn // co, bm, co), ij(lambda i, j: (j, i, 0)))
        out_shape = jax.ShapeDtypeStruct((N_DEV, M, co), out_dtype)
    else:
        out_spec = pl.BlockSpec((bm, bn), ij(lambda i, j: (i, j)))
        out_shape = jax.ShapeDtypeStruct((M, N), out_dtype)
    grid = (N // bn, M // bm) if j_outer else (M // bm, N // bn)
    return pl.pallas_call(
        body, grid=grid, in_specs=[a_spec, b_spec] * n, out_specs=out_spec, out_shape=out_shape, name=name,
        compiler_params=_params(("parallel", "parallel")),
    )(*[x for pair in pairs for x in pair])


def _mm_fused(pairs, mode, name, bm, bn, epilogue, extras, out_dtypes, j_outer=False, b_blocks=False,
              sum_shape=None, wide_first=None):
    a0, b0 = pairs[0]
    cb = b0.shape[2] if b_blocks else None
    b_shape = (b0.shape[1], N_DEV * cb) if b_blocks else b0.shape
    if mode == "nn":
        (M, K), (K2, N) = a0.shape, b_shape
    else:
        (M, K), (N, K2) = a0.shape, b_shape
    bm, bn = min(bm, M), min(bn, N)
    assert mode in ("nn", "nt") and K == K2 and M % bm == 0 and N % bn == 0, (name, a0.shape, b0.shape)
    dims = _DIMS[mode]
    n, ne, no = len(pairs), len(extras), len(out_dtypes)

    def body(*refs):
        prods = [_tile_product(refs[2 * t], refs[2 * t + 1], mode, b_blocks) for t in range(n)]
        results = epilogue(prods, [r[...] for r in refs[2 * n:2 * n + ne]])
        out_refs = refs[2 * n + ne:]
        for o_ref, val, dt in zip(out_refs, results, out_dtypes):
            o_ref[...] = val.astype(dt)
        if sum_shape is not None:
            s_ref = out_refs[no]

            @pl.when((pl.program_id(0) == 0) & (pl.program_id(1) == 0))
            def _():
                s_ref[...] = jnp.zeros_like(s_ref)

            s_ref[...] += results[no]

    def ij(f):
        return (lambda j, i: f(i, j)) if j_outer else f

    a_spec = pl.BlockSpec((bm, K), ij(lambda i, j: (i, 0)))
    once = dict(pipeline_mode=pl.Buffered(1)) if bn == N else {}
    if b_blocks and mode == "nt":
        b_spec = pl.BlockSpec((N_DEV, bn, cb), ij(lambda i, j: (0, j, 0)), **once)
    elif b_blocks:
        b_spec = pl.BlockSpec((bn // cb, K, cb), ij(lambda i, j: (j, 0, 0)), **once)
    elif mode == "nt":
        b_spec = pl.BlockSpec((bn, K), ij(lambda i, j: (j, 0)), **once)
    else:
        b_spec = pl.BlockSpec((K, bn), ij(lambda i, j: (0, j)), **once)
    e_specs = [pl.BlockSpec((1, bn), ij(lambda i, j: (0, j))) if first is None
               else pl.BlockSpec((bm, bn), ij(lambda i, j, first=first: (i, first + j))) for _, first in extras]
    tile = pl.BlockSpec((bm, bn), ij(lambda i, j: (i, j)))
    out_specs = [tile] * no
    out_shape = [jax.ShapeDtypeStruct((M, N), dt) for dt in out_dtypes]
    if wide_first is not None:
        assert bn == N
        out_specs[0] = pl.BlockSpec((bm, wide_first[1]), ij(lambda i, j: (i, 0)))
        out_shape[0] = jax.ShapeDtypeStruct((M, wide_first[0]), out_dtypes[0])
    if sum_shape is not None:
        assert sum_shape[1] in (1, bn) and (sum_shape[1] == 1 or bn == N)
        out_specs.append(_full(sum_shape))
        out_shape.append(jax.ShapeDtypeStruct(sum_shape, F32))
    grid = (N // bn, M // bm) if j_outer else (M // bm, N // bn)
    sem = ("arbitrary", "arbitrary") if sum_shape is not None else ("parallel", "parallel")
    return pl.pallas_call(
        body, grid=grid, in_specs=[a_spec, b_spec] * n + e_specs, out_specs=out_specs, out_shape=out_shape,
        name=name, compiler_params=_params(sem),
    )(*[x for pair in pairs for x in pair], *[arr for arr, _ in extras])


def _rms(x, gain):
    return x * lax.rsqrt(jnp.mean(x * x, axis=-1, keepdims=True) + EPS) * gain


def _silu(x):
    return x * jax.nn.sigmoid(x)


def _act(g, u):
    return _silu(g) * u


def _merge(g0, g1, a_dn, a_swa):
    return jax.nn.sigmoid(g0) * a_dn + jax.nn.sigmoid(g1) * a_swa


def _dn_post(c, is_v, q_scale):
    a = _silu(c)
    rs = lax.rsqrt(jnp.sum(a * a, axis=-1, keepdims=True) + EPS) * q_scale
    return a * jnp.where(is_v, 1.0, rs)


def _dn_out(o, z, gain):
    return _rms(o, gain) * _silu(z)


def _dot(a, b, dims=_DIMS["nn"], hi=False):
    if a.ndim == 3 or b.ndim == 3:
        batch = a.shape[0] if a.ndim == 3 else b.shape[0]
        a = a if a.ndim == 3 else jnp.broadcast_to(a, (batch,) + a.shape)
        b = b if b.ndim == 3 else jnp.broadcast_to(b, (batch,) + b.shape)
        ((ca,), (cb,)), _ = dims
        dims = (((ca + 1,), (cb + 1,)), ((0,), (0,)))
    if hi:
        return lax.dot_general(a, b, dims, precision=HI, preferred_element_type=F32)
    return lax.dot_general(a.astype(BF16), b.astype(BF16), dims, preferred_element_type=F32)


def _pieces(x):
    hi = x.astype(BF16)
    r1 = x - hi.astype(F32)
    mid = r1.astype(BF16)
    return hi, mid, (r1 - mid.astype(F32)).astype(BF16)


def _sel_left_impl(m, x):
    mb = m.astype(BF16)
    hi, mid, lo = _pieces(x)
    return _dot(mb, hi) + (_dot(mb, mid) + _dot(mb, lo))


@jax.custom_vjp
def _sel_left(m, mt, x):
    return _sel_left_impl(m, x)


_sel_left.defvjp(lambda m, mt, x: (_sel_left_impl(m, x), (m, mt)),
                 lambda res, ct: (jnp.zeros_like(res[0]), jnp.zeros_like(res[1]), _sel_left_impl(res[1], ct)))


def _sel_right_impl(x, s):
    sb = s.astype(BF16)
    hi, mid, lo = _pieces(x)
    return _dot(hi, sb) + (_dot(mid, sb) + _dot(lo, sb))


@jax.custom_vjp
def _sel_right(x, s, st):
    return _sel_right_impl(x, s)


_sel_right.defvjp(lambda x, s, st: (_sel_right_impl(x, s), (s, st)),
                  lambda res, ct: (_sel_right_impl(ct, res[1]), jnp.zeros_like(res[0]), jnp.zeros_like(res[1])))


def _sel_nt_impl(s, x):
    sb = s.astype(BF16)
    hi, mid, lo = _pieces(x)
    return _dot(sb, hi, _DIMS["nt"]) + (_dot(sb, mid, _DIMS["nt"]) + _dot(sb, lo, _DIMS["nt"]))


def _sel_tn_impl(x, s):
    sb = s.astype(BF16)
    hi, mid, lo = _pieces(x)
    return _dot(hi, sb, _DIMS["tn"]) + (_dot(mid, sb, _DIMS["tn"]) + _dot(lo, sb, _DIMS["tn"]))


@jax.custom_vjp
def _sel_nt(s, x):
    return _sel_nt_impl(s, x)


_sel_nt.defvjp(lambda s, x: (_sel_nt_impl(s, x), s),
               lambda s, ct: (jnp.zeros_like(s), _sel_tn_impl(ct, s)))


def _dot3_impl(a, b):
    a_hi, a_lo, _ = _pieces(a)
    b_hi, b_lo, _ = _pieces(b)
    return _dot(a_hi, b_hi) + (_dot(a_hi, b_lo) + _dot(a_lo, b_hi))


@jax.custom_vjp
def _dot3(a, b):
    return _dot3_impl(a, b)


_dot3.defvjp(lambda a, b: (_dot3_impl(a, b), (a, b)),
             lambda res, ct: (_dot(ct, res[1], _DIMS["nt"]), _dot(res[0], ct, _DIMS["tn"])))


def _inv_impl(a, eye, strict):
    t = eye - a
    p = _dot(a, a)
    for level in range(5):
        t = t + _dot(t, p)
        if level < 4:
            p = _dot(p, p)
    t = t + _dot(t, eye - t - _dot3_impl(a, t))
    return jnp.where(strict > 0.5, t, eye)


@jax.custom_vjp
def _inv_given(a, t):
    return t.astype(F32)


_inv_given.defvjp(lambda a, t: (t.astype(F32), t),
                  lambda t, ct: (-_dot(_dot(t, ct, _DIMS["tn"]), t, _DIMS["nt"]), jnp.zeros_like(t)))


@jax.custom_vjp
def _lanes_join(a, b):
    return jnp.concatenate([a, b], axis=-1)


_lanes_join.defvjp(lambda a, b: (jnp.concatenate([a, b], axis=-1), None),
                   lambda _, ct: (ct[..., :ct.shape[-1] // 2], ct[..., ct.shape[-1] // 2:]))


@jax.custom_vjp
def _lanes_halves(y):
    h = y.shape[-1] // 2
    return y[..., :h], y[..., h:]


_lanes_halves.defvjp(lambda y: ((y[..., :y.shape[-1] // 2], y[..., y.shape[-1] // 2:]), None),
                     lambda _, ct: (jnp.concatenate(ct, axis=-1),))

GROUP = 4
GROUP_ROWS = GROUP * CHUNK


def _block_consts(n):
    ii = lax.broadcasted_iota(jnp.int32, (n, n), 0)
    jj = lax.broadcasted_iota(jnp.int32, (n, n), 1)
    shift = CHUNK.bit_length() - 1
    same = jnp.right_shift(ii, shift) == jnp.right_shift(jj, shift)
    return same & (ii >= jj), same & (ii <= jj), same & (ii > jj), same, ii == jj


def _lane0(n):
    s = (lax.broadcasted_iota(jnp.int32, (LANES, n), 0) == 0).astype(F32)
    st = (lax.broadcasted_iota(jnp.int32, (n, LANES), 1) == 0).astype(F32)
    return s, st


def _dn_group(q, k, v, g, beta, t_saved=None):
    n = GROUP_ROWS
    low_b, upp_b, strict_b, _, eye_b = _block_consts(n)
    low, upp, eye = low_b.astype(F32), upp_b.astype(F32), eye_b.astype(F32)
    gc = _sel_left(low, upp, g)
    per_chunk = (g.shape[0], GROUP, CHUNK, LANES)
    gl = jnp.broadcast_to(jnp.sum(g.reshape(per_chunk), axis=2, keepdims=True), per_chunk).reshape(g.shape)
    s, st = _lane0(n)
    col = _sel_right(gc, s, st)
    row = _sel_nt(st, gc)
    decay = jnp.exp(jnp.where(low_b, col - row, -jnp.inf))
    kb = k * beta
    vb = v * beta
    a = jnp.where(strict_b, _dot(kb, k, _DIMS["nt"]) * decay, 0.0)
    t = _inv_impl(a, eye, strict_b.astype(F32)) if t_saved is None else _inv_given(a, t_saved)
    u, w = _lanes_halves(_dot3(t, _lanes_join(vb, kb * jnp.exp(gc))))
    return u, w, q * jnp.exp(gc), k * jnp.exp(gl - gc), t


def _dn_chunk(q, k, g):
    ii = lax.broadcasted_iota(jnp.int32, (CHUNK, CHUNK), 0)
    jj = lax.broadcasted_iota(jnp.int32, (CHUNK, CHUNK), 1)
    low = (ii >= jj).astype(F32)
    upp = (ii <= jj).astype(F32)
    s, st = _lane0(CHUNK)
    gc = _sel_left(low, upp, g)
    col = _sel_right(gc, s, st)
    row = _sel_nt(st, gc)
    decay = jnp.exp(jnp.where(ii >= jj, col - row, -jnp.inf))
    qk = _dot(q, k, _DIMS["nt"]) * decay
    return qk, jnp.exp(jnp.sum(g, axis=-2, keepdims=True))


def _dn_step(s, u, w, qe, kd, qk, egl):
    v_new = u - _dot(w, s)
    o = _dot(qe, s) + _dot(qk, v_new)
    s_new = s * egl + _dot(kd, v_new, _DIMS["tn"])
    return s_new, o


def _swa_block(q, kband, vband, qg, kg, sinks, band):
    kn = _rms(kband, kg)
    qn = _rms(q, qg) * (SWA_DIM ** -0.5)
    logits = _dot(qn, kn, _DIMS["nt"]) + band
    m = lax.stop_gradient(jnp.maximum(jnp.max(logits, axis=-1, keepdims=True), sinks))
    p = jnp.exp(logits - m)
    denom = jnp.sum(p, axis=-1, keepdims=True) + jnp.exp(sinks - m)
    return _dot(p * (1.0 / denom), vband)


def _adamw(w, g, m, v):
    m = ADAM_B1 * m + (1.0 - ADAM_B1) * g
    v = ADAM_B2 * v + (1.0 - ADAM_B2) * jnp.square(g)
    m_hat = m / (1.0 - ADAM_B1 ** ADAM_STEP)
    v_hat = v / (1.0 - ADAM_B2 ** ADAM_STEP)
    delta = -ADAM_LR * (m_hat / (jnp.sqrt(v_hat) + ADAM_EPS) + ADAM_WD * w)
    return delta, m, v


def _row(tm, c, cb=0):
    return pl.BlockSpec((tm, c), lambda i, cb=cb: (i, cb))


def _full(shape):
    nd = len(shape)
    return pl.BlockSpec(shape, lambda *_, nd=nd: (0,) * nd)


def _norm_fwd(x, gain, name, tm=1024):
    S = x.shape[0]

    def body(x_ref, g_ref, h_ref):
        h_ref[...] = _rms(x_ref[...], g_ref[...]).astype(BF16)

    return pl.pallas_call(
        body, grid=(S // tm,), in_specs=[_row(tm, D_MODEL), _full((1, D_MODEL))],
        out_specs=_row(tm, D_MODEL), out_shape=jax.ShapeDtypeStruct((S, D_MODEL), BF16),
        name=name, compiler_params=_params(("parallel",)))(x, gain)


def _shift_down(x, s):
    row = lax.broadcasted_iota(jnp.int32, x.shape, 0)
    return jnp.where(row >= s, pltpu.roll(x, s, axis=0), 0.0)


def _shift_up(x, s):
    n = x.shape[0]
    row = lax.broadcasted_iota(jnp.int32, x.shape, 0)
    return jnp.where(row < n - s, pltpu.roll(x, n - s, axis=0), 0.0)


def _conv(x, w):
    out = w[DN_CONV - 1:DN_CONV] * x
    for s in range(1, DN_CONV):
        out = out + w[DN_CONV - 1 - s:DN_CONV - s] * _shift_down(x, s)
    return out


def _dn_conv_fwd(proj, conv_w):
    S = proj.shape[0]
    nb = DN_QKV // LANES

    def body(x_ref, w_ref, o_ref):
        j = pl.program_id(0)
        q_scale = jnp.where(j < DN_HEADS, DN_DIM ** -0.5, 1.0).astype(F32)
        o_ref[...] = _dn_post(_conv(x_ref[...], w_ref[...]), j >= 2 * DN_HEADS, q_scale)

    return pl.pallas_call(
        body, grid=(nb,),
        in_specs=[pl.BlockSpec((S, LANES), lambda j: (0, P_QKV // LANES + j)),
                  pl.BlockSpec((DN_CONV, LANES), lambda j: (0, j))],
        out_specs=pl.BlockSpec((S, LANES), lambda j: (0, j)),
        out_shape=jax.ShapeDtypeStruct((S, DN_QKV), F32), name="dn_conv_fwd",
        compiler_params=_params(("parallel",)))(proj, conv_w)


def _dn_conv_bwd(proj, conv_w, dqkvn, dproj):
    S = proj.shape[0]
    nb = DN_QKV // LANES

    def body(x_ref, w_ref, d_ref, _, dx_ref, dw_ref):
        j = pl.program_id(0)
        q_scale = jnp.where(j < DN_HEADS, DN_DIM ** -0.5, 1.0).astype(F32)
        x = x_ref[...]
        w = w_ref[...]
        _, vjp = jax.vjp(lambda c: _dn_post(c, j >= 2 * DN_HEADS, q_scale), _conv(x, w))
        (dc,) = vjp(d_ref[0])
        dx = w[DN_CONV - 1:DN_CONV] * dc
        dw_ref[DN_CONV - 1:DN_CONV, :] = jnp.sum(dc * x, axis=0, keepdims=True)
        for s in range(1, DN_CONV):
            dx = dx + w[DN_CONV - 1 - s:DN_CONV - s] * _shift_up(dc, s)
            dw_ref[DN_CONV - 1 - s:DN_CONV - s, :] = jnp.sum(dc * _shift_down(x, s), axis=0, keepdims=True)
        dx_ref[...] = dx.astype(BF16)

    return pl.pallas_call(
        body, grid=(nb,),
        in_specs=[pl.BlockSpec((S, LANES), lambda j: (0, P_QKV // LANES + j)),
                  pl.BlockSpec((DN_CONV, LANES), lambda j: (0, j)),
                  pl.BlockSpec((1, S, LANES), lambda j: (lax.div(j, DN_HEADS), 0, lax.rem(j, DN_HEADS))),
                  pl.BlockSpec(memory_space=pl.ANY)],
        out_specs=[pl.BlockSpec((S, LANES), lambda j: (0, P_QKV // LANES + j)),
                   pl.BlockSpec((DN_CONV, LANES), lambda j: (0, j))],
        out_shape=[jax.ShapeDtypeStruct(dproj.shape, dproj.dtype), jax.ShapeDtypeStruct((DN_CONV, DN_QKV), F32)],
        input_output_aliases={3: 0},
        name="dn_conv_bwd", compiler_params=_params(("parallel",)))(proj, conv_w, dqkvn, dproj)


def _expanders():
    eb = np.zeros((LANES, DN_WIDTH), np.float32)
    ea = np.zeros((LANES, DN_WIDTH), np.float32)
    for h in range(DN_HEADS):
        eb[h, h * DN_DIM:(h + 1) * DN_DIM] = 1.0
        ea[DN_HEADS + h, h * DN_DIM:(h + 1) * DN_DIM] = 1.0
    return jnp.asarray(eb), jnp.asarray(ea), jnp.asarray(eb.T), jnp.asarray(ea.T)


def _dn_gate_args(a_log, dt_bias):
    alog = jnp.repeat(a_log.reshape(1, DN_HEADS), DN_DIM, axis=1)
    dtb = _pad_to(jnp.pad(dt_bias.reshape(1, DN_HEADS), ((0, 0), (DN_HEADS, 0))), (1, LANES))
    return _expanders() + (alog, dtb)


def _dn_gate_specs(tm):
    return [_row(tm, LANES, P_BA // LANES), _full((LANES, DN_WIDTH)), _full((LANES, DN_WIDTH)),
            _full((DN_WIDTH, LANES)), _full((DN_WIDTH, LANES)), _full((1, DN_WIDTH)), _full((1, LANES))]


def _dn_gate_fn(ba, eb, ea, ebt, eat, alog, dtb):
    beta = _sel_right(jax.nn.sigmoid(ba), eb, ebt)
    g = -jnp.exp(alog) * _sel_right(jax.nn.softplus(ba + dtb), ea, eat)
    return beta, g


def _dn_gate_fwd(proj, a_log, dt_bias, tm=1024):
    S = proj.shape[0]
    args = _dn_gate_args(a_log, dt_bias)

    def body(ba_ref, eb_ref, ea_ref, ebt_ref, eat_ref, al_ref, dt_ref, beta_ref, g_ref):
        beta, g = _dn_gate_fn(ba_ref[...], eb_ref[...], ea_ref[...], ebt_ref[...], eat_ref[...], al_ref[...],
                              dt_ref[...])
        beta_ref[...] = beta
        g_ref[...] = g

    return pl.pallas_call(
        body, grid=(S // tm,), in_specs=_dn_gate_specs(tm), out_specs=[_row(tm, DN_WIDTH), _row(tm, DN_WIDTH)],
        out_shape=[jax.ShapeDtypeStruct((S, DN_WIDTH), F32), jax.ShapeDtypeStruct((S, DN_WIDTH), F32)],
        name="dn_gate_fwd", compiler_params=_params(("parallel",)))(proj, *args)


def _dn_gate_bwd(proj, a_log, dt_bias, dbeta, dg, dproj, tm=1024):
    S = proj.shape[0]
    args = _dn_gate_args(a_log, dt_bias)

    def body(ba_ref, eb_ref, ea_ref, ebt_ref, eat_ref, al_ref, dt_ref, dbeta_ref, dg_ref, _, dba_ref, dal_ref,
             ddt_ref):
        eb, ea, ebt, eat = eb_ref[...], ea_ref[...], ebt_ref[...], eat_ref[...]
        _, vjp = jax.vjp(lambda ba, al, dt: _dn_gate_fn(ba, eb, ea, ebt, eat, al, dt), ba_ref[...], al_ref[...],
                         dt_ref[...])
        dba, dal, ddt = vjp((dbeta_ref[...], dg_ref[...]))
        dba_ref[...] = dba.astype(BF16)

        @pl.when(pl.program_id(0) == 0)
        def _():
            dal_ref[...] = jnp.zeros_like(dal_ref)
            ddt_ref[...] = jnp.zeros_like(ddt_ref)

        dal_ref[...] += dal
        ddt_ref[...] += ddt

    return pl.pallas_call(
        body, grid=(S // tm,),
        in_specs=_dn_gate_specs(tm) + [_row(tm, DN_WIDTH), _row(tm, DN_WIDTH), pl.BlockSpec(memory_space=pl.ANY)],
        out_specs=[_row(tm, LANES, P_BA // LANES), _full((1, DN_WIDTH)), _full((1, LANES))],
        out_shape=[jax.ShapeDtypeStruct(dproj.shape, dproj.dtype), jax.ShapeDtypeStruct((1, DN_WIDTH), F32),
                   jax.ShapeDtypeStruct((1, LANES), F32)],
        input_output_aliases={len(args) + 3: 0},
        name="dn_gate_bwd", compiler_params=_params(("arbitrary",)))(proj, *args, dbeta, dg, dproj)


PREP_GROUPS = 4
PREP_CHUNKS = GROUP * PREP_GROUPS


def _dn_prep_specs():
    rows = PREP_CHUNKS * CHUNK
    q = pl.BlockSpec((rows, LANES), lambda h, c: (c, h))
    k = pl.BlockSpec((rows, LANES), lambda h, c: (c, DN_HEADS + h))
    v = pl.BlockSpec((rows, LANES), lambda h, c: (c, 2 * DN_HEADS + h))
    qk = pl.BlockSpec((1, rows, CHUNK), lambda h, c: (h, c, 0))
    egl = pl.BlockSpec((1, PREP_CHUNKS, 1, LANES), lambda h, c: (h, c, 0, 0))
    return q, k, v, qk, egl


def _dn_prep_fwd(qkvn, g, beta):
    S = qkvn.shape[0]
    nc = S // CHUNK
    q, k, v, qks, egl = _dn_prep_specs()

    def body(q_ref, k_ref, v_ref, g_ref, b_ref, u_ref, w_ref, qe_ref, kd_ref, qk_ref, egl_ref, t_ref):
        rows = PREP_CHUNKS * CHUNK
        grp = (PREP_GROUPS, GROUP_ROWS, LANES)
        chk = (PREP_CHUNKS, CHUNK, LANES)
        q, k, g = q_ref[...], k_ref[...], g_ref[...]
        u, w, qe, kd, t = _dn_group(q.reshape(grp), k.reshape(grp), v_ref[...].reshape(grp), g.reshape(grp),
                                    b_ref[...].reshape(grp))
        u_ref[...] = u.reshape(rows, LANES)
        w_ref[...] = w.reshape(rows, LANES)
        qe_ref[...] = qe.reshape(rows, LANES)
        kd_ref[...] = kd.reshape(rows, LANES)
        t_ref[0] = t.reshape(rows, GROUP_ROWS).astype(BF16)
        qk, e = _dn_chunk(q.reshape(chk), k.reshape(chk), g.reshape(chk))
        qk_ref[0] = qk.reshape(rows, CHUNK)
        egl_ref[0] = e

    wide = jax.ShapeDtypeStruct((S, DN_WIDTH), F32)
    return pl.pallas_call(
        body, grid=(DN_HEADS, nc // PREP_CHUNKS), in_specs=[q, k, v, q, q],
        out_specs=[q, q, q, q, qks, egl, _dn_tinv_spec()],
        out_shape=[wide, wide, wide, wide, jax.ShapeDtypeStruct((DN_HEADS, S, CHUNK), F32),
                   jax.ShapeDtypeStruct((DN_HEADS, nc, 1, LANES), F32),
                   jax.ShapeDtypeStruct((DN_HEADS, S, GROUP_ROWS), BF16)],
        name="dn_prep_fwd", compiler_params=_params(("parallel", "parallel")))(qkvn, qkvn, qkvn, g, beta)


def _dn_tinv_spec():
    return pl.BlockSpec((1, PREP_CHUNKS * CHUNK, GROUP_ROWS), lambda h, c: (h, c, 0))


def _dn_prep_bwd(qkvn, g, beta, tinv, du, dw, dqe, dkd, dqk, degl):
    S = qkvn.shape[0]
    nc = S // CHUNK
    q, k, v, qks, egl = _dn_prep_specs()

    def body(q_ref, k_ref, v_ref, g_ref, b_ref, t_ref, du_ref, dw_ref, dqe_ref, dkd_ref, dqk_ref, degl_ref,
             dqkv_ref, dg_ref, db_ref):
        rows = PREP_CHUNKS * CHUNK
        grp = (PREP_GROUPS, GROUP_ROWS, LANES)
        chk = (PREP_CHUNKS, CHUNK, LANES)
        q, k, g = q_ref[...], k_ref[...], g_ref[...]
        t_saved = t_ref[0].reshape(PREP_GROUPS, GROUP_ROWS, GROUP_ROWS)
        _, vjp = jax.vjp(lambda *x: _dn_group(*x, t_saved=t_saved)[:4], q.reshape(grp), k.reshape(grp),
                         v_ref[...].reshape(grp), g.reshape(grp), b_ref[...].reshape(grp))
        dq, dk, dv, dg, db = vjp((du_ref[...].reshape(grp), dw_ref[...].reshape(grp), dqe_ref[...].reshape(grp),
                                  dkd_ref[...].reshape(grp)))
        _, vjp = jax.vjp(_dn_chunk, q.reshape(chk), k.reshape(chk), g.reshape(chk))
        dq2, dk2, dg2 = vjp((dqk_ref[0].reshape(PREP_CHUNKS, CHUNK, CHUNK), degl_ref[0]))
        dqkv_ref[0] = dq.reshape(rows, LANES) + dq2.reshape(rows, LANES)
        dqkv_ref[1] = dk.reshape(rows, LANES) + dk2.reshape(rows, LANES)
        dqkv_ref[2] = dv.reshape(rows, LANES)
        dg_ref[...] = dg.reshape(rows, LANES) + dg2.reshape(rows, LANES)
        db_ref[...] = db.reshape(rows, LANES)

    wide = jax.ShapeDtypeStruct((S, DN_WIDTH), F32)
    rows = PREP_CHUNKS * CHUNK
    return pl.pallas_call(
        body, grid=(DN_HEADS, nc // PREP_CHUNKS), in_specs=[q, k, v, q, q, _dn_tinv_spec(), q, q, q, q, qks, egl],
        out_specs=[pl.BlockSpec((3, rows, LANES), lambda h, c: (0, c, h)), q, q],
        out_shape=[jax.ShapeDtypeStruct((3, S, DN_WIDTH), F32), wide, wide],
        name="dn_prep_bwd", compiler_params=_params(("parallel", "parallel")),
    )(qkvn, qkvn, qkvn, g, beta, tinv, du, dw, dqe, dkd, dqk, degl)


SCAN_CHUNKS = 8


def _dn_scan_specs(nc, reverse):
    nb = nc // SCAN_CHUNKS

    def cidx(c):
        return nb - 1 - c if reverse else c

    hc = pl.BlockSpec((SCAN_CHUNKS * CHUNK, DN_WIDTH), lambda c: (cidx(c), 0))
    qk = pl.BlockSpec((DN_HEADS, SCAN_CHUNKS * CHUNK, CHUNK), lambda c: (0, cidx(c), 0))
    egl = pl.BlockSpec((DN_HEADS, SCAN_CHUNKS, 1, LANES), lambda c: (0, cidx(c), 0, 0))
    st = pl.BlockSpec((DN_HEADS, SCAN_CHUNKS, DN_DIM, DN_DIM), lambda c: (0, cidx(c), 0, 0))
    return hc, qk, egl, st


def _heads(ref, i):
    return jnp.stack([ref[pl.ds(i * CHUNK, CHUNK), pl.ds(h * DN_DIM, DN_DIM)] for h in range(DN_HEADS)])


def _dn_scan_fwd(u, w, qe, kd, qk, egl):
    S = u.shape[0]
    nc = S // CHUNK
    hc, qks, egls, st = _dn_scan_specs(nc, False)

    def body(u_ref, w_ref, qe_ref, kd_ref, qk_ref, egl_ref, o_ref, st_ref, s_scr):
        @pl.when(pl.program_id(0) == 0)
        def _():
            s_scr[...] = jnp.zeros_like(s_scr)

        s = s_scr[...]
        for i in range(SCAN_CHUNKS):
            rows = pl.ds(i * CHUNK, CHUNK)
            st_ref[:, i] = s
            s, o = _dn_step(s, _heads(u_ref, i), _heads(w_ref, i), _heads(qe_ref, i), _heads(kd_ref, i),
                            qk_ref[:, rows, :], egl_ref[:, i])
            for h in range(DN_HEADS):
                o_ref[rows, pl.ds(h * DN_DIM, DN_DIM)] = o[h]
        s_scr[...] = s

    return pl.pallas_call(
        body, grid=(nc // SCAN_CHUNKS,), in_specs=[hc, hc, hc, hc, qks, egls], out_specs=[hc, st],
        out_shape=[jax.ShapeDtypeStruct((S, DN_WIDTH), F32), jax.ShapeDtypeStruct((DN_HEADS, nc, DN_DIM, DN_DIM), F32)],
        scratch_shapes=[pltpu.VMEM((DN_HEADS, DN_DIM, DN_DIM), F32)], name="dn_scan_fwd",
        compiler_params=_params(("arbitrary",)))(u, w, qe, kd, qk, egl)


def _dn_scan_bwd(u, w, qe, kd, qk, egl, states, do):
    S = u.shape[0]
    nc = S // CHUNK
    hc, qks, egls, st = _dn_scan_specs(nc, True)

    def body(u_ref, w_ref, qe_ref, kd_ref, qk_ref, egl_ref, st_ref, do_ref,
             du_ref, dw_ref, dqe_ref, dkd_ref, dqk_ref, degl_ref, ds_scr):
        @pl.when(pl.program_id(0) == 0)
        def _():
            ds_scr[...] = jnp.zeros_like(ds_scr)

        ds = ds_scr[...]
        for i in reversed(range(SCAN_CHUNKS)):
            rows = pl.ds(i * CHUNK, CHUNK)
            _, vjp = jax.vjp(_dn_step, st_ref[:, i], _heads(u_ref, i), _heads(w_ref, i), _heads(qe_ref, i),
                             _heads(kd_ref, i), qk_ref[:, rows, :], egl_ref[:, i])
            ds, du, dw, dqe, dkd, dqk, degl = vjp((ds, _heads(do_ref, i)))
            dqk_ref[:, rows, :] = dqk
            degl_ref[:, i] = degl
            for h in range(DN_HEADS):
                cols = pl.ds(h * DN_DIM, DN_DIM)
                du_ref[rows, cols] = du[h]
                dw_ref[rows, cols] = dw[h]
                dqe_ref[rows, cols] = dqe[h]
                dkd_ref[rows, cols] = dkd[h]
        ds_scr[...] = ds

    wide = jax.ShapeDtypeStruct((S, DN_WIDTH), F32)
    return pl.pallas_call(
        body, grid=(nc // SCAN_CHUNKS,), in_specs=[hc, hc, hc, hc, qks, egls, st, hc],
        out_specs=[hc, hc, hc, hc, qks, egls],
        out_shape=[wide, wide, wide, wide, jax.ShapeDtypeStruct((DN_HEADS, S, CHUNK), F32),
                   jax.ShapeDtypeStruct((DN_HEADS, nc, 1, LANES), F32)],
        scratch_shapes=[pltpu.VMEM((DN_HEADS, DN_DIM, DN_DIM), F32)], name="dn_scan_bwd",
        compiler_params=_params(("arbitrary",)))(u, w, qe, kd, qk, egl, states, do)


def _dn_out_fwd(o, proj, gain, tm=1024):
    S = o.shape[0]

    def body(o_ref, z_ref, g_ref, y_ref):
        y_ref[...] = _dn_out(o_ref[...], z_ref[...], g_ref[...]).astype(BF16)

    hs = pl.BlockSpec((tm, LANES), lambda i, h: (i, h))
    zs = pl.BlockSpec((tm, LANES), lambda i, h: (i, P_Z // LANES + h))
    return pl.pallas_call(
        body, grid=(S // tm, DN_HEADS), in_specs=[hs, zs, _full((1, DN_DIM))], out_specs=hs,
        out_shape=jax.ShapeDtypeStruct((S, DN_WIDTH), BF16), name="dn_out_fwd",
        compiler_params=_params(("parallel", "parallel")))(o, proj, gain)


_ANY = pl.BlockSpec(memory_space=pl.ANY)


def _dn_out_bwd(o, proj, gain, dy, dproj, tm=1024):
    S = o.shape[0]

    def body(o_ref, z_ref, g_ref, dy_ref, _, do_ref, dz_ref, dg_ref):
        _, vjp = jax.vjp(_dn_out, o_ref[...], z_ref[...], g_ref[...])
        do, dz, dg = vjp(dy_ref[...])
        do_ref[...] = do
        dz_ref[...] = dz.astype(BF16)

        @pl.when((pl.program_id(0) == 0) & (pl.program_id(1) == 0))
        def _():
            dg_ref[...] = jnp.zeros_like(dg_ref)

        dg_ref[...] += dg

    hs = pl.BlockSpec((tm, LANES), lambda i, h: (i, h))
    zs = pl.BlockSpec((tm, LANES), lambda i, h: (i, P_Z // LANES + h))
    return pl.pallas_call(
        body, grid=(S // tm, DN_HEADS), in_specs=[hs, zs, _full((1, DN_DIM)), hs, _ANY],
        out_specs=[hs, zs, _full((1, DN_DIM))],
        out_shape=[jax.ShapeDtypeStruct((S, DN_WIDTH), F32), jax.ShapeDtypeStruct(dproj.shape, dproj.dtype),
                   jax.ShapeDtypeStruct((1, DN_DIM), F32)],
        input_output_aliases={4: 1},
        name="dn_out_bwd", compiler_params=_params(("arbitrary", "arbitrary")))(o, proj, gain, dy, dproj)


def _rel_buckets():
    qi = np.arange(BLOCK)[:, None]
    kj = np.arange(2 * BLOCK)[None, :]
    n = np.maximum(BLOCK + qi - kj, 0)
    max_exact = REL_BUCKETS // 2
    nf = np.maximum(n, 1).astype(np.float32)
    large = max_exact + (np.log(nf / np.float32(max_exact)) / np.float32(math.log(REL_MAX_DIST / max_exact))
                         * np.float32(REL_BUCKETS - max_exact)).astype(np.int32)
    large = np.minimum(large, REL_BUCKETS - 1)
    return np.where(n < max_exact, n, large).astype(np.int32)


def _bias_fwd(rel_bias):
    buckets = jnp.asarray(_rel_buckets())

    def body(rb_ref, bk_ref, o_ref):
        bk = bk_ref[...]
        for h in range(SWA_HEADS):
            acc = jnp.zeros((BLOCK, 2 * BLOCK), F32)
            for b in range(REL_BUCKETS):
                acc = jnp.where(bk == b, rb_ref[b, h], acc)
            for first in range(2):
                o_ref[first, h] = jnp.where(_swa_mask(1 - first), acc, -jnp.inf)

    return pl.pallas_call(
        body, in_specs=[pl.BlockSpec(memory_space=pltpu.SMEM), pl.BlockSpec(memory_space=pltpu.VMEM)],
        out_specs=pl.BlockSpec(memory_space=pltpu.VMEM),
        out_shape=jax.ShapeDtypeStruct((2, SWA_HEADS, BLOCK, 2 * BLOCK), F32), name="swa_bias_fwd",
        compiler_params=_params())(rel_bias, buckets)


def _bias_bwd(dbias):
    buckets = jnp.asarray(_rel_buckets())

    def body(d_ref, bk_ref, o_ref):
        bk = bk_ref[...]
        lane = lax.broadcasted_iota(jnp.int32, (1, LANES), 1)
        for h in range(SWA_HEADS):
            d = d_ref[h]
            row = jnp.zeros((1, LANES), F32)
            for b in range(REL_BUCKETS):
                part = jnp.sum(jnp.where(bk == b, d, 0.0), axis=1, keepdims=True)
                row = jnp.where(lane == b, jnp.sum(part, axis=0, keepdims=True), row)
            o_ref[h:h + 1, :] = row

    return pl.pallas_call(
        body, in_specs=[pl.BlockSpec(memory_space=pltpu.VMEM), pl.BlockSpec(memory_space=pltpu.VMEM)],
        out_specs=pl.BlockSpec(memory_space=pltpu.VMEM),
        out_shape=jax.ShapeDtypeStruct((SWA_HEADS, LANES), F32), name="swa_bias_bwd",
        compiler_params=_params())(dbias, buckets)


def _swa_mask(n):
    qi = lax.broadcasted_iota(jnp.int32, (BLOCK, 2 * BLOCK), 0)
    kj = lax.broadcasted_iota(jnp.int32, (BLOCK, 2 * BLOCK), 1)
    dist = BLOCK + qi - kj
    return (dist >= 0) & (dist < WINDOW) & ((n > 0) | (kj >= BLOCK))


def _swa_in_specs():
    q = pl.BlockSpec((BLOCK, SWA_WIDTH), lambda n: (n, P_SQ // SWA_WIDTH))
    kc = pl.BlockSpec((BLOCK, SWA_KVW), lambda n: (n, P_SK // SWA_KVW))
    kp = pl.BlockSpec((BLOCK, SWA_KVW), lambda n: (jnp.maximum(n - 1, 0), P_SK // SWA_KVW))
    vc = pl.BlockSpec((BLOCK, SWA_KVW), lambda n: (n, P_SV // SWA_KVW))
    vp = pl.BlockSpec((BLOCK, SWA_KVW), lambda n: (jnp.maximum(n - 1, 0), P_SV // SWA_KVW))
    band = pl.BlockSpec((None, SWA_HEADS, BLOCK, 2 * BLOCK), lambda n: (jnp.where(n == 0, 1, 0), 0, 0, 0))
    small = [_full((1, SWA_DIM)), _full((1, SWA_DIM)), _full((1, SWA_HEADS)), band]
    return [q, kp, kc, vp, vc] + small


def _swa_load(q_ref, kp_ref, kc_ref, vp_ref, vc_ref, s_ref):
    q = jnp.stack([q_ref[:, pl.ds(h * SWA_DIM, SWA_DIM)] for h in range(SWA_HEADS)])
    kbands, vbands = [], []
    for kv in range(SWA_KV):
        cols = pl.ds(kv * SWA_DIM, SWA_DIM)
        kbands += [jnp.concatenate([kp_ref[:, cols], kc_ref[:, cols]], axis=0)] * SWA_GROUP
        vbands += [jnp.concatenate([vp_ref[:, cols], vc_ref[:, cols]], axis=0)] * SWA_GROUP
    sinks = jnp.stack([s_ref[:, pl.ds(h, 1)] for h in range(SWA_HEADS)])
    return q, jnp.stack(kbands), jnp.stack(vbands), sinks


def _swa_fwd(proj, q_gain, k_gain, sinks, bias):
    S = proj.shape[0]

    def body(q_ref, kp_ref, kc_ref, vp_ref, vc_ref, qg_ref, kg_ref, s_ref, bias_ref, y_ref):
        q, kband, vband, sk = _swa_load(q_ref, kp_ref, kc_ref, vp_ref, vc_ref, s_ref)
        out = _swa_block(q, kband, vband, qg_ref[...], kg_ref[...], sk, bias_ref[...])
        for h in range(SWA_HEADS):
            y_ref[:, pl.ds(h * SWA_DIM, SWA_DIM)] = out[h].astype(BF16)

    return pl.pallas_call(
        body, grid=(S // BLOCK,), in_specs=_swa_in_specs(),
        out_specs=pl.BlockSpec((BLOCK, SWA_WIDTH), lambda n: (n, 0)),
        out_shape=jax.ShapeDtypeStruct((S, SWA_WIDTH), BF16), name="swa_fwd",
        compiler_params=_params(("parallel",)))(proj, proj, proj, proj, proj, q_gain, k_gain, sinks, bias)


def _swa_bwd(proj, q_gain, k_gain, sinks, bias, dy, dproj):
    S = proj.shape[0]

    def body(q_ref, kp_ref, kc_ref, vp_ref, vc_ref, qg_ref, kg_ref, s_ref, bias_ref, dy_ref, _,
             dq_ref, dk_ref, dv_ref, dqg_ref, dkg_ref, ds_ref, dbias_ref):
        n = pl.program_id(0)

        @pl.when(n == 0)
        def _():
            for r in (dk_ref, dv_ref, dqg_ref, dkg_ref, ds_ref, dbias_ref):
                r[...] = jnp.zeros_like(r)

        cur = pl.ds(pl.multiple_of(n * BLOCK, BLOCK), BLOCK)
        prev = pl.ds(pl.multiple_of(jnp.maximum(n - 1, 0) * BLOCK, BLOCK), BLOCK)
        q, kband, vband, sk = _swa_load(q_ref, kp_ref, kc_ref, vp_ref, vc_ref, s_ref)
        _, vjp = jax.vjp(_swa_block, q, kband, vband, qg_ref[...], kg_ref[...], sk, bias_ref[...])
        dy = jnp.stack([dy_ref[:, pl.ds(h * SWA_DIM, SWA_DIM)] for h in range(SWA_HEADS)])
        dq, dkb, dvb, dqg, dkg, dsk, dbs = vjp(dy)
        for h in range(SWA_HEADS):
            dq_ref[:, pl.ds(h * SWA_DIM, SWA_DIM)] = dq[h].astype(BF16)
            ds_ref[:, pl.ds(h, 1)] += dsk[h]
        dbias_ref[...] += dbs
        dqg_ref[...] += dqg
        dkg_ref[...] += dkg
        for kv in range(SWA_KV):
            cols = pl.ds(kv * SWA_DIM, SWA_DIM)
            group = range(kv * SWA_GROUP, (kv + 1) * SWA_GROUP)
            dk_kv = sum(dkb[h] for h in group)
            dv_kv = sum(dvb[h] for h in group)
            dk_ref[cur, cols] += dk_kv[BLOCK:]
            dv_ref[cur, cols] += dv_kv[BLOCK:]

            @pl.when(n > 0)
            def _(cols=cols, dk_kv=dk_kv, dv_kv=dv_kv):
                dk_ref[prev, cols] += dk_kv[:BLOCK]
                dv_ref[prev, cols] += dv_kv[:BLOCK]

    return pl.pallas_call(
        body, grid=(S // BLOCK,),
        in_specs=_swa_in_specs() + [pl.BlockSpec((BLOCK, SWA_WIDTH), lambda n: (n, 0)),
                                    pl.BlockSpec(memory_space=pl.ANY)],
        out_specs=[pl.BlockSpec((BLOCK, SWA_WIDTH), lambda n: (n, P_SQ // SWA_WIDTH)), _full((S, SWA_KVW)),
                   _full((S, SWA_KVW)), _full((1, SWA_DIM)), _full((1, SWA_DIM)), _full((1, SWA_HEADS)),
                   _full((SWA_HEADS, BLOCK, 2 * BLOCK))],
        out_shape=[jax.ShapeDtypeStruct(dproj.shape, dproj.dtype), jax.ShapeDtypeStruct((S, SWA_KVW), F32),
                   jax.ShapeDtypeStruct((S, SWA_KVW), F32), jax.ShapeDtypeStruct((1, SWA_DIM), F32),
                   jax.ShapeDtypeStruct((1, SWA_DIM), F32), jax.ShapeDtypeStruct((1, SWA_HEADS), F32),
                   jax.ShapeDtypeStruct((SWA_HEADS, BLOCK, 2 * BLOCK), F32)],
        input_output_aliases={10: 0},
        name="swa_bwd", compiler_params=_params(("arbitrary",)),
    )(proj, proj, proj, proj, proj, q_gain, k_gain, sinks, bias, dy, dproj)


def _kv_into(dproj, dk, dv, tm=1024):
    S = dk.shape[0]

    def body(dk_ref, dv_ref, _, o_ref):
        o_ref[:, :SWA_KVW] = dk_ref[...].astype(BF16)
        o_ref[:, SWA_KVW:] = dv_ref[...].astype(BF16)

    return pl.pallas_call(
        body, grid=(S // tm,), in_specs=[_row(tm, SWA_KVW), _row(tm, SWA_KVW), pl.BlockSpec(memory_space=pl.ANY)],
        out_specs=_row(tm, 2 * SWA_KVW, P_SK // (2 * SWA_KVW)),
        out_shape=jax.ShapeDtypeStruct(dproj.shape, dproj.dtype), input_output_aliases={2: 0},
        name="swa_kv_into", compiler_params=_params(("parallel",)))(dk, dv, dproj)


def _position():
    return lax.axis_index("x"), lax.axis_index("y"), lax.axis_index("c")


def _all_gather(shards, name="all_gather_weights"):
    na = len(shards)

    def body(*refs):
        x_refs, out_refs = refs[:na], refs[na:2 * na]
        send_sems, recv_sems, local_sems = refs[2 * na:]
        x, y, c = _position()
        me, sibling = (x, y, c), (x, y, 1 - c)
        chips = [(1 - x, y), (x, 1 - y), (1 - x, 1 - y)]

        def copy(a, k, block, to, own=False):
            px, py, pc = block
            slot = out_refs[a].at[4 * px + 2 * py + pc]
            return pltpu.make_async_remote_copy(
                src_ref=x_refs[a] if own else slot, dst_ref=slot, send_sem=send_sems.at[7 * a + k],
                recv_sem=recv_sems.at[7 * a + k], device_id=to, device_id_type=MESH_ID)

        mine = [pltpu.make_async_copy(x_refs[a], out_refs[a].at[4 * x + 2 * y + c], local_sems.at[a])
                for a in range(na)]
        for cp in mine:
            cp.start()
        first = []
        for a in range(na):
            first.append(copy(a, 0, me, sibling, own=True))
            first += [copy(a, 1 + j, me, (*chip, c), own=True) for j, chip in enumerate(chips)]
        for cp in first:
            cp.start()
        passed = []
        for j, chip in enumerate(chips):
            for a in range(na):
                copy(a, 1 + j, (*chip, c), me).wait_recv()
                passed.append(copy(a, 4 + j, (*chip, c), sibling))
                passed[-1].start()
        for a in range(na):
            copy(a, 0, sibling, me).wait_recv()
            for j, chip in enumerate(chips):
                copy(a, 4 + j, (*chip, 1 - c), me).wait_recv()
        for cp in first + passed:
            cp.wait_send()
        for cp in mine:
            cp.wait()

    return pl.pallas_call(
        body, in_specs=[pl.BlockSpec(memory_space=pl.ANY)] * na, out_specs=[pl.BlockSpec(memory_space=pl.ANY)] * na,
        out_shape=[jax.ShapeDtypeStruct((N_DEV,) + s.shape, s.dtype) for s in shards],
        scratch_shapes=[pltpu.SemaphoreType.DMA((7 * na,)), pltpu.SemaphoreType.DMA((7 * na,)),
                        pltpu.SemaphoreType.DMA((na,))],
        name=name)(*shards)


_HBM = pl.BlockSpec(memory_space=pltpu.HBM)
_SEM = pl.BlockSpec(memory_space=pltpu.SEMAPHORE)
_DATAFLOW = pltpu.SideEffectType.DATAFLOW_SIDE_EFFECTING


def _peers(x, y, c):
    out = []
    for k in range(1, N_DEV):
        px, py, pc = x ^ (k >> 2), y ^ ((k >> 1) & 1), c ^ (k & 1)
        out.append(((px, py, pc), 4 * px + 2 * py + pc))
    return out


def _split_copies(src_refs, land_refs, send_sems, recv_sems, scatter):
    x, y, c = _position()
    me = 4 * x + 2 * y + c
    sends, recvs = [], []
    for k, (peer_id, peer) in enumerate(_peers(x, y, c)):
        for a, (src, land) in enumerate(zip(src_refs, land_refs)):
            sems = dict(send_sem=send_sems.at[7 * a + k], recv_sem=recv_sems.at[7 * a + k],
                        device_id=peer_id, device_id_type=MESH_ID)
            mine = src.at[peer] if scatter else src
            sends.append(pltpu.make_async_remote_copy(src_ref=mine, dst_ref=land.at[me], **sems))
            recvs.append(pltpu.make_async_remote_copy(src_ref=mine, dst_ref=land.at[peer], **sems))
    return sends, recvs


def _all_gather_direct(shards, name, after):
    na = len(shards)

    def body(*refs):
        x_refs, out_refs = refs[:na], refs[na + 1:2 * na + 1]
        send_sems, recv_sems, local_sems = refs[2 * na + 1:]
        x, y, c = _position()
        me = 4 * x + 2 * y + c
        local = [pltpu.make_async_copy(x_refs[a], out_refs[a].at[me], local_sems.at[a]) for a in range(na)]
        sends, recvs = _split_copies(x_refs, out_refs, send_sems, recv_sems, False)
        for cp in local + sends:
            cp.start()
        for cp in recvs:
            cp.wait_recv()
        for cp in sends:
            cp.wait_send()
        for cp in local:
            cp.wait()

    return pl.pallas_call(
        body, in_specs=[pl.BlockSpec(memory_space=pl.ANY)] * (na + 1),
        out_specs=[pl.BlockSpec(memory_space=pl.ANY)] * na,
        out_shape=[jax.ShapeDtypeStruct((N_DEV,) + s.shape, s.dtype) for s in shards],
        scratch_shapes=[pltpu.SemaphoreType.DMA((7 * na,)), pltpu.SemaphoreType.DMA((7 * na,)),
                        pltpu.SemaphoreType.DMA((na,))],
        name=name)(*shards, after)


def _exchange_start(srcs, scatter, name, after=None):
    na = len(srcs)
    lands = [lax.empty(s.shape if scatter else (N_DEV,) + s.shape, s.dtype) for s in srcs]
    extra = [] if after is None else [after]

    def body(*refs):
        src_refs, land_refs = refs[:na], refs[na:2 * na]
        send_sems, recv_sems = refs[2 * na + len(extra)], refs[2 * na + len(extra) + 1]
        token = refs[-1]
        sends, _ = _split_copies(src_refs, land_refs, send_sems, recv_sems, scatter)
        for cp in sends:
            cp.start()
        token[...] = jnp.zeros_like(token)

    hbm = lambda a: pltpu.HBM(a.shape, a.dtype)
    out = pl.pallas_call(
        body, name=name,
        out_shape=(pltpu.SemaphoreType.DMA((7 * na,)), pltpu.SemaphoreType.DMA((7 * na,)),
                   *[hbm(s) for s in srcs], *[hbm(l) for l in lands], jax.ShapeDtypeStruct((8, LANES), F32)),
        in_specs=[_HBM] * (2 * na) + [pl.BlockSpec(memory_space=pl.ANY)] * len(extra),
        out_specs=(_SEM, _SEM, *[_HBM] * (2 * na), pl.BlockSpec(memory_space=pltpu.VMEM)),
        input_output_aliases={i: 2 + i for i in range(2 * na)},
        compiler_params=pltpu.CompilerParams(has_side_effects=_DATAFLOW),
    )(*[pltpu.with_memory_space_constraint(s, pltpu.HBM) for s in srcs],
      *[pltpu.with_memory_space_constraint(l, pltpu.HBM) for l in lands], *extra)
    return (out[0], out[1], list(out[2:2 + na]), list(out[2 + na:2 + 2 * na])), out[-1]


def _exchange_wait(handle, after, scatter, name):
    send_sems, recv_sems, srcs, lands = handle
    na = len(srcs)

    def body(*refs):
        src_refs, land_refs = refs[:na], refs[na:2 * na]
        s_sems, r_sems = refs[2 * na], refs[2 * na + 1]
        sends, recvs = _split_copies(src_refs, land_refs, s_sems, r_sems, scatter)
        for cp in sends:
            cp.wait_send()
        for cp in recvs:
            cp.wait_recv()

    hbm = lambda a: pltpu.HBM(a.shape, a.dtype)
    out = pl.pallas_call(
        body, name=name, out_shape=(*[hbm(s) for s in srcs], *[hbm(l) for l in lands]),
        in_specs=[_HBM] * (2 * na) + [_SEM, _SEM, pl.BlockSpec(memory_space=pl.ANY)],
        out_specs=tuple([_HBM] * (2 * na)), input_output_aliases={i: i for i in range(2 * na)},
        compiler_params=pltpu.CompilerParams(has_side_effects=_DATAFLOW),
    )(*srcs, *lands, send_sems, recv_sems, after)
    return list(out[:na]), list(out[na:])


def _own_slot(landed, own):
    me = 4 * lax.axis_index("x") + 2 * lax.axis_index("y") + lax.axis_index("c")
    return lax.dynamic_update_slice_in_dim(landed, own[None], me, axis=0)


def _adam_update(parts, w, m, v, name, tr=256):
    parts = parts if isinstance(parts, (list, tuple)) else [parts]
    _, r, c = w.shape
    tr = _pick_rows(r // len(parts), tr)
    cp = parts[0].shape[2]
    per = r // len(parts) // tr

    def body(*refs):
        p_refs = refs[:len(parts)]
        w_ref, m_ref, v_ref, g_ref, d_ref, nm_ref, nv_ref = refs[len(parts):]
        step = pl.program_id(0)
        for k, p_ref in enumerate(p_refs):
            def update(p_ref=p_ref):
                g = p_ref[0, :, pl.ds(0, c)].astype(F32)
                for i in range(1, N_DEV):
                    g = g + p_ref[i, :, pl.ds(0, c)].astype(F32)
                delta, nm, nv = _adamw(w_ref[0], g, m_ref[0], v_ref[0])
                g_ref[0] = g
                d_ref[0] = delta
                nm_ref[0] = nm
                nv_ref[0] = nv

            if len(parts) == 1:
                update()
            else:
                pl.when((step >= k * per) & (step < (k + 1) * per))(update)

    rs = pl.BlockSpec((1, tr, c), lambda i: (0, i, 0))
    p_specs = [pl.BlockSpec((N_DEV, tr, cp), lambda i, k=k: (0, jnp.clip(i - k * per, 0, per - 1), 0))
               for k in range(len(parts))]
    return pl.pallas_call(
        body, grid=(r // tr,), in_specs=p_specs + [rs, rs, rs],
        out_specs=[rs] * 4, out_shape=[jax.ShapeDtypeStruct((1, r, c), F32)] * 4, name=name,
        compiler_params=_params(("parallel",)))(*parts, w, m, v)


def _pick_rows(rows, target):
    if rows <= target:
        return rows
    t = target
    while t >= 16:
        if rows % t == 0:
            return t
        t -= 16
    return rows


BIG = ("w_in", "w_branch_dn", "w_branch_swa", "w_out", "w_gate", "w_up", "w_down")
IN_SHARD, IN_WIRE = D_IN // N_DEV, 640
FF_SHARD, FF_WIRE = D_FF // N_DEV, 384
D_FFP = N_DEV * FF_WIRE
BIG_SHAPES = {"w_in": ((D_MODEL, IN_SHARD), (D_MODEL, IN_WIRE)),
              "w_branch_dn": ((DN_WIDTH, LANES), (DN_WIDTH, LANES)),
              "w_branch_swa": ((SWA_WIDTH, LANES), (SWA_WIDTH, LANES)),
              "w_out": ((LANES, D_MODEL), (LANES, D_MODEL)),
              "w_gate": ((D_MODEL, FF_SHARD), (D_MODEL, FF_WIRE)),
              "w_up": ((D_MODEL, FF_SHARD), (D_MODEL, FF_WIRE)),
              "w_down": ((FF_SHARD, D_MODEL), (FF_WIRE, D_MODEL))}
CONV_SHARD, CONV_WIRE = (DN_CONV, DN_QKV // N_DEV), (8, 256)


def _pad_to(a, shape):
    return jnp.pad(a, [(0, t - s) for s, t in zip(a.shape, shape)])


_IN_SEGS = ((R_GATE, 2048, P_GATE), (R_QKV, DN_QKV, P_QKV), (R_Z, DN_WIDTH, P_Z), (R_SQ, SWA_WIDTH, P_SQ),
            (R_SK, SWA_KVW, P_SK), (R_SV, SWA_KVW, P_SV), (R_B, 8, P_BA))


def _w_in_from_blocks(blocks):
    parts = []
    for rs, n, _ in _IN_SEGS:
        for dev in range(N_DEV):
            lo, hi = max(rs, IN_SHARD * dev), min(rs + n, IN_SHARD * (dev + 1))
            if lo < hi:
                parts.append(blocks[dev, :, lo - IN_SHARD * dev:hi - IN_SHARD * dev])
    parts.append(jnp.zeros((blocks.shape[1], P_WIDTH - P_BA - 8), blocks.dtype))
    return jnp.concatenate(parts, axis=1)


def _w_in_to_blocks(g):
    out = []
    for dev in range(N_DEV):
        parts = []
        for rs, n, ps in sorted(_IN_SEGS):
            lo, hi = max(rs, IN_SHARD * dev), min(rs + n, IN_SHARD * (dev + 1))
            if lo < hi:
                parts.append(g[:, ps + lo - rs:ps + hi - rs])
        parts.append(jnp.zeros((g.shape[0], IN_WIRE - IN_SHARD), g.dtype))
        out.append(jnp.concatenate(parts, axis=1))
    return jnp.stack(out)


SMALL = {"attn_norm": (0, (1, D_MODEL)), "ffn_norm": (1, (1, D_MODEL)), "dn_out_norm": (2, (1, DN_DIM)),
         "swa_q_norm": (3, (1, SWA_DIM)), "swa_k_norm": (4, (1, SWA_DIM)), "dn_a_log": (5, (1, DN_HEADS)),
         "dn_dt_bias": (6, (1, DN_HEADS)), "swa_sinks": (7, (1, SWA_HEADS)), "rel_bias": (8, (REL_BUCKETS, SWA_HEADS))}
SMALL_SHEET = (48, D_MODEL)


LOSS_ROW = 40


def _small_pack(grads, loss_local):
    names = list(SMALL)

    def body(*refs):
        o_ref = refs[-1]
        o_ref[...] = jnp.zeros_like(o_ref)
        for n, ref in zip(names, refs):
            r0, (nr, nc) = SMALL[n]
            o_ref[r0:r0 + nr, 0:nc] = ref[...]
        o_ref[LOSS_ROW:LOSS_ROW + 1, 0:1] = refs[len(names)][...]

    return pl.pallas_call(
        body, in_specs=[pl.BlockSpec(memory_space=pltpu.VMEM)] * (len(names) + 1),
        out_specs=pl.BlockSpec(memory_space=pltpu.VMEM), out_shape=jax.ShapeDtypeStruct(SMALL_SHEET, F32),
        name="small_pack", compiler_params=_params())(*[grads[n].reshape(SMALL[n][1]) for n in names], loss_local)


def _small_update(sheets, w, m, v):
    names = list(SMALL)
    k = len(names)

    def body(*refs):
        p_ref = refs[0]
        ins, outs = refs[1:1 + 3 * k], refs[1 + 3 * k:]
        loss = p_ref[0, LOSS_ROW:LOSS_ROW + 1, 0:1]
        for i in range(1, N_DEV):
            loss = loss + p_ref[i, LOSS_ROW:LOSS_ROW + 1, 0:1]
        outs[4 * k][...] = loss
        for t, n in enumerate(names):
            r0, (nr, nc) = SMALL[n]
            g = p_ref[0, r0:r0 + nr, 0:nc]
            for i in range(1, N_DEV):
                g = g + p_ref[i, r0:r0 + nr, 0:nc]
            delta, nm, nv = _adamw(ins[t][...], g, ins[k + t][...], ins[2 * k + t][...])
            for kind, val in enumerate((g, delta, nm, nv)):
                outs[kind * k + t][...] = val

    shapes = [jax.ShapeDtypeStruct(SMALL[n][1], F32) for n in names]
    vm = pl.BlockSpec(memory_space=pltpu.VMEM)
    res = pl.pallas_call(
        body, in_specs=[vm] * (1 + 3 * k), out_specs=[vm] * (4 * k + 1),
        out_shape=shapes * 4 + [jax.ShapeDtypeStruct((1, 1), F32)], name="adam_small", compiler_params=_params(),
    )(sheets, *[d[n].reshape(SMALL[n][1]) for d in (w, m, v) for n in names])
    return {n: tuple(res[kind * k + t] for kind in range(4)) for t, n in enumerate(names)}, res[4 * k]


def kernel(x, attn_norm, w_in, dn_conv, dn_a_log, dn_dt_bias, dn_out_norm, swa_q_norm, swa_k_norm, swa_sinks, rel_bias, w_branch_dn, w_branch_swa, w_out, ffn_norm, w_gate, w_up, w_down, loss_target, m_attn_norm, m_w_in, m_dn_conv, m_dn_a_log, m_dn_dt_bias, m_dn_out_norm, m_swa_q_norm, m_swa_k_norm, m_swa_sinks, m_rel_bias, m_w_branch_dn, m_w_branch_swa, m_w_out, m_ffn_norm, m_w_gate, m_w_up, m_w_down, v_attn_norm, v_w_in, v_dn_conv, v_dn_a_log, v_dn_dt_bias, v_dn_out_norm, v_swa_q_norm, v_swa_k_norm, v_swa_sinks, v_rel_bias, v_w_branch_dn, v_w_branch_swa, v_w_out, v_ffn_norm, v_w_gate, v_w_up, v_w_down):
    args = dict(locals())
    S = x.shape[1]
    xs = x.reshape(S, D_MODEL)
    target = loss_target.reshape(S, D_MODEL)

    w_loc = {n: args[n].reshape(BIG_SHAPES[n][0]) for n in BIG}
    conv_loc = dn_conv.reshape(CONV_SHARD)
    wire = {n: _pad_to(w_loc[n], BIG_SHAPES[n][1]).astype(BF16) for n in BIG}
    first = _all_gather([wire["w_in"], _pad_to(conv_loc, CONV_WIRE)])
    later = [n for n in BIG if n != "w_in"]
    rest_handle, rest_token = _exchange_start([wire[n] for n in later], False, "gather_rest_start", after=first[1])
    w_pad = _w_in_from_blocks(first[0])
    conv_w = jnp.concatenate([first[1][d, :DN_CONV, :CONV_SHARD[1]] for d in range(N_DEV)], axis=1)

    h = _norm_fwd(xs, attn_norm + rest_token[0, 0], "norm1_fwd")
    proj = _mm([(h, w_pad)], "nn", F32, "mm_in", 1024, 1664, j_outer=True)
    qkvn = _dn_conv_fwd(proj, conv_w)
    beta, g = _dn_gate_fwd(proj, dn_a_log, dn_dt_bias)
    u, w, qe, kd, qk, egl, tinv = _dn_prep_fwd(qkvn, g, beta)
    o, states = _dn_scan_fwd(u, w, qe, kd, qk, egl)
    y_dn = _dn_out_fwd(o, proj, dn_out_norm)
    bias = _bias_fwd(rel_bias)
    y_swa = _swa_fwd(proj, swa_q_norm, swa_k_norm, swa_sinks, bias)
    rest_src, rest_land = _exchange_wait(rest_handle, y_swa, False, "gather_rest_wait")
    G = {n: _own_slot(land, src) for n, src, land in zip(later, rest_src, rest_land)}
    w_bdn, w_bswa, w_g, w_u = G["w_branch_dn"], G["w_branch_swa"], G["w_gate"], G["w_up"]
    w_o = G["w_out"].reshape(D_MODEL, D_MODEL)
    w_d = G["w_down"].reshape(D_FFP, D_MODEL)
    gates = [(proj, P_GATE // 512), (proj, (P_GATE + D_MODEL) // 512)]
    a_dn, a_swa, merged = _mm_fused(
        [(y_dn, w_bdn), (y_swa, w_bswa)], "nn", "mm_branch_merge", 1024, 512,
        lambda p, e: (p[0], p[1], _merge(e[0], e[1], p[0], p[1])), gates, (F32, F32, BF16), b_blocks=True)

    def resid_norm(p, e):
        x1 = e[0] + p[0]
        return x1, _rms(x1, e[1])

    x1, h2 = _mm_fused([(merged, w_o)], "nn", "mm_out_norm", 512, D_MODEL, resid_norm,
                       [(xs, 0), (ffn_norm, None)], (F32, BF16))
    gate, up, act = _mm_fused([(h2, w_g), (h2, w_u)], "nn", "mm_gate_up_act", 1024, 768,
                              lambda p, e: (p[0], p[1], _act(p[0], p[1])), [], (F32, F32, BF16),
                              j_outer=True, b_blocks=True)

    def loss_head(p, e):
        diff = e[0] + p[0] - e[1]
        dy = diff * (1.0 / D_MODEL)
        part = jnp.sum(jnp.mean(diff * diff, axis=-1, keepdims=True), axis=0, keepdims=True) * 0.5
        return dy, dy, part

    dy, dy_b, loss_local = _mm_fused([(act, w_d)], "nn", "mm_down_loss", 512, D_MODEL, loss_head,
                                     [(x1, 0), (target, 0)], (F32, BF16), sum_shape=(1, 1))

    def act_bwd(p, e):
        _, vjp = jax.vjp(_act, e[0], e[1])
        return vjp(p[0])

    dgate, dup = _mm_fused([(dy_b, w_d)], "nt", "mm_dact_act", 1024, 768, act_bwd, [(gate, 0), (up, 0)],
                           (BF16, BF16), j_outer=True)
    g_w_down = _mm([(act, dy_b)], "tn", BF16, "mm_dw_down", 768, D_MODEL, j_outer=True)
    g_w_down = g_w_down.reshape(N_DEV, FF_WIRE, D_MODEL)
    g_w_gate = _mm([(h2, dgate)], "tn", BF16, "mm_dw_gate", D_MODEL, 768, out_blocks=True)
    g_w_up = _mm([(h2, dup)], "tn", BF16, "mm_dw_up", D_MODEL, 768, out_blocks=True)
    ffn_handle, ffn_token = _exchange_start([g_w_down, g_w_gate, g_w_up], True, "scatter_ffn_start")

    def norm_bwd(p, e):
        _, vjp = jax.vjp(_rms, e[0], e[2])
        dx, dgain = vjp(sum(p))
        dx = dx + e[1]
        return dx, dx, dgain

    dx1, dx1_b, g_ffn_norm = _mm_fused(
        [(dgate, w_g), (dup, w_u)], "nt", "mm_dh2_norm", 256, D_MODEL, norm_bwd,
        [(x1, 0), (dy, 0), (ffn_norm + ffn_token[0, 0], None)], (F32, BF16), b_blocks=True, sum_shape=(1, D_MODEL))
    def merge_bwd(p, e):
        _, vjp = jax.vjp(_merge, *e)
        dg0, dg1, da_dn, da_swa = vjp(p[0])
        return jnp.concatenate([dg0, dg1], axis=1), da_dn, da_swa

    dproj, da_dn, da_swa = _mm_fused(
        [(dx1_b, w_o)], "nt", "mm_dmerged_merge", 512, D_MODEL, merge_bwd,
        [(proj, P_GATE // D_MODEL), (proj, P_GATE // D_MODEL + 1), (a_dn, 0), (a_swa, 0)], (BF16,) * 3,
        wide_first=(P_WIDTH, 2 * D_MODEL))
    g_w_out = _mm([(merged, dx1_b)], "tn", BF16, "mm_dw_out", 512, D_MODEL, j_outer=True)
    g_w_out = g_w_out.reshape(N_DEV, LANES, D_MODEL)
    dy_dn = _mm([(da_dn, w_bdn)], "nt", F32, "mm_dy_dn", 1024, DN_WIDTH, b_blocks=True)
    dy_swa = _mm([(da_swa, w_bswa)], "nt", F32, "mm_dy_swa", 1024, SWA_WIDTH, b_blocks=True)
    g_w_bdn = _mm([(y_dn, da_dn)], "tn", BF16, "mm_dw_branch_dn", DN_WIDTH, 512, out_blocks=True)
    g_w_bswa = _mm([(y_swa, da_swa)], "tn", BF16, "mm_dw_branch_swa", SWA_WIDTH, 512, out_blocks=True)
    dproj, dsk, dsv, g_q_norm, g_k_norm, g_sinks, dbias = _swa_bwd(proj, swa_q_norm, swa_k_norm, swa_sinks, bias,
                                                                   dy_swa, dproj)
    dproj = _kv_into(dproj, dsk, dsv)
    g_rel_bias = _bias_bwd(dbias)[:, :REL_BUCKETS].T
    mix_handle, mix_token = _exchange_start([g_w_out, g_w_bdn, g_w_bswa], True, "scatter_mix_start")
    do, dproj, g_out_norm = _dn_out_bwd(o, proj, dn_out_norm + mix_token[0, 0], dy_dn, dproj)
    du, dw, dqe, dkd, dqk, degl = _dn_scan_bwd(u, w, qe, kd, qk, egl, states, do)
    dqkvn, dgd, dbeta = _dn_prep_bwd(qkvn, g, beta, tinv, du, dw, dqe, dkd, dqk, degl)
    dproj, dal, ddt = _dn_gate_bwd(proj, dn_a_log, dn_dt_bias, dbeta, dgd, dproj)
    g_a_log = dal.reshape(DN_HEADS, DN_DIM).sum(axis=1)
    g_dt_bias = ddt[0, DN_HEADS:2 * DN_HEADS]
    dproj, g_conv = _dn_conv_bwd(proj, conv_w, dqkvn, dproj)
    half = D_MODEL // 2
    in_handles = []
    in_token = jnp.zeros((8, LANES), F32)
    for part in range(2):
        h_rows = h[:, part * half:(part + 1) * half] + in_token[0, 0].astype(BF16)
        g_w_in = _w_in_to_blocks(_mm([(h_rows, dproj)], "tn", BF16, "mm_dw_in_%d" % part, half, 1664, j_outer=True))
        handle, in_token = _exchange_start([g_w_in], True, "scatter_in_%d_start" % part)
        in_handles.append(handle)
    dx, g_attn_norm = _mm_fused(
        [(dproj, w_pad)], "nt", "mm_dh_norm", 512, D_MODEL, lambda p, e: norm_bwd(p, e)[1:],
        [(xs, 0), (dx1, 0), (attn_norm + in_token[0, 0], None)], (F32,), sum_shape=(1, D_MODEL))

    g_small = {"attn_norm": g_attn_norm, "ffn_norm": g_ffn_norm, "rel_bias": g_rel_bias, "dn_out_norm": g_out_norm,
               "swa_q_norm": g_q_norm, "swa_k_norm": g_k_norm, "dn_a_log": g_a_log, "dn_dt_bias": g_dt_bias,
               "swa_sinks": g_sinks}
    me = 4 * lax.axis_index("x") + 2 * lax.axis_index("y") + lax.axis_index("c")
    outs = {}

    def finish(handle, group, name, after):
        srcs, lands = _exchange_wait(handle, after, True, name)
        for n, src, land in zip(group, srcs, lands):
            parts = _own_slot(land, lax.dynamic_index_in_dim(src, me, 0, keepdims=False))
            outs[n] = _adam_update(parts, args[n], args["m_" + n], args["v_" + n], "adam_" + n)

    finish(ffn_handle, ("w_down", "w_gate", "w_up"), "scatter_ffn_wait", dx)
    finish(mix_handle, ("w_out", "w_branch_dn", "w_branch_swa"), "scatter_mix_wait", dx)
    sheets, conv_all = _all_gather_direct([_small_pack(g_small, loss_local), _pad_to(g_conv, (8, DN_QKV))],
                                          "all_gather_small",
                                          after=outs["w_up"][0])
    in_parts = []
    for part, handle in enumerate(in_handles):
        (src,), (land,) = _exchange_wait(handle, sheets, True, "scatter_in_%d_wait" % part)
        in_parts.append(_own_slot(land, lax.dynamic_index_in_dim(src, me, 0, keepdims=False)))
    outs["w_in"] = _adam_update(in_parts, w_in, m_w_in, v_w_in, "adam_w_in")
    conv_parts = lax.dynamic_slice(conv_all, (0, 0, me * CONV_SHARD[1]), (N_DEV,) + CONV_SHARD)
    outs["dn_conv"] = _adam_update(conv_parts, dn_conv, m_dn_conv, v_dn_conv, "adam_dn_conv")
    small_outs, loss = _small_update(sheets, {n: args[n] for n in SMALL}, {n: args["m_" + n] for n in SMALL},
                                     {n: args["v_" + n] for n in SMALL})
    outs.update(small_outs)

    names = ("attn_norm", "w_in", "dn_conv", "dn_a_log", "dn_dt_bias", "dn_out_norm", "swa_q_norm", "swa_k_norm",
             "swa_sinks", "rel_bias", "w_branch_dn", "w_branch_swa", "w_out", "ffn_norm", "w_gate", "w_up", "w_down")
    results = []
    for kind in range(4):
        results += [outs[n][kind].reshape(args[n].shape) for n in names]

    return (loss.reshape(()), dx.reshape(x.shape), *results)
```

```python
import math

import numpy as np
import jax
import jax.numpy as jnp
from jax import lax
from jax.experimental import pallas as pl
from jax.experimental.pallas import tpu as pltpu

F32 = jnp.float32
BF16 = jnp.bfloat16
HI = lax.Precision.HIGHEST

D_MODEL = 1024
DN_HEADS = 4
DN_DIM = 128
DN_WIDTH = 512
DN_QKV = 1536
DN_CONV = 4
CHUNK = 64
SWA_HEADS = 8
SWA_KV = 2
SWA_GROUP = 4
SWA_DIM = 64
SWA_WIDTH = 512
SWA_KVW = 128
WINDOW = 128
BLOCK = 128
REL_BUCKETS = 32
REL_MAX_DIST = 128
D_FF = 2816
D_IN = 4872
EPS = 1e-6
N_DEV = 8

ADAM_LR = 0.001
ADAM_B1 = 0.9
ADAM_B2 = 0.999
ADAM_EPS = 1e-08
ADAM_WD = 0.01
ADAM_STEP = 10

P_GATE, P_QKV, P_Z, P_SQ, P_SK, P_SV, P_BA = 0, 2048, 3584, 4096, 4608, 4736, 4864
P_WIDTH = 4992
R_QKV, R_Z, R_B, R_A, R_SQ, R_SK, R_SV, R_GATE = 0, 1536, 2048, 2052, 2056, 2568, 2696, 2824

VMEM_LIMIT = 56 * 1024 * 1024
LANES = 128
MESH_ID = pl.DeviceIdType.MESH


def _params(sem=None):
    return pltpu.CompilerParams(dimension_semantics=sem, vmem_limit_bytes=VMEM_LIMIT)


def _pick(dim, target):
    if dim <= target:
        return dim
    t = target - target % LANES
    while t >= LANES:
        if dim % t == 0:
            return t
        t -= LANES
    return dim


_DIMS = {"nn": (((1,), (0,)), ((), ())), "nt": (((1,), (1,)), ((), ())), "tn": (((0,), (0,)), ((), ()))}


def _tile_product(a_ref, b_ref, mode, b_blocks):
    a = a_ref[...].astype(BF16)
    b = jnp.concatenate([b_ref[d] for d in range(b_ref.shape[0])], axis=1) if b_blocks else b_ref[...]
    return lax.dot_general(a, b.astype(BF16), _DIMS[mode], preferred_element_type=F32)


def _mm(pairs, mode, out_dtype, name, bm, bn, j_outer=False, b_blocks=False, out_blocks=False):
    a0, b0 = pairs[0]
    cb = b0.shape[2] if b_blocks else None
    b_shape = (b0.shape[1], N_DEV * cb) if b_blocks else b0.shape
    if mode == "nn":
        (M, K), (K2, N) = a0.shape, b_shape
    elif mode == "nt":
        (M, K), (N, K2) = a0.shape, b_shape
    else:
        (K, M), (K2, N) = a0.shape, b_shape
    bm, bn = min(bm, M), min(bn, N)
    assert K == K2 and M % bm == 0 and N % bn == 0, (name, a0.shape, b0.shape, bm, bn)
    co = N // N_DEV
    assert not out_blocks or bn % co == 0
    dims = _DIMS[mode]
    n = len(pairs)

    def body(*refs):
        o_ref = refs[2 * n]
        acc = None
        for t in range(n):
            p = _tile_product(refs[2 * t], refs[2 * t + 1], mode, b_blocks)
            acc = p if acc is None else acc + p
        if out_blocks:
            for d in range(bn // co):
                o_ref[d] = acc[:, d * co:(d + 1) * co].astype(out_dtype)
        else:
            o_ref[...] = acc.astype(out_dtype)

    def ij(f):
        return (lambda j, i: f(i, j)) if j_outer else f

    a_spec = pl.BlockSpec((K, bm), ij(lambda i, j: (0, i))) if mode == "tn" else pl.BlockSpec((bm, K), ij(lambda i, j: (i, 0)))
    if b_blocks and mode == "nt":
        b_spec = pl.BlockSpec((N_DEV, bn, cb), ij(lambda i, j: (0, j, 0)))
    elif b_blocks:
        b_spec = pl.BlockSpec((bn // cb, K, cb), ij(lambda i, j: (j, 0, 0)))
    elif mode == "nt":
        b_spec = pl.BlockSpec((bn, K), ij(lambda i, j: (j, 0)))
    else:
        b_spec = pl.BlockSpec((K, bn), ij(lambda i, j: (0, j)))
    if out_blocks:
        out_spec = pl.BlockSpec((bn // co, bm, co), ij(lambda i, j: (j, i, 0)))
        out_shape = jax.ShapeDtypeStruct((N_DEV, M, co), out_dtype)
    else:
        out_spec = pl.BlockSpec((bm, bn), ij(lambda i, j: (i, j)))
        out_shape = jax.ShapeDtypeStruct((M, N), out_dtype)
    grid = (N // bn, M // bm) if j_outer else (M // bm, N // bn)
    return pl.pallas_call(
        body, grid=grid, in_specs=[a_spec, b_spec] * n, out_specs=out_spec, out_shape=out_shape, name=name,
        compiler_params=_params(("parallel", "parallel")),
    )(*[x for pair in pairs for x in pair])


def _mm_fused(pairs, mode, name, bm, bn, epilogue, extras, out_dtypes, j_outer=False, b_blocks=False,
              sum_shape=None, wide_first=None):
    a0, b0 = pairs[0]
    cb = b0.shape[2] if b_blocks else None
    b_shape = (b0.shape[1], N_DEV * cb) if b_blocks else b0.shape
    if mode == "nn":
        (M, K), (K2, N) = a0.shape, b_shape
    else:
        (M, K), (N, K2) = a0.shape, b_shape
    bm, bn = min(bm, M), min(bn, N)
    assert mode in ("nn", "nt") and K == K2 and M % bm == 0 and N % bn == 0, (name, a0.shape, b0.shape)
    dims = _DIMS[mode]
    n, ne, no = len(pairs), len(extras), len(out_dtypes)

    def body(*refs):
        prods = [_tile_product(refs[2 * t], refs[2 * t + 1], mode, b_blocks) for t in range(n)]
        results = epilogue(prods, [r[...] for r in refs[2 * n:2 * n + ne]])
        out_refs = refs[2 * n + ne:]
        for o_ref, val, dt in zip(out_refs, results, out_dtypes):
            o_ref[...] = val.astype(dt)
        if sum_shape is not None:
            s_ref = out_refs[no]

            @pl.when((pl.program_id(0) == 0) & (pl.program_id(1) == 0))
            def _():
                s_ref[...] = jnp.zeros_like(s_ref)

            s_ref[...] += results[no]

    def ij(f):
        return (lambda j, i: f(i, j)) if j_outer else f

    a_spec = pl.BlockSpec((bm, K), ij(lambda i, j: (i, 0)))
    once = dict(pipeline_mode=pl.Buffered(1)) if bn == N else {}
    if b_blocks and mode == "nt":
        b_spec = pl.BlockSpec((N_DEV, bn, cb), ij(lambda i, j: (0, j, 0)), **once)
    elif b_blocks:
        b_spec = pl.BlockSpec((bn // cb, K, cb), ij(lambda i, j: (j, 0, 0)), **once)
    elif mode == "nt":
        b_spec = pl.BlockSpec((bn, K), ij(lambda i, j: (j, 0)), **once)
    else:
        b_spec = pl.BlockSpec((K, bn), ij(lambda i, j: (0, j)), **once)
    e_specs = [pl.BlockSpec((1, bn), ij(lambda i, j: (0, j))) if first is None
               else pl.BlockSpec((bm, bn), ij(lambda i, j, first=first: (i, first + j))) for _, first in extras]
    tile = pl.BlockSpec((bm, bn), ij(lambda i, j: (i, j)))
    out_specs = [tile] * no
    out_shape = [jax.ShapeDtypeStruct((M, N), dt) for dt in out_dtypes]
    if wide_first is not None:
        assert bn == N
        out_specs[0] = pl.BlockSpec((bm, wide_first[1]), ij(lambda i, j: (i, 0)))
        out_shape[0] = jax.ShapeDtypeStruct((M, wide_first[0]), out_dtypes[0])
    if sum_shape is not None:
        assert sum_shape[1] in (1, bn) and (sum_shape[1] == 1 or bn == N)
        out_specs.append(_full(sum_shape))
        out_shape.append(jax.ShapeDtypeStruct(sum_shape, F32))
    grid = (N // bn, M // bm) if j_outer else (M // bm, N // bn)
    sem = ("arbitrary", "arbitrary") if sum_shape is not None else ("parallel", "parallel")
    return pl.pallas_call(
        body, grid=grid, in_specs=[a_spec, b_spec] * n + e_specs, out_specs=out_specs, out_shape=out_shape,
        name=name, compiler_params=_params(sem),
    )(*[x for pair in pairs for x in pair], *[arr for arr, _ in extras])


def _rms(x, gain):
    return x * lax.rsqrt(jnp.mean(x * x, axis=-1, keepdims=True) + EPS) * gain


def _silu(x):
    return x * jax.nn.sigmoid(x)


def _act(g, u):
    return _silu(g) * u


def _merge(g0, g1, a_dn, a_swa):
    return jax.nn.sigmoid(g0) * a_dn + jax.nn.sigmoid(g1) * a_swa


def _dn_post(c, is_v, q_scale):
    a = _silu(c)
    rs = lax.rsqrt(jnp.sum(a * a, axis=-1, keepdims=True) + EPS) * q_scale
    return a * jnp.where(is_v, 1.0, rs)


def _dn_out(o, z, gain):
    return _rms(o, gain) * _silu(z)


def _dot(a, b, dims=_DIMS["nn"], hi=False):
    if a.ndim == 3 or b.ndim == 3:
        batch = a.shape[0] if a.ndim == 3 else b.shape[0]
        a = a if a.ndim == 3 else jnp.broadcast_to(a, (batch,) + a.shape)
        b = b if b.ndim == 3 else jnp.broadcast_to(b, (batch,) + b.shape)
        ((ca,), (cb,)), _ = dims
        dims = (((ca + 1,), (cb + 1,)), ((0,), (0,)))
    if hi:
        return lax.dot_general(a, b, dims, precision=HI, preferred_element_type=F32)
    return lax.dot_general(a.astype(BF16), b.astype(BF16), dims, preferred_element_type=F32)


def _pieces(x):
    hi = x.astype(BF16)
    r1 = x - hi.astype(F32)
    mid = r1.astype(BF16)
    return hi, mid, (r1 - mid.astype(F32)).astype(BF16)


def _sel_left_impl(m, x):
    mb = m.astype(BF16)
    hi, mid, lo = _pieces(x)
    return _dot(mb, hi) + (_dot(mb, mid) + _dot(mb, lo))


@jax.custom_vjp
def _sel_left(m, mt, x):
    return _sel_left_impl(m, x)


_sel_left.defvjp(lambda m, mt, x: (_sel_left_impl(m, x), (m, mt)),
                 lambda res, ct: (jnp.zeros_like(res[0]), jnp.zeros_like(res[1]), _sel_left_impl(res[1], ct)))


def _sel_right_impl(x, s):
    sb = s.astype(BF16)
    hi, mid, lo = _pieces(x)
    return _dot(hi, sb) + (_dot(mid, sb) + _dot(lo, sb))


@jax.custom_vjp
def _sel_right(x, s, st):
    return _sel_right_impl(x, s)


_sel_right.defvjp(lambda x, s, st: (_sel_right_impl(x, s), (s, st)),
                  lambda res, ct: (_sel_right_impl(ct, res[1]), jnp.zeros_like(res[0]), jnp.zeros_like(res[1])))


def _sel_nt_impl(s, x):
    sb = s.astype(BF16)
    hi, mid, lo = _pieces(x)
    return _dot(sb, hi, _DIMS["nt"]) + (_dot(sb, mid, _DIMS["nt"]) + _dot(sb, lo, _DIMS["nt"]))


def _sel_tn_impl(x, s):
    sb = s.astype(BF16)
    hi, mid, lo = _pieces(x)
    return _dot(hi, sb, _DIMS["tn"]) + (_dot(mid, sb, _DIMS["tn"]) + _dot(lo, sb, _DIMS["tn"]))


@jax.custom_vjp
def _sel_nt(s, x):
    return _sel_nt_impl(s, x)


_sel_nt.defvjp(lambda s, x: (_sel_nt_impl(s, x), s),
               lambda s, ct: (jnp.zeros_like(s), _sel_tn_impl(ct, s)))


def _dot3_impl(a, b):
    a_hi, a_lo, _ = _pieces(a)
    b_hi, b_lo, _ = _pieces(b)
    return _dot(a_hi, b_hi) + (_dot(a_hi, b_lo) + _dot(a_lo, b_hi))


@jax.custom_vjp
def _dot3(a, b):
    return _dot3_impl(a, b)


_dot3.defvjp(lambda a, b: (_dot3_impl(a, b), (a, b)),
             lambda res, ct: (_dot(ct, res[1], _DIMS["nt"]), _dot(res[0], ct, _DIMS["tn"])))


def _inv_impl(a, eye, strict):
    t = eye - a
    p = _dot(a, a)
    for level in range(5):
        t = t + _dot(t, p)
        if level < 4:
            p = _dot(p, p)
    t = t + _dot(t, eye - t - _dot3_impl(a, t))
    return jnp.where(strict > 0.5, t, eye)


@jax.custom_vjp
def _inv_given(a, t):
    return t.astype(F32)


_inv_given.defvjp(lambda a, t: (t.astype(F32), t),
                  lambda t, ct: (-_dot(_dot(t, ct, _DIMS["tn"]), t, _DIMS["nt"]), jnp.zeros_like(t)))


@jax.custom_vjp
def _lanes_join(a, b):
    return jnp.concatenate([a, b], axis=-1)


_lanes_join.defvjp(lambda a, b: (jnp.concatenate([a, b], axis=-1), None),
                   lambda _, ct: (ct[..., :ct.shape[-1] // 2], ct[..., ct.shape[-1] // 2:]))


@jax.custom_vjp
def _lanes_halves(y):
    h = y.shape[-1] // 2
    return y[..., :h], y[..., h:]


_lanes_halves.defvjp(lambda y: ((y[..., :y.shape[-1] // 2], y[..., y.shape[-1] // 2:]), None),
                     lambda _, ct: (jnp.concatenate(ct, axis=-1),))

GROUP = 4
GROUP_ROWS = GROUP * CHUNK


def _block_consts(n):
    ii = lax.broadcasted_iota(jnp.int32, (n, n), 0)
    jj = lax.broadcasted_iota(jnp.int32, (n, n), 1)
    shift = CHUNK.bit_length() - 1
    same = jnp.right_shift(ii, shift) == jnp.right_shift(jj, shift)
    return same & (ii >= jj), same & (ii <= jj), same & (ii > jj), same, ii == jj


def _lane0(n):
    s = (lax.broadcasted_iota(jnp.int32, (LANES, n), 0) == 0).astype(F32)
    st = (lax.broadcasted_iota(jnp.int32, (n, LANES), 1) == 0).astype(F32)
    return s, st


def _dn_group(q, k, v, g, beta, t_saved=None):
    n = GROUP_ROWS
    low_b, upp_b, strict_b, _, eye_b = _block_consts(n)
    low, upp, eye = low_b.astype(F32), upp_b.astype(F32), eye_b.astype(F32)
    gc = _sel_left(low, upp, g)
    per_chunk = (g.shape[0], GROUP, CHUNK, LANES)
    g_last = jnp.sum(g.reshape(per_chunk), axis=2, keepdims=True)
    gl = jnp.broadcast_to(g_last, per_chunk).reshape(g.shape)
    s, st = _lane0(n)
    col = _sel_right(gc, s, st)
    row = _sel_nt(st, gc)
    decay = jnp.exp(jnp.where(low_b, col - row, -jnp.inf))
    kb = k * beta
    vb = v * beta
    a = jnp.where(strict_b, _dot(kb, k, _DIMS["nt"]) * decay, 0.0)
    t = _inv_impl(a, eye, strict_b.astype(F32)) if t_saved is None else _inv_given(a, t_saved)
    u, w = _lanes_halves(_dot3(t, _lanes_join(vb, kb * jnp.exp(gc))))
    fold = (jnp.bitwise_and(lax.broadcasted_iota(jnp.int32, (n, CHUNK), 0), CHUNK - 1)
            == lax.broadcasted_iota(jnp.int32, (n, CHUNK), 1)).astype(F32)
    fold_t = (jnp.bitwise_and(lax.broadcasted_iota(jnp.int32, (CHUNK, n), 1), CHUNK - 1)
              == lax.broadcasted_iota(jnp.int32, (CHUNK, n), 0)).astype(F32)
    qk = _sel_right(_dot(q, k, _DIMS["nt"]) * decay, fold, fold_t)
    return u, w, q * jnp.exp(gc), k * jnp.exp(gl - gc), qk, jnp.exp(g_last), t


def _dn_step(s, u, w, qe, kd, qk, egl):
    v_new = u - _dot(w, s)
    o = _dot(qe, s) + _dot(qk, v_new)
    s_new = s * egl + _dot(kd, v_new, _DIMS["tn"])
    return s_new, o


def _swa_block(q, kband, vband, qg, kg, sinks, band):
    kn = _rms(kband, kg)
    qn = _rms(q, qg) * (SWA_DIM ** -0.5)
    logits = _dot(qn, kn, _DIMS["nt"]) + band
    m = lax.stop_gradient(jnp.maximum(jnp.max(logits, axis=-1, keepdims=True), sinks))
    p = jnp.exp(logits - m)
    denom = jnp.sum(p, axis=-1, keepdims=True) + jnp.exp(sinks - m)
    return _dot(p * (1.0 / denom), vband)


def _adamw(w, g, m, v):
    m = ADAM_B1 * m + (1.0 - ADAM_B1) * g
    v = ADAM_B2 * v + (1.0 - ADAM_B2) * jnp.square(g)
    m_hat = m / (1.0 - ADAM_B1 ** ADAM_STEP)
    v_hat = v / (1.0 - ADAM_B2 ** ADAM_STEP)
    delta = -ADAM_LR * (m_hat / (jnp.sqrt(v_hat) + ADAM_EPS) + ADAM_WD * w)
    return delta, m, v


def _row(tm, c, cb=0):
    return pl.BlockSpec((tm, c), lambda i, cb=cb: (i, cb))


def _full(shape):
    nd = len(shape)
    return pl.BlockSpec(shape, lambda *_, nd=nd: (0,) * nd)


def _norm_fwd(x, gain, name, tm=1024):
    S = x.shape[0]

    def body(x_ref, g_ref, h_ref):
        h_ref[...] = _rms(x_ref[...], g_ref[...]).astype(BF16)

    return pl.pallas_call(
        body, grid=(S // tm,), in_specs=[_row(tm, D_MODEL), _full((1, D_MODEL))],
        out_specs=_row(tm, D_MODEL), out_shape=jax.ShapeDtypeStruct((S, D_MODEL), BF16),
        name=name, compiler_params=_params(("parallel",)))(x, gain)


def _shift_down(x, s):
    row = lax.broadcasted_iota(jnp.int32, x.shape, 0)
    return jnp.where(row >= s, pltpu.roll(x, s, axis=0), 0.0)


def _shift_up(x, s):
    n = x.shape[0]
    row = lax.broadcasted_iota(jnp.int32, x.shape, 0)
    return jnp.where(row < n - s, pltpu.roll(x, n - s, axis=0), 0.0)


def _conv(x, w):
    out = w[DN_CONV - 1:DN_CONV] * x
    for s in range(1, DN_CONV):
        out = out + w[DN_CONV - 1 - s:DN_CONV - s] * _shift_down(x, s)
    return out


def _dn_conv_fwd(proj, conv_w):
    S = proj.shape[0]
    nb = DN_QKV // LANES

    def body(x_ref, w_ref, o_ref):
        j = pl.program_id(0)
        q_scale = jnp.where(j < DN_HEADS, DN_DIM ** -0.5, 1.0).astype(F32)
        o_ref[...] = _dn_post(_conv(x_ref[...], w_ref[...]), j >= 2 * DN_HEADS, q_scale)

    return pl.pallas_call(
        body, grid=(nb,),
        in_specs=[pl.BlockSpec((S, LANES), lambda j: (0, P_QKV // LANES + j)),
                  pl.BlockSpec((DN_CONV, LANES), lambda j: (0, j))],
        out_specs=pl.BlockSpec((S, LANES), lambda j: (0, j)),
        out_shape=jax.ShapeDtypeStruct((S, DN_QKV), F32), name="dn_conv_fwd",
        compiler_params=_params(("parallel",)))(proj, conv_w)


def _dn_conv_bwd(proj, conv_w, dqkvn, dproj):
    S = proj.shape[0]
    nb = DN_QKV // LANES

    def body(x_ref, w_ref, d_ref, _, dx_ref, dw_ref):
        j = pl.program_id(0)
        q_scale = jnp.where(j < DN_HEADS, DN_DIM ** -0.5, 1.0).astype(F32)
        x = x_ref[...]
        w = w_ref[...]
        _, vjp = jax.vjp(lambda c: _dn_post(c, j >= 2 * DN_HEADS, q_scale), _conv(x, w))
        (dc,) = vjp(d_ref[0])
        dx = w[DN_CONV - 1:DN_CONV] * dc
        dw_ref[DN_CONV - 1:DN_CONV, :] = jnp.sum(dc * x, axis=0, keepdims=True)
        for s in range(1, DN_CONV):
            dx = dx + w[DN_CONV - 1 - s:DN_CONV - s] * _shift_up(dc, s)
            dw_ref[DN_CONV - 1 - s:DN_CONV - s, :] = jnp.sum(dc * _shift_down(x, s), axis=0, keepdims=True)
        dx_ref[...] = dx.astype(BF16)

    return pl.pallas_call(
        body, grid=(nb,),
        in_specs=[pl.BlockSpec((S, LANES), lambda j: (0, P_QKV // LANES + j)),
                  pl.BlockSpec((DN_CONV, LANES), lambda j: (0, j)),
                  pl.BlockSpec((1, S, LANES), lambda j: (lax.div(j, DN_HEADS), 0, lax.rem(j, DN_HEADS))),
                  pl.BlockSpec(memory_space=pl.ANY)],
        out_specs=[pl.BlockSpec((S, LANES), lambda j: (0, P_QKV // LANES + j)),
                   pl.BlockSpec((DN_CONV, LANES), lambda j: (0, j))],
        out_shape=[jax.ShapeDtypeStruct(dproj.shape, dproj.dtype), jax.ShapeDtypeStruct((DN_CONV, DN_QKV), F32)],
        input_output_aliases={3: 0},
        name="dn_conv_bwd", compiler_params=_params(("parallel",)))(proj, conv_w, dqkvn, dproj)


def _expanders():
    eb = np.zeros((LANES, DN_WIDTH), np.float32)
    ea = np.zeros((LANES, DN_WIDTH), np.float32)
    for h in range(DN_HEADS):
        eb[h, h * DN_DIM:(h + 1) * DN_DIM] = 1.0
        ea[DN_HEADS + h, h * DN_DIM:(h + 1) * DN_DIM] = 1.0
    return jnp.asarray(eb), jnp.asarray(ea), jnp.asarray(eb.T), jnp.asarray(ea.T)


def _dn_gate_args(a_log, dt_bias):
    alog = jnp.repeat(a_log.reshape(1, DN_HEADS), DN_DIM, axis=1)
    dtb = _pad_to(jnp.pad(dt_bias.reshape(1, DN_HEADS), ((0, 0), (DN_HEADS, 0))), (1, LANES))
    return _expanders() + (alog, dtb)


def _dn_gate_specs(tm):
    return [_row(tm, LANES, P_BA // LANES), _full((LANES, DN_WIDTH)), _full((LANES, DN_WIDTH)),
            _full((DN_WIDTH, LANES)), _full((DN_WIDTH, LANES)), _full((1, DN_WIDTH)), _full((1, LANES))]


def _dn_gate_fn(ba, eb, ea, ebt, eat, alog, dtb):
    beta = _sel_right(jax.nn.sigmoid(ba), eb, ebt)
    g = -jnp.exp(alog) * _sel_right(jax.nn.softplus(ba + dtb), ea, eat)
    return beta, g


def _dn_gate_fwd(proj, a_log, dt_bias, tm=1024):
    S = proj.shape[0]
    args = _dn_gate_args(a_log, dt_bias)

    def body(ba_ref, eb_ref, ea_ref, ebt_ref, eat_ref, al_ref, dt_ref, beta_ref, g_ref):
        beta, g = _dn_gate_fn(ba_ref[...], eb_ref[...], ea_ref[...], ebt_ref[...], eat_ref[...], al_ref[...],
                              dt_ref[...])
        beta_ref[...] = beta
        g_ref[...] = g

    return pl.pallas_call(
        body, grid=(S // tm,), in_specs=_dn_gate_specs(tm), out_specs=[_row(tm, DN_WIDTH), _row(tm, DN_WIDTH)],
        out_shape=[jax.ShapeDtypeStruct((S, DN_WIDTH), F32), jax.ShapeDtypeStruct((S, DN_WIDTH), F32)],
        name="dn_gate_fwd", compiler_params=_params(("parallel",)))(proj, *args)


def _dn_gate_bwd(proj, a_log, dt_bias, dbeta, dg, dproj, tm=1024):
    S = proj.shape[0]
    args = _dn_gate_args(a_log, dt_bias)

    def body(ba_ref, eb_ref, ea_ref, ebt_ref, eat_ref, al_ref, dt_ref, dbeta_ref, dg_ref, _, dba_ref, dal_ref,
             ddt_ref):
        eb, ea, ebt, eat = eb_ref[...], ea_ref[...], ebt_ref[...], eat_ref[...]
        _, vjp = jax.vjp(lambda ba, al, dt: _dn_gate_fn(ba, eb, ea, ebt, eat, al, dt), ba_ref[...], al_ref[...],
                         dt_ref[...])
        dba, dal, ddt = vjp((dbeta_ref[...], dg_ref[...]))
        dba_ref[...] = dba.astype(BF16)

        @pl.when(pl.program_id(0) == 0)
        def _():
            dal_ref[...] = jnp.zeros_like(dal_ref)
            ddt_ref[...] = jnp.zeros_like(ddt_ref)

        dal_ref[...] += dal
        ddt_ref[...] += ddt

    return pl.pallas_call(
        body, grid=(S // tm,),
        in_specs=_dn_gate_specs(tm) + [_row(tm, DN_WIDTH), _row(tm, DN_WIDTH), pl.BlockSpec(memory_space=pl.ANY)],
        out_specs=[_row(tm, LANES, P_BA // LANES), _full((1, DN_WIDTH)), _full((1, LANES))],
        out_shape=[jax.ShapeDtypeStruct(dproj.shape, dproj.dtype), jax.ShapeDtypeStruct((1, DN_WIDTH), F32),
                   jax.ShapeDtypeStruct((1, LANES), F32)],
        input_output_aliases={len(args) + 3: 0},
        name="dn_gate_bwd", compiler_params=_params(("arbitrary",)))(proj, *args, dbeta, dg, dproj)


PREP_GROUPS = 4
PREP_CHUNKS = GROUP * PREP_GROUPS


def _dn_prep_specs():
    rows = PREP_CHUNKS * CHUNK
    q = pl.BlockSpec((rows, LANES), lambda h, c: (c, h))
    k = pl.BlockSpec((rows, LANES), lambda h, c: (c, DN_HEADS + h))
    v = pl.BlockSpec((rows, LANES), lambda h, c: (c, 2 * DN_HEADS + h))
    qk = pl.BlockSpec((1, rows, CHUNK), lambda h, c: (h, c, 0))
    egl = pl.BlockSpec((1, PREP_CHUNKS, 1, LANES), lambda h, c: (h, c, 0, 0))
    return q, k, v, qk, egl


def _dn_prep_fwd(qkvn, g, beta):
    S = qkvn.shape[0]
    nc = S // CHUNK
    q, k, v, qks, egl = _dn_prep_specs()

    def body(q_ref, k_ref, v_ref, g_ref, b_ref, u_ref, w_ref, qe_ref, kd_ref, qk_ref, egl_ref, t_ref):
        rows = PREP_CHUNKS * CHUNK
        grp = (PREP_GROUPS, GROUP_ROWS, LANES)
        u, w, qe, kd, qk, e, t = _dn_group(q_ref[...].reshape(grp), k_ref[...].reshape(grp), v_ref[...].reshape(grp),
                                           g_ref[...].reshape(grp), b_ref[...].reshape(grp))
        u_ref[...] = u.reshape(rows, LANES)
        w_ref[...] = w.reshape(rows, LANES)
        qe_ref[...] = qe.reshape(rows, LANES)
        kd_ref[...] = kd.reshape(rows, LANES)
        t_ref[0] = t.reshape(rows, GROUP_ROWS).astype(BF16)
        qk_ref[0] = qk.reshape(rows, CHUNK)
        egl_ref[0] = e.reshape(PREP_CHUNKS, 1, LANES)

    wide = jax.ShapeDtypeStruct((S, DN_WIDTH), F32)
    return pl.pallas_call(
        body, grid=(DN_HEADS, nc // PREP_CHUNKS), in_specs=[q, k, v, q, q],
        out_specs=[q, q, q, q, qks, egl, _dn_tinv_spec()],
        out_shape=[wide, wide, wide, wide, jax.ShapeDtypeStruct((DN_HEADS, S, CHUNK), F32),
                   jax.ShapeDtypeStruct((DN_HEADS, nc, 1, LANES), F32),
                   jax.ShapeDtypeStruct((DN_HEADS, S, GROUP_ROWS), BF16)],
        name="dn_prep_fwd", compiler_params=_params(("parallel", "parallel")))(qkvn, qkvn, qkvn, g, beta)


def _dn_tinv_spec():
    return pl.BlockSpec((1, PREP_CHUNKS * CHUNK, GROUP_ROWS), lambda h, c: (h, c, 0))


def _dn_prep_bwd(qkvn, g, beta, tinv, du, dw, dqe, dkd, dqk, degl):
    S = qkvn.shape[0]
    nc = S // CHUNK
    q, k, v, qks, egl = _dn_prep_specs()

    def body(q_ref, k_ref, v_ref, g_ref, b_ref, t_ref, du_ref, dw_ref, dqe_ref, dkd_ref, dqk_ref, degl_ref,
             dqkv_ref, dg_ref, db_ref):
        rows = PREP_CHUNKS * CHUNK
        grp = (PREP_GROUPS, GROUP_ROWS, LANES)
        t_saved = t_ref[0].reshape(PREP_GROUPS, GROUP_ROWS, GROUP_ROWS)
        _, vjp = jax.vjp(lambda *x: _dn_group(*x, t_saved=t_saved)[:6], q_ref[...].reshape(grp),
                         k_ref[...].reshape(grp), v_ref[...].reshape(grp), g_ref[...].reshape(grp),
                         b_ref[...].reshape(grp))
        dq, dk, dv, dg, db = vjp((du_ref[...].reshape(grp), dw_ref[...].reshape(grp), dqe_ref[...].reshape(grp),
                                  dkd_ref[...].reshape(grp), dqk_ref[0].reshape(PREP_GROUPS, GROUP_ROWS, CHUNK),
                                  degl_ref[0].reshape(PREP_GROUPS, GROUP, 1, LANES)))
        dqkv_ref[0] = dq.reshape(rows, LANES)
        dqkv_ref[1] = dk.reshape(rows, LANES)
        dqkv_ref[2] = dv.reshape(rows, LANES)
        dg_ref[...] = dg.reshape(rows, LANES)
        db_ref[...] = db.reshape(rows, LANES)

    wide = jax.ShapeDtypeStruct((S, DN_WIDTH), F32)
    rows = PREP_CHUNKS * CHUNK
    return pl.pallas_call(
        body, grid=(DN_HEADS, nc // PREP_CHUNKS), in_specs=[q, k, v, q, q, _dn_tinv_spec(), q, q, q, q, qks, egl],
        out_specs=[pl.BlockSpec((3, rows, LANES), lambda h, c: (0, c, h)), q, q],
        out_shape=[jax.ShapeDtypeStruct((3, S, DN_WIDTH), F32), wide, wide],
        name="dn_prep_bwd", compiler_params=_params(("parallel", "parallel")),
    )(qkvn, qkvn, qkvn, g, beta, tinv, du, dw, dqe, dkd, dqk, degl)


SCAN_CHUNKS = 8


def _dn_scan_specs(nc, reverse):
    nb = nc // SCAN_CHUNKS

    def cidx(c):
        return nb - 1 - c if reverse else c

    hc = pl.BlockSpec((SCAN_CHUNKS * CHUNK, DN_WIDTH), lambda c: (cidx(c), 0))
    qk = pl.BlockSpec((DN_HEADS, SCAN_CHUNKS * CHUNK, CHUNK), lambda c: (0, cidx(c), 0))
    egl = pl.BlockSpec((DN_HEADS, SCAN_CHUNKS, 1, LANES), lambda c: (0, cidx(c), 0, 0))
    st = pl.BlockSpec((DN_HEADS, SCAN_CHUNKS, DN_DIM, DN_DIM), lambda c: (0, cidx(c), 0, 0))
    return hc, qk, egl, st


def _heads(ref, i):
    return jnp.stack([ref[pl.ds(i * CHUNK, CHUNK), pl.ds(h * DN_DIM, DN_DIM)] for h in range(DN_HEADS)])


def _dn_scan_fwd(u, w, qe, kd, qk, egl):
    S = u.shape[0]
    nc = S // CHUNK
    hc, qks, egls, st = _dn_scan_specs(nc, False)

    def body(u_ref, w_ref, qe_ref, kd_ref, qk_ref, egl_ref, o_ref, st_ref, s_scr):
        @pl.when(pl.program_id(0) == 0)
        def _():
            s_scr[...] = jnp.zeros_like(s_scr)

        s = s_scr[...]
        for i in range(SCAN_CHUNKS):
            rows = pl.ds(i * CHUNK, CHUNK)
            st_ref[:, i] = s
            s, o = _dn_step(s, _heads(u_ref, i), _heads(w_ref, i), _heads(qe_ref, i), _heads(kd_ref, i),
                            qk_ref[:, rows, :], egl_ref[:, i])
            for h in range(DN_HEADS):
                o_ref[rows, pl.ds(h * DN_DIM, DN_DIM)] = o[h]
        s_scr[...] = s

    return pl.pallas_call(
        body, grid=(nc // SCAN_CHUNKS,), in_specs=[hc, hc, hc, hc, qks, egls], out_specs=[hc, st],
        out_shape=[jax.ShapeDtypeStruct((S, DN_WIDTH), F32), jax.ShapeDtypeStruct((DN_HEADS, nc, DN_DIM, DN_DIM), F32)],
        scratch_shapes=[pltpu.VMEM((DN_HEADS, DN_DIM, DN_DIM), F32)], name="dn_scan_fwd",
        compiler_params=_params(("arbitrary",)))(u, w, qe, kd, qk, egl)


def _dn_scan_bwd(u, w, qe, kd, qk, egl, states, do):
    S = u.shape[0]
    nc = S // CHUNK
    hc, qks, egls, st = _dn_scan_specs(nc, True)

    def body(u_ref, w_ref, qe_ref, kd_ref, qk_ref, egl_ref, st_ref, do_ref,
             du_ref, dw_ref, dqe_ref, dkd_ref, dqk_ref, degl_ref, ds_scr):
        @pl.when(pl.program_id(0) == 0)
        def _():
            ds_scr[...] = jnp.zeros_like(ds_scr)

        ds = ds_scr[...]
        for i in reversed(range(SCAN_CHUNKS)):
            rows = pl.ds(i * CHUNK, CHUNK)
            _, vjp = jax.vjp(_dn_step, st_ref[:, i], _heads(u_ref, i), _heads(w_ref, i), _heads(qe_ref, i),
                             _heads(kd_ref, i), qk_ref[:, rows, :], egl_ref[:, i])
            ds, du, dw, dqe, dkd, dqk, degl = vjp((ds, _heads(do_ref, i)))
            dqk_ref[:, rows, :] = dqk
            degl_ref[:, i] = degl
            for h in range(DN_HEADS):
                cols = pl.ds(h * DN_DIM, DN_DIM)
                du_ref[rows, cols] = du[h]
                dw_ref[rows, cols] = dw[h]
                dqe_ref[rows, cols] = dqe[h]
                dkd_ref[rows, cols] = dkd[h]
        ds_scr[...] = ds

    wide = jax.ShapeDtypeStruct((S, DN_WIDTH), F32)
    return pl.pallas_call(
        body, grid=(nc // SCAN_CHUNKS,), in_specs=[hc, hc, hc, hc, qks, egls, st, hc],
        out_specs=[hc, hc, hc, hc, qks, egls],
        out_shape=[wide, wide, wide, wide, jax.ShapeDtypeStruct((DN_HEADS, S, CHUNK), F32),
                   jax.ShapeDtypeStruct((DN_HEADS, nc, 1, LANES), F32)],
        scratch_shapes=[pltpu.VMEM((DN_HEADS, DN_DIM, DN_DIM), F32)], name="dn_scan_bwd",
        compiler_params=_params(("arbitrary",)))(u, w, qe, kd, qk, egl, states, do)


def _dn_out_fwd(o, proj, gain, tm=1024):
    S = o.shape[0]

    def body(o_ref, z_ref, g_ref, y_ref):
        y_ref[...] = _dn_out(o_ref[...], z_ref[...], g_ref[...]).astype(BF16)

    hs = pl.BlockSpec((tm, LANES), lambda i, h: (i, h))
    zs = pl.BlockSpec((tm, LANES), lambda i, h: (i, P_Z // LANES + h))
    return pl.pallas_call(
        body, grid=(S // tm, DN_HEADS), in_specs=[hs, zs, _full((1, DN_DIM))], out_specs=hs,
        out_shape=jax.ShapeDtypeStruct((S, DN_WIDTH), BF16), name="dn_out_fwd",
        compiler_params=_params(("parallel", "parallel")))(o, proj, gain)


_ANY = pl.BlockSpec(memory_space=pl.ANY)


def _dn_out_bwd(o, proj, gain, dy, dproj, tm=1024):
    S = o.shape[0]

    def body(o_ref, z_ref, g_ref, dy_ref, _, do_ref, dz_ref, dg_ref):
        _, vjp = jax.vjp(_dn_out, o_ref[...], z_ref[...], g_ref[...])
        do, dz, dg = vjp(dy_ref[...])
        do_ref[...] = do
        dz_ref[...] = dz.astype(BF16)

        @pl.when((pl.program_id(0) == 0) & (pl.program_id(1) == 0))
        def _():
            dg_ref[...] = jnp.zeros_like(dg_ref)

        dg_ref[...] += dg

    hs = pl.BlockSpec((tm, LANES), lambda i, h: (i, h))
    zs = pl.BlockSpec((tm, LANES), lambda i, h: (i, P_Z // LANES + h))
    return pl.pallas_call(
        body, grid=(S // tm, DN_HEADS), in_specs=[hs, zs, _full((1, DN_DIM)), hs, _ANY],
        out_specs=[hs, zs, _full((1, DN_DIM))],
        out_shape=[jax.ShapeDtypeStruct((S, DN_WIDTH), F32), jax.ShapeDtypeStruct(dproj.shape, dproj.dtype),
                   jax.ShapeDtypeStruct((1, DN_DIM), F32)],
        input_output_aliases={4: 1},
        name="dn_out_bwd", compiler_params=_params(("arbitrary", "arbitrary")))(o, proj, gain, dy, dproj)


def _rel_buckets():
    qi = np.arange(BLOCK)[:, None]
    kj = np.arange(2 * BLOCK)[None, :]
    n = np.maximum(BLOCK + qi - kj, 0)
    max_exact = REL_BUCKETS // 2
    nf = np.maximum(n, 1).astype(np.float32)
    large = max_exact + (np.log(nf / np.float32(max_exact)) / np.float32(math.log(REL_MAX_DIST / max_exact))
                         * np.float32(REL_BUCKETS - max_exact)).astype(np.int32)
    large = np.minimum(large, REL_BUCKETS - 1)
    return np.where(n < max_exact, n, large).astype(np.int32)


def _bias_fwd(rel_bias):
    buckets = jnp.asarray(_rel_buckets())

    def body(rb_ref, bk_ref, o_ref):
        bk = bk_ref[...]
        for h in range(SWA_HEADS):
            acc = jnp.zeros((BLOCK, 2 * BLOCK), F32)
            for b in range(REL_BUCKETS):
                acc = jnp.where(bk == b, rb_ref[b, h], acc)
            for first in range(2):
                o_ref[first, h] = jnp.where(_swa_mask(1 - first), acc, -jnp.inf)

    return pl.pallas_call(
        body, in_specs=[pl.BlockSpec(memory_space=pltpu.SMEM), pl.BlockSpec(memory_space=pltpu.VMEM)],
        out_specs=pl.BlockSpec(memory_space=pltpu.VMEM),
        out_shape=jax.ShapeDtypeStruct((2, SWA_HEADS, BLOCK, 2 * BLOCK), F32), name="swa_bias_fwd",
        compiler_params=_params())(rel_bias, buckets)


def _bias_bwd(dbias):
    buckets = jnp.asarray(_rel_buckets())

    def body(d_ref, bk_ref, o_ref):
        bk = bk_ref[...]
        lane = lax.broadcasted_iota(jnp.int32, (1, LANES), 1)
        for h in range(SWA_HEADS):
            d = d_ref[h]
            row = jnp.zeros((1, LANES), F32)
            for b in range(REL_BUCKETS):
                part = jnp.sum(jnp.where(bk == b, d, 0.0), axis=1, keepdims=True)
                row = jnp.where(lane == b, jnp.sum(part, axis=0, keepdims=True), row)
            o_ref[h:h + 1, :] = row

    return pl.pallas_call(
        body, in_specs=[pl.BlockSpec(memory_space=pltpu.VMEM), pl.BlockSpec(memory_space=pltpu.VMEM)],
        out_specs=pl.BlockSpec(memory_space=pltpu.VMEM),
        out_shape=jax.ShapeDtypeStruct((SWA_HEADS, LANES), F32), name="swa_bias_bwd",
        compiler_params=_params())(dbias, buckets)


def _swa_mask(n):
    qi = lax.broadcasted_iota(jnp.int32, (BLOCK, 2 * BLOCK), 0)
    kj = lax.broadcasted_iota(jnp.int32, (BLOCK, 2 * BLOCK), 1)
    dist = BLOCK + qi - kj
    return (dist >= 0) & (dist < WINDOW) & ((n > 0) | (kj >= BLOCK))


def _swa_in_specs():
    q = pl.BlockSpec((BLOCK, SWA_WIDTH), lambda n: (n, P_SQ // SWA_WIDTH))
    kc = pl.BlockSpec((BLOCK, SWA_KVW), lambda n: (n, P_SK // SWA_KVW))
    kp = pl.BlockSpec((BLOCK, SWA_KVW), lambda n: (jnp.maximum(n - 1, 0), P_SK // SWA_KVW))
    vc = pl.BlockSpec((BLOCK, SWA_KVW), lambda n: (n, P_SV // SWA_KVW))
    vp = pl.BlockSpec((BLOCK, SWA_KVW), lambda n: (jnp.maximum(n - 1, 0), P_SV // SWA_KVW))
    band = pl.BlockSpec((None, SWA_HEADS, BLOCK, 2 * BLOCK), lambda n: (jnp.where(n == 0, 1, 0), 0, 0, 0))
    small = [_full((1, SWA_DIM)), _full((1, SWA_DIM)), _full((1, SWA_HEADS)), band]
    return [q, kp, kc, vp, vc] + small


def _swa_load(q_ref, kp_ref, kc_ref, vp_ref, vc_ref, s_ref):
    q = jnp.stack([q_ref[:, pl.ds(h * SWA_DIM, SWA_DIM)] for h in range(SWA_HEADS)])
    kbands, vbands = [], []
    for kv in range(SWA_KV):
        cols = pl.ds(kv * SWA_DIM, SWA_DIM)
        kbands += [jnp.concatenate([kp_ref[:, cols], kc_ref[:, cols]], axis=0)] * SWA_GROUP
        vbands += [jnp.concatenate([vp_ref[:, cols], vc_ref[:, cols]], axis=0)] * SWA_GROUP
    sinks = jnp.stack([s_ref[:, pl.ds(h, 1)] for h in range(SWA_HEADS)])
    return q, jnp.stack(kbands), jnp.stack(vbands), sinks


def _swa_fwd(proj, q_gain, k_gain, sinks, bias):
    S = proj.shape[0]

    def body(q_ref, kp_ref, kc_ref, vp_ref, vc_ref, qg_ref, kg_ref, s_ref, bias_ref, y_ref):
        q, kband, vband, sk = _swa_load(q_ref, kp_ref, kc_ref, vp_ref, vc_ref, s_ref)
        out = _swa_block(q, kband, vband, qg_ref[...], kg_ref[...], sk, bias_ref[...])
        for h in range(SWA_HEADS):
            y_ref[:, pl.ds(h * SWA_DIM, SWA_DIM)] = out[h].astype(BF16)

    return pl.pallas_call(
        body, grid=(S // BLOCK,), in_specs=_swa_in_specs(),
        out_specs=pl.BlockSpec((BLOCK, SWA_WIDTH), lambda n: (n, 0)),
        out_shape=jax.ShapeDtypeStruct((S, SWA_WIDTH), BF16), name="swa_fwd",
        compiler_params=_params(("parallel",)))(proj, proj, proj, proj, proj, q_gain, k_gain, sinks, bias)


def _swa_bwd(proj, q_gain, k_gain, sinks, bias, dy, dproj):
    S = proj.shape[0]

    def body(q_ref, kp_ref, kc_ref, vp_ref, vc_ref, qg_ref, kg_ref, s_ref, bias_ref, dy_ref, _,
             dq_ref, dk_ref, dv_ref, dqg_ref, dkg_ref, ds_ref, dbias_ref):
        n = pl.program_id(0)

        @pl.when(n == 0)
        def _():
            for r in (dk_ref, dv_ref, dqg_ref, dkg_ref, ds_ref, dbias_ref):
                r[...] = jnp.zeros_like(r)

        cur = pl.ds(pl.multiple_of(n * BLOCK, BLOCK), BLOCK)
        prev = pl.ds(pl.multiple_of(jnp.maximum(n - 1, 0) * BLOCK, BLOCK), BLOCK)
        q, kband, vband, sk = _swa_load(q_ref, kp_ref, kc_ref, vp_ref, vc_ref, s_ref)
        _, vjp = jax.vjp(_swa_block, q, kband, vband, qg_ref[...], kg_ref[...], sk, bias_ref[...])
        dy = jnp.stack([dy_ref[:, pl.ds(h * SWA_DIM, SWA_DIM)] for h in range(SWA_HEADS)])
        dq, dkb, dvb, dqg, dkg, dsk, dbs = vjp(dy)
        for h in range(SWA_HEADS):
            dq_ref[:, pl.ds(h * SWA_DIM, SWA_DIM)] = dq[h].astype(BF16)
            ds_ref[:, pl.ds(h, 1)] += dsk[h]
        dbias_ref[...] += dbs
        dqg_ref[...] += dqg
        dkg_ref[...] += dkg
        for kv in range(SWA_KV):
            cols = pl.ds(kv * SWA_DIM, SWA_DIM)
            group = range(kv * SWA_GROUP, (kv + 1) * SWA_GROUP)
            dk_kv = sum(dkb[h] for h in group)
            dv_kv = sum(dvb[h] for h in group)
            dk_ref[cur, cols] += dk_kv[BLOCK:]
            dv_ref[cur, cols] += dv_kv[BLOCK:]

            @pl.when(n > 0)
            def _(cols=cols, dk_kv=dk_kv, dv_kv=dv_kv):
                dk_ref[prev, cols] += dk_kv[:BLOCK]
                dv_ref[prev, cols] += dv_kv[:BLOCK]

    return pl.pallas_call(
        body, grid=(S // BLOCK,),
        in_specs=_swa_in_specs() + [pl.BlockSpec((BLOCK, SWA_WIDTH), lambda n: (n, 0)),
                                    pl.BlockSpec(memory_space=pl.ANY)],
        out_specs=[pl.BlockSpec((BLOCK, SWA_WIDTH), lambda n: (n, P_SQ // SWA_WIDTH)), _full((S, SWA_KVW)),
                   _full((S, SWA_KVW)), _full((1, SWA_DIM)), _full((1, SWA_DIM)), _full((1, SWA_HEADS)),
                   _full((SWA_HEADS, BLOCK, 2 * BLOCK))],
        out_shape=[jax.ShapeDtypeStruct(dproj.shape, dproj.dtype), jax.ShapeDtypeStruct((S, SWA_KVW), F32),
                   jax.ShapeDtypeStruct((S, SWA_KVW), F32), jax.ShapeDtypeStruct((1, SWA_DIM), F32),
                   jax.ShapeDtypeStruct((1, SWA_DIM), F32), jax.ShapeDtypeStruct((1, SWA_HEADS), F32),
                   jax.ShapeDtypeStruct((SWA_HEADS, BLOCK, 2 * BLOCK), F32)],
        input_output_aliases={10: 0},
        name="swa_bwd", compiler_params=_params(("arbitrary",)),
    )(proj, proj, proj, proj, proj, q_gain, k_gain, sinks, bias, dy, dproj)


def _kv_into(dproj, dk, dv, tm=1024):
    S = dk.shape[0]

    def body(dk_ref, dv_ref, _, o_ref):
        o_ref[:, :SWA_KVW] = dk_ref[...].astype(BF16)
        o_ref[:, SWA_KVW:] = dv_ref[...].astype(BF16)

    return pl.pallas_call(
        body, grid=(S // tm,), in_specs=[_row(tm, SWA_KVW), _row(tm, SWA_KVW), pl.BlockSpec(memory_space=pl.ANY)],
        out_specs=_row(tm, 2 * SWA_KVW, P_SK // (2 * SWA_KVW)),
        out_shape=jax.ShapeDtypeStruct(dproj.shape, dproj.dtype), input_output_aliases={2: 0},
        name="swa_kv_into", compiler_params=_params(("parallel",)))(dk, dv, dproj)


def _position():
    return lax.axis_index("x"), lax.axis_index("y"), lax.axis_index("c")


def _all_gather(shards, name="all_gather_weights"):
    na = len(shards)

    def body(*refs):
        x_refs, out_refs = refs[:na], refs[na:2 * na]
        send_sems, recv_sems, local_sems = refs[2 * na:]
        x, y, c = _position()
        me, sibling = (x, y, c), (x, y, 1 - c)
        chips = [(1 - x, y), (x, 1 - y), (1 - x, 1 - y)]

        def copy(a, k, block, to, own=False):
            px, py, pc = block
            slot = out_refs[a].at[4 * px + 2 * py + pc]
            return pltpu.make_async_remote_copy(
                src_ref=x_refs[a] if own else slot, dst_ref=slot, send_sem=send_sems.at[7 * a + k],
                recv_sem=recv_sems.at[7 * a + k], device_id=to, device_id_type=MESH_ID)

        mine = [pltpu.make_async_copy(x_refs[a], out_refs[a].at[4 * x + 2 * y + c], local_sems.at[a])
                for a in range(na)]
        for cp in mine:
            cp.start()
        first = []
        for a in range(na):
            first.append(copy(a, 0, me, sibling, own=True))
            first += [copy(a, 1 + j, me, (*chip, c), own=True) for j, chip in enumerate(chips)]
        for cp in first:
            cp.start()
        passed = []
        for j, chip in enumerate(chips):
            for a in range(na):
                copy(a, 1 + j, (*chip, c), me).wait_recv()
                passed.append(copy(a, 4 + j, (*chip, c), sibling))
                passed[-1].start()
        for a in range(na):
            copy(a, 0, sibling, me).wait_recv()
            for j, chip in enumerate(chips):
                copy(a, 4 + j, (*chip, 1 - c), me).wait_recv()
        for cp in first + passed:
            cp.wait_send()
        for cp in mine:
            cp.wait()

    return pl.pallas_call(
        body, in_specs=[pl.BlockSpec(memory_space=pl.ANY)] * na, out_specs=[pl.BlockSpec(memory_space=pl.ANY)] * na,
        out_shape=[jax.ShapeDtypeStruct((N_DEV,) + s.shape, s.dtype) for s in shards],
        scratch_shapes=[pltpu.SemaphoreType.DMA((7 * na,)), pltpu.SemaphoreType.DMA((7 * na,)),
                        pltpu.SemaphoreType.DMA((na,))],
        name=name)(*shards)


_HBM = pl.BlockSpec(memory_space=pltpu.HBM)
_SEM = pl.BlockSpec(memory_space=pltpu.SEMAPHORE)
_DATAFLOW = pltpu.SideEffectType.DATAFLOW_SIDE_EFFECTING


def _peers(x, y, c):
    out = []
    for k in range(1, N_DEV):
        px, py, pc = x ^ (k >> 2), y ^ ((k >> 1) & 1), c ^ (k & 1)
        out.append(((px, py, pc), 4 * px + 2 * py + pc))
    return out


def _split_copies(src_refs, land_refs, send_sems, recv_sems, scatter):
    x, y, c = _position()
    me = 4 * x + 2 * y + c
    sends, recvs = [], []
    for k, (peer_id, peer) in enumerate(_peers(x, y, c)):
        for a, (src, land) in enumerate(zip(src_refs, land_refs)):
            sems = dict(send_sem=send_sems.at[7 * a + k], recv_sem=recv_sems.at[7 * a + k],
                        device_id=peer_id, device_id_type=MESH_ID)
            mine = src.at[peer] if scatter else src
            sends.append(pltpu.make_async_remote_copy(src_ref=mine, dst_ref=land.at[me], **sems))
            recvs.append(pltpu.make_async_remote_copy(src_ref=mine, dst_ref=land.at[peer], **sems))
    return sends, recvs


def _all_gather_direct(shards, name, after):
    na = len(shards)

    def body(*refs):
        x_refs, out_refs = refs[:na], refs[na + 1:2 * na + 1]
        send_sems, recv_sems, local_sems = refs[2 * na + 1:]
        x, y, c = _position()
        me = 4 * x + 2 * y + c
        local = [pltpu.make_async_copy(x_refs[a], out_refs[a].at[me], local_sems.at[a]) for a in range(na)]
        sends, recvs = _split_copies(x_refs, out_refs, send_sems, recv_sems, False)
        for cp in local + sends:
            cp.start()
        for cp in recvs:
            cp.wait_recv()
        for cp in sends:
            cp.wait_send()
        for cp in local:
            cp.wait()

    return pl.pallas_call(
        body, in_specs=[pl.BlockSpec(memory_space=pl.ANY)] * (na + 1),
        out_specs=[pl.BlockSpec(memory_space=pl.ANY)] * na,
        out_shape=[jax.ShapeDtypeStruct((N_DEV,) + s.shape, s.dtype) for s in shards],
        scratch_shapes=[pltpu.SemaphoreType.DMA((7 * na,)), pltpu.SemaphoreType.DMA((7 * na,)),
                        pltpu.SemaphoreType.DMA((na,))],
        name=name)(*shards, after)


def _exchange_start(srcs, scatter, name, after=None):
    na = len(srcs)
    lands = [lax.empty(s.shape if scatter else (N_DEV,) + s.shape, s.dtype) for s in srcs]
    extra = [] if after is None else [after]

    def body(*refs):
        src_refs, land_refs = refs[:na], refs[na:2 * na]
        send_sems, recv_sems = refs[2 * na + len(extra)], refs[2 * na + len(extra) + 1]
        token = refs[-1]
        sends, _ = _split_copies(src_refs, land_refs, send_sems, recv_sems, scatter)
        for cp in sends:
            cp.start()
        token[...] = jnp.zeros_like(token)

    hbm = lambda a: pltpu.HBM(a.shape, a.dtype)
    out = pl.pallas_call(
        body, name=name,
        out_shape=(pltpu.SemaphoreType.DMA((7 * na,)), pltpu.SemaphoreType.DMA((7 * na,)),
                   *[hbm(s) for s in srcs], *[hbm(l) for l in lands], jax.ShapeDtypeStruct((8, LANES), F32)),
        in_specs=[_HBM] * (2 * na) + [pl.BlockSpec(memory_space=pl.ANY)] * len(extra),
        out_specs=(_SEM, _SEM, *[_HBM] * (2 * na), pl.BlockSpec(memory_space=pltpu.VMEM)),
        input_output_aliases={i: 2 + i for i in range(2 * na)},
        compiler_params=pltpu.CompilerParams(has_side_effects=_DATAFLOW),
    )(*[pltpu.with_memory_space_constraint(s, pltpu.HBM) for s in srcs],
      *[pltpu.with_memory_space_constraint(l, pltpu.HBM) for l in lands], *extra)
    return (out[0], out[1], list(out[2:2 + na]), list(out[2 + na:2 + 2 * na])), out[-1]


def _exchange_wait(handle, after, scatter, name):
    send_sems, recv_sems, srcs, lands = handle
    na = len(srcs)

    def body(*refs):
        src_refs, land_refs = refs[:na], refs[na:2 * na]
        s_sems, r_sems = refs[2 * na], refs[2 * na + 1]
        sends, recvs = _split_copies(src_refs, land_refs, s_sems, r_sems, scatter)
        for cp in sends:
            cp.wait_send()
        for cp in recvs:
            cp.wait_recv()

    hbm = lambda a: pltpu.HBM(a.shape, a.dtype)
    out = pl.pallas_call(
        body, name=name, out_shape=(*[hbm(s) for s in srcs], *[hbm(l) for l in lands]),
        in_specs=[_HBM] * (2 * na) + [_SEM, _SEM, pl.BlockSpec(memory_space=pl.ANY)],
        out_specs=tuple([_HBM] * (2 * na)), input_output_aliases={i: i for i in range(2 * na)},
        compiler_params=pltpu.CompilerParams(has_side_effects=_DATAFLOW),
    )(*srcs, *lands, send_sems, recv_sems, after)
    return list(out[:na]), list(out[na:])


def _own_slot(landed, own):
    me = 4 * lax.axis_index("x") + 2 * lax.axis_index("y") + lax.axis_index("c")
    return lax.dynamic_update_slice_in_dim(landed, own[None], me, axis=0)


def _adam_update(parts, w, m, v, name, tr=256):
    _, r, c = w.shape
    tr = _pick_rows(r, tr)
    cp = parts.shape[2]

    def body(p_ref, w_ref, m_ref, v_ref, g_ref, d_ref, nm_ref, nv_ref):
        g = p_ref[0, :, pl.ds(0, c)].astype(F32)
        for i in range(1, N_DEV):
            g = g + p_ref[i, :, pl.ds(0, c)].astype(F32)
        delta, nm, nv = _adamw(w_ref[0], g, m_ref[0], v_ref[0])
        g_ref[0] = g
        d_ref[0] = delta
        nm_ref[0] = nm
        nv_ref[0] = nv

    rs = pl.BlockSpec((1, tr, c), lambda i: (0, i, 0))
    return pl.pallas_call(
        body, grid=(r // tr,), in_specs=[pl.BlockSpec((N_DEV, tr, cp), lambda i: (0, i, 0)), rs, rs, rs],
        out_specs=[rs] * 4, out_shape=[jax.ShapeDtypeStruct((1, r, c), F32)] * 4, name=name,
        compiler_params=_params(("parallel",)))(parts, w, m, v)


def _pick_rows(rows, target):
    if rows <= target:
        return rows
    t = target
    while t >= 16:
        if rows % t == 0:
            return t
        t -= 16
    return rows


BIG = ("w_in", "w_branch_dn", "w_branch_swa", "w_out", "w_gate", "w_up", "w_down")
IN_SHARD, IN_WIRE = D_IN // N_DEV, 640
FF_SHARD, FF_WIRE = D_FF // N_DEV, 384
D_FFP = N_DEV * FF_WIRE
BIG_SHAPES = {"w_in": ((D_MODEL, IN_SHARD), (D_MODEL, IN_WIRE)),
              "w_branch_dn": ((DN_WIDTH, LANES), (DN_WIDTH, LANES)),
              "w_branch_swa": ((SWA_WIDTH, LANES), (SWA_WIDTH, LANES)),
              "w_out": ((LANES, D_MODEL), (LANES, D_MODEL)),
              "w_gate": ((D_MODEL, FF_SHARD), (D_MODEL, FF_WIRE)),
              "w_up": ((D_MODEL, FF_SHARD), (D_MODEL, FF_WIRE)),
              "w_down": ((FF_SHARD, D_MODEL), (FF_WIRE, D_MODEL))}
CONV_SHARD, CONV_WIRE = (DN_CONV, DN_QKV // N_DEV), (8, 256)


def _pad_to(a, shape):
    return jnp.pad(a, [(0, t - s) for s, t in zip(a.shape, shape)])


_IN_SEGS = ((R_GATE, 2048, P_GATE), (R_QKV, DN_QKV, P_QKV), (R_Z, DN_WIDTH, P_Z), (R_SQ, SWA_WIDTH, P_SQ),
            (R_SK, SWA_KVW, P_SK), (R_SV, SWA_KVW, P_SV), (R_B, 8, P_BA))


def _w_in_from_blocks(blocks):
    parts = []
    for rs, n, _ in _IN_SEGS:
        for dev in range(N_DEV):
            lo, hi = max(rs, IN_SHARD * dev), min(rs + n, IN_SHARD * (dev + 1))
            if lo < hi:
                parts.append(blocks[dev, :, lo - IN_SHARD * dev:hi - IN_SHARD * dev])
    parts.append(jnp.zeros((blocks.shape[1], P_WIDTH - P_BA - 8), blocks.dtype))
    return jnp.concatenate(parts, axis=1)


def _w_in_to_blocks(g):
    out = []
    for dev in range(N_DEV):
        parts = []
        for rs, n, ps in sorted(_IN_SEGS):
            lo, hi = max(rs, IN_SHARD * dev), min(rs + n, IN_SHARD * (dev + 1))
            if lo < hi:
                parts.append(g[:, ps + lo - rs:ps + hi - rs])
        parts.append(jnp.zeros((g.shape[0], IN_WIRE - IN_SHARD), g.dtype))
        out.append(jnp.concatenate(parts, axis=1))
    return jnp.stack(out)


SMALL = {"attn_norm": (0, (1, D_MODEL)), "ffn_norm": (1, (1, D_MODEL)), "dn_out_norm": (2, (1, DN_DIM)),
         "swa_q_norm": (3, (1, SWA_DIM)), "swa_k_norm": (4, (1, SWA_DIM)), "dn_a_log": (5, (1, DN_HEADS)),
         "dn_dt_bias": (6, (1, DN_HEADS)), "swa_sinks": (7, (1, SWA_HEADS)), "rel_bias": (8, (REL_BUCKETS, SWA_HEADS))}
SMALL_SHEET = (48, D_MODEL)


LOSS_ROW = 40


def _small_pack(grads, loss_local):
    names = list(SMALL)

    def body(*refs):
        o_ref = refs[-1]
        o_ref[...] = jnp.zeros_like(o_ref)
        for n, ref in zip(names, refs):
            r0, (nr, nc) = SMALL[n]
            o_ref[r0:r0 + nr, 0:nc] = ref[...]
        o_ref[LOSS_ROW:LOSS_ROW + 1, 0:1] = refs[len(names)][...]

    return pl.pallas_call(
        body, in_specs=[pl.BlockSpec(memory_space=pltpu.VMEM)] * (len(names) + 1),
        out_specs=pl.BlockSpec(memory_space=pltpu.VMEM), out_shape=jax.ShapeDtypeStruct(SMALL_SHEET, F32),
        name="small_pack", compiler_params=_params())(*[grads[n].reshape(SMALL[n][1]) for n in names], loss_local)


def _small_update(sheets, w, m, v):
    names = list(SMALL)
    k = len(names)

    def body(*refs):
        p_ref = refs[0]
        ins, outs = refs[1:1 + 3 * k], refs[1 + 3 * k:]
        loss = p_ref[0, LOSS_ROW:LOSS_ROW + 1, 0:1]
        for i in range(1, N_DEV):
            loss = loss + p_ref[i, LOSS_ROW:LOSS_ROW + 1, 0:1]
        outs[4 * k][...] = loss
        for t, n in enumerate(names):
            r0, (nr, nc) = SMALL[n]
            g = p_ref[0, r0:r0 + nr, 0:nc]
            for i in range(1, N_DEV):
                g = g + p_ref[i, r0:r0 + nr, 0:nc]
            delta, nm, nv = _adamw(ins[t][...], g, ins[k + t][...], ins[2 * k + t][...])
            for kind, val in enumerate((g, delta, nm, nv)):
                outs[kind * k + t][...] = val

    shapes = [jax.ShapeDtypeStruct(SMALL[n][1], F32) for n in names]
    vm = pl.BlockSpec(memory_space=pltpu.VMEM)
    res = pl.pallas_call(
        body, in_specs=[vm] * (1 + 3 * k), out_specs=[vm] * (4 * k + 1),
        out_shape=shapes * 4 + [jax.ShapeDtypeStruct((1, 1), F32)], name="adam_small", compiler_params=_params(),
    )(sheets, *[d[n].reshape(SMALL[n][1]) for d in (w, m, v) for n in names])
    return {n: tuple(res[kind * k + t] for kind in range(4)) for t, n in enumerate(names)}, res[4 * k]


def kernel(x, attn_norm, w_in, dn_conv, dn_a_log, dn_dt_bias, dn_out_norm, swa_q_norm, swa_k_norm, swa_sinks, rel_bias, w_branch_dn, w_branch_swa, w_out, ffn_norm, w_gate, w_up, w_down, loss_target, m_attn_norm, m_w_in, m_dn_conv, m_dn_a_log, m_dn_dt_bias, m_dn_out_norm, m_swa_q_norm, m_swa_k_norm, m_swa_sinks, m_rel_bias, m_w_branch_dn, m_w_branch_swa, m_w_out, m_ffn_norm, m_w_gate, m_w_up, m_w_down, v_attn_norm, v_w_in, v_dn_conv, v_dn_a_log, v_dn_dt_bias, v_dn_out_norm, v_swa_q_norm, v_swa_k_norm, v_swa_sinks, v_rel_bias, v_w_branch_dn, v_w_branch_swa, v_w_out, v_ffn_norm, v_w_gate, v_w_up, v_w_down):
    args = dict(locals())
    S = x.shape[1]
    xs = x.reshape(S, D_MODEL)
    target = loss_target.reshape(S, D_MODEL)

    w_loc = {n: args[n].reshape(BIG_SHAPES[n][0]) for n in BIG}
    conv_loc = dn_conv.reshape(CONV_SHARD)
    wire = {n: _pad_to(w_loc[n], BIG_SHAPES[n][1]).astype(BF16) for n in BIG}
    first = _all_gather([wire["w_in"], _pad_to(conv_loc, CONV_WIRE)])
    later = [n for n in BIG if n != "w_in"]
    rest_handle, rest_token = _exchange_start([wire[n] for n in later], False, "gather_rest_start", after=first[1])
    w_pad = _w_in_from_blocks(first[0])
    conv_w = jnp.concatenate([first[1][d, :DN_CONV, :CONV_SHARD[1]] for d in range(N_DEV)], axis=1)

    h = _norm_fwd(xs, attn_norm + rest_token[0, 0], "norm1_fwd")
    proj = _mm([(h, w_pad)], "nn", F32, "mm_in", 1024, 1664, j_outer=True)
    qkvn = _dn_conv_fwd(proj, conv_w)
    beta, g = _dn_gate_fwd(proj, dn_a_log, dn_dt_bias)
    u, w, qe, kd, qk, egl, tinv = _dn_prep_fwd(qkvn, g, beta)
    o, states = _dn_scan_fwd(u, w, qe, kd, qk, egl)
    y_dn = _dn_out_fwd(o, proj, dn_out_norm)
    bias = _bias_fwd(rel_bias)
    y_swa = _swa_fwd(proj, swa_q_norm, swa_k_norm, swa_sinks, bias)
    rest_src, rest_land = _exchange_wait(rest_handle, y_swa, False, "gather_rest_wait")
    G = {n: _own_slot(land, src) for n, src, land in zip(later, rest_src, rest_land)}
    w_bdn, w_bswa, w_g, w_u = G["w_branch_dn"], G["w_branch_swa"], G["w_gate"], G["w_up"]
    w_o = G["w_out"].reshape(D_MODEL, D_MODEL)
    w_d = G["w_down"].reshape(D_FFP, D_MODEL)
    gates = [(proj, P_GATE // 512), (proj, (P_GATE + D_MODEL) // 512)]
    a_dn, a_swa, merged = _mm_fused(
        [(y_dn, w_bdn), (y_swa, w_bswa)], "nn", "mm_branch_merge", 1024, 512,
        lambda p, e: (p[0], p[1], _merge(e[0], e[1], p[0], p[1])), gates, (F32, F32, BF16), b_blocks=True)

    def resid_norm(p, e):
        x1 = e[0] + p[0]
        return x1, _rms(x1, e[1])

    x1, h2 = _mm_fused([(merged, w_o)], "nn", "mm_out_norm", 512, D_MODEL, resid_norm,
                       [(xs, 0), (ffn_norm, None)], (F32, BF16))
    gate, up, act = _mm_fused([(h2, w_g), (h2, w_u)], "nn", "mm_gate_up_act", 1024, 768,
                              lambda p, e: (p[0], p[1], _act(p[0], p[1])), [], (F32, F32, BF16),
                              j_outer=True, b_blocks=True)

    def loss_head(p, e):
        diff = e[0] + p[0] - e[1]
        dy = diff * (1.0 / D_MODEL)
        part = jnp.sum(jnp.mean(diff * diff, axis=-1, keepdims=True), axis=0, keepdims=True) * 0.5
        return dy, dy, part

    dy, dy_b, loss_local = _mm_fused([(act, w_d)], "nn", "mm_down_loss", 512, D_MODEL, loss_head,
                                     [(x1, 0), (target, 0)], (F32, BF16), sum_shape=(1, 1))

    def act_bwd(p, e):
        _, vjp = jax.vjp(_act, e[0], e[1])
        return vjp(p[0])

    dgate, dup = _mm_fused([(dy_b, w_d)], "nt", "mm_dact_act", 1024, 768, act_bwd, [(gate, 0), (up, 0)],
                           (BF16, BF16), j_outer=True)
    g_w_down = _mm([(act, dy_b)], "tn", BF16, "mm_dw_down", 768, D_MODEL, j_outer=True)
    g_w_down = g_w_down.reshape(N_DEV, FF_WIRE, D_MODEL)
    g_w_gate = _mm([(h2, dgate)], "tn", BF16, "mm_dw_gate", D_MODEL, 768, out_blocks=True)
    g_w_up = _mm([(h2, dup)], "tn", BF16, "mm_dw_up", D_MODEL, 768, out_blocks=True)
    ffn_handle, ffn_token = _exchange_start([g_w_down, g_w_gate, g_w_up], True, "scatter_ffn_start")

    def norm_bwd(p, e):
        _, vjp = jax.vjp(_rms, e[0], e[2])
        dx, dgain = vjp(sum(p))
        dx = dx + e[1]
        return dx, dx, dgain

    dx1, dx1_b, g_ffn_norm = _mm_fused(
        [(dgate, w_g), (dup, w_u)], "nt", "mm_dh2_norm", 256, D_MODEL, norm_bwd,
        [(x1, 0), (dy, 0), (ffn_norm + ffn_token[0, 0], None)], (F32, BF16), b_blocks=True, sum_shape=(1, D_MODEL))
    def merge_bwd(p, e):
        _, vjp = jax.vjp(_merge, *e)
        dg0, dg1, da_dn, da_swa = vjp(p[0])
        return jnp.concatenate([dg0, dg1], axis=1), da_dn, da_swa

    dproj, da_dn, da_swa = _mm_fused(
        [(dx1_b, w_o)], "nt", "mm_dmerged_merge", 512, D_MODEL, merge_bwd,
        [(proj, P_GATE // D_MODEL), (proj, P_GATE // D_MODEL + 1), (a_dn, 0), (a_swa, 0)], (BF16,) * 3,
        wide_first=(P_WIDTH, 2 * D_MODEL))
    g_w_out = _mm([(merged, dx1_b)], "tn", BF16, "mm_dw_out", 512, D_MODEL, j_outer=True)
    g_w_out = g_w_out.reshape(N_DEV, LANES, D_MODEL)
    dy_dn = _mm([(da_dn, w_bdn)], "nt", F32, "mm_dy_dn", 1024, DN_WIDTH, b_blocks=True)
    dy_swa = _mm([(da_swa, w_bswa)], "nt", F32, "mm_dy_swa", 1024, SWA_WIDTH, b_blocks=True)
    g_w_bdn = _mm([(y_dn, da_dn)], "tn", BF16, "mm_dw_branch_dn", DN_WIDTH, 512, out_blocks=True)
    g_w_bswa = _mm([(y_swa, da_swa)], "tn", BF16, "mm_dw_branch_swa", SWA_WIDTH, 512, out_blocks=True)
    dproj, dsk, dsv, g_q_norm, g_k_norm, g_sinks, dbias = _swa_bwd(proj, swa_q_norm, swa_k_norm, swa_sinks, bias,
                                                                   dy_swa, dproj)
    dproj = _kv_into(dproj, dsk, dsv)
    g_rel_bias = _bias_bwd(dbias)[:, :REL_BUCKETS].T
    mix_handle, mix_token = _exchange_start([g_w_out, g_w_bdn, g_w_bswa], True, "scatter_mix_start")
    do, dproj, g_out_norm = _dn_out_bwd(o, proj, dn_out_norm + mix_token[0, 0], dy_dn, dproj)
    du, dw, dqe, dkd, dqk, degl = _dn_scan_bwd(u, w, qe, kd, qk, egl, states, do)
    dqkvn, dgd, dbeta = _dn_prep_bwd(qkvn, g, beta, tinv, du, dw, dqe, dkd, dqk, degl)
    dproj, dal, ddt = _dn_gate_bwd(proj, dn_a_log, dn_dt_bias, dbeta, dgd, dproj)
    g_a_log = dal.reshape(DN_HEADS, DN_DIM).sum(axis=1)
    g_dt_bias = ddt[0, DN_HEADS:2 * DN_HEADS]
    dproj, g_conv = _dn_conv_bwd(proj, conv_w, dqkvn, dproj)
    g_w_in = _w_in_to_blocks(_mm([(h, dproj)], "tn", BF16, "mm_dw_in", 512, 1664, j_outer=True))
    in_handle, in_token = _exchange_start([g_w_in], True, "scatter_in_start")
    dx, g_attn_norm = _mm_fused(
        [(dproj, w_pad)], "nt", "mm_dh_norm", 512, D_MODEL, lambda p, e: norm_bwd(p, e)[1:],
        [(xs, 0), (dx1, 0), (attn_norm + in_token[0, 0], None)], (F32,), sum_shape=(1, D_MODEL))

    g_small = {"attn_norm": g_attn_norm, "ffn_norm": g_ffn_norm, "rel_bias": g_rel_bias, "dn_out_norm": g_out_norm,
               "swa_q_norm": g_q_norm, "swa_k_norm": g_k_norm, "dn_a_log": g_a_log, "dn_dt_bias": g_dt_bias,
               "swa_sinks": g_sinks}
    me = 4 * lax.axis_index("x") + 2 * lax.axis_index("y") + lax.axis_index("c")
    outs = {}

    def finish(handle, group, name, after):
        srcs, lands = _exchange_wait(handle, after, True, name)
        for n, src, land in zip(group, srcs, lands):
            parts = _own_slot(land, lax.dynamic_index_in_dim(src, me, 0, keepdims=False))
            outs[n] = _adam_update(parts, args[n], args["m_" + n], args["v_" + n], "adam_" + n)

    finish(ffn_handle, ("w_down", "w_gate", "w_up"), "scatter_ffn_wait", dx)
    finish(mix_handle, ("w_out", "w_branch_dn", "w_branch_swa"), "scatter_mix_wait", dx)
    sheets, conv_all = _all_gather_direct([_small_pack(g_small, loss_local), _pad_to(g_conv, (8, DN_QKV))],
                                          "all_gather_small",
                                          after=outs["w_up"][0])
    finish(in_handle, ("w_in",), "scatter_in_wait", sheets)
    conv_parts = lax.dynamic_slice(conv_all, (0, 0, me * CONV_SHARD[1]), (N_DEV,) + CONV_SHARD)
    outs["dn_conv"] = _adam_update(conv_parts, dn_conv, m_dn_conv, v_dn_conv, "adam_dn_conv")
    small_outs, loss = _small_update(sheets, {n: args[n] for n in SMALL}, {n: args["m_" + n] for n in SMALL},
                                     {n: args["v_" + n] for n in SMALL})
    outs.update(small_outs)

    names = ("attn_norm", "w_in", "dn_conv", "dn_a_log", "dn_dt_bias", "dn_out_norm", "swa_q_norm", "swa_k_norm",
             "swa_sinks", "rel_bias", "w_branch_dn", "w_branch_swa", "w_out", "ffn_norm", "w_gate", "w_up", "w_down")
    results = []
    for kind in range(4):
        results += [outs[n][kind].reshape(args[n].shape) for n in names]

    return (loss.reshape(()), dx.reshape(x.shape), *results)
```

```python
import math

import numpy as np
import jax
import jax.numpy as jnp
from jax import lax
from jax.experimental import pallas as pl
from jax.experimental.pallas import tpu as pltpu

F32 = jnp.float32
BF16 = jnp.bfloat16
HI = lax.Precision.HIGHEST

D_MODEL = 1024
DN_HEADS = 4
DN_DIM = 128
DN_WIDTH = 512
DN_QKV = 1536
DN_CONV = 4
CHUNK = 64
SWA_HEADS = 8
SWA_KV = 2
SWA_GROUP = 4
SWA_DIM = 64
SWA_WIDTH = 512
SWA_KVW = 128
WINDOW = 128
BLOCK = 128
REL_BUCKETS = 32
REL_MAX_DIST = 128
D_FF = 2816
D_IN = 4872
EPS = 1e-6
N_DEV = 8

ADAM_LR = 0.001
ADAM_B1 = 0.9
ADAM_B2 = 0.999
ADAM_EPS = 1e-08
ADAM_WD = 0.01
ADAM_STEP = 10

P_GATE, P_QKV, P_Z, P_SQ, P_SK, P_SV, P_BA = 0, 2048, 3584, 4096, 4608, 4736, 4864
P_WIDTH = 4992
R_QKV, R_Z, R_B, R_A, R_SQ, R_SK, R_SV, R_GATE = 0, 1536, 2048, 2052, 2056, 2568, 2696, 2824

VMEM_LIMIT = 56 * 1024 * 1024
LANES = 128
MESH_ID = pl.DeviceIdType.MESH


def _params(sem=None):
    return pltpu.CompilerParams(dimension_semantics=sem, vmem_limit_bytes=VMEM_LIMIT)


def _pick(dim, target):
    if dim <= target:
        return dim
    t = target - target % LANES
    while t >= LANES:
        if dim % t == 0:
            return t
        t -= LANES
    return dim


_DIMS = {"nn": (((1,), (0,)), ((), ())), "nt": (((1,), (1,)), ((), ())), "tn": (((0,), (0,)), ((), ()))}


def _tile_product(a_ref, b_ref, mode, b_blocks):
    a = a_ref[...].astype(BF16)
    b = jnp.concatenate([b_ref[d] for d in range(b_ref.shape[0])], axis=1) if b_blocks else b_ref[...]
    return lax.dot_general(a, b.astype(BF16), _DIMS[mode], preferred_element_type=F32)


def _mm(pairs, mode, out_dtype, name, bm, bn, j_outer=False, b_blocks=False, out_blocks=False):
    a0, b0 = pairs[0]
    cb = b0.shape[2] if b_blocks else None
    b_shape = (b0.shape[1], N_DEV * cb) if b_blocks else b0.shape
    if mode == "nn":
        (M, K), (K2, N) = a0.shape, b_shape
    elif mode == "nt":
        (M, K), (N, K2) = a0.shape, b_shape
    else:
        (K, M), (K2, N) = a0.shape, b_shape
    bm, bn = min(bm, M), min(bn, N)
    assert K == K2 and M % bm == 0 and N % bn == 0, (name, a0.shape, b0.shape, bm, bn)
    co = N // N_DEV
    assert not out_blocks or bn % co == 0
    dims = _DIMS[mode]
    n = len(pairs)

    def body(*refs):
        o_ref = refs[2 * n]
        acc = None
        for t in range(n):
            p = _tile_product(refs[2 * t], refs[2 * t + 1], mode, b_blocks)
            acc = p if acc is None else acc + p
        if out_blocks:
            for d in range(bn // co):
                o_ref[d] = acc[:, d * co:(d + 1) * co].astype(out_dtype)
        else:
            o_ref[...] = acc.astype(out_dtype)

    def ij(f):
        return (lambda j, i: f(i, j)) if j_outer else f

    a_spec = pl.BlockSpec((K, bm), ij(lambda i, j: (0, i))) if mode == "tn" else pl.BlockSpec((bm, K), ij(lambda i, j: (i, 0)))
    if b_blocks and mode == "nt":
        b_spec = pl.BlockSpec((N_DEV, bn, cb), ij(lambda i, j: (0, j, 0)))
    elif b_blocks:
        b_spec = pl.BlockSpec((bn // cb, K, cb), ij(lambda i, j: (j, 0, 0)))
    elif mode == "nt":
        b_spec = pl.BlockSpec((bn, K), ij(lambda i, j: (j, 0)))
    else:
        b_spec = pl.BlockSpec((K, bn), ij(lambda i, j: (0, j)))
    if out_blocks:
        out_spec = pl.BlockSpec((bn // co, bm, co), ij(lambda i, j: (j, i, 0)))
        out_shape = jax.ShapeDtypeStruct((N_DEV, M, co), out_dtype)
    else:
        out_spec = pl.BlockSpec((bm, bn), ij(lambda i, j: (i, j)))
        out_shape = jax.ShapeDtypeStruct((M, N), out_dtype)
    grid = (N // bn, M // bm) if j_outer else (M // bm, N // bn)
    return pl.pallas_call(
        body, grid=grid, in_specs=[a_spec, b_spec] * n, out_specs=out_spec, out_shape=out_shape, name=name,
        compiler_params=_params(("parallel", "parallel")),
    )(*[x for pair in pairs for x in pair])


def _mm_fused(pairs, mode, name, bm, bn, epilogue, extras, out_dtypes, j_outer=False, b_blocks=False,
              sum_shape=None, wide_first=None):
    a0, b0 = pairs[0]
    cb = b0.shape[2] if b_blocks else None
    b_shape = (b0.shape[1], N_DEV * cb) if b_blocks else b0.shape
    if mode == "nn":
        (M, K), (K2, N) = a0.shape, b_shape
    else:
        (M, K), (N, K2) = a0.shape, b_shape
    bm, bn = min(bm, M), min(bn, N)
    assert mode in ("nn", "nt") and K == K2 and M % bm == 0 and N % bn == 0, (name, a0.shape, b0.shape)
    dims = _DIMS[mode]
    n, ne, no = len(pairs), len(extras), len(out_dtypes)

    def body(*refs):
        prods = [_tile_product(refs[2 * t], refs[2 * t + 1], mode, b_blocks) for t in range(n)]
        results = epilogue(prods, [r[...] for r in refs[2 * n:2 * n + ne]])
        out_refs = refs[2 * n + ne:]
        for o_ref, val, dt in zip(out_refs, results, out_dtypes):
            o_ref[...] = val.astype(dt)
        if sum_shape is not None:
            s_ref = out_refs[no]

            @pl.when((pl.program_id(0) == 0) & (pl.program_id(1) == 0))
            def _():
                s_ref[...] = jnp.zeros_like(s_ref)

            s_ref[...] += results[no]

    def ij(f):
        return (lambda j, i: f(i, j)) if j_outer else f

    a_spec = pl.BlockSpec((bm, K), ij(lambda i, j: (i, 0)))
    once = dict(pipeline_mode=pl.Buffered(1)) if bn == N else {}
    if b_blocks and mode == "nt":
        b_spec = pl.BlockSpec((N_DEV, bn, cb), ij(lambda i, j: (0, j, 0)), **once)
    elif b_blocks:
        b_spec = pl.BlockSpec((bn // cb, K, cb), ij(lambda i, j: (j, 0, 0)), **once)
    elif mode == "nt":
        b_spec = pl.BlockSpec((bn, K), ij(lambda i, j: (j, 0)), **once)
    else:
        b_spec = pl.BlockSpec((K, bn), ij(lambda i, j: (0, j)), **once)
    e_specs = [pl.BlockSpec((1, bn), ij(lambda i, j: (0, j))) if first is None
               else pl.BlockSpec((bm, bn), ij(lambda i, j, first=first: (i, first + j))) for _, first in extras]
    tile = pl.BlockSpec((bm, bn), ij(lambda i, j: (i, j)))
    out_specs = [tile] * no
    out_shape = [jax.ShapeDtypeStruct((M, N), dt) for dt in out_dtypes]
    if wide_first is not None:
        assert bn == N
        out_specs[0] = pl.BlockSpec((bm, wide_first[1]), ij(lambda i, j: (i, 0)))
        out_shape[0] = jax.ShapeDtypeStruct((M, wide_first[0]), out_dtypes[0])
    if sum_shape is not None:
        assert sum_shape[1] in (1, bn) and (sum_shape[1] == 1 or bn == N)
        out_specs.append(_full(sum_shape))
        out_shape.append(jax.ShapeDtypeStruct(sum_shape, F32))
    grid = (N // bn, M // bm) if j_outer else (M // bm, N // bn)
    sem = ("arbitrary", "arbitrary") if sum_shape is not None else ("parallel", "parallel")
    return pl.pallas_call(
        body, grid=grid, in_specs=[a_spec, b_spec] * n + e_specs, out_specs=out_specs, out_shape=out_shape,
        name=name, compiler_params=_params(sem),
    )(*[x for pair in pairs for x in pair], *[arr for arr, _ in extras])


def _rms(x, gain):
    return x * lax.rsqrt(jnp.mean(x * x, axis=-1, keepdims=True) + EPS) * gain


def _silu(x):
    return x * jax.nn.sigmoid(x)


def _act(g, u):
    return _silu(g) * u


def _merge(g0, g1, a_dn, a_swa):
    return jax.nn.sigmoid(g0) * a_dn + jax.nn.sigmoid(g1) * a_swa


def _dn_post(c, q_scale):
    a = _silu(c)
    return a * (lax.rsqrt(jnp.sum(a * a, axis=-1, keepdims=True) + EPS) * q_scale)


def _dn_out(o, z, gain):
    return _rms(o, gain) * _silu(z)


def _dot(a, b, dims=_DIMS["nn"], hi=False):
    if a.ndim == 3 or b.ndim == 3:
        batch = a.shape[0] if a.ndim == 3 else b.shape[0]
        a = a if a.ndim == 3 else jnp.broadcast_to(a, (batch,) + a.shape)
        b = b if b.ndim == 3 else jnp.broadcast_to(b, (batch,) + b.shape)
        ((ca,), (cb,)), _ = dims
        dims = (((ca + 1,), (cb + 1,)), ((0,), (0,)))
    if hi:
        return lax.dot_general(a, b, dims, precision=HI, preferred_element_type=F32)
    return lax.dot_general(a.astype(BF16), b.astype(BF16), dims, preferred_element_type=F32)


def _pieces(x):
    hi = x.astype(BF16)
    r1 = x - hi.astype(F32)
    mid = r1.astype(BF16)
    return hi, mid, (r1 - mid.astype(F32)).astype(BF16)


def _sel_left_impl(m, x):
    mb = m.astype(BF16)
    hi, mid, lo = _pieces(x)
    return _dot(mb, hi) + (_dot(mb, mid) + _dot(mb, lo))


@jax.custom_vjp
def _sel_left(m, mt, x):
    return _sel_left_impl(m, x)


_sel_left.defvjp(lambda m, mt, x: (_sel_left_impl(m, x), (m, mt)),
                 lambda res, ct: (jnp.zeros_like(res[0]), jnp.zeros_like(res[1]), _sel_left_impl(res[1], ct)))


def _sel_right_impl(x, s):
    sb = s.astype(BF16)
    hi, mid, lo = _pieces(x)
    return _dot(hi, sb) + (_dot(mid, sb) + _dot(lo, sb))


@jax.custom_vjp
def _sel_right(x, s, st):
    return _sel_right_impl(x, s)


_sel_right.defvjp(lambda x, s, st: (_sel_right_impl(x, s), (s, st)),
                  lambda res, ct: (_sel_right_impl(ct, res[1]), jnp.zeros_like(res[0]), jnp.zeros_like(res[1])))


def _dot3_impl(a, b):
    a_hi, a_lo, _ = _pieces(a)
    b_hi, b_lo, _ = _pieces(b)
    return _dot(a_hi, b_hi) + (_dot(a_hi, b_lo) + _dot(a_lo, b_hi))


@jax.custom_vjp
def _dot3(a, b):
    return _dot3_impl(a, b)


_dot3.defvjp(lambda a, b: (_dot3_impl(a, b), (a, b)),
             lambda res, ct: (_dot(ct, res[1], _DIMS["nt"]), _dot(res[0], ct, _DIMS["tn"])))


def _inv_impl(a, eye, strict):
    t = eye - a
    p = _dot(a, a)
    for level in range(5):
        t = t + _dot(t, p)
        if level < 4:
            p = _dot(p, p)
    t = t + _dot(t, eye - t - _dot3_impl(a, t))
    return jnp.where(strict > 0.5, t, eye)


@jax.custom_vjp
def _inv_given(a, t):
    return t.astype(F32)


_inv_given.defvjp(lambda a, t: (t.astype(F32), t),
                  lambda t, ct: (-_dot(_dot(t, ct, _DIMS["tn"]), t, _DIMS["nt"]), jnp.zeros_like(t)))


@jax.custom_vjp
def _lanes_join(a, b):
    return jnp.concatenate([a, b], axis=-1)


_lanes_join.defvjp(lambda a, b: (jnp.concatenate([a, b], axis=-1), None),
                   lambda _, ct: (ct[..., :ct.shape[-1] // 2], ct[..., ct.shape[-1] // 2:]))


@jax.custom_vjp
def _lanes_halves(y):
    h = y.shape[-1] // 2
    return y[..., :h], y[..., h:]


_lanes_halves.defvjp(lambda y: ((y[..., :y.shape[-1] // 2], y[..., y.shape[-1] // 2:]), None),
                     lambda _, ct: (jnp.concatenate(ct, axis=-1),))

GROUP = 4
GROUP_ROWS = GROUP * CHUNK


def _block_consts(n):
    ii = lax.broadcasted_iota(jnp.int32, (n, n), 0)
    jj = lax.broadcasted_iota(jnp.int32, (n, n), 1)
    shift = CHUNK.bit_length() - 1
    same = jnp.right_shift(ii, shift) == jnp.right_shift(jj, shift)
    return same & (ii >= jj), same & (ii <= jj), same & (ii > jj), same, ii == jj


def _lane0(n):
    s = (lax.broadcasted_iota(jnp.int32, (LANES, n), 0) == 0).astype(F32)
    st = (lax.broadcasted_iota(jnp.int32, (n, LANES), 1) == 0).astype(F32)
    return s, st


def _dn_group(q, k, v, g, beta, t_saved=None):
    n = GROUP_ROWS
    low_b, upp_b, strict_b, _, eye_b = _block_consts(n)
    low, upp, eye = low_b.astype(F32), upp_b.astype(F32), eye_b.astype(F32)
    gc = _sel_left(low, upp, g)
    per_chunk = (g.shape[0], GROUP, CHUNK, LANES)
    g_last = jnp.sum(g.reshape(per_chunk), axis=2, keepdims=True)
    gl = jnp.broadcast_to(g_last, per_chunk).reshape(g.shape)
    s, st = _lane0(n)
    col = _sel_right(gc, s, st)
    row = jnp.swapaxes(col, 1, 2)
    decay = jnp.exp(jnp.where(low_b, col - row, -jnp.inf))
    kb = k * beta
    vb = v * beta
    a = jnp.where(strict_b, _dot(kb, k, _DIMS["nt"]) * decay, 0.0)
    t = _inv_impl(a, eye, strict_b.astype(F32)) if t_saved is None else _inv_given(a, t_saved)
    u, w = _lanes_halves(_dot3(t, _lanes_join(vb, kb * jnp.exp(gc))))
    fold = (jnp.bitwise_and(lax.broadcasted_iota(jnp.int32, (n, CHUNK), 0), CHUNK - 1)
            == lax.broadcasted_iota(jnp.int32, (n, CHUNK), 1)).astype(F32)
    fold_t = (jnp.bitwise_and(lax.broadcasted_iota(jnp.int32, (CHUNK, n), 1), CHUNK - 1)
              == lax.broadcasted_iota(jnp.int32, (CHUNK, n), 0)).astype(F32)
    qk = _sel_right(_dot(q, k, _DIMS["nt"]) * decay, fold, fold_t)
    return u, w, q * jnp.exp(gc), k * jnp.exp(gl - gc), qk, jnp.exp(g_last), t


def _dn_step(s, u, w, qe, kd, qk, egl):
    v_new = u - _dot(w, s)
    o = _dot(qe, s) + _dot(qk, v_new)
    s_new = s * egl + _dot(kd, v_new, _DIMS["tn"])
    return s_new, o


def _swa_block(q, kband, vband, qg, kg, sinks, band):
    kn = _rms(kband, kg)
    qn = _rms(q, qg) * (SWA_DIM ** -0.5)
    logits = _dot(qn, kn, _DIMS["nt"]) + band
    m = lax.stop_gradient(jnp.maximum(jnp.max(logits, axis=-1, keepdims=True), sinks))
    p = jnp.exp(logits - m)
    denom = jnp.sum(p, axis=-1, keepdims=True) + jnp.exp(sinks - m)
    return _dot(p * (1.0 / denom), vband)


def _adamw(w, g, m, v):
    m = ADAM_B1 * m + (1.0 - ADAM_B1) * g
    v = ADAM_B2 * v + (1.0 - ADAM_B2) * jnp.square(g)
    m_hat = m / (1.0 - ADAM_B1 ** ADAM_STEP)
    v_hat = v / (1.0 - ADAM_B2 ** ADAM_STEP)
    delta = -ADAM_LR * (m_hat / (jnp.sqrt(v_hat) + ADAM_EPS) + ADAM_WD * w)
    return delta, m, v


def _row(tm, c, cb=0):
    return pl.BlockSpec((tm, c), lambda i, cb=cb: (i, cb))


def _full(shape):
    nd = len(shape)
    return pl.BlockSpec(shape, lambda *_, nd=nd: (0,) * nd)


def _norm_fwd(x, gain, name, tm=1024):
    S = x.shape[0]

    def body(x_ref, g_ref, h_ref):
        h_ref[...] = _rms(x_ref[...], g_ref[...]).astype(BF16)

    return pl.pallas_call(
        body, grid=(S // tm,), in_specs=[_row(tm, D_MODEL), _full((1, D_MODEL))],
        out_specs=_row(tm, D_MODEL), out_shape=jax.ShapeDtypeStruct((S, D_MODEL), BF16),
        name=name, compiler_params=_params(("parallel",)))(x, gain)


def _shift_down(x, s):
    row = lax.broadcasted_iota(jnp.int32, x.shape, 0)
    return jnp.where(row >= s, pltpu.roll(x, s, axis=0), 0.0)


def _shift_up(x, s):
    n = x.shape[0]
    row = lax.broadcasted_iota(jnp.int32, x.shape, 0)
    return jnp.where(row < n - s, pltpu.roll(x, n - s, axis=0), 0.0)


def _conv(x, w):
    out = w[DN_CONV - 1:DN_CONV] * x
    for s in range(1, DN_CONV):
        out = out + w[DN_CONV - 1 - s:DN_CONV - s] * _shift_down(x, s)
    return out


def _dn_conv_fwd(proj, conv_w):
    S = proj.shape[0]
    nb = DN_QKV // LANES

    def body(x_ref, w_ref, o_ref):
        j = pl.program_id(0)
        q_scale = jnp.where(j < DN_HEADS, DN_DIM ** -0.5, 1.0).astype(F32)
        c = _conv(x_ref[...], w_ref[...])

        @pl.when(j < 2 * DN_HEADS)
        def _():
            o_ref[...] = _dn_post(c, q_scale)

        @pl.when(j >= 2 * DN_HEADS)
        def _():
            o_ref[...] = _silu(c)

    return pl.pallas_call(
        body, grid=(nb,),
        in_specs=[pl.BlockSpec((S, LANES), lambda j: (0, P_QKV // LANES + j)),
                  pl.BlockSpec((DN_CONV, LANES), lambda j: (0, j))],
        out_specs=pl.BlockSpec((S, LANES), lambda j: (0, j)),
        out_shape=jax.ShapeDtypeStruct((S, DN_QKV), F32), name="dn_conv_fwd",
        compiler_params=_params(("parallel",)))(proj, conv_w)


def _dn_conv_bwd(proj, conv_w, dqkvn, dproj):
    S = proj.shape[0]
    nb = DN_QKV // LANES

    def body(x_ref, w_ref, d_ref, _, dx_ref, dw_ref, dc_scr):
        j = pl.program_id(0)
        q_scale = jnp.where(j < DN_HEADS, DN_DIM ** -0.5, 1.0).astype(F32)
        x = x_ref[...]
        w = w_ref[...]
        c = _conv(x, w)

        @pl.when(j < 2 * DN_HEADS)
        def _():
            _, vjp = jax.vjp(lambda c: _dn_post(c, q_scale), c)
            dc_scr[...] = vjp(d_ref[0])[0]

        @pl.when(j >= 2 * DN_HEADS)
        def _():
            _, vjp = jax.vjp(_silu, c)
            dc_scr[...] = vjp(d_ref[0])[0]

        dc = dc_scr[...]
        dx = w[DN_CONV - 1:DN_CONV] * dc
        dw_ref[DN_CONV - 1:DN_CONV, :] = jnp.sum(dc * x, axis=0, keepdims=True)
        for s in range(1, DN_CONV):
            dx = dx + w[DN_CONV - 1 - s:DN_CONV - s] * _shift_up(dc, s)
            dw_ref[DN_CONV - 1 - s:DN_CONV - s, :] = jnp.sum(dc * _shift_down(x, s), axis=0, keepdims=True)
        dx_ref[...] = dx.astype(BF16)

    return pl.pallas_call(
        body, grid=(nb,),
        in_specs=[pl.BlockSpec((S, LANES), lambda j: (0, P_QKV // LANES + j)),
                  pl.BlockSpec((DN_CONV, LANES), lambda j: (0, j)),
                  pl.BlockSpec((1, S, LANES), lambda j: (lax.div(j, DN_HEADS), 0, lax.rem(j, DN_HEADS))),
                  pl.BlockSpec(memory_space=pl.ANY)],
        out_specs=[pl.BlockSpec((S, LANES), lambda j: (0, P_QKV // LANES + j)),
                   pl.BlockSpec((DN_CONV, LANES), lambda j: (0, j))],
        out_shape=[jax.ShapeDtypeStruct(dproj.shape, dproj.dtype), jax.ShapeDtypeStruct((DN_CONV, DN_QKV), F32)],
        input_output_aliases={3: 0}, scratch_shapes=[pltpu.VMEM((S, LANES), F32)],
        name="dn_conv_bwd", compiler_params=_params(("parallel",)))(proj, conv_w, dqkvn, dproj)


def _expanders():
    eb = np.zeros((LANES, DN_WIDTH), np.float32)
    ea = np.zeros((LANES, DN_WIDTH), np.float32)
    for h in range(DN_HEADS):
        eb[h, h * DN_DIM:(h + 1) * DN_DIM] = 1.0
        ea[DN_HEADS + h, h * DN_DIM:(h + 1) * DN_DIM] = 1.0
    return jnp.asarray(eb), jnp.asarray(ea), jnp.asarray(eb.T), jnp.asarray(ea.T)


def _dn_gate_args(a_log, dt_bias):
    alog = jnp.repeat(a_log.reshape(1, DN_HEADS), DN_DIM, axis=1)
    dtb = _pad_to(jnp.pad(dt_bias.reshape(1, DN_HEADS), ((0, 0), (DN_HEADS, 0))), (1, LANES))
    return _expanders() + (alog, dtb)


def _dn_gate_specs(tm):
    return [_row(tm, LANES, P_BA // LANES), _full((LANES, DN_WIDTH)), _full((LANES, DN_WIDTH)),
            _full((DN_WIDTH, LANES)), _full((DN_WIDTH, LANES)), _full((1, DN_WIDTH)), _full((1, LANES))]


def _dn_gate_fn(ba, eb, ea, ebt, eat, alog, dtb):
    beta = _sel_right(jax.nn.sigmoid(ba), eb, ebt)
    g = -jnp.exp(alog) * _sel_right(jax.nn.softplus(ba + dtb), ea, eat)
    return beta, g


def _dn_gate_fwd(proj, a_log, dt_bias, tm=1024):
    S = proj.shape[0]
    args = _dn_gate_args(a_log, dt_bias)

    def body(ba_ref, eb_ref, ea_ref, ebt_ref, eat_ref, al_ref, dt_ref, beta_ref, g_ref):
        beta, g = _dn_gate_fn(ba_ref[...], eb_ref[...], ea_ref[...], ebt_ref[...], eat_ref[...], al_ref[...],
                              dt_ref[...])
        beta_ref[...] = beta
        g_ref[...] = g

    return pl.pallas_call(
        body, grid=(S // tm,), in_specs=_dn_gate_specs(tm), out_specs=[_row(tm, DN_WIDTH), _row(tm, DN_WIDTH)],
        out_shape=[jax.ShapeDtypeStruct((S, DN_WIDTH), F32), jax.ShapeDtypeStruct((S, DN_WIDTH), F32)],
        name="dn_gate_fwd", compiler_params=_params(("parallel",)))(proj, *args)


def _dn_gate_bwd(proj, a_log, dt_bias, dbeta, dg, dproj, tm=1024):
    S = proj.shape[0]
    args = _dn_gate_args(a_log, dt_bias)

    def body(ba_ref, eb_ref, ea_ref, ebt_ref, eat_ref, al_ref, dt_ref, dbeta_ref, dg_ref, _, dba_ref, dal_ref,
             ddt_ref):
        eb, ea, ebt, eat = eb_ref[...], ea_ref[...], ebt_ref[...], eat_ref[...]
        _, vjp = jax.vjp(lambda ba, al, dt: _dn_gate_fn(ba, eb, ea, ebt, eat, al, dt), ba_ref[...], al_ref[...],
                         dt_ref[...])
        dba, dal, ddt = vjp((dbeta_ref[...], dg_ref[...]))
        dba_ref[...] = dba.astype(BF16)

        @pl.when(pl.program_id(0) == 0)
        def _():
            dal_ref[...] = jnp.zeros_like(dal_ref)
            ddt_ref[...] = jnp.zeros_like(ddt_ref)

        dal_ref[...] += dal
        ddt_ref[...] += ddt

    return pl.pallas_call(
        body, grid=(S // tm,),
        in_specs=_dn_gate_specs(tm) + [_row(tm, DN_WIDTH), _row(tm, DN_WIDTH), pl.BlockSpec(memory_space=pl.ANY)],
        out_specs=[_row(tm, LANES, P_BA // LANES), _full((1, DN_WIDTH)), _full((1, LANES))],
        out_shape=[jax.ShapeDtypeStruct(dproj.shape, dproj.dtype), jax.ShapeDtypeStruct((1, DN_WIDTH), F32),
                   jax.ShapeDtypeStruct((1, LANES), F32)],
        input_output_aliases={len(args) + 3: 0},
        name="dn_gate_bwd", compiler_params=_params(("arbitrary",)))(proj, *args, dbeta, dg, dproj)


PREP_GROUPS = 4
PREP_CHUNKS = GROUP * PREP_GROUPS


def _dn_prep_specs():
    rows = PREP_CHUNKS * CHUNK
    q = pl.BlockSpec((rows, LANES), lambda h, c: (c, h))
    k = pl.BlockSpec((rows, LANES), lambda h, c: (c, DN_HEADS + h))
    v = pl.BlockSpec((rows, LANES), lambda h, c: (c, 2 * DN_HEADS + h))
    qk = pl.BlockSpec((1, rows, CHUNK), lambda h, c: (h, c, 0))
    egl = pl.BlockSpec((1, PREP_CHUNKS, 1, LANES), lambda h, c: (h, c, 0, 0))
    return q, k, v, qk, egl


def _dn_prep_fwd(qkvn, g, beta):
    S = qkvn.shape[0]
    nc = S // CHUNK
    q, k, v, qks, egl = _dn_prep_specs()

    def body(q_ref, k_ref, v_ref, g_ref, b_ref, u_ref, w_ref, qe_ref, kd_ref, qk_ref, egl_ref, t_ref):
        rows = PREP_CHUNKS * CHUNK
        grp = (PREP_GROUPS, GROUP_ROWS, LANES)
        u, w, qe, kd, qk, e, t = _dn_group(q_ref[...].reshape(grp), k_ref[...].reshape(grp), v_ref[...].reshape(grp),
                                           g_ref[...].reshape(grp), b_ref[...].reshape(grp))
        u_ref[...] = u.reshape(rows, LANES)
        w_ref[...] = w.reshape(rows, LANES)
        qe_ref[...] = qe.reshape(rows, LANES)
        kd_ref[...] = kd.reshape(rows, LANES)
        t_ref[0] = t.reshape(rows, GROUP_ROWS).astype(BF16)
        qk_ref[0] = qk.reshape(rows, CHUNK)
        egl_ref[0] = e.reshape(PREP_CHUNKS, 1, LANES)

    wide = jax.ShapeDtypeStruct((S, DN_WIDTH), F32)
    return pl.pallas_call(
        body, grid=(DN_HEADS, nc // PREP_CHUNKS), in_specs=[q, k, v, q, q],
        out_specs=[q, q, q, q, qks, egl, _dn_tinv_spec()],
        out_shape=[wide, wide, wide, wide, jax.ShapeDtypeStruct((DN_HEADS, S, CHUNK), F32),
                   jax.ShapeDtypeStruct((DN_HEADS, nc, 1, LANES), F32),
                   jax.ShapeDtypeStruct((DN_HEADS, S, GROUP_ROWS), BF16)],
        name="dn_prep_fwd", compiler_params=_params(("parallel", "parallel")))(qkvn, qkvn, qkvn, g, beta)


def _dn_tinv_spec():
    return pl.BlockSpec((1, PREP_CHUNKS * CHUNK, GROUP_ROWS), lambda h, c: (h, c, 0))


def _dn_prep_bwd(qkvn, g, beta, tinv, du, dw, dqe, dkd, dqk, degl):
    S = qkvn.shape[0]
    nc = S // CHUNK
    q, k, v, qks, egl = _dn_prep_specs()

    def body(q_ref, k_ref, v_ref, g_ref, b_ref, t_ref, du_ref, dw_ref, dqe_ref, dkd_ref, dqk_ref, degl_ref,
             dqkv_ref, dg_ref, db_ref):
        rows = PREP_CHUNKS * CHUNK
        grp = (PREP_GROUPS, GROUP_ROWS, LANES)
        t_saved = t_ref[0].reshape(PREP_GROUPS, GROUP_ROWS, GROUP_ROWS)
        _, vjp = jax.vjp(lambda *x: _dn_group(*x, t_saved=t_saved)[:6], q_ref[...].reshape(grp),
                         k_ref[...].reshape(grp), v_ref[...].reshape(grp), g_ref[...].reshape(grp),
                         b_ref[...].reshape(grp))
        dq, dk, dv, dg, db = vjp((du_ref[...].reshape(grp), dw_ref[...].reshape(grp), dqe_ref[...].reshape(grp),
                                  dkd_ref[...].reshape(grp), dqk_ref[0].reshape(PREP_GROUPS, GROUP_ROWS, CHUNK),
                                  degl_ref[0].reshape(PREP_GROUPS, GROUP, 1, LANES)))
        dqkv_ref[0] = dq.reshape(rows, LANES)
        dqkv_ref[1] = dk.reshape(rows, LANES)
        dqkv_ref[2] = dv.reshape(rows, LANES)
        dg_ref[...] = dg.reshape(rows, LANES)
        db_ref[...] = db.reshape(rows, LANES)

    wide = jax.ShapeDtypeStruct((S, DN_WIDTH), F32)
    rows = PREP_CHUNKS * CHUNK
    return pl.pallas_call(
        body, grid=(DN_HEADS, nc // PREP_CHUNKS), in_specs=[q, k, v, q, q, _dn_tinv_spec(), q, q, q, q, qks, egl],
        out_specs=[pl.BlockSpec((3, rows, LANES), lambda h, c: (0, c, h)), q, q],
        out_shape=[jax.ShapeDtypeStruct((3, S, DN_WIDTH), F32), wide, wide],
        name="dn_prep_bwd", compiler_params=_params(("parallel", "parallel")),
    )(qkvn, qkvn, qkvn, g, beta, tinv, du, dw, dqe, dkd, dqk, degl)


SCAN_CHUNKS = 8


def _dn_scan_specs(nc, reverse):
    nb = nc // SCAN_CHUNKS

    def cidx(c):
        return nb - 1 - c if reverse else c

    hc = pl.BlockSpec((SCAN_CHUNKS * CHUNK, DN_WIDTH), lambda c: (cidx(c), 0))
    qk = pl.BlockSpec((DN_HEADS, SCAN_CHUNKS * CHUNK, CHUNK), lambda c: (0, cidx(c), 0))
    egl = pl.BlockSpec((DN_HEADS, SCAN_CHUNKS, 1, LANES), lambda c: (0, cidx(c), 0, 0))
    st = pl.BlockSpec((DN_HEADS, SCAN_CHUNKS, DN_DIM, DN_DIM), lambda c: (0, cidx(c), 0, 0))
    return hc, qk, egl, st


def _heads(ref, i):
    return jnp.stack([ref[pl.ds(i * CHUNK, CHUNK), pl.ds(h * DN_DIM, DN_DIM)] for h in range(DN_HEADS)])


def _dn_scan_fwd(u, w, qe, kd, qk, egl):
    S = u.shape[0]
    nc = S // CHUNK
    hc, qks, egls, st = _dn_scan_specs(nc, False)

    def body(u_ref, w_ref, qe_ref, kd_ref, qk_ref, egl_ref, o_ref, st_ref, s_scr):
        @pl.when(pl.program_id(0) == 0)
        def _():
            s_scr[...] = jnp.zeros_like(s_scr)

        s = s_scr[...]
        for i in range(SCAN_CHUNKS):
            rows = pl.ds(i * CHUNK, CHUNK)
            st_ref[:, i] = s
            s, o = _dn_step(s, _heads(u_ref, i), _heads(w_ref, i), _heads(qe_ref, i), _heads(kd_ref, i),
                            qk_ref[:, rows, :], egl_ref[:, i])
            for h in range(DN_HEADS):
                o_ref[rows, pl.ds(h * DN_DIM, DN_DIM)] = o[h]
        s_scr[...] = s

    return pl.pallas_call(
        body, grid=(nc // SCAN_CHUNKS,), in_specs=[hc, hc, hc, hc, qks, egls], out_specs=[hc, st],
        out_shape=[jax.ShapeDtypeStruct((S, DN_WIDTH), F32), jax.ShapeDtypeStruct((DN_HEADS, nc, DN_DIM, DN_DIM), F32)],
        scratch_shapes=[pltpu.VMEM((DN_HEADS, DN_DIM, DN_DIM), F32)], name="dn_scan_fwd",
        compiler_params=_params(("arbitrary",)))(u, w, qe, kd, qk, egl)


def _dn_scan_bwd(u, w, qe, kd, qk, egl, states, do):
    S = u.shape[0]
    nc = S // CHUNK
    hc, qks, egls, st = _dn_scan_specs(nc, True)

    def body(u_ref, w_ref, qe_ref, kd_ref, qk_ref, egl_ref, st_ref, do_ref,
             du_ref, dw_ref, dqe_ref, dkd_ref, dqk_ref, degl_ref, ds_scr):
        @pl.when(pl.program_id(0) == 0)
        def _():
            ds_scr[...] = jnp.zeros_like(ds_scr)

        ds = ds_scr[...]
        for i in reversed(range(SCAN_CHUNKS)):
            rows = pl.ds(i * CHUNK, CHUNK)
            _, vjp = jax.vjp(_dn_step, st_ref[:, i], _heads(u_ref, i), _heads(w_ref, i), _heads(qe_ref, i),
                             _heads(kd_ref, i), qk_ref[:, rows, :], egl_ref[:, i])
            ds, du, dw, dqe, dkd, dqk, degl = vjp((ds, _heads(do_ref, i)))
            dqk_ref[:, rows, :] = dqk
            degl_ref[:, i] = degl
            for h in range(DN_HEADS):
                cols = pl.ds(h * DN_DIM, DN_DIM)
                du_ref[rows, cols] = du[h]
                dw_ref[rows, cols] = dw[h]
                dqe_ref[rows, cols] = dqe[h]
                dkd_ref[rows, cols] = dkd[h]
        ds_scr[...] = ds

    wide = jax.ShapeDtypeStruct((S, DN_WIDTH), F32)
    return pl.pallas_call(
        body, grid=(nc // SCAN_CHUNKS,), in_specs=[hc, hc, hc, hc, qks, egls, st, hc],
        out_specs=[hc, hc, hc, hc, qks, egls],
        out_shape=[wide, wide, wide, wide, jax.ShapeDtypeStruct((DN_HEADS, S, CHUNK), F32),
                   jax.ShapeDtypeStruct((DN_HEADS, nc, 1, LANES), F32)],
        scratch_shapes=[pltpu.VMEM((DN_HEADS, DN_DIM, DN_DIM), F32)], name="dn_scan_bwd",
        compiler_params=_params(("arbitrary",)))(u, w, qe, kd, qk, egl, states, do)


def _dn_out_fwd(o, proj, gain, tm=1024):
    S = o.shape[0]

    def body(o_ref, z_ref, g_ref, y_ref):
        y_ref[...] = _dn_out(o_ref[...], z_ref[...], g_ref[...]).astype(BF16)

    hs = pl.BlockSpec((tm, LANES), lambda i, h: (i, h))
    zs = pl.BlockSpec((tm, LANES), lambda i, h: (i, P_Z // LANES + h))
    return pl.pallas_call(
        body, grid=(S // tm, DN_HEADS), in_specs=[hs, zs, _full((1, DN_DIM))], out_specs=hs,
        out_shape=jax.ShapeDtypeStruct((S, DN_WIDTH), BF16), name="dn_out_fwd",
        compiler_params=_params(("parallel", "parallel")))(o, proj, gain)


_ANY = pl.BlockSpec(memory_space=pl.ANY)


def _dn_out_bwd(o, proj, gain, dy, dproj, tm=1024):
    S = o.shape[0]

    def body(o_ref, z_ref, g_ref, dy_ref, _, do_ref, dz_ref, dg_ref):
        _, vjp = jax.vjp(_dn_out, o_ref[...], z_ref[...], g_ref[...])
        do, dz, dg = vjp(dy_ref[...])
        do_ref[...] = do
        dz_ref[...] = dz.astype(BF16)

        @pl.when((pl.program_id(0) == 0) & (pl.program_id(1) == 0))
        def _():
            dg_ref[...] = jnp.zeros_like(dg_ref)

        dg_ref[...] += dg

    hs = pl.BlockSpec((tm, LANES), lambda i, h: (i, h))
    zs = pl.BlockSpec((tm, LANES), lambda i, h: (i, P_Z // LANES + h))
    return pl.pallas_call(
        body, grid=(S // tm, DN_HEADS), in_specs=[hs, zs, _full((1, DN_DIM)), hs, _ANY],
        out_specs=[hs, zs, _full((1, DN_DIM))],
        out_shape=[jax.ShapeDtypeStruct((S, DN_WIDTH), F32), jax.ShapeDtypeStruct(dproj.shape, dproj.dtype),
                   jax.ShapeDtypeStruct((1, DN_DIM), F32)],
        input_output_aliases={4: 1},
        name="dn_out_bwd", compiler_params=_params(("arbitrary", "arbitrary")))(o, proj, gain, dy, dproj)


def _rel_buckets():
    qi = np.arange(BLOCK)[:, None]
    kj = np.arange(2 * BLOCK)[None, :]
    n = np.maximum(BLOCK + qi - kj, 0)
    max_exact = REL_BUCKETS // 2
    nf = np.maximum(n, 1).astype(np.float32)
    large = max_exact + (np.log(nf / np.float32(max_exact)) / np.float32(math.log(REL_MAX_DIST / max_exact))
                         * np.float32(REL_BUCKETS - max_exact)).astype(np.int32)
    large = np.minimum(large, REL_BUCKETS - 1)
    return np.where(n < max_exact, n, large).astype(np.int32)


def _bias_fwd(rel_bias):
    buckets = jnp.asarray(_rel_buckets())

    def body(rb_ref, bk_ref, o_ref):
        bk = bk_ref[...]
        for h in range(SWA_HEADS):
            acc = jnp.zeros((BLOCK, 2 * BLOCK), F32)
            for b in range(REL_BUCKETS):
                acc = jnp.where(bk == b, rb_ref[b, h], acc)
            for first in range(2):
                o_ref[first, h] = jnp.where(_swa_mask(1 - first), acc, -jnp.inf)

    return pl.pallas_call(
        body, in_specs=[pl.BlockSpec(memory_space=pltpu.SMEM), pl.BlockSpec(memory_space=pltpu.VMEM)],
        out_specs=pl.BlockSpec(memory_space=pltpu.VMEM),
        out_shape=jax.ShapeDtypeStruct((2, SWA_HEADS, BLOCK, 2 * BLOCK), F32), name="swa_bias_fwd",
        compiler_params=_params())(rel_bias, buckets)


def _bias_bwd(dbias):
    buckets = jnp.asarray(_rel_buckets())

    def body(d_ref, bk_ref, o_ref):
        bk = bk_ref[...]
        lane = lax.broadcasted_iota(jnp.int32, (1, LANES), 1)
        for h in range(SWA_HEADS):
            d = d_ref[h]
            row = jnp.zeros((1, LANES), F32)
            for b in range(REL_BUCKETS):
                part = jnp.sum(jnp.where(bk == b, d, 0.0), axis=1, keepdims=True)
                row = jnp.where(lane == b, jnp.sum(part, axis=0, keepdims=True), row)
            o_ref[h:h + 1, :] = row

    return pl.pallas_call(
        body, in_specs=[pl.BlockSpec(memory_space=pltpu.VMEM), pl.BlockSpec(memory_space=pltpu.VMEM)],
        out_specs=pl.BlockSpec(memory_space=pltpu.VMEM),
        out_shape=jax.ShapeDtypeStruct((SWA_HEADS, LANES), F32), name="swa_bias_bwd",
        compiler_params=_params())(dbias, buckets)


def _swa_mask(n):
    qi = lax.broadcasted_iota(jnp.int32, (BLOCK, 2 * BLOCK), 0)
    kj = lax.broadcasted_iota(jnp.int32, (BLOCK, 2 * BLOCK), 1)
    dist = BLOCK + qi - kj
    return (dist >= 0) & (dist < WINDOW) & ((n > 0) | (kj >= BLOCK))


def _swa_in_specs():
    q = pl.BlockSpec((BLOCK, SWA_WIDTH), lambda n: (n, P_SQ // SWA_WIDTH))
    kc = pl.BlockSpec((BLOCK, SWA_KVW), lambda n: (n, P_SK // SWA_KVW))
    kp = pl.BlockSpec((BLOCK, SWA_KVW), lambda n: (jnp.maximum(n - 1, 0), P_SK // SWA_KVW))
    vc = pl.BlockSpec((BLOCK, SWA_KVW), lambda n: (n, P_SV // SWA_KVW))
    vp = pl.BlockSpec((BLOCK, SWA_KVW), lambda n: (jnp.maximum(n - 1, 0), P_SV // SWA_KVW))
    band = pl.BlockSpec((None, SWA_HEADS, BLOCK, 2 * BLOCK), lambda n: (jnp.where(n == 0, 1, 0), 0, 0, 0))
    small = [_full((1, SWA_DIM)), _full((1, SWA_DIM)), _full((1, SWA_HEADS)), band]
    return [q, kp, kc, vp, vc] + small


def _swa_load(q_ref, kp_ref, kc_ref, vp_ref, vc_ref, s_ref):
    q = jnp.stack([q_ref[:, pl.ds(h * SWA_DIM, SWA_DIM)] for h in range(SWA_HEADS)])
    kbands, vbands = [], []
    for kv in range(SWA_KV):
        cols = pl.ds(kv * SWA_DIM, SWA_DIM)
        kbands += [jnp.concatenate([kp_ref[:, cols], kc_ref[:, cols]], axis=0)] * SWA_GROUP
        vbands += [jnp.concatenate([vp_ref[:, cols], vc_ref[:, cols]], axis=0)] * SWA_GROUP
    sinks = jnp.stack([s_ref[:, pl.ds(h, 1)] for h in range(SWA_HEADS)])
    return q, jnp.stack(kbands), jnp.stack(vbands), sinks


def _swa_fwd(proj, q_gain, k_gain, sinks, bias):
    S = proj.shape[0]

    def body(q_ref, kp_ref, kc_ref, vp_ref, vc_ref, qg_ref, kg_ref, s_ref, bias_ref, y_ref):
        q, kband, vband, sk = _swa_load(q_ref, kp_ref, kc_ref, vp_ref, vc_ref, s_ref)
        out = _swa_block(q, kband, vband, qg_ref[...], kg_ref[...], sk, bias_ref[...])
        for h in range(SWA_HEADS):
            y_ref[:, pl.ds(h * SWA_DIM, SWA_DIM)] = out[h].astype(BF16)

    return pl.pallas_call(
        body, grid=(S // BLOCK,), in_specs=_swa_in_specs(),
        out_specs=pl.BlockSpec((BLOCK, SWA_WIDTH), lambda n: (n, 0)),
        out_shape=jax.ShapeDtypeStruct((S, SWA_WIDTH), BF16), name="swa_fwd",
        compiler_params=_params(("parallel",)))(proj, proj, proj, proj, proj, q_gain, k_gain, sinks, bias)


def _swa_bwd(proj, q_gain, k_gain, sinks, bias, dy, dproj):
    S = proj.shape[0]

    def body(q_ref, kp_ref, kc_ref, vp_ref, vc_ref, qg_ref, kg_ref, s_ref, bias_ref, dy_ref, _,
             dq_ref, dk_ref, dv_ref, dqg_ref, dkg_ref, ds_ref, dbias_ref):
        n = pl.program_id(0)

        @pl.when(n == 0)
        def _():
            for r in (dk_ref, dv_ref, dqg_ref, dkg_ref, ds_ref, dbias_ref):
                r[...] = jnp.zeros_like(r)

        cur = pl.ds(pl.multiple_of(n * BLOCK, BLOCK), BLOCK)
        prev = pl.ds(pl.multiple_of(jnp.maximum(n - 1, 0) * BLOCK, BLOCK), BLOCK)
        q, kband, vband, sk = _swa_load(q_ref, kp_ref, kc_ref, vp_ref, vc_ref, s_ref)
        _, vjp = jax.vjp(_swa_block, q, kband, vband, qg_ref[...], kg_ref[...], sk, bias_ref[...])
        dy = jnp.stack([dy_ref[:, pl.ds(h * SWA_DIM, SWA_DIM)] for h in range(SWA_HEADS)])
        dq, dkb, dvb, dqg, dkg, dsk, dbs = vjp(dy)
        for h in range(SWA_HEADS):
            dq_ref[:, pl.ds(h * SWA_DIM, SWA_DIM)] = dq[h].astype(BF16)
            ds_ref[:, pl.ds(h, 1)] += dsk[h]
        dbias_ref[...] += dbs
        dqg_ref[...] += dqg
        dkg_ref[...] += dkg
        for kv in range(SWA_KV):
            cols = pl.ds(kv * SWA_DIM, SWA_DIM)
            group = range(kv * SWA_GROUP, (kv + 1) * SWA_GROUP)
            dk_kv = sum(dkb[h] for h in group)
            dv_kv = sum(dvb[h] for h in group)
            dk_ref[cur, cols] += dk_kv[BLOCK:]
            dv_ref[cur, cols] += dv_kv[BLOCK:]

            @pl.when(n > 0)
            def _(cols=cols, dk_kv=dk_kv, dv_kv=dv_kv):
                dk_ref[prev, cols] += dk_kv[:BLOCK]
                dv_ref[prev, cols] += dv_kv[:BLOCK]

    return pl.pallas_call(
        body, grid=(S // BLOCK,),
        in_specs=_swa_in_specs() + [pl.BlockSpec((BLOCK, SWA_WIDTH), lambda n: (n, 0)),
                                    pl.BlockSpec(memory_space=pl.ANY)],
        out_specs=[pl.BlockSpec((BLOCK, SWA_WIDTH), lambda n: (n, P_SQ // SWA_WIDTH)), _full((S, SWA_KVW)),
                   _full((S, SWA_KVW)), _full((1, SWA_DIM)), _full((1, SWA_DIM)), _full((1, SWA_HEADS)),
                   _full((SWA_HEADS, BLOCK, 2 * BLOCK))],
        out_shape=[jax.ShapeDtypeStruct(dproj.shape, dproj.dtype), jax.ShapeDtypeStruct((S, SWA_KVW), F32),
                   jax.ShapeDtypeStruct((S, SWA_KVW), F32), jax.ShapeDtypeStruct((1, SWA_DIM), F32),
                   jax.ShapeDtypeStruct((1, SWA_DIM), F32), jax.ShapeDtypeStruct((1, SWA_HEADS), F32),
                   jax.ShapeDtypeStruct((SWA_HEADS, BLOCK, 2 * BLOCK), F32)],
        input_output_aliases={10: 0},
        name="swa_bwd", compiler_params=_params(("arbitrary",)),
    )(proj, proj, proj, proj, proj, q_gain, k_gain, sinks, bias, dy, dproj)


def _kv_into(dproj, dk, dv, tm=1024):
    S = dk.shape[0]

    def body(dk_ref, dv_ref, _, o_ref):
        o_ref[:, :SWA_KVW] = dk_ref[...].astype(BF16)
        o_ref[:, SWA_KVW:] = dv_ref[...].astype(BF16)

    return pl.pallas_call(
        body, grid=(S // tm,), in_specs=[_row(tm, SWA_KVW), _row(tm, SWA_KVW), pl.BlockSpec(memory_space=pl.ANY)],
        out_specs=_row(tm, 2 * SWA_KVW, P_SK // (2 * SWA_KVW)),
        out_shape=jax.ShapeDtypeStruct(dproj.shape, dproj.dtype), input_output_aliases={2: 0},
        name="swa_kv_into", compiler_params=_params(("parallel",)))(dk, dv, dproj)


def _position():
    return lax.axis_index("x"), lax.axis_index("y"), lax.axis_index("c")


def _all_gather(shards, name="all_gather_weights"):
    na = len(shards)

    def body(*refs):
        x_refs, out_refs = refs[:na], refs[na:2 * na]
        send_sems, recv_sems, local_sems = refs[2 * na:]
        x, y, c = _position()
        me, sibling = (x, y, c), (x, y, 1 - c)
        chips = [(1 - x, y), (x, 1 - y), (1 - x, 1 - y)]

        def copy(a, k, block, to, own=False):
            px, py, pc = block
            slot = out_refs[a].at[4 * px + 2 * py + pc]
            return pltpu.make_async_remote_copy(
                src_ref=x_refs[a] if own else slot, dst_ref=slot, send_sem=send_sems.at[7 * a + k],
                recv_sem=recv_sems.at[7 * a + k], device_id=to, device_id_type=MESH_ID)

        mine = [pltpu.make_async_copy(x_refs[a], out_refs[a].at[4 * x + 2 * y + c], local_sems.at[a])
                for a in range(na)]
        for cp in mine:
            cp.start()
        first = []
        for a in range(na):
            first.append(copy(a, 0, me, sibling, own=True))
            first += [copy(a, 1 + j, me, (*chip, c), own=True) for j, chip in enumerate(chips)]
        for cp in first:
            cp.start()
        passed = []
        for j, chip in enumerate(chips):
            for a in range(na):
                copy(a, 1 + j, (*chip, c), me).wait_recv()
                passed.append(copy(a, 4 + j, (*chip, c), sibling))
                passed[-1].start()
        for a in range(na):
            copy(a, 0, sibling, me).wait_recv()
            for j, chip in enumerate(chips):
                copy(a, 4 + j, (*chip, 1 - c), me).wait_recv()
        for cp in first + passed:
            cp.wait_send()
        for cp in mine:
            cp.wait()

    return pl.pallas_call(
        body, in_specs=[pl.BlockSpec(memory_space=pl.ANY)] * na, out_specs=[pl.BlockSpec(memory_space=pl.ANY)] * na,
        out_shape=[jax.ShapeDtypeStruct((N_DEV,) + s.shape, s.dtype) for s in shards],
        scratch_shapes=[pltpu.SemaphoreType.DMA((7 * na,)), pltpu.SemaphoreType.DMA((7 * na,)),
                        pltpu.SemaphoreType.DMA((na,))],
        name=name)(*shards)


_HBM = pl.BlockSpec(memory_space=pltpu.HBM)
_SEM = pl.BlockSpec(memory_space=pltpu.SEMAPHORE)
_DATAFLOW = pltpu.SideEffectType.DATAFLOW_SIDE_EFFECTING


def _peers(x, y, c):
    out = []
    for k in range(1, N_DEV):
        px, py, pc = x ^ (k >> 2), y ^ ((k >> 1) & 1), c ^ (k & 1)
        out.append(((px, py, pc), 4 * px + 2 * py + pc))
    return out


def _split_copies(src_refs, land_refs, send_sems, recv_sems, scatter):
    x, y, c = _position()
    me = 4 * x + 2 * y + c
    sends, recvs = [], []
    for k, (peer_id, peer) in enumerate(_peers(x, y, c)):
        for a, (src, land) in enumerate(zip(src_refs, land_refs)):
            sems = dict(send_sem=send_sems.at[7 * a + k], recv_sem=recv_sems.at[7 * a + k],
                        device_id=peer_id, device_id_type=MESH_ID)
            mine = src.at[peer] if scatter else src
            sends.append(pltpu.make_async_remote_copy(src_ref=mine, dst_ref=land.at[me], **sems))
            recvs.append(pltpu.make_async_remote_copy(src_ref=mine, dst_ref=land.at[peer], **sems))
    return sends, recvs


def _all_gather_direct(shards, name, after):
    na = len(shards)

    def body(*refs):
        x_refs, out_refs = refs[:na], refs[na + 1:2 * na + 1]
        send_sems, recv_sems, local_sems = refs[2 * na + 1:]
        x, y, c = _position()
        me = 4 * x + 2 * y + c
        local = [pltpu.make_async_copy(x_refs[a], out_refs[a].at[me], local_sems.at[a]) for a in range(na)]
        sends, recvs = _split_copies(x_refs, out_refs, send_sems, recv_sems, False)
        for cp in local + sends:
            cp.start()
        for cp in recvs:
            cp.wait_recv()
        for cp in sends:
            cp.wait_send()
        for cp in local:
            cp.wait()

    return pl.pallas_call(
        body, in_specs=[pl.BlockSpec(memory_space=pl.ANY)] * (na + 1),
        out_specs=[pl.BlockSpec(memory_space=pl.ANY)] * na,
        out_shape=[jax.ShapeDtypeStruct((N_DEV,) + s.shape, s.dtype) for s in shards],
        scratch_shapes=[pltpu.SemaphoreType.DMA((7 * na,)), pltpu.SemaphoreType.DMA((7 * na,)),
                        pltpu.SemaphoreType.DMA((na,))],
        name=name)(*shards, after)


def _exchange_start(srcs, scatter, name, after=None):
    na = len(srcs)
    lands = [lax.empty(s.shape if scatter else (N_DEV,) + s.shape, s.dtype) for s in srcs]
    extra = [] if after is None else [after]

    def body(*refs):
        src_refs, land_refs = refs[:na], refs[na:2 * na]
        send_sems, recv_sems = refs[2 * na + len(extra)], refs[2 * na + len(extra) + 1]
        token = refs[-1]
        sends, _ = _split_copies(src_refs, land_refs, send_sems, recv_sems, scatter)
        for cp in sends:
            cp.start()
        token[...] = jnp.zeros_like(token)

    hbm = lambda a: pltpu.HBM(a.shape, a.dtype)
    out = pl.pallas_call(
        body, name=name,
        out_shape=(pltpu.SemaphoreType.DMA((7 * na,)), pltpu.SemaphoreType.DMA((7 * na,)),
                   *[hbm(s) for s in srcs], *[hbm(l) for l in lands], jax.ShapeDtypeStruct((8, LANES), F32)),
        in_specs=[_HBM] * (2 * na) + [pl.BlockSpec(memory_space=pl.ANY)] * len(extra),
        out_specs=(_SEM, _SEM, *[_HBM] * (2 * na), pl.BlockSpec(memory_space=pltpu.VMEM)),
        input_output_aliases={i: 2 + i for i in range(2 * na)},
        compiler_params=pltpu.CompilerParams(has_side_effects=_DATAFLOW),
    )(*[pltpu.with_memory_space_constraint(s, pltpu.HBM) for s in srcs],
      *[pltpu.with_memory_space_constraint(l, pltpu.HBM) for l in lands], *extra)
    return (out[0], out[1], list(out[2:2 + na]), list(out[2 + na:2 + 2 * na])), out[-1]


def _exchange_wait(handle, after, scatter, name):
    send_sems, recv_sems, srcs, lands = handle
    na = len(srcs)

    def body(*refs):
        src_refs, land_refs = refs[:na], refs[na:2 * na]
        s_sems, r_sems = refs[2 * na], refs[2 * na + 1]
        sends, recvs = _split_copies(src_refs, land_refs, s_sems, r_sems, scatter)
        for cp in sends:
            cp.wait_send()
        for cp in recvs:
            cp.wait_recv()

    hbm = lambda a: pltpu.HBM(a.shape, a.dtype)
    out = pl.pallas_call(
        body, name=name, out_shape=(*[hbm(s) for s in srcs], *[hbm(l) for l in lands]),
        in_specs=[_HBM] * (2 * na) + [_SEM, _SEM, pl.BlockSpec(memory_space=pl.ANY)],
        out_specs=tuple([_HBM] * (2 * na)), input_output_aliases={i: i for i in range(2 * na)},
        compiler_params=pltpu.CompilerParams(has_side_effects=_DATAFLOW),
    )(*srcs, *lands, send_sems, recv_sems, after)
    return list(out[:na]), list(out[na:])


def _own_slot(landed, own):
    me = 4 * lax.axis_index("x") + 2 * lax.axis_index("y") + lax.axis_index("c")
    return lax.dynamic_update_slice_in_dim(landed, own[None], me, axis=0)


def _adam_update(parts, w, m, v, name, tr=256):
    _, r, c = w.shape
    tr = _pick_rows(r, tr)
    cp = parts.shape[2]

    def body(p_ref, w_ref, m_ref, v_ref, g_ref, d_ref, nm_ref, nv_ref):
        g = p_ref[0, :, pl.ds(0, c)].astype(F32)
        for i in range(1, N_DEV):
            g = g + p_ref[i, :, pl.ds(0, c)].astype(F32)
        delta, nm, nv = _adamw(w_ref[0], g, m_ref[0], v_ref[0])
        g_ref[0] = g
        d_ref[0] = delta
        nm_ref[0] = nm
        nv_ref[0] = nv

    rs = pl.BlockSpec((1, tr, c), lambda i: (0, i, 0))
    return pl.pallas_call(
        body, grid=(r // tr,), in_specs=[pl.BlockSpec((N_DEV, tr, cp), lambda i: (0, i, 0)), rs, rs, rs],
        out_specs=[rs] * 4, out_shape=[jax.ShapeDtypeStruct((1, r, c), F32)] * 4, name=name,
        compiler_params=_params(("parallel",)))(parts, w, m, v)


def _pick_rows(rows, target):
    if rows <= target:
        return rows
    t = target
    while t >= 16:
        if rows % t == 0:
            return t
        t -= 16
    return rows


BIG = ("w_in", "w_branch_dn", "w_branch_swa", "w_out", "w_gate", "w_up", "w_down")
IN_SHARD, IN_WIRE = D_IN // N_DEV, 640
FF_SHARD, FF_WIRE = D_FF // N_DEV, 384
D_FFP = N_DEV * FF_WIRE
BIG_SHAPES = {"w_in": ((D_MODEL, IN_SHARD), (D_MODEL, IN_WIRE)),
              "w_branch_dn": ((DN_WIDTH, LANES), (DN_WIDTH, LANES)),
              "w_branch_swa": ((SWA_WIDTH, LANES), (SWA_WIDTH, LANES)),
              "w_out": ((LANES, D_MODEL), (LANES, D_MODEL)),
              "w_gate": ((D_MODEL, FF_SHARD), (D_MODEL, FF_WIRE)),
              "w_up": ((D_MODEL, FF_SHARD), (D_MODEL, FF_WIRE)),
              "w_down": ((FF_SHARD, D_MODEL), (FF_WIRE, D_MODEL))}
CONV_SHARD, CONV_WIRE = (DN_CONV, DN_QKV // N_DEV), (8, 256)


def _pad_to(a, shape):
    return jnp.pad(a, [(0, t - s) for s, t in zip(a.shape, shape)])


_IN_SEGS = ((R_GATE, 2048, P_GATE), (R_QKV, DN_QKV, P_QKV), (R_Z, DN_WIDTH, P_Z), (R_SQ, SWA_WIDTH, P_SQ),
            (R_SK, SWA_KVW, P_SK), (R_SV, SWA_KVW, P_SV), (R_B, 8, P_BA))


def _w_in_from_blocks(blocks):
    parts = []
    for rs, n, _ in _IN_SEGS:
        for dev in range(N_DEV):
            lo, hi = max(rs, IN_SHARD * dev), min(rs + n, IN_SHARD * (dev + 1))
            if lo < hi:
                parts.append(blocks[dev, :, lo - IN_SHARD * dev:hi - IN_SHARD * dev])
    parts.append(jnp.zeros((blocks.shape[1], P_WIDTH - P_BA - 8), blocks.dtype))
    return jnp.concatenate(parts, axis=1)


def _w_in_to_blocks(g):
    out = []
    for dev in range(N_DEV):
        parts = []
        for rs, n, ps in sorted(_IN_SEGS):
            lo, hi = max(rs, IN_SHARD * dev), min(rs + n, IN_SHARD * (dev + 1))
            if lo < hi:
                parts.append(g[:, ps + lo - rs:ps + hi - rs])
        parts.append(jnp.zeros((g.shape[0], IN_WIRE - IN_SHARD), g.dtype))
        out.append(jnp.concatenate(parts, axis=1))
    return jnp.stack(out)


SMALL = {"attn_norm": (0, (1, D_MODEL)), "ffn_norm": (1, (1, D_MODEL)), "dn_out_norm": (2, (1, DN_DIM)),
         "swa_q_norm": (3, (1, SWA_DIM)), "swa_k_norm": (4, (1, SWA_DIM)), "dn_a_log": (5, (1, DN_HEADS)),
         "dn_dt_bias": (6, (1, DN_HEADS)), "swa_sinks": (7, (1, SWA_HEADS)), "rel_bias": (8, (REL_BUCKETS, SWA_HEADS))}
SMALL_SHEET = (48, D_MODEL)


LOSS_ROW = 40


def _small_pack(grads, loss_local):
    names = list(SMALL)

    def body(*refs):
        o_ref = refs[-1]
        o_ref[...] = jnp.zeros_like(o_ref)
        for n, ref in zip(names, refs):
            r0, (nr, nc) = SMALL[n]
            o_ref[r0:r0 + nr, 0:nc] = ref[...]
        o_ref[LOSS_ROW:LOSS_ROW + 1, 0:1] = refs[len(names)][...]

    return pl.pallas_call(
        body, in_specs=[pl.BlockSpec(memory_space=pltpu.VMEM)] * (len(names) + 1),
        out_specs=pl.BlockSpec(memory_space=pltpu.VMEM), out_shape=jax.ShapeDtypeStruct(SMALL_SHEET, F32),
        name="small_pack", compiler_params=_params())(*[grads[n].reshape(SMALL[n][1]) for n in names], loss_local)


def _small_update(sheets, w, m, v):
    names = list(SMALL)
    k = len(names)

    def body(*refs):
        p_ref = refs[0]
        ins, outs = refs[1:1 + 3 * k], refs[1 + 3 * k:]
        loss = p_ref[0, LOSS_ROW:LOSS_ROW + 1, 0:1]
        for i in range(1, N_DEV):
            loss = loss + p_ref[i, LOSS_ROW:LOSS_ROW + 1, 0:1]
        outs[4 * k][...] = loss
        for t, n in enumerate(names):
            r0, (nr, nc) = SMALL[n]
            g = p_ref[0, r0:r0 + nr, 0:nc]
            for i in range(1, N_DEV):
                g = g + p_ref[i, r0:r0 + nr, 0:nc]
            delta, nm, nv = _adamw(ins[t][...], g, ins[k + t][...], ins[2 * k + t][...])
            for kind, val in enumerate((g, delta, nm, nv)):
                outs[kind * k + t][...] = val

    shapes = [jax.ShapeDtypeStruct(SMALL[n][1], F32) for n in names]
    vm = pl.BlockSpec(memory_space=pltpu.VMEM)
    res = pl.pallas_call(
        body, in_specs=[vm] * (1 + 3 * k), out_specs=[vm] * (4 * k + 1),
        out_shape=shapes * 4 + [jax.ShapeDtypeStruct((1, 1), F32)], name="adam_small", compiler_params=_params(),
    )(sheets, *[d[n].reshape(SMALL[n][1]) for d in (w, m, v) for n in names])
    return {n: tuple(res[kind * k + t] for kind in range(4)) for t, n in enumerate(names)}, res[4 * k]


def kernel(x, attn_norm, w_in, dn_conv, dn_a_log, dn_dt_bias, dn_out_norm, swa_q_norm, swa_k_norm, swa_sinks, rel_bias, w_branch_dn, w_branch_swa, w_out, ffn_norm, w_gate, w_up, w_down, loss_target, m_attn_norm, m_w_in, m_dn_conv, m_dn_a_log, m_dn_dt_bias, m_dn_out_norm, m_swa_q_norm, m_swa_k_norm, m_swa_sinks, m_rel_bias, m_w_branch_dn, m_w_branch_swa, m_w_out, m_ffn_norm, m_w_gate, m_w_up, m_w_down, v_attn_norm, v_w_in, v_dn_conv, v_dn_a_log, v_dn_dt_bias, v_dn_out_norm, v_swa_q_norm, v_swa_k_norm, v_swa_sinks, v_rel_bias, v_w_branch_dn, v_w_branch_swa, v_w_out, v_ffn_norm, v_w_gate, v_w_up, v_w_down):
    args = dict(locals())
    S = x.shape[1]
    xs = x.reshape(S, D_MODEL)
    target = loss_target.reshape(S, D_MODEL)

    w_loc = {n: args[n].reshape(BIG_SHAPES[n][0]) for n in BIG}
    conv_loc = dn_conv.reshape(CONV_SHARD)
    wire = {n: _pad_to(w_loc[n], BIG_SHAPES[n][1]).astype(BF16) for n in BIG}
    first = _all_gather([wire["w_in"], _pad_to(conv_loc, CONV_WIRE)])
    later = [n for n in BIG if n != "w_in"]
    rest_handle, rest_token = _exchange_start([wire[n] for n in later], False, "gather_rest_start", after=first[1])
    w_pad = _w_in_from_blocks(first[0])
    conv_w = jnp.concatenate([first[1][d, :DN_CONV, :CONV_SHARD[1]] for d in range(N_DEV)], axis=1)

    h = _norm_fwd(xs, attn_norm + rest_token[0, 0], "norm1_fwd")
    proj = _mm([(h, w_pad)], "nn", F32, "mm_in", 1024, 1664, j_outer=True)
    qkvn = _dn_conv_fwd(proj, conv_w)
    beta, g = _dn_gate_fwd(proj, dn_a_log, dn_dt_bias)
    u, w, qe, kd, qk, egl, tinv = _dn_prep_fwd(qkvn, g, beta)
    o, states = _dn_scan_fwd(u, w, qe, kd, qk, egl)
    y_dn = _dn_out_fwd(o, proj, dn_out_norm)
    bias = _bias_fwd(rel_bias)
    y_swa = _swa_fwd(proj, swa_q_norm, swa_k_norm, swa_sinks, bias)
    rest_src, rest_land = _exchange_wait(rest_handle, y_swa, False, "gather_rest_wait")
    G = {n: _own_slot(land, src) for n, src, land in zip(later, rest_src, rest_land)}
    w_bdn, w_bswa, w_g, w_u = G["w_branch_dn"], G["w_branch_swa"], G["w_gate"], G["w_up"]
    w_o = G["w_out"].reshape(D_MODEL, D_MODEL)
    w_d = G["w_down"].reshape(D_FFP, D_MODEL)
    gates = [(proj, P_GATE // 512), (proj, (P_GATE + D_MODEL) // 512)]
    a_dn, a_swa, merged = _mm_fused(
        [(y_dn, w_bdn), (y_swa, w_bswa)], "nn", "mm_branch_merge", 1024, 512,
        lambda p, e: (p[0], p[1], _merge(e[0], e[1], p[0], p[1])), gates, (F32, F32, BF16), b_blocks=True)

    def resid_norm(p, e):
        x1 = e[0] + p[0]
        return x1, _rms(x1, e[1])

    x1, h2 = _mm_fused([(merged, w_o)], "nn", "mm_out_norm", 512, D_MODEL, resid_norm,
                       [(xs, 0), (ffn_norm, None)], (F32, BF16))
    gate, up, act = _mm_fused([(h2, w_g), (h2, w_u)], "nn", "mm_gate_up_act", 1024, 768,
                              lambda p, e: (p[0], p[1], _act(p[0], p[1])), [], (F32, F32, BF16),
                              j_outer=True, b_blocks=True)

    def loss_head(p, e):
        diff = e[0] + p[0] - e[1]
        dy = diff * (1.0 / D_MODEL)
        part = jnp.sum(jnp.mean(diff * diff, axis=-1, keepdims=True), axis=0, keepdims=True) * 0.5
        return dy, dy, part

    dy, dy_b, loss_local = _mm_fused([(act, w_d)], "nn", "mm_down_loss", 512, D_MODEL, loss_head,
                                     [(x1, 0), (target, 0)], (F32, BF16), sum_shape=(1, 1))

    def act_bwd(p, e):
        _, vjp = jax.vjp(_act, e[0], e[1])
        return vjp(p[0])

    dgate, dup = _mm_fused([(dy_b, w_d)], "nt", "mm_dact_act", 1024, 768, act_bwd, [(gate, 0), (up, 0)],
                           (BF16, BF16), j_outer=True)
    g_w_down = _mm([(act, dy_b)], "tn", BF16, "mm_dw_down", 768, D_MODEL, j_outer=True)
    g_w_down = g_w_down.reshape(N_DEV, FF_WIRE, D_MODEL)
    g_w_gate = _mm([(h2, dgate)], "tn", BF16, "mm_dw_gate", D_MODEL, 768, out_blocks=True)
    g_w_up = _mm([(h2, dup)], "tn", BF16, "mm_dw_up", D_MODEL, 768, out_blocks=True)
    ffn_handle, ffn_token = _exchange_start([g_w_down, g_w_gate, g_w_up], True, "scatter_ffn_start")

    def norm_bwd(p, e):
        _, vjp = jax.vjp(_rms, e[0], e[2])
        dx, dgain = vjp(sum(p))
        dx = dx + e[1]
        return dx, dx, dgain

    dx1, dx1_b, g_ffn_norm = _mm_fused(
        [(dgate, w_g), (dup, w_u)], "nt", "mm_dh2_norm", 256, D_MODEL, norm_bwd,
        [(x1, 0), (dy, 0), (ffn_norm + ffn_token[0, 0], None)], (F32, BF16), b_blocks=True, sum_shape=(1, D_MODEL))
    def merge_bwd(p, e):
        _, vjp = jax.vjp(_merge, *e)
        dg0, dg1, da_dn, da_swa = vjp(p[0])
        return jnp.concatenate([dg0, dg1], axis=1), da_dn, da_swa

    dproj, da_dn, da_swa = _mm_fused(
        [(dx1_b, w_o)], "nt", "mm_dmerged_merge", 512, D_MODEL, merge_bwd,
        [(proj, P_GATE // D_MODEL), (proj, P_GATE // D_MODEL + 1), (a_dn, 0), (a_swa, 0)], (BF16,) * 3,
        wide_first=(P_WIDTH, 2 * D_MODEL))
    g_w_out = _mm([(merged, dx1_b)], "tn", BF16, "mm_dw_out", 512, D_MODEL, j_outer=True)
    g_w_out = g_w_out.reshape(N_DEV, LANES, D_MODEL)
    dy_dn = _mm([(da_dn, w_bdn)], "nt", F32, "mm_dy_dn", 1024, DN_WIDTH, b_blocks=True)
    dy_swa = _mm([(da_swa, w_bswa)], "nt", F32, "mm_dy_swa", 1024, SWA_WIDTH, b_blocks=True)
    g_w_bdn = _mm([(y_dn, da_dn)], "tn", BF16, "mm_dw_branch_dn", DN_WIDTH, 512, out_blocks=True)
    g_w_bswa = _mm([(y_swa, da_swa)], "tn", BF16, "mm_dw_branch_swa", SWA_WIDTH, 512, out_blocks=True)
    dproj, dsk, dsv, g_q_norm, g_k_norm, g_sinks, dbias = _swa_bwd(proj, swa_q_norm, swa_k_norm, swa_sinks, bias,
                                                                   dy_swa, dproj)
    dproj = _kv_into(dproj, dsk, dsv)
    g_rel_bias = _bias_bwd(dbias)[:, :REL_BUCKETS].T
    mix_handle, mix_token = _exchange_start([g_w_out, g_w_bdn, g_w_bswa], True, "scatter_mix_start")
    do, dproj, g_out_norm = _dn_out_bwd(o, proj, dn_out_norm + mix_token[0, 0], dy_dn, dproj)
    du, dw, dqe, dkd, dqk, degl = _dn_scan_bwd(u, w, qe, kd, qk, egl, states, do)
    dqkvn, dgd, dbeta = _dn_prep_bwd(qkvn, g, beta, tinv, du, dw, dqe, dkd, dqk, degl)
    dproj, dal, ddt = _dn_gate_bwd(proj, dn_a_log, dn_dt_bias, dbeta, dgd, dproj)
    g_a_log = dal.reshape(DN_HEADS, DN_DIM).sum(axis=1)
    g_dt_bias = ddt[0, DN_HEADS:2 * DN_HEADS]
    dproj, g_conv = _dn_conv_bwd(proj, conv_w, dqkvn, dproj)
    g_w_in = _w_in_to_blocks(_mm([(h, dproj)], "tn", BF16, "mm_dw_in", 512, 1664, j_outer=True))
    in_handle, in_token = _exchange_start([g_w_in], True, "scatter_in_start")
    dx, g_attn_norm = _mm_fused(
        [(dproj, w_pad)], "nt", "mm_dh_norm", 512, D_MODEL, lambda p, e: norm_bwd(p, e)[1:],
        [(xs, 0), (dx1, 0), (attn_norm + in_token[0, 0], None)], (F32,), sum_shape=(1, D_MODEL))

    g_small = {"attn_norm": g_attn_norm, "ffn_norm": g_ffn_norm, "rel_bias": g_rel_bias, "dn_out_norm": g_out_norm,
               "swa_q_norm": g_q_norm, "swa_k_norm": g_k_norm, "dn_a_log": g_a_log, "dn_dt_bias": g_dt_bias,
               "swa_sinks": g_sinks}
    me = 4 * lax.axis_index("x") + 2 * lax.axis_index("y") + lax.axis_index("c")
    outs = {}

    def finish(handle, group, name, after):
        srcs, lands = _exchange_wait(handle, after, True, name)
        for n, src, land in zip(group, srcs, lands):
            parts = _own_slot(land, lax.dynamic_index_in_dim(src, me, 0, keepdims=False))
            outs[n] = _adam_update(parts, args[n], args["m_" + n], args["v_" + n], "adam_" + n)

    finish(ffn_handle, ("w_down", "w_gate", "w_up"), "scatter_ffn_wait", dx)
    finish(mix_handle, ("w_out", "w_branch_dn", "w_branch_swa"), "scatter_mix_wait", dx)
    sheets, conv_all = _all_gather_direct([_small_pack(g_small, loss_local), _pad_to(g_conv, (8, DN_QKV))],
                                          "all_gather_small",
                                          after=outs["w_up"][0])
    finish(in_handle, ("w_in",), "scatter_in_wait", sheets)
    conv_parts = lax.dynamic_slice(conv_all, (0, 0, me * CONV_SHARD[1]), (N_DEV,) + CONV_SHARD)
    outs["dn_conv"] = _adam_update(conv_parts, dn_conv, m_dn_conv, v_dn_conv, "adam_dn_conv")
    small_outs, loss = _small_update(sheets, {n: args[n] for n in SMALL}, {n: args["m_" + n] for n in SMALL},
                                     {n: args["v_" + n] for n in SMALL})
    outs.update(small_outs)

    names = ("attn_norm", "w_in", "dn_conv", "dn_a_log", "dn_dt_bias", "dn_out_norm", "swa_q_norm", "swa_k_norm",
             "swa_sinks", "rel_bias", "w_branch_dn", "w_branch_swa", "w_out", "ffn_norm", "w_gate", "w_up", "w_down")
    results = []
    for kind in range(4):
        results += [outs[n][kind].reshape(args[n].shape) for n in names]

    return (loss.reshape(()), dx.reshape(x.shape), *results)
```

```python
import math

import numpy as np
import jax
import jax.numpy as jnp
from jax import lax
from jax.experimental import pallas as pl
from jax.experimental.pallas import tpu as pltpu

F32 = jnp.float32
BF16 = jnp.bfloat16
HI = lax.Precision.HIGHEST

D_MODEL = 1024
DN_HEADS = 4
DN_DIM = 128
DN_WIDTH = 512
DN_QKV = 1536
DN_CONV = 4
CHUNK = 64
SWA_HEADS = 8
SWA_KV = 2
SWA_GROUP = 4
SWA_DIM = 64
SWA_WIDTH = 512
SWA_KVW = 128
WINDOW = 128
BLOCK = 128
REL_BUCKETS = 32
REL_MAX_DIST = 128
D_FF = 2816
D_IN = 4872
EPS = 1e-6
N_DEV = 8

ADAM_LR = 0.001
ADAM_B1 = 0.9
ADAM_B2 = 0.999
ADAM_EPS = 1e-08
ADAM_WD = 0.01
ADAM_STEP = 10

P_GATE, P_QKV, P_Z, P_SQ, P_SK, P_SV, P_BA = 0, 2048, 3584, 4096, 4608, 4736, 4864
P_WIDTH = 4992
R_QKV, R_Z, R_B, R_A, R_SQ, R_SK, R_SV, R_GATE = 0, 1536, 2048, 2052, 2056, 2568, 2696, 2824

VMEM_LIMIT = 56 * 1024 * 1024
LANES = 128
MESH_ID = pl.DeviceIdType.MESH


def _params(sem=None):
    return pltpu.CompilerParams(dimension_semantics=sem, vmem_limit_bytes=VMEM_LIMIT)


def _pick(dim, target):
    if dim <= target:
        return dim
    t = target - target % LANES
    while t >= LANES:
        if dim % t == 0:
            return t
        t -= LANES
    return dim


_DIMS = {"nn": (((1,), (0,)), ((), ())), "nt": (((1,), (1,)), ((), ())), "tn": (((0,), (0,)), ((), ()))}


def _tile_product(a_ref, b_ref, mode, b_blocks):
    a = a_ref[...].astype(BF16)
    b = jnp.concatenate([b_ref[d] for d in range(b_ref.shape[0])], axis=1) if b_blocks else b_ref[...]
    return lax.dot_general(a, b.astype(BF16), _DIMS[mode], preferred_element_type=F32)


def _mm(pairs, mode, out_dtype, name, bm, bn, j_outer=False, b_blocks=False, out_blocks=False):
    a0, b0 = pairs[0]
    cb = b0.shape[2] if b_blocks else None
    b_shape = (b0.shape[1], N_DEV * cb) if b_blocks else b0.shape
    if mode == "nn":
        (M, K), (K2, N) = a0.shape, b_shape
    elif mode == "nt":
        (M, K), (N, K2) = a0.shape, b_shape
    else:
        (K, M), (K2, N) = a0.shape, b_shape
    bm, bn = min(bm, M), min(bn, N)
    assert K == K2 and M % bm == 0 and N % bn == 0, (name, a0.shape, b0.shape, bm, bn)
    co = N // N_DEV
    assert not out_blocks or bn % co == 0
    dims = _DIMS[mode]
    n = len(pairs)

    def body(*refs):
        o_ref = refs[2 * n]
        acc = None
        for t in range(n):
            p = _tile_product(refs[2 * t], refs[2 * t + 1], mode, b_blocks)
            acc = p if acc is None else acc + p
        if out_blocks:
            for d in range(bn // co):
                o_ref[d] = acc[:, d * co:(d + 1) * co].astype(out_dtype)
        else:
            o_ref[...] = acc.astype(out_dtype)

    def ij(f):
        return (lambda j, i: f(i, j)) if j_outer else f

    a_spec = pl.BlockSpec((K, bm), ij(lambda i, j: (0, i))) if mode == "tn" else pl.BlockSpec((bm, K), ij(lambda i, j: (i, 0)))
    if b_blocks and mode == "nt":
        b_spec = pl.BlockSpec((N_DEV, bn, cb), ij(lambda i, j: (0, j, 0)))
    elif b_blocks:
        b_spec = pl.BlockSpec((bn // cb, K, cb), ij(lambda i, j: (j, 0, 0)))
    elif mode == "nt":
        b_spec = pl.BlockSpec((bn, K), ij(lambda i, j: (j, 0)))
    else:
        b_spec = pl.BlockSpec((K, bn), ij(lambda i, j: (0, j)))
    if out_blocks:
        out_spec = pl.BlockSpec((bn // co, bm, co), ij(lambda i, j: (j, i, 0)))
        out_shape = jax.ShapeDtypeStruct((N_DEV, M, co), out_dtype)
    else:
        out_spec = pl.BlockSpec((bm, bn), ij(lambda i, j: (i, j)))
        out_shape = jax.ShapeDtypeStruct((M, N), out_dtype)
    grid = (N // bn, M // bm) if j_outer else (M // bm, N // bn)
    return pl.pallas_call(
        body, grid=grid, in_specs=[a_spec, b_spec] * n, out_specs=out_spec, out_shape=out_shape, name=name,
        compiler_params=_params(("parallel", "parallel")),
    )(*[x for pair in pairs for x in pair])


def _mm_fused(pairs, mode, name, bm, bn, epilogue, extras, out_dtypes, j_outer=False, b_blocks=False,
              sum_shape=None, wide_first=None):
    a0, b0 = pairs[0]
    cb = b0.shape[2] if b_blocks else None
    b_shape = (b0.shape[1], N_DEV * cb) if b_blocks else b0.shape
    if mode == "nn":
        (M, K), (K2, N) = a0.shape, b_shape
    else:
        (M, K), (N, K2) = a0.shape, b_shape
    bm, bn = min(bm, M), min(bn, N)
    assert mode in ("nn", "nt") and K == K2 and M % bm == 0 and N % bn == 0, (name, a0.shape, b0.shape)
    dims = _DIMS[mode]
    n, ne, no = len(pairs), len(extras), len(out_dtypes)

    def body(*refs):
        prods = [_tile_product(refs[2 * t], refs[2 * t + 1], mode, b_blocks) for t in range(n)]
        results = epilogue(prods, [r[...] for r in refs[2 * n:2 * n + ne]])
        out_refs = refs[2 * n + ne:]
        for o_ref, val, dt in zip(out_refs, results, out_dtypes):
            o_ref[...] = val.astype(dt)
        if sum_shape is not None:
            s_ref = out_refs[no]

            @pl.when((pl.program_id(0) == 0) & (pl.program_id(1) == 0))
            def _():
                s_ref[...] = jnp.zeros_like(s_ref)

            s_ref[...] += results[no]

    def ij(f):
        return (lambda j, i: f(i, j)) if j_outer else f

    a_spec = pl.BlockSpec((bm, K), ij(lambda i, j: (i, 0)))
    once = dict(pipeline_mode=pl.Buffered(1)) if bn == N else {}
    if b_blocks and mode == "nt":
        b_spec = pl.BlockSpec((N_DEV, bn, cb), ij(lambda i, j: (0, j, 0)), **once)
    elif b_blocks:
        b_spec = pl.BlockSpec((bn // cb, K, cb), ij(lambda i, j: (j, 0, 0)), **once)
    elif mode == "nt":
        b_spec = pl.BlockSpec((bn, K), ij(lambda i, j: (j, 0)), **once)
    else:
        b_spec = pl.BlockSpec((K, bn), ij(lambda i, j: (0, j)), **once)
    e_specs = [pl.BlockSpec((1, bn), ij(lambda i, j: (0, j))) if first is None
               else pl.BlockSpec((bm, bn), ij(lambda i, j, first=first: (i, first + j))) for _, first in extras]
    tile = pl.BlockSpec((bm, bn), ij(lambda i, j: (i, j)))
    out_specs = [tile] * no
    out_shape = [jax.ShapeDtypeStruct((M, N), dt) for dt in out_dtypes]
    if wide_first is not None:
        assert bn == N
        out_specs[0] = pl.BlockSpec((bm, wide_first[1]), ij(lambda i, j: (i, 0)))
        out_shape[0] = jax.ShapeDtypeStruct((M, wide_first[0]), out_dtypes[0])
    if sum_shape is not None:
        assert sum_shape[1] in (1, bn) and (sum_shape[1] == 1 or bn == N)
        out_specs.append(_full(sum_shape))
        out_shape.append(jax.ShapeDtypeStruct(sum_shape, F32))
    grid = (N // bn, M // bm) if j_outer else (M // bm, N // bn)
    sem = ("arbitrary", "arbitrary") if sum_shape is not None else ("parallel", "parallel")
    return pl.pallas_call(
        body, grid=grid, in_specs=[a_spec, b_spec] * n + e_specs, out_specs=out_specs, out_shape=out_shape,
        name=name, compiler_params=_params(sem),
    )(*[x for pair in pairs for x in pair], *[arr for arr, _ in extras])


def _rms(x, gain):
    return x * lax.rsqrt(jnp.mean(x * x, axis=-1, keepdims=True) + EPS) * gain


def _silu(x):
    return x * jax.nn.sigmoid(x)


def _act(g, u):
    return _silu(g) * u


def _merge(g0, g1, a_dn, a_swa):
    return jax.nn.sigmoid(g0) * a_dn + jax.nn.sigmoid(g1) * a_swa


def _dn_post(c, is_v, q_scale):
    a = _silu(c)
    rs = lax.rsqrt(jnp.sum(a * a, axis=-1, keepdims=True) + EPS) * q_scale
    return a * jnp.where(is_v, 1.0, rs)


def _dn_out(o, z, gain):
    return _rms(o, gain) * _silu(z)


def _dot(a, b, dims=_DIMS["nn"], hi=False):
    if a.ndim == 3 or b.ndim == 3:
        batch = a.shape[0] if a.ndim == 3 else b.shape[0]
        a = a if a.ndim == 3 else jnp.broadcast_to(a, (batch,) + a.shape)
        b = b if b.ndim == 3 else jnp.broadcast_to(b, (batch,) + b.shape)
        ((ca,), (cb,)), _ = dims
        dims = (((ca + 1,), (cb + 1,)), ((0,), (0,)))
    if hi:
        return lax.dot_general(a, b, dims, precision=HI, preferred_element_type=F32)
    return lax.dot_general(a.astype(BF16), b.astype(BF16), dims, preferred_element_type=F32)


def _pieces(x):
    hi = x.astype(BF16)
    r1 = x - hi.astype(F32)
    mid = r1.astype(BF16)
    return hi, mid, (r1 - mid.astype(F32)).astype(BF16)


def _sel_left_impl(m, x):
    mb = m.astype(BF16)
    hi, mid, lo = _pieces(x)
    return _dot(mb, hi) + (_dot(mb, mid) + _dot(mb, lo))


@jax.custom_vjp
def _sel_left(m, mt, x):
    return _sel_left_impl(m, x)


_sel_left.defvjp(lambda m, mt, x: (_sel_left_impl(m, x), (m, mt)),
                 lambda res, ct: (jnp.zeros_like(res[0]), jnp.zeros_like(res[1]), _sel_left_impl(res[1], ct)))


def _sel_right_impl(x, s):
    sb = s.astype(BF16)
    hi, mid, lo = _pieces(x)
    return _dot(hi, sb) + (_dot(mid, sb) + _dot(lo, sb))


@jax.custom_vjp
def _sel_right(x, s, st):
    return _sel_right_impl(x, s)


_sel_right.defvjp(lambda x, s, st: (_sel_right_impl(x, s), (s, st)),
                  lambda res, ct: (_sel_right_impl(ct, res[1]), jnp.zeros_like(res[0]), jnp.zeros_like(res[1])))


def _dot3_impl(a, b):
    a_hi, a_lo, _ = _pieces(a)
    b_hi, b_lo, _ = _pieces(b)
    return _dot(a_hi, b_hi) + (_dot(a_hi, b_lo) + _dot(a_lo, b_hi))


@jax.custom_vjp
def _dot3(a, b):
    return _dot3_impl(a, b)


_dot3.defvjp(lambda a, b: (_dot3_impl(a, b), (a, b)),
             lambda res, ct: (_dot(ct, res[1], _DIMS["nt"]), _dot(res[0], ct, _DIMS["tn"])))


def _inv_impl(a, eye, strict):
    t = eye - a
    p = _dot(a, a)
    for level in range(5):
        t = t + _dot(t, p)
        if level < 4:
            p = _dot(p, p)
    t = t + _dot(t, eye - t - _dot3_impl(a, t))
    return jnp.where(strict > 0.5, t, eye)


@jax.custom_vjp
def _inv_given(a, t):
    return t.astype(F32)


_inv_given.defvjp(lambda a, t: (t.astype(F32), t),
                  lambda t, ct: (-_dot(_dot(t, ct, _DIMS["tn"]), t, _DIMS["nt"]), jnp.zeros_like(t)))


@jax.custom_vjp
def _lanes_join(a, b):
    return jnp.concatenate([a, b], axis=-1)


_lanes_join.defvjp(lambda a, b: (jnp.concatenate([a, b], axis=-1), None),
                   lambda _, ct: (ct[..., :ct.shape[-1] // 2], ct[..., ct.shape[-1] // 2:]))


@jax.custom_vjp
def _lanes_halves(y):
    h = y.shape[-1] // 2
    return y[..., :h], y[..., h:]


_lanes_halves.defvjp(lambda y: ((y[..., :y.shape[-1] // 2], y[..., y.shape[-1] // 2:]), None),
                     lambda _, ct: (jnp.concatenate(ct, axis=-1),))

GROUP = 4
GROUP_ROWS = GROUP * CHUNK


def _block_consts(n):
    ii = lax.broadcasted_iota(jnp.int32, (n, n), 0)
    jj = lax.broadcasted_iota(jnp.int32, (n, n), 1)
    shift = CHUNK.bit_length() - 1
    same = jnp.right_shift(ii, shift) == jnp.right_shift(jj, shift)
    return same & (ii >= jj), same & (ii <= jj), same & (ii > jj), same, ii == jj


def _lane0(n):
    s = (lax.broadcasted_iota(jnp.int32, (LANES, n), 0) == 0).astype(F32)
    st = (lax.broadcasted_iota(jnp.int32, (n, LANES), 1) == 0).astype(F32)
    return s, st


def _dn_group(q, k, v, g, beta, t_saved=None):
    n = GROUP_ROWS
    low_b, upp_b, strict_b, _, eye_b = _block_consts(n)
    low, upp, eye = low_b.astype(F32), upp_b.astype(F32), eye_b.astype(F32)
    gc = _sel_left(low, upp, g)
    per_chunk = (g.shape[0], GROUP, CHUNK, LANES)
    g_last = jnp.sum(g.reshape(per_chunk), axis=2, keepdims=True)
    gl = jnp.broadcast_to(g_last, per_chunk).reshape(g.shape)
    s, st = _lane0(n)
    col = _sel_right(gc, s, st)
    row = jnp.swapaxes(col, 1, 2)
    decay = jnp.exp(jnp.where(low_b, col - row, -jnp.inf))
    kb = k * beta
    vb = v * beta
    a = jnp.where(strict_b, _dot(kb, k, _DIMS["nt"]) * decay, 0.0)
    t = _inv_impl(a, eye, strict_b.astype(F32)) if t_saved is None else _inv_given(a, t_saved)
    u, w = _lanes_halves(_dot3(t, _lanes_join(vb, kb * jnp.exp(gc))))
    fold = (jnp.bitwise_and(lax.broadcasted_iota(jnp.int32, (n, CHUNK), 0), CHUNK - 1)
            == lax.broadcasted_iota(jnp.int32, (n, CHUNK), 1)).astype(F32)
    fold_t = (jnp.bitwise_and(lax.broadcasted_iota(jnp.int32, (CHUNK, n), 1), CHUNK - 1)
              == lax.broadcasted_iota(jnp.int32, (CHUNK, n), 0)).astype(F32)
    qk = _sel_right(_dot(q, k, _DIMS["nt"]) * decay, fold, fold_t)
    return u, w, q * jnp.exp(gc), k * jnp.exp(gl - gc), qk, jnp.exp(g_last), t


def _dn_step(s, u, w, qe, kd, qk, egl):
    v_new = u - _dot(w, s)
    o = _dot(qe, s) + _dot(qk, v_new)
    s_new = s * egl + _dot(kd, v_new, _DIMS["tn"])
    return s_new, o


def _swa_block(q, kband, vband, qg, kg, sinks, band):
    kn = _rms(kband, kg)
    qn = _rms(q, qg) * (SWA_DIM ** -0.5)
    logits = _dot(qn, kn, _DIMS["nt"]) + band
    m = lax.stop_gradient(jnp.maximum(jnp.max(logits, axis=-1, keepdims=True), sinks))
    p = jnp.exp(logits - m)
    denom = jnp.sum(p, axis=-1, keepdims=True) + jnp.exp(sinks - m)
    return _dot(p * (1.0 / denom), vband)


def _adamw(w, g, m, v):
    m = ADAM_B1 * m + (1.0 - ADAM_B1) * g
    v = ADAM_B2 * v + (1.0 - ADAM_B2) * jnp.square(g)
    m_hat = m / (1.0 - ADAM_B1 ** ADAM_STEP)
    v_hat = v / (1.0 - ADAM_B2 ** ADAM_STEP)
    delta = -ADAM_LR * (m_hat / (jnp.sqrt(v_hat) + ADAM_EPS) + ADAM_WD * w)
    return delta, m, v


def _row(tm, c, cb=0):
    return pl.BlockSpec((tm, c), lambda i, cb=cb: (i, cb))


def _full(shape):
    nd = len(shape)
    return pl.BlockSpec(shape, lambda *_, nd=nd: (0,) * nd)


def _norm_fwd(x, gain, name, tm=1024):
    S = x.shape[0]

    def body(x_ref, g_ref, h_ref):
        h_ref[...] = _rms(x_ref[...], g_ref[...]).astype(BF16)

    return pl.pallas_call(
        body, grid=(S // tm,), in_specs=[_row(tm, D_MODEL), _full((1, D_MODEL))],
        out_specs=_row(tm, D_MODEL), out_shape=jax.ShapeDtypeStruct((S, D_MODEL), BF16),
        name=name, compiler_params=_params(("parallel",)))(x, gain)


def _shift_down(x, s):
    row = lax.broadcasted_iota(jnp.int32, x.shape, 0)
    return jnp.where(row >= s, pltpu.roll(x, s, axis=0), 0.0)


def _shift_up(x, s):
    n = x.shape[0]
    row = lax.broadcasted_iota(jnp.int32, x.shape, 0)
    return jnp.where(row < n - s, pltpu.roll(x, n - s, axis=0), 0.0)


def _conv(x, w):
    out = w[DN_CONV - 1:DN_CONV] * x
    for s in range(1, DN_CONV):
        out = out + w[DN_CONV - 1 - s:DN_CONV - s] * _shift_down(x, s)
    return out


def _dn_conv_fwd(proj, conv_w):
    S = proj.shape[0]
    nb = DN_QKV // LANES

    def body(x_ref, w_ref, o_ref):
        j = pl.program_id(0)
        q_scale = jnp.where(j < DN_HEADS, DN_DIM ** -0.5, 1.0).astype(F32)
        o_ref[...] = _dn_post(_conv(x_ref[...], w_ref[...]), j >= 2 * DN_HEADS, q_scale)

    return pl.pallas_call(
        body, grid=(nb,),
        in_specs=[pl.BlockSpec((S, LANES), lambda j: (0, P_QKV // LANES + j)),
                  pl.BlockSpec((DN_CONV, LANES), lambda j: (0, j))],
        out_specs=pl.BlockSpec((S, LANES), lambda j: (0, j)),
        out_shape=jax.ShapeDtypeStruct((S, DN_QKV), F32), name="dn_conv_fwd",
        compiler_params=_params(("parallel",)))(proj, conv_w)


def _dn_conv_bwd(proj, conv_w, dqkvn, dproj):
    S = proj.shape[0]
    nb = DN_QKV // LANES

    def body(x_ref, w_ref, d_ref, _, dx_ref, dw_ref):
        j = pl.program_id(0)
        q_scale = jnp.where(j < DN_HEADS, DN_DIM ** -0.5, 1.0).astype(F32)
        x = x_ref[...]
        w = w_ref[...]
        _, vjp = jax.vjp(lambda c: _dn_post(c, j >= 2 * DN_HEADS, q_scale), _conv(x, w))
        (dc,) = vjp(d_ref[0])
        dx = w[DN_CONV - 1:DN_CONV] * dc
        dw_ref[DN_CONV - 1:DN_CONV, :] = jnp.sum(dc * x, axis=0, keepdims=True)
        for s in range(1, DN_CONV):
            dx = dx + w[DN_CONV - 1 - s:DN_CONV - s] * _shift_up(dc, s)
            dw_ref[DN_CONV - 1 - s:DN_CONV - s, :] = jnp.sum(dc * _shift_down(x, s), axis=0, keepdims=True)
        dx_ref[...] = dx.astype(BF16)

    return pl.pallas_call(
        body, grid=(nb,),
        in_specs=[pl.BlockSpec((S, LANES), lambda j: (0, P_QKV // LANES + j)),
                  pl.BlockSpec((DN_CONV, LANES), lambda j: (0, j)),
                  pl.BlockSpec((1, S, LANES), lambda j: (lax.div(j, DN_HEADS), 0, lax.rem(j, DN_HEADS))),
                  pl.BlockSpec(memory_space=pl.ANY)],
        out_specs=[pl.BlockSpec((S, LANES), lambda j: (0, P_QKV // LANES + j)),
                   pl.BlockSpec((DN_CONV, LANES), lambda j: (0, j))],
        out_shape=[jax.ShapeDtypeStruct(dproj.shape, dproj.dtype), jax.ShapeDtypeStruct((DN_CONV, DN_QKV), F32)],
        input_output_aliases={3: 0},
        name="dn_conv_bwd", compiler_params=_params(("parallel",)))(proj, conv_w, dqkvn, dproj)


def _expanders():
    eb = np.zeros((LANES, DN_WIDTH), np.float32)
    ea = np.zeros((LANES, DN_WIDTH), np.float32)
    for h in range(DN_HEADS):
        eb[h, h * DN_DIM:(h + 1) * DN_DIM] = 1.0
        ea[DN_HEADS + h, h * DN_DIM:(h + 1) * DN_DIM] = 1.0
    return jnp.asarray(eb), jnp.asarray(ea), jnp.asarray(eb.T), jnp.asarray(ea.T)


def _dn_gate_args(a_log, dt_bias):
    alog = jnp.repeat(a_log.reshape(1, DN_HEADS), DN_DIM, axis=1)
    dtb = _pad_to(jnp.pad(dt_bias.reshape(1, DN_HEADS), ((0, 0), (DN_HEADS, 0))), (1, LANES))
    return _expanders() + (alog, dtb)


def _dn_gate_specs(tm):
    return [_row(tm, LANES, P_BA // LANES), _full((LANES, DN_WIDTH)), _full((LANES, DN_WIDTH)),
            _full((DN_WIDTH, LANES)), _full((DN_WIDTH, LANES)), _full((1, DN_WIDTH)), _full((1, LANES))]


def _dn_gate_fn(ba, eb, ea, ebt, eat, alog, dtb):
    beta = _sel_right(jax.nn.sigmoid(ba), eb, ebt)
    g = -jnp.exp(alog) * _sel_right(jax.nn.softplus(ba + dtb), ea, eat)
    return beta, g


def _dn_gate_fwd(proj, a_log, dt_bias, tm=1024):
    S = proj.shape[0]
    args = _dn_gate_args(a_log, dt_bias)

    def body(ba_ref, eb_ref, ea_ref, ebt_ref, eat_ref, al_ref, dt_ref, beta_ref, g_ref):
        beta, g = _dn_gate_fn(ba_ref[...], eb_ref[...], ea_ref[...], ebt_ref[...], eat_ref[...], al_ref[...],
                              dt_ref[...])
        beta_ref[...] = beta
        g_ref[...] = g

    return pl.pallas_call(
        body, grid=(S // tm,), in_specs=_dn_gate_specs(tm), out_specs=[_row(tm, DN_WIDTH), _row(tm, DN_WIDTH)],
        out_shape=[jax.ShapeDtypeStruct((S, DN_WIDTH), F32), jax.ShapeDtypeStruct((S, DN_WIDTH), F32)],
        name="dn_gate_fwd", compiler_params=_params(("parallel",)))(proj, *args)


def _dn_gate_bwd(proj, a_log, dt_bias, dbeta, dg, dproj, tm=1024):
    S = proj.shape[0]
    args = _dn_gate_args(a_log, dt_bias)

    def body(ba_ref, eb_ref, ea_ref, ebt_ref, eat_ref, al_ref, dt_ref, dbeta_ref, dg_ref, _, dba_ref, dal_ref,
             ddt_ref):
        eb, ea, ebt, eat = eb_ref[...], ea_ref[...], ebt_ref[...], eat_ref[...]
        _, vjp = jax.vjp(lambda ba, al, dt: _dn_gate_fn(ba, eb, ea, ebt, eat, al, dt), ba_ref[...], al_ref[...],
                         dt_ref[...])
        dba, dal, ddt = vjp((dbeta_ref[...], dg_ref[...]))
        dba_ref[...] = dba.astype(BF16)

        @pl.when(pl.program_id(0) == 0)
        def _():
            dal_ref[...] = jnp.zeros_like(dal_ref)
            ddt_ref[...] = jnp.zeros_like(ddt_ref)

        dal_ref[...] += dal
        ddt_ref[...] += ddt

    return pl.pallas_call(
        body, grid=(S // tm,),
        in_specs=_dn_gate_specs(tm) + [_row(tm, DN_WIDTH), _row(tm, DN_WIDTH), pl.BlockSpec(memory_space=pl.ANY)],
        out_specs=[_row(tm, LANES, P_BA // LANES), _full((1, DN_WIDTH)), _full((1, LANES))],
        out_shape=[jax.ShapeDtypeStruct(dproj.shape, dproj.dtype), jax.ShapeDtypeStruct((1, DN_WIDTH), F32),
                   jax.ShapeDtypeStruct((1, LANES), F32)],
        input_output_aliases={len(args) + 3: 0},
        name="dn_gate_bwd", compiler_params=_params(("arbitrary",)))(proj, *args, dbeta, dg, dproj)


PREP_GROUPS = 4
PREP_CHUNKS = GROUP * PREP_GROUPS


def _dn_prep_specs():
    rows = PREP_CHUNKS * CHUNK
    q = pl.BlockSpec((rows, LANES), lambda h, c: (c, h))
    k = pl.BlockSpec((rows, LANES), lambda h, c: (c, DN_HEADS + h))
    v = pl.BlockSpec((rows, LANES), lambda h, c: (c, 2 * DN_HEADS + h))
    qk = pl.BlockSpec((1, rows, CHUNK), lambda h, c: (h, c, 0))
    egl = pl.BlockSpec((1, PREP_CHUNKS, 1, LANES), lambda h, c: (h, c, 0, 0))
    return q, k, v, qk, egl


def _dn_prep_fwd(qkvn, g, beta):
    S = qkvn.shape[0]
    nc = S // CHUNK
    q, k, v, qks, egl = _dn_prep_specs()

    def body(q_ref, k_ref, v_ref, g_ref, b_ref, u_ref, w_ref, qe_ref, kd_ref, qk_ref, egl_ref, t_ref):
        rows = PREP_CHUNKS * CHUNK
        grp = (PREP_GROUPS, GROUP_ROWS, LANES)
        u, w, qe, kd, qk, e, t = _dn_group(q_ref[...].reshape(grp), k_ref[...].reshape(grp), v_ref[...].reshape(grp),
                                           g_ref[...].reshape(grp), b_ref[...].reshape(grp))
        u_ref[...] = u.reshape(rows, LANES)
        w_ref[...] = w.reshape(rows, LANES)
        qe_ref[...] = qe.reshape(rows, LANES)
        kd_ref[...] = kd.reshape(rows, LANES)
        t_ref[0] = t.reshape(rows, GROUP_ROWS).astype(BF16)
        qk_ref[0] = qk.reshape(rows, CHUNK)
        egl_ref[0] = e.reshape(PREP_CHUNKS, 1, LANES)

    wide = jax.ShapeDtypeStruct((S, DN_WIDTH), F32)
    return pl.pallas_call(
        body, grid=(DN_HEADS, nc // PREP_CHUNKS), in_specs=[q, k, v, q, q],
        out_specs=[q, q, q, q, qks, egl, _dn_tinv_spec()],
        out_shape=[wide, wide, wide, wide, jax.ShapeDtypeStruct((DN_HEADS, S, CHUNK), F32),
                   jax.ShapeDtypeStruct((DN_HEADS, nc, 1, LANES), F32),
                   jax.ShapeDtypeStruct((DN_HEADS, S, GROUP_ROWS), BF16)],
        name="dn_prep_fwd", compiler_params=_params(("parallel", "parallel")))(qkvn, qkvn, qkvn, g, beta)


def _dn_tinv_spec():
    return pl.BlockSpec((1, PREP_CHUNKS * CHUNK, GROUP_ROWS), lambda h, c: (h, c, 0))


def _dn_prep_bwd(qkvn, g, beta, tinv, du, dw, dqe, dkd, dqk, degl):
    S = qkvn.shape[0]
    nc = S // CHUNK
    q, k, v, qks, egl = _dn_prep_specs()

    def body(q_ref, k_ref, v_ref, g_ref, b_ref, t_ref, du_ref, dw_ref, dqe_ref, dkd_ref, dqk_ref, degl_ref,
             dqkv_ref, dg_ref, db_ref):
        rows = PREP_CHUNKS * CHUNK
        grp = (PREP_GROUPS, GROUP_ROWS, LANES)
        t_saved = t_ref[0].reshape(PREP_GROUPS, GROUP_ROWS, GROUP_ROWS)
        _, vjp = jax.vjp(lambda *x: _dn_group(*x, t_saved=t_saved)[:6], q_ref[...].reshape(grp),
                         k_ref[...].reshape(grp), v_ref[...].reshape(grp), g_ref[...].reshape(grp),
                         b_ref[...].reshape(grp))
        dq, dk, dv, dg, db = vjp((du_ref[...].reshape(grp), dw_ref[...].reshape(grp), dqe_ref[...].reshape(grp),
                                  dkd_ref[...].reshape(grp), dqk_ref[0].reshape(PREP_GROUPS, GROUP_ROWS, CHUNK),
                                  degl_ref[0].reshape(PREP_GROUPS, GROUP, 1, LANES)))
        dqkv_ref[0] = dq.reshape(rows, LANES)
        dqkv_ref[1] = dk.reshape(rows, LANES)
        dqkv_ref[2] = dv.reshape(rows, LANES)
        dg_ref[...] = dg.reshape(rows, LANES)
        db_ref[...] = db.reshape(rows, LANES)

    wide = jax.ShapeDtypeStruct((S, DN_WIDTH), F32)
    rows = PREP_CHUNKS * CHUNK
    return pl.pallas_call(
        body, grid=(DN_HEADS, nc // PREP_CHUNKS), in_specs=[q, k, v, q, q, _dn_tinv_spec(), q, q, q, q, qks, egl],
        out_specs=[pl.BlockSpec((3, rows, LANES), lambda h, c: (0, c, h)), q, q],
        out_shape=[jax.ShapeDtypeStruct((3, S, DN_WIDTH), F32), wide, wide],
        name="dn_prep_bwd", compiler_params=_params(("parallel", "parallel")),
    )(qkvn, qkvn, qkvn, g, beta, tinv, du, dw, dqe, dkd, dqk, degl)


SCAN_CHUNKS = 8


def _dn_scan_specs(nc, reverse):
    nb = nc // SCAN_CHUNKS

    def cidx(c):
        return nb - 1 - c if reverse else c

    hc = pl.BlockSpec((SCAN_CHUNKS * CHUNK, DN_WIDTH), lambda c: (cidx(c), 0))
    qk = pl.BlockSpec((DN_HEADS, SCAN_CHUNKS * CHUNK, CHUNK), lambda c: (0, cidx(c), 0))
    egl = pl.BlockSpec((DN_HEADS, SCAN_CHUNKS, 1, LANES), lambda c: (0, cidx(c), 0, 0))
    st = pl.BlockSpec((DN_HEADS, SCAN_CHUNKS, DN_DIM, DN_DIM), lambda c: (0, cidx(c), 0, 0))
    return hc, qk, egl, st


def _heads(ref, i):
    return jnp.stack([ref[pl.ds(i * CHUNK, CHUNK), pl.ds(h * DN_DIM, DN_DIM)] for h in range(DN_HEADS)])


def _dn_scan_fwd(u, w, qe, kd, qk, egl):
    S = u.shape[0]
    nc = S // CHUNK
    hc, qks, egls, st = _dn_scan_specs(nc, False)

    def body(u_ref, w_ref, qe_ref, kd_ref, qk_ref, egl_ref, o_ref, st_ref, s_scr):
        @pl.when(pl.program_id(0) == 0)
        def _():
            s_scr[...] = jnp.zeros_like(s_scr)

        s = s_scr[...]
        for i in range(SCAN_CHUNKS):
            rows = pl.ds(i * CHUNK, CHUNK)
            st_ref[:, i] = s
            s, o = _dn_step(s, _heads(u_ref, i), _heads(w_ref, i), _heads(qe_ref, i), _heads(kd_ref, i),
                            qk_ref[:, rows, :], egl_ref[:, i])
            for h in range(DN_HEADS):
                o_ref[rows, pl.ds(h * DN_DIM, DN_DIM)] = o[h]
        s_scr[...] = s

    return pl.pallas_call(
        body, grid=(nc // SCAN_CHUNKS,), in_specs=[hc, hc, hc, hc, qks, egls], out_specs=[hc, st],
        out_shape=[jax.ShapeDtypeStruct((S, DN_WIDTH), F32), jax.ShapeDtypeStruct((DN_HEADS, nc, DN_DIM, DN_DIM), F32)],
        scratch_shapes=[pltpu.VMEM((DN_HEADS, DN_DIM, DN_DIM), F32)], name="dn_scan_fwd",
        compiler_params=_params(("arbitrary",)))(u, w, qe, kd, qk, egl)


def _dn_scan_bwd(u, w, qe, kd, qk, egl, states, do):
    S = u.shape[0]
    nc = S // CHUNK
    hc, qks, egls, st = _dn_scan_specs(nc, True)

    def body(u_ref, w_ref, qe_ref, kd_ref, qk_ref, egl_ref, st_ref, do_ref,
             du_ref, dw_ref, dqe_ref, dkd_ref, dqk_ref, degl_ref, ds_scr):
        @pl.when(pl.program_id(0) == 0)
        def _():
            ds_scr[...] = jnp.zeros_like(ds_scr)

        ds = ds_scr[...]
        for i in reversed(range(SCAN_CHUNKS)):
            rows = pl.ds(i * CHUNK, CHUNK)
            _, vjp = jax.vjp(_dn_step, st_ref[:, i], _heads(u_ref, i), _heads(w_ref, i), _heads(qe_ref, i),
                             _heads(kd_ref, i), qk_ref[:, rows, :], egl_ref[:, i])
            ds, du, dw, dqe, dkd, dqk, degl = vjp((ds, _heads(do_ref, i)))
            dqk_ref[:, rows, :] = dqk
            degl_ref[:, i] = degl
            for h in range(DN_HEADS):
                cols = pl.ds(h * DN_DIM, DN_DIM)
                du_ref[rows, cols] = du[h]
                dw_ref[rows, cols] = dw[h]
                dqe_ref[rows, cols] = dqe[h]
                dkd_ref[rows, cols] = dkd[h]
        ds_scr[...] = ds

    wide = jax.ShapeDtypeStruct((S, DN_WIDTH), F32)
    return pl.pallas_call(
        body, grid=(nc // SCAN_CHUNKS,), in_specs=[hc, hc, hc, hc, qks, egls, st, hc],
        out_specs=[hc, hc, hc, hc, qks, egls],
        out_shape=[wide, wide, wide, wide, jax.ShapeDtypeStruct((DN_HEADS, S, CHUNK), F32),
                   jax.ShapeDtypeStruct((DN_HEADS, nc, 1, LANES), F32)],
        scratch_shapes=[pltpu.VMEM((DN_HEADS, DN_DIM, DN_DIM), F32)], name="dn_scan_bwd",
        compiler_params=_params(("arbitrary",)))(u, w, qe, kd, qk, egl, states, do)


def _dn_out_fwd(o, proj, gain, tm=1024):
    S = o.shape[0]

    def body(o_ref, z_ref, g_ref, y_ref):
        y_ref[...] = _dn_out(o_ref[...], z_ref[...], g_ref[...]).astype(BF16)

    hs = pl.BlockSpec((tm, LANES), lambda i, h: (i, h))
    zs = pl.BlockSpec((tm, LANES), lambda i, h: (i, P_Z // LANES + h))
    return pl.pallas_call(
        body, grid=(S // tm, DN_HEADS), in_specs=[hs, zs, _full((1, DN_DIM))], out_specs=hs,
        out_shape=jax.ShapeDtypeStruct((S, DN_WIDTH), BF16), name="dn_out_fwd",
        compiler_params=_params(("parallel", "parallel")))(o, proj, gain)


_ANY = pl.BlockSpec(memory_space=pl.ANY)


def _dn_out_bwd(o, proj, gain, dy, dproj, tm=1024):
    S = o.shape[0]

    def body(o_ref, z_ref, g_ref, dy_ref, _, do_ref, dz_ref, dg_ref):
        _, vjp = jax.vjp(_dn_out, o_ref[...], z_ref[...], g_ref[...])
        do, dz, dg = vjp(dy_ref[...])
        do_ref[...] = do
        dz_ref[...] = dz.astype(BF16)

        @pl.when((pl.program_id(0) == 0) & (pl.program_id(1) == 0))
        def _():
            dg_ref[...] = jnp.zeros_like(dg_ref)

        dg_ref[...] += dg

    hs = pl.BlockSpec((tm, LANES), lambda i, h: (i, h))
    zs = pl.BlockSpec((tm, LANES), lambda i, h: (i, P_Z // LANES + h))
    return pl.pallas_call(
        body, grid=(S // tm, DN_HEADS), in_specs=[hs, zs, _full((1, DN_DIM)), hs, _ANY],
        out_specs=[hs, zs, _full((1, DN_DIM))],
        out_shape=[jax.ShapeDtypeStruct((S, DN_WIDTH), F32), jax.ShapeDtypeStruct(dproj.shape, dproj.dtype),
                   jax.ShapeDtypeStruct((1, DN_DIM), F32)],
        input_output_aliases={4: 1},
        name="dn_out_bwd", compiler_params=_params(("arbitrary", "arbitrary")))(o, proj, gain, dy, dproj)


def _rel_buckets():
    qi = np.arange(BLOCK)[:, None]
    kj = np.arange(2 * BLOCK)[None, :]
    n = np.maximum(BLOCK + qi - kj, 0)
    max_exact = REL_BUCKETS // 2
    nf = np.maximum(n, 1).astype(np.float32)
    large = max_exact + (np.log(nf / np.float32(max_exact)) / np.float32(math.log(REL_MAX_DIST / max_exact))
                         * np.float32(REL_BUCKETS - max_exact)).astype(np.int32)
    large = np.minimum(large, REL_BUCKETS - 1)
    return np.where(n < max_exact, n, large).astype(np.int32)


def _bias_fwd(rel_bias):
    buckets = jnp.asarray(_rel_buckets())

    def body(rb_ref, bk_ref, o_ref):
        bk = bk_ref[...]
        for h in range(SWA_HEADS):
            acc = jnp.zeros((BLOCK, 2 * BLOCK), F32)
            for b in range(REL_BUCKETS):
                acc = jnp.where(bk == b, rb_ref[b, h], acc)
            for first in range(2):
                o_ref[first, h] = jnp.where(_swa_mask(1 - first), acc, -jnp.inf)

    return pl.pallas_call(
        body, in_specs=[pl.BlockSpec(memory_space=pltpu.SMEM), pl.BlockSpec(memory_space=pltpu.VMEM)],
        out_specs=pl.BlockSpec(memory_space=pltpu.VMEM),
        out_shape=jax.ShapeDtypeStruct((2, SWA_HEADS, BLOCK, 2 * BLOCK), F32), name="swa_bias_fwd",
        compiler_params=_params())(rel_bias, buckets)


def _bias_bwd(dbias):
    buckets = jnp.asarray(_rel_buckets())

    def body(d_ref, bk_ref, o_ref):
        bk = bk_ref[...]
        lane = lax.broadcasted_iota(jnp.int32, (1, LANES), 1)
        for h in range(SWA_HEADS):
            d = d_ref[h]
            row = jnp.zeros((1, LANES), F32)
            for b in range(REL_BUCKETS):
                part = jnp.sum(jnp.where(bk == b, d, 0.0), axis=1, keepdims=True)
                row = jnp.where(lane == b, jnp.sum(part, axis=0, keepdims=True), row)
            o_ref[h:h + 1, :] = row

    return pl.pallas_call(
        body, in_specs=[pl.BlockSpec(memory_space=pltpu.VMEM), pl.BlockSpec(memory_space=pltpu.VMEM)],
        out_specs=pl.BlockSpec(memory_space=pltpu.VMEM),
        out_shape=jax.ShapeDtypeStruct((SWA_HEADS, LANES), F32), name="swa_bias_bwd",
        compiler_params=_params())(dbias, buckets)


def _swa_mask(n):
    qi = lax.broadcasted_iota(jnp.int32, (BLOCK, 2 * BLOCK), 0)
    kj = lax.broadcasted_iota(jnp.int32, (BLOCK, 2 * BLOCK), 1)
    dist = BLOCK + qi - kj
    return (dist >= 0) & (dist < WINDOW) & ((n > 0) | (kj >= BLOCK))


def _swa_in_specs():
    q = pl.BlockSpec((BLOCK, SWA_WIDTH), lambda n: (n, P_SQ // SWA_WIDTH))
    kc = pl.BlockSpec((BLOCK, SWA_KVW), lambda n: (n, P_SK // SWA_KVW))
    kp = pl.BlockSpec((BLOCK, SWA_KVW), lambda n: (jnp.maximum(n - 1, 0), P_SK // SWA_KVW))
    vc = pl.BlockSpec((BLOCK, SWA_KVW), lambda n: (n, P_SV // SWA_KVW))
    vp = pl.BlockSpec((BLOCK, SWA_KVW), lambda n: (jnp.maximum(n - 1, 0), P_SV // SWA_KVW))
    band = pl.BlockSpec((None, SWA_HEADS, BLOCK, 2 * BLOCK), lambda n: (jnp.where(n == 0, 1, 0), 0, 0, 0))
    small = [_full((1, SWA_DIM)), _full((1, SWA_DIM)), _full((1, SWA_HEADS)), band]
    return [q, kp, kc, vp, vc] + small


def _swa_load(q_ref, kp_ref, kc_ref, vp_ref, vc_ref, s_ref):
    q = jnp.stack([q_ref[:, pl.ds(h * SWA_DIM, SWA_DIM)] for h in range(SWA_HEADS)])
    kbands, vbands = [], []
    for kv in range(SWA_KV):
        cols = pl.ds(kv * SWA_DIM, SWA_DIM)
        kbands += [jnp.concatenate([kp_ref[:, cols], kc_ref[:, cols]], axis=0)] * SWA_GROUP
        vbands += [jnp.concatenate([vp_ref[:, cols], vc_ref[:, cols]], axis=0)] * SWA_GROUP
    sinks = jnp.stack([s_ref[:, pl.ds(h, 1)] for h in range(SWA_HEADS)])
    return q, jnp.stack(kbands), jnp.stack(vbands), sinks


def _swa_fwd(proj, q_gain, k_gain, sinks, bias):
    S = proj.shape[0]

    def body(q_ref, kp_ref, kc_ref, vp_ref, vc_ref, qg_ref, kg_ref, s_ref, bias_ref, y_ref):
        q, kband, vband, sk = _swa_load(q_ref, kp_ref, kc_ref, vp_ref, vc_ref, s_ref)
        out = _swa_block(q, kband, vband, qg_ref[...], kg_ref[...], sk, bias_ref[...])
        for h in range(SWA_HEADS):
            y_ref[:, pl.ds(h * SWA_DIM, SWA_DIM)] = out[h].astype(BF16)

    return pl.pallas_call(
        body, grid=(S // BLOCK,), in_specs=_swa_in_specs(),
        out_specs=pl.BlockSpec((BLOCK, SWA_WIDTH), lambda n: (n, 0)),
        out_shape=jax.ShapeDtypeStruct((S, SWA_WIDTH), BF16), name="swa_fwd",
        compiler_params=_params(("parallel",)))(proj, proj, proj, proj, proj, q_gain, k_gain, sinks, bias)


def _swa_bwd(proj, q_gain, k_gain, sinks, bias, dy, dproj):
    S = proj.shape[0]

    def body(q_ref, kp_ref, kc_ref, vp_ref, vc_ref, qg_ref, kg_ref, s_ref, bias_ref, dy_ref, _,
             dq_ref, dk_ref, dv_ref, dqg_ref, dkg_ref, ds_ref, dbias_ref):
        n = pl.program_id(0)

        @pl.when(n == 0)
        def _():
            for r in (dk_ref, dv_ref, dqg_ref, dkg_ref, ds_ref, dbias_ref):
                r[...] = jnp.zeros_like(r)

        cur = pl.ds(pl.multiple_of(n * BLOCK, BLOCK), BLOCK)
        prev = pl.ds(pl.multiple_of(jnp.maximum(n - 1, 0) * BLOCK, BLOCK), BLOCK)
        q, kband, vband, sk = _swa_load(q_ref, kp_ref, kc_ref, vp_ref, vc_ref, s_ref)
        _, vjp = jax.vjp(_swa_block, q, kband, vband, qg_ref[...], kg_ref[...], sk, bias_ref[...])
        dy = jnp.stack([dy_ref[:, pl.ds(h * SWA_DIM, SWA_DIM)] for h in range(SWA_HEADS)])
        dq, dkb, dvb, dqg, dkg, dsk, dbs = vjp(dy)
        for h in range(SWA_HEADS):
            dq_ref[:, pl.ds(h * SWA_DIM, SWA_DIM)] = dq[h].astype(BF16)
            ds_ref[:, pl.ds(h, 1)] += dsk[h]
        dbias_ref[...] += dbs
        dqg_ref[...] += dqg
        dkg_ref[...] += dkg
        for kv in range(SWA_KV):
            cols = pl.ds(kv * SWA_DIM, SWA_DIM)
            group = range(kv * SWA_GROUP, (kv + 1) * SWA_GROUP)
            dk_kv = sum(dkb[h] for h in group)
            dv_kv = sum(dvb[h] for h in group)
            dk_ref[cur, cols] += dk_kv[BLOCK:]
            dv_ref[cur, cols] += dv_kv[BLOCK:]

            @pl.when(n > 0)
            def _(cols=cols, dk_kv=dk_kv, dv_kv=dv_kv):
                dk_ref[prev, cols] += dk_kv[:BLOCK]
                dv_ref[prev, cols] += dv_kv[:BLOCK]

    return pl.pallas_call(
        body, grid=(S // BLOCK,),
        in_specs=_swa_in_specs() + [pl.BlockSpec((BLOCK, SWA_WIDTH), lambda n: (n, 0)),
                                    pl.BlockSpec(memory_space=pl.ANY)],
        out_specs=[pl.BlockSpec((BLOCK, SWA_WIDTH), lambda n: (n, P_SQ // SWA_WIDTH)), _full((S, SWA_KVW)),
                   _full((S, SWA_KVW)), _full((1, SWA_DIM)), _full((1, SWA_DIM)), _full((1, SWA_HEADS)),
                   _full((SWA_HEADS, BLOCK, 2 * BLOCK))],
        out_shape=[jax.ShapeDtypeStruct(dproj.shape, dproj.dtype), jax.ShapeDtypeStruct((S, SWA_KVW), F32),
                   jax.ShapeDtypeStruct((S, SWA_KVW), F32), jax.ShapeDtypeStruct((1, SWA_DIM), F32),
                   jax.ShapeDtypeStruct((1, SWA_DIM), F32), jax.ShapeDtypeStruct((1, SWA_HEADS), F32),
                   jax.ShapeDtypeStruct((SWA_HEADS, BLOCK, 2 * BLOCK), F32)],
        input_output_aliases={10: 0},
        name="swa_bwd", compiler_params=_params(("arbitrary",)),
    )(proj, proj, proj, proj, proj, q_gain, k_gain, sinks, bias, dy, dproj)


def _kv_into(dproj, dk, dv, tm=1024):
    S = dk.shape[0]

    def body(dk_ref, dv_ref, _, o_ref):
        o_ref[:, :SWA_KVW] = dk_ref[...].astype(BF16)
        o_ref[:, SWA_KVW:] = dv_ref[...].astype(BF16)

    return pl.pallas_call(
        body, grid=(S // tm,), in_specs=[_row(tm, SWA_KVW), _row(tm, SWA_KVW), pl.BlockSpec(memory_space=pl.ANY)],
        out_specs=_row(tm, 2 * SWA_KVW, P_SK // (2 * SWA_KVW)),
        out_shape=jax.ShapeDtypeStruct(dproj.shape, dproj.dtype), input_output_aliases={2: 0},
        name="swa_kv_into", compiler_params=_params(("parallel",)))(dk, dv, dproj)


def _position():
    return lax.axis_index("x"), lax.axis_index("y"), lax.axis_index("c")


def _all_gather(shards, name="all_gather_weights"):
    na = len(shards)

    def body(*refs):
        x_refs, out_refs = refs[:na], refs[na:2 * na]
        send_sems, recv_sems, local_sems = refs[2 * na:]
        x, y, c = _position()
        me, sibling = (x, y, c), (x, y, 1 - c)
        chips = [(1 - x, y), (x, 1 - y), (1 - x, 1 - y)]

        def copy(a, k, block, to, own=False):
            px, py, pc = block
            slot = out_refs[a].at[4 * px + 2 * py + pc]
            return pltpu.make_async_remote_copy(
                src_ref=x_refs[a] if own else slot, dst_ref=slot, send_sem=send_sems.at[7 * a + k],
                recv_sem=recv_sems.at[7 * a + k], device_id=to, device_id_type=MESH_ID)

        mine = [pltpu.make_async_copy(x_refs[a], out_refs[a].at[4 * x + 2 * y + c], local_sems.at[a])
                for a in range(na)]
        for cp in mine:
            cp.start()
        first = []
        for a in range(na):
            first.append(copy(a, 0, me, sibling, own=True))
            first += [copy(a, 1 + j, me, (*chip, c), own=True) for j, chip in enumerate(chips)]
        for cp in first:
            cp.start()
        passed = []
        for j, chip in enumerate(chips):
            for a in range(na):
                copy(a, 1 + j, (*chip, c), me).wait_recv()
                passed.append(copy(a, 4 + j, (*chip, c), sibling))
                passed[-1].start()
        for a in range(na):
            copy(a, 0, sibling, me).wait_recv()
            for j, chip in enumerate(chips):
                copy(a, 4 + j, (*chip, 1 - c), me).wait_recv()
        for cp in first + passed:
            cp.wait_send()
        for cp in mine:
            cp.wait()

    return pl.pallas_call(
        body, in_specs=[pl.BlockSpec(memory_space=pl.ANY)] * na, out_specs=[pl.BlockSpec(memory_space=pl.ANY)] * na,
        out_shape=[jax.ShapeDtypeStruct((N_DEV,) + s.shape, s.dtype) for s in shards],
        scratch_shapes=[pltpu.SemaphoreType.DMA((7 * na,)), pltpu.SemaphoreType.DMA((7 * na,)),
                        pltpu.SemaphoreType.DMA((na,))],
        name=name)(*shards)


_HBM = pl.BlockSpec(memory_space=pltpu.HBM)
_SEM = pl.BlockSpec(memory_space=pltpu.SEMAPHORE)
_DATAFLOW = pltpu.SideEffectType.DATAFLOW_SIDE_EFFECTING


def _peers(x, y, c):
    out = []
    for k in range(1, N_DEV):
        px, py, pc = x ^ (k >> 2), y ^ ((k >> 1) & 1), c ^ (k & 1)
        out.append(((px, py, pc), 4 * px + 2 * py + pc))
    return out


def _split_copies(src_refs, land_refs, send_sems, recv_sems, scatter):
    x, y, c = _position()
    me = 4 * x + 2 * y + c
    sends, recvs = [], []
    for k, (peer_id, peer) in enumerate(_peers(x, y, c)):
        for a, (src, land) in enumerate(zip(src_refs, land_refs)):
            sems = dict(send_sem=send_sems.at[7 * a + k], recv_sem=recv_sems.at[7 * a + k],
                        device_id=peer_id, device_id_type=MESH_ID)
            mine = src.at[peer] if scatter else src
            sends.append(pltpu.make_async_remote_copy(src_ref=mine, dst_ref=land.at[me], **sems))
            recvs.append(pltpu.make_async_remote_copy(src_ref=mine, dst_ref=land.at[peer], **sems))
    return sends, recvs


def _all_gather_direct(shards, name, after):
    na = len(shards)

    def body(*refs):
        x_refs, out_refs = refs[:na], refs[na + 1:2 * na + 1]
        send_sems, recv_sems, local_sems = refs[2 * na + 1:]
        x, y, c = _position()
        me = 4 * x + 2 * y + c
        local = [pltpu.make_async_copy(x_refs[a], out_refs[a].at[me], local_sems.at[a]) for a in range(na)]
        sends, recvs = _split_copies(x_refs, out_refs, send_sems, recv_sems, False)
        for cp in local + sends:
            cp.start()
        for cp in recvs:
            cp.wait_recv()
        for cp in sends:
            cp.wait_send()
        for cp in local:
            cp.wait()

    return pl.pallas_call(
        body, in_specs=[pl.BlockSpec(memory_space=pl.ANY)] * (na + 1),
        out_specs=[pl.BlockSpec(memory_space=pl.ANY)] * na,
        out_shape=[jax.ShapeDtypeStruct((N_DEV,) + s.shape, s.dtype) for s in shards],
        scratch_shapes=[pltpu.SemaphoreType.DMA((7 * na,)), pltpu.SemaphoreType.DMA((7 * na,)),
                        pltpu.SemaphoreType.DMA((na,))],
        name=name)(*shards, after)


def _exchange_start(srcs, scatter, name, after=None):
    na = len(srcs)
    lands = [lax.empty(s.shape if scatter else (N_DEV,) + s.shape, s.dtype) for s in srcs]
    extra = [] if after is None else [after]

    def body(*refs):
        src_refs, land_refs = refs[:na], refs[na:2 * na]
        send_sems, recv_sems = refs[2 * na + len(extra)], refs[2 * na + len(extra) + 1]
        token = refs[-1]
        sends, _ = _split_copies(src_refs, land_refs, send_sems, recv_sems, scatter)
        for cp in sends:
            cp.start()
        token[...] = jnp.zeros_like(token)

    hbm = lambda a: pltpu.HBM(a.shape, a.dtype)
    out = pl.pallas_call(
        body, name=name,
        out_shape=(pltpu.SemaphoreType.DMA((7 * na,)), pltpu.SemaphoreType.DMA((7 * na,)),
                   *[hbm(s) for s in srcs], *[hbm(l) for l in lands], jax.ShapeDtypeStruct((8, LANES), F32)),
        in_specs=[_HBM] * (2 * na) + [pl.BlockSpec(memory_space=pl.ANY)] * len(extra),
        out_specs=(_SEM, _SEM, *[_HBM] * (2 * na), pl.BlockSpec(memory_space=pltpu.VMEM)),
        input_output_aliases={i: 2 + i for i in range(2 * na)},
        compiler_params=pltpu.CompilerParams(has_side_effects=_DATAFLOW),
    )(*[pltpu.with_memory_space_constraint(s, pltpu.HBM) for s in srcs],
      *[pltpu.with_memory_space_constraint(l, pltpu.HBM) for l in lands], *extra)
    return (out[0], out[1], list(out[2:2 + na]), list(out[2 + na:2 + 2 * na])), out[-1]


def _exchange_wait(handle, after, scatter, name):
    send_sems, recv_sems, srcs, lands = handle
    na = len(srcs)

    def body(*refs):
        src_refs, land_refs = refs[:na], refs[na:2 * na]
        s_sems, r_sems = refs[2 * na], refs[2 * na + 1]
        sends, recvs = _split_copies(src_refs, land_refs, s_sems, r_sems, scatter)
        for cp in sends:
            cp.wait_send()
        for cp in recvs:
            cp.wait_recv()

    hbm = lambda a: pltpu.HBM(a.shape, a.dtype)
    out = pl.pallas_call(
        body, name=name, out_shape=(*[hbm(s) for s in srcs], *[hbm(l) for l in lands]),
        in_specs=[_HBM] * (2 * na) + [_SEM, _SEM, pl.BlockSpec(memory_space=pl.ANY)],
        out_specs=tuple([_HBM] * (2 * na)), input_output_aliases={i: i for i in range(2 * na)},
        compiler_params=pltpu.CompilerParams(has_side_effects=_DATAFLOW),
    )(*srcs, *lands, send_sems, recv_sems, after)
    return list(out[:na]), list(out[na:])


def _own_slot(landed, own):
    me = 4 * lax.axis_index("x") + 2 * lax.axis_index("y") + lax.axis_index("c")
    return lax.dynamic_update_slice_in_dim(landed, own[None], me, axis=0)


def _adam_update(parts, w, m, v, name, tr=256):
    _, r, c = w.shape
    tr = _pick_rows(r, tr)
    cp = parts.shape[2]

    def body(p_ref, w_ref, m_ref, v_ref, g_ref, d_ref, nm_ref, nv_ref):
        g = p_ref[0, :, pl.ds(0, c)].astype(F32)
        for i in range(1, N_DEV):
            g = g + p_ref[i, :, pl.ds(0, c)].astype(F32)
        delta, nm, nv = _adamw(w_ref[0], g, m_ref[0], v_ref[0])
        g_ref[0] = g
        d_ref[0] = delta
        nm_ref[0] = nm
        nv_ref[0] = nv

    rs = pl.BlockSpec((1, tr, c), lambda i: (0, i, 0))
    return pl.pallas_call(
        body, grid=(r // tr,), in_specs=[pl.BlockSpec((N_DEV, tr, cp), lambda i: (0, i, 0)), rs, rs, rs],
        out_specs=[rs] * 4, out_shape=[jax.ShapeDtypeStruct((1, r, c), F32)] * 4, name=name,
        compiler_params=_params(("parallel",)))(parts, w, m, v)


def _pick_rows(rows, target):
    if rows <= target:
        return rows
    t = target
    while t >= 16:
        if rows % t == 0:
            return t
        t -= 16
    return rows


BIG = ("w_in", "w_branch_dn", "w_branch_swa", "w_out", "w_gate", "w_up", "w_down")
IN_SHARD, IN_WIRE = D_IN // N_DEV, 640
FF_SHARD, FF_WIRE = D_FF // N_DEV, 384
D_FFP = N_DEV * FF_WIRE
BIG_SHAPES = {"w_in": ((D_MODEL, IN_SHARD), (D_MODEL, IN_WIRE)),
              "w_branch_dn": ((DN_WIDTH, LANES), (DN_WIDTH, LANES)),
              "w_branch_swa": ((SWA_WIDTH, LANES), (SWA_WIDTH, LANES)),
              "w_out": ((LANES, D_MODEL), (LANES, D_MODEL)),
              "w_gate": ((D_MODEL, FF_SHARD), (D_MODEL, FF_WIRE)),
              "w_up": ((D_MODEL, FF_SHARD), (D_MODEL, FF_WIRE)),
              "w_down": ((FF_SHARD, D_MODEL), (FF_WIRE, D_MODEL))}
CONV_SHARD, CONV_WIRE = (DN_CONV, DN_QKV // N_DEV), (8, 256)


def _pad_to(a, shape):
    return jnp.pad(a, [(0, t - s) for s, t in zip(a.shape, shape)])


_IN_SEGS = ((R_GATE, 2048, P_GATE), (R_QKV, DN_QKV, P_QKV), (R_Z, DN_WIDTH, P_Z), (R_SQ, SWA_WIDTH, P_SQ),
            (R_SK, SWA_KVW, P_SK), (R_SV, SWA_KVW, P_SV), (R_B, 8, P_BA))


def _w_in_from_blocks(blocks):
    parts = []
    for rs, n, _ in _IN_SEGS:
        for dev in range(N_DEV):
            lo, hi = max(rs, IN_SHARD * dev), min(rs + n, IN_SHARD * (dev + 1))
            if lo < hi:
                parts.append(blocks[dev, :, lo - IN_SHARD * dev:hi - IN_SHARD * dev])
    parts.append(jnp.zeros((blocks.shape[1], P_WIDTH - P_BA - 8), blocks.dtype))
    return jnp.concatenate(parts, axis=1)


def _w_in_to_blocks(g):
    out = []
    for dev in range(N_DEV):
        parts = []
        for rs, n, ps in sorted(_IN_SEGS):
            lo, hi = max(rs, IN_SHARD * dev), min(rs + n, IN_SHARD * (dev + 1))
            if lo < hi:
                parts.append(g[:, ps + lo - rs:ps + hi - rs])
        parts.append(jnp.zeros((g.shape[0], IN_WIRE - IN_SHARD), g.dtype))
        out.append(jnp.concatenate(parts, axis=1))
    return jnp.stack(out)


SMALL = {"attn_norm": (0, (1, D_MODEL)), "ffn_norm": (1, (1, D_MODEL)), "dn_out_norm": (2, (1, DN_DIM)),
         "swa_q_norm": (3, (1, SWA_DIM)), "swa_k_norm": (4, (1, SWA_DIM)), "dn_a_log": (5, (1, DN_HEADS)),
         "dn_dt_bias": (6, (1, DN_HEADS)), "swa_sinks": (7, (1, SWA_HEADS)), "rel_bias": (8, (REL_BUCKETS, SWA_HEADS))}
SMALL_SHEET = (48, D_MODEL)


LOSS_ROW = 40


def _small_pack(grads, loss_local):
    names = list(SMALL)

    def body(*refs):
        o_ref = refs[-1]
        o_ref[...] = jnp.zeros_like(o_ref)
        for n, ref in zip(names, refs):
            r0, (nr, nc) = SMALL[n]
            o_ref[r0:r0 + nr, 0:nc] = ref[...]
        o_ref[LOSS_ROW:LOSS_ROW + 1, 0:1] = refs[len(names)][...]

    return pl.pallas_call(
        body, in_specs=[pl.BlockSpec(memory_space=pltpu.VMEM)] * (len(names) + 1),
        out_specs=pl.BlockSpec(memory_space=pltpu.VMEM), out_shape=jax.ShapeDtypeStruct(SMALL_SHEET, F32),
        name="small_pack", compiler_params=_params())(*[grads[n].reshape(SMALL[n][1]) for n in names], loss_local)


def _small_update(sheets, w, m, v):
    names = list(SMALL)
    k = len(names)

    def body(*refs):
        p_ref = refs[0]
        ins, outs = refs[1:1 + 3 * k], refs[1 + 3 * k:]
        loss = p_ref[0, LOSS_ROW:LOSS_ROW + 1, 0:1]
        for i in range(1, N_DEV):
            loss = loss + p_ref[i, LOSS_ROW:LOSS_ROW + 1, 0:1]
        outs[4 * k][...] = loss
        for t, n in enumerate(names):
            r0, (nr, nc) = SMALL[n]
            g = p_ref[0, r0:r0 + nr, 0:nc]
            for i in range(1, N_DEV):
                g = g + p_ref[i, r0:r0 + nr, 0:nc]
            delta, nm, nv = _adamw(ins[t][...], g, ins[k + t][...], ins[2 * k + t][...])
            for kind, val in enumerate((g, delta, nm, nv)):
                outs[kind * k + t][...] = val

    shapes = [jax.ShapeDtypeStruct(SMALL[n][1], F32) for n in names]
    vm = pl.BlockSpec(memory_space=pltpu.VMEM)
    res = pl.pallas_call(
        body, in_specs=[vm] * (1 + 3 * k), out_specs=[vm] * (4 * k + 1),
        out_shape=shapes * 4 + [jax.ShapeDtypeStruct((1, 1), F32)], name="adam_small", compiler_params=_params(),
    )(sheets, *[d[n].reshape(SMALL[n][1]) for d in (w, m, v) for n in names])
    return {n: tuple(res[kind * k + t] for kind in range(4)) for t, n in enumerate(names)}, res[4 * k]


def kernel(x, attn_norm, w_in, dn_conv, dn_a_log, dn_dt_bias, dn_out_norm, swa_q_norm, swa_k_norm, swa_sinks, rel_bias, w_branch_dn, w_branch_swa, w_out, ffn_norm, w_gate, w_up, w_down, loss_target, m_attn_norm, m_w_in, m_dn_conv, m_dn_a_log, m_dn_dt_bias, m_dn_out_norm, m_swa_q_norm, m_swa_k_norm, m_swa_sinks, m_rel_bias, m_w_branch_dn, m_w_branch_swa, m_w_out, m_ffn_norm, m_w_gate, m_w_up, m_w_down, v_attn_norm, v_w_in, v_dn_conv, v_dn_a_log, v_dn_dt_bias, v_dn_out_norm, v_swa_q_norm, v_swa_k_norm, v_swa_sinks, v_rel_bias, v_w_branch_dn, v_w_branch_swa, v_w_out, v_ffn_norm, v_w_gate, v_w_up, v_w_down):
    args = dict(locals())
    S = x.shape[1]
    xs = x.reshape(S, D_MODEL)
    target = loss_target.reshape(S, D_MODEL)

    w_loc = {n: args[n].reshape(BIG_SHAPES[n][0]) for n in BIG}
    conv_loc = dn_conv.reshape(CONV_SHARD)
    wire = {n: _pad_to(w_loc[n], BIG_SHAPES[n][1]).astype(BF16) for n in BIG}
    first = _all_gather([wire["w_in"], _pad_to(conv_loc, CONV_WIRE)])
    later = [n for n in BIG if n != "w_in"]
    rest_handle, rest_token = _exchange_start([wire[n] for n in later], False, "gather_rest_start", after=first[1])
    w_pad = _w_in_from_blocks(first[0])
    conv_w = jnp.concatenate([first[1][d, :DN_CONV, :CONV_SHARD[1]] for d in range(N_DEV)], axis=1)

    h = _norm_fwd(xs, attn_norm + rest_token[0, 0], "norm1_fwd")
    proj = _mm([(h, w_pad)], "nn", F32, "mm_in", 1024, 1664, j_outer=True)
    qkvn = _dn_conv_fwd(proj, conv_w)
    beta, g = _dn_gate_fwd(proj, dn_a_log, dn_dt_bias)
    u, w, qe, kd, qk, egl, tinv = _dn_prep_fwd(qkvn, g, beta)
    o, states = _dn_scan_fwd(u, w, qe, kd, qk, egl)
    y_dn = _dn_out_fwd(o, proj, dn_out_norm)
    bias = _bias_fwd(rel_bias)
    y_swa = _swa_fwd(proj, swa_q_norm, swa_k_norm, swa_sinks, bias)
    rest_src, rest_land = _exchange_wait(rest_handle, y_swa, False, "gather_rest_wait")
    G = {n: _own_slot(land, src) for n, src, land in zip(later, rest_src, rest_land)}
    w_bdn, w_bswa, w_g, w_u = G["w_branch_dn"], G["w_branch_swa"], G["w_gate"], G["w_up"]
    w_o = G["w_out"].reshape(D_MODEL, D_MODEL)
    w_d = G["w_down"].reshape(D_FFP, D_MODEL)
    gates = [(proj, P_GATE // 512), (proj, (P_GATE + D_MODEL) // 512)]
    a_dn, a_swa, merged = _mm_fused(
        [(y_dn, w_bdn), (y_swa, w_bswa)], "nn", "mm_branch_merge", 1024, 512,
        lambda p, e: (p[0], p[1], _merge(e[0], e[1], p[0], p[1])), gates, (F32, F32, BF16), b_blocks=True)

    def resid_norm(p, e):
        x1 = e[0] + p[0]
        return x1, _rms(x1, e[1])

    x1, h2 = _mm_fused([(merged, w_o)], "nn", "mm_out_norm", 512, D_MODEL, resid_norm,
                       [(xs, 0), (ffn_norm, None)], (F32, BF16))
    gate, up, act = _mm_fused([(h2, w_g), (h2, w_u)], "nn", "mm_gate_up_act", 1024, 768,
                              lambda p, e: (p[0], p[1], _act(p[0], p[1])), [], (F32, F32, BF16),
                              j_outer=True, b_blocks=True)

    def loss_head(p, e):
        diff = e[0] + p[0] - e[1]
        dy = diff * (1.0 / D_MODEL)
        part = jnp.sum(jnp.mean(diff * diff, axis=-1, keepdims=True), axis=0, keepdims=True) * 0.5
        return dy, dy, part

    dy, dy_b, loss_local = _mm_fused([(act, w_d)], "nn", "mm_down_loss", 512, D_MODEL, loss_head,
                                     [(x1, 0), (target, 0)], (F32, BF16), sum_shape=(1, 1))

    def act_bwd(p, e):
        _, vjp = jax.vjp(_act, e[0], e[1])
        return vjp(p[0])

    dgate, dup = _mm_fused([(dy_b, w_d)], "nt", "mm_dact_act", 1024, 768, act_bwd, [(gate, 0), (up, 0)],
                           (BF16, BF16), j_outer=True)
    g_w_down = _mm([(act, dy_b)], "tn", BF16, "mm_dw_down", 768, D_MODEL, j_outer=True)
    g_w_down = g_w_down.reshape(N_DEV, FF_WIRE, D_MODEL)
    g_w_gate = _mm([(h2, dgate)], "tn", BF16, "mm_dw_gate", D_MODEL, 768, out_blocks=True)
    g_w_up = _mm([(h2, dup)], "tn", BF16, "mm_dw_up", D_MODEL, 768, out_blocks=True)
    ffn_handle, ffn_token = _exchange_start([g_w_down, g_w_gate, g_w_up], True, "scatter_ffn_start")

    def norm_bwd(p, e):
        _, vjp = jax.vjp(_rms, e[0], e[2])
        dx, dgain = vjp(sum(p))
        dx = dx + e[1]
        return dx, dx, dgain

    dx1, dx1_b, g_ffn_norm = _mm_fused(
        [(dgate, w_g), (dup, w_u)], "nt", "mm_dh2_norm", 256, D_MODEL, norm_bwd,
        [(x1, 0), (dy, 0), (ffn_norm + ffn_token[0, 0], None)], (F32, BF16), b_blocks=True, sum_shape=(1, D_MODEL))
    def merge_bwd(p, e):
        _, vjp = jax.vjp(_merge, *e)
        dg0, dg1, da_dn, da_swa = vjp(p[0])
        return jnp.concatenate([dg0, dg1], axis=1), da_dn, da_swa

    dproj, da_dn, da_swa = _mm_fused(
        [(dx1_b, w_o)], "nt", "mm_dmerged_merge", 512, D_MODEL, merge_bwd,
        [(proj, P_GATE // D_MODEL), (proj, P_GATE // D_MODEL + 1), (a_dn, 0), (a_swa, 0)], (BF16,) * 3,
        wide_first=(P_WIDTH, 2 * D_MODEL))
    g_w_out = _mm([(merged, dx1_b)], "tn", BF16, "mm_dw_out", 512, D_MODEL, j_outer=True)
    g_w_out = g_w_out.reshape(N_DEV, LANES, D_MODEL)
    dy_dn = _mm([(da_dn, w_bdn)], "nt", F32, "mm_dy_dn", 1024, DN_WIDTH, b_blocks=True)
    dy_swa = _mm([(da_swa, w_bswa)], "nt", F32, "mm_dy_swa", 1024, SWA_WIDTH, b_blocks=True)
    g_w_bdn = _mm([(y_dn, da_dn)], "tn", BF16, "mm_dw_branch_dn", DN_WIDTH, 512, out_blocks=True)
    g_w_bswa = _mm([(y_swa, da_swa)], "tn", BF16, "mm_dw_branch_swa", SWA_WIDTH, 512, out_blocks=True)
    dproj, dsk, dsv, g_q_norm, g_k_norm, g_sinks, dbias = _swa_bwd(proj, swa_q_norm, swa_k_norm, swa_sinks, bias,
                                                                   dy_swa, dproj)
    dproj = _kv_into(dproj, dsk, dsv)
    g_rel_bias = _bias_bwd(dbias)[:, :REL_BUCKETS].T
    mix_handle, mix_token = _exchange_start([g_w_out, g_w_bdn, g_w_bswa], True, "scatter_mix_start")
    do, dproj, g_out_norm = _dn_out_bwd(o, proj, dn_out_norm + mix_token[0, 0], dy_dn, dproj)
    du, dw, dqe, dkd, dqk, degl = _dn_scan_bwd(u, w, qe, kd, qk, egl, states, do)
    dqkvn, dgd, dbeta = _dn_prep_bwd(qkvn, g, beta, tinv, du, dw, dqe, dkd, dqk, degl)
    dproj, dal, ddt = _dn_gate_bwd(proj, dn_a_log, dn_dt_bias, dbeta, dgd, dproj)
    g_a_log = dal.reshape(DN_HEADS, DN_DIM).sum(axis=1)
    g_dt_bias = ddt[0, DN_HEADS:2 * DN_HEADS]
    dproj, g_conv = _dn_conv_bwd(proj, conv_w, dqkvn, dproj)
    g_w_in = _w_in_to_blocks(_mm([(h, dproj)], "tn", BF16, "mm_dw_in", 512, 1664, j_outer=True))
    in_handle, in_token = _exchange_start([g_w_in], True, "scatter_in_start")
    dx, g_attn_norm = _mm_fused(
        [(dproj, w_pad)], "nt", "mm_dh_norm", 512, D_MODEL, lambda p, e: norm_bwd(p, e)[1:],
        [(xs, 0), (dx1, 0), (attn_norm + in_token[0, 0], None)], (F32,), sum_shape=(1, D_MODEL))

    g_small = {"attn_norm": g_attn_norm, "ffn_norm": g_ffn_norm, "rel_bias": g_rel_bias, "dn_out_norm": g_out_norm,
               "swa_q_norm": g_q_norm, "swa_k_norm": g_k_norm, "dn_a_log": g_a_log, "dn_dt_bias": g_dt_bias,
               "swa_sinks": g_sinks}
    me = 4 * lax.axis_index("x") + 2 * lax.axis_index("y") + lax.axis_index("c")
    outs = {}

    def finish(handle, group, name, after):
        srcs, lands = _exchange_wait(handle, after, True, name)
        for n, src, land in zip(group, srcs, lands):
            parts = _own_slot(land, lax.dynamic_index_in_dim(src, me, 0, keepdims=False))
            outs[n] = _adam_update(parts, args[n], args["m_" + n], args["v_" + n], "adam_" + n)

    finish(ffn_handle, ("w_down", "w_gate", "w_up"), "scatter_ffn_wait", dx)
    finish(mix_handle, ("w_out", "w_branch_dn", "w_branch_swa"), "scatter_mix_wait", dx)
    sheets, conv_all = _all_gather_direct([_small_pack(g_small, loss_local), _pad_to(g_conv, (8, DN_QKV))],
                                          "all_gather_small",
                                          after=outs["w_up"][0])
    finish(in_handle, ("w_in",), "scatter_in_wait", sheets)
    conv_parts = lax.dynamic_slice(conv_all, (0, 0, me * CONV_SHARD[1]), (N_DEV,) + CONV_SHARD)
    outs["dn_conv"] = _adam_update(conv_parts, dn_conv, m_dn_conv, v_dn_conv, "adam_dn_conv")
    small_outs, loss = _small_update(sheets, {n: args[n] for n in SMALL}, {n: args["m_" + n] for n in SMALL},
                                     {n: args["v_" + n] for n in SMALL})
    outs.update(small_outs)

    names = ("attn_norm", "w_in", "dn_conv", "dn_a_log", "dn_dt_bias", "dn_out_norm", "swa_q_norm", "swa_k_norm",
             "swa_sinks", "rel_bias", "w_branch_dn", "w_branch_swa", "w_out", "ffn_norm", "w_gate", "w_up", "w_down")
    results = []
    for kind in range(4):
        results += [outs[n][kind].reshape(args[n].shape) for n in names]

    return (loss.reshape(()), dx.reshape(x.shape), *results)
```

```python
import math

import numpy as np
import jax
import jax.numpy as jnp
from jax import lax
from jax.experimental import pallas as pl
from jax.experimental.pallas import tpu as pltpu

F32 = jnp.float32
BF16 = jnp.bfloat16
HI = lax.Precision.HIGHEST

D_MODEL = 1024
DN_HEADS = 4
DN_DIM = 128
DN_WIDTH = 512
DN_QKV = 1536
DN_CONV = 4
CHUNK = 64
SWA_HEADS = 8
SWA_KV = 2
SWA_GROUP = 4
SWA_DIM = 64
SWA_WIDTH = 512
SWA_KVW = 128
WINDOW = 128
BLOCK = 128
REL_BUCKETS = 32
REL_MAX_DIST = 128
D_FF = 2816
D_IN = 4872
EPS = 1e-6
N_DEV = 8

ADAM_LR = 0.001
ADAM_B1 = 0.9
ADAM_B2 = 0.999
ADAM_EPS = 1e-08
ADAM_WD = 0.01
ADAM_STEP = 10

P_GATE, P_QKV, P_Z, P_SQ, P_SK, P_SV, P_BA = 0, 2048, 3584, 4096, 4608, 4736, 4864
P_WIDTH = 4992
R_QKV, R_Z, R_B, R_A, R_SQ, R_SK, R_SV, R_GATE = 0, 1536, 2048, 2052, 2056, 2568, 2696, 2824

VMEM_LIMIT = 56 * 1024 * 1024
LANES = 128
MESH_ID = pl.DeviceIdType.MESH


def _params(sem=None):
    return pltpu.CompilerParams(dimension_semantics=sem, vmem_limit_bytes=VMEM_LIMIT)


def _pick(dim, target):
    if dim <= target:
        return dim
    t = target - target % LANES
    while t >= LANES:
        if dim % t == 0:
            return t
        t -= LANES
    return dim


_DIMS = {"nn": (((1,), (0,)), ((), ())), "nt": (((1,), (1,)), ((), ())), "tn": (((0,), (0,)), ((), ()))}


def _tile_product(a_ref, b_ref, mode, b_blocks):
    a = a_ref[...].astype(BF16)
    b = jnp.concatenate([b_ref[d] for d in range(b_ref.shape[0])], axis=1) if b_blocks else b_ref[...]
    return lax.dot_general(a, b.astype(BF16), _DIMS[mode], preferred_element_type=F32)


def _mm(pairs, mode, out_dtype, name, bm, bn, j_outer=False, b_blocks=False, out_blocks=False):
    a0, b0 = pairs[0]
    cb = b0.shape[2] if b_blocks else None
    b_shape = (b0.shape[1], N_DEV * cb) if b_blocks else b0.shape
    if mode == "nn":
        (M, K), (K2, N) = a0.shape, b_shape
    elif mode == "nt":
        (M, K), (N, K2) = a0.shape, b_shape
    else:
        (K, M), (K2, N) = a0.shape, b_shape
    bm, bn = min(bm, M), min(bn, N)
    assert K == K2 and M % bm == 0 and N % bn == 0, (name, a0.shape, b0.shape, bm, bn)
    co = N // N_DEV
    assert not out_blocks or bn % co == 0
    dims = _DIMS[mode]
    n = len(pairs)

    def body(*refs):
        o_ref = refs[2 * n]
        acc = None
        for t in range(n):
            p = _tile_product(refs[2 * t], refs[2 * t + 1], mode, b_blocks)
            acc = p if acc is None else acc + p
        if out_blocks:
            for d in range(bn // co):
                o_ref[d] = acc[:, d * co:(d + 1) * co].astype(out_dtype)
        else:
            o_ref[...] = acc.astype(out_dtype)

    def ij(f):
        return (lambda j, i: f(i, j)) if j_outer else f

    a_spec = pl.BlockSpec((K, bm), ij(lambda i, j: (0, i))) if mode == "tn" else pl.BlockSpec((bm, K), ij(lambda i, j: (i, 0)))
    if b_blocks and mode == "nt":
        b_spec = pl.BlockSpec((N_DEV, bn, cb), ij(lambda i, j: (0, j, 0)))
    elif b_blocks:
        b_spec = pl.BlockSpec((bn // cb, K, cb), ij(lambda i, j: (j, 0, 0)))
    elif mode == "nt":
        b_spec = pl.BlockSpec((bn, K), ij(lambda i, j: (j, 0)))
    else:
        b_spec = pl.BlockSpec((K, bn), ij(lambda i, j: (0, j)))
    if out_blocks:
        out_spec = pl.BlockSpec((bn // co, bm, co), ij(lambda i, j: (j, i, 0)))
        out_shape = jax.ShapeDtypeStruct((N_DEV, M, co), out_dtype)
    else:
        out_spec = pl.BlockSpec((bm, bn), ij(lambda i, j: (i, j)))
        out_shape = jax.ShapeDtypeStruct((M, N), out_dtype)
    grid = (N // bn, M // bm) if j_outer else (M // bm, N // bn)
    return pl.pallas_call(
        body, grid=grid, in_specs=[a_spec, b_spec] * n, out_specs=out_spec, out_shape=out_shape, name=name,
        compiler_params=_params(("parallel", "parallel")),
    )(*[x for pair in pairs for x in pair])


def _mm_fused(pairs, mode, name, bm, bn, epilogue, extras, out_dtypes, j_outer=False, b_blocks=False,
              sum_shape=None, wide_first=None):
    a0, b0 = pairs[0]
    cb = b0.shape[2] if b_blocks else None
    b_shape = (b0.shape[1], N_DEV * cb) if b_blocks else b0.shape
    if mode == "nn":
        (M, K), (K2, N) = a0.shape, b_shape
    else:
        (M, K), (N, K2) = a0.shape, b_shape
    bm, bn = min(bm, M), min(bn, N)
    assert mode in ("nn", "nt") and K == K2 and M % bm == 0 and N % bn == 0, (name, a0.shape, b0.shape)
    dims = _DIMS[mode]
    n, ne, no = len(pairs), len(extras), len(out_dtypes)

    def body(*refs):
        prods = [_tile_product(refs[2 * t], refs[2 * t + 1], mode, b_blocks) for t in range(n)]
        results = epilogue(prods, [r[...] for r in refs[2 * n:2 * n + ne]])
        out_refs = refs[2 * n + ne:]
        for o_ref, val, dt in zip(out_refs, results, out_dtypes):
            o_ref[...] = val.astype(dt)
        if sum_shape is not None:
            s_ref = out_refs[no]

            @pl.when((pl.program_id(0) == 0) & (pl.program_id(1) == 0))
            def _():
                s_ref[...] = jnp.zeros_like(s_ref)

            s_ref[...] += results[no]

    def ij(f):
        return (lambda j, i: f(i, j)) if j_outer else f

    a_spec = pl.BlockSpec((bm, K), ij(lambda i, j: (i, 0)))
    once = dict(pipeline_mode=pl.Buffered(1)) if bn == N else {}
    if b_blocks and mode == "nt":
        b_spec = pl.BlockSpec((N_DEV, bn, cb), ij(lambda i, j: (0, j, 0)), **once)
    elif b_blocks:
        b_spec = pl.BlockSpec((bn // cb, K, cb), ij(lambda i, j: (j, 0, 0)), **once)
    elif mode == "nt":
        b_spec = pl.BlockSpec((bn, K), ij(lambda i, j: (j, 0)), **once)
    else:
        b_spec = pl.BlockSpec((K, bn), ij(lambda i, j: (0, j)), **once)
    e_specs = [pl.BlockSpec((1, bn), ij(lambda i, j: (0, j))) if first is None
               else pl.BlockSpec((bm, bn), ij(lambda i, j, first=first: (i, first + j))) for _, first in extras]
    tile = pl.BlockSpec((bm, bn), ij(lambda i, j: (i, j)))
    out_specs = [tile] * no
    out_shape = [jax.ShapeDtypeStruct((M, N), dt) for dt in out_dtypes]
    if wide_first is not None:
        assert bn == N
        out_specs[0] = pl.BlockSpec((bm, wide_first[1]), ij(lambda i, j: (i, 0)))
        out_shape[0] = jax.ShapeDtypeStruct((M, wide_first[0]), out_dtypes[0])
    if sum_shape is not None:
        assert sum_shape[1] in (1, bn) and (sum_shape[1] == 1 or bn == N)
        out_specs.append(_full(sum_shape))
        out_shape.append(jax.ShapeDtypeStruct(sum_shape, F32))
    grid = (N // bn, M // bm) if j_outer else (M // bm, N // bn)
    sem = ("arbitrary", "arbitrary") if sum_shape is not None else ("parallel", "parallel")
    return pl.pallas_call(
        body, grid=grid, in_specs=[a_spec, b_spec] * n + e_specs, out_specs=out_specs, out_shape=out_shape,
        name=name, compiler_params=_params(sem),
    )(*[x for pair in pairs for x in pair], *[arr for arr, _ in extras])


def _rms(x, gain):
    return x * lax.rsqrt(jnp.mean(x * x, axis=-1, keepdims=True) + EPS) * gain


def _silu(x):
    return x * jax.nn.sigmoid(x)


def _act(g, u):
    return _silu(g) * u


def _merge(g0, g1, a_dn, a_swa):
    return jax.nn.sigmoid(g0) * a_dn + jax.nn.sigmoid(g1) * a_swa


def _dn_post(c, is_v, q_scale):
    a = _silu(c)
    rs = lax.rsqrt(jnp.sum(a * a, axis=-1, keepdims=True) + EPS) * q_scale
    return a * jnp.where(is_v, 1.0, rs)


def _dn_out(o, z, gain):
    return _rms(o, gain) * _silu(z)


def _dot(a, b, dims=_DIMS["nn"], hi=False):
    if a.ndim == 3 or b.ndim == 3:
        batch = a.shape[0] if a.ndim == 3 else b.shape[0]
        a = a if a.ndim == 3 else jnp.broadcast_to(a, (batch,) + a.shape)
        b = b if b.ndim == 3 else jnp.broadcast_to(b, (batch,) + b.shape)
        ((ca,), (cb,)), _ = dims
        dims = (((ca + 1,), (cb + 1,)), ((0,), (0,)))
    if hi:
        return lax.dot_general(a, b, dims, precision=HI, preferred_element_type=F32)
    return lax.dot_general(a.astype(BF16), b.astype(BF16), dims, preferred_element_type=F32)


def _pieces(x):
    hi = x.astype(BF16)
    r1 = x - hi.astype(F32)
    mid = r1.astype(BF16)
    return hi, mid, (r1 - mid.astype(F32)).astype(BF16)


def _sel_left_impl(m, x):
    mb = m.astype(BF16)
    hi, mid, lo = _pieces(x)
    return _dot(mb, hi) + (_dot(mb, mid) + _dot(mb, lo))


@jax.custom_vjp
def _sel_left(m, mt, x):
    return _sel_left_impl(m, x)


_sel_left.defvjp(lambda m, mt, x: (_sel_left_impl(m, x), (m, mt)),
                 lambda res, ct: (jnp.zeros_like(res[0]), jnp.zeros_like(res[1]), _sel_left_impl(res[1], ct)))


def _sel_right_impl(x, s):
    sb = s.astype(BF16)
    hi, mid, lo = _pieces(x)
    return _dot(hi, sb) + (_dot(mid, sb) + _dot(lo, sb))


@jax.custom_vjp
def _sel_right(x, s, st):
    return _sel_right_impl(x, s)


_sel_right.defvjp(lambda x, s, st: (_sel_right_impl(x, s), (s, st)),
                  lambda res, ct: (_sel_right_impl(ct, res[1]), jnp.zeros_like(res[0]), jnp.zeros_like(res[1])))


def _dot3_impl(a, b):
    a_hi, a_lo, _ = _pieces(a)
    b_hi, b_lo, _ = _pieces(b)
    return _dot(a_hi, b_hi) + (_dot(a_hi, b_lo) + _dot(a_lo, b_hi))


@jax.custom_vjp
def _dot3(a, b):
    return _dot3_impl(a, b)


_dot3.defvjp(lambda a, b: (_dot3_impl(a, b), (a, b)),
             lambda res, ct: (_dot(ct, res[1], _DIMS["nt"]), _dot(res[0], ct, _DIMS["tn"])))


def _inv_impl(a, eye, strict):
    t = eye - a
    p = _dot(a, a)
    for level in range(5):
        t = t + _dot(t, p)
        if level < 4:
            p = _dot(p, p)
    t = t + _dot(t, eye - t - _dot3_impl(a, t))
    return jnp.where(strict > 0.5, t, eye)


@jax.custom_vjp
def _inv_given(a, t):
    return t.astype(F32)


_inv_given.defvjp(lambda a, t: (t.astype(F32), t),
                  lambda t, ct: (-_dot(_dot(t, ct, _DIMS["tn"]), t, _DIMS["nt"]), jnp.zeros_like(t)))


@jax.custom_vjp
def _lanes_join(a, b):
    return jnp.concatenate([a, b], axis=-1)


_lanes_join.defvjp(lambda a, b: (jnp.concatenate([a, b], axis=-1), None),
                   lambda _, ct: (ct[..., :ct.shape[-1] // 2], ct[..., ct.shape[-1] // 2:]))


@jax.custom_vjp
def _lanes_halves(y):
    h = y.shape[-1] // 2
    return y[..., :h], y[..., h:]


_lanes_halves.defvjp(lambda y: ((y[..., :y.shape[-1] // 2], y[..., y.shape[-1] // 2:]), None),
                     lambda _, ct: (jnp.concatenate(ct, axis=-1),))

GROUP = 4
GROUP_ROWS = GROUP * CHUNK


def _block_consts(n):
    ii = lax.broadcasted_iota(jnp.int32, (n, n), 0)
    jj = lax.broadcasted_iota(jnp.int32, (n, n), 1)
    shift = CHUNK.bit_length() - 1
    same = jnp.right_shift(ii, shift) == jnp.right_shift(jj, shift)
    return same & (ii >= jj), same & (ii <= jj), same & (ii > jj), same, ii == jj


def _lane0(n):
    s = (lax.broadcasted_iota(jnp.int32, (LANES, n), 0) == 0).astype(F32)
    st = (lax.broadcasted_iota(jnp.int32, (n, LANES), 1) == 0).astype(F32)
    return s, st


def _dn_group(q, k, v, g, beta, t_saved=None):
    n = GROUP_ROWS
    low_b, upp_b, strict_b, _, eye_b = _block_consts(n)
    low, upp, eye = low_b.astype(F32), upp_b.astype(F32), eye_b.astype(F32)
    gc = _sel_left(low, upp, g)
    per_chunk = (g.shape[0], GROUP, CHUNK, LANES)
    g_last = jnp.sum(g.reshape(per_chunk), axis=2, keepdims=True)
    gl = jnp.broadcast_to(g_last, per_chunk).reshape(g.shape)
    s, st = _lane0(n)
    col = _sel_right(gc, s, st)
    row = jnp.swapaxes(col, 1, 2)
    decay = jnp.exp(jnp.where(low_b, col - row, -jnp.inf))
    kb = k * beta
    vb = v * beta
    a = jnp.where(strict_b, _dot(kb, k, _DIMS["nt"]) * decay, 0.0)
    t = _inv_impl(a, eye, strict_b.astype(F32)) if t_saved is None else _inv_given(a, t_saved)
    u, w = _lanes_halves(_dot3(t, _lanes_join(vb, kb * jnp.exp(gc))))
    fold = (jnp.bitwise_and(lax.broadcasted_iota(jnp.int32, (n, CHUNK), 0), CHUNK - 1)
            == lax.broadcasted_iota(jnp.int32, (n, CHUNK), 1)).astype(F32)
    fold_t = (jnp.bitwise_and(lax.broadcasted_iota(jnp.int32, (CHUNK, n), 1), CHUNK - 1)
              == lax.broadcasted_iota(jnp.int32, (CHUNK, n), 0)).astype(F32)
    qk = _sel_right(_dot(q, k, _DIMS["nt"]) * decay, fold, fold_t)
    return u, w, q * jnp.exp(gc), k * jnp.exp(gl - gc), qk, jnp.exp(g_last), t


def _dn_step(s, u, w, qe, kd, qk, egl):
    v_new = u - _dot(w, s)
    o = _dot(qe, s) + _dot(qk, v_new)
    s_new = s * egl + _dot(kd, v_new, _DIMS["tn"])
    return s_new, o


def _swa_block(q, kband, vband, qg, kg, sinks, band):
    kn = _rms(kband, kg)
    qn = _rms(q, qg) * (SWA_DIM ** -0.5)
    logits = _dot(qn, kn, _DIMS["nt"]) + band
    m = lax.stop_gradient(jnp.maximum(jnp.max(logits, axis=-1, keepdims=True), sinks))
    p = jnp.exp(logits - m)
    denom = jnp.sum(p, axis=-1, keepdims=True) + jnp.exp(sinks - m)
    return _dot(p * (1.0 / denom), vband)


def _adamw(w, g, m, v):
    m = ADAM_B1 * m + (1.0 - ADAM_B1) * g
    v = ADAM_B2 * v + (1.0 - ADAM_B2) * jnp.square(g)
    m_hat = m / (1.0 - ADAM_B1 ** ADAM_STEP)
    v_hat = v / (1.0 - ADAM_B2 ** ADAM_STEP)
    delta = -ADAM_LR * (m_hat / (jnp.sqrt(v_hat) + ADAM_EPS) + ADAM_WD * w)
    return delta, m, v


def _row(tm, c, cb=0):
    return pl.BlockSpec((tm, c), lambda i, cb=cb: (i, cb))


def _full(shape):
    nd = len(shape)
    return pl.BlockSpec(shape, lambda *_, nd=nd: (0,) * nd)


def _norm_fwd(x, gain, name, tm=1024):
    S = x.shape[0]

    def body(x_ref, g_ref, h_ref):
        h_ref[...] = _rms(x_ref[...], g_ref[...]).astype(BF16)

    return pl.pallas_call(
        body, grid=(S // tm,), in_specs=[_row(tm, D_MODEL), _full((1, D_MODEL))],
        out_specs=_row(tm, D_MODEL), out_shape=jax.ShapeDtypeStruct((S, D_MODEL), BF16),
        name=name, compiler_params=_params(("parallel",)))(x, gain)


def _shift_down(x, s):
    row = lax.broadcasted_iota(jnp.int32, x.shape, 0)
    return jnp.where(row >= s, pltpu.roll(x, s, axis=0), 0.0)


def _shift_up(x, s):
    n = x.shape[0]
    row = lax.broadcasted_iota(jnp.int32, x.shape, 0)
    return jnp.where(row < n - s, pltpu.roll(x, n - s, axis=0), 0.0)


def _conv(x, w):
    out = w[DN_CONV - 1:DN_CONV] * x
    for s in range(1, DN_CONV):
        out = out + w[DN_CONV - 1 - s:DN_CONV - s] * _shift_down(x, s)
    return out


def _dn_conv_fwd(proj, conv_w):
    S = proj.shape[0]
    nb = DN_QKV // LANES

    def body(x_ref, w_ref, o_ref):
        j = pl.program_id(0)
        q_scale = jnp.where(j < DN_HEADS, DN_DIM ** -0.5, 1.0).astype(F32)
        o_ref[...] = _dn_post(_conv(x_ref[...], w_ref[...]), j >= 2 * DN_HEADS, q_scale)

    return pl.pallas_call(
        body, grid=(nb,),
        in_specs=[pl.BlockSpec((S, LANES), lambda j: (0, P_QKV // LANES + j)),
                  pl.BlockSpec((DN_CONV, LANES), lambda j: (0, j))],
        out_specs=pl.BlockSpec((S, LANES), lambda j: (0, j)),
        out_shape=jax.ShapeDtypeStruct((S, DN_QKV), F32), name="dn_conv_fwd",
        compiler_params=_params(("parallel",)))(proj, conv_w)


def _dn_conv_bwd(proj, conv_w, dqkvn, dproj):
    S = proj.shape[0]
    nb = DN_QKV // LANES

    def body(x_ref, w_ref, d_ref, _, dx_ref, dw_ref):
        j = pl.program_id(0)
        q_scale = jnp.where(j < DN_HEADS, DN_DIM ** -0.5, 1.0).astype(F32)
        x = x_ref[...]
        w = w_ref[...]
        _, vjp = jax.vjp(lambda c: _dn_post(c, j >= 2 * DN_HEADS, q_scale), _conv(x, w))
        (dc,) = vjp(d_ref[0])
        dx = w[DN_CONV - 1:DN_CONV] * dc
        dw_ref[DN_CONV - 1:DN_CONV, :] = jnp.sum(dc * x, axis=0, keepdims=True)
        for s in range(1, DN_CONV):
            dx = dx + w[DN_CONV - 1 - s:DN_CONV - s] * _shift_up(dc, s)
            dw_ref[DN_CONV - 1 - s:DN_CONV - s, :] = jnp.sum(dc * _shift_down(x, s), axis=0, keepdims=True)
        dx_ref[...] = dx.astype(BF16)

    return pl.pallas_call(
        body, grid=(nb,),
        in_specs=[pl.BlockSpec((S, LANES), lambda j: (0, P_QKV // LANES + j)),
                  pl.BlockSpec((DN_CONV, LANES), lambda j: (0, j)),
                  pl.BlockSpec((1, S, LANES), lambda j: (lax.div(j, DN_HEADS), 0, lax.rem(j, DN_HEADS))),
                  pl.BlockSpec(memory_space=pl.ANY)],
        out_specs=[pl.BlockSpec((S, LANES), lambda j: (0, P_QKV // LANES + j)),
                   pl.BlockSpec((DN_CONV, LANES), lambda j: (0, j))],
        out_shape=[jax.ShapeDtypeStruct(dproj.shape, dproj.dtype), jax.ShapeDtypeStruct((DN_CONV, DN_QKV), F32)],
        input_output_aliases={3: 0},
        name="dn_conv_bwd", compiler_params=_params(("parallel",)))(proj, conv_w, dqkvn, dproj)


def _expanders():
    eb = np.zeros((LANES, DN_WIDTH), np.float32)
    ea = np.zeros((LANES, DN_WIDTH), np.float32)
    for h in range(DN_HEADS):
        eb[h, h * DN_DIM:(h + 1) * DN_DIM] = 1.0
        ea[DN_HEADS + h, h * DN_DIM:(h + 1) * DN_DIM] = 1.0
    return jnp.asarray(eb), jnp.asarray(ea), jnp.asarray(eb.T), jnp.asarray(ea.T)


def _dn_gate_args(a_log, dt_bias):
    alog = jnp.repeat(a_log.reshape(1, DN_HEADS), DN_DIM, axis=1)
    dtb = _pad_to(jnp.pad(dt_bias.reshape(1, DN_HEADS), ((0, 0), (DN_HEADS, 0))), (1, LANES))
    return _expanders() + (alog, dtb)


def _dn_gate_specs(tm):
    return [_row(tm, LANES, P_BA // LANES), _full((LANES, DN_WIDTH)), _full((LANES, DN_WIDTH)),
            _full((DN_WIDTH, LANES)), _full((DN_WIDTH, LANES)), _full((1, DN_WIDTH)), _full((1, LANES))]


def _dn_gate_fn(ba, eb, ea, ebt, eat, alog, dtb):
    beta = _sel_right(jax.nn.sigmoid(ba), eb, ebt)
    g = -jnp.exp(alog) * _sel_right(jax.nn.softplus(ba + dtb), ea, eat)
    return beta, g


def _dn_gate_fwd(proj, a_log, dt_bias, tm=1024):
    S = proj.shape[0]
    args = _dn_gate_args(a_log, dt_bias)

    def body(ba_ref, eb_ref, ea_ref, ebt_ref, eat_ref, al_ref, dt_ref, beta_ref, g_ref):
        beta, g = _dn_gate_fn(ba_ref[...], eb_ref[...], ea_ref[...], ebt_ref[...], eat_ref[...], al_ref[...],
                              dt_ref[...])
        beta_ref[...] = beta
        g_ref[...] = g

    return pl.pallas_call(
        body, grid=(S // tm,), in_specs=_dn_gate_specs(tm), out_specs=[_row(tm, DN_WIDTH), _row(tm, DN_WIDTH)],
        out_shape=[jax.ShapeDtypeStruct((S, DN_WIDTH), F32), jax.ShapeDtypeStruct((S, DN_WIDTH), F32)],
        name="dn_gate_fwd", compiler_params=_params(("parallel",)))(proj, *args)


def _dn_gate_bwd(proj, a_log, dt_bias, dbeta, dg, dproj, tm=1024):
    S = proj.shape[0]
    args = _dn_gate_args(a_log, dt_bias)

    def body(ba_ref, eb_ref, ea_ref, ebt_ref, eat_ref, al_ref, dt_ref, dbeta_ref, dg_ref, _, dba_ref, dal_ref,
             ddt_ref):
        eb, ea, ebt, eat = eb_ref[...], ea_ref[...], ebt_ref[...], eat_ref[...]
        _, vjp = jax.vjp(lambda ba, al, dt: _dn_gate_fn(ba, eb, ea, ebt, eat, al, dt), ba_ref[...], al_ref[...],
                         dt_ref[...])
        dba, dal, ddt = vjp((dbeta_ref[...], dg_ref[...]))
        dba_ref[...] = dba.astype(BF16)

        @pl.when(pl.program_id(0) == 0)
        def _():
            dal_ref[...] = jnp.zeros_like(dal_ref)
            ddt_ref[...] = jnp.zeros_like(ddt_ref)

        dal_ref[...] += dal
        ddt_ref[...] += ddt

    return pl.pallas_call(
        body, grid=(S // tm,),
        in_specs=_dn_gate_specs(tm) + [_row(tm, DN_WIDTH), _row(tm, DN_WIDTH), pl.BlockSpec(memory_space=pl.ANY)],
        out_specs=[_row(tm, LANES, P_BA // LANES), _full((1, DN_WIDTH)), _full((1, LANES))],
        out_shape=[jax.ShapeDtypeStruct(dproj.shape, dproj.dtype), jax.ShapeDtypeStruct((1, DN_WIDTH), F32),
                   jax.ShapeDtypeStruct((1, LANES), F32)],
        input_output_aliases={len(args) + 3: 0},
        name="dn_gate_bwd", compiler_params=_params(("arbitrary",)))(proj, *args, dbeta, dg, dproj)


PREP_GROUPS = 8
PREP_CHUNKS = GROUP * PREP_GROUPS


def _dn_prep_specs():
    rows = PREP_CHUNKS * CHUNK
    q = pl.BlockSpec((rows, LANES), lambda h, c: (c, h))
    k = pl.BlockSpec((rows, LANES), lambda h, c: (c, DN_HEADS + h))
    v = pl.BlockSpec((rows, LANES), lambda h, c: (c, 2 * DN_HEADS + h))
    qk = pl.BlockSpec((1, rows, CHUNK), lambda h, c: (h, c, 0))
    egl = pl.BlockSpec((1, PREP_CHUNKS, 1, LANES), lambda h, c: (h, c, 0, 0))
    return q, k, v, qk, egl


def _dn_prep_fwd(qkvn, g, beta):
    S = qkvn.shape[0]
    nc = S // CHUNK
    q, k, v, qks, egl = _dn_prep_specs()

    def body(q_ref, k_ref, v_ref, g_ref, b_ref, u_ref, w_ref, qe_ref, kd_ref, qk_ref, egl_ref, t_ref):
        rows = PREP_CHUNKS * CHUNK
        grp = (PREP_GROUPS, GROUP_ROWS, LANES)
        u, w, qe, kd, qk, e, t = _dn_group(q_ref[...].reshape(grp), k_ref[...].reshape(grp), v_ref[...].reshape(grp),
                                           g_ref[...].reshape(grp), b_ref[...].reshape(grp))
        u_ref[...] = u.reshape(rows, LANES)
        w_ref[...] = w.reshape(rows, LANES)
        qe_ref[...] = qe.reshape(rows, LANES)
        kd_ref[...] = kd.reshape(rows, LANES)
        t_ref[0] = t.reshape(rows, GROUP_ROWS).astype(BF16)
        qk_ref[0] = qk.reshape(rows, CHUNK)
        egl_ref[0] = e.reshape(PREP_CHUNKS, 1, LANES)

    wide = jax.ShapeDtypeStruct((S, DN_WIDTH), F32)
    return pl.pallas_call(
        body, grid=(DN_HEADS, nc // PREP_CHUNKS), in_specs=[q, k, v, q, q],
        out_specs=[q, q, q, q, qks, egl, _dn_tinv_spec()],
        out_shape=[wide, wide, wide, wide, jax.ShapeDtypeStruct((DN_HEADS, S, CHUNK), F32),
                   jax.ShapeDtypeStruct((DN_HEADS, nc, 1, LANES), F32),
                   jax.ShapeDtypeStruct((DN_HEADS, S, GROUP_ROWS), BF16)],
        name="dn_prep_fwd", compiler_params=_params(("parallel", "parallel")))(qkvn, qkvn, qkvn, g, beta)


def _dn_tinv_spec():
    return pl.BlockSpec((1, PREP_CHUNKS * CHUNK, GROUP_ROWS), lambda h, c: (h, c, 0))


def _dn_prep_bwd(qkvn, g, beta, tinv, du, dw, dqe, dkd, dqk, degl):
    S = qkvn.shape[0]
    nc = S // CHUNK
    q, k, v, qks, egl = _dn_prep_specs()

    def body(q_ref, k_ref, v_ref, g_ref, b_ref, t_ref, du_ref, dw_ref, dqe_ref, dkd_ref, dqk_ref, degl_ref,
             dqkv_ref, dg_ref, db_ref):
        rows = PREP_CHUNKS * CHUNK
        grp = (PREP_GROUPS, GROUP_ROWS, LANES)
        t_saved = t_ref[0].reshape(PREP_GROUPS, GROUP_ROWS, GROUP_ROWS)
        _, vjp = jax.vjp(lambda *x: _dn_group(*x, t_saved=t_saved)[:6], q_ref[...].reshape(grp),
                         k_ref[...].reshape(grp), v_ref[...].reshape(grp), g_ref[...].reshape(grp),
                         b_ref[...].reshape(grp))
        dq, dk, dv, dg, db = vjp((du_ref[...].reshape(grp), dw_ref[...].reshape(grp), dqe_ref[...].reshape(grp),
                                  dkd_ref[...].reshape(grp), dqk_ref[0].reshape(PREP_GROUPS, GROUP_ROWS, CHUNK),
                                  degl_ref[0].reshape(PREP_GROUPS, GROUP, 1, LANES)))
        dqkv_ref[0] = dq.reshape(rows, LANES)
        dqkv_ref[1] = dk.reshape(rows, LANES)
        dqkv_ref[2] = dv.reshape(rows, LANES)
        dg_ref[...] = dg.reshape(rows, LANES)
        db_ref[...] = db.reshape(rows, LANES)

    wide = jax.ShapeDtypeStruct((S, DN_WIDTH), F32)
    rows = PREP_CHUNKS * CHUNK
    return pl.pallas_call(
        body, grid=(DN_HEADS, nc // PREP_CHUNKS), in_specs=[q, k, v, q, q, _dn_tinv_spec(), q, q, q, q, qks, egl],
        out_specs=[pl.BlockSpec((3, rows, LANES), lambda h, c: (0, c, h)), q, q],
        out_shape=[jax.ShapeDtypeStruct((3, S, DN_WIDTH), F32), wide, wide],
        name="dn_prep_bwd", compiler_params=_params(("parallel", "parallel")),
    )(qkvn, qkvn, qkvn, g, beta, tinv, du, dw, dqe, dkd, dqk, degl)


SCAN_CHUNKS = 16


def _dn_scan_specs(nc, reverse):
    nb = nc // SCAN_CHUNKS

    def cidx(c):
        return nb - 1 - c if reverse else c

    hc = pl.BlockSpec((SCAN_CHUNKS * CHUNK, DN_WIDTH), lambda c: (cidx(c), 0))
    qk = pl.BlockSpec((DN_HEADS, SCAN_CHUNKS * CHUNK, CHUNK), lambda c: (0, cidx(c), 0))
    egl = pl.BlockSpec((DN_HEADS, SCAN_CHUNKS, 1, LANES), lambda c: (0, cidx(c), 0, 0))
    st = pl.BlockSpec((DN_HEADS, SCAN_CHUNKS, DN_DIM, DN_DIM), lambda c: (0, cidx(c), 0, 0))
    return hc, qk, egl, st


def _heads(ref, i):
    return jnp.stack([ref[pl.ds(i * CHUNK, CHUNK), pl.ds(h * DN_DIM, DN_DIM)] for h in range(DN_HEADS)])


def _dn_scan_fwd(u, w, qe, kd, qk, egl):
    S = u.shape[0]
    nc = S // CHUNK
    hc, qks, egls, st = _dn_scan_specs(nc, False)

    def body(u_ref, w_ref, qe_ref, kd_ref, qk_ref, egl_ref, o_ref, st_ref, s_scr):
        @pl.when(pl.program_id(0) == 0)
        def _():
            s_scr[...] = jnp.zeros_like(s_scr)

        s = s_scr[...]
        for i in range(SCAN_CHUNKS):
            rows = pl.ds(i * CHUNK, CHUNK)
            st_ref[:, i] = s
            s, o = _dn_step(s, _heads(u_ref, i), _heads(w_ref, i), _heads(qe_ref, i), _heads(kd_ref, i),
                            qk_ref[:, rows, :], egl_ref[:, i])
            for h in range(DN_HEADS):
                o_ref[rows, pl.ds(h * DN_DIM, DN_DIM)] = o[h]
        s_scr[...] = s

    return pl.pallas_call(
        body, grid=(nc // SCAN_CHUNKS,), in_specs=[hc, hc, hc, hc, qks, egls], out_specs=[hc, st],
        out_shape=[jax.ShapeDtypeStruct((S, DN_WIDTH), F32), jax.ShapeDtypeStruct((DN_HEADS, nc, DN_DIM, DN_DIM), F32)],
        scratch_shapes=[pltpu.VMEM((DN_HEADS, DN_DIM, DN_DIM), F32)], name="dn_scan_fwd",
        compiler_params=_params(("arbitrary",)))(u, w, qe, kd, qk, egl)


def _dn_scan_bwd(u, w, qe, kd, qk, egl, states, do):
    S = u.shape[0]
    nc = S // CHUNK
    hc, qks, egls, st = _dn_scan_specs(nc, True)

    def body(u_ref, w_ref, qe_ref, kd_ref, qk_ref, egl_ref, st_ref, do_ref,
             du_ref, dw_ref, dqe_ref, dkd_ref, dqk_ref, degl_ref, ds_scr):
        @pl.when(pl.program_id(0) == 0)
        def _():
            ds_scr[...] = jnp.zeros_like(ds_scr)

        ds = ds_scr[...]
        for i in reversed(range(SCAN_CHUNKS)):
            rows = pl.ds(i * CHUNK, CHUNK)
            _, vjp = jax.vjp(_dn_step, st_ref[:, i], _heads(u_ref, i), _heads(w_ref, i), _heads(qe_ref, i),
                             _heads(kd_ref, i), qk_ref[:, rows, :], egl_ref[:, i])
            ds, du, dw, dqe, dkd, dqk, degl = vjp((ds, _heads(do_ref, i)))
            dqk_ref[:, rows, :] = dqk
            degl_ref[:, i] = degl
            for h in range(DN_HEADS):
                cols = pl.ds(h * DN_DIM, DN_DIM)
                du_ref[rows, cols] = du[h]
                dw_ref[rows, cols] = dw[h]
                dqe_ref[rows, cols] = dqe[h]
                dkd_ref[rows, cols] = dkd[h]
        ds_scr[...] = ds

    wide = jax.ShapeDtypeStruct((S, DN_WIDTH), F32)
    return pl.pallas_call(
        body, grid=(nc // SCAN_CHUNKS,), in_specs=[hc, hc, hc, hc, qks, egls, st, hc],
        out_specs=[hc, hc, hc, hc, qks, egls],
        out_shape=[wide, wide, wide, wide, jax.ShapeDtypeStruct((DN_HEADS, S, CHUNK), F32),
                   jax.ShapeDtypeStruct((DN_HEADS, nc, 1, LANES), F32)],
        scratch_shapes=[pltpu.VMEM((DN_HEADS, DN_DIM, DN_DIM), F32)], name="dn_scan_bwd",
        compiler_params=_params(("arbitrary",)))(u, w, qe, kd, qk, egl, states, do)


def _dn_out_fwd(o, proj, gain, tm=1024):
    S = o.shape[0]

    def body(o_ref, z_ref, g_ref, y_ref):
        y_ref[...] = _dn_out(o_ref[...], z_ref[...], g_ref[...]).astype(BF16)

    hs = pl.BlockSpec((tm, LANES), lambda i, h: (i, h))
    zs = pl.BlockSpec((tm, LANES), lambda i, h: (i, P_Z // LANES + h))
    return pl.pallas_call(
        body, grid=(S // tm, DN_HEADS), in_specs=[hs, zs, _full((1, DN_DIM))], out_specs=hs,
        out_shape=jax.ShapeDtypeStruct((S, DN_WIDTH), BF16), name="dn_out_fwd",
        compiler_params=_params(("parallel", "parallel")))(o, proj, gain)


_ANY = pl.BlockSpec(memory_space=pl.ANY)


def _dn_out_bwd(o, proj, gain, dy, dproj, tm=1024):
    S = o.shape[0]

    def body(o_ref, z_ref, g_ref, dy_ref, _, do_ref, dz_ref, dg_ref):
        _, vjp = jax.vjp(_dn_out, o_ref[...], z_ref[...], g_ref[...])
        do, dz, dg = vjp(dy_ref[...])
        do_ref[...] = do
        dz_ref[...] = dz.astype(BF16)

        @pl.when((pl.program_id(0) == 0) & (pl.program_id(1) == 0))
        def _():
            dg_ref[...] = jnp.zeros_like(dg_ref)

        dg_ref[...] += dg

    hs = pl.BlockSpec((tm, LANES), lambda i, h: (i, h))
    zs = pl.BlockSpec((tm, LANES), lambda i, h: (i, P_Z // LANES + h))
    return pl.pallas_call(
        body, grid=(S // tm, DN_HEADS), in_specs=[hs, zs, _full((1, DN_DIM)), hs, _ANY],
        out_specs=[hs, zs, _full((1, DN_DIM))],
        out_shape=[jax.ShapeDtypeStruct((S, DN_WIDTH), F32), jax.ShapeDtypeStruct(dproj.shape, dproj.dtype),
                   jax.ShapeDtypeStruct((1, DN_DIM), F32)],
        input_output_aliases={4: 1},
        name="dn_out_bwd", compiler_params=_params(("arbitrary", "arbitrary")))(o, proj, gain, dy, dproj)


def _rel_buckets():
    qi = np.arange(BLOCK)[:, None]
    kj = np.arange(2 * BLOCK)[None, :]
    n = np.maximum(BLOCK + qi - kj, 0)
    max_exact = REL_BUCKETS // 2
    nf = np.maximum(n, 1).astype(np.float32)
    large = max_exact + (np.log(nf / np.float32(max_exact)) / np.float32(math.log(REL_MAX_DIST / max_exact))
                         * np.float32(REL_BUCKETS - max_exact)).astype(np.int32)
    large = np.minimum(large, REL_BUCKETS - 1)
    return np.where(n < max_exact, n, large).astype(np.int32)


def _bias_fwd(rel_bias):
    buckets = jnp.asarray(_rel_buckets())

    def body(rb_ref, bk_ref, o_ref):
        bk = bk_ref[...]
        for h in range(SWA_HEADS):
            acc = jnp.zeros((BLOCK, 2 * BLOCK), F32)
            for b in range(REL_BUCKETS):
                acc = jnp.where(bk == b, rb_ref[b, h], acc)
            for first in range(2):
                o_ref[first, h] = jnp.where(_swa_mask(1 - first), acc, -jnp.inf)

    return pl.pallas_call(
        body, in_specs=[pl.BlockSpec(memory_space=pltpu.SMEM), pl.BlockSpec(memory_space=pltpu.VMEM)],
        out_specs=pl.BlockSpec(memory_space=pltpu.VMEM),
        out_shape=jax.ShapeDtypeStruct((2, SWA_HEADS, BLOCK, 2 * BLOCK), F32), name="swa_bias_fwd",
        compiler_params=_params())(rel_bias, buckets)


def _bias_bwd(dbias):
    buckets = jnp.asarray(_rel_buckets())

    def body(d_ref, bk_ref, o_ref):
        bk = bk_ref[...]
        lane = lax.broadcasted_iota(jnp.int32, (1, LANES), 1)
        for h in range(SWA_HEADS):
            d = d_ref[h]
            row = jnp.zeros((1, LANES), F32)
            for b in range(REL_BUCKETS):
                part = jnp.sum(jnp.where(bk == b, d, 0.0), axis=1, keepdims=True)
                row = jnp.where(lane == b, jnp.sum(part, axis=0, keepdims=True), row)
            o_ref[h:h + 1, :] = row

    return pl.pallas_call(
        body, in_specs=[pl.BlockSpec(memory_space=pltpu.VMEM), pl.BlockSpec(memory_space=pltpu.VMEM)],
        out_specs=pl.BlockSpec(memory_space=pltpu.VMEM),
        out_shape=jax.ShapeDtypeStruct((SWA_HEADS, LANES), F32), name="swa_bias_bwd",
        compiler_params=_params())(dbias, buckets)


def _swa_mask(n):
    qi = lax.broadcasted_iota(jnp.int32, (BLOCK, 2 * BLOCK), 0)
    kj = lax.broadcasted_iota(jnp.int32, (BLOCK, 2 * BLOCK), 1)
    dist = BLOCK + qi - kj
    return (dist >= 0) & (dist < WINDOW) & ((n > 0) | (kj >= BLOCK))


def _swa_in_specs():
    q = pl.BlockSpec((BLOCK, SWA_WIDTH), lambda n: (n, P_SQ // SWA_WIDTH))
    kc = pl.BlockSpec((BLOCK, SWA_KVW), lambda n: (n, P_SK // SWA_KVW))
    kp = pl.BlockSpec((BLOCK, SWA_KVW), lambda n: (jnp.maximum(n - 1, 0), P_SK // SWA_KVW))
    vc = pl.BlockSpec((BLOCK, SWA_KVW), lambda n: (n, P_SV // SWA_KVW))
    vp = pl.BlockSpec((BLOCK, SWA_KVW), lambda n: (jnp.maximum(n - 1, 0), P_SV // SWA_KVW))
    band = pl.BlockSpec((None, SWA_HEADS, BLOCK, 2 * BLOCK), lambda n: (jnp.where(n == 0, 1, 0), 0, 0, 0))
    small = [_full((1, SWA_DIM)), _full((1, SWA_DIM)), _full((1, SWA_HEADS)), band]
    return [q, kp, kc, vp, vc] + small


def _swa_load(q_ref, kp_ref, kc_ref, vp_ref, vc_ref, s_ref):
    q = jnp.stack([q_ref[:, pl.ds(h * SWA_DIM, SWA_DIM)] for h in range(SWA_HEADS)])
    kbands, vbands = [], []
    for kv in range(SWA_KV):
        cols = pl.ds(kv * SWA_DIM, SWA_DIM)
        kbands += [jnp.concatenate([kp_ref[:, cols], kc_ref[:, cols]], axis=0)] * SWA_GROUP
        vbands += [jnp.concatenate([vp_ref[:, cols], vc_ref[:, cols]], axis=0)] * SWA_GROUP
    sinks = jnp.stack([s_ref[:, pl.ds(h, 1)] for h in range(SWA_HEADS)])
    return q, jnp.stack(kbands), jnp.stack(vbands), sinks


def _swa_fwd(proj, q_gain, k_gain, sinks, bias):
    S = proj.shape[0]

    def body(q_ref, kp_ref, kc_ref, vp_ref, vc_ref, qg_ref, kg_ref, s_ref, bias_ref, y_ref):
        q, kband, vband, sk = _swa_load(q_ref, kp_ref, kc_ref, vp_ref, vc_ref, s_ref)
        out = _swa_block(q, kband, vband, qg_ref[...], kg_ref[...], sk, bias_ref[...])
        for h in range(SWA_HEADS):
            y_ref[:, pl.ds(h * SWA_DIM, SWA_DIM)] = out[h].astype(BF16)

    return pl.pallas_call(
        body, grid=(S // BLOCK,), in_specs=_swa_in_specs(),
        out_specs=pl.BlockSpec((BLOCK, SWA_WIDTH), lambda n: (n, 0)),
        out_shape=jax.ShapeDtypeStruct((S, SWA_WIDTH), BF16), name="swa_fwd",
        compiler_params=_params(("parallel",)))(proj, proj, proj, proj, proj, q_gain, k_gain, sinks, bias)


def _swa_bwd(proj, q_gain, k_gain, sinks, bias, dy, dproj):
    S = proj.shape[0]

    def body(q_ref, kp_ref, kc_ref, vp_ref, vc_ref, qg_ref, kg_ref, s_ref, bias_ref, dy_ref, _,
             dq_ref, dk_ref, dv_ref, dqg_ref, dkg_ref, ds_ref, dbias_ref):
        n = pl.program_id(0)

        @pl.when(n == 0)
        def _():
            for r in (dk_ref, dv_ref, dqg_ref, dkg_ref, ds_ref, dbias_ref):
                r[...] = jnp.zeros_like(r)

        cur = pl.ds(pl.multiple_of(n * BLOCK, BLOCK), BLOCK)
        prev = pl.ds(pl.multiple_of(jnp.maximum(n - 1, 0) * BLOCK, BLOCK), BLOCK)
        q, kband, vband, sk = _swa_load(q_ref, kp_ref, kc_ref, vp_ref, vc_ref, s_ref)
        _, vjp = jax.vjp(_swa_block, q, kband, vband, qg_ref[...], kg_ref[...], sk, bias_ref[...])
        dy = jnp.stack([dy_ref[:, pl.ds(h * SWA_DIM, SWA_DIM)] for h in range(SWA_HEADS)])
        dq, dkb, dvb, dqg, dkg, dsk, dbs = vjp(dy)
        for h in range(SWA_HEADS):
            dq_ref[:, pl.ds(h * SWA_DIM, SWA_DIM)] = dq[h].astype(BF16)
            ds_ref[:, pl.ds(h, 1)] += dsk[h]
        dbias_ref[...] += dbs
        dqg_ref[...] += dqg
        dkg_ref[...] += dkg
        for kv in range(SWA_KV):
            cols = pl.ds(kv * SWA_DIM, SWA_DIM)
            group = range(kv * SWA_GROUP, (kv + 1) * SWA_GROUP)
            dk_kv = sum(dkb[h] for h in group)
            dv_kv = sum(dvb[h] for h in group)
            dk_ref[cur, cols] += dk_kv[BLOCK:]
            dv_ref[cur, cols] += dv_kv[BLOCK:]

            @pl.when(n > 0)
            def _(cols=cols, dk_kv=dk_kv, dv_kv=dv_kv):
                dk_ref[prev, cols] += dk_kv[:BLOCK]
                dv_ref[prev, cols] += dv_kv[:BLOCK]

    return pl.pallas_call(
        body, grid=(S // BLOCK,),
        in_specs=_swa_in_specs() + [pl.BlockSpec((BLOCK, SWA_WIDTH), lambda n: (n, 0)),
                                    pl.BlockSpec(memory_space=pl.ANY)],
        out_specs=[pl.BlockSpec((BLOCK, SWA_WIDTH), lambda n: (n, P_SQ // SWA_WIDTH)), _full((S, SWA_KVW)),
                   _full((S, SWA_KVW)), _full((1, SWA_DIM)), _full((1, SWA_DIM)), _full((1, SWA_HEADS)),
                   _full((SWA_HEADS, BLOCK, 2 * BLOCK))],
        out_shape=[jax.ShapeDtypeStruct(dproj.shape, dproj.dtype), jax.ShapeDtypeStruct((S, SWA_KVW), F32),
                   jax.ShapeDtypeStruct((S, SWA_KVW), F32), jax.ShapeDtypeStruct((1, SWA_DIM), F32),
                   jax.ShapeDtypeStruct((1, SWA_DIM), F32), jax.ShapeDtypeStruct((1, SWA_HEADS), F32),
                   jax.ShapeDtypeStruct((SWA_HEADS, BLOCK, 2 * BLOCK), F32)],
        input_output_aliases={10: 0},
        name="swa_bwd", compiler_params=_params(("arbitrary",)),
    )(proj, proj, proj, proj, proj, q_gain, k_gain, sinks, bias, dy, dproj)


def _kv_into(dproj, dk, dv, tm=1024):
    S = dk.shape[0]

    def body(dk_ref, dv_ref, _, o_ref):
        o_ref[:, :SWA_KVW] = dk_ref[...].astype(BF16)
        o_ref[:, SWA_KVW:] = dv_ref[...].astype(BF16)

    return pl.pallas_call(
        body, grid=(S // tm,), in_specs=[_row(tm, SWA_KVW), _row(tm, SWA_KVW), pl.BlockSpec(memory_space=pl.ANY)],
        out_specs=_row(tm, 2 * SWA_KVW, P_SK // (2 * SWA_KVW)),
        out_shape=jax.ShapeDtypeStruct(dproj.shape, dproj.dtype), input_output_aliases={2: 0},
        name="swa_kv_into", compiler_params=_params(("parallel",)))(dk, dv, dproj)


def _position():
    return lax.axis_index("x"), lax.axis_index("y"), lax.axis_index("c")


def _all_gather(shards, name="all_gather_weights"):
    na = len(shards)

    def body(*refs):
        x_refs, out_refs = refs[:na], refs[na:2 * na]
        send_sems, recv_sems, local_sems = refs[2 * na:]
        x, y, c = _position()
        me, sibling = (x, y, c), (x, y, 1 - c)
        chips = [(1 - x, y), (x, 1 - y), (1 - x, 1 - y)]

        def copy(a, k, block, to, own=False):
            px, py, pc = block
            slot = out_refs[a].at[4 * px + 2 * py + pc]
            return pltpu.make_async_remote_copy(
                src_ref=x_refs[a] if own else slot, dst_ref=slot, send_sem=send_sems.at[7 * a + k],
                recv_sem=recv_sems.at[7 * a + k], device_id=to, device_id_type=MESH_ID)

        mine = [pltpu.make_async_copy(x_refs[a], out_refs[a].at[4 * x + 2 * y + c], local_sems.at[a])
                for a in range(na)]
        for cp in mine:
            cp.start()
        first = []
        for a in range(na):
            first.append(copy(a, 0, me, sibling, own=True))
            first += [copy(a, 1 + j, me, (*chip, c), own=True) for j, chip in enumerate(chips)]
        for cp in first:
            cp.start()
        passed = []
        for j, chip in enumerate(chips):
            for a in range(na):
                copy(a, 1 + j, (*chip, c), me).wait_recv()
                passed.append(copy(a, 4 + j, (*chip, c), sibling))
                passed[-1].start()
        for a in range(na):
            copy(a, 0, sibling, me).wait_recv()
            for j, chip in enumerate(chips):
                copy(a, 4 + j, (*chip, 1 - c), me).wait_recv()
        for cp in first + passed:
            cp.wait_send()
        for cp in mine:
            cp.wait()

    return pl.pallas_call(
        body, in_specs=[pl.BlockSpec(memory_space=pl.ANY)] * na, out_specs=[pl.BlockSpec(memory_space=pl.ANY)] * na,
        out_shape=[jax.ShapeDtypeStruct((N_DEV,) + s.shape, s.dtype) for s in shards],
        scratch_shapes=[pltpu.SemaphoreType.DMA((7 * na,)), pltpu.SemaphoreType.DMA((7 * na,)),
                        pltpu.SemaphoreType.DMA((na,))],
        name=name)(*shards)


_HBM = pl.BlockSpec(memory_space=pltpu.HBM)
_SEM = pl.BlockSpec(memory_space=pltpu.SEMAPHORE)
_DATAFLOW = pltpu.SideEffectType.DATAFLOW_SIDE_EFFECTING


def _peers(x, y, c):
    out = []
    for k in range(1, N_DEV):
        px, py, pc = x ^ (k >> 2), y ^ ((k >> 1) & 1), c ^ (k & 1)
        out.append(((px, py, pc), 4 * px + 2 * py + pc))
    return out


def _split_copies(src_refs, land_refs, send_sems, recv_sems, scatter):
    x, y, c = _position()
    me = 4 * x + 2 * y + c
    sends, recvs = [], []
    for k, (peer_id, peer) in enumerate(_peers(x, y, c)):
        for a, (src, land) in enumerate(zip(src_refs, land_refs)):
            sems = dict(send_sem=send_sems.at[7 * a + k], recv_sem=recv_sems.at[7 * a + k],
                        device_id=peer_id, device_id_type=MESH_ID)
            mine = src.at[peer] if scatter else src
            sends.append(pltpu.make_async_remote_copy(src_ref=mine, dst_ref=land.at[me], **sems))
            recvs.append(pltpu.make_async_remote_copy(src_ref=mine, dst_ref=land.at[peer], **sems))
    return sends, recvs


def _all_gather_direct(shards, name, after):
    na = len(shards)

    def body(*refs):
        x_refs, out_refs = refs[:na], refs[na + 1:2 * na + 1]
        send_sems, recv_sems, local_sems = refs[2 * na + 1:]
        x, y, c = _position()
        me = 4 * x + 2 * y + c
        local = [pltpu.make_async_copy(x_refs[a], out_refs[a].at[me], local_sems.at[a]) for a in range(na)]
        sends, recvs = _split_copies(x_refs, out_refs, send_sems, recv_sems, False)
        for cp in local + sends:
            cp.start()
        for cp in recvs:
            cp.wait_recv()
        for cp in sends:
            cp.wait_send()
        for cp in local:
            cp.wait()

    return pl.pallas_call(
        body, in_specs=[pl.BlockSpec(memory_space=pl.ANY)] * (na + 1),
        out_specs=[pl.BlockSpec(memory_space=pl.ANY)] * na,
        out_shape=[jax.ShapeDtypeStruct((N_DEV,) + s.shape, s.dtype) for s in shards],
        scratch_shapes=[pltpu.SemaphoreType.DMA((7 * na,)), pltpu.SemaphoreType.DMA((7 * na,)),
                        pltpu.SemaphoreType.DMA((na,))],
        name=name)(*shards, after)


def _exchange_start(srcs, scatter, name, after=None):
    na = len(srcs)
    lands = [lax.empty(s.shape if scatter else (N_DEV,) + s.shape, s.dtype) for s in srcs]
    extra = [] if after is None else [after]

    def body(*refs):
        src_refs, land_refs = refs[:na], refs[na:2 * na]
        send_sems, recv_sems = refs[2 * na + len(extra)], refs[2 * na + len(extra) + 1]
        token = refs[-1]
        sends, _ = _split_copies(src_refs, land_refs, send_sems, recv_sems, scatter)
        for cp in sends:
            cp.start()
        token[...] = jnp.zeros_like(token)

    hbm = lambda a: pltpu.HBM(a.shape, a.dtype)
    out = pl.pallas_call(
        body, name=name,
        out_shape=(pltpu.SemaphoreType.DMA((7 * na,)), pltpu.SemaphoreType.DMA((7 * na,)),
                   *[hbm(s) for s in srcs], *[hbm(l) for l in lands], jax.ShapeDtypeStruct((8, LANES), F32)),
        in_specs=[_HBM] * (2 * na) + [pl.BlockSpec(memory_space=pl.ANY)] * len(extra),
        out_specs=(_SEM, _SEM, *[_HBM] * (2 * na), pl.BlockSpec(memory_space=pltpu.VMEM)),
        input_output_aliases={i: 2 + i for i in range(2 * na)},
        compiler_params=pltpu.CompilerParams(has_side_effects=_DATAFLOW),
    )(*[pltpu.with_memory_space_constraint(s, pltpu.HBM) for s in srcs],
      *[pltpu.with_memory_space_constraint(l, pltpu.HBM) for l in lands], *extra)
    return (out[0], out[1], list(out[2:2 + na]), list(out[2 + na:2 + 2 * na])), out[-1]


def _exchange_wait(handle, after, scatter, name):
    send_sems, recv_sems, srcs, lands = handle
    na = len(srcs)

    def body(*refs):
        src_refs, land_refs = refs[:na], refs[na:2 * na]
        s_sems, r_sems = refs[2 * na], refs[2 * na + 1]
        sends, recvs = _split_copies(src_refs, land_refs, s_sems, r_sems, scatter)
        for cp in sends:
            cp.wait_send()
        for cp in recvs:
            cp.wait_recv()

    hbm = lambda a: pltpu.HBM(a.shape, a.dtype)
    out = pl.pallas_call(
        body, name=name, out_shape=(*[hbm(s) for s in srcs], *[hbm(l) for l in lands]),
        in_specs=[_HBM] * (2 * na) + [_SEM, _SEM, pl.BlockSpec(memory_space=pl.ANY)],
        out_specs=tuple([_HBM] * (2 * na)), input_output_aliases={i: i for i in range(2 * na)},
        compiler_params=pltpu.CompilerParams(has_side_effects=_DATAFLOW),
    )(*srcs, *lands, send_sems, recv_sems, after)
    return list(out[:na]), list(out[na:])


def _own_slot(landed, own):
    me = 4 * lax.axis_index("x") + 2 * lax.axis_index("y") + lax.axis_index("c")
    return lax.dynamic_update_slice_in_dim(landed, own[None], me, axis=0)


def _adam_update(parts, w, m, v, name, tr=256):
    _, r, c = w.shape
    tr = _pick_rows(r, tr)
    cp = parts.shape[2]

    def body(p_ref, w_ref, m_ref, v_ref, g_ref, d_ref, nm_ref, nv_ref):
        g = p_ref[0, :, pl.ds(0, c)].astype(F32)
        for i in range(1, N_DEV):
            g = g + p_ref[i, :, pl.ds(0, c)].astype(F32)
        delta, nm, nv = _adamw(w_ref[0], g, m_ref[0], v_ref[0])
        g_ref[0] = g
        d_ref[0] = delta
        nm_ref[0] = nm
        nv_ref[0] = nv

    rs = pl.BlockSpec((1, tr, c), lambda i: (0, i, 0))
    return pl.pallas_call(
        body, grid=(r // tr,), in_specs=[pl.BlockSpec((N_DEV, tr, cp), lambda i: (0, i, 0)), rs, rs, rs],
        out_specs=[rs] * 4, out_shape=[jax.ShapeDtypeStruct((1, r, c), F32)] * 4, name=name,
        compiler_params=_params(("parallel",)))(parts, w, m, v)


def _pick_rows(rows, target):
    if rows <= target:
        return rows
    t = target
    while t >= 16:
        if rows % t == 0:
            return t
        t -= 16
    return rows


BIG = ("w_in", "w_branch_dn", "w_branch_swa", "w_out", "w_gate", "w_up", "w_down")
IN_SHARD, IN_WIRE = D_IN // N_DEV, 640
FF_SHARD, FF_WIRE = D_FF // N_DEV, 384
D_FFP = N_DEV * FF_WIRE
BIG_SHAPES = {"w_in": ((D_MODEL, IN_SHARD), (D_MODEL, IN_WIRE)),
              "w_branch_dn": ((DN_WIDTH, LANES), (DN_WIDTH, LANES)),
              "w_branch_swa": ((SWA_WIDTH, LANES), (SWA_WIDTH, LANES)),
              "w_out": ((LANES, D_MODEL), (LANES, D_MODEL)),
              "w_gate": ((D_MODEL, FF_SHARD), (D_MODEL, FF_WIRE)),
              "w_up": ((D_MODEL, FF_SHARD), (D_MODEL, FF_WIRE)),
              "w_down": ((FF_SHARD, D_MODEL), (FF_WIRE, D_MODEL))}
CONV_SHARD, CONV_WIRE = (DN_CONV, DN_QKV // N_DEV), (8, 256)


def _pad_to(a, shape):
    return jnp.pad(a, [(0, t - s) for s, t in zip(a.shape, shape)])


_IN_SEGS = ((R_GATE, 2048, P_GATE), (R_QKV, DN_QKV, P_QKV), (R_Z, DN_WIDTH, P_Z), (R_SQ, SWA_WIDTH, P_SQ),
            (R_SK, SWA_KVW, P_SK), (R_SV, SWA_KVW, P_SV), (R_B, 8, P_BA))


def _w_in_from_blocks(blocks):
    parts = []
    for rs, n, _ in _IN_SEGS:
        for dev in range(N_DEV):
            lo, hi = max(rs, IN_SHARD * dev), min(rs + n, IN_SHARD * (dev + 1))
            if lo < hi:
                parts.append(blocks[dev, :, lo - IN_SHARD * dev:hi - IN_SHARD * dev])
    parts.append(jnp.zeros((blocks.shape[1], P_WIDTH - P_BA - 8), blocks.dtype))
    return jnp.concatenate(parts, axis=1)


def _w_in_to_blocks(g):
    out = []
    for dev in range(N_DEV):
        parts = []
        for rs, n, ps in sorted(_IN_SEGS):
            lo, hi = max(rs, IN_SHARD * dev), min(rs + n, IN_SHARD * (dev + 1))
            if lo < hi:
                parts.append(g[:, ps + lo - rs:ps + hi - rs])
        parts.append(jnp.zeros((g.shape[0], IN_WIRE - IN_SHARD), g.dtype))
        out.append(jnp.concatenate(parts, axis=1))
    return jnp.stack(out)


SMALL = {"attn_norm": (0, (1, D_MODEL)), "ffn_norm": (1, (1, D_MODEL)), "dn_out_norm": (2, (1, DN_DIM)),
         "swa_q_norm": (3, (1, SWA_DIM)), "swa_k_norm": (4, (1, SWA_DIM)), "dn_a_log": (5, (1, DN_HEADS)),
         "dn_dt_bias": (6, (1, DN_HEADS)), "swa_sinks": (7, (1, SWA_HEADS)), "rel_bias": (8, (REL_BUCKETS, SWA_HEADS))}
SMALL_SHEET = (48, D_MODEL)


LOSS_ROW = 40


def _small_pack(grads, loss_local):
    names = list(SMALL)

    def body(*refs):
        o_ref = refs[-1]
        o_ref[...] = jnp.zeros_like(o_ref)
        for n, ref in zip(names, refs):
            r0, (nr, nc) = SMALL[n]
            o_ref[r0:r0 + nr, 0:nc] = ref[...]
        o_ref[LOSS_ROW:LOSS_ROW + 1, 0:1] = refs[len(names)][...]

    return pl.pallas_call(
        body, in_specs=[pl.BlockSpec(memory_space=pltpu.VMEM)] * (len(names) + 1),
        out_specs=pl.BlockSpec(memory_space=pltpu.VMEM), out_shape=jax.ShapeDtypeStruct(SMALL_SHEET, F32),
        name="small_pack", compiler_params=_params())(*[grads[n].reshape(SMALL[n][1]) for n in names], loss_local)


def _small_update(sheets, w, m, v):
    names = list(SMALL)
    k = len(names)

    def body(*refs):
        p_ref = refs[0]
        ins, outs = refs[1:1 + 3 * k], refs[1 + 3 * k:]
        loss = p_ref[0, LOSS_ROW:LOSS_ROW + 1, 0:1]
        for i in range(1, N_DEV):
            loss = loss + p_ref[i, LOSS_ROW:LOSS_ROW + 1, 0:1]
        outs[4 * k][...] = loss
        for t, n in enumerate(names):
            r0, (nr, nc) = SMALL[n]
            g = p_ref[0, r0:r0 + nr, 0:nc]
            for i in range(1, N_DEV):
                g = g + p_ref[i, r0:r0 + nr, 0:nc]
            delta, nm, nv = _adamw(ins[t][...], g, ins[k + t][...], ins[2 * k + t][...])
            for kind, val in enumerate((g, delta, nm, nv)):
                outs[kind * k + t][...] = val

    shapes = [jax.ShapeDtypeStruct(SMALL[n][1], F32) for n in names]
    vm = pl.BlockSpec(memory_space=pltpu.VMEM)
    res = pl.pallas_call(
        body, in_specs=[vm] * (1 + 3 * k), out_specs=[vm] * (4 * k + 1),
        out_shape=shapes * 4 + [jax.ShapeDtypeStruct((1, 1), F32)], name="adam_small", compiler_params=_params(),
    )(sheets, *[d[n].reshape(SMALL[n][1]) for d in (w, m, v) for n in names])
    return {n: tuple(res[kind * k + t] for kind in range(4)) for t, n in enumerate(names)}, res[4 * k]


def kernel(x, attn_norm, w_in, dn_conv, dn_a_log, dn_dt_bias, dn_out_norm, swa_q_norm, swa_k_norm, swa_sinks, rel_bias, w_branch_dn, w_branch_swa, w_out, ffn_norm, w_gate, w_up, w_down, loss_target, m_attn_norm, m_w_in, m_dn_conv, m_dn_a_log, m_dn_dt_bias, m_dn_out_norm, m_swa_q_norm, m_swa_k_norm, m_swa_sinks, m_rel_bias, m_w_branch_dn, m_w_branch_swa, m_w_out, m_ffn_norm, m_w_gate, m_w_up, m_w_down, v_attn_norm, v_w_in, v_dn_conv, v_dn_a_log, v_dn_dt_bias, v_dn_out_norm, v_swa_q_norm, v_swa_k_norm, v_swa_sinks, v_rel_bias, v_w_branch_dn, v_w_branch_swa, v_w_out, v_ffn_norm, v_w_gate, v_w_up, v_w_down):
    args = dict(locals())
    S = x.shape[1]
    xs = x.reshape(S, D_MODEL)
    target = loss_target.reshape(S, D_MODEL)

    w_loc = {n: args[n].reshape(BIG_SHAPES[n][0]) for n in BIG}
    conv_loc = dn_conv.reshape(CONV_SHARD)
    wire = {n: _pad_to(w_loc[n], BIG_SHAPES[n][1]).astype(BF16) for n in BIG}
    first = _all_gather([wire["w_in"], _pad_to(conv_loc, CONV_WIRE)])
    later = [n for n in BIG if n != "w_in"]
    rest_handle, rest_token = _exchange_start([wire[n] for n in later], False, "gather_rest_start", after=first[1])
    w_pad = _w_in_from_blocks(first[0])
    conv_w = jnp.concatenate([first[1][d, :DN_CONV, :CONV_SHARD[1]] for d in range(N_DEV)], axis=1)

    h = _norm_fwd(xs, attn_norm + rest_token[0, 0], "norm1_fwd")
    proj = _mm([(h, w_pad)], "nn", F32, "mm_in", 1024, 1664, j_outer=True)
    qkvn = _dn_conv_fwd(proj, conv_w)
    beta, g = _dn_gate_fwd(proj, dn_a_log, dn_dt_bias)
    u, w, qe, kd, qk, egl, tinv = _dn_prep_fwd(qkvn, g, beta)
    o, states = _dn_scan_fwd(u, w, qe, kd, qk, egl)
    y_dn = _dn_out_fwd(o, proj, dn_out_norm)
    bias = _bias_fwd(rel_bias)
    y_swa = _swa_fwd(proj, swa_q_norm, swa_k_norm, swa_sinks, bias)
    rest_src, rest_land = _exchange_wait(rest_handle, y_swa, False, "gather_rest_wait")
    G = {n: _own_slot(land, src) for n, src, land in zip(later, rest_src, rest_land)}
    w_bdn, w_bswa, w_g, w_u = G["w_branch_dn"], G["w_branch_swa"], G["w_gate"], G["w_up"]
    w_o = G["w_out"].reshape(D_MODEL, D_MODEL)
    w_d = G["w_down"].reshape(D_FFP, D_MODEL)
    gates = [(proj, P_GATE // 512), (proj, (P_GATE + D_MODEL) // 512)]
    a_dn, a_swa, merged = _mm_fused(
        [(y_dn, w_bdn), (y_swa, w_bswa)], "nn", "mm_branch_merge", 1024, 512,
        lambda p, e: (p[0], p[1], _merge(e[0], e[1], p[0], p[1])), gates, (F32, F32, BF16), b_blocks=True)

    def resid_norm(p, e):
        x1 = e[0] + p[0]
        return x1, _rms(x1, e[1])

    x1, h2 = _mm_fused([(merged, w_o)], "nn", "mm_out_norm", 512, D_MODEL, resid_norm,
                       [(xs, 0), (ffn_norm, None)], (F32, BF16))
    gate, up, act = _mm_fused([(h2, w_g), (h2, w_u)], "nn", "mm_gate_up_act", 1024, 768,
                              lambda p, e: (p[0], p[1], _act(p[0], p[1])), [], (F32, F32, BF16),
                              j_outer=True, b_blocks=True)

    def loss_head(p, e):
        diff = e[0] + p[0] - e[1]
        dy = diff * (1.0 / D_MODEL)
        part = jnp.sum(jnp.mean(diff * diff, axis=-1, keepdims=True), axis=0, keepdims=True) * 0.5
        return dy, dy, part

    dy, dy_b, loss_local = _mm_fused([(act, w_d)], "nn", "mm_down_loss", 512, D_MODEL, loss_head,
                                     [(x1, 0), (target, 0)], (F32, BF16), sum_shape=(1, 1))

    def act_bwd(p, e):
        _, vjp = jax.vjp(_act, e[0], e[1])
        return vjp(p[0])

    dgate, dup = _mm_fused([(dy_b, w_d)], "nt", "mm_dact_act", 1024, 768, act_bwd, [(gate, 0), (up, 0)],
                           (BF16, BF16), j_outer=True)
    g_w_down = _mm([(act, dy_b)], "tn", BF16, "mm_dw_down", 768, D_MODEL, j_outer=True)
    g_w_down = g_w_down.reshape(N_DEV, FF_WIRE, D_MODEL)
    g_w_gate = _mm([(h2, dgate)], "tn", BF16, "mm_dw_gate", D_MODEL, 768, out_blocks=True)
    g_w_up = _mm([(h2, dup)], "tn", BF16, "mm_dw_up", D_MODEL, 768, out_blocks=True)
    ffn_handle, ffn_token = _exchange_start([g_w_down, g_w_gate, g_w_up], True, "scatter_ffn_start")

    def norm_bwd(p, e):
        _, vjp = jax.vjp(_rms, e[0], e[2])
        dx, dgain = vjp(sum(p))
        dx = dx + e[1]
        return dx, dx, dgain

    dx1, dx1_b, g_ffn_norm = _mm_fused(
        [(dgate, w_g), (dup, w_u)], "nt", "mm_dh2_norm", 256, D_MODEL, norm_bwd,
        [(x1, 0), (dy, 0), (ffn_norm + ffn_token[0, 0], None)], (F32, BF16), b_blocks=True, sum_shape=(1, D_MODEL))
    def merge_bwd(p, e):
        _, vjp = jax.vjp(_merge, *e)
        dg0, dg1, da_dn, da_swa = vjp(p[0])
        return jnp.concatenate([dg0, dg1], axis=1), da_dn, da_swa

    dproj, da_dn, da_swa = _mm_fused(
        [(dx1_b, w_o)], "nt", "mm_dmerged_merge", 512, D_MODEL, merge_bwd,
        [(proj, P_GATE // D_MODEL), (proj, P_GATE // D_MODEL + 1), (a_dn, 0), (a_swa, 0)], (BF16,) * 3,
        wide_first=(P_WIDTH, 2 * D_MODEL))
    g_w_out = _mm([(merged, dx1_b)], "tn", BF16, "mm_dw_out", 512, D_MODEL, j_outer=True)
    g_w_out = g_w_out.reshape(N_DEV, LANES, D_MODEL)
    dy_dn = _mm([(da_dn, w_bdn)], "nt", F32, "mm_dy_dn", 1024, DN_WIDTH, b_blocks=True)
    dy_swa = _mm([(da_swa, w_bswa)], "nt", F32, "mm_dy_swa", 1024, SWA_WIDTH, b_blocks=True)
    g_w_bdn = _mm([(y_dn, da_dn)], "tn", BF16, "mm_dw_branch_dn", DN_WIDTH, 512, out_blocks=True)
    g_w_bswa = _mm([(y_swa, da_swa)], "tn", BF16, "mm_dw_branch_swa", SWA_WIDTH, 512, out_blocks=True)
    dproj, dsk, dsv, g_q_norm, g_k_norm, g_sinks, dbias = _swa_bwd(proj, swa_q_norm, swa_k_norm, swa_sinks, bias,
                                                                   dy_swa, dproj)
    dproj = _kv_into(dproj, dsk, dsv)
    g_rel_bias = _bias_bwd(dbias)[:, :REL_BUCKETS].T
    mix_handle, mix_token = _exchange_start([g_w_out, g_w_bdn, g_w_bswa], True, "scatter_mix_start")
    do, dproj, g_out_norm = _dn_out_bwd(o, proj, dn_out_norm + mix_token[0, 0], dy_dn, dproj)
    du, dw, dqe, dkd, dqk, degl = _dn_scan_bwd(u, w, qe, kd, qk, egl, states, do)
    dqkvn, dgd, dbeta = _dn_prep_bwd(qkvn, g, beta, tinv, du, dw, dqe, dkd, dqk, degl)
    dproj, dal, ddt = _dn_gate_bwd(proj, dn_a_log, dn_dt_bias, dbeta, dgd, dproj)
    g_a_log = dal.reshape(DN_HEADS, DN_DIM).sum(axis=1)
    g_dt_bias = ddt[0, DN_HEADS:2 * DN_HEADS]
    dproj, g_conv = _dn_conv_bwd(proj, conv_w, dqkvn, dproj)
    g_w_in = _w_in_to_blocks(_mm([(h, dproj)], "tn", BF16, "mm_dw_in", 512, 1664, j_outer=True))
    in_handle, in_token = _exchange_start([g_w_in], True, "scatter_in_start")
    dx, g_attn_norm = _mm_fused(
        [(dproj, w_pad)], "nt", "mm_dh_norm", 512, D_MODEL, lambda p, e: norm_bwd(p, e)[1:],
        [(xs, 0), (dx1, 0), (attn_norm + in_token[0, 0], None)], (F32,), sum_shape=(1, D_MODEL))

    g_small = {"attn_norm": g_attn_norm, "ffn_norm": g_ffn_norm, "rel_bias": g_rel_bias, "dn_out_norm": g_out_norm,
               "swa_q_norm": g_q_norm, "swa_k_norm": g_k_norm, "dn_a_log": g_a_log, "dn_dt_bias": g_dt_bias,
               "swa_sinks": g_sinks}
    me = 4 * lax.axis_index("x") + 2 * lax.axis_index("y") + lax.axis_index("c")
    outs = {}

    def finish(handle, group, name, after):
        srcs, lands = _exchange_wait(handle, after, True, name)
        for n, src, land in zip(group, srcs, lands):
            parts = _own_slot(land, lax.dynamic_index_in_dim(src, me, 0, keepdims=False))
            outs[n] = _adam_update(parts, args[n], args["m_" + n], args["v_" + n], "adam_" + n)

    finish(ffn_handle, ("w_down", "w_gate", "w_up"), "scatter_ffn_wait", dx)
    finish(mix_handle, ("w_out", "w_branch_dn", "w_branch_swa"), "scatter_mix_wait", dx)
    sheets, conv_all = _all_gather_direct([_small_pack(g_small, loss_local), _pad_to(g_conv, (8, DN_QKV))],
                                          "all_gather_small",
                                          after=outs["w_up"][0])
    finish(in_handle, ("w_in",), "scatter_in_wait", sheets)
    conv_parts = lax.dynamic_slice(conv_all, (0, 0, me * CONV_SHARD[1]), (N_DEV,) + CONV_SHARD)
    outs["dn_conv"] = _adam_update(conv_parts, dn_conv, m_dn_conv, v_dn_conv, "adam_dn_conv")
    small_outs, loss = _small_update(sheets, {n: args[n] for n in SMALL}, {n: args["m_" + n] for n in SMALL},
                                     {n: args["v_" + n] for n in SMALL})
    outs.update(small_outs)

    names = ("attn_norm", "w_in", "dn_conv", "dn_a_log", "dn_dt_bias", "dn_out_norm", "swa_q_norm", "swa_k_norm",
             "swa_sinks", "rel_bias", "w_branch_dn", "w_branch_swa", "w_out", "ffn_norm", "w_gate", "w_up", "w_down")
    results = []
    for kind in range(4):
        results += [outs[n][kind].reshape(args[n].shape) for n in names]

    return (loss.reshape(()), dx.reshape(x.shape), *results)
```

```python
import math

import numpy as np
import jax
import jax.numpy as jnp
from jax import lax
from jax.experimental import pallas as pl
from jax.experimental.pallas import tpu as pltpu

F32 = jnp.float32
BF16 = jnp.bfloat16
HI = lax.Precision.HIGHEST

D_MODEL = 1024
DN_HEADS = 4
DN_DIM = 128
DN_WIDTH = 512
DN_QKV = 1536
DN_CONV = 4
CHUNK = 64
SWA_HEADS = 8
SWA_KV = 2
SWA_GROUP = 4
SWA_DIM = 64
SWA_WIDTH = 512
SWA_KVW = 128
WINDOW = 128
BLOCK = 128
REL_BUCKETS = 32
REL_MAX_DIST = 128
D_FF = 2816
D_IN = 4872
EPS = 1e-6
N_DEV = 8

ADAM_LR = 0.001
ADAM_B1 = 0.9
ADAM_B2 = 0.999
ADAM_EPS = 1e-08
ADAM_WD = 0.01
ADAM_STEP = 10

P_GATE, P_QKV, P_Z, P_SQ, P_SK, P_SV, P_BA = 0, 2048, 3584, 4096, 4608, 4736, 4864
P_WIDTH = 4992
R_QKV, R_Z, R_B, R_A, R_SQ, R_SK, R_SV, R_GATE = 0, 1536, 2048, 2052, 2056, 2568, 2696, 2824

VMEM_LIMIT = 56 * 1024 * 1024
LANES = 128
MESH_ID = pl.DeviceIdType.MESH


def _params(sem=None):
    return pltpu.CompilerParams(dimension_semantics=sem, vmem_limit_bytes=VMEM_LIMIT)


def _pick(dim, target):
    if dim <= target:
        return dim
    t = target - target % LANES
    while t >= LANES:
        if dim % t == 0:
            return t
        t -= LANES
    return dim


_DIMS = {"nn": (((1,), (0,)), ((), ())), "nt": (((1,), (1,)), ((), ())), "tn": (((0,), (0,)), ((), ()))}


def _tile_product(a_ref, b_ref, mode, b_blocks):
    a = a_ref[...].astype(BF16)
    b = jnp.concatenate([b_ref[d] for d in range(b_ref.shape[0])], axis=1) if b_blocks else b_ref[...]
    return lax.dot_general(a, b.astype(BF16), _DIMS[mode], preferred_element_type=F32)


def _mm(pairs, mode, out_dtype, name, bm, bn, j_outer=False, b_blocks=False, out_blocks=False):
    a0, b0 = pairs[0]
    cb = b0.shape[2] if b_blocks else None
    b_shape = (b0.shape[1], N_DEV * cb) if b_blocks else b0.shape
    if mode == "nn":
        (M, K), (K2, N) = a0.shape, b_shape
    elif mode == "nt":
        (M, K), (N, K2) = a0.shape, b_shape
    else:
        (K, M), (K2, N) = a0.shape, b_shape
    bm, bn = min(bm, M), min(bn, N)
    assert K == K2 and M % bm == 0 and N % bn == 0, (name, a0.shape, b0.shape, bm, bn)
    co = N // N_DEV
    assert not out_blocks or bn % co == 0
    dims = _DIMS[mode]
    n = len(pairs)

    def body(*refs):
        o_ref = refs[2 * n]
        acc = None
        for t in range(n):
            p = _tile_product(refs[2 * t], refs[2 * t + 1], mode, b_blocks)
            acc = p if acc is None else acc + p
        if out_blocks:
            for d in range(bn // co):
                o_ref[d] = acc[:, d * co:(d + 1) * co].astype(out_dtype)
        else:
            o_ref[...] = acc.astype(out_dtype)

    def ij(f):
        return (lambda j, i: f(i, j)) if j_outer else f

    a_spec = pl.BlockSpec((K, bm), ij(lambda i, j: (0, i))) if mode == "tn" else pl.BlockSpec((bm, K), ij(lambda i, j: (i, 0)))
    if b_blocks and mode == "nt":
        b_spec = pl.BlockSpec((N_DEV, bn, cb), ij(lambda i, j: (0, j, 0)))
    elif b_blocks:
        b_spec = pl.BlockSpec((bn // cb, K, cb), ij(lambda i, j: (j, 0, 0)))
    elif mode == "nt":
        b_spec = pl.BlockSpec((bn, K), ij(lambda i, j: (j, 0)))
    else:
        b_spec = pl.BlockSpec((K, bn), ij(lambda i, j: (0, j)))
    if out_blocks:
        out_spec = pl.BlockSpec((bn // co, bm, co), ij(lambda i, j: (j, i, 0)))
        out_shape = jax.ShapeDtypeStruct((N_DEV, M, co), out_dtype)
    else:
        out_spec = pl.BlockSpec((bm, bn), ij(lambda i, j: (i, j)))
        out_shape = jax.ShapeDtypeStruct((M, N), out_dtype)
    grid = (N // bn, M // bm) if j_outer else (M // bm, N // bn)
    return pl.pallas_call(
        body, grid=grid, in_specs=[a_spec, b_spec] * n, out_specs=out_spec, out_shape=out_shape, name=name,
        compiler_params=_params(("parallel", "parallel")),
    )(*[x for pair in pairs for x in pair])


def _mm_fused(pairs, mode, name, bm, bn, epilogue, extras, out_dtypes, j_outer=False, b_blocks=False,
              sum_shape=None, wide_first=None):
    a0, b0 = pairs[0]
    cb = b0.shape[2] if b_blocks else None
    b_shape = (b0.shape[1], N_DEV * cb) if b_blocks else b0.shape
    if mode == "nn":
        (M, K), (K2, N) = a0.shape, b_shape
    else:
        (M, K), (N, K2) = a0.shape, b_shape
    bm, bn = min(bm, M), min(bn, N)
    assert mode in ("nn", "nt") and K == K2 and M % bm == 0 and N % bn == 0, (name, a0.shape, b0.shape)
    dims = _DIMS[mode]
    n, ne, no = len(pairs), len(extras), len(out_dtypes)

    def body(*refs):
        prods = [_tile_product(refs[2 * t], refs[2 * t + 1], mode, b_blocks) for t in range(n)]
        results = epilogue(prods, [r[...] for r in refs[2 * n:2 * n + ne]])
        out_refs = refs[2 * n + ne:]
        for o_ref, val, dt in zip(out_refs, results, out_dtypes):
            o_ref[...] = val.astype(dt)
        if sum_shape is not None:
            s_ref = out_refs[no]

            @pl.when((pl.program_id(0) == 0) & (pl.program_id(1) == 0))
            def _():
                s_ref[...] = jnp.zeros_like(s_ref)

            s_ref[...] += results[no]

    def ij(f):
        return (lambda j, i: f(i, j)) if j_outer else f

    a_spec = pl.BlockSpec((bm, K), ij(lambda i, j: (i, 0)))
    once = dict(pipeline_mode=pl.Buffered(1)) if bn == N else {}
    if b_blocks and mode == "nt":
        b_spec = pl.BlockSpec((N_DEV, bn, cb), ij(lambda i, j: (0, j, 0)), **once)
    elif b_blocks:
        b_spec = pl.BlockSpec((bn // cb, K, cb), ij(lambda i, j: (j, 0, 0)), **once)
    elif mode == "nt":
        b_spec = pl.BlockSpec((bn, K), ij(lambda i, j: (j, 0)), **once)
    else:
        b_spec = pl.BlockSpec((K, bn), ij(lambda i, j: (0, j)), **once)
    e_specs = [pl.BlockSpec((1, bn), ij(lambda i, j: (0, j))) if first is None
               else pl.BlockSpec((bm, bn), ij(lambda i, j, first=first: (i, first + j))) for _, first in extras]
    tile = pl.BlockSpec((bm, bn), ij(lambda i, j: (i, j)))
    out_specs = [tile] * no
    out_shape = [jax.ShapeDtypeStruct((M, N), dt) for dt in out_dtypes]
    if wide_first is not None:
        assert bn == N
        out_specs[0] = pl.BlockSpec((bm, wide_first[1]), ij(lambda i, j: (i, 0)))
        out_shape[0] = jax.ShapeDtypeStruct((M, wide_first[0]), out_dtypes[0])
    if sum_shape is not None:
        assert sum_shape[1] in (1, bn) and (sum_shape[1] == 1 or bn == N)
        out_specs.append(_full(sum_shape))
        out_shape.append(jax.ShapeDtypeStruct(sum_shape, F32))
    grid = (N // bn, M // bm) if j_outer else (M // bm, N // bn)
    sem = ("arbitrary", "arbitrary") if sum_shape is not None else ("parallel", "parallel")
    return pl.pallas_call(
        body, grid=grid, in_specs=[a_spec, b_spec] * n + e_specs, out_specs=out_specs, out_shape=out_shape,
        name=name, compiler_params=_params(sem),
    )(*[x for pair in pairs for x in pair], *[arr for arr, _ in extras])


def _rms(x, gain):
    return x * lax.rsqrt(jnp.mean(x * x, axis=-1, keepdims=True) + EPS) * gain


def _silu(x):
    return x * jax.nn.sigmoid(x)


def _act(g, u):
    return _silu(g) * u


def _merge(g0, g1, a_dn, a_swa):
    return jax.nn.sigmoid(g0) * a_dn + jax.nn.sigmoid(g1) * a_swa


def _dn_post(c, is_v, q_scale):
    a = _silu(c)
    rs = lax.rsqrt(jnp.sum(a * a, axis=-1, keepdims=True) + EPS) * q_scale
    return a * jnp.where(is_v, 1.0, rs)


def _dn_out(o, z, gain):
    return _rms(o, gain) * _silu(z)


def _dot(a, b, dims=_DIMS["nn"], hi=False):
    if a.ndim == 3 or b.ndim == 3:
        batch = a.shape[0] if a.ndim == 3 else b.shape[0]
        a = a if a.ndim == 3 else jnp.broadcast_to(a, (batch,) + a.shape)
        b = b if b.ndim == 3 else jnp.broadcast_to(b, (batch,) + b.shape)
        ((ca,), (cb,)), _ = dims
        dims = (((ca + 1,), (cb + 1,)), ((0,), (0,)))
    if hi:
        return lax.dot_general(a, b, dims, precision=HI, preferred_element_type=F32)
    return lax.dot_general(a.astype(BF16), b.astype(BF16), dims, preferred_element_type=F32)


def _pieces(x):
    hi = x.astype(BF16)
    r1 = x - hi.astype(F32)
    mid = r1.astype(BF16)
    return hi, mid, (r1 - mid.astype(F32)).astype(BF16)


def _sel_left_impl(m, x):
    mb = m.astype(BF16)
    hi, mid, lo = _pieces(x)
    return _dot(mb, hi) + (_dot(mb, mid) + _dot(mb, lo))


@jax.custom_vjp
def _sel_left(m, mt, x):
    return _sel_left_impl(m, x)


_sel_left.defvjp(lambda m, mt, x: (_sel_left_impl(m, x), (m, mt)),
                 lambda res, ct: (jnp.zeros_like(res[0]), jnp.zeros_like(res[1]), _sel_left_impl(res[1], ct)))


def _sel_right_impl(x, s):
    sb = s.astype(BF16)
    hi, mid, lo = _pieces(x)
    return _dot(hi, sb) + (_dot(mid, sb) + _dot(lo, sb))


@jax.custom_vjp
def _sel_right(x, s, st):
    return _sel_right_impl(x, s)


_sel_right.defvjp(lambda x, s, st: (_sel_right_impl(x, s), (s, st)),
                  lambda res, ct: (_sel_right_impl(ct, res[1]), jnp.zeros_like(res[0]), jnp.zeros_like(res[1])))


def _dot3_impl(a, b):
    a_hi, a_lo, _ = _pieces(a)
    b_hi, b_lo, _ = _pieces(b)
    return _dot(a_hi, b_hi) + (_dot(a_hi, b_lo) + _dot(a_lo, b_hi))


@jax.custom_vjp
def _dot3(a, b):
    return _dot3_impl(a, b)


_dot3.defvjp(lambda a, b: (_dot3_impl(a, b), (a, b)),
             lambda res, ct: (_dot(ct, res[1], _DIMS["nt"]), _dot(res[0], ct, _DIMS["tn"])))


def _inv_impl(a, eye, strict):
    t = eye - a
    p = _dot(a, a)
    for level in range(5):
        t = t + _dot(t, p)
        if level < 4:
            p = _dot(p, p)
    t = t + _dot(t, eye - t - _dot3_impl(a, t))
    return jnp.where(strict > 0.5, t, eye)


@jax.custom_vjp
def _inv_given(a, t):
    return t.astype(F32)


_inv_given.defvjp(lambda a, t: (t.astype(F32), t),
                  lambda t, ct: (-_dot(_dot(t, ct, _DIMS["tn"]), t, _DIMS["nt"]), jnp.zeros_like(t)))


@jax.custom_vjp
def _lanes_join(a, b):
    return jnp.concatenate([a, b], axis=-1)


_lanes_join.defvjp(lambda a, b: (jnp.concatenate([a, b], axis=-1), None),
                   lambda _, ct: (ct[..., :ct.shape[-1] // 2], ct[..., ct.shape[-1] // 2:]))


@jax.custom_vjp
def _lanes_halves(y):
    h = y.shape[-1] // 2
    return y[..., :h], y[..., h:]


_lanes_halves.defvjp(lambda y: ((y[..., :y.shape[-1] // 2], y[..., y.shape[-1] // 2:]), None),
                     lambda _, ct: (jnp.concatenate(ct, axis=-1),))

GROUP = 4
GROUP_ROWS = GROUP * CHUNK


def _block_consts(n):
    ii = lax.broadcasted_iota(jnp.int32, (n, n), 0)
    jj = lax.broadcasted_iota(jnp.int32, (n, n), 1)
    shift = CHUNK.bit_length() - 1
    same = jnp.right_shift(ii, shift) == jnp.right_shift(jj, shift)
    return same & (ii >= jj), same & (ii <= jj), same & (ii > jj), same, ii == jj


def _lane0(n):
    s = (lax.broadcasted_iota(jnp.int32, (LANES, n), 0) == 0).astype(F32)
    st = (lax.broadcasted_iota(jnp.int32, (n, LANES), 1) == 0).astype(F32)
    return s, st


def _dn_group(q, k, v, g, beta, t_saved=None):
    n = GROUP_ROWS
    low_b, upp_b, strict_b, _, eye_b = _block_consts(n)
    low, upp, eye = low_b.astype(F32), upp_b.astype(F32), eye_b.astype(F32)
    gc = _sel_left(low, upp, g)
    per_chunk = (g.shape[0], GROUP, CHUNK, LANES)
    g_last = jnp.sum(g.reshape(per_chunk), axis=2, keepdims=True)
    gl = jnp.broadcast_to(g_last, per_chunk).reshape(g.shape)
    s, st = _lane0(n)
    col = _sel_right(gc, s, st)
    row = jnp.swapaxes(col, 1, 2)
    decay = jnp.exp(jnp.where(low_b, col - row, -jnp.inf))
    kb = k * beta
    vb = v * beta
    a = jnp.where(strict_b, _dot(kb, k, _DIMS["nt"]) * decay, 0.0)
    t = _inv_impl(a, eye, strict_b.astype(F32)) if t_saved is None else _inv_given(a, t_saved)
    u, w = _lanes_halves(_dot3(t, _lanes_join(vb, kb * jnp.exp(gc))))
    fold = (jnp.bitwise_and(lax.broadcasted_iota(jnp.int32, (n, CHUNK), 0), CHUNK - 1)
            == lax.broadcasted_iota(jnp.int32, (n, CHUNK), 1)).astype(F32)
    fold_t = (jnp.bitwise_and(lax.broadcasted_iota(jnp.int32, (CHUNK, n), 1), CHUNK - 1)
              == lax.broadcasted_iota(jnp.int32, (CHUNK, n), 0)).astype(F32)
    qk = _sel_right(_dot(q, k, _DIMS["nt"]) * decay, fold, fold_t)
    return u, w, q * jnp.exp(gc), k * jnp.exp(gl - gc), qk, jnp.exp(g_last), t


def _dn_step(s, u, w, qe, kd, qk, egl):
    v_new = u - _dot(w, s)
    o = _dot(qe, s) + _dot(qk, v_new)
    s_new = s * egl + _dot(kd, v_new, _DIMS["tn"])
    return s_new, o


def _swa_block(q, kband, vband, qg, kg, sinks, band):
    kn = _rms(kband, kg)
    qn = _rms(q, qg) * (SWA_DIM ** -0.5)
    logits = _dot(qn, kn, _DIMS["nt"]) + band
    m = lax.stop_gradient(jnp.maximum(jnp.max(logits, axis=-1, keepdims=True), sinks))
    p = jnp.exp(logits - m)
    denom = jnp.sum(p, axis=-1, keepdims=True) + jnp.exp(sinks - m)
    return _dot(p * (1.0 / denom), vband)


def _adamw(w, g, m, v):
    m = ADAM_B1 * m + (1.0 - ADAM_B1) * g
    v = ADAM_B2 * v + (1.0 - ADAM_B2) * jnp.square(g)
    m_hat = m / (1.0 - ADAM_B1 ** ADAM_STEP)
    v_hat = v / (1.0 - ADAM_B2 ** ADAM_STEP)
    delta = -ADAM_LR * (m_hat / (jnp.sqrt(v_hat) + ADAM_EPS) + ADAM_WD * w)
    return delta, m, v


def _row(tm, c, cb=0):
    return pl.BlockSpec((tm, c), lambda i, cb=cb: (i, cb))


def _full(shape):
    nd = len(shape)
    return pl.BlockSpec(shape, lambda *_, nd=nd: (0,) * nd)


def _norm_fwd(x, gain, name, tm=1024):
    S = x.shape[0]

    def body(x_ref, g_ref, h_ref):
        h_ref[...] = _rms(x_ref[...], g_ref[...]).astype(BF16)

    return pl.pallas_call(
        body, grid=(S // tm,), in_specs=[_row(tm, D_MODEL), _full((1, D_MODEL))],
        out_specs=_row(tm, D_MODEL), out_shape=jax.ShapeDtypeStruct((S, D_MODEL), BF16),
        name=name, compiler_params=_params(("parallel",)))(x, gain)


def _shift_down(x, s):
    row = lax.broadcasted_iota(jnp.int32, x.shape, 0)
    return jnp.where(row >= s, pltpu.roll(x, s, axis=0), 0.0)


def _shift_up(x, s):
    n = x.shape[0]
    row = lax.broadcasted_iota(jnp.int32, x.shape, 0)
    return jnp.where(row < n - s, pltpu.roll(x, n - s, axis=0), 0.0)


def _conv(x, w):
    out = w[DN_CONV - 1:DN_CONV] * x
    for s in range(1, DN_CONV):
        out = out + w[DN_CONV - 1 - s:DN_CONV - s] * _shift_down(x, s)
    return out


def _dn_conv_fwd(proj, conv_w):
    S = proj.shape[0]
    nb = DN_QKV // LANES

    def body(x_ref, w_ref, o_ref):
        j = pl.program_id(0)
        q_scale = jnp.where(j < DN_HEADS, DN_DIM ** -0.5, 1.0).astype(F32)
        o_ref[...] = _dn_post(_conv(x_ref[...], w_ref[...]), j >= 2 * DN_HEADS, q_scale)

    return pl.pallas_call(
        body, grid=(nb,),
        in_specs=[pl.BlockSpec((S, LANES), lambda j: (0, P_QKV // LANES + j)),
                  pl.BlockSpec((DN_CONV, LANES), lambda j: (0, j))],
        out_specs=pl.BlockSpec((S, LANES), lambda j: (0, j)),
        out_shape=jax.ShapeDtypeStruct((S, DN_QKV), F32), name="dn_conv_fwd",
        compiler_params=_params(("parallel",)))(proj, conv_w)


def _dn_conv_bwd(proj, conv_w, dqkvn, dproj):
    S = proj.shape[0]
    nb = DN_QKV // LANES

    def body(x_ref, w_ref, d_ref, _, dx_ref, dw_ref):
        j = pl.program_id(0)
        q_scale = jnp.where(j < DN_HEADS, DN_DIM ** -0.5, 1.0).astype(F32)
        x = x_ref[...]
        w = w_ref[...]
        _, vjp = jax.vjp(lambda c: _dn_post(c, j >= 2 * DN_HEADS, q_scale), _conv(x, w))
        (dc,) = vjp(d_ref[0])
        dx = w[DN_CONV - 1:DN_CONV] * dc
        dw_ref[DN_CONV - 1:DN_CONV, :] = jnp.sum(dc * x, axis=0, keepdims=True)
        for s in range(1, DN_CONV):
            dx = dx + w[DN_CONV - 1 - s:DN_CONV - s] * _shift_up(dc, s)
            dw_ref[DN_CONV - 1 - s:DN_CONV - s, :] = jnp.sum(dc * _shift_down(x, s), axis=0, keepdims=True)
        dx_ref[...] = dx.astype(BF16)

    return pl.pallas_call(
        body, grid=(nb,),
        in_specs=[pl.BlockSpec((S, LANES), lambda j: (0, P_QKV // LANES + j)),
                  pl.BlockSpec((DN_CONV, LANES), lambda j: (0, j)),
                  pl.BlockSpec((1, S, LANES), lambda j: (lax.div(j, DN_HEADS), 0, lax.rem(j, DN_HEADS))),
                  pl.BlockSpec(memory_space=pl.ANY)],
        out_specs=[pl.BlockSpec((S, LANES), lambda j: (0, P_QKV // LANES + j)),
                   pl.BlockSpec((DN_CONV, LANES), lambda j: (0, j))],
        out_shape=[jax.ShapeDtypeStruct(dproj.shape, dproj.dtype), jax.ShapeDtypeStruct((DN_CONV, DN_QKV), F32)],
        input_output_aliases={3: 0},
        name="dn_conv_bwd", compiler_params=_params(("parallel",)))(proj, conv_w, dqkvn, dproj)


def _expanders():
    eb = np.zeros((LANES, DN_WIDTH), np.float32)
    ea = np.zeros((LANES, DN_WIDTH), np.float32)
    for h in range(DN_HEADS):
        eb[h, h * DN_DIM:(h + 1) * DN_DIM] = 1.0
        ea[DN_HEADS + h, h * DN_DIM:(h + 1) * DN_DIM] = 1.0
    return jnp.asarray(eb), jnp.asarray(ea), jnp.asarray(eb.T), jnp.asarray(ea.T)


def _dn_gate_args(a_log, dt_bias):
    alog = jnp.repeat(a_log.reshape(1, DN_HEADS), DN_DIM, axis=1)
    dtb = _pad_to(jnp.pad(dt_bias.reshape(1, DN_HEADS), ((0, 0), (DN_HEADS, 0))), (1, LANES))
    return _expanders() + (alog, dtb)


def _dn_gate_specs(tm):
    return [_row(tm, LANES, P_BA // LANES), _full((LANES, DN_WIDTH)), _full((LANES, DN_WIDTH)),
            _full((DN_WIDTH, LANES)), _full((DN_WIDTH, LANES)), _full((1, DN_WIDTH)), _full((1, LANES))]


def _dn_gate_fn(ba, eb, ea, ebt, eat, alog, dtb):
    beta = _sel_right(jax.nn.sigmoid(ba), eb, ebt)
    g = -jnp.exp(alog) * _sel_right(jax.nn.softplus(ba + dtb), ea, eat)
    return beta, g


def _dn_gate_fwd(proj, a_log, dt_bias, tm=1024):
    S = proj.shape[0]
    args = _dn_gate_args(a_log, dt_bias)

    def body(ba_ref, eb_ref, ea_ref, ebt_ref, eat_ref, al_ref, dt_ref, beta_ref, g_ref):
        beta, g = _dn_gate_fn(ba_ref[...], eb_ref[...], ea_ref[...], ebt_ref[...], eat_ref[...], al_ref[...],
                              dt_ref[...])
        beta_ref[...] = beta
        g_ref[...] = g

    return pl.pallas_call(
        body, grid=(S // tm,), in_specs=_dn_gate_specs(tm), out_specs=[_row(tm, DN_WIDTH), _row(tm, DN_WIDTH)],
        out_shape=[jax.ShapeDtypeStruct((S, DN_WIDTH), F32), jax.ShapeDtypeStruct((S, DN_WIDTH), F32)],
        name="dn_gate_fwd", compiler_params=_params(("parallel",)))(proj, *args)


def _dn_gate_bwd(proj, a_log, dt_bias, dbeta, dg, dproj, tm=1024):
    S = proj.shape[0]
    args = _dn_gate_args(a_log, dt_bias)

    def body(ba_ref, eb_ref, ea_ref, ebt_ref, eat_ref, al_ref, dt_ref, dbeta_ref, dg_ref, _, dba_ref, dal_ref,
             ddt_ref):
        eb, ea, ebt, eat = eb_ref[...], ea_ref[...], ebt_ref[...], eat_ref[...]
        _, vjp = jax.vjp(lambda ba, al, dt: _dn_gate_fn(ba, eb, ea, ebt, eat, al, dt), ba_ref[...], al_ref[...],
                         dt_ref[...])
        dba, dal, ddt = vjp((dbeta_ref[...], dg_ref[...]))
        dba_ref[...] = dba.astype(BF16)

        @pl.when(pl.program_id(0) == 0)
        def _():
            dal_ref[...] = jnp.zeros_like(dal_ref)
            ddt_ref[...] = jnp.zeros_like(ddt_ref)

        dal_ref[...] += dal
        ddt_ref[...] += ddt

    return pl.pallas_call(
        body, grid=(S // tm,),
        in_specs=_dn_gate_specs(tm) + [_row(tm, DN_WIDTH), _row(tm, DN_WIDTH), pl.BlockSpec(memory_space=pl.ANY)],
        out_specs=[_row(tm, LANES, P_BA // LANES), _full((1, DN_WIDTH)), _full((1, LANES))],
        out_shape=[jax.ShapeDtypeStruct(dproj.shape, dproj.dtype), jax.ShapeDtypeStruct((1, DN_WIDTH), F32),
                   jax.ShapeDtypeStruct((1, LANES), F32)],
        input_output_aliases={len(args) + 3: 0},
        name="dn_gate_bwd", compiler_params=_params(("arbitrary",)))(proj, *args, dbeta, dg, dproj)


PREP_GROUPS = 8
PREP_CHUNKS = GROUP * PREP_GROUPS


def _dn_prep_specs():
    rows = PREP_CHUNKS * CHUNK
    q = pl.BlockSpec((rows, LANES), lambda h, c: (c, h))
    k = pl.BlockSpec((rows, LANES), lambda h, c: (c, DN_HEADS + h))
    v = pl.BlockSpec((rows, LANES), lambda h, c: (c, 2 * DN_HEADS + h))
    qk = pl.BlockSpec((1, rows, CHUNK), lambda h, c: (h, c, 0))
    egl = pl.BlockSpec((1, PREP_CHUNKS, 1, LANES), lambda h, c: (h, c, 0, 0))
    return q, k, v, qk, egl


def _dn_prep_fwd(qkvn, g, beta):
    S = qkvn.shape[0]
    nc = S // CHUNK
    q, k, v, qks, egl = _dn_prep_specs()

    def body(q_ref, k_ref, v_ref, g_ref, b_ref, u_ref, w_ref, qe_ref, kd_ref, qk_ref, egl_ref, t_ref):
        rows = PREP_CHUNKS * CHUNK
        grp = (PREP_GROUPS, GROUP_ROWS, LANES)
        u, w, qe, kd, qk, e, t = _dn_group(q_ref[...].reshape(grp), k_ref[...].reshape(grp), v_ref[...].reshape(grp),
                                           g_ref[...].reshape(grp), b_ref[...].reshape(grp))
        u_ref[...] = u.reshape(rows, LANES)
        w_ref[...] = w.reshape(rows, LANES)
        qe_ref[...] = qe.reshape(rows, LANES)
        kd_ref[...] = kd.reshape(rows, LANES)
        t_ref[0] = t.reshape(rows, GROUP_ROWS).astype(BF16)
        qk_ref[0] = qk.reshape(rows, CHUNK)
        egl_ref[0] = e.reshape(PREP_CHUNKS, 1, LANES)

    wide = jax.ShapeDtypeStruct((S, DN_WIDTH), F32)
    return pl.pallas_call(
        body, grid=(DN_HEADS, nc // PREP_CHUNKS), in_specs=[q, k, v, q, q],
        out_specs=[q, q, q, q, qks, egl, _dn_tinv_spec()],
        out_shape=[wide, wide, wide, wide, jax.ShapeDtypeStruct((DN_HEADS, S, CHUNK), F32),
                   jax.ShapeDtypeStruct((DN_HEADS, nc, 1, LANES), F32),
                   jax.ShapeDtypeStruct((DN_HEADS, S, GROUP_ROWS), BF16)],
        name="dn_prep_fwd", compiler_params=_params(("parallel", "parallel")))(qkvn, qkvn, qkvn, g, beta)


def _dn_tinv_spec():
    return pl.BlockSpec((1, PREP_CHUNKS * CHUNK, GROUP_ROWS), lambda h, c: (h, c, 0))


def _dn_prep_bwd(qkvn, g, beta, tinv, du, dw, dqe, dkd, dqk, degl):
    S = qkvn.shape[0]
    nc = S // CHUNK
    q, k, v, qks, egl = _dn_prep_specs()

    def body(q_ref, k_ref, v_ref, g_ref, b_ref, t_ref, du_ref, dw_ref, dqe_ref, dkd_ref, dqk_ref, degl_ref,
             dqkv_ref, dg_ref, db_ref):
        rows = PREP_CHUNKS * CHUNK
        grp = (PREP_GROUPS, GROUP_ROWS, LANES)
        t_saved = t_ref[0].reshape(PREP_GROUPS, GROUP_ROWS, GROUP_ROWS)
        _, vjp = jax.vjp(lambda *x: _dn_group(*x, t_saved=t_saved)[:6], q_ref[...].reshape(grp),
                         k_ref[...].reshape(grp), v_ref[...].reshape(grp), g_ref[...].reshape(grp),
                         b_ref[...].reshape(grp))
        dq, dk, dv, dg, db = vjp((du_ref[...].reshape(grp), dw_ref[...].reshape(grp), dqe_ref[...].reshape(grp),
                                  dkd_ref[...].reshape(grp), dqk_ref[0].reshape(PREP_GROUPS, GROUP_ROWS, CHUNK),
                                  degl_ref[0].reshape(PREP_GROUPS, GROUP, 1, LANES)))
        dqkv_ref[0] = dq.reshape(rows, LANES)
        dqkv_ref[1] = dk.reshape(rows, LANES)
        dqkv_ref[2] = dv.reshape(rows, LANES)
        dg_ref[...] = dg.reshape(rows, LANES)
        db_ref[...] = db.reshape(rows, LANES)

    wide = jax.ShapeDtypeStruct((S, DN_WIDTH), F32)
    rows = PREP_CHUNKS * CHUNK
    return pl.pallas_call(
        body, grid=(DN_HEADS, nc // PREP_CHUNKS), in_specs=[q, k, v, q, q, _dn_tinv_spec(), q, q, q, q, qks, egl],
        out_specs=[pl.BlockSpec((3, rows, LANES), lambda h, c: (0, c, h)), q, q],
        out_shape=[jax.ShapeDtypeStruct((3, S, DN_WIDTH), F32), wide, wide],
        name="dn_prep_bwd", compiler_params=_params(("parallel", "parallel")),
    )(qkvn, qkvn, qkvn, g, beta, tinv, du, dw, dqe, dkd, dqk, degl)


SCAN_CHUNKS = 16


def _dn_scan_specs(nc, reverse):
    nb = nc // SCAN_CHUNKS

    def cidx(c):
        return nb - 1 - c if reverse else c

    hc = pl.BlockSpec((SCAN_CHUNKS * CHUNK, DN_WIDTH), lambda c: (cidx(c), 0))
    qk = pl.BlockSpec((DN_HEADS, SCAN_CHUNKS * CHUNK, CHUNK), lambda c: (0, cidx(c), 0))
    egl = pl.BlockSpec((DN_HEADS, SCAN_CHUNKS, 1, LANES), lambda c: (0, cidx(c), 0, 0))
    st = pl.BlockSpec((DN_HEADS, SCAN_CHUNKS, DN_DIM, DN_DIM), lambda c: (0, cidx(c), 0, 0))
    return hc, qk, egl, st


def _heads(ref, i):
    return jnp.stack([ref[pl.ds(i * CHUNK, CHUNK), pl.ds(h * DN_DIM, DN_DIM)] for h in range(DN_HEADS)])


def _dn_scan_fwd(u, w, qe, kd, qk, egl):
    S = u.shape[0]
    nc = S // CHUNK
    hc, qks, egls, st = _dn_scan_specs(nc, False)

    def body(u_ref, w_ref, qe_ref, kd_ref, qk_ref, egl_ref, o_ref, st_ref, s_scr):
        @pl.when(pl.program_id(0) == 0)
        def _():
            s_scr[...] = jnp.zeros_like(s_scr)

        s = s_scr[...]
        for i in range(SCAN_CHUNKS):
            rows = pl.ds(i * CHUNK, CHUNK)
            st_ref[:, i] = s
            s, o = _dn_step(s, _heads(u_ref, i), _heads(w_ref, i), _heads(qe_ref, i), _heads(kd_ref, i),
                            qk_ref[:, rows, :], egl_ref[:, i])
            for h in range(DN_HEADS):
                o_ref[rows, pl.ds(h * DN_DIM, DN_DIM)] = o[h]
        s_scr[...] = s

    return pl.pallas_call(
        body, grid=(nc // SCAN_CHUNKS,), in_specs=[hc, hc, hc, hc, qks, egls], out_specs=[hc, st],
        out_shape=[jax.ShapeDtypeStruct((S, DN_WIDTH), F32), jax.ShapeDtypeStruct((DN_HEADS, nc, DN_DIM, DN_DIM), F32)],
        scratch_shapes=[pltpu.VMEM((DN_HEADS, DN_DIM, DN_DIM), F32)], name="dn_scan_fwd",
        compiler_params=_params(("arbitrary",)))(u, w, qe, kd, qk, egl)


def _dn_scan_bwd(u, w, qe, kd, qk, egl, states, do):
    S = u.shape[0]
    nc = S // CHUNK
    hc, qks, egls, st = _dn_scan_specs(nc, True)

    def body(u_ref, w_ref, qe_ref, kd_ref, qk_ref, egl_ref, st_ref, do_ref,
             du_ref, dw_ref, dqe_ref, dkd_ref, dqk_ref, degl_ref, ds_scr):
        @pl.when(pl.program_id(0) == 0)
        def _():
            ds_scr[...] = jnp.zeros_like(ds_scr)

        ds = ds_scr[...]
        for i in reversed(range(SCAN_CHUNKS)):
            rows = pl.ds(i * CHUNK, CHUNK)
            _, vjp = jax.vjp(_dn_step, st_ref[:, i], _heads(u_ref, i), _heads(w_ref, i), _heads(qe_ref, i),
                             _heads(kd_ref, i), qk_ref[:, rows, :], egl_ref[:, i])
            ds, du, dw, dqe, dkd, dqk, degl = vjp((ds, _heads(do_ref, i)))
            dqk_ref[:, rows, :] = dqk
            degl_ref[:, i] = degl
            for h in range(DN_HEADS):
                cols = pl.ds(h * DN_DIM, DN_DIM)
                du_ref[rows, cols] = du[h]
                dw_ref[rows, cols] = dw[h]
                dqe_ref[rows, cols] = dqe[h]
                dkd_ref[rows, cols] = dkd[h]
        ds_scr[...] = ds

    wide = jax.ShapeDtypeStruct((S, DN_WIDTH), F32)
    return pl.pallas_call(
        body, grid=(nc // SCAN_CHUNKS,), in_specs=[hc, hc, hc, hc, qks, egls, st, hc],
        out_specs=[hc, hc, hc, hc, qks, egls],
        out_shape=[wide, wide, wide, wide, jax.ShapeDtypeStruct((DN_HEADS, S, CHUNK), F32),
                   jax.ShapeDtypeStruct((DN_HEADS, nc, 1, LANES), F32)],
        scratch_shapes=[pltpu.VMEM((DN_HEADS, DN_DIM, DN_DIM), F32)], name="dn_scan_bwd",
        compiler_params=_params(("arbitrary",)))(u, w, qe, kd, qk, egl, states, do)


def _dn_out_fwd(o, proj, gain, tm=1024):
    S = o.shape[0]

    def body(o_ref, z_ref, g_ref, y_ref):
        y_ref[...] = _dn_out(o_ref[...], z_ref[...], g_ref[...]).astype(BF16)

    hs = pl.BlockSpec((tm, LANES), lambda i, h: (i, h))
    zs = pl.BlockSpec((tm, LANES), lambda i, h: (i, P_Z // LANES + h))
    return pl.pallas_call(
        body, grid=(S // tm, DN_HEADS), in_specs=[hs, zs, _full((1, DN_DIM))], out_specs=hs,
        out_shape=jax.ShapeDtypeStruct((S, DN_WIDTH), BF16), name="dn_out_fwd",
        compiler_params=_params(("parallel", "parallel")))(o, proj, gain)


_ANY = pl.BlockSpec(memory_space=pl.ANY)


def _dn_out_bwd(o, proj, gain, dy, dproj, tm=1024):
    S = o.shape[0]

    def body(o_ref, z_ref, g_ref, dy_ref, _, do_ref, dz_ref, dg_ref):
        _, vjp = jax.vjp(_dn_out, o_ref[...], z_ref[...], g_ref[...])
        do, dz, dg = vjp(dy_ref[...])
        do_ref[...] = do
        dz_ref[...] = dz.astype(BF16)

        @pl.when((pl.program_id(0) == 0) & (pl.program_id(1) == 0))
        def _():
            dg_ref[...] = jnp.zeros_like(dg_ref)

        dg_ref[...] += dg

    hs = pl.BlockSpec((tm, LANES), lambda i, h: (i, h))
    zs = pl.BlockSpec((tm, LANES), lambda i, h: (i, P_Z // LANES + h))
    return pl.pallas_call(
        body, grid=(S // tm, DN_HEADS), in_specs=[hs, zs, _full((1, DN_DIM)), hs, _ANY],
        out_specs=[hs, zs, _full((1, DN_DIM))],
        out_shape=[jax.ShapeDtypeStruct((S, DN_WIDTH), F32), jax.ShapeDtypeStruct(dproj.shape, dproj.dtype),
                   jax.ShapeDtypeStruct((1, DN_DIM), F32)],
        input_output_aliases={4: 1},
        name="dn_out_bwd", compiler_params=_params(("arbitrary", "arbitrary")))(o, proj, gain, dy, dproj)


def _rel_buckets():
    qi = np.arange(BLOCK)[:, None]
    kj = np.arange(2 * BLOCK)[None, :]
    n = np.maximum(BLOCK + qi - kj, 0)
    max_exact = REL_BUCKETS // 2
    nf = np.maximum(n, 1).astype(np.float32)
    large = max_exact + (np.log(nf / np.float32(max_exact)) / np.float32(math.log(REL_MAX_DIST / max_exact))
                         * np.float32(REL_BUCKETS - max_exact)).astype(np.int32)
    large = np.minimum(large, REL_BUCKETS - 1)
    return np.where(n < max_exact, n, large).astype(np.int32)


def _bias_fwd(rel_bias):
    buckets = jnp.asarray(_rel_buckets())

    def body(rb_ref, bk_ref, o_ref):
        bk = bk_ref[...]
        for h in range(SWA_HEADS):
            acc = jnp.zeros((BLOCK, 2 * BLOCK), F32)
            for b in range(REL_BUCKETS):
                acc = jnp.where(bk == b, rb_ref[b, h], acc)
            for first in range(2):
                o_ref[first, h] = jnp.where(_swa_mask(1 - first), acc, -jnp.inf)

    return pl.pallas_call(
        body, in_specs=[pl.BlockSpec(memory_space=pltpu.SMEM), pl.BlockSpec(memory_space=pltpu.VMEM)],
        out_specs=pl.BlockSpec(memory_space=pltpu.VMEM),
        out_shape=jax.ShapeDtypeStruct((2, SWA_HEADS, BLOCK, 2 * BLOCK), F32), name="swa_bias_fwd",
        compiler_params=_params())(rel_bias, buckets)


def _bias_bwd(dbias):
    buckets = jnp.asarray(_rel_buckets())

    def body(d_ref, bk_ref, o_ref):
        bk = bk_ref[...]
        lane = lax.broadcasted_iota(jnp.int32, (1, LANES), 1)
        for h in range(SWA_HEADS):
            d = d_ref[h]
            row = jnp.zeros((1, LANES), F32)
            for b in range(REL_BUCKETS):
                part = jnp.sum(jnp.where(bk == b, d, 0.0), axis=1, keepdims=True)
                row = jnp.where(lane == b, jnp.sum(part, axis=0, keepdims=True), row)
            o_ref[h:h + 1, :] = row

    return pl.pallas_call(
        body, in_specs=[pl.BlockSpec(memory_space=pltpu.VMEM), pl.BlockSpec(memory_space=pltpu.VMEM)],
        out_specs=pl.BlockSpec(memory_space=pltpu.VMEM),
        out_shape=jax.ShapeDtypeStruct((SWA_HEADS, LANES), F32), name="swa_bias_bwd",
        compiler_params=_params())(dbias, buckets)


def _swa_mask(n):
    qi = lax.broadcasted_iota(jnp.int32, (BLOCK, 2 * BLOCK), 0)
    kj = lax.broadcasted_iota(jnp.int32, (BLOCK, 2 * BLOCK), 1)
    dist = BLOCK + qi - kj
    return (dist >= 0) & (dist < WINDOW) & ((n > 0) | (kj >= BLOCK))


def _swa_in_specs():
    q = pl.BlockSpec((BLOCK, SWA_WIDTH), lambda n: (n, P_SQ // SWA_WIDTH))
    kc = pl.BlockSpec((BLOCK, SWA_KVW), lambda n: (n, P_SK // SWA_KVW))
    kp = pl.BlockSpec((BLOCK, SWA_KVW), lambda n: (jnp.maximum(n - 1, 0), P_SK // SWA_KVW))
    vc = pl.BlockSpec((BLOCK, SWA_KVW), lambda n: (n, P_SV // SWA_KVW))
    vp = pl.BlockSpec((BLOCK, SWA_KVW), lambda n: (jnp.maximum(n - 1, 0), P_SV // SWA_KVW))
    band = pl.BlockSpec((None, SWA_HEADS, BLOCK, 2 * BLOCK), lambda n: (jnp.where(n == 0, 1, 0), 0, 0, 0))
    small = [_full((1, SWA_DIM)), _full((1, SWA_DIM)), _full((1, SWA_HEADS)), band]
    return [q, kp, kc, vp, vc] + small


def _swa_load(q_ref, kp_ref, kc_ref, vp_ref, vc_ref, s_ref):
    q = jnp.stack([q_ref[:, pl.ds(h * SWA_DIM, SWA_DIM)] for h in range(SWA_HEADS)])
    kbands, vbands = [], []
    for kv in range(SWA_KV):
        cols = pl.ds(kv * SWA_DIM, SWA_DIM)
        kbands += [jnp.concatenate([kp_ref[:, cols], kc_ref[:, cols]], axis=0)] * SWA_GROUP
        vbands += [jnp.concatenate([vp_ref[:, cols], vc_ref[:, cols]], axis=0)] * SWA_GROUP
    sinks = jnp.stack([s_ref[:, pl.ds(h, 1)] for h in range(SWA_HEADS)])
    return q, jnp.stack(kbands), jnp.stack(vbands), sinks


def _swa_fwd(proj, q_gain, k_gain, sinks, bias):
    S = proj.shape[0]

    def body(q_ref, kp_ref, kc_ref, vp_ref, vc_ref, qg_ref, kg_ref, s_ref, bias_ref, y_ref):
        q, kband, vband, sk = _swa_load(q_ref, kp_ref, kc_ref, vp_ref, vc_ref, s_ref)
        out = _swa_block(q, kband, vband, qg_ref[...], kg_ref[...], sk, bias_ref[...])
        for h in range(SWA_HEADS):
            y_ref[:, pl.ds(h * SWA_DIM, SWA_DIM)] = out[h].astype(BF16)

    return pl.pallas_call(
        body, grid=(S // BLOCK,), in_specs=_swa_in_specs(),
        out_specs=pl.BlockSpec((BLOCK, SWA_WIDTH), lambda n: (n, 0)),
        out_shape=jax.ShapeDtypeStruct((S, SWA_WIDTH), BF16), name="swa_fwd",
        compiler_params=_params(("parallel",)))(proj, proj, proj, proj, proj, q_gain, k_gain, sinks, bias)


def _swa_bwd(proj, q_gain, k_gain, sinks, bias, dy, dproj):
    S = proj.shape[0]

    def body(q_ref, kp_ref, kc_ref, vp_ref, vc_ref, qg_ref, kg_ref, s_ref, bias_ref, dy_ref, _,
             dq_ref, dk_ref, dv_ref, dqg_ref, dkg_ref, ds_ref, dbias_ref):
        n = pl.program_id(0)

        @pl.when(n == 0)
        def _():
            for r in (dk_ref, dv_ref, dqg_ref, dkg_ref, ds_ref, dbias_ref):
                r[...] = jnp.zeros_like(r)

        cur = pl.ds(pl.multiple_of(n * BLOCK, BLOCK), BLOCK)
        prev = pl.ds(pl.multiple_of(jnp.maximum(n - 1, 0) * BLOCK, BLOCK), BLOCK)
        q, kband, vband, sk = _swa_load(q_ref, kp_ref, kc_ref, vp_ref, vc_ref, s_ref)
        _, vjp = jax.vjp(_swa_block, q, kband, vband, qg_ref[...], kg_ref[...], sk, bias_ref[...])
        dy = jnp.stack([dy_ref[:, pl.ds(h * SWA_DIM, SWA_DIM)] for h in range(SWA_HEADS)])
        dq, dkb, dvb, dqg, dkg, dsk, dbs = vjp(dy)
        for h in range(SWA_HEADS):
            dq_ref[:, pl.ds(h * SWA_DIM, SWA_DIM)] = dq[h].astype(BF16)
            ds_ref[:, pl.ds(h, 1)] += dsk[h]
        dbias_ref[...] += dbs
        dqg_ref[...] += dqg
        dkg_ref[...] += dkg
        for kv in range(SWA_KV):
            cols = pl.ds(kv * SWA_DIM, SWA_DIM)
            group = range(kv * SWA_GROUP, (kv + 1) * SWA_GROUP)
            dk_kv = sum(dkb[h] for h in group)
            dv_kv = sum(dvb[h] for h in group)
            dk_ref[cur, cols] += dk_kv[BLOCK:]
            dv_ref[cur, cols] += dv_kv[BLOCK:]

            @pl.when(n > 0)
            def _(cols=cols, dk_kv=dk_kv, dv_kv=dv_kv):
                dk_ref[prev, cols] += dk_kv[:BLOCK]
                dv_ref[prev, cols] += dv_kv[:BLOCK]

    return pl.pallas_call(
        body, grid=(S // BLOCK,),
        in_specs=_swa_in_specs() + [pl.BlockSpec((BLOCK, SWA_WIDTH), lambda n: (n, 0)),
                                    pl.BlockSpec(memory_space=pl.ANY)],
        out_specs=[pl.BlockSpec((BLOCK, SWA_WIDTH), lambda n: (n, P_SQ // SWA_WIDTH)), _full((S, SWA_KVW)),
                   _full((S, SWA_KVW)), _full((1, SWA_DIM)), _full((1, SWA_DIM)), _full((1, SWA_HEADS)),
                   _full((SWA_HEADS, BLOCK, 2 * BLOCK))],
        out_shape=[jax.ShapeDtypeStruct(dproj.shape, dproj.dtype), jax.ShapeDtypeStruct((S, SWA_KVW), F32),
                   jax.ShapeDtypeStruct((S, SWA_KVW), F32), jax.ShapeDtypeStruct((1, SWA_DIM), F32),
                   jax.ShapeDtypeStruct((1, SWA_DIM), F32), jax.ShapeDtypeStruct((1, SWA_HEADS), F32),
                   jax.ShapeDtypeStruct((SWA_HEADS, BLOCK, 2 * BLOCK), F32)],
        input_output_aliases={10: 0},
        name="swa_bwd", compiler_params=_params(("arbitrary",)),
    )(proj, proj, proj, proj, proj, q_gain, k_gain, sinks, bias, dy, dproj)


def _kv_into(dproj, dk, dv, tm=1024):
    S = dk.shape[0]

    def body(dk_ref, dv_ref, _, o_ref):
        o_ref[:, :SWA_KVW] = dk_ref[...].astype(BF16)
        o_ref[:, SWA_KVW:] = dv_ref[...].astype(BF16)

    return pl.pallas_call(
        body, grid=(S // tm,), in_specs=[_row(tm, SWA_KVW), _row(tm, SWA_KVW), pl.BlockSpec(memory_space=pl.ANY)],
        out_specs=_row(tm, 2 * SWA_KVW, P_SK // (2 * SWA_KVW)),
        out_shape=jax.ShapeDtypeStruct(dproj.shape, dproj.dtype), input_output_aliases={2: 0},
        name="swa_kv_into", compiler_params=_params(("parallel",)))(dk, dv, dproj)


def _position():
    return lax.axis_index("x"), lax.axis_index("y"), lax.axis_index("c")


def _all_gather(shards, name="all_gather_weights"):
    na = len(shards)

    def body(*refs):
        x_refs, out_refs = refs[:na], refs[na:2 * na]
        send_sems, recv_sems, local_sems = refs[2 * na:]
        x, y, c = _position()
        me, sibling = (x, y, c), (x, y, 1 - c)
        chips = [(1 - x, y), (x, 1 - y), (1 - x, 1 - y)]

        def copy(a, k, block, to, own=False):
            px, py, pc = block
            slot = out_refs[a].at[4 * px + 2 * py + pc]
            return pltpu.make_async_remote_copy(
                src_ref=x_refs[a] if own else slot, dst_ref=slot, send_sem=send_sems.at[7 * a + k],
                recv_sem=recv_sems.at[7 * a + k], device_id=to, device_id_type=MESH_ID)

        mine = [pltpu.make_async_copy(x_refs[a], out_refs[a].at[4 * x + 2 * y + c], local_sems.at[a])
                for a in range(na)]
        for cp in mine:
            cp.start()
        first = []
        for a in range(na):
            first.append(copy(a, 0, me, sibling, own=True))
            first += [copy(a, 1 + j, me, (*chip, c), own=True) for j, chip in enumerate(chips)]
        for cp in first:
            cp.start()
        passed = []
        for j, chip in enumerate(chips):
            for a in range(na):
                copy(a, 1 + j, (*chip, c), me).wait_recv()
                passed.append(copy(a, 4 + j, (*chip, c), sibling))
                passed[-1].start()
        for a in range(na):
            copy(a, 0, sibling, me).wait_recv()
            for j, chip in enumerate(chips):
                copy(a, 4 + j, (*chip, 1 - c), me).wait_recv()
        for cp in first + passed:
            cp.wait_send()
        for cp in mine:
            cp.wait()

    return pl.pallas_call(
        body, in_specs=[pl.BlockSpec(memory_space=pl.ANY)] * na, out_specs=[pl.BlockSpec(memory_space=pl.ANY)] * na,
        out_shape=[jax.ShapeDtypeStruct((N_DEV,) + s.shape, s.dtype) for s in shards],
        scratch_shapes=[pltpu.SemaphoreType.DMA((7 * na,)), pltpu.SemaphoreType.DMA((7 * na,)),
                        pltpu.SemaphoreType.DMA((na,))],
        name=name)(*shards)


_HBM = pl.BlockSpec(memory_space=pltpu.HBM)
_SEM = pl.BlockSpec(memory_space=pltpu.SEMAPHORE)
_DATAFLOW = pltpu.SideEffectType.DATAFLOW_SIDE_EFFECTING


def _peers(x, y, c):
    out = []
    for k in range(1, N_DEV):
        px, py, pc = x ^ (k >> 2), y ^ ((k >> 1) & 1), c ^ (k & 1)
        out.append(((px, py, pc), 4 * px + 2 * py + pc))
    return out


def _split_copies(src_refs, land_refs, send_sems, recv_sems, scatter):
    x, y, c = _position()
    me = 4 * x + 2 * y + c
    sends, recvs = [], []
    for k, (peer_id, peer) in enumerate(_peers(x, y, c)):
        for a, (src, land) in enumerate(zip(src_refs, land_refs)):
            sems = dict(send_sem=send_sems.at[7 * a + k], recv_sem=recv_sems.at[7 * a + k],
                        device_id=peer_id, device_id_type=MESH_ID)
            mine = src.at[peer] if scatter else src
            sends.append(pltpu.make_async_remote_copy(src_ref=mine, dst_ref=land.at[me], **sems))
            recvs.append(pltpu.make_async_remote_copy(src_ref=mine, dst_ref=land.at[peer], **sems))
    return sends, recvs


def _all_gather_direct(shards, name, after):
    na = len(shards)

    def body(*refs):
        x_refs, out_refs = refs[:na], refs[na + 1:2 * na + 1]
        send_sems, recv_sems, local_sems = refs[2 * na + 1:]
        x, y, c = _position()
        me = 4 * x + 2 * y + c
        local = [pltpu.make_async_copy(x_refs[a], out_refs[a].at[me], local_sems.at[a]) for a in range(na)]
        sends, recvs = _split_copies(x_refs, out_refs, send_sems, recv_sems, False)
        for cp in local + sends:
            cp.start()
        for cp in recvs:
            cp.wait_recv()
        for cp in sends:
            cp.wait_send()
        for cp in local:
            cp.wait()

    return pl.pallas_call(
        body, in_specs=[pl.BlockSpec(memory_space=pl.ANY)] * (na + 1),
        out_specs=[pl.BlockSpec(memory_space=pl.ANY)] * na,
        out_shape=[jax.ShapeDtypeStruct((N_DEV,) + s.shape, s.dtype) for s in shards],
        scratch_shapes=[pltpu.SemaphoreType.DMA((7 * na,)), pltpu.SemaphoreType.DMA((7 * na,)),
                        pltpu.SemaphoreType.DMA((na,))],
        name=name)(*shards, after)


def _exchange_start(srcs, scatter, name, after=None):
    na = len(srcs)
    lands = [lax.empty(s.shape if scatter else (N_DEV,) + s.shape, s.dtype) for s in srcs]
    extra = [] if after is None else [after]

    def body(*refs):
        src_refs, land_refs = refs[:na], refs[na:2 * na]
        send_sems, recv_sems = refs[2 * na + len(extra)], refs[2 * na + len(extra) + 1]
        token = refs[-1]
        sends, _ = _split_copies(src_refs, land_refs, send_sems, recv_sems, scatter)
        for cp in sends:
            cp.start()
        token[...] = jnp.zeros_like(token)

    hbm = lambda a: pltpu.HBM(a.shape, a.dtype)
    out = pl.pallas_call(
        body, name=name,
        out_shape=(pltpu.SemaphoreType.DMA((7 * na,)), pltpu.SemaphoreType.DMA((7 * na,)),
                   *[hbm(s) for s in srcs], *[hbm(l) for l in lands], jax.ShapeDtypeStruct((8, LANES), F32)),
        in_specs=[_HBM] * (2 * na) + [pl.BlockSpec(memory_space=pl.ANY)] * len(extra),
        out_specs=(_SEM, _SEM, *[_HBM] * (2 * na), pl.BlockSpec(memory_space=pltpu.VMEM)),
        input_output_aliases={i: 2 + i for i in range(2 * na)},
        compiler_params=pltpu.CompilerParams(has_side_effects=_DATAFLOW),
    )(*[pltpu.with_memory_space_constraint(s, pltpu.HBM) for s in srcs],
      *[pltpu.with_memory_space_constraint(l, pltpu.HBM) for l in lands], *extra)
    return (out[0], out[1], list(out[2:2 + na]), list(out[2 + na:2 + 2 * na])), out[-1]


def _exchange_wait(handle, after, scatter, name):
    send_sems, recv_sems, srcs, lands = handle
    na = len(srcs)

    def body(*refs):
        src_refs, land_refs = refs[:na], refs[na:2 * na]
        s_sems, r_sems = refs[2 * na], refs[2 * na + 1]
        sends, recvs = _split_copies(src_refs, land_refs, s_sems, r_sems, scatter)
        for cp in sends:
            cp.wait_send()
        for cp in recvs:
            cp.wait_recv()

    hbm = lambda a: pltpu.HBM(a.shape, a.dtype)
    out = pl.pallas_call(
        body, name=name, out_shape=(*[hbm(s) for s in srcs], *[hbm(l) for l in lands]),
        in_specs=[_HBM] * (2 * na) + [_SEM, _SEM, pl.BlockSpec(memory_space=pl.ANY)],
        out_specs=tuple([_HBM] * (2 * na)), input_output_aliases={i: i for i in range(2 * na)},
        compiler_params=pltpu.CompilerParams(has_side_effects=_DATAFLOW),
    )(*srcs, *lands, send_sems, recv_sems, after)
    return list(out[:na]), list(out[na:])


def _own_slot(landed, own):
    me = 4 * lax.axis_index("x") + 2 * lax.axis_index("y") + lax.axis_index("c")
    return lax.dynamic_update_slice_in_dim(landed, own[None], me, axis=0)


def _adam_update(parts, w, m, v, name, tr=256):
    _, r, c = w.shape
    tr = _pick_rows(r, tr)
    cp = parts.shape[2]

    def body(p_ref, w_ref, m_ref, v_ref, g_ref, d_ref, nm_ref, nv_ref):
        g = p_ref[0, :, pl.ds(0, c)].astype(F32)
        for i in range(1, N_DEV):
            g = g + p_ref[i, :, pl.ds(0, c)].astype(F32)
        delta, nm, nv = _adamw(w_ref[0], g, m_ref[0], v_ref[0])
        g_ref[0] = g
        d_ref[0] = delta
        nm_ref[0] = nm
        nv_ref[0] = nv

    rs = pl.BlockSpec((1, tr, c), lambda i: (0, i, 0))
    return pl.pallas_call(
        body, grid=(r // tr,), in_specs=[pl.BlockSpec((N_DEV, tr, cp), lambda i: (0, i, 0)), rs, rs, rs],
        out_specs=[rs] * 4, out_shape=[jax.ShapeDtypeStruct((1, r, c), F32)] * 4, name=name,
        compiler_params=_params(("parallel",)))(parts, w, m, v)


def _pick_rows(rows, target):
    if rows <= target:
        return rows
    t = target
    while t >= 16:
        if rows % t == 0:
            return t
        t -= 16
    return rows


BIG = ("w_in", "w_branch_dn", "w_branch_swa", "w_out", "w_gate", "w_up", "w_down")
IN_SHARD, IN_WIRE = D_IN // N_DEV, 640
FF_SHARD, FF_WIRE = D_FF // N_DEV, 384
D_FFP = N_DEV * FF_WIRE
BIG_SHAPES = {"w_in": ((D_MODEL, IN_SHARD), (D_MODEL, IN_WIRE)),
              "w_branch_dn": ((DN_WIDTH, LANES), (DN_WIDTH, LANES)),
              "w_branch_swa": ((SWA_WIDTH, LANES), (SWA_WIDTH, LANES)),
              "w_out": ((LANES, D_MODEL), (LANES, D_MODEL)),
              "w_gate": ((D_MODEL, FF_SHARD), (D_MODEL, FF_WIRE)),
              "w_up": ((D_MODEL, FF_SHARD), (D_MODEL, FF_WIRE)),
              "w_down": ((FF_SHARD, D_MODEL), (FF_WIRE, D_MODEL))}
CONV_SHARD, CONV_WIRE = (DN_CONV, DN_QKV // N_DEV), (8, 256)


def _pad_to(a, shape):
    return jnp.pad(a, [(0, t - s) for s, t in zip(a.shape, shape)])


IN_TILE_ROWS = 256
_IN_SEGS = ((R_GATE, 2048, P_GATE), (R_QKV, DN_QKV, P_QKV), (R_Z, DN_WIDTH, P_Z), (R_SQ, SWA_WIDTH, P_SQ),
            (R_SK, SWA_KVW, P_SK), (R_SV, SWA_KVW, P_SV), (R_B, 8, P_BA))


def _w_in_from_blocks(blocks):
    tm = IN_TILE_ROWS

    def body(b_ref, o_ref):
        parts = []
        for rs, n, _ in _IN_SEGS:
            for dev in range(N_DEV):
                lo, hi = max(rs, IN_SHARD * dev), min(rs + n, IN_SHARD * (dev + 1))
                if lo < hi:
                    parts.append(b_ref[dev][:, lo - IN_SHARD * dev:hi - IN_SHARD * dev])
        parts.append(jnp.zeros((tm, P_WIDTH - P_BA - 8), b_ref.dtype))
        o_ref[...] = jnp.concatenate(parts, axis=1)

    return pl.pallas_call(
        body, grid=(D_MODEL // tm,), in_specs=[pl.BlockSpec((N_DEV, tm, IN_WIRE), lambda i: (0, i, 0))],
        out_specs=pl.BlockSpec((tm, P_WIDTH), lambda i: (i, 0)),
        out_shape=jax.ShapeDtypeStruct((D_MODEL, P_WIDTH), blocks.dtype), name="w_in_from_blocks",
        compiler_params=_params(("parallel",)))(blocks)


def _w_in_to_blocks(g):
    tm = IN_TILE_ROWS

    def body(g_ref, o_ref):
        for dev in range(N_DEV):
            parts = []
            for rs, n, ps in sorted(_IN_SEGS):
                lo, hi = max(rs, IN_SHARD * dev), min(rs + n, IN_SHARD * (dev + 1))
                if lo < hi:
                    parts.append(g_ref[:, ps + lo - rs:ps + hi - rs])
            parts.append(jnp.zeros((tm, IN_WIRE - IN_SHARD), g_ref.dtype))
            o_ref[dev] = jnp.concatenate(parts, axis=1)

    return pl.pallas_call(
        body, grid=(D_MODEL // tm,), in_specs=[pl.BlockSpec((tm, P_WIDTH), lambda i: (i, 0))],
        out_specs=pl.BlockSpec((N_DEV, tm, IN_WIRE), lambda i: (0, i, 0)),
        out_shape=jax.ShapeDtypeStruct((N_DEV, D_MODEL, IN_WIRE), g.dtype), name="w_in_to_blocks",
        compiler_params=_params(("parallel",)))(g)


SMALL = {"attn_norm": (0, (1, D_MODEL)), "ffn_norm": (1, (1, D_MODEL)), "dn_out_norm": (2, (1, DN_DIM)),
         "swa_q_norm": (3, (1, SWA_DIM)), "swa_k_norm": (4, (1, SWA_DIM)), "dn_a_log": (5, (1, DN_HEADS)),
         "dn_dt_bias": (6, (1, DN_HEADS)), "swa_sinks": (7, (1, SWA_HEADS)), "rel_bias": (8, (REL_BUCKETS, SWA_HEADS))}
SMALL_SHEET = (48, D_MODEL)


LOSS_ROW = 40


def _small_pack(grads, loss_local):
    names = list(SMALL)

    def body(*refs):
        o_ref = refs[-1]
        o_ref[...] = jnp.zeros_like(o_ref)
        for n, ref in zip(names, refs):
            r0, (nr, nc) = SMALL[n]
            o_ref[r0:r0 + nr, 0:nc] = ref[...]
        o_ref[LOSS_ROW:LOSS_ROW + 1, 0:1] = refs[len(names)][...]

    return pl.pallas_call(
        body, in_specs=[pl.BlockSpec(memory_space=pltpu.VMEM)] * (len(names) + 1),
        out_specs=pl.BlockSpec(memory_space=pltpu.VMEM), out_shape=jax.ShapeDtypeStruct(SMALL_SHEET, F32),
        name="small_pack", compiler_params=_params())(*[grads[n].reshape(SMALL[n][1]) for n in names], loss_local)


def _small_update(sheets, w, m, v):
    names = list(SMALL)
    k = len(names)

    def body(*refs):
        p_ref = refs[0]
        ins, outs = refs[1:1 + 3 * k], refs[1 + 3 * k:]
        loss = p_ref[0, LOSS_ROW:LOSS_ROW + 1, 0:1]
        for i in range(1, N_DEV):
            loss = loss + p_ref[i, LOSS_ROW:LOSS_ROW + 1, 0:1]
        outs[4 * k][...] = loss
        for t, n in enumerate(names):
            r0, (nr, nc) = SMALL[n]
            g = p_ref[0, r0:r0 + nr, 0:nc]
            for i in range(1, N_DEV):
                g = g + p_ref[i, r0:r0 + nr, 0:nc]
            delta, nm, nv = _adamw(ins[t][...], g, ins[k + t][...], ins[2 * k + t][...])
            for kind, val in enumerate((g, delta, nm, nv)):
                outs[kind * k + t][...] = val

    shapes = [jax.ShapeDtypeStruct(SMALL[n][1], F32) for n in names]
    vm = pl.BlockSpec(memory_space=pltpu.VMEM)
    res = pl.pallas_call(
        body, in_specs=[vm] * (1 + 3 * k), out_specs=[vm] * (4 * k + 1),
        out_shape=shapes * 4 + [jax.ShapeDtypeStruct((1, 1), F32)], name="adam_small", compiler_params=_params(),
    )(sheets, *[d[n].reshape(SMALL[n][1]) for d in (w, m, v) for n in names])
    return {n: tuple(res[kind * k + t] for kind in range(4)) for t, n in enumerate(names)}, res[4 * k]


def kernel(x, attn_norm, w_in, dn_conv, dn_a_log, dn_dt_bias, dn_out_norm, swa_q_norm, swa_k_norm, swa_sinks, rel_bias, w_branch_dn, w_branch_swa, w_out, ffn_norm, w_gate, w_up, w_down, loss_target, m_attn_norm, m_w_in, m_dn_conv, m_dn_a_log, m_dn_dt_bias, m_dn_out_norm, m_swa_q_norm, m_swa_k_norm, m_swa_sinks, m_rel_bias, m_w_branch_dn, m_w_branch_swa, m_w_out, m_ffn_norm, m_w_gate, m_w_up, m_w_down, v_attn_norm, v_w_in, v_dn_conv, v_dn_a_log, v_dn_dt_bias, v_dn_out_norm, v_swa_q_norm, v_swa_k_norm, v_swa_sinks, v_rel_bias, v_w_branch_dn, v_w_branch_swa, v_w_out, v_ffn_norm, v_w_gate, v_w_up, v_w_down):
    args = dict(locals())
    S = x.shape[1]
    xs = x.reshape(S, D_MODEL)
    target = loss_target.reshape(S, D_MODEL)

    w_loc = {n: args[n].reshape(BIG_SHAPES[n][0]) for n in BIG}
    conv_loc = dn_conv.reshape(CONV_SHARD)
    wire = {n: _pad_to(w_loc[n], BIG_SHAPES[n][1]).astype(BF16) for n in BIG}
    first = _all_gather([wire["w_in"], _pad_to(conv_loc, CONV_WIRE)])
    later = [n for n in BIG if n != "w_in"]
    rest_handle, rest_token = _exchange_start([wire[n] for n in later], False, "gather_rest_start", after=first[1])
    w_pad = _w_in_from_blocks(first[0])
    conv_w = jnp.concatenate([first[1][d, :DN_CONV, :CONV_SHARD[1]] for d in range(N_DEV)], axis=1)

    h = _norm_fwd(xs, attn_norm + rest_token[0, 0], "norm1_fwd")
    proj = _mm([(h, w_pad)], "nn", F32, "mm_in", 1024, 1664, j_outer=True)
    qkvn = _dn_conv_fwd(proj, conv_w)
    beta, g = _dn_gate_fwd(proj, dn_a_log, dn_dt_bias)
    u, w, qe, kd, qk, egl, tinv = _dn_prep_fwd(qkvn, g, beta)
    o, states = _dn_scan_fwd(u, w, qe, kd, qk, egl)
    y_dn = _dn_out_fwd(o, proj, dn_out_norm)
    bias = _bias_fwd(rel_bias)
    y_swa = _swa_fwd(proj, swa_q_norm, swa_k_norm, swa_sinks, bias)
    rest_src, rest_land = _exchange_wait(rest_handle, y_swa, False, "gather_rest_wait")
    G = {n: _own_slot(land, src) for n, src, land in zip(later, rest_src, rest_land)}
    w_bdn, w_bswa, w_g, w_u = G["w_branch_dn"], G["w_branch_swa"], G["w_gate"], G["w_up"]
    w_o = G["w_out"].reshape(D_MODEL, D_MODEL)
    w_d = G["w_down"].reshape(D_FFP, D_MODEL)
    gates = [(proj, P_GATE // 512), (proj, (P_GATE + D_MODEL) // 512)]
    a_dn, a_swa, merged = _mm_fused(
        [(y_dn, w_bdn), (y_swa, w_bswa)], "nn", "mm_branch_merge", 1024, 512,
        lambda p, e: (p[0], p[1], _merge(e[0], e[1], p[0], p[1])), gates, (F32, F32, BF16), b_blocks=True)

    def resid_norm(p, e):
        x1 = e[0] + p[0]
        return x1, _rms(x1, e[1])

    x1, h2 = _mm_fused([(merged, w_o)], "nn", "mm_out_norm", 512, D_MODEL, resid_norm,
                       [(xs, 0), (ffn_norm, None)], (F32, BF16))
    gate, up, act = _mm_fused([(h2, w_g), (h2, w_u)], "nn", "mm_gate_up_act", 1024, 768,
                              lambda p, e: (p[0], p[1], _act(p[0], p[1])), [], (F32, F32, BF16),
                              j_outer=True, b_blocks=True)

    def loss_head(p, e):
        diff = e[0] + p[0] - e[1]
        dy = diff * (1.0 / D_MODEL)
        part = jnp.sum(jnp.mean(diff * diff, axis=-1, keepdims=True), axis=0, keepdims=True) * 0.5
        return dy, dy, part

    dy, dy_b, loss_local = _mm_fused([(act, w_d)], "nn", "mm_down_loss", 512, D_MODEL, loss_head,
                                     [(x1, 0), (target, 0)], (F32, BF16), sum_shape=(1, 1))

    def act_bwd(p, e):
        _, vjp = jax.vjp(_act, e[0], e[1])
        return vjp(p[0])

    dgate, dup = _mm_fused([(dy_b, w_d)], "nt", "mm_dact_act", 1024, 768, act_bwd, [(gate, 0), (up, 0)],
                           (BF16, BF16), j_outer=True)
    g_w_down = _mm([(act, dy_b)], "tn", BF16, "mm_dw_down", 768, D_MODEL, j_outer=True)
    g_w_down = g_w_down.reshape(N_DEV, FF_WIRE, D_MODEL)
    g_w_gate = _mm([(h2, dgate)], "tn", BF16, "mm_dw_gate", D_MODEL, 768, out_blocks=True)
    g_w_up = _mm([(h2, dup)], "tn", BF16, "mm_dw_up", D_MODEL, 768, out_blocks=True)
    ffn_handle, ffn_token = _exchange_start([g_w_down, g_w_gate, g_w_up], True, "scatter_ffn_start")

    def norm_bwd(p, e):
        _, vjp = jax.vjp(_rms, e[0], e[2])
        dx, dgain = vjp(sum(p))
        dx = dx + e[1]
        return dx, dx, dgain

    dx1, dx1_b, g_ffn_norm = _mm_fused(
        [(dgate, w_g), (dup, w_u)], "nt", "mm_dh2_norm", 256, D_MODEL, norm_bwd,
        [(x1, 0), (dy, 0), (ffn_norm + ffn_token[0, 0], None)], (F32, BF16), b_blocks=True, sum_shape=(1, D_MODEL))
    def merge_bwd(p, e):
        _, vjp = jax.vjp(_merge, *e)
        dg0, dg1, da_dn, da_swa = vjp(p[0])
        return jnp.concatenate([dg0, dg1], axis=1), da_dn, da_swa

    dproj, da_dn, da_swa = _mm_fused(
        [(dx1_b, w_o)], "nt", "mm_dmerged_merge", 512, D_MODEL, merge_bwd,
        [(proj, P_GATE // D_MODEL), (proj, P_GATE // D_MODEL + 1), (a_dn, 0), (a_swa, 0)], (BF16,) * 3,
        wide_first=(P_WIDTH, 2 * D_MODEL))
    g_w_out = _mm([(merged, dx1_b)], "tn", BF16, "mm_dw_out", 512, D_MODEL, j_outer=True)
    g_w_out = g_w_out.reshape(N_DEV, LANES, D_MODEL)
    dy_dn = _mm([(da_dn, w_bdn)], "nt", F32, "mm_dy_dn", 1024, DN_WIDTH, b_blocks=True)
    dy_swa = _mm([(da_swa, w_bswa)], "nt", F32, "mm_dy_swa", 1024, SWA_WIDTH, b_blocks=True)
    g_w_bdn = _mm([(y_dn, da_dn)], "tn", BF16, "mm_dw_branch_dn", DN_WIDTH, 512, out_blocks=True)
    g_w_bswa = _mm([(y_swa, da_swa)], "tn", BF16, "mm_dw_branch_swa", SWA_WIDTH, 512, out_blocks=True)
    dproj, dsk, dsv, g_q_norm, g_k_norm, g_sinks, dbias = _swa_bwd(proj, swa_q_norm, swa_k_norm, swa_sinks, bias,
                                                                   dy_swa, dproj)
    dproj = _kv_into(dproj, dsk, dsv)
    g_rel_bias = _bias_bwd(dbias)[:, :REL_BUCKETS].T
    mix_handle, mix_token = _exchange_start([g_w_out, g_w_bdn, g_w_bswa], True, "scatter_mix_start")
    do, dproj, g_out_norm = _dn_out_bwd(o, proj, dn_out_norm + mix_token[0, 0], dy_dn, dproj)
    du, dw, dqe, dkd, dqk, degl = _dn_scan_bwd(u, w, qe, kd, qk, egl, states, do)
    dqkvn, dgd, dbeta = _dn_prep_bwd(qkvn, g, beta, tinv, du, dw, dqe, dkd, dqk, degl)
    dproj, dal, ddt = _dn_gate_bwd(proj, dn_a_log, dn_dt_bias, dbeta, dgd, dproj)
    g_a_log = dal.reshape(DN_HEADS, DN_DIM).sum(axis=1)
    g_dt_bias = ddt[0, DN_HEADS:2 * DN_HEADS]
    dproj, g_conv = _dn_conv_bwd(proj, conv_w, dqkvn, dproj)
    g_w_in = _w_in_to_blocks(_mm([(h, dproj)], "tn", BF16, "mm_dw_in", 512, 1664, j_outer=True))
    in_handle, in_token = _exchange_start([g_w_in], True, "scatter_in_start")
    dx, g_attn_norm = _mm_fused(
        [(dproj, w_pad)], "nt", "mm_dh_norm", 512, D_MODEL, lambda p, e: norm_bwd(p, e)[1:],
        [(xs, 0), (dx1, 0), (attn_norm + in_token[0, 0], None)], (F32,), sum_shape=(1, D_MODEL))

    g_small = {"attn_norm": g_attn_norm, "ffn_norm": g_ffn_norm, "rel_bias": g_rel_bias, "dn_out_norm": g_out_norm,
               "swa_q_norm": g_q_norm, "swa_k_norm": g_k_norm, "dn_a_log": g_a_log, "dn_dt_bias": g_dt_bias,
               "swa_sinks": g_sinks}
    me = 4 * lax.axis_index("x") + 2 * lax.axis_index("y") + lax.axis_index("c")
    outs = {}

    def finish(handle, group, name, after):
        srcs, lands = _exchange_wait(handle, after, True, name)
        for n, src, land in zip(group, srcs, lands):
            parts = _own_slot(land, lax.dynamic_index_in_dim(src, me, 0, keepdims=False))
            outs[n] = _adam_update(parts, args[n], args["m_" + n], args["v_" + n], "adam_" + n)

    finish(ffn_handle, ("w_down", "w_gate", "w_up"), "scatter_ffn_wait", dx)
    finish(mix_handle, ("w_out", "w_branch_dn", "w_branch_swa"), "scatter_mix_wait", dx)
    sheets, conv_all = _all_gather_direct([_small_pack(g_small, loss_local), _pad_to(g_conv, (8, DN_QKV))],
                                          "all_gather_small",
                                          after=outs["w_up"][0])
    finish(in_handle, ("w_in",), "scatter_in_wait", sheets)
    conv_parts = lax.dynamic_slice(conv_all, (0, 0, me * CONV_SHARD[1]), (N_DEV,) + CONV_SHARD)
    outs["dn_conv"] = _adam_update(conv_parts, dn_conv, m_dn_conv, v_dn_conv, "adam_dn_conv")
    small_outs, loss = _small_update(sheets, {n: args[n] for n in SMALL}, {n: args["m_" + n] for n in SMALL},
                                     {n: args["v_" + n] for n in SMALL})
    outs.update(small_outs)

    names = ("attn_norm", "w_in", "dn_conv", "dn_a_log", "dn_dt_bias", "dn_out_norm", "swa_q_norm", "swa_k_norm",
             "swa_sinks", "rel_bias", "w_branch_dn", "w_branch_swa", "w_out", "ffn_norm", "w_gate", "w_up", "w_down")
    results = []
    for kind in range(4):
        results += [outs[n][kind].reshape(args[n].shape) for n in names]

    return (loss.reshape(()), dx.reshape(x.shape), *results)
```

```python
import math

import numpy as np
import jax
import jax.numpy as jnp
from jax import lax
from jax.experimental import pallas as pl
from jax.experimental.pallas import tpu as pltpu

F32 = jnp.float32
BF16 = jnp.bfloat16
HI = lax.Precision.HIGHEST

D_MODEL = 1024
DN_HEADS = 4
DN_DIM = 128
DN_WIDTH = 512
DN_QKV = 1536
DN_CONV = 4
CHUNK = 64
SWA_HEADS = 8
SWA_KV = 2
SWA_GROUP = 4
SWA_DIM = 64
SWA_WIDTH = 512
SWA_KVW = 128
WINDOW = 128
BLOCK = 128
REL_BUCKETS = 32
REL_MAX_DIST = 128
D_FF = 2816
D_IN = 4872
EPS = 1e-6
N_DEV = 8

ADAM_LR = 0.001
ADAM_B1 = 0.9
ADAM_B2 = 0.999
ADAM_EPS = 1e-08
ADAM_WD = 0.01
ADAM_STEP = 10

P_GATE, P_QKV, P_Z, P_SQ, P_SK, P_SV, P_BA = 0, 2048, 3584, 4096, 4608, 4736, 4864
P_WIDTH = 4992
R_QKV, R_Z, R_B, R_A, R_SQ, R_SK, R_SV, R_GATE = 0, 1536, 2048, 2052, 2056, 2568, 2696, 2824

VMEM_LIMIT = 56 * 1024 * 1024
LANES = 128
MESH_ID = pl.DeviceIdType.MESH


def _params(sem=None):
    return pltpu.CompilerParams(dimension_semantics=sem, vmem_limit_bytes=VMEM_LIMIT)


def _pick(dim, target):
    if dim <= target:
        return dim
    t = target - target % LANES
    while t >= LANES:
        if dim % t == 0:
            return t
        t -= LANES
    return dim


_DIMS = {"nn": (((1,), (0,)), ((), ())), "nt": (((1,), (1,)), ((), ())), "tn": (((0,), (0,)), ((), ()))}


def _tile_product(a_ref, b_ref, mode, b_blocks):
    a = a_ref[...].astype(BF16)
    b = jnp.concatenate([b_ref[d] for d in range(b_ref.shape[0])], axis=1) if b_blocks else b_ref[...]
    return lax.dot_general(a, b.astype(BF16), _DIMS[mode], preferred_element_type=F32)


def _mm(pairs, mode, out_dtype, name, bm, bn, j_outer=False, b_blocks=False, out_blocks=False):
    a0, b0 = pairs[0]
    cb = b0.shape[2] if b_blocks else None
    b_shape = (b0.shape[1], N_DEV * cb) if b_blocks else b0.shape
    if mode == "nn":
        (M, K), (K2, N) = a0.shape, b_shape
    elif mode == "nt":
        (M, K), (N, K2) = a0.shape, b_shape
    else:
        (K, M), (K2, N) = a0.shape, b_shape
    bm, bn = min(bm, M), min(bn, N)
    assert K == K2 and M % bm == 0 and N % bn == 0, (name, a0.shape, b0.shape, bm, bn)
    co = N // N_DEV
    assert not out_blocks or bn % co == 0
    dims = _DIMS[mode]
    n = len(pairs)

    def body(*refs):
        o_ref = refs[2 * n]
        acc = None
        for t in range(n):
            p = _tile_product(refs[2 * t], refs[2 * t + 1], mode, b_blocks)
            acc = p if acc is None else acc + p
        if out_blocks:
            for d in range(bn // co):
                o_ref[d] = acc[:, d * co:(d + 1) * co].astype(out_dtype)
        else:
            o_ref[...] = acc.astype(out_dtype)

    def ij(f):
        return (lambda j, i: f(i, j)) if j_outer else f

    a_spec = pl.BlockSpec((K, bm), ij(lambda i, j: (0, i))) if mode == "tn" else pl.BlockSpec((bm, K), ij(lambda i, j: (i, 0)))
    if b_blocks and mode == "nt":
        b_spec = pl.BlockSpec((N_DEV, bn, cb), ij(lambda i, j: (0, j, 0)))
    elif b_blocks:
        b_spec = pl.BlockSpec((bn // cb, K, cb), ij(lambda i, j: (j, 0, 0)))
    elif mode == "nt":
        b_spec = pl.BlockSpec((bn, K), ij(lambda i, j: (j, 0)))
    else:
        b_spec = pl.BlockSpec((K, bn), ij(lambda i, j: (0, j)))
    if out_blocks:
        out_spec = pl.BlockSpec((bn // co, bm, co), ij(lambda i, j: (j, i, 0)))
        out_shape = jax.ShapeDtypeStruct((N_DEV, M, co), out_dtype)
    else:
        out_spec = pl.BlockSpec((bm, bn), ij(lambda i, j: (i, j)))
        out_shape = jax.ShapeDtypeStruct((M, N), out_dtype)
    grid = (N // bn, M // bm) if j_outer else (M // bm, N // bn)
    return pl.pallas_call(
        body, grid=grid, in_specs=[a_spec, b_spec] * n, out_specs=out_spec, out_shape=out_shape, name=name,
        compiler_params=_params(("parallel", "parallel")),
    )(*[x for pair in pairs for x in pair])


def _mm_fused(pairs, mode, name, bm, bn, epilogue, extras, out_dtypes, j_outer=False, b_blocks=False,
              sum_shape=None, wide_first=None):
    a0, b0 = pairs[0]
    cb = b0.shape[2] if b_blocks else None
    b_shape = (b0.shape[1], N_DEV * cb) if b_blocks else b0.shape
    if mode == "nn":
        (M, K), (K2, N) = a0.shape, b_shape
    else:
        (M, K), (N, K2) = a0.shape, b_shape
    bm, bn = min(bm, M), min(bn, N)
    assert mode in ("nn", "nt") and K == K2 and M % bm == 0 and N % bn == 0, (name, a0.shape, b0.shape)
    dims = _DIMS[mode]
    n, ne, no = len(pairs), len(extras), len(out_dtypes)

    def body(*refs):
        prods = [_tile_product(refs[2 * t], refs[2 * t + 1], mode, b_blocks) for t in range(n)]
        results = epilogue(prods, [r[...] for r in refs[2 * n:2 * n + ne]])
        out_refs = refs[2 * n + ne:]
        for o_ref, val, dt in zip(out_refs, results, out_dtypes):
            o_ref[...] = val.astype(dt)
        if sum_shape is not None:
            s_ref = out_refs[no]

            @pl.when((pl.program_id(0) == 0) & (pl.program_id(1) == 0))
            def _():
                s_ref[...] = jnp.zeros_like(s_ref)

            s_ref[...] += results[no]

    def ij(f):
        return (lambda j, i: f(i, j)) if j_outer else f

    a_spec = pl.BlockSpec((bm, K), ij(lambda i, j: (i, 0)))
    once = dict(pipeline_mode=pl.Buffered(1)) if bn == N else {}
    if b_blocks and mode == "nt":
        b_spec = pl.BlockSpec((N_DEV, bn, cb), ij(lambda i, j: (0, j, 0)), **once)
    elif b_blocks:
        b_spec = pl.BlockSpec((bn // cb, K, cb), ij(lambda i, j: (j, 0, 0)), **once)
    elif mode == "nt":
        b_spec = pl.BlockSpec((bn, K), ij(lambda i, j: (j, 0)), **once)
    else:
        b_spec = pl.BlockSpec((K, bn), ij(lambda i, j: (0, j)), **once)
    e_specs = [pl.BlockSpec((1, bn), ij(lambda i, j: (0, j))) if first is None
               else pl.BlockSpec((bm, bn), ij(lambda i, j, first=first: (i, first + j))) for _, first in extras]
    tile = pl.BlockSpec((bm, bn), ij(lambda i, j: (i, j)))
    out_specs = [tile] * no
    out_shape = [jax.ShapeDtypeStruct((M, N), dt) for dt in out_dtypes]
    if wide_first is not None:
        assert bn == N
        out_specs[0] = pl.BlockSpec((bm, wide_first[1]), ij(lambda i, j: (i, 0)))
        out_shape[0] = jax.ShapeDtypeStruct((M, wide_first[0]), out_dtypes[0])
    if sum_shape is not None:
        assert sum_shape[1] in (1, bn) and (sum_shape[1] == 1 or bn == N)
        out_specs.append(_full(sum_shape))
        out_shape.append(jax.ShapeDtypeStruct(sum_shape, F32))
    grid = (N // bn, M // bm) if j_outer else (M // bm, N // bn)
    sem = ("arbitrary", "arbitrary") if sum_shape is not None else ("parallel", "parallel")
    return pl.pallas_call(
        body, grid=grid, in_specs=[a_spec, b_spec] * n + e_specs, out_specs=out_specs, out_shape=out_shape,
        name=name, compiler_params=_params(sem),
    )(*[x for pair in pairs for x in pair], *[arr for arr, _ in extras])


def _rms(x, gain):
    return x * lax.rsqrt(jnp.mean(x * x, axis=-1, keepdims=True) + EPS) * gain


def _silu(x):
    return x * jax.nn.sigmoid(x)


def _act(g, u):
    return _silu(g) * u


def _merge(g0, g1, a_dn, a_swa):
    return jax.nn.sigmoid(g0) * a_dn + jax.nn.sigmoid(g1) * a_swa


def _dn_post(c, is_v, q_scale):
    a = _silu(c)
    rs = lax.rsqrt(jnp.sum(a * a, axis=-1, keepdims=True) + EPS) * q_scale
    return a * jnp.where(is_v, 1.0, rs)


def _dn_out(o, z, gain):
    return _rms(o, gain) * _silu(z)


def _dot(a, b, dims=_DIMS["nn"], hi=False):
    if a.ndim == 3 or b.ndim == 3:
        batch = a.shape[0] if a.ndim == 3 else b.shape[0]
        a = a if a.ndim == 3 else jnp.broadcast_to(a, (batch,) + a.shape)
        b = b if b.ndim == 3 else jnp.broadcast_to(b, (batch,) + b.shape)
        ((ca,), (cb,)), _ = dims
        dims = (((ca + 1,), (cb + 1,)), ((0,), (0,)))
    if hi:
        return lax.dot_general(a, b, dims, precision=HI, preferred_element_type=F32)
    return lax.dot_general(a.astype(BF16), b.astype(BF16), dims, preferred_element_type=F32)


def _pieces(x):
    hi = x.astype(BF16)
    r1 = x - hi.astype(F32)
    mid = r1.astype(BF16)
    return hi, mid, (r1 - mid.astype(F32)).astype(BF16)


def _sel_left_impl(m, x):
    mb = m.astype(BF16)
    hi, mid, lo = _pieces(x)
    return _dot(mb, hi) + (_dot(mb, mid) + _dot(mb, lo))


@jax.custom_vjp
def _sel_left(m, mt, x):
    return _sel_left_impl(m, x)


_sel_left.defvjp(lambda m, mt, x: (_sel_left_impl(m, x), (m, mt)),
                 lambda res, ct: (jnp.zeros_like(res[0]), jnp.zeros_like(res[1]), _sel_left_impl(res[1], ct)))


def _sel_right_impl(x, s):
    sb = s.astype(BF16)
    hi, mid, lo = _pieces(x)
    return _dot(hi, sb) + (_dot(mid, sb) + _dot(lo, sb))


@jax.custom_vjp
def _sel_right(x, s, st):
    return _sel_right_impl(x, s)


_sel_right.defvjp(lambda x, s, st: (_sel_right_impl(x, s), (s, st)),
                  lambda res, ct: (_sel_right_impl(ct, res[1]), jnp.zeros_like(res[0]), jnp.zeros_like(res[1])))


def _dot3_impl(a, b):
    a_hi, a_lo, _ = _pieces(a)
    b_hi, b_lo, _ = _pieces(b)
    return _dot(a_hi, b_hi) + (_dot(a_hi, b_lo) + _dot(a_lo, b_hi))


@jax.custom_vjp
def _dot3(a, b):
    return _dot3_impl(a, b)


_dot3.defvjp(lambda a, b: (_dot3_impl(a, b), (a, b)),
             lambda res, ct: (_dot(ct, res[1], _DIMS["nt"]), _dot(res[0], ct, _DIMS["tn"])))


def _inv_impl(a, eye, strict):
    t = eye - a
    p = _dot(a, a)
    for level in range(5):
        t = t + _dot(t, p)
        if level < 4:
            p = _dot(p, p)
    t = t + _dot(t, eye - t - _dot3_impl(a, t))
    return jnp.where(strict > 0.5, t, eye)


@jax.custom_vjp
def _inv_given(a, t):
    return t.astype(F32)


_inv_given.defvjp(lambda a, t: (t.astype(F32), t),
                  lambda t, ct: (-_dot(_dot(t, ct, _DIMS["tn"]), t, _DIMS["nt"]), jnp.zeros_like(t)))


@jax.custom_vjp
def _lanes_join(a, b):
    return jnp.concatenate([a, b], axis=-1)


_lanes_join.defvjp(lambda a, b: (jnp.concatenate([a, b], axis=-1), None),
                   lambda _, ct: (ct[..., :ct.shape[-1] // 2], ct[..., ct.shape[-1] // 2:]))


@jax.custom_vjp
def _lanes_halves(y):
    h = y.shape[-1] // 2
    return y[..., :h], y[..., h:]


_lanes_halves.defvjp(lambda y: ((y[..., :y.shape[-1] // 2], y[..., y.shape[-1] // 2:]), None),
                     lambda _, ct: (jnp.concatenate(ct, axis=-1),))

GROUP = 4
GROUP_ROWS = GROUP * CHUNK


def _block_consts(n):
    ii = lax.broadcasted_iota(jnp.int32, (n, n), 0)
    jj = lax.broadcasted_iota(jnp.int32, (n, n), 1)
    shift = CHUNK.bit_length() - 1
    same = jnp.right_shift(ii, shift) == jnp.right_shift(jj, shift)
    return same & (ii >= jj), same & (ii <= jj), same & (ii > jj), same, ii == jj


def _lane0(n):
    s = (lax.broadcasted_iota(jnp.int32, (LANES, n), 0) == 0).astype(F32)
    st = (lax.broadcasted_iota(jnp.int32, (n, LANES), 1) == 0).astype(F32)
    return s, st


def _dn_group(q, k, v, g, beta, t_saved=None):
    n = GROUP_ROWS
    low_b, upp_b, strict_b, _, eye_b = _block_consts(n)
    low, upp, eye = low_b.astype(F32), upp_b.astype(F32), eye_b.astype(F32)
    gc = _sel_left(low, upp, g)
    per_chunk = (g.shape[0], GROUP, CHUNK, LANES)
    g_last = jnp.sum(g.reshape(per_chunk), axis=2, keepdims=True)
    gl = jnp.broadcast_to(g_last, per_chunk).reshape(g.shape)
    s, st = _lane0(n)
    col = _sel_right(gc, s, st)
    row = jnp.swapaxes(col, 1, 2)
    decay = jnp.exp(jnp.where(low_b, col - row, -jnp.inf))
    kb = k * beta
    vb = v * beta
    a = jnp.where(strict_b, _dot(kb, k, _DIMS["nt"]) * decay, 0.0)
    t = _inv_impl(a, eye, strict_b.astype(F32)) if t_saved is None else _inv_given(a, t_saved)
    u, w = _lanes_halves(_dot3(t, _lanes_join(vb, kb * jnp.exp(gc))))
    fold = (jnp.bitwise_and(lax.broadcasted_iota(jnp.int32, (n, CHUNK), 0), CHUNK - 1)
            == lax.broadcasted_iota(jnp.int32, (n, CHUNK), 1)).astype(F32)
    fold_t = (jnp.bitwise_and(lax.broadcasted_iota(jnp.int32, (CHUNK, n), 1), CHUNK - 1)
              == lax.broadcasted_iota(jnp.int32, (CHUNK, n), 0)).astype(F32)
    qk = _sel_right(_dot(q, k, _DIMS["nt"]) * decay, fold, fold_t)
    return u, w, q * jnp.exp(gc), k * jnp.exp(gl - gc), qk, jnp.exp(g_last), t


def _dn_step(s, u, w, qe, kd, qk, egl):
    v_new = u - _dot(w, s)
    o = _dot(qe, s) + _dot(qk, v_new)
    s_new = s * egl + _dot(kd, v_new, _DIMS["tn"])
    return s_new, o


def _swa_block(q, kband, vband, qg, kg, sinks, band):
    kn = _rms(kband, kg)
    qn = _rms(q, qg) * (SWA_DIM ** -0.5)
    logits = _dot(qn, kn, _DIMS["nt"]) + band
    m = lax.stop_gradient(jnp.maximum(jnp.max(logits, axis=-1, keepdims=True), sinks))
    p = jnp.exp(logits - m)
    denom = jnp.sum(p, axis=-1, keepdims=True) + jnp.exp(sinks - m)
    return _dot(p * (1.0 / denom), vband)


def _adamw(w, g, m, v):
    m = ADAM_B1 * m + (1.0 - ADAM_B1) * g
    v = ADAM_B2 * v + (1.0 - ADAM_B2) * jnp.square(g)
    m_hat = m / (1.0 - ADAM_B1 ** ADAM_STEP)
    v_hat = v / (1.0 - ADAM_B2 ** ADAM_STEP)
    delta = -ADAM_LR * (m_hat / (jnp.sqrt(v_hat) + ADAM_EPS) + ADAM_WD * w)
    return delta, m, v


def _row(tm, c, cb=0):
    return pl.BlockSpec((tm, c), lambda i, cb=cb: (i, cb))


def _full(shape):
    nd = len(shape)
    return pl.BlockSpec(shape, lambda *_, nd=nd: (0,) * nd)


def _norm_fwd(x, gain, name, tm=1024):
    S = x.shape[0]

    def body(x_ref, g_ref, h_ref):
        h_ref[...] = _rms(x_ref[...], g_ref[...]).astype(BF16)

    return pl.pallas_call(
        body, grid=(S // tm,), in_specs=[_row(tm, D_MODEL), _full((1, D_MODEL))],
        out_specs=_row(tm, D_MODEL), out_shape=jax.ShapeDtypeStruct((S, D_MODEL), BF16),
        name=name, compiler_params=_params(("parallel",)))(x, gain)


def _shift_down(x, s):
    row = lax.broadcasted_iota(jnp.int32, x.shape, 0)
    return jnp.where(row >= s, pltpu.roll(x, s, axis=0), 0.0)


def _shift_up(x, s):
    n = x.shape[0]
    row = lax.broadcasted_iota(jnp.int32, x.shape, 0)
    return jnp.where(row < n - s, pltpu.roll(x, n - s, axis=0), 0.0)


def _conv(x, w):
    out = w[DN_CONV - 1:DN_CONV] * x
    for s in range(1, DN_CONV):
        out = out + w[DN_CONV - 1 - s:DN_CONV - s] * _shift_down(x, s)
    return out


def _dn_conv_fwd(proj, conv_w):
    S = proj.shape[0]
    nb = DN_QKV // LANES

    def body(x_ref, w_ref, o_ref):
        j = pl.program_id(0)
        q_scale = jnp.where(j < DN_HEADS, DN_DIM ** -0.5, 1.0).astype(F32)
        o_ref[...] = _dn_post(_conv(x_ref[...], w_ref[...]), j >= 2 * DN_HEADS, q_scale)

    return pl.pallas_call(
        body, grid=(nb,),
        in_specs=[pl.BlockSpec((S, LANES), lambda j: (0, P_QKV // LANES + j)),
                  pl.BlockSpec((DN_CONV, LANES), lambda j: (0, j))],
        out_specs=pl.BlockSpec((S, LANES), lambda j: (0, j)),
        out_shape=jax.ShapeDtypeStruct((S, DN_QKV), F32), name="dn_conv_fwd",
        compiler_params=_params(("parallel",)))(proj, conv_w)


def _dn_conv_bwd(proj, conv_w, dqkvn, dproj):
    S = proj.shape[0]
    nb = DN_QKV // LANES

    def body(x_ref, w_ref, d_ref, _, dx_ref, dw_ref):
        j = pl.program_id(0)
        q_scale = jnp.where(j < DN_HEADS, DN_DIM ** -0.5, 1.0).astype(F32)
        x = x_ref[...]
        w = w_ref[...]
        _, vjp = jax.vjp(lambda c: _dn_post(c, j >= 2 * DN_HEADS, q_scale), _conv(x, w))
        (dc,) = vjp(d_ref[0])
        dx = w[DN_CONV - 1:DN_CONV] * dc
        dw_ref[DN_CONV - 1:DN_CONV, :] = jnp.sum(dc * x, axis=0, keepdims=True)
        for s in range(1, DN_CONV):
            dx = dx + w[DN_CONV - 1 - s:DN_CONV - s] * _shift_up(dc, s)
            dw_ref[DN_CONV - 1 - s:DN_CONV - s, :] = jnp.sum(dc * _shift_down(x, s), axis=0, keepdims=True)
        dx_ref[...] = dx.astype(BF16)

    return pl.pallas_call(
        body, grid=(nb,),
        in_specs=[pl.BlockSpec((S, LANES), lambda j: (0, P_QKV // LANES + j)),
                  pl.BlockSpec((DN_CONV, LANES), lambda j: (0, j)),
                  pl.BlockSpec((1, S, LANES), lambda j: (lax.div(j, DN_HEADS), 0, lax.rem(j, DN_HEADS))),
                  pl.BlockSpec(memory_space=pl.ANY)],
        out_specs=[pl.BlockSpec((S, LANES), lambda j: (0, P_QKV // LANES + j)),
                   pl.BlockSpec((DN_CONV, LANES), lambda j: (0, j))],
        out_shape=[jax.ShapeDtypeStruct(dproj.shape, dproj.dtype), jax.ShapeDtypeStruct((DN_CONV, DN_QKV), F32)],
        input_output_aliases={3: 0},
        name="dn_conv_bwd", compiler_params=_params(("parallel",)))(proj, conv_w, dqkvn, dproj)


def _expanders():
    eb = np.zeros((LANES, DN_WIDTH), np.float32)
    ea = np.zeros((LANES, DN_WIDTH), np.float32)
    for h in range(DN_HEADS):
        eb[h, h * DN_DIM:(h + 1) * DN_DIM] = 1.0
        ea[DN_HEADS + h, h * DN_DIM:(h + 1) * DN_DIM] = 1.0
    return jnp.asarray(eb), jnp.asarray(ea), jnp.asarray(eb.T), jnp.asarray(ea.T)


def _dn_gate_args(a_log, dt_bias):
    alog = jnp.repeat(a_log.reshape(1, DN_HEADS), DN_DIM, axis=1)
    dtb = _pad_to(jnp.pad(dt_bias.reshape(1, DN_HEADS), ((0, 0), (DN_HEADS, 0))), (1, LANES))
    return _expanders() + (alog, dtb)


def _dn_gate_specs(tm):
    return [_row(tm, LANES, P_BA // LANES), _full((LANES, DN_WIDTH)), _full((LANES, DN_WIDTH)),
            _full((DN_WIDTH, LANES)), _full((DN_WIDTH, LANES)), _full((1, DN_WIDTH)), _full((1, LANES))]


def _dn_gate_fn(ba, eb, ea, ebt, eat, alog, dtb):
    beta = _sel_right(jax.nn.sigmoid(ba), eb, ebt)
    g = -jnp.exp(alog) * _sel_right(jax.nn.softplus(ba + dtb), ea, eat)
    return beta, g


def _dn_gate_fwd(proj, a_log, dt_bias, tm=1024):
    S = proj.shape[0]
    args = _dn_gate_args(a_log, dt_bias)

    def body(ba_ref, eb_ref, ea_ref, ebt_ref, eat_ref, al_ref, dt_ref, beta_ref, g_ref):
        beta, g = _dn_gate_fn(ba_ref[...], eb_ref[...], ea_ref[...], ebt_ref[...], eat_ref[...], al_ref[...],
                              dt_ref[...])
        beta_ref[...] = beta
        g_ref[...] = g

    return pl.pallas_call(
        body, grid=(S // tm,), in_specs=_dn_gate_specs(tm), out_specs=[_row(tm, DN_WIDTH), _row(tm, DN_WIDTH)],
        out_shape=[jax.ShapeDtypeStruct((S, DN_WIDTH), F32), jax.ShapeDtypeStruct((S, DN_WIDTH), F32)],
        name="dn_gate_fwd", compiler_params=_params(("parallel",)))(proj, *args)


def _dn_gate_bwd(proj, a_log, dt_bias, dbeta, dg, dproj, tm=1024):
    S = proj.shape[0]
    args = _dn_gate_args(a_log, dt_bias)

    def body(ba_ref, eb_ref, ea_ref, ebt_ref, eat_ref, al_ref, dt_ref, dbeta_ref, dg_ref, _, dba_ref, dal_ref,
             ddt_ref):
        eb, ea, ebt, eat = eb_ref[...], ea_ref[...], ebt_ref[...], eat_ref[...]
        _, vjp = jax.vjp(lambda ba, al, dt: _dn_gate_fn(ba, eb, ea, ebt, eat, al, dt), ba_ref[...], al_ref[...],
                         dt_ref[...])
        dba, dal, ddt = vjp((dbeta_ref[...], dg_ref[...]))
        dba_ref[...] = dba.astype(BF16)

        @pl.when(pl.program_id(0) == 0)
        def _():
            dal_ref[...] = jnp.zeros_like(dal_ref)
            ddt_ref[...] = jnp.zeros_like(ddt_ref)

        dal_ref[...] += dal
        ddt_ref[...] += ddt

    return pl.pallas_call(
        body, grid=(S // tm,),
        in_specs=_dn_gate_specs(tm) + [_row(tm, DN_WIDTH), _row(tm, DN_WIDTH), pl.BlockSpec(memory_space=pl.ANY)],
        out_specs=[_row(tm, LANES, P_BA // LANES), _full((1, DN_WIDTH)), _full((1, LANES))],
        out_shape=[jax.ShapeDtypeStruct(dproj.shape, dproj.dtype), jax.ShapeDtypeStruct((1, DN_WIDTH), F32),
                   jax.ShapeDtypeStruct((1, LANES), F32)],
        input_output_aliases={len(args) + 3: 0},
        name="dn_gate_bwd", compiler_params=_params(("arbitrary",)))(proj, *args, dbeta, dg, dproj)


PREP_GROUPS = 8
PREP_CHUNKS = GROUP * PREP_GROUPS


def _dn_prep_specs():
    rows = PREP_CHUNKS * CHUNK
    q = pl.BlockSpec((rows, LANES), lambda h, c: (c, h))
    k = pl.BlockSpec((rows, LANES), lambda h, c: (c, DN_HEADS + h))
    v = pl.BlockSpec((rows, LANES), lambda h, c: (c, 2 * DN_HEADS + h))
    qk = pl.BlockSpec((1, rows, CHUNK), lambda h, c: (h, c, 0))
    egl = pl.BlockSpec((1, PREP_CHUNKS, 1, LANES), lambda h, c: (h, c, 0, 0))
    return q, k, v, qk, egl


def _dn_prep_fwd(qkvn, g, beta):
    S = qkvn.shape[0]
    nc = S // CHUNK
    q, k, v, qks, egl = _dn_prep_specs()

    def body(q_ref, k_ref, v_ref, g_ref, b_ref, u_ref, w_ref, qe_ref, kd_ref, qk_ref, egl_ref, t_ref):
        rows = PREP_CHUNKS * CHUNK
        grp = (PREP_GROUPS, GROUP_ROWS, LANES)
        u, w, qe, kd, qk, e, t = _dn_group(q_ref[...].reshape(grp), k_ref[...].reshape(grp), v_ref[...].reshape(grp),
                                           g_ref[...].reshape(grp), b_ref[...].reshape(grp))
        u_ref[...] = u.reshape(rows, LANES)
        w_ref[...] = w.reshape(rows, LANES)
        qe_ref[...] = qe.reshape(rows, LANES)
        kd_ref[...] = kd.reshape(rows, LANES)
        t_ref[0] = t.reshape(rows, GROUP_ROWS).astype(BF16)
        qk_ref[0] = qk.reshape(rows, CHUNK)
        egl_ref[0] = e.reshape(PREP_CHUNKS, 1, LANES)

    wide = jax.ShapeDtypeStruct((S, DN_WIDTH), F32)
    return pl.pallas_call(
        body, grid=(DN_HEADS, nc // PREP_CHUNKS), in_specs=[q, k, v, q, q],
        out_specs=[q, q, q, q, qks, egl, _dn_tinv_spec()],
        out_shape=[wide, wide, wide, wide, jax.ShapeDtypeStruct((DN_HEADS, S, CHUNK), F32),
                   jax.ShapeDtypeStruct((DN_HEADS, nc, 1, LANES), F32),
                   jax.ShapeDtypeStruct((DN_HEADS, S, GROUP_ROWS), BF16)],
        name="dn_prep_fwd", compiler_params=_params(("parallel", "parallel")))(qkvn, qkvn, qkvn, g, beta)


def _dn_tinv_spec():
    return pl.BlockSpec((1, PREP_CHUNKS * CHUNK, GROUP_ROWS), lambda h, c: (h, c, 0))


def _dn_prep_bwd(qkvn, g, beta, tinv, du, dw, dqe, dkd, dqk, degl):
    S = qkvn.shape[0]
    nc = S // CHUNK
    q, k, v, qks, egl = _dn_prep_specs()

    def body(q_ref, k_ref, v_ref, g_ref, b_ref, t_ref, du_ref, dw_ref, dqe_ref, dkd_ref, dqk_ref, degl_ref,
             dqkv_ref, dg_ref, db_ref):
        rows = PREP_CHUNKS * CHUNK
        grp = (PREP_GROUPS, GROUP_ROWS, LANES)
        t_saved = t_ref[0].reshape(PREP_GROUPS, GROUP_ROWS, GROUP_ROWS)
        _, vjp = jax.vjp(lambda *x: _dn_group(*x, t_saved=t_saved)[:6], q_ref[...].reshape(grp),
                         k_ref[...].reshape(grp), v_ref[...].reshape(grp), g_ref[...].reshape(grp),
                         b_ref[...].reshape(grp))
        dq, dk, dv, dg, db = vjp((du_ref[...].reshape(grp), dw_ref[...].reshape(grp), dqe_ref[...].reshape(grp),
                                  dkd_ref[...].reshape(grp), dqk_ref[0].reshape(PREP_GROUPS, GROUP_ROWS, CHUNK),
                                  degl_ref[0].reshape(PREP_GROUPS, GROUP, 1, LANES)))
        dqkv_ref[0] = dq.reshape(rows, LANES)
        dqkv_ref[1] = dk.reshape(rows, LANES)
        dqkv_ref[2] = dv.reshape(rows, LANES)
        dg_ref[...] = dg.reshape(rows, LANES)
        db_ref[...] = db.reshape(rows, LANES)

    wide = jax.ShapeDtypeStruct((S, DN_WIDTH), F32)
    rows = PREP_CHUNKS * CHUNK
    return pl.pallas_call(
        body, grid=(DN_HEADS, nc // PREP_CHUNKS), in_specs=[q, k, v, q, q, _dn_tinv_spec(), q, q, q, q, qks, egl],
        out_specs=[pl.BlockSpec((3, rows, LANES), lambda h, c: (0, c, h)), q, q],
        out_shape=[jax.ShapeDtypeStruct((3, S, DN_WIDTH), F32), wide, wide],
        name="dn_prep_bwd", compiler_params=_params(("parallel", "parallel")),
    )(qkvn, qkvn, qkvn, g, beta, tinv, du, dw, dqe, dkd, dqk, degl)


SCAN_CHUNKS = 16


def _dn_scan_specs(nc, reverse):
    nb = nc // SCAN_CHUNKS

    def cidx(c):
        return nb - 1 - c if reverse else c

    hc = pl.BlockSpec((SCAN_CHUNKS * CHUNK, DN_WIDTH), lambda c: (cidx(c), 0))
    qk = pl.BlockSpec((DN_HEADS, SCAN_CHUNKS * CHUNK, CHUNK), lambda c: (0, cidx(c), 0))
    egl = pl.BlockSpec((DN_HEADS, SCAN_CHUNKS, 1, LANES), lambda c: (0, cidx(c), 0, 0))
    st = pl.BlockSpec((DN_HEADS, SCAN_CHUNKS, DN_DIM, DN_DIM), lambda c: (0, cidx(c), 0, 0))
    return hc, qk, egl, st


def _heads(ref, i):
    return jnp.stack([ref[pl.ds(i * CHUNK, CHUNK), pl.ds(h * DN_DIM, DN_DIM)] for h in range(DN_HEADS)])


def _dn_scan_fwd(u, w, qe, kd, qk, egl):
    S = u.shape[0]
    nc = S // CHUNK
    hc, qks, egls, st = _dn_scan_specs(nc, False)

    def body(u_ref, w_ref, qe_ref, kd_ref, qk_ref, egl_ref, o_ref, st_ref, s_scr):
        @pl.when(pl.program_id(0) == 0)
        def _():
            s_scr[...] = jnp.zeros_like(s_scr)

        s = s_scr[...]
        for i in range(SCAN_CHUNKS):
            rows = pl.ds(i * CHUNK, CHUNK)
            st_ref[:, i] = s
            s, o = _dn_step(s, _heads(u_ref, i), _heads(w_ref, i), _heads(qe_ref, i), _heads(kd_ref, i),
                            qk_ref[:, rows, :], egl_ref[:, i])
            for h in range(DN_HEADS):
                o_ref[rows, pl.ds(h * DN_DIM, DN_DIM)] = o[h]
        s_scr[...] = s

    return pl.pallas_call(
        body, grid=(nc // SCAN_CHUNKS,), in_specs=[hc, hc, hc, hc, qks, egls], out_specs=[hc, st],
        out_shape=[jax.ShapeDtypeStruct((S, DN_WIDTH), F32), jax.ShapeDtypeStruct((DN_HEADS, nc, DN_DIM, DN_DIM), F32)],
        scratch_shapes=[pltpu.VMEM((DN_HEADS, DN_DIM, DN_DIM), F32)], name="dn_scan_fwd",
        compiler_params=_params(("arbitrary",)))(u, w, qe, kd, qk, egl)


def _dn_scan_bwd(u, w, qe, kd, qk, egl, states, do):
    S = u.shape[0]
    nc = S // CHUNK
    hc, qks, egls, st = _dn_scan_specs(nc, True)

    def body(u_ref, w_ref, qe_ref, kd_ref, qk_ref, egl_ref, st_ref, do_ref,
             du_ref, dw_ref, dqe_ref, dkd_ref, dqk_ref, degl_ref, ds_scr):
        @pl.when(pl.program_id(0) == 0)
        def _():
            ds_scr[...] = jnp.zeros_like(ds_scr)

        ds = ds_scr[...]
        for i in reversed(range(SCAN_CHUNKS)):
            rows = pl.ds(i * CHUNK, CHUNK)
            _, vjp = jax.vjp(_dn_step, st_ref[:, i], _heads(u_ref, i), _heads(w_ref, i), _heads(qe_ref, i),
                             _heads(kd_ref, i), qk_ref[:, rows, :], egl_ref[:, i])
            ds, du, dw, dqe, dkd, dqk, degl = vjp((ds, _heads(do_ref, i)))
            dqk_ref[:, rows, :] = dqk
            degl_ref[:, i] = degl
            for h in range(DN_HEADS):
                cols = pl.ds(h * DN_DIM, DN_DIM)
                du_ref[rows, cols] = du[h]
                dw_ref[rows, cols] = dw[h]
                dqe_ref[rows, cols] = dqe[h]
                dkd_ref[rows, cols] = dkd[h]
        ds_scr[...] = ds

    wide = jax.ShapeDtypeStruct((S, DN_WIDTH), F32)
    return pl.pallas_call(
        body, grid=(nc // SCAN_CHUNKS,), in_specs=[hc, hc, hc, hc, qks, egls, st, hc],
        out_specs=[hc, hc, hc, hc, qks, egls],
        out_shape=[wide, wide, wide, wide, jax.ShapeDtypeStruct((DN_HEADS, S, CHUNK), F32),
                   jax.ShapeDtypeStruct((DN_HEADS, nc, 1, LANES), F32)],
        scratch_shapes=[pltpu.VMEM((DN_HEADS, DN_DIM, DN_DIM), F32)], name="dn_scan_bwd",
        compiler_params=_params(("arbitrary",)))(u, w, qe, kd, qk, egl, states, do)


def _dn_out_fwd(o, proj, gain, tm=1024):
    S = o.shape[0]

    def body(o_ref, z_ref, g_ref, y_ref):
        y_ref[...] = _dn_out(o_ref[...], z_ref[...], g_ref[...]).astype(BF16)

    hs = pl.BlockSpec((tm, LANES), lambda i, h: (i, h))
    zs = pl.BlockSpec((tm, LANES), lambda i, h: (i, P_Z // LANES + h))
    return pl.pallas_call(
        body, grid=(S // tm, DN_HEADS), in_specs=[hs, zs, _full((1, DN_DIM))], out_specs=hs,
        out_shape=jax.ShapeDtypeStruct((S, DN_WIDTH), BF16), name="dn_out_fwd",
        compiler_params=_params(("parallel", "parallel")))(o, proj, gain)


_ANY = pl.BlockSpec(memory_space=pl.ANY)


def _dn_out_bwd(o, proj, gain, dy, dproj, tm=1024):
    S = o.shape[0]

    def body(o_ref, z_ref, g_ref, dy_ref, _, do_ref, dz_ref, dg_ref):
        _, vjp = jax.vjp(_dn_out, o_ref[...], z_ref[...], g_ref[...])
        do, dz, dg = vjp(dy_ref[...])
        do_ref[...] = do
        dz_ref[...] = dz.astype(BF16)

        @pl.when((pl.program_id(0) == 0) & (pl.program_id(1) == 0))
        def _():
            dg_ref[...] = jnp.zeros_like(dg_ref)

        dg_ref[...] += dg

    hs = pl.BlockSpec((tm, LANES), lambda i, h: (i, h))
    zs = pl.BlockSpec((tm, LANES), lambda i, h: (i, P_Z // LANES + h))
    return pl.pallas_call(
        body, grid=(S // tm, DN_HEADS), in_specs=[hs, zs, _full((1, DN_DIM)), hs, _ANY],
        out_specs=[hs, zs, _full((1, DN_DIM))],
        out_shape=[jax.ShapeDtypeStruct((S, DN_WIDTH), F32), jax.ShapeDtypeStruct(dproj.shape, dproj.dtype),
                   jax.ShapeDtypeStruct((1, DN_DIM), F32)],
        input_output_aliases={4: 1},
        name="dn_out_bwd", compiler_params=_params(("arbitrary", "arbitrary")))(o, proj, gain, dy, dproj)


def _rel_buckets():
    qi = np.arange(BLOCK)[:, None]
    kj = np.arange(2 * BLOCK)[None, :]
    n = np.maximum(BLOCK + qi - kj, 0)
    max_exact = REL_BUCKETS // 2
    nf = np.maximum(n, 1).astype(np.float32)
    large = max_exact + (np.log(nf / np.float32(max_exact)) / np.float32(math.log(REL_MAX_DIST / max_exact))
                         * np.float32(REL_BUCKETS - max_exact)).astype(np.int32)
    large = np.minimum(large, REL_BUCKETS - 1)
    return np.where(n < max_exact, n, large).astype(np.int32)


def _bias_fwd(rel_bias):
    buckets = jnp.asarray(_rel_buckets())

    def body(rb_ref, bk_ref, o_ref):
        bk = bk_ref[...]
        for h in range(SWA_HEADS):
            acc = jnp.zeros((BLOCK, 2 * BLOCK), F32)
            for b in range(REL_BUCKETS):
                acc = jnp.where(bk == b, rb_ref[b, h], acc)
            for first in range(2):
                o_ref[first, h] = jnp.where(_swa_mask(1 - first), acc, -jnp.inf)

    return pl.pallas_call(
        body, in_specs=[pl.BlockSpec(memory_space=pltpu.SMEM), pl.BlockSpec(memory_space=pltpu.VMEM)],
        out_specs=pl.BlockSpec(memory_space=pltpu.VMEM),
        out_shape=jax.ShapeDtypeStruct((2, SWA_HEADS, BLOCK, 2 * BLOCK), F32), name="swa_bias_fwd",
        compiler_params=_params())(rel_bias, buckets)


def _bias_bwd(dbias):
    buckets = jnp.asarray(_rel_buckets())

    def body(d_ref, bk_ref, o_ref):
        bk = bk_ref[...]
        lane = lax.broadcasted_iota(jnp.int32, (1, LANES), 1)
        for h in range(SWA_HEADS):
            d = d_ref[h]
            row = jnp.zeros((1, LANES), F32)
            for b in range(REL_BUCKETS):
                part = jnp.sum(jnp.where(bk == b, d, 0.0), axis=1, keepdims=True)
                row = jnp.where(lane == b, jnp.sum(part, axis=0, keepdims=True), row)
            o_ref[h:h + 1, :] = row

    return pl.pallas_call(
        body, in_specs=[pl.BlockSpec(memory_space=pltpu.VMEM), pl.BlockSpec(memory_space=pltpu.VMEM)],
        out_specs=pl.BlockSpec(memory_space=pltpu.VMEM),
        out_shape=jax.ShapeDtypeStruct((SWA_HEADS, LANES), F32), name="swa_bias_bwd",
        compiler_params=_params())(dbias, buckets)


def _swa_mask(n):
    qi = lax.broadcasted_iota(jnp.int32, (BLOCK, 2 * BLOCK), 0)
    kj = lax.broadcasted_iota(jnp.int32, (BLOCK, 2 * BLOCK), 1)
    dist = BLOCK + qi - kj
    return (dist >= 0) & (dist < WINDOW) & ((n > 0) | (kj >= BLOCK))


def _swa_in_specs():
    q = pl.BlockSpec((BLOCK, SWA_WIDTH), lambda n: (n, P_SQ // SWA_WIDTH))
    kc = pl.BlockSpec((BLOCK, SWA_KVW), lambda n: (n, P_SK // SWA_KVW))
    kp = pl.BlockSpec((BLOCK, SWA_KVW), lambda n: (jnp.maximum(n - 1, 0), P_SK // SWA_KVW))
    vc = pl.BlockSpec((BLOCK, SWA_KVW), lambda n: (n, P_SV // SWA_KVW))
    vp = pl.BlockSpec((BLOCK, SWA_KVW), lambda n: (jnp.maximum(n - 1, 0), P_SV // SWA_KVW))
    band = pl.BlockSpec((None, SWA_HEADS, BLOCK, 2 * BLOCK), lambda n: (jnp.where(n == 0, 1, 0), 0, 0, 0))
    small = [_full((1, SWA_DIM)), _full((1, SWA_DIM)), _full((1, SWA_HEADS)), band]
    return [q, kp, kc, vp, vc] + small


def _swa_load(q_ref, kp_ref, kc_ref, vp_ref, vc_ref, s_ref):
    q = jnp.stack([q_ref[:, pl.ds(h * SWA_DIM, SWA_DIM)] for h in range(SWA_HEADS)])
    kbands, vbands = [], []
    for kv in range(SWA_KV):
        cols = pl.ds(kv * SWA_DIM, SWA_DIM)
        kbands += [jnp.concatenate([kp_ref[:, cols], kc_ref[:, cols]], axis=0)] * SWA_GROUP
        vbands += [jnp.concatenate([vp_ref[:, cols], vc_ref[:, cols]], axis=0)] * SWA_GROUP
    sinks = jnp.stack([s_ref[:, pl.ds(h, 1)] for h in range(SWA_HEADS)])
    return q, jnp.stack(kbands), jnp.stack(vbands), sinks


def _swa_fwd(proj, q_gain, k_gain, sinks, bias):
    S = proj.shape[0]

    def body(q_ref, kp_ref, kc_ref, vp_ref, vc_ref, qg_ref, kg_ref, s_ref, bias_ref, y_ref):
        q, kband, vband, sk = _swa_load(q_ref, kp_ref, kc_ref, vp_ref, vc_ref, s_ref)
        out = _swa_block(q, kband, vband, qg_ref[...], kg_ref[...], sk, bias_ref[...])
        for h in range(SWA_HEADS):
            y_ref[:, pl.ds(h * SWA_DIM, SWA_DIM)] = out[h].astype(BF16)

    return pl.pallas_call(
        body, grid=(S // BLOCK,), in_specs=_swa_in_specs(),
        out_specs=pl.BlockSpec((BLOCK, SWA_WIDTH), lambda n: (n, 0)),
        out_shape=jax.ShapeDtypeStruct((S, SWA_WIDTH), BF16), name="swa_fwd",
        compiler_params=_params(("parallel",)))(proj, proj, proj, proj, proj, q_gain, k_gain, sinks, bias)


def _swa_bwd(proj, q_gain, k_gain, sinks, bias, dy, dproj):
    S = proj.shape[0]

    def body(q_ref, kp_ref, kc_ref, vp_ref, vc_ref, qg_ref, kg_ref, s_ref, bias_ref, dy_ref, _,
             dq_ref, dk_ref, dv_ref, dqg_ref, dkg_ref, ds_ref, dbias_ref):
        n = pl.program_id(0)

        @pl.when(n == 0)
        def _():
            for r in (dk_ref, dv_ref, dqg_ref, dkg_ref, ds_ref, dbias_ref):
                r[...] = jnp.zeros_like(r)

        cur = pl.ds(pl.multiple_of(n * BLOCK, BLOCK), BLOCK)
        prev = pl.ds(pl.multiple_of(jnp.maximum(n - 1, 0) * BLOCK, BLOCK), BLOCK)
        q, kband, vband, sk = _swa_load(q_ref, kp_ref, kc_ref, vp_ref, vc_ref, s_ref)
        _, vjp = jax.vjp(_swa_block, q, kband, vband, qg_ref[...], kg_ref[...], sk, bias_ref[...])
        dy = jnp.stack([dy_ref[:, pl.ds(h * SWA_DIM, SWA_DIM)] for h in range(SWA_HEADS)])
        dq, dkb, dvb, dqg, dkg, dsk, dbs = vjp(dy)
        for h in range(SWA_HEADS):
            dq_ref[:, pl.ds(h * SWA_DIM, SWA_DIM)] = dq[h].astype(BF16)
            ds_ref[:, pl.ds(h, 1)] += dsk[h]
        dbias_ref[...] += dbs
        dqg_ref[...] += dqg
        dkg_ref[...] += dkg
        for kv in range(SWA_KV):
            cols = pl.ds(kv * SWA_DIM, SWA_DIM)
            group = range(kv * SWA_GROUP, (kv + 1) * SWA_GROUP)
            dk_kv = sum(dkb[h] for h in group)
            dv_kv = sum(dvb[h] for h in group)
            dk_ref[cur, cols] += dk_kv[BLOCK:]
            dv_ref[cur, cols] += dv_kv[BLOCK:]

            @pl.when(n > 0)
            def _(cols=cols, dk_kv=dk_kv, dv_kv=dv_kv):
                dk_ref[prev, cols] += dk_kv[:BLOCK]
                dv_ref[prev, cols] += dv_kv[:BLOCK]

    return pl.pallas_call(
        body, grid=(S // BLOCK,),
        in_specs=_swa_in_specs() + [pl.BlockSpec((BLOCK, SWA_WIDTH), lambda n: (n, 0)),
                                    pl.BlockSpec(memory_space=pl.ANY)],
        out_specs=[pl.BlockSpec((BLOCK, SWA_WIDTH), lambda n: (n, P_SQ // SWA_WIDTH)), _full((S, SWA_KVW)),
                   _full((S, SWA_KVW)), _full((1, SWA_DIM)), _full((1, SWA_DIM)), _full((1, SWA_HEADS)),
                   _full((SWA_HEADS, BLOCK, 2 * BLOCK))],
        out_shape=[jax.ShapeDtypeStruct(dproj.shape, dproj.dtype), jax.ShapeDtypeStruct((S, SWA_KVW), F32),
                   jax.ShapeDtypeStruct((S, SWA_KVW), F32), jax.ShapeDtypeStruct((1, SWA_DIM), F32),
                   jax.ShapeDtypeStruct((1, SWA_DIM), F32), jax.ShapeDtypeStruct((1, SWA_HEADS), F32),
                   jax.ShapeDtypeStruct((SWA_HEADS, BLOCK, 2 * BLOCK), F32)],
        input_output_aliases={10: 0},
        name="swa_bwd", compiler_params=_params(("arbitrary",)),
    )(proj, proj, proj, proj, proj, q_gain, k_gain, sinks, bias, dy, dproj)


def _kv_into(dproj, dk, dv, tm=1024):
    S = dk.shape[0]

    def body(dk_ref, dv_ref, _, o_ref):
        o_ref[:, :SWA_KVW] = dk_ref[...].astype(BF16)
        o_ref[:, SWA_KVW:] = dv_ref[...].astype(BF16)

    return pl.pallas_call(
        body, grid=(S // tm,), in_specs=[_row(tm, SWA_KVW), _row(tm, SWA_KVW), pl.BlockSpec(memory_space=pl.ANY)],
        out_specs=_row(tm, 2 * SWA_KVW, P_SK // (2 * SWA_KVW)),
        out_shape=jax.ShapeDtypeStruct(dproj.shape, dproj.dtype), input_output_aliases={2: 0},
        name="swa_kv_into", compiler_params=_params(("parallel",)))(dk, dv, dproj)


def _position():
    return lax.axis_index("x"), lax.axis_index("y"), lax.axis_index("c")


def _all_gather(shards, name="all_gather_weights"):
    na = len(shards)

    def body(*refs):
        x_refs, out_refs = refs[:na], refs[na:2 * na]
        send_sems, recv_sems, local_sems = refs[2 * na:]
        x, y, c = _position()
        me, sibling = (x, y, c), (x, y, 1 - c)
        chips = [(1 - x, y), (x, 1 - y), (1 - x, 1 - y)]

        def copy(a, k, block, to, own=False):
            px, py, pc = block
            slot = out_refs[a].at[4 * px + 2 * py + pc]
            return pltpu.make_async_remote_copy(
                src_ref=x_refs[a] if own else slot, dst_ref=slot, send_sem=send_sems.at[7 * a + k],
                recv_sem=recv_sems.at[7 * a + k], device_id=to, device_id_type=MESH_ID)

        mine = [pltpu.make_async_copy(x_refs[a], out_refs[a].at[4 * x + 2 * y + c], local_sems.at[a])
                for a in range(na)]
        for cp in mine:
            cp.start()
        first = []
        for a in range(na):
            first.append(copy(a, 0, me, sibling, own=True))
            first += [copy(a, 1 + j, me, (*chip, c), own=True) for j, chip in enumerate(chips)]
        for cp in first:
            cp.start()
        passed = []
        for j, chip in enumerate(chips):
            for a in range(na):
                copy(a, 1 + j, (*chip, c), me).wait_recv()
                passed.append(copy(a, 4 + j, (*chip, c), sibling))
                passed[-1].start()
        for a in range(na):
            copy(a, 0, sibling, me).wait_recv()
            for j, chip in enumerate(chips):
                copy(a, 4 + j, (*chip, 1 - c), me).wait_recv()
        for cp in first + passed:
            cp.wait_send()
        for cp in mine:
            cp.wait()

    return pl.pallas_call(
        body, in_specs=[pl.BlockSpec(memory_space=pl.ANY)] * na, out_specs=[pl.BlockSpec(memory_space=pl.ANY)] * na,
        out_shape=[jax.ShapeDtypeStruct((N_DEV,) + s.shape, s.dtype) for s in shards],
        scratch_shapes=[pltpu.SemaphoreType.DMA((7 * na,)), pltpu.SemaphoreType.DMA((7 * na,)),
                        pltpu.SemaphoreType.DMA((na,))],
        name=name)(*shards)


_HBM = pl.BlockSpec(memory_space=pltpu.HBM)
_SEM = pl.BlockSpec(memory_space=pltpu.SEMAPHORE)
_DATAFLOW = pltpu.SideEffectType.DATAFLOW_SIDE_EFFECTING


def _peers(x, y, c):
    out = []
    for k in range(1, N_DEV):
        px, py, pc = x ^ (k >> 2), y ^ ((k >> 1) & 1), c ^ (k & 1)
        out.append(((px, py, pc), 4 * px + 2 * py + pc))
    return out


def _split_copies(src_refs, land_refs, send_sems, recv_sems, scatter):
    x, y, c = _position()
    me = 4 * x + 2 * y + c
    sends, recvs = [], []
    for k, (peer_id, peer) in enumerate(_peers(x, y, c)):
        for a, (src, land) in enumerate(zip(src_refs, land_refs)):
            sems = dict(send_sem=send_sems.at[7 * a + k], recv_sem=recv_sems.at[7 * a + k],
                        device_id=peer_id, device_id_type=MESH_ID)
            mine = src.at[peer] if scatter else src
            sends.append(pltpu.make_async_remote_copy(src_ref=mine, dst_ref=land.at[me], **sems))
            recvs.append(pltpu.make_async_remote_copy(src_ref=mine, dst_ref=land.at[peer], **sems))
    return sends, recvs


def _all_gather_direct(shards, name, after):
    na, nb = len(shards), len(after)

    def body(*refs):
        x_refs, out_refs = refs[:na], refs[na + nb:2 * na + nb]
        send_sems, recv_sems, local_sems = refs[2 * na + nb:]
        x, y, c = _position()
        me = 4 * x + 2 * y + c
        local = [pltpu.make_async_copy(x_refs[a], out_refs[a].at[me], local_sems.at[a]) for a in range(na)]
        sends, recvs = _split_copies(x_refs, out_refs, send_sems, recv_sems, False)
        for cp in local + sends:
            cp.start()
        for cp in recvs:
            cp.wait_recv()
        for cp in sends:
            cp.wait_send()
        for cp in local:
            cp.wait()

    return pl.pallas_call(
        body, in_specs=[pl.BlockSpec(memory_space=pl.ANY)] * (na + nb),
        out_specs=[pl.BlockSpec(memory_space=pl.ANY)] * na,
        out_shape=[jax.ShapeDtypeStruct((N_DEV,) + s.shape, s.dtype) for s in shards],
        scratch_shapes=[pltpu.SemaphoreType.DMA((7 * na,)), pltpu.SemaphoreType.DMA((7 * na,)),
                        pltpu.SemaphoreType.DMA((na,))],
        name=name)(*shards, *after)


def _exchange_start(srcs, scatter, name, after=None):
    na = len(srcs)
    lands = [lax.empty(s.shape if scatter else (N_DEV,) + s.shape, s.dtype) for s in srcs]
    extra = [] if after is None else [after]

    def body(*refs):
        src_refs, land_refs = refs[:na], refs[na:2 * na]
        send_sems, recv_sems = refs[2 * na + len(extra)], refs[2 * na + len(extra) + 1]
        token = refs[-1]
        sends, _ = _split_copies(src_refs, land_refs, send_sems, recv_sems, scatter)
        for cp in sends:
            cp.start()
        token[...] = jnp.zeros_like(token)

    hbm = lambda a: pltpu.HBM(a.shape, a.dtype)
    out = pl.pallas_call(
        body, name=name,
        out_shape=(pltpu.SemaphoreType.DMA((7 * na,)), pltpu.SemaphoreType.DMA((7 * na,)),
                   *[hbm(s) for s in srcs], *[hbm(l) for l in lands], jax.ShapeDtypeStruct((8, LANES), F32)),
        in_specs=[_HBM] * (2 * na) + [pl.BlockSpec(memory_space=pl.ANY)] * len(extra),
        out_specs=(_SEM, _SEM, *[_HBM] * (2 * na), pl.BlockSpec(memory_space=pltpu.VMEM)),
        input_output_aliases={i: 2 + i for i in range(2 * na)},
        compiler_params=pltpu.CompilerParams(has_side_effects=_DATAFLOW),
    )(*[pltpu.with_memory_space_constraint(s, pltpu.HBM) for s in srcs],
      *[pltpu.with_memory_space_constraint(l, pltpu.HBM) for l in lands], *extra)
    return (out[0], out[1], list(out[2:2 + na]), list(out[2 + na:2 + 2 * na])), out[-1]


def _exchange_wait(handle, after, scatter, name):
    send_sems, recv_sems, srcs, lands = handle
    na = len(srcs)

    def body(*refs):
        src_refs, land_refs = refs[:na], refs[na:2 * na]
        s_sems, r_sems = refs[2 * na], refs[2 * na + 1]
        sends, recvs = _split_copies(src_refs, land_refs, s_sems, r_sems, scatter)
        for cp in sends:
            cp.wait_send()
        for cp in recvs:
            cp.wait_recv()

    hbm = lambda a: pltpu.HBM(a.shape, a.dtype)
    out = pl.pallas_call(
        body, name=name, out_shape=(*[hbm(s) for s in srcs], *[hbm(l) for l in lands]),
        in_specs=[_HBM] * (2 * na) + [_SEM, _SEM, pl.BlockSpec(memory_space=pl.ANY)],
        out_specs=tuple([_HBM] * (2 * na)), input_output_aliases={i: i for i in range(2 * na)},
        compiler_params=pltpu.CompilerParams(has_side_effects=_DATAFLOW),
    )(*srcs, *lands, send_sems, recv_sems, after)
    return list(out[:na]), list(out[na:])


def _own_slot(landed, own):
    me = 4 * lax.axis_index("x") + 2 * lax.axis_index("y") + lax.axis_index("c")
    return lax.dynamic_update_slice_in_dim(landed, own[None], me, axis=0)


def _adam_update(parts, w, m, v, name, tr=256):
    _, r, c = w.shape
    tr = _pick_rows(r, tr)
    cp = parts.shape[2]

    def body(p_ref, w_ref, m_ref, v_ref, g_ref, d_ref, nm_ref, nv_ref):
        g = p_ref[0, :, pl.ds(0, c)].astype(F32)
        for i in range(1, N_DEV):
            g = g + p_ref[i, :, pl.ds(0, c)].astype(F32)
        delta, nm, nv = _adamw(w_ref[0], g, m_ref[0], v_ref[0])
        g_ref[0] = g
        d_ref[0] = delta
        nm_ref[0] = nm
        nv_ref[0] = nv

    rs = pl.BlockSpec((1, tr, c), lambda i: (0, i, 0))
    return pl.pallas_call(
        body, grid=(r // tr,), in_specs=[pl.BlockSpec((N_DEV, tr, cp), lambda i: (0, i, 0)), rs, rs, rs],
        out_specs=[rs] * 4, out_shape=[jax.ShapeDtypeStruct((1, r, c), F32)] * 4, name=name,
        compiler_params=_params(("parallel",)))(parts, w, m, v)


def _pick_rows(rows, target):
    if rows <= target:
        return rows
    t = target
    while t >= 16:
        if rows % t == 0:
            return t
        t -= 16
    return rows


BIG = ("w_in", "w_branch_dn", "w_branch_swa", "w_out", "w_gate", "w_up", "w_down")
IN_SHARD, IN_WIRE = D_IN // N_DEV, 640
FF_SHARD, FF_WIRE = D_FF // N_DEV, 384
D_FFP = N_DEV * FF_WIRE
BIG_SHAPES = {"w_in": ((D_MODEL, IN_SHARD), (D_MODEL, IN_WIRE)),
              "w_branch_dn": ((DN_WIDTH, LANES), (DN_WIDTH, LANES)),
              "w_branch_swa": ((SWA_WIDTH, LANES), (SWA_WIDTH, LANES)),
              "w_out": ((LANES, D_MODEL), (LANES, D_MODEL)),
              "w_gate": ((D_MODEL, FF_SHARD), (D_MODEL, FF_WIRE)),
              "w_up": ((D_MODEL, FF_SHARD), (D_MODEL, FF_WIRE)),
              "w_down": ((FF_SHARD, D_MODEL), (FF_WIRE, D_MODEL))}
CONV_SHARD, CONV_WIRE = (DN_CONV, DN_QKV // N_DEV), (8, 256)


def _pad_to(a, shape):
    return jnp.pad(a, [(0, t - s) for s, t in zip(a.shape, shape)])


IN_TILE_ROWS = 256
_IN_SEGS = ((R_GATE, 2048, P_GATE), (R_QKV, DN_QKV, P_QKV), (R_Z, DN_WIDTH, P_Z), (R_SQ, SWA_WIDTH, P_SQ),
            (R_SK, SWA_KVW, P_SK), (R_SV, SWA_KVW, P_SV), (R_B, 8, P_BA))


def _w_in_from_blocks(blocks):
    tm = IN_TILE_ROWS

    def body(b_ref, o_ref):
        parts = []
        for rs, n, _ in _IN_SEGS:
            for dev in range(N_DEV):
                lo, hi = max(rs, IN_SHARD * dev), min(rs + n, IN_SHARD * (dev + 1))
                if lo < hi:
                    parts.append(b_ref[dev][:, lo - IN_SHARD * dev:hi - IN_SHARD * dev])
        parts.append(jnp.zeros((tm, P_WIDTH - P_BA - 8), b_ref.dtype))
        o_ref[...] = jnp.concatenate(parts, axis=1)

    return pl.pallas_call(
        body, grid=(D_MODEL // tm,), in_specs=[pl.BlockSpec((N_DEV, tm, IN_WIRE), lambda i: (0, i, 0))],
        out_specs=pl.BlockSpec((tm, P_WIDTH), lambda i: (i, 0)),
        out_shape=jax.ShapeDtypeStruct((D_MODEL, P_WIDTH), blocks.dtype), name="w_in_from_blocks",
        compiler_params=_params(("parallel",)))(blocks)


def _w_in_to_blocks(g):
    tm = IN_TILE_ROWS

    def body(g_ref, o_ref):
        for dev in range(N_DEV):
            parts = []
            for rs, n, ps in sorted(_IN_SEGS):
                lo, hi = max(rs, IN_SHARD * dev), min(rs + n, IN_SHARD * (dev + 1))
                if lo < hi:
                    parts.append(g_ref[:, ps + lo - rs:ps + hi - rs])
            parts.append(jnp.zeros((tm, IN_WIRE - IN_SHARD), g_ref.dtype))
            o_ref[dev] = jnp.concatenate(parts, axis=1)

    return pl.pallas_call(
        body, grid=(D_MODEL // tm,), in_specs=[pl.BlockSpec((tm, P_WIDTH), lambda i: (i, 0))],
        out_specs=pl.BlockSpec((N_DEV, tm, IN_WIRE), lambda i: (0, i, 0)),
        out_shape=jax.ShapeDtypeStruct((N_DEV, D_MODEL, IN_WIRE), g.dtype), name="w_in_to_blocks",
        compiler_params=_params(("parallel",)))(g)


SMALL = {"attn_norm": (0, (1, D_MODEL)), "ffn_norm": (1, (1, D_MODEL)), "dn_out_norm": (2, (1, DN_DIM)),
         "swa_q_norm": (3, (1, SWA_DIM)), "swa_k_norm": (4, (1, SWA_DIM)), "dn_a_log": (5, (1, DN_HEADS)),
         "dn_dt_bias": (6, (1, DN_HEADS)), "swa_sinks": (7, (1, SWA_HEADS)), "rel_bias": (8, (REL_BUCKETS, SWA_HEADS))}
SMALL_SHEET = (48, D_MODEL)


LOSS_ROW = 40


def _small_pack(grads, loss_local):
    names = list(SMALL)

    def body(*refs):
        o_ref = refs[-1]
        o_ref[...] = jnp.zeros_like(o_ref)
        for n, ref in zip(names, refs):
            r0, (nr, nc) = SMALL[n]
            o_ref[r0:r0 + nr, 0:nc] = ref[...]
        o_ref[LOSS_ROW:LOSS_ROW + 1, 0:1] = refs[len(names)][...]

    return pl.pallas_call(
        body, in_specs=[pl.BlockSpec(memory_space=pltpu.VMEM)] * (len(names) + 1),
        out_specs=pl.BlockSpec(memory_space=pltpu.VMEM), out_shape=jax.ShapeDtypeStruct(SMALL_SHEET, F32),
        name="small_pack", compiler_params=_params())(*[grads[n].reshape(SMALL[n][1]) for n in names], loss_local)


def _small_update(sheets, w, m, v):
    names = list(SMALL)
    k = len(names)

    def body(*refs):
        p_ref = refs[0]
        ins, outs = refs[1:1 + 3 * k], refs[1 + 3 * k:]
        loss = p_ref[0, LOSS_ROW:LOSS_ROW + 1, 0:1]
        for i in range(1, N_DEV):
            loss = loss + p_ref[i, LOSS_ROW:LOSS_ROW + 1, 0:1]
        outs[4 * k][...] = loss
        for t, n in enumerate(names):
            r0, (nr, nc) = SMALL[n]
            g = p_ref[0, r0:r0 + nr, 0:nc]
            for i in range(1, N_DEV):
                g = g + p_ref[i, r0:r0 + nr, 0:nc]
            delta, nm, nv = _adamw(ins[t][...], g, ins[k + t][...], ins[2 * k + t][...])
            for kind, val in enumerate((g, delta, nm, nv)):
                outs[kind * k + t][...] = val

    shapes = [jax.ShapeDtypeStruct(SMALL[n][1], F32) for n in names]
    vm = pl.BlockSpec(memory_space=pltpu.VMEM)
    res = pl.pallas_call(
        body, in_specs=[vm] * (1 + 3 * k), out_specs=[vm] * (4 * k + 1),
        out_shape=shapes * 4 + [jax.ShapeDtypeStruct((1, 1), F32)], name="adam_small", compiler_params=_params(),
    )(sheets, *[d[n].reshape(SMALL[n][1]) for d in (w, m, v) for n in names])
    return {n: tuple(res[kind * k + t] for kind in range(4)) for t, n in enumerate(names)}, res[4 * k]


def kernel(x, attn_norm, w_in, dn_conv, dn_a_log, dn_dt_bias, dn_out_norm, swa_q_norm, swa_k_norm, swa_sinks, rel_bias, w_branch_dn, w_branch_swa, w_out, ffn_norm, w_gate, w_up, w_down, loss_target, m_attn_norm, m_w_in, m_dn_conv, m_dn_a_log, m_dn_dt_bias, m_dn_out_norm, m_swa_q_norm, m_swa_k_norm, m_swa_sinks, m_rel_bias, m_w_branch_dn, m_w_branch_swa, m_w_out, m_ffn_norm, m_w_gate, m_w_up, m_w_down, v_attn_norm, v_w_in, v_dn_conv, v_dn_a_log, v_dn_dt_bias, v_dn_out_norm, v_swa_q_norm, v_swa_k_norm, v_swa_sinks, v_rel_bias, v_w_branch_dn, v_w_branch_swa, v_w_out, v_ffn_norm, v_w_gate, v_w_up, v_w_down):
    args = dict(locals())
    S = x.shape[1]
    xs = x.reshape(S, D_MODEL)
    target = loss_target.reshape(S, D_MODEL)

    w_loc = {n: args[n].reshape(BIG_SHAPES[n][0]) for n in BIG}
    conv_loc = dn_conv.reshape(CONV_SHARD)
    wire = {n: _pad_to(w_loc[n], BIG_SHAPES[n][1]).astype(BF16) for n in BIG}
    first = _all_gather([wire["w_in"], _pad_to(conv_loc, CONV_WIRE)])
    later = [n for n in BIG if n != "w_in"]
    rest_handle, rest_token = _exchange_start([wire[n] for n in later], False, "gather_rest_start", after=first[1])
    w_pad = _w_in_from_blocks(first[0])
    conv_w = jnp.concatenate([first[1][d, :DN_CONV, :CONV_SHARD[1]] for d in range(N_DEV)], axis=1)

    h = _norm_fwd(xs, attn_norm + rest_token[0, 0], "norm1_fwd")
    proj = _mm([(h, w_pad)], "nn", F32, "mm_in", 1024, 1664, j_outer=True)
    qkvn = _dn_conv_fwd(proj, conv_w)
    beta, g = _dn_gate_fwd(proj, dn_a_log, dn_dt_bias)
    u, w, qe, kd, qk, egl, tinv = _dn_prep_fwd(qkvn, g, beta)
    o, states = _dn_scan_fwd(u, w, qe, kd, qk, egl)
    y_dn = _dn_out_fwd(o, proj, dn_out_norm)
    bias = _bias_fwd(rel_bias)
    y_swa = _swa_fwd(proj, swa_q_norm, swa_k_norm, swa_sinks, bias)
    rest_src, rest_land = _exchange_wait(rest_handle, y_swa, False, "gather_rest_wait")
    G = {n: _own_slot(land, src) for n, src, land in zip(later, rest_src, rest_land)}
    w_bdn, w_bswa, w_g, w_u = G["w_branch_dn"], G["w_branch_swa"], G["w_gate"], G["w_up"]
    w_o = G["w_out"].reshape(D_MODEL, D_MODEL)
    w_d = G["w_down"].reshape(D_FFP, D_MODEL)
    gates = [(proj, P_GATE // 512), (proj, (P_GATE + D_MODEL) // 512)]
    a_dn, a_swa, merged = _mm_fused(
        [(y_dn, w_bdn), (y_swa, w_bswa)], "nn", "mm_branch_merge", 1024, 512,
        lambda p, e: (p[0], p[1], _merge(e[0], e[1], p[0], p[1])), gates, (F32, F32, BF16), b_blocks=True)

    def resid_norm(p, e):
        x1 = e[0] + p[0]
        return x1, _rms(x1, e[1])

    x1, h2 = _mm_fused([(merged, w_o)], "nn", "mm_out_norm", 512, D_MODEL, resid_norm,
                       [(xs, 0), (ffn_norm, None)], (F32, BF16))
    gate, up, act = _mm_fused([(h2, w_g), (h2, w_u)], "nn", "mm_gate_up_act", 1024, 768,
                              lambda p, e: (p[0], p[1], _act(p[0], p[1])), [], (F32, F32, BF16),
                              j_outer=True, b_blocks=True)

    def loss_head(p, e):
        diff = e[0] + p[0] - e[1]
        dy = diff * (1.0 / D_MODEL)
        part = jnp.sum(jnp.mean(diff * diff, axis=-1, keepdims=True), axis=0, keepdims=True) * 0.5
        return dy, dy, part

    dy, dy_b, loss_local = _mm_fused([(act, w_d)], "nn", "mm_down_loss", 512, D_MODEL, loss_head,
                                     [(x1, 0), (target, 0)], (F32, BF16), sum_shape=(1, 1))

    def act_bwd(p, e):
        _, vjp = jax.vjp(_act, e[0], e[1])
        return vjp(p[0])

    dgate, dup = _mm_fused([(dy_b, w_d)], "nt", "mm_dact_act", 1024, 768, act_bwd, [(gate, 0), (up, 0)],
                           (BF16, BF16), j_outer=True)
    g_w_down = _mm([(act, dy_b)], "tn", BF16, "mm_dw_down", 768, D_MODEL, j_outer=True)
    g_w_down = g_w_down.reshape(N_DEV, FF_WIRE, D_MODEL)
    g_w_gate = _mm([(h2, dgate)], "tn", BF16, "mm_dw_gate", D_MODEL, 768, out_blocks=True)
    g_w_up = _mm([(h2, dup)], "tn", BF16, "mm_dw_up", D_MODEL, 768, out_blocks=True)
    ffn_handle, ffn_token = _exchange_start([g_w_down, g_w_gate, g_w_up], True, "scatter_ffn_start")

    def norm_bwd(p, e):
        _, vjp = jax.vjp(_rms, e[0], e[2])
        dx, dgain = vjp(sum(p))
        dx = dx + e[1]
        return dx, dx, dgain

    dx1, dx1_b, g_ffn_norm = _mm_fused(
        [(dgate, w_g), (dup, w_u)], "nt", "mm_dh2_norm", 256, D_MODEL, norm_bwd,
        [(x1, 0), (dy, 0), (ffn_norm + ffn_token[0, 0], None)], (F32, BF16), b_blocks=True, sum_shape=(1, D_MODEL))
    def merge_bwd(p, e):
        _, vjp = jax.vjp(_merge, *e)
        dg0, dg1, da_dn, da_swa = vjp(p[0])
        return jnp.concatenate([dg0, dg1], axis=1), da_dn, da_swa

    dproj, da_dn, da_swa = _mm_fused(
        [(dx1_b, w_o)], "nt", "mm_dmerged_merge", 512, D_MODEL, merge_bwd,
        [(proj, P_GATE // D_MODEL), (proj, P_GATE // D_MODEL + 1), (a_dn, 0), (a_swa, 0)], (BF16,) * 3,
        wide_first=(P_WIDTH, 2 * D_MODEL))
    g_w_out = _mm([(merged, dx1_b)], "tn", BF16, "mm_dw_out", 512, D_MODEL, j_outer=True)
    g_w_out = g_w_out.reshape(N_DEV, LANES, D_MODEL)
    dy_dn = _mm([(da_dn, w_bdn)], "nt", F32, "mm_dy_dn", 1024, DN_WIDTH, b_blocks=True)
    dy_swa = _mm([(da_swa, w_bswa)], "nt", F32, "mm_dy_swa", 1024, SWA_WIDTH, b_blocks=True)
    g_w_bdn = _mm([(y_dn, da_dn)], "tn", BF16, "mm_dw_branch_dn", DN_WIDTH, 512, out_blocks=True)
    g_w_bswa = _mm([(y_swa, da_swa)], "tn", BF16, "mm_dw_branch_swa", SWA_WIDTH, 512, out_blocks=True)
    dproj, dsk, dsv, g_q_norm, g_k_norm, g_sinks, dbias = _swa_bwd(proj, swa_q_norm, swa_k_norm, swa_sinks, bias,
                                                                   dy_swa, dproj)
    dproj = _kv_into(dproj, dsk, dsv)
    g_rel_bias = _bias_bwd(dbias)[:, :REL_BUCKETS].T
    mix_handle, mix_token = _exchange_start([g_w_out, g_w_bdn, g_w_bswa], True, "scatter_mix_start")
    do, dproj, g_out_norm = _dn_out_bwd(o, proj, dn_out_norm + mix_token[0, 0], dy_dn, dproj)
    du, dw, dqe, dkd, dqk, degl = _dn_scan_bwd(u, w, qe, kd, qk, egl, states, do)
    dqkvn, dgd, dbeta = _dn_prep_bwd(qkvn, g, beta, tinv, du, dw, dqe, dkd, dqk, degl)
    dproj, dal, ddt = _dn_gate_bwd(proj, dn_a_log, dn_dt_bias, dbeta, dgd, dproj)
    g_a_log = dal.reshape(DN_HEADS, DN_DIM).sum(axis=1)
    g_dt_bias = ddt[0, DN_HEADS:2 * DN_HEADS]
    dproj, g_conv = _dn_conv_bwd(proj, conv_w, dqkvn, dproj)
    g_w_in = _w_in_to_blocks(_mm([(h, dproj)], "tn", BF16, "mm_dw_in", 512, 1664, j_outer=True))
    in_handle, in_token = _exchange_start([g_w_in], True, "scatter_in_start")
    dx, g_attn_norm = _mm_fused(
        [(dproj, w_pad)], "nt", "mm_dh_norm", 512, D_MODEL, lambda p, e: norm_bwd(p, e)[1:],
        [(xs, 0), (dx1, 0), (attn_norm + in_token[0, 0], None)], (F32,), sum_shape=(1, D_MODEL))

    g_small = {"attn_norm": g_attn_norm, "ffn_norm": g_ffn_norm, "rel_bias": g_rel_bias, "dn_out_norm": g_out_norm,
               "swa_q_norm": g_q_norm, "swa_k_norm": g_k_norm, "dn_a_log": g_a_log, "dn_dt_bias": g_dt_bias,
               "swa_sinks": g_sinks}
    me = 4 * lax.axis_index("x") + 2 * lax.axis_index("y") + lax.axis_index("c")
    outs = {}

    def finish(handle, group, name, after):
        srcs, lands = _exchange_wait(handle, after, True, name)
        for n, src, land in zip(group, srcs, lands):
            parts = _own_slot(land, lax.dynamic_index_in_dim(src, me, 0, keepdims=False))
            outs[n] = _adam_update(parts, args[n], args["m_" + n], args["v_" + n], "adam_" + n)

    finish(ffn_handle, ("w_down", "w_gate", "w_up"), "scatter_ffn_wait", dx)
    finish(mix_handle, ("w_out", "w_branch_dn", "w_branch_swa"), "scatter_mix_wait", dx)
    sheets, conv_all = _all_gather_direct([_small_pack(g_small, loss_local), _pad_to(g_conv, (8, DN_QKV))],
                                          "all_gather_small",
                                          after=[outs[n][0] for n in sorted(outs)])
    finish(in_handle, ("w_in",), "scatter_in_wait", sheets)
    conv_parts = lax.dynamic_slice(conv_all, (0, 0, me * CONV_SHARD[1]), (N_DEV,) + CONV_SHARD)
    outs["dn_conv"] = _adam_update(conv_parts, dn_conv, m_dn_conv, v_dn_conv, "adam_dn_conv")
    small_outs, loss = _small_update(sheets, {n: args[n] for n in SMALL}, {n: args["m_" + n] for n in SMALL},
                                     {n: args["v_" + n] for n in SMALL})
    outs.update(small_outs)

    names = ("attn_norm", "w_in", "dn_conv", "dn_a_log", "dn_dt_bias", "dn_out_norm", "swa_q_norm", "swa_k_norm",
             "swa_sinks", "rel_bias", "w_branch_dn", "w_branch_swa", "w_out", "ffn_norm", "w_gate", "w_up", "w_down")
    results = []
    for kind in range(4):
        results += [outs[n][kind].reshape(args[n].shape) for n in names]

    return (loss.reshape(()), dx.reshape(x.shape), *results)
```

```python
import math

import numpy as np
import jax
import jax.numpy as jnp
from jax import lax
from jax.experimental import pallas as pl
from jax.experimental.pallas import tpu as pltpu

F32 = jnp.float32
BF16 = jnp.bfloat16
HI = lax.Precision.HIGHEST

D_MODEL = 1024
DN_HEADS = 4
DN_DIM = 128
DN_WIDTH = 512
DN_QKV = 1536
DN_CONV = 4
CHUNK = 64
SWA_HEADS = 8
SWA_KV = 2
SWA_GROUP = 4
SWA_DIM = 64
SWA_WIDTH = 512
SWA_KVW = 128
WINDOW = 128
BLOCK = 128
REL_BUCKETS = 32
REL_MAX_DIST = 128
D_FF = 2816
D_IN = 4872
EPS = 1e-6
N_DEV = 8

ADAM_LR = 0.001
ADAM_B1 = 0.9
ADAM_B2 = 0.999
ADAM_EPS = 1e-08
ADAM_WD = 0.01
ADAM_STEP = 10

P_GATE, P_QKV, P_Z, P_SQ, P_SK, P_SV, P_BA = 0, 2048, 3584, 4096, 4608, 4736, 4864
P_WIDTH = 4992
R_QKV, R_Z, R_B, R_A, R_SQ, R_SK, R_SV, R_GATE = 0, 1536, 2048, 2052, 2056, 2568, 2696, 2824

VMEM_LIMIT = 56 * 1024 * 1024
LANES = 128
MESH_ID = pl.DeviceIdType.MESH


def _params(sem=None):
    return pltpu.CompilerParams(dimension_semantics=sem, vmem_limit_bytes=VMEM_LIMIT)


def _pick(dim, target):
    if dim <= target:
        return dim
    t = target - target % LANES
    while t >= LANES:
        if dim % t == 0:
            return t
        t -= LANES
    return dim


_DIMS = {"nn": (((1,), (0,)), ((), ())), "nt": (((1,), (1,)), ((), ())), "tn": (((0,), (0,)), ((), ()))}


def _tile_product(a_ref, b_ref, mode, b_blocks):
    a = a_ref[...].astype(BF16)
    b = jnp.concatenate([b_ref[d] for d in range(b_ref.shape[0])], axis=1) if b_blocks else b_ref[...]
    return lax.dot_general(a, b.astype(BF16), _DIMS[mode], preferred_element_type=F32)


def _mm(pairs, mode, out_dtype, name, bm, bn, j_outer=False, b_blocks=False, out_blocks=False):
    a0, b0 = pairs[0]
    cb = b0.shape[2] if b_blocks else None
    b_shape = (b0.shape[1], N_DEV * cb) if b_blocks else b0.shape
    if mode == "nn":
        (M, K), (K2, N) = a0.shape, b_shape
    elif mode == "nt":
        (M, K), (N, K2) = a0.shape, b_shape
    else:
        (K, M), (K2, N) = a0.shape, b_shape
    bm, bn = min(bm, M), min(bn, N)
    assert K == K2 and M % bm == 0 and N % bn == 0, (name, a0.shape, b0.shape, bm, bn)
    co = N // N_DEV
    assert not out_blocks or bn % co == 0
    dims = _DIMS[mode]
    n = len(pairs)

    def body(*refs):
        o_ref = refs[2 * n]
        acc = None
        for t in range(n):
            p = _tile_product(refs[2 * t], refs[2 * t + 1], mode, b_blocks)
            acc = p if acc is None else acc + p
        if out_blocks:
            for d in range(bn // co):
                o_ref[d] = acc[:, d * co:(d + 1) * co].astype(out_dtype)
        else:
            o_ref[...] = acc.astype(out_dtype)

    def ij(f):
        return (lambda j, i: f(i, j)) if j_outer else f

    a_spec = pl.BlockSpec((K, bm), ij(lambda i, j: (0, i))) if mode == "tn" else pl.BlockSpec((bm, K), ij(lambda i, j: (i, 0)))
    if b_blocks and mode == "nt":
        b_spec = pl.BlockSpec((N_DEV, bn, cb), ij(lambda i, j: (0, j, 0)))
    elif b_blocks:
        b_spec = pl.BlockSpec((bn // cb, K, cb), ij(lambda i, j: (j, 0, 0)))
    elif mode == "nt":
        b_spec = pl.BlockSpec((bn, K), ij(lambda i, j: (j, 0)))
    else:
        b_spec = pl.BlockSpec((K, bn), ij(lambda i, j: (0, j)))
    if out_blocks:
        out_spec = pl.BlockSpec((bn // co, bm, co), ij(lambda i, j: (j, i, 0)))
        out_shape = jax.ShapeDtypeStruct((N_DEV, M, co), out_dtype)
    else:
        out_spec = pl.BlockSpec((bm, bn), ij(lambda i, j: (i, j)))
        out_shape = jax.ShapeDtypeStruct((M, N), out_dtype)
    grid = (N // bn, M // bm) if j_outer else (M // bm, N // bn)
    return pl.pallas_call(
        body, grid=grid, in_specs=[a_spec, b_spec] * n, out_specs=out_spec, out_shape=out_shape, name=name,
        compiler_params=_params(("parallel", "parallel")),
    )(*[x for pair in pairs for x in pair])


def _mm_fused(pairs, mode, name, bm, bn, epilogue, extras, out_dtypes, j_outer=False, b_blocks=False,
              sum_shape=None, wide_first=None):
    a0, b0 = pairs[0]
    cb = b0.shape[2] if b_blocks else None
    b_shape = (b0.shape[1], N_DEV * cb) if b_blocks else b0.shape
    if mode == "nn":
        (M, K), (K2, N) = a0.shape, b_shape
    else:
        (M, K), (N, K2) = a0.shape, b_shape
    bm, bn = min(bm, M), min(bn, N)
    assert mode in ("nn", "nt") and K == K2 and M % bm == 0 and N % bn == 0, (name, a0.shape, b0.shape)
    dims = _DIMS[mode]
    n, ne, no = len(pairs), len(extras), len(out_dtypes)

    def body(*refs):
        prods = [_tile_product(refs[2 * t], refs[2 * t + 1], mode, b_blocks) for t in range(n)]
        results = epilogue(prods, [r[...] for r in refs[2 * n:2 * n + ne]])
        out_refs = refs[2 * n + ne:]
        for o_ref, val, dt in zip(out_refs, results, out_dtypes):
            o_ref[...] = val.astype(dt)
        if sum_shape is not None:
            s_ref = out_refs[no]

            @pl.when((pl.program_id(0) == 0) & (pl.program_id(1) == 0))
            def _():
                s_ref[...] = jnp.zeros_like(s_ref)

            s_ref[...] += results[no]

    def ij(f):
        return (lambda j, i: f(i, j)) if j_outer else f

    a_spec = pl.BlockSpec((bm, K), ij(lambda i, j: (i, 0)))
    once = dict(pipeline_mode=pl.Buffered(1)) if bn == N else {}
    if b_blocks and mode == "nt":
        b_spec = pl.BlockSpec((N_DEV, bn, cb), ij(lambda i, j: (0, j, 0)), **once)
    elif b_blocks:
        b_spec = pl.BlockSpec((bn // cb, K, cb), ij(lambda i, j: (j, 0, 0)), **once)
    elif mode == "nt":
        b_spec = pl.BlockSpec((bn, K), ij(lambda i, j: (j, 0)), **once)
    else:
        b_spec = pl.BlockSpec((K, bn), ij(lambda i, j: (0, j)), **once)
    e_specs = [pl.BlockSpec((1, bn), ij(lambda i, j: (0, j))) if first is None
               else pl.BlockSpec((bm, bn), ij(lambda i, j, first=first: (i, first + j))) for _, first in extras]
    tile = pl.BlockSpec((bm, bn), ij(lambda i, j: (i, j)))
    out_specs = [tile] * no
    out_shape = [jax.ShapeDtypeStruct((M, N), dt) for dt in out_dtypes]
    if wide_first is not None:
        assert bn == N
        out_specs[0] = pl.BlockSpec((bm, wide_first[1]), ij(lambda i, j: (i, 0)))
        out_shape[0] = jax.ShapeDtypeStruct((M, wide_first[0]), out_dtypes[0])
    if sum_shape is not None:
        assert sum_shape[1] in (1, bn) and (sum_shape[1] == 1 or bn == N)
        out_specs.append(_full(sum_shape))
        out_shape.append(jax.ShapeDtypeStruct(sum_shape, F32))
    grid = (N // bn, M // bm) if j_outer else (M // bm, N // bn)
    sem = ("arbitrary", "arbitrary") if sum_shape is not None else ("parallel", "parallel")
    return pl.pallas_call(
        body, grid=grid, in_specs=[a_spec, b_spec] * n + e_specs, out_specs=out_specs, out_shape=out_shape,
        name=name, compiler_params=_params(sem),
    )(*[x for pair in pairs for x in pair], *[arr for arr, _ in extras])


def _rms(x, gain):
    return x * lax.rsqrt(jnp.mean(x * x, axis=-1, keepdims=True) + EPS) * gain


def _silu(x):
    return x * jax.nn.sigmoid(x)


def _act(g, u):
    return _silu(g) * u


def _merge(g0, g1, a_dn, a_swa):
    return jax.nn.sigmoid(g0) * a_dn + jax.nn.sigmoid(g1) * a_swa


def _dn_post(c, is_v, q_scale):
    a = _silu(c)
    rs = lax.rsqrt(jnp.sum(a * a, axis=-1, keepdims=True) + EPS) * q_scale
    return a * jnp.where(is_v, 1.0, rs)


def _dn_out(o, z, gain):
    return _rms(o, gain) * _silu(z)


def _dot(a, b, dims=_DIMS["nn"], hi=False):
    if a.ndim == 3 or b.ndim == 3:
        batch = a.shape[0] if a.ndim == 3 else b.shape[0]
        a = a if a.ndim == 3 else jnp.broadcast_to(a, (batch,) + a.shape)
        b = b if b.ndim == 3 else jnp.broadcast_to(b, (batch,) + b.shape)
        ((ca,), (cb,)), _ = dims
        dims = (((ca + 1,), (cb + 1,)), ((0,), (0,)))
    if hi:
        return lax.dot_general(a, b, dims, precision=HI, preferred_element_type=F32)
    return lax.dot_general(a.astype(BF16), b.astype(BF16), dims, preferred_element_type=F32)


def _pieces(x):
    hi = x.astype(BF16)
    r1 = x - hi.astype(F32)
    mid = r1.astype(BF16)
    return hi, mid, (r1 - mid.astype(F32)).astype(BF16)


def _sel_left_impl(m, x):
    mb = m.astype(BF16)
    hi, mid, lo = _pieces(x)
    return _dot(mb, hi) + (_dot(mb, mid) + _dot(mb, lo))


@jax.custom_vjp
def _sel_left(m, mt, x):
    return _sel_left_impl(m, x)


_sel_left.defvjp(lambda m, mt, x: (_sel_left_impl(m, x), (m, mt)),
                 lambda res, ct: (jnp.zeros_like(res[0]), jnp.zeros_like(res[1]), _sel_left_impl(res[1], ct)))


def _sel_right_impl(x, s):
    sb = s.astype(BF16)
    hi, mid, lo = _pieces(x)
    return _dot(hi, sb) + (_dot(mid, sb) + _dot(lo, sb))


@jax.custom_vjp
def _sel_right(x, s, st):
    return _sel_right_impl(x, s)


_sel_right.defvjp(lambda x, s, st: (_sel_right_impl(x, s), (s, st)),
                  lambda res, ct: (_sel_right_impl(ct, res[1]), jnp.zeros_like(res[0]), jnp.zeros_like(res[1])))


def _dot3_impl(a, b):
    a_hi, a_lo, _ = _pieces(a)
    b_hi, b_lo, _ = _pieces(b)
    return _dot(a_hi, b_hi) + (_dot(a_hi, b_lo) + _dot(a_lo, b_hi))


@jax.custom_vjp
def _dot3(a, b):
    return _dot3_impl(a, b)


_dot3.defvjp(lambda a, b: (_dot3_impl(a, b), (a, b)),
             lambda res, ct: (_dot(ct, res[1], _DIMS["nt"]), _dot(res[0], ct, _DIMS["tn"])))


def _inv_impl(a, eye, strict):
    t = eye - a
    p = _dot(a, a)
    for level in range(5):
        t = t + _dot(t, p)
        if level < 4:
            p = _dot(p, p)
    t = t + _dot(t, eye - t - _dot3_impl(a, t))
    return jnp.where(strict > 0.5, t, eye)


@jax.custom_vjp
def _inv_given(a, t):
    return t.astype(F32)


_inv_given.defvjp(lambda a, t: (t.astype(F32), t),
                  lambda t, ct: (-_dot(_dot(t, ct, _DIMS["tn"]), t, _DIMS["nt"]), jnp.zeros_like(t)))


@jax.custom_vjp
def _lanes_join(a, b):
    return jnp.concatenate([a, b], axis=-1)


_lanes_join.defvjp(lambda a, b: (jnp.concatenate([a, b], axis=-1), None),
                   lambda _, ct: (ct[..., :ct.shape[-1] // 2], ct[..., ct.shape[-1] // 2:]))


@jax.custom_vjp
def _lanes_halves(y):
    h = y.shape[-1] // 2
    return y[..., :h], y[..., h:]


_lanes_halves.defvjp(lambda y: ((y[..., :y.shape[-1] // 2], y[..., y.shape[-1] // 2:]), None),
                     lambda _, ct: (jnp.concatenate(ct, axis=-1),))

GROUP = 4
GROUP_ROWS = GROUP * CHUNK


def _block_consts(n):
    ii = lax.broadcasted_iota(jnp.int32, (n, n), 0)
    jj = lax.broadcasted_iota(jnp.int32, (n, n), 1)
    shift = CHUNK.bit_length() - 1
    same = jnp.right_shift(ii, shift) == jnp.right_shift(jj, shift)
    return same & (ii >= jj), same & (ii <= jj), same & (ii > jj), same, ii == jj


def _lane0(n):
    s = (lax.broadcasted_iota(jnp.int32, (LANES, n), 0) == 0).astype(F32)
    st = (lax.broadcasted_iota(jnp.int32, (n, LANES), 1) == 0).astype(F32)
    return s, st


def _dn_group(q, k, v, g, beta, t_saved=None):
    n = GROUP_ROWS
    low_b, upp_b, strict_b, _, eye_b = _block_consts(n)
    low, upp, eye = low_b.astype(F32), upp_b.astype(F32), eye_b.astype(F32)
    gc = _sel_left(low, upp, g)
    per_chunk = (g.shape[0], GROUP, CHUNK, LANES)
    g_last = jnp.sum(g.reshape(per_chunk), axis=2, keepdims=True)
    gl = jnp.broadcast_to(g_last, per_chunk).reshape(g.shape)
    s, st = _lane0(n)
    col = _sel_right(gc, s, st)
    row = jnp.swapaxes(col, 1, 2)
    decay = jnp.exp(jnp.where(low_b, col - row, -jnp.inf))
    kb = k * beta
    vb = v * beta
    a = jnp.where(strict_b, _dot(kb, k, _DIMS["nt"]) * decay, 0.0)
    t = _inv_impl(a, eye, strict_b.astype(F32)) if t_saved is None else _inv_given(a, t_saved)
    u, w = _lanes_halves(_dot3(t, _lanes_join(vb, kb * jnp.exp(gc))))
    fold = (jnp.bitwise_and(lax.broadcasted_iota(jnp.int32, (n, CHUNK), 0), CHUNK - 1)
            == lax.broadcasted_iota(jnp.int32, (n, CHUNK), 1)).astype(F32)
    fold_t = (jnp.bitwise_and(lax.broadcasted_iota(jnp.int32, (CHUNK, n), 1), CHUNK - 1)
              == lax.broadcasted_iota(jnp.int32, (CHUNK, n), 0)).astype(F32)
    qk = _sel_right(_dot(q, k, _DIMS["nt"]) * decay, fold, fold_t)
    return u, w, q * jnp.exp(gc), k * jnp.exp(gl - gc), qk, jnp.exp(g_last), t


def _dn_step(s, u, w, qe, kd, qk, egl):
    v_new = u - _dot(w, s)
    o = _dot(qe, s) + _dot(qk, v_new)
    s_new = s * egl + _dot(kd, v_new, _DIMS["tn"])
    return s_new, o


def _swa_block(q, kband, vband, qg, kg, sinks, band):
    kn = _rms(kband, kg)
    qn = _rms(q, qg) * (SWA_DIM ** -0.5)
    logits = _dot(qn, kn, _DIMS["nt"]) + band
    m = lax.stop_gradient(jnp.maximum(jnp.max(logits, axis=-1, keepdims=True), sinks))
    p = jnp.exp(logits - m)
    denom = jnp.sum(p, axis=-1, keepdims=True) + jnp.exp(sinks - m)
    return _dot(p * (1.0 / denom), vband)


def _adamw(w, g, m, v):
    m = ADAM_B1 * m + (1.0 - ADAM_B1) * g
    v = ADAM_B2 * v + (1.0 - ADAM_B2) * jnp.square(g)
    m_hat = m / (1.0 - ADAM_B1 ** ADAM_STEP)
    v_hat = v / (1.0 - ADAM_B2 ** ADAM_STEP)
    delta = -ADAM_LR * (m_hat / (jnp.sqrt(v_hat) + ADAM_EPS) + ADAM_WD * w)
    return delta, m, v


def _row(tm, c, cb=0):
    return pl.BlockSpec((tm, c), lambda i, cb=cb: (i, cb))


def _full(shape):
    nd = len(shape)
    return pl.BlockSpec(shape, lambda *_, nd=nd: (0,) * nd)


def _norm_fwd(x, gain, name, tm=1024):
    S = x.shape[0]

    def body(x_ref, g_ref, h_ref):
        h_ref[...] = _rms(x_ref[...], g_ref[...]).astype(BF16)

    return pl.pallas_call(
        body, grid=(S // tm,), in_specs=[_row(tm, D_MODEL), _full((1, D_MODEL))],
        out_specs=_row(tm, D_MODEL), out_shape=jax.ShapeDtypeStruct((S, D_MODEL), BF16),
        name=name, compiler_params=_params(("parallel",)))(x, gain)


def _shift_down(x, s):
    row = lax.broadcasted_iota(jnp.int32, x.shape, 0)
    return jnp.where(row >= s, pltpu.roll(x, s, axis=0), 0.0)


def _shift_up(x, s):
    n = x.shape[0]
    row = lax.broadcasted_iota(jnp.int32, x.shape, 0)
    return jnp.where(row < n - s, pltpu.roll(x, n - s, axis=0), 0.0)


def _conv(x, w):
    out = w[DN_CONV - 1:DN_CONV] * x
    for s in range(1, DN_CONV):
        out = out + w[DN_CONV - 1 - s:DN_CONV - s] * _shift_down(x, s)
    return out


def _dn_conv_fwd(proj, conv_w):
    S = proj.shape[0]
    nb = DN_QKV // LANES

    def body(x_ref, w_ref, o_ref):
        j = pl.program_id(0)
        q_scale = jnp.where(j < DN_HEADS, DN_DIM ** -0.5, 1.0).astype(F32)
        o_ref[...] = _dn_post(_conv(x_ref[...], w_ref[...]), j >= 2 * DN_HEADS, q_scale)

    return pl.pallas_call(
        body, grid=(nb,),
        in_specs=[pl.BlockSpec((S, LANES), lambda j: (0, P_QKV // LANES + j)),
                  pl.BlockSpec((DN_CONV, LANES), lambda j: (0, j))],
        out_specs=pl.BlockSpec((S, LANES), lambda j: (0, j)),
        out_shape=jax.ShapeDtypeStruct((S, DN_QKV), F32), name="dn_conv_fwd",
        compiler_params=_params(("parallel",)))(proj, conv_w)


def _dn_conv_bwd(proj, conv_w, dqkvn, dproj):
    S = proj.shape[0]
    nb = DN_QKV // LANES

    def body(x_ref, w_ref, d_ref, _, dx_ref, dw_ref):
        j = pl.program_id(0)
        q_scale = jnp.where(j < DN_HEADS, DN_DIM ** -0.5, 1.0).astype(F32)
        x = x_ref[...]
        w = w_ref[...]
        _, vjp = jax.vjp(lambda c: _dn_post(c, j >= 2 * DN_HEADS, q_scale), _conv(x, w))
        (dc,) = vjp(d_ref[0])
        dx = w[DN_CONV - 1:DN_CONV] * dc
        dw_ref[DN_CONV - 1:DN_CONV, :] = jnp.sum(dc * x, axis=0, keepdims=True)
        for s in range(1, DN_CONV):
            dx = dx + w[DN_CONV - 1 - s:DN_CONV - s] * _shift_up(dc, s)
            dw_ref[DN_CONV - 1 - s:DN_CONV - s, :] = jnp.sum(dc * _shift_down(x, s), axis=0, keepdims=True)
        dx_ref[...] = dx.astype(BF16)

    return pl.pallas_call(
        body, grid=(nb,),
        in_specs=[pl.BlockSpec((S, LANES), lambda j: (0, P_QKV // LANES + j)),
                  pl.BlockSpec((DN_CONV, LANES), lambda j: (0, j)),
                  pl.BlockSpec((1, S, LANES), lambda j: (lax.div(j, DN_HEADS), 0, lax.rem(j, DN_HEADS))),
                  pl.BlockSpec(memory_space=pl.ANY)],
        out_specs=[pl.BlockSpec((S, LANES), lambda j: (0, P_QKV // LANES + j)),
                   pl.BlockSpec((DN_CONV, LANES), lambda j: (0, j))],
        out_shape=[jax.ShapeDtypeStruct(dproj.shape, dproj.dtype), jax.ShapeDtypeStruct((DN_CONV, DN_QKV), F32)],
        input_output_aliases={3: 0},
        name="dn_conv_bwd", compiler_params=_params(("parallel",)))(proj, conv_w, dqkvn, dproj)


def _expanders():
    eb = np.zeros((LANES, DN_WIDTH), np.float32)
    ea = np.zeros((LANES, DN_WIDTH), np.float32)
    for h in range(DN_HEADS):
        eb[h, h * DN_DIM:(h + 1) * DN_DIM] = 1.0
        ea[DN_HEADS + h, h * DN_DIM:(h + 1) * DN_DIM] = 1.0
    return jnp.asarray(eb), jnp.asarray(ea), jnp.asarray(eb.T), jnp.asarray(ea.T)


def _dn_gate_args(a_log, dt_bias):
    alog = jnp.repeat(a_log.reshape(1, DN_HEADS), DN_DIM, axis=1)
    dtb = _pad_to(jnp.pad(dt_bias.reshape(1, DN_HEADS), ((0, 0), (DN_HEADS, 0))), (1, LANES))
    return _expanders() + (alog, dtb)


def _dn_gate_specs(tm):
    return [_row(tm, LANES, P_BA // LANES), _full((LANES, DN_WIDTH)), _full((LANES, DN_WIDTH)),
            _full((DN_WIDTH, LANES)), _full((DN_WIDTH, LANES)), _full((1, DN_WIDTH)), _full((1, LANES))]


def _dn_gate_fn(ba, eb, ea, ebt, eat, alog, dtb):
    beta = _sel_right(jax.nn.sigmoid(ba), eb, ebt)
    g = -jnp.exp(alog) * _sel_right(jax.nn.softplus(ba + dtb), ea, eat)
    return beta, g


def _dn_gate_fwd(proj, a_log, dt_bias, tm=1024):
    S = proj.shape[0]
    args = _dn_gate_args(a_log, dt_bias)

    def body(ba_ref, eb_ref, ea_ref, ebt_ref, eat_ref, al_ref, dt_ref, beta_ref, g_ref):
        beta, g = _dn_gate_fn(ba_ref[...], eb_ref[...], ea_ref[...], ebt_ref[...], eat_ref[...], al_ref[...],
                              dt_ref[...])
        beta_ref[...] = beta
        g_ref[...] = g

    return pl.pallas_call(
        body, grid=(S // tm,), in_specs=_dn_gate_specs(tm), out_specs=[_row(tm, DN_WIDTH), _row(tm, DN_WIDTH)],
        out_shape=[jax.ShapeDtypeStruct((S, DN_WIDTH), F32), jax.ShapeDtypeStruct((S, DN_WIDTH), F32)],
        name="dn_gate_fwd", compiler_params=_params(("parallel",)))(proj, *args)


def _dn_gate_bwd(proj, a_log, dt_bias, dbeta, dg, dproj, tm=1024):
    S = proj.shape[0]
    args = _dn_gate_args(a_log, dt_bias)

    def body(ba_ref, eb_ref, ea_ref, ebt_ref, eat_ref, al_ref, dt_ref, dbeta_ref, dg_ref, _, dba_ref, dal_ref,
             ddt_ref):
        eb, ea, ebt, eat = eb_ref[...], ea_ref[...], ebt_ref[...], eat_ref[...]
        _, vjp = jax.vjp(lambda ba, al, dt: _dn_gate_fn(ba, eb, ea, ebt, eat, al, dt), ba_ref[...], al_ref[...],
                         dt_ref[...])
        dba, dal, ddt = vjp((dbeta_ref[...], dg_ref[...]))
        dba_ref[...] = dba.astype(BF16)

        @pl.when(pl.program_id(0) == 0)
        def _():
            dal_ref[...] = jnp.zeros_like(dal_ref)
            ddt_ref[...] = jnp.zeros_like(ddt_ref)

        dal_ref[...] += dal
        ddt_ref[...] += ddt

    return pl.pallas_call(
        body, grid=(S // tm,),
        in_specs=_dn_gate_specs(tm) + [_row(tm, DN_WIDTH), _row(tm, DN_WIDTH), pl.BlockSpec(memory_space=pl.ANY)],
        out_specs=[_row(tm, LANES, P_BA // LANES), _full((1, DN_WIDTH)), _full((1, LANES))],
        out_shape=[jax.ShapeDtypeStruct(dproj.shape, dproj.dtype), jax.ShapeDtypeStruct((1, DN_WIDTH), F32),
                   jax.ShapeDtypeStruct((1, LANES), F32)],
        input_output_aliases={len(args) + 3: 0},
        name="dn_gate_bwd", compiler_params=_params(("arbitrary",)))(proj, *args, dbeta, dg, dproj)


PREP_GROUPS = 8
PREP_CHUNKS = GROUP * PREP_GROUPS


def _dn_prep_specs():
    rows = PREP_CHUNKS * CHUNK
    q = pl.BlockSpec((rows, LANES), lambda h, c: (c, h))
    k = pl.BlockSpec((rows, LANES), lambda h, c: (c, DN_HEADS + h))
    v = pl.BlockSpec((rows, LANES), lambda h, c: (c, 2 * DN_HEADS + h))
    qk = pl.BlockSpec((1, rows, CHUNK), lambda h, c: (h, c, 0))
    egl = pl.BlockSpec((1, PREP_CHUNKS, 1, LANES), lambda h, c: (h, c, 0, 0))
    return q, k, v, qk, egl


def _dn_prep_fwd(qkvn, g, beta):
    S = qkvn.shape[0]
    nc = S // CHUNK
    q, k, v, qks, egl = _dn_prep_specs()

    def body(q_ref, k_ref, v_ref, g_ref, b_ref, u_ref, w_ref, qe_ref, kd_ref, qk_ref, egl_ref, t_ref):
        rows = PREP_CHUNKS * CHUNK
        grp = (PREP_GROUPS, GROUP_ROWS, LANES)
        u, w, qe, kd, qk, e, t = _dn_group(q_ref[...].reshape(grp), k_ref[...].reshape(grp), v_ref[...].reshape(grp),
                                           g_ref[...].reshape(grp), b_ref[...].reshape(grp))
        u_ref[...] = u.reshape(rows, LANES)
        w_ref[...] = w.reshape(rows, LANES)
        qe_ref[...] = qe.reshape(rows, LANES)
        kd_ref[...] = kd.reshape(rows, LANES)
        t_ref[0] = t.reshape(rows, GROUP_ROWS).astype(BF16)
        qk_ref[0] = qk.reshape(rows, CHUNK)
        egl_ref[0] = e.reshape(PREP_CHUNKS, 1, LANES)

    wide = jax.ShapeDtypeStruct((S, DN_WIDTH), F32)
    return pl.pallas_call(
        body, grid=(DN_HEADS, nc // PREP_CHUNKS), in_specs=[q, k, v, q, q],
        out_specs=[q, q, q, q, qks, egl, _dn_tinv_spec()],
        out_shape=[wide, wide, wide, wide, jax.ShapeDtypeStruct((DN_HEADS, S, CHUNK), F32),
                   jax.ShapeDtypeStruct((DN_HEADS, nc, 1, LANES), F32),
                   jax.ShapeDtypeStruct((DN_HEADS, S, GROUP_ROWS), BF16)],
        name="dn_prep_fwd", compiler_params=_params(("parallel", "parallel")))(qkvn, qkvn, qkvn, g, beta)


def _dn_tinv_spec():
    return pl.BlockSpec((1, PREP_CHUNKS * CHUNK, GROUP_ROWS), lambda h, c: (h, c, 0))


def _dn_prep_bwd(qkvn, g, beta, tinv, du, dw, dqe, dkd, dqk, degl):
    S = qkvn.shape[0]
    nc = S // CHUNK
    q, k, v, qks, egl = _dn_prep_specs()

    def body(q_ref, k_ref, v_ref, g_ref, b_ref, t_ref, du_ref, dw_ref, dqe_ref, dkd_ref, dqk_ref, degl_ref,
             dqkv_ref, dg_ref, db_ref):
        rows = PREP_CHUNKS * CHUNK
        grp = (PREP_GROUPS, GROUP_ROWS, LANES)
        t_saved = t_ref[0].reshape(PREP_GROUPS, GROUP_ROWS, GROUP_ROWS)
        _, vjp = jax.vjp(lambda *x: _dn_group(*x, t_saved=t_saved)[:6], q_ref[...].reshape(grp),
                         k_ref[...].reshape(grp), v_ref[...].reshape(grp), g_ref[...].reshape(grp),
                         b_ref[...].reshape(grp))
        dq, dk, dv, dg, db = vjp((du_ref[...].reshape(grp), dw_ref[...].reshape(grp), dqe_ref[...].reshape(grp),
                                  dkd_ref[...].reshape(grp), dqk_ref[0].reshape(PREP_GROUPS, GROUP_ROWS, CHUNK),
                                  degl_ref[0].reshape(PREP_GROUPS, GROUP, 1, LANES)))
        dqkv_ref[0] = dq.reshape(rows, LANES)
        dqkv_ref[1] = dk.reshape(rows, LANES)
        dqkv_ref[2] = dv.reshape(rows, LANES)
        dg_ref[...] = dg.reshape(rows, LANES)
        db_ref[...] = db.reshape(rows, LANES)

    wide = jax.ShapeDtypeStruct((S, DN_WIDTH), F32)
    rows = PREP_CHUNKS * CHUNK
    return pl.pallas_call(
        body, grid=(DN_HEADS, nc // PREP_CHUNKS), in_specs=[q, k, v, q, q, _dn_tinv_spec(), q, q, q, q, qks, egl],
        out_specs=[pl.BlockSpec((3, rows, LANES), lambda h, c: (0, c, h)), q, q],
        out_shape=[jax.ShapeDtypeStruct((3, S, DN_WIDTH), F32), wide, wide],
        name="dn_prep_bwd", compiler_params=_params(("parallel", "parallel")),
    )(qkvn, qkvn, qkvn, g, beta, tinv, du, dw, dqe, dkd, dqk, degl)


SCAN_CHUNKS = 16


def _dn_scan_specs(nc, reverse):
    nb = nc // SCAN_CHUNKS

    def cidx(c):
        return nb - 1 - c if reverse else c

    hc = pl.BlockSpec((SCAN_CHUNKS * CHUNK, DN_WIDTH), lambda c: (cidx(c), 0))
    qk = pl.BlockSpec((DN_HEADS, SCAN_CHUNKS * CHUNK, CHUNK), lambda c: (0, cidx(c), 0))
    egl = pl.BlockSpec((DN_HEADS, SCAN_CHUNKS, 1, LANES), lambda c: (0, cidx(c), 0, 0))
    st = pl.BlockSpec((DN_HEADS, SCAN_CHUNKS, DN_DIM, DN_DIM), lambda c: (0, cidx(c), 0, 0))
    return hc, qk, egl, st


def _heads(ref, i):
    return jnp.stack([ref[pl.ds(i * CHUNK, CHUNK), pl.ds(h * DN_DIM, DN_DIM)] for h in range(DN_HEADS)])


def _dn_scan_fwd(u, w, qe, kd, qk, egl):
    S = u.shape[0]
    nc = S // CHUNK
    hc, qks, egls, st = _dn_scan_specs(nc, False)

    def body(u_ref, w_ref, qe_ref, kd_ref, qk_ref, egl_ref, o_ref, st_ref, s_scr):
        @pl.when(pl.program_id(0) == 0)
        def _():
            s_scr[...] = jnp.zeros_like(s_scr)

        s = s_scr[...]
        for i in range(SCAN_CHUNKS):
            rows = pl.ds(i * CHUNK, CHUNK)
            st_ref[:, i] = s
            s, o = _dn_step(s, _heads(u_ref, i), _heads(w_ref, i), _heads(qe_ref, i), _heads(kd_ref, i),
                            qk_ref[:, rows, :], egl_ref[:, i])
            for h in range(DN_HEADS):
                o_ref[rows, pl.ds(h * DN_DIM, DN_DIM)] = o[h]
        s_scr[...] = s

    return pl.pallas_call(
        body, grid=(nc // SCAN_CHUNKS,), in_specs=[hc, hc, hc, hc, qks, egls], out_specs=[hc, st],
        out_shape=[jax.ShapeDtypeStruct((S, DN_WIDTH), F32), jax.ShapeDtypeStruct((DN_HEADS, nc, DN_DIM, DN_DIM), F32)],
        scratch_shapes=[pltpu.VMEM((DN_HEADS, DN_DIM, DN_DIM), F32)], name="dn_scan_fwd",
        compiler_params=_params(("arbitrary",)))(u, w, qe, kd, qk, egl)


def _dn_scan_bwd(u, w, qe, kd, qk, egl, states, do):
    S = u.shape[0]
    nc = S // CHUNK
    hc, qks, egls, st = _dn_scan_specs(nc, True)

    def body(u_ref, w_ref, qe_ref, kd_ref, qk_ref, egl_ref, st_ref, do_ref,
             du_ref, dw_ref, dqe_ref, dkd_ref, dqk_ref, degl_ref, ds_scr):
        @pl.when(pl.program_id(0) == 0)
        def _():
            ds_scr[...] = jnp.zeros_like(ds_scr)

        ds = ds_scr[...]
        for i in reversed(range(SCAN_CHUNKS)):
            rows = pl.ds(i * CHUNK, CHUNK)
            _, vjp = jax.vjp(_dn_step, st_ref[:, i], _heads(u_ref, i), _heads(w_ref, i), _heads(qe_ref, i),
                             _heads(kd_ref, i), qk_ref[:, rows, :], egl_ref[:, i])
            ds, du, dw, dqe, dkd, dqk, degl = vjp((ds, _heads(do_ref, i)))
            dqk_ref[:, rows, :] = dqk
            degl_ref[:, i] = degl
            for h in range(DN_HEADS):
                cols = pl.ds(h * DN_DIM, DN_DIM)
                du_ref[rows, cols] = du[h]
                dw_ref[rows, cols] = dw[h]
                dqe_ref[rows, cols] = dqe[h]
                dkd_ref[rows, cols] = dkd[h]
        ds_scr[...] = ds

    wide = jax.ShapeDtypeStruct((S, DN_WIDTH), F32)
    return pl.pallas_call(
        body, grid=(nc // SCAN_CHUNKS,), in_specs=[hc, hc, hc, hc, qks, egls, st, hc],
        out_specs=[hc, hc, hc, hc, qks, egls],
        out_shape=[wide, wide, wide, wide, jax.ShapeDtypeStruct((DN_HEADS, S, CHUNK), F32),
                   jax.ShapeDtypeStruct((DN_HEADS, nc, 1, LANES), F32)],
        scratch_shapes=[pltpu.VMEM((DN_HEADS, DN_DIM, DN_DIM), F32)], name="dn_scan_bwd",
        compiler_params=_params(("arbitrary",)))(u, w, qe, kd, qk, egl, states, do)


def _dn_out_fwd(o, proj, gain, tm=1024):
    S = o.shape[0]

    def body(o_ref, z_ref, g_ref, y_ref):
        y_ref[...] = _dn_out(o_ref[...], z_ref[...], g_ref[...]).astype(BF16)

    hs = pl.BlockSpec((tm, LANES), lambda i, h: (i, h))
    zs = pl.BlockSpec((tm, LANES), lambda i, h: (i, P_Z // LANES + h))
    return pl.pallas_call(
        body, grid=(S // tm, DN_HEADS), in_specs=[hs, zs, _full((1, DN_DIM))], out_specs=hs,
        out_shape=jax.ShapeDtypeStruct((S, DN_WIDTH), BF16), name="dn_out_fwd",
        compiler_params=_params(("parallel", "parallel")))(o, proj, gain)


_ANY = pl.BlockSpec(memory_space=pl.ANY)


def _dn_out_bwd(o, proj, gain, dy, dproj, tm=1024):
    S = o.shape[0]

    def body(o_ref, z_ref, g_ref, dy_ref, _, do_ref, dz_ref, dg_ref):
        _, vjp = jax.vjp(_dn_out, o_ref[...], z_ref[...], g_ref[...])
        do, dz, dg = vjp(dy_ref[...])
        do_ref[...] = do
        dz_ref[...] = dz.astype(BF16)

        @pl.when((pl.program_id(0) == 0) & (pl.program_id(1) == 0))
        def _():
            dg_ref[...] = jnp.zeros_like(dg_ref)

        dg_ref[...] += dg

    hs = pl.BlockSpec((tm, LANES), lambda i, h: (i, h))
    zs = pl.BlockSpec((tm, LANES), lambda i, h: (i, P_Z // LANES + h))
    return pl.pallas_call(
        body, grid=(S // tm, DN_HEADS), in_specs=[hs, zs, _full((1, DN_DIM)), hs, _ANY],
        out_specs=[hs, zs, _full((1, DN_DIM))],
        out_shape=[jax.ShapeDtypeStruct((S, DN_WIDTH), F32), jax.ShapeDtypeStruct(dproj.shape, dproj.dtype),
                   jax.ShapeDtypeStruct((1, DN_DIM), F32)],
        input_output_aliases={4: 1},
        name="dn_out_bwd", compiler_params=_params(("arbitrary", "arbitrary")))(o, proj, gain, dy, dproj)


def _rel_buckets():
    qi = np.arange(BLOCK)[:, None]
    kj = np.arange(2 * BLOCK)[None, :]
    n = np.maximum(BLOCK + qi - kj, 0)
    max_exact = REL_BUCKETS // 2
    nf = np.maximum(n, 1).astype(np.float32)
    large = max_exact + (np.log(nf / np.float32(max_exact)) / np.float32(math.log(REL_MAX_DIST / max_exact))
                         * np.float32(REL_BUCKETS - max_exact)).astype(np.int32)
    large = np.minimum(large, REL_BUCKETS - 1)
    return np.where(n < max_exact, n, large).astype(np.int32)


def _bias_fwd(rel_bias):
    buckets = jnp.asarray(_rel_buckets())

    def body(rb_ref, bk_ref, o_ref):
        bk = bk_ref[...]
        for h in range(SWA_HEADS):
            acc = jnp.zeros((BLOCK, 2 * BLOCK), F32)
            for b in range(REL_BUCKETS):
                acc = jnp.where(bk == b, rb_ref[b, h], acc)
            for first in range(2):
                o_ref[first, h] = jnp.where(_swa_mask(1 - first), acc, -jnp.inf)

    return pl.pallas_call(
        body, in_specs=[pl.BlockSpec(memory_space=pltpu.SMEM), pl.BlockSpec(memory_space=pltpu.VMEM)],
        out_specs=pl.BlockSpec(memory_space=pltpu.VMEM),
        out_shape=jax.ShapeDtypeStruct((2, SWA_HEADS, BLOCK, 2 * BLOCK), F32), name="swa_bias_fwd",
        compiler_params=_params())(rel_bias, buckets)


def _bias_bwd(dbias):
    buckets = jnp.asarray(_rel_buckets())

    def body(d_ref, bk_ref, o_ref):
        bk = bk_ref[...]
        lane = lax.broadcasted_iota(jnp.int32, (1, LANES), 1)
        for h in range(SWA_HEADS):
            d = d_ref[h]
            row = jnp.zeros((1, LANES), F32)
            for b in range(REL_BUCKETS):
                part = jnp.sum(jnp.where(bk == b, d, 0.0), axis=1, keepdims=True)
                row = jnp.where(lane == b, jnp.sum(part, axis=0, keepdims=True), row)
            o_ref[h:h + 1, :] = row

    return pl.pallas_call(
        body, in_specs=[pl.BlockSpec(memory_space=pltpu.VMEM), pl.BlockSpec(memory_space=pltpu.VMEM)],
        out_specs=pl.BlockSpec(memory_space=pltpu.VMEM),
        out_shape=jax.ShapeDtypeStruct((SWA_HEADS, LANES), F32), name="swa_bias_bwd",
        compiler_params=_params())(dbias, buckets)


def _swa_mask(n):
    qi = lax.broadcasted_iota(jnp.int32, (BLOCK, 2 * BLOCK), 0)
    kj = lax.broadcasted_iota(jnp.int32, (BLOCK, 2 * BLOCK), 1)
    dist = BLOCK + qi - kj
    return (dist >= 0) & (dist < WINDOW) & ((n > 0) | (kj >= BLOCK))


def _swa_in_specs():
    q = pl.BlockSpec((BLOCK, SWA_WIDTH), lambda n: (n, P_SQ // SWA_WIDTH))
    kc = pl.BlockSpec((BLOCK, SWA_KVW), lambda n: (n, P_SK // SWA_KVW))
    kp = pl.BlockSpec((BLOCK, SWA_KVW), lambda n: (jnp.maximum(n - 1, 0), P_SK // SWA_KVW))
    vc = pl.BlockSpec((BLOCK, SWA_KVW), lambda n: (n, P_SV // SWA_KVW))
    vp = pl.BlockSpec((BLOCK, SWA_KVW), lambda n: (jnp.maximum(n - 1, 0), P_SV // SWA_KVW))
    band = pl.BlockSpec((None, SWA_HEADS, BLOCK, 2 * BLOCK), lambda n: (jnp.where(n == 0, 1, 0), 0, 0, 0))
    small = [_full((1, SWA_DIM)), _full((1, SWA_DIM)), _full((1, SWA_HEADS)), band]
    return [q, kp, kc, vp, vc] + small


def _swa_load(q_ref, kp_ref, kc_ref, vp_ref, vc_ref, s_ref):
    q = jnp.stack([q_ref[:, pl.ds(h * SWA_DIM, SWA_DIM)] for h in range(SWA_HEADS)])
    kbands, vbands = [], []
    for kv in range(SWA_KV):
        cols = pl.ds(kv * SWA_DIM, SWA_DIM)
        kbands += [jnp.concatenate([kp_ref[:, cols], kc_ref[:, cols]], axis=0)] * SWA_GROUP
        vbands += [jnp.concatenate([vp_ref[:, cols], vc_ref[:, cols]], axis=0)] * SWA_GROUP
    sinks = jnp.stack([s_ref[:, pl.ds(h, 1)] for h in range(SWA_HEADS)])
    return q, jnp.stack(kbands), jnp.stack(vbands), sinks


def _swa_fwd(proj, q_gain, k_gain, sinks, bias):
    S = proj.shape[0]

    def body(q_ref, kp_ref, kc_ref, vp_ref, vc_ref, qg_ref, kg_ref, s_ref, bias_ref, y_ref):
        q, kband, vband, sk = _swa_load(q_ref, kp_ref, kc_ref, vp_ref, vc_ref, s_ref)
        out = _swa_block(q, kband, vband, qg_ref[...], kg_ref[...], sk, bias_ref[...])
        for h in range(SWA_HEADS):
            y_ref[:, pl.ds(h * SWA_DIM, SWA_DIM)] = out[h].astype(BF16)

    return pl.pallas_call(
        body, grid=(S // BLOCK,), in_specs=_swa_in_specs(),
        out_specs=pl.BlockSpec((BLOCK, SWA_WIDTH), lambda n: (n, 0)),
        out_shape=jax.ShapeDtypeStruct((S, SWA_WIDTH), BF16), name="swa_fwd",
        compiler_params=_params(("parallel",)))(proj, proj, proj, proj, proj, q_gain, k_gain, sinks, bias)


def _swa_bwd(proj, q_gain, k_gain, sinks, bias, dy, dproj):
    S = proj.shape[0]

    def body(q_ref, kp_ref, kc_ref, vp_ref, vc_ref, qg_ref, kg_ref, s_ref, bias_ref, dy_ref, _,
             dq_ref, dk_ref, dv_ref, dqg_ref, dkg_ref, ds_ref, dbias_ref):
        n = pl.program_id(0)

        @pl.when(n == 0)
        def _():
            for r in (dk_ref, dv_ref, dqg_ref, dkg_ref, ds_ref, dbias_ref):
                r[...] = jnp.zeros_like(r)

        cur = pl.ds(pl.multiple_of(n * BLOCK, BLOCK), BLOCK)
        prev = pl.ds(pl.multiple_of(jnp.maximum(n - 1, 0) * BLOCK, BLOCK), BLOCK)
        q, kband, vband, sk = _swa_load(q_ref, kp_ref, kc_ref, vp_ref, vc_ref, s_ref)
        _, vjp = jax.vjp(_swa_block, q, kband, vband, qg_ref[...], kg_ref[...], sk, bias_ref[...])
        dy = jnp.stack([dy_ref[:, pl.ds(h * SWA_DIM, SWA_DIM)] for h in range(SWA_HEADS)])
        dq, dkb, dvb, dqg, dkg, dsk, dbs = vjp(dy)
        for h in range(SWA_HEADS):
            dq_ref[:, pl.ds(h * SWA_DIM, SWA_DIM)] = dq[h].astype(BF16)
            ds_ref[:, pl.ds(h, 1)] += dsk[h]
        dbias_ref[...] += dbs
        dqg_ref[...] += dqg
        dkg_ref[...] += dkg
        for kv in range(SWA_KV):
            cols = pl.ds(kv * SWA_DIM, SWA_DIM)
            group = range(kv * SWA_GROUP, (kv + 1) * SWA_GROUP)
            dk_kv = sum(dkb[h] for h in group)
            dv_kv = sum(dvb[h] for h in group)
            dk_ref[cur, cols] += dk_kv[BLOCK:]
            dv_ref[cur, cols] += dv_kv[BLOCK:]

            @pl.when(n > 0)
            def _(cols=cols, dk_kv=dk_kv, dv_kv=dv_kv):
                dk_ref[prev, cols] += dk_kv[:BLOCK]
                dv_ref[prev, cols] += dv_kv[:BLOCK]

    return pl.pallas_call(
        body, grid=(S // BLOCK,),
        in_specs=_swa_in_specs() + [pl.BlockSpec((BLOCK, SWA_WIDTH), lambda n: (n, 0)),
                                    pl.BlockSpec(memory_space=pl.ANY)],
        out_specs=[pl.BlockSpec((BLOCK, SWA_WIDTH), lambda n: (n, P_SQ // SWA_WIDTH)), _full((S, SWA_KVW)),
                   _full((S, SWA_KVW)), _full((1, SWA_DIM)), _full((1, SWA_DIM)), _full((1, SWA_HEADS)),
                   _full((SWA_HEADS, BLOCK, 2 * BLOCK))],
        out_shape=[jax.ShapeDtypeStruct(dproj.shape, dproj.dtype), jax.ShapeDtypeStruct((S, SWA_KVW), F32),
                   jax.ShapeDtypeStruct((S, SWA_KVW), F32), jax.ShapeDtypeStruct((1, SWA_DIM), F32),
                   jax.ShapeDtypeStruct((1, SWA_DIM), F32), jax.ShapeDtypeStruct((1, SWA_HEADS), F32),
                   jax.ShapeDtypeStruct((SWA_HEADS, BLOCK, 2 * BLOCK), F32)],
        input_output_aliases={10: 0},
        name="swa_bwd", compiler_params=_params(("arbitrary",)),
    )(proj, proj, proj, proj, proj, q_gain, k_gain, sinks, bias, dy, dproj)


def _kv_into(dproj, dk, dv, tm=1024):
    S = dk.shape[0]

    def body(dk_ref, dv_ref, _, o_ref):
        o_ref[:, :SWA_KVW] = dk_ref[...].astype(BF16)
        o_ref[:, SWA_KVW:] = dv_ref[...].astype(BF16)

    return pl.pallas_call(
        body, grid=(S // tm,), in_specs=[_row(tm, SWA_KVW), _row(tm, SWA_KVW), pl.BlockSpec(memory_space=pl.ANY)],
        out_specs=_row(tm, 2 * SWA_KVW, P_SK // (2 * SWA_KVW)),
        out_shape=jax.ShapeDtypeStruct(dproj.shape, dproj.dtype), input_output_aliases={2: 0},
        name="swa_kv_into", compiler_params=_params(("parallel",)))(dk, dv, dproj)


def _position():
    return lax.axis_index("x"), lax.axis_index("y"), lax.axis_index("c")


def _all_gather(shards, name="all_gather_weights"):
    na = len(shards)

    def body(*refs):
        x_refs, out_refs = refs[:na], refs[na:2 * na]
        send_sems, recv_sems, local_sems = refs[2 * na:]
        x, y, c = _position()
        me, sibling = (x, y, c), (x, y, 1 - c)
        chips = [(1 - x, y), (x, 1 - y), (1 - x, 1 - y)]

        def copy(a, k, block, to, own=False):
            px, py, pc = block
            slot = out_refs[a].at[4 * px + 2 * py + pc]
            return pltpu.make_async_remote_copy(
                src_ref=x_refs[a] if own else slot, dst_ref=slot, send_sem=send_sems.at[7 * a + k],
                recv_sem=recv_sems.at[7 * a + k], device_id=to, device_id_type=MESH_ID)

        mine = [pltpu.make_async_copy(x_refs[a], out_refs[a].at[4 * x + 2 * y + c], local_sems.at[a])
                for a in range(na)]
        for cp in mine:
            cp.start()
        first = []
        for a in range(na):
            first.append(copy(a, 0, me, sibling, own=True))
            first += [copy(a, 1 + j, me, (*chip, c), own=True) for j, chip in enumerate(chips)]
        for cp in first:
            cp.start()
        passed = []
        for j, chip in enumerate(chips):
            for a in range(na):
                copy(a, 1 + j, (*chip, c), me).wait_recv()
                passed.append(copy(a, 4 + j, (*chip, c), sibling))
                passed[-1].start()
        for a in range(na):
            copy(a, 0, sibling, me).wait_recv()
            for j, chip in enumerate(chips):
                copy(a, 4 + j, (*chip, 1 - c), me).wait_recv()
        for cp in first + passed:
            cp.wait_send()
        for cp in mine:
            cp.wait()

    return pl.pallas_call(
        body, in_specs=[pl.BlockSpec(memory_space=pl.ANY)] * na, out_specs=[pl.BlockSpec(memory_space=pl.ANY)] * na,
        out_shape=[jax.ShapeDtypeStruct((N_DEV,) + s.shape, s.dtype) for s in shards],
        scratch_shapes=[pltpu.SemaphoreType.DMA((7 * na,)), pltpu.SemaphoreType.DMA((7 * na,)),
                        pltpu.SemaphoreType.DMA((na,))],
        name=name)(*shards)


_HBM = pl.BlockSpec(memory_space=pltpu.HBM)
_SEM = pl.BlockSpec(memory_space=pltpu.SEMAPHORE)
_DATAFLOW = pltpu.SideEffectType.DATAFLOW_SIDE_EFFECTING


def _peers(x, y, c):
    out = []
    for k in range(1, N_DEV):
        px, py, pc = x ^ (k >> 2), y ^ ((k >> 1) & 1), c ^ (k & 1)
        out.append(((px, py, pc), 4 * px + 2 * py + pc))
    return out


def _split_copies(src_refs, land_refs, send_sems, recv_sems, scatter):
    x, y, c = _position()
    me = 4 * x + 2 * y + c
    sends, recvs = [], []
    for k, (peer_id, peer) in enumerate(_peers(x, y, c)):
        for a, (src, land) in enumerate(zip(src_refs, land_refs)):
            sems = dict(send_sem=send_sems.at[7 * a + k], recv_sem=recv_sems.at[7 * a + k],
                        device_id=peer_id, device_id_type=MESH_ID)
            mine = src.at[peer] if scatter else src
            sends.append(pltpu.make_async_remote_copy(src_ref=mine, dst_ref=land.at[me], **sems))
            recvs.append(pltpu.make_async_remote_copy(src_ref=mine, dst_ref=land.at[peer], **sems))
    return sends, recvs


def _all_gather_direct(shards, name, after):
    na, nb = len(shards), len(after)

    def body(*refs):
        x_refs, out_refs = refs[:na], refs[na + nb:2 * na + nb]
        send_sems, recv_sems, local_sems = refs[2 * na + nb:]
        x, y, c = _position()
        me = 4 * x + 2 * y + c
        local = [pltpu.make_async_copy(x_refs[a], out_refs[a].at[me], local_sems.at[a]) for a in range(na)]
        sends, recvs = _split_copies(x_refs, out_refs, send_sems, recv_sems, False)
        for cp in local + sends:
            cp.start()
        for cp in recvs:
            cp.wait_recv()
        for cp in sends:
            cp.wait_send()
        for cp in local:
            cp.wait()

    return pl.pallas_call(
        body, in_specs=[pl.BlockSpec(memory_space=pl.ANY)] * (na + nb),
        out_specs=[pl.BlockSpec(memory_space=pl.ANY)] * na,
        out_shape=[jax.ShapeDtypeStruct((N_DEV,) + s.shape, s.dtype) for s in shards],
        scratch_shapes=[pltpu.SemaphoreType.DMA((7 * na,)), pltpu.SemaphoreType.DMA((7 * na,)),
                        pltpu.SemaphoreType.DMA((na,))],
        name=name)(*shards, *after)


def _exchange_start(srcs, scatter, name, after=None):
    na = len(srcs)
    lands = [lax.empty(s.shape if scatter else (N_DEV,) + s.shape, s.dtype) for s in srcs]
    extra = [] if after is None else [after]

    def body(*refs):
        src_refs, land_refs = refs[:na], refs[na:2 * na]
        send_sems, recv_sems = refs[2 * na + len(extra)], refs[2 * na + len(extra) + 1]
        token = refs[-1]
        sends, _ = _split_copies(src_refs, land_refs, send_sems, recv_sems, scatter)
        for cp in sends:
            cp.start()
        token[...] = jnp.zeros_like(token)

    hbm = lambda a: pltpu.HBM(a.shape, a.dtype)
    out = pl.pallas_call(
        body, name=name,
        out_shape=(pltpu.SemaphoreType.DMA((7 * na,)), pltpu.SemaphoreType.DMA((7 * na,)),
                   *[hbm(s) for s in srcs], *[hbm(l) for l in lands], jax.ShapeDtypeStruct((8, LANES), F32)),
        in_specs=[_HBM] * (2 * na) + [pl.BlockSpec(memory_space=pl.ANY)] * len(extra),
        out_specs=(_SEM, _SEM, *[_HBM] * (2 * na), pl.BlockSpec(memory_space=pltpu.VMEM)),
        input_output_aliases={i: 2 + i for i in range(2 * na)},
        compiler_params=pltpu.CompilerParams(has_side_effects=_DATAFLOW),
    )(*[pltpu.with_memory_space_constraint(s, pltpu.HBM) for s in srcs],
      *[pltpu.with_memory_space_constraint(l, pltpu.HBM) for l in lands], *extra)
    return (out[0], out[1], list(out[2:2 + na]), list(out[2 + na:2 + 2 * na])), out[-1]


def _exchange_wait(handle, after, scatter, name):
    send_sems, recv_sems, srcs, lands = handle
    na = len(srcs)

    def body(*refs):
        src_refs, land_refs = refs[:na], refs[na:2 * na]
        s_sems, r_sems = refs[2 * na], refs[2 * na + 1]
        sends, recvs = _split_copies(src_refs, land_refs, s_sems, r_sems, scatter)
        for cp in sends:
            cp.wait_send()
        for cp in recvs:
            cp.wait_recv()

    hbm = lambda a: pltpu.HBM(a.shape, a.dtype)
    out = pl.pallas_call(
        body, name=name, out_shape=(*[hbm(s) for s in srcs], *[hbm(l) for l in lands]),
        in_specs=[_HBM] * (2 * na) + [_SEM, _SEM, pl.BlockSpec(memory_space=pl.ANY)],
        out_specs=tuple([_HBM] * (2 * na)), input_output_aliases={i: i for i in range(2 * na)},
        compiler_params=pltpu.CompilerParams(has_side_effects=_DATAFLOW),
    )(*srcs, *lands, send_sems, recv_sems, after)
    return list(out[:na]), list(out[na:])


def _own_slot(landed, own):
    me = 4 * lax.axis_index("x") + 2 * lax.axis_index("y") + lax.axis_index("c")
    return lax.dynamic_update_slice_in_dim(landed, own[None], me, axis=0)


def _adam_update(parts, w, m, v, name, tr=256, turned=False):
    _, r, c = w.shape
    tr = _pick_rows(r, tr)
    cp = parts.shape[2]

    def body(p_ref, w_ref, m_ref, v_ref, g_ref, d_ref, nm_ref, nv_ref):
        cols = pl.ds(0, cp if turned else c)
        g = p_ref[0, :, cols].astype(F32)
        for i in range(1, N_DEV):
            g = g + p_ref[i, :, cols].astype(F32)
        if turned:
            g = g.T[:c]
        delta, nm, nv = _adamw(w_ref[0], g, m_ref[0], v_ref[0])
        g_ref[0] = g
        d_ref[0] = delta
        nm_ref[0] = nm
        nv_ref[0] = nv

    if turned:
        w, m, v = (jnp.transpose(a, (0, 2, 1)) for a in (w, m, v))
        rs = pl.BlockSpec((1, c, tr), lambda i: (0, 0, i))
    else:
        rs = pl.BlockSpec((1, tr, c), lambda i: (0, i, 0))
    outs = pl.pallas_call(
        body, grid=(r // tr,), in_specs=[pl.BlockSpec((N_DEV, tr, cp), lambda i: (0, i, 0)), rs, rs, rs],
        out_specs=[rs] * 4, out_shape=[jax.ShapeDtypeStruct(w.shape, F32)] * 4, name=name,
        compiler_params=_params(("parallel",)))(parts, w, m, v)
    return [jnp.transpose(o, (0, 2, 1)) for o in outs] if turned else outs


def _pick_rows(rows, target):
    if rows <= target:
        return rows
    t = target
    while t >= 16:
        if rows % t == 0:
            return t
        t -= 16
    return rows


BIG = ("w_in", "w_branch_dn", "w_branch_swa", "w_out", "w_gate", "w_up", "w_down")
IN_SHARD, IN_WIRE = D_IN // N_DEV, 640
FF_SHARD, FF_WIRE = D_FF // N_DEV, 384
D_FFP = N_DEV * FF_WIRE
BIG_SHAPES = {"w_in": ((D_MODEL, IN_SHARD), (D_MODEL, IN_WIRE)),
              "w_branch_dn": ((DN_WIDTH, LANES), (DN_WIDTH, LANES)),
              "w_branch_swa": ((SWA_WIDTH, LANES), (SWA_WIDTH, LANES)),
              "w_out": ((LANES, D_MODEL), (LANES, D_MODEL)),
              "w_gate": ((D_MODEL, FF_SHARD), (D_MODEL, FF_WIRE)),
              "w_up": ((D_MODEL, FF_SHARD), (D_MODEL, FF_WIRE)),
              "w_down": ((FF_SHARD, D_MODEL), (FF_WIRE, D_MODEL))}
CONV_SHARD, CONV_WIRE = (DN_CONV, DN_QKV // N_DEV), (8, 256)


def _pad_to(a, shape):
    return jnp.pad(a, [(0, t - s) for s, t in zip(a.shape, shape)])


IN_TILE_ROWS = 256
_IN_SEGS = ((R_GATE, 2048, P_GATE), (R_QKV, DN_QKV, P_QKV), (R_Z, DN_WIDTH, P_Z), (R_SQ, SWA_WIDTH, P_SQ),
            (R_SK, SWA_KVW, P_SK), (R_SV, SWA_KVW, P_SV), (R_B, 8, P_BA))


def _w_in_from_blocks(blocks):
    tm = IN_TILE_ROWS

    def body(b_ref, o_ref):
        parts = []
        for rs, n, _ in _IN_SEGS:
            for dev in range(N_DEV):
                lo, hi = max(rs, IN_SHARD * dev), min(rs + n, IN_SHARD * (dev + 1))
                if lo < hi:
                    parts.append(b_ref[dev][:, lo - IN_SHARD * dev:hi - IN_SHARD * dev])
        parts.append(jnp.zeros((tm, P_WIDTH - P_BA - 8), b_ref.dtype))
        o_ref[...] = jnp.concatenate(parts, axis=1)

    return pl.pallas_call(
        body, grid=(D_MODEL // tm,), in_specs=[pl.BlockSpec((N_DEV, tm, IN_WIRE), lambda i: (0, i, 0))],
        out_specs=pl.BlockSpec((tm, P_WIDTH), lambda i: (i, 0)),
        out_shape=jax.ShapeDtypeStruct((D_MODEL, P_WIDTH), blocks.dtype), name="w_in_from_blocks",
        compiler_params=_params(("parallel",)))(blocks)


def _w_in_to_blocks(g):
    tm = IN_TILE_ROWS

    def body(g_ref, o_ref):
        for dev in range(N_DEV):
            parts = []
            for rs, n, ps in sorted(_IN_SEGS):
                lo, hi = max(rs, IN_SHARD * dev), min(rs + n, IN_SHARD * (dev + 1))
                if lo < hi:
                    parts.append(g_ref[:, ps + lo - rs:ps + hi - rs])
            parts.append(jnp.zeros((tm, IN_WIRE - IN_SHARD), g_ref.dtype))
            o_ref[dev] = jnp.concatenate(parts, axis=1)

    return pl.pallas_call(
        body, grid=(D_MODEL // tm,), in_specs=[pl.BlockSpec((tm, P_WIDTH), lambda i: (i, 0))],
        out_specs=pl.BlockSpec((N_DEV, tm, IN_WIRE), lambda i: (0, i, 0)),
        out_shape=jax.ShapeDtypeStruct((N_DEV, D_MODEL, IN_WIRE), g.dtype), name="w_in_to_blocks",
        compiler_params=_params(("parallel",)))(g)


SMALL = {"attn_norm": (0, (1, D_MODEL)), "ffn_norm": (1, (1, D_MODEL)), "dn_out_norm": (2, (1, DN_DIM)),
         "swa_q_norm": (3, (1, SWA_DIM)), "swa_k_norm": (4, (1, SWA_DIM)), "dn_a_log": (5, (1, DN_HEADS)),
         "dn_dt_bias": (6, (1, DN_HEADS)), "swa_sinks": (7, (1, SWA_HEADS)), "rel_bias": (8, (REL_BUCKETS, SWA_HEADS))}
SMALL_SHEET = (48, D_MODEL)


LOSS_ROW = 40


def _small_pack(grads, loss_local):
    names = list(SMALL)

    def body(*refs):
        o_ref = refs[-1]
        o_ref[...] = jnp.zeros_like(o_ref)
        for n, ref in zip(names, refs):
            r0, (nr, nc) = SMALL[n]
            o_ref[r0:r0 + nr, 0:nc] = ref[...]
        o_ref[LOSS_ROW:LOSS_ROW + 1, 0:1] = refs[len(names)][...]

    return pl.pallas_call(
        body, in_specs=[pl.BlockSpec(memory_space=pltpu.VMEM)] * (len(names) + 1),
        out_specs=pl.BlockSpec(memory_space=pltpu.VMEM), out_shape=jax.ShapeDtypeStruct(SMALL_SHEET, F32),
        name="small_pack", compiler_params=_params())(*[grads[n].reshape(SMALL[n][1]) for n in names], loss_local)


def _small_update(sheets, w, m, v):
    names = list(SMALL)
    k = len(names)

    def body(*refs):
        p_ref = refs[0]
        ins, outs = refs[1:1 + 3 * k], refs[1 + 3 * k:]
        loss = p_ref[0, LOSS_ROW:LOSS_ROW + 1, 0:1]
        for i in range(1, N_DEV):
            loss = loss + p_ref[i, LOSS_ROW:LOSS_ROW + 1, 0:1]
        outs[4 * k][...] = loss
        for t, n in enumerate(names):
            r0, (nr, nc) = SMALL[n]
            g = p_ref[0, r0:r0 + nr, 0:nc]
            for i in range(1, N_DEV):
                g = g + p_ref[i, r0:r0 + nr, 0:nc]
            delta, nm, nv = _adamw(ins[t][...], g, ins[k + t][...], ins[2 * k + t][...])
            for kind, val in enumerate((g, delta, nm, nv)):
                outs[kind * k + t][...] = val

    shapes = [jax.ShapeDtypeStruct(SMALL[n][1], F32) for n in names]
    vm = pl.BlockSpec(memory_space=pltpu.VMEM)
    res = pl.pallas_call(
        body, in_specs=[vm] * (1 + 3 * k), out_specs=[vm] * (4 * k + 1),
        out_shape=shapes * 4 + [jax.ShapeDtypeStruct((1, 1), F32)], name="adam_small", compiler_params=_params(),
    )(sheets, *[d[n].reshape(SMALL[n][1]) for d in (w, m, v) for n in names])
    return {n: tuple(res[kind * k + t] for kind in range(4)) for t, n in enumerate(names)}, res[4 * k]


def kernel(x, attn_norm, w_in, dn_conv, dn_a_log, dn_dt_bias, dn_out_norm, swa_q_norm, swa_k_norm, swa_sinks, rel_bias, w_branch_dn, w_branch_swa, w_out, ffn_norm, w_gate, w_up, w_down, loss_target, m_attn_norm, m_w_in, m_dn_conv, m_dn_a_log, m_dn_dt_bias, m_dn_out_norm, m_swa_q_norm, m_swa_k_norm, m_swa_sinks, m_rel_bias, m_w_branch_dn, m_w_branch_swa, m_w_out, m_ffn_norm, m_w_gate, m_w_up, m_w_down, v_attn_norm, v_w_in, v_dn_conv, v_dn_a_log, v_dn_dt_bias, v_dn_out_norm, v_swa_q_norm, v_swa_k_norm, v_swa_sinks, v_rel_bias, v_w_branch_dn, v_w_branch_swa, v_w_out, v_ffn_norm, v_w_gate, v_w_up, v_w_down):
    args = dict(locals())
    S = x.shape[1]
    xs = x.reshape(S, D_MODEL)
    target = loss_target.reshape(S, D_MODEL)

    w_loc = {n: args[n].reshape(BIG_SHAPES[n][0]) for n in BIG}
    conv_loc = dn_conv.reshape(CONV_SHARD)
    wire = {n: _pad_to(w_loc[n], BIG_SHAPES[n][1]).astype(BF16) for n in BIG}
    first = _all_gather([wire["w_in"], _pad_to(conv_loc, CONV_WIRE)])
    later = [n for n in BIG if n != "w_in"]
    rest_handle, rest_token = _exchange_start([wire[n] for n in later], False, "gather_rest_start", after=first[1])
    w_pad = _w_in_from_blocks(first[0])
    conv_w = jnp.concatenate([first[1][d, :DN_CONV, :CONV_SHARD[1]] for d in range(N_DEV)], axis=1)

    h = _norm_fwd(xs, attn_norm + rest_token[0, 0], "norm1_fwd")
    proj = _mm([(h, w_pad)], "nn", F32, "mm_in", 1024, 1664, j_outer=True)
    qkvn = _dn_conv_fwd(proj, conv_w)
    beta, g = _dn_gate_fwd(proj, dn_a_log, dn_dt_bias)
    u, w, qe, kd, qk, egl, tinv = _dn_prep_fwd(qkvn, g, beta)
    o, states = _dn_scan_fwd(u, w, qe, kd, qk, egl)
    y_dn = _dn_out_fwd(o, proj, dn_out_norm)
    bias = _bias_fwd(rel_bias)
    y_swa = _swa_fwd(proj, swa_q_norm, swa_k_norm, swa_sinks, bias)
    rest_src, rest_land = _exchange_wait(rest_handle, y_swa, False, "gather_rest_wait")
    G = {n: _own_slot(land, src) for n, src, land in zip(later, rest_src, rest_land)}
    w_bdn, w_bswa, w_g, w_u = G["w_branch_dn"], G["w_branch_swa"], G["w_gate"], G["w_up"]
    w_o = G["w_out"].reshape(D_MODEL, D_MODEL)
    w_d = G["w_down"].reshape(D_FFP, D_MODEL)
    gates = [(proj, P_GATE // 512), (proj, (P_GATE + D_MODEL) // 512)]
    a_dn, a_swa, merged = _mm_fused(
        [(y_dn, w_bdn), (y_swa, w_bswa)], "nn", "mm_branch_merge", 1024, 512,
        lambda p, e: (p[0], p[1], _merge(e[0], e[1], p[0], p[1])), gates, (F32, F32, BF16), b_blocks=True)

    def resid_norm(p, e):
        x1 = e[0] + p[0]
        return x1, _rms(x1, e[1])

    x1, h2 = _mm_fused([(merged, w_o)], "nn", "mm_out_norm", 512, D_MODEL, resid_norm,
                       [(xs, 0), (ffn_norm, None)], (F32, BF16))
    gate, up, act = _mm_fused([(h2, w_g), (h2, w_u)], "nn", "mm_gate_up_act", 1024, 768,
                              lambda p, e: (p[0], p[1], _act(p[0], p[1])), [], (F32, F32, BF16),
                              j_outer=True, b_blocks=True)

    def loss_head(p, e):
        diff = e[0] + p[0] - e[1]
        dy = diff * (1.0 / D_MODEL)
        part = jnp.sum(jnp.mean(diff * diff, axis=-1, keepdims=True), axis=0, keepdims=True) * 0.5
        return dy, dy, part

    dy, dy_b, loss_local = _mm_fused([(act, w_d)], "nn", "mm_down_loss", 512, D_MODEL, loss_head,
                                     [(x1, 0), (target, 0)], (F32, BF16), sum_shape=(1, 1))

    def act_bwd(p, e):
        _, vjp = jax.vjp(_act, e[0], e[1])
        return vjp(p[0])

    dgate, dup = _mm_fused([(dy_b, w_d)], "nt", "mm_dact_act", 1024, 768, act_bwd, [(gate, 0), (up, 0)],
                           (BF16, BF16), j_outer=True)
    g_w_down = _mm([(act, dy_b)], "tn", BF16, "mm_dw_down", 768, D_MODEL, j_outer=True)
    g_w_down = g_w_down.reshape(N_DEV, FF_WIRE, D_MODEL)
    g_w_gate = _mm([(h2, dgate)], "tn", BF16, "mm_dw_gate", D_MODEL, 768, out_blocks=True)
    g_w_up = _mm([(h2, dup)], "tn", BF16, "mm_dw_up", D_MODEL, 768, out_blocks=True)
    ffn_handle, ffn_token = _exchange_start([g_w_down, g_w_gate, g_w_up], True, "scatter_ffn_start")

    def norm_bwd(p, e):
        _, vjp = jax.vjp(_rms, e[0], e[2])
        dx, dgain = vjp(sum(p))
        dx = dx + e[1]
        return dx, dx, dgain

    dx1, dx1_b, g_ffn_norm = _mm_fused(
        [(dgate, w_g), (dup, w_u)], "nt", "mm_dh2_norm", 256, D_MODEL, norm_bwd,
        [(x1, 0), (dy, 0), (ffn_norm + ffn_token[0, 0], None)], (F32, BF16), b_blocks=True, sum_shape=(1, D_MODEL))
    def merge_bwd(p, e):
        _, vjp = jax.vjp(_merge, *e)
        dg0, dg1, da_dn, da_swa = vjp(p[0])
        return jnp.concatenate([dg0, dg1], axis=1), da_dn, da_swa

    dproj, da_dn, da_swa = _mm_fused(
        [(dx1_b, w_o)], "nt", "mm_dmerged_merge", 512, D_MODEL, merge_bwd,
        [(proj, P_GATE // D_MODEL), (proj, P_GATE // D_MODEL + 1), (a_dn, 0), (a_swa, 0)], (BF16,) * 3,
        wide_first=(P_WIDTH, 2 * D_MODEL))
    g_w_out = _mm([(merged, dx1_b)], "tn", BF16, "mm_dw_out", 512, D_MODEL, j_outer=True)
    g_w_out = g_w_out.reshape(N_DEV, LANES, D_MODEL)
    dy_dn = _mm([(da_dn, w_bdn)], "nt", F32, "mm_dy_dn", 1024, DN_WIDTH, b_blocks=True)
    dy_swa = _mm([(da_swa, w_bswa)], "nt", F32, "mm_dy_swa", 1024, SWA_WIDTH, b_blocks=True)
    g_w_bdn = _mm([(y_dn, da_dn)], "tn", BF16, "mm_dw_branch_dn", DN_WIDTH, 512, out_blocks=True)
    g_w_bswa = _mm([(y_swa, da_swa)], "tn", BF16, "mm_dw_branch_swa", SWA_WIDTH, 512, out_blocks=True)
    dproj, dsk, dsv, g_q_norm, g_k_norm, g_sinks, dbias = _swa_bwd(proj, swa_q_norm, swa_k_norm, swa_sinks, bias,
                                                                   dy_swa, dproj)
    dproj = _kv_into(dproj, dsk, dsv)
    g_rel_bias = _bias_bwd(dbias)[:, :REL_BUCKETS].T
    mix_handle, mix_token = _exchange_start([g_w_out, g_w_bdn, g_w_bswa], True, "scatter_mix_start")
    do, dproj, g_out_norm = _dn_out_bwd(o, proj, dn_out_norm + mix_token[0, 0], dy_dn, dproj)
    du, dw, dqe, dkd, dqk, degl = _dn_scan_bwd(u, w, qe, kd, qk, egl, states, do)
    dqkvn, dgd, dbeta = _dn_prep_bwd(qkvn, g, beta, tinv, du, dw, dqe, dkd, dqk, degl)
    dproj, dal, ddt = _dn_gate_bwd(proj, dn_a_log, dn_dt_bias, dbeta, dgd, dproj)
    g_a_log = dal.reshape(DN_HEADS, DN_DIM).sum(axis=1)
    g_dt_bias = ddt[0, DN_HEADS:2 * DN_HEADS]
    dproj, g_conv = _dn_conv_bwd(proj, conv_w, dqkvn, dproj)
    g_w_in = _w_in_to_blocks(_mm([(h, dproj)], "tn", BF16, "mm_dw_in", 512, 1664, j_outer=True))
    in_handle, in_token = _exchange_start([g_w_in], True, "scatter_in_start")
    dx, g_attn_norm = _mm_fused(
        [(dproj, w_pad)], "nt", "mm_dh_norm", 512, D_MODEL, lambda p, e: norm_bwd(p, e)[1:],
        [(xs, 0), (dx1, 0), (attn_norm + in_token[0, 0], None)], (F32,), sum_shape=(1, D_MODEL))

    g_small = {"attn_norm": g_attn_norm, "ffn_norm": g_ffn_norm, "rel_bias": g_rel_bias, "dn_out_norm": g_out_norm,
               "swa_q_norm": g_q_norm, "swa_k_norm": g_k_norm, "dn_a_log": g_a_log, "dn_dt_bias": g_dt_bias,
               "swa_sinks": g_sinks}
    me = 4 * lax.axis_index("x") + 2 * lax.axis_index("y") + lax.axis_index("c")
    outs = {}

    def finish(handle, group, name, after):
        srcs, lands = _exchange_wait(handle, after, True, name)
        for n, src, land in zip(group, srcs, lands):
            parts = _own_slot(land, lax.dynamic_index_in_dim(src, me, 0, keepdims=False))
            outs[n] = _adam_update(parts, args[n], args["m_" + n], args["v_" + n], "adam_" + n,
                                   turned=args[n].shape[2] % LANES != 0)

    finish(ffn_handle, ("w_down", "w_gate", "w_up"), "scatter_ffn_wait", dx)
    finish(mix_handle, ("w_out", "w_branch_dn", "w_branch_swa"), "scatter_mix_wait", dx)
    sheets, conv_all = _all_gather_direct([_small_pack(g_small, loss_local), _pad_to(g_conv, (8, DN_QKV))],
                                          "all_gather_small",
                                          after=[outs[n][0] for n in sorted(outs)])
    finish(in_handle, ("w_in",), "scatter_in_wait", sheets)
    conv_parts = lax.dynamic_slice(conv_all, (0, 0, me * CONV_SHARD[1]), (N_DEV,) + CONV_SHARD)
    outs["dn_conv"] = _adam_update(conv_parts, dn_conv, m_dn_conv, v_dn_conv, "adam_dn_conv")
    small_outs, loss = _small_update(sheets, {n: args[n] for n in SMALL}, {n: args["m_" + n] for n in SMALL},
                                     {n: args["v_" + n] for n in SMALL})
    outs.update(small_outs)

    names = ("attn_norm", "w_in", "dn_conv", "dn_a_log", "dn_dt_bias", "dn_out_norm", "swa_q_norm", "swa_k_norm",
             "swa_sinks", "rel_bias", "w_branch_dn", "w_branch_swa", "w_out", "ffn_norm", "w_gate", "w_up", "w_down")
    results = []
    for kind in range(4):
        results += [outs[n][kind].reshape(args[n].shape) for n in names]

    return (loss.reshape(()), dx.reshape(x.shape), *results)
```

```python
import math

import numpy as np
import jax
import jax.numpy as jnp
from jax import lax
from jax.experimental import pallas as pl
from jax.experimental.pallas import tpu as pltpu

F32 = jnp.float32
BF16 = jnp.bfloat16
HI = lax.Precision.HIGHEST

D_MODEL = 1024
DN_HEADS = 4
DN_DIM = 128
DN_WIDTH = 512
DN_QKV = 1536
DN_CONV = 4
CHUNK = 64
SWA_HEADS = 8
SWA_KV = 2
SWA_GROUP = 4
SWA_DIM = 64
SWA_WIDTH = 512
SWA_KVW = 128
WINDOW = 128
BLOCK = 128
REL_BUCKETS = 32
REL_MAX_DIST = 128
D_FF = 2816
D_IN = 4872
EPS = 1e-6
N_DEV = 8

ADAM_LR = 0.001
ADAM_B1 = 0.9
ADAM_B2 = 0.999
ADAM_EPS = 1e-08
ADAM_WD = 0.01
ADAM_STEP = 10

P_GATE, P_QKV, P_Z, P_SQ, P_SK, P_SV, P_BA = 0, 2048, 3584, 4096, 4608, 4736, 4864
P_WIDTH = 4992
R_QKV, R_Z, R_B, R_A, R_SQ, R_SK, R_SV, R_GATE = 0, 1536, 2048, 2052, 2056, 2568, 2696, 2824

VMEM_LIMIT = 56 * 1024 * 1024
LANES = 128
MESH_ID = pl.DeviceIdType.MESH


def _params(sem=None):
    return pltpu.CompilerParams(dimension_semantics=sem, vmem_limit_bytes=VMEM_LIMIT)


def _pick(dim, target):
    if dim <= target:
        return dim
    t = target - target % LANES
    while t >= LANES:
        if dim % t == 0:
            return t
        t -= LANES
    return dim


_DIMS = {"nn": (((1,), (0,)), ((), ())), "nt": (((1,), (1,)), ((), ())), "tn": (((0,), (0,)), ((), ()))}


def _tile_product(a_ref, b_ref, mode, b_blocks):
    a = a_ref[...].astype(BF16)
    b = jnp.concatenate([b_ref[d] for d in range(b_ref.shape[0])], axis=1) if b_blocks else b_ref[...]
    return lax.dot_general(a, b.astype(BF16), _DIMS[mode], preferred_element_type=F32)


def _mm(pairs, mode, out_dtype, name, bm, bn, j_outer=False, b_blocks=False, out_blocks=False):
    a0, b0 = pairs[0]
    cb = b0.shape[2] if b_blocks else None
    b_shape = (b0.shape[1], N_DEV * cb) if b_blocks else b0.shape
    if mode == "nn":
        (M, K), (K2, N) = a0.shape, b_shape
    elif mode == "nt":
        (M, K), (N, K2) = a0.shape, b_shape
    else:
        (K, M), (K2, N) = a0.shape, b_shape
    bm, bn = min(bm, M), min(bn, N)
    assert K == K2 and M % bm == 0 and N % bn == 0, (name, a0.shape, b0.shape, bm, bn)
    co = N // N_DEV
    assert not out_blocks or bn % co == 0
    dims = _DIMS[mode]
    n = len(pairs)

    def body(*refs):
        o_ref = refs[2 * n]
        acc = None
        for t in range(n):
            p = _tile_product(refs[2 * t], refs[2 * t + 1], mode, b_blocks)
            acc = p if acc is None else acc + p
        if out_blocks:
            for d in range(bn // co):
                o_ref[d] = acc[:, d * co:(d + 1) * co].astype(out_dtype)
        else:
            o_ref[...] = acc.astype(out_dtype)

    def ij(f):
        return (lambda j, i: f(i, j)) if j_outer else f

    a_spec = pl.BlockSpec((K, bm), ij(lambda i, j: (0, i))) if mode == "tn" else pl.BlockSpec((bm, K), ij(lambda i, j: (i, 0)))
    if b_blocks and mode == "nt":
        b_spec = pl.BlockSpec((N_DEV, bn, cb), ij(lambda i, j: (0, j, 0)))
    elif b_blocks:
        b_spec = pl.BlockSpec((bn // cb, K, cb), ij(lambda i, j: (j, 0, 0)))
    elif mode == "nt":
        b_spec = pl.BlockSpec((bn, K), ij(lambda i, j: (j, 0)))
    else:
        b_spec = pl.BlockSpec((K, bn), ij(lambda i, j: (0, j)))
    if out_blocks:
        out_spec = pl.BlockSpec((bn // co, bm, co), ij(lambda i, j: (j, i, 0)))
        out_shape = jax.ShapeDtypeStruct((N_DEV, M, co), out_dtype)
    else:
        out_spec = pl.BlockSpec((bm, bn), ij(lambda i, j: (i, j)))
        out_shape = jax.ShapeDtypeStruct((M, N), out_dtype)
    grid = (N // bn, M // bm) if j_outer else (M // bm, N // bn)
    return pl.pallas_call(
        body, grid=grid, in_specs=[a_spec, b_spec] * n, out_specs=out_spec, out_shape=out_shape, name=name,
        compiler_params=_params(("parallel", "parallel")),
    )(*[x for pair in pairs for x in pair])


def _mm_fused(pairs, mode, name, bm, bn, epilogue, extras, out_dtypes, j_outer=False, b_blocks=False,
              sum_shape=None, wide_first=None):
    a0, b0 = pairs[0]
    cb = b0.shape[2] if b_blocks else None
    b_shape = (b0.shape[1], N_DEV * cb) if b_blocks else b0.shape
    if mode == "nn":
        (M, K), (K2, N) = a0.shape, b_shape
    else:
        (M, K), (N, K2) = a0.shape, b_shape
    bm, bn = min(bm, M), min(bn, N)
    assert mode in ("nn", "nt") and K == K2 and M % bm == 0 and N % bn == 0, (name, a0.shape, b0.shape)
    dims = _DIMS[mode]
    n, ne, no = len(pairs), len(extras), len(out_dtypes)

    def body(*refs):
        prods = [_tile_product(refs[2 * t], refs[2 * t + 1], mode, b_blocks) for t in range(n)]
        results = epilogue(prods, [r[...] for r in refs[2 * n:2 * n + ne]])
        out_refs = refs[2 * n + ne:]
        for o_ref, val, dt in zip(out_refs, results, out_dtypes):
            o_ref[...] = val.astype(dt)
        if sum_shape is not None:
            s_ref = out_refs[no]

            @pl.when((pl.program_id(0) == 0) & (pl.program_id(1) == 0))
            def _():
                s_ref[...] = jnp.zeros_like(s_ref)

            s_ref[...] += results[no]

    def ij(f):
        return (lambda j, i: f(i, j)) if j_outer else f

    a_spec = pl.BlockSpec((bm, K), ij(lambda i, j: (i, 0)))
    once = dict(pipeline_mode=pl.Buffered(1)) if bn == N else {}
    if b_blocks and mode == "nt":
        b_spec = pl.BlockSpec((N_DEV, bn, cb), ij(lambda i, j: (0, j, 0)), **once)
    elif b_blocks:
        b_spec = pl.BlockSpec((bn // cb, K, cb), ij(lambda i, j: (j, 0, 0)), **once)
    elif mode == "nt":
        b_spec = pl.BlockSpec((bn, K), ij(lambda i, j: (j, 0)), **once)
    else:
        b_spec = pl.BlockSpec((K, bn), ij(lambda i, j: (0, j)), **once)
    e_specs = [pl.BlockSpec((1, bn), ij(lambda i, j: (0, j))) if first is None
               else pl.BlockSpec((bm, bn), ij(lambda i, j, first=first: (i, first + j))) for _, first in extras]
    tile = pl.BlockSpec((bm, bn), ij(lambda i, j: (i, j)))
    out_specs = [tile] * no
    out_shape = [jax.ShapeDtypeStruct((M, N), dt) for dt in out_dtypes]
    if wide_first is not None:
        assert bn == N
        out_specs[0] = pl.BlockSpec((bm, wide_first[1]), ij(lambda i, j: (i, 0)))
        out_shape[0] = jax.ShapeDtypeStruct((M, wide_first[0]), out_dtypes[0])
    if sum_shape is not None:
        assert sum_shape[1] in (1, bn) and (sum_shape[1] == 1 or bn == N)
        out_specs.append(_full(sum_shape))
        out_shape.append(jax.ShapeDtypeStruct(sum_shape, F32))
    grid = (N // bn, M // bm) if j_outer else (M // bm, N // bn)
    sem = ("arbitrary", "arbitrary") if sum_shape is not None else ("parallel", "parallel")
    return pl.pallas_call(
        body, grid=grid, in_specs=[a_spec, b_spec] * n + e_specs, out_specs=out_specs, out_shape=out_shape,
        name=name, compiler_params=_params(sem),
    )(*[x for pair in pairs for x in pair], *[arr for arr, _ in extras])


def _rms(x, gain):
    return x * lax.rsqrt(jnp.mean(x * x, axis=-1, keepdims=True) + EPS) * gain


def _silu(x):
    return x * jax.nn.sigmoid(x)


def _act(g, u):
    return _silu(g) * u


def _merge(g0, g1, a_dn, a_swa):
    return jax.nn.sigmoid(g0) * a_dn + jax.nn.sigmoid(g1) * a_swa


def _dn_post(c, is_v, q_scale):
    a = _silu(c)
    rs = lax.rsqrt(jnp.sum(a * a, axis=-1, keepdims=True) + EPS) * q_scale
    return a * jnp.where(is_v, 1.0, rs)


def _dn_out(o, z, gain):
    return _rms(o, gain) * _silu(z)


def _dot(a, b, dims=_DIMS["nn"], hi=False):
    if a.ndim == 3 or b.ndim == 3:
        batch = a.shape[0] if a.ndim == 3 else b.shape[0]
        a = a if a.ndim == 3 else jnp.broadcast_to(a, (batch,) + a.shape)
        b = b if b.ndim == 3 else jnp.broadcast_to(b, (batch,) + b.shape)
        ((ca,), (cb,)), _ = dims
        dims = (((ca + 1,), (cb + 1,)), ((0,), (0,)))
    if hi:
        return lax.dot_general(a, b, dims, precision=HI, preferred_element_type=F32)
    return lax.dot_general(a.astype(BF16), b.astype(BF16), dims, preferred_element_type=F32)


def _pieces(x):
    hi = x.astype(BF16)
    r1 = x - hi.astype(F32)
    mid = r1.astype(BF16)
    return hi, mid, (r1 - mid.astype(F32)).astype(BF16)


def _sel_left_impl(m, x):
    mb = m.astype(BF16)
    hi, mid, lo = _pieces(x)
    return _dot(mb, hi) + (_dot(mb, mid) + _dot(mb, lo))


@jax.custom_vjp
def _sel_left(m, mt, x):
    return _sel_left_impl(m, x)


_sel_left.defvjp(lambda m, mt, x: (_sel_left_impl(m, x), (m, mt)),
                 lambda res, ct: (jnp.zeros_like(res[0]), jnp.zeros_like(res[1]), _sel_left_impl(res[1], ct)))


def _sel_right_impl(x, s):
    sb = s.astype(BF16)
    hi, mid, lo = _pieces(x)
    return _dot(hi, sb) + (_dot(mid, sb) + _dot(lo, sb))


@jax.custom_vjp
def _sel_right(x, s, st):
    return _sel_right_impl(x, s)


_sel_right.defvjp(lambda x, s, st: (_sel_right_impl(x, s), (s, st)),
                  lambda res, ct: (_sel_right_impl(ct, res[1]), jnp.zeros_like(res[0]), jnp.zeros_like(res[1])))


def _dot3_impl(a, b):
    a_hi, a_lo, _ = _pieces(a)
    b_hi, b_lo, _ = _pieces(b)
    return _dot(a_hi, b_hi) + (_dot(a_hi, b_lo) + _dot(a_lo, b_hi))


@jax.custom_vjp
def _dot3(a, b):
    return _dot3_impl(a, b)


_dot3.defvjp(lambda a, b: (_dot3_impl(a, b), (a, b)),
             lambda res, ct: (_dot(ct, res[1], _DIMS["nt"]), _dot(res[0], ct, _DIMS["tn"])))


def _inv_impl(a, eye, strict):
    t = eye - a
    p = _dot(a, a)
    for level in range(5):
        t = t + _dot(t, p)
        if level < 4:
            p = _dot(p, p)
    t = t + _dot(t, eye - t - _dot3_impl(a, t))
    return jnp.where(strict > 0.5, t, eye)


@jax.custom_vjp
def _inv_given(a, t):
    return t.astype(F32)


_inv_given.defvjp(lambda a, t: (t.astype(F32), t),
                  lambda t, ct: (-_dot(_dot(t, ct, _DIMS["tn"]), t, _DIMS["nt"]), jnp.zeros_like(t)))


@jax.custom_vjp
def _lanes_join(a, b):
    return jnp.concatenate([a, b], axis=-1)


_lanes_join.defvjp(lambda a, b: (jnp.concatenate([a, b], axis=-1), None),
                   lambda _, ct: (ct[..., :ct.shape[-1] // 2], ct[..., ct.shape[-1] // 2:]))


@jax.custom_vjp
def _lanes_halves(y):
    h = y.shape[-1] // 2
    return y[..., :h], y[..., h:]


_lanes_halves.defvjp(lambda y: ((y[..., :y.shape[-1] // 2], y[..., y.shape[-1] // 2:]), None),
                     lambda _, ct: (jnp.concatenate(ct, axis=-1),))

GROUP = 4
GROUP_ROWS = GROUP * CHUNK


def _block_consts(n):
    ii = lax.broadcasted_iota(jnp.int32, (n, n), 0)
    jj = lax.broadcasted_iota(jnp.int32, (n, n), 1)
    shift = CHUNK.bit_length() - 1
    same = jnp.right_shift(ii, shift) == jnp.right_shift(jj, shift)
    return same & (ii >= jj), same & (ii <= jj), same & (ii > jj), same, ii == jj


def _lane0(n):
    s = (lax.broadcasted_iota(jnp.int32, (LANES, n), 0) == 0).astype(F32)
    st = (lax.broadcasted_iota(jnp.int32, (n, LANES), 1) == 0).astype(F32)
    return s, st


def _dn_group(q, k, v, g, beta, t_saved=None):
    n = GROUP_ROWS
    low_b, upp_b, strict_b, _, eye_b = _block_consts(n)
    low, upp, eye = low_b.astype(F32), upp_b.astype(F32), eye_b.astype(F32)
    gc = _sel_left(low, upp, g)
    per_chunk = (g.shape[0], GROUP, CHUNK, LANES)
    g_last = jnp.sum(g.reshape(per_chunk), axis=2, keepdims=True)
    gl = jnp.broadcast_to(g_last, per_chunk).reshape(g.shape)
    s, st = _lane0(n)
    col = _sel_right(gc, s, st)
    row = jnp.swapaxes(col, 1, 2)
    decay = jnp.exp(jnp.where(low_b, col - row, -jnp.inf))
    kb = k * beta
    vb = v * beta
    a = jnp.where(strict_b, _dot(kb, k, _DIMS["nt"]) * decay, 0.0)
    t = _inv_impl(a, eye, strict_b.astype(F32)) if t_saved is None else _inv_given(a, t_saved)
    u, w = _lanes_halves(_dot3(t, _lanes_join(vb, kb * jnp.exp(gc))))
    fold = (jnp.bitwise_and(lax.broadcasted_iota(jnp.int32, (n, CHUNK), 0), CHUNK - 1)
            == lax.broadcasted_iota(jnp.int32, (n, CHUNK), 1)).astype(F32)
    fold_t = (jnp.bitwise_and(lax.broadcasted_iota(jnp.int32, (CHUNK, n), 1), CHUNK - 1)
              == lax.broadcasted_iota(jnp.int32, (CHUNK, n), 0)).astype(F32)
    qk = _sel_right(_dot(q, k, _DIMS["nt"]) * decay, fold, fold_t)
    return u, w, q * jnp.exp(gc), k * jnp.exp(gl - gc), qk, jnp.exp(g_last), t


def _dn_step(s, u, w, qe, kd, qk, egl):
    v_new = u - _dot(w, s)
    o = _dot(qe, s) + _dot(qk, v_new)
    s_new = s * egl + _dot(kd, v_new, _DIMS["tn"])
    return s_new, o


def _swa_block(q, kband, vband, qg, kg, sinks, band):
    kn = _rms(kband, kg)
    qn = _rms(q, qg) * (SWA_DIM ** -0.5)
    logits = _dot(qn, kn, _DIMS["nt"]) + band
    m = lax.stop_gradient(jnp.maximum(jnp.max(logits, axis=-1, keepdims=True), sinks))
    p = jnp.exp(logits - m)
    denom = jnp.sum(p, axis=-1, keepdims=True) + jnp.exp(sinks - m)
    return _dot(p * (1.0 / denom), vband)


def _adamw(w, g, m, v):
    m = ADAM_B1 * m + (1.0 - ADAM_B1) * g
    v = ADAM_B2 * v + (1.0 - ADAM_B2) * jnp.square(g)
    m_hat = m / (1.0 - ADAM_B1 ** ADAM_STEP)
    v_hat = v / (1.0 - ADAM_B2 ** ADAM_STEP)
    delta = -ADAM_LR * (m_hat / (jnp.sqrt(v_hat) + ADAM_EPS) + ADAM_WD * w)
    return delta, m, v


def _row(tm, c, cb=0):
    return pl.BlockSpec((tm, c), lambda i, cb=cb: (i, cb))


def _full(shape):
    nd = len(shape)
    return pl.BlockSpec(shape, lambda *_, nd=nd: (0,) * nd)


def _norm_fwd(x, gain, name, tm=1024):
    S = x.shape[0]

    def body(x_ref, g_ref, h_ref):
        h_ref[...] = _rms(x_ref[...], g_ref[...]).astype(BF16)

    return pl.pallas_call(
        body, grid=(S // tm,), in_specs=[_row(tm, D_MODEL), _full((1, D_MODEL))],
        out_specs=_row(tm, D_MODEL), out_shape=jax.ShapeDtypeStruct((S, D_MODEL), BF16),
        name=name, compiler_params=_params(("parallel",)))(x, gain)


def _shift_down(x, s):
    row = lax.broadcasted_iota(jnp.int32, x.shape, 0)
    return jnp.where(row >= s, pltpu.roll(x, s, axis=0), 0.0)


def _shift_up(x, s):
    n = x.shape[0]
    row = lax.broadcasted_iota(jnp.int32, x.shape, 0)
    return jnp.where(row < n - s, pltpu.roll(x, n - s, axis=0), 0.0)


def _conv(x, w):
    out = w[DN_CONV - 1:DN_CONV] * x
    for s in range(1, DN_CONV):
        out = out + w[DN_CONV - 1 - s:DN_CONV - s] * _shift_down(x, s)
    return out


def _dn_conv_fwd(proj, conv_w):
    S = proj.shape[0]
    nb = DN_QKV // LANES

    def body(x_ref, w_ref, o_ref):
        j = pl.program_id(0)
        q_scale = jnp.where(j < DN_HEADS, DN_DIM ** -0.5, 1.0).astype(F32)
        o_ref[...] = _dn_post(_conv(x_ref[...], w_ref[...]), j >= 2 * DN_HEADS, q_scale)

    return pl.pallas_call(
        body, grid=(nb,),
        in_specs=[pl.BlockSpec((S, LANES), lambda j: (0, P_QKV // LANES + j)),
                  pl.BlockSpec((DN_CONV, LANES), lambda j: (0, j))],
        out_specs=pl.BlockSpec((S, LANES), lambda j: (0, j)),
        out_shape=jax.ShapeDtypeStruct((S, DN_QKV), F32), name="dn_conv_fwd",
        compiler_params=_params(("parallel",)))(proj, conv_w)


def _dn_conv_bwd(proj, conv_w, dqkvn, dproj):
    S = proj.shape[0]
    nb = DN_QKV // LANES

    def body(x_ref, w_ref, d_ref, _, dx_ref, dw_ref):
        j = pl.program_id(0)
        q_scale = jnp.where(j < DN_HEADS, DN_DIM ** -0.5, 1.0).astype(F32)
        x = x_ref[...]
        w = w_ref[...]
        _, vjp = jax.vjp(lambda c: _dn_post(c, j >= 2 * DN_HEADS, q_scale), _conv(x, w))
        (dc,) = vjp(d_ref[0])
        dx = w[DN_CONV - 1:DN_CONV] * dc
        dw_ref[DN_CONV - 1:DN_CONV, :] = jnp.sum(dc * x, axis=0, keepdims=True)
        for s in range(1, DN_CONV):
            dx = dx + w[DN_CONV - 1 - s:DN_CONV - s] * _shift_up(dc, s)
            dw_ref[DN_CONV - 1 - s:DN_CONV - s, :] = jnp.sum(dc * _shift_down(x, s), axis=0, keepdims=True)
        dx_ref[...] = dx.astype(BF16)

    return pl.pallas_call(
        body, grid=(nb,),
        in_specs=[pl.BlockSpec((S, LANES), lambda j: (0, P_QKV // LANES + j)),
                  pl.BlockSpec((DN_CONV, LANES), lambda j: (0, j)),
                  pl.BlockSpec((1, S, LANES), lambda j: (lax.div(j, DN_HEADS), 0, lax.rem(j, DN_HEADS))),
                  pl.BlockSpec(memory_space=pl.ANY)],
        out_specs=[pl.BlockSpec((S, LANES), lambda j: (0, P_QKV // LANES + j)),
                   pl.BlockSpec((DN_CONV, LANES), lambda j: (0, j))],
        out_shape=[jax.ShapeDtypeStruct(dproj.shape, dproj.dtype), jax.ShapeDtypeStruct((DN_CONV, DN_QKV), F32)],
        input_output_aliases={3: 0},
        name="dn_conv_bwd", compiler_params=_params(("parallel",)))(proj, conv_w, dqkvn, dproj)


def _expanders():
    eb = np.zeros((LANES, DN_WIDTH), np.float32)
    ea = np.zeros((LANES, DN_WIDTH), np.float32)
    for h in range(DN_HEADS):
        eb[h, h * DN_DIM:(h + 1) * DN_DIM] = 1.0
        ea[DN_HEADS + h, h * DN_DIM:(h + 1) * DN_DIM] = 1.0
    return jnp.asarray(eb), jnp.asarray(ea), jnp.asarray(eb.T), jnp.asarray(ea.T)


def _dn_gate_args(a_log, dt_bias):
    alog = jnp.repeat(a_log.reshape(1, DN_HEADS), DN_DIM, axis=1)
    dtb = _pad_to(jnp.pad(dt_bias.reshape(1, DN_HEADS), ((0, 0), (DN_HEADS, 0))), (1, LANES))
    return _expanders() + (alog, dtb)


def _dn_gate_specs(tm):
    return [_row(tm, LANES, P_BA // LANES), _full((LANES, DN_WIDTH)), _full((LANES, DN_WIDTH)),
            _full((DN_WIDTH, LANES)), _full((DN_WIDTH, LANES)), _full((1, DN_WIDTH)), _full((1, LANES))]


def _dn_gate_fn(ba, eb, ea, ebt, eat, alog, dtb):
    beta = _sel_right(jax.nn.sigmoid(ba), eb, ebt)
    g = -jnp.exp(alog) * _sel_right(jax.nn.softplus(ba + dtb), ea, eat)
    return beta, g


def _dn_gate_fwd(proj, a_log, dt_bias, tm=1024):
    S = proj.shape[0]
    args = _dn_gate_args(a_log, dt_bias)

    def body(ba_ref, eb_ref, ea_ref, ebt_ref, eat_ref, al_ref, dt_ref, beta_ref, g_ref):
        beta, g = _dn_gate_fn(ba_ref[...], eb_ref[...], ea_ref[...], ebt_ref[...], eat_ref[...], al_ref[...],
                              dt_ref[...])
        beta_ref[...] = beta
        g_ref[...] = g

    return pl.pallas_call(
        body, grid=(S // tm,), in_specs=_dn_gate_specs(tm), out_specs=[_row(tm, DN_WIDTH), _row(tm, DN_WIDTH)],
        out_shape=[jax.ShapeDtypeStruct((S, DN_WIDTH), F32), jax.ShapeDtypeStruct((S, DN_WIDTH), F32)],
        name="dn_gate_fwd", compiler_params=_params(("parallel",)))(proj, *args)


def _dn_gate_bwd(proj, a_log, dt_bias, dbeta, dg, dproj, tm=1024):
    S = proj.shape[0]
    args = _dn_gate_args(a_log, dt_bias)

    def body(ba_ref, eb_ref, ea_ref, ebt_ref, eat_ref, al_ref, dt_ref, dbeta_ref, dg_ref, _, dba_ref, dal_ref,
             ddt_ref):
        eb, ea, ebt, eat = eb_ref[...], ea_ref[...], ebt_ref[...], eat_ref[...]
        _, vjp = jax.vjp(lambda ba, al, dt: _dn_gate_fn(ba, eb, ea, ebt, eat, al, dt), ba_ref[...], al_ref[...],
                         dt_ref[...])
        dba, dal, ddt = vjp((dbeta_ref[...], dg_ref[...]))
        dba_ref[...] = dba.astype(BF16)

        @pl.when(pl.program_id(0) == 0)
        def _():
            dal_ref[...] = jnp.zeros_like(dal_ref)
            ddt_ref[...] = jnp.zeros_like(ddt_ref)

        dal_ref[...] += dal
        ddt_ref[...] += ddt

    return pl.pallas_call(
        body, grid=(S // tm,),
        in_specs=_dn_gate_specs(tm) + [_row(tm, DN_WIDTH), _row(tm, DN_WIDTH), pl.BlockSpec(memory_space=pl.ANY)],
        out_specs=[_row(tm, LANES, P_BA // LANES), _full((1, DN_WIDTH)), _full((1, LANES))],
        out_shape=[jax.ShapeDtypeStruct(dproj.shape, dproj.dtype), jax.ShapeDtypeStruct((1, DN_WIDTH), F32),
                   jax.ShapeDtypeStruct((1, LANES), F32)],
        input_output_aliases={len(args) + 3: 0},
        name="dn_gate_bwd", compiler_params=_params(("arbitrary",)))(proj, *args, dbeta, dg, dproj)


PREP_GROUPS = 8
PREP_CHUNKS = GROUP * PREP_GROUPS


def _dn_prep_specs():
    rows = PREP_CHUNKS * CHUNK
    q = pl.BlockSpec((rows, LANES), lambda h, c: (c, h))
    k = pl.BlockSpec((rows, LANES), lambda h, c: (c, DN_HEADS + h))
    v = pl.BlockSpec((rows, LANES), lambda h, c: (c, 2 * DN_HEADS + h))
    qk = pl.BlockSpec((1, rows, CHUNK), lambda h, c: (h, c, 0))
    egl = pl.BlockSpec((1, PREP_CHUNKS, 1, LANES), lambda h, c: (h, c, 0, 0))
    return q, k, v, qk, egl


def _dn_prep_fwd(qkvn, g, beta):
    S = qkvn.shape[0]
    nc = S // CHUNK
    q, k, v, qks, egl = _dn_prep_specs()

    def body(q_ref, k_ref, v_ref, g_ref, b_ref, u_ref, w_ref, qe_ref, kd_ref, qk_ref, egl_ref, t_ref):
        rows = PREP_CHUNKS * CHUNK
        grp = (PREP_GROUPS, GROUP_ROWS, LANES)
        u, w, qe, kd, qk, e, t = _dn_group(q_ref[...].reshape(grp), k_ref[...].reshape(grp), v_ref[...].reshape(grp),
                                           g_ref[...].reshape(grp), b_ref[...].reshape(grp))
        u_ref[...] = u.reshape(rows, LANES)
        w_ref[...] = w.reshape(rows, LANES)
        qe_ref[...] = qe.reshape(rows, LANES)
        kd_ref[...] = kd.reshape(rows, LANES)
        t_ref[0] = t.reshape(rows, GROUP_ROWS).astype(BF16)
        qk_ref[0] = qk.reshape(rows, CHUNK)
        egl_ref[0] = e.reshape(PREP_CHUNKS, 1, LANES)

    wide = jax.ShapeDtypeStruct((S, DN_WIDTH), F32)
    return pl.pallas_call(
        body, grid=(DN_HEADS, nc // PREP_CHUNKS), in_specs=[q, k, v, q, q],
        out_specs=[q, q, q, q, qks, egl, _dn_tinv_spec()],
        out_shape=[wide, wide, wide, wide, jax.ShapeDtypeStruct((DN_HEADS, S, CHUNK), F32),
                   jax.ShapeDtypeStruct((DN_HEADS, nc, 1, LANES), F32),
                   jax.ShapeDtypeStruct((DN_HEADS, S, GROUP_ROWS), BF16)],
        name="dn_prep_fwd", compiler_params=_params(("parallel", "parallel")))(qkvn, qkvn, qkvn, g, beta)


def _dn_tinv_spec():
    return pl.BlockSpec((1, PREP_CHUNKS * CHUNK, GROUP_ROWS), lambda h, c: (h, c, 0))


def _dn_prep_bwd(qkvn, g, beta, tinv, du, dw, dqe, dkd, dqk, degl):
    S = qkvn.shape[0]
    nc = S // CHUNK
    q, k, v, qks, egl = _dn_prep_specs()

    def body(q_ref, k_ref, v_ref, g_ref, b_ref, t_ref, du_ref, dw_ref, dqe_ref, dkd_ref, dqk_ref, degl_ref,
             dqkv_ref, dg_ref, db_ref):
        rows = PREP_CHUNKS * CHUNK
        grp = (PREP_GROUPS, GROUP_ROWS, LANES)
        t_saved = t_ref[0].reshape(PREP_GROUPS, GROUP_ROWS, GROUP_ROWS)
        _, vjp = jax.vjp(lambda *x: _dn_group(*x, t_saved=t_saved)[:6], q_ref[...].reshape(grp),
                         k_ref[...].reshape(grp), v_ref[...].reshape(grp), g_ref[...].reshape(grp),
                         b_ref[...].reshape(grp))
        dq, dk, dv, dg, db = vjp((du_ref[...].reshape(grp), dw_ref[...].reshape(grp), dqe_ref[...].reshape(grp),
                                  dkd_ref[...].reshape(grp), dqk_ref[0].reshape(PREP_GROUPS, GROUP_ROWS, CHUNK),
                                  degl_ref[0].reshape(PREP_GROUPS, GROUP, 1, LANES)))
        dqkv_ref[0] = dq.reshape(rows, LANES)
        dqkv_ref[1] = dk.reshape(rows, LANES)
        dqkv_ref[2] = dv.reshape(rows, LANES)
        dg_ref[...] = dg.reshape(rows, LANES)
        db_ref[...] = db.reshape(rows, LANES)

    wide = jax.ShapeDtypeStruct((S, DN_WIDTH), F32)
    rows = PREP_CHUNKS * CHUNK
    return pl.pallas_call(
        body, grid=(DN_HEADS, nc // PREP_CHUNKS), in_specs=[q, k, v, q, q, _dn_tinv_spec(), q, q, q, q, qks, egl],
        out_specs=[pl.BlockSpec((3, rows, LANES), lambda h, c: (0, c, h)), q, q],
        out_shape=[jax.ShapeDtypeStruct((3, S, DN_WIDTH), F32), wide, wide],
        name="dn_prep_bwd", compiler_params=_params(("parallel", "parallel")),
    )(qkvn, qkvn, qkvn, g, beta, tinv, du, dw, dqe, dkd, dqk, degl)


SCAN_CHUNKS = 16


def _dn_scan_specs(nc, reverse):
    nb = nc // SCAN_CHUNKS

    def cidx(c):
        return nb - 1 - c if reverse else c

    hc = pl.BlockSpec((SCAN_CHUNKS * CHUNK, DN_WIDTH), lambda c: (cidx(c), 0))
    qk = pl.BlockSpec((DN_HEADS, SCAN_CHUNKS * CHUNK, CHUNK), lambda c: (0, cidx(c), 0))
    egl = pl.BlockSpec((DN_HEADS, SCAN_CHUNKS, 1, LANES), lambda c: (0, cidx(c), 0, 0))
    st = pl.BlockSpec((DN_HEADS, SCAN_CHUNKS, DN_DIM, DN_DIM), lambda c: (0, cidx(c), 0, 0))
    return hc, qk, egl, st


def _heads(ref, i):
    return jnp.stack([ref[pl.ds(i * CHUNK, CHUNK), pl.ds(h * DN_DIM, DN_DIM)] for h in range(DN_HEADS)])


def _dn_scan_fwd(u, w, qe, kd, qk, egl):
    S = u.shape[0]
    nc = S // CHUNK
    hc, qks, egls, st = _dn_scan_specs(nc, False)

    def body(u_ref, w_ref, qe_ref, kd_ref, qk_ref, egl_ref, o_ref, st_ref, s_scr):
        @pl.when(pl.program_id(0) == 0)
        def _():
            s_scr[...] = jnp.zeros_like(s_scr)

        s = s_scr[...]
        for i in range(SCAN_CHUNKS):
            rows = pl.ds(i * CHUNK, CHUNK)
            st_ref[:, i] = s
            s, o = _dn_step(s, _heads(u_ref, i), _heads(w_ref, i), _heads(qe_ref, i), _heads(kd_ref, i),
                            qk_ref[:, rows, :], egl_ref[:, i])
            for h in range(DN_HEADS):
                o_ref[rows, pl.ds(h * DN_DIM, DN_DIM)] = o[h]
        s_scr[...] = s

    return pl.pallas_call(
        body, grid=(nc // SCAN_CHUNKS,), in_specs=[hc, hc, hc, hc, qks, egls], out_specs=[hc, st],
        out_shape=[jax.ShapeDtypeStruct((S, DN_WIDTH), F32), jax.ShapeDtypeStruct((DN_HEADS, nc, DN_DIM, DN_DIM), F32)],
        scratch_shapes=[pltpu.VMEM((DN_HEADS, DN_DIM, DN_DIM), F32)], name="dn_scan_fwd",
        compiler_params=_params(("arbitrary",)))(u, w, qe, kd, qk, egl)


def _dn_scan_bwd(u, w, qe, kd, qk, egl, states, do):
    S = u.shape[0]
    nc = S // CHUNK
    hc, qks, egls, st = _dn_scan_specs(nc, True)

    def body(u_ref, w_ref, qe_ref, kd_ref, qk_ref, egl_ref, st_ref, do_ref,
             du_ref, dw_ref, dqe_ref, dkd_ref, dqk_ref, degl_ref, ds_scr):
        @pl.when(pl.program_id(0) == 0)
        def _():
            ds_scr[...] = jnp.zeros_like(ds_scr)

        ds = ds_scr[...]
        for i in reversed(range(SCAN_CHUNKS)):
            rows = pl.ds(i * CHUNK, CHUNK)
            _, vjp = jax.vjp(_dn_step, st_ref[:, i], _heads(u_ref, i), _heads(w_ref, i), _heads(qe_ref, i),
                             _heads(kd_ref, i), qk_ref[:, rows, :], egl_ref[:, i])
            ds, du, dw, dqe, dkd, dqk, degl = vjp((ds, _heads(do_ref, i)))
            dqk_ref[:, rows, :] = dqk
            degl_ref[:, i] = degl
            for h in range(DN_HEADS):
                cols = pl.ds(h * DN_DIM, DN_DIM)
                du_ref[rows, cols] = du[h]
                dw_ref[rows, cols] = dw[h]
                dqe_ref[rows, cols] = dqe[h]
                dkd_ref[rows, cols] = dkd[h]
        ds_scr[...] = ds

    wide = jax.ShapeDtypeStruct((S, DN_WIDTH), F32)
    return pl.pallas_call(
        body, grid=(nc // SCAN_CHUNKS,), in_specs=[hc, hc, hc, hc, qks, egls, st, hc],
        out_specs=[hc, hc, hc, hc, qks, egls],
        out_shape=[wide, wide, wide, wide, jax.ShapeDtypeStruct((DN_HEADS, S, CHUNK), F32),
                   jax.ShapeDtypeStruct((DN_HEADS, nc, 1, LANES), F32)],
        scratch_shapes=[pltpu.VMEM((DN_HEADS, DN_DIM, DN_DIM), F32)], name="dn_scan_bwd",
        compiler_params=_params(("arbitrary",)))(u, w, qe, kd, qk, egl, states, do)


def _dn_out_fwd(o, proj, gain, tm=1024):
    S = o.shape[0]

    def body(o_ref, z_ref, g_ref, y_ref):
        y_ref[...] = _dn_out(o_ref[...], z_ref[...], g_ref[...]).astype(BF16)

    hs = pl.BlockSpec((tm, LANES), lambda i, h: (i, h))
    zs = pl.BlockSpec((tm, LANES), lambda i, h: (i, P_Z // LANES + h))
    return pl.pallas_call(
        body, grid=(S // tm, DN_HEADS), in_specs=[hs, zs, _full((1, DN_DIM))], out_specs=hs,
        out_shape=jax.ShapeDtypeStruct((S, DN_WIDTH), BF16), name="dn_out_fwd",
        compiler_params=_params(("parallel", "parallel")))(o, proj, gain)


_ANY = pl.BlockSpec(memory_space=pl.ANY)


def _dn_out_bwd(o, proj, gain, dy, dproj, tm=1024):
    S = o.shape[0]

    def body(o_ref, z_ref, g_ref, dy_ref, _, do_ref, dz_ref, dg_ref):
        _, vjp = jax.vjp(_dn_out, o_ref[...], z_ref[...], g_ref[...])
        do, dz, dg = vjp(dy_ref[...])
        do_ref[...] = do
        dz_ref[...] = dz.astype(BF16)

        @pl.when((pl.program_id(0) == 0) & (pl.program_id(1) == 0))
        def _():
            dg_ref[...] = jnp.zeros_like(dg_ref)

        dg_ref[...] += dg

    hs = pl.BlockSpec((tm, LANES), lambda i, h: (i, h))
    zs = pl.BlockSpec((tm, LANES), lambda i, h: (i, P_Z // LANES + h))
    return pl.pallas_call(
        body, grid=(S // tm, DN_HEADS), in_specs=[hs, zs, _full((1, DN_DIM)), hs, _ANY],
        out_specs=[hs, zs, _full((1, DN_DIM))],
        out_shape=[jax.ShapeDtypeStruct((S, DN_WIDTH), F32), jax.ShapeDtypeStruct(dproj.shape, dproj.dtype),
                   jax.ShapeDtypeStruct((1, DN_DIM), F32)],
        input_output_aliases={4: 1},
        name="dn_out_bwd", compiler_params=_params(("arbitrary", "arbitrary")))(o, proj, gain, dy, dproj)


def _rel_buckets():
    qi = np.arange(BLOCK)[:, None]
    kj = np.arange(2 * BLOCK)[None, :]
    n = np.maximum(BLOCK + qi - kj, 0)
    max_exact = REL_BUCKETS // 2
    nf = np.maximum(n, 1).astype(np.float32)
    large = max_exact + (np.log(nf / np.float32(max_exact)) / np.float32(math.log(REL_MAX_DIST / max_exact))
                         * np.float32(REL_BUCKETS - max_exact)).astype(np.int32)
    large = np.minimum(large, REL_BUCKETS - 1)
    return np.where(n < max_exact, n, large).astype(np.int32)


def _bias_fwd(rel_bias):
    buckets = jnp.asarray(_rel_buckets())

    def body(rb_ref, bk_ref, o_ref):
        bk = bk_ref[...]
        for h in range(SWA_HEADS):
            acc = jnp.zeros((BLOCK, 2 * BLOCK), F32)
            for b in range(REL_BUCKETS):
                acc = jnp.where(bk == b, rb_ref[b, h], acc)
            for first in range(2):
                o_ref[first, h] = jnp.where(_swa_mask(1 - first), acc, -jnp.inf)

    return pl.pallas_call(
        body, in_specs=[pl.BlockSpec(memory_space=pltpu.SMEM), pl.BlockSpec(memory_space=pltpu.VMEM)],
        out_specs=pl.BlockSpec(memory_space=pltpu.VMEM),
        out_shape=jax.ShapeDtypeStruct((2, SWA_HEADS, BLOCK, 2 * BLOCK), F32), name="swa_bias_fwd",
        compiler_params=_params())(rel_bias, buckets)


def _bias_bwd(dbias):
    buckets = jnp.asarray(_rel_buckets())

    def body(d_ref, bk_ref, o_ref):
        bk = bk_ref[...]
        lane = lax.broadcasted_iota(jnp.int32, (1, LANES), 1)
        for h in range(SWA_HEADS):
            d = d_ref[h]
            row = jnp.zeros((1, LANES), F32)
            for b in range(REL_BUCKETS):
                part = jnp.sum(jnp.where(bk == b, d, 0.0), axis=1, keepdims=True)
                row = jnp.where(lane == b, jnp.sum(part, axis=0, keepdims=True), row)
            o_ref[h:h + 1, :] = row

    return pl.pallas_call(
        body, in_specs=[pl.BlockSpec(memory_space=pltpu.VMEM), pl.BlockSpec(memory_space=pltpu.VMEM)],
        out_specs=pl.BlockSpec(memory_space=pltpu.VMEM),
        out_shape=jax.ShapeDtypeStruct((SWA_HEADS, LANES), F32), name="swa_bias_bwd",
        compiler_params=_params())(dbias, buckets)


def _swa_mask(n):
    qi = lax.broadcasted_iota(jnp.int32, (BLOCK, 2 * BLOCK), 0)
    kj = lax.broadcasted_iota(jnp.int32, (BLOCK, 2 * BLOCK), 1)
    dist = BLOCK + qi - kj
    return (dist >= 0) & (dist < WINDOW) & ((n > 0) | (kj >= BLOCK))


def _swa_in_specs():
    q = pl.BlockSpec((BLOCK, SWA_WIDTH), lambda n: (n, P_SQ // SWA_WIDTH))
    kc = pl.BlockSpec((BLOCK, SWA_KVW), lambda n: (n, P_SK // SWA_KVW))
    kp = pl.BlockSpec((BLOCK, SWA_KVW), lambda n: (jnp.maximum(n - 1, 0), P_SK // SWA_KVW))
    vc = pl.BlockSpec((BLOCK, SWA_KVW), lambda n: (n, P_SV // SWA_KVW))
    vp = pl.BlockSpec((BLOCK, SWA_KVW), lambda n: (jnp.maximum(n - 1, 0), P_SV // SWA_KVW))
    band = pl.BlockSpec((None, SWA_HEADS, BLOCK, 2 * BLOCK), lambda n: (jnp.where(n == 0, 1, 0), 0, 0, 0))
    small = [_full((1, SWA_DIM)), _full((1, SWA_DIM)), _full((1, SWA_HEADS)), band]
    return [q, kp, kc, vp, vc] + small


def _swa_load(q_ref, kp_ref, kc_ref, vp_ref, vc_ref, s_ref):
    q = jnp.stack([q_ref[:, pl.ds(h * SWA_DIM, SWA_DIM)] for h in range(SWA_HEADS)])
    kbands, vbands = [], []
    for kv in range(SWA_KV):
        cols = pl.ds(kv * SWA_DIM, SWA_DIM)
        kbands += [jnp.concatenate([kp_ref[:, cols], kc_ref[:, cols]], axis=0)] * SWA_GROUP
        vbands += [jnp.concatenate([vp_ref[:, cols], vc_ref[:, cols]], axis=0)] * SWA_GROUP
    sinks = jnp.stack([s_ref[:, pl.ds(h, 1)] for h in range(SWA_HEADS)])
    return q, jnp.stack(kbands), jnp.stack(vbands), sinks


def _swa_fwd(proj, q_gain, k_gain, sinks, bias):
    S = proj.shape[0]

    def body(q_ref, kp_ref, kc_ref, vp_ref, vc_ref, qg_ref, kg_ref, s_ref, bias_ref, y_ref):
        q, kband, vband, sk = _swa_load(q_ref, kp_ref, kc_ref, vp_ref, vc_ref, s_ref)
        out = _swa_block(q, kband, vband, qg_ref[...], kg_ref[...], sk, bias_ref[...])
        for h in range(SWA_HEADS):
            y_ref[:, pl.ds(h * SWA_DIM, SWA_DIM)] = out[h].astype(BF16)

    return pl.pallas_call(
        body, grid=(S // BLOCK,), in_specs=_swa_in_specs(),
        out_specs=pl.BlockSpec((BLOCK, SWA_WIDTH), lambda n: (n, 0)),
        out_shape=jax.ShapeDtypeStruct((S, SWA_WIDTH), BF16), name="swa_fwd",
        compiler_params=_params(("parallel",)))(proj, proj, proj, proj, proj, q_gain, k_gain, sinks, bias)


def _swa_bwd(proj, q_gain, k_gain, sinks, bias, dy, dproj):
    S = proj.shape[0]

    def body(q_ref, kp_ref, kc_ref, vp_ref, vc_ref, qg_ref, kg_ref, s_ref, bias_ref, dy_ref, _,
             dq_ref, dk_ref, dv_ref, dqg_ref, dkg_ref, ds_ref, dbias_ref):
        n = pl.program_id(0)

        @pl.when(n == 0)
        def _():
            for r in (dk_ref, dv_ref, dqg_ref, dkg_ref, ds_ref, dbias_ref):
                r[...] = jnp.zeros_like(r)

        cur = pl.ds(pl.multiple_of(n * BLOCK, BLOCK), BLOCK)
        prev = pl.ds(pl.multiple_of(jnp.maximum(n - 1, 0) * BLOCK, BLOCK), BLOCK)
        q, kband, vband, sk = _swa_load(q_ref, kp_ref, kc_ref, vp_ref, vc_ref, s_ref)
        _, vjp = jax.vjp(_swa_block, q, kband, vband, qg_ref[...], kg_ref[...], sk, bias_ref[...])
        dy = jnp.stack([dy_ref[:, pl.ds(h * SWA_DIM, SWA_DIM)] for h in range(SWA_HEADS)])
        dq, dkb, dvb, dqg, dkg, dsk, dbs = vjp(dy)
        for h in range(SWA_HEADS):
            dq_ref[:, pl.ds(h * SWA_DIM, SWA_DIM)] = dq[h].astype(BF16)
            ds_ref[:, pl.ds(h, 1)] += dsk[h]
        dbias_ref[...] += dbs
        dqg_ref[...] += dqg
        dkg_ref[...] += dkg
        for kv in range(SWA_KV):
            cols = pl.ds(kv * SWA_DIM, SWA_DIM)
            group = range(kv * SWA_GROUP, (kv + 1) * SWA_GROUP)
            dk_kv = sum(dkb[h] for h in group)
            dv_kv = sum(dvb[h] for h in group)
            dk_ref[cur, cols] += dk_kv[BLOCK:]
            dv_ref[cur, cols] += dv_kv[BLOCK:]

            @pl.when(n > 0)
            def _(cols=cols, dk_kv=dk_kv, dv_kv=dv_kv):
                dk_ref[prev, cols] += dk_kv[:BLOCK]
                dv_ref[prev, cols] += dv_kv[:BLOCK]

    return pl.pallas_call(
        body, grid=(S // BLOCK,),
        in_specs=_swa_in_specs() + [pl.BlockSpec((BLOCK, SWA_WIDTH), lambda n: (n, 0)),
                                    pl.BlockSpec(memory_space=pl.ANY)],
        out_specs=[pl.BlockSpec((BLOCK, SWA_WIDTH), lambda n: (n, P_SQ // SWA_WIDTH)), _full((S, SWA_KVW)),
                   _full((S, SWA_KVW)), _full((1, SWA_DIM)), _full((1, SWA_DIM)), _full((1, SWA_HEADS)),
                   _full((SWA_HEADS, BLOCK, 2 * BLOCK))],
        out_shape=[jax.ShapeDtypeStruct(dproj.shape, dproj.dtype), jax.ShapeDtypeStruct((S, SWA_KVW), F32),
                   jax.ShapeDtypeStruct((S, SWA_KVW), F32), jax.ShapeDtypeStruct((1, SWA_DIM), F32),
                   jax.ShapeDtypeStruct((1, SWA_DIM), F32), jax.ShapeDtypeStruct((1, SWA_HEADS), F32),
                   jax.ShapeDtypeStruct((SWA_HEADS, BLOCK, 2 * BLOCK), F32)],
        input_output_aliases={10: 0},
        name="swa_bwd", compiler_params=_params(("arbitrary",)),
    )(proj, proj, proj, proj, proj, q_gain, k_gain, sinks, bias, dy, dproj)


def _kv_into(dproj, dk, dv, tm=1024):
    S = dk.shape[0]

    def body(dk_ref, dv_ref, _, o_ref):
        o_ref[:, :SWA_KVW] = dk_ref[...].astype(BF16)
        o_ref[:, SWA_KVW:] = dv_ref[...].astype(BF16)

    return pl.pallas_call(
        body, grid=(S // tm,), in_specs=[_row(tm, SWA_KVW), _row(tm, SWA_KVW), pl.BlockSpec(memory_space=pl.ANY)],
        out_specs=_row(tm, 2 * SWA_KVW, P_SK // (2 * SWA_KVW)),
        out_shape=jax.ShapeDtypeStruct(dproj.shape, dproj.dtype), input_output_aliases={2: 0},
        name="swa_kv_into", compiler_params=_params(("parallel",)))(dk, dv, dproj)


def _position():
    return lax.axis_index("x"), lax.axis_index("y"), lax.axis_index("c")


def _all_gather(shards, name="all_gather_weights"):
    na = len(shards)

    def body(*refs):
        x_refs, out_refs = refs[:na], refs[na:2 * na]
        send_sems, recv_sems, local_sems = refs[2 * na:]
        x, y, c = _position()
        me, sibling = (x, y, c), (x, y, 1 - c)
        chips = [(1 - x, y), (x, 1 - y), (1 - x, 1 - y)]

        def copy(a, k, block, to, own=False):
            px, py, pc = block
            slot = out_refs[a].at[4 * px + 2 * py + pc]
            return pltpu.make_async_remote_copy(
                src_ref=x_refs[a] if own else slot, dst_ref=slot, send_sem=send_sems.at[7 * a + k],
                recv_sem=recv_sems.at[7 * a + k], device_id=to, device_id_type=MESH_ID)

        mine = [pltpu.make_async_copy(x_refs[a], out_refs[a].at[4 * x + 2 * y + c], local_sems.at[a])
                for a in range(na)]
        for cp in mine:
            cp.start()
        first = []
        for a in range(na):
            first.append(copy(a, 0, me, sibling, own=True))
            first += [copy(a, 1 + j, me, (*chip, c), own=True) for j, chip in enumerate(chips)]
        for cp in first:
            cp.start()
        passed = []
        for j, chip in enumerate(chips):
            for a in range(na):
                copy(a, 1 + j, (*chip, c), me).wait_recv()
                passed.append(copy(a, 4 + j, (*chip, c), sibling))
                passed[-1].start()
        for a in range(na):
            copy(a, 0, sibling, me).wait_recv()
            for j, chip in enumerate(chips):
                copy(a, 4 + j, (*chip, 1 - c), me).wait_recv()
        for cp in first + passed:
            cp.wait_send()
        for cp in mine:
            cp.wait()

    return pl.pallas_call(
        body, in_specs=[pl.BlockSpec(memory_space=pl.ANY)] * na, out_specs=[pl.BlockSpec(memory_space=pl.ANY)] * na,
        out_shape=[jax.ShapeDtypeStruct((N_DEV,) + s.shape, s.dtype) for s in shards],
        scratch_shapes=[pltpu.SemaphoreType.DMA((7 * na,)), pltpu.SemaphoreType.DMA((7 * na,)),
                        pltpu.SemaphoreType.DMA((na,))],
        name=name)(*shards)


_HBM = pl.BlockSpec(memory_space=pltpu.HBM)
_SEM = pl.BlockSpec(memory_space=pltpu.SEMAPHORE)
_DATAFLOW = pltpu.SideEffectType.DATAFLOW_SIDE_EFFECTING


def _peers(x, y, c):
    out = []
    for k in range(1, N_DEV):
        px, py, pc = x ^ (k >> 2), y ^ ((k >> 1) & 1), c ^ (k & 1)
        out.append(((px, py, pc), 4 * px + 2 * py + pc))
    return out


def _split_copies(src_refs, land_refs, send_sems, recv_sems, scatter):
    x, y, c = _position()
    me = 4 * x + 2 * y + c
    sends, recvs = [], []
    for k, (peer_id, peer) in enumerate(_peers(x, y, c)):
        for a, (src, land) in enumerate(zip(src_refs, land_refs)):
            sems = dict(send_sem=send_sems.at[7 * a + k], recv_sem=recv_sems.at[7 * a + k],
                        device_id=peer_id, device_id_type=MESH_ID)
            mine = src.at[peer] if scatter else src
            sends.append(pltpu.make_async_remote_copy(src_ref=mine, dst_ref=land.at[me], **sems))
            recvs.append(pltpu.make_async_remote_copy(src_ref=mine, dst_ref=land.at[peer], **sems))
    return sends, recvs


def _all_gather_direct(shards, name, after):
    na, nb = len(shards), len(after)

    def body(*refs):
        x_refs, out_refs = refs[:na], refs[na + nb:2 * na + nb]
        send_sems, recv_sems, local_sems = refs[2 * na + nb:]
        x, y, c = _position()
        me = 4 * x + 2 * y + c
        local = [pltpu.make_async_copy(x_refs[a], out_refs[a].at[me], local_sems.at[a]) for a in range(na)]
        sends, recvs = _split_copies(x_refs, out_refs, send_sems, recv_sems, False)
        for cp in local + sends:
            cp.start()
        for cp in recvs:
            cp.wait_recv()
        for cp in sends:
            cp.wait_send()
        for cp in local:
            cp.wait()

    return pl.pallas_call(
        body, in_specs=[pl.BlockSpec(memory_space=pl.ANY)] * (na + nb),
        out_specs=[pl.BlockSpec(memory_space=pl.ANY)] * na,
        out_shape=[jax.ShapeDtypeStruct((N_DEV,) + s.shape, s.dtype) for s in shards],
        scratch_shapes=[pltpu.SemaphoreType.DMA((7 * na,)), pltpu.SemaphoreType.DMA((7 * na,)),
                        pltpu.SemaphoreType.DMA((na,))],
        name=name)(*shards, *after)


def _exchange_start(srcs, scatter, name, after=None):
    na = len(srcs)
    lands = [lax.empty(s.shape if scatter else (N_DEV,) + s.shape, s.dtype) for s in srcs]
    extra = [] if after is None else [after]

    def body(*refs):
        src_refs, land_refs = refs[:na], refs[na:2 * na]
        send_sems, recv_sems = refs[2 * na + len(extra)], refs[2 * na + len(extra) + 1]
        token = refs[-1]
        sends, _ = _split_copies(src_refs, land_refs, send_sems, recv_sems, scatter)
        for cp in sends:
            cp.start()
        token[...] = jnp.zeros_like(token)

    hbm = lambda a: pltpu.HBM(a.shape, a.dtype)
    out = pl.pallas_call(
        body, name=name,
        out_shape=(pltpu.SemaphoreType.DMA((7 * na,)), pltpu.SemaphoreType.DMA((7 * na,)),
                   *[hbm(s) for s in srcs], *[hbm(l) for l in lands], jax.ShapeDtypeStruct((8, LANES), F32)),
        in_specs=[_HBM] * (2 * na) + [pl.BlockSpec(memory_space=pl.ANY)] * len(extra),
        out_specs=(_SEM, _SEM, *[_HBM] * (2 * na), pl.BlockSpec(memory_space=pltpu.VMEM)),
        input_output_aliases={i: 2 + i for i in range(2 * na)},
        compiler_params=pltpu.CompilerParams(has_side_effects=_DATAFLOW),
    )(*[pltpu.with_memory_space_constraint(s, pltpu.HBM) for s in srcs],
      *[pltpu.with_memory_space_constraint(l, pltpu.HBM) for l in lands], *extra)
    return (out[0], out[1], list(out[2:2 + na]), list(out[2 + na:2 + 2 * na])), out[-1]


def _exchange_wait(handle, after, scatter, name):
    send_sems, recv_sems, srcs, lands = handle
    na = len(srcs)

    def body(*refs):
        src_refs, land_refs = refs[:na], refs[na:2 * na]
        s_sems, r_sems = refs[2 * na], refs[2 * na + 1]
        sends, recvs = _split_copies(src_refs, land_refs, s_sems, r_sems, scatter)
        for cp in sends:
            cp.wait_send()
        for cp in recvs:
            cp.wait_recv()

    hbm = lambda a: pltpu.HBM(a.shape, a.dtype)
    out = pl.pallas_call(
        body, name=name, out_shape=(*[hbm(s) for s in srcs], *[hbm(l) for l in lands]),
        in_specs=[_HBM] * (2 * na) + [_SEM, _SEM, pl.BlockSpec(memory_space=pl.ANY)],
        out_specs=tuple([_HBM] * (2 * na)), input_output_aliases={i: i for i in range(2 * na)},
        compiler_params=pltpu.CompilerParams(has_side_effects=_DATAFLOW),
    )(*srcs, *lands, send_sems, recv_sems, after)
    return list(out[:na]), list(out[na:])


def _own_slot(landed, own):
    me = 4 * lax.axis_index("x") + 2 * lax.axis_index("y") + lax.axis_index("c")
    return lax.dynamic_update_slice_in_dim(landed, own[None], me, axis=0)


def _adam_update(parts, w, m, v, name, tr=256, turned=False):
    _, r, c = w.shape
    tr = _pick_rows(r, tr)
    cp = parts.shape[2]

    def body(p_ref, w_ref, m_ref, v_ref, g_ref, d_ref, nm_ref, nv_ref):
        cols = pl.ds(0, cp if turned else c)
        g = p_ref[0, :, cols].astype(F32)
        for i in range(1, N_DEV):
            g = g + p_ref[i, :, cols].astype(F32)
        if turned:
            g = g.T[:c]
        delta, nm, nv = _adamw(w_ref[0], g, m_ref[0], v_ref[0])
        g_ref[0] = g
        d_ref[0] = delta
        nm_ref[0] = nm
        nv_ref[0] = nv

    if turned:
        w, m, v = (jnp.transpose(a, (0, 2, 1)) for a in (w, m, v))
        rs = pl.BlockSpec((1, c, tr), lambda i: (0, 0, i))
    else:
        rs = pl.BlockSpec((1, tr, c), lambda i: (0, i, 0))
    outs = pl.pallas_call(
        body, grid=(r // tr,), in_specs=[pl.BlockSpec((N_DEV, tr, cp), lambda i: (0, i, 0)), rs, rs, rs],
        out_specs=[rs] * 4, out_shape=[jax.ShapeDtypeStruct(w.shape, F32)] * 4, name=name,
        compiler_params=_params(("parallel",)))(parts, w, m, v)
    return [*([jnp.transpose(o, (0, 2, 1)) for o in outs] if turned else outs), outs[0]]


def _pick_rows(rows, target):
    if rows <= target:
        return rows
    t = target
    while t >= 16:
        if rows % t == 0:
            return t
        t -= 16
    return rows


BIG = ("w_in", "w_branch_dn", "w_branch_swa", "w_out", "w_gate", "w_up", "w_down")
IN_SHARD, IN_WIRE = D_IN // N_DEV, 640
FF_SHARD, FF_WIRE = D_FF // N_DEV, 384
D_FFP = N_DEV * FF_WIRE
BIG_SHAPES = {"w_in": ((D_MODEL, IN_SHARD), (D_MODEL, IN_WIRE)),
              "w_branch_dn": ((DN_WIDTH, LANES), (DN_WIDTH, LANES)),
              "w_branch_swa": ((SWA_WIDTH, LANES), (SWA_WIDTH, LANES)),
              "w_out": ((LANES, D_MODEL), (LANES, D_MODEL)),
              "w_gate": ((D_MODEL, FF_SHARD), (D_MODEL, FF_WIRE)),
              "w_up": ((D_MODEL, FF_SHARD), (D_MODEL, FF_WIRE)),
              "w_down": ((FF_SHARD, D_MODEL), (FF_WIRE, D_MODEL))}
CONV_SHARD, CONV_WIRE = (DN_CONV, DN_QKV // N_DEV), (8, 256)


def _pad_to(a, shape):
    return jnp.pad(a, [(0, t - s) for s, t in zip(a.shape, shape)])


IN_TILE_ROWS = 256
_IN_SEGS = ((R_GATE, 2048, P_GATE), (R_QKV, DN_QKV, P_QKV), (R_Z, DN_WIDTH, P_Z), (R_SQ, SWA_WIDTH, P_SQ),
            (R_SK, SWA_KVW, P_SK), (R_SV, SWA_KVW, P_SV), (R_B, 8, P_BA))


def _w_in_from_blocks(blocks):
    tm = IN_TILE_ROWS

    def body(b_ref, o_ref):
        parts = []
        for rs, n, _ in _IN_SEGS:
            for dev in range(N_DEV):
                lo, hi = max(rs, IN_SHARD * dev), min(rs + n, IN_SHARD * (dev + 1))
                if lo < hi:
                    parts.append(b_ref[dev][:, lo - IN_SHARD * dev:hi - IN_SHARD * dev])
        parts.append(jnp.zeros((tm, P_WIDTH - P_BA - 8), b_ref.dtype))
        o_ref[...] = jnp.concatenate(parts, axis=1)

    return pl.pallas_call(
        body, grid=(D_MODEL // tm,), in_specs=[pl.BlockSpec((N_DEV, tm, IN_WIRE), lambda i: (0, i, 0))],
        out_specs=pl.BlockSpec((tm, P_WIDTH), lambda i: (i, 0)),
        out_shape=jax.ShapeDtypeStruct((D_MODEL, P_WIDTH), blocks.dtype), name="w_in_from_blocks",
        compiler_params=_params(("parallel",)))(blocks)


def _w_in_to_blocks(g):
    tm = IN_TILE_ROWS

    def body(g_ref, o_ref):
        for dev in range(N_DEV):
            parts = []
            for rs, n, ps in sorted(_IN_SEGS):
                lo, hi = max(rs, IN_SHARD * dev), min(rs + n, IN_SHARD * (dev + 1))
                if lo < hi:
                    parts.append(g_ref[:, ps + lo - rs:ps + hi - rs])
            parts.append(jnp.zeros((tm, IN_WIRE - IN_SHARD), g_ref.dtype))
            o_ref[dev] = jnp.concatenate(parts, axis=1)

    return pl.pallas_call(
        body, grid=(D_MODEL // tm,), in_specs=[pl.BlockSpec((tm, P_WIDTH), lambda i: (i, 0))],
        out_specs=pl.BlockSpec((N_DEV, tm, IN_WIRE), lambda i: (0, i, 0)),
        out_shape=jax.ShapeDtypeStruct((N_DEV, D_MODEL, IN_WIRE), g.dtype), name="w_in_to_blocks",
        compiler_params=_params(("parallel",)))(g)


SMALL = {"attn_norm": (0, (1, D_MODEL)), "ffn_norm": (1, (1, D_MODEL)), "dn_out_norm": (2, (1, DN_DIM)),
         "swa_q_norm": (3, (1, SWA_DIM)), "swa_k_norm": (4, (1, SWA_DIM)), "dn_a_log": (5, (1, DN_HEADS)),
         "dn_dt_bias": (6, (1, DN_HEADS)), "swa_sinks": (7, (1, SWA_HEADS)), "rel_bias": (8, (REL_BUCKETS, SWA_HEADS))}
SMALL_SHEET = (48, D_MODEL)


LOSS_ROW = 40


def _small_pack(grads, loss_local):
    names = list(SMALL)

    def body(*refs):
        o_ref = refs[-1]
        o_ref[...] = jnp.zeros_like(o_ref)
        for n, ref in zip(names, refs):
            r0, (nr, nc) = SMALL[n]
            o_ref[r0:r0 + nr, 0:nc] = ref[...]
        o_ref[LOSS_ROW:LOSS_ROW + 1, 0:1] = refs[len(names)][...]

    return pl.pallas_call(
        body, in_specs=[pl.BlockSpec(memory_space=pltpu.VMEM)] * (len(names) + 1),
        out_specs=pl.BlockSpec(memory_space=pltpu.VMEM), out_shape=jax.ShapeDtypeStruct(SMALL_SHEET, F32),
        name="small_pack", compiler_params=_params())(*[grads[n].reshape(SMALL[n][1]) for n in names], loss_local)


def _small_update(sheets, w, m, v):
    names = list(SMALL)
    k = len(names)

    def body(*refs):
        p_ref = refs[0]
        ins, outs = refs[1:1 + 3 * k], refs[1 + 3 * k:]
        loss = p_ref[0, LOSS_ROW:LOSS_ROW + 1, 0:1]
        for i in range(1, N_DEV):
            loss = loss + p_ref[i, LOSS_ROW:LOSS_ROW + 1, 0:1]
        outs[4 * k][...] = loss
        for t, n in enumerate(names):
            r0, (nr, nc) = SMALL[n]
            g = p_ref[0, r0:r0 + nr, 0:nc]
            for i in range(1, N_DEV):
                g = g + p_ref[i, r0:r0 + nr, 0:nc]
            delta, nm, nv = _adamw(ins[t][...], g, ins[k + t][...], ins[2 * k + t][...])
            for kind, val in enumerate((g, delta, nm, nv)):
                outs[kind * k + t][...] = val

    shapes = [jax.ShapeDtypeStruct(SMALL[n][1], F32) for n in names]
    vm = pl.BlockSpec(memory_space=pltpu.VMEM)
    res = pl.pallas_call(
        body, in_specs=[vm] * (1 + 3 * k), out_specs=[vm] * (4 * k + 1),
        out_shape=shapes * 4 + [jax.ShapeDtypeStruct((1, 1), F32)], name="adam_small", compiler_params=_params(),
    )(sheets, *[d[n].reshape(SMALL[n][1]) for d in (w, m, v) for n in names])
    return {n: tuple(res[kind * k + t] for kind in range(4)) for t, n in enumerate(names)}, res[4 * k]


def kernel(x, attn_norm, w_in, dn_conv, dn_a_log, dn_dt_bias, dn_out_norm, swa_q_norm, swa_k_norm, swa_sinks, rel_bias, w_branch_dn, w_branch_swa, w_out, ffn_norm, w_gate, w_up, w_down, loss_target, m_attn_norm, m_w_in, m_dn_conv, m_dn_a_log, m_dn_dt_bias, m_dn_out_norm, m_swa_q_norm, m_swa_k_norm, m_swa_sinks, m_rel_bias, m_w_branch_dn, m_w_branch_swa, m_w_out, m_ffn_norm, m_w_gate, m_w_up, m_w_down, v_attn_norm, v_w_in, v_dn_conv, v_dn_a_log, v_dn_dt_bias, v_dn_out_norm, v_swa_q_norm, v_swa_k_norm, v_swa_sinks, v_rel_bias, v_w_branch_dn, v_w_branch_swa, v_w_out, v_ffn_norm, v_w_gate, v_w_up, v_w_down):
    args = dict(locals())
    S = x.shape[1]
    xs = x.reshape(S, D_MODEL)
    target = loss_target.reshape(S, D_MODEL)

    w_loc = {n: args[n].reshape(BIG_SHAPES[n][0]) for n in BIG}
    conv_loc = dn_conv.reshape(CONV_SHARD)
    wire = {n: _pad_to(w_loc[n], BIG_SHAPES[n][1]).astype(BF16) for n in BIG}
    first = _all_gather([wire["w_in"], _pad_to(conv_loc, CONV_WIRE)])
    later = [n for n in BIG if n != "w_in"]
    rest_handle, rest_token = _exchange_start([wire[n] for n in later], False, "gather_rest_start", after=first[1])
    w_pad = _w_in_from_blocks(first[0])
    conv_w = jnp.concatenate([first[1][d, :DN_CONV, :CONV_SHARD[1]] for d in range(N_DEV)], axis=1)

    h = _norm_fwd(xs, attn_norm + rest_token[0, 0], "norm1_fwd")
    proj = _mm([(h, w_pad)], "nn", F32, "mm_in", 1024, 1664, j_outer=True)
    qkvn = _dn_conv_fwd(proj, conv_w)
    beta, g = _dn_gate_fwd(proj, dn_a_log, dn_dt_bias)
    u, w, qe, kd, qk, egl, tinv = _dn_prep_fwd(qkvn, g, beta)
    o, states = _dn_scan_fwd(u, w, qe, kd, qk, egl)
    y_dn = _dn_out_fwd(o, proj, dn_out_norm)
    bias = _bias_fwd(rel_bias)
    y_swa = _swa_fwd(proj, swa_q_norm, swa_k_norm, swa_sinks, bias)
    rest_src, rest_land = _exchange_wait(rest_handle, y_swa, False, "gather_rest_wait")
    G = {n: _own_slot(land, src) for n, src, land in zip(later, rest_src, rest_land)}
    w_bdn, w_bswa, w_g, w_u = G["w_branch_dn"], G["w_branch_swa"], G["w_gate"], G["w_up"]
    w_o = G["w_out"].reshape(D_MODEL, D_MODEL)
    w_d = G["w_down"].reshape(D_FFP, D_MODEL)
    gates = [(proj, P_GATE // 512), (proj, (P_GATE + D_MODEL) // 512)]
    a_dn, a_swa, merged = _mm_fused(
        [(y_dn, w_bdn), (y_swa, w_bswa)], "nn", "mm_branch_merge", 1024, 512,
        lambda p, e: (p[0], p[1], _merge(e[0], e[1], p[0], p[1])), gates, (F32, F32, BF16), b_blocks=True)

    def resid_norm(p, e):
        x1 = e[0] + p[0]
        return x1, _rms(x1, e[1])

    x1, h2 = _mm_fused([(merged, w_o)], "nn", "mm_out_norm", 512, D_MODEL, resid_norm,
                       [(xs, 0), (ffn_norm, None)], (F32, BF16))
    gate, up, act = _mm_fused([(h2, w_g), (h2, w_u)], "nn", "mm_gate_up_act", 1024, 768,
                              lambda p, e: (p[0], p[1], _act(p[0], p[1])), [], (F32, F32, BF16),
                              j_outer=True, b_blocks=True)

    def loss_head(p, e):
        diff = e[0] + p[0] - e[1]
        dy = diff * (1.0 / D_MODEL)
        part = jnp.sum(jnp.mean(diff * diff, axis=-1, keepdims=True), axis=0, keepdims=True) * 0.5
        return dy, dy, part

    dy, dy_b, loss_local = _mm_fused([(act, w_d)], "nn", "mm_down_loss", 512, D_MODEL, loss_head,
                                     [(x1, 0), (target, 0)], (F32, BF16), sum_shape=(1, 1))

    def act_bwd(p, e):
        _, vjp = jax.vjp(_act, e[0], e[1])
        return vjp(p[0])

    dgate, dup = _mm_fused([(dy_b, w_d)], "nt", "mm_dact_act", 1024, 768, act_bwd, [(gate, 0), (up, 0)],
                           (BF16, BF16), j_outer=True)
    g_w_down = _mm([(act, dy_b)], "tn", BF16, "mm_dw_down", 768, D_MODEL, j_outer=True)
    g_w_down = g_w_down.reshape(N_DEV, FF_WIRE, D_MODEL)
    g_w_gate = _mm([(h2, dgate)], "tn", BF16, "mm_dw_gate", D_MODEL, 768, out_blocks=True)
    g_w_up = _mm([(h2, dup)], "tn", BF16, "mm_dw_up", D_MODEL, 768, out_blocks=True)
    ffn_handle, ffn_token = _exchange_start([g_w_down, g_w_gate, g_w_up], True, "scatter_ffn_start")

    def norm_bwd(p, e):
        _, vjp = jax.vjp(_rms, e[0], e[2])
        dx, dgain = vjp(sum(p))
        dx = dx + e[1]
        return dx, dx, dgain

    dx1, dx1_b, g_ffn_norm = _mm_fused(
        [(dgate, w_g), (dup, w_u)], "nt", "mm_dh2_norm", 256, D_MODEL, norm_bwd,
        [(x1, 0), (dy, 0), (ffn_norm + ffn_token[0, 0], None)], (F32, BF16), b_blocks=True, sum_shape=(1, D_MODEL))
    def merge_bwd(p, e):
        _, vjp = jax.vjp(_merge, *e)
        dg0, dg1, da_dn, da_swa = vjp(p[0])
        return jnp.concatenate([dg0, dg1], axis=1), da_dn, da_swa

    dproj, da_dn, da_swa = _mm_fused(
        [(dx1_b, w_o)], "nt", "mm_dmerged_merge", 512, D_MODEL, merge_bwd,
        [(proj, P_GATE // D_MODEL), (proj, P_GATE // D_MODEL + 1), (a_dn, 0), (a_swa, 0)], (BF16,) * 3,
        wide_first=(P_WIDTH, 2 * D_MODEL))
    g_w_out = _mm([(merged, dx1_b)], "tn", BF16, "mm_dw_out", 512, D_MODEL, j_outer=True)
    g_w_out = g_w_out.reshape(N_DEV, LANES, D_MODEL)
    dy_dn = _mm([(da_dn, w_bdn)], "nt", F32, "mm_dy_dn", 1024, DN_WIDTH, b_blocks=True)
    dy_swa = _mm([(da_swa, w_bswa)], "nt", F32, "mm_dy_swa", 1024, SWA_WIDTH, b_blocks=True)
    g_w_bdn = _mm([(y_dn, da_dn)], "tn", BF16, "mm_dw_branch_dn", DN_WIDTH, 512, out_blocks=True)
    g_w_bswa = _mm([(y_swa, da_swa)], "tn", BF16, "mm_dw_branch_swa", SWA_WIDTH, 512, out_blocks=True)
    dproj, dsk, dsv, g_q_norm, g_k_norm, g_sinks, dbias = _swa_bwd(proj, swa_q_norm, swa_k_norm, swa_sinks, bias,
                                                                   dy_swa, dproj)
    dproj = _kv_into(dproj, dsk, dsv)
    g_rel_bias = _bias_bwd(dbias)[:, :REL_BUCKETS].T
    mix_handle, mix_token = _exchange_start([g_w_out, g_w_bdn, g_w_bswa], True, "scatter_mix_start")
    do, dproj, g_out_norm = _dn_out_bwd(o, proj, dn_out_norm + mix_token[0, 0], dy_dn, dproj)
    du, dw, dqe, dkd, dqk, degl = _dn_scan_bwd(u, w, qe, kd, qk, egl, states, do)
    dqkvn, dgd, dbeta = _dn_prep_bwd(qkvn, g, beta, tinv, du, dw, dqe, dkd, dqk, degl)
    dproj, dal, ddt = _dn_gate_bwd(proj, dn_a_log, dn_dt_bias, dbeta, dgd, dproj)
    g_a_log = dal.reshape(DN_HEADS, DN_DIM).sum(axis=1)
    g_dt_bias = ddt[0, DN_HEADS:2 * DN_HEADS]
    dproj, g_conv = _dn_conv_bwd(proj, conv_w, dqkvn, dproj)
    g_w_in = _w_in_to_blocks(_mm([(h, dproj)], "tn", BF16, "mm_dw_in", 512, 1664, j_outer=True))
    in_handle, in_token = _exchange_start([g_w_in], True, "scatter_in_start")
    dx, g_attn_norm = _mm_fused(
        [(dproj, w_pad)], "nt", "mm_dh_norm", 512, D_MODEL, lambda p, e: norm_bwd(p, e)[1:],
        [(xs, 0), (dx1, 0), (attn_norm + in_token[0, 0], None)], (F32,), sum_shape=(1, D_MODEL))

    g_small = {"attn_norm": g_attn_norm, "ffn_norm": g_ffn_norm, "rel_bias": g_rel_bias, "dn_out_norm": g_out_norm,
               "swa_q_norm": g_q_norm, "swa_k_norm": g_k_norm, "dn_a_log": g_a_log, "dn_dt_bias": g_dt_bias,
               "swa_sinks": g_sinks}
    me = 4 * lax.axis_index("x") + 2 * lax.axis_index("y") + lax.axis_index("c")
    outs = {}

    def finish(handle, group, name, after):
        srcs, lands = _exchange_wait(handle, after, True, name)
        for n, src, land in zip(group, srcs, lands):
            parts = _own_slot(land, lax.dynamic_index_in_dim(src, me, 0, keepdims=False))
            outs[n] = _adam_update(parts, args[n], args["m_" + n], args["v_" + n], "adam_" + n,
                                   turned=args[n].shape[2] % LANES != 0)

    finish(ffn_handle, ("w_down", "w_gate", "w_up"), "scatter_ffn_wait", dx)
    finish(mix_handle, ("w_out", "w_branch_dn", "w_branch_swa"), "scatter_mix_wait", dx)
    sheets, conv_all = _all_gather_direct([_small_pack(g_small, loss_local), _pad_to(g_conv, (8, DN_QKV))],
                                          "all_gather_small",
                                          after=[outs[n][4] for n in sorted(outs)])
    finish(in_handle, ("w_in",), "scatter_in_wait", sheets)
    conv_parts = lax.dynamic_slice(conv_all, (0, 0, me * CONV_SHARD[1]), (N_DEV,) + CONV_SHARD)
    outs["dn_conv"] = _adam_update(conv_parts, dn_conv, m_dn_conv, v_dn_conv, "adam_dn_conv")
    small_outs, loss = _small_update(sheets, {n: args[n] for n in SMALL}, {n: args["m_" + n] for n in SMALL},
                                     {n: args["v_" + n] for n in SMALL})
    outs.update(small_outs)

    names = ("attn_norm", "w_in", "dn_conv", "dn_a_log", "dn_dt_bias", "dn_out_norm", "swa_q_norm", "swa_k_norm",
             "swa_sinks", "rel_bias", "w_branch_dn", "w_branch_swa", "w_out", "ffn_norm", "w_gate", "w_up", "w_down")
    results = []
    for kind in range(4):
        results += [outs[n][kind].reshape(args[n].shape) for n in names]

    return (loss.reshape(()), dx.reshape(x.shape), *results)
```

```python
import math

import numpy as np
import jax
import jax.numpy as jnp
from jax import lax
from jax.experimental import pallas as pl
from jax.experimental.pallas import tpu as pltpu

F32 = jnp.float32
BF16 = jnp.bfloat16
HI = lax.Precision.HIGHEST

D_MODEL = 1024
DN_HEADS = 4
DN_DIM = 128
DN_WIDTH = 512
DN_QKV = 1536
DN_CONV = 4
CHUNK = 64
SWA_HEADS = 8
SWA_KV = 2
SWA_GROUP = 4
SWA_DIM = 64
SWA_WIDTH = 512
SWA_KVW = 128
WINDOW = 128
BLOCK = 128
REL_BUCKETS = 32
REL_MAX_DIST = 128
D_FF = 2816
D_IN = 4872
EPS = 1e-6
N_DEV = 8

ADAM_LR = 0.001
ADAM_B1 = 0.9
ADAM_B2 = 0.999
ADAM_EPS = 1e-08
ADAM_WD = 0.01
ADAM_STEP = 10

P_GATE, P_QKV, P_Z, P_SQ, P_SK, P_SV, P_BA = 0, 2048, 3584, 4096, 4608, 4736, 4864
P_WIDTH = 4992
R_QKV, R_Z, R_B, R_A, R_SQ, R_SK, R_SV, R_GATE = 0, 1536, 2048, 2052, 2056, 2568, 2696, 2824

VMEM_LIMIT = 56 * 1024 * 1024
LANES = 128
MESH_ID = pl.DeviceIdType.MESH


def _params(sem=None):
    return pltpu.CompilerParams(dimension_semantics=sem, vmem_limit_bytes=VMEM_LIMIT)


def _pick(dim, target):
    if dim <= target:
        return dim
    t = target - target % LANES
    while t >= LANES:
        if dim % t == 0:
            return t
        t -= LANES
    return dim


_DIMS = {"nn": (((1,), (0,)), ((), ())), "nt": (((1,), (1,)), ((), ())), "tn": (((0,), (0,)), ((), ()))}


def _tile_product(a_ref, b_ref, mode, b_blocks):
    a = a_ref[...].astype(BF16)
    b = jnp.concatenate([b_ref[d] for d in range(b_ref.shape[0])], axis=1) if b_blocks else b_ref[...]
    return lax.dot_general(a, b.astype(BF16), _DIMS[mode], preferred_element_type=F32)


def _mm(pairs, mode, out_dtype, name, bm, bn, j_outer=False, b_blocks=False, out_blocks=False):
    a0, b0 = pairs[0]
    cb = b0.shape[2] if b_blocks else None
    b_shape = (b0.shape[1], N_DEV * cb) if b_blocks else b0.shape
    if mode == "nn":
        (M, K), (K2, N) = a0.shape, b_shape
    elif mode == "nt":
        (M, K), (N, K2) = a0.shape, b_shape
    else:
        (K, M), (K2, N) = a0.shape, b_shape
    bm, bn = min(bm, M), min(bn, N)
    assert K == K2 and M % bm == 0 and N % bn == 0, (name, a0.shape, b0.shape, bm, bn)
    co = N // N_DEV
    assert not out_blocks or bn % co == 0
    dims = _DIMS[mode]
    n = len(pairs)

    def body(*refs):
        o_ref = refs[2 * n]
        acc = None
        for t in range(n):
            p = _tile_product(refs[2 * t], refs[2 * t + 1], mode, b_blocks)
            acc = p if acc is None else acc + p
        if out_blocks:
            for d in range(bn // co):
                o_ref[d] = acc[:, d * co:(d + 1) * co].astype(out_dtype)
        else:
            o_ref[...] = acc.astype(out_dtype)

    def ij(f):
        return (lambda j, i: f(i, j)) if j_outer else f

    a_spec = pl.BlockSpec((K, bm), ij(lambda i, j: (0, i))) if mode == "tn" else pl.BlockSpec((bm, K), ij(lambda i, j: (i, 0)))
    if b_blocks and mode == "nt":
        b_spec = pl.BlockSpec((N_DEV, bn, cb), ij(lambda i, j: (0, j, 0)))
    elif b_blocks:
        b_spec = pl.BlockSpec((bn // cb, K, cb), ij(lambda i, j: (j, 0, 0)))
    elif mode == "nt":
        b_spec = pl.BlockSpec((bn, K), ij(lambda i, j: (j, 0)))
    else:
        b_spec = pl.BlockSpec((K, bn), ij(lambda i, j: (0, j)))
    if out_blocks:
        out_spec = pl.BlockSpec((bn // co, bm, co), ij(lambda i, j: (j, i, 0)))
        out_shape = jax.ShapeDtypeStruct((N_DEV, M, co), out_dtype)
    else:
        out_spec = pl.BlockSpec((bm, bn), ij(lambda i, j: (i, j)))
        out_shape = jax.ShapeDtypeStruct((M, N), out_dtype)
    grid = (N // bn, M // bm) if j_outer else (M // bm, N // bn)
    return pl.pallas_call(
        body, grid=grid, in_specs=[a_spec, b_spec] * n, out_specs=out_spec, out_shape=out_shape, name=name,
        compiler_params=_params(("parallel", "parallel")),
    )(*[x for pair in pairs for x in pair])


def _mm_fused(pairs, mode, name, bm, bn, epilogue, extras, out_dtypes, j_outer=False, b_blocks=False,
              sum_shape=None, wide_first=None):
    a0, b0 = pairs[0]
    cb = b0.shape[2] if b_blocks else None
    b_shape = (b0.shape[1], N_DEV * cb) if b_blocks else b0.shape
    if mode == "nn":
        (M, K), (K2, N) = a0.shape, b_shape
    else:
        (M, K), (N, K2) = a0.shape, b_shape
    bm, bn = min(bm, M), min(bn, N)
    assert mode in ("nn", "nt") and K == K2 and M % bm == 0 and N % bn == 0, (name, a0.shape, b0.shape)
    dims = _DIMS[mode]
    n, ne, no = len(pairs), len(extras), len(out_dtypes)

    def body(*refs):
        prods = [_tile_product(refs[2 * t], refs[2 * t + 1], mode, b_blocks) for t in range(n)]
        results = epilogue(prods, [r[...] for r in refs[2 * n:2 * n + ne]])
        out_refs = refs[2 * n + ne:]
        for o_ref, val, dt in zip(out_refs, results, out_dtypes):
            o_ref[...] = val.astype(dt)
        if sum_shape is not None:
            s_ref = out_refs[no]

            @pl.when((pl.program_id(0) == 0) & (pl.program_id(1) == 0))
            def _():
                s_ref[...] = jnp.zeros_like(s_ref)

            s_ref[...] += results[no]

    def ij(f):
        return (lambda j, i: f(i, j)) if j_outer else f

    a_spec = pl.BlockSpec((bm, K), ij(lambda i, j: (i, 0)))
    once = dict(pipeline_mode=pl.Buffered(1)) if bn == N else {}
    if b_blocks and mode == "nt":
        b_spec = pl.BlockSpec((N_DEV, bn, cb), ij(lambda i, j: (0, j, 0)), **once)
    elif b_blocks:
        b_spec = pl.BlockSpec((bn // cb, K, cb), ij(lambda i, j: (j, 0, 0)), **once)
    elif mode == "nt":
        b_spec = pl.BlockSpec((bn, K), ij(lambda i, j: (j, 0)), **once)
    else:
        b_spec = pl.BlockSpec((K, bn), ij(lambda i, j: (0, j)), **once)
    e_specs = [pl.BlockSpec((1, bn), ij(lambda i, j: (0, j))) if first is None
               else pl.BlockSpec((bm, bn), ij(lambda i, j, first=first: (i, first + j))) for _, first in extras]
    tile = pl.BlockSpec((bm, bn), ij(lambda i, j: (i, j)))
    out_specs = [tile] * no
    out_shape = [jax.ShapeDtypeStruct((M, N), dt) for dt in out_dtypes]
    if wide_first is not None:
        assert bn == N
        out_specs[0] = pl.BlockSpec((bm, wide_first[1]), ij(lambda i, j: (i, 0)))
        out_shape[0] = jax.ShapeDtypeStruct((M, wide_first[0]), out_dtypes[0])
    if sum_shape is not None:
        assert sum_shape[1] in (1, bn) and (sum_shape[1] == 1 or bn == N)
        out_specs.append(_full(sum_shape))
        out_shape.append(jax.ShapeDtypeStruct(sum_shape, F32))
    grid = (N // bn, M // bm) if j_outer else (M // bm, N // bn)
    sem = ("arbitrary", "arbitrary") if sum_shape is not None else ("parallel", "parallel")
    return pl.pallas_call(
        body, grid=grid, in_specs=[a_spec, b_spec] * n + e_specs, out_specs=out_specs, out_shape=out_shape,
        name=name, compiler_params=_params(sem),
    )(*[x for pair in pairs for x in pair], *[arr for arr, _ in extras])


def _rms(x, gain):
    return x * lax.rsqrt(jnp.mean(x * x, axis=-1, keepdims=True) + EPS) * gain


def _silu(x):
    return x * jax.nn.sigmoid(x)


def _act(g, u):
    return _silu(g) * u


def _merge(g0, g1, a_dn, a_swa):
    return jax.nn.sigmoid(g0) * a_dn + jax.nn.sigmoid(g1) * a_swa


def _dn_post(c, is_v, q_scale):
    a = _silu(c)
    rs = lax.rsqrt(jnp.sum(a * a, axis=-1, keepdims=True) + EPS) * q_scale
    return a * jnp.where(is_v, 1.0, rs)


def _dn_out(o, z, gain):
    return _rms(o, gain) * _silu(z)


def _dot(a, b, dims=_DIMS["nn"], hi=False):
    if a.ndim == 3 or b.ndim == 3:
        batch = a.shape[0] if a.ndim == 3 else b.shape[0]
        a = a if a.ndim == 3 else jnp.broadcast_to(a, (batch,) + a.shape)
        b = b if b.ndim == 3 else jnp.broadcast_to(b, (batch,) + b.shape)
        ((ca,), (cb,)), _ = dims
        dims = (((ca + 1,), (cb + 1,)), ((0,), (0,)))
    if hi:
        return lax.dot_general(a, b, dims, precision=HI, preferred_element_type=F32)
    return lax.dot_general(a.astype(BF16), b.astype(BF16), dims, preferred_element_type=F32)


def _pieces(x):
    hi = x.astype(BF16)
    r1 = x - hi.astype(F32)
    mid = r1.astype(BF16)
    return hi, mid, (r1 - mid.astype(F32)).astype(BF16)


def _sel_left_impl(m, x):
    mb = m.astype(BF16)
    hi, mid, lo = _pieces(x)
    return _dot(mb, hi) + (_dot(mb, mid) + _dot(mb, lo))


@jax.custom_vjp
def _sel_left(m, mt, x):
    return _sel_left_impl(m, x)


_sel_left.defvjp(lambda m, mt, x: (_sel_left_impl(m, x), (m, mt)),
                 lambda res, ct: (jnp.zeros_like(res[0]), jnp.zeros_like(res[1]), _sel_left_impl(res[1], ct)))


def _sel_right_impl(x, s):
    sb = s.astype(BF16)
    hi, mid, lo = _pieces(x)
    return _dot(hi, sb) + (_dot(mid, sb) + _dot(lo, sb))


@jax.custom_vjp
def _sel_right(x, s, st):
    return _sel_right_impl(x, s)


_sel_right.defvjp(lambda x, s, st: (_sel_right_impl(x, s), (s, st)),
                  lambda res, ct: (_sel_right_impl(ct, res[1]), jnp.zeros_like(res[0]), jnp.zeros_like(res[1])))


def _dot3_impl(a, b):
    a_hi, a_lo, _ = _pieces(a)
    b_hi, b_lo, _ = _pieces(b)
    return _dot(a_hi, b_hi) + (_dot(a_hi, b_lo) + _dot(a_lo, b_hi))


@jax.custom_vjp
def _dot3(a, b):
    return _dot3_impl(a, b)


_dot3.defvjp(lambda a, b: (_dot3_impl(a, b), (a, b)),
             lambda res, ct: (_dot(ct, res[1], _DIMS["nt"]), _dot(res[0], ct, _DIMS["tn"])))


def _inv_impl(a, eye, strict):
    t = eye - a
    p = _dot(a, a)
    for level in range(5):
        t = t + _dot(t, p)
        if level < 4:
            p = _dot(p, p)
    t = t + _dot(t, eye - t - _dot3_impl(a, t))
    return jnp.where(strict > 0.5, t, eye)


@jax.custom_vjp
def _inv_given(a, t):
    return t.astype(F32)


_inv_given.defvjp(lambda a, t: (t.astype(F32), t),
                  lambda t, ct: (-_dot(_dot(t, ct, _DIMS["tn"]), t, _DIMS["nt"]), jnp.zeros_like(t)))


@jax.custom_vjp
def _lanes_join(a, b):
    return jnp.concatenate([a, b], axis=-1)


_lanes_join.defvjp(lambda a, b: (jnp.concatenate([a, b], axis=-1), None),
                   lambda _, ct: (ct[..., :ct.shape[-1] // 2], ct[..., ct.shape[-1] // 2:]))


@jax.custom_vjp
def _lanes_halves(y):
    h = y.shape[-1] // 2
    return y[..., :h], y[..., h:]


_lanes_halves.defvjp(lambda y: ((y[..., :y.shape[-1] // 2], y[..., y.shape[-1] // 2:]), None),
                     lambda _, ct: (jnp.concatenate(ct, axis=-1),))

GROUP = 4
GROUP_ROWS = GROUP * CHUNK


def _block_consts(n):
    ii = lax.broadcasted_iota(jnp.int32, (n, n), 0)
    jj = lax.broadcasted_iota(jnp.int32, (n, n), 1)
    shift = CHUNK.bit_length() - 1
    same = jnp.right_shift(ii, shift) == jnp.right_shift(jj, shift)
    return same & (ii >= jj), same & (ii <= jj), same & (ii > jj), same, ii == jj


def _lane0(n):
    s = (lax.broadcasted_iota(jnp.int32, (LANES, n), 0) == 0).astype(F32)
    st = (lax.broadcasted_iota(jnp.int32, (n, LANES), 1) == 0).astype(F32)
    return s, st


def _dn_group(q, k, v, g, beta, t_saved=None):
    n = GROUP_ROWS
    low_b, upp_b, strict_b, _, eye_b = _block_consts(n)
    low, upp, eye = low_b.astype(F32), upp_b.astype(F32), eye_b.astype(F32)
    gc = _sel_left(low, upp, g)
    per_chunk = (g.shape[0], GROUP, CHUNK, LANES)
    g_last = jnp.sum(g.reshape(per_chunk), axis=2, keepdims=True)
    gl = jnp.broadcast_to(g_last, per_chunk).reshape(g.shape)
    s, st = _lane0(n)
    col = _sel_right(gc, s, st)
    row = jnp.swapaxes(col, 1, 2)
    decay = jnp.exp(jnp.where(low_b, col - row, -jnp.inf))
    kb = k * beta
    vb = v * beta
    a = jnp.where(strict_b, _dot(kb, k, _DIMS["nt"]) * decay, 0.0)
    t = _inv_impl(a, eye, strict_b.astype(F32)) if t_saved is None else _inv_given(a, t_saved)
    u, w = _lanes_halves(_dot3(t, _lanes_join(vb, kb * jnp.exp(gc))))
    fold = (jnp.bitwise_and(lax.broadcasted_iota(jnp.int32, (n, CHUNK), 0), CHUNK - 1)
            == lax.broadcasted_iota(jnp.int32, (n, CHUNK), 1)).astype(F32)
    fold_t = (jnp.bitwise_and(lax.broadcasted_iota(jnp.int32, (CHUNK, n), 1), CHUNK - 1)
              == lax.broadcasted_iota(jnp.int32, (CHUNK, n), 0)).astype(F32)
    qk = _sel_right(_dot(q, k, _DIMS["nt"]) * decay, fold, fold_t)
    return u, w, q * jnp.exp(gc), k * jnp.exp(gl - gc), qk, jnp.exp(g_last), t


def _dn_step(s, u, w, qe, kd, qk, egl):
    v_new = u - _dot(w, s)
    o = _dot(qe, s) + _dot(qk, v_new)
    s_new = s * egl + _dot(kd, v_new, _DIMS["tn"])
    return s_new, o


def _swa_block(q, kband, vband, qg, kg, sinks, band):
    kn = _rms(kband, kg)
    qn = _rms(q, qg) * (SWA_DIM ** -0.5)
    logits = _dot(qn, kn, _DIMS["nt"]) + band
    m = lax.stop_gradient(jnp.maximum(jnp.max(logits, axis=-1, keepdims=True), sinks))
    p = jnp.exp(logits - m)
    denom = jnp.sum(p, axis=-1, keepdims=True) + jnp.exp(sinks - m)
    return _dot(p * (1.0 / denom), vband)


def _adamw(w, g, m, v):
    m = ADAM_B1 * m + (1.0 - ADAM_B1) * g
    v = ADAM_B2 * v + (1.0 - ADAM_B2) * jnp.square(g)
    m_hat = m / (1.0 - ADAM_B1 ** ADAM_STEP)
    v_hat = v / (1.0 - ADAM_B2 ** ADAM_STEP)
    delta = -ADAM_LR * (m_hat / (jnp.sqrt(v_hat) + ADAM_EPS) + ADAM_WD * w)
    return delta, m, v


def _row(tm, c, cb=0):
    return pl.BlockSpec((tm, c), lambda i, cb=cb: (i, cb))


def _full(shape):
    nd = len(shape)
    return pl.BlockSpec(shape, lambda *_, nd=nd: (0,) * nd)


def _norm_fwd(x, gain, name, tm=1024):
    S = x.shape[0]

    def body(x_ref, g_ref, h_ref):
        h_ref[...] = _rms(x_ref[...], g_ref[...]).astype(BF16)

    return pl.pallas_call(
        body, grid=(S // tm,), in_specs=[_row(tm, D_MODEL), _full((1, D_MODEL))],
        out_specs=_row(tm, D_MODEL), out_shape=jax.ShapeDtypeStruct((S, D_MODEL), BF16),
        name=name, compiler_params=_params(("parallel",)))(x, gain)


def _shift_down(x, s):
    row = lax.broadcasted_iota(jnp.int32, x.shape, 0)
    return jnp.where(row >= s, pltpu.roll(x, s, axis=0), 0.0)


def _shift_up(x, s):
    n = x.shape[0]
    row = lax.broadcasted_iota(jnp.int32, x.shape, 0)
    return jnp.where(row < n - s, pltpu.roll(x, n - s, axis=0), 0.0)


def _conv(x, w):
    out = w[DN_CONV - 1:DN_CONV] * x
    for s in range(1, DN_CONV):
        out = out + w[DN_CONV - 1 - s:DN_CONV - s] * _shift_down(x, s)
    return out


def _dn_conv_fwd(proj, conv_w):
    S = proj.shape[0]
    nb = DN_QKV // LANES

    def body(x_ref, w_ref, o_ref):
        j = pl.program_id(0)
        q_scale = jnp.where(j < DN_HEADS, DN_DIM ** -0.5, 1.0).astype(F32)
        o_ref[...] = _dn_post(_conv(x_ref[...], w_ref[...]), j >= 2 * DN_HEADS, q_scale)

    return pl.pallas_call(
        body, grid=(nb,),
        in_specs=[pl.BlockSpec((S, LANES), lambda j: (0, P_QKV // LANES + j)),
                  pl.BlockSpec((DN_CONV, LANES), lambda j: (0, j))],
        out_specs=pl.BlockSpec((S, LANES), lambda j: (0, j)),
        out_shape=jax.ShapeDtypeStruct((S, DN_QKV), F32), name="dn_conv_fwd",
        compiler_params=_params(("parallel",)))(proj, conv_w)


def _dn_conv_bwd(proj, conv_w, dqkvn, dproj):
    S = proj.shape[0]
    nb = DN_QKV // LANES

    def body(x_ref, w_ref, d_ref, _, dx_ref, dw_ref):
        j = pl.program_id(0)
        q_scale = jnp.where(j < DN_HEADS, DN_DIM ** -0.5, 1.0).astype(F32)
        x = x_ref[...]
        w = w_ref[...]
        _, vjp = jax.vjp(lambda c: _dn_post(c, j >= 2 * DN_HEADS, q_scale), _conv(x, w))
        (dc,) = vjp(d_ref[0])
        dx = w[DN_CONV - 1:DN_CONV] * dc
        dw_ref[DN_CONV - 1:DN_CONV, :] = jnp.sum(dc * x, axis=0, keepdims=True)
        for s in range(1, DN_CONV):
            dx = dx + w[DN_CONV - 1 - s:DN_CONV - s] * _shift_up(dc, s)
            dw_ref[DN_CONV - 1 - s:DN_CONV - s, :] = jnp.sum(dc * _shift_down(x, s), axis=0, keepdims=True)
        dx_ref[...] = dx.astype(BF16)

    return pl.pallas_call(
        body, grid=(nb,),
        in_specs=[pl.BlockSpec((S, LANES), lambda j: (0, P_QKV // LANES + j)),
                  pl.BlockSpec((DN_CONV, LANES), lambda j: (0, j)),
                  pl.BlockSpec((1, S, LANES), lambda j: (lax.div(j, DN_HEADS), 0, lax.rem(j, DN_HEADS))),
                  pl.BlockSpec(memory_space=pl.ANY)],
        out_specs=[pl.BlockSpec((S, LANES), lambda j: (0, P_QKV // LANES + j)),
                   pl.BlockSpec((DN_CONV, LANES), lambda j: (0, j))],
        out_shape=[jax.ShapeDtypeStruct(dproj.shape, dproj.dtype), jax.ShapeDtypeStruct((DN_CONV, DN_QKV), F32)],
        input_output_aliases={3: 0},
        name="dn_conv_bwd", compiler_params=_params(("parallel",)))(proj, conv_w, dqkvn, dproj)


def _expanders():
    eb = np.zeros((LANES, DN_WIDTH), np.float32)
    ea = np.zeros((LANES, DN_WIDTH), np.float32)
    for h in range(DN_HEADS):
        eb[h, h * DN_DIM:(h + 1) * DN_DIM] = 1.0
        ea[DN_HEADS + h, h * DN_DIM:(h + 1) * DN_DIM] = 1.0
    return jnp.asarray(eb), jnp.asarray(ea), jnp.asarray(eb.T), jnp.asarray(ea.T)


def _dn_gate_args(a_log, dt_bias):
    alog = jnp.repeat(a_log.reshape(1, DN_HEADS), DN_DIM, axis=1)
    dtb = _pad_to(jnp.pad(dt_bias.reshape(1, DN_HEADS), ((0, 0), (DN_HEADS, 0))), (1, LANES))
    return _expanders() + (alog, dtb)


def _dn_gate_specs(tm):
    return [_row(tm, LANES, P_BA // LANES), _full((LANES, DN_WIDTH)), _full((LANES, DN_WIDTH)),
            _full((DN_WIDTH, LANES)), _full((DN_WIDTH, LANES)), _full((1, DN_WIDTH)), _full((1, LANES))]


def _dn_gate_fn(ba, eb, ea, ebt, eat, alog, dtb):
    beta = _sel_right(jax.nn.sigmoid(ba), eb, ebt)
    g = -jnp.exp(alog) * _sel_right(jax.nn.softplus(ba + dtb), ea, eat)
    return beta, g


def _dn_gate_fwd(proj, a_log, dt_bias, tm=1024):
    S = proj.shape[0]
    args = _dn_gate_args(a_log, dt_bias)

    def body(ba_ref, eb_ref, ea_ref, ebt_ref, eat_ref, al_ref, dt_ref, beta_ref, g_ref):
        beta, g = _dn_gate_fn(ba_ref[...], eb_ref[...], ea_ref[...], ebt_ref[...], eat_ref[...], al_ref[...],
                              dt_ref[...])
        beta_ref[...] = beta
        g_ref[...] = g

    return pl.pallas_call(
        body, grid=(S // tm,), in_specs=_dn_gate_specs(tm), out_specs=[_row(tm, DN_WIDTH), _row(tm, DN_WIDTH)],
        out_shape=[jax.ShapeDtypeStruct((S, DN_WIDTH), F32), jax.ShapeDtypeStruct((S, DN_WIDTH), F32)],
        name="dn_gate_fwd", compiler_params=_params(("parallel",)))(proj, *args)


def _dn_gate_bwd(proj, a_log, dt_bias, dbeta, dg, dproj, tm=1024):
    S = proj.shape[0]
    args = _dn_gate_args(a_log, dt_bias)

    def body(ba_ref, eb_ref, ea_ref, ebt_ref, eat_ref, al_ref, dt_ref, dbeta_ref, dg_ref, _, dba_ref, dal_ref,
             ddt_ref):
        eb, ea, ebt, eat = eb_ref[...], ea_ref[...], ebt_ref[...], eat_ref[...]
        _, vjp = jax.vjp(lambda ba, al, dt: _dn_gate_fn(ba, eb, ea, ebt, eat, al, dt), ba_ref[...], al_ref[...],
                         dt_ref[...])
        dba, dal, ddt = vjp((dbeta_ref[...], dg_ref[...]))
        dba_ref[...] = dba.astype(BF16)

        @pl.when(pl.program_id(0) == 0)
        def _():
            dal_ref[...] = jnp.zeros_like(dal_ref)
            ddt_ref[...] = jnp.zeros_like(ddt_ref)

        dal_ref[...] += dal
        ddt_ref[...] += ddt

    return pl.pallas_call(
        body, grid=(S // tm,),
        in_specs=_dn_gate_specs(tm) + [_row(tm, DN_WIDTH), _row(tm, DN_WIDTH), pl.BlockSpec(memory_space=pl.ANY)],
        out_specs=[_row(tm, LANES, P_BA // LANES), _full((1, DN_WIDTH)), _full((1, LANES))],
        out_shape=[jax.ShapeDtypeStruct(dproj.shape, dproj.dtype), jax.ShapeDtypeStruct((1, DN_WIDTH), F32),
                   jax.ShapeDtypeStruct((1, LANES), F32)],
        input_output_aliases={len(args) + 3: 0},
        name="dn_gate_bwd", compiler_params=_params(("arbitrary",)))(proj, *args, dbeta, dg, dproj)


PREP_GROUPS = 8
PREP_CHUNKS = GROUP * PREP_GROUPS


def _dn_prep_specs():
    rows = PREP_CHUNKS * CHUNK
    q = pl.BlockSpec((rows, LANES), lambda h, c: (c, h))
    k = pl.BlockSpec((rows, LANES), lambda h, c: (c, DN_HEADS + h))
    v = pl.BlockSpec((rows, LANES), lambda h, c: (c, 2 * DN_HEADS + h))
    qk = pl.BlockSpec((1, rows, CHUNK), lambda h, c: (h, c, 0))
    egl = pl.BlockSpec((1, PREP_CHUNKS, 1, LANES), lambda h, c: (h, c, 0, 0))
    return q, k, v, qk, egl


def _dn_prep_fwd(qkvn, g, beta):
    S = qkvn.shape[0]
    nc = S // CHUNK
    q, k, v, qks, egl = _dn_prep_specs()

    def body(q_ref, k_ref, v_ref, g_ref, b_ref, u_ref, w_ref, qe_ref, kd_ref, qk_ref, egl_ref, t_ref):
        rows = PREP_CHUNKS * CHUNK
        grp = (PREP_GROUPS, GROUP_ROWS, LANES)
        u, w, qe, kd, qk, e, t = _dn_group(q_ref[...].reshape(grp), k_ref[...].reshape(grp), v_ref[...].reshape(grp),
                                           g_ref[...].reshape(grp), b_ref[...].reshape(grp))
        u_ref[...] = u.reshape(rows, LANES)
        w_ref[...] = w.reshape(rows, LANES)
        qe_ref[...] = qe.reshape(rows, LANES)
        kd_ref[...] = kd.reshape(rows, LANES)
        t_ref[0] = t.reshape(rows, GROUP_ROWS).astype(BF16)
        qk_ref[0] = qk.reshape(rows, CHUNK)
        egl_ref[0] = e.reshape(PREP_CHUNKS, 1, LANES)

    wide = jax.ShapeDtypeStruct((S, DN_WIDTH), F32)
    return pl.pallas_call(
        body, grid=(DN_HEADS, nc // PREP_CHUNKS), in_specs=[q, k, v, q, q],
        out_specs=[q, q, q, q, qks, egl, _dn_tinv_spec()],
        out_shape=[wide, wide, wide, wide, jax.ShapeDtypeStruct((DN_HEADS, S, CHUNK), F32),
                   jax.ShapeDtypeStruct((DN_HEADS, nc, 1, LANES), F32),
                   jax.ShapeDtypeStruct((DN_HEADS, S, GROUP_ROWS), BF16)],
        name="dn_prep_fwd", compiler_params=_params(("parallel", "parallel")))(qkvn, qkvn, qkvn, g, beta)


def _dn_tinv_spec():
    return pl.BlockSpec((1, PREP_CHUNKS * CHUNK, GROUP_ROWS), lambda h, c: (h, c, 0))


def _dn_prep_bwd(qkvn, g, beta, tinv, du, dw, dqe, dkd, dqk, degl):
    S = qkvn.shape[0]
    nc = S // CHUNK
    q, k, v, qks, egl = _dn_prep_specs()

    def body(q_ref, k_ref, v_ref, g_ref, b_ref, t_ref, du_ref, dw_ref, dqe_ref, dkd_ref, dqk_ref, degl_ref,
             dqkv_ref, dg_ref, db_ref):
        rows = PREP_CHUNKS * CHUNK
        grp = (PREP_GROUPS, GROUP_ROWS, LANES)
        t_saved = t_ref[0].reshape(PREP_GROUPS, GROUP_ROWS, GROUP_ROWS)
        _, vjp = jax.vjp(lambda *x: _dn_group(*x, t_saved=t_saved)[:6], q_ref[...].reshape(grp),
                         k_ref[...].reshape(grp), v_ref[...].reshape(grp), g_ref[...].reshape(grp),
                         b_ref[...].reshape(grp))
        dq, dk, dv, dg, db = vjp((du_ref[...].reshape(grp), dw_ref[...].reshape(grp), dqe_ref[...].reshape(grp),
                                  dkd_ref[...].reshape(grp), dqk_ref[0].reshape(PREP_GROUPS, GROUP_ROWS, CHUNK),
                                  degl_ref[0].reshape(PREP_GROUPS, GROUP, 1, LANES)))
        dqkv_ref[0] = dq.reshape(rows, LANES)
        dqkv_ref[1] = dk.reshape(rows, LANES)
        dqkv_ref[2] = dv.reshape(rows, LANES)
        dg_ref[...] = dg.reshape(rows, LANES)
        db_ref[...] = db.reshape(rows, LANES)

    wide = jax.ShapeDtypeStruct((S, DN_WIDTH), F32)
    rows = PREP_CHUNKS * CHUNK
    return pl.pallas_call(
        body, grid=(DN_HEADS, nc // PREP_CHUNKS), in_specs=[q, k, v, q, q, _dn_tinv_spec(), q, q, q, q, qks, egl],
        out_specs=[pl.BlockSpec((3, rows, LANES), lambda h, c: (0, c, h)), q, q],
        out_shape=[jax.ShapeDtypeStruct((3, S, DN_WIDTH), F32), wide, wide],
        name="dn_prep_bwd", compiler_params=_params(("parallel", "parallel")),
    )(qkvn, qkvn, qkvn, g, beta, tinv, du, dw, dqe, dkd, dqk, degl)


SCAN_CHUNKS = 16


def _dn_scan_specs(nc, reverse):
    nb = nc // SCAN_CHUNKS

    def cidx(c):
        return nb - 1 - c if reverse else c

    hc = pl.BlockSpec((SCAN_CHUNKS * CHUNK, DN_WIDTH), lambda c: (cidx(c), 0))
    qk = pl.BlockSpec((DN_HEADS, SCAN_CHUNKS * CHUNK, CHUNK), lambda c: (0, cidx(c), 0))
    egl = pl.BlockSpec((DN_HEADS, SCAN_CHUNKS, 1, LANES), lambda c: (0, cidx(c), 0, 0))
    st = pl.BlockSpec((DN_HEADS, SCAN_CHUNKS, DN_DIM, DN_DIM), lambda c: (0, cidx(c), 0, 0))
    return hc, qk, egl, st


def _heads(ref, i):
    return jnp.stack([ref[pl.ds(i * CHUNK, CHUNK), pl.ds(h * DN_DIM, DN_DIM)] for h in range(DN_HEADS)])


def _dn_scan_fwd(u, w, qe, kd, qk, egl):
    S = u.shape[0]
    nc = S // CHUNK
    hc, qks, egls, st = _dn_scan_specs(nc, False)

    def body(u_ref, w_ref, qe_ref, kd_ref, qk_ref, egl_ref, o_ref, st_ref, s_scr):
        @pl.when(pl.program_id(0) == 0)
        def _():
            s_scr[...] = jnp.zeros_like(s_scr)

        s = s_scr[...]
        for i in range(SCAN_CHUNKS):
            rows = pl.ds(i * CHUNK, CHUNK)
            st_ref[:, i] = s
            s, o = _dn_step(s, _heads(u_ref, i), _heads(w_ref, i), _heads(qe_ref, i), _heads(kd_ref, i),
                            qk_ref[:, rows, :], egl_ref[:, i])
            for h in range(DN_HEADS):
                o_ref[rows, pl.ds(h * DN_DIM, DN_DIM)] = o[h]
        s_scr[...] = s

    return pl.pallas_call(
        body, grid=(nc // SCAN_CHUNKS,), in_specs=[hc, hc, hc, hc, qks, egls], out_specs=[hc, st],
        out_shape=[jax.ShapeDtypeStruct((S, DN_WIDTH), F32), jax.ShapeDtypeStruct((DN_HEADS, nc, DN_DIM, DN_DIM), F32)],
        scratch_shapes=[pltpu.VMEM((DN_HEADS, DN_DIM, DN_DIM), F32)], name="dn_scan_fwd",
        compiler_params=_params(("arbitrary",)))(u, w, qe, kd, qk, egl)


def _dn_scan_bwd(u, w, qe, kd, qk, egl, states, do):
    S = u.shape[0]
    nc = S // CHUNK
    hc, qks, egls, st = _dn_scan_specs(nc, True)

    def body(u_ref, w_ref, qe_ref, kd_ref, qk_ref, egl_ref, st_ref, do_ref,
             du_ref, dw_ref, dqe_ref, dkd_ref, dqk_ref, degl_ref, ds_scr):
        @pl.when(pl.program_id(0) == 0)
        def _():
            ds_scr[...] = jnp.zeros_like(ds_scr)

        ds = ds_scr[...]
        for i in reversed(range(SCAN_CHUNKS)):
            rows = pl.ds(i * CHUNK, CHUNK)
            _, vjp = jax.vjp(_dn_step, st_ref[:, i], _heads(u_ref, i), _heads(w_ref, i), _heads(qe_ref, i),
                             _heads(kd_ref, i), qk_ref[:, rows, :], egl_ref[:, i])
            ds, du, dw, dqe, dkd, dqk, degl = vjp((ds, _heads(do_ref, i)))
            dqk_ref[:, rows, :] = dqk
            degl_ref[:, i] = degl
            for h in range(DN_HEADS):
                cols = pl.ds(h * DN_DIM, DN_DIM)
                du_ref[rows, cols] = du[h]
                dw_ref[rows, cols] = dw[h]
                dqe_ref[rows, cols] = dqe[h]
                dkd_ref[rows, cols] = dkd[h]
        ds_scr[...] = ds

    wide = jax.ShapeDtypeStruct((S, DN_WIDTH), F32)
    return pl.pallas_call(
        body, grid=(nc // SCAN_CHUNKS,), in_specs=[hc, hc, hc, hc, qks, egls, st, hc],
        out_specs=[hc, hc, hc, hc, qks, egls],
        out_shape=[wide, wide, wide, wide, jax.ShapeDtypeStruct((DN_HEADS, S, CHUNK), F32),
                   jax.ShapeDtypeStruct((DN_HEADS, nc, 1, LANES), F32)],
        scratch_shapes=[pltpu.VMEM((DN_HEADS, DN_DIM, DN_DIM), F32)], name="dn_scan_bwd",
        compiler_params=_params(("arbitrary",)))(u, w, qe, kd, qk, egl, states, do)


def _dn_out_fwd(o, proj, gain, tm=1024):
    S = o.shape[0]

    def body(o_ref, z_ref, g_ref, y_ref):
        y_ref[...] = _dn_out(o_ref[...], z_ref[...], g_ref[...]).astype(BF16)

    hs = pl.BlockSpec((tm, LANES), lambda i, h: (i, h))
    zs = pl.BlockSpec((tm, LANES), lambda i, h: (i, P_Z // LANES + h))
    return pl.pallas_call(
        body, grid=(S // tm, DN_HEADS), in_specs=[hs, zs, _full((1, DN_DIM))], out_specs=hs,
        out_shape=jax.ShapeDtypeStruct((S, DN_WIDTH), BF16), name="dn_out_fwd",
        compiler_params=_params(("parallel", "parallel")))(o, proj, gain)


_ANY = pl.BlockSpec(memory_space=pl.ANY)


def _dn_out_bwd(o, proj, gain, dy, dproj, tm=1024):
    S = o.shape[0]

    def body(o_ref, z_ref, g_ref, dy_ref, _, do_ref, dz_ref, dg_ref):
        _, vjp = jax.vjp(_dn_out, o_ref[...], z_ref[...], g_ref[...])
        do, dz, dg = vjp(dy_ref[...])
        do_ref[...] = do
        dz_ref[...] = dz.astype(BF16)

        @pl.when((pl.program_id(0) == 0) & (pl.program_id(1) == 0))
        def _():
            dg_ref[...] = jnp.zeros_like(dg_ref)

        dg_ref[...] += dg

    hs = pl.BlockSpec((tm, LANES), lambda i, h: (i, h))
    zs = pl.BlockSpec((tm, LANES), lambda i, h: (i, P_Z // LANES + h))
    return pl.pallas_call(
        body, grid=(S // tm, DN_HEADS), in_specs=[hs, zs, _full((1, DN_DIM)), hs, _ANY],
        out_specs=[hs, zs, _full((1, DN_DIM))],
        out_shape=[jax.ShapeDtypeStruct((S, DN_WIDTH), F32), jax.ShapeDtypeStruct(dproj.shape, dproj.dtype),
                   jax.ShapeDtypeStruct((1, DN_DIM), F32)],
        input_output_aliases={4: 1},
        name="dn_out_bwd", compiler_params=_params(("arbitrary", "arbitrary")))(o, proj, gain, dy, dproj)


def _rel_buckets():
    qi = np.arange(BLOCK)[:, None]
    kj = np.arange(2 * BLOCK)[None, :]
    n = np.maximum(BLOCK + qi - kj, 0)
    max_exact = REL_BUCKETS // 2
    nf = np.maximum(n, 1).astype(np.float32)
    large = max_exact + (np.log(nf / np.float32(max_exact)) / np.float32(math.log(REL_MAX_DIST / max_exact))
                         * np.float32(REL_BUCKETS - max_exact)).astype(np.int32)
    large = np.minimum(large, REL_BUCKETS - 1)
    return np.where(n < max_exact, n, large).astype(np.int32)


def _bias_fwd(rel_bias):
    buckets = jnp.asarray(_rel_buckets())

    def body(rb_ref, bk_ref, o_ref):
        bk = bk_ref[...]
        for h in range(SWA_HEADS):
            acc = jnp.zeros((BLOCK, 2 * BLOCK), F32)
            for b in range(REL_BUCKETS):
                acc = jnp.where(bk == b, rb_ref[b, h], acc)
            for first in range(2):
                o_ref[first, h] = jnp.where(_swa_mask(1 - first), acc, -jnp.inf)

    return pl.pallas_call(
        body, in_specs=[pl.BlockSpec(memory_space=pltpu.SMEM), pl.BlockSpec(memory_space=pltpu.VMEM)],
        out_specs=pl.BlockSpec(memory_space=pltpu.VMEM),
        out_shape=jax.ShapeDtypeStruct((2, SWA_HEADS, BLOCK, 2 * BLOCK), F32), name="swa_bias_fwd",
        compiler_params=_params())(rel_bias, buckets)


def _bias_bwd(dbias):
    buckets = jnp.asarray(_rel_buckets())

    def body(d_ref, bk_ref, o_ref):
        bk = bk_ref[...]
        lane = lax.broadcasted_iota(jnp.int32, (1, LANES), 1)
        for h in range(SWA_HEADS):
            d = d_ref[h]
            row = jnp.zeros((1, LANES), F32)
            for b in range(REL_BUCKETS):
                part = jnp.sum(jnp.where(bk == b, d, 0.0), axis=1, keepdims=True)
                row = jnp.where(lane == b, jnp.sum(part, axis=0, keepdims=True), row)
            o_ref[h:h + 1, :] = row

    return pl.pallas_call(
        body, in_specs=[pl.BlockSpec(memory_space=pltpu.VMEM), pl.BlockSpec(memory_space=pltpu.VMEM)],
        out_specs=pl.BlockSpec(memory_space=pltpu.VMEM),
        out_shape=jax.ShapeDtypeStruct((SWA_HEADS, LANES), F32), name="swa_bias_bwd",
        compiler_params=_params())(dbias, buckets)


def _swa_mask(n):
    qi = lax.broadcasted_iota(jnp.int32, (BLOCK, 2 * BLOCK), 0)
    kj = lax.broadcasted_iota(jnp.int32, (BLOCK, 2 * BLOCK), 1)
    dist = BLOCK + qi - kj
    return (dist >= 0) & (dist < WINDOW) & ((n > 0) | (kj >= BLOCK))


def _swa_in_specs():
    q = pl.BlockSpec((BLOCK, SWA_WIDTH), lambda n: (n, P_SQ // SWA_WIDTH))
    kc = pl.BlockSpec((BLOCK, SWA_KVW), lambda n: (n, P_SK // SWA_KVW))
    kp = pl.BlockSpec((BLOCK, SWA_KVW), lambda n: (jnp.maximum(n - 1, 0), P_SK // SWA_KVW))
    vc = pl.BlockSpec((BLOCK, SWA_KVW), lambda n: (n, P_SV // SWA_KVW))
    vp = pl.BlockSpec((BLOCK, SWA_KVW), lambda n: (jnp.maximum(n - 1, 0), P_SV // SWA_KVW))
    band = pl.BlockSpec((None, SWA_HEADS, BLOCK, 2 * BLOCK), lambda n: (jnp.where(n == 0, 1, 0), 0, 0, 0))
    small = [_full((1, SWA_DIM)), _full((1, SWA_DIM)), _full((1, SWA_HEADS)), band]
    return [q, kp, kc, vp, vc] + small


def _swa_load(q_ref, kp_ref, kc_ref, vp_ref, vc_ref, s_ref):
    q = jnp.stack([q_ref[:, pl.ds(h * SWA_DIM, SWA_DIM)] for h in range(SWA_HEADS)])
    kbands, vbands = [], []
    for kv in range(SWA_KV):
        cols = pl.ds(kv * SWA_DIM, SWA_DIM)
        kbands += [jnp.concatenate([kp_ref[:, cols], kc_ref[:, cols]], axis=0)] * SWA_GROUP
        vbands += [jnp.concatenate([vp_ref[:, cols], vc_ref[:, cols]], axis=0)] * SWA_GROUP
    sinks = jnp.stack([s_ref[:, pl.ds(h, 1)] for h in range(SWA_HEADS)])
    return q, jnp.stack(kbands), jnp.stack(vbands), sinks


def _swa_fwd(proj, q_gain, k_gain, sinks, bias):
    S = proj.shape[0]

    def body(q_ref, kp_ref, kc_ref, vp_ref, vc_ref, qg_ref, kg_ref, s_ref, bias_ref, y_ref):
        q, kband, vband, sk = _swa_load(q_ref, kp_ref, kc_ref, vp_ref, vc_ref, s_ref)
        out = _swa_block(q, kband, vband, qg_ref[...], kg_ref[...], sk, bias_ref[...])
        for h in range(SWA_HEADS):
            y_ref[:, pl.ds(h * SWA_DIM, SWA_DIM)] = out[h].astype(BF16)

    return pl.pallas_call(
        body, grid=(S // BLOCK,), in_specs=_swa_in_specs(),
        out_specs=pl.BlockSpec((BLOCK, SWA_WIDTH), lambda n: (n, 0)),
        out_shape=jax.ShapeDtypeStruct((S, SWA_WIDTH), BF16), name="swa_fwd",
        compiler_params=_params(("parallel",)))(proj, proj, proj, proj, proj, q_gain, k_gain, sinks, bias)


def _swa_bwd(proj, q_gain, k_gain, sinks, bias, dy, dproj):
    S = proj.shape[0]

    def body(q_ref, kp_ref, kc_ref, vp_ref, vc_ref, qg_ref, kg_ref, s_ref, bias_ref, dy_ref, _,
             dq_ref, dk_ref, dv_ref, dqg_ref, dkg_ref, ds_ref, dbias_ref):
        n = pl.program_id(0)

        @pl.when(n == 0)
        def _():
            for r in (dk_ref, dv_ref, dqg_ref, dkg_ref, ds_ref, dbias_ref):
                r[...] = jnp.zeros_like(r)

        cur = pl.ds(pl.multiple_of(n * BLOCK, BLOCK), BLOCK)
        prev = pl.ds(pl.multiple_of(jnp.maximum(n - 1, 0) * BLOCK, BLOCK), BLOCK)
        q, kband, vband, sk = _swa_load(q_ref, kp_ref, kc_ref, vp_ref, vc_ref, s_ref)
        _, vjp = jax.vjp(_swa_block, q, kband, vband, qg_ref[...], kg_ref[...], sk, bias_ref[...])
        dy = jnp.stack([dy_ref[:, pl.ds(h * SWA_DIM, SWA_DIM)] for h in range(SWA_HEADS)])
        dq, dkb, dvb, dqg, dkg, dsk, dbs = vjp(dy)
        for h in range(SWA_HEADS):
            dq_ref[:, pl.ds(h * SWA_DIM, SWA_DIM)] = dq[h].astype(BF16)
            ds_ref[:, pl.ds(h, 1)] += dsk[h]
        dbias_ref[...] += dbs
        dqg_ref[...] += dqg
        dkg_ref[...] += dkg
        for kv in range(SWA_KV):
            cols = pl.ds(kv * SWA_DIM, SWA_DIM)
            group = range(kv * SWA_GROUP, (kv + 1) * SWA_GROUP)
            dk_kv = sum(dkb[h] for h in group)
            dv_kv = sum(dvb[h] for h in group)
            dk_ref[cur, cols] += dk_kv[BLOCK:]
            dv_ref[cur, cols] += dv_kv[BLOCK:]

            @pl.when(n > 0)
            def _(cols=cols, dk_kv=dk_kv, dv_kv=dv_kv):
                dk_ref[prev, cols] += dk_kv[:BLOCK]
                dv_ref[prev, cols] += dv_kv[:BLOCK]

    return pl.pallas_call(
        body, grid=(S // BLOCK,),
        in_specs=_swa_in_specs() + [pl.BlockSpec((BLOCK, SWA_WIDTH), lambda n: (n, 0)),
                                    pl.BlockSpec(memory_space=pl.ANY)],
        out_specs=[pl.BlockSpec((BLOCK, SWA_WIDTH), lambda n: (n, P_SQ // SWA_WIDTH)), _full((S, SWA_KVW)),
                   _full((S, SWA_KVW)), _full((1, SWA_DIM)), _full((1, SWA_DIM)), _full((1, SWA_HEADS)),
                   _full((SWA_HEADS, BLOCK, 2 * BLOCK))],
        out_shape=[jax.ShapeDtypeStruct(dproj.shape, dproj.dtype), jax.ShapeDtypeStruct((S, SWA_KVW), F32),
                   jax.ShapeDtypeStruct((S, SWA_KVW), F32), jax.ShapeDtypeStruct((1, SWA_DIM), F32),
                   jax.ShapeDtypeStruct((1, SWA_DIM), F32), jax.ShapeDtypeStruct((1, SWA_HEADS), F32),
                   jax.ShapeDtypeStruct((SWA_HEADS, BLOCK, 2 * BLOCK), F32)],
        input_output_aliases={10: 0},
        name="swa_bwd", compiler_params=_params(("arbitrary",)),
    )(proj, proj, proj, proj, proj, q_gain, k_gain, sinks, bias, dy, dproj)


def _kv_into(dproj, dk, dv, tm=1024):
    S = dk.shape[0]

    def body(dk_ref, dv_ref, _, o_ref):
        o_ref[:, :SWA_KVW] = dk_ref[...].astype(BF16)
        o_ref[:, SWA_KVW:] = dv_ref[...].astype(BF16)

    return pl.pallas_call(
        body, grid=(S // tm,), in_specs=[_row(tm, SWA_KVW), _row(tm, SWA_KVW), pl.BlockSpec(memory_space=pl.ANY)],
        out_specs=_row(tm, 2 * SWA_KVW, P_SK // (2 * SWA_KVW)),
        out_shape=jax.ShapeDtypeStruct(dproj.shape, dproj.dtype), input_output_aliases={2: 0},
        name="swa_kv_into", compiler_params=_params(("parallel",)))(dk, dv, dproj)


def _position():
    return lax.axis_index("x"), lax.axis_index("y"), lax.axis_index("c")


def _all_gather(shards, name="all_gather_weights"):
    na = len(shards)

    def body(*refs):
        x_refs, out_refs = refs[:na], refs[na:2 * na]
        send_sems, recv_sems, local_sems = refs[2 * na:]
        x, y, c = _position()
        me, sibling = (x, y, c), (x, y, 1 - c)
        chips = [(1 - x, y), (x, 1 - y), (1 - x, 1 - y)]

        def copy(a, k, block, to, own=False):
            px, py, pc = block
            slot = out_refs[a].at[4 * px + 2 * py + pc]
            return pltpu.make_async_remote_copy(
                src_ref=x_refs[a] if own else slot, dst_ref=slot, send_sem=send_sems.at[7 * a + k],
                recv_sem=recv_sems.at[7 * a + k], device_id=to, device_id_type=MESH_ID)

        mine = [pltpu.make_async_copy(x_refs[a], out_refs[a].at[4 * x + 2 * y + c], local_sems.at[a])
                for a in range(na)]
        for cp in mine:
            cp.start()
        first = []
        for a in range(na):
            first.append(copy(a, 0, me, sibling, own=True))
            first += [copy(a, 1 + j, me, (*chip, c), own=True) for j, chip in enumerate(chips)]
        for cp in first:
            cp.start()
        passed = []
        for j, chip in enumerate(chips):
            for a in range(na):
                copy(a, 1 + j, (*chip, c), me).wait_recv()
                passed.append(copy(a, 4 + j, (*chip, c), sibling))
                passed[-1].start()
        for a in range(na):
            copy(a, 0, sibling, me).wait_recv()
            for j, chip in enumerate(chips):
                copy(a, 4 + j, (*chip, 1 - c), me).wait_recv()
        for cp in first + passed:
            cp.wait_send()
        for cp in mine:
            cp.wait()

    return pl.pallas_call(
        body, in_specs=[pl.BlockSpec(memory_space=pl.ANY)] * na, out_specs=[pl.BlockSpec(memory_space=pl.ANY)] * na,
        out_shape=[jax.ShapeDtypeStruct((N_DEV,) + s.shape, s.dtype) for s in shards],
        scratch_shapes=[pltpu.SemaphoreType.DMA((7 * na,)), pltpu.SemaphoreType.DMA((7 * na,)),
                        pltpu.SemaphoreType.DMA((na,))],
        name=name)(*shards)


_HBM = pl.BlockSpec(memory_space=pltpu.HBM)
_SEM = pl.BlockSpec(memory_space=pltpu.SEMAPHORE)
_DATAFLOW = pltpu.SideEffectType.DATAFLOW_SIDE_EFFECTING


def _peers(x, y, c):
    out = []
    for k in range(1, N_DEV):
        px, py, pc = x ^ (k >> 2), y ^ ((k >> 1) & 1), c ^ (k & 1)
        out.append(((px, py, pc), 4 * px + 2 * py + pc))
    return out


def _split_copies(src_refs, land_refs, send_sems, recv_sems, scatter):
    x, y, c = _position()
    me = 4 * x + 2 * y + c
    sends, recvs = [], []
    for k, (peer_id, peer) in enumerate(_peers(x, y, c)):
        for a, (src, land) in enumerate(zip(src_refs, land_refs)):
            sems = dict(send_sem=send_sems.at[7 * a + k], recv_sem=recv_sems.at[7 * a + k],
                        device_id=peer_id, device_id_type=MESH_ID)
            mine = src.at[peer] if scatter else src
            sends.append(pltpu.make_async_remote_copy(src_ref=mine, dst_ref=land.at[me], **sems))
            recvs.append(pltpu.make_async_remote_copy(src_ref=mine, dst_ref=land.at[peer], **sems))
    return sends, recvs


def _all_gather_direct(shards, name, after):
    na, nb = len(shards), len(after)

    def body(*refs):
        x_refs, out_refs = refs[:na], refs[na + nb:2 * na + nb]
        send_sems, recv_sems, local_sems = refs[2 * na + nb:]
        x, y, c = _position()
        me = 4 * x + 2 * y + c
        local = [pltpu.make_async_copy(x_refs[a], out_refs[a].at[me], local_sems.at[a]) for a in range(na)]
        sends, recvs = _split_copies(x_refs, out_refs, send_sems, recv_sems, False)
        for cp in local + sends:
            cp.start()
        for cp in recvs:
            cp.wait_recv()
        for cp in sends:
            cp.wait_send()
        for cp in local:
            cp.wait()

    return pl.pallas_call(
        body, in_specs=[pl.BlockSpec(memory_space=pl.ANY)] * (na + nb),
        out_specs=[pl.BlockSpec(memory_space=pl.ANY)] * na,
        out_shape=[jax.ShapeDtypeStruct((N_DEV,) + s.shape, s.dtype) for s in shards],
        scratch_shapes=[pltpu.SemaphoreType.DMA((7 * na,)), pltpu.SemaphoreType.DMA((7 * na,)),
                        pltpu.SemaphoreType.DMA((na,))],
        name=name)(*shards, *after)


def _exchange_start(srcs, scatter, name, after=None):
    na = len(srcs)
    lands = [lax.empty(s.shape if scatter else (N_DEV,) + s.shape, s.dtype) for s in srcs]
    extra = [] if after is None else [after]

    def body(*refs):
        src_refs, land_refs = refs[:na], refs[na:2 * na]
        send_sems, recv_sems = refs[2 * na + len(extra)], refs[2 * na + len(extra) + 1]
        token = refs[-1]
        sends, _ = _split_copies(src_refs, land_refs, send_sems, recv_sems, scatter)
        for cp in sends:
            cp.start()
        token[...] = jnp.zeros_like(token)

    hbm = lambda a: pltpu.HBM(a.shape, a.dtype)
    out = pl.pallas_call(
        body, name=name,
        out_shape=(pltpu.SemaphoreType.DMA((7 * na,)), pltpu.SemaphoreType.DMA((7 * na,)),
                   *[hbm(s) for s in srcs], *[hbm(l) for l in lands], jax.ShapeDtypeStruct((8, LANES), F32)),
        in_specs=[_HBM] * (2 * na) + [pl.BlockSpec(memory_space=pl.ANY)] * len(extra),
        out_specs=(_SEM, _SEM, *[_HBM] * (2 * na), pl.BlockSpec(memory_space=pltpu.VMEM)),
        input_output_aliases={i: 2 + i for i in range(2 * na)},
        compiler_params=pltpu.CompilerParams(has_side_effects=_DATAFLOW),
    )(*[pltpu.with_memory_space_constraint(s, pltpu.HBM) for s in srcs],
      *[pltpu.with_memory_space_constraint(l, pltpu.HBM) for l in lands], *extra)
    return (out[0], out[1], list(out[2:2 + na]), list(out[2 + na:2 + 2 * na])), out[-1]


def _exchange_wait(handle, after, scatter, name):
    send_sems, recv_sems, srcs, lands = handle
    na = len(srcs)

    def body(*refs):
        src_refs, land_refs = refs[:na], refs[na:2 * na]
        s_sems, r_sems = refs[2 * na], refs[2 * na + 1]
        sends, recvs = _split_copies(src_refs, land_refs, s_sems, r_sems, scatter)
        for cp in sends:
            cp.wait_send()
        for cp in recvs:
            cp.wait_recv()

    hbm = lambda a: pltpu.HBM(a.shape, a.dtype)
    out = pl.pallas_call(
        body, name=name, out_shape=(*[hbm(s) for s in srcs], *[hbm(l) for l in lands]),
        in_specs=[_HBM] * (2 * na) + [_SEM, _SEM, pl.BlockSpec(memory_space=pl.ANY)],
        out_specs=tuple([_HBM] * (2 * na)), input_output_aliases={i: i for i in range(2 * na)},
        compiler_params=pltpu.CompilerParams(has_side_effects=_DATAFLOW),
    )(*srcs, *lands, send_sems, recv_sems, after)
    return list(out[:na]), list(out[na:])


def _own_slot(landed, own):
    me = 4 * lax.axis_index("x") + 2 * lax.axis_index("y") + lax.axis_index("c")
    return lax.dynamic_update_slice_in_dim(landed, own[None], me, axis=0)


def _adam_update(parts, w, m, v, name, tr=256, turned=False):
    _, r, c = w.shape
    tr = _pick_rows(r, tr)
    cp = parts.shape[2]
    flat = turned and c % 8 != 0
    at = (slice(None), 0) if flat else (0,)

    def body(p_ref, w_ref, m_ref, v_ref, g_ref, d_ref, nm_ref, nv_ref):
        cols = pl.ds(0, cp if turned else c)
        g = p_ref[0, :, cols].astype(F32)
        for i in range(1, N_DEV):
            g = g + p_ref[i, :, cols].astype(F32)
        if turned:
            g = g.T[:c]
        delta, nm, nv = _adamw(w_ref[at], g, m_ref[at], v_ref[at])
        g_ref[at] = g
        d_ref[at] = delta
        nm_ref[at] = nm
        nv_ref[at] = nv

    there, back = ((2, 0, 1), (1, 2, 0)) if flat else ((0, 2, 1), (0, 2, 1))
    if turned:
        w, m, v = (jnp.transpose(a, there) for a in (w, m, v))
        rs = pl.BlockSpec((c, 1, tr), lambda i: (0, 0, i)) if flat else pl.BlockSpec((1, c, tr), lambda i: (0, 0, i))
    else:
        rs = pl.BlockSpec((1, tr, c), lambda i: (0, i, 0))
    outs = pl.pallas_call(
        body, grid=(r // tr,), in_specs=[pl.BlockSpec((N_DEV, tr, cp), lambda i: (0, i, 0)), rs, rs, rs],
        out_specs=[rs] * 4, out_shape=[jax.ShapeDtypeStruct(w.shape, F32)] * 4, name=name,
        compiler_params=_params(("parallel",)))(parts, w, m, v)
    return [*([jnp.transpose(o, back) for o in outs] if turned else outs), outs[0]]


def _pick_rows(rows, target):
    if rows <= target:
        return rows
    t = target
    while t >= 16:
        if rows % t == 0:
            return t
        t -= 16
    return rows


BIG = ("w_in", "w_branch_dn", "w_branch_swa", "w_out", "w_gate", "w_up", "w_down")
IN_SHARD, IN_WIRE = D_IN // N_DEV, 640
FF_SHARD, FF_WIRE = D_FF // N_DEV, 384
D_FFP = N_DEV * FF_WIRE
BIG_SHAPES = {"w_in": ((D_MODEL, IN_SHARD), (D_MODEL, IN_WIRE)),
              "w_branch_dn": ((DN_WIDTH, LANES), (DN_WIDTH, LANES)),
              "w_branch_swa": ((SWA_WIDTH, LANES), (SWA_WIDTH, LANES)),
              "w_out": ((LANES, D_MODEL), (LANES, D_MODEL)),
              "w_gate": ((D_MODEL, FF_SHARD), (D_MODEL, FF_WIRE)),
              "w_up": ((D_MODEL, FF_SHARD), (D_MODEL, FF_WIRE)),
              "w_down": ((FF_SHARD, D_MODEL), (FF_WIRE, D_MODEL))}
CONV_SHARD, CONV_WIRE = (DN_CONV, DN_QKV // N_DEV), (8, 256)


def _pad_to(a, shape):
    return jnp.pad(a, [(0, t - s) for s, t in zip(a.shape, shape)])


IN_TILE_ROWS = 256
_IN_SEGS = ((R_GATE, 2048, P_GATE), (R_QKV, DN_QKV, P_QKV), (R_Z, DN_WIDTH, P_Z), (R_SQ, SWA_WIDTH, P_SQ),
            (R_SK, SWA_KVW, P_SK), (R_SV, SWA_KVW, P_SV), (R_B, 8, P_BA))


def _w_in_from_blocks(blocks):
    tm = IN_TILE_ROWS

    def body(b_ref, o_ref):
        parts = []
        for rs, n, _ in _IN_SEGS:
            for dev in range(N_DEV):
                lo, hi = max(rs, IN_SHARD * dev), min(rs + n, IN_SHARD * (dev + 1))
                if lo < hi:
                    parts.append(b_ref[dev][:, lo - IN_SHARD * dev:hi - IN_SHARD * dev])
        parts.append(jnp.zeros((tm, P_WIDTH - P_BA - 8), b_ref.dtype))
        o_ref[...] = jnp.concatenate(parts, axis=1)

    return pl.pallas_call(
        body, grid=(D_MODEL // tm,), in_specs=[pl.BlockSpec((N_DEV, tm, IN_WIRE), lambda i: (0, i, 0))],
        out_specs=pl.BlockSpec((tm, P_WIDTH), lambda i: (i, 0)),
        out_shape=jax.ShapeDtypeStruct((D_MODEL, P_WIDTH), blocks.dtype), name="w_in_from_blocks",
        compiler_params=_params(("parallel",)))(blocks)


def _w_in_to_blocks(g):
    tm = IN_TILE_ROWS

    def body(g_ref, o_ref):
        for dev in range(N_DEV):
            parts = []
            for rs, n, ps in sorted(_IN_SEGS):
                lo, hi = max(rs, IN_SHARD * dev), min(rs + n, IN_SHARD * (dev + 1))
                if lo < hi:
                    parts.append(g_ref[:, ps + lo - rs:ps + hi - rs])
            parts.append(jnp.zeros((tm, IN_WIRE - IN_SHARD), g_ref.dtype))
            o_ref[dev] = jnp.concatenate(parts, axis=1)

    return pl.pallas_call(
        body, grid=(D_MODEL // tm,), in_specs=[pl.BlockSpec((tm, P_WIDTH), lambda i: (i, 0))],
        out_specs=pl.BlockSpec((N_DEV, tm, IN_WIRE), lambda i: (0, i, 0)),
        out_shape=jax.ShapeDtypeStruct((N_DEV, D_MODEL, IN_WIRE), g.dtype), name="w_in_to_blocks",
        compiler_params=_params(("parallel",)))(g)


SMALL = {"attn_norm": (0, (1, D_MODEL)), "ffn_norm": (1, (1, D_MODEL)), "dn_out_norm": (2, (1, DN_DIM)),
         "swa_q_norm": (3, (1, SWA_DIM)), "swa_k_norm": (4, (1, SWA_DIM)), "dn_a_log": (5, (1, DN_HEADS)),
         "dn_dt_bias": (6, (1, DN_HEADS)), "swa_sinks": (7, (1, SWA_HEADS)), "rel_bias": (8, (REL_BUCKETS, SWA_HEADS))}
SMALL_SHEET = (48, D_MODEL)


LOSS_ROW = 40


def _small_pack(grads, loss_local):
    names = list(SMALL)

    def body(*refs):
        o_ref = refs[-1]
        o_ref[...] = jnp.zeros_like(o_ref)
        for n, ref in zip(names, refs):
            r0, (nr, nc) = SMALL[n]
            o_ref[r0:r0 + nr, 0:nc] = ref[...]
        o_ref[LOSS_ROW:LOSS_ROW + 1, 0:1] = refs[len(names)][...]

    return pl.pallas_call(
        body, in_specs=[pl.BlockSpec(memory_space=pltpu.VMEM)] * (len(names) + 1),
        out_specs=pl.BlockSpec(memory_space=pltpu.VMEM), out_shape=jax.ShapeDtypeStruct(SMALL_SHEET, F32),
        name="small_pack", compiler_params=_params())(*[grads[n].reshape(SMALL[n][1]) for n in names], loss_local)


def _small_update(sheets, w, m, v):
    names = list(SMALL)
    k = len(names)

    def body(*refs):
        p_ref = refs[0]
        ins, outs = refs[1:1 + 3 * k], refs[1 + 3 * k:]
        loss = p_ref[0, LOSS_ROW:LOSS_ROW + 1, 0:1]
        for i in range(1, N_DEV):
            loss = loss + p_ref[i, LOSS_ROW:LOSS_ROW + 1, 0:1]
        outs[4 * k][...] = loss
        for t, n in enumerate(names):
            r0, (nr, nc) = SMALL[n]
            g = p_ref[0, r0:r0 + nr, 0:nc]
            for i in range(1, N_DEV):
                g = g + p_ref[i, r0:r0 + nr, 0:nc]
            delta, nm, nv = _adamw(ins[t][...], g, ins[k + t][...], ins[2 * k + t][...])
            for kind, val in enumerate((g, delta, nm, nv)):
                outs[kind * k + t][...] = val

    shapes = [jax.ShapeDtypeStruct(SMALL[n][1], F32) for n in names]
    vm = pl.BlockSpec(memory_space=pltpu.VMEM)
    res = pl.pallas_call(
        body, in_specs=[vm] * (1 + 3 * k), out_specs=[vm] * (4 * k + 1),
        out_shape=shapes * 4 + [jax.ShapeDtypeStruct((1, 1), F32)], name="adam_small", compiler_params=_params(),
    )(sheets, *[d[n].reshape(SMALL[n][1]) for d in (w, m, v) for n in names])
    return {n: tuple(res[kind * k + t] for kind in range(4)) for t, n in enumerate(names)}, res[4 * k]


def kernel(x, attn_norm, w_in, dn_conv, dn_a_log, dn_dt_bias, dn_out_norm, swa_q_norm, swa_k_norm, swa_sinks, rel_bias, w_branch_dn, w_branch_swa, w_out, ffn_norm, w_gate, w_up, w_down, loss_target, m_attn_norm, m_w_in, m_dn_conv, m_dn_a_log, m_dn_dt_bias, m_dn_out_norm, m_swa_q_norm, m_swa_k_norm, m_swa_sinks, m_rel_bias, m_w_branch_dn, m_w_branch_swa, m_w_out, m_ffn_norm, m_w_gate, m_w_up, m_w_down, v_attn_norm, v_w_in, v_dn_conv, v_dn_a_log, v_dn_dt_bias, v_dn_out_norm, v_swa_q_norm, v_swa_k_norm, v_swa_sinks, v_rel_bias, v_w_branch_dn, v_w_branch_swa, v_w_out, v_ffn_norm, v_w_gate, v_w_up, v_w_down):
    args = dict(locals())
    S = x.shape[1]
    xs = x.reshape(S, D_MODEL)
    target = loss_target.reshape(S, D_MODEL)

    w_loc = {n: args[n].reshape(BIG_SHAPES[n][0]) for n in BIG}
    conv_loc = dn_conv.reshape(CONV_SHARD)
    wire = {n: _pad_to(w_loc[n], BIG_SHAPES[n][1]).astype(BF16) for n in BIG}
    first = _all_gather([wire["w_in"], _pad_to(conv_loc, CONV_WIRE)])
    later = [n for n in BIG if n != "w_in"]
    rest_handle, rest_token = _exchange_start([wire[n] for n in later], False, "gather_rest_start", after=first[1])
    w_pad = _w_in_from_blocks(first[0])
    conv_w = jnp.concatenate([first[1][d, :DN_CONV, :CONV_SHARD[1]] for d in range(N_DEV)], axis=1)

    h = _norm_fwd(xs, attn_norm + rest_token[0, 0], "norm1_fwd")
    proj = _mm([(h, w_pad)], "nn", F32, "mm_in", 1024, 1664, j_outer=True)
    qkvn = _dn_conv_fwd(proj, conv_w)
    beta, g = _dn_gate_fwd(proj, dn_a_log, dn_dt_bias)
    u, w, qe, kd, qk, egl, tinv = _dn_prep_fwd(qkvn, g, beta)
    o, states = _dn_scan_fwd(u, w, qe, kd, qk, egl)
    y_dn = _dn_out_fwd(o, proj, dn_out_norm)
    bias = _bias_fwd(rel_bias)
    y_swa = _swa_fwd(proj, swa_q_norm, swa_k_norm, swa_sinks, bias)
    rest_src, rest_land = _exchange_wait(rest_handle, y_swa, False, "gather_rest_wait")
    G = {n: _own_slot(land, src) for n, src, land in zip(later, rest_src, rest_land)}
    w_bdn, w_bswa, w_g, w_u = G["w_branch_dn"], G["w_branch_swa"], G["w_gate"], G["w_up"]
    w_o = G["w_out"].reshape(D_MODEL, D_MODEL)
    w_d = G["w_down"].reshape(D_FFP, D_MODEL)
    gates = [(proj, P_GATE // 512), (proj, (P_GATE + D_MODEL) // 512)]
    a_dn, a_swa, merged = _mm_fused(
        [(y_dn, w_bdn), (y_swa, w_bswa)], "nn", "mm_branch_merge", 1024, 512,
        lambda p, e: (p[0], p[1], _merge(e[0], e[1], p[0], p[1])), gates, (F32, F32, BF16), b_blocks=True)

    def resid_norm(p, e):
        x1 = e[0] + p[0]
        return x1, _rms(x1, e[1])

    x1, h2 = _mm_fused([(merged, w_o)], "nn", "mm_out_norm", 512, D_MODEL, resid_norm,
                       [(xs, 0), (ffn_norm, None)], (F32, BF16))
    gate, up, act = _mm_fused([(h2, w_g), (h2, w_u)], "nn", "mm_gate_up_act", 1024, 768,
                              lambda p, e: (p[0], p[1], _act(p[0], p[1])), [], (F32, F32, BF16),
                              j_outer=True, b_blocks=True)

    def loss_head(p, e):
        diff = e[0] + p[0] - e[1]
        dy = diff * (1.0 / D_MODEL)
        part = jnp.sum(jnp.mean(diff * diff, axis=-1, keepdims=True), axis=0, keepdims=True) * 0.5
        return dy, dy, part

    dy, dy_b, loss_local = _mm_fused([(act, w_d)], "nn", "mm_down_loss", 512, D_MODEL, loss_head,
                                     [(x1, 0), (target, 0)], (F32, BF16), sum_shape=(1, 1))

    def act_bwd(p, e):
        _, vjp = jax.vjp(_act, e[0], e[1])
        return vjp(p[0])

    dgate, dup = _mm_fused([(dy_b, w_d)], "nt", "mm_dact_act", 1024, 768, act_bwd, [(gate, 0), (up, 0)],
                           (BF16, BF16), j_outer=True)
    g_w_down = _mm([(act, dy_b)], "tn", BF16, "mm_dw_down", 768, D_MODEL, j_outer=True)
    g_w_down = g_w_down.reshape(N_DEV, FF_WIRE, D_MODEL)
    g_w_gate = _mm([(h2, dgate)], "tn", BF16, "mm_dw_gate", D_MODEL, 768, out_blocks=True)
    g_w_up = _mm([(h2, dup)], "tn", BF16, "mm_dw_up", D_MODEL, 768, out_blocks=True)
    ffn_handle, ffn_token = _exchange_start([g_w_down, g_w_gate, g_w_up], True, "scatter_ffn_start")

    def norm_bwd(p, e):
        _, vjp = jax.vjp(_rms, e[0], e[2])
        dx, dgain = vjp(sum(p))
        dx = dx + e[1]
        return dx, dx, dgain

    dx1, dx1_b, g_ffn_norm = _mm_fused(
        [(dgate, w_g), (dup, w_u)], "nt", "mm_dh2_norm", 256, D_MODEL, norm_bwd,
        [(x1, 0), (dy, 0), (ffn_norm + ffn_token[0, 0], None)], (F32, BF16), b_blocks=True, sum_shape=(1, D_MODEL))
    def merge_bwd(p, e):
        _, vjp = jax.vjp(_merge, *e)
        dg0, dg1, da_dn, da_swa = vjp(p[0])
        return jnp.concatenate([dg0, dg1], axis=1), da_dn, da_swa

    dproj, da_dn, da_swa = _mm_fused(
        [(dx1_b, w_o)], "nt", "mm_dmerged_merge", 512, D_MODEL, merge_bwd,
        [(proj, P_GATE // D_MODEL), (proj, P_GATE // D_MODEL + 1), (a_dn, 0), (a_swa, 0)], (BF16,) * 3,
        wide_first=(P_WIDTH, 2 * D_MODEL))
    g_w_out = _mm([(merged, dx1_b)], "tn", BF16, "mm_dw_out", 512, D_MODEL, j_outer=True)
    g_w_out = g_w_out.reshape(N_DEV, LANES, D_MODEL)
    dy_dn = _mm([(da_dn, w_bdn)], "nt", F32, "mm_dy_dn", 1024, DN_WIDTH, b_blocks=True)
    dy_swa = _mm([(da_swa, w_bswa)], "nt", F32, "mm_dy_swa", 1024, SWA_WIDTH, b_blocks=True)
    g_w_bdn = _mm([(y_dn, da_dn)], "tn", BF16, "mm_dw_branch_dn", DN_WIDTH, 512, out_blocks=True)
    g_w_bswa = _mm([(y_swa, da_swa)], "tn", BF16, "mm_dw_branch_swa", SWA_WIDTH, 512, out_blocks=True)
    dproj, dsk, dsv, g_q_norm, g_k_norm, g_sinks, dbias = _swa_bwd(proj, swa_q_norm, swa_k_norm, swa_sinks, bias,
                                                                   dy_swa, dproj)
    dproj = _kv_into(dproj, dsk, dsv)
    g_rel_bias = _bias_bwd(dbias)[:, :REL_BUCKETS].T
    mix_handle, mix_token = _exchange_start([g_w_out, g_w_bdn, g_w_bswa], True, "scatter_mix_start")
    do, dproj, g_out_norm = _dn_out_bwd(o, proj, dn_out_norm + mix_token[0, 0], dy_dn, dproj)
    du, dw, dqe, dkd, dqk, degl = _dn_scan_bwd(u, w, qe, kd, qk, egl, states, do)
    dqkvn, dgd, dbeta = _dn_prep_bwd(qkvn, g, beta, tinv, du, dw, dqe, dkd, dqk, degl)
    dproj, dal, ddt = _dn_gate_bwd(proj, dn_a_log, dn_dt_bias, dbeta, dgd, dproj)
    g_a_log = dal.reshape(DN_HEADS, DN_DIM).sum(axis=1)
    g_dt_bias = ddt[0, DN_HEADS:2 * DN_HEADS]
    dproj, g_conv = _dn_conv_bwd(proj, conv_w, dqkvn, dproj)
    g_w_in = _w_in_to_blocks(_mm([(h, dproj)], "tn", BF16, "mm_dw_in", 512, 1664, j_outer=True))
    in_handle, in_token = _exchange_start([g_w_in], True, "scatter_in_start")
    dx, g_attn_norm = _mm_fused(
        [(dproj, w_pad)], "nt", "mm_dh_norm", 512, D_MODEL, lambda p, e: norm_bwd(p, e)[1:],
        [(xs, 0), (dx1, 0), (attn_norm + in_token[0, 0], None)], (F32,), sum_shape=(1, D_MODEL))

    g_small = {"attn_norm": g_attn_norm, "ffn_norm": g_ffn_norm, "rel_bias": g_rel_bias, "dn_out_norm": g_out_norm,
               "swa_q_norm": g_q_norm, "swa_k_norm": g_k_norm, "dn_a_log": g_a_log, "dn_dt_bias": g_dt_bias,
               "swa_sinks": g_sinks}
    me = 4 * lax.axis_index("x") + 2 * lax.axis_index("y") + lax.axis_index("c")
    outs = {}

    def finish(handle, group, name, after):
        srcs, lands = _exchange_wait(handle, after, True, name)
        for n, src, land in zip(group, srcs, lands):
            parts = _own_slot(land, lax.dynamic_index_in_dim(src, me, 0, keepdims=False))
            outs[n] = _adam_update(parts, args[n], args["m_" + n], args["v_" + n], "adam_" + n,
                                   turned=args[n].shape[2] % LANES != 0)

    finish(ffn_handle, ("w_down", "w_gate", "w_up"), "scatter_ffn_wait", dx)
    finish(mix_handle, ("w_out", "w_branch_dn", "w_branch_swa"), "scatter_mix_wait", dx)
    sheets, conv_all = _all_gather_direct([_small_pack(g_small, loss_local), _pad_to(g_conv, (8, DN_QKV))],
                                          "all_gather_small",
                                          after=[outs[n][4] for n in sorted(outs)])
    finish(in_handle, ("w_in",), "scatter_in_wait", sheets)
    conv_parts = lax.dynamic_slice(conv_all, (0, 0, me * CONV_SHARD[1]), (N_DEV,) + CONV_SHARD)
    outs["dn_conv"] = _adam_update(conv_parts, dn_conv, m_dn_conv, v_dn_conv, "adam_dn_conv")
    small_outs, loss = _small_update(sheets, {n: args[n] for n in SMALL}, {n: args["m_" + n] for n in SMALL},
                                     {n: args["v_" + n] for n in SMALL})
    outs.update(small_outs)

    names = ("attn_norm", "w_in", "dn_conv", "dn_a_log", "dn_dt_bias", "dn_out_norm", "swa_q_norm", "swa_k_norm",
             "swa_sinks", "rel_bias", "w_branch_dn", "w_branch_swa", "w_out", "ffn_norm", "w_gate", "w_up", "w_down")
    results = []
    for kind in range(4):
        results += [outs[n][kind].reshape(args[n].shape) for n in names]

    return (loss.reshape(()), dx.reshape(x.shape), *results)
```

```python
import math

import numpy as np
import jax
import jax.numpy as jnp
from jax import lax
from jax.experimental import pallas as pl
from jax.experimental.pallas import tpu as pltpu

F32 = jnp.float32
BF16 = jnp.bfloat16
HI = lax.Precision.HIGHEST

D_MODEL = 1024
DN_HEADS = 4
DN_DIM = 128
DN_WIDTH = 512
DN_QKV = 1536
DN_CONV = 4
CHUNK = 64
SWA_HEADS = 8
SWA_KV = 2
SWA_GROUP = 4
SWA_DIM = 64
SWA_WIDTH = 512
SWA_KVW = 128
WINDOW = 128
BLOCK = 128
REL_BUCKETS = 32
REL_MAX_DIST = 128
D_FF = 2816
D_IN = 4872
EPS = 1e-6
N_DEV = 8

ADAM_LR = 0.001
ADAM_B1 = 0.9
ADAM_B2 = 0.999
ADAM_EPS = 1e-08
ADAM_WD = 0.01
ADAM_STEP = 10

P_GATE, P_QKV, P_Z, P_SQ, P_SK, P_SV, P_BA = 0, 2048, 3584, 4096, 4608, 4736, 4864
P_WIDTH = 4992
R_QKV, R_Z, R_B, R_A, R_SQ, R_SK, R_SV, R_GATE = 0, 1536, 2048, 2052, 2056, 2568, 2696, 2824

VMEM_LIMIT = 56 * 1024 * 1024
LANES = 128
MESH_ID = pl.DeviceIdType.MESH


def _params(sem=None):
    return pltpu.CompilerParams(dimension_semantics=sem, vmem_limit_bytes=VMEM_LIMIT)


def _pick(dim, target):
    if dim <= target:
        return dim
    t = target - target % LANES
    while t >= LANES:
        if dim % t == 0:
            return t
        t -= LANES
    return dim


_DIMS = {"nn": (((1,), (0,)), ((), ())), "nt": (((1,), (1,)), ((), ())), "tn": (((0,), (0,)), ((), ()))}


def _tile_product(a_ref, b_ref, mode, b_blocks):
    a = a_ref[...].astype(BF16)
    b = jnp.concatenate([b_ref[d] for d in range(b_ref.shape[0])], axis=1) if b_blocks else b_ref[...]
    return lax.dot_general(a, b.astype(BF16), _DIMS[mode], preferred_element_type=F32)


def _mm(pairs, mode, out_dtype, name, bm, bn, j_outer=False, b_blocks=False, out_blocks=False):
    a0, b0 = pairs[0]
    cb = b0.shape[2] if b_blocks else None
    b_shape = (b0.shape[1], N_DEV * cb) if b_blocks else b0.shape
    if mode == "nn":
        (M, K), (K2, N) = a0.shape, b_shape
    elif mode == "nt":
        (M, K), (N, K2) = a0.shape, b_shape
    else:
        (K, M), (K2, N) = a0.shape, b_shape
    bm, bn = min(bm, M), min(bn, N)
    assert K == K2 and M % bm == 0 and N % bn == 0, (name, a0.shape, b0.shape, bm, bn)
    co = N // N_DEV
    assert not out_blocks or bn % co == 0
    dims = _DIMS[mode]
    n = len(pairs)

    def body(*refs):
        o_ref = refs[2 * n]
        acc = None
        for t in range(n):
            p = _tile_product(refs[2 * t], refs[2 * t + 1], mode, b_blocks)
            acc = p if acc is None else acc + p
        if out_blocks:
            for d in range(bn // co):
                o_ref[d] = acc[:, d * co:(d + 1) * co].astype(out_dtype)
        else:
            o_ref[...] = acc.astype(out_dtype)

    def ij(f):
        return (lambda j, i: f(i, j)) if j_outer else f

    a_spec = pl.BlockSpec((K, bm), ij(lambda i, j: (0, i))) if mode == "tn" else pl.BlockSpec((bm, K), ij(lambda i, j: (i, 0)))
    if b_blocks and mode == "nt":
        b_spec = pl.BlockSpec((N_DEV, bn, cb), ij(lambda i, j: (0, j, 0)))
    elif b_blocks:
        b_spec = pl.BlockSpec((bn // cb, K, cb), ij(lambda i, j: (j, 0, 0)))
    elif mode == "nt":
        b_spec = pl.BlockSpec((bn, K), ij(lambda i, j: (j, 0)))
    else:
        b_spec = pl.BlockSpec((K, bn), ij(lambda i, j: (0, j)))
    if out_blocks:
        out_spec = pl.BlockSpec((bn // co, bm, co), ij(lambda i, j: (j, i, 0)))
        out_shape = jax.ShapeDtypeStruct((N_DEV, M, co), out_dtype)
    else:
        out_spec = pl.BlockSpec((bm, bn), ij(lambda i, j: (i, j)))
        out_shape = jax.ShapeDtypeStruct((M, N), out_dtype)
    grid = (N // bn, M // bm) if j_outer else (M // bm, N // bn)
    return pl.pallas_call(
        body, grid=grid, in_specs=[a_spec, b_spec] * n, out_specs=out_spec, out_shape=out_shape, name=name,
        compiler_params=_params(("parallel", "parallel")),
    )(*[x for pair in pairs for x in pair])


def _mm_fused(pairs, mode, name, bm, bn, epilogue, extras, out_dtypes, j_outer=False, b_blocks=False,
              sum_shape=None, wide_first=None):
    a0, b0 = pairs[0]
    cb = b0.shape[2] if b_blocks else None
    b_shape = (b0.shape[1], N_DEV * cb) if b_blocks else b0.shape
    if mode == "nn":
        (M, K), (K2, N) = a0.shape, b_shape
    else:
        (M, K), (N, K2) = a0.shape, b_shape
    bm, bn = min(bm, M), min(bn, N)
    assert mode in ("nn", "nt") and K == K2 and M % bm == 0 and N % bn == 0, (name, a0.shape, b0.shape)
    dims = _DIMS[mode]
    n, ne, no = len(pairs), len(extras), len(out_dtypes)

    def body(*refs):
        prods = [_tile_product(refs[2 * t], refs[2 * t + 1], mode, b_blocks) for t in range(n)]
        results = epilogue(prods, [r[...] for r in refs[2 * n:2 * n + ne]])
        out_refs = refs[2 * n + ne:]
        for o_ref, val, dt in zip(out_refs, results, out_dtypes):
            o_ref[...] = val.astype(dt)
        if sum_shape is not None:
            s_ref = out_refs[no]

            @pl.when((pl.program_id(0) == 0) & (pl.program_id(1) == 0))
            def _():
                s_ref[...] = jnp.zeros_like(s_ref)

            s_ref[...] += results[no]

    def ij(f):
        return (lambda j, i: f(i, j)) if j_outer else f

    a_spec = pl.BlockSpec((bm, K), ij(lambda i, j: (i, 0)))
    once = dict(pipeline_mode=pl.Buffered(1)) if bn == N else {}
    if b_blocks and mode == "nt":
        b_spec = pl.BlockSpec((N_DEV, bn, cb), ij(lambda i, j: (0, j, 0)), **once)
    elif b_blocks:
        b_spec = pl.BlockSpec((bn // cb, K, cb), ij(lambda i, j: (j, 0, 0)), **once)
    elif mode == "nt":
        b_spec = pl.BlockSpec((bn, K), ij(lambda i, j: (j, 0)), **once)
    else:
        b_spec = pl.BlockSpec((K, bn), ij(lambda i, j: (0, j)), **once)
    e_specs = [pl.BlockSpec((1, bn), ij(lambda i, j: (0, j))) if first is None
               else pl.BlockSpec((bm, bn), ij(lambda i, j, first=first: (i, first + j))) for _, first in extras]
    tile = pl.BlockSpec((bm, bn), ij(lambda i, j: (i, j)))
    out_specs = [tile] * no
    out_shape = [jax.ShapeDtypeStruct((M, N), dt) for dt in out_dtypes]
    if wide_first is not None:
        assert bn == N
        out_specs[0] = pl.BlockSpec((bm, wide_first[1]), ij(lambda i, j: (i, 0)))
        out_shape[0] = jax.ShapeDtypeStruct((M, wide_first[0]), out_dtypes[0])
    if sum_shape is not None:
        assert sum_shape[1] in (1, bn) and (sum_shape[1] == 1 or bn == N)
        out_specs.append(_full(sum_shape))
        out_shape.append(jax.ShapeDtypeStruct(sum_shape, F32))
    grid = (N // bn, M // bm) if j_outer else (M // bm, N // bn)
    sem = ("arbitrary", "arbitrary") if sum_shape is not None else ("parallel", "parallel")
    return pl.pallas_call(
        body, grid=grid, in_specs=[a_spec, b_spec] * n + e_specs, out_specs=out_specs, out_shape=out_shape,
        name=name, compiler_params=_params(sem),
    )(*[x for pair in pairs for x in pair], *[arr for arr, _ in extras])


def _rms(x, gain):
    return x * lax.rsqrt(jnp.mean(x * x, axis=-1, keepdims=True) + EPS) * gain


def _silu(x):
    return x * jax.nn.sigmoid(x)


def _act(g, u):
    return _silu(g) * u


def _merge(g0, g1, a_dn, a_swa):
    return jax.nn.sigmoid(g0) * a_dn + jax.nn.sigmoid(g1) * a_swa


def _dn_post(c, is_v, q_scale):
    a = _silu(c)
    rs = lax.rsqrt(jnp.sum(a * a, axis=-1, keepdims=True) + EPS) * q_scale
    return a * jnp.where(is_v, 1.0, rs)


def _dn_out(o, z, gain):
    return _rms(o, gain) * _silu(z)


def _dot(a, b, dims=_DIMS["nn"], hi=False):
    if a.ndim == 3 or b.ndim == 3:
        batch = a.shape[0] if a.ndim == 3 else b.shape[0]
        a = a if a.ndim == 3 else jnp.broadcast_to(a, (batch,) + a.shape)
        b = b if b.ndim == 3 else jnp.broadcast_to(b, (batch,) + b.shape)
        ((ca,), (cb,)), _ = dims
        dims = (((ca + 1,), (cb + 1,)), ((0,), (0,)))
    if hi:
        return lax.dot_general(a, b, dims, precision=HI, preferred_element_type=F32)
    return lax.dot_general(a.astype(BF16), b.astype(BF16), dims, preferred_element_type=F32)


def _pieces(x):
    hi = x.astype(BF16)
    r1 = x - hi.astype(F32)
    mid = r1.astype(BF16)
    return hi, mid, (r1 - mid.astype(F32)).astype(BF16)


def _sel_left_impl(m, x):
    mb = m.astype(BF16)
    hi, mid, lo = _pieces(x)
    return _dot(mb, hi) + (_dot(mb, mid) + _dot(mb, lo))


@jax.custom_vjp
def _sel_left(m, mt, x):
    return _sel_left_impl(m, x)


_sel_left.defvjp(lambda m, mt, x: (_sel_left_impl(m, x), (m, mt)),
                 lambda res, ct: (jnp.zeros_like(res[0]), jnp.zeros_like(res[1]), _sel_left_impl(res[1], ct)))


def _sel_right_impl(x, s):
    sb = s.astype(BF16)
    hi, mid, lo = _pieces(x)
    return _dot(hi, sb) + (_dot(mid, sb) + _dot(lo, sb))


@jax.custom_vjp
def _sel_right(x, s, st):
    return _sel_right_impl(x, s)


_sel_right.defvjp(lambda x, s, st: (_sel_right_impl(x, s), (s, st)),
                  lambda res, ct: (_sel_right_impl(ct, res[1]), jnp.zeros_like(res[0]), jnp.zeros_like(res[1])))


def _dot3_impl(a, b):
    a_hi, a_lo, _ = _pieces(a)
    b_hi, b_lo, _ = _pieces(b)
    return _dot(a_hi, b_hi) + (_dot(a_hi, b_lo) + _dot(a_lo, b_hi))


@jax.custom_vjp
def _dot3(a, b):
    return _dot3_impl(a, b)


_dot3.defvjp(lambda a, b: (_dot3_impl(a, b), (a, b)),
             lambda res, ct: (_dot(ct, res[1], _DIMS["nt"]), _dot(res[0], ct, _DIMS["tn"])))


def _inv_impl(a, eye, strict):
    t = eye - a
    p = _dot(a, a)
    for level in range(5):
        t = t + _dot(t, p)
        if level < 4:
            p = _dot(p, p)
    t = t + _dot(t, eye - t - _dot3_impl(a, t))
    return jnp.where(strict > 0.5, t, eye)


@jax.custom_vjp
def _inv_given(a, t):
    return t.astype(F32)


_inv_given.defvjp(lambda a, t: (t.astype(F32), t),
                  lambda t, ct: (-_dot(_dot(t, ct, _DIMS["tn"]), t, _DIMS["nt"]), jnp.zeros_like(t)))


@jax.custom_vjp
def _lanes_join(a, b):
    return jnp.concatenate([a, b], axis=-1)


_lanes_join.defvjp(lambda a, b: (jnp.concatenate([a, b], axis=-1), None),
                   lambda _, ct: (ct[..., :ct.shape[-1] // 2], ct[..., ct.shape[-1] // 2:]))


@jax.custom_vjp
def _lanes_halves(y):
    h = y.shape[-1] // 2
    return y[..., :h], y[..., h:]


_lanes_halves.defvjp(lambda y: ((y[..., :y.shape[-1] // 2], y[..., y.shape[-1] // 2:]), None),
                     lambda _, ct: (jnp.concatenate(ct, axis=-1),))

GROUP = 4
GROUP_ROWS = GROUP * CHUNK


def _block_consts(n):
    ii = lax.broadcasted_iota(jnp.int32, (n, n), 0)
    jj = lax.broadcasted_iota(jnp.int32, (n, n), 1)
    shift = CHUNK.bit_length() - 1
    same = jnp.right_shift(ii, shift) == jnp.right_shift(jj, shift)
    return same & (ii >= jj), same & (ii <= jj), same & (ii > jj), same, ii == jj


def _lane0(n):
    s = (lax.broadcasted_iota(jnp.int32, (LANES, n), 0) == 0).astype(F32)
    st = (lax.broadcasted_iota(jnp.int32, (n, LANES), 1) == 0).astype(F32)
    return s, st


def _dn_group(q, k, v, g, beta, t_saved=None):
    n = GROUP_ROWS
    low_b, upp_b, strict_b, _, eye_b = _block_consts(n)
    low, upp, eye = low_b.astype(F32), upp_b.astype(F32), eye_b.astype(F32)
    gc = _sel_left(low, upp, g)
    per_chunk = (g.shape[0], GROUP, CHUNK, LANES)
    g_last = jnp.sum(g.reshape(per_chunk), axis=2, keepdims=True)
    gl = jnp.broadcast_to(g_last, per_chunk).reshape(g.shape)
    s, st = _lane0(n)
    col = _sel_right(gc, s, st)
    row = jnp.swapaxes(col, 1, 2)
    decay = jnp.exp(jnp.where(low_b, col - row, -jnp.inf))
    kb = k * beta
    vb = v * beta
    a = jnp.where(strict_b, _dot(kb, k, _DIMS["nt"]) * decay, 0.0)
    t = _inv_impl(a, eye, strict_b.astype(F32)) if t_saved is None else _inv_given(a, t_saved)
    u, w = _lanes_halves(_dot3(t, _lanes_join(vb, kb * jnp.exp(gc))))
    fold = (jnp.bitwise_and(lax.broadcasted_iota(jnp.int32, (n, CHUNK), 0), CHUNK - 1)
            == lax.broadcasted_iota(jnp.int32, (n, CHUNK), 1)).astype(F32)
    fold_t = (jnp.bitwise_and(lax.broadcasted_iota(jnp.int32, (CHUNK, n), 1), CHUNK - 1)
              == lax.broadcasted_iota(jnp.int32, (CHUNK, n), 0)).astype(F32)
    qk = _sel_right(_dot(q, k, _DIMS["nt"]) * decay, fold, fold_t)
    return u, w, q * jnp.exp(gc), k * jnp.exp(gl - gc), qk, jnp.exp(g_last), t


def _dn_step(s, u, w, qe, kd, qk, egl):
    v_new = u - _dot(w, s)
    o = _dot(qe, s) + _dot(qk, v_new)
    s_new = s * egl + _dot(kd, v_new, _DIMS["tn"])
    return s_new, o


def _swa_block(q, kband, vband, qg, kg, sinks, band):
    kn = _rms(kband, kg)
    qn = _rms(q, qg) * (SWA_DIM ** -0.5)
    logits = _dot(qn, kn, _DIMS["nt"]) + band
    m = lax.stop_gradient(jnp.maximum(jnp.max(logits, axis=-1, keepdims=True), sinks))
    p = jnp.exp(logits - m)
    denom = jnp.sum(p, axis=-1, keepdims=True) + jnp.exp(sinks - m)
    return _dot(p * (1.0 / denom), vband)


def _adamw(w, g, m, v):
    m = ADAM_B1 * m + (1.0 - ADAM_B1) * g
    v = ADAM_B2 * v + (1.0 - ADAM_B2) * jnp.square(g)
    m_hat = m / (1.0 - ADAM_B1 ** ADAM_STEP)
    v_hat = v / (1.0 - ADAM_B2 ** ADAM_STEP)
    delta = -ADAM_LR * (m_hat / (jnp.sqrt(v_hat) + ADAM_EPS) + ADAM_WD * w)
    return delta, m, v


def _row(tm, c, cb=0):
    return pl.BlockSpec((tm, c), lambda i, cb=cb: (i, cb))


def _full(shape):
    nd = len(shape)
    return pl.BlockSpec(shape, lambda *_, nd=nd: (0,) * nd)


def _norm_fwd(x, gain, name, tm=1024):
    S = x.shape[0]

    def body(x_ref, g_ref, h_ref):
        h_ref[...] = _rms(x_ref[...], g_ref[...]).astype(BF16)

    return pl.pallas_call(
        body, grid=(S // tm,), in_specs=[_row(tm, D_MODEL), _full((1, D_MODEL))],
        out_specs=_row(tm, D_MODEL), out_shape=jax.ShapeDtypeStruct((S, D_MODEL), BF16),
        name=name, compiler_params=_params(("parallel",)))(x, gain)


def _shift_down(x, s):
    row = lax.broadcasted_iota(jnp.int32, x.shape, 0)
    return jnp.where(row >= s, pltpu.roll(x, s, axis=0), 0.0)


def _shift_up(x, s):
    n = x.shape[0]
    row = lax.broadcasted_iota(jnp.int32, x.shape, 0)
    return jnp.where(row < n - s, pltpu.roll(x, n - s, axis=0), 0.0)


def _conv(x, w):
    out = w[DN_CONV - 1:DN_CONV] * x
    for s in range(1, DN_CONV):
        out = out + w[DN_CONV - 1 - s:DN_CONV - s] * _shift_down(x, s)
    return out


def _dn_conv_fwd(proj, conv_w):
    S = proj.shape[0]
    nb = DN_QKV // LANES

    def body(x_ref, w_ref, o_ref):
        j = pl.program_id(0)
        q_scale = jnp.where(j < DN_HEADS, DN_DIM ** -0.5, 1.0).astype(F32)
        o_ref[...] = _dn_post(_conv(x_ref[...], w_ref[...]), j >= 2 * DN_HEADS, q_scale)

    return pl.pallas_call(
        body, grid=(nb,),
        in_specs=[pl.BlockSpec((S, LANES), lambda j: (0, P_QKV // LANES + j)),
                  pl.BlockSpec((DN_CONV, LANES), lambda j: (0, j))],
        out_specs=pl.BlockSpec((S, LANES), lambda j: (0, j)),
        out_shape=jax.ShapeDtypeStruct((S, DN_QKV), F32), name="dn_conv_fwd",
        compiler_params=_params(("parallel",)))(proj, conv_w)


def _dn_conv_bwd(proj, conv_w, dqkvn, dproj):
    S = proj.shape[0]
    nb = DN_QKV // LANES

    def body(x_ref, w_ref, d_ref, _, dx_ref, dw_ref):
        j = pl.program_id(0)
        q_scale = jnp.where(j < DN_HEADS, DN_DIM ** -0.5, 1.0).astype(F32)
        x = x_ref[...]
        w = w_ref[...]
        _, vjp = jax.vjp(lambda c: _dn_post(c, j >= 2 * DN_HEADS, q_scale), _conv(x, w))
        (dc,) = vjp(d_ref[0])
        dx = w[DN_CONV - 1:DN_CONV] * dc
        dw_ref[DN_CONV - 1:DN_CONV, :] = jnp.sum(dc * x, axis=0, keepdims=True)
        for s in range(1, DN_CONV):
            dx = dx + w[DN_CONV - 1 - s:DN_CONV - s] * _shift_up(dc, s)
            dw_ref[DN_CONV - 1 - s:DN_CONV - s, :] = jnp.sum(dc * _shift_down(x, s), axis=0, keepdims=True)
        dx_ref[...] = dx.astype(BF16)

    return pl.pallas_call(
        body, grid=(nb,),
        in_specs=[pl.BlockSpec((S, LANES), lambda j: (0, P_QKV // LANES + j)),
                  pl.BlockSpec((DN_CONV, LANES), lambda j: (0, j)),
                  pl.BlockSpec((1, S, LANES), lambda j: (lax.div(j, DN_HEADS), 0, lax.rem(j, DN_HEADS))),
                  pl.BlockSpec(memory_space=pl.ANY)],
        out_specs=[pl.BlockSpec((S, LANES), lambda j: (0, P_QKV // LANES + j)),
                   pl.BlockSpec((DN_CONV, LANES), lambda j: (0, j))],
        out_shape=[jax.ShapeDtypeStruct(dproj.shape, dproj.dtype), jax.ShapeDtypeStruct((DN_CONV, DN_QKV), F32)],
        input_output_aliases={3: 0},
        name="dn_conv_bwd", compiler_params=_params(("parallel",)))(proj, conv_w, dqkvn, dproj)


def _expanders():
    eb = np.zeros((LANES, DN_WIDTH), np.float32)
    ea = np.zeros((LANES, DN_WIDTH), np.float32)
    for h in range(DN_HEADS):
        eb[h, h * DN_DIM:(h + 1) * DN_DIM] = 1.0
        ea[DN_HEADS + h, h * DN_DIM:(h + 1) * DN_DIM] = 1.0
    return jnp.asarray(eb), jnp.asarray(ea), jnp.asarray(eb.T), jnp.asarray(ea.T)


def _dn_gate_args(a_log, dt_bias):
    alog = jnp.repeat(a_log.reshape(1, DN_HEADS), DN_DIM, axis=1)
    dtb = _pad_to(jnp.pad(dt_bias.reshape(1, DN_HEADS), ((0, 0), (DN_HEADS, 0))), (1, LANES))
    return _expanders() + (alog, dtb)


def _dn_gate_specs(tm):
    return [_row(tm, LANES, P_BA // LANES), _full((LANES, DN_WIDTH)), _full((LANES, DN_WIDTH)),
            _full((DN_WIDTH, LANES)), _full((DN_WIDTH, LANES)), _full((1, DN_WIDTH)), _full((1, LANES))]


def _dn_gate_fn(ba, eb, ea, ebt, eat, alog, dtb):
    beta = _sel_right(jax.nn.sigmoid(ba), eb, ebt)
    g = -jnp.exp(alog) * _sel_right(jax.nn.softplus(ba + dtb), ea, eat)
    return beta, g


def _dn_gate_fwd(proj, a_log, dt_bias, tm=1024):
    S = proj.shape[0]
    args = _dn_gate_args(a_log, dt_bias)

    def body(ba_ref, eb_ref, ea_ref, ebt_ref, eat_ref, al_ref, dt_ref, beta_ref, g_ref):
        beta, g = _dn_gate_fn(ba_ref[...], eb_ref[...], ea_ref[...], ebt_ref[...], eat_ref[...], al_ref[...],
                              dt_ref[...])
        beta_ref[...] = beta
        g_ref[...] = g

    return pl.pallas_call(
        body, grid=(S // tm,), in_specs=_dn_gate_specs(tm), out_specs=[_row(tm, DN_WIDTH), _row(tm, DN_WIDTH)],
        out_shape=[jax.ShapeDtypeStruct((S, DN_WIDTH), F32), jax.ShapeDtypeStruct((S, DN_WIDTH), F32)],
        name="dn_gate_fwd", compiler_params=_params(("parallel",)))(proj, *args)


def _dn_gate_bwd(proj, a_log, dt_bias, dbeta, dg, dproj, tm=1024):
    S = proj.shape[0]
    args = _dn_gate_args(a_log, dt_bias)

    def body(ba_ref, eb_ref, ea_ref, ebt_ref, eat_ref, al_ref, dt_ref, dbeta_ref, dg_ref, _, dba_ref, dal_ref,
             ddt_ref):
        eb, ea, ebt, eat = eb_ref[...], ea_ref[...], ebt_ref[...], eat_ref[...]
        _, vjp = jax.vjp(lambda ba, al, dt: _dn_gate_fn(ba, eb, ea, ebt, eat, al, dt), ba_ref[...], al_ref[...],
                         dt_ref[...])
        dba, dal, ddt = vjp((dbeta_ref[...], dg_ref[...]))
        dba_ref[...] = dba.astype(BF16)

        @pl.when(pl.program_id(0) == 0)
        def _():
            dal_ref[...] = jnp.zeros_like(dal_ref)
            ddt_ref[...] = jnp.zeros_like(ddt_ref)

        dal_ref[...] += dal
        ddt_ref[...] += ddt

    return pl.pallas_call(
        body, grid=(S // tm,),
        in_specs=_dn_gate_specs(tm) + [_row(tm, DN_WIDTH), _row(tm, DN_WIDTH), pl.BlockSpec(memory_space=pl.ANY)],
        out_specs=[_row(tm, LANES, P_BA // LANES), _full((1, DN_WIDTH)), _full((1, LANES))],
        out_shape=[jax.ShapeDtypeStruct(dproj.shape, dproj.dtype), jax.ShapeDtypeStruct((1, DN_WIDTH), F32),
                   jax.ShapeDtypeStruct((1, LANES), F32)],
        input_output_aliases={len(args) + 3: 0},
        name="dn_gate_bwd", compiler_params=_params(("arbitrary",)))(proj, *args, dbeta, dg, dproj)


PREP_GROUPS = 8
PREP_CHUNKS = GROUP * PREP_GROUPS


def _dn_prep_specs():
    rows = PREP_CHUNKS * CHUNK
    q = pl.BlockSpec((rows, LANES), lambda h, c: (c, h))
    k = pl.BlockSpec((rows, LANES), lambda h, c: (c, DN_HEADS + h))
    v = pl.BlockSpec((rows, LANES), lambda h, c: (c, 2 * DN_HEADS + h))
    qk = pl.BlockSpec((1, rows, CHUNK), lambda h, c: (h, c, 0))
    egl = pl.BlockSpec((1, PREP_CHUNKS, 1, LANES), lambda h, c: (h, c, 0, 0))
    return q, k, v, qk, egl


def _dn_prep_fwd(qkvn, g, beta):
    S = qkvn.shape[0]
    nc = S // CHUNK
    q, k, v, qks, egl = _dn_prep_specs()

    def body(q_ref, k_ref, v_ref, g_ref, b_ref, u_ref, w_ref, qe_ref, kd_ref, qk_ref, egl_ref, t_ref):
        rows = PREP_CHUNKS * CHUNK
        grp = (PREP_GROUPS, GROUP_ROWS, LANES)
        u, w, qe, kd, qk, e, t = _dn_group(q_ref[...].reshape(grp), k_ref[...].reshape(grp), v_ref[...].reshape(grp),
                                           g_ref[...].reshape(grp), b_ref[...].reshape(grp))
        u_ref[...] = u.reshape(rows, LANES)
        w_ref[...] = w.reshape(rows, LANES)
        qe_ref[...] = qe.reshape(rows, LANES)
        kd_ref[...] = kd.reshape(rows, LANES)
        t_ref[0] = t.reshape(rows, GROUP_ROWS).astype(BF16)
        qk_ref[0] = qk.reshape(rows, CHUNK)
        egl_ref[0] = e.reshape(PREP_CHUNKS, 1, LANES)

    wide = jax.ShapeDtypeStruct((S, DN_WIDTH), F32)
    return pl.pallas_call(
        body, grid=(DN_HEADS, nc // PREP_CHUNKS), in_specs=[q, k, v, q, q],
        out_specs=[q, q, q, q, qks, egl, _dn_tinv_spec()],
        out_shape=[wide, wide, wide, wide, jax.ShapeDtypeStruct((DN_HEADS, S, CHUNK), F32),
                   jax.ShapeDtypeStruct((DN_HEADS, nc, 1, LANES), F32),
                   jax.ShapeDtypeStruct((DN_HEADS, S, GROUP_ROWS), BF16)],
        name="dn_prep_fwd", compiler_params=_params(("parallel", "parallel")))(qkvn, qkvn, qkvn, g, beta)


def _dn_tinv_spec():
    return pl.BlockSpec((1, PREP_CHUNKS * CHUNK, GROUP_ROWS), lambda h, c: (h, c, 0))


def _dn_prep_bwd(qkvn, g, beta, tinv, du, dw, dqe, dkd, dqk, degl):
    S = qkvn.shape[0]
    nc = S // CHUNK
    q, k, v, qks, egl = _dn_prep_specs()

    def body(q_ref, k_ref, v_ref, g_ref, b_ref, t_ref, du_ref, dw_ref, dqe_ref, dkd_ref, dqk_ref, degl_ref,
             dqkv_ref, dg_ref, db_ref):
        rows = PREP_CHUNKS * CHUNK
        grp = (PREP_GROUPS, GROUP_ROWS, LANES)
        t_saved = t_ref[0].reshape(PREP_GROUPS, GROUP_ROWS, GROUP_ROWS)
        _, vjp = jax.vjp(lambda *x: _dn_group(*x, t_saved=t_saved)[:6], q_ref[...].reshape(grp),
                         k_ref[...].reshape(grp), v_ref[...].reshape(grp), g_ref[...].reshape(grp),
                         b_ref[...].reshape(grp))
        dq, dk, dv, dg, db = vjp((du_ref[...].reshape(grp), dw_ref[...].reshape(grp), dqe_ref[...].reshape(grp),
                                  dkd_ref[...].reshape(grp), dqk_ref[0].reshape(PREP_GROUPS, GROUP_ROWS, CHUNK),
                                  degl_ref[0].reshape(PREP_GROUPS, GROUP, 1, LANES)))
        dqkv_ref[0] = dq.reshape(rows, LANES)
        dqkv_ref[1] = dk.reshape(rows, LANES)
        dqkv_ref[2] = dv.reshape(rows, LANES)
        dg_ref[...] = dg.reshape(rows, LANES)
        db_ref[...] = db.reshape(rows, LANES)

    wide = jax.ShapeDtypeStruct((S, DN_WIDTH), F32)
    rows = PREP_CHUNKS * CHUNK
    return pl.pallas_call(
        body, grid=(DN_HEADS, nc // PREP_CHUNKS), in_specs=[q, k, v, q, q, _dn_tinv_spec(), q, q, q, q, qks, egl],
        out_specs=[pl.BlockSpec((3, rows, LANES), lambda h, c: (0, c, h)), q, q],
        out_shape=[jax.ShapeDtypeStruct((3, S, DN_WIDTH), F32), wide, wide],
        name="dn_prep_bwd", compiler_params=_params(("parallel", "parallel")),
    )(qkvn, qkvn, qkvn, g, beta, tinv, du, dw, dqe, dkd, dqk, degl)


SCAN_CHUNKS = 16


def _dn_scan_specs(nc, reverse):
    nb = nc // SCAN_CHUNKS

    def cidx(c):
        return nb - 1 - c if reverse else c

    hc = pl.BlockSpec((SCAN_CHUNKS * CHUNK, DN_WIDTH), lambda c: (cidx(c), 0))
    qk = pl.BlockSpec((DN_HEADS, SCAN_CHUNKS * CHUNK, CHUNK), lambda c: (0, cidx(c), 0))
    egl = pl.BlockSpec((DN_HEADS, SCAN_CHUNKS, 1, LANES), lambda c: (0, cidx(c), 0, 0))
    st = pl.BlockSpec((DN_HEADS, SCAN_CHUNKS, DN_DIM, DN_DIM), lambda c: (0, cidx(c), 0, 0))
    return hc, qk, egl, st


def _heads(ref, i):
    return jnp.stack([ref[pl.ds(i * CHUNK, CHUNK), pl.ds(h * DN_DIM, DN_DIM)] for h in range(DN_HEADS)])


def _dn_scan_fwd(u, w, qe, kd, qk, egl):
    S = u.shape[0]
    nc = S // CHUNK
    hc, qks, egls, st = _dn_scan_specs(nc, False)

    def body(u_ref, w_ref, qe_ref, kd_ref, qk_ref, egl_ref, o_ref, st_ref, s_scr):
        @pl.when(pl.program_id(0) == 0)
        def _():
            s_scr[...] = jnp.zeros_like(s_scr)

        s = s_scr[...]
        for i in range(SCAN_CHUNKS):
            rows = pl.ds(i * CHUNK, CHUNK)
            st_ref[:, i] = s
            s, o = _dn_step(s, _heads(u_ref, i), _heads(w_ref, i), _heads(qe_ref, i), _heads(kd_ref, i),
                            qk_ref[:, rows, :], egl_ref[:, i])
            for h in range(DN_HEADS):
                o_ref[rows, pl.ds(h * DN_DIM, DN_DIM)] = o[h]
        s_scr[...] = s

    return pl.pallas_call(
        body, grid=(nc // SCAN_CHUNKS,), in_specs=[hc, hc, hc, hc, qks, egls], out_specs=[hc, st],
        out_shape=[jax.ShapeDtypeStruct((S, DN_WIDTH), F32), jax.ShapeDtypeStruct((DN_HEADS, nc, DN_DIM, DN_DIM), F32)],
        scratch_shapes=[pltpu.VMEM((DN_HEADS, DN_DIM, DN_DIM), F32)], name="dn_scan_fwd",
        compiler_params=_params(("arbitrary",)))(u, w, qe, kd, qk, egl)


def _dn_scan_bwd(u, w, qe, kd, qk, egl, states, do):
    S = u.shape[0]
    nc = S // CHUNK
    hc, qks, egls, st = _dn_scan_specs(nc, True)

    def body(u_ref, w_ref, qe_ref, kd_ref, qk_ref, egl_ref, st_ref, do_ref,
             du_ref, dw_ref, dqe_ref, dkd_ref, dqk_ref, degl_ref, ds_scr):
        @pl.when(pl.program_id(0) == 0)
        def _():
            ds_scr[...] = jnp.zeros_like(ds_scr)

        ds = ds_scr[...]
        for i in reversed(range(SCAN_CHUNKS)):
            rows = pl.ds(i * CHUNK, CHUNK)
            _, vjp = jax.vjp(_dn_step, st_ref[:, i], _heads(u_ref, i), _heads(w_ref, i), _heads(qe_ref, i),
                             _heads(kd_ref, i), qk_ref[:, rows, :], egl_ref[:, i])
            ds, du, dw, dqe, dkd, dqk, degl = vjp((ds, _heads(do_ref, i)))
            dqk_ref[:, rows, :] = dqk
            degl_ref[:, i] = degl
            for h in range(DN_HEADS):
                cols = pl.ds(h * DN_DIM, DN_DIM)
                du_ref[rows, cols] = du[h]
                dw_ref[rows, cols] = dw[h]
                dqe_ref[rows, cols] = dqe[h]
                dkd_ref[rows, cols] = dkd[h]
        ds_scr[...] = ds

    wide = jax.ShapeDtypeStruct((S, DN_WIDTH), F32)
    return pl.pallas_call(
        body, grid=(nc // SCAN_CHUNKS,), in_specs=[hc, hc, hc, hc, qks, egls, st, hc],
        out_specs=[hc, hc, hc, hc, qks, egls],
        out_shape=[wide, wide, wide, wide, jax.ShapeDtypeStruct((DN_HEADS, S, CHUNK), F32),
                   jax.ShapeDtypeStruct((DN_HEADS, nc, 1, LANES), F32)],
        scratch_shapes=[pltpu.VMEM((DN_HEADS, DN_DIM, DN_DIM), F32)], name="dn_scan_bwd",
        compiler_params=_params(("arbitrary",)))(u, w, qe, kd, qk, egl, states, do)


def _dn_out_fwd(o, proj, gain, tm=1024):
    S = o.shape[0]

    def body(o_ref, z_ref, g_ref, y_ref):
        y_ref[...] = _dn_out(o_ref[...], z_ref[...], g_ref[...]).astype(BF16)

    hs = pl.BlockSpec((tm, LANES), lambda i, h: (i, h))
    zs = pl.BlockSpec((tm, LANES), lambda i, h: (i, P_Z // LANES + h))
    return pl.pallas_call(
        body, grid=(S // tm, DN_HEADS), in_specs=[hs, zs, _full((1, DN_DIM))], out_specs=hs,
        out_shape=jax.ShapeDtypeStruct((S, DN_WIDTH), BF16), name="dn_out_fwd",
        compiler_params=_params(("parallel", "parallel")))(o, proj, gain)


_ANY = pl.BlockSpec(memory_space=pl.ANY)


def _dn_out_bwd(o, proj, gain, dy, dproj, tm=1024):
    S = o.shape[0]

    def body(o_ref, z_ref, g_ref, dy_ref, _, do_ref, dz_ref, dg_ref):
        _, vjp = jax.vjp(_dn_out, o_ref[...], z_ref[...], g_ref[...])
        do, dz, dg = vjp(dy_ref[...])
        do_ref[...] = do
        dz_ref[...] = dz.astype(BF16)

        @pl.when((pl.program_id(0) == 0) & (pl.program_id(1) == 0))
        def _():
            dg_ref[...] = jnp.zeros_like(dg_ref)

        dg_ref[...] += dg

    hs = pl.BlockSpec((tm, LANES), lambda i, h: (i, h))
    zs = pl.BlockSpec((tm, LANES), lambda i, h: (i, P_Z // LANES + h))
    return pl.pallas_call(
        body, grid=(S // tm, DN_HEADS), in_specs=[hs, zs, _full((1, DN_DIM)), hs, _ANY],
        out_specs=[hs, zs, _full((1, DN_DIM))],
        out_shape=[jax.ShapeDtypeStruct((S, DN_WIDTH), F32), jax.ShapeDtypeStruct(dproj.shape, dproj.dtype),
                   jax.ShapeDtypeStruct((1, DN_DIM), F32)],
        input_output_aliases={4: 1},
        name="dn_out_bwd", compiler_params=_params(("arbitrary", "arbitrary")))(o, proj, gain, dy, dproj)


def _rel_buckets():
    qi = np.arange(BLOCK)[:, None]
    kj = np.arange(2 * BLOCK)[None, :]
    n = np.maximum(BLOCK + qi - kj, 0)
    max_exact = REL_BUCKETS // 2
    nf = np.maximum(n, 1).astype(np.float32)
    large = max_exact + (np.log(nf / np.float32(max_exact)) / np.float32(math.log(REL_MAX_DIST / max_exact))
                         * np.float32(REL_BUCKETS - max_exact)).astype(np.int32)
    large = np.minimum(large, REL_BUCKETS - 1)
    return np.where(n < max_exact, n, large).astype(np.int32)


def _bias_fwd(rel_bias):
    buckets = jnp.asarray(_rel_buckets())

    def body(rb_ref, bk_ref, o_ref):
        bk = bk_ref[...]
        for h in range(SWA_HEADS):
            acc = jnp.zeros((BLOCK, 2 * BLOCK), F32)
            for b in range(REL_BUCKETS):
                acc = jnp.where(bk == b, rb_ref[b, h], acc)
            for first in range(2):
                o_ref[first, h] = jnp.where(_swa_mask(1 - first), acc, -jnp.inf)

    return pl.pallas_call(
        body, in_specs=[pl.BlockSpec(memory_space=pltpu.SMEM), pl.BlockSpec(memory_space=pltpu.VMEM)],
        out_specs=pl.BlockSpec(memory_space=pltpu.VMEM),
        out_shape=jax.ShapeDtypeStruct((2, SWA_HEADS, BLOCK, 2 * BLOCK), F32), name="swa_bias_fwd",
        compiler_params=_params())(rel_bias, buckets)


def _bias_bwd(dbias):
    buckets = jnp.asarray(_rel_buckets())

    def body(d_ref, bk_ref, o_ref):
        bk = bk_ref[...]
        lane = lax.broadcasted_iota(jnp.int32, (1, LANES), 1)
        for h in range(SWA_HEADS):
            d = d_ref[h]
            row = jnp.zeros((1, LANES), F32)
            for b in range(REL_BUCKETS):
                part = jnp.sum(jnp.where(bk == b, d, 0.0), axis=1, keepdims=True)
                row = jnp.where(lane == b, jnp.sum(part, axis=0, keepdims=True), row)
            o_ref[h:h + 1, :] = row

    return pl.pallas_call(
        body, in_specs=[pl.BlockSpec(memory_space=pltpu.VMEM), pl.BlockSpec(memory_space=pltpu.VMEM)],
        out_specs=pl.BlockSpec(memory_space=pltpu.VMEM),
        out_shape=jax.ShapeDtypeStruct((SWA_HEADS, LANES), F32), name="swa_bias_bwd",
        compiler_params=_params())(dbias, buckets)


def _swa_mask(n):
    qi = lax.broadcasted_iota(jnp.int32, (BLOCK, 2 * BLOCK), 0)
    kj = lax.broadcasted_iota(jnp.int32, (BLOCK, 2 * BLOCK), 1)
    dist = BLOCK + qi - kj
    return (dist >= 0) & (dist < WINDOW) & ((n > 0) | (kj >= BLOCK))


def _swa_in_specs():
    q = pl.BlockSpec((BLOCK, SWA_WIDTH), lambda n: (n, P_SQ // SWA_WIDTH))
    kc = pl.BlockSpec((BLOCK, SWA_KVW), lambda n: (n, P_SK // SWA_KVW))
    kp = pl.BlockSpec((BLOCK, SWA_KVW), lambda n: (jnp.maximum(n - 1, 0), P_SK // SWA_KVW))
    vc = pl.BlockSpec((BLOCK, SWA_KVW), lambda n: (n, P_SV // SWA_KVW))
    vp = pl.BlockSpec((BLOCK, SWA_KVW), lambda n: (jnp.maximum(n - 1, 0), P_SV // SWA_KVW))
    band = pl.BlockSpec((None, SWA_HEADS, BLOCK, 2 * BLOCK), lambda n: (jnp.where(n == 0, 1, 0), 0, 0, 0))
    small = [_full((1, SWA_DIM)), _full((1, SWA_DIM)), _full((1, SWA_HEADS)), band]
    return [q, kp, kc, vp, vc] + small


def _swa_load(q_ref, kp_ref, kc_ref, vp_ref, vc_ref, s_ref):
    q = jnp.stack([q_ref[:, pl.ds(h * SWA_DIM, SWA_DIM)] for h in range(SWA_HEADS)])
    kbands, vbands = [], []
    for kv in range(SWA_KV):
        cols = pl.ds(kv * SWA_DIM, SWA_DIM)
        kbands += [jnp.concatenate([kp_ref[:, cols], kc_ref[:, cols]], axis=0)] * SWA_GROUP
        vbands += [jnp.concatenate([vp_ref[:, cols], vc_ref[:, cols]], axis=0)] * SWA_GROUP
    sinks = jnp.stack([s_ref[:, pl.ds(h, 1)] for h in range(SWA_HEADS)])
    return q, jnp.stack(kbands), jnp.stack(vbands), sinks


def _swa_fwd(proj, q_gain, k_gain, sinks, bias):
    S = proj.shape[0]

    def body(q_ref, kp_ref, kc_ref, vp_ref, vc_ref, qg_ref, kg_ref, s_ref, bias_ref, y_ref):
        q, kband, vband, sk = _swa_load(q_ref, kp_ref, kc_ref, vp_ref, vc_ref, s_ref)
        out = _swa_block(q, kband, vband, qg_ref[...], kg_ref[...], sk, bias_ref[...])
        for h in range(SWA_HEADS):
            y_ref[:, pl.ds(h * SWA_DIM, SWA_DIM)] = out[h].astype(BF16)

    return pl.pallas_call(
        body, grid=(S // BLOCK,), in_specs=_swa_in_specs(),
        out_specs=pl.BlockSpec((BLOCK, SWA_WIDTH), lambda n: (n, 0)),
        out_shape=jax.ShapeDtypeStruct((S, SWA_WIDTH), BF16), name="swa_fwd",
        compiler_params=_params(("parallel",)))(proj, proj, proj, proj, proj, q_gain, k_gain, sinks, bias)


def _swa_bwd(proj, q_gain, k_gain, sinks, bias, dy, dproj):
    S = proj.shape[0]

    def body(q_ref, kp_ref, kc_ref, vp_ref, vc_ref, qg_ref, kg_ref, s_ref, bias_ref, dy_ref, _,
             dq_ref, dk_ref, dv_ref, dqg_ref, dkg_ref, ds_ref, dbias_ref):
        n = pl.program_id(0)

        @pl.when(n == 0)
        def _():
            for r in (dk_ref, dv_ref, dqg_ref, dkg_ref, ds_ref, dbias_ref):
                r[...] = jnp.zeros_like(r)

        cur = pl.ds(pl.multiple_of(n * BLOCK, BLOCK), BLOCK)
        prev = pl.ds(pl.multiple_of(jnp.maximum(n - 1, 0) * BLOCK, BLOCK), BLOCK)
        q, kband, vband, sk = _swa_load(q_ref, kp_ref, kc_ref, vp_ref, vc_ref, s_ref)
        _, vjp = jax.vjp(_swa_block, q, kband, vband, qg_ref[...], kg_ref[...], sk, bias_ref[...])
        dy = jnp.stack([dy_ref[:, pl.ds(h * SWA_DIM, SWA_DIM)] for h in range(SWA_HEADS)])
        dq, dkb, dvb, dqg, dkg, dsk, dbs = vjp(dy)
        for h in range(SWA_HEADS):
            dq_ref[:, pl.ds(h * SWA_DIM, SWA_DIM)] = dq[h].astype(BF16)
            ds_ref[:, pl.ds(h, 1)] += dsk[h]
        dbias_ref[...] += dbs
        dqg_ref[...] += dqg
        dkg_ref[...] += dkg
        for kv in range(SWA_KV):
            cols = pl.ds(kv * SWA_DIM, SWA_DIM)
            group = range(kv * SWA_GROUP, (kv + 1) * SWA_GROUP)
            dk_kv = sum(dkb[h] for h in group)
            dv_kv = sum(dvb[h] for h in group)
            dk_ref[cur, cols] += dk_kv[BLOCK:]
            dv_ref[cur, cols] += dv_kv[BLOCK:]

            @pl.when(n > 0)
            def _(cols=cols, dk_kv=dk_kv, dv_kv=dv_kv):
                dk_ref[prev, cols] += dk_kv[:BLOCK]
                dv_ref[prev, cols] += dv_kv[:BLOCK]

    return pl.pallas_call(
        body, grid=(S // BLOCK,),
        in_specs=_swa_in_specs() + [pl.BlockSpec((BLOCK, SWA_WIDTH), lambda n: (n, 0)),
                                    pl.BlockSpec(memory_space=pl.ANY)],
        out_specs=[pl.BlockSpec((BLOCK, SWA_WIDTH), lambda n: (n, P_SQ // SWA_WIDTH)), _full((S, SWA_KVW)),
                   _full((S, SWA_KVW)), _full((1, SWA_DIM)), _full((1, SWA_DIM)), _full((1, SWA_HEADS)),
                   _full((SWA_HEADS, BLOCK, 2 * BLOCK))],
        out_shape=[jax.ShapeDtypeStruct(dproj.shape, dproj.dtype), jax.ShapeDtypeStruct((S, SWA_KVW), F32),
                   jax.ShapeDtypeStruct((S, SWA_KVW), F32), jax.ShapeDtypeStruct((1, SWA_DIM), F32),
                   jax.ShapeDtypeStruct((1, SWA_DIM), F32), jax.ShapeDtypeStruct((1, SWA_HEADS), F32),
                   jax.ShapeDtypeStruct((SWA_HEADS, BLOCK, 2 * BLOCK), F32)],
        input_output_aliases={10: 0},
        name="swa_bwd", compiler_params=_params(("arbitrary",)),
    )(proj, proj, proj, proj, proj, q_gain, k_gain, sinks, bias, dy, dproj)


def _kv_into(dproj, dk, dv, tm=1024):
    S = dk.shape[0]

    def body(dk_ref, dv_ref, _, o_ref):
        o_ref[:, :SWA_KVW] = dk_ref[...].astype(BF16)
        o_ref[:, SWA_KVW:] = dv_ref[...].astype(BF16)

    return pl.pallas_call(
        body, grid=(S // tm,), in_specs=[_row(tm, SWA_KVW), _row(tm, SWA_KVW), pl.BlockSpec(memory_space=pl.ANY)],
        out_specs=_row(tm, 2 * SWA_KVW, P_SK // (2 * SWA_KVW)),
        out_shape=jax.ShapeDtypeStruct(dproj.shape, dproj.dtype), input_output_aliases={2: 0},
        name="swa_kv_into", compiler_params=_params(("parallel",)))(dk, dv, dproj)


def _position():
    return lax.axis_index("x"), lax.axis_index("y"), lax.axis_index("c")


_HBM = pl.BlockSpec(memory_space=pltpu.HBM)
_SEM = pl.BlockSpec(memory_space=pltpu.SEMAPHORE)
_DATAFLOW = pltpu.SideEffectType.DATAFLOW_SIDE_EFFECTING


def _two_level_copies(x_refs, out_refs, send_sems, recv_sems):
    x, y, c = _position()
    me, sibling = (x, y, c), (x, y, 1 - c)
    chips = [(1 - x, y), (x, 1 - y), (1 - x, 1 - y)]

    def copy(a, k, block, to, own=False):
        px, py, pc = block
        slot = out_refs[a].at[4 * px + 2 * py + pc]
        return pltpu.make_async_remote_copy(
            src_ref=x_refs[a] if own else slot, dst_ref=slot, send_sem=send_sems.at[7 * a + k],
            recv_sem=recv_sems.at[7 * a + k], device_id=to, device_id_type=MESH_ID)

    return copy, me, sibling, chips


def _all_gather_start(shards, name):
    na = len(shards)
    lands = [lax.empty((N_DEV,) + s.shape, s.dtype) for s in shards]

    def body(*refs):
        x_refs, out_refs = refs[:na], refs[na:2 * na]
        send_sems, recv_sems = refs[2 * na], refs[2 * na + 1]
        token, local_sems = refs[-2], refs[-1]
        copy, me, sibling, chips = _two_level_copies(x_refs, out_refs, send_sems, recv_sems)
        x, y, c = me
        mine = [pltpu.make_async_copy(x_refs[a], out_refs[a].at[4 * x + 2 * y + c], local_sems.at[a])
                for a in range(na)]
        for cp in mine:
            cp.start()
        for a in range(na):
            copy(a, 0, me, sibling, own=True).start()
            for j, chip in enumerate(chips):
                copy(a, 1 + j, me, (*chip, c), own=True).start()
        for cp in mine:
            cp.wait()
        token[...] = jnp.zeros_like(token)

    hbm = lambda a: pltpu.HBM(a.shape, a.dtype)
    out = pl.pallas_call(
        body, name=name,
        out_shape=(pltpu.SemaphoreType.DMA((7 * na,)), pltpu.SemaphoreType.DMA((7 * na,)),
                   *[hbm(s) for s in shards], *[hbm(l) for l in lands], jax.ShapeDtypeStruct((8, LANES), F32)),
        in_specs=[_HBM] * (2 * na),
        out_specs=(_SEM, _SEM, *[_HBM] * (2 * na), pl.BlockSpec(memory_space=pltpu.VMEM)),
        scratch_shapes=[pltpu.SemaphoreType.DMA((na,))],
        input_output_aliases={i: 2 + i for i in range(2 * na)},
        compiler_params=pltpu.CompilerParams(has_side_effects=_DATAFLOW),
    )(*[pltpu.with_memory_space_constraint(s, pltpu.HBM) for s in shards],
      *[pltpu.with_memory_space_constraint(l, pltpu.HBM) for l in lands])
    return (out[0], out[1], list(out[2:2 + na]), list(out[2 + na:2 + 2 * na])), out[-1]


def _all_gather_finish(handle, after, name):
    send_sems, recv_sems, srcs, lands = handle
    na = len(srcs)

    def body(*refs):
        x_refs, out_refs = refs[:na], refs[na:2 * na]
        copy, me, sibling, chips = _two_level_copies(x_refs, out_refs, refs[2 * na], refs[2 * na + 1])
        c = me[2]
        for a in range(na):
            copy(a, 0, sibling, me).wait_recv()
            copy(a, 0, me, sibling, own=True).wait_send()
            for j, chip in enumerate(chips):
                copy(a, 1 + j, (*chip, c), me).wait_recv()
                copy(a, 1 + j, me, (*chip, c), own=True).wait_send()

    hbm = lambda a: pltpu.HBM(a.shape, a.dtype)
    out = pl.pallas_call(
        body, name=name, out_shape=(*[hbm(s) for s in srcs], *[hbm(l) for l in lands]),
        in_specs=[_HBM] * (2 * na) + [_SEM, _SEM] + [pl.BlockSpec(memory_space=pl.ANY)] * len(after),
        out_specs=tuple([_HBM] * (2 * na)), input_output_aliases={i: i for i in range(2 * na)},
        compiler_params=pltpu.CompilerParams(has_side_effects=_DATAFLOW),
    )(*srcs, *lands, send_sems, recv_sems, *after)
    return _all_gather_relay(list(out[na:]), name + "_relay")


def _all_gather_relay(lands, name):
    na = len(lands)

    def body(*refs):
        out_refs = refs[na:2 * na]
        send_sems, recv_sems = refs[2 * na:]
        x, y, c = _position()
        chips = [(1 - x, y), (x, 1 - y), (1 - x, 1 - y)]

        def copy(a, j, core):
            px, py = chips[j]
            slot = out_refs[a].at[4 * px + 2 * py + core]
            return pltpu.make_async_remote_copy(
                src_ref=slot, dst_ref=slot, send_sem=send_sems.at[3 * a + j], recv_sem=recv_sems.at[3 * a + j],
                device_id=(x, y, 1 - c), device_id_type=MESH_ID)

        sends = [copy(a, j, c) for a in range(na) for j in range(3)]
        for cp in sends:
            cp.start()
        for a in range(na):
            for j in range(3):
                copy(a, j, 1 - c).wait_recv()
        for cp in sends:
            cp.wait_send()

    return pl.pallas_call(
        body, in_specs=[pl.BlockSpec(memory_space=pl.ANY)] * na, out_specs=[pl.BlockSpec(memory_space=pl.ANY)] * na,
        out_shape=[jax.ShapeDtypeStruct(l.shape, l.dtype) for l in lands],
        input_output_aliases={i: i for i in range(na)},
        scratch_shapes=[pltpu.SemaphoreType.DMA((3 * na,)), pltpu.SemaphoreType.DMA((3 * na,))],
        name=name)(*lands)


def _peers(x, y, c):
    out = []
    for k in range(1, N_DEV):
        px, py, pc = x ^ (k >> 2), y ^ ((k >> 1) & 1), c ^ (k & 1)
        out.append(((px, py, pc), 4 * px + 2 * py + pc))
    return out


def _split_copies(src_refs, land_refs, send_sems, recv_sems, scatter):
    x, y, c = _position()
    me = 4 * x + 2 * y + c
    sends, recvs = [], []
    for k, (peer_id, peer) in enumerate(_peers(x, y, c)):
        for a, (src, land) in enumerate(zip(src_refs, land_refs)):
            sems = dict(send_sem=send_sems.at[7 * a + k], recv_sem=recv_sems.at[7 * a + k],
                        device_id=peer_id, device_id_type=MESH_ID)
            mine = src.at[peer] if scatter else src
            sends.append(pltpu.make_async_remote_copy(src_ref=mine, dst_ref=land.at[me], **sems))
            recvs.append(pltpu.make_async_remote_copy(src_ref=mine, dst_ref=land.at[peer], **sems))
    return sends, recvs


def _all_gather_direct(shards, name, after):
    na, nb = len(shards), len(after)

    def body(*refs):
        x_refs, out_refs = refs[:na], refs[na + nb:2 * na + nb]
        send_sems, recv_sems, local_sems = refs[2 * na + nb:]
        x, y, c = _position()
        me = 4 * x + 2 * y + c
        local = [pltpu.make_async_copy(x_refs[a], out_refs[a].at[me], local_sems.at[a]) for a in range(na)]
        sends, recvs = _split_copies(x_refs, out_refs, send_sems, recv_sems, False)
        for cp in local + sends:
            cp.start()
        for cp in recvs:
            cp.wait_recv()
        for cp in sends:
            cp.wait_send()
        for cp in local:
            cp.wait()

    return pl.pallas_call(
        body, in_specs=[pl.BlockSpec(memory_space=pl.ANY)] * (na + nb),
        out_specs=[pl.BlockSpec(memory_space=pl.ANY)] * na,
        out_shape=[jax.ShapeDtypeStruct((N_DEV,) + s.shape, s.dtype) for s in shards],
        scratch_shapes=[pltpu.SemaphoreType.DMA((7 * na,)), pltpu.SemaphoreType.DMA((7 * na,)),
                        pltpu.SemaphoreType.DMA((na,))],
        name=name)(*shards, *after)


def _exchange_start(srcs, scatter, name, after=None):
    na = len(srcs)
    lands = [lax.empty(s.shape if scatter else (N_DEV,) + s.shape, s.dtype) for s in srcs]
    extra = [] if after is None else [after]

    def body(*refs):
        src_refs, land_refs = refs[:na], refs[na:2 * na]
        send_sems, recv_sems = refs[2 * na + len(extra)], refs[2 * na + len(extra) + 1]
        token = refs[-1]
        sends, _ = _split_copies(src_refs, land_refs, send_sems, recv_sems, scatter)
        for cp in sends:
            cp.start()
        token[...] = jnp.zeros_like(token)

    hbm = lambda a: pltpu.HBM(a.shape, a.dtype)
    out = pl.pallas_call(
        body, name=name,
        out_shape=(pltpu.SemaphoreType.DMA((7 * na,)), pltpu.SemaphoreType.DMA((7 * na,)),
                   *[hbm(s) for s in srcs], *[hbm(l) for l in lands], jax.ShapeDtypeStruct((8, LANES), F32)),
        in_specs=[_HBM] * (2 * na) + [pl.BlockSpec(memory_space=pl.ANY)] * len(extra),
        out_specs=(_SEM, _SEM, *[_HBM] * (2 * na), pl.BlockSpec(memory_space=pltpu.VMEM)),
        input_output_aliases={i: 2 + i for i in range(2 * na)},
        compiler_params=pltpu.CompilerParams(has_side_effects=_DATAFLOW),
    )(*[pltpu.with_memory_space_constraint(s, pltpu.HBM) for s in srcs],
      *[pltpu.with_memory_space_constraint(l, pltpu.HBM) for l in lands], *extra)
    return (out[0], out[1], list(out[2:2 + na]), list(out[2 + na:2 + 2 * na])), out[-1]


def _exchange_wait(handle, after, scatter, name):
    send_sems, recv_sems, srcs, lands = handle
    na = len(srcs)

    def body(*refs):
        src_refs, land_refs = refs[:na], refs[na:2 * na]
        s_sems, r_sems = refs[2 * na], refs[2 * na + 1]
        sends, recvs = _split_copies(src_refs, land_refs, s_sems, r_sems, scatter)
        for cp in sends:
            cp.wait_send()
        for cp in recvs:
            cp.wait_recv()

    hbm = lambda a: pltpu.HBM(a.shape, a.dtype)
    out = pl.pallas_call(
        body, name=name, out_shape=(*[hbm(s) for s in srcs], *[hbm(l) for l in lands]),
        in_specs=[_HBM] * (2 * na) + [_SEM, _SEM, pl.BlockSpec(memory_space=pl.ANY)],
        out_specs=tuple([_HBM] * (2 * na)), input_output_aliases={i: i for i in range(2 * na)},
        compiler_params=pltpu.CompilerParams(has_side_effects=_DATAFLOW),
    )(*srcs, *lands, send_sems, recv_sems, after)
    return list(out[:na]), list(out[na:])


def _own_slot(landed, own):
    me = 4 * lax.axis_index("x") + 2 * lax.axis_index("y") + lax.axis_index("c")
    return lax.dynamic_update_slice_in_dim(landed, own[None], me, axis=0)


def _adam_update(parts, w, m, v, name, tr=256, turned=False):
    _, r, c = w.shape
    tr = _pick_rows(r, tr)
    cp = parts.shape[2]
    flat = turned and c % 8 != 0
    at = (slice(None), 0) if flat else (0,)

    def body(p_ref, w_ref, m_ref, v_ref, g_ref, d_ref, nm_ref, nv_ref):
        cols = pl.ds(0, cp if turned else c)
        g = p_ref[0, :, cols].astype(F32)
        for i in range(1, N_DEV):
            g = g + p_ref[i, :, cols].astype(F32)
        if turned:
            g = g.T[:c]
        delta, nm, nv = _adamw(w_ref[at], g, m_ref[at], v_ref[at])
        g_ref[at] = g
        d_ref[at] = delta
        nm_ref[at] = nm
        nv_ref[at] = nv

    there, back = ((2, 0, 1), (1, 2, 0)) if flat else ((0, 2, 1), (0, 2, 1))
    if turned:
        w, m, v = (jnp.transpose(a, there) for a in (w, m, v))
        rs = pl.BlockSpec((c, 1, tr), lambda i: (0, 0, i)) if flat else pl.BlockSpec((1, c, tr), lambda i: (0, 0, i))
    else:
        rs = pl.BlockSpec((1, tr, c), lambda i: (0, i, 0))
    outs = pl.pallas_call(
        body, grid=(r // tr,), in_specs=[pl.BlockSpec((N_DEV, tr, cp), lambda i: (0, i, 0)), rs, rs, rs],
        out_specs=[rs] * 4, out_shape=[jax.ShapeDtypeStruct(w.shape, F32)] * 4, name=name,
        compiler_params=_params(("parallel",)))(parts, w, m, v)
    return [*([jnp.transpose(o, back) for o in outs] if turned else outs), outs[0]]


def _pick_rows(rows, target):
    if rows <= target:
        return rows
    t = target
    while t >= 16:
        if rows % t == 0:
            return t
        t -= 16
    return rows


BIG = ("w_in", "w_branch_dn", "w_branch_swa", "w_out", "w_gate", "w_up", "w_down")
IN_SHARD, IN_WIRE = D_IN // N_DEV, 640
FF_SHARD, FF_WIRE = D_FF // N_DEV, 384
D_FFP = N_DEV * FF_WIRE
BIG_SHAPES = {"w_in": ((D_MODEL, IN_SHARD), (D_MODEL, IN_WIRE)),
              "w_branch_dn": ((DN_WIDTH, LANES), (DN_WIDTH, LANES)),
              "w_branch_swa": ((SWA_WIDTH, LANES), (SWA_WIDTH, LANES)),
              "w_out": ((LANES, D_MODEL), (LANES, D_MODEL)),
              "w_gate": ((D_MODEL, FF_SHARD), (D_MODEL, FF_WIRE)),
              "w_up": ((D_MODEL, FF_SHARD), (D_MODEL, FF_WIRE)),
              "w_down": ((FF_SHARD, D_MODEL), (FF_WIRE, D_MODEL))}
CONV_SHARD, CONV_WIRE = (DN_CONV, DN_QKV // N_DEV), (8, 256)


def _pad_to(a, shape):
    return jnp.pad(a, [(0, t - s) for s, t in zip(a.shape, shape)])


IN_TILE_ROWS = 256
_IN_SEGS = ((R_GATE, 2048, P_GATE), (R_QKV, DN_QKV, P_QKV), (R_Z, DN_WIDTH, P_Z), (R_SQ, SWA_WIDTH, P_SQ),
            (R_SK, SWA_KVW, P_SK), (R_SV, SWA_KVW, P_SV), (R_B, 8, P_BA))


def _w_in_from_blocks(blocks, after):
    tm = IN_TILE_ROWS

    def body(b_ref, _, o_ref):
        parts = []
        for rs, n, _ in _IN_SEGS:
            for dev in range(N_DEV):
                lo, hi = max(rs, IN_SHARD * dev), min(rs + n, IN_SHARD * (dev + 1))
                if lo < hi:
                    parts.append(b_ref[dev][:, lo - IN_SHARD * dev:hi - IN_SHARD * dev])
        parts.append(jnp.zeros((tm, P_WIDTH - P_BA - 8), b_ref.dtype))
        o_ref[...] = jnp.concatenate(parts, axis=1)

    return pl.pallas_call(
        body, grid=(D_MODEL // tm,),
        in_specs=[pl.BlockSpec((N_DEV, tm, IN_WIRE), lambda i: (0, i, 0)), pl.BlockSpec(memory_space=pl.ANY)],
        out_specs=pl.BlockSpec((tm, P_WIDTH), lambda i: (i, 0)),
        out_shape=jax.ShapeDtypeStruct((D_MODEL, P_WIDTH), blocks.dtype), name="w_in_from_blocks",
        compiler_params=_params(("parallel",)))(blocks, after)


def _w_in_to_blocks(g):
    tm = IN_TILE_ROWS

    def body(g_ref, o_ref):
        for dev in range(N_DEV):
            parts = []
            for rs, n, ps in sorted(_IN_SEGS):
                lo, hi = max(rs, IN_SHARD * dev), min(rs + n, IN_SHARD * (dev + 1))
                if lo < hi:
                    parts.append(g_ref[:, ps + lo - rs:ps + hi - rs])
            parts.append(jnp.zeros((tm, IN_WIRE - IN_SHARD), g_ref.dtype))
            o_ref[dev] = jnp.concatenate(parts, axis=1)

    return pl.pallas_call(
        body, grid=(D_MODEL // tm,), in_specs=[pl.BlockSpec((tm, P_WIDTH), lambda i: (i, 0))],
        out_specs=pl.BlockSpec((N_DEV, tm, IN_WIRE), lambda i: (0, i, 0)),
        out_shape=jax.ShapeDtypeStruct((N_DEV, D_MODEL, IN_WIRE), g.dtype), name="w_in_to_blocks",
        compiler_params=_params(("parallel",)))(g)


SMALL = {"attn_norm": (0, (1, D_MODEL)), "ffn_norm": (1, (1, D_MODEL)), "dn_out_norm": (2, (1, DN_DIM)),
         "swa_q_norm": (3, (1, SWA_DIM)), "swa_k_norm": (4, (1, SWA_DIM)), "dn_a_log": (5, (1, DN_HEADS)),
         "dn_dt_bias": (6, (1, DN_HEADS)), "swa_sinks": (7, (1, SWA_HEADS)), "rel_bias": (8, (REL_BUCKETS, SWA_HEADS))}
SMALL_SHEET = (48, D_MODEL)


LOSS_ROW = 40


def _small_pack(grads, loss_local):
    names = list(SMALL)

    def body(*refs):
        o_ref = refs[-1]
        o_ref[...] = jnp.zeros_like(o_ref)
        for n, ref in zip(names, refs):
            r0, (nr, nc) = SMALL[n]
            o_ref[r0:r0 + nr, 0:nc] = ref[...]
        o_ref[LOSS_ROW:LOSS_ROW + 1, 0:1] = refs[len(names)][...]

    return pl.pallas_call(
        body, in_specs=[pl.BlockSpec(memory_space=pltpu.VMEM)] * (len(names) + 1),
        out_specs=pl.BlockSpec(memory_space=pltpu.VMEM), out_shape=jax.ShapeDtypeStruct(SMALL_SHEET, F32),
        name="small_pack", compiler_params=_params())(*[grads[n].reshape(SMALL[n][1]) for n in names], loss_local)


def _small_update(sheets, w, m, v):
    names = list(SMALL)
    k = len(names)

    def body(*refs):
        p_ref = refs[0]
        ins, outs = refs[1:1 + 3 * k], refs[1 + 3 * k:]
        loss = p_ref[0, LOSS_ROW:LOSS_ROW + 1, 0:1]
        for i in range(1, N_DEV):
            loss = loss + p_ref[i, LOSS_ROW:LOSS_ROW + 1, 0:1]
        outs[4 * k][...] = loss
        for t, n in enumerate(names):
            r0, (nr, nc) = SMALL[n]
            g = p_ref[0, r0:r0 + nr, 0:nc]
            for i in range(1, N_DEV):
                g = g + p_ref[i, r0:r0 + nr, 0:nc]
            delta, nm, nv = _adamw(ins[t][...], g, ins[k + t][...], ins[2 * k + t][...])
            for kind, val in enumerate((g, delta, nm, nv)):
                outs[kind * k + t][...] = val

    shapes = [jax.ShapeDtypeStruct(SMALL[n][1], F32) for n in names]
    vm = pl.BlockSpec(memory_space=pltpu.VMEM)
    res = pl.pallas_call(
        body, in_specs=[vm] * (1 + 3 * k), out_specs=[vm] * (4 * k + 1),
        out_shape=shapes * 4 + [jax.ShapeDtypeStruct((1, 1), F32)], name="adam_small", compiler_params=_params(),
    )(sheets, *[d[n].reshape(SMALL[n][1]) for d in (w, m, v) for n in names])
    return {n: tuple(res[kind * k + t] for kind in range(4)) for t, n in enumerate(names)}, res[4 * k]


def kernel(x, attn_norm, w_in, dn_conv, dn_a_log, dn_dt_bias, dn_out_norm, swa_q_norm, swa_k_norm, swa_sinks, rel_bias, w_branch_dn, w_branch_swa, w_out, ffn_norm, w_gate, w_up, w_down, loss_target, m_attn_norm, m_w_in, m_dn_conv, m_dn_a_log, m_dn_dt_bias, m_dn_out_norm, m_swa_q_norm, m_swa_k_norm, m_swa_sinks, m_rel_bias, m_w_branch_dn, m_w_branch_swa, m_w_out, m_ffn_norm, m_w_gate, m_w_up, m_w_down, v_attn_norm, v_w_in, v_dn_conv, v_dn_a_log, v_dn_dt_bias, v_dn_out_norm, v_swa_q_norm, v_swa_k_norm, v_swa_sinks, v_rel_bias, v_w_branch_dn, v_w_branch_swa, v_w_out, v_ffn_norm, v_w_gate, v_w_up, v_w_down):
    args = dict(locals())
    S = x.shape[1]
    xs = x.reshape(S, D_MODEL)
    target = loss_target.reshape(S, D_MODEL)

    w_loc = {n: args[n].reshape(BIG_SHAPES[n][0]) for n in BIG}
    conv_loc = dn_conv.reshape(CONV_SHARD)
    def on_wire(n, zero=0.0):
        return _pad_to(w_loc[n] + zero, BIG_SHAPES[n][1]).astype(BF16)

    first_handle, first_token = _all_gather_start([on_wire("w_in"), _pad_to(conv_loc, CONV_WIRE)],
                                                  "all_gather_weights_start")
    zero = first_token[0, 0]
    h = _norm_fwd(xs, attn_norm + zero, "norm1_fwd")
    later = [n for n in BIG if n != "w_in"]
    wire = [on_wire(n, zero) for n in later]
    first = _all_gather_finish(first_handle, [h] + wire, "all_gather_weights_finish")
    rest_handle, rest_token = _exchange_start(wire, False, "gather_rest_start", after=first[1])
    w_pad = _w_in_from_blocks(first[0], rest_token)
    conv_w = jnp.concatenate([first[1][d, :DN_CONV, :CONV_SHARD[1]] for d in range(N_DEV)], axis=1)

    proj = _mm([(h, w_pad)], "nn", F32, "mm_in", 1024, 1664, j_outer=True)
    qkvn = _dn_conv_fwd(proj, conv_w)
    beta, g = _dn_gate_fwd(proj, dn_a_log, dn_dt_bias)
    u, w, qe, kd, qk, egl, tinv = _dn_prep_fwd(qkvn, g, beta)
    o, states = _dn_scan_fwd(u, w, qe, kd, qk, egl)
    y_dn = _dn_out_fwd(o, proj, dn_out_norm)
    bias = _bias_fwd(rel_bias)
    y_swa = _swa_fwd(proj, swa_q_norm, swa_k_norm, swa_sinks, bias)
    rest_src, rest_land = _exchange_wait(rest_handle, y_swa, False, "gather_rest_wait")
    G = {n: _own_slot(land, src) for n, src, land in zip(later, rest_src, rest_land)}
    w_bdn, w_bswa, w_g, w_u = G["w_branch_dn"], G["w_branch_swa"], G["w_gate"], G["w_up"]
    w_o = G["w_out"].reshape(D_MODEL, D_MODEL)
    w_d = G["w_down"].reshape(D_FFP, D_MODEL)
    gates = [(proj, P_GATE // 512), (proj, (P_GATE + D_MODEL) // 512)]
    a_dn, a_swa, merged = _mm_fused(
        [(y_dn, w_bdn), (y_swa, w_bswa)], "nn", "mm_branch_merge", 1024, 512,
        lambda p, e: (p[0], p[1], _merge(e[0], e[1], p[0], p[1])), gates, (F32, F32, BF16), b_blocks=True)

    def resid_norm(p, e):
        x1 = e[0] + p[0]
        return x1, _rms(x1, e[1])

    x1, h2 = _mm_fused([(merged, w_o)], "nn", "mm_out_norm", 512, D_MODEL, resid_norm,
                       [(xs, 0), (ffn_norm, None)], (F32, BF16))
    gate, up, act = _mm_fused([(h2, w_g), (h2, w_u)], "nn", "mm_gate_up_act", 1024, 768,
                              lambda p, e: (p[0], p[1], _act(p[0], p[1])), [], (F32, F32, BF16),
                              j_outer=True, b_blocks=True)

    def loss_head(p, e):
        diff = e[0] + p[0] - e[1]
        dy = diff * (1.0 / D_MODEL)
        part = jnp.sum(jnp.mean(diff * diff, axis=-1, keepdims=True), axis=0, keepdims=True) * 0.5
        return dy, dy, part

    dy, dy_b, loss_local = _mm_fused([(act, w_d)], "nn", "mm_down_loss", 512, D_MODEL, loss_head,
                                     [(x1, 0), (target, 0)], (F32, BF16), sum_shape=(1, 1))

    def act_bwd(p, e):
        _, vjp = jax.vjp(_act, e[0], e[1])
        return vjp(p[0])

    dgate, dup = _mm_fused([(dy_b, w_d)], "nt", "mm_dact_act", 1024, 768, act_bwd, [(gate, 0), (up, 0)],
                           (BF16, BF16), j_outer=True)
    g_w_down = _mm([(act, dy_b)], "tn", BF16, "mm_dw_down", 768, D_MODEL, j_outer=True)
    g_w_down = g_w_down.reshape(N_DEV, FF_WIRE, D_MODEL)
    g_w_gate = _mm([(h2, dgate)], "tn", BF16, "mm_dw_gate", D_MODEL, 768, out_blocks=True)
    g_w_up = _mm([(h2, dup)], "tn", BF16, "mm_dw_up", D_MODEL, 768, out_blocks=True)
    ffn_handle, ffn_token = _exchange_start([g_w_down, g_w_gate, g_w_up], True, "scatter_ffn_start")

    def norm_bwd(p, e):
        _, vjp = jax.vjp(_rms, e[0], e[2])
        dx, dgain = vjp(sum(p))
        dx = dx + e[1]
        return dx, dx, dgain

    dx1, dx1_b, g_ffn_norm = _mm_fused(
        [(dgate, w_g), (dup, w_u)], "nt", "mm_dh2_norm", 256, D_MODEL, norm_bwd,
        [(x1, 0), (dy, 0), (ffn_norm + ffn_token[0, 0], None)], (F32, BF16), b_blocks=True, sum_shape=(1, D_MODEL))
    def merge_bwd(p, e):
        _, vjp = jax.vjp(_merge, *e)
        dg0, dg1, da_dn, da_swa = vjp(p[0])
        return jnp.concatenate([dg0, dg1], axis=1), da_dn, da_swa

    dproj, da_dn, da_swa = _mm_fused(
        [(dx1_b, w_o)], "nt", "mm_dmerged_merge", 512, D_MODEL, merge_bwd,
        [(proj, P_GATE // D_MODEL), (proj, P_GATE // D_MODEL + 1), (a_dn, 0), (a_swa, 0)], (BF16,) * 3,
        wide_first=(P_WIDTH, 2 * D_MODEL))
    g_w_out = _mm([(merged, dx1_b)], "tn", BF16, "mm_dw_out", 512, D_MODEL, j_outer=True)
    g_w_out = g_w_out.reshape(N_DEV, LANES, D_MODEL)
    dy_dn = _mm([(da_dn, w_bdn)], "nt", F32, "mm_dy_dn", 1024, DN_WIDTH, b_blocks=True)
    dy_swa = _mm([(da_swa, w_bswa)], "nt", F32, "mm_dy_swa", 1024, SWA_WIDTH, b_blocks=True)
    g_w_bdn = _mm([(y_dn, da_dn)], "tn", BF16, "mm_dw_branch_dn", DN_WIDTH, 512, out_blocks=True)
    g_w_bswa = _mm([(y_swa, da_swa)], "tn", BF16, "mm_dw_branch_swa", SWA_WIDTH, 512, out_blocks=True)
    dproj, dsk, dsv, g_q_norm, g_k_norm, g_sinks, dbias = _swa_bwd(proj, swa_q_norm, swa_k_norm, swa_sinks, bias,
                                                                   dy_swa, dproj)
    dproj = _kv_into(dproj, dsk, dsv)
    g_rel_bias = _bias_bwd(dbias)[:, :REL_BUCKETS].T
    mix_handle, mix_token = _exchange_start([g_w_out, g_w_bdn, g_w_bswa], True, "scatter_mix_start")
    do, dproj, g_out_norm = _dn_out_bwd(o, proj, dn_out_norm + mix_token[0, 0], dy_dn, dproj)
    du, dw, dqe, dkd, dqk, degl = _dn_scan_bwd(u, w, qe, kd, qk, egl, states, do)
    dqkvn, dgd, dbeta = _dn_prep_bwd(qkvn, g, beta, tinv, du, dw, dqe, dkd, dqk, degl)
    dproj, dal, ddt = _dn_gate_bwd(proj, dn_a_log, dn_dt_bias, dbeta, dgd, dproj)
    g_a_log = dal.reshape(DN_HEADS, DN_DIM).sum(axis=1)
    g_dt_bias = ddt[0, DN_HEADS:2 * DN_HEADS]
    dproj, g_conv = _dn_conv_bwd(proj, conv_w, dqkvn, dproj)
    g_w_in = _w_in_to_blocks(_mm([(h, dproj)], "tn", BF16, "mm_dw_in", 512, 1664, j_outer=True))
    in_handle, in_token = _exchange_start([g_w_in], True, "scatter_in_start")
    dx, g_attn_norm = _mm_fused(
        [(dproj, w_pad)], "nt", "mm_dh_norm", 512, D_MODEL, lambda p, e: norm_bwd(p, e)[1:],
        [(xs, 0), (dx1, 0), (attn_norm + in_token[0, 0], None)], (F32,), sum_shape=(1, D_MODEL))

    g_small = {"attn_norm": g_attn_norm, "ffn_norm": g_ffn_norm, "rel_bias": g_rel_bias, "dn_out_norm": g_out_norm,
               "swa_q_norm": g_q_norm, "swa_k_norm": g_k_norm, "dn_a_log": g_a_log, "dn_dt_bias": g_dt_bias,
               "swa_sinks": g_sinks}
    me = 4 * lax.axis_index("x") + 2 * lax.axis_index("y") + lax.axis_index("c")
    outs = {}

    def finish(handle, group, name, after):
        srcs, lands = _exchange_wait(handle, after, True, name)
        for n, src, land in zip(group, srcs, lands):
            parts = _own_slot(land, lax.dynamic_index_in_dim(src, me, 0, keepdims=False))
            outs[n] = _adam_update(parts, args[n], args["m_" + n], args["v_" + n], "adam_" + n,
                                   turned=args[n].shape[2] % LANES != 0)

    finish(ffn_handle, ("w_down", "w_gate", "w_up"), "scatter_ffn_wait", dx)
    finish(mix_handle, ("w_out", "w_branch_dn", "w_branch_swa"), "scatter_mix_wait", dx)
    sheets, conv_all = _all_gather_direct([_small_pack(g_small, loss_local), _pad_to(g_conv, (8, DN_QKV))],
                                          "all_gather_small",
                                          after=[outs[n][4] for n in sorted(outs)])
    finish(in_handle, ("w_in",), "scatter_in_wait", sheets)
    conv_parts = lax.dynamic_slice(conv_all, (0, 0, me * CONV_SHARD[1]), (N_DEV,) + CONV_SHARD)
    outs["dn_conv"] = _adam_update(conv_parts, dn_conv, m_dn_conv, v_dn_conv, "adam_dn_conv")
    small_outs, loss = _small_update(sheets, {n: args[n] for n in SMALL}, {n: args["m_" + n] for n in SMALL},
                                     {n: args["v_" + n] for n in SMALL})
    outs.update(small_outs)

    names = ("attn_norm", "w_in", "dn_conv", "dn_a_log", "dn_dt_bias", "dn_out_norm", "swa_q_norm", "swa_k_norm",
             "swa_sinks", "rel_bias", "w_branch_dn", "w_branch_swa", "w_out", "ffn_norm", "w_gate", "w_up", "w_down")
    results = []
    for kind in range(4):
        results += [outs[n][kind].reshape(args[n].shape) for n in names]

    return (loss.reshape(()), dx.reshape(x.shape), *results)
```

```python
import math

import numpy as np
import jax
import jax.numpy as jnp
from jax import lax
from jax.experimental import pallas as pl
from jax.experimental.pallas import tpu as pltpu

F32 = jnp.float32
BF16 = jnp.bfloat16
HI = lax.Precision.HIGHEST

D_MODEL = 1024
DN_HEADS = 4
DN_DIM = 128
DN_WIDTH = 512
DN_QKV = 1536
DN_CONV = 4
CHUNK = 64
SWA_HEADS = 8
SWA_KV = 2
SWA_GROUP = 4
SWA_DIM = 64
SWA_WIDTH = 512
SWA_KVW = 128
WINDOW = 128
BLOCK = 128
REL_BUCKETS = 32
REL_MAX_DIST = 128
D_FF = 2816
D_IN = 4872
EPS = 1e-6
N_DEV = 8

ADAM_LR = 0.001
ADAM_B1 = 0.9
ADAM_B2 = 0.999
ADAM_EPS = 1e-08
ADAM_WD = 0.01
ADAM_STEP = 10

P_GATE, P_QKV, P_Z, P_SQ, P_SK, P_SV, P_BA = 0, 2048, 3584, 4096, 4608, 4736, 4864
P_WIDTH = 4992
R_QKV, R_Z, R_B, R_A, R_SQ, R_SK, R_SV, R_GATE = 0, 1536, 2048, 2052, 2056, 2568, 2696, 2824

VMEM_LIMIT = 56 * 1024 * 1024
LANES = 128
MESH_ID = pl.DeviceIdType.MESH


def _params(sem=None):
    return pltpu.CompilerParams(dimension_semantics=sem, vmem_limit_bytes=VMEM_LIMIT)


def _pick(dim, target):
    if dim <= target:
        return dim
    t = target - target % LANES
    while t >= LANES:
        if dim % t == 0:
            return t
        t -= LANES
    return dim


_DIMS = {"nn": (((1,), (0,)), ((), ())), "nt": (((1,), (1,)), ((), ())), "tn": (((0,), (0,)), ((), ()))}


def _tile_product(a_ref, b_ref, mode, b_blocks):
    a = a_ref[...].astype(BF16)
    b = jnp.concatenate([b_ref[d] for d in range(b_ref.shape[0])], axis=1) if b_blocks else b_ref[...]
    return lax.dot_general(a, b.astype(BF16), _DIMS[mode], preferred_element_type=F32)


def _mm(pairs, mode, out_dtype, name, bm, bn, j_outer=False, b_blocks=False, out_blocks=False):
    a0, b0 = pairs[0]
    cb = b0.shape[2] if b_blocks else None
    b_shape = (b0.shape[1], N_DEV * cb) if b_blocks else b0.shape
    if mode == "nn":
        (M, K), (K2, N) = a0.shape, b_shape
    elif mode == "nt":
        (M, K), (N, K2) = a0.shape, b_shape
    else:
        (K, M), (K2, N) = a0.shape, b_shape
    bm, bn = min(bm, M), min(bn, N)
    assert K == K2 and M % bm == 0 and N % bn == 0, (name, a0.shape, b0.shape, bm, bn)
    co = N // N_DEV
    assert not out_blocks or bn % co == 0
    dims = _DIMS[mode]
    n = len(pairs)

    def body(*refs):
        o_ref = refs[2 * n]
        acc = None
        for t in range(n):
            p = _tile_product(refs[2 * t], refs[2 * t + 1], mode, b_blocks)
            acc = p if acc is None else acc + p
        if out_blocks:
            for d in range(bn // co):
                o_ref[d] = acc[:, d * co:(d + 1) * co].astype(out_dtype)
        else:
            o_ref[...] = acc.astype(out_dtype)

    def ij(f):
        return (lambda j, i: f(i, j)) if j_outer else f

    a_spec = pl.BlockSpec((K, bm), ij(lambda i, j: (0, i))) if mode == "tn" else pl.BlockSpec((bm, K), ij(lambda i, j: (i, 0)))
    if b_blocks and mode == "nt":
        b_spec = pl.BlockSpec((N_DEV, bn, cb), ij(lambda i, j: (0, j, 0)))
    elif b_blocks:
        b_spec = pl.BlockSpec((bn // cb, K, cb), ij(lambda i, j: (j, 0, 0)))
    elif mode == "nt":
        b_spec = pl.BlockSpec((bn, K), ij(lambda i, j: (j, 0)))
    else:
        b_spec = pl.BlockSpec((K, bn), ij(lambda i, j: (0, j)))
    if out_blocks:
        out_spec = pl.BlockSpec((bn // co, bm, co), ij(lambda i, j: (j, i, 0)))
        out_shape = jax.ShapeDtypeStruct((N_DEV, M, co), out_dtype)
    else:
        out_spec = pl.BlockSpec((bm, bn), ij(lambda i, j: (i, j)))
        out_shape = jax.ShapeDtypeStruct((M, N), out_dtype)
    grid = (N // bn, M // bm) if j_outer else (M // bm, N // bn)
    return pl.pallas_call(
        body, grid=grid, in_specs=[a_spec, b_spec] * n, out_specs=out_spec, out_shape=out_shape, name=name,
        compiler_params=_params(("parallel", "parallel")),
    )(*[x for pair in pairs for x in pair])


def _mm_fused(pairs, mode, name, bm, bn, epilogue, extras, out_dtypes, j_outer=False, b_blocks=False,
              sum_shape=None, wide_first=None):
    a0, b0 = pairs[0]
    cb = b0.shape[2] if b_blocks else None
    b_shape = (b0.shape[1], N_DEV * cb) if b_blocks else b0.shape
    if mode == "nn":
        (M, K), (K2, N) = a0.shape, b_shape
    else:
        (M, K), (N, K2) = a0.shape, b_shape
    bm, bn = min(bm, M), min(bn, N)
    assert mode in ("nn", "nt") and K == K2 and M % bm == 0 and N % bn == 0, (name, a0.shape, b0.shape)
    dims = _DIMS[mode]
    n, ne, no = len(pairs), len(extras), len(out_dtypes)

    def body(*refs):
        prods = [_tile_product(refs[2 * t], refs[2 * t + 1], mode, b_blocks) for t in range(n)]
        results = epilogue(prods, [r[...] for r in refs[2 * n:2 * n + ne]])
        out_refs = refs[2 * n + ne:]
        for o_ref, val, dt in zip(out_refs, results, out_dtypes):
            o_ref[...] = val.astype(dt)
        if sum_shape is not None:
            s_ref = out_refs[no]

            @pl.when((pl.program_id(0) == 0) & (pl.program_id(1) == 0))
            def _():
                s_ref[...] = jnp.zeros_like(s_ref)

            s_ref[...] += results[no]

    def ij(f):
        return (lambda j, i: f(i, j)) if j_outer else f

    a_spec = pl.BlockSpec((bm, K), ij(lambda i, j: (i, 0)))
    once = dict(pipeline_mode=pl.Buffered(1)) if bn == N else {}
    if b_blocks and mode == "nt":
        b_spec = pl.BlockSpec((N_DEV, bn, cb), ij(lambda i, j: (0, j, 0)), **once)
    elif b_blocks:
        b_spec = pl.BlockSpec((bn // cb, K, cb), ij(lambda i, j: (j, 0, 0)), **once)
    elif mode == "nt":
        b_spec = pl.BlockSpec((bn, K), ij(lambda i, j: (j, 0)), **once)
    else:
        b_spec = pl.BlockSpec((K, bn), ij(lambda i, j: (0, j)), **once)
    e_specs = [pl.BlockSpec((1, bn), ij(lambda i, j: (0, j))) if first is None
               else pl.BlockSpec((bm, bn), ij(lambda i, j, first=first: (i, first + j))) for _, first in extras]
    tile = pl.BlockSpec((bm, bn), ij(lambda i, j: (i, j)))
    out_specs = [tile] * no
    out_shape = [jax.ShapeDtypeStruct((M, N), dt) for dt in out_dtypes]
    if wide_first is not None:
        assert bn == N
        out_specs[0] = pl.BlockSpec((bm, wide_first[1]), ij(lambda i, j: (i, 0)))
        out_shape[0] = jax.ShapeDtypeStruct((M, wide_first[0]), out_dtypes[0])
    if sum_shape is not None:
        assert sum_shape[1] in (1, bn) and (sum_shape[1] == 1 or bn == N)
        out_specs.append(_full(sum_shape))
        out_shape.append(jax.ShapeDtypeStruct(sum_shape, F32))
    grid = (N // bn, M // bm) if j_outer else (M // bm, N // bn)
    sem = ("arbitrary", "arbitrary") if sum_shape is not None else ("parallel", "parallel")
    return pl.pallas_call(
        body, grid=grid, in_specs=[a_spec, b_spec] * n + e_specs, out_specs=out_specs, out_shape=out_shape,
        name=name, compiler_params=_params(sem),
    )(*[x for pair in pairs for x in pair], *[arr for arr, _ in extras])


def _rms(x, gain):
    return x * lax.rsqrt(jnp.mean(x * x, axis=-1, keepdims=True) + EPS) * gain


def _silu(x):
    return x * jax.nn.sigmoid(x)


def _act(g, u):
    return _silu(g) * u


def _merge(g0, g1, a_dn, a_swa):
    return jax.nn.sigmoid(g0) * a_dn + jax.nn.sigmoid(g1) * a_swa


def _dn_post(c, is_v, q_scale):
    a = _silu(c)
    rs = lax.rsqrt(jnp.sum(a * a, axis=-1, keepdims=True) + EPS) * q_scale
    return a * jnp.where(is_v, 1.0, rs)


def _dn_out(o, z, gain):
    return _rms(o, gain) * _silu(z)


def _dot(a, b, dims=_DIMS["nn"], hi=False):
    if a.ndim == 3 or b.ndim == 3:
        batch = a.shape[0] if a.ndim == 3 else b.shape[0]
        a = a if a.ndim == 3 else jnp.broadcast_to(a, (batch,) + a.shape)
        b = b if b.ndim == 3 else jnp.broadcast_to(b, (batch,) + b.shape)
        ((ca,), (cb,)), _ = dims
        dims = (((ca + 1,), (cb + 1,)), ((0,), (0,)))
    if hi:
        return lax.dot_general(a, b, dims, precision=HI, preferred_element_type=F32)
    return lax.dot_general(a.astype(BF16), b.astype(BF16), dims, preferred_element_type=F32)


def _pieces(x):
    hi = x.astype(BF16)
    r1 = x - hi.astype(F32)
    mid = r1.astype(BF16)
    return hi, mid, (r1 - mid.astype(F32)).astype(BF16)


def _sel_left_impl(m, x):
    mb = m.astype(BF16)
    hi, mid, lo = _pieces(x)
    return _dot(mb, hi) + (_dot(mb, mid) + _dot(mb, lo))


@jax.custom_vjp
def _sel_left(m, mt, x):
    return _sel_left_impl(m, x)


_sel_left.defvjp(lambda m, mt, x: (_sel_left_impl(m, x), (m, mt)),
                 lambda res, ct: (jnp.zeros_like(res[0]), jnp.zeros_like(res[1]), _sel_left_impl(res[1], ct)))


def _sel_right_impl(x, s):
    sb = s.astype(BF16)
    hi, mid, lo = _pieces(x)
    return _dot(hi, sb) + (_dot(mid, sb) + _dot(lo, sb))


@jax.custom_vjp
def _sel_right(x, s, st):
    return _sel_right_impl(x, s)


_sel_right.defvjp(lambda x, s, st: (_sel_right_impl(x, s), (s, st)),
                  lambda res, ct: (_sel_right_impl(ct, res[1]), jnp.zeros_like(res[0]), jnp.zeros_like(res[1])))


def _dot3_impl(a, b):
    a_hi, a_lo, _ = _pieces(a)
    b_hi, b_lo, _ = _pieces(b)
    return _dot(a_hi, b_hi) + (_dot(a_hi, b_lo) + _dot(a_lo, b_hi))


@jax.custom_vjp
def _dot3(a, b):
    return _dot3_impl(a, b)


_dot3.defvjp(lambda a, b: (_dot3_impl(a, b), (a, b)),
             lambda res, ct: (_dot(ct, res[1], _DIMS["nt"]), _dot(res[0], ct, _DIMS["tn"])))


def _inv_impl(a, eye, strict):
    t = eye - a
    p = _dot(a, a)
    for level in range(5):
        t = t + _dot(t, p)
        if level < 4:
            p = _dot(p, p)
    t = t + _dot(t, eye - t - _dot3_impl(a, t))
    return jnp.where(strict > 0.5, t, eye)


@jax.custom_vjp
def _inv_given(a, t):
    return t.astype(F32)


_inv_given.defvjp(lambda a, t: (t.astype(F32), t),
                  lambda t, ct: (-_dot(_dot(t, ct, _DIMS["tn"]), t, _DIMS["nt"]), jnp.zeros_like(t)))


@jax.custom_vjp
def _lanes_join(a, b):
    return jnp.concatenate([a, b], axis=-1)


_lanes_join.defvjp(lambda a, b: (jnp.concatenate([a, b], axis=-1), None),
                   lambda _, ct: (ct[..., :ct.shape[-1] // 2], ct[..., ct.shape[-1] // 2:]))


@jax.custom_vjp
def _lanes_halves(y):
    h = y.shape[-1] // 2
    return y[..., :h], y[..., h:]


_lanes_halves.defvjp(lambda y: ((y[..., :y.shape[-1] // 2], y[..., y.shape[-1] // 2:]), None),
                     lambda _, ct: (jnp.concatenate(ct, axis=-1),))

GROUP = 4
GROUP_ROWS = GROUP * CHUNK


def _block_consts(n):
    ii = lax.broadcasted_iota(jnp.int32, (n, n), 0)
    jj = lax.broadcasted_iota(jnp.int32, (n, n), 1)
    shift = CHUNK.bit_length() - 1
    same = jnp.right_shift(ii, shift) == jnp.right_shift(jj, shift)
    return same & (ii >= jj), same & (ii <= jj), same & (ii > jj), same, ii == jj


def _lane0(n):
    s = (lax.broadcasted_iota(jnp.int32, (LANES, n), 0) == 0).astype(F32)
    st = (lax.broadcasted_iota(jnp.int32, (n, LANES), 1) == 0).astype(F32)
    return s, st


def _dn_group(q, k, v, g, beta, t_saved=None):
    n = GROUP_ROWS
    low_b, upp_b, strict_b, _, eye_b = _block_consts(n)
    low, upp, eye = low_b.astype(F32), upp_b.astype(F32), eye_b.astype(F32)
    gc = _sel_left(low, upp, g)
    per_chunk = (g.shape[0], GROUP, CHUNK, LANES)
    g_last = jnp.sum(g.reshape(per_chunk), axis=2, keepdims=True)
    gl = jnp.broadcast_to(g_last, per_chunk).reshape(g.shape)
    s, st = _lane0(n)
    col = _sel_right(gc, s, st)
    row = jnp.swapaxes(col, 1, 2)
    decay = jnp.exp(jnp.where(low_b, col - row, -jnp.inf))
    kb = k * beta
    vb = v * beta
    a = jnp.where(strict_b, _dot(kb, k, _DIMS["nt"]) * decay, 0.0)
    t = _inv_impl(a, eye, strict_b.astype(F32)) if t_saved is None else _inv_given(a, t_saved)
    u, w = _lanes_halves(_dot3(t, _lanes_join(vb, kb * jnp.exp(gc))))
    fold = (jnp.bitwise_and(lax.broadcasted_iota(jnp.int32, (n, CHUNK), 0), CHUNK - 1)
            == lax.broadcasted_iota(jnp.int32, (n, CHUNK), 1)).astype(F32)
    fold_t = (jnp.bitwise_and(lax.broadcasted_iota(jnp.int32, (CHUNK, n), 1), CHUNK - 1)
              == lax.broadcasted_iota(jnp.int32, (CHUNK, n), 0)).astype(F32)
    qk = _sel_right(_dot(q, k, _DIMS["nt"]) * decay, fold, fold_t)
    return u, w, q * jnp.exp(gc), k * jnp.exp(gl - gc), qk, jnp.exp(g_last), t


def _dn_step(s, u, w, qe, kd, qk, egl):
    v_new = u - _dot(w, s)
    o = _dot(qe, s) + _dot(qk, v_new)
    s_new = s * egl + _dot(kd, v_new, _DIMS["tn"])
    return s_new, o


def _swa_block(q, kband, vband, qg, kg, sinks, band):
    kn = _rms(kband, kg)
    qn = _rms(q, qg) * (SWA_DIM ** -0.5)
    logits = _dot(qn, kn, _DIMS["nt"]) + band
    m = lax.stop_gradient(jnp.maximum(jnp.max(logits, axis=-1, keepdims=True), sinks))
    p = jnp.exp(logits - m)
    denom = jnp.sum(p, axis=-1, keepdims=True) + jnp.exp(sinks - m)
    return _dot(p * (1.0 / denom), vband)


def _adamw(w, g, m, v):
    m = ADAM_B1 * m + (1.0 - ADAM_B1) * g
    v = ADAM_B2 * v + (1.0 - ADAM_B2) * jnp.square(g)
    m_hat = m / (1.0 - ADAM_B1 ** ADAM_STEP)
    v_hat = v / (1.0 - ADAM_B2 ** ADAM_STEP)
    delta = -ADAM_LR * (m_hat / (jnp.sqrt(v_hat) + ADAM_EPS) + ADAM_WD * w)
    return delta, m, v


def _row(tm, c, cb=0):
    return pl.BlockSpec((tm, c), lambda i, cb=cb: (i, cb))


def _full(shape):
    nd = len(shape)
    return pl.BlockSpec(shape, lambda *_, nd=nd: (0,) * nd)


def _norm_fwd(x, gain, name, tm=1024):
    S = x.shape[0]

    def body(x_ref, g_ref, h_ref):
        h_ref[...] = _rms(x_ref[...], g_ref[...]).astype(BF16)

    return pl.pallas_call(
        body, grid=(S // tm,), in_specs=[_row(tm, D_MODEL), _full((1, D_MODEL))],
        out_specs=_row(tm, D_MODEL), out_shape=jax.ShapeDtypeStruct((S, D_MODEL), BF16),
        name=name, compiler_params=_params(("parallel",)))(x, gain)


def _shift_down(x, s):
    row = lax.broadcasted_iota(jnp.int32, x.shape, 0)
    return jnp.where(row >= s, pltpu.roll(x, s, axis=0), 0.0)


def _shift_up(x, s):
    n = x.shape[0]
    row = lax.broadcasted_iota(jnp.int32, x.shape, 0)
    return jnp.where(row < n - s, pltpu.roll(x, n - s, axis=0), 0.0)


def _conv(x, w):
    out = w[DN_CONV - 1:DN_CONV] * x
    for s in range(1, DN_CONV):
        out = out + w[DN_CONV - 1 - s:DN_CONV - s] * _shift_down(x, s)
    return out


def _dn_conv_fwd(proj, conv_w):
    S = proj.shape[0]
    nb = DN_QKV // LANES

    def body(x_ref, w_ref, o_ref):
        j = pl.program_id(0)
        q_scale = jnp.where(j < DN_HEADS, DN_DIM ** -0.5, 1.0).astype(F32)
        o_ref[...] = _dn_post(_conv(x_ref[...], w_ref[...]), j >= 2 * DN_HEADS, q_scale)

    return pl.pallas_call(
        body, grid=(nb,),
        in_specs=[pl.BlockSpec((S, LANES), lambda j: (0, P_QKV // LANES + j)),
                  pl.BlockSpec((DN_CONV, LANES), lambda j: (0, j))],
        out_specs=pl.BlockSpec((S, LANES), lambda j: (0, j)),
        out_shape=jax.ShapeDtypeStruct((S, DN_QKV), F32), name="dn_conv_fwd",
        compiler_params=_params(("parallel",)))(proj, conv_w)


def _dn_conv_bwd(proj, conv_w, dqkvn, dproj):
    S = proj.shape[0]
    nb = DN_QKV // LANES

    def body(x_ref, w_ref, d_ref, _, dx_ref, dw_ref):
        j = pl.program_id(0)
        q_scale = jnp.where(j < DN_HEADS, DN_DIM ** -0.5, 1.0).astype(F32)
        x = x_ref[...]
        w = w_ref[...]
        _, vjp = jax.vjp(lambda c: _dn_post(c, j >= 2 * DN_HEADS, q_scale), _conv(x, w))
        (dc,) = vjp(d_ref[0])
        dx = w[DN_CONV - 1:DN_CONV] * dc
        dw_ref[DN_CONV - 1:DN_CONV, :] = jnp.sum(dc * x, axis=0, keepdims=True)
        for s in range(1, DN_CONV):
            dx = dx + w[DN_CONV - 1 - s:DN_CONV - s] * _shift_up(dc, s)
            dw_ref[DN_CONV - 1 - s:DN_CONV - s, :] = jnp.sum(dc * _shift_down(x, s), axis=0, keepdims=True)
        dx_ref[...] = dx.astype(BF16)

    return pl.pallas_call(
        body, grid=(nb,),
        in_specs=[pl.BlockSpec((S, LANES), lambda j: (0, P_QKV // LANES + j)),
                  pl.BlockSpec((DN_CONV, LANES), lambda j: (0, j)),
                  pl.BlockSpec((1, S, LANES), lambda j: (lax.div(j, DN_HEADS), 0, lax.rem(j, DN_HEADS))),
                  pl.BlockSpec(memory_space=pl.ANY)],
        out_specs=[pl.BlockSpec((S, LANES), lambda j: (0, P_QKV // LANES + j)),
                   pl.BlockSpec((DN_CONV, LANES), lambda j: (0, j))],
        out_shape=[jax.ShapeDtypeStruct(dproj.shape, dproj.dtype), jax.ShapeDtypeStruct((DN_CONV, DN_QKV), F32)],
        input_output_aliases={3: 0},
        name="dn_conv_bwd", compiler_params=_params(("parallel",)))(proj, conv_w, dqkvn, dproj)


def _expanders():
    eb = np.zeros((LANES, DN_WIDTH), np.float32)
    ea = np.zeros((LANES, DN_WIDTH), np.float32)
    for h in range(DN_HEADS):
        eb[h, h * DN_DIM:(h + 1) * DN_DIM] = 1.0
        ea[DN_HEADS + h, h * DN_DIM:(h + 1) * DN_DIM] = 1.0
    return jnp.asarray(eb), jnp.asarray(ea), jnp.asarray(eb.T), jnp.asarray(ea.T)


def _dn_gate_args(a_log, dt_bias):
    alog = jnp.repeat(a_log.reshape(1, DN_HEADS), DN_DIM, axis=1)
    dtb = _pad_to(jnp.pad(dt_bias.reshape(1, DN_HEADS), ((0, 0), (DN_HEADS, 0))), (1, LANES))
    return _expanders() + (alog, dtb)


def _dn_gate_specs(tm):
    return [_row(tm, LANES, P_BA // LANES), _full((LANES, DN_WIDTH)), _full((LANES, DN_WIDTH)),
            _full((DN_WIDTH, LANES)), _full((DN_WIDTH, LANES)), _full((1, DN_WIDTH)), _full((1, LANES))]


def _dn_gate_fn(ba, eb, ea, ebt, eat, alog, dtb):
    beta = _sel_right(jax.nn.sigmoid(ba), eb, ebt)
    g = -jnp.exp(alog) * _sel_right(jax.nn.softplus(ba + dtb), ea, eat)
    return beta, g


def _dn_gate_fwd(proj, a_log, dt_bias, tm=1024):
    S = proj.shape[0]
    args = _dn_gate_args(a_log, dt_bias)

    def body(ba_ref, eb_ref, ea_ref, ebt_ref, eat_ref, al_ref, dt_ref, beta_ref, g_ref):
        beta, g = _dn_gate_fn(ba_ref[...], eb_ref[...], ea_ref[...], ebt_ref[...], eat_ref[...], al_ref[...],
                              dt_ref[...])
        beta_ref[...] = beta
        g_ref[...] = g

    return pl.pallas_call(
        body, grid=(S // tm,), in_specs=_dn_gate_specs(tm), out_specs=[_row(tm, DN_WIDTH), _row(tm, DN_WIDTH)],
        out_shape=[jax.ShapeDtypeStruct((S, DN_WIDTH), F32), jax.ShapeDtypeStruct((S, DN_WIDTH), F32)],
        name="dn_gate_fwd", compiler_params=_params(("parallel",)))(proj, *args)


def _dn_gate_bwd(proj, a_log, dt_bias, dbeta, dg, dproj, tm=1024):
    S = proj.shape[0]
    args = _dn_gate_args(a_log, dt_bias)

    def body(ba_ref, eb_ref, ea_ref, ebt_ref, eat_ref, al_ref, dt_ref, dbeta_ref, dg_ref, _, dba_ref, dal_ref,
             ddt_ref):
        eb, ea, ebt, eat = eb_ref[...], ea_ref[...], ebt_ref[...], eat_ref[...]
        _, vjp = jax.vjp(lambda ba, al, dt: _dn_gate_fn(ba, eb, ea, ebt, eat, al, dt), ba_ref[...], al_ref[...],
                         dt_ref[...])
        dba, dal, ddt = vjp((dbeta_ref[...], dg_ref[...]))
        dba_ref[...] = dba.astype(BF16)

        @pl.when(pl.program_id(0) == 0)
        def _():
            dal_ref[...] = jnp.zeros_like(dal_ref)
            ddt_ref[...] = jnp.zeros_like(ddt_ref)

        dal_ref[...] += dal
        ddt_ref[...] += ddt

    return pl.pallas_call(
        body, grid=(S // tm,),
        in_specs=_dn_gate_specs(tm) + [_row(tm, DN_WIDTH), _row(tm, DN_WIDTH), pl.BlockSpec(memory_space=pl.ANY)],
        out_specs=[_row(tm, LANES, P_BA // LANES), _full((1, DN_WIDTH)), _full((1, LANES))],
        out_shape=[jax.ShapeDtypeStruct(dproj.shape, dproj.dtype), jax.ShapeDtypeStruct((1, DN_WIDTH), F32),
                   jax.ShapeDtypeStruct((1, LANES), F32)],
        input_output_aliases={len(args) + 3: 0},
        name="dn_gate_bwd", compiler_params=_params(("arbitrary",)))(proj, *args, dbeta, dg, dproj)


PREP_GROUPS = 8
PREP_CHUNKS = GROUP * PREP_GROUPS


def _dn_prep_specs():
    rows = PREP_CHUNKS * CHUNK
    q = pl.BlockSpec((rows, LANES), lambda h, c: (c, h))
    k = pl.BlockSpec((rows, LANES), lambda h, c: (c, DN_HEADS + h))
    v = pl.BlockSpec((rows, LANES), lambda h, c: (c, 2 * DN_HEADS + h))
    qk = pl.BlockSpec((1, rows, CHUNK), lambda h, c: (h, c, 0))
    egl = pl.BlockSpec((1, PREP_CHUNKS, 1, LANES), lambda h, c: (h, c, 0, 0))
    return q, k, v, qk, egl


def _dn_prep_fwd(qkvn, g, beta):
    S = qkvn.shape[0]
    nc = S // CHUNK
    q, k, v, qks, egl = _dn_prep_specs()

    def body(q_ref, k_ref, v_ref, g_ref, b_ref, u_ref, w_ref, qe_ref, kd_ref, qk_ref, egl_ref, t_ref):
        rows = PREP_CHUNKS * CHUNK
        grp = (PREP_GROUPS, GROUP_ROWS, LANES)
        u, w, qe, kd, qk, e, t = _dn_group(q_ref[...].reshape(grp), k_ref[...].reshape(grp), v_ref[...].reshape(grp),
                                           g_ref[...].reshape(grp), b_ref[...].reshape(grp))
        u_ref[...] = u.reshape(rows, LANES)
        w_ref[...] = w.reshape(rows, LANES)
        qe_ref[...] = qe.reshape(rows, LANES)
        kd_ref[...] = kd.reshape(rows, LANES)
        t_ref[0] = t.reshape(rows, GROUP_ROWS).astype(BF16)
        qk_ref[0] = qk.reshape(rows, CHUNK)
        egl_ref[0] = e.reshape(PREP_CHUNKS, 1, LANES)

    wide = jax.ShapeDtypeStruct((S, DN_WIDTH), F32)
    return pl.pallas_call(
        body, grid=(DN_HEADS, nc // PREP_CHUNKS), in_specs=[q, k, v, q, q],
        out_specs=[q, q, q, q, qks, egl, _dn_tinv_spec()],
        out_shape=[wide, wide, wide, wide, jax.ShapeDtypeStruct((DN_HEADS, S, CHUNK), F32),
                   jax.ShapeDtypeStruct((DN_HEADS, nc, 1, LANES), F32),
                   jax.ShapeDtypeStruct((DN_HEADS, S, GROUP_ROWS), BF16)],
        name="dn_prep_fwd", compiler_params=_params(("parallel", "parallel")))(qkvn, qkvn, qkvn, g, beta)


def _dn_tinv_spec():
    return pl.BlockSpec((1, PREP_CHUNKS * CHUNK, GROUP_ROWS), lambda h, c: (h, c, 0))


def _dn_prep_bwd(qkvn, g, beta, tinv, du, dw, dqe, dkd, dqk, degl):
    S = qkvn.shape[0]
    nc = S // CHUNK
    q, k, v, qks, egl = _dn_prep_specs()

    def body(q_ref, k_ref, v_ref, g_ref, b_ref, t_ref, du_ref, dw_ref, dqe_ref, dkd_ref, dqk_ref, degl_ref,
             dqkv_ref, dg_ref, db_ref):
        rows = PREP_CHUNKS * CHUNK
        grp = (PREP_GROUPS, GROUP_ROWS, LANES)
        t_saved = t_ref[0].reshape(PREP_GROUPS, GROUP_ROWS, GROUP_ROWS)
        _, vjp = jax.vjp(lambda *x: _dn_group(*x, t_saved=t_saved)[:6], q_ref[...].reshape(grp),
                         k_ref[...].reshape(grp), v_ref[...].reshape(grp), g_ref[...].reshape(grp),
                         b_ref[...].reshape(grp))
        dq, dk, dv, dg, db = vjp((du_ref[...].reshape(grp), dw_ref[...].reshape(grp), dqe_ref[...].reshape(grp),
                                  dkd_ref[...].reshape(grp), dqk_ref[0].reshape(PREP_GROUPS, GROUP_ROWS, CHUNK),
                                  degl_ref[0].reshape(PREP_GROUPS, GROUP, 1, LANES)))
        dqkv_ref[0] = dq.reshape(rows, LANES)
        dqkv_ref[1] = dk.reshape(rows, LANES)
        dqkv_ref[2] = dv.reshape(rows, LANES)
        dg_ref[...] = dg.reshape(rows, LANES)
        db_ref[...] = db.reshape(rows, LANES)

    wide = jax.ShapeDtypeStruct((S, DN_WIDTH), F32)
    rows = PREP_CHUNKS * CHUNK
    return pl.pallas_call(
        body, grid=(DN_HEADS, nc // PREP_CHUNKS), in_specs=[q, k, v, q, q, _dn_tinv_spec(), q, q, q, q, qks, egl],
        out_specs=[pl.BlockSpec((3, rows, LANES), lambda h, c: (0, c, h)), q, q],
        out_shape=[jax.ShapeDtypeStruct((3, S, DN_WIDTH), F32), wide, wide],
        name="dn_prep_bwd", compiler_params=_params(("parallel", "parallel")),
    )(qkvn, qkvn, qkvn, g, beta, tinv, du, dw, dqe, dkd, dqk, degl)


SCAN_CHUNKS = 16


def _dn_scan_specs(nc, reverse):
    nb = nc // SCAN_CHUNKS

    def cidx(c):
        return nb - 1 - c if reverse else c

    hc = pl.BlockSpec((SCAN_CHUNKS * CHUNK, DN_WIDTH), lambda c: (cidx(c), 0))
    qk = pl.BlockSpec((DN_HEADS, SCAN_CHUNKS * CHUNK, CHUNK), lambda c: (0, cidx(c), 0))
    egl = pl.BlockSpec((DN_HEADS, SCAN_CHUNKS, 1, LANES), lambda c: (0, cidx(c), 0, 0))
    st = pl.BlockSpec((DN_HEADS, SCAN_CHUNKS, DN_DIM, DN_DIM), lambda c: (0, cidx(c), 0, 0))
    return hc, qk, egl, st


def _heads(ref, i):
    return jnp.stack([ref[pl.ds(i * CHUNK, CHUNK), pl.ds(h * DN_DIM, DN_DIM)] for h in range(DN_HEADS)])


def _dn_scan_fwd(u, w, qe, kd, qk, egl):
    S = u.shape[0]
    nc = S // CHUNK
    hc, qks, egls, st = _dn_scan_specs(nc, False)

    def body(u_ref, w_ref, qe_ref, kd_ref, qk_ref, egl_ref, o_ref, st_ref, s_scr):
        @pl.when(pl.program_id(0) == 0)
        def _():
            s_scr[...] = jnp.zeros_like(s_scr)

        s = s_scr[...]
        for i in range(SCAN_CHUNKS):
            rows = pl.ds(i * CHUNK, CHUNK)
            st_ref[:, i] = s
            s, o = _dn_step(s, _heads(u_ref, i), _heads(w_ref, i), _heads(qe_ref, i), _heads(kd_ref, i),
                            qk_ref[:, rows, :], egl_ref[:, i])
            for h in range(DN_HEADS):
                o_ref[rows, pl.ds(h * DN_DIM, DN_DIM)] = o[h]
        s_scr[...] = s

    return pl.pallas_call(
        body, grid=(nc // SCAN_CHUNKS,), in_specs=[hc, hc, hc, hc, qks, egls], out_specs=[hc, st],
        out_shape=[jax.ShapeDtypeStruct((S, DN_WIDTH), F32), jax.ShapeDtypeStruct((DN_HEADS, nc, DN_DIM, DN_DIM), F32)],
        scratch_shapes=[pltpu.VMEM((DN_HEADS, DN_DIM, DN_DIM), F32)], name="dn_scan_fwd",
        compiler_params=_params(("arbitrary",)))(u, w, qe, kd, qk, egl)


def _dn_scan_bwd(u, w, qe, kd, qk, egl, states, do):
    S = u.shape[0]
    nc = S // CHUNK
    hc, qks, egls, st = _dn_scan_specs(nc, True)

    def body(u_ref, w_ref, qe_ref, kd_ref, qk_ref, egl_ref, st_ref, do_ref,
             du_ref, dw_ref, dqe_ref, dkd_ref, dqk_ref, degl_ref, ds_scr):
        @pl.when(pl.program_id(0) == 0)
        def _():
            ds_scr[...] = jnp.zeros_like(ds_scr)

        ds = ds_scr[...]
        for i in reversed(range(SCAN_CHUNKS)):
            rows = pl.ds(i * CHUNK, CHUNK)
            _, vjp = jax.vjp(_dn_step, st_ref[:, i], _heads(u_ref, i), _heads(w_ref, i), _heads(qe_ref, i),
                             _heads(kd_ref, i), qk_ref[:, rows, :], egl_ref[:, i])
            ds, du, dw, dqe, dkd, dqk, degl = vjp((ds, _heads(do_ref, i)))
            dqk_ref[:, rows, :] = dqk
            degl_ref[:, i] = degl
            for h in range(DN_HEADS):
                cols = pl.ds(h * DN_DIM, DN_DIM)
                du_ref[rows, cols] = du[h]
                dw_ref[rows, cols] = dw[h]
                dqe_ref[rows, cols] = dqe[h]
                dkd_ref[rows, cols] = dkd[h]
        ds_scr[...] = ds

    wide = jax.ShapeDtypeStruct((S, DN_WIDTH), F32)
    return pl.pallas_call(
        body, grid=(nc // SCAN_CHUNKS,), in_specs=[hc, hc, hc, hc, qks, egls, st, hc],
        out_specs=[hc, hc, hc, hc, qks, egls],
        out_shape=[wide, wide, wide, wide, jax.ShapeDtypeStruct((DN_HEADS, S, CHUNK), F32),
                   jax.ShapeDtypeStruct((DN_HEADS, nc, 1, LANES), F32)],
        scratch_shapes=[pltpu.VMEM((DN_HEADS, DN_DIM, DN_DIM), F32)], name="dn_scan_bwd",
        compiler_params=_params(("arbitrary",)))(u, w, qe, kd, qk, egl, states, do)


def _dn_out_fwd(o, proj, gain, tm=1024):
    S = o.shape[0]

    def body(o_ref, z_ref, g_ref, y_ref):
        y_ref[...] = _dn_out(o_ref[...], z_ref[...], g_ref[...]).astype(BF16)

    hs = pl.BlockSpec((tm, LANES), lambda i, h: (i, h))
    zs = pl.BlockSpec((tm, LANES), lambda i, h: (i, P_Z // LANES + h))
    return pl.pallas_call(
        body, grid=(S // tm, DN_HEADS), in_specs=[hs, zs, _full((1, DN_DIM))], out_specs=hs,
        out_shape=jax.ShapeDtypeStruct((S, DN_WIDTH), BF16), name="dn_out_fwd",
        compiler_params=_params(("parallel", "parallel")))(o, proj, gain)


_ANY = pl.BlockSpec(memory_space=pl.ANY)


def _dn_out_bwd(o, proj, gain, dy, dproj, tm=1024):
    S = o.shape[0]

    def body(o_ref, z_ref, g_ref, dy_ref, _, do_ref, dz_ref, dg_ref):
        _, vjp = jax.vjp(_dn_out, o_ref[...], z_ref[...], g_ref[...])
        do, dz, dg = vjp(dy_ref[...])
        do_ref[...] = do
        dz_ref[...] = dz.astype(BF16)

        @pl.when((pl.program_id(0) == 0) & (pl.program_id(1) == 0))
        def _():
            dg_ref[...] = jnp.zeros_like(dg_ref)

        dg_ref[...] += dg

    hs = pl.BlockSpec((tm, LANES), lambda i, h: (i, h))
    zs = pl.BlockSpec((tm, LANES), lambda i, h: (i, P_Z // LANES + h))
    return pl.pallas_call(
        body, grid=(S // tm, DN_HEADS), in_specs=[hs, zs, _full((1, DN_DIM)), hs, _ANY],
        out_specs=[hs, zs, _full((1, DN_DIM))],
        out_shape=[jax.ShapeDtypeStruct((S, DN_WIDTH), F32), jax.ShapeDtypeStruct(dproj.shape, dproj.dtype),
                   jax.ShapeDtypeStruct((1, DN_DIM), F32)],
        input_output_aliases={4: 1},
        name="dn_out_bwd", compiler_params=_params(("arbitrary", "arbitrary")))(o, proj, gain, dy, dproj)


def _rel_buckets():
    qi = np.arange(BLOCK)[:, None]
    kj = np.arange(2 * BLOCK)[None, :]
    n = np.maximum(BLOCK + qi - kj, 0)
    max_exact = REL_BUCKETS // 2
    nf = np.maximum(n, 1).astype(np.float32)
    large = max_exact + (np.log(nf / np.float32(max_exact)) / np.float32(math.log(REL_MAX_DIST / max_exact))
                         * np.float32(REL_BUCKETS - max_exact)).astype(np.int32)
    large = np.minimum(large, REL_BUCKETS - 1)
    return np.where(n < max_exact, n, large).astype(np.int32)


def _bias_fwd(rel_bias):
    buckets = jnp.asarray(_rel_buckets())

    def body(rb_ref, bk_ref, o_ref):
        bk = bk_ref[...]
        for h in range(SWA_HEADS):
            acc = jnp.zeros((BLOCK, 2 * BLOCK), F32)
            for b in range(REL_BUCKETS):
                acc = jnp.where(bk == b, rb_ref[b, h], acc)
            for first in range(2):
                o_ref[first, h] = jnp.where(_swa_mask(1 - first), acc, -jnp.inf)

    return pl.pallas_call(
        body, in_specs=[pl.BlockSpec(memory_space=pltpu.SMEM), pl.BlockSpec(memory_space=pltpu.VMEM)],
        out_specs=pl.BlockSpec(memory_space=pltpu.VMEM),
        out_shape=jax.ShapeDtypeStruct((2, SWA_HEADS, BLOCK, 2 * BLOCK), F32), name="swa_bias_fwd",
        compiler_params=_params())(rel_bias, buckets)


def _bias_bwd(dbias):
    buckets = jnp.asarray(_rel_buckets())

    def body(d_ref, bk_ref, o_ref):
        bk = bk_ref[...]
        lane = lax.broadcasted_iota(jnp.int32, (1, LANES), 1)
        for h in range(SWA_HEADS):
            d = d_ref[h]
            row = jnp.zeros((1, LANES), F32)
            for b in range(REL_BUCKETS):
                part = jnp.sum(jnp.where(bk == b, d, 0.0), axis=1, keepdims=True)
                row = jnp.where(lane == b, jnp.sum(part, axis=0, keepdims=True), row)
            o_ref[h:h + 1, :] = row

    return pl.pallas_call(
        body, in_specs=[pl.BlockSpec(memory_space=pltpu.VMEM), pl.BlockSpec(memory_space=pltpu.VMEM)],
        out_specs=pl.BlockSpec(memory_space=pltpu.VMEM),
        out_shape=jax.ShapeDtypeStruct((SWA_HEADS, LANES), F32), name="swa_bias_bwd",
        compiler_params=_params())(dbias, buckets)


def _swa_mask(n):
    qi = lax.broadcasted_iota(jnp.int32, (BLOCK, 2 * BLOCK), 0)
    kj = lax.broadcasted_iota(jnp.int32, (BLOCK, 2 * BLOCK), 1)
    dist = BLOCK + qi - kj
    return (dist >= 0) & (dist < WINDOW) & ((n > 0) | (kj >= BLOCK))


def _swa_in_specs():
    q = pl.BlockSpec((BLOCK, SWA_WIDTH), lambda n: (n, P_SQ // SWA_WIDTH))
    kc = pl.BlockSpec((BLOCK, SWA_KVW), lambda n: (n, P_SK // SWA_KVW))
    kp = pl.BlockSpec((BLOCK, SWA_KVW), lambda n: (jnp.maximum(n - 1, 0), P_SK // SWA_KVW))
    vc = pl.BlockSpec((BLOCK, SWA_KVW), lambda n: (n, P_SV // SWA_KVW))
    vp = pl.BlockSpec((BLOCK, SWA_KVW), lambda n: (jnp.maximum(n - 1, 0), P_SV // SWA_KVW))
    band = pl.BlockSpec((None, SWA_HEADS, BLOCK, 2 * BLOCK), lambda n: (jnp.where(n == 0, 1, 0), 0, 0, 0))
    small = [_full((1, SWA_DIM)), _full((1, SWA_DIM)), _full((1, SWA_HEADS)), band]
    return [q, kp, kc, vp, vc] + small


def _swa_load(q_ref, kp_ref, kc_ref, vp_ref, vc_ref, s_ref):
    q = jnp.stack([q_ref[:, pl.ds(h * SWA_DIM, SWA_DIM)] for h in range(SWA_HEADS)])
    kbands, vbands = [], []
    for kv in range(SWA_KV):
        cols = pl.ds(kv * SWA_DIM, SWA_DIM)
        kbands += [jnp.concatenate([kp_ref[:, cols], kc_ref[:, cols]], axis=0)] * SWA_GROUP
        vbands += [jnp.concatenate([vp_ref[:, cols], vc_ref[:, cols]], axis=0)] * SWA_GROUP
    sinks = jnp.stack([s_ref[:, pl.ds(h, 1)] for h in range(SWA_HEADS)])
    return q, jnp.stack(kbands), jnp.stack(vbands), sinks


def _swa_fwd(proj, q_gain, k_gain, sinks, bias):
    S = proj.shape[0]

    def body(q_ref, kp_ref, kc_ref, vp_ref, vc_ref, qg_ref, kg_ref, s_ref, bias_ref, y_ref):
        q, kband, vband, sk = _swa_load(q_ref, kp_ref, kc_ref, vp_ref, vc_ref, s_ref)
        out = _swa_block(q, kband, vband, qg_ref[...], kg_ref[...], sk, bias_ref[...])
        for h in range(SWA_HEADS):
            y_ref[:, pl.ds(h * SWA_DIM, SWA_DIM)] = out[h].astype(BF16)

    return pl.pallas_call(
        body, grid=(S // BLOCK,), in_specs=_swa_in_specs(),
        out_specs=pl.BlockSpec((BLOCK, SWA_WIDTH), lambda n: (n, 0)),
        out_shape=jax.ShapeDtypeStruct((S, SWA_WIDTH), BF16), name="swa_fwd",
        compiler_params=_params(("parallel",)))(proj, proj, proj, proj, proj, q_gain, k_gain, sinks, bias)


def _swa_bwd(proj, q_gain, k_gain, sinks, bias, dy, dproj):
    S = proj.shape[0]

    def body(q_ref, kp_ref, kc_ref, vp_ref, vc_ref, qg_ref, kg_ref, s_ref, bias_ref, dy_ref, _,
             dq_ref, dk_ref, dv_ref, dqg_ref, dkg_ref, ds_ref, dbias_ref):
        n = pl.program_id(0)

        @pl.when(n == 0)
        def _():
            for r in (dk_ref, dv_ref, dqg_ref, dkg_ref, ds_ref, dbias_ref):
                r[...] = jnp.zeros_like(r)

        cur = pl.ds(pl.multiple_of(n * BLOCK, BLOCK), BLOCK)
        prev = pl.ds(pl.multiple_of(jnp.maximum(n - 1, 0) * BLOCK, BLOCK), BLOCK)
        q, kband, vband, sk = _swa_load(q_ref, kp_ref, kc_ref, vp_ref, vc_ref, s_ref)
        _, vjp = jax.vjp(_swa_block, q, kband, vband, qg_ref[...], kg_ref[...], sk, bias_ref[...])
        dy = jnp.stack([dy_ref[:, pl.ds(h * SWA_DIM, SWA_DIM)] for h in range(SWA_HEADS)])
        dq, dkb, dvb, dqg, dkg, dsk, dbs = vjp(dy)
        for h in range(SWA_HEADS):
            dq_ref[:, pl.ds(h * SWA_DIM, SWA_DIM)] = dq[h].astype(BF16)
            ds_ref[:, pl.ds(h, 1)] += dsk[h]
        dbias_ref[...] += dbs
        dqg_ref[...] += dqg
        dkg_ref[...] += dkg
        for kv in range(SWA_KV):
            cols = pl.ds(kv * SWA_DIM, SWA_DIM)
            group = range(kv * SWA_GROUP, (kv + 1) * SWA_GROUP)
            dk_kv = sum(dkb[h] for h in group)
            dv_kv = sum(dvb[h] for h in group)
            dk_ref[cur, cols] += dk_kv[BLOCK:]
            dv_ref[cur, cols] += dv_kv[BLOCK:]

            @pl.when(n > 0)
            def _(cols=cols, dk_kv=dk_kv, dv_kv=dv_kv):
                dk_ref[prev, cols] += dk_kv[:BLOCK]
                dv_ref[prev, cols] += dv_kv[:BLOCK]

    return pl.pallas_call(
        body, grid=(S // BLOCK,),
        in_specs=_swa_in_specs() + [pl.BlockSpec((BLOCK, SWA_WIDTH), lambda n: (n, 0)),
                                    pl.BlockSpec(memory_space=pl.ANY)],
        out_specs=[pl.BlockSpec((BLOCK, SWA_WIDTH), lambda n: (n, P_SQ // SWA_WIDTH)), _full((S, SWA_KVW)),
                   _full((S, SWA_KVW)), _full((1, SWA_DIM)), _full((1, SWA_DIM)), _full((1, SWA_HEADS)),
                   _full((SWA_HEADS, BLOCK, 2 * BLOCK))],
        out_shape=[jax.ShapeDtypeStruct(dproj.shape, dproj.dtype), jax.ShapeDtypeStruct((S, SWA_KVW), F32),
                   jax.ShapeDtypeStruct((S, SWA_KVW), F32), jax.ShapeDtypeStruct((1, SWA_DIM), F32),
                   jax.ShapeDtypeStruct((1, SWA_DIM), F32), jax.ShapeDtypeStruct((1, SWA_HEADS), F32),
                   jax.ShapeDtypeStruct((SWA_HEADS, BLOCK, 2 * BLOCK), F32)],
        input_output_aliases={10: 0},
        name="swa_bwd", compiler_params=_params(("arbitrary",)),
    )(proj, proj, proj, proj, proj, q_gain, k_gain, sinks, bias, dy, dproj)


def _kv_into(dproj, dk, dv, tm=1024):
    S = dk.shape[0]

    def body(dk_ref, dv_ref, _, o_ref):
        o_ref[:, :SWA_KVW] = dk_ref[...].astype(BF16)
        o_ref[:, SWA_KVW:] = dv_ref[...].astype(BF16)

    return pl.pallas_call(
        body, grid=(S // tm,), in_specs=[_row(tm, SWA_KVW), _row(tm, SWA_KVW), pl.BlockSpec(memory_space=pl.ANY)],
        out_specs=_row(tm, 2 * SWA_KVW, P_SK // (2 * SWA_KVW)),
        out_shape=jax.ShapeDtypeStruct(dproj.shape, dproj.dtype), input_output_aliases={2: 0},
        name="swa_kv_into", compiler_params=_params(("parallel",)))(dk, dv, dproj)


def _position():
    return lax.axis_index("x"), lax.axis_index("y"), lax.axis_index("c")


_HBM = pl.BlockSpec(memory_space=pltpu.HBM)
_SEM = pl.BlockSpec(memory_space=pltpu.SEMAPHORE)
_DATAFLOW = pltpu.SideEffectType.DATAFLOW_SIDE_EFFECTING


def _two_level_copies(x_refs, out_refs, send_sems, recv_sems):
    x, y, c = _position()
    me, sibling = (x, y, c), (x, y, 1 - c)
    chips = [(1 - x, y), (x, 1 - y), (1 - x, 1 - y)]

    def copy(a, k, block, to, own=False):
        px, py, pc = block
        slot = out_refs[a].at[4 * px + 2 * py + pc]
        return pltpu.make_async_remote_copy(
            src_ref=x_refs[a] if own else slot, dst_ref=slot, send_sem=send_sems.at[7 * a + k],
            recv_sem=recv_sems.at[7 * a + k], device_id=to, device_id_type=MESH_ID)

    return copy, me, sibling, chips


def _all_gather_start(shards, name):
    na = len(shards)
    lands = [lax.empty((N_DEV,) + s.shape, s.dtype) for s in shards]

    def body(*refs):
        x_refs, out_refs = refs[:na], refs[na:2 * na]
        send_sems, recv_sems = refs[2 * na], refs[2 * na + 1]
        token = refs[-1]
        copy, me, sibling, chips = _two_level_copies(x_refs, out_refs, send_sems, recv_sems)
        for a in range(na):
            copy(a, 0, me, sibling, own=True).start()
            for j, chip in enumerate(chips):
                copy(a, 1 + j, me, (*chip, me[2]), own=True).start()
        token[...] = jnp.zeros_like(token)

    hbm = lambda a: pltpu.HBM(a.shape, a.dtype)
    out = pl.pallas_call(
        body, name=name,
        out_shape=(pltpu.SemaphoreType.DMA((7 * na,)), pltpu.SemaphoreType.DMA((7 * na,)),
                   *[hbm(s) for s in shards], *[hbm(l) for l in lands], jax.ShapeDtypeStruct((8, LANES), F32)),
        in_specs=[_HBM] * (2 * na),
        out_specs=(_SEM, _SEM, *[_HBM] * (2 * na), pl.BlockSpec(memory_space=pltpu.VMEM)),
        input_output_aliases={i: 2 + i for i in range(2 * na)},
        compiler_params=pltpu.CompilerParams(has_side_effects=_DATAFLOW),
    )(*[pltpu.with_memory_space_constraint(s, pltpu.HBM) for s in shards],
      *[pltpu.with_memory_space_constraint(l, pltpu.HBM) for l in lands])
    return (out[0], out[1], list(out[2:2 + na]), list(out[2 + na:2 + 2 * na])), out[-1]


def _all_gather_finish(handle, after, name):
    send_sems, recv_sems, srcs, lands = handle
    na = len(srcs)

    def body(*refs):
        x_refs, out_refs = refs[:na], refs[na:2 * na]
        copy, me, sibling, chips = _two_level_copies(x_refs, out_refs, refs[2 * na], refs[2 * na + 1])
        c = me[2]
        for a in range(na):
            copy(a, 0, sibling, me).wait_recv()
            copy(a, 0, me, sibling, own=True).wait_send()
            for j, chip in enumerate(chips):
                copy(a, 1 + j, (*chip, c), me).wait_recv()
                copy(a, 1 + j, me, (*chip, c), own=True).wait_send()

    hbm = lambda a: pltpu.HBM(a.shape, a.dtype)
    out = pl.pallas_call(
        body, name=name, out_shape=(*[hbm(s) for s in srcs], *[hbm(l) for l in lands]),
        in_specs=[_HBM] * (2 * na) + [_SEM, _SEM] + [pl.BlockSpec(memory_space=pl.ANY)] * len(after),
        out_specs=tuple([_HBM] * (2 * na)), input_output_aliases={i: i for i in range(2 * na)},
        compiler_params=pltpu.CompilerParams(has_side_effects=_DATAFLOW),
    )(*srcs, *lands, send_sems, recv_sems, *after)
    return _all_gather_relay(list(out[:na]), list(out[na:]), name + "_relay")


def _all_gather_relay(shards, lands, name):
    na = len(lands)

    def body(*refs):
        x_refs, out_refs = refs[:na], refs[2 * na:3 * na]
        send_sems, recv_sems, local_sems = refs[3 * na:]
        x, y, c = _position()
        chips = [(1 - x, y), (x, 1 - y), (1 - x, 1 - y)]
        mine = [pltpu.make_async_copy(x_refs[a], out_refs[a].at[4 * x + 2 * y + c], local_sems.at[a])
                for a in range(na)]
        for cp in mine:
            cp.start()

        def copy(a, j, core):
            px, py = chips[j]
            slot = out_refs[a].at[4 * px + 2 * py + core]
            return pltpu.make_async_remote_copy(
                src_ref=slot, dst_ref=slot, send_sem=send_sems.at[3 * a + j], recv_sem=recv_sems.at[3 * a + j],
                device_id=(x, y, 1 - c), device_id_type=MESH_ID)

        sends = [copy(a, j, c) for a in range(na) for j in range(3)]
        for cp in sends:
            cp.start()
        for a in range(na):
            for j in range(3):
                copy(a, j, 1 - c).wait_recv()
        for cp in sends:
            cp.wait_send()
        for cp in mine:
            cp.wait()

    return pl.pallas_call(
        body, in_specs=[pl.BlockSpec(memory_space=pl.ANY)] * (2 * na),
        out_specs=[pl.BlockSpec(memory_space=pl.ANY)] * na,
        out_shape=[jax.ShapeDtypeStruct(l.shape, l.dtype) for l in lands],
        input_output_aliases={na + i: i for i in range(na)},
        scratch_shapes=[pltpu.SemaphoreType.DMA((3 * na,)), pltpu.SemaphoreType.DMA((3 * na,)),
                        pltpu.SemaphoreType.DMA((na,))],
        name=name)(*shards, *lands)


def _peers(x, y, c):
    out = []
    for k in range(1, N_DEV):
        px, py, pc = x ^ (k >> 2), y ^ ((k >> 1) & 1), c ^ (k & 1)
        out.append(((px, py, pc), 4 * px + 2 * py + pc))
    return out


def _split_copies(src_refs, land_refs, send_sems, recv_sems, scatter):
    x, y, c = _position()
    me = 4 * x + 2 * y + c
    sends, recvs = [], []
    for k, (peer_id, peer) in enumerate(_peers(x, y, c)):
        for a, (src, land) in enumerate(zip(src_refs, land_refs)):
            sems = dict(send_sem=send_sems.at[7 * a + k], recv_sem=recv_sems.at[7 * a + k],
                        device_id=peer_id, device_id_type=MESH_ID)
            mine = src.at[peer] if scatter else src
            sends.append(pltpu.make_async_remote_copy(src_ref=mine, dst_ref=land.at[me], **sems))
            recvs.append(pltpu.make_async_remote_copy(src_ref=mine, dst_ref=land.at[peer], **sems))
    return sends, recvs


def _all_gather_direct(shards, name, after):
    na, nb = len(shards), len(after)

    def body(*refs):
        x_refs, out_refs = refs[:na], refs[na + nb:2 * na + nb]
        send_sems, recv_sems, local_sems = refs[2 * na + nb:]
        x, y, c = _position()
        me = 4 * x + 2 * y + c
        local = [pltpu.make_async_copy(x_refs[a], out_refs[a].at[me], local_sems.at[a]) for a in range(na)]
        sends, recvs = _split_copies(x_refs, out_refs, send_sems, recv_sems, False)
        for cp in local + sends:
            cp.start()
        for cp in recvs:
            cp.wait_recv()
        for cp in sends:
            cp.wait_send()
        for cp in local:
            cp.wait()

    return pl.pallas_call(
        body, in_specs=[pl.BlockSpec(memory_space=pl.ANY)] * (na + nb),
        out_specs=[pl.BlockSpec(memory_space=pl.ANY)] * na,
        out_shape=[jax.ShapeDtypeStruct((N_DEV,) + s.shape, s.dtype) for s in shards],
        scratch_shapes=[pltpu.SemaphoreType.DMA((7 * na,)), pltpu.SemaphoreType.DMA((7 * na,)),
                        pltpu.SemaphoreType.DMA((na,))],
        name=name)(*shards, *after)


def _exchange_start(srcs, scatter, name, after=None):
    na = len(srcs)
    lands = [lax.empty(s.shape if scatter else (N_DEV,) + s.shape, s.dtype) for s in srcs]
    extra = [] if after is None else [after]

    def body(*refs):
        src_refs, land_refs = refs[:na], refs[na:2 * na]
        send_sems, recv_sems = refs[2 * na + len(extra)], refs[2 * na + len(extra) + 1]
        token = refs[-1]
        sends, _ = _split_copies(src_refs, land_refs, send_sems, recv_sems, scatter)
        for cp in sends:
            cp.start()
        token[...] = jnp.zeros_like(token)

    hbm = lambda a: pltpu.HBM(a.shape, a.dtype)
    out = pl.pallas_call(
        body, name=name,
        out_shape=(pltpu.SemaphoreType.DMA((7 * na,)), pltpu.SemaphoreType.DMA((7 * na,)),
                   *[hbm(s) for s in srcs], *[hbm(l) for l in lands], jax.ShapeDtypeStruct((8, LANES), F32)),
        in_specs=[_HBM] * (2 * na) + [pl.BlockSpec(memory_space=pl.ANY)] * len(extra),
        out_specs=(_SEM, _SEM, *[_HBM] * (2 * na), pl.BlockSpec(memory_space=pltpu.VMEM)),
        input_output_aliases={i: 2 + i for i in range(2 * na)},
        compiler_params=pltpu.CompilerParams(has_side_effects=_DATAFLOW),
    )(*[pltpu.with_memory_space_constraint(s, pltpu.HBM) for s in srcs],
      *[pltpu.with_memory_space_constraint(l, pltpu.HBM) for l in lands], *extra)
    return (out[0], out[1], list(out[2:2 + na]), list(out[2 + na:2 + 2 * na])), out[-1]


def _exchange_wait(handle, after, scatter, name):
    send_sems, recv_sems, srcs, lands = handle
    na = len(srcs)

    def body(*refs):
        src_refs, land_refs = refs[:na], refs[na:2 * na]
        s_sems, r_sems = refs[2 * na], refs[2 * na + 1]
        sends, recvs = _split_copies(src_refs, land_refs, s_sems, r_sems, scatter)
        for cp in sends:
            cp.wait_send()
        for cp in recvs:
            cp.wait_recv()

    hbm = lambda a: pltpu.HBM(a.shape, a.dtype)
    out = pl.pallas_call(
        body, name=name, out_shape=(*[hbm(s) for s in srcs], *[hbm(l) for l in lands]),
        in_specs=[_HBM] * (2 * na) + [_SEM, _SEM, pl.BlockSpec(memory_space=pl.ANY)],
        out_specs=tuple([_HBM] * (2 * na)), input_output_aliases={i: i for i in range(2 * na)},
        compiler_params=pltpu.CompilerParams(has_side_effects=_DATAFLOW),
    )(*srcs, *lands, send_sems, recv_sems, after)
    return list(out[:na]), list(out[na:])


def _own_slot(landed, own):
    me = 4 * lax.axis_index("x") + 2 * lax.axis_index("y") + lax.axis_index("c")
    return lax.dynamic_update_slice_in_dim(landed, own[None], me, axis=0)


def _adam_update(parts, w, m, v, name, tr=256, turned=False):
    _, r, c = w.shape
    tr = _pick_rows(r, tr)
    cp = parts.shape[2]
    flat = turned and c % 8 != 0
    at = (slice(None), 0) if flat else (0,)

    def body(p_ref, w_ref, m_ref, v_ref, g_ref, d_ref, nm_ref, nv_ref):
        cols = pl.ds(0, cp if turned else c)
        g = p_ref[0, :, cols].astype(F32)
        for i in range(1, N_DEV):
            g = g + p_ref[i, :, cols].astype(F32)
        if turned:
            g = g.T[:c]
        delta, nm, nv = _adamw(w_ref[at], g, m_ref[at], v_ref[at])
        g_ref[at] = g
        d_ref[at] = delta
        nm_ref[at] = nm
        nv_ref[at] = nv

    there, back = ((2, 0, 1), (1, 2, 0)) if flat else ((0, 2, 1), (0, 2, 1))
    if turned:
        w, m, v = (jnp.transpose(a, there) for a in (w, m, v))
        rs = pl.BlockSpec((c, 1, tr), lambda i: (0, 0, i)) if flat else pl.BlockSpec((1, c, tr), lambda i: (0, 0, i))
    else:
        rs = pl.BlockSpec((1, tr, c), lambda i: (0, i, 0))
    outs = pl.pallas_call(
        body, grid=(r // tr,), in_specs=[pl.BlockSpec((N_DEV, tr, cp), lambda i: (0, i, 0)), rs, rs, rs],
        out_specs=[rs] * 4, out_shape=[jax.ShapeDtypeStruct(w.shape, F32)] * 4, name=name,
        compiler_params=_params(("parallel",)))(parts, w, m, v)
    return [*([jnp.transpose(o, back) for o in outs] if turned else outs), outs[0]]


def _pick_rows(rows, target):
    if rows <= target:
        return rows
    t = target
    while t >= 16:
        if rows % t == 0:
            return t
        t -= 16
    return rows


BIG = ("w_in", "w_branch_dn", "w_branch_swa", "w_out", "w_gate", "w_up", "w_down")
IN_SHARD, IN_WIRE = D_IN // N_DEV, 640
FF_SHARD, FF_WIRE = D_FF // N_DEV, 384
D_FFP = N_DEV * FF_WIRE
BIG_SHAPES = {"w_in": ((D_MODEL, IN_SHARD), (D_MODEL, IN_WIRE)),
              "w_branch_dn": ((DN_WIDTH, LANES), (DN_WIDTH, LANES)),
              "w_branch_swa": ((SWA_WIDTH, LANES), (SWA_WIDTH, LANES)),
              "w_out": ((LANES, D_MODEL), (LANES, D_MODEL)),
              "w_gate": ((D_MODEL, FF_SHARD), (D_MODEL, FF_WIRE)),
              "w_up": ((D_MODEL, FF_SHARD), (D_MODEL, FF_WIRE)),
              "w_down": ((FF_SHARD, D_MODEL), (FF_WIRE, D_MODEL))}
CONV_SHARD, CONV_WIRE = (DN_CONV, DN_QKV // N_DEV), (8, 256)


def _pad_to(a, shape):
    return jnp.pad(a, [(0, t - s) for s, t in zip(a.shape, shape)])


IN_TILE_ROWS = 256
_IN_SEGS = ((R_GATE, 2048, P_GATE), (R_QKV, DN_QKV, P_QKV), (R_Z, DN_WIDTH, P_Z), (R_SQ, SWA_WIDTH, P_SQ),
            (R_SK, SWA_KVW, P_SK), (R_SV, SWA_KVW, P_SV), (R_B, 8, P_BA))


def _w_in_from_blocks(blocks, after):
    tm = IN_TILE_ROWS

    def body(b_ref, _, o_ref):
        parts = []
        for rs, n, _ in _IN_SEGS:
            for dev in range(N_DEV):
                lo, hi = max(rs, IN_SHARD * dev), min(rs + n, IN_SHARD * (dev + 1))
                if lo < hi:
                    parts.append(b_ref[dev][:, lo - IN_SHARD * dev:hi - IN_SHARD * dev])
        parts.append(jnp.zeros((tm, P_WIDTH - P_BA - 8), b_ref.dtype))
        o_ref[...] = jnp.concatenate(parts, axis=1)

    return pl.pallas_call(
        body, grid=(D_MODEL // tm,),
        in_specs=[pl.BlockSpec((N_DEV, tm, IN_WIRE), lambda i: (0, i, 0)), pl.BlockSpec(memory_space=pl.ANY)],
        out_specs=pl.BlockSpec((tm, P_WIDTH), lambda i: (i, 0)),
        out_shape=jax.ShapeDtypeStruct((D_MODEL, P_WIDTH), blocks.dtype), name="w_in_from_blocks",
        compiler_params=_params(("parallel",)))(blocks, after)


def _w_in_to_blocks(g):
    tm = IN_TILE_ROWS

    def body(g_ref, o_ref):
        for dev in range(N_DEV):
            parts = []
            for rs, n, ps in sorted(_IN_SEGS):
                lo, hi = max(rs, IN_SHARD * dev), min(rs + n, IN_SHARD * (dev + 1))
                if lo < hi:
                    parts.append(g_ref[:, ps + lo - rs:ps + hi - rs])
            parts.append(jnp.zeros((tm, IN_WIRE - IN_SHARD), g_ref.dtype))
            o_ref[dev] = jnp.concatenate(parts, axis=1)

    return pl.pallas_call(
        body, grid=(D_MODEL // tm,), in_specs=[pl.BlockSpec((tm, P_WIDTH), lambda i: (i, 0))],
        out_specs=pl.BlockSpec((N_DEV, tm, IN_WIRE), lambda i: (0, i, 0)),
        out_shape=jax.ShapeDtypeStruct((N_DEV, D_MODEL, IN_WIRE), g.dtype), name="w_in_to_blocks",
        compiler_params=_params(("parallel",)))(g)


SMALL = {"attn_norm": (0, (1, D_MODEL)), "ffn_norm": (1, (1, D_MODEL)), "dn_out_norm": (2, (1, DN_DIM)),
         "swa_q_norm": (3, (1, SWA_DIM)), "swa_k_norm": (4, (1, SWA_DIM)), "dn_a_log": (5, (1, DN_HEADS)),
         "dn_dt_bias": (6, (1, DN_HEADS)), "swa_sinks": (7, (1, SWA_HEADS)), "rel_bias": (8, (REL_BUCKETS, SWA_HEADS))}
SMALL_SHEET = (48, D_MODEL)


LOSS_ROW = 40


def _small_pack(grads, loss_local):
    names = list(SMALL)

    def body(*refs):
        o_ref = refs[-1]
        o_ref[...] = jnp.zeros_like(o_ref)
        for n, ref in zip(names, refs):
            r0, (nr, nc) = SMALL[n]
            o_ref[r0:r0 + nr, 0:nc] = ref[...]
        o_ref[LOSS_ROW:LOSS_ROW + 1, 0:1] = refs[len(names)][...]

    return pl.pallas_call(
        body, in_specs=[pl.BlockSpec(memory_space=pltpu.VMEM)] * (len(names) + 1),
        out_specs=pl.BlockSpec(memory_space=pltpu.VMEM), out_shape=jax.ShapeDtypeStruct(SMALL_SHEET, F32),
        name="small_pack", compiler_params=_params())(*[grads[n].reshape(SMALL[n][1]) for n in names], loss_local)


def _small_update(sheets, w, m, v):
    names = list(SMALL)
    k = len(names)

    def body(*refs):
        p_ref = refs[0]
        ins, outs = refs[1:1 + 3 * k], refs[1 + 3 * k:]
        loss = p_ref[0, LOSS_ROW:LOSS_ROW + 1, 0:1]
        for i in range(1, N_DEV):
            loss = loss + p_ref[i, LOSS_ROW:LOSS_ROW + 1, 0:1]
        outs[4 * k][...] = loss
        for t, n in enumerate(names):
            r0, (nr, nc) = SMALL[n]
            g = p_ref[0, r0:r0 + nr, 0:nc]
            for i in range(1, N_DEV):
                g = g + p_ref[i, r0:r0 + nr, 0:nc]
            delta, nm, nv = _adamw(ins[t][...], g, ins[k + t][...], ins[2 * k + t][...])
            for kind, val in enumerate((g, delta, nm, nv)):
                outs[kind * k + t][...] = val

    shapes = [jax.ShapeDtypeStruct(SMALL[n][1], F32) for n in names]
    vm = pl.BlockSpec(memory_space=pltpu.VMEM)
    res = pl.pallas_call(
        body, in_specs=[vm] * (1 + 3 * k), out_specs=[vm] * (4 * k + 1),
        out_shape=shapes * 4 + [jax.ShapeDtypeStruct((1, 1), F32)], name="adam_small", compiler_params=_params(),
    )(sheets, *[d[n].reshape(SMALL[n][1]) for d in (w, m, v) for n in names])
    return {n: tuple(res[kind * k + t] for kind in range(4)) for t, n in enumerate(names)}, res[4 * k]


def kernel(x, attn_norm, w_in, dn_conv, dn_a_log, dn_dt_bias, dn_out_norm, swa_q_norm, swa_k_norm, swa_sinks, rel_bias, w_branch_dn, w_branch_swa, w_out, ffn_norm, w_gate, w_up, w_down, loss_target, m_attn_norm, m_w_in, m_dn_conv, m_dn_a_log, m_dn_dt_bias, m_dn_out_norm, m_swa_q_norm, m_swa_k_norm, m_swa_sinks, m_rel_bias, m_w_branch_dn, m_w_branch_swa, m_w_out, m_ffn_norm, m_w_gate, m_w_up, m_w_down, v_attn_norm, v_w_in, v_dn_conv, v_dn_a_log, v_dn_dt_bias, v_dn_out_norm, v_swa_q_norm, v_swa_k_norm, v_swa_sinks, v_rel_bias, v_w_branch_dn, v_w_branch_swa, v_w_out, v_ffn_norm, v_w_gate, v_w_up, v_w_down):
    args = dict(locals())
    S = x.shape[1]
    xs = x.reshape(S, D_MODEL)
    target = loss_target.reshape(S, D_MODEL)

    w_loc = {n: args[n].reshape(BIG_SHAPES[n][0]) for n in BIG}
    conv_loc = dn_conv.reshape(CONV_SHARD)
    def on_wire(n, zero=0.0):
        return _pad_to(w_loc[n] + zero, BIG_SHAPES[n][1]).astype(BF16)

    first_handle, first_token = _all_gather_start([on_wire("w_in"), _pad_to(conv_loc, CONV_WIRE)],
                                                  "all_gather_weights_start")
    zero = first_token[0, 0]
    h = _norm_fwd(xs, attn_norm + zero, "norm1_fwd")
    later = [n for n in BIG if n != "w_in"]
    wire = [on_wire(n, zero) for n in later]
    first = _all_gather_finish(first_handle, [h] + wire, "all_gather_weights_finish")
    rest_handle, rest_token = _exchange_start(wire, False, "gather_rest_start", after=first[1])
    w_pad = _w_in_from_blocks(first[0], rest_token)
    conv_w = jnp.concatenate([first[1][d, :DN_CONV, :CONV_SHARD[1]] for d in range(N_DEV)], axis=1)

    proj = _mm([(h, w_pad)], "nn", F32, "mm_in", 1024, 1664, j_outer=True)
    qkvn = _dn_conv_fwd(proj, conv_w)
    beta, g = _dn_gate_fwd(proj, dn_a_log, dn_dt_bias)
    u, w, qe, kd, qk, egl, tinv = _dn_prep_fwd(qkvn, g, beta)
    o, states = _dn_scan_fwd(u, w, qe, kd, qk, egl)
    y_dn = _dn_out_fwd(o, proj, dn_out_norm)
    bias = _bias_fwd(rel_bias)
    y_swa = _swa_fwd(proj, swa_q_norm, swa_k_norm, swa_sinks, bias)
    rest_src, rest_land = _exchange_wait(rest_handle, y_swa, False, "gather_rest_wait")
    G = {n: _own_slot(land, src) for n, src, land in zip(later, rest_src, rest_land)}
    w_bdn, w_bswa, w_g, w_u = G["w_branch_dn"], G["w_branch_swa"], G["w_gate"], G["w_up"]
    w_o = G["w_out"].reshape(D_MODEL, D_MODEL)
    w_d = G["w_down"].reshape(D_FFP, D_MODEL)
    gates = [(proj, P_GATE // 512), (proj, (P_GATE + D_MODEL) // 512)]
    a_dn, a_swa, merged = _mm_fused(
        [(y_dn, w_bdn), (y_swa, w_bswa)], "nn", "mm_branch_merge", 1024, 512,
        lambda p, e: (p[0], p[1], _merge(e[0], e[1], p[0], p[1])), gates, (F32, F32, BF16), b_blocks=True)

    def resid_norm(p, e):
        x1 = e[0] + p[0]
        return x1, _rms(x1, e[1])

    x1, h2 = _mm_fused([(merged, w_o)], "nn", "mm_out_norm", 512, D_MODEL, resid_norm,
                       [(xs, 0), (ffn_norm, None)], (F32, BF16))
    gate, up, act = _mm_fused([(h2, w_g), (h2, w_u)], "nn", "mm_gate_up_act", 1024, 768,
                              lambda p, e: (p[0], p[1], _act(p[0], p[1])), [], (F32, F32, BF16),
                              j_outer=True, b_blocks=True)

    def loss_head(p, e):
        diff = e[0] + p[0] - e[1]
        dy = diff * (1.0 / D_MODEL)
        part = jnp.sum(jnp.mean(diff * diff, axis=-1, keepdims=True), axis=0, keepdims=True) * 0.5
        return dy, dy, part

    dy, dy_b, loss_local = _mm_fused([(act, w_d)], "nn", "mm_down_loss", 512, D_MODEL, loss_head,
                                     [(x1, 0), (target, 0)], (F32, BF16), sum_shape=(1, 1))

    def act_bwd(p, e):
        _, vjp = jax.vjp(_act, e[0], e[1])
        return vjp(p[0])

    dgate, dup = _mm_fused([(dy_b, w_d)], "nt", "mm_dact_act", 1024, 768, act_bwd, [(gate, 0), (up, 0)],
                           (BF16, BF16), j_outer=True)
    g_w_down = _mm([(act, dy_b)], "tn", BF16, "mm_dw_down", 768, D_MODEL, j_outer=True)
    g_w_down = g_w_down.reshape(N_DEV, FF_WIRE, D_MODEL)
    g_w_gate = _mm([(h2, dgate)], "tn", BF16, "mm_dw_gate", D_MODEL, 768, out_blocks=True)
    g_w_up = _mm([(h2, dup)], "tn", BF16, "mm_dw_up", D_MODEL, 768, out_blocks=True)
    ffn_handle, ffn_token = _exchange_start([g_w_down, g_w_gate, g_w_up], True, "scatter_ffn_start")

    def norm_bwd(p, e):
        _, vjp = jax.vjp(_rms, e[0], e[2])
        dx, dgain = vjp(sum(p))
        dx = dx + e[1]
        return dx, dx, dgain

    dx1, dx1_b, g_ffn_norm = _mm_fused(
        [(dgate, w_g), (dup, w_u)], "nt", "mm_dh2_norm", 256, D_MODEL, norm_bwd,
        [(x1, 0), (dy, 0), (ffn_norm + ffn_token[0, 0], None)], (F32, BF16), b_blocks=True, sum_shape=(1, D_MODEL))
    def merge_bwd(p, e):
        _, vjp = jax.vjp(_merge, *e)
        dg0, dg1, da_dn, da_swa = vjp(p[0])
        return jnp.concatenate([dg0, dg1], axis=1), da_dn, da_swa

    dproj, da_dn, da_swa = _mm_fused(
        [(dx1_b, w_o)], "nt", "mm_dmerged_merge", 512, D_MODEL, merge_bwd,
        [(proj, P_GATE // D_MODEL), (proj, P_GATE // D_MODEL + 1), (a_dn, 0), (a_swa, 0)], (BF16,) * 3,
        wide_first=(P_WIDTH, 2 * D_MODEL))
    g_w_out = _mm([(merged, dx1_b)], "tn", BF16, "mm_dw_out", 512, D_MODEL, j_outer=True)
    g_w_out = g_w_out.reshape(N_DEV, LANES, D_MODEL)
    dy_dn = _mm([(da_dn, w_bdn)], "nt", F32, "mm_dy_dn", 1024, DN_WIDTH, b_blocks=True)
    dy_swa = _mm([(da_swa, w_bswa)], "nt", F32, "mm_dy_swa", 1024, SWA_WIDTH, b_blocks=True)
    g_w_bdn = _mm([(y_dn, da_dn)], "tn", BF16, "mm_dw_branch_dn", DN_WIDTH, 512, out_blocks=True)
    g_w_bswa = _mm([(y_swa, da_swa)], "tn", BF16, "mm_dw_branch_swa", SWA_WIDTH, 512, out_blocks=True)
    dproj, dsk, dsv, g_q_norm, g_k_norm, g_sinks, dbias = _swa_bwd(proj, swa_q_norm, swa_k_norm, swa_sinks, bias,
                                                                   dy_swa, dproj)
    dproj = _kv_into(dproj, dsk, dsv)
    g_rel_bias = _bias_bwd(dbias)[:, :REL_BUCKETS].T
    mix_handle, mix_token = _exchange_start([g_w_out, g_w_bdn, g_w_bswa], True, "scatter_mix_start")
    do, dproj, g_out_norm = _dn_out_bwd(o, proj, dn_out_norm + mix_token[0, 0], dy_dn, dproj)
    du, dw, dqe, dkd, dqk, degl = _dn_scan_bwd(u, w, qe, kd, qk, egl, states, do)
    dqkvn, dgd, dbeta = _dn_prep_bwd(qkvn, g, beta, tinv, du, dw, dqe, dkd, dqk, degl)
    dproj, dal, ddt = _dn_gate_bwd(proj, dn_a_log, dn_dt_bias, dbeta, dgd, dproj)
    g_a_log = dal.reshape(DN_HEADS, DN_DIM).sum(axis=1)
    g_dt_bias = ddt[0, DN_HEADS:2 * DN_HEADS]
    dproj, g_conv = _dn_conv_bwd(proj, conv_w, dqkvn, dproj)
    g_w_in = _w_in_to_blocks(_mm([(h, dproj)], "tn", BF16, "mm_dw_in", 512, 1664, j_outer=True))
    in_handle, in_token = _exchange_start([g_w_in], True, "scatter_in_start")
    dx, g_attn_norm = _mm_fused(
        [(dproj, w_pad)], "nt", "mm_dh_norm", 512, D_MODEL, lambda p, e: norm_bwd(p, e)[1:],
        [(xs, 0), (dx1, 0), (attn_norm + in_token[0, 0], None)], (F32,), sum_shape=(1, D_MODEL))

    g_small = {"attn_norm": g_attn_norm, "ffn_norm": g_ffn_norm, "rel_bias": g_rel_bias, "dn_out_norm": g_out_norm,
               "swa_q_norm": g_q_norm, "swa_k_norm": g_k_norm, "dn_a_log": g_a_log, "dn_dt_bias": g_dt_bias,
               "swa_sinks": g_sinks}
    me = 4 * lax.axis_index("x") + 2 * lax.axis_index("y") + lax.axis_index("c")
    outs = {}

    def finish(handle, group, name, after):
        srcs, lands = _exchange_wait(handle, after, True, name)
        for n, src, land in zip(group, srcs, lands):
            parts = _own_slot(land, lax.dynamic_index_in_dim(src, me, 0, keepdims=False))
            outs[n] = _adam_update(parts, args[n], args["m_" + n], args["v_" + n], "adam_" + n,
                                   turned=args[n].shape[2] % LANES != 0)

    finish(ffn_handle, ("w_down", "w_gate", "w_up"), "scatter_ffn_wait", dx)
    finish(mix_handle, ("w_out", "w_branch_dn", "w_branch_swa"), "scatter_mix_wait", dx)
    sheets, conv_all = _all_gather_direct([_small_pack(g_small, loss_local), _pad_to(g_conv, (8, DN_QKV))],
                                          "all_gather_small",
                                          after=[outs[n][4] for n in sorted(outs)])
    finish(in_handle, ("w_in",), "scatter_in_wait", sheets)
    conv_parts = lax.dynamic_slice(conv_all, (0, 0, me * CONV_SHARD[1]), (N_DEV,) + CONV_SHARD)
    outs["dn_conv"] = _adam_update(conv_parts, dn_conv, m_dn_conv, v_dn_conv, "adam_dn_conv")
    small_outs, loss = _small_update(sheets, {n: args[n] for n in SMALL}, {n: args["m_" + n] for n in SMALL},
                                     {n: args["v_" + n] for n in SMALL})
    outs.update(small_outs)

    names = ("attn_norm", "w_in", "dn_conv", "dn_a_log", "dn_dt_bias", "dn_out_norm", "swa_q_norm", "swa_k_norm",
             "swa_sinks", "rel_bias", "w_branch_dn", "w_branch_swa", "w_out", "ffn_norm", "w_gate", "w_up", "w_down")
    results = []
    for kind in range(4):
        results += [outs[n][kind].reshape(args[n].shape) for n in names]

    return (loss.reshape(()), dx.reshape(x.shape), *results)
```

```python
import math

import numpy as np
import jax
import jax.numpy as jnp
from jax import lax
from jax.experimental import pallas as pl
from jax.experimental.pallas import tpu as pltpu

F32 = jnp.float32
BF16 = jnp.bfloat16
HI = lax.Precision.HIGHEST

D_MODEL = 1024
DN_HEADS = 4
DN_DIM = 128
DN_WIDTH = 512
DN_QKV = 1536
DN_CONV = 4
CHUNK = 64
SWA_HEADS = 8
SWA_KV = 2
SWA_GROUP = 4
SWA_DIM = 64
SWA_WIDTH = 512
SWA_KVW = 128
WINDOW = 128
BLOCK = 128
REL_BUCKETS = 32
REL_MAX_DIST = 128
D_FF = 2816
D_IN = 4872
EPS = 1e-6
N_DEV = 8

ADAM_LR = 0.001
ADAM_B1 = 0.9
ADAM_B2 = 0.999
ADAM_EPS = 1e-08
ADAM_WD = 0.01
ADAM_STEP = 10

P_GATE, P_QKV, P_Z, P_SQ, P_SK, P_SV, P_BA = 0, 2048, 3584, 4096, 4608, 4736, 4864
P_WIDTH = 4992
R_QKV, R_Z, R_B, R_A, R_SQ, R_SK, R_SV, R_GATE = 0, 1536, 2048, 2052, 2056, 2568, 2696, 2824

VMEM_LIMIT = 56 * 1024 * 1024
LANES = 128
MESH_ID = pl.DeviceIdType.MESH


def _params(sem=None):
    return pltpu.CompilerParams(dimension_semantics=sem, vmem_limit_bytes=VMEM_LIMIT)


def _pick(dim, target):
    if dim <= target:
        return dim
    t = target - target % LANES
    while t >= LANES:
        if dim % t == 0:
            return t
        t -= LANES
    return dim


_DIMS = {"nn": (((1,), (0,)), ((), ())), "nt": (((1,), (1,)), ((), ())), "tn": (((0,), (0,)), ((), ()))}


def _tile_product(a_ref, b_ref, mode, b_blocks):
    a = a_ref[...].astype(BF16)
    b = jnp.concatenate([b_ref[d] for d in range(b_ref.shape[0])], axis=1) if b_blocks else b_ref[...]
    return lax.dot_general(a, b.astype(BF16), _DIMS[mode], preferred_element_type=F32)


def _mm(pairs, mode, out_dtype, name, bm, bn, j_outer=False, b_blocks=False, out_blocks=False):
    a0, b0 = pairs[0]
    cb = b0.shape[2] if b_blocks else None
    b_shape = (b0.shape[1], N_DEV * cb) if b_blocks else b0.shape
    if mode == "nn":
        (M, K), (K2, N) = a0.shape, b_shape
    elif mode == "nt":
        (M, K), (N, K2) = a0.shape, b_shape
    else:
        (K, M), (K2, N) = a0.shape, b_shape
    bm, bn = min(bm, M), min(bn, N)
    assert K == K2 and M % bm == 0 and N % bn == 0, (name, a0.shape, b0.shape, bm, bn)
    co = N // N_DEV
    assert not out_blocks or bn % co == 0
    dims = _DIMS[mode]
    n = len(pairs)

    def body(*refs):
        o_ref = refs[2 * n]
        acc = None
        for t in range(n):
            p = _tile_product(refs[2 * t], refs[2 * t + 1], mode, b_blocks)
            acc = p if acc is None else acc + p
        if out_blocks:
            for d in range(bn // co):
                o_ref[d] = acc[:, d * co:(d + 1) * co].astype(out_dtype)
        else:
            o_ref[...] = acc.astype(out_dtype)

    def ij(f):
        return (lambda j, i: f(i, j)) if j_outer else f

    a_spec = pl.BlockSpec((K, bm), ij(lambda i, j: (0, i))) if mode == "tn" else pl.BlockSpec((bm, K), ij(lambda i, j: (i, 0)))
    if b_blocks and mode == "nt":
        b_spec = pl.BlockSpec((N_DEV, bn, cb), ij(lambda i, j: (0, j, 0)))
    elif b_blocks:
        b_spec = pl.BlockSpec((bn // cb, K, cb), ij(lambda i, j: (j, 0, 0)))
    elif mode == "nt":
        b_spec = pl.BlockSpec((bn, K), ij(lambda i, j: (j, 0)))
    else:
        b_spec = pl.BlockSpec((K, bn), ij(lambda i, j: (0, j)))
    if out_blocks:
        out_spec = pl.BlockSpec((bn // co, bm, co), ij(lambda i, j: (j, i, 0)))
        out_shape = jax.ShapeDtypeStruct((N_DEV, M, co), out_dtype)
    else:
        out_spec = pl.BlockSpec((bm, bn), ij(lambda i, j: (i, j)))
        out_shape = jax.ShapeDtypeStruct((M, N), out_dtype)
    grid = (N // bn, M // bm) if j_outer else (M // bm, N // bn)
    return pl.pallas_call(
        body, grid=grid, in_specs=[a_spec, b_spec] * n, out_specs=out_spec, out_shape=out_shape, name=name,
        compiler_params=_params(("parallel", "parallel")),
    )(*[x for pair in pairs for x in pair])


def _mm_fused(pairs, mode, name, bm, bn, epilogue, extras, out_dtypes, j_outer=False, b_blocks=False,
              sum_shape=None, wide_first=None):
    a0, b0 = pairs[0]
    cb = b0.shape[2] if b_blocks else None
    b_shape = (b0.shape[1], N_DEV * cb) if b_blocks else b0.shape
    if mode == "nn":
        (M, K), (K2, N) = a0.shape, b_shape
    else:
        (M, K), (N, K2) = a0.shape, b_shape
    bm, bn = min(bm, M), min(bn, N)
    assert mode in ("nn", "nt") and K == K2 and M % bm == 0 and N % bn == 0, (name, a0.shape, b0.shape)
    dims = _DIMS[mode]
    n, ne, no = len(pairs), len(extras), len(out_dtypes)

    def body(*refs):
        prods = [_tile_product(refs[2 * t], refs[2 * t + 1], mode, b_blocks) for t in range(n)]
        results = epilogue(prods, [r[...] for r in refs[2 * n:2 * n + ne]])
        out_refs = refs[2 * n + ne:]
        for o_ref, val, dt in zip(out_refs, results, out_dtypes):
            o_ref[...] = val.astype(dt)
        if sum_shape is not None:
            s_ref = out_refs[no]

            @pl.when((pl.program_id(0) == 0) & (pl.program_id(1) == 0))
            def _():
                s_ref[...] = jnp.zeros_like(s_ref)

            s_ref[...] += results[no]

    def ij(f):
        return (lambda j, i: f(i, j)) if j_outer else f

    a_spec = pl.BlockSpec((bm, K), ij(lambda i, j: (i, 0)))
    once = dict(pipeline_mode=pl.Buffered(1)) if bn == N else {}
    if b_blocks and mode == "nt":
        b_spec = pl.BlockSpec((N_DEV, bn, cb), ij(lambda i, j: (0, j, 0)), **once)
    elif b_blocks:
        b_spec = pl.BlockSpec((bn // cb, K, cb), ij(lambda i, j: (j, 0, 0)), **once)
    elif mode == "nt":
        b_spec = pl.BlockSpec((bn, K), ij(lambda i, j: (j, 0)), **once)
    else:
        b_spec = pl.BlockSpec((K, bn), ij(lambda i, j: (0, j)), **once)
    e_specs = [pl.BlockSpec((1, bn), ij(lambda i, j: (0, j))) if first is None
               else pl.BlockSpec((bm, bn), ij(lambda i, j, first=first: (i, first + j))) for _, first in extras]
    tile = pl.BlockSpec((bm, bn), ij(lambda i, j: (i, j)))
    out_specs = [tile] * no
    out_shape = [jax.ShapeDtypeStruct((M, N), dt) for dt in out_dtypes]
    if wide_first is not None:
        assert bn == N
        out_specs[0] = pl.BlockSpec((bm, wide_first[1]), ij(lambda i, j: (i, 0)))
        out_shape[0] = jax.ShapeDtypeStruct((M, wide_first[0]), out_dtypes[0])
    if sum_shape is not None:
        assert sum_shape[1] in (1, bn) and (sum_shape[1] == 1 or bn == N)
        out_specs.append(_full(sum_shape))
        out_shape.append(jax.ShapeDtypeStruct(sum_shape, F32))
    grid = (N // bn, M // bm) if j_outer else (M // bm, N // bn)
    sem = ("arbitrary", "arbitrary") if sum_shape is not None else ("parallel", "parallel")
    return pl.pallas_call(
        body, grid=grid, in_specs=[a_spec, b_spec] * n + e_specs, out_specs=out_specs, out_shape=out_shape,
        name=name, compiler_params=_params(sem),
    )(*[x for pair in pairs for x in pair], *[arr for arr, _ in extras])


def _rms(x, gain):
    return x * lax.rsqrt(jnp.mean(x * x, axis=-1, keepdims=True) + EPS) * gain


def _silu(x):
    return x * jax.nn.sigmoid(x)


def _act(g, u):
    return _silu(g) * u


def _merge(g0, g1, a_dn, a_swa):
    return jax.nn.sigmoid(g0) * a_dn + jax.nn.sigmoid(g1) * a_swa


def _dn_post(c, is_v, q_scale):
    a = _silu(c)
    rs = lax.rsqrt(jnp.sum(a * a, axis=-1, keepdims=True) + EPS) * q_scale
    return a * jnp.where(is_v, 1.0, rs)


def _dn_out(o, z, gain):
    return _rms(o, gain) * _silu(z)


def _dot(a, b, dims=_DIMS["nn"], hi=False):
    if a.ndim == 3 or b.ndim == 3:
        batch = a.shape[0] if a.ndim == 3 else b.shape[0]
        a = a if a.ndim == 3 else jnp.broadcast_to(a, (batch,) + a.shape)
        b = b if b.ndim == 3 else jnp.broadcast_to(b, (batch,) + b.shape)
        ((ca,), (cb,)), _ = dims
        dims = (((ca + 1,), (cb + 1,)), ((0,), (0,)))
    if hi:
        return lax.dot_general(a, b, dims, precision=HI, preferred_element_type=F32)
    return lax.dot_general(a.astype(BF16), b.astype(BF16), dims, preferred_element_type=F32)


def _pieces(x):
    hi = x.astype(BF16)
    r1 = x - hi.astype(F32)
    mid = r1.astype(BF16)
    return hi, mid, (r1 - mid.astype(F32)).astype(BF16)


def _sel_left_impl(m, x):
    mb = m.astype(BF16)
    hi, mid, lo = _pieces(x)
    return _dot(mb, hi) + (_dot(mb, mid) + _dot(mb, lo))


@jax.custom_vjp
def _sel_left(m, mt, x):
    return _sel_left_impl(m, x)


_sel_left.defvjp(lambda m, mt, x: (_sel_left_impl(m, x), (m, mt)),
                 lambda res, ct: (jnp.zeros_like(res[0]), jnp.zeros_like(res[1]), _sel_left_impl(res[1], ct)))


def _sel_right_impl(x, s):
    sb = s.astype(BF16)
    hi, mid, lo = _pieces(x)
    return _dot(hi, sb) + (_dot(mid, sb) + _dot(lo, sb))


@jax.custom_vjp
def _sel_right(x, s, st):
    return _sel_right_impl(x, s)


_sel_right.defvjp(lambda x, s, st: (_sel_right_impl(x, s), (s, st)),
                  lambda res, ct: (_sel_right_impl(ct, res[1]), jnp.zeros_like(res[0]), jnp.zeros_like(res[1])))


def _dot3_impl(a, b):
    a_hi, a_lo, _ = _pieces(a)
    b_hi, b_lo, _ = _pieces(b)
    return _dot(a_hi, b_hi) + (_dot(a_hi, b_lo) + _dot(a_lo, b_hi))


@jax.custom_vjp
def _dot3(a, b):
    return _dot3_impl(a, b)


_dot3.defvjp(lambda a, b: (_dot3_impl(a, b), (a, b)),
             lambda res, ct: (_dot(ct, res[1], _DIMS["nt"]), _dot(res[0], ct, _DIMS["tn"])))


def _inv_impl(a, eye, strict):
    t = eye - a
    p = _dot(a, a)
    for level in range(5):
        t = t + _dot(t, p)
        if level < 4:
            p = _dot(p, p)
    t = t + _dot(t, eye - t - _dot3_impl(a, t))
    return jnp.where(strict > 0.5, t, eye)


@jax.custom_vjp
def _inv_given(a, t):
    return t.astype(F32)


_inv_given.defvjp(lambda a, t: (t.astype(F32), t),
                  lambda t, ct: (-_dot(_dot(t, ct, _DIMS["tn"]), t, _DIMS["nt"]), jnp.zeros_like(t)))


@jax.custom_vjp
def _lanes_join(a, b):
    return jnp.concatenate([a, b], axis=-1)


_lanes_join.defvjp(lambda a, b: (jnp.concatenate([a, b], axis=-1), None),
                   lambda _, ct: (ct[..., :ct.shape[-1] // 2], ct[..., ct.shape[-1] // 2:]))


@jax.custom_vjp
def _lanes_halves(y):
    h = y.shape[-1] // 2
    return y[..., :h], y[..., h:]


_lanes_halves.defvjp(lambda y: ((y[..., :y.shape[-1] // 2], y[..., y.shape[-1] // 2:]), None),
                     lambda _, ct: (jnp.concatenate(ct, axis=-1),))

GROUP = 4
GROUP_ROWS = GROUP * CHUNK


def _block_consts(n):
    ii = lax.broadcasted_iota(jnp.int32, (n, n), 0)
    jj = lax.broadcasted_iota(jnp.int32, (n, n), 1)
    shift = CHUNK.bit_length() - 1
    same = jnp.right_shift(ii, shift) == jnp.right_shift(jj, shift)
    return same & (ii >= jj), same & (ii <= jj), same & (ii > jj), same, ii == jj


def _lane0(n):
    s = (lax.broadcasted_iota(jnp.int32, (LANES, n), 0) == 0).astype(F32)
    st = (lax.broadcasted_iota(jnp.int32, (n, LANES), 1) == 0).astype(F32)
    return s, st


def _dn_group(q, k, v, g, beta, t_saved=None):
    n = GROUP_ROWS
    low_b, upp_b, strict_b, _, eye_b = _block_consts(n)
    low, upp, eye = low_b.astype(F32), upp_b.astype(F32), eye_b.astype(F32)
    gc = _sel_left(low, upp, g)
    per_chunk = (g.shape[0], GROUP, CHUNK, LANES)
    g_last = jnp.sum(g.reshape(per_chunk), axis=2, keepdims=True)
    gl = jnp.broadcast_to(g_last, per_chunk).reshape(g.shape)
    s, st = _lane0(n)
    col = _sel_right(gc, s, st)
    row = jnp.swapaxes(col, 1, 2)
    decay = jnp.exp(jnp.where(low_b, col - row, -jnp.inf))
    kb = k * beta
    vb = v * beta
    a = jnp.where(strict_b, _dot(kb, k, _DIMS["nt"]) * decay, 0.0)
    t = _inv_impl(a, eye, strict_b.astype(F32)) if t_saved is None else _inv_given(a, t_saved)
    u, w = _lanes_halves(_dot3(t, _lanes_join(vb, kb * jnp.exp(gc))))
    fold = (jnp.bitwise_and(lax.broadcasted_iota(jnp.int32, (n, CHUNK), 0), CHUNK - 1)
            == lax.broadcasted_iota(jnp.int32, (n, CHUNK), 1)).astype(F32)
    fold_t = (jnp.bitwise_and(lax.broadcasted_iota(jnp.int32, (CHUNK, n), 1), CHUNK - 1)
              == lax.broadcasted_iota(jnp.int32, (CHUNK, n), 0)).astype(F32)
    qk = _sel_right(_dot(q, k, _DIMS["nt"]) * decay, fold, fold_t)
    return u, w, q * jnp.exp(gc), k * jnp.exp(gl - gc), qk, jnp.exp(g_last), t


def _dn_step(s, u, w, qe, kd, qk, egl):
    v_new = u - _dot(w, s)
    o = _dot(qe, s) + _dot(qk, v_new)
    s_new = s * egl + _dot(kd, v_new, _DIMS["tn"])
    return s_new, o


def _swa_block(q, kband, vband, qg, kg, sinks, band):
    kn = _rms(kband, kg)
    qn = _rms(q, qg) * (SWA_DIM ** -0.5)
    logits = _dot(qn, kn, _DIMS["nt"]) + band
    m = lax.stop_gradient(jnp.maximum(jnp.max(logits, axis=-1, keepdims=True), sinks))
    p = jnp.exp(logits - m)
    denom = jnp.sum(p, axis=-1, keepdims=True) + jnp.exp(sinks - m)
    return _dot(p * (1.0 / denom), vband)


def _adamw(w, g, m, v):
    m = ADAM_B1 * m + (1.0 - ADAM_B1) * g
    v = ADAM_B2 * v + (1.0 - ADAM_B2) * jnp.square(g)
    m_hat = m / (1.0 - ADAM_B1 ** ADAM_STEP)
    v_hat = v / (1.0 - ADAM_B2 ** ADAM_STEP)
    delta = -ADAM_LR * (m_hat / (jnp.sqrt(v_hat) + ADAM_EPS) + ADAM_WD * w)
    return delta, m, v


def _row(tm, c, cb=0):
    return pl.BlockSpec((tm, c), lambda i, cb=cb: (i, cb))


def _full(shape):
    nd = len(shape)
    return pl.BlockSpec(shape, lambda *_, nd=nd: (0,) * nd)


def _norm_fwd(x, gain, name, tm=1024):
    S = x.shape[0]

    def body(x_ref, g_ref, h_ref):
        h_ref[...] = _rms(x_ref[...], g_ref[...]).astype(BF16)

    return pl.pallas_call(
        body, grid=(S // tm,), in_specs=[_row(tm, D_MODEL), _full((1, D_MODEL))],
        out_specs=_row(tm, D_MODEL), out_shape=jax.ShapeDtypeStruct((S, D_MODEL), BF16),
        name=name, compiler_params=_params(("parallel",)))(x, gain)


def _shift_down(x, s):
    row = lax.broadcasted_iota(jnp.int32, x.shape, 0)
    return jnp.where(row >= s, pltpu.roll(x, s, axis=0), 0.0)


def _shift_up(x, s):
    n = x.shape[0]
    row = lax.broadcasted_iota(jnp.int32, x.shape, 0)
    return jnp.where(row < n - s, pltpu.roll(x, n - s, axis=0), 0.0)


def _conv(x, w):
    out = w[DN_CONV - 1:DN_CONV] * x
    for s in range(1, DN_CONV):
        out = out + w[DN_CONV - 1 - s:DN_CONV - s] * _shift_down(x, s)
    return out


def _dn_conv_fwd(proj, conv_w):
    S = proj.shape[0]
    nb = DN_QKV // LANES

    def body(x_ref, w_ref, o_ref):
        j = pl.program_id(0)
        q_scale = jnp.where(j < DN_HEADS, DN_DIM ** -0.5, 1.0).astype(F32)
        o_ref[...] = _dn_post(_conv(x_ref[...], w_ref[...]), j >= 2 * DN_HEADS, q_scale)

    return pl.pallas_call(
        body, grid=(nb,),
        in_specs=[pl.BlockSpec((S, LANES), lambda j: (0, P_QKV // LANES + j)),
                  pl.BlockSpec((DN_CONV, LANES), lambda j: (0, j))],
        out_specs=pl.BlockSpec((S, LANES), lambda j: (0, j)),
        out_shape=jax.ShapeDtypeStruct((S, DN_QKV), F32), name="dn_conv_fwd",
        compiler_params=_params(("parallel",)))(proj, conv_w)


def _dn_conv_bwd(proj, conv_w, dqkvn, dproj):
    S = proj.shape[0]
    nb = DN_QKV // LANES

    def body(x_ref, w_ref, d_ref, _, dx_ref, dw_ref):
        j = pl.program_id(0)
        q_scale = jnp.where(j < DN_HEADS, DN_DIM ** -0.5, 1.0).astype(F32)
        x = x_ref[...]
        w = w_ref[...]
        _, vjp = jax.vjp(lambda c: _dn_post(c, j >= 2 * DN_HEADS, q_scale), _conv(x, w))
        (dc,) = vjp(d_ref[0])
        dx = w[DN_CONV - 1:DN_CONV] * dc
        dw_ref[DN_CONV - 1:DN_CONV, :] = jnp.sum(dc * x, axis=0, keepdims=True)
        for s in range(1, DN_CONV):
            dx = dx + w[DN_CONV - 1 - s:DN_CONV - s] * _shift_up(dc, s)
            dw_ref[DN_CONV - 1 - s:DN_CONV - s, :] = jnp.sum(dc * _shift_down(x, s), axis=0, keepdims=True)
        dx_ref[...] = dx.astype(BF16)

    return pl.pallas_call(
        body, grid=(nb,),
        in_specs=[pl.BlockSpec((S, LANES), lambda j: (0, P_QKV // LANES + j)),
                  pl.BlockSpec((DN_CONV, LANES), lambda j: (0, j)),
                  pl.BlockSpec((1, S, LANES), lambda j: (lax.div(j, DN_HEADS), 0, lax.rem(j, DN_HEADS))),
                  pl.BlockSpec(memory_space=pl.ANY)],
        out_specs=[pl.BlockSpec((S, LANES), lambda j: (0, P_QKV // LANES + j)),
                   pl.BlockSpec((DN_CONV, LANES), lambda j: (0, j))],
        out_shape=[jax.ShapeDtypeStruct(dproj.shape, dproj.dtype), jax.ShapeDtypeStruct((DN_CONV, DN_QKV), F32)],
        input_output_aliases={3: 0},
        name="dn_conv_bwd", compiler_params=_params(("parallel",)))(proj, conv_w, dqkvn, dproj)


def _expanders():
    eb = np.zeros((LANES, DN_WIDTH), np.float32)
    ea = np.zeros((LANES, DN_WIDTH), np.float32)
    for h in range(DN_HEADS):
        eb[h, h * DN_DIM:(h + 1) * DN_DIM] = 1.0
        ea[DN_HEADS + h, h * DN_DIM:(h + 1) * DN_DIM] = 1.0
    return jnp.asarray(eb), jnp.asarray(ea), jnp.asarray(eb.T), jnp.asarray(ea.T)


def _dn_gate_args(a_log, dt_bias):
    alog = jnp.repeat(a_log.reshape(1, DN_HEADS), DN_DIM, axis=1)
    dtb = _pad_to(jnp.pad(dt_bias.reshape(1, DN_HEADS), ((0, 0), (DN_HEADS, 0))), (1, LANES))
    return _expanders() + (alog, dtb)


def _dn_gate_specs(tm):
    return [_row(tm, LANES, P_BA // LANES), _full((LANES, DN_WIDTH)), _full((LANES, DN_WIDTH)),
            _full((DN_WIDTH, LANES)), _full((DN_WIDTH, LANES)), _full((1, DN_WIDTH)), _full((1, LANES))]


def _dn_gate_fn(ba, eb, ea, ebt, eat, alog, dtb):
    beta = _sel_right(jax.nn.sigmoid(ba), eb, ebt)
    g = -jnp.exp(alog) * _sel_right(jax.nn.softplus(ba + dtb), ea, eat)
    return beta, g


def _dn_gate_fwd(proj, a_log, dt_bias, tm=1024):
    S = proj.shape[0]
    args = _dn_gate_args(a_log, dt_bias)

    def body(ba_ref, eb_ref, ea_ref, ebt_ref, eat_ref, al_ref, dt_ref, beta_ref, g_ref):
        beta, g = _dn_gate_fn(ba_ref[...], eb_ref[...], ea_ref[...], ebt_ref[...], eat_ref[...], al_ref[...],
                              dt_ref[...])
        beta_ref[...] = beta
        g_ref[...] = g

    return pl.pallas_call(
        body, grid=(S // tm,), in_specs=_dn_gate_specs(tm), out_specs=[_row(tm, DN_WIDTH), _row(tm, DN_WIDTH)],
        out_shape=[jax.ShapeDtypeStruct((S, DN_WIDTH), F32), jax.ShapeDtypeStruct((S, DN_WIDTH), F32)],
        name="dn_gate_fwd", compiler_params=_params(("parallel",)))(proj, *args)


def _dn_gate_bwd(proj, a_log, dt_bias, dbeta, dg, dproj, tm=1024):
    S = proj.shape[0]
    args = _dn_gate_args(a_log, dt_bias)

    def body(ba_ref, eb_ref, ea_ref, ebt_ref, eat_ref, al_ref, dt_ref, dbeta_ref, dg_ref, _, dba_ref, dal_ref,
             ddt_ref):
        eb, ea, ebt, eat = eb_ref[...], ea_ref[...], ebt_ref[...], eat_ref[...]
        _, vjp = jax.vjp(lambda ba, al, dt: _dn_gate_fn(ba, eb, ea, ebt, eat, al, dt), ba_ref[...], al_ref[...],
                         dt_ref[...])
        dba, dal, ddt = vjp((dbeta_ref[...], dg_ref[...]))
        dba_ref[...] = dba.astype(BF16)

        @pl.when(pl.program_id(0) == 0)
        def _():
            dal_ref[...] = jnp.zeros_like(dal_ref)
            ddt_ref[...] = jnp.zeros_like(ddt_ref)

        dal_ref[...] += dal
        ddt_ref[...] += ddt

    return pl.pallas_call(
        body, grid=(S // tm,),
        in_specs=_dn_gate_specs(tm) + [_row(tm, DN_WIDTH), _row(tm, DN_WIDTH), pl.BlockSpec(memory_space=pl.ANY)],
        out_specs=[_row(tm, LANES, P_BA // LANES), _full((1, DN_WIDTH)), _full((1, LANES))],
        out_shape=[jax.ShapeDtypeStruct(dproj.shape, dproj.dtype), jax.ShapeDtypeStruct((1, DN_WIDTH), F32),
                   jax.ShapeDtypeStruct((1, LANES), F32)],
        input_output_aliases={len(args) + 3: 0},
        name="dn_gate_bwd", compiler_params=_params(("arbitrary",)))(proj, *args, dbeta, dg, dproj)


PREP_GROUPS = 8
PREP_CHUNKS = GROUP * PREP_GROUPS


def _dn_prep_specs():
    rows = PREP_CHUNKS * CHUNK
    q = pl.BlockSpec((rows, LANES), lambda h, c: (c, h))
    k = pl.BlockSpec((rows, LANES), lambda h, c: (c, DN_HEADS + h))
    v = pl.BlockSpec((rows, LANES), lambda h, c: (c, 2 * DN_HEADS + h))
    qk = pl.BlockSpec((1, rows, CHUNK), lambda h, c: (h, c, 0))
    egl = pl.BlockSpec((1, PREP_CHUNKS, 1, LANES), lambda h, c: (h, c, 0, 0))
    return q, k, v, qk, egl


def _dn_prep_fwd(qkvn, g, beta):
    S = qkvn.shape[0]
    nc = S // CHUNK
    q, k, v, qks, egl = _dn_prep_specs()

    def body(q_ref, k_ref, v_ref, g_ref, b_ref, u_ref, w_ref, qe_ref, kd_ref, qk_ref, egl_ref, t_ref):
        rows = PREP_CHUNKS * CHUNK
        grp = (PREP_GROUPS, GROUP_ROWS, LANES)
        u, w, qe, kd, qk, e, t = _dn_group(q_ref[...].reshape(grp), k_ref[...].reshape(grp), v_ref[...].reshape(grp),
                                           g_ref[...].reshape(grp), b_ref[...].reshape(grp))
        u_ref[...] = u.reshape(rows, LANES)
        w_ref[...] = w.reshape(rows, LANES)
        qe_ref[...] = qe.reshape(rows, LANES)
        kd_ref[...] = kd.reshape(rows, LANES)
        t_ref[0] = t.reshape(rows, GROUP_ROWS).astype(BF16)
        qk_ref[0] = qk.reshape(rows, CHUNK)
        egl_ref[0] = e.reshape(PREP_CHUNKS, 1, LANES)

    wide = jax.ShapeDtypeStruct((S, DN_WIDTH), F32)
    return pl.pallas_call(
        body, grid=(DN_HEADS, nc // PREP_CHUNKS), in_specs=[q, k, v, q, q],
        out_specs=[q, q, q, q, qks, egl, _dn_tinv_spec()],
        out_shape=[wide, wide, wide, wide, jax.ShapeDtypeStruct((DN_HEADS, S, CHUNK), F32),
                   jax.ShapeDtypeStruct((DN_HEADS, nc, 1, LANES), F32),
                   jax.ShapeDtypeStruct((DN_HEADS, S, GROUP_ROWS), BF16)],
        name="dn_prep_fwd", compiler_params=_params(("parallel", "parallel")))(qkvn, qkvn, qkvn, g, beta)


def _dn_tinv_spec():
    return pl.BlockSpec((1, PREP_CHUNKS * CHUNK, GROUP_ROWS), lambda h, c: (h, c, 0))


def _dn_prep_bwd(qkvn, g, beta, tinv, du, dw, dqe, dkd, dqk, degl):
    S = qkvn.shape[0]
    nc = S // CHUNK
    q, k, v, qks, egl = _dn_prep_specs()

    def body(q_ref, k_ref, v_ref, g_ref, b_ref, t_ref, du_ref, dw_ref, dqe_ref, dkd_ref, dqk_ref, degl_ref,
             dqkv_ref, dg_ref, db_ref):
        rows = PREP_CHUNKS * CHUNK
        grp = (PREP_GROUPS, GROUP_ROWS, LANES)
        t_saved = t_ref[0].reshape(PREP_GROUPS, GROUP_ROWS, GROUP_ROWS)
        _, vjp = jax.vjp(lambda *x: _dn_group(*x, t_saved=t_saved)[:6], q_ref[...].reshape(grp),
                         k_ref[...].reshape(grp), v_ref[...].reshape(grp), g_ref[...].reshape(grp),
                         b_ref[...].reshape(grp))
        dq, dk, dv, dg, db = vjp((du_ref[...].reshape(grp), dw_ref[...].reshape(grp), dqe_ref[...].reshape(grp),
                                  dkd_ref[...].reshape(grp), dqk_ref[0].reshape(PREP_GROUPS, GROUP_ROWS, CHUNK),
                                  degl_ref[0].reshape(PREP_GROUPS, GROUP, 1, LANES)))
        dqkv_ref[0] = dq.reshape(rows, LANES)
        dqkv_ref[1] = dk.reshape(rows, LANES)
        dqkv_ref[2] = dv.reshape(rows, LANES)
        dg_ref[...] = dg.reshape(rows, LANES)
        db_ref[...] = db.reshape(rows, LANES)

    wide = jax.ShapeDtypeStruct((S, DN_WIDTH), F32)
    rows = PREP_CHUNKS * CHUNK
    return pl.pallas_call(
        body, grid=(DN_HEADS, nc // PREP_CHUNKS), in_specs=[q, k, v, q, q, _dn_tinv_spec(), q, q, q, q, qks, egl],
        out_specs=[pl.BlockSpec((3, rows, LANES), lambda h, c: (0, c, h)), q, q],
        out_shape=[jax.ShapeDtypeStruct((3, S, DN_WIDTH), F32), wide, wide],
        name="dn_prep_bwd", compiler_params=_params(("parallel", "parallel")),
    )(qkvn, qkvn, qkvn, g, beta, tinv, du, dw, dqe, dkd, dqk, degl)


SCAN_CHUNKS = 16


def _dn_scan_specs(nc, reverse):
    nb = nc // SCAN_CHUNKS

    def cidx(c):
        return nb - 1 - c if reverse else c

    hc = pl.BlockSpec((SCAN_CHUNKS * CHUNK, DN_WIDTH), lambda c: (cidx(c), 0))
    qk = pl.BlockSpec((DN_HEADS, SCAN_CHUNKS * CHUNK, CHUNK), lambda c: (0, cidx(c), 0))
    egl = pl.BlockSpec((DN_HEADS, SCAN_CHUNKS, 1, LANES), lambda c: (0, cidx(c), 0, 0))
    st = pl.BlockSpec((DN_HEADS, SCAN_CHUNKS, DN_DIM, DN_DIM), lambda c: (0, cidx(c), 0, 0))
    return hc, qk, egl, st


def _heads(ref, i):
    return jnp.stack([ref[pl.ds(i * CHUNK, CHUNK), pl.ds(h * DN_DIM, DN_DIM)] for h in range(DN_HEADS)])


def _dn_scan_fwd(u, w, qe, kd, qk, egl):
    S = u.shape[0]
    nc = S // CHUNK
    hc, qks, egls, st = _dn_scan_specs(nc, False)

    def body(u_ref, w_ref, qe_ref, kd_ref, qk_ref, egl_ref, o_ref, st_ref, s_scr):
        @pl.when(pl.program_id(0) == 0)
        def _():
            s_scr[...] = jnp.zeros_like(s_scr)

        s = s_scr[...]
        for i in range(SCAN_CHUNKS):
            rows = pl.ds(i * CHUNK, CHUNK)
            st_ref[:, i] = s
            s, o = _dn_step(s, _heads(u_ref, i), _heads(w_ref, i), _heads(qe_ref, i), _heads(kd_ref, i),
                            qk_ref[:, rows, :], egl_ref[:, i])
            for h in range(DN_HEADS):
                o_ref[rows, pl.ds(h * DN_DIM, DN_DIM)] = o[h]
        s_scr[...] = s

    return pl.pallas_call(
        body, grid=(nc // SCAN_CHUNKS,), in_specs=[hc, hc, hc, hc, qks, egls], out_specs=[hc, st],
        out_shape=[jax.ShapeDtypeStruct((S, DN_WIDTH), F32), jax.ShapeDtypeStruct((DN_HEADS, nc, DN_DIM, DN_DIM), F32)],
        scratch_shapes=[pltpu.VMEM((DN_HEADS, DN_DIM, DN_DIM), F32)], name="dn_scan_fwd",
        compiler_params=_params(("arbitrary",)))(u, w, qe, kd, qk, egl)


def _dn_scan_bwd(u, w, qe, kd, qk, egl, states, do):
    S = u.shape[0]
    nc = S // CHUNK
    hc, qks, egls, st = _dn_scan_specs(nc, True)

    def body(u_ref, w_ref, qe_ref, kd_ref, qk_ref, egl_ref, st_ref, do_ref,
             du_ref, dw_ref, dqe_ref, dkd_ref, dqk_ref, degl_ref, ds_scr):
        @pl.when(pl.program_id(0) == 0)
        def _():
            ds_scr[...] = jnp.zeros_like(ds_scr)

        ds = ds_scr[...]
        for i in reversed(range(SCAN_CHUNKS)):
            rows = pl.ds(i * CHUNK, CHUNK)
            _, vjp = jax.vjp(_dn_step, st_ref[:, i], _heads(u_ref, i), _heads(w_ref, i), _heads(qe_ref, i),
                             _heads(kd_ref, i), qk_ref[:, rows, :], egl_ref[:, i])
            ds, du, dw, dqe, dkd, dqk, degl = vjp((ds, _heads(do_ref, i)))
            dqk_ref[:, rows, :] = dqk
            degl_ref[:, i] = degl
            for h in range(DN_HEADS):
                cols = pl.ds(h * DN_DIM, DN_DIM)
                du_ref[rows, cols] = du[h]
                dw_ref[rows, cols] = dw[h]
                dqe_ref[rows, cols] = dqe[h]
                dkd_ref[rows, cols] = dkd[h]
        ds_scr[...] = ds

    wide = jax.ShapeDtypeStruct((S, DN_WIDTH), F32)
    return pl.pallas_call(
        body, grid=(nc // SCAN_CHUNKS,), in_specs=[hc, hc, hc, hc, qks, egls, st, hc],
        out_specs=[hc, hc, hc, hc, qks, egls],
        out_shape=[wide, wide, wide, wide, jax.ShapeDtypeStruct((DN_HEADS, S, CHUNK), F32),
                   jax.ShapeDtypeStruct((DN_HEADS, nc, 1, LANES), F32)],
        scratch_shapes=[pltpu.VMEM((DN_HEADS, DN_DIM, DN_DIM), F32)], name="dn_scan_bwd",
        compiler_params=_params(("arbitrary",)))(u, w, qe, kd, qk, egl, states, do)


def _dn_out_fwd(o, proj, gain, tm=1024):
    S = o.shape[0]

    def body(o_ref, z_ref, g_ref, y_ref):
        y_ref[...] = _dn_out(o_ref[...], z_ref[...], g_ref[...]).astype(BF16)

    hs = pl.BlockSpec((tm, LANES), lambda i, h: (i, h))
    zs = pl.BlockSpec((tm, LANES), lambda i, h: (i, P_Z // LANES + h))
    return pl.pallas_call(
        body, grid=(S // tm, DN_HEADS), in_specs=[hs, zs, _full((1, DN_DIM))], out_specs=hs,
        out_shape=jax.ShapeDtypeStruct((S, DN_WIDTH), BF16), name="dn_out_fwd",
        compiler_params=_params(("parallel", "parallel")))(o, proj, gain)


_ANY = pl.BlockSpec(memory_space=pl.ANY)


def _dn_out_bwd(o, proj, gain, dy, dproj, tm=1024):
    S = o.shape[0]

    def body(o_ref, z_ref, g_ref, dy_ref, _, do_ref, dz_ref, dg_ref):
        _, vjp = jax.vjp(_dn_out, o_ref[...], z_ref[...], g_ref[...])
        do, dz, dg = vjp(dy_ref[...])
        do_ref[...] = do
        dz_ref[...] = dz.astype(BF16)

        @pl.when((pl.program_id(0) == 0) & (pl.program_id(1) == 0))
        def _():
            dg_ref[...] = jnp.zeros_like(dg_ref)

        dg_ref[...] += dg

    hs = pl.BlockSpec((tm, LANES), lambda i, h: (i, h))
    zs = pl.BlockSpec((tm, LANES), lambda i, h: (i, P_Z // LANES + h))
    return pl.pallas_call(
        body, grid=(S // tm, DN_HEADS), in_specs=[hs, zs, _full((1, DN_DIM)), hs, _ANY],
        out_specs=[hs, zs, _full((1, DN_DIM))],
        out_shape=[jax.ShapeDtypeStruct((S, DN_WIDTH), F32), jax.ShapeDtypeStruct(dproj.shape, dproj.dtype),
                   jax.ShapeDtypeStruct((1, DN_DIM), F32)],
        input_output_aliases={4: 1},
        name="dn_out_bwd", compiler_params=_params(("arbitrary", "arbitrary")))(o, proj, gain, dy, dproj)


def _rel_buckets():
    qi = np.arange(BLOCK)[:, None]
    kj = np.arange(2 * BLOCK)[None, :]
    n = np.maximum(BLOCK + qi - kj, 0)
    max_exact = REL_BUCKETS // 2
    nf = np.maximum(n, 1).astype(np.float32)
    large = max_exact + (np.log(nf / np.float32(max_exact)) / np.float32(math.log(REL_MAX_DIST / max_exact))
                         * np.float32(REL_BUCKETS - max_exact)).astype(np.int32)
    large = np.minimum(large, REL_BUCKETS - 1)
    return np.where(n < max_exact, n, large).astype(np.int32)


def _bias_fwd(rel_bias):
    buckets = jnp.asarray(_rel_buckets())

    def body(rb_ref, bk_ref, o_ref):
        bk = bk_ref[...]
        for h in range(SWA_HEADS):
            acc = jnp.zeros((BLOCK, 2 * BLOCK), F32)
            for b in range(REL_BUCKETS):
                acc = jnp.where(bk == b, rb_ref[b, h], acc)
            for first in range(2):
                o_ref[first, h] = jnp.where(_swa_mask(1 - first), acc, -jnp.inf)

    return pl.pallas_call(
        body, in_specs=[pl.BlockSpec(memory_space=pltpu.SMEM), pl.BlockSpec(memory_space=pltpu.VMEM)],
        out_specs=pl.BlockSpec(memory_space=pltpu.VMEM),
        out_shape=jax.ShapeDtypeStruct((2, SWA_HEADS, BLOCK, 2 * BLOCK), F32), name="swa_bias_fwd",
        compiler_params=_params())(rel_bias, buckets)


def _bias_bwd(dbias):
    buckets = jnp.asarray(_rel_buckets())

    def body(d_ref, bk_ref, o_ref):
        bk = bk_ref[...]
        lane = lax.broadcasted_iota(jnp.int32, (1, LANES), 1)
        for h in range(SWA_HEADS):
            d = d_ref[h]
            row = jnp.zeros((1, LANES), F32)
            for b in range(REL_BUCKETS):
                part = jnp.sum(jnp.where(bk == b, d, 0.0), axis=1, keepdims=True)
                row = jnp.where(lane == b, jnp.sum(part, axis=0, keepdims=True), row)
            o_ref[h:h + 1, :] = row

    return pl.pallas_call(
        body, in_specs=[pl.BlockSpec(memory_space=pltpu.VMEM), pl.BlockSpec(memory_space=pltpu.VMEM)],
        out_specs=pl.BlockSpec(memory_space=pltpu.VMEM),
        out_shape=jax.ShapeDtypeStruct((SWA_HEADS, LANES), F32), name="swa_bias_bwd",
        compiler_params=_params())(dbias, buckets)


def _swa_mask(n):
    qi = lax.broadcasted_iota(jnp.int32, (BLOCK, 2 * BLOCK), 0)
    kj = lax.broadcasted_iota(jnp.int32, (BLOCK, 2 * BLOCK), 1)
    dist = BLOCK + qi - kj
    return (dist >= 0) & (dist < WINDOW) & ((n > 0) | (kj >= BLOCK))


def _swa_in_specs():
    q = pl.BlockSpec((BLOCK, SWA_WIDTH), lambda n: (n, P_SQ // SWA_WIDTH))
    kc = pl.BlockSpec((BLOCK, SWA_KVW), lambda n: (n, P_SK // SWA_KVW))
    kp = pl.BlockSpec((BLOCK, SWA_KVW), lambda n: (jnp.maximum(n - 1, 0), P_SK // SWA_KVW))
    vc = pl.BlockSpec((BLOCK, SWA_KVW), lambda n: (n, P_SV // SWA_KVW))
    vp = pl.BlockSpec((BLOCK, SWA_KVW), lambda n: (jnp.maximum(n - 1, 0), P_SV // SWA_KVW))
    band = pl.BlockSpec((None, SWA_HEADS, BLOCK, 2 * BLOCK), lambda n: (jnp.where(n == 0, 1, 0), 0, 0, 0))
    small = [_full((1, SWA_DIM)), _full((1, SWA_DIM)), _full((1, SWA_HEADS)), band]
    return [q, kp, kc, vp, vc] + small


def _swa_load(q_ref, kp_ref, kc_ref, vp_ref, vc_ref, s_ref):
    q = jnp.stack([q_ref[:, pl.ds(h * SWA_DIM, SWA_DIM)] for h in range(SWA_HEADS)])
    kbands, vbands = [], []
    for kv in range(SWA_KV):
        cols = pl.ds(kv * SWA_DIM, SWA_DIM)
        kbands += [jnp.concatenate([kp_ref[:, cols], kc_ref[:, cols]], axis=0)] * SWA_GROUP
        vbands += [jnp.concatenate([vp_ref[:, cols], vc_ref[:, cols]], axis=0)] * SWA_GROUP
    sinks = jnp.stack([s_ref[:, pl.ds(h, 1)] for h in range(SWA_HEADS)])
    return q, jnp.stack(kbands), jnp.stack(vbands), sinks


def _swa_fwd(proj, q_gain, k_gain, sinks, bias):
    S = proj.shape[0]

    def body(q_ref, kp_ref, kc_ref, vp_ref, vc_ref, qg_ref, kg_ref, s_ref, bias_ref, y_ref):
        q, kband, vband, sk = _swa_load(q_ref, kp_ref, kc_ref, vp_ref, vc_ref, s_ref)
        out = _swa_block(q, kband, vband, qg_ref[...], kg_ref[...], sk, bias_ref[...])
        for h in range(SWA_HEADS):
            y_ref[:, pl.ds(h * SWA_DIM, SWA_DIM)] = out[h].astype(BF16)

    return pl.pallas_call(
        body, grid=(S // BLOCK,), in_specs=_swa_in_specs(),
        out_specs=pl.BlockSpec((BLOCK, SWA_WIDTH), lambda n: (n, 0)),
        out_shape=jax.ShapeDtypeStruct((S, SWA_WIDTH), BF16), name="swa_fwd",
        compiler_params=_params(("parallel",)))(proj, proj, proj, proj, proj, q_gain, k_gain, sinks, bias)


def _swa_bwd(proj, q_gain, k_gain, sinks, bias, dy, dproj):
    S = proj.shape[0]

    def body(q_ref, kp_ref, kc_ref, vp_ref, vc_ref, qg_ref, kg_ref, s_ref, bias_ref, dy_ref, _,
             dq_ref, dk_ref, dv_ref, dqg_ref, dkg_ref, ds_ref, dbias_ref):
        n = pl.program_id(0)

        @pl.when(n == 0)
        def _():
            for r in (dk_ref, dv_ref, dqg_ref, dkg_ref, ds_ref, dbias_ref):
                r[...] = jnp.zeros_like(r)

        cur = pl.ds(pl.multiple_of(n * BLOCK, BLOCK), BLOCK)
        prev = pl.ds(pl.multiple_of(jnp.maximum(n - 1, 0) * BLOCK, BLOCK), BLOCK)
        q, kband, vband, sk = _swa_load(q_ref, kp_ref, kc_ref, vp_ref, vc_ref, s_ref)
        _, vjp = jax.vjp(_swa_block, q, kband, vband, qg_ref[...], kg_ref[...], sk, bias_ref[...])
        dy = jnp.stack([dy_ref[:, pl.ds(h * SWA_DIM, SWA_DIM)] for h in range(SWA_HEADS)])
        dq, dkb, dvb, dqg, dkg, dsk, dbs = vjp(dy)
        for h in range(SWA_HEADS):
            dq_ref[:, pl.ds(h * SWA_DIM, SWA_DIM)] = dq[h].astype(BF16)
            ds_ref[:, pl.ds(h, 1)] += dsk[h]
        dbias_ref[...] += dbs
        dqg_ref[...] += dqg
        dkg_ref[...] += dkg
        for kv in range(SWA_KV):
            cols = pl.ds(kv * SWA_DIM, SWA_DIM)
            group = range(kv * SWA_GROUP, (kv + 1) * SWA_GROUP)
            dk_kv = sum(dkb[h] for h in group)
            dv_kv = sum(dvb[h] for h in group)
            dk_ref[cur, cols] += dk_kv[BLOCK:]
            dv_ref[cur, cols] += dv_kv[BLOCK:]

            @pl.when(n > 0)
            def _(cols=cols, dk_kv=dk_kv, dv_kv=dv_kv):
                dk_ref[prev, cols] += dk_kv[:BLOCK]
                dv_ref[prev, cols] += dv_kv[:BLOCK]

    return pl.pallas_call(
        body, grid=(S // BLOCK,),
        in_specs=_swa_in_specs() + [pl.BlockSpec((BLOCK, SWA_WIDTH), lambda n: (n, 0)),
                                    pl.BlockSpec(memory_space=pl.ANY)],
        out_specs=[pl.BlockSpec((BLOCK, SWA_WIDTH), lambda n: (n, P_SQ // SWA_WIDTH)), _full((S, SWA_KVW)),
                   _full((S, SWA_KVW)), _full((1, SWA_DIM)), _full((1, SWA_DIM)), _full((1, SWA_HEADS)),
                   _full((SWA_HEADS, BLOCK, 2 * BLOCK))],
        out_shape=[jax.ShapeDtypeStruct(dproj.shape, dproj.dtype), jax.ShapeDtypeStruct((S, SWA_KVW), F32),
                   jax.ShapeDtypeStruct((S, SWA_KVW), F32), jax.ShapeDtypeStruct((1, SWA_DIM), F32),
                   jax.ShapeDtypeStruct((1, SWA_DIM), F32), jax.ShapeDtypeStruct((1, SWA_HEADS), F32),
                   jax.ShapeDtypeStruct((SWA_HEADS, BLOCK, 2 * BLOCK), F32)],
        input_output_aliases={10: 0},
        name="swa_bwd", compiler_params=_params(("arbitrary",)),
    )(proj, proj, proj, proj, proj, q_gain, k_gain, sinks, bias, dy, dproj)


def _kv_into(dproj, dk, dv, tm=1024):
    S = dk.shape[0]

    def body(dk_ref, dv_ref, _, o_ref):
        o_ref[:, :SWA_KVW] = dk_ref[...].astype(BF16)
        o_ref[:, SWA_KVW:] = dv_ref[...].astype(BF16)

    return pl.pallas_call(
        body, grid=(S // tm,), in_specs=[_row(tm, SWA_KVW), _row(tm, SWA_KVW), pl.BlockSpec(memory_space=pl.ANY)],
        out_specs=_row(tm, 2 * SWA_KVW, P_SK // (2 * SWA_KVW)),
        out_shape=jax.ShapeDtypeStruct(dproj.shape, dproj.dtype), input_output_aliases={2: 0},
        name="swa_kv_into", compiler_params=_params(("parallel",)))(dk, dv, dproj)


def _position():
    return lax.axis_index("x"), lax.axis_index("y"), lax.axis_index("c")


_HBM = pl.BlockSpec(memory_space=pltpu.HBM)
_SEM = pl.BlockSpec(memory_space=pltpu.SEMAPHORE)
_DATAFLOW = pltpu.SideEffectType.DATAFLOW_SIDE_EFFECTING


def _two_level_copies(x_refs, out_refs, send_sems, recv_sems):
    x, y, c = _position()
    me, sibling = (x, y, c), (x, y, 1 - c)
    chips = [(1 - x, y), (x, 1 - y), (1 - x, 1 - y)]

    def copy(a, k, block, to, own=False):
        px, py, pc = block
        slot = out_refs[a].at[4 * px + 2 * py + pc]
        return pltpu.make_async_remote_copy(
            src_ref=x_refs[a] if own else slot, dst_ref=slot, send_sem=send_sems.at[7 * a + k],
            recv_sem=recv_sems.at[7 * a + k], device_id=to, device_id_type=MESH_ID)

    return copy, me, sibling, chips


def _all_gather_start(shards, name):
    na = len(shards)
    lands = [lax.empty((N_DEV,) + s.shape, s.dtype) for s in shards]

    def body(*refs):
        x_refs, out_refs = refs[:na], refs[na:2 * na]
        send_sems, recv_sems = refs[2 * na], refs[2 * na + 1]
        token = refs[-1]
        copy, me, sibling, chips = _two_level_copies(x_refs, out_refs, send_sems, recv_sems)
        for a in range(na):
            copy(a, 0, me, sibling, own=True).start()
            for j, chip in enumerate(chips):
                copy(a, 1 + j, me, (*chip, me[2]), own=True).start()
        token[...] = jnp.zeros_like(token)

    hbm = lambda a: pltpu.HBM(a.shape, a.dtype)
    out = pl.pallas_call(
        body, name=name,
        out_shape=(pltpu.SemaphoreType.DMA((7 * na,)), pltpu.SemaphoreType.DMA((7 * na,)),
                   *[hbm(s) for s in shards], *[hbm(l) for l in lands], jax.ShapeDtypeStruct((8, LANES), F32)),
        in_specs=[_HBM] * (2 * na),
        out_specs=(_SEM, _SEM, *[_HBM] * (2 * na), pl.BlockSpec(memory_space=pltpu.VMEM)),
        input_output_aliases={i: 2 + i for i in range(2 * na)},
        compiler_params=pltpu.CompilerParams(has_side_effects=_DATAFLOW),
    )(*[pltpu.with_memory_space_constraint(s, pltpu.HBM) for s in shards],
      *[pltpu.with_memory_space_constraint(l, pltpu.HBM) for l in lands])
    return (out[0], out[1], list(out[2:2 + na]), list(out[2 + na:2 + 2 * na])), out[-1]


def _all_gather_finish(handle, after, name):
    send_sems, recv_sems, srcs, lands = handle
    na = len(srcs)

    def body(*refs):
        x_refs, out_refs = refs[:na], refs[na:2 * na]
        copy, me, sibling, chips = _two_level_copies(x_refs, out_refs, refs[2 * na], refs[2 * na + 1])
        c = me[2]
        for a in range(na):
            copy(a, 0, sibling, me).wait_recv()
            copy(a, 0, me, sibling, own=True).wait_send()
            for j, chip in enumerate(chips):
                copy(a, 1 + j, (*chip, c), me).wait_recv()
                copy(a, 1 + j, me, (*chip, c), own=True).wait_send()

    hbm = lambda a: pltpu.HBM(a.shape, a.dtype)
    out = pl.pallas_call(
        body, name=name, out_shape=(*[hbm(s) for s in srcs], *[hbm(l) for l in lands]),
        in_specs=[_HBM] * (2 * na) + [_SEM, _SEM] + [pl.BlockSpec(memory_space=pl.ANY)] * len(after),
        out_specs=tuple([_HBM] * (2 * na)), input_output_aliases={i: i for i in range(2 * na)},
        compiler_params=pltpu.CompilerParams(has_side_effects=_DATAFLOW),
    )(*srcs, *lands, send_sems, recv_sems, *after)
    lands = _all_gather_relay(list(out[na:]), name + "_relay")
    return [_own_slot(land, src) for land, src in zip(lands, out[:na])]


def _all_gather_relay(lands, name):
    na = len(lands)

    def body(*refs):
        out_refs = refs[na:2 * na]
        send_sems, recv_sems = refs[2 * na:]
        x, y, c = _position()
        chips = [(1 - x, y), (x, 1 - y), (1 - x, 1 - y)]

        def copy(a, j, core):
            px, py = chips[j]
            slot = out_refs[a].at[4 * px + 2 * py + core]
            return pltpu.make_async_remote_copy(
                src_ref=slot, dst_ref=slot, send_sem=send_sems.at[3 * a + j], recv_sem=recv_sems.at[3 * a + j],
                device_id=(x, y, 1 - c), device_id_type=MESH_ID)

        sends = [copy(a, j, c) for a in range(na) for j in range(3)]
        for cp in sends:
            cp.start()
        for a in range(na):
            for j in range(3):
                copy(a, j, 1 - c).wait_recv()
        for cp in sends:
            cp.wait_send()

    return pl.pallas_call(
        body, in_specs=[pl.BlockSpec(memory_space=pl.ANY)] * na, out_specs=[pl.BlockSpec(memory_space=pl.ANY)] * na,
        out_shape=[jax.ShapeDtypeStruct(l.shape, l.dtype) for l in lands],
        input_output_aliases={i: i for i in range(na)},
        scratch_shapes=[pltpu.SemaphoreType.DMA((3 * na,)), pltpu.SemaphoreType.DMA((3 * na,))],
        name=name)(*lands)


def _peers(x, y, c):
    out = []
    for k in range(1, N_DEV):
        px, py, pc = x ^ (k >> 2), y ^ ((k >> 1) & 1), c ^ (k & 1)
        out.append(((px, py, pc), 4 * px + 2 * py + pc))
    return out


def _split_copies(src_refs, land_refs, send_sems, recv_sems, scatter):
    x, y, c = _position()
    me = 4 * x + 2 * y + c
    sends, recvs = [], []
    for k, (peer_id, peer) in enumerate(_peers(x, y, c)):
        for a, (src, land) in enumerate(zip(src_refs, land_refs)):
            sems = dict(send_sem=send_sems.at[7 * a + k], recv_sem=recv_sems.at[7 * a + k],
                        device_id=peer_id, device_id_type=MESH_ID)
            mine = src.at[peer] if scatter else src
            sends.append(pltpu.make_async_remote_copy(src_ref=mine, dst_ref=land.at[me], **sems))
            recvs.append(pltpu.make_async_remote_copy(src_ref=mine, dst_ref=land.at[peer], **sems))
    return sends, recvs


def _all_gather_direct(shards, name, after):
    na, nb = len(shards), len(after)

    def body(*refs):
        x_refs, out_refs = refs[:na], refs[na + nb:2 * na + nb]
        send_sems, recv_sems, local_sems = refs[2 * na + nb:]
        x, y, c = _position()
        me = 4 * x + 2 * y + c
        local = [pltpu.make_async_copy(x_refs[a], out_refs[a].at[me], local_sems.at[a]) for a in range(na)]
        sends, recvs = _split_copies(x_refs, out_refs, send_sems, recv_sems, False)
        for cp in local + sends:
            cp.start()
        for cp in recvs:
            cp.wait_recv()
        for cp in sends:
            cp.wait_send()
        for cp in local:
            cp.wait()

    return pl.pallas_call(
        body, in_specs=[pl.BlockSpec(memory_space=pl.ANY)] * (na + nb),
        out_specs=[pl.BlockSpec(memory_space=pl.ANY)] * na,
        out_shape=[jax.ShapeDtypeStruct((N_DEV,) + s.shape, s.dtype) for s in shards],
        scratch_shapes=[pltpu.SemaphoreType.DMA((7 * na,)), pltpu.SemaphoreType.DMA((7 * na,)),
                        pltpu.SemaphoreType.DMA((na,))],
        name=name)(*shards, *after)


def _exchange_start(srcs, scatter, name, after=None):
    na = len(srcs)
    lands = [lax.empty(s.shape if scatter else (N_DEV,) + s.shape, s.dtype) for s in srcs]
    extra = [] if after is None else [after]

    def body(*refs):
        src_refs, land_refs = refs[:na], refs[na:2 * na]
        send_sems, recv_sems = refs[2 * na + len(extra)], refs[2 * na + len(extra) + 1]
        token = refs[-1]
        sends, _ = _split_copies(src_refs, land_refs, send_sems, recv_sems, scatter)
        for cp in sends:
            cp.start()
        token[...] = jnp.zeros_like(token)

    hbm = lambda a: pltpu.HBM(a.shape, a.dtype)
    out = pl.pallas_call(
        body, name=name,
        out_shape=(pltpu.SemaphoreType.DMA((7 * na,)), pltpu.SemaphoreType.DMA((7 * na,)),
                   *[hbm(s) for s in srcs], *[hbm(l) for l in lands], jax.ShapeDtypeStruct((8, LANES), F32)),
        in_specs=[_HBM] * (2 * na) + [pl.BlockSpec(memory_space=pl.ANY)] * len(extra),
        out_specs=(_SEM, _SEM, *[_HBM] * (2 * na), pl.BlockSpec(memory_space=pltpu.VMEM)),
        input_output_aliases={i: 2 + i for i in range(2 * na)},
        compiler_params=pltpu.CompilerParams(has_side_effects=_DATAFLOW),
    )(*[pltpu.with_memory_space_constraint(s, pltpu.HBM) for s in srcs],
      *[pltpu.with_memory_space_constraint(l, pltpu.HBM) for l in lands], *extra)
    return (out[0], out[1], list(out[2:2 + na]), list(out[2 + na:2 + 2 * na])), out[-1]


def _exchange_wait(handle, after, scatter, name):
    send_sems, recv_sems, srcs, lands = handle
    na = len(srcs)

    def body(*refs):
        src_refs, land_refs = refs[:na], refs[na:2 * na]
        s_sems, r_sems = refs[2 * na], refs[2 * na + 1]
        sends, recvs = _split_copies(src_refs, land_refs, s_sems, r_sems, scatter)
        for cp in sends:
            cp.wait_send()
        for cp in recvs:
            cp.wait_recv()

    hbm = lambda a: pltpu.HBM(a.shape, a.dtype)
    out = pl.pallas_call(
        body, name=name, out_shape=(*[hbm(s) for s in srcs], *[hbm(l) for l in lands]),
        in_specs=[_HBM] * (2 * na) + [_SEM, _SEM, pl.BlockSpec(memory_space=pl.ANY)],
        out_specs=tuple([_HBM] * (2 * na)), input_output_aliases={i: i for i in range(2 * na)},
        compiler_params=pltpu.CompilerParams(has_side_effects=_DATAFLOW),
    )(*srcs, *lands, send_sems, recv_sems, after)
    return list(out[:na]), list(out[na:])


def _own_slot(landed, own):
    me = 4 * lax.axis_index("x") + 2 * lax.axis_index("y") + lax.axis_index("c")
    return lax.dynamic_update_slice_in_dim(landed, own[None], me, axis=0)


def _adam_update(parts, w, m, v, name, tr=256, turned=False):
    _, r, c = w.shape
    tr = _pick_rows(r, tr)
    cp = parts.shape[2]
    flat = turned and c % 8 != 0
    at = (slice(None), 0) if flat else (0,)

    def body(p_ref, w_ref, m_ref, v_ref, g_ref, d_ref, nm_ref, nv_ref):
        cols = pl.ds(0, cp if turned else c)
        g = p_ref[0, :, cols].astype(F32)
        for i in range(1, N_DEV):
            g = g + p_ref[i, :, cols].astype(F32)
        if turned:
            g = g.T[:c]
        delta, nm, nv = _adamw(w_ref[at], g, m_ref[at], v_ref[at])
        g_ref[at] = g
        d_ref[at] = delta
        nm_ref[at] = nm
        nv_ref[at] = nv

    there, back = ((2, 0, 1), (1, 2, 0)) if flat else ((0, 2, 1), (0, 2, 1))
    if turned:
        w, m, v = (jnp.transpose(a, there) for a in (w, m, v))
        rs = pl.BlockSpec((c, 1, tr), lambda i: (0, 0, i)) if flat else pl.BlockSpec((1, c, tr), lambda i: (0, 0, i))
    else:
        rs = pl.BlockSpec((1, tr, c), lambda i: (0, i, 0))
    outs = pl.pallas_call(
        body, grid=(r // tr,), in_specs=[pl.BlockSpec((N_DEV, tr, cp), lambda i: (0, i, 0)), rs, rs, rs],
        out_specs=[rs] * 4, out_shape=[jax.ShapeDtypeStruct(w.shape, F32)] * 4, name=name,
        compiler_params=_params(("parallel",)))(parts, w, m, v)
    return [*([jnp.transpose(o, back) for o in outs] if turned else outs), outs[0]]


def _pick_rows(rows, target):
    if rows <= target:
        return rows
    t = target
    while t >= 16:
        if rows % t == 0:
            return t
        t -= 16
    return rows


BIG = ("w_in", "w_branch_dn", "w_branch_swa", "w_out", "w_gate", "w_up", "w_down")
IN_SHARD, IN_WIRE = D_IN // N_DEV, 640
FF_SHARD, FF_WIRE = D_FF // N_DEV, 384
D_FFP = N_DEV * FF_WIRE
BIG_SHAPES = {"w_in": ((D_MODEL, IN_SHARD), (D_MODEL, IN_WIRE)),
              "w_branch_dn": ((DN_WIDTH, LANES), (DN_WIDTH, LANES)),
              "w_branch_swa": ((SWA_WIDTH, LANES), (SWA_WIDTH, LANES)),
              "w_out": ((LANES, D_MODEL), (LANES, D_MODEL)),
              "w_gate": ((D_MODEL, FF_SHARD), (D_MODEL, FF_WIRE)),
              "w_up": ((D_MODEL, FF_SHARD), (D_MODEL, FF_WIRE)),
              "w_down": ((FF_SHARD, D_MODEL), (FF_WIRE, D_MODEL))}
CONV_SHARD, CONV_WIRE = (DN_CONV, DN_QKV // N_DEV), (8, 256)


def _pad_to(a, shape):
    return jnp.pad(a, [(0, t - s) for s, t in zip(a.shape, shape)])


IN_TILE_ROWS = 256
_IN_SEGS = ((R_GATE, 2048, P_GATE), (R_QKV, DN_QKV, P_QKV), (R_Z, DN_WIDTH, P_Z), (R_SQ, SWA_WIDTH, P_SQ),
            (R_SK, SWA_KVW, P_SK), (R_SV, SWA_KVW, P_SV), (R_B, 8, P_BA))


def _w_in_from_blocks(blocks, after):
    tm = IN_TILE_ROWS

    def body(b_ref, _, o_ref):
        parts = []
        for rs, n, _ in _IN_SEGS:
            for dev in range(N_DEV):
                lo, hi = max(rs, IN_SHARD * dev), min(rs + n, IN_SHARD * (dev + 1))
                if lo < hi:
                    parts.append(b_ref[dev][:, lo - IN_SHARD * dev:hi - IN_SHARD * dev])
        parts.append(jnp.zeros((tm, P_WIDTH - P_BA - 8), b_ref.dtype))
        o_ref[...] = jnp.concatenate(parts, axis=1)

    return pl.pallas_call(
        body, grid=(D_MODEL // tm,),
        in_specs=[pl.BlockSpec((N_DEV, tm, IN_WIRE), lambda i: (0, i, 0)), pl.BlockSpec(memory_space=pl.ANY)],
        out_specs=pl.BlockSpec((tm, P_WIDTH), lambda i: (i, 0)),
        out_shape=jax.ShapeDtypeStruct((D_MODEL, P_WIDTH), blocks.dtype), name="w_in_from_blocks",
        compiler_params=_params(("parallel",)))(blocks, after)


def _w_in_to_blocks(g):
    tm = IN_TILE_ROWS

    def body(g_ref, o_ref):
        for dev in range(N_DEV):
            parts = []
            for rs, n, ps in sorted(_IN_SEGS):
                lo, hi = max(rs, IN_SHARD * dev), min(rs + n, IN_SHARD * (dev + 1))
                if lo < hi:
                    parts.append(g_ref[:, ps + lo - rs:ps + hi - rs])
            parts.append(jnp.zeros((tm, IN_WIRE - IN_SHARD), g_ref.dtype))
            o_ref[dev] = jnp.concatenate(parts, axis=1)

    return pl.pallas_call(
        body, grid=(D_MODEL // tm,), in_specs=[pl.BlockSpec((tm, P_WIDTH), lambda i: (i, 0))],
        out_specs=pl.BlockSpec((N_DEV, tm, IN_WIRE), lambda i: (0, i, 0)),
        out_shape=jax.ShapeDtypeStruct((N_DEV, D_MODEL, IN_WIRE), g.dtype), name="w_in_to_blocks",
        compiler_params=_params(("parallel",)))(g)


SMALL = {"attn_norm": (0, (1, D_MODEL)), "ffn_norm": (1, (1, D_MODEL)), "dn_out_norm": (2, (1, DN_DIM)),
         "swa_q_norm": (3, (1, SWA_DIM)), "swa_k_norm": (4, (1, SWA_DIM)), "dn_a_log": (5, (1, DN_HEADS)),
         "dn_dt_bias": (6, (1, DN_HEADS)), "swa_sinks": (7, (1, SWA_HEADS)), "rel_bias": (8, (REL_BUCKETS, SWA_HEADS))}
SMALL_SHEET = (48, D_MODEL)


LOSS_ROW = 40


def _small_pack(grads, loss_local):
    names = list(SMALL)

    def body(*refs):
        o_ref = refs[-1]
        o_ref[...] = jnp.zeros_like(o_ref)
        for n, ref in zip(names, refs):
            r0, (nr, nc) = SMALL[n]
            o_ref[r0:r0 + nr, 0:nc] = ref[...]
        o_ref[LOSS_ROW:LOSS_ROW + 1, 0:1] = refs[len(names)][...]

    return pl.pallas_call(
        body, in_specs=[pl.BlockSpec(memory_space=pltpu.VMEM)] * (len(names) + 1),
        out_specs=pl.BlockSpec(memory_space=pltpu.VMEM), out_shape=jax.ShapeDtypeStruct(SMALL_SHEET, F32),
        name="small_pack", compiler_params=_params())(*[grads[n].reshape(SMALL[n][1]) for n in names], loss_local)


def _small_update(sheets, w, m, v):
    names = list(SMALL)
    k = len(names)

    def body(*refs):
        p_ref = refs[0]
        ins, outs = refs[1:1 + 3 * k], refs[1 + 3 * k:]
        loss = p_ref[0, LOSS_ROW:LOSS_ROW + 1, 0:1]
        for i in range(1, N_DEV):
            loss = loss + p_ref[i, LOSS_ROW:LOSS_ROW + 1, 0:1]
        outs[4 * k][...] = loss
        for t, n in enumerate(names):
            r0, (nr, nc) = SMALL[n]
            g = p_ref[0, r0:r0 + nr, 0:nc]
            for i in range(1, N_DEV):
                g = g + p_ref[i, r0:r0 + nr, 0:nc]
            delta, nm, nv = _adamw(ins[t][...], g, ins[k + t][...], ins[2 * k + t][...])
            for kind, val in enumerate((g, delta, nm, nv)):
                outs[kind * k + t][...] = val

    shapes = [jax.ShapeDtypeStruct(SMALL[n][1], F32) for n in names]
    vm = pl.BlockSpec(memory_space=pltpu.VMEM)
    res = pl.pallas_call(
        body, in_specs=[vm] * (1 + 3 * k), out_specs=[vm] * (4 * k + 1),
        out_shape=shapes * 4 + [jax.ShapeDtypeStruct((1, 1), F32)], name="adam_small", compiler_params=_params(),
    )(sheets, *[d[n].reshape(SMALL[n][1]) for d in (w, m, v) for n in names])
    return {n: tuple(res[kind * k + t] for kind in range(4)) for t, n in enumerate(names)}, res[4 * k]


def kernel(x, attn_norm, w_in, dn_conv, dn_a_log, dn_dt_bias, dn_out_norm, swa_q_norm, swa_k_norm, swa_sinks, rel_bias, w_branch_dn, w_branch_swa, w_out, ffn_norm, w_gate, w_up, w_down, loss_target, m_attn_norm, m_w_in, m_dn_conv, m_dn_a_log, m_dn_dt_bias, m_dn_out_norm, m_swa_q_norm, m_swa_k_norm, m_swa_sinks, m_rel_bias, m_w_branch_dn, m_w_branch_swa, m_w_out, m_ffn_norm, m_w_gate, m_w_up, m_w_down, v_attn_norm, v_w_in, v_dn_conv, v_dn_a_log, v_dn_dt_bias, v_dn_out_norm, v_swa_q_norm, v_swa_k_norm, v_swa_sinks, v_rel_bias, v_w_branch_dn, v_w_branch_swa, v_w_out, v_ffn_norm, v_w_gate, v_w_up, v_w_down):
    args = dict(locals())
    S = x.shape[1]
    xs = x.reshape(S, D_MODEL)
    target = loss_target.reshape(S, D_MODEL)

    w_loc = {n: args[n].reshape(BIG_SHAPES[n][0]) for n in BIG}
    conv_loc = dn_conv.reshape(CONV_SHARD)
    def on_wire(n, zero=0.0):
        return _pad_to(w_loc[n] + zero, BIG_SHAPES[n][1]).astype(BF16)

    first_handle, first_token = _all_gather_start([on_wire("w_in"), _pad_to(conv_loc, CONV_WIRE)],
                                                  "all_gather_weights_start")
    zero = first_token[0, 0]
    h = _norm_fwd(xs, attn_norm + zero, "norm1_fwd")
    later = [n for n in BIG if n != "w_in"]
    wire = [on_wire(n, zero) for n in later]
    bias = _bias_fwd(rel_bias + zero)
    first = _all_gather_finish(first_handle, [h, bias] + wire, "all_gather_weights_finish")
    rest_handle, rest_token = _exchange_start(wire, False, "gather_rest_start", after=first[1])
    w_pad = _w_in_from_blocks(first[0], rest_token)
    conv_w = jnp.concatenate([first[1][d, :DN_CONV, :CONV_SHARD[1]] for d in range(N_DEV)], axis=1)

    proj = _mm([(h, w_pad)], "nn", F32, "mm_in", 1024, 1664, j_outer=True)
    qkvn = _dn_conv_fwd(proj, conv_w)
    beta, g = _dn_gate_fwd(proj, dn_a_log, dn_dt_bias)
    u, w, qe, kd, qk, egl, tinv = _dn_prep_fwd(qkvn, g, beta)
    o, states = _dn_scan_fwd(u, w, qe, kd, qk, egl)
    y_dn = _dn_out_fwd(o, proj, dn_out_norm)
    y_swa = _swa_fwd(proj, swa_q_norm, swa_k_norm, swa_sinks, bias)
    rest_src, rest_land = _exchange_wait(rest_handle, y_swa, False, "gather_rest_wait")
    G = {n: _own_slot(land, src) for n, src, land in zip(later, rest_src, rest_land)}
    w_bdn, w_bswa, w_g, w_u = G["w_branch_dn"], G["w_branch_swa"], G["w_gate"], G["w_up"]
    w_o = G["w_out"].reshape(D_MODEL, D_MODEL)
    w_d = G["w_down"].reshape(D_FFP, D_MODEL)
    gates = [(proj, P_GATE // 512), (proj, (P_GATE + D_MODEL) // 512)]
    a_dn, a_swa, merged = _mm_fused(
        [(y_dn, w_bdn), (y_swa, w_bswa)], "nn", "mm_branch_merge", 1024, 512,
        lambda p, e: (p[0], p[1], _merge(e[0], e[1], p[0], p[1])), gates, (F32, F32, BF16), b_blocks=True)

    def resid_norm(p, e):
        x1 = e[0] + p[0]
        return x1, _rms(x1, e[1])

    x1, h2 = _mm_fused([(merged, w_o)], "nn", "mm_out_norm", 512, D_MODEL, resid_norm,
                       [(xs, 0), (ffn_norm, None)], (F32, BF16))
    gate, up, act = _mm_fused([(h2, w_g), (h2, w_u)], "nn", "mm_gate_up_act", 1024, 768,
                              lambda p, e: (p[0], p[1], _act(p[0], p[1])), [], (F32, F32, BF16),
                              j_outer=True, b_blocks=True)

    def loss_head(p, e):
        diff = e[0] + p[0] - e[1]
        dy = diff * (1.0 / D_MODEL)
        part = jnp.sum(jnp.mean(diff * diff, axis=-1, keepdims=True), axis=0, keepdims=True) * 0.5
        return dy, dy, part

    dy, dy_b, loss_local = _mm_fused([(act, w_d)], "nn", "mm_down_loss", 512, D_MODEL, loss_head,
                                     [(x1, 0), (target, 0)], (F32, BF16), sum_shape=(1, 1))

    def act_bwd(p, e):
        _, vjp = jax.vjp(_act, e[0], e[1])
        return vjp(p[0])

    dgate, dup = _mm_fused([(dy_b, w_d)], "nt", "mm_dact_act", 1024, 768, act_bwd, [(gate, 0), (up, 0)],
                           (BF16, BF16), j_outer=True)
    g_w_down = _mm([(act, dy_b)], "tn", BF16, "mm_dw_down", 768, D_MODEL, j_outer=True)
    g_w_down = g_w_down.reshape(N_DEV, FF_WIRE, D_MODEL)
    g_w_gate = _mm([(h2, dgate)], "tn", BF16, "mm_dw_gate", D_MODEL, 768, out_blocks=True)
    g_w_up = _mm([(h2, dup)], "tn", BF16, "mm_dw_up", D_MODEL, 768, out_blocks=True)
    ffn_handle, ffn_token = _exchange_start([g_w_down, g_w_gate, g_w_up], True, "scatter_ffn_start")

    def norm_bwd(p, e):
        _, vjp = jax.vjp(_rms, e[0], e[2])
        dx, dgain = vjp(sum(p))
        dx = dx + e[1]
        return dx, dx, dgain

    dx1, dx1_b, g_ffn_norm = _mm_fused(
        [(dgate, w_g), (dup, w_u)], "nt", "mm_dh2_norm", 256, D_MODEL, norm_bwd,
        [(x1, 0), (dy, 0), (ffn_norm + ffn_token[0, 0], None)], (F32, BF16), b_blocks=True, sum_shape=(1, D_MODEL))
    def merge_bwd(p, e):
        _, vjp = jax.vjp(_merge, *e)
        dg0, dg1, da_dn, da_swa = vjp(p[0])
        return jnp.concatenate([dg0, dg1], axis=1), da_dn, da_swa

    dproj, da_dn, da_swa = _mm_fused(
        [(dx1_b, w_o)], "nt", "mm_dmerged_merge", 512, D_MODEL, merge_bwd,
        [(proj, P_GATE // D_MODEL), (proj, P_GATE // D_MODEL + 1), (a_dn, 0), (a_swa, 0)], (BF16,) * 3,
        wide_first=(P_WIDTH, 2 * D_MODEL))
    g_w_out = _mm([(merged, dx1_b)], "tn", BF16, "mm_dw_out", 512, D_MODEL, j_outer=True)
    g_w_out = g_w_out.reshape(N_DEV, LANES, D_MODEL)
    dy_dn = _mm([(da_dn, w_bdn)], "nt", F32, "mm_dy_dn", 1024, DN_WIDTH, b_blocks=True)
    dy_swa = _mm([(da_swa, w_bswa)], "nt", F32, "mm_dy_swa", 1024, SWA_WIDTH, b_blocks=True)
    g_w_bdn = _mm([(y_dn, da_dn)], "tn", BF16, "mm_dw_branch_dn", DN_WIDTH, 512, out_blocks=True)
    g_w_bswa = _mm([(y_swa, da_swa)], "tn", BF16, "mm_dw_branch_swa", SWA_WIDTH, 512, out_blocks=True)
    dproj, dsk, dsv, g_q_norm, g_k_norm, g_sinks, dbias = _swa_bwd(proj, swa_q_norm, swa_k_norm, swa_sinks, bias,
                                                                   dy_swa, dproj)
    dproj = _kv_into(dproj, dsk, dsv)
    g_rel_bias = _bias_bwd(dbias)[:, :REL_BUCKETS].T
    mix_handle, mix_token = _exchange_start([g_w_out, g_w_bdn, g_w_bswa], True, "scatter_mix_start")
    do, dproj, g_out_norm = _dn_out_bwd(o, proj, dn_out_norm + mix_token[0, 0], dy_dn, dproj)
    du, dw, dqe, dkd, dqk, degl = _dn_scan_bwd(u, w, qe, kd, qk, egl, states, do)
    dqkvn, dgd, dbeta = _dn_prep_bwd(qkvn, g, beta, tinv, du, dw, dqe, dkd, dqk, degl)
    dproj, dal, ddt = _dn_gate_bwd(proj, dn_a_log, dn_dt_bias, dbeta, dgd, dproj)
    g_a_log = dal.reshape(DN_HEADS, DN_DIM).sum(axis=1)
    g_dt_bias = ddt[0, DN_HEADS:2 * DN_HEADS]
    dproj, g_conv = _dn_conv_bwd(proj, conv_w, dqkvn, dproj)
    g_w_in = _w_in_to_blocks(_mm([(h, dproj)], "tn", BF16, "mm_dw_in", 512, 1664, j_outer=True))
    in_handle, in_token = _exchange_start([g_w_in], True, "scatter_in_start")
    dx, g_attn_norm = _mm_fused(
        [(dproj, w_pad)], "nt", "mm_dh_norm", 512, D_MODEL, lambda p, e: norm_bwd(p, e)[1:],
        [(xs, 0), (dx1, 0), (attn_norm + in_token[0, 0], None)], (F32,), sum_shape=(1, D_MODEL))

    g_small = {"attn_norm": g_attn_norm, "ffn_norm": g_ffn_norm, "rel_bias": g_rel_bias, "dn_out_norm": g_out_norm,
               "swa_q_norm": g_q_norm, "swa_k_norm": g_k_norm, "dn_a_log": g_a_log, "dn_dt_bias": g_dt_bias,
               "swa_sinks": g_sinks}
    me = 4 * lax.axis_index("x") + 2 * lax.axis_index("y") + lax.axis_index("c")
    outs = {}

    def finish(handle, group, name, after):
        srcs, lands = _exchange_wait(handle, after, True, name)
        for n, src, land in zip(group, srcs, lands):
            parts = _own_slot(land, lax.dynamic_index_in_dim(src, me, 0, keepdims=False))
            outs[n] = _adam_update(parts, args[n], args["m_" + n], args["v_" + n], "adam_" + n,
                                   turned=args[n].shape[2] % LANES != 0)

    finish(ffn_handle, ("w_down", "w_gate", "w_up"), "scatter_ffn_wait", dx)
    finish(mix_handle, ("w_out", "w_branch_dn", "w_branch_swa"), "scatter_mix_wait", dx)
    sheets, conv_all = _all_gather_direct([_small_pack(g_small, loss_local), _pad_to(g_conv, (8, DN_QKV))],
                                          "all_gather_small",
                                          after=[outs[n][4] for n in sorted(outs)])
    finish(in_handle, ("w_in",), "scatter_in_wait", sheets)
    conv_parts = lax.dynamic_slice(conv_all, (0, 0, me * CONV_SHARD[1]), (N_DEV,) + CONV_SHARD)
    outs["dn_conv"] = _adam_update(conv_parts, dn_conv, m_dn_conv, v_dn_conv, "adam_dn_conv")
    small_outs, loss = _small_update(sheets, {n: args[n] for n in SMALL}, {n: args["m_" + n] for n in SMALL},
                                     {n: args["v_" + n] for n in SMALL})
    outs.update(small_outs)

    names = ("attn_norm", "w_in", "dn_conv", "dn_a_log", "dn_dt_bias", "dn_out_norm", "swa_q_norm", "swa_k_norm",
             "swa_sinks", "rel_bias", "w_branch_dn", "w_branch_swa", "w_out", "ffn_norm", "w_gate", "w_up", "w_down")
    results = []
    for kind in range(4):
        results += [outs[n][kind].reshape(args[n].shape) for n in names]

    return (loss.reshape(()), dx.reshape(x.shape), *results)
```

```python
import math

import numpy as np
import jax
import jax.numpy as jnp
from jax import lax
from jax.experimental import pallas as pl
from jax.experimental.pallas import tpu as pltpu

F32 = jnp.float32
BF16 = jnp.bfloat16
HI = lax.Precision.HIGHEST

D_MODEL = 1024
DN_HEADS = 4
DN_DIM = 128
DN_WIDTH = 512
DN_QKV = 1536
DN_CONV = 4
CHUNK = 64
SWA_HEADS = 8
SWA_KV = 2
SWA_GROUP = 4
SWA_DIM = 64
SWA_WIDTH = 512
SWA_KVW = 128
WINDOW = 128
BLOCK = 128
REL_BUCKETS = 32
REL_MAX_DIST = 128
D_FF = 2816
D_IN = 4872
EPS = 1e-6
N_DEV = 8

ADAM_LR = 0.001
ADAM_B1 = 0.9
ADAM_B2 = 0.999
ADAM_EPS = 1e-08
ADAM_WD = 0.01
ADAM_STEP = 10

P_GATE, P_QKV, P_Z, P_SQ, P_SK, P_SV, P_BA = 0, 2048, 3584, 4096, 4608, 4736, 4864
P_WIDTH = 4992
R_QKV, R_Z, R_B, R_A, R_SQ, R_SK, R_SV, R_GATE = 0, 1536, 2048, 2052, 2056, 2568, 2696, 2824

VMEM_LIMIT = 56 * 1024 * 1024
LANES = 128
MESH_ID = pl.DeviceIdType.MESH


def _params(sem=None):
    return pltpu.CompilerParams(dimension_semantics=sem, vmem_limit_bytes=VMEM_LIMIT)


def _pick(dim, target):
    if dim <= target:
        return dim
    t = target - target % LANES
    while t >= LANES:
        if dim % t == 0:
            return t
        t -= LANES
    return dim


_DIMS = {"nn": (((1,), (0,)), ((), ())), "nt": (((1,), (1,)), ((), ())), "tn": (((0,), (0,)), ((), ()))}


def _tile_product(a_ref, b_ref, mode, b_blocks):
    a = a_ref[...].astype(BF16)
    b = jnp.concatenate([b_ref[d] for d in range(b_ref.shape[0])], axis=1) if b_blocks else b_ref[...]
    return lax.dot_general(a, b.astype(BF16), _DIMS[mode], preferred_element_type=F32)


def _mm(pairs, mode, out_dtype, name, bm, bn, j_outer=False, b_blocks=False, out_blocks=False):
    a0, b0 = pairs[0]
    cb = b0.shape[2] if b_blocks else None
    b_shape = (b0.shape[1], N_DEV * cb) if b_blocks else b0.shape
    if mode == "nn":
        (M, K), (K2, N) = a0.shape, b_shape
    elif mode == "nt":
        (M, K), (N, K2) = a0.shape, b_shape
    else:
        (K, M), (K2, N) = a0.shape, b_shape
    bm, bn = min(bm, M), min(bn, N)
    assert K == K2 and M % bm == 0 and N % bn == 0, (name, a0.shape, b0.shape, bm, bn)
    co = N // N_DEV
    assert not out_blocks or bn % co == 0
    dims = _DIMS[mode]
    n = len(pairs)

    def body(*refs):
        o_ref = refs[2 * n]
        acc = None
        for t in range(n):
            p = _tile_product(refs[2 * t], refs[2 * t + 1], mode, b_blocks)
            acc = p if acc is None else acc + p
        if out_blocks:
            for d in range(bn // co):
                o_ref[d] = acc[:, d * co:(d + 1) * co].astype(out_dtype)
        else:
            o_ref[...] = acc.astype(out_dtype)

    def ij(f):
        return (lambda j, i: f(i, j)) if j_outer else f

    a_spec = pl.BlockSpec((K, bm), ij(lambda i, j: (0, i))) if mode == "tn" else pl.BlockSpec((bm, K), ij(lambda i, j: (i, 0)))
    if b_blocks and mode == "nt":
        b_spec = pl.BlockSpec((N_DEV, bn, cb), ij(lambda i, j: (0, j, 0)))
    elif b_blocks:
        b_spec = pl.BlockSpec((bn // cb, K, cb), ij(lambda i, j: (j, 0, 0)))
    elif mode == "nt":
        b_spec = pl.BlockSpec((bn, K), ij(lambda i, j: (j, 0)))
    else:
        b_spec = pl.BlockSpec((K, bn), ij(lambda i, j: (0, j)))
    if out_blocks:
        out_spec = pl.BlockSpec((bn // co, bm, co), ij(lambda i, j: (j, i, 0)))
        out_shape = jax.ShapeDtypeStruct((N_DEV, M, co), out_dtype)
    else:
        out_spec = pl.BlockSpec((bm, bn), ij(lambda i, j: (i, j)))
        out_shape = jax.ShapeDtypeStruct((M, N), out_dtype)
    grid = (N // bn, M // bm) if j_outer else (M // bm, N // bn)
    return pl.pallas_call(
        body, grid=grid, in_specs=[a_spec, b_spec] * n, out_specs=out_spec, out_shape=out_shape, name=name,
        compiler_params=_params(("parallel", "parallel")),
    )(*[x for pair in pairs for x in pair])


def _mm_fused(pairs, mode, name, bm, bn, epilogue, extras, out_dtypes, j_outer=False, b_blocks=False,
              sum_shape=None, wide_first=None):
    a0, b0 = pairs[0]
    cb = b0.shape[2] if b_blocks else None
    b_shape = (b0.shape[1], N_DEV * cb) if b_blocks else b0.shape
    if mode == "nn":
        (M, K), (K2, N) = a0.shape, b_shape
    else:
        (M, K), (N, K2) = a0.shape, b_shape
    bm, bn = min(bm, M), min(bn, N)
    assert mode in ("nn", "nt") and K == K2 and M % bm == 0 and N % bn == 0, (name, a0.shape, b0.shape)
    dims = _DIMS[mode]
    n, ne, no = len(pairs), len(extras), len(out_dtypes)

    def body(*refs):
        prods = [_tile_product(refs[2 * t], refs[2 * t + 1], mode, b_blocks) for t in range(n)]
        results = epilogue(prods, [r[...] for r in refs[2 * n:2 * n + ne]])
        out_refs = refs[2 * n + ne:]
        for o_ref, val, dt in zip(out_refs, results, out_dtypes):
            o_ref[...] = val.astype(dt)
        if sum_shape is not None:
            s_ref = out_refs[no]

            @pl.when((pl.program_id(0) == 0) & (pl.program_id(1) == 0))
            def _():
                s_ref[...] = jnp.zeros_like(s_ref)

            s_ref[...] += results[no]

    def ij(f):
        return (lambda j, i: f(i, j)) if j_outer else f

    a_spec = pl.BlockSpec((bm, K), ij(lambda i, j: (i, 0)))
    once = dict(pipeline_mode=pl.Buffered(1)) if bn == N else {}
    if b_blocks and mode == "nt":
        b_spec = pl.BlockSpec((N_DEV, bn, cb), ij(lambda i, j: (0, j, 0)), **once)
    elif b_blocks:
        b_spec = pl.BlockSpec((bn // cb, K, cb), ij(lambda i, j: (j, 0, 0)), **once)
    elif mode == "nt":
        b_spec = pl.BlockSpec((bn, K), ij(lambda i, j: (j, 0)), **once)
    else:
        b_spec = pl.BlockSpec((K, bn), ij(lambda i, j: (0, j)), **once)
    e_specs = [pl.BlockSpec((1, bn), ij(lambda i, j: (0, j))) if first is None
               else pl.BlockSpec((bm, bn), ij(lambda i, j, first=first: (i, first + j))) for _, first in extras]
    tile = pl.BlockSpec((bm, bn), ij(lambda i, j: (i, j)))
    out_specs = [tile] * no
    out_shape = [jax.ShapeDtypeStruct((M, N), dt) for dt in out_dtypes]
    if wide_first is not None:
        assert bn == N
        out_specs[0] = pl.BlockSpec((bm, wide_first[1]), ij(lambda i, j: (i, 0)))
        out_shape[0] = jax.ShapeDtypeStruct((M, wide_first[0]), out_dtypes[0])
    if sum_shape is not None:
        assert sum_shape[1] in (1, bn) and (sum_shape[1] == 1 or bn == N)
        out_specs.append(_full(sum_shape))
        out_shape.append(jax.ShapeDtypeStruct(sum_shape, F32))
    grid = (N // bn, M // bm) if j_outer else (M // bm, N // bn)
    sem = ("arbitrary", "arbitrary") if sum_shape is not None else ("parallel", "parallel")
    return pl.pallas_call(
        body, grid=grid, in_specs=[a_spec, b_spec] * n + e_specs, out_specs=out_specs, out_shape=out_shape,
        name=name, compiler_params=_params(sem),
    )(*[x for pair in pairs for x in pair], *[arr for arr, _ in extras])


def _rms(x, gain):
    return x * lax.rsqrt(jnp.mean(x * x, axis=-1, keepdims=True) + EPS) * gain


def _silu(x):
    return x * jax.nn.sigmoid(x)


def _act(g, u):
    return _silu(g) * u


def _merge(g0, g1, a_dn, a_swa):
    return jax.nn.sigmoid(g0) * a_dn + jax.nn.sigmoid(g1) * a_swa


def _dn_post(c, is_v, q_scale):
    a = _silu(c)
    rs = lax.rsqrt(jnp.sum(a * a, axis=-1, keepdims=True) + EPS) * q_scale
    return a * jnp.where(is_v, 1.0, rs)


def _dn_out(o, z, gain):
    return _rms(o, gain) * _silu(z)


def _dot(a, b, dims=_DIMS["nn"], hi=False):
    if a.ndim == 3 or b.ndim == 3:
        batch = a.shape[0] if a.ndim == 3 else b.shape[0]
        a = a if a.ndim == 3 else jnp.broadcast_to(a, (batch,) + a.shape)
        b = b if b.ndim == 3 else jnp.broadcast_to(b, (batch,) + b.shape)
        ((ca,), (cb,)), _ = dims
        dims = (((ca + 1,), (cb + 1,)), ((0,), (0,)))
    if hi:
        return lax.dot_general(a, b, dims, precision=HI, preferred_element_type=F32)
    return lax.dot_general(a.astype(BF16), b.astype(BF16), dims, preferred_element_type=F32)


def _pieces(x):
    hi = x.astype(BF16)
    r1 = x - hi.astype(F32)
    mid = r1.astype(BF16)
    return hi, mid, (r1 - mid.astype(F32)).astype(BF16)


def _sel_left_impl(m, x):
    mb = m.astype(BF16)
    hi, mid, lo = _pieces(x)
    return _dot(mb, hi) + (_dot(mb, mid) + _dot(mb, lo))


@jax.custom_vjp
def _sel_left(m, mt, x):
    return _sel_left_impl(m, x)


_sel_left.defvjp(lambda m, mt, x: (_sel_left_impl(m, x), (m, mt)),
                 lambda res, ct: (jnp.zeros_like(res[0]), jnp.zeros_like(res[1]), _sel_left_impl(res[1], ct)))


def _sel_right_impl(x, s):
    sb = s.astype(BF16)
    hi, mid, lo = _pieces(x)
    return _dot(hi, sb) + (_dot(mid, sb) + _dot(lo, sb))


@jax.custom_vjp
def _sel_right(x, s, st):
    return _sel_right_impl(x, s)


_sel_right.defvjp(lambda x, s, st: (_sel_right_impl(x, s), (s, st)),
                  lambda res, ct: (_sel_right_impl(ct, res[1]), jnp.zeros_like(res[0]), jnp.zeros_like(res[1])))


def _dot3_impl(a, b):
    a_hi, a_lo, _ = _pieces(a)
    b_hi, b_lo, _ = _pieces(b)
    return _dot(a_hi, b_hi) + (_dot(a_hi, b_lo) + _dot(a_lo, b_hi))


@jax.custom_vjp
def _dot3(a, b):
    return _dot3_impl(a, b)


_dot3.defvjp(lambda a, b: (_dot3_impl(a, b), (a, b)),
             lambda res, ct: (_dot(ct, res[1], _DIMS["nt"]), _dot(res[0], ct, _DIMS["tn"])))


def _inv_impl(a, eye, strict):
    t = eye - a
    p = _dot(a, a)
    for level in range(5):
        t = t + _dot(t, p)
        if level < 4:
            p = _dot(p, p)
    t = t + _dot(t, eye - t - _dot3_impl(a, t))
    return jnp.where(strict > 0.5, t, eye)


@jax.custom_vjp
def _inv_given(a, t):
    return t.astype(F32)


_inv_given.defvjp(lambda a, t: (t.astype(F32), t),
                  lambda t, ct: (-_dot(_dot(t, ct, _DIMS["tn"]), t, _DIMS["nt"]), jnp.zeros_like(t)))


@jax.custom_vjp
def _lanes_join(a, b):
    return jnp.concatenate([a, b], axis=-1)


_lanes_join.defvjp(lambda a, b: (jnp.concatenate([a, b], axis=-1), None),
                   lambda _, ct: (ct[..., :ct.shape[-1] // 2], ct[..., ct.shape[-1] // 2:]))


@jax.custom_vjp
def _lanes_halves(y):
    h = y.shape[-1] // 2
    return y[..., :h], y[..., h:]


_lanes_halves.defvjp(lambda y: ((y[..., :y.shape[-1] // 2], y[..., y.shape[-1] // 2:]), None),
                     lambda _, ct: (jnp.concatenate(ct, axis=-1),))

GROUP = 4
GROUP_ROWS = GROUP * CHUNK


def _block_consts(n):
    ii = lax.broadcasted_iota(jnp.int32, (n, n), 0)
    jj = lax.broadcasted_iota(jnp.int32, (n, n), 1)
    shift = CHUNK.bit_length() - 1
    same = jnp.right_shift(ii, shift) == jnp.right_shift(jj, shift)
    return same & (ii >= jj), same & (ii <= jj), same & (ii > jj), same, ii == jj


def _lane0(n):
    s = (lax.broadcasted_iota(jnp.int32, (LANES, n), 0) == 0).astype(F32)
    st = (lax.broadcasted_iota(jnp.int32, (n, LANES), 1) == 0).astype(F32)
    return s, st


def _dn_group(q, k, v, g, beta, t_saved=None):
    n = GROUP_ROWS
    low_b, upp_b, strict_b, _, eye_b = _block_consts(n)
    low, upp, eye = low_b.astype(F32), upp_b.astype(F32), eye_b.astype(F32)
    gc = _sel_left(low, upp, g)
    per_chunk = (g.shape[0], GROUP, CHUNK, LANES)
    g_last = jnp.sum(g.reshape(per_chunk), axis=2, keepdims=True)
    gl = jnp.broadcast_to(g_last, per_chunk).reshape(g.shape)
    s, st = _lane0(n)
    col = _sel_right(gc, s, st)
    row = jnp.swapaxes(col, 1, 2)
    decay = jnp.exp(jnp.where(low_b, col - row, -jnp.inf))
    kb = k * beta
    vb = v * beta
    a = jnp.where(strict_b, _dot(kb, k, _DIMS["nt"]) * decay, 0.0)
    t = _inv_impl(a, eye, strict_b.astype(F32)) if t_saved is None else _inv_given(a, t_saved)
    u, w = _lanes_halves(_dot3(t, _lanes_join(vb, kb * jnp.exp(gc))))
    fold = (jnp.bitwise_and(lax.broadcasted_iota(jnp.int32, (n, CHUNK), 0), CHUNK - 1)
            == lax.broadcasted_iota(jnp.int32, (n, CHUNK), 1)).astype(F32)
    fold_t = (jnp.bitwise_and(lax.broadcasted_iota(jnp.int32, (CHUNK, n), 1), CHUNK - 1)
              == lax.broadcasted_iota(jnp.int32, (CHUNK, n), 0)).astype(F32)
    qk = _sel_right(_dot(q, k, _DIMS["nt"]) * decay, fold, fold_t)
    return u, w, q * jnp.exp(gc), k * jnp.exp(gl - gc), qk, jnp.exp(g_last), t


def _dn_step(s, u, w, qe, kd, qk, egl):
    v_new = u - _dot(w, s)
    o = _dot(qe, s) + _dot(qk, v_new)
    s_new = s * egl + _dot(kd, v_new, _DIMS["tn"])
    return s_new, o


def _swa_block(q, kband, vband, qg, kg, sinks, band):
    kn = _rms(kband, kg)
    qn = _rms(q, qg) * (SWA_DIM ** -0.5)
    logits = _dot(qn, kn, _DIMS["nt"]) + band
    m = lax.stop_gradient(jnp.maximum(jnp.max(logits, axis=-1, keepdims=True), sinks))
    p = jnp.exp(logits - m)
    denom = jnp.sum(p, axis=-1, keepdims=True) + jnp.exp(sinks - m)
    return _dot(p * (1.0 / denom), vband)


def _adamw(w, g, m, v):
    m = ADAM_B1 * m + (1.0 - ADAM_B1) * g
    v = ADAM_B2 * v + (1.0 - ADAM_B2) * jnp.square(g)
    m_hat = m / (1.0 - ADAM_B1 ** ADAM_STEP)
    v_hat = v / (1.0 - ADAM_B2 ** ADAM_STEP)
    delta = -ADAM_LR * (m_hat / (jnp.sqrt(v_hat) + ADAM_EPS) + ADAM_WD * w)
    return delta, m, v


def _row(tm, c, cb=0):
    return pl.BlockSpec((tm, c), lambda i, cb=cb: (i, cb))


def _full(shape):
    nd = len(shape)
    return pl.BlockSpec(shape, lambda *_, nd=nd: (0,) * nd)


def _norm_fwd(x, gain, name, tm=1024):
    S = x.shape[0]

    def body(x_ref, g_ref, h_ref):
        h_ref[...] = _rms(x_ref[...], g_ref[...]).astype(BF16)

    return pl.pallas_call(
        body, grid=(S // tm,), in_specs=[_row(tm, D_MODEL), _full((1, D_MODEL))],
        out_specs=_row(tm, D_MODEL), out_shape=jax.ShapeDtypeStruct((S, D_MODEL), BF16),
        name=name, compiler_params=_params(("parallel",)))(x, gain)


def _shift_down(x, s):
    row = lax.broadcasted_iota(jnp.int32, x.shape, 0)
    return jnp.where(row >= s, pltpu.roll(x, s, axis=0), 0.0)


def _shift_up(x, s):
    n = x.shape[0]
    row = lax.broadcasted_iota(jnp.int32, x.shape, 0)
    return jnp.where(row < n - s, pltpu.roll(x, n - s, axis=0), 0.0)


def _conv(x, w):
    out = w[DN_CONV - 1:DN_CONV] * x
    for s in range(1, DN_CONV):
        out = out + w[DN_CONV - 1 - s:DN_CONV - s] * _shift_down(x, s)
    return out


def _dn_conv_fwd(proj, conv_w):
    S = proj.shape[0]
    nb = DN_QKV // LANES

    def body(x_ref, w_ref, o_ref, c_ref):
        j = pl.program_id(0)
        q_scale = jnp.where(j < DN_HEADS, DN_DIM ** -0.5, 1.0).astype(F32)
        c = _conv(x_ref[...], w_ref[...])
        c_ref[...] = c
        o_ref[...] = _dn_post(c, j >= 2 * DN_HEADS, q_scale)

    col = pl.BlockSpec((S, LANES), lambda j: (0, j))
    return pl.pallas_call(
        body, grid=(nb,),
        in_specs=[pl.BlockSpec((S, LANES), lambda j: (0, P_QKV // LANES + j)),
                  pl.BlockSpec((DN_CONV, LANES), lambda j: (0, j))],
        out_specs=[col, col], out_shape=[jax.ShapeDtypeStruct((S, DN_QKV), F32)] * 2, name="dn_conv_fwd",
        compiler_params=_params(("parallel",)))(proj, conv_w)


def _dn_conv_bwd(proj, conv_w, conv_out, dqkvn, dproj):
    S = proj.shape[0]
    nb = DN_QKV // LANES

    def body(x_ref, w_ref, c_ref, d_ref, _, dx_ref, dw_ref):
        j = pl.program_id(0)
        q_scale = jnp.where(j < DN_HEADS, DN_DIM ** -0.5, 1.0).astype(F32)
        x = x_ref[...]
        w = w_ref[...]
        _, vjp = jax.vjp(lambda c: _dn_post(c, j >= 2 * DN_HEADS, q_scale), c_ref[...])
        (dc,) = vjp(d_ref[0])
        dx = w[DN_CONV - 1:DN_CONV] * dc
        dw_ref[DN_CONV - 1:DN_CONV, :] = jnp.sum(dc * x, axis=0, keepdims=True)
        for s in range(1, DN_CONV):
            up = _shift_up(dc, s)
            dx = dx + w[DN_CONV - 1 - s:DN_CONV - s] * up
            dw_ref[DN_CONV - 1 - s:DN_CONV - s, :] = jnp.sum(up * x, axis=0, keepdims=True)
        dx_ref[...] = dx.astype(BF16)

    return pl.pallas_call(
        body, grid=(nb,),
        in_specs=[pl.BlockSpec((S, LANES), lambda j: (0, P_QKV // LANES + j)),
                  pl.BlockSpec((DN_CONV, LANES), lambda j: (0, j)),
                  pl.BlockSpec((S, LANES), lambda j: (0, j)),
                  pl.BlockSpec((1, S, LANES), lambda j: (lax.div(j, DN_HEADS), 0, lax.rem(j, DN_HEADS))),
                  pl.BlockSpec(memory_space=pl.ANY)],
        out_specs=[pl.BlockSpec((S, LANES), lambda j: (0, P_QKV // LANES + j)),
                   pl.BlockSpec((DN_CONV, LANES), lambda j: (0, j))],
        out_shape=[jax.ShapeDtypeStruct(dproj.shape, dproj.dtype), jax.ShapeDtypeStruct((DN_CONV, DN_QKV), F32)],
        input_output_aliases={4: 0},
        name="dn_conv_bwd", compiler_params=_params(("parallel",)))(proj, conv_w, conv_out, dqkvn, dproj)


def _expanders():
    eb = np.zeros((LANES, DN_WIDTH), np.float32)
    ea = np.zeros((LANES, DN_WIDTH), np.float32)
    for h in range(DN_HEADS):
        eb[h, h * DN_DIM:(h + 1) * DN_DIM] = 1.0
        ea[DN_HEADS + h, h * DN_DIM:(h + 1) * DN_DIM] = 1.0
    return jnp.asarray(eb), jnp.asarray(ea), jnp.asarray(eb.T), jnp.asarray(ea.T)


def _dn_gate_args(a_log, dt_bias):
    alog = jnp.repeat(a_log.reshape(1, DN_HEADS), DN_DIM, axis=1)
    dtb = _pad_to(jnp.pad(dt_bias.reshape(1, DN_HEADS), ((0, 0), (DN_HEADS, 0))), (1, LANES))
    return _expanders() + (alog, dtb)


def _dn_gate_specs(tm):
    return [_row(tm, LANES, P_BA // LANES), _full((LANES, DN_WIDTH)), _full((LANES, DN_WIDTH)),
            _full((DN_WIDTH, LANES)), _full((DN_WIDTH, LANES)), _full((1, DN_WIDTH)), _full((1, LANES))]


def _dn_gate_fn(ba, eb, ea, ebt, eat, alog, dtb):
    beta = _sel_right(jax.nn.sigmoid(ba), eb, ebt)
    g = -jnp.exp(alog) * _sel_right(jax.nn.softplus(ba + dtb), ea, eat)
    return beta, g


def _dn_gate_fwd(proj, a_log, dt_bias, tm=1024):
    S = proj.shape[0]
    args = _dn_gate_args(a_log, dt_bias)

    def body(ba_ref, eb_ref, ea_ref, ebt_ref, eat_ref, al_ref, dt_ref, beta_ref, g_ref):
        beta, g = _dn_gate_fn(ba_ref[...], eb_ref[...], ea_ref[...], ebt_ref[...], eat_ref[...], al_ref[...],
                              dt_ref[...])
        beta_ref[...] = beta
        g_ref[...] = g

    return pl.pallas_call(
        body, grid=(S // tm,), in_specs=_dn_gate_specs(tm), out_specs=[_row(tm, DN_WIDTH), _row(tm, DN_WIDTH)],
        out_shape=[jax.ShapeDtypeStruct((S, DN_WIDTH), F32), jax.ShapeDtypeStruct((S, DN_WIDTH), F32)],
        name="dn_gate_fwd", compiler_params=_params(("parallel",)))(proj, *args)


def _dn_gate_bwd(proj, a_log, dt_bias, dbeta, dg, dproj, tm=1024):
    S = proj.shape[0]
    args = _dn_gate_args(a_log, dt_bias)

    def body(ba_ref, eb_ref, ea_ref, ebt_ref, eat_ref, al_ref, dt_ref, dbeta_ref, dg_ref, _, dba_ref, dal_ref,
             ddt_ref):
        eb, ea, ebt, eat = eb_ref[...], ea_ref[...], ebt_ref[...], eat_ref[...]
        _, vjp = jax.vjp(lambda ba, al, dt: _dn_gate_fn(ba, eb, ea, ebt, eat, al, dt), ba_ref[...], al_ref[...],
                         dt_ref[...])
        dba, dal, ddt = vjp((dbeta_ref[...], dg_ref[...]))
        dba_ref[...] = dba.astype(BF16)

        @pl.when(pl.program_id(0) == 0)
        def _():
            dal_ref[...] = jnp.zeros_like(dal_ref)
            ddt_ref[...] = jnp.zeros_like(ddt_ref)

        dal_ref[...] += dal
        ddt_ref[...] += ddt

    return pl.pallas_call(
        body, grid=(S // tm,),
        in_specs=_dn_gate_specs(tm) + [_row(tm, DN_WIDTH), _row(tm, DN_WIDTH), pl.BlockSpec(memory_space=pl.ANY)],
        out_specs=[_row(tm, LANES, P_BA // LANES), _full((1, DN_WIDTH)), _full((1, LANES))],
        out_shape=[jax.ShapeDtypeStruct(dproj.shape, dproj.dtype), jax.ShapeDtypeStruct((1, DN_WIDTH), F32),
                   jax.ShapeDtypeStruct((1, LANES), F32)],
        input_output_aliases={len(args) + 3: 0},
        name="dn_gate_bwd", compiler_params=_params(("arbitrary",)))(proj, *args, dbeta, dg, dproj)


PREP_GROUPS = 8
PREP_CHUNKS = GROUP * PREP_GROUPS


def _dn_prep_specs():
    rows = PREP_CHUNKS * CHUNK
    q = pl.BlockSpec((rows, LANES), lambda h, c: (c, h))
    k = pl.BlockSpec((rows, LANES), lambda h, c: (c, DN_HEADS + h))
    v = pl.BlockSpec((rows, LANES), lambda h, c: (c, 2 * DN_HEADS + h))
    qk = pl.BlockSpec((1, rows, CHUNK), lambda h, c: (h, c, 0))
    egl = pl.BlockSpec((1, PREP_CHUNKS, 1, LANES), lambda h, c: (h, c, 0, 0))
    return q, k, v, qk, egl


def _dn_prep_fwd(qkvn, g, beta):
    S = qkvn.shape[0]
    nc = S // CHUNK
    q, k, v, qks, egl = _dn_prep_specs()

    def body(q_ref, k_ref, v_ref, g_ref, b_ref, u_ref, w_ref, qe_ref, kd_ref, qk_ref, egl_ref, t_ref):
        rows = PREP_CHUNKS * CHUNK
        grp = (PREP_GROUPS, GROUP_ROWS, LANES)
        u, w, qe, kd, qk, e, t = _dn_group(q_ref[...].reshape(grp), k_ref[...].reshape(grp), v_ref[...].reshape(grp),
                                           g_ref[...].reshape(grp), b_ref[...].reshape(grp))
        u_ref[...] = u.reshape(rows, LANES)
        w_ref[...] = w.reshape(rows, LANES)
        qe_ref[...] = qe.reshape(rows, LANES)
        kd_ref[...] = kd.reshape(rows, LANES)
        t_ref[0] = t.reshape(rows, GROUP_ROWS).astype(BF16)
        qk_ref[0] = qk.reshape(rows, CHUNK)
        egl_ref[0] = e.reshape(PREP_CHUNKS, 1, LANES)

    wide = jax.ShapeDtypeStruct((S, DN_WIDTH), F32)
    return pl.pallas_call(
        body, grid=(DN_HEADS, nc // PREP_CHUNKS), in_specs=[q, k, v, q, q],
        out_specs=[q, q, q, q, qks, egl, _dn_tinv_spec()],
        out_shape=[wide, wide, wide, wide, jax.ShapeDtypeStruct((DN_HEADS, S, CHUNK), F32),
                   jax.ShapeDtypeStruct((DN_HEADS, nc, 1, LANES), F32),
                   jax.ShapeDtypeStruct((DN_HEADS, S, GROUP_ROWS), BF16)],
        name="dn_prep_fwd", compiler_params=_params(("parallel", "parallel")))(qkvn, qkvn, qkvn, g, beta)


def _dn_tinv_spec():
    return pl.BlockSpec((1, PREP_CHUNKS * CHUNK, GROUP_ROWS), lambda h, c: (h, c, 0))


def _dn_prep_bwd(qkvn, g, beta, tinv, du, dw, dqe, dkd, dqk, degl):
    S = qkvn.shape[0]
    nc = S // CHUNK
    q, k, v, qks, egl = _dn_prep_specs()

    def body(q_ref, k_ref, v_ref, g_ref, b_ref, t_ref, du_ref, dw_ref, dqe_ref, dkd_ref, dqk_ref, degl_ref,
             dqkv_ref, dg_ref, db_ref):
        rows = PREP_CHUNKS * CHUNK
        grp = (PREP_GROUPS, GROUP_ROWS, LANES)
        t_saved = t_ref[0].reshape(PREP_GROUPS, GROUP_ROWS, GROUP_ROWS)
        _, vjp = jax.vjp(lambda *x: _dn_group(*x, t_saved=t_saved)[:6], q_ref[...].reshape(grp),
                         k_ref[...].reshape(grp), v_ref[...].reshape(grp), g_ref[...].reshape(grp),
                         b_ref[...].reshape(grp))
        dq, dk, dv, dg, db = vjp((du_ref[...].reshape(grp), dw_ref[...].reshape(grp), dqe_ref[...].reshape(grp),
                                  dkd_ref[...].reshape(grp), dqk_ref[0].reshape(PREP_GROUPS, GROUP_ROWS, CHUNK),
                                  degl_ref[0].reshape(PREP_GROUPS, GROUP, 1, LANES)))
        dqkv_ref[0] = dq.reshape(rows, LANES)
        dqkv_ref[1] = dk.reshape(rows, LANES)
        dqkv_ref[2] = dv.reshape(rows, LANES)
        dg_ref[...] = dg.reshape(rows, LANES)
        db_ref[...] = db.reshape(rows, LANES)

    wide = jax.ShapeDtypeStruct((S, DN_WIDTH), F32)
    rows = PREP_CHUNKS * CHUNK
    return pl.pallas_call(
        body, grid=(DN_HEADS, nc // PREP_CHUNKS), in_specs=[q, k, v, q, q, _dn_tinv_spec(), q, q, q, q, qks, egl],
        out_specs=[pl.BlockSpec((3, rows, LANES), lambda h, c: (0, c, h)), q, q],
        out_shape=[jax.ShapeDtypeStruct((3, S, DN_WIDTH), F32), wide, wide],
        name="dn_prep_bwd", compiler_params=_params(("parallel", "parallel")),
    )(qkvn, qkvn, qkvn, g, beta, tinv, du, dw, dqe, dkd, dqk, degl)


SCAN_CHUNKS = 16


def _dn_scan_specs(nc, reverse):
    nb = nc // SCAN_CHUNKS

    def cidx(c):
        return nb - 1 - c if reverse else c

    hc = pl.BlockSpec((SCAN_CHUNKS * CHUNK, DN_WIDTH), lambda c: (cidx(c), 0))
    qk = pl.BlockSpec((DN_HEADS, SCAN_CHUNKS * CHUNK, CHUNK), lambda c: (0, cidx(c), 0))
    egl = pl.BlockSpec((DN_HEADS, SCAN_CHUNKS, 1, LANES), lambda c: (0, cidx(c), 0, 0))
    st = pl.BlockSpec((DN_HEADS, SCAN_CHUNKS, DN_DIM, DN_DIM), lambda c: (0, cidx(c), 0, 0))
    return hc, qk, egl, st


def _heads(ref, i):
    return jnp.stack([ref[pl.ds(i * CHUNK, CHUNK), pl.ds(h * DN_DIM, DN_DIM)] for h in range(DN_HEADS)])


def _dn_scan_fwd(u, w, qe, kd, qk, egl):
    S = u.shape[0]
    nc = S // CHUNK
    hc, qks, egls, st = _dn_scan_specs(nc, False)

    def body(u_ref, w_ref, qe_ref, kd_ref, qk_ref, egl_ref, o_ref, st_ref, s_scr):
        @pl.when(pl.program_id(0) == 0)
        def _():
            s_scr[...] = jnp.zeros_like(s_scr)

        s = s_scr[...]
        for i in range(SCAN_CHUNKS):
            rows = pl.ds(i * CHUNK, CHUNK)
            st_ref[:, i] = s
            s, o = _dn_step(s, _heads(u_ref, i), _heads(w_ref, i), _heads(qe_ref, i), _heads(kd_ref, i),
                            qk_ref[:, rows, :], egl_ref[:, i])
            for h in range(DN_HEADS):
                o_ref[rows, pl.ds(h * DN_DIM, DN_DIM)] = o[h]
        s_scr[...] = s

    return pl.pallas_call(
        body, grid=(nc // SCAN_CHUNKS,), in_specs=[hc, hc, hc, hc, qks, egls], out_specs=[hc, st],
        out_shape=[jax.ShapeDtypeStruct((S, DN_WIDTH), F32), jax.ShapeDtypeStruct((DN_HEADS, nc, DN_DIM, DN_DIM), F32)],
        scratch_shapes=[pltpu.VMEM((DN_HEADS, DN_DIM, DN_DIM), F32)], name="dn_scan_fwd",
        compiler_params=_params(("arbitrary",)))(u, w, qe, kd, qk, egl)


def _dn_scan_bwd(u, w, qe, kd, qk, egl, states, do):
    S = u.shape[0]
    nc = S // CHUNK
    hc, qks, egls, st = _dn_scan_specs(nc, True)

    def body(u_ref, w_ref, qe_ref, kd_ref, qk_ref, egl_ref, st_ref, do_ref,
             du_ref, dw_ref, dqe_ref, dkd_ref, dqk_ref, degl_ref, ds_scr):
        @pl.when(pl.program_id(0) == 0)
        def _():
            ds_scr[...] = jnp.zeros_like(ds_scr)

        ds = ds_scr[...]
        for i in reversed(range(SCAN_CHUNKS)):
            rows = pl.ds(i * CHUNK, CHUNK)
            _, vjp = jax.vjp(_dn_step, st_ref[:, i], _heads(u_ref, i), _heads(w_ref, i), _heads(qe_ref, i),
                             _heads(kd_ref, i), qk_ref[:, rows, :], egl_ref[:, i])
            ds, du, dw, dqe, dkd, dqk, degl = vjp((ds, _heads(do_ref, i)))
            dqk_ref[:, rows, :] = dqk
            degl_ref[:, i] = degl
            for h in range(DN_HEADS):
                cols = pl.ds(h * DN_DIM, DN_DIM)
                du_ref[rows, cols] = du[h]
                dw_ref[rows, cols] = dw[h]
                dqe_ref[rows, cols] = dqe[h]
                dkd_ref[rows, cols] = dkd[h]
        ds_scr[...] = ds

    wide = jax.ShapeDtypeStruct((S, DN_WIDTH), F32)
    return pl.pallas_call(
        body, grid=(nc // SCAN_CHUNKS,), in_specs=[hc, hc, hc, hc, qks, egls, st, hc],
        out_specs=[hc, hc, hc, hc, qks, egls],
        out_shape=[wide, wide, wide, wide, jax.ShapeDtypeStruct((DN_HEADS, S, CHUNK), F32),
                   jax.ShapeDtypeStruct((DN_HEADS, nc, 1, LANES), F32)],
        scratch_shapes=[pltpu.VMEM((DN_HEADS, DN_DIM, DN_DIM), F32)], name="dn_scan_bwd",
        compiler_params=_params(("arbitrary",)))(u, w, qe, kd, qk, egl, states, do)


def _dn_out_fwd(o, proj, gain, tm=1024):
    S = o.shape[0]

    def body(o_ref, z_ref, g_ref, y_ref):
        y_ref[...] = _dn_out(o_ref[...], z_ref[...], g_ref[...]).astype(BF16)

    hs = pl.BlockSpec((tm, LANES), lambda i, h: (i, h))
    zs = pl.BlockSpec((tm, LANES), lambda i, h: (i, P_Z // LANES + h))
    return pl.pallas_call(
        body, grid=(S // tm, DN_HEADS), in_specs=[hs, zs, _full((1, DN_DIM))], out_specs=hs,
        out_shape=jax.ShapeDtypeStruct((S, DN_WIDTH), BF16), name="dn_out_fwd",
        compiler_params=_params(("parallel", "parallel")))(o, proj, gain)


_ANY = pl.BlockSpec(memory_space=pl.ANY)


def _dn_out_bwd(o, proj, gain, dy, dproj, tm=1024):
    S = o.shape[0]

    def body(o_ref, z_ref, g_ref, dy_ref, _, do_ref, dz_ref, dg_ref):
        _, vjp = jax.vjp(_dn_out, o_ref[...], z_ref[...], g_ref[...])
        do, dz, dg = vjp(dy_ref[...])
        do_ref[...] = do
        dz_ref[...] = dz.astype(BF16)

        @pl.when((pl.program_id(0) == 0) & (pl.program_id(1) == 0))
        def _():
            dg_ref[...] = jnp.zeros_like(dg_ref)

        dg_ref[...] += dg

    hs = pl.BlockSpec((tm, LANES), lambda i, h: (i, h))
    zs = pl.BlockSpec((tm, LANES), lambda i, h: (i, P_Z // LANES + h))
    return pl.pallas_call(
        body, grid=(S // tm, DN_HEADS), in_specs=[hs, zs, _full((1, DN_DIM)), hs, _ANY],
        out_specs=[hs, zs, _full((1, DN_DIM))],
        out_shape=[jax.ShapeDtypeStruct((S, DN_WIDTH), F32), jax.ShapeDtypeStruct(dproj.shape, dproj.dtype),
                   jax.ShapeDtypeStruct((1, DN_DIM), F32)],
        input_output_aliases={4: 1},
        name="dn_out_bwd", compiler_params=_params(("arbitrary", "arbitrary")))(o, proj, gain, dy, dproj)


def _rel_buckets():
    qi = np.arange(BLOCK)[:, None]
    kj = np.arange(2 * BLOCK)[None, :]
    n = np.maximum(BLOCK + qi - kj, 0)
    max_exact = REL_BUCKETS // 2
    nf = np.maximum(n, 1).astype(np.float32)
    large = max_exact + (np.log(nf / np.float32(max_exact)) / np.float32(math.log(REL_MAX_DIST / max_exact))
                         * np.float32(REL_BUCKETS - max_exact)).astype(np.int32)
    large = np.minimum(large, REL_BUCKETS - 1)
    return np.where(n < max_exact, n, large).astype(np.int32)


def _bias_fwd(rel_bias):
    buckets = jnp.asarray(_rel_buckets())

    def body(rb_ref, bk_ref, o_ref):
        bk = bk_ref[...]
        for h in range(SWA_HEADS):
            acc = jnp.zeros((BLOCK, 2 * BLOCK), F32)
            for b in range(REL_BUCKETS):
                acc = jnp.where(bk == b, rb_ref[b, h], acc)
            for first in range(2):
                o_ref[first, h] = jnp.where(_swa_mask(1 - first), acc, -jnp.inf)

    return pl.pallas_call(
        body, in_specs=[pl.BlockSpec(memory_space=pltpu.SMEM), pl.BlockSpec(memory_space=pltpu.VMEM)],
        out_specs=pl.BlockSpec(memory_space=pltpu.VMEM),
        out_shape=jax.ShapeDtypeStruct((2, SWA_HEADS, BLOCK, 2 * BLOCK), F32), name="swa_bias_fwd",
        compiler_params=_params())(rel_bias, buckets)


def _bias_bwd(dbias):
    buckets = jnp.asarray(_rel_buckets())

    def body(d_ref, bk_ref, o_ref):
        bk = bk_ref[...]
        lane = lax.broadcasted_iota(jnp.int32, (1, LANES), 1)
        for h in range(SWA_HEADS):
            d = d_ref[h]
            row = jnp.zeros((1, LANES), F32)
            for b in range(REL_BUCKETS):
                part = jnp.sum(jnp.where(bk == b, d, 0.0), axis=1, keepdims=True)
                row = jnp.where(lane == b, jnp.sum(part, axis=0, keepdims=True), row)
            o_ref[h:h + 1, :] = row

    return pl.pallas_call(
        body, in_specs=[pl.BlockSpec(memory_space=pltpu.VMEM), pl.BlockSpec(memory_space=pltpu.VMEM)],
        out_specs=pl.BlockSpec(memory_space=pltpu.VMEM),
        out_shape=jax.ShapeDtypeStruct((SWA_HEADS, LANES), F32), name="swa_bias_bwd",
        compiler_params=_params())(dbias, buckets)


def _swa_mask(n):
    qi = lax.broadcasted_iota(jnp.int32, (BLOCK, 2 * BLOCK), 0)
    kj = lax.broadcasted_iota(jnp.int32, (BLOCK, 2 * BLOCK), 1)
    dist = BLOCK + qi - kj
    return (dist >= 0) & (dist < WINDOW) & ((n > 0) | (kj >= BLOCK))


def _swa_in_specs():
    q = pl.BlockSpec((BLOCK, SWA_WIDTH), lambda n: (n, P_SQ // SWA_WIDTH))
    kc = pl.BlockSpec((BLOCK, SWA_KVW), lambda n: (n, P_SK // SWA_KVW))
    kp = pl.BlockSpec((BLOCK, SWA_KVW), lambda n: (jnp.maximum(n - 1, 0), P_SK // SWA_KVW))
    vc = pl.BlockSpec((BLOCK, SWA_KVW), lambda n: (n, P_SV // SWA_KVW))
    vp = pl.BlockSpec((BLOCK, SWA_KVW), lambda n: (jnp.maximum(n - 1, 0), P_SV // SWA_KVW))
    band = pl.BlockSpec((None, SWA_HEADS, BLOCK, 2 * BLOCK), lambda n: (jnp.where(n == 0, 1, 0), 0, 0, 0))
    small = [_full((1, SWA_DIM)), _full((1, SWA_DIM)), _full((1, SWA_HEADS)), band]
    return [q, kp, kc, vp, vc] + small


def _swa_load(q_ref, kp_ref, kc_ref, vp_ref, vc_ref, s_ref):
    q = jnp.stack([q_ref[:, pl.ds(h * SWA_DIM, SWA_DIM)] for h in range(SWA_HEADS)])
    kbands, vbands = [], []
    for kv in range(SWA_KV):
        cols = pl.ds(kv * SWA_DIM, SWA_DIM)
        kbands += [jnp.concatenate([kp_ref[:, cols], kc_ref[:, cols]], axis=0)] * SWA_GROUP
        vbands += [jnp.concatenate([vp_ref[:, cols], vc_ref[:, cols]], axis=0)] * SWA_GROUP
    sinks = jnp.stack([s_ref[:, pl.ds(h, 1)] for h in range(SWA_HEADS)])
    return q, jnp.stack(kbands), jnp.stack(vbands), sinks


def _swa_fwd(proj, q_gain, k_gain, sinks, bias):
    S = proj.shape[0]

    def body(q_ref, kp_ref, kc_ref, vp_ref, vc_ref, qg_ref, kg_ref, s_ref, bias_ref, y_ref):
        q, kband, vband, sk = _swa_load(q_ref, kp_ref, kc_ref, vp_ref, vc_ref, s_ref)
        out = _swa_block(q, kband, vband, qg_ref[...], kg_ref[...], sk, bias_ref[...])
        for h in range(SWA_HEADS):
            y_ref[:, pl.ds(h * SWA_DIM, SWA_DIM)] = out[h].astype(BF16)

    return pl.pallas_call(
        body, grid=(S // BLOCK,), in_specs=_swa_in_specs(),
        out_specs=pl.BlockSpec((BLOCK, SWA_WIDTH), lambda n: (n, 0)),
        out_shape=jax.ShapeDtypeStruct((S, SWA_WIDTH), BF16), name="swa_fwd",
        compiler_params=_params(("parallel",)))(proj, proj, proj, proj, proj, q_gain, k_gain, sinks, bias)


def _swa_bwd(proj, q_gain, k_gain, sinks, bias, dy, dproj):
    S = proj.shape[0]

    def body(q_ref, kp_ref, kc_ref, vp_ref, vc_ref, qg_ref, kg_ref, s_ref, bias_ref, dy_ref, _,
             dq_ref, dk_ref, dv_ref, dqg_ref, dkg_ref, ds_ref, dbias_ref):
        n = pl.program_id(0)

        @pl.when(n == 0)
        def _():
            for r in (dk_ref, dv_ref, dqg_ref, dkg_ref, ds_ref, dbias_ref):
                r[...] = jnp.zeros_like(r)

        cur = pl.ds(pl.multiple_of(n * BLOCK, BLOCK), BLOCK)
        prev = pl.ds(pl.multiple_of(jnp.maximum(n - 1, 0) * BLOCK, BLOCK), BLOCK)
        q, kband, vband, sk = _swa_load(q_ref, kp_ref, kc_ref, vp_ref, vc_ref, s_ref)
        _, vjp = jax.vjp(_swa_block, q, kband, vband, qg_ref[...], kg_ref[...], sk, bias_ref[...])
        dy = jnp.stack([dy_ref[:, pl.ds(h * SWA_DIM, SWA_DIM)] for h in range(SWA_HEADS)])
        dq, dkb, dvb, dqg, dkg, dsk, dbs = vjp(dy)
        for h in range(SWA_HEADS):
            dq_ref[:, pl.ds(h * SWA_DIM, SWA_DIM)] = dq[h].astype(BF16)
            ds_ref[:, pl.ds(h, 1)] += dsk[h]
        dbias_ref[...] += dbs
        dqg_ref[...] += dqg
        dkg_ref[...] += dkg
        for kv in range(SWA_KV):
            cols = pl.ds(kv * SWA_DIM, SWA_DIM)
            group = range(kv * SWA_GROUP, (kv + 1) * SWA_GROUP)
            dk_kv = sum(dkb[h] for h in group)
            dv_kv = sum(dvb[h] for h in group)
            dk_ref[cur, cols] += dk_kv[BLOCK:]
            dv_ref[cur, cols] += dv_kv[BLOCK:]

            @pl.when(n > 0)
            def _(cols=cols, dk_kv=dk_kv, dv_kv=dv_kv):
                dk_ref[prev, cols] += dk_kv[:BLOCK]
                dv_ref[prev, cols] += dv_kv[:BLOCK]

    return pl.pallas_call(
        body, grid=(S // BLOCK,),
        in_specs=_swa_in_specs() + [pl.BlockSpec((BLOCK, SWA_WIDTH), lambda n: (n, 0)),
                                    pl.BlockSpec(memory_space=pl.ANY)],
        out_specs=[pl.BlockSpec((BLOCK, SWA_WIDTH), lambda n: (n, P_SQ // SWA_WIDTH)), _full((S, SWA_KVW)),
                   _full((S, SWA_KVW)), _full((1, SWA_DIM)), _full((1, SWA_DIM)), _full((1, SWA_HEADS)),
                   _full((SWA_HEADS, BLOCK, 2 * BLOCK))],
        out_shape=[jax.ShapeDtypeStruct(dproj.shape, dproj.dtype), jax.ShapeDtypeStruct((S, SWA_KVW), F32),
                   jax.ShapeDtypeStruct((S, SWA_KVW), F32), jax.ShapeDtypeStruct((1, SWA_DIM), F32),
                   jax.ShapeDtypeStruct((1, SWA_DIM), F32), jax.ShapeDtypeStruct((1, SWA_HEADS), F32),
                   jax.ShapeDtypeStruct((SWA_HEADS, BLOCK, 2 * BLOCK), F32)],
        input_output_aliases={10: 0},
        name="swa_bwd", compiler_params=_params(("arbitrary",)),
    )(proj, proj, proj, proj, proj, q_gain, k_gain, sinks, bias, dy, dproj)


def _kv_into(dproj, dk, dv, tm=1024):
    S = dk.shape[0]

    def body(dk_ref, dv_ref, _, o_ref):
        o_ref[:, :SWA_KVW] = dk_ref[...].astype(BF16)
        o_ref[:, SWA_KVW:] = dv_ref[...].astype(BF16)

    return pl.pallas_call(
        body, grid=(S // tm,), in_specs=[_row(tm, SWA_KVW), _row(tm, SWA_KVW), pl.BlockSpec(memory_space=pl.ANY)],
        out_specs=_row(tm, 2 * SWA_KVW, P_SK // (2 * SWA_KVW)),
        out_shape=jax.ShapeDtypeStruct(dproj.shape, dproj.dtype), input_output_aliases={2: 0},
        name="swa_kv_into", compiler_params=_params(("parallel",)))(dk, dv, dproj)


def _position():
    return lax.axis_index("x"), lax.axis_index("y"), lax.axis_index("c")


_HBM = pl.BlockSpec(memory_space=pltpu.HBM)
_SEM = pl.BlockSpec(memory_space=pltpu.SEMAPHORE)
_DATAFLOW = pltpu.SideEffectType.DATAFLOW_SIDE_EFFECTING


def _two_level_copies(x_refs, out_refs, send_sems, recv_sems):
    x, y, c = _position()
    me, sibling = (x, y, c), (x, y, 1 - c)
    chips = [(1 - x, y), (x, 1 - y), (1 - x, 1 - y)]

    def copy(a, k, block, to, own=False):
        px, py, pc = block
        slot = out_refs[a].at[4 * px + 2 * py + pc]
        return pltpu.make_async_remote_copy(
            src_ref=x_refs[a] if own else slot, dst_ref=slot, send_sem=send_sems.at[7 * a + k],
            recv_sem=recv_sems.at[7 * a + k], device_id=to, device_id_type=MESH_ID)

    return copy, me, sibling, chips


def _all_gather_start(shards, name):
    na = len(shards)
    lands = [lax.empty((N_DEV,) + s.shape, s.dtype) for s in shards]

    def body(*refs):
        x_refs, out_refs = refs[:na], refs[na:2 * na]
        send_sems, recv_sems = refs[2 * na], refs[2 * na + 1]
        token = refs[-1]
        copy, me, sibling, chips = _two_level_copies(x_refs, out_refs, send_sems, recv_sems)
        for a in range(na):
            copy(a, 0, me, sibling, own=True).start()
            for j, chip in enumerate(chips):
                copy(a, 1 + j, me, (*chip, me[2]), own=True).start()
        token[...] = jnp.zeros_like(token)

    hbm = lambda a: pltpu.HBM(a.shape, a.dtype)
    out = pl.pallas_call(
        body, name=name,
        out_shape=(pltpu.SemaphoreType.DMA((7 * na,)), pltpu.SemaphoreType.DMA((7 * na,)),
                   *[hbm(s) for s in shards], *[hbm(l) for l in lands], jax.ShapeDtypeStruct((8, LANES), F32)),
        in_specs=[_HBM] * (2 * na),
        out_specs=(_SEM, _SEM, *[_HBM] * (2 * na), pl.BlockSpec(memory_space=pltpu.VMEM)),
        input_output_aliases={i: 2 + i for i in range(2 * na)},
        compiler_params=pltpu.CompilerParams(has_side_effects=_DATAFLOW),
    )(*[pltpu.with_memory_space_constraint(s, pltpu.HBM) for s in shards],
      *[pltpu.with_memory_space_constraint(l, pltpu.HBM) for l in lands])
    return (out[0], out[1], list(out[2:2 + na]), list(out[2 + na:2 + 2 * na])), out[-1]


def _all_gather_finish(handle, after, name):
    send_sems, recv_sems, srcs, lands = handle
    na = len(srcs)

    def body(*refs):
        x_refs, out_refs = refs[:na], refs[na:2 * na]
        copy, me, sibling, chips = _two_level_copies(x_refs, out_refs, refs[2 * na], refs[2 * na + 1])
        c = me[2]
        for a in range(na):
            copy(a, 0, sibling, me).wait_recv()
            copy(a, 0, me, sibling, own=True).wait_send()
            for j, chip in enumerate(chips):
                copy(a, 1 + j, (*chip, c), me).wait_recv()
                copy(a, 1 + j, me, (*chip, c), own=True).wait_send()

    hbm = lambda a: pltpu.HBM(a.shape, a.dtype)
    out = pl.pallas_call(
        body, name=name, out_shape=(*[hbm(s) for s in srcs], *[hbm(l) for l in lands]),
        in_specs=[_HBM] * (2 * na) + [_SEM, _SEM] + [pl.BlockSpec(memory_space=pl.ANY)] * len(after),
        out_specs=tuple([_HBM] * (2 * na)), input_output_aliases={i: i for i in range(2 * na)},
        compiler_params=pltpu.CompilerParams(has_side_effects=_DATAFLOW),
    )(*srcs, *lands, send_sems, recv_sems, *after)
    lands = _all_gather_relay(list(out[na:]), name + "_relay")
    return [_own_slot(land, src) for land, src in zip(lands, out[:na])]


def _all_gather_relay(lands, name):
    na = len(lands)

    def body(*refs):
        out_refs = refs[na:2 * na]
        send_sems, recv_sems = refs[2 * na:]
        x, y, c = _position()
        chips = [(1 - x, y), (x, 1 - y), (1 - x, 1 - y)]

        def copy(a, j, core):
            px, py = chips[j]
            slot = out_refs[a].at[4 * px + 2 * py + core]
            return pltpu.make_async_remote_copy(
                src_ref=slot, dst_ref=slot, send_sem=send_sems.at[3 * a + j], recv_sem=recv_sems.at[3 * a + j],
                device_id=(x, y, 1 - c), device_id_type=MESH_ID)

        sends = [copy(a, j, c) for a in range(na) for j in range(3)]
        for cp in sends:
            cp.start()
        for a in range(na):
            for j in range(3):
                copy(a, j, 1 - c).wait_recv()
        for cp in sends:
            cp.wait_send()

    return pl.pallas_call(
        body, in_specs=[pl.BlockSpec(memory_space=pl.ANY)] * na, out_specs=[pl.BlockSpec(memory_space=pl.ANY)] * na,
        out_shape=[jax.ShapeDtypeStruct(l.shape, l.dtype) for l in lands],
        input_output_aliases={i: i for i in range(na)},
        scratch_shapes=[pltpu.SemaphoreType.DMA((3 * na,)), pltpu.SemaphoreType.DMA((3 * na,))],
        name=name)(*lands)


def _peers(x, y, c):
    out = []
    for k in range(1, N_DEV):
        px, py, pc = x ^ (k >> 2), y ^ ((k >> 1) & 1), c ^ (k & 1)
        out.append(((px, py, pc), 4 * px + 2 * py + pc))
    return out


def _split_copies(src_refs, land_refs, send_sems, recv_sems, scatter):
    x, y, c = _position()
    me = 4 * x + 2 * y + c
    sends, recvs = [], []
    for k, (peer_id, peer) in enumerate(_peers(x, y, c)):
        for a, (src, land) in enumerate(zip(src_refs, land_refs)):
            sems = dict(send_sem=send_sems.at[7 * a + k], recv_sem=recv_sems.at[7 * a + k],
                        device_id=peer_id, device_id_type=MESH_ID)
            mine = src.at[peer] if scatter else src
            sends.append(pltpu.make_async_remote_copy(src_ref=mine, dst_ref=land.at[me], **sems))
            recvs.append(pltpu.make_async_remote_copy(src_ref=mine, dst_ref=land.at[peer], **sems))
    return sends, recvs


def _all_gather_direct(shards, name, after):
    na, nb = len(shards), len(after)

    def body(*refs):
        x_refs, out_refs = refs[:na], refs[na + nb:2 * na + nb]
        send_sems, recv_sems, local_sems = refs[2 * na + nb:]
        x, y, c = _position()
        me = 4 * x + 2 * y + c
        local = [pltpu.make_async_copy(x_refs[a], out_refs[a].at[me], local_sems.at[a]) for a in range(na)]
        sends, recvs = _split_copies(x_refs, out_refs, send_sems, recv_sems, False)
        for cp in local + sends:
            cp.start()
        for cp in recvs:
            cp.wait_recv()
        for cp in sends:
            cp.wait_send()
        for cp in local:
            cp.wait()

    return pl.pallas_call(
        body, in_specs=[pl.BlockSpec(memory_space=pl.ANY)] * (na + nb),
        out_specs=[pl.BlockSpec(memory_space=pl.ANY)] * na,
        out_shape=[jax.ShapeDtypeStruct((N_DEV,) + s.shape, s.dtype) for s in shards],
        scratch_shapes=[pltpu.SemaphoreType.DMA((7 * na,)), pltpu.SemaphoreType.DMA((7 * na,)),
                        pltpu.SemaphoreType.DMA((na,))],
        name=name)(*shards, *after)


def _exchange_start(srcs, scatter, name, after=None):
    na = len(srcs)
    lands = [lax.empty(s.shape if scatter else (N_DEV,) + s.shape, s.dtype) for s in srcs]
    extra = [] if after is None else [after]

    def body(*refs):
        src_refs, land_refs = refs[:na], refs[na:2 * na]
        send_sems, recv_sems = refs[2 * na + len(extra)], refs[2 * na + len(extra) + 1]
        token = refs[-1]
        sends, _ = _split_copies(src_refs, land_refs, send_sems, recv_sems, scatter)
        for cp in sends:
            cp.start()
        token[...] = jnp.zeros_like(token)

    hbm = lambda a: pltpu.HBM(a.shape, a.dtype)
    out = pl.pallas_call(
        body, name=name,
        out_shape=(pltpu.SemaphoreType.DMA((7 * na,)), pltpu.SemaphoreType.DMA((7 * na,)),
                   *[hbm(s) for s in srcs], *[hbm(l) for l in lands], jax.ShapeDtypeStruct((8, LANES), F32)),
        in_specs=[_HBM] * (2 * na) + [pl.BlockSpec(memory_space=pl.ANY)] * len(extra),
        out_specs=(_SEM, _SEM, *[_HBM] * (2 * na), pl.BlockSpec(memory_space=pltpu.VMEM)),
        input_output_aliases={i: 2 + i for i in range(2 * na)},
        compiler_params=pltpu.CompilerParams(has_side_effects=_DATAFLOW),
    )(*[pltpu.with_memory_space_constraint(s, pltpu.HBM) for s in srcs],
      *[pltpu.with_memory_space_constraint(l, pltpu.HBM) for l in lands], *extra)
    return (out[0], out[1], list(out[2:2 + na]), list(out[2 + na:2 + 2 * na])), out[-1]


def _exchange_wait(handle, after, scatter, name):
    send_sems, recv_sems, srcs, lands = handle
    na = len(srcs)

    def body(*refs):
        src_refs, land_refs = refs[:na], refs[na:2 * na]
        s_sems, r_sems = refs[2 * na], refs[2 * na + 1]
        sends, recvs = _split_copies(src_refs, land_refs, s_sems, r_sems, scatter)
        for cp in sends:
            cp.wait_send()
        for cp in recvs:
            cp.wait_recv()

    hbm = lambda a: pltpu.HBM(a.shape, a.dtype)
    out = pl.pallas_call(
        body, name=name, out_shape=(*[hbm(s) for s in srcs], *[hbm(l) for l in lands]),
        in_specs=[_HBM] * (2 * na) + [_SEM, _SEM, pl.BlockSpec(memory_space=pl.ANY)],
        out_specs=tuple([_HBM] * (2 * na)), input_output_aliases={i: i for i in range(2 * na)},
        compiler_params=pltpu.CompilerParams(has_side_effects=_DATAFLOW),
    )(*srcs, *lands, send_sems, recv_sems, after)
    return list(out[:na]), list(out[na:])


def _own_slot(landed, own):
    me = 4 * lax.axis_index("x") + 2 * lax.axis_index("y") + lax.axis_index("c")
    return lax.dynamic_update_slice_in_dim(landed, own[None], me, axis=0)


def _adam_update(parts, w, m, v, name, tr=256, turned=False):
    _, r, c = w.shape
    tr = _pick_rows(r, tr)
    cp = parts.shape[2]
    flat = turned and c % 8 != 0
    at = (slice(None), 0) if flat else (0,)

    def body(p_ref, w_ref, m_ref, v_ref, g_ref, d_ref, nm_ref, nv_ref):
        cols = pl.ds(0, cp if turned else c)
        g = p_ref[0, :, cols].astype(F32)
        for i in range(1, N_DEV):
            g = g + p_ref[i, :, cols].astype(F32)
        if turned:
            g = g.T[:c]
        delta, nm, nv = _adamw(w_ref[at], g, m_ref[at], v_ref[at])
        g_ref[at] = g
        d_ref[at] = delta
        nm_ref[at] = nm
        nv_ref[at] = nv

    there, back = ((2, 0, 1), (1, 2, 0)) if flat else ((0, 2, 1), (0, 2, 1))
    if turned:
        w, m, v = (jnp.transpose(a, there) for a in (w, m, v))
        rs = pl.BlockSpec((c, 1, tr), lambda i: (0, 0, i)) if flat else pl.BlockSpec((1, c, tr), lambda i: (0, 0, i))
    else:
        rs = pl.BlockSpec((1, tr, c), lambda i: (0, i, 0))
    outs = pl.pallas_call(
        body, grid=(r // tr,), in_specs=[pl.BlockSpec((N_DEV, tr, cp), lambda i: (0, i, 0)), rs, rs, rs],
        out_specs=[rs] * 4, out_shape=[jax.ShapeDtypeStruct(w.shape, F32)] * 4, name=name,
        compiler_params=_params(("parallel",)))(parts, w, m, v)
    return [*([jnp.transpose(o, back) for o in outs] if turned else outs), outs[0]]


def _pick_rows(rows, target):
    if rows <= target:
        return rows
    t = target
    while t >= 16:
        if rows % t == 0:
            return t
        t -= 16
    return rows


BIG = ("w_in", "w_branch_dn", "w_branch_swa", "w_out", "w_gate", "w_up", "w_down")
IN_SHARD, IN_WIRE = D_IN // N_DEV, 640
FF_SHARD, FF_WIRE = D_FF // N_DEV, 384
D_FFP = N_DEV * FF_WIRE
BIG_SHAPES = {"w_in": ((D_MODEL, IN_SHARD), (D_MODEL, IN_WIRE)),
              "w_branch_dn": ((DN_WIDTH, LANES), (DN_WIDTH, LANES)),
              "w_branch_swa": ((SWA_WIDTH, LANES), (SWA_WIDTH, LANES)),
              "w_out": ((LANES, D_MODEL), (LANES, D_MODEL)),
              "w_gate": ((D_MODEL, FF_SHARD), (D_MODEL, FF_WIRE)),
              "w_up": ((D_MODEL, FF_SHARD), (D_MODEL, FF_WIRE)),
              "w_down": ((FF_SHARD, D_MODEL), (FF_WIRE, D_MODEL))}
CONV_SHARD, CONV_WIRE = (DN_CONV, DN_QKV // N_DEV), (8, 256)


def _pad_to(a, shape):
    return jnp.pad(a, [(0, t - s) for s, t in zip(a.shape, shape)])


IN_TILE_ROWS = 256
_IN_SEGS = ((R_GATE, 2048, P_GATE), (R_QKV, DN_QKV, P_QKV), (R_Z, DN_WIDTH, P_Z), (R_SQ, SWA_WIDTH, P_SQ),
            (R_SK, SWA_KVW, P_SK), (R_SV, SWA_KVW, P_SV), (R_B, 8, P_BA))


def _w_in_from_blocks(blocks, after):
    tm = IN_TILE_ROWS

    def body(b_ref, _, o_ref):
        parts = []
        for rs, n, _ in _IN_SEGS:
            for dev in range(N_DEV):
                lo, hi = max(rs, IN_SHARD * dev), min(rs + n, IN_SHARD * (dev + 1))
                if lo < hi:
                    parts.append(b_ref[dev][:, lo - IN_SHARD * dev:hi - IN_SHARD * dev])
        parts.append(jnp.zeros((tm, P_WIDTH - P_BA - 8), b_ref.dtype))
        o_ref[...] = jnp.concatenate(parts, axis=1)

    return pl.pallas_call(
        body, grid=(D_MODEL // tm,),
        in_specs=[pl.BlockSpec((N_DEV, tm, IN_WIRE), lambda i: (0, i, 0)), pl.BlockSpec(memory_space=pl.ANY)],
        out_specs=pl.BlockSpec((tm, P_WIDTH), lambda i: (i, 0)),
        out_shape=jax.ShapeDtypeStruct((D_MODEL, P_WIDTH), blocks.dtype), name="w_in_from_blocks",
        compiler_params=_params(("parallel",)))(blocks, after)


def _w_in_to_blocks(g):
    tm = IN_TILE_ROWS

    def body(g_ref, o_ref):
        for dev in range(N_DEV):
            parts = []
            for rs, n, ps in sorted(_IN_SEGS):
                lo, hi = max(rs, IN_SHARD * dev), min(rs + n, IN_SHARD * (dev + 1))
                if lo < hi:
                    parts.append(g_ref[:, ps + lo - rs:ps + hi - rs])
            parts.append(jnp.zeros((tm, IN_WIRE - IN_SHARD), g_ref.dtype))
            o_ref[dev] = jnp.concatenate(parts, axis=1)

    return pl.pallas_call(
        body, grid=(D_MODEL // tm,), in_specs=[pl.BlockSpec((tm, P_WIDTH), lambda i: (i, 0))],
        out_specs=pl.BlockSpec((N_DEV, tm, IN_WIRE), lambda i: (0, i, 0)),
        out_shape=jax.ShapeDtypeStruct((N_DEV, D_MODEL, IN_WIRE), g.dtype), name="w_in_to_blocks",
        compiler_params=_params(("parallel",)))(g)


SMALL = {"attn_norm": (0, (1, D_MODEL)), "ffn_norm": (1, (1, D_MODEL)), "dn_out_norm": (2, (1, DN_DIM)),
         "swa_q_norm": (3, (1, SWA_DIM)), "swa_k_norm": (4, (1, SWA_DIM)), "dn_a_log": (5, (1, DN_HEADS)),
         "dn_dt_bias": (6, (1, DN_HEADS)), "swa_sinks": (7, (1, SWA_HEADS)), "rel_bias": (8, (REL_BUCKETS, SWA_HEADS))}
SMALL_SHEET = (48, D_MODEL)


LOSS_ROW = 40


def _small_pack(grads, loss_local):
    names = list(SMALL)

    def body(*refs):
        o_ref = refs[-1]
        o_ref[...] = jnp.zeros_like(o_ref)
        for n, ref in zip(names, refs):
            r0, (nr, nc) = SMALL[n]
            o_ref[r0:r0 + nr, 0:nc] = ref[...]
        o_ref[LOSS_ROW:LOSS_ROW + 1, 0:1] = refs[len(names)][...]

    return pl.pallas_call(
        body, in_specs=[pl.BlockSpec(memory_space=pltpu.VMEM)] * (len(names) + 1),
        out_specs=pl.BlockSpec(memory_space=pltpu.VMEM), out_shape=jax.ShapeDtypeStruct(SMALL_SHEET, F32),
        name="small_pack", compiler_params=_params())(*[grads[n].reshape(SMALL[n][1]) for n in names], loss_local)


def _small_update(sheets, w, m, v):
    names = list(SMALL)
    k = len(names)

    def body(*refs):
        p_ref = refs[0]
        ins, outs = refs[1:1 + 3 * k], refs[1 + 3 * k:]
        loss = p_ref[0, LOSS_ROW:LOSS_ROW + 1, 0:1]
        for i in range(1, N_DEV):
            loss = loss + p_ref[i, LOSS_ROW:LOSS_ROW + 1, 0:1]
        outs[4 * k][...] = loss
        for t, n in enumerate(names):
            r0, (nr, nc) = SMALL[n]
            g = p_ref[0, r0:r0 + nr, 0:nc]
            for i in range(1, N_DEV):
                g = g + p_ref[i, r0:r0 + nr, 0:nc]
            delta, nm, nv = _adamw(ins[t][...], g, ins[k + t][...], ins[2 * k + t][...])
            for kind, val in enumerate((g, delta, nm, nv)):
                outs[kind * k + t][...] = val

    shapes = [jax.ShapeDtypeStruct(SMALL[n][1], F32) for n in names]
    vm = pl.BlockSpec(memory_space=pltpu.VMEM)
    res = pl.pallas_call(
        body, in_specs=[vm] * (1 + 3 * k), out_specs=[vm] * (4 * k + 1),
        out_shape=shapes * 4 + [jax.ShapeDtypeStruct((1, 1), F32)], name="adam_small", compiler_params=_params(),
    )(sheets, *[d[n].reshape(SMALL[n][1]) for d in (w, m, v) for n in names])
    return {n: tuple(res[kind * k + t] for kind in range(4)) for t, n in enumerate(names)}, res[4 * k]


def kernel(x, attn_norm, w_in, dn_conv, dn_a_log, dn_dt_bias, dn_out_norm, swa_q_norm, swa_k_norm, swa_sinks, rel_bias, w_branch_dn, w_branch_swa, w_out, ffn_norm, w_gate, w_up, w_down, loss_target, m_attn_norm, m_w_in, m_dn_conv, m_dn_a_log, m_dn_dt_bias, m_dn_out_norm, m_swa_q_norm, m_swa_k_norm, m_swa_sinks, m_rel_bias, m_w_branch_dn, m_w_branch_swa, m_w_out, m_ffn_norm, m_w_gate, m_w_up, m_w_down, v_attn_norm, v_w_in, v_dn_conv, v_dn_a_log, v_dn_dt_bias, v_dn_out_norm, v_swa_q_norm, v_swa_k_norm, v_swa_sinks, v_rel_bias, v_w_branch_dn, v_w_branch_swa, v_w_out, v_ffn_norm, v_w_gate, v_w_up, v_w_down):
    args = dict(locals())
    S = x.shape[1]
    xs = x.reshape(S, D_MODEL)
    target = loss_target.reshape(S, D_MODEL)

    w_loc = {n: args[n].reshape(BIG_SHAPES[n][0]) for n in BIG}
    conv_loc = dn_conv.reshape(CONV_SHARD)
    def on_wire(n, zero=0.0):
        return _pad_to(w_loc[n] + zero, BIG_SHAPES[n][1]).astype(BF16)

    first_handle, first_token = _all_gather_start([on_wire("w_in"), _pad_to(conv_loc, CONV_WIRE)],
                                                  "all_gather_weights_start")
    zero = first_token[0, 0]
    h = _norm_fwd(xs, attn_norm + zero, "norm1_fwd")
    later = [n for n in BIG if n != "w_in"]
    wire = [on_wire(n, zero) for n in later]
    bias = _bias_fwd(rel_bias + zero)
    first = _all_gather_finish(first_handle, [h, bias] + wire, "all_gather_weights_finish")
    rest_handle, rest_token = _exchange_start(wire, False, "gather_rest_start", after=first[1])
    w_pad = _w_in_from_blocks(first[0], rest_token)
    conv_w = jnp.concatenate([first[1][d, :DN_CONV, :CONV_SHARD[1]] for d in range(N_DEV)], axis=1)

    proj = _mm([(h, w_pad)], "nn", F32, "mm_in", 1024, 1664, j_outer=True)
    qkvn, conv_out = _dn_conv_fwd(proj, conv_w)
    beta, g = _dn_gate_fwd(proj, dn_a_log, dn_dt_bias)
    u, w, qe, kd, qk, egl, tinv = _dn_prep_fwd(qkvn, g, beta)
    o, states = _dn_scan_fwd(u, w, qe, kd, qk, egl)
    y_dn = _dn_out_fwd(o, proj, dn_out_norm)
    y_swa = _swa_fwd(proj, swa_q_norm, swa_k_norm, swa_sinks, bias)
    rest_src, rest_land = _exchange_wait(rest_handle, y_swa, False, "gather_rest_wait")
    G = {n: _own_slot(land, src) for n, src, land in zip(later, rest_src, rest_land)}
    w_bdn, w_bswa, w_g, w_u = G["w_branch_dn"], G["w_branch_swa"], G["w_gate"], G["w_up"]
    w_o = G["w_out"].reshape(D_MODEL, D_MODEL)
    w_d = G["w_down"].reshape(D_FFP, D_MODEL)
    gates = [(proj, P_GATE // 512), (proj, (P_GATE + D_MODEL) // 512)]
    a_dn, a_swa, merged = _mm_fused(
        [(y_dn, w_bdn), (y_swa, w_bswa)], "nn", "mm_branch_merge", 1024, 512,
        lambda p, e: (p[0], p[1], _merge(e[0], e[1], p[0], p[1])), gates, (F32, F32, BF16), b_blocks=True)

    def resid_norm(p, e):
        x1 = e[0] + p[0]
        return x1, _rms(x1, e[1])

    x1, h2 = _mm_fused([(merged, w_o)], "nn", "mm_out_norm", 512, D_MODEL, resid_norm,
                       [(xs, 0), (ffn_norm, None)], (F32, BF16))
    gate, up, act = _mm_fused([(h2, w_g), (h2, w_u)], "nn", "mm_gate_up_act", 1024, 768,
                              lambda p, e: (p[0], p[1], _act(p[0], p[1])), [], (F32, F32, BF16),
                              j_outer=True, b_blocks=True)

    def loss_head(p, e):
        diff = e[0] + p[0] - e[1]
        dy = diff * (1.0 / D_MODEL)
        part = jnp.sum(jnp.mean(diff * diff, axis=-1, keepdims=True), axis=0, keepdims=True) * 0.5
        return dy, dy, part

    dy, dy_b, loss_local = _mm_fused([(act, w_d)], "nn", "mm_down_loss", 512, D_MODEL, loss_head,
                                     [(x1, 0), (target, 0)], (F32, BF16), sum_shape=(1, 1))

    def act_bwd(p, e):
        _, vjp = jax.vjp(_act, e[0], e[1])
        return vjp(p[0])

    dgate, dup = _mm_fused([(dy_b, w_d)], "nt", "mm_dact_act", 1024, 768, act_bwd, [(gate, 0), (up, 0)],
                           (BF16, BF16), j_outer=True)
    g_w_down = _mm([(act, dy_b)], "tn", BF16, "mm_dw_down", 768, D_MODEL, j_outer=True)
    g_w_down = g_w_down.reshape(N_DEV, FF_WIRE, D_MODEL)
    g_w_gate = _mm([(h2, dgate)], "tn", BF16, "mm_dw_gate", D_MODEL, 768, out_blocks=True)
    g_w_up = _mm([(h2, dup)], "tn", BF16, "mm_dw_up", D_MODEL, 768, out_blocks=True)
    ffn_handle, ffn_token = _exchange_start([g_w_down, g_w_gate, g_w_up], True, "scatter_ffn_start")

    def norm_bwd(p, e):
        _, vjp = jax.vjp(_rms, e[0], e[2])
        dx, dgain = vjp(sum(p))
        dx = dx + e[1]
        return dx, dx, dgain

    dx1, dx1_b, g_ffn_norm = _mm_fused(
        [(dgate, w_g), (dup, w_u)], "nt", "mm_dh2_norm", 256, D_MODEL, norm_bwd,
        [(x1, 0), (dy, 0), (ffn_norm + ffn_token[0, 0], None)], (F32, BF16), b_blocks=True, sum_shape=(1, D_MODEL))
    def merge_bwd(p, e):
        _, vjp = jax.vjp(_merge, *e)
        dg0, dg1, da_dn, da_swa = vjp(p[0])
        return jnp.concatenate([dg0, dg1], axis=1), da_dn, da_swa

    dproj, da_dn, da_swa = _mm_fused(
        [(dx1_b, w_o)], "nt", "mm_dmerged_merge", 512, D_MODEL, merge_bwd,
        [(proj, P_GATE // D_MODEL), (proj, P_GATE // D_MODEL + 1), (a_dn, 0), (a_swa, 0)], (BF16,) * 3,
        wide_first=(P_WIDTH, 2 * D_MODEL))
    g_w_out = _mm([(merged, dx1_b)], "tn", BF16, "mm_dw_out", 512, D_MODEL, j_outer=True)
    g_w_out = g_w_out.reshape(N_DEV, LANES, D_MODEL)
    dy_dn = _mm([(da_dn, w_bdn)], "nt", F32, "mm_dy_dn", 1024, DN_WIDTH, b_blocks=True)
    dy_swa = _mm([(da_swa, w_bswa)], "nt", F32, "mm_dy_swa", 1024, SWA_WIDTH, b_blocks=True)
    g_w_bdn = _mm([(y_dn, da_dn)], "tn", BF16, "mm_dw_branch_dn", DN_WIDTH, 512, out_blocks=True)
    g_w_bswa = _mm([(y_swa, da_swa)], "tn", BF16, "mm_dw_branch_swa", SWA_WIDTH, 512, out_blocks=True)
    dproj, dsk, dsv, g_q_norm, g_k_norm, g_sinks, dbias = _swa_bwd(proj, swa_q_norm, swa_k_norm, swa_sinks, bias,
                                                                   dy_swa, dproj)
    dproj = _kv_into(dproj, dsk, dsv)
    g_rel_bias = _bias_bwd(dbias)[:, :REL_BUCKETS].T
    mix_handle, mix_token = _exchange_start([g_w_out, g_w_bdn, g_w_bswa], True, "scatter_mix_start")
    do, dproj, g_out_norm = _dn_out_bwd(o, proj, dn_out_norm + mix_token[0, 0], dy_dn, dproj)
    du, dw, dqe, dkd, dqk, degl = _dn_scan_bwd(u, w, qe, kd, qk, egl, states, do)
    dqkvn, dgd, dbeta = _dn_prep_bwd(qkvn, g, beta, tinv, du, dw, dqe, dkd, dqk, degl)
    dproj, dal, ddt = _dn_gate_bwd(proj, dn_a_log, dn_dt_bias, dbeta, dgd, dproj)
    g_a_log = dal.reshape(DN_HEADS, DN_DIM).sum(axis=1)
    g_dt_bias = ddt[0, DN_HEADS:2 * DN_HEADS]
    dproj, g_conv = _dn_conv_bwd(proj, conv_w, conv_out, dqkvn, dproj)
    g_w_in = _w_in_to_blocks(_mm([(h, dproj)], "tn", BF16, "mm_dw_in", 512, 1664, j_outer=True))
    in_handle, in_token = _exchange_start([g_w_in], True, "scatter_in_start")
    dx, g_attn_norm = _mm_fused(
        [(dproj, w_pad)], "nt", "mm_dh_norm", 512, D_MODEL, lambda p, e: norm_bwd(p, e)[1:],
        [(xs, 0), (dx1, 0), (attn_norm + in_token[0, 0], None)], (F32,), sum_shape=(1, D_MODEL))

    g_small = {"attn_norm": g_attn_norm, "ffn_norm": g_ffn_norm, "rel_bias": g_rel_bias, "dn_out_norm": g_out_norm,
               "swa_q_norm": g_q_norm, "swa_k_norm": g_k_norm, "dn_a_log": g_a_log, "dn_dt_bias": g_dt_bias,
               "swa_sinks": g_sinks}
    me = 4 * lax.axis_index("x") + 2 * lax.axis_index("y") + lax.axis_index("c")
    outs = {}

    def finish(handle, group, name, after):
        srcs, lands = _exchange_wait(handle, after, True, name)
        for n, src, land in zip(group, srcs, lands):
            parts = _own_slot(land, lax.dynamic_index_in_dim(src, me, 0, keepdims=False))
            outs[n] = _adam_update(parts, args[n], args["m_" + n], args["v_" + n], "adam_" + n,
                                   turned=args[n].shape[2] % LANES != 0)

    finish(ffn_handle, ("w_down", "w_gate", "w_up"), "scatter_ffn_wait", dx)
    finish(mix_handle, ("w_out", "w_branch_dn", "w_branch_swa"), "scatter_mix_wait", dx)
    sheets, conv_all = _all_gather_direct([_small_pack(g_small, loss_local), _pad_to(g_conv, (8, DN_QKV))],
                                          "all_gather_small",
                                          after=[outs[n][4] for n in sorted(outs)])
    finish(in_handle, ("w_in",), "scatter_in_wait", sheets)
    conv_parts = lax.dynamic_slice(conv_all, (0, 0, me * CONV_SHARD[1]), (N_DEV,) + CONV_SHARD)
    outs["dn_conv"] = _adam_update(conv_parts, dn_conv, m_dn_conv, v_dn_conv, "adam_dn_conv")
    small_outs, loss = _small_update(sheets, {n: args[n] for n in SMALL}, {n: args["m_" + n] for n in SMALL},
                                     {n: args["v_" + n] for n in SMALL})
    outs.update(small_outs)

    names = ("attn_norm", "w_in", "dn_conv", "dn_a_log", "dn_dt_bias", "dn_out_norm", "swa_q_norm", "swa_k_norm",
             "swa_sinks", "rel_bias", "w_branch_dn", "w_branch_swa", "w_out", "ffn_norm", "w_gate", "w_up", "w_down")
    results = []
    for kind in range(4):
        results += [outs[n][kind].reshape(args[n].shape) for n in names]

    return (loss.reshape(()), dx.reshape(x.shape), *results)
```

```python
import math

import numpy as np
import jax
import jax.numpy as jnp
from jax import lax
from jax.experimental import pallas as pl
from jax.experimental.pallas import tpu as pltpu

F32 = jnp.float32
BF16 = jnp.bfloat16
HI = lax.Precision.HIGHEST

D_MODEL = 1024
DN_HEADS = 4
DN_DIM = 128
DN_WIDTH = 512
DN_QKV = 1536
DN_CONV = 4
CHUNK = 64
SWA_HEADS = 8
SWA_KV = 2
SWA_GROUP = 4
SWA_DIM = 64
SWA_WIDTH = 512
SWA_KVW = 128
WINDOW = 128
BLOCK = 128
REL_BUCKETS = 32
REL_MAX_DIST = 128
D_FF = 2816
D_IN = 4872
EPS = 1e-6
N_DEV = 8

ADAM_LR = 0.001
ADAM_B1 = 0.9
ADAM_B2 = 0.999
ADAM_EPS = 1e-08
ADAM_WD = 0.01
ADAM_STEP = 10

P_GATE, P_QKV, P_Z, P_SQ, P_SK, P_SV, P_BA = 0, 2048, 3584, 4096, 4608, 4736, 4864
P_WIDTH = 4992
R_QKV, R_Z, R_B, R_A, R_SQ, R_SK, R_SV, R_GATE = 0, 1536, 2048, 2052, 2056, 2568, 2696, 2824

VMEM_LIMIT = 56 * 1024 * 1024
LANES = 128
MESH_ID = pl.DeviceIdType.MESH


def _params(sem=None):
    return pltpu.CompilerParams(dimension_semantics=sem, vmem_limit_bytes=VMEM_LIMIT)


def _pick(dim, target):
    if dim <= target:
        return dim
    t = target - target % LANES
    while t >= LANES:
        if dim % t == 0:
            return t
        t -= LANES
    return dim


_DIMS = {"nn": (((1,), (0,)), ((), ())), "nt": (((1,), (1,)), ((), ())), "tn": (((0,), (0,)), ((), ()))}


def _tile_product(a_ref, b_ref, mode, b_blocks):
    a = a_ref[...].astype(BF16)
    b = jnp.concatenate([b_ref[d] for d in range(b_ref.shape[0])], axis=1) if b_blocks else b_ref[...]
    return lax.dot_general(a, b.astype(BF16), _DIMS[mode], preferred_element_type=F32)


def _mm(pairs, mode, out_dtype, name, bm, bn, j_outer=False, b_blocks=False, out_blocks=False):
    a0, b0 = pairs[0]
    cb = b0.shape[2] if b_blocks else None
    b_shape = (b0.shape[1], N_DEV * cb) if b_blocks else b0.shape
    if mode == "nn":
        (M, K), (K2, N) = a0.shape, b_shape
    elif mode == "nt":
        (M, K), (N, K2) = a0.shape, b_shape
    else:
        (K, M), (K2, N) = a0.shape, b_shape
    bm, bn = min(bm, M), min(bn, N)
    assert K == K2 and M % bm == 0 and N % bn == 0, (name, a0.shape, b0.shape, bm, bn)
    co = N // N_DEV
    assert not out_blocks or bn % co == 0
    dims = _DIMS[mode]
    n = len(pairs)

    def body(*refs):
        o_ref = refs[2 * n]
        acc = None
        for t in range(n):
            p = _tile_product(refs[2 * t], refs[2 * t + 1], mode, b_blocks)
            acc = p if acc is None else acc + p
        if out_blocks:
            for d in range(bn // co):
                o_ref[d] = acc[:, d * co:(d + 1) * co].astype(out_dtype)
        else:
            o_ref[...] = acc.astype(out_dtype)

    def ij(f):
        return (lambda j, i: f(i, j)) if j_outer else f

    a_spec = pl.BlockSpec((K, bm), ij(lambda i, j: (0, i))) if mode == "tn" else pl.BlockSpec((bm, K), ij(lambda i, j: (i, 0)))
    if b_blocks and mode == "nt":
        b_spec = pl.BlockSpec((N_DEV, bn, cb), ij(lambda i, j: (0, j, 0)))
    elif b_blocks:
        b_spec = pl.BlockSpec((bn // cb, K, cb), ij(lambda i, j: (j, 0, 0)))
    elif mode == "nt":
        b_spec = pl.BlockSpec((bn, K), ij(lambda i, j: (j, 0)))
    else:
        b_spec = pl.BlockSpec((K, bn), ij(lambda i, j: (0, j)))
    if out_blocks:
        out_spec = pl.BlockSpec((bn // co, bm, co), ij(lambda i, j: (j, i, 0)))
        out_shape = jax.ShapeDtypeStruct((N_DEV, M, co), out_dtype)
    else:
        out_spec = pl.BlockSpec((bm, bn), ij(lambda i, j: (i, j)))
        out_shape = jax.ShapeDtypeStruct((M, N), out_dtype)
    grid = (N // bn, M // bm) if j_outer else (M // bm, N // bn)
    return pl.pallas_call(
        body, grid=grid, in_specs=[a_spec, b_spec] * n, out_specs=out_spec, out_shape=out_shape, name=name,
        compiler_params=_params(("parallel", "parallel")),
    )(*[x for pair in pairs for x in pair])


def _mm_fused(pairs, mode, name, bm, bn, epilogue, extras, out_dtypes, j_outer=False, b_blocks=False,
              sum_shape=None, wide_first=None):
    a0, b0 = pairs[0]
    cb = b0.shape[2] if b_blocks else None
    b_shape = (b0.shape[1], N_DEV * cb) if b_blocks else b0.shape
    if mode == "nn":
        (M, K), (K2, N) = a0.shape, b_shape
    else:
        (M, K), (N, K2) = a0.shape, b_shape
    bm, bn = min(bm, M), min(bn, N)
    assert mode in ("nn", "nt") and K == K2 and M % bm == 0 and N % bn == 0, (name, a0.shape, b0.shape)
    dims = _DIMS[mode]
    n, ne, no = len(pairs), len(extras), len(out_dtypes)

    def body(*refs):
        prods = [_tile_product(refs[2 * t], refs[2 * t + 1], mode, b_blocks) for t in range(n)]
        results = epilogue(prods, [r[...] for r in refs[2 * n:2 * n + ne]])
        out_refs = refs[2 * n + ne:]
        for o_ref, val, dt in zip(out_refs, results, out_dtypes):
            o_ref[...] = val.astype(dt)
        if sum_shape is not None:
            s_ref = out_refs[no]

            @pl.when((pl.program_id(0) == 0) & (pl.program_id(1) == 0))
            def _():
                s_ref[...] = jnp.zeros_like(s_ref)

            s_ref[...] += results[no]

    def ij(f):
        return (lambda j, i: f(i, j)) if j_outer else f

    a_spec = pl.BlockSpec((bm, K), ij(lambda i, j: (i, 0)))
    once = dict(pipeline_mode=pl.Buffered(1)) if bn == N else {}
    if b_blocks and mode == "nt":
        b_spec = pl.BlockSpec((N_DEV, bn, cb), ij(lambda i, j: (0, j, 0)), **once)
    elif b_blocks:
        b_spec = pl.BlockSpec((bn // cb, K, cb), ij(lambda i, j: (j, 0, 0)), **once)
    elif mode == "nt":
        b_spec = pl.BlockSpec((bn, K), ij(lambda i, j: (j, 0)), **once)
    else:
        b_spec = pl.BlockSpec((K, bn), ij(lambda i, j: (0, j)), **once)
    e_specs = [pl.BlockSpec((1, bn), ij(lambda i, j: (0, j))) if first is None
               else pl.BlockSpec((bm, bn), ij(lambda i, j, first=first: (i, first + j))) for _, first in extras]
    tile = pl.BlockSpec((bm, bn), ij(lambda i, j: (i, j)))
    out_specs = [tile] * no
    out_shape = [jax.ShapeDtypeStruct((M, N), dt) for dt in out_dtypes]
    if wide_first is not None:
        assert bn == N
        out_specs[0] = pl.BlockSpec((bm, wide_first[1]), ij(lambda i, j: (i, 0)))
        out_shape[0] = jax.ShapeDtypeStruct((M, wide_first[0]), out_dtypes[0])
    if sum_shape is not None:
        assert sum_shape[1] in (1, bn) and (sum_shape[1] == 1 or bn == N)
        out_specs.append(_full(sum_shape))
        out_shape.append(jax.ShapeDtypeStruct(sum_shape, F32))
    grid = (N // bn, M // bm) if j_outer else (M // bm, N // bn)
    sem = ("arbitrary", "arbitrary") if sum_shape is not None else ("parallel", "parallel")
    return pl.pallas_call(
        body, grid=grid, in_specs=[a_spec, b_spec] * n + e_specs, out_specs=out_specs, out_shape=out_shape,
        name=name, compiler_params=_params(sem),
    )(*[x for pair in pairs for x in pair], *[arr for arr, _ in extras])


def _rms(x, gain):
    return x * lax.rsqrt(jnp.mean(x * x, axis=-1, keepdims=True) + EPS) * gain


def _silu(x):
    return x * jax.nn.sigmoid(x)


def _act(g, u):
    return _silu(g) * u


def _merge(g0, g1, a_dn, a_swa):
    return jax.nn.sigmoid(g0) * a_dn + jax.nn.sigmoid(g1) * a_swa


def _dn_post(c, is_v, q_scale):
    a = _silu(c)
    rs = lax.rsqrt(jnp.sum(a * a, axis=-1, keepdims=True) + EPS) * q_scale
    return a * jnp.where(is_v, 1.0, rs)


def _dn_post_bwd(c, d, is_v, q_scale):
    sig = jax.nn.sigmoid(c)
    a = c * sig
    u = lax.rsqrt(jnp.sum(a * a, axis=-1, keepdims=True) + EPS)
    f = jnp.where(is_v, 1.0, u * q_scale)
    k = jnp.where(is_v, 0.0, jnp.sum(d * a, axis=-1, keepdims=True) * (u * u * u) * q_scale)
    return (f * d - a * k) * (sig + a * (1.0 - sig))


def _dn_out(o, z, gain):
    return _rms(o, gain) * _silu(z)


def _dot(a, b, dims=_DIMS["nn"], hi=False):
    if a.ndim == 3 or b.ndim == 3:
        batch = a.shape[0] if a.ndim == 3 else b.shape[0]
        a = a if a.ndim == 3 else jnp.broadcast_to(a, (batch,) + a.shape)
        b = b if b.ndim == 3 else jnp.broadcast_to(b, (batch,) + b.shape)
        ((ca,), (cb,)), _ = dims
        dims = (((ca + 1,), (cb + 1,)), ((0,), (0,)))
    if hi:
        return lax.dot_general(a, b, dims, precision=HI, preferred_element_type=F32)
    return lax.dot_general(a.astype(BF16), b.astype(BF16), dims, preferred_element_type=F32)


def _pieces(x):
    hi = x.astype(BF16)
    r1 = x - hi.astype(F32)
    mid = r1.astype(BF16)
    return hi, mid, (r1 - mid.astype(F32)).astype(BF16)


def _sel_left_impl(m, x):
    mb = m.astype(BF16)
    hi, mid, lo = _pieces(x)
    return _dot(mb, hi) + (_dot(mb, mid) + _dot(mb, lo))


@jax.custom_vjp
def _sel_left(m, mt, x):
    return _sel_left_impl(m, x)


_sel_left.defvjp(lambda m, mt, x: (_sel_left_impl(m, x), (m, mt)),
                 lambda res, ct: (jnp.zeros_like(res[0]), jnp.zeros_like(res[1]), _sel_left_impl(res[1], ct)))


def _sel_right_impl(x, s):
    sb = s.astype(BF16)
    hi, mid, lo = _pieces(x)
    return _dot(hi, sb) + (_dot(mid, sb) + _dot(lo, sb))


@jax.custom_vjp
def _sel_right(x, s, st):
    return _sel_right_impl(x, s)


_sel_right.defvjp(lambda x, s, st: (_sel_right_impl(x, s), (s, st)),
                  lambda res, ct: (_sel_right_impl(ct, res[1]), jnp.zeros_like(res[0]), jnp.zeros_like(res[1])))


def _dot3_impl(a, b):
    a_hi, a_lo, _ = _pieces(a)
    b_hi, b_lo, _ = _pieces(b)
    return _dot(a_hi, b_hi) + (_dot(a_hi, b_lo) + _dot(a_lo, b_hi))


@jax.custom_vjp
def _dot3(a, b):
    return _dot3_impl(a, b)


_dot3.defvjp(lambda a, b: (_dot3_impl(a, b), (a, b)),
             lambda res, ct: (_dot(ct, res[1], _DIMS["nt"]), _dot(res[0], ct, _DIMS["tn"])))


def _inv_impl(a, eye, strict):
    t = eye - a
    p = _dot(a, a)
    for level in range(5):
        t = t + _dot(t, p)
        if level < 4:
            p = _dot(p, p)
    t = t + _dot(t, eye - t - _dot3_impl(a, t))
    return jnp.where(strict > 0.5, t, eye)


@jax.custom_vjp
def _inv_given(a, t):
    return t.astype(F32)


_inv_given.defvjp(lambda a, t: (t.astype(F32), t),
                  lambda t, ct: (-_dot(_dot(t, ct, _DIMS["tn"]), t, _DIMS["nt"]), jnp.zeros_like(t)))


@jax.custom_vjp
def _lanes_join(a, b):
    return jnp.concatenate([a, b], axis=-1)


_lanes_join.defvjp(lambda a, b: (jnp.concatenate([a, b], axis=-1), None),
                   lambda _, ct: (ct[..., :ct.shape[-1] // 2], ct[..., ct.shape[-1] // 2:]))


@jax.custom_vjp
def _lanes_halves(y):
    h = y.shape[-1] // 2
    return y[..., :h], y[..., h:]


_lanes_halves.defvjp(lambda y: ((y[..., :y.shape[-1] // 2], y[..., y.shape[-1] // 2:]), None),
                     lambda _, ct: (jnp.concatenate(ct, axis=-1),))

GROUP = 4
GROUP_ROWS = GROUP * CHUNK


def _block_consts(n):
    ii = lax.broadcasted_iota(jnp.int32, (n, n), 0)
    jj = lax.broadcasted_iota(jnp.int32, (n, n), 1)
    shift = CHUNK.bit_length() - 1
    same = jnp.right_shift(ii, shift) == jnp.right_shift(jj, shift)
    return same & (ii >= jj), same & (ii <= jj), same & (ii > jj), same, ii == jj


def _lane0(n):
    s = (lax.broadcasted_iota(jnp.int32, (LANES, n), 0) == 0).astype(F32)
    st = (lax.broadcasted_iota(jnp.int32, (n, LANES), 1) == 0).astype(F32)
    return s, st


def _dn_group(q, k, v, g, beta, t_saved=None):
    n = GROUP_ROWS
    low_b, upp_b, strict_b, _, eye_b = _block_consts(n)
    low, upp, eye = low_b.astype(F32), upp_b.astype(F32), eye_b.astype(F32)
    gc = _sel_left(low, upp, g)
    per_chunk = (g.shape[0], GROUP, CHUNK, LANES)
    g_last = jnp.sum(g.reshape(per_chunk), axis=2, keepdims=True)
    gl = jnp.broadcast_to(g_last, per_chunk).reshape(g.shape)
    s, st = _lane0(n)
    col = _sel_right(gc, s, st)
    row = jnp.swapaxes(col, 1, 2)
    decay = jnp.exp(jnp.where(low_b, col - row, -jnp.inf))
    kb = k * beta
    vb = v * beta
    a = jnp.where(strict_b, _dot(kb, k, _DIMS["nt"]) * decay, 0.0)
    t = _inv_impl(a, eye, strict_b.astype(F32)) if t_saved is None else _inv_given(a, t_saved)
    u, w = _lanes_halves(_dot3(t, _lanes_join(vb, kb * jnp.exp(gc))))
    fold = (jnp.bitwise_and(lax.broadcasted_iota(jnp.int32, (n, CHUNK), 0), CHUNK - 1)
            == lax.broadcasted_iota(jnp.int32, (n, CHUNK), 1)).astype(F32)
    fold_t = (jnp.bitwise_and(lax.broadcasted_iota(jnp.int32, (CHUNK, n), 1), CHUNK - 1)
              == lax.broadcasted_iota(jnp.int32, (CHUNK, n), 0)).astype(F32)
    qk = _sel_right(_dot(q, k, _DIMS["nt"]) * decay, fold, fold_t)
    return u, w, q * jnp.exp(gc), k * jnp.exp(gl - gc), qk, jnp.exp(g_last), t


def _dn_step(s, u, w, qe, kd, qk, egl):
    v_new = u - _dot(w, s)
    o = _dot(qe, s) + _dot(qk, v_new)
    s_new = s * egl + _dot(kd, v_new, _DIMS["tn"])
    return s_new, o


def _swa_block(q, kband, vband, qg, kg, sinks, band):
    kn = _rms(kband, kg)
    qn = _rms(q, qg) * (SWA_DIM ** -0.5)
    logits = _dot(qn, kn, _DIMS["nt"]) + band
    m = lax.stop_gradient(jnp.maximum(jnp.max(logits, axis=-1, keepdims=True), sinks))
    p = jnp.exp(logits - m)
    denom = jnp.sum(p, axis=-1, keepdims=True) + jnp.exp(sinks - m)
    return _dot(p * (1.0 / denom), vband)


def _adamw(w, g, m, v):
    m = ADAM_B1 * m + (1.0 - ADAM_B1) * g
    v = ADAM_B2 * v + (1.0 - ADAM_B2) * jnp.square(g)
    m_hat = m / (1.0 - ADAM_B1 ** ADAM_STEP)
    v_hat = v / (1.0 - ADAM_B2 ** ADAM_STEP)
    delta = -ADAM_LR * (m_hat / (jnp.sqrt(v_hat) + ADAM_EPS) + ADAM_WD * w)
    return delta, m, v


def _row(tm, c, cb=0):
    return pl.BlockSpec((tm, c), lambda i, cb=cb: (i, cb))


def _full(shape):
    nd = len(shape)
    return pl.BlockSpec(shape, lambda *_, nd=nd: (0,) * nd)


def _norm_fwd(x, gain, name, tm=1024):
    S = x.shape[0]

    def body(x_ref, g_ref, h_ref):
        h_ref[...] = _rms(x_ref[...], g_ref[...]).astype(BF16)

    return pl.pallas_call(
        body, grid=(S // tm,), in_specs=[_row(tm, D_MODEL), _full((1, D_MODEL))],
        out_specs=_row(tm, D_MODEL), out_shape=jax.ShapeDtypeStruct((S, D_MODEL), BF16),
        name=name, compiler_params=_params(("parallel",)))(x, gain)


def _shift_down(x, s):
    row = lax.broadcasted_iota(jnp.int32, x.shape, 0)
    return jnp.where(row >= s, pltpu.roll(x, s, axis=0), 0.0)


def _shift_up(x, s):
    n = x.shape[0]
    row = lax.broadcasted_iota(jnp.int32, x.shape, 0)
    return jnp.where(row < n - s, pltpu.roll(x, n - s, axis=0), 0.0)


def _conv(x, w):
    out = w[DN_CONV - 1:DN_CONV] * x
    for s in range(1, DN_CONV):
        out = out + w[DN_CONV - 1 - s:DN_CONV - s] * _shift_down(x, s)
    return out


def _dn_conv_fwd(proj, conv_w):
    S = proj.shape[0]
    nb = DN_QKV // LANES

    def body(x_ref, w_ref, o_ref, c_ref):
        j = pl.program_id(0)
        q_scale = jnp.where(j < DN_HEADS, DN_DIM ** -0.5, 1.0).astype(F32)
        c = _conv(x_ref[...], w_ref[...])
        c_ref[...] = c
        o_ref[...] = _dn_post(c, j >= 2 * DN_HEADS, q_scale)

    col = pl.BlockSpec((S, LANES), lambda j: (0, j))
    return pl.pallas_call(
        body, grid=(nb,),
        in_specs=[pl.BlockSpec((S, LANES), lambda j: (0, P_QKV // LANES + j)),
                  pl.BlockSpec((DN_CONV, LANES), lambda j: (0, j))],
        out_specs=[col, col], out_shape=[jax.ShapeDtypeStruct((S, DN_QKV), F32)] * 2, name="dn_conv_fwd",
        compiler_params=_params(("parallel",)))(proj, conv_w)


def _dn_conv_bwd(proj, conv_w, conv_out, dqkvn, dproj):
    S = proj.shape[0]
    nb = DN_QKV // LANES

    def body(x_ref, w_ref, c_ref, d_ref, _, dx_ref, dw_ref):
        j = pl.program_id(0)
        q_scale = jnp.where(j < DN_HEADS, DN_DIM ** -0.5, 1.0).astype(F32)
        x = x_ref[...]
        w = w_ref[...]
        dc = _dn_post_bwd(c_ref[...], d_ref[0], j >= 2 * DN_HEADS, q_scale)
        dx = w[DN_CONV - 1:DN_CONV] * dc
        dw_ref[DN_CONV - 1:DN_CONV, :] = jnp.sum(dc * x, axis=0, keepdims=True)
        for s in range(1, DN_CONV):
            up = _shift_up(dc, s)
            dx = dx + w[DN_CONV - 1 - s:DN_CONV - s] * up
            dw_ref[DN_CONV - 1 - s:DN_CONV - s, :] = jnp.sum(up * x, axis=0, keepdims=True)
        dx_ref[...] = dx.astype(BF16)

    return pl.pallas_call(
        body, grid=(nb,),
        in_specs=[pl.BlockSpec((S, LANES), lambda j: (0, P_QKV // LANES + j)),
                  pl.BlockSpec((DN_CONV, LANES), lambda j: (0, j)),
                  pl.BlockSpec((S, LANES), lambda j: (0, j)),
                  pl.BlockSpec((1, S, LANES), lambda j: (lax.div(j, DN_HEADS), 0, lax.rem(j, DN_HEADS))),
                  pl.BlockSpec(memory_space=pl.ANY)],
        out_specs=[pl.BlockSpec((S, LANES), lambda j: (0, P_QKV // LANES + j)),
                   pl.BlockSpec((DN_CONV, LANES), lambda j: (0, j))],
        out_shape=[jax.ShapeDtypeStruct(dproj.shape, dproj.dtype), jax.ShapeDtypeStruct((DN_CONV, DN_QKV), F32)],
        input_output_aliases={4: 0},
        name="dn_conv_bwd", compiler_params=_params(("parallel",)))(proj, conv_w, conv_out, dqkvn, dproj)


def _expanders():
    eb = np.zeros((LANES, DN_WIDTH), np.float32)
    ea = np.zeros((LANES, DN_WIDTH), np.float32)
    for h in range(DN_HEADS):
        eb[h, h * DN_DIM:(h + 1) * DN_DIM] = 1.0
        ea[DN_HEADS + h, h * DN_DIM:(h + 1) * DN_DIM] = 1.0
    return jnp.asarray(eb), jnp.asarray(ea), jnp.asarray(eb.T), jnp.asarray(ea.T)


def _dn_gate_args(a_log, dt_bias):
    alog = jnp.repeat(a_log.reshape(1, DN_HEADS), DN_DIM, axis=1)
    dtb = _pad_to(jnp.pad(dt_bias.reshape(1, DN_HEADS), ((0, 0), (DN_HEADS, 0))), (1, LANES))
    return _expanders() + (alog, dtb)


def _dn_gate_specs(tm):
    return [_row(tm, LANES, P_BA // LANES), _full((LANES, DN_WIDTH)), _full((LANES, DN_WIDTH)),
            _full((DN_WIDTH, LANES)), _full((DN_WIDTH, LANES)), _full((1, DN_WIDTH)), _full((1, LANES))]


def _dn_gate_fn(ba, eb, ea, ebt, eat, alog, dtb):
    beta = _sel_right(jax.nn.sigmoid(ba), eb, ebt)
    g = -jnp.exp(alog) * _sel_right(jax.nn.softplus(ba + dtb), ea, eat)
    return beta, g


def _dn_gate_fwd(proj, a_log, dt_bias, tm=1024):
    S = proj.shape[0]
    args = _dn_gate_args(a_log, dt_bias)

    def body(ba_ref, eb_ref, ea_ref, ebt_ref, eat_ref, al_ref, dt_ref, beta_ref, g_ref):
        beta, g = _dn_gate_fn(ba_ref[...], eb_ref[...], ea_ref[...], ebt_ref[...], eat_ref[...], al_ref[...],
                              dt_ref[...])
        beta_ref[...] = beta
        g_ref[...] = g

    return pl.pallas_call(
        body, grid=(S // tm,), in_specs=_dn_gate_specs(tm), out_specs=[_row(tm, DN_WIDTH), _row(tm, DN_WIDTH)],
        out_shape=[jax.ShapeDtypeStruct((S, DN_WIDTH), F32), jax.ShapeDtypeStruct((S, DN_WIDTH), F32)],
        name="dn_gate_fwd", compiler_params=_params(("parallel",)))(proj, *args)


def _dn_gate_bwd(proj, a_log, dt_bias, dbeta, dg, dproj, tm=1024):
    S = proj.shape[0]
    args = _dn_gate_args(a_log, dt_bias)

    def body(ba_ref, eb_ref, ea_ref, ebt_ref, eat_ref, al_ref, dt_ref, dbeta_ref, dg_ref, _, dba_ref, dal_ref,
             ddt_ref):
        eb, ea, ebt, eat = eb_ref[...], ea_ref[...], ebt_ref[...], eat_ref[...]
        _, vjp = jax.vjp(lambda ba, al, dt: _dn_gate_fn(ba, eb, ea, ebt, eat, al, dt), ba_ref[...], al_ref[...],
                         dt_ref[...])
        dba, dal, ddt = vjp((dbeta_ref[...], dg_ref[...]))
        dba_ref[...] = dba.astype(BF16)

        @pl.when(pl.program_id(0) == 0)
        def _():
            dal_ref[...] = jnp.zeros_like(dal_ref)
            ddt_ref[...] = jnp.zeros_like(ddt_ref)

        dal_ref[...] += dal
        ddt_ref[...] += ddt

    return pl.pallas_call(
        body, grid=(S // tm,),
        in_specs=_dn_gate_specs(tm) + [_row(tm, DN_WIDTH), _row(tm, DN_WIDTH), pl.BlockSpec(memory_space=pl.ANY)],
        out_specs=[_row(tm, LANES, P_BA // LANES), _full((1, DN_WIDTH)), _full((1, LANES))],
        out_shape=[jax.ShapeDtypeStruct(dproj.shape, dproj.dtype), jax.ShapeDtypeStruct((1, DN_WIDTH), F32),
                   jax.ShapeDtypeStruct((1, LANES), F32)],
        input_output_aliases={len(args) + 3: 0},
        name="dn_gate_bwd", compiler_params=_params(("arbitrary",)))(proj, *args, dbeta, dg, dproj)


PREP_GROUPS = 8
PREP_CHUNKS = GROUP * PREP_GROUPS


def _dn_prep_specs():
    rows = PREP_CHUNKS * CHUNK
    q = pl.BlockSpec((rows, LANES), lambda h, c: (c, h))
    k = pl.BlockSpec((rows, LANES), lambda h, c: (c, DN_HEADS + h))
    v = pl.BlockSpec((rows, LANES), lambda h, c: (c, 2 * DN_HEADS + h))
    qk = pl.BlockSpec((1, rows, CHUNK), lambda h, c: (h, c, 0))
    egl = pl.BlockSpec((1, PREP_CHUNKS, 1, LANES), lambda h, c: (h, c, 0, 0))
    return q, k, v, qk, egl


def _dn_prep_fwd(qkvn, g, beta):
    S = qkvn.shape[0]
    nc = S // CHUNK
    q, k, v, qks, egl = _dn_prep_specs()

    def body(q_ref, k_ref, v_ref, g_ref, b_ref, u_ref, w_ref, qe_ref, kd_ref, qk_ref, egl_ref, t_ref):
        rows = PREP_CHUNKS * CHUNK
        grp = (PREP_GROUPS, GROUP_ROWS, LANES)
        u, w, qe, kd, qk, e, t = _dn_group(q_ref[...].reshape(grp), k_ref[...].reshape(grp), v_ref[...].reshape(grp),
                                           g_ref[...].reshape(grp), b_ref[...].reshape(grp))
        u_ref[...] = u.reshape(rows, LANES)
        w_ref[...] = w.reshape(rows, LANES)
        qe_ref[...] = qe.reshape(rows, LANES)
        kd_ref[...] = kd.reshape(rows, LANES)
        t_ref[0] = t.reshape(rows, GROUP_ROWS).astype(BF16)
        qk_ref[0] = qk.reshape(rows, CHUNK)
        egl_ref[0] = e.reshape(PREP_CHUNKS, 1, LANES)

    wide = jax.ShapeDtypeStruct((S, DN_WIDTH), F32)
    return pl.pallas_call(
        body, grid=(DN_HEADS, nc // PREP_CHUNKS), in_specs=[q, k, v, q, q],
        out_specs=[q, q, q, q, qks, egl, _dn_tinv_spec()],
        out_shape=[wide, wide, wide, wide, jax.ShapeDtypeStruct((DN_HEADS, S, CHUNK), F32),
                   jax.ShapeDtypeStruct((DN_HEADS, nc, 1, LANES), F32),
                   jax.ShapeDtypeStruct((DN_HEADS, S, GROUP_ROWS), BF16)],
        name="dn_prep_fwd", compiler_params=_params(("parallel", "parallel")))(qkvn, qkvn, qkvn, g, beta)


def _dn_tinv_spec():
    return pl.BlockSpec((1, PREP_CHUNKS * CHUNK, GROUP_ROWS), lambda h, c: (h, c, 0))


def _dn_prep_bwd(qkvn, g, beta, tinv, du, dw, dqe, dkd, dqk, degl):
    S = qkvn.shape[0]
    nc = S // CHUNK
    q, k, v, qks, egl = _dn_prep_specs()

    def body(q_ref, k_ref, v_ref, g_ref, b_ref, t_ref, du_ref, dw_ref, dqe_ref, dkd_ref, dqk_ref, degl_ref,
             dqkv_ref, dg_ref, db_ref):
        rows = PREP_CHUNKS * CHUNK
        grp = (PREP_GROUPS, GROUP_ROWS, LANES)
        t_saved = t_ref[0].reshape(PREP_GROUPS, GROUP_ROWS, GROUP_ROWS)
        _, vjp = jax.vjp(lambda *x: _dn_group(*x, t_saved=t_saved)[:6], q_ref[...].reshape(grp),
                         k_ref[...].reshape(grp), v_ref[...].reshape(grp), g_ref[...].reshape(grp),
                         b_ref[...].reshape(grp))
        dq, dk, dv, dg, db = vjp((du_ref[...].reshape(grp), dw_ref[...].reshape(grp), dqe_ref[...].reshape(grp),
                                  dkd_ref[...].reshape(grp), dqk_ref[0].reshape(PREP_GROUPS, GROUP_ROWS, CHUNK),
                                  degl_ref[0].reshape(PREP_GROUPS, GROUP, 1, LANES)))
        dqkv_ref[0] = dq.reshape(rows, LANES)
        dqkv_ref[1] = dk.reshape(rows, LANES)
        dqkv_ref[2] = dv.reshape(rows, LANES)
        dg_ref[...] = dg.reshape(rows, LANES)
        db_ref[...] = db.reshape(rows, LANES)

    wide = jax.ShapeDtypeStruct((S, DN_WIDTH), F32)
    rows = PREP_CHUNKS * CHUNK
    return pl.pallas_call(
        body, grid=(DN_HEADS, nc // PREP_CHUNKS), in_specs=[q, k, v, q, q, _dn_tinv_spec(), q, q, q, q, qks, egl],
        out_specs=[pl.BlockSpec((3, rows, LANES), lambda h, c: (0, c, h)), q, q],
        out_shape=[jax.ShapeDtypeStruct((3, S, DN_WIDTH), F32), wide, wide],
        name="dn_prep_bwd", compiler_params=_params(("parallel", "parallel")),
    )(qkvn, qkvn, qkvn, g, beta, tinv, du, dw, dqe, dkd, dqk, degl)


SCAN_CHUNKS = 16


def _dn_scan_specs(nc, reverse):
    nb = nc // SCAN_CHUNKS

    def cidx(c):
        return nb - 1 - c if reverse else c

    hc = pl.BlockSpec((SCAN_CHUNKS * CHUNK, DN_WIDTH), lambda c: (cidx(c), 0))
    qk = pl.BlockSpec((DN_HEADS, SCAN_CHUNKS * CHUNK, CHUNK), lambda c: (0, cidx(c), 0))
    egl = pl.BlockSpec((DN_HEADS, SCAN_CHUNKS, 1, LANES), lambda c: (0, cidx(c), 0, 0))
    st = pl.BlockSpec((DN_HEADS, SCAN_CHUNKS, DN_DIM, DN_DIM), lambda c: (0, cidx(c), 0, 0))
    return hc, qk, egl, st


def _heads(ref, i):
    return jnp.stack([ref[pl.ds(i * CHUNK, CHUNK), pl.ds(h * DN_DIM, DN_DIM)] for h in range(DN_HEADS)])


def _dn_scan_fwd(u, w, qe, kd, qk, egl):
    S = u.shape[0]
    nc = S // CHUNK
    hc, qks, egls, st = _dn_scan_specs(nc, False)

    def body(u_ref, w_ref, qe_ref, kd_ref, qk_ref, egl_ref, o_ref, st_ref, s_scr):
        @pl.when(pl.program_id(0) == 0)
        def _():
            s_scr[...] = jnp.zeros_like(s_scr)

        s = s_scr[...]
        for i in range(SCAN_CHUNKS):
            rows = pl.ds(i * CHUNK, CHUNK)
            st_ref[:, i] = s
            s, o = _dn_step(s, _heads(u_ref, i), _heads(w_ref, i), _heads(qe_ref, i), _heads(kd_ref, i),
                            qk_ref[:, rows, :], egl_ref[:, i])
            for h in range(DN_HEADS):
                o_ref[rows, pl.ds(h * DN_DIM, DN_DIM)] = o[h]
        s_scr[...] = s

    return pl.pallas_call(
        body, grid=(nc // SCAN_CHUNKS,), in_specs=[hc, hc, hc, hc, qks, egls], out_specs=[hc, st],
        out_shape=[jax.ShapeDtypeStruct((S, DN_WIDTH), F32), jax.ShapeDtypeStruct((DN_HEADS, nc, DN_DIM, DN_DIM), F32)],
        scratch_shapes=[pltpu.VMEM((DN_HEADS, DN_DIM, DN_DIM), F32)], name="dn_scan_fwd",
        compiler_params=_params(("arbitrary",)))(u, w, qe, kd, qk, egl)


def _dn_scan_bwd(u, w, qe, kd, qk, egl, states, do):
    S = u.shape[0]
    nc = S // CHUNK
    hc, qks, egls, st = _dn_scan_specs(nc, True)

    def body(u_ref, w_ref, qe_ref, kd_ref, qk_ref, egl_ref, st_ref, do_ref,
             du_ref, dw_ref, dqe_ref, dkd_ref, dqk_ref, degl_ref, ds_scr):
        @pl.when(pl.program_id(0) == 0)
        def _():
            ds_scr[...] = jnp.zeros_like(ds_scr)

        ds = ds_scr[...]
        for i in reversed(range(SCAN_CHUNKS)):
            rows = pl.ds(i * CHUNK, CHUNK)
            _, vjp = jax.vjp(_dn_step, st_ref[:, i], _heads(u_ref, i), _heads(w_ref, i), _heads(qe_ref, i),
                             _heads(kd_ref, i), qk_ref[:, rows, :], egl_ref[:, i])
            ds, du, dw, dqe, dkd, dqk, degl = vjp((ds, _heads(do_ref, i)))
            dqk_ref[:, rows, :] = dqk
            degl_ref[:, i] = degl
            for h in range(DN_HEADS):
                cols = pl.ds(h * DN_DIM, DN_DIM)
                du_ref[rows, cols] = du[h]
                dw_ref[rows, cols] = dw[h]
                dqe_ref[rows, cols] = dqe[h]
                dkd_ref[rows, cols] = dkd[h]
        ds_scr[...] = ds

    wide = jax.ShapeDtypeStruct((S, DN_WIDTH), F32)
    return pl.pallas_call(
        body, grid=(nc // SCAN_CHUNKS,), in_specs=[hc, hc, hc, hc, qks, egls, st, hc],
        out_specs=[hc, hc, hc, hc, qks, egls],
        out_shape=[wide, wide, wide, wide, jax.ShapeDtypeStruct((DN_HEADS, S, CHUNK), F32),
                   jax.ShapeDtypeStruct((DN_HEADS, nc, 1, LANES), F32)],
        scratch_shapes=[pltpu.VMEM((DN_HEADS, DN_DIM, DN_DIM), F32)], name="dn_scan_bwd",
        compiler_params=_params(("arbitrary",)))(u, w, qe, kd, qk, egl, states, do)


def _dn_out_fwd(o, proj, gain, tm=1024):
    S = o.shape[0]

    def body(o_ref, z_ref, g_ref, y_ref):
        y_ref[...] = _dn_out(o_ref[...], z_ref[...], g_ref[...]).astype(BF16)

    hs = pl.BlockSpec((tm, LANES), lambda i, h: (i, h))
    zs = pl.BlockSpec((tm, LANES), lambda i, h: (i, P_Z // LANES + h))
    return pl.pallas_call(
        body, grid=(S // tm, DN_HEADS), in_specs=[hs, zs, _full((1, DN_DIM))], out_specs=hs,
        out_shape=jax.ShapeDtypeStruct((S, DN_WIDTH), BF16), name="dn_out_fwd",
        compiler_params=_params(("parallel", "parallel")))(o, proj, gain)


_ANY = pl.BlockSpec(memory_space=pl.ANY)


def _dn_out_bwd(o, proj, gain, dy, dproj, tm=1024):
    S = o.shape[0]

    def body(o_ref, z_ref, g_ref, dy_ref, _, do_ref, dz_ref, dg_ref):
        _, vjp = jax.vjp(_dn_out, o_ref[...], z_ref[...], g_ref[...])
        do, dz, dg = vjp(dy_ref[...])
        do_ref[...] = do
        dz_ref[...] = dz.astype(BF16)

        @pl.when((pl.program_id(0) == 0) & (pl.program_id(1) == 0))
        def _():
            dg_ref[...] = jnp.zeros_like(dg_ref)

        dg_ref[...] += dg

    hs = pl.BlockSpec((tm, LANES), lambda i, h: (i, h))
    zs = pl.BlockSpec((tm, LANES), lambda i, h: (i, P_Z // LANES + h))
    return pl.pallas_call(
        body, grid=(S // tm, DN_HEADS), in_specs=[hs, zs, _full((1, DN_DIM)), hs, _ANY],
        out_specs=[hs, zs, _full((1, DN_DIM))],
        out_shape=[jax.ShapeDtypeStruct((S, DN_WIDTH), F32), jax.ShapeDtypeStruct(dproj.shape, dproj.dtype),
                   jax.ShapeDtypeStruct((1, DN_DIM), F32)],
        input_output_aliases={4: 1},
        name="dn_out_bwd", compiler_params=_params(("arbitrary", "arbitrary")))(o, proj, gain, dy, dproj)


def _rel_buckets():
    qi = np.arange(BLOCK)[:, None]
    kj = np.arange(2 * BLOCK)[None, :]
    n = np.maximum(BLOCK + qi - kj, 0)
    max_exact = REL_BUCKETS // 2
    nf = np.maximum(n, 1).astype(np.float32)
    large = max_exact + (np.log(nf / np.float32(max_exact)) / np.float32(math.log(REL_MAX_DIST / max_exact))
                         * np.float32(REL_BUCKETS - max_exact)).astype(np.int32)
    large = np.minimum(large, REL_BUCKETS - 1)
    return np.where(n < max_exact, n, large).astype(np.int32)


def _bias_fwd(rel_bias):
    buckets = jnp.asarray(_rel_buckets())

    def body(rb_ref, bk_ref, o_ref):
        bk = bk_ref[...]
        for h in range(SWA_HEADS):
            acc = jnp.zeros((BLOCK, 2 * BLOCK), F32)
            for b in range(REL_BUCKETS):
                acc = jnp.where(bk == b, rb_ref[b, h], acc)
            for first in range(2):
                o_ref[first, h] = jnp.where(_swa_mask(1 - first), acc, -jnp.inf)

    return pl.pallas_call(
        body, in_specs=[pl.BlockSpec(memory_space=pltpu.SMEM), pl.BlockSpec(memory_space=pltpu.VMEM)],
        out_specs=pl.BlockSpec(memory_space=pltpu.VMEM),
        out_shape=jax.ShapeDtypeStruct((2, SWA_HEADS, BLOCK, 2 * BLOCK), F32), name="swa_bias_fwd",
        compiler_params=_params())(rel_bias, buckets)


def _bias_bwd(dbias):
    buckets = jnp.asarray(_rel_buckets())

    def body(d_ref, bk_ref, o_ref):
        bk = bk_ref[...]
        lane = lax.broadcasted_iota(jnp.int32, (1, LANES), 1)
        for h in range(SWA_HEADS):
            d = d_ref[h]
            row = jnp.zeros((1, LANES), F32)
            for b in range(REL_BUCKETS):
                part = jnp.sum(jnp.where(bk == b, d, 0.0), axis=1, keepdims=True)
                row = jnp.where(lane == b, jnp.sum(part, axis=0, keepdims=True), row)
            o_ref[h:h + 1, :] = row

    return pl.pallas_call(
        body, in_specs=[pl.BlockSpec(memory_space=pltpu.VMEM), pl.BlockSpec(memory_space=pltpu.VMEM)],
        out_specs=pl.BlockSpec(memory_space=pltpu.VMEM),
        out_shape=jax.ShapeDtypeStruct((SWA_HEADS, LANES), F32), name="swa_bias_bwd",
        compiler_params=_params())(dbias, buckets)


def _swa_mask(n):
    qi = lax.broadcasted_iota(jnp.int32, (BLOCK, 2 * BLOCK), 0)
    kj = lax.broadcasted_iota(jnp.int32, (BLOCK, 2 * BLOCK), 1)
    dist = BLOCK + qi - kj
    return (dist >= 0) & (dist < WINDOW) & ((n > 0) | (kj >= BLOCK))


def _swa_in_specs():
    q = pl.BlockSpec((BLOCK, SWA_WIDTH), lambda n: (n, P_SQ // SWA_WIDTH))
    kc = pl.BlockSpec((BLOCK, SWA_KVW), lambda n: (n, P_SK // SWA_KVW))
    kp = pl.BlockSpec((BLOCK, SWA_KVW), lambda n: (jnp.maximum(n - 1, 0), P_SK // SWA_KVW))
    vc = pl.BlockSpec((BLOCK, SWA_KVW), lambda n: (n, P_SV // SWA_KVW))
    vp = pl.BlockSpec((BLOCK, SWA_KVW), lambda n: (jnp.maximum(n - 1, 0), P_SV // SWA_KVW))
    band = pl.BlockSpec((None, SWA_HEADS, BLOCK, 2 * BLOCK), lambda n: (jnp.where(n == 0, 1, 0), 0, 0, 0))
    small = [_full((1, SWA_DIM)), _full((1, SWA_DIM)), _full((1, SWA_HEADS)), band]
    return [q, kp, kc, vp, vc] + small


def _swa_load(q_ref, kp_ref, kc_ref, vp_ref, vc_ref, s_ref):
    q = jnp.stack([q_ref[:, pl.ds(h * SWA_DIM, SWA_DIM)] for h in range(SWA_HEADS)])
    kbands, vbands = [], []
    for kv in range(SWA_KV):
        cols = pl.ds(kv * SWA_DIM, SWA_DIM)
        kbands += [jnp.concatenate([kp_ref[:, cols], kc_ref[:, cols]], axis=0)] * SWA_GROUP
        vbands += [jnp.concatenate([vp_ref[:, cols], vc_ref[:, cols]], axis=0)] * SWA_GROUP
    sinks = jnp.stack([s_ref[:, pl.ds(h, 1)] for h in range(SWA_HEADS)])
    return q, jnp.stack(kbands), jnp.stack(vbands), sinks


def _swa_fwd(proj, q_gain, k_gain, sinks, bias):
    S = proj.shape[0]

    def body(q_ref, kp_ref, kc_ref, vp_ref, vc_ref, qg_ref, kg_ref, s_ref, bias_ref, y_ref):
        q, kband, vband, sk = _swa_load(q_ref, kp_ref, kc_ref, vp_ref, vc_ref, s_ref)
        out = _swa_block(q, kband, vband, qg_ref[...], kg_ref[...], sk, bias_ref[...])
        for h in range(SWA_HEADS):
            y_ref[:, pl.ds(h * SWA_DIM, SWA_DIM)] = out[h].astype(BF16)

    return pl.pallas_call(
        body, grid=(S // BLOCK,), in_specs=_swa_in_specs(),
        out_specs=pl.BlockSpec((BLOCK, SWA_WIDTH), lambda n: (n, 0)),
        out_shape=jax.ShapeDtypeStruct((S, SWA_WIDTH), BF16), name="swa_fwd",
        compiler_params=_params(("parallel",)))(proj, proj, proj, proj, proj, q_gain, k_gain, sinks, bias)


def _swa_bwd(proj, q_gain, k_gain, sinks, bias, dy, dproj):
    S = proj.shape[0]

    def body(q_ref, kp_ref, kc_ref, vp_ref, vc_ref, qg_ref, kg_ref, s_ref, bias_ref, dy_ref, _,
             dq_ref, dk_ref, dv_ref, dqg_ref, dkg_ref, ds_ref, dbias_ref):
        n = pl.program_id(0)

        @pl.when(n == 0)
        def _():
            for r in (dk_ref, dv_ref, dqg_ref, dkg_ref, ds_ref, dbias_ref):
                r[...] = jnp.zeros_like(r)

        cur = pl.ds(pl.multiple_of(n * BLOCK, BLOCK), BLOCK)
        prev = pl.ds(pl.multiple_of(jnp.maximum(n - 1, 0) * BLOCK, BLOCK), BLOCK)
        q, kband, vband, sk = _swa_load(q_ref, kp_ref, kc_ref, vp_ref, vc_ref, s_ref)
        _, vjp = jax.vjp(_swa_block, q, kband, vband, qg_ref[...], kg_ref[...], sk, bias_ref[...])
        dy = jnp.stack([dy_ref[:, pl.ds(h * SWA_DIM, SWA_DIM)] for h in range(SWA_HEADS)])
        dq, dkb, dvb, dqg, dkg, dsk, dbs = vjp(dy)
        for h in range(SWA_HEADS):
            dq_ref[:, pl.ds(h * SWA_DIM, SWA_DIM)] = dq[h].astype(BF16)
            ds_ref[:, pl.ds(h, 1)] += dsk[h]
        dbias_ref[...] += dbs
        dqg_ref[...] += dqg
        dkg_ref[...] += dkg
        for kv in range(SWA_KV):
            cols = pl.ds(kv * SWA_DIM, SWA_DIM)
            group = range(kv * SWA_GROUP, (kv + 1) * SWA_GROUP)
            dk_kv = sum(dkb[h] for h in group)
            dv_kv = sum(dvb[h] for h in group)
            dk_ref[cur, cols] += dk_kv[BLOCK:]
            dv_ref[cur, cols] += dv_kv[BLOCK:]

            @pl.when(n > 0)
            def _(cols=cols, dk_kv=dk_kv, dv_kv=dv_kv):
                dk_ref[prev, cols] += dk_kv[:BLOCK]
                dv_ref[prev, cols] += dv_kv[:BLOCK]

    return pl.pallas_call(
        body, grid=(S // BLOCK,),
        in_specs=_swa_in_specs() + [pl.BlockSpec((BLOCK, SWA_WIDTH), lambda n: (n, 0)),
                                    pl.BlockSpec(memory_space=pl.ANY)],
        out_specs=[pl.BlockSpec((BLOCK, SWA_WIDTH), lambda n: (n, P_SQ // SWA_WIDTH)), _full((S, SWA_KVW)),
                   _full((S, SWA_KVW)), _full((1, SWA_DIM)), _full((1, SWA_DIM)), _full((1, SWA_HEADS)),
                   _full((SWA_HEADS, BLOCK, 2 * BLOCK))],
        out_shape=[jax.ShapeDtypeStruct(dproj.shape, dproj.dtype), jax.ShapeDtypeStruct((S, SWA_KVW), F32),
                   jax.ShapeDtypeStruct((S, SWA_KVW), F32), jax.ShapeDtypeStruct((1, SWA_DIM), F32),
                   jax.ShapeDtypeStruct((1, SWA_DIM), F32), jax.ShapeDtypeStruct((1, SWA_HEADS), F32),
                   jax.ShapeDtypeStruct((SWA_HEADS, BLOCK, 2 * BLOCK), F32)],
        input_output_aliases={10: 0},
        name="swa_bwd", compiler_params=_params(("arbitrary",)),
    )(proj, proj, proj, proj, proj, q_gain, k_gain, sinks, bias, dy, dproj)


def _kv_into(dproj, dk, dv, tm=1024):
    S = dk.shape[0]

    def body(dk_ref, dv_ref, _, o_ref):
        o_ref[:, :SWA_KVW] = dk_ref[...].astype(BF16)
        o_ref[:, SWA_KVW:] = dv_ref[...].astype(BF16)

    return pl.pallas_call(
        body, grid=(S // tm,), in_specs=[_row(tm, SWA_KVW), _row(tm, SWA_KVW), pl.BlockSpec(memory_space=pl.ANY)],
        out_specs=_row(tm, 2 * SWA_KVW, P_SK // (2 * SWA_KVW)),
        out_shape=jax.ShapeDtypeStruct(dproj.shape, dproj.dtype), input_output_aliases={2: 0},
        name="swa_kv_into", compiler_params=_params(("parallel",)))(dk, dv, dproj)


def _position():
    return lax.axis_index("x"), lax.axis_index("y"), lax.axis_index("c")


_HBM = pl.BlockSpec(memory_space=pltpu.HBM)
_SEM = pl.BlockSpec(memory_space=pltpu.SEMAPHORE)
_DATAFLOW = pltpu.SideEffectType.DATAFLOW_SIDE_EFFECTING


def _two_level_copies(x_refs, out_refs, send_sems, recv_sems):
    x, y, c = _position()
    me, sibling = (x, y, c), (x, y, 1 - c)
    chips = [(1 - x, y), (x, 1 - y), (1 - x, 1 - y)]

    def copy(a, k, block, to, own=False):
        px, py, pc = block
        slot = out_refs[a].at[4 * px + 2 * py + pc]
        return pltpu.make_async_remote_copy(
            src_ref=x_refs[a] if own else slot, dst_ref=slot, send_sem=send_sems.at[7 * a + k],
            recv_sem=recv_sems.at[7 * a + k], device_id=to, device_id_type=MESH_ID)

    return copy, me, sibling, chips


def _all_gather_start(shards, name):
    na = len(shards)
    lands = [lax.empty((N_DEV,) + s.shape, s.dtype) for s in shards]

    def body(*refs):
        x_refs, out_refs = refs[:na], refs[na:2 * na]
        send_sems, recv_sems = refs[2 * na], refs[2 * na + 1]
        token = refs[-1]
        copy, me, sibling, chips = _two_level_copies(x_refs, out_refs, send_sems, recv_sems)
        for a in range(na):
            copy(a, 0, me, sibling, own=True).start()
            for j, chip in enumerate(chips):
                copy(a, 1 + j, me, (*chip, me[2]), own=True).start()
        token[...] = jnp.zeros_like(token)

    hbm = lambda a: pltpu.HBM(a.shape, a.dtype)
    out = pl.pallas_call(
        body, name=name,
        out_shape=(pltpu.SemaphoreType.DMA((7 * na,)), pltpu.SemaphoreType.DMA((7 * na,)),
                   *[hbm(s) for s in shards], *[hbm(l) for l in lands], jax.ShapeDtypeStruct((8, LANES), F32)),
        in_specs=[_HBM] * (2 * na),
        out_specs=(_SEM, _SEM, *[_HBM] * (2 * na), pl.BlockSpec(memory_space=pltpu.VMEM)),
        input_output_aliases={i: 2 + i for i in range(2 * na)},
        compiler_params=pltpu.CompilerParams(has_side_effects=_DATAFLOW),
    )(*[pltpu.with_memory_space_constraint(s, pltpu.HBM) for s in shards],
      *[pltpu.with_memory_space_constraint(l, pltpu.HBM) for l in lands])
    return (out[0], out[1], list(out[2:2 + na]), list(out[2 + na:2 + 2 * na])), out[-1]


def _all_gather_finish(handle, after, name):
    send_sems, recv_sems, srcs, lands = handle
    na = len(srcs)

    def body(*refs):
        x_refs, out_refs = refs[:na], refs[na:2 * na]
        copy, me, sibling, chips = _two_level_copies(x_refs, out_refs, refs[2 * na], refs[2 * na + 1])
        c = me[2]
        for a in range(na):
            copy(a, 0, sibling, me).wait_recv()
            copy(a, 0, me, sibling, own=True).wait_send()
            for j, chip in enumerate(chips):
                copy(a, 1 + j, (*chip, c), me).wait_recv()
                copy(a, 1 + j, me, (*chip, c), own=True).wait_send()

    hbm = lambda a: pltpu.HBM(a.shape, a.dtype)
    out = pl.pallas_call(
        body, name=name, out_shape=(*[hbm(s) for s in srcs], *[hbm(l) for l in lands]),
        in_specs=[_HBM] * (2 * na) + [_SEM, _SEM] + [pl.BlockSpec(memory_space=pl.ANY)] * len(after),
        out_specs=tuple([_HBM] * (2 * na)), input_output_aliases={i: i for i in range(2 * na)},
        compiler_params=pltpu.CompilerParams(has_side_effects=_DATAFLOW),
    )(*srcs, *lands, send_sems, recv_sems, *after)
    lands = _all_gather_relay(list(out[na:]), name + "_relay")
    return [_own_slot(land, src) for land, src in zip(lands, out[:na])]


def _all_gather_relay(lands, name):
    na = len(lands)

    def body(*refs):
        out_refs = refs[na:2 * na]
        send_sems, recv_sems = refs[2 * na:]
        x, y, c = _position()
        chips = [(1 - x, y), (x, 1 - y), (1 - x, 1 - y)]

        def copy(a, j, core):
            px, py = chips[j]
            slot = out_refs[a].at[4 * px + 2 * py + core]
            return pltpu.make_async_remote_copy(
                src_ref=slot, dst_ref=slot, send_sem=send_sems.at[3 * a + j], recv_sem=recv_sems.at[3 * a + j],
                device_id=(x, y, 1 - c), device_id_type=MESH_ID)

        sends = [copy(a, j, c) for a in range(na) for j in range(3)]
        for cp in sends:
            cp.start()
        for a in range(na):
            for j in range(3):
                copy(a, j, 1 - c).wait_recv()
        for cp in sends:
            cp.wait_send()

    return pl.pallas_call(
        body, in_specs=[pl.BlockSpec(memory_space=pl.ANY)] * na, out_specs=[pl.BlockSpec(memory_space=pl.ANY)] * na,
        out_shape=[jax.ShapeDtypeStruct(l.shape, l.dtype) for l in lands],
        input_output_aliases={i: i for i in range(na)},
        scratch_shapes=[pltpu.SemaphoreType.DMA((3 * na,)), pltpu.SemaphoreType.DMA((3 * na,))],
        name=name)(*lands)


def _peers(x, y, c):
    out = []
    for k in range(1, N_DEV):
        px, py, pc = x ^ (k >> 2), y ^ ((k >> 1) & 1), c ^ (k & 1)
        out.append(((px, py, pc), 4 * px + 2 * py + pc))
    return out


def _split_copies(src_refs, land_refs, send_sems, recv_sems, scatter):
    x, y, c = _position()
    me = 4 * x + 2 * y + c
    sends, recvs = [], []
    for k, (peer_id, peer) in enumerate(_peers(x, y, c)):
        for a, (src, land) in enumerate(zip(src_refs, land_refs)):
            sems = dict(send_sem=send_sems.at[7 * a + k], recv_sem=recv_sems.at[7 * a + k],
                        device_id=peer_id, device_id_type=MESH_ID)
            mine = src.at[peer] if scatter else src
            sends.append(pltpu.make_async_remote_copy(src_ref=mine, dst_ref=land.at[me], **sems))
            recvs.append(pltpu.make_async_remote_copy(src_ref=mine, dst_ref=land.at[peer], **sems))
    return sends, recvs


def _all_gather_direct(shards, name, after):
    na, nb = len(shards), len(after)

    def body(*refs):
        x_refs, out_refs = refs[:na], refs[na + nb:2 * na + nb]
        send_sems, recv_sems, local_sems = refs[2 * na + nb:]
        x, y, c = _position()
        me = 4 * x + 2 * y + c
        local = [pltpu.make_async_copy(x_refs[a], out_refs[a].at[me], local_sems.at[a]) for a in range(na)]
        sends, recvs = _split_copies(x_refs, out_refs, send_sems, recv_sems, False)
        for cp in local + sends:
            cp.start()
        for cp in recvs:
            cp.wait_recv()
        for cp in sends:
            cp.wait_send()
        for cp in local:
            cp.wait()

    return pl.pallas_call(
        body, in_specs=[pl.BlockSpec(memory_space=pl.ANY)] * (na + nb),
        out_specs=[pl.BlockSpec(memory_space=pl.ANY)] * na,
        out_shape=[jax.ShapeDtypeStruct((N_DEV,) + s.shape, s.dtype) for s in shards],
        scratch_shapes=[pltpu.SemaphoreType.DMA((7 * na,)), pltpu.SemaphoreType.DMA((7 * na,)),
                        pltpu.SemaphoreType.DMA((na,))],
        name=name)(*shards, *after)


def _exchange_start(srcs, scatter, name, after=None):
    na = len(srcs)
    lands = [lax.empty(s.shape if scatter else (N_DEV,) + s.shape, s.dtype) for s in srcs]
    extra = [] if after is None else [after]

    def body(*refs):
        src_refs, land_refs = refs[:na], refs[na:2 * na]
        send_sems, recv_sems = refs[2 * na + len(extra)], refs[2 * na + len(extra) + 1]
        token = refs[-1]
        sends, _ = _split_copies(src_refs, land_refs, send_sems, recv_sems, scatter)
        for cp in sends:
            cp.start()
        token[...] = jnp.zeros_like(token)

    hbm = lambda a: pltpu.HBM(a.shape, a.dtype)
    out = pl.pallas_call(
        body, name=name,
        out_shape=(pltpu.SemaphoreType.DMA((7 * na,)), pltpu.SemaphoreType.DMA((7 * na,)),
                   *[hbm(s) for s in srcs], *[hbm(l) for l in lands], jax.ShapeDtypeStruct((8, LANES), F32)),
        in_specs=[_HBM] * (2 * na) + [pl.BlockSpec(memory_space=pl.ANY)] * len(extra),
        out_specs=(_SEM, _SEM, *[_HBM] * (2 * na), pl.BlockSpec(memory_space=pltpu.VMEM)),
        input_output_aliases={i: 2 + i for i in range(2 * na)},
        compiler_params=pltpu.CompilerParams(has_side_effects=_DATAFLOW),
    )(*[pltpu.with_memory_space_constraint(s, pltpu.HBM) for s in srcs],
      *[pltpu.with_memory_space_constraint(l, pltpu.HBM) for l in lands], *extra)
    return (out[0], out[1], list(out[2:2 + na]), list(out[2 + na:2 + 2 * na])), out[-1]


def _exchange_wait(handle, after, scatter, name):
    send_sems, recv_sems, srcs, lands = handle
    na = len(srcs)

    def body(*refs):
        src_refs, land_refs = refs[:na], refs[na:2 * na]
        s_sems, r_sems = refs[2 * na], refs[2 * na + 1]
        sends, recvs = _split_copies(src_refs, land_refs, s_sems, r_sems, scatter)
        for cp in sends:
            cp.wait_send()
        for cp in recvs:
            cp.wait_recv()

    hbm = lambda a: pltpu.HBM(a.shape, a.dtype)
    out = pl.pallas_call(
        body, name=name, out_shape=(*[hbm(s) for s in srcs], *[hbm(l) for l in lands]),
        in_specs=[_HBM] * (2 * na) + [_SEM, _SEM, pl.BlockSpec(memory_space=pl.ANY)],
        out_specs=tuple([_HBM] * (2 * na)), input_output_aliases={i: i for i in range(2 * na)},
        compiler_params=pltpu.CompilerParams(has_side_effects=_DATAFLOW),
    )(*srcs, *lands, send_sems, recv_sems, after)
    return list(out[:na]), list(out[na:])


def _own_slot(landed, own):
    me = 4 * lax.axis_index("x") + 2 * lax.axis_index("y") + lax.axis_index("c")
    return lax.dynamic_update_slice_in_dim(landed, own[None], me, axis=0)


def _adam_update(parts, w, m, v, name, tr=256, turned=False):
    _, r, c = w.shape
    tr = _pick_rows(r, tr)
    cp = parts.shape[2]
    flat = turned and c % 8 != 0
    at = (slice(None), 0) if flat else (0,)

    def body(p_ref, w_ref, m_ref, v_ref, g_ref, d_ref, nm_ref, nv_ref):
        cols = pl.ds(0, cp if turned else c)
        g = p_ref[0, :, cols].astype(F32)
        for i in range(1, N_DEV):
            g = g + p_ref[i, :, cols].astype(F32)
        if turned:
            g = g.T[:c]
        delta, nm, nv = _adamw(w_ref[at], g, m_ref[at], v_ref[at])
        g_ref[at] = g
        d_ref[at] = delta
        nm_ref[at] = nm
        nv_ref[at] = nv

    there, back = ((2, 0, 1), (1, 2, 0)) if flat else ((0, 2, 1), (0, 2, 1))
    if turned:
        w, m, v = (jnp.transpose(a, there) for a in (w, m, v))
        rs = pl.BlockSpec((c, 1, tr), lambda i: (0, 0, i)) if flat else pl.BlockSpec((1, c, tr), lambda i: (0, 0, i))
    else:
        rs = pl.BlockSpec((1, tr, c), lambda i: (0, i, 0))
    outs = pl.pallas_call(
        body, grid=(r // tr,), in_specs=[pl.BlockSpec((N_DEV, tr, cp), lambda i: (0, i, 0)), rs, rs, rs],
        out_specs=[rs] * 4, out_shape=[jax.ShapeDtypeStruct(w.shape, F32)] * 4, name=name,
        compiler_params=_params(("parallel",)))(parts, w, m, v)
    return [*([jnp.transpose(o, back) for o in outs] if turned else outs), outs[0]]


def _pick_rows(rows, target):
    if rows <= target:
        return rows
    t = target
    while t >= 16:
        if rows % t == 0:
            return t
        t -= 16
    return rows


BIG = ("w_in", "w_branch_dn", "w_branch_swa", "w_out", "w_gate", "w_up", "w_down")
IN_SHARD, IN_WIRE = D_IN // N_DEV, 640
FF_SHARD, FF_WIRE = D_FF // N_DEV, 384
D_FFP = N_DEV * FF_WIRE
BIG_SHAPES = {"w_in": ((D_MODEL, IN_SHARD), (D_MODEL, IN_WIRE)),
              "w_branch_dn": ((DN_WIDTH, LANES), (DN_WIDTH, LANES)),
              "w_branch_swa": ((SWA_WIDTH, LANES), (SWA_WIDTH, LANES)),
              "w_out": ((LANES, D_MODEL), (LANES, D_MODEL)),
              "w_gate": ((D_MODEL, FF_SHARD), (D_MODEL, FF_WIRE)),
              "w_up": ((D_MODEL, FF_SHARD), (D_MODEL, FF_WIRE)),
              "w_down": ((FF_SHARD, D_MODEL), (FF_WIRE, D_MODEL))}
CONV_SHARD, CONV_WIRE = (DN_CONV, DN_QKV // N_DEV), (8, 256)


def _pad_to(a, shape):
    return jnp.pad(a, [(0, t - s) for s, t in zip(a.shape, shape)])


IN_TILE_ROWS = 256
_IN_SEGS = ((R_GATE, 2048, P_GATE), (R_QKV, DN_QKV, P_QKV), (R_Z, DN_WIDTH, P_Z), (R_SQ, SWA_WIDTH, P_SQ),
            (R_SK, SWA_KVW, P_SK), (R_SV, SWA_KVW, P_SV), (R_B, 8, P_BA))


def _w_in_from_blocks(blocks, after):
    tm = IN_TILE_ROWS

    def body(b_ref, _, o_ref):
        parts = []
        for rs, n, _ in _IN_SEGS:
            for dev in range(N_DEV):
                lo, hi = max(rs, IN_SHARD * dev), min(rs + n, IN_SHARD * (dev + 1))
                if lo < hi:
                    parts.append(b_ref[dev][:, lo - IN_SHARD * dev:hi - IN_SHARD * dev])
        parts.append(jnp.zeros((tm, P_WIDTH - P_BA - 8), b_ref.dtype))
        o_ref[...] = jnp.concatenate(parts, axis=1)

    return pl.pallas_call(
        body, grid=(D_MODEL // tm,),
        in_specs=[pl.BlockSpec((N_DEV, tm, IN_WIRE), lambda i: (0, i, 0)), pl.BlockSpec(memory_space=pl.ANY)],
        out_specs=pl.BlockSpec((tm, P_WIDTH), lambda i: (i, 0)),
        out_shape=jax.ShapeDtypeStruct((D_MODEL, P_WIDTH), blocks.dtype), name="w_in_from_blocks",
        compiler_params=_params(("parallel",)))(blocks, after)


def _w_in_to_blocks(g):
    tm = IN_TILE_ROWS

    def body(g_ref, o_ref):
        for dev in range(N_DEV):
            parts = []
            for rs, n, ps in sorted(_IN_SEGS):
                lo, hi = max(rs, IN_SHARD * dev), min(rs + n, IN_SHARD * (dev + 1))
                if lo < hi:
                    parts.append(g_ref[:, ps + lo - rs:ps + hi - rs])
            parts.append(jnp.zeros((tm, IN_WIRE - IN_SHARD), g_ref.dtype))
            o_ref[dev] = jnp.concatenate(parts, axis=1)

    return pl.pallas_call(
        body, grid=(D_MODEL // tm,), in_specs=[pl.BlockSpec((tm, P_WIDTH), lambda i: (i, 0))],
        out_specs=pl.BlockSpec((N_DEV, tm, IN_WIRE), lambda i: (0, i, 0)),
        out_shape=jax.ShapeDtypeStruct((N_DEV, D_MODEL, IN_WIRE), g.dtype), name="w_in_to_blocks",
        compiler_params=_params(("parallel",)))(g)


SMALL = {"attn_norm": (0, (1, D_MODEL)), "ffn_norm": (1, (1, D_MODEL)), "dn_out_norm": (2, (1, DN_DIM)),
         "swa_q_norm": (3, (1, SWA_DIM)), "swa_k_norm": (4, (1, SWA_DIM)), "dn_a_log": (5, (1, DN_HEADS)),
         "dn_dt_bias": (6, (1, DN_HEADS)), "swa_sinks": (7, (1, SWA_HEADS)), "rel_bias": (8, (REL_BUCKETS, SWA_HEADS))}
SMALL_SHEET = (48, D_MODEL)


LOSS_ROW = 40


def _small_pack(grads, loss_local):
    names = list(SMALL)

    def body(*refs):
        o_ref = refs[-1]
        o_ref[...] = jnp.zeros_like(o_ref)
        for n, ref in zip(names, refs):
            r0, (nr, nc) = SMALL[n]
            o_ref[r0:r0 + nr, 0:nc] = ref[...]
        o_ref[LOSS_ROW:LOSS_ROW + 1, 0:1] = refs[len(names)][...]

    return pl.pallas_call(
        body, in_specs=[pl.BlockSpec(memory_space=pltpu.VMEM)] * (len(names) + 1),
        out_specs=pl.BlockSpec(memory_space=pltpu.VMEM), out_shape=jax.ShapeDtypeStruct(SMALL_SHEET, F32),
        name="small_pack", compiler_params=_params())(*[grads[n].reshape(SMALL[n][1]) for n in names], loss_local)


def _small_update(sheets, w, m, v):
    names = list(SMALL)
    k = len(names)

    def body(*refs):
        p_ref = refs[0]
        ins, outs = refs[1:1 + 3 * k], refs[1 + 3 * k:]
        loss = p_ref[0, LOSS_ROW:LOSS_ROW + 1, 0:1]
        for i in range(1, N_DEV):
            loss = loss + p_ref[i, LOSS_ROW:LOSS_ROW + 1, 0:1]
        outs[4 * k][...] = loss
        for t, n in enumerate(names):
            r0, (nr, nc) = SMALL[n]
            g = p_ref[0, r0:r0 + nr, 0:nc]
            for i in range(1, N_DEV):
                g = g + p_ref[i, r0:r0 + nr, 0:nc]
            delta, nm, nv = _adamw(ins[t][...], g, ins[k + t][...], ins[2 * k + t][...])
            for kind, val in enumerate((g, delta, nm, nv)):
                outs[kind * k + t][...] = val

    shapes = [jax.ShapeDtypeStruct(SMALL[n][1], F32) for n in names]
    vm = pl.BlockSpec(memory_space=pltpu.VMEM)
    res = pl.pallas_call(
        body, in_specs=[vm] * (1 + 3 * k), out_specs=[vm] * (4 * k + 1),
        out_shape=shapes * 4 + [jax.ShapeDtypeStruct((1, 1), F32)], name="adam_small", compiler_params=_params(),
    )(sheets, *[d[n].reshape(SMALL[n][1]) for d in (w, m, v) for n in names])
    return {n: tuple(res[kind * k + t] for kind in range(4)) for t, n in enumerate(names)}, res[4 * k]


def kernel(x, attn_norm, w_in, dn_conv, dn_a_log, dn_dt_bias, dn_out_norm, swa_q_norm, swa_k_norm, swa_sinks, rel_bias, w_branch_dn, w_branch_swa, w_out, ffn_norm, w_gate, w_up, w_down, loss_target, m_attn_norm, m_w_in, m_dn_conv, m_dn_a_log, m_dn_dt_bias, m_dn_out_norm, m_swa_q_norm, m_swa_k_norm, m_swa_sinks, m_rel_bias, m_w_branch_dn, m_w_branch_swa, m_w_out, m_ffn_norm, m_w_gate, m_w_up, m_w_down, v_attn_norm, v_w_in, v_dn_conv, v_dn_a_log, v_dn_dt_bias, v_dn_out_norm, v_swa_q_norm, v_swa_k_norm, v_swa_sinks, v_rel_bias, v_w_branch_dn, v_w_branch_swa, v_w_out, v_ffn_norm, v_w_gate, v_w_up, v_w_down):
    args = dict(locals())
    S = x.shape[1]
    xs = x.reshape(S, D_MODEL)
    target = loss_target.reshape(S, D_MODEL)

    w_loc = {n: args[n].reshape(BIG_SHAPES[n][0]) for n in BIG}
    conv_loc = dn_conv.reshape(CONV_SHARD)
    def on_wire(n, zero=0.0):
        return _pad_to(w_loc[n] + zero, BIG_SHAPES[n][1]).astype(BF16)

    first_handle, first_token = _all_gather_start([on_wire("w_in"), _pad_to(conv_loc, CONV_WIRE)],
                                                  "all_gather_weights_start")
    zero = first_token[0, 0]
    h = _norm_fwd(xs, attn_norm + zero, "norm1_fwd")
    later = [n for n in BIG if n != "w_in"]
    wire = [on_wire(n, zero) for n in later]
    bias = _bias_fwd(rel_bias + zero)
    first = _all_gather_finish(first_handle, [h, bias] + wire, "all_gather_weights_finish")
    rest_handle, rest_token = _exchange_start(wire, False, "gather_rest_start", after=first[1])
    w_pad = _w_in_from_blocks(first[0], rest_token)
    conv_w = jnp.concatenate([first[1][d, :DN_CONV, :CONV_SHARD[1]] for d in range(N_DEV)], axis=1)

    proj = _mm([(h, w_pad)], "nn", F32, "mm_in", 1024, 1664, j_outer=True)
    qkvn, conv_out = _dn_conv_fwd(proj, conv_w)
    beta, g = _dn_gate_fwd(proj, dn_a_log, dn_dt_bias)
    u, w, qe, kd, qk, egl, tinv = _dn_prep_fwd(qkvn, g, beta)
    o, states = _dn_scan_fwd(u, w, qe, kd, qk, egl)
    y_dn = _dn_out_fwd(o, proj, dn_out_norm)
    y_swa = _swa_fwd(proj, swa_q_norm, swa_k_norm, swa_sinks, bias)
    rest_src, rest_land = _exchange_wait(rest_handle, y_swa, False, "gather_rest_wait")
    G = {n: _own_slot(land, src) for n, src, land in zip(later, rest_src, rest_land)}
    w_bdn, w_bswa, w_g, w_u = G["w_branch_dn"], G["w_branch_swa"], G["w_gate"], G["w_up"]
    w_o = G["w_out"].reshape(D_MODEL, D_MODEL)
    w_d = G["w_down"].reshape(D_FFP, D_MODEL)
    gates = [(proj, P_GATE // 512), (proj, (P_GATE + D_MODEL) // 512)]
    a_dn, a_swa, merged = _mm_fused(
        [(y_dn, w_bdn), (y_swa, w_bswa)], "nn", "mm_branch_merge", 1024, 512,
        lambda p, e: (p[0], p[1], _merge(e[0], e[1], p[0], p[1])), gates, (F32, F32, BF16), b_blocks=True)

    def resid_norm(p, e):
        x1 = e[0] + p[0]
        return x1, _rms(x1, e[1])

    x1, h2 = _mm_fused([(merged, w_o)], "nn", "mm_out_norm", 512, D_MODEL, resid_norm,
                       [(xs, 0), (ffn_norm, None)], (F32, BF16))
    gate, up, act = _mm_fused([(h2, w_g), (h2, w_u)], "nn", "mm_gate_up_act", 1024, 768,
                              lambda p, e: (p[0], p[1], _act(p[0], p[1])), [], (F32, F32, BF16),
                              j_outer=True, b_blocks=True)

    def loss_head(p, e):
        diff = e[0] + p[0] - e[1]
        dy = diff * (1.0 / D_MODEL)
        part = jnp.sum(jnp.mean(diff * diff, axis=-1, keepdims=True), axis=0, keepdims=True) * 0.5
        return dy, dy, part

    dy, dy_b, loss_local = _mm_fused([(act, w_d)], "nn", "mm_down_loss", 512, D_MODEL, loss_head,
                                     [(x1, 0), (target, 0)], (F32, BF16), sum_shape=(1, 1))

    def act_bwd(p, e):
        _, vjp = jax.vjp(_act, e[0], e[1])
        return vjp(p[0])

    dgate, dup = _mm_fused([(dy_b, w_d)], "nt", "mm_dact_act", 1024, 768, act_bwd, [(gate, 0), (up, 0)],
                           (BF16, BF16), j_outer=True)
    g_w_down = _mm([(act, dy_b)], "tn", BF16, "mm_dw_down", 768, D_MODEL, j_outer=True)
    g_w_down = g_w_down.reshape(N_DEV, FF_WIRE, D_MODEL)
    g_w_gate = _mm([(h2, dgate)], "tn", BF16, "mm_dw_gate", D_MODEL, 768, out_blocks=True)
    g_w_up = _mm([(h2, dup)], "tn", BF16, "mm_dw_up", D_MODEL, 768, out_blocks=True)
    ffn_handle, ffn_token = _exchange_start([g_w_down, g_w_gate, g_w_up], True, "scatter_ffn_start")

    def norm_bwd(p, e):
        _, vjp = jax.vjp(_rms, e[0], e[2])
        dx, dgain = vjp(sum(p))
        dx = dx + e[1]
        return dx, dx, dgain

    dx1, dx1_b, g_ffn_norm = _mm_fused(
        [(dgate, w_g), (dup, w_u)], "nt", "mm_dh2_norm", 256, D_MODEL, norm_bwd,
        [(x1, 0), (dy, 0), (ffn_norm + ffn_token[0, 0], None)], (F32, BF16), b_blocks=True, sum_shape=(1, D_MODEL))
    def merge_bwd(p, e):
        _, vjp = jax.vjp(_merge, *e)
        dg0, dg1, da_dn, da_swa = vjp(p[0])
        return jnp.concatenate([dg0, dg1], axis=1), da_dn, da_swa

    dproj, da_dn, da_swa = _mm_fused(
        [(dx1_b, w_o)], "nt", "mm_dmerged_merge", 512, D_MODEL, merge_bwd,
        [(proj, P_GATE // D_MODEL), (proj, P_GATE // D_MODEL + 1), (a_dn, 0), (a_swa, 0)], (BF16,) * 3,
        wide_first=(P_WIDTH, 2 * D_MODEL))
    g_w_out = _mm([(merged, dx1_b)], "tn", BF16, "mm_dw_out", 512, D_MODEL, j_outer=True)
    g_w_out = g_w_out.reshape(N_DEV, LANES, D_MODEL)
    dy_dn = _mm([(da_dn, w_bdn)], "nt", F32, "mm_dy_dn", 1024, DN_WIDTH, b_blocks=True)
    dy_swa = _mm([(da_swa, w_bswa)], "nt", F32, "mm_dy_swa", 1024, SWA_WIDTH, b_blocks=True)
    g_w_bdn = _mm([(y_dn, da_dn)], "tn", BF16, "mm_dw_branch_dn", DN_WIDTH, 512, out_blocks=True)
    g_w_bswa = _mm([(y_swa, da_swa)], "tn", BF16, "mm_dw_branch_swa", SWA_WIDTH, 512, out_blocks=True)
    dproj, dsk, dsv, g_q_norm, g_k_norm, g_sinks, dbias = _swa_bwd(proj, swa_q_norm, swa_k_norm, swa_sinks, bias,
                                                                   dy_swa, dproj)
    dproj = _kv_into(dproj, dsk, dsv)
    g_rel_bias = _bias_bwd(dbias)[:, :REL_BUCKETS].T
    mix_handle, mix_token = _exchange_start([g_w_out, g_w_bdn, g_w_bswa], True, "scatter_mix_start")
    do, dproj, g_out_norm = _dn_out_bwd(o, proj, dn_out_norm + mix_token[0, 0], dy_dn, dproj)
    du, dw, dqe, dkd, dqk, degl = _dn_scan_bwd(u, w, qe, kd, qk, egl, states, do)
    dqkvn, dgd, dbeta = _dn_prep_bwd(qkvn, g, beta, tinv, du, dw, dqe, dkd, dqk, degl)
    dproj, dal, ddt = _dn_gate_bwd(proj, dn_a_log, dn_dt_bias, dbeta, dgd, dproj)
    g_a_log = dal.reshape(DN_HEADS, DN_DIM).sum(axis=1)
    g_dt_bias = ddt[0, DN_HEADS:2 * DN_HEADS]
    dproj, g_conv = _dn_conv_bwd(proj, conv_w, conv_out, dqkvn, dproj)
    g_w_in = _w_in_to_blocks(_mm([(h, dproj)], "tn", BF16, "mm_dw_in", 512, 1664, j_outer=True))
    in_handle, in_token = _exchange_start([g_w_in], True, "scatter_in_start")
    dx, g_attn_norm = _mm_fused(
        [(dproj, w_pad)], "nt", "mm_dh_norm", 512, D_MODEL, lambda p, e: norm_bwd(p, e)[1:],
        [(xs, 0), (dx1, 0), (attn_norm + in_token[0, 0], None)], (F32,), sum_shape=(1, D_MODEL))

    g_small = {"attn_norm": g_attn_norm, "ffn_norm": g_ffn_norm, "rel_bias": g_rel_bias, "dn_out_norm": g_out_norm,
               "swa_q_norm": g_q_norm, "swa_k_norm": g_k_norm, "dn_a_log": g_a_log, "dn_dt_bias": g_dt_bias,
               "swa_sinks": g_sinks}
    me = 4 * lax.axis_index("x") + 2 * lax.axis_index("y") + lax.axis_index("c")
    outs = {}

    def finish(handle, group, name, after):
        srcs, lands = _exchange_wait(handle, after, True, name)
        for n, src, land in zip(group, srcs, lands):
            parts = _own_slot(land, lax.dynamic_index_in_dim(src, me, 0, keepdims=False))
            outs[n] = _adam_update(parts, args[n], args["m_" + n], args["v_" + n], "adam_" + n,
                                   turned=args[n].shape[2] % LANES != 0)

    finish(ffn_handle, ("w_down", "w_gate", "w_up"), "scatter_ffn_wait", dx)
    finish(mix_handle, ("w_out", "w_branch_dn", "w_branch_swa"), "scatter_mix_wait", dx)
    sheets, conv_all = _all_gather_direct([_small_pack(g_small, loss_local), _pad_to(g_conv, (8, DN_QKV))],
                                          "all_gather_small",
                                          after=[outs[n][4] for n in sorted(outs)])
    finish(in_handle, ("w_in",), "scatter_in_wait", sheets)
    conv_parts = lax.dynamic_slice(conv_all, (0, 0, me * CONV_SHARD[1]), (N_DEV,) + CONV_SHARD)
    outs["dn_conv"] = _adam_update(conv_parts, dn_conv, m_dn_conv, v_dn_conv, "adam_dn_conv")
    small_outs, loss = _small_update(sheets, {n: args[n] for n in SMALL}, {n: args["m_" + n] for n in SMALL},
                                     {n: args["v_" + n] for n in SMALL})
    outs.update(small_outs)

    names = ("attn_norm", "w_in", "dn_conv", "dn_a_log", "dn_dt_bias", "dn_out_norm", "swa_q_norm", "swa_k_norm",
             "swa_sinks", "rel_bias", "w_branch_dn", "w_branch_swa", "w_out", "ffn_norm", "w_gate", "w_up", "w_down")
    results = []
    for kind in range(4):
        results += [outs[n][kind].reshape(args[n].shape) for n in names]

    return (loss.reshape(()), dx.reshape(x.shape), *results)
```

```python
import math

import numpy as np
import jax
import jax.numpy as jnp
from jax import lax
from jax.experimental import pallas as pl
from jax.experimental.pallas import tpu as pltpu

F32 = jnp.float32
BF16 = jnp.bfloat16
HI = lax.Precision.HIGHEST

D_MODEL = 1024
DN_HEADS = 4
DN_DIM = 128
DN_WIDTH = 512
DN_QKV = 1536
DN_CONV = 4
CHUNK = 64
SWA_HEADS = 8
SWA_KV = 2
SWA_GROUP = 4
SWA_DIM = 64
SWA_WIDTH = 512
SWA_KVW = 128
WINDOW = 128
BLOCK = 128
REL_BUCKETS = 32
REL_MAX_DIST = 128
D_FF = 2816
D_IN = 4872
EPS = 1e-6
N_DEV = 8

ADAM_LR = 0.001
ADAM_B1 = 0.9
ADAM_B2 = 0.999
ADAM_EPS = 1e-08
ADAM_WD = 0.01
ADAM_STEP = 10

P_GATE, P_QKV, P_Z, P_SQ, P_SK, P_SV, P_BA = 0, 2048, 3584, 4096, 4608, 4736, 4864
P_WIDTH = 4992
R_QKV, R_Z, R_B, R_A, R_SQ, R_SK, R_SV, R_GATE = 0, 1536, 2048, 2052, 2056, 2568, 2696, 2824

VMEM_LIMIT = 56 * 1024 * 1024
LANES = 128
MESH_ID = pl.DeviceIdType.MESH


def _params(sem=None):
    return pltpu.CompilerParams(dimension_semantics=sem, vmem_limit_bytes=VMEM_LIMIT)


def _pick(dim, target):
    if dim <= target:
        return dim
    t = target - target % LANES
    while t >= LANES:
        if dim % t == 0:
            return t
        t -= LANES
    return dim


_DIMS = {"nn": (((1,), (0,)), ((), ())), "nt": (((1,), (1,)), ((), ())), "tn": (((0,), (0,)), ((), ()))}


def _tile_product(a_ref, b_ref, mode, b_blocks):
    a = a_ref[...].astype(BF16)
    b = jnp.concatenate([b_ref[d] for d in range(b_ref.shape[0])], axis=1) if b_blocks else b_ref[...]
    return lax.dot_general(a, b.astype(BF16), _DIMS[mode], preferred_element_type=F32)


def _mm(pairs, mode, out_dtype, name, bm, bn, j_outer=False, b_blocks=False, out_blocks=False):
    a0, b0 = pairs[0]
    cb = b0.shape[2] if b_blocks else None
    b_shape = (b0.shape[1], N_DEV * cb) if b_blocks else b0.shape
    if mode == "nn":
        (M, K), (K2, N) = a0.shape, b_shape
    elif mode == "nt":
        (M, K), (N, K2) = a0.shape, b_shape
    else:
        (K, M), (K2, N) = a0.shape, b_shape
    bm, bn = min(bm, M), min(bn, N)
    assert K == K2 and M % bm == 0 and N % bn == 0, (name, a0.shape, b0.shape, bm, bn)
    co = N // N_DEV
    assert not out_blocks or bn % co == 0
    dims = _DIMS[mode]
    n = len(pairs)

    def body(*refs):
        o_ref = refs[2 * n]
        acc = None
        for t in range(n):
            p = _tile_product(refs[2 * t], refs[2 * t + 1], mode, b_blocks)
            acc = p if acc is None else acc + p
        if out_blocks:
            for d in range(bn // co):
                o_ref[d] = acc[:, d * co:(d + 1) * co].astype(out_dtype)
        else:
            o_ref[...] = acc.astype(out_dtype)

    def ij(f):
        return (lambda j, i: f(i, j)) if j_outer else f

    a_spec = pl.BlockSpec((K, bm), ij(lambda i, j: (0, i))) if mode == "tn" else pl.BlockSpec((bm, K), ij(lambda i, j: (i, 0)))
    if b_blocks and mode == "nt":
        b_spec = pl.BlockSpec((N_DEV, bn, cb), ij(lambda i, j: (0, j, 0)))
    elif b_blocks:
        b_spec = pl.BlockSpec((bn // cb, K, cb), ij(lambda i, j: (j, 0, 0)))
    elif mode == "nt":
        b_spec = pl.BlockSpec((bn, K), ij(lambda i, j: (j, 0)))
    else:
        b_spec = pl.BlockSpec((K, bn), ij(lambda i, j: (0, j)))
    if out_blocks:
        out_spec = pl.BlockSpec((bn // co, bm, co), ij(lambda i, j: (j, i, 0)))
        out_shape = jax.ShapeDtypeStruct((N_DEV, M, co), out_dtype)
    else:
        out_spec = pl.BlockSpec((bm, bn), ij(lambda i, j: (i, j)))
        out_shape = jax.ShapeDtypeStruct((M, N), out_dtype)
    grid = (N // bn, M // bm) if j_outer else (M // bm, N // bn)
    return pl.pallas_call(
        body, grid=grid, in_specs=[a_spec, b_spec] * n, out_specs=out_spec, out_shape=out_shape, name=name,
        compiler_params=_params(("parallel", "parallel")),
    )(*[x for pair in pairs for x in pair])


def _mm_fused(pairs, mode, name, bm, bn, epilogue, extras, out_dtypes, j_outer=False, b_blocks=False,
              sum_shape=None, wide_first=None):
    a0, b0 = pairs[0]
    cb = b0.shape[2] if b_blocks else None
    b_shape = (b0.shape[1], N_DEV * cb) if b_blocks else b0.shape
    if mode == "nn":
        (M, K), (K2, N) = a0.shape, b_shape
    else:
        (M, K), (N, K2) = a0.shape, b_shape
    bm, bn = min(bm, M), min(bn, N)
    assert mode in ("nn", "nt") and K == K2 and M % bm == 0 and N % bn == 0, (name, a0.shape, b0.shape)
    dims = _DIMS[mode]
    n, ne, no = len(pairs), len(extras), len(out_dtypes)

    def body(*refs):
        prods = [_tile_product(refs[2 * t], refs[2 * t + 1], mode, b_blocks) for t in range(n)]
        results = epilogue(prods, [r[...] for r in refs[2 * n:2 * n + ne]])
        out_refs = refs[2 * n + ne:]
        for o_ref, val, dt in zip(out_refs, results, out_dtypes):
            o_ref[...] = val.astype(dt)
        if sum_shape is not None:
            s_ref = out_refs[no]

            @pl.when((pl.program_id(0) == 0) & (pl.program_id(1) == 0))
            def _():
                s_ref[...] = jnp.zeros_like(s_ref)

            s_ref[...] += results[no]

    def ij(f):
        return (lambda j, i: f(i, j)) if j_outer else f

    a_spec = pl.BlockSpec((bm, K), ij(lambda i, j: (i, 0)))
    once = dict(pipeline_mode=pl.Buffered(1)) if bn == N else {}
    if b_blocks and mode == "nt":
        b_spec = pl.BlockSpec((N_DEV, bn, cb), ij(lambda i, j: (0, j, 0)), **once)
    elif b_blocks:
        b_spec = pl.BlockSpec((bn // cb, K, cb), ij(lambda i, j: (j, 0, 0)), **once)
    elif mode == "nt":
        b_spec = pl.BlockSpec((bn, K), ij(lambda i, j: (j, 0)), **once)
    else:
        b_spec = pl.BlockSpec((K, bn), ij(lambda i, j: (0, j)), **once)
    e_specs = [pl.BlockSpec((1, bn), ij(lambda i, j: (0, j))) if first is None
               else pl.BlockSpec((bm, bn), ij(lambda i, j, first=first: (i, first + j))) for _, first in extras]
    tile = pl.BlockSpec((bm, bn), ij(lambda i, j: (i, j)))
    out_specs = [tile] * no
    out_shape = [jax.ShapeDtypeStruct((M, N), dt) for dt in out_dtypes]
    if wide_first is not None:
        assert bn == N
        out_specs[0] = pl.BlockSpec((bm, wide_first[1]), ij(lambda i, j: (i, 0)))
        out_shape[0] = jax.ShapeDtypeStruct((M, wide_first[0]), out_dtypes[0])
    if sum_shape is not None:
        assert sum_shape[1] in (1, bn) and (sum_shape[1] == 1 or bn == N)
        out_specs.append(_full(sum_shape))
        out_shape.append(jax.ShapeDtypeStruct(sum_shape, F32))
    grid = (N // bn, M // bm) if j_outer else (M // bm, N // bn)
    sem = ("arbitrary", "arbitrary") if sum_shape is not None else ("parallel", "parallel")
    return pl.pallas_call(
        body, grid=grid, in_specs=[a_spec, b_spec] * n + e_specs, out_specs=out_specs, out_shape=out_shape,
        name=name, compiler_params=_params(sem),
    )(*[x for pair in pairs for x in pair], *[arr for arr, _ in extras])


def _rms(x, gain):
    return x * lax.rsqrt(jnp.mean(x * x, axis=-1, keepdims=True) + EPS) * gain


def _silu(x):
    return x * jax.nn.sigmoid(x)


def _act(g, u):
    return _silu(g) * u


def _merge(g0, g1, a_dn, a_swa):
    return jax.nn.sigmoid(g0) * a_dn + jax.nn.sigmoid(g1) * a_swa


def _dn_post(c, is_v, q_scale):
    a = _silu(c)
    rs = lax.rsqrt(jnp.sum(a * a, axis=-1, keepdims=True) + EPS) * q_scale
    return a * jnp.where(is_v, 1.0, rs)


def _dn_post_bwd(c, d, is_v, q_scale):
    sig = jax.nn.sigmoid(c)
    a = c * sig
    u = lax.rsqrt(jnp.sum(a * a, axis=-1, keepdims=True) + EPS)
    f = jnp.where(is_v, 1.0, u * q_scale)
    k = jnp.where(is_v, 0.0, jnp.sum(d * a, axis=-1, keepdims=True) * (u * u * u) * q_scale)
    return (f * d - a * k) * (sig + a * (1.0 - sig))


def _dn_out(o, z, gain):
    return _rms(o, gain) * _silu(z)


def _dot(a, b, dims=_DIMS["nn"], hi=False):
    if a.ndim == 3 or b.ndim == 3:
        batch = a.shape[0] if a.ndim == 3 else b.shape[0]
        a = a if a.ndim == 3 else jnp.broadcast_to(a, (batch,) + a.shape)
        b = b if b.ndim == 3 else jnp.broadcast_to(b, (batch,) + b.shape)
        ((ca,), (cb,)), _ = dims
        dims = (((ca + 1,), (cb + 1,)), ((0,), (0,)))
    if hi:
        return lax.dot_general(a, b, dims, precision=HI, preferred_element_type=F32)
    return lax.dot_general(a.astype(BF16), b.astype(BF16), dims, preferred_element_type=F32)


def _pieces(x):
    hi = x.astype(BF16)
    r1 = x - hi.astype(F32)
    mid = r1.astype(BF16)
    return hi, mid, (r1 - mid.astype(F32)).astype(BF16)


def _sel_left_impl(m, x):
    mb = m.astype(BF16)
    hi, mid, lo = _pieces(x)
    return _dot(mb, hi) + (_dot(mb, mid) + _dot(mb, lo))


@jax.custom_vjp
def _sel_left(m, mt, x):
    return _sel_left_impl(m, x)


_sel_left.defvjp(lambda m, mt, x: (_sel_left_impl(m, x), (m, mt)),
                 lambda res, ct: (jnp.zeros_like(res[0]), jnp.zeros_like(res[1]), _sel_left_impl(res[1], ct)))


def _sel_right_impl(x, s):
    sb = s.astype(BF16)
    hi, mid, lo = _pieces(x)
    return _dot(hi, sb) + (_dot(mid, sb) + _dot(lo, sb))


@jax.custom_vjp
def _sel_right(x, s, st):
    return _sel_right_impl(x, s)


_sel_right.defvjp(lambda x, s, st: (_sel_right_impl(x, s), (s, st)),
                  lambda res, ct: (_sel_right_impl(ct, res[1]), jnp.zeros_like(res[0]), jnp.zeros_like(res[1])))


def _dot3_impl(a, b):
    a_hi, a_lo, _ = _pieces(a)
    b_hi, b_lo, _ = _pieces(b)
    return _dot(a_hi, b_hi) + (_dot(a_hi, b_lo) + _dot(a_lo, b_hi))


@jax.custom_vjp
def _dot3(a, b):
    return _dot3_impl(a, b)


_dot3.defvjp(lambda a, b: (_dot3_impl(a, b), (a, b)),
             lambda res, ct: (_dot(ct, res[1], _DIMS["nt"]), _dot(res[0], ct, _DIMS["tn"])))


def _inv_impl(a, eye, strict):
    t = eye - a
    p = _dot(a, a)
    for level in range(5):
        t = t + _dot(t, p)
        if level < 4:
            p = _dot(p, p)
    t = t + _dot(t, eye - t - _dot3_impl(a, t))
    return jnp.where(strict > 0.5, t, eye)


@jax.custom_vjp
def _inv_given(a, t):
    return t.astype(F32)


_inv_given.defvjp(lambda a, t: (t.astype(F32), t),
                  lambda t, ct: (-_dot(_dot(t, ct, _DIMS["tn"]), t, _DIMS["nt"]), jnp.zeros_like(t)))


@jax.custom_vjp
def _lanes_join(a, b):
    return jnp.concatenate([a, b], axis=-1)


_lanes_join.defvjp(lambda a, b: (jnp.concatenate([a, b], axis=-1), None),
                   lambda _, ct: (ct[..., :ct.shape[-1] // 2], ct[..., ct.shape[-1] // 2:]))


@jax.custom_vjp
def _lanes_halves(y):
    h = y.shape[-1] // 2
    return y[..., :h], y[..., h:]


_lanes_halves.defvjp(lambda y: ((y[..., :y.shape[-1] // 2], y[..., y.shape[-1] // 2:]), None),
                     lambda _, ct: (jnp.concatenate(ct, axis=-1),))

GROUP = 4
GROUP_ROWS = GROUP * CHUNK


def _block_consts(n):
    ii = lax.broadcasted_iota(jnp.int32, (n, n), 0)
    jj = lax.broadcasted_iota(jnp.int32, (n, n), 1)
    shift = CHUNK.bit_length() - 1
    same = jnp.right_shift(ii, shift) == jnp.right_shift(jj, shift)
    return same & (ii >= jj), same & (ii <= jj), same & (ii > jj), same, ii == jj


def _lane0(n):
    s = (lax.broadcasted_iota(jnp.int32, (LANES, n), 0) == 0).astype(F32)
    st = (lax.broadcasted_iota(jnp.int32, (n, LANES), 1) == 0).astype(F32)
    return s, st


def _dn_group(q, k, v, g, beta, t_saved=None):
    n = GROUP_ROWS
    low_b, upp_b, strict_b, _, eye_b = _block_consts(n)
    low, upp, eye = low_b.astype(F32), upp_b.astype(F32), eye_b.astype(F32)
    gc = _sel_left(low, upp, g)
    per_chunk = (g.shape[0], GROUP, CHUNK, LANES)
    g_last = jnp.sum(g.reshape(per_chunk), axis=2, keepdims=True)
    gl = jnp.broadcast_to(g_last, per_chunk).reshape(g.shape)
    s, st = _lane0(n)
    col = _sel_right(gc, s, st)
    row = jnp.swapaxes(col, 1, 2)
    decay = jnp.exp(jnp.where(low_b, col - row, -jnp.inf))
    kb = k * beta
    vb = v * beta
    a = jnp.where(strict_b, _dot(kb, k, _DIMS["nt"]) * decay, 0.0)
    t = _inv_impl(a, eye, strict_b.astype(F32)) if t_saved is None else _inv_given(a, t_saved)
    u, w = _lanes_halves(_dot3(t, _lanes_join(vb, kb * jnp.exp(gc))))
    fold = (jnp.bitwise_and(lax.broadcasted_iota(jnp.int32, (n, CHUNK), 0), CHUNK - 1)
            == lax.broadcasted_iota(jnp.int32, (n, CHUNK), 1)).astype(F32)
    fold_t = (jnp.bitwise_and(lax.broadcasted_iota(jnp.int32, (CHUNK, n), 1), CHUNK - 1)
              == lax.broadcasted_iota(jnp.int32, (CHUNK, n), 0)).astype(F32)
    qk = _sel_right(_dot(q, k, _DIMS["nt"]) * decay, fold, fold_t)
    return u, w, q * jnp.exp(gc), k * jnp.exp(gl - gc), qk, jnp.exp(g_last), t


def _dn_step(s, u, w, qe, kd, qk, egl):
    v_new = u - _dot(w, s)
    o = _dot(qe, s) + _dot(qk, v_new)
    s_new = s * egl + _dot(kd, v_new, _DIMS["tn"])
    return s_new, o


def _swa_block(q, kband, vband, qg, kg, sinks, band):
    kn = _rms(kband, kg)
    qn = _rms(q, qg) * (SWA_DIM ** -0.5)
    logits = _dot(qn, kn, _DIMS["nt"]) + band
    m = lax.stop_gradient(jnp.maximum(jnp.max(logits, axis=-1, keepdims=True), sinks))
    p = jnp.exp(logits - m)
    denom = jnp.sum(p, axis=-1, keepdims=True) + jnp.exp(sinks - m)
    return _dot(p * (1.0 / denom), vband)


def _adamw(w, g, m, v):
    m = ADAM_B1 * m + (1.0 - ADAM_B1) * g
    v = ADAM_B2 * v + (1.0 - ADAM_B2) * jnp.square(g)
    m_hat = m / (1.0 - ADAM_B1 ** ADAM_STEP)
    v_hat = v / (1.0 - ADAM_B2 ** ADAM_STEP)
    delta = -ADAM_LR * (m_hat / (jnp.sqrt(v_hat) + ADAM_EPS) + ADAM_WD * w)
    return delta, m, v


def _row(tm, c, cb=0):
    return pl.BlockSpec((tm, c), lambda i, cb=cb: (i, cb))


def _full(shape):
    nd = len(shape)
    return pl.BlockSpec(shape, lambda *_, nd=nd: (0,) * nd)


def _norm_fwd(x, gain, name, tm=1024):
    S = x.shape[0]

    def body(x_ref, g_ref, h_ref):
        h_ref[...] = _rms(x_ref[...], g_ref[...]).astype(BF16)

    return pl.pallas_call(
        body, grid=(S // tm,), in_specs=[_row(tm, D_MODEL), _full((1, D_MODEL))],
        out_specs=_row(tm, D_MODEL), out_shape=jax.ShapeDtypeStruct((S, D_MODEL), BF16),
        name=name, compiler_params=_params(("parallel",)))(x, gain)


def _shift_down(x, s):
    row = lax.broadcasted_iota(jnp.int32, x.shape, 0)
    return jnp.where(row >= s, pltpu.roll(x, s, axis=0), 0.0)


def _shift_up(x, s):
    n = x.shape[0]
    row = lax.broadcasted_iota(jnp.int32, x.shape, 0)
    return jnp.where(row < n - s, pltpu.roll(x, n - s, axis=0), 0.0)


def _conv(x, w):
    out = w[DN_CONV - 1:DN_CONV] * x
    for s in range(1, DN_CONV):
        out = out + w[DN_CONV - 1 - s:DN_CONV - s] * _shift_down(x, s)
    return out


def _dn_conv_fwd(proj, conv_w):
    S = proj.shape[0]
    nb = DN_QKV // LANES

    def body(x_ref, w_ref, o_ref, c_ref):
        j = pl.program_id(0)
        q_scale = jnp.where(j < DN_HEADS, DN_DIM ** -0.5, 1.0).astype(F32)
        c = _conv(x_ref[...], w_ref[...])
        c_ref[...] = c
        o_ref[...] = _dn_post(c, j >= 2 * DN_HEADS, q_scale)

    col = pl.BlockSpec((S, LANES), lambda j: (0, j))
    return pl.pallas_call(
        body, grid=(nb,),
        in_specs=[pl.BlockSpec((S, LANES), lambda j: (0, P_QKV // LANES + j)),
                  pl.BlockSpec((DN_CONV, LANES), lambda j: (0, j))],
        out_specs=[col, col], out_shape=[jax.ShapeDtypeStruct((S, DN_QKV), F32)] * 2, name="dn_conv_fwd",
        compiler_params=_params(("parallel",)))(proj, conv_w)


def _dn_conv_bwd(proj, conv_w, conv_out, dqkvn, dproj):
    S = proj.shape[0]
    nb = DN_QKV // LANES

    def body(x_ref, w_ref, c_ref, d_ref, _, dx_ref, dw_ref):
        j = pl.program_id(0)
        q_scale = jnp.where(j < DN_HEADS, DN_DIM ** -0.5, 1.0).astype(F32)
        x = x_ref[...]
        w = w_ref[...]
        dc = _dn_post_bwd(c_ref[...], d_ref[0], j >= 2 * DN_HEADS, q_scale)
        dx = w[DN_CONV - 1:DN_CONV] * dc
        dw_ref[DN_CONV - 1:DN_CONV, :] = jnp.sum(dc * x, axis=0, keepdims=True)
        for s in range(1, DN_CONV):
            up = _shift_up(dc, s)
            dx = dx + w[DN_CONV - 1 - s:DN_CONV - s] * up
            dw_ref[DN_CONV - 1 - s:DN_CONV - s, :] = jnp.sum(up * x, axis=0, keepdims=True)
        dx_ref[...] = dx.astype(BF16)

    return pl.pallas_call(
        body, grid=(nb,),
        in_specs=[pl.BlockSpec((S, LANES), lambda j: (0, P_QKV // LANES + j)),
                  pl.BlockSpec((DN_CONV, LANES), lambda j: (0, j)),
                  pl.BlockSpec((S, LANES), lambda j: (0, j)),
                  pl.BlockSpec((1, S, LANES), lambda j: (lax.div(j, DN_HEADS), 0, lax.rem(j, DN_HEADS))),
                  pl.BlockSpec(memory_space=pl.ANY)],
        out_specs=[pl.BlockSpec((S, LANES), lambda j: (0, P_QKV // LANES + j)),
                   pl.BlockSpec((DN_CONV, LANES), lambda j: (0, j))],
        out_shape=[jax.ShapeDtypeStruct(dproj.shape, dproj.dtype), jax.ShapeDtypeStruct((DN_CONV, DN_QKV), F32)],
        input_output_aliases={4: 0},
        name="dn_conv_bwd", compiler_params=_params(("parallel",)))(proj, conv_w, conv_out, dqkvn, dproj)


def _expanders():
    eb = np.zeros((LANES, DN_WIDTH), np.float32)
    ea = np.zeros((LANES, DN_WIDTH), np.float32)
    for h in range(DN_HEADS):
        eb[h, h * DN_DIM:(h + 1) * DN_DIM] = 1.0
        ea[DN_HEADS + h, h * DN_DIM:(h + 1) * DN_DIM] = 1.0
    return jnp.asarray(eb), jnp.asarray(ea), jnp.asarray(eb.T), jnp.asarray(ea.T)


def _dn_gate_args(a_log, dt_bias):
    alog = jnp.repeat(a_log.reshape(1, DN_HEADS), DN_DIM, axis=1)
    dtb = _pad_to(jnp.pad(dt_bias.reshape(1, DN_HEADS), ((0, 0), (DN_HEADS, 0))), (1, LANES))
    return _expanders() + (alog, dtb)


def _dn_gate_specs(tm):
    return [_row(tm, LANES, P_BA // LANES), _full((LANES, DN_WIDTH)), _full((LANES, DN_WIDTH)),
            _full((DN_WIDTH, LANES)), _full((DN_WIDTH, LANES)), _full((1, DN_WIDTH)), _full((1, LANES))]


def _dn_gate_fn(ba, eb, ea, ebt, eat, alog, dtb):
    beta = _sel_right(jax.nn.sigmoid(ba), eb, ebt)
    g = -jnp.exp(alog) * _sel_right(jax.nn.softplus(ba + dtb), ea, eat)
    return beta, g


def _dn_gate_fwd(proj, a_log, dt_bias, tm=1024):
    S = proj.shape[0]
    args = _dn_gate_args(a_log, dt_bias)

    def body(ba_ref, eb_ref, ea_ref, ebt_ref, eat_ref, al_ref, dt_ref, beta_ref, g_ref):
        beta, g = _dn_gate_fn(ba_ref[...], eb_ref[...], ea_ref[...], ebt_ref[...], eat_ref[...], al_ref[...],
                              dt_ref[...])
        beta_ref[...] = beta
        g_ref[...] = g

    return pl.pallas_call(
        body, grid=(S // tm,), in_specs=_dn_gate_specs(tm), out_specs=[_row(tm, DN_WIDTH), _row(tm, DN_WIDTH)],
        out_shape=[jax.ShapeDtypeStruct((S, DN_WIDTH), F32), jax.ShapeDtypeStruct((S, DN_WIDTH), F32)],
        name="dn_gate_fwd", compiler_params=_params(("parallel",)))(proj, *args)


def _dn_gate_bwd(proj, a_log, dt_bias, dbeta, dg, dproj, tm=1024):
    S = proj.shape[0]
    args = _dn_gate_args(a_log, dt_bias)

    def body(ba_ref, eb_ref, ea_ref, ebt_ref, eat_ref, al_ref, dt_ref, dbeta_ref, dg_ref, _, dba_ref, dal_ref,
             ddt_ref):
        eb, ea, ebt, eat = eb_ref[...], ea_ref[...], ebt_ref[...], eat_ref[...]
        _, vjp = jax.vjp(lambda ba, al, dt: _dn_gate_fn(ba, eb, ea, ebt, eat, al, dt), ba_ref[...], al_ref[...],
                         dt_ref[...])
        dba, dal, ddt = vjp((dbeta_ref[...], dg_ref[...]))
        dba_ref[...] = dba.astype(BF16)

        @pl.when(pl.program_id(0) == 0)
        def _():
            dal_ref[...] = jnp.zeros_like(dal_ref)
            ddt_ref[...] = jnp.zeros_like(ddt_ref)

        dal_ref[...] += dal
        ddt_ref[...] += ddt

    return pl.pallas_call(
        body, grid=(S // tm,),
        in_specs=_dn_gate_specs(tm) + [_row(tm, DN_WIDTH), _row(tm, DN_WIDTH), pl.BlockSpec(memory_space=pl.ANY)],
        out_specs=[_row(tm, LANES, P_BA // LANES), _full((1, DN_WIDTH)), _full((1, LANES))],
        out_shape=[jax.ShapeDtypeStruct(dproj.shape, dproj.dtype), jax.ShapeDtypeStruct((1, DN_WIDTH), F32),
                   jax.ShapeDtypeStruct((1, LANES), F32)],
        input_output_aliases={len(args) + 3: 0},
        name="dn_gate_bwd", compiler_params=_params(("arbitrary",)))(proj, *args, dbeta, dg, dproj)


PREP_GROUPS = 8
PREP_CHUNKS = GROUP * PREP_GROUPS


def _dn_prep_specs():
    rows = PREP_CHUNKS * CHUNK
    q = pl.BlockSpec((rows, LANES), lambda h, c: (c, h))
    k = pl.BlockSpec((rows, LANES), lambda h, c: (c, DN_HEADS + h))
    v = pl.BlockSpec((rows, LANES), lambda h, c: (c, 2 * DN_HEADS + h))
    qk = pl.BlockSpec((1, rows, CHUNK), lambda h, c: (h, c, 0))
    egl = pl.BlockSpec((1, PREP_CHUNKS, 1, LANES), lambda h, c: (h, c, 0, 0))
    return q, k, v, qk, egl


def _dn_prep_fwd(qkvn, g, beta):
    S = qkvn.shape[0]
    nc = S // CHUNK
    q, k, v, qks, egl = _dn_prep_specs()

    def body(q_ref, k_ref, v_ref, g_ref, b_ref, u_ref, w_ref, qe_ref, kd_ref, qk_ref, egl_ref, t_ref):
        rows = PREP_CHUNKS * CHUNK
        grp = (PREP_GROUPS, GROUP_ROWS, LANES)
        u, w, qe, kd, qk, e, t = _dn_group(q_ref[...].reshape(grp), k_ref[...].reshape(grp), v_ref[...].reshape(grp),
                                           g_ref[...].reshape(grp), b_ref[...].reshape(grp))
        u_ref[...] = u.reshape(rows, LANES)
        w_ref[...] = w.reshape(rows, LANES)
        qe_ref[...] = qe.reshape(rows, LANES)
        kd_ref[...] = kd.reshape(rows, LANES)
        t_ref[0] = t.reshape(rows, GROUP_ROWS).astype(BF16)
        qk_ref[0] = qk.reshape(rows, CHUNK)
        egl_ref[0] = e.reshape(PREP_CHUNKS, 1, LANES)

    wide = jax.ShapeDtypeStruct((S, DN_WIDTH), F32)
    return pl.pallas_call(
        body, grid=(DN_HEADS, nc // PREP_CHUNKS), in_specs=[q, k, v, q, q],
        out_specs=[q, q, q, q, qks, egl, _dn_tinv_spec()],
        out_shape=[wide, wide, wide, wide, jax.ShapeDtypeStruct((DN_HEADS, S, CHUNK), F32),
                   jax.ShapeDtypeStruct((DN_HEADS, nc, 1, LANES), F32),
                   jax.ShapeDtypeStruct((DN_HEADS, S, GROUP_ROWS), BF16)],
        name="dn_prep_fwd", compiler_params=_params(("parallel", "parallel")))(qkvn, qkvn, qkvn, g, beta)


def _dn_tinv_spec():
    return pl.BlockSpec((1, PREP_CHUNKS * CHUNK, GROUP_ROWS), lambda h, c: (h, c, 0))


def _dn_prep_bwd(qkvn, g, beta, tinv, du, dw, dqe, dkd, dqk, degl):
    S = qkvn.shape[0]
    nc = S // CHUNK
    q, k, v, qks, egl = _dn_prep_specs()

    def body(q_ref, k_ref, v_ref, g_ref, b_ref, t_ref, du_ref, dw_ref, dqe_ref, dkd_ref, dqk_ref, degl_ref,
             dqkv_ref, dg_ref, db_ref):
        rows = PREP_CHUNKS * CHUNK
        grp = (PREP_GROUPS, GROUP_ROWS, LANES)
        t_saved = t_ref[0].reshape(PREP_GROUPS, GROUP_ROWS, GROUP_ROWS)
        _, vjp = jax.vjp(lambda *x: _dn_group(*x, t_saved=t_saved)[:6], q_ref[...].reshape(grp),
                         k_ref[...].reshape(grp), v_ref[...].reshape(grp), g_ref[...].reshape(grp),
                         b_ref[...].reshape(grp))
        dq, dk, dv, dg, db = vjp((du_ref[...].reshape(grp), dw_ref[...].reshape(grp), dqe_ref[...].reshape(grp),
                                  dkd_ref[...].reshape(grp), dqk_ref[0].reshape(PREP_GROUPS, GROUP_ROWS, CHUNK),
                                  degl_ref[0].reshape(PREP_GROUPS, GROUP, 1, LANES)))
        dqkv_ref[0] = dq.reshape(rows, LANES)
        dqkv_ref[1] = dk.reshape(rows, LANES)
        dqkv_ref[2] = dv.reshape(rows, LANES)
        dg_ref[...] = dg.reshape(rows, LANES)
        db_ref[...] = db.reshape(rows, LANES)

    wide = jax.ShapeDtypeStruct((S, DN_WIDTH), F32)
    rows = PREP_CHUNKS * CHUNK
    return pl.pallas_call(
        body, grid=(DN_HEADS, nc // PREP_CHUNKS), in_specs=[q, k, v, q, q, _dn_tinv_spec(), q, q, q, q, qks, egl],
        out_specs=[pl.BlockSpec((3, rows, LANES), lambda h, c: (0, c, h)), q, q],
        out_shape=[jax.ShapeDtypeStruct((3, S, DN_WIDTH), F32), wide, wide],
        name="dn_prep_bwd", compiler_params=_params(("parallel", "parallel")),
    )(qkvn, qkvn, qkvn, g, beta, tinv, du, dw, dqe, dkd, dqk, degl)


SCAN_CHUNKS = 16


def _dn_scan_specs(nc, reverse):
    nb = nc // SCAN_CHUNKS

    def cidx(c):
        return nb - 1 - c if reverse else c

    hc = pl.BlockSpec((SCAN_CHUNKS * CHUNK, DN_WIDTH), lambda c: (cidx(c), 0))
    qk = pl.BlockSpec((DN_HEADS, SCAN_CHUNKS * CHUNK, CHUNK), lambda c: (0, cidx(c), 0))
    egl = pl.BlockSpec((DN_HEADS, SCAN_CHUNKS, 1, LANES), lambda c: (0, cidx(c), 0, 0))
    st = pl.BlockSpec((DN_HEADS, SCAN_CHUNKS, DN_DIM, DN_DIM), lambda c: (0, cidx(c), 0, 0))
    return hc, qk, egl, st


def _heads(ref, i):
    return jnp.stack([ref[pl.ds(i * CHUNK, CHUNK), pl.ds(h * DN_DIM, DN_DIM)] for h in range(DN_HEADS)])


def _dn_scan_fwd(u, w, qe, kd, qk, egl):
    S = u.shape[0]
    nc = S // CHUNK
    hc, qks, egls, st = _dn_scan_specs(nc, False)

    def body(u_ref, w_ref, qe_ref, kd_ref, qk_ref, egl_ref, o_ref, st_ref, s_scr):
        @pl.when(pl.program_id(0) == 0)
        def _():
            s_scr[...] = jnp.zeros_like(s_scr)

        s = s_scr[...]
        for i in range(SCAN_CHUNKS):
            rows = pl.ds(i * CHUNK, CHUNK)
            st_ref[:, i] = s
            s, o = _dn_step(s, _heads(u_ref, i), _heads(w_ref, i), _heads(qe_ref, i), _heads(kd_ref, i),
                            qk_ref[:, rows, :], egl_ref[:, i])
            for h in range(DN_HEADS):
                o_ref[rows, pl.ds(h * DN_DIM, DN_DIM)] = o[h]
        s_scr[...] = s

    return pl.pallas_call(
        body, grid=(nc // SCAN_CHUNKS,), in_specs=[hc, hc, hc, hc, qks, egls], out_specs=[hc, st],
        out_shape=[jax.ShapeDtypeStruct((S, DN_WIDTH), F32), jax.ShapeDtypeStruct((DN_HEADS, nc, DN_DIM, DN_DIM), F32)],
        scratch_shapes=[pltpu.VMEM((DN_HEADS, DN_DIM, DN_DIM), F32)], name="dn_scan_fwd",
        compiler_params=_params(("arbitrary",)))(u, w, qe, kd, qk, egl)


def _dn_scan_bwd(u, w, qe, kd, qk, egl, states, do):
    S = u.shape[0]
    nc = S // CHUNK
    hc, qks, egls, st = _dn_scan_specs(nc, True)

    def body(u_ref, w_ref, qe_ref, kd_ref, qk_ref, egl_ref, st_ref, do_ref,
             du_ref, dw_ref, dqe_ref, dkd_ref, dqk_ref, degl_ref, ds_scr):
        @pl.when(pl.program_id(0) == 0)
        def _():
            ds_scr[...] = jnp.zeros_like(ds_scr)

        ds = ds_scr[...]
        for i in reversed(range(SCAN_CHUNKS)):
            rows = pl.ds(i * CHUNK, CHUNK)
            _, vjp = jax.vjp(_dn_step, st_ref[:, i], _heads(u_ref, i), _heads(w_ref, i), _heads(qe_ref, i),
                             _heads(kd_ref, i), qk_ref[:, rows, :], egl_ref[:, i])
            ds, du, dw, dqe, dkd, dqk, degl = vjp((ds, _heads(do_ref, i)))
            dqk_ref[:, rows, :] = dqk
            degl_ref[:, i] = degl
            for h in range(DN_HEADS):
                cols = pl.ds(h * DN_DIM, DN_DIM)
                du_ref[rows, cols] = du[h]
                dw_ref[rows, cols] = dw[h]
                dqe_ref[rows, cols] = dqe[h]
                dkd_ref[rows, cols] = dkd[h]
        ds_scr[...] = ds

    wide = jax.ShapeDtypeStruct((S, DN_WIDTH), F32)
    return pl.pallas_call(
        body, grid=(nc // SCAN_CHUNKS,), in_specs=[hc, hc, hc, hc, qks, egls, st, hc],
        out_specs=[hc, hc, hc, hc, qks, egls],
        out_shape=[wide, wide, wide, wide, jax.ShapeDtypeStruct((DN_HEADS, S, CHUNK), F32),
                   jax.ShapeDtypeStruct((DN_HEADS, nc, 1, LANES), F32)],
        scratch_shapes=[pltpu.VMEM((DN_HEADS, DN_DIM, DN_DIM), F32)], name="dn_scan_bwd",
        compiler_params=_params(("arbitrary",)))(u, w, qe, kd, qk, egl, states, do)


def _dn_out_fwd(o, proj, gain, tm=1024):
    S = o.shape[0]

    def body(o_ref, z_ref, g_ref, y_ref):
        y_ref[...] = _dn_out(o_ref[...], z_ref[...], g_ref[...]).astype(BF16)

    hs = pl.BlockSpec((tm, LANES), lambda i, h: (i, h))
    zs = pl.BlockSpec((tm, LANES), lambda i, h: (i, P_Z // LANES + h))
    return pl.pallas_call(
        body, grid=(S // tm, DN_HEADS), in_specs=[hs, zs, _full((1, DN_DIM))], out_specs=hs,
        out_shape=jax.ShapeDtypeStruct((S, DN_WIDTH), BF16), name="dn_out_fwd",
        compiler_params=_params(("parallel", "parallel")))(o, proj, gain)


_ANY = pl.BlockSpec(memory_space=pl.ANY)


def _dn_out_bwd(o, proj, gain, dy, dproj, tm=1024):
    S = o.shape[0]

    def body(o_ref, z_ref, g_ref, dy_ref, _, do_ref, dz_ref, dg_ref):
        _, vjp = jax.vjp(_dn_out, o_ref[...], z_ref[...], g_ref[...])
        do, dz, dg = vjp(dy_ref[...])
        do_ref[...] = do
        dz_ref[...] = dz.astype(BF16)

        @pl.when((pl.program_id(0) == 0) & (pl.program_id(1) == 0))
        def _():
            dg_ref[...] = jnp.zeros_like(dg_ref)

        dg_ref[...] += dg

    hs = pl.BlockSpec((tm, LANES), lambda i, h: (i, h))
    zs = pl.BlockSpec((tm, LANES), lambda i, h: (i, P_Z // LANES + h))
    return pl.pallas_call(
        body, grid=(S // tm, DN_HEADS), in_specs=[hs, zs, _full((1, DN_DIM)), hs, _ANY],
        out_specs=[hs, zs, _full((1, DN_DIM))],
        out_shape=[jax.ShapeDtypeStruct((S, DN_WIDTH), F32), jax.ShapeDtypeStruct(dproj.shape, dproj.dtype),
                   jax.ShapeDtypeStruct((1, DN_DIM), F32)],
        input_output_aliases={4: 1},
        name="dn_out_bwd", compiler_params=_params(("arbitrary", "arbitrary")))(o, proj, gain, dy, dproj)


def _rel_buckets():
    qi = np.arange(BLOCK)[:, None]
    kj = np.arange(2 * BLOCK)[None, :]
    n = np.maximum(BLOCK + qi - kj, 0)
    max_exact = REL_BUCKETS // 2
    nf = np.maximum(n, 1).astype(np.float32)
    large = max_exact + (np.log(nf / np.float32(max_exact)) / np.float32(math.log(REL_MAX_DIST / max_exact))
                         * np.float32(REL_BUCKETS - max_exact)).astype(np.int32)
    large = np.minimum(large, REL_BUCKETS - 1)
    return np.where(n < max_exact, n, large).astype(np.int32)


def _bias_fwd(rel_bias):
    buckets = jnp.asarray(_rel_buckets())

    def body(rb_ref, bk_ref, o_ref):
        bk = bk_ref[...]
        for h in range(SWA_HEADS):
            acc = jnp.zeros((BLOCK, 2 * BLOCK), F32)
            for b in range(REL_BUCKETS):
                acc = jnp.where(bk == b, rb_ref[b, h], acc)
            for first in range(2):
                o_ref[first, h] = jnp.where(_swa_mask(1 - first), acc, -jnp.inf)

    return pl.pallas_call(
        body, in_specs=[pl.BlockSpec(memory_space=pltpu.SMEM), pl.BlockSpec(memory_space=pltpu.VMEM)],
        out_specs=pl.BlockSpec(memory_space=pltpu.VMEM),
        out_shape=jax.ShapeDtypeStruct((2, SWA_HEADS, BLOCK, 2 * BLOCK), F32), name="swa_bias_fwd",
        compiler_params=_params())(rel_bias, buckets)


def _bias_bwd(dbias):
    buckets = jnp.asarray(_rel_buckets())

    def body(d_ref, bk_ref, o_ref):
        bk = bk_ref[...]
        lane = lax.broadcasted_iota(jnp.int32, (1, LANES), 1)
        for h in range(SWA_HEADS):
            d = d_ref[h]
            row = jnp.zeros((1, LANES), F32)
            for b in range(REL_BUCKETS):
                part = jnp.sum(jnp.where(bk == b, d, 0.0), axis=1, keepdims=True)
                row = jnp.where(lane == b, jnp.sum(part, axis=0, keepdims=True), row)
            o_ref[h:h + 1, :] = row

    return pl.pallas_call(
        body, in_specs=[pl.BlockSpec(memory_space=pltpu.VMEM), pl.BlockSpec(memory_space=pltpu.VMEM)],
        out_specs=pl.BlockSpec(memory_space=pltpu.VMEM),
        out_shape=jax.ShapeDtypeStruct((SWA_HEADS, LANES), F32), name="swa_bias_bwd",
        compiler_params=_params())(dbias, buckets)


def _swa_mask(n):
    qi = lax.broadcasted_iota(jnp.int32, (BLOCK, 2 * BLOCK), 0)
    kj = lax.broadcasted_iota(jnp.int32, (BLOCK, 2 * BLOCK), 1)
    dist = BLOCK + qi - kj
    return (dist >= 0) & (dist < WINDOW) & ((n > 0) | (kj >= BLOCK))


def _swa_in_specs():
    q = pl.BlockSpec((BLOCK, SWA_WIDTH), lambda n: (n, P_SQ // SWA_WIDTH))
    kc = pl.BlockSpec((BLOCK, SWA_KVW), lambda n: (n, P_SK // SWA_KVW))
    kp = pl.BlockSpec((BLOCK, SWA_KVW), lambda n: (jnp.maximum(n - 1, 0), P_SK // SWA_KVW))
    vc = pl.BlockSpec((BLOCK, SWA_KVW), lambda n: (n, P_SV // SWA_KVW))
    vp = pl.BlockSpec((BLOCK, SWA_KVW), lambda n: (jnp.maximum(n - 1, 0), P_SV // SWA_KVW))
    band = pl.BlockSpec((None, SWA_HEADS, BLOCK, 2 * BLOCK), lambda n: (jnp.where(n == 0, 1, 0), 0, 0, 0))
    small = [_full((1, SWA_DIM)), _full((1, SWA_DIM)), _full((1, SWA_HEADS)), band]
    return [q, kp, kc, vp, vc] + small


def _swa_load(q_ref, kp_ref, kc_ref, vp_ref, vc_ref, s_ref):
    q = jnp.stack([q_ref[:, pl.ds(h * SWA_DIM, SWA_DIM)] for h in range(SWA_HEADS)])
    kbands, vbands = [], []
    for kv in range(SWA_KV):
        cols = pl.ds(kv * SWA_DIM, SWA_DIM)
        kbands += [jnp.concatenate([kp_ref[:, cols], kc_ref[:, cols]], axis=0)] * SWA_GROUP
        vbands += [jnp.concatenate([vp_ref[:, cols], vc_ref[:, cols]], axis=0)] * SWA_GROUP
    sinks = jnp.stack([s_ref[:, pl.ds(h, 1)] for h in range(SWA_HEADS)])
    return q, jnp.stack(kbands), jnp.stack(vbands), sinks


def _swa_fwd(proj, q_gain, k_gain, sinks, bias):
    S = proj.shape[0]

    def body(q_ref, kp_ref, kc_ref, vp_ref, vc_ref, qg_ref, kg_ref, s_ref, bias_ref, y_ref):
        q, kband, vband, sk = _swa_load(q_ref, kp_ref, kc_ref, vp_ref, vc_ref, s_ref)
        out = _swa_block(q, kband, vband, qg_ref[...], kg_ref[...], sk, bias_ref[...])
        for h in range(SWA_HEADS):
            y_ref[:, pl.ds(h * SWA_DIM, SWA_DIM)] = out[h].astype(BF16)

    return pl.pallas_call(
        body, grid=(S // BLOCK,), in_specs=_swa_in_specs(),
        out_specs=pl.BlockSpec((BLOCK, SWA_WIDTH), lambda n: (n, 0)),
        out_shape=jax.ShapeDtypeStruct((S, SWA_WIDTH), BF16), name="swa_fwd",
        compiler_params=_params(("parallel",)))(proj, proj, proj, proj, proj, q_gain, k_gain, sinks, bias)


def _swa_bwd(proj, q_gain, k_gain, sinks, bias, dy, dproj):
    S = proj.shape[0]

    def body(q_ref, kp_ref, kc_ref, vp_ref, vc_ref, qg_ref, kg_ref, s_ref, bias_ref, dy_ref, _,
             dq_ref, dk_ref, dv_ref, dqg_ref, dkg_ref, ds_ref, dbias_ref):
        n = pl.program_id(0)

        @pl.when(n == 0)
        def _():
            for r in (dk_ref, dv_ref, dqg_ref, dkg_ref, ds_ref, dbias_ref):
                r[...] = jnp.zeros_like(r)

        cur = pl.ds(pl.multiple_of(n * BLOCK, BLOCK), BLOCK)
        prev = pl.ds(pl.multiple_of(jnp.maximum(n - 1, 0) * BLOCK, BLOCK), BLOCK)
        q, kband, vband, sk = _swa_load(q_ref, kp_ref, kc_ref, vp_ref, vc_ref, s_ref)
        _, vjp = jax.vjp(_swa_block, q, kband, vband, qg_ref[...], kg_ref[...], sk, bias_ref[...])
        dy = jnp.stack([dy_ref[:, pl.ds(h * SWA_DIM, SWA_DIM)] for h in range(SWA_HEADS)])
        dq, dkb, dvb, dqg, dkg, dsk, dbs = vjp(dy)
        for h in range(SWA_HEADS):
            dq_ref[:, pl.ds(h * SWA_DIM, SWA_DIM)] = dq[h].astype(BF16)
            ds_ref[:, pl.ds(h, 1)] += dsk[h]
        dbias_ref[...] += dbs
        dqg_ref[...] += dqg
        dkg_ref[...] += dkg
        for kv in range(SWA_KV):
            cols = pl.ds(kv * SWA_DIM, SWA_DIM)
            group = range(kv * SWA_GROUP, (kv + 1) * SWA_GROUP)
            dk_kv = sum(dkb[h] for h in group)
            dv_kv = sum(dvb[h] for h in group)
            dk_ref[cur, cols] += dk_kv[BLOCK:]
            dv_ref[cur, cols] += dv_kv[BLOCK:]

            @pl.when(n > 0)
            def _(cols=cols, dk_kv=dk_kv, dv_kv=dv_kv):
                dk_ref[prev, cols] += dk_kv[:BLOCK]
                dv_ref[prev, cols] += dv_kv[:BLOCK]

    return pl.pallas_call(
        body, grid=(S // BLOCK,),
        in_specs=_swa_in_specs() + [pl.BlockSpec((BLOCK, SWA_WIDTH), lambda n: (n, 0)),
                                    pl.BlockSpec(memory_space=pl.ANY)],
        out_specs=[pl.BlockSpec((BLOCK, SWA_WIDTH), lambda n: (n, P_SQ // SWA_WIDTH)), _full((S, SWA_KVW)),
                   _full((S, SWA_KVW)), _full((1, SWA_DIM)), _full((1, SWA_DIM)), _full((1, SWA_HEADS)),
                   _full((SWA_HEADS, BLOCK, 2 * BLOCK))],
        out_shape=[jax.ShapeDtypeStruct(dproj.shape, dproj.dtype), jax.ShapeDtypeStruct((S, SWA_KVW), F32),
                   jax.ShapeDtypeStruct((S, SWA_KVW), F32), jax.ShapeDtypeStruct((1, SWA_DIM), F32),
                   jax.ShapeDtypeStruct((1, SWA_DIM), F32), jax.ShapeDtypeStruct((1, SWA_HEADS), F32),
                   jax.ShapeDtypeStruct((SWA_HEADS, BLOCK, 2 * BLOCK), F32)],
        input_output_aliases={10: 0},
        name="swa_bwd", compiler_params=_params(("arbitrary",)),
    )(proj, proj, proj, proj, proj, q_gain, k_gain, sinks, bias, dy, dproj)


def _kv_into(dproj, dk, dv, tm=1024):
    S = dk.shape[0]

    def body(dk_ref, dv_ref, _, o_ref):
        o_ref[:, :SWA_KVW] = dk_ref[...].astype(BF16)
        o_ref[:, SWA_KVW:] = dv_ref[...].astype(BF16)

    return pl.pallas_call(
        body, grid=(S // tm,), in_specs=[_row(tm, SWA_KVW), _row(tm, SWA_KVW), pl.BlockSpec(memory_space=pl.ANY)],
        out_specs=_row(tm, 2 * SWA_KVW, P_SK // (2 * SWA_KVW)),
        out_shape=jax.ShapeDtypeStruct(dproj.shape, dproj.dtype), input_output_aliases={2: 0},
        name="swa_kv_into", compiler_params=_params(("parallel",)))(dk, dv, dproj)


def _position():
    return lax.axis_index("x"), lax.axis_index("y"), lax.axis_index("c")


_HBM = pl.BlockSpec(memory_space=pltpu.HBM)
_SEM = pl.BlockSpec(memory_space=pltpu.SEMAPHORE)
_DATAFLOW = pltpu.SideEffectType.DATAFLOW_SIDE_EFFECTING


def _two_level_copies(x_refs, out_refs, send_sems, recv_sems):
    x, y, c = _position()
    me, sibling = (x, y, c), (x, y, 1 - c)
    chips = [(1 - x, y), (x, 1 - y), (1 - x, 1 - y)]

    def copy(a, k, block, to, own=False):
        px, py, pc = block
        slot = out_refs[a].at[4 * px + 2 * py + pc]
        return pltpu.make_async_remote_copy(
            src_ref=x_refs[a] if own else slot, dst_ref=slot, send_sem=send_sems.at[7 * a + k],
            recv_sem=recv_sems.at[7 * a + k], device_id=to, device_id_type=MESH_ID)

    return copy, me, sibling, chips


def _all_gather_start(shards, name):
    na = len(shards)
    lands = [lax.empty((N_DEV,) + s.shape, s.dtype) for s in shards]

    def body(*refs):
        x_refs, out_refs = refs[:na], refs[na:2 * na]
        send_sems, recv_sems = refs[2 * na], refs[2 * na + 1]
        token = refs[-1]
        copy, me, sibling, chips = _two_level_copies(x_refs, out_refs, send_sems, recv_sems)
        for a in range(na):
            copy(a, 0, me, sibling, own=True).start()
            for j, chip in enumerate(chips):
                copy(a, 1 + j, me, (*chip, me[2]), own=True).start()
        token[...] = jnp.zeros_like(token)

    hbm = lambda a: pltpu.HBM(a.shape, a.dtype)
    out = pl.pallas_call(
        body, name=name,
        out_shape=(pltpu.SemaphoreType.DMA((7 * na,)), pltpu.SemaphoreType.DMA((7 * na,)),
                   *[hbm(s) for s in shards], *[hbm(l) for l in lands], jax.ShapeDtypeStruct((8, LANES), F32)),
        in_specs=[_HBM] * (2 * na),
        out_specs=(_SEM, _SEM, *[_HBM] * (2 * na), pl.BlockSpec(memory_space=pltpu.VMEM)),
        input_output_aliases={i: 2 + i for i in range(2 * na)},
        compiler_params=pltpu.CompilerParams(has_side_effects=_DATAFLOW),
    )(*[pltpu.with_memory_space_constraint(s, pltpu.HBM) for s in shards],
      *[pltpu.with_memory_space_constraint(l, pltpu.HBM) for l in lands])
    return (out[0], out[1], list(out[2:2 + na]), list(out[2 + na:2 + 2 * na])), out[-1]


def _all_gather_finish(handle, after, name):
    send_sems, recv_sems, srcs, lands = handle
    na = len(srcs)

    def body(*refs):
        x_refs, out_refs = refs[:na], refs[na:2 * na]
        copy, me, sibling, chips = _two_level_copies(x_refs, out_refs, refs[2 * na], refs[2 * na + 1])
        c = me[2]
        for a in range(na):
            copy(a, 0, sibling, me).wait_recv()
            copy(a, 0, me, sibling, own=True).wait_send()
            for j, chip in enumerate(chips):
                copy(a, 1 + j, (*chip, c), me).wait_recv()
                copy(a, 1 + j, me, (*chip, c), own=True).wait_send()

    hbm = lambda a: pltpu.HBM(a.shape, a.dtype)
    out = pl.pallas_call(
        body, name=name, out_shape=(*[hbm(s) for s in srcs], *[hbm(l) for l in lands]),
        in_specs=[_HBM] * (2 * na) + [_SEM, _SEM] + [pl.BlockSpec(memory_space=pl.ANY)] * len(after),
        out_specs=tuple([_HBM] * (2 * na)), input_output_aliases={i: i for i in range(2 * na)},
        compiler_params=pltpu.CompilerParams(has_side_effects=_DATAFLOW),
    )(*srcs, *lands, send_sems, recv_sems, *after)
    lands = _all_gather_relay(list(out[na:]), name + "_relay")
    return [_own_slot(land, src) for land, src in zip(lands, out[:na])]


def _all_gather_relay(lands, name):
    na = len(lands)

    def body(*refs):
        out_refs = refs[na:2 * na]
        send_sems, recv_sems = refs[2 * na:]
        x, y, c = _position()
        chips = [(1 - x, y), (x, 1 - y), (1 - x, 1 - y)]

        def copy(a, j, core):
            px, py = chips[j]
            slot = out_refs[a].at[4 * px + 2 * py + core]
            return pltpu.make_async_remote_copy(
                src_ref=slot, dst_ref=slot, send_sem=send_sems.at[3 * a + j], recv_sem=recv_sems.at[3 * a + j],
                device_id=(x, y, 1 - c), device_id_type=MESH_ID)

        sends = [copy(a, j, c) for a in range(na) for j in range(3)]
        for cp in sends:
            cp.start()
        for a in range(na):
            for j in range(3):
                copy(a, j, 1 - c).wait_recv()
        for cp in sends:
            cp.wait_send()

    return pl.pallas_call(
        body, in_specs=[pl.BlockSpec(memory_space=pl.ANY)] * na, out_specs=[pl.BlockSpec(memory_space=pl.ANY)] * na,
        out_shape=[jax.ShapeDtypeStruct(l.shape, l.dtype) for l in lands],
        input_output_aliases={i: i for i in range(na)},
        scratch_shapes=[pltpu.SemaphoreType.DMA((3 * na,)), pltpu.SemaphoreType.DMA((3 * na,))],
        name=name)(*lands)


def _peers(x, y, c):
    out = []
    for k in range(1, N_DEV):
        px, py, pc = x ^ (k >> 2), y ^ ((k >> 1) & 1), c ^ (k & 1)
        out.append(((px, py, pc), 4 * px + 2 * py + pc))
    return out


def _split_copies(src_refs, land_refs, send_sems, recv_sems, scatter):
    x, y, c = _position()
    me = 4 * x + 2 * y + c
    sends, recvs = [], []
    for k, (peer_id, peer) in enumerate(_peers(x, y, c)):
        for a, (src, land) in enumerate(zip(src_refs, land_refs)):
            sems = dict(send_sem=send_sems.at[7 * a + k], recv_sem=recv_sems.at[7 * a + k],
                        device_id=peer_id, device_id_type=MESH_ID)
            mine = src.at[peer] if scatter else src
            sends.append(pltpu.make_async_remote_copy(src_ref=mine, dst_ref=land.at[me], **sems))
            recvs.append(pltpu.make_async_remote_copy(src_ref=mine, dst_ref=land.at[peer], **sems))
    return sends, recvs


def _all_gather_direct(shards, name, after):
    na, nb = len(shards), len(after)

    def body(*refs):
        x_refs, out_refs = refs[:na], refs[na + nb:2 * na + nb]
        send_sems, recv_sems, local_sems = refs[2 * na + nb:]
        x, y, c = _position()
        me = 4 * x + 2 * y + c
        local = [pltpu.make_async_copy(x_refs[a], out_refs[a].at[me], local_sems.at[a]) for a in range(na)]
        sends, recvs = _split_copies(x_refs, out_refs, send_sems, recv_sems, False)
        for cp in local + sends:
            cp.start()
        for cp in recvs:
            cp.wait_recv()
        for cp in sends:
            cp.wait_send()
        for cp in local:
            cp.wait()

    return pl.pallas_call(
        body, in_specs=[pl.BlockSpec(memory_space=pl.ANY)] * (na + nb),
        out_specs=[pl.BlockSpec(memory_space=pl.ANY)] * na,
        out_shape=[jax.ShapeDtypeStruct((N_DEV,) + s.shape, s.dtype) for s in shards],
        scratch_shapes=[pltpu.SemaphoreType.DMA((7 * na,)), pltpu.SemaphoreType.DMA((7 * na,)),
                        pltpu.SemaphoreType.DMA((na,))],
        name=name)(*shards, *after)


def _exchange_start(srcs, scatter, name, after=None):
    na = len(srcs)
    lands = [lax.empty(s.shape if scatter else (N_DEV,) + s.shape, s.dtype) for s in srcs]
    extra = [] if after is None else [after]

    def body(*refs):
        src_refs, land_refs = refs[:na], refs[na:2 * na]
        send_sems, recv_sems = refs[2 * na + len(extra)], refs[2 * na + len(extra) + 1]
        token = refs[-1]
        sends, _ = _split_copies(src_refs, land_refs, send_sems, recv_sems, scatter)
        for cp in sends:
            cp.start()
        token[...] = jnp.zeros_like(token)

    hbm = lambda a: pltpu.HBM(a.shape, a.dtype)
    out = pl.pallas_call(
        body, name=name,
        out_shape=(pltpu.SemaphoreType.DMA((7 * na,)), pltpu.SemaphoreType.DMA((7 * na,)),
                   *[hbm(s) for s in srcs], *[hbm(l) for l in lands], jax.ShapeDtypeStruct((8, LANES), F32)),
        in_specs=[_HBM] * (2 * na) + [pl.BlockSpec(memory_space=pl.ANY)] * len(extra),
        out_specs=(_SEM, _SEM, *[_HBM] * (2 * na), pl.BlockSpec(memory_space=pltpu.VMEM)),
        input_output_aliases={i: 2 + i for i in range(2 * na)},
        compiler_params=pltpu.CompilerParams(has_side_effects=_DATAFLOW),
    )(*[pltpu.with_memory_space_constraint(s, pltpu.HBM) for s in srcs],
      *[pltpu.with_memory_space_constraint(l, pltpu.HBM) for l in lands], *extra)
    return (out[0], out[1], list(out[2:2 + na]), list(out[2 + na:2 + 2 * na])), out[-1]


def _exchange_wait(handle, after, scatter, name):
    send_sems, recv_sems, srcs, lands = handle
    na = len(srcs)

    def body(*refs):
        src_refs, land_refs = refs[:na], refs[na:2 * na]
        s_sems, r_sems = refs[2 * na], refs[2 * na + 1]
        sends, recvs = _split_copies(src_refs, land_refs, s_sems, r_sems, scatter)
        for cp in sends:
            cp.wait_send()
        for cp in recvs:
            cp.wait_recv()

    hbm = lambda a: pltpu.HBM(a.shape, a.dtype)
    out = pl.pallas_call(
        body, name=name, out_shape=(*[hbm(s) for s in srcs], *[hbm(l) for l in lands]),
        in_specs=[_HBM] * (2 * na) + [_SEM, _SEM, pl.BlockSpec(memory_space=pl.ANY)],
        out_specs=tuple([_HBM] * (2 * na)), input_output_aliases={i: i for i in range(2 * na)},
        compiler_params=pltpu.CompilerParams(has_side_effects=_DATAFLOW),
    )(*srcs, *lands, send_sems, recv_sems, after)
    return list(out[:na]), list(out[na:])


def _own_slot(landed, own):
    me = 4 * lax.axis_index("x") + 2 * lax.axis_index("y") + lax.axis_index("c")
    return lax.dynamic_update_slice_in_dim(landed, own[None], me, axis=0)


def _adam_update(parts, w, m, v, name, tr=256, turned=False):
    _, r, c = w.shape
    tr = _pick_rows(r, tr)
    cp = parts.shape[2]
    flat = turned and c % 8 != 0
    at = (slice(None), 0) if flat else (0,)

    def body(p_ref, w_ref, m_ref, v_ref, g_ref, d_ref, nm_ref, nv_ref):
        cols = pl.ds(0, cp if turned else c)
        g = p_ref[0, :, cols].astype(F32)
        for i in range(1, N_DEV):
            g = g + p_ref[i, :, cols].astype(F32)
        if turned:
            g = g.T[:c]
        delta, nm, nv = _adamw(w_ref[at], g, m_ref[at], v_ref[at])
        g_ref[at] = g
        d_ref[at] = delta
        nm_ref[at] = nm
        nv_ref[at] = nv

    there, back = ((2, 0, 1), (1, 2, 0)) if flat else ((0, 2, 1), (0, 2, 1))
    if turned:
        w, m, v = (jnp.transpose(a, there) for a in (w, m, v))
        rs = pl.BlockSpec((c, 1, tr), lambda i: (0, 0, i)) if flat else pl.BlockSpec((1, c, tr), lambda i: (0, 0, i))
    else:
        rs = pl.BlockSpec((1, tr, c), lambda i: (0, i, 0))
    outs = pl.pallas_call(
        body, grid=(r // tr,), in_specs=[pl.BlockSpec((N_DEV, tr, cp), lambda i: (0, i, 0)), rs, rs, rs],
        out_specs=[rs] * 4, out_shape=[jax.ShapeDtypeStruct(w.shape, F32)] * 4, name=name,
        compiler_params=_params(("parallel",)))(parts, w, m, v)
    return [*([jnp.transpose(o, back) for o in outs] if turned else outs), outs[0]]


def _pick_rows(rows, target):
    if rows <= target:
        return rows
    t = target
    while t >= 16:
        if rows % t == 0:
            return t
        t -= 16
    return rows


BIG = ("w_in", "w_branch_dn", "w_branch_swa", "w_out", "w_gate", "w_up", "w_down")
IN_SHARD, IN_WIRE = D_IN // N_DEV, 640
FF_SHARD, FF_WIRE = D_FF // N_DEV, 384
D_FFP = N_DEV * FF_WIRE
BIG_SHAPES = {"w_in": ((D_MODEL, IN_SHARD), (D_MODEL, IN_WIRE)),
              "w_branch_dn": ((DN_WIDTH, LANES), (DN_WIDTH, LANES)),
              "w_branch_swa": ((SWA_WIDTH, LANES), (SWA_WIDTH, LANES)),
              "w_out": ((LANES, D_MODEL), (LANES, D_MODEL)),
              "w_gate": ((D_MODEL, FF_SHARD), (D_MODEL, FF_WIRE)),
              "w_up": ((D_MODEL, FF_SHARD), (D_MODEL, FF_WIRE)),
              "w_down": ((FF_SHARD, D_MODEL), (FF_WIRE, D_MODEL))}
CONV_SHARD, CONV_WIRE = (DN_CONV, DN_QKV // N_DEV), (8, 256)


def _pad_to(a, shape):
    return jnp.pad(a, [(0, t - s) for s, t in zip(a.shape, shape)])


IN_TILE_ROWS = 256
_IN_SEGS = ((R_GATE, 2048, P_GATE), (R_QKV, DN_QKV, P_QKV), (R_Z, DN_WIDTH, P_Z), (R_SQ, SWA_WIDTH, P_SQ),
            (R_SK, SWA_KVW, P_SK), (R_SV, SWA_KVW, P_SV), (R_B, 8, P_BA))


def _w_in_from_blocks(blocks, after):
    tm = IN_TILE_ROWS

    def body(b_ref, _, o_ref):
        parts = []
        for rs, n, _ in _IN_SEGS:
            for dev in range(N_DEV):
                lo, hi = max(rs, IN_SHARD * dev), min(rs + n, IN_SHARD * (dev + 1))
                if lo < hi:
                    parts.append(b_ref[dev][:, lo - IN_SHARD * dev:hi - IN_SHARD * dev])
        parts.append(jnp.zeros((tm, P_WIDTH - P_BA - 8), b_ref.dtype))
        o_ref[...] = jnp.concatenate(parts, axis=1)

    return pl.pallas_call(
        body, grid=(D_MODEL // tm,),
        in_specs=[pl.BlockSpec((N_DEV, tm, IN_WIRE), lambda i: (0, i, 0)), pl.BlockSpec(memory_space=pl.ANY)],
        out_specs=pl.BlockSpec((tm, P_WIDTH), lambda i: (i, 0)),
        out_shape=jax.ShapeDtypeStruct((D_MODEL, P_WIDTH), blocks.dtype), name="w_in_from_blocks",
        compiler_params=_params(("parallel",)))(blocks, after)


def _w_in_to_blocks(g):
    tm = IN_TILE_ROWS

    def body(g_ref, o_ref):
        for dev in range(N_DEV):
            parts = []
            for rs, n, ps in sorted(_IN_SEGS):
                lo, hi = max(rs, IN_SHARD * dev), min(rs + n, IN_SHARD * (dev + 1))
                if lo < hi:
                    parts.append(g_ref[:, ps + lo - rs:ps + hi - rs])
            parts.append(jnp.zeros((tm, IN_WIRE - IN_SHARD), g_ref.dtype))
            o_ref[dev] = jnp.concatenate(parts, axis=1)

    return pl.pallas_call(
        body, grid=(D_MODEL // tm,), in_specs=[pl.BlockSpec((tm, P_WIDTH), lambda i: (i, 0))],
        out_specs=pl.BlockSpec((N_DEV, tm, IN_WIRE), lambda i: (0, i, 0)),
        out_shape=jax.ShapeDtypeStruct((N_DEV, D_MODEL, IN_WIRE), g.dtype), name="w_in_to_blocks",
        compiler_params=_params(("parallel",)))(g)


SMALL = {"attn_norm": (0, (1, D_MODEL)), "ffn_norm": (1, (1, D_MODEL)), "dn_out_norm": (2, (1, DN_DIM)),
         "swa_q_norm": (3, (1, SWA_DIM)), "swa_k_norm": (4, (1, SWA_DIM)), "dn_a_log": (5, (1, DN_HEADS)),
         "dn_dt_bias": (6, (1, DN_HEADS)), "swa_sinks": (7, (1, SWA_HEADS)), "rel_bias": (8, (REL_BUCKETS, SWA_HEADS))}
SMALL_SHEET = (48, D_MODEL)


LOSS_ROW = 40


def _small_pack(grads, loss_local):
    names = list(SMALL)

    def body(*refs):
        o_ref = refs[-1]
        o_ref[...] = jnp.zeros_like(o_ref)
        for n, ref in zip(names, refs):
            r0, (nr, nc) = SMALL[n]
            o_ref[r0:r0 + nr, 0:nc] = ref[...]
        o_ref[LOSS_ROW:LOSS_ROW + 1, 0:1] = refs[len(names)][...]

    return pl.pallas_call(
        body, in_specs=[pl.BlockSpec(memory_space=pltpu.VMEM)] * (len(names) + 1),
        out_specs=pl.BlockSpec(memory_space=pltpu.VMEM), out_shape=jax.ShapeDtypeStruct(SMALL_SHEET, F32),
        name="small_pack", compiler_params=_params())(*[grads[n].reshape(SMALL[n][1]) for n in names], loss_local)


def _small_update(sheets, w, m, v):
    names = list(SMALL)
    k = len(names)

    def body(*refs):
        p_ref = refs[0]
        ins, outs = refs[1:1 + 3 * k], refs[1 + 3 * k:]
        loss = p_ref[0, LOSS_ROW:LOSS_ROW + 1, 0:1]
        for i in range(1, N_DEV):
            loss = loss + p_ref[i, LOSS_ROW:LOSS_ROW + 1, 0:1]
        outs[4 * k][...] = loss
        for t, n in enumerate(names):
            r0, (nr, nc) = SMALL[n]
            g = p_ref[0, r0:r0 + nr, 0:nc]
            for i in range(1, N_DEV):
                g = g + p_ref[i, r0:r0 + nr, 0:nc]
            delta, nm, nv = _adamw(ins[t][...], g, ins[k + t][...], ins[2 * k + t][...])
            for kind, val in enumerate((g, delta, nm, nv)):
                outs[kind * k + t][...] = val

    shapes = [jax.ShapeDtypeStruct(SMALL[n][1], F32) for n in names]
    vm = pl.BlockSpec(memory_space=pltpu.VMEM)
    res = pl.pallas_call(
        body, in_specs=[vm] * (1 + 3 * k), out_specs=[vm] * (4 * k + 1),
        out_shape=shapes * 4 + [jax.ShapeDtypeStruct((1, 1), F32)], name="adam_small", compiler_params=_params(),
    )(sheets, *[d[n].reshape(SMALL[n][1]) for d in (w, m, v) for n in names])
    return {n: tuple(res[kind * k + t] for kind in range(4)) for t, n in enumerate(names)}, res[4 * k]


def kernel(x, attn_norm, w_in, dn_conv, dn_a_log, dn_dt_bias, dn_out_norm, swa_q_norm, swa_k_norm, swa_sinks, rel_bias, w_branch_dn, w_branch_swa, w_out, ffn_norm, w_gate, w_up, w_down, loss_target, m_attn_norm, m_w_in, m_dn_conv, m_dn_a_log, m_dn_dt_bias, m_dn_out_norm, m_swa_q_norm, m_swa_k_norm, m_swa_sinks, m_rel_bias, m_w_branch_dn, m_w_branch_swa, m_w_out, m_ffn_norm, m_w_gate, m_w_up, m_w_down, v_attn_norm, v_w_in, v_dn_conv, v_dn_a_log, v_dn_dt_bias, v_dn_out_norm, v_swa_q_norm, v_swa_k_norm, v_swa_sinks, v_rel_bias, v_w_branch_dn, v_w_branch_swa, v_w_out, v_ffn_norm, v_w_gate, v_w_up, v_w_down):
    args = dict(locals())
    S = x.shape[1]
    xs = x.reshape(S, D_MODEL)
    target = loss_target.reshape(S, D_MODEL)

    w_loc = {n: args[n].reshape(BIG_SHAPES[n][0]) for n in BIG}
    conv_loc = dn_conv.reshape(CONV_SHARD)
    def on_wire(n, zero=0.0):
        return _pad_to(w_loc[n] + zero, BIG_SHAPES[n][1]).astype(BF16)

    first_handle, first_token = _all_gather_start([on_wire("w_in"), _pad_to(conv_loc, CONV_WIRE)],
                                                  "all_gather_weights_start")
    zero = first_token[0, 0]
    h = _norm_fwd(xs, attn_norm + zero, "norm1_fwd")
    later = [n for n in BIG if n != "w_in"]
    wire = [on_wire(n, zero) for n in later]
    bias = _bias_fwd(rel_bias + zero)
    first = _all_gather_finish(first_handle, [h, bias] + wire, "all_gather_weights_finish")
    rest_handle, rest_token = _exchange_start(wire, False, "gather_rest_start", after=first[1])
    w_pad = _w_in_from_blocks(first[0], rest_token)
    conv_w = jnp.concatenate([first[1][d, :DN_CONV, :CONV_SHARD[1]] for d in range(N_DEV)], axis=1)

    proj = _mm([(h, w_pad)], "nn", F32, "mm_in", 1024, 1664, j_outer=True)
    qkvn, conv_out = _dn_conv_fwd(proj, conv_w)
    beta, g = _dn_gate_fwd(proj, dn_a_log, dn_dt_bias)
    u, w, qe, kd, qk, egl, tinv = _dn_prep_fwd(qkvn, g, beta)
    o, states = _dn_scan_fwd(u, w, qe, kd, qk, egl)
    y_dn = _dn_out_fwd(o, proj, dn_out_norm)
    y_swa = _swa_fwd(proj, swa_q_norm, swa_k_norm, swa_sinks, bias)
    rest_src, rest_land = _exchange_wait(rest_handle, y_swa, False, "gather_rest_wait")
    G = {n: _own_slot(land, src) for n, src, land in zip(later, rest_src, rest_land)}
    w_bdn, w_bswa, w_g, w_u = G["w_branch_dn"], G["w_branch_swa"], G["w_gate"], G["w_up"]
    w_o = G["w_out"].reshape(D_MODEL, D_MODEL)
    w_d = G["w_down"].reshape(D_FFP, D_MODEL)
    gates = [(proj, P_GATE // 512), (proj, (P_GATE + D_MODEL) // 512)]
    a_dn, a_swa, merged = _mm_fused(
        [(y_dn, w_bdn), (y_swa, w_bswa)], "nn", "mm_branch_merge", 1024, 512,
        lambda p, e: (p[0], p[1], _merge(e[0], e[1], p[0], p[1])), gates, (F32, F32, BF16), b_blocks=True)

    def resid_norm(p, e):
        x1 = e[0] + p[0]
        return x1, _rms(x1, e[1])

    x1, h2 = _mm_fused([(merged, w_o)], "nn", "mm_out_norm", 512, D_MODEL, resid_norm,
                       [(xs, 0), (ffn_norm, None)], (F32, BF16))
    gate, up, act = _mm_fused([(h2, w_g), (h2, w_u)], "nn", "mm_gate_up_act", 1024, 768,
                              lambda p, e: (p[0], p[1], _act(p[0], p[1])), [], (F32, F32, BF16),
                              j_outer=True, b_blocks=True)

    def loss_head(p, e):
        diff = e[0] + p[0] - e[1]
        dy = diff * (1.0 / D_MODEL)
        part = jnp.sum(jnp.mean(diff * diff, axis=-1, keepdims=True), axis=0, keepdims=True) * 0.5
        return dy, dy, part

    dy, dy_b, loss_local = _mm_fused([(act, w_d)], "nn", "mm_down_loss", 512, D_MODEL, loss_head,
                                     [(x1, 0), (target, 0)], (F32, BF16), sum_shape=(1, 1))

    def act_bwd(p, e):
        sig = jax.nn.sigmoid(e[0])
        a = e[0] * sig
        return p[0] * e[1] * (sig + a * (1.0 - sig)), p[0] * a

    dgate, dup = _mm_fused([(dy_b, w_d)], "nt", "mm_dact_act", 1024, 768, act_bwd, [(gate, 0), (up, 0)],
                           (BF16, BF16), j_outer=True)
    g_w_down = _mm([(act, dy_b)], "tn", BF16, "mm_dw_down", 768, D_MODEL, j_outer=True)
    g_w_down = g_w_down.reshape(N_DEV, FF_WIRE, D_MODEL)
    g_w_gate = _mm([(h2, dgate)], "tn", BF16, "mm_dw_gate", D_MODEL, 768, out_blocks=True)
    g_w_up = _mm([(h2, dup)], "tn", BF16, "mm_dw_up", D_MODEL, 768, out_blocks=True)
    ffn_handle, ffn_token = _exchange_start([g_w_down, g_w_gate, g_w_up], True, "scatter_ffn_start")

    def norm_bwd(p, e):
        _, vjp = jax.vjp(_rms, e[0], e[2])
        dx, dgain = vjp(sum(p))
        dx = dx + e[1]
        return dx, dx, dgain

    dx1, dx1_b, g_ffn_norm = _mm_fused(
        [(dgate, w_g), (dup, w_u)], "nt", "mm_dh2_norm", 256, D_MODEL, norm_bwd,
        [(x1, 0), (dy, 0), (ffn_norm + ffn_token[0, 0], None)], (F32, BF16), b_blocks=True, sum_shape=(1, D_MODEL))
    def merge_bwd(p, e):
        _, vjp = jax.vjp(_merge, *e)
        dg0, dg1, da_dn, da_swa = vjp(p[0])
        return jnp.concatenate([dg0, dg1], axis=1), da_dn, da_swa

    dproj, da_dn, da_swa = _mm_fused(
        [(dx1_b, w_o)], "nt", "mm_dmerged_merge", 512, D_MODEL, merge_bwd,
        [(proj, P_GATE // D_MODEL), (proj, P_GATE // D_MODEL + 1), (a_dn, 0), (a_swa, 0)], (BF16,) * 3,
        wide_first=(P_WIDTH, 2 * D_MODEL))
    g_w_out = _mm([(merged, dx1_b)], "tn", BF16, "mm_dw_out", 512, D_MODEL, j_outer=True)
    g_w_out = g_w_out.reshape(N_DEV, LANES, D_MODEL)
    dy_dn = _mm([(da_dn, w_bdn)], "nt", F32, "mm_dy_dn", 1024, DN_WIDTH, b_blocks=True)
    dy_swa = _mm([(da_swa, w_bswa)], "nt", F32, "mm_dy_swa", 1024, SWA_WIDTH, b_blocks=True)
    g_w_bdn = _mm([(y_dn, da_dn)], "tn", BF16, "mm_dw_branch_dn", DN_WIDTH, 512, out_blocks=True)
    g_w_bswa = _mm([(y_swa, da_swa)], "tn", BF16, "mm_dw_branch_swa", SWA_WIDTH, 512, out_blocks=True)
    dproj, dsk, dsv, g_q_norm, g_k_norm, g_sinks, dbias = _swa_bwd(proj, swa_q_norm, swa_k_norm, swa_sinks, bias,
                                                                   dy_swa, dproj)
    dproj = _kv_into(dproj, dsk, dsv)
    g_rel_bias = _bias_bwd(dbias)[:, :REL_BUCKETS].T
    mix_handle, mix_token = _exchange_start([g_w_out, g_w_bdn, g_w_bswa], True, "scatter_mix_start")
    do, dproj, g_out_norm = _dn_out_bwd(o, proj, dn_out_norm + mix_token[0, 0], dy_dn, dproj)
    du, dw, dqe, dkd, dqk, degl = _dn_scan_bwd(u, w, qe, kd, qk, egl, states, do)
    dqkvn, dgd, dbeta = _dn_prep_bwd(qkvn, g, beta, tinv, du, dw, dqe, dkd, dqk, degl)
    dproj, dal, ddt = _dn_gate_bwd(proj, dn_a_log, dn_dt_bias, dbeta, dgd, dproj)
    g_a_log = dal.reshape(DN_HEADS, DN_DIM).sum(axis=1)
    g_dt_bias = ddt[0, DN_HEADS:2 * DN_HEADS]
    dproj, g_conv = _dn_conv_bwd(proj, conv_w, conv_out, dqkvn, dproj)
    g_w_in = _w_in_to_blocks(_mm([(h, dproj)], "tn", BF16, "mm_dw_in", 512, 1664, j_outer=True))
    in_handle, in_token = _exchange_start([g_w_in], True, "scatter_in_start")
    dx, g_attn_norm = _mm_fused(
        [(dproj, w_pad)], "nt", "mm_dh_norm", 512, D_MODEL, lambda p, e: norm_bwd(p, e)[1:],
        [(xs, 0), (dx1, 0), (attn_norm + in_token[0, 0], None)], (F32,), sum_shape=(1, D_MODEL))

    g_small = {"attn_norm": g_attn_norm, "ffn_norm": g_ffn_norm, "rel_bias": g_rel_bias, "dn_out_norm": g_out_norm,
               "swa_q_norm": g_q_norm, "swa_k_norm": g_k_norm, "dn_a_log": g_a_log, "dn_dt_bias": g_dt_bias,
               "swa_sinks": g_sinks}
    me = 4 * lax.axis_index("x") + 2 * lax.axis_index("y") + lax.axis_index("c")
    outs = {}

    def finish(handle, group, name, after):
        srcs, lands = _exchange_wait(handle, after, True, name)
        for n, src, land in zip(group, srcs, lands):
            parts = _own_slot(land, lax.dynamic_index_in_dim(src, me, 0, keepdims=False))
            outs[n] = _adam_update(parts, args[n], args["m_" + n], args["v_" + n], "adam_" + n,
                                   turned=args[n].shape[2] % LANES != 0)

    finish(ffn_handle, ("w_down", "w_gate", "w_up"), "scatter_ffn_wait", dx)
    finish(mix_handle, ("w_out", "w_branch_dn", "w_branch_swa"), "scatter_mix_wait", dx)
    sheets, conv_all = _all_gather_direct([_small_pack(g_small, loss_local), _pad_to(g_conv, (8, DN_QKV))],
                                          "all_gather_small",
                                          after=[outs[n][4] for n in sorted(outs)])
    finish(in_handle, ("w_in",), "scatter_in_wait", sheets)
    conv_parts = lax.dynamic_slice(conv_all, (0, 0, me * CONV_SHARD[1]), (N_DEV,) + CONV_SHARD)
    outs["dn_conv"] = _adam_update(conv_parts, dn_conv, m_dn_conv, v_dn_conv, "adam_dn_conv")
    small_outs, loss = _small_update(sheets, {n: args[n] for n in SMALL}, {n: args["m_" + n] for n in SMALL},
                                     {n: args["v_" + n] for n in SMALL})
    outs.update(small_outs)

    names = ("attn_norm", "w_in", "dn_conv", "dn_a_log", "dn_dt_bias", "dn_out_norm", "swa_q_norm", "swa_k_norm",
             "swa_sinks", "rel_bias", "w_branch_dn", "w_branch_swa", "w_out", "ffn_norm", "w_gate", "w_up", "w_down")
    results = []
    for kind in range(4):
        results += [outs[n][kind].reshape(args[n].shape) for n in names]

    return (loss.reshape(()), dx.reshape(x.shape), *results)
```

```python
import math

import numpy as np
import jax
import jax.numpy as jnp
from jax import lax
from jax.experimental import pallas as pl
from jax.experimental.pallas import tpu as pltpu

F32 = jnp.float32
BF16 = jnp.bfloat16
HI = lax.Precision.HIGHEST

D_MODEL = 1024
DN_HEADS = 4
DN_DIM = 128
DN_WIDTH = 512
DN_QKV = 1536
DN_CONV = 4
CHUNK = 64
SWA_HEADS = 8
SWA_KV = 2
SWA_GROUP = 4
SWA_DIM = 64
SWA_WIDTH = 512
SWA_KVW = 128
WINDOW = 128
BLOCK = 128
REL_BUCKETS = 32
REL_MAX_DIST = 128
D_FF = 2816
D_IN = 4872
EPS = 1e-6
N_DEV = 8

ADAM_LR = 0.001
ADAM_B1 = 0.9
ADAM_B2 = 0.999
ADAM_EPS = 1e-08
ADAM_WD = 0.01
ADAM_STEP = 10

P_GATE, P_QKV, P_Z, P_SQ, P_SK, P_SV, P_BA = 0, 2048, 3584, 4096, 4608, 4736, 4864
P_WIDTH = 4992
R_QKV, R_Z, R_B, R_A, R_SQ, R_SK, R_SV, R_GATE = 0, 1536, 2048, 2052, 2056, 2568, 2696, 2824

VMEM_LIMIT = 56 * 1024 * 1024
LANES = 128
MESH_ID = pl.DeviceIdType.MESH


def _params(sem=None):
    return pltpu.CompilerParams(dimension_semantics=sem, vmem_limit_bytes=VMEM_LIMIT)


def _pick(dim, target):
    if dim <= target:
        return dim
    t = target - target % LANES
    while t >= LANES:
        if dim % t == 0:
            return t
        t -= LANES
    return dim


_DIMS = {"nn": (((1,), (0,)), ((), ())), "nt": (((1,), (1,)), ((), ())), "tn": (((0,), (0,)), ((), ()))}


def _tile_product(a_ref, b_ref, mode, b_blocks):
    a = a_ref[...].astype(BF16)
    b = jnp.concatenate([b_ref[d] for d in range(b_ref.shape[0])], axis=1) if b_blocks else b_ref[...]
    return lax.dot_general(a, b.astype(BF16), _DIMS[mode], preferred_element_type=F32)


def _mm(pairs, mode, out_dtype, name, bm, bn, j_outer=False, b_blocks=False, out_blocks=False):
    a0, b0 = pairs[0]
    cb = b0.shape[2] if b_blocks else None
    b_shape = (b0.shape[1], N_DEV * cb) if b_blocks else b0.shape
    if mode == "nn":
        (M, K), (K2, N) = a0.shape, b_shape
    elif mode == "nt":
        (M, K), (N, K2) = a0.shape, b_shape
    else:
        (K, M), (K2, N) = a0.shape, b_shape
    bm, bn = min(bm, M), min(bn, N)
    assert K == K2 and M % bm == 0 and N % bn == 0, (name, a0.shape, b0.shape, bm, bn)
    co = N // N_DEV
    assert not out_blocks or bn % co == 0
    dims = _DIMS[mode]
    n = len(pairs)

    def body(*refs):
        o_ref = refs[2 * n]
        acc = None
        for t in range(n):
            p = _tile_product(refs[2 * t], refs[2 * t + 1], mode, b_blocks)
            acc = p if acc is None else acc + p
        if out_blocks:
            for d in range(bn // co):
                o_ref[d] = acc[:, d * co:(d + 1) * co].astype(out_dtype)
        else:
            o_ref[...] = acc.astype(out_dtype)

    def ij(f):
        return (lambda j, i: f(i, j)) if j_outer else f

    a_spec = pl.BlockSpec((K, bm), ij(lambda i, j: (0, i))) if mode == "tn" else pl.BlockSpec((bm, K), ij(lambda i, j: (i, 0)))
    if b_blocks and mode == "nt":
        b_spec = pl.BlockSpec((N_DEV, bn, cb), ij(lambda i, j: (0, j, 0)))
    elif b_blocks:
        b_spec = pl.BlockSpec((bn // cb, K, cb), ij(lambda i, j: (j, 0, 0)))
    elif mode == "nt":
        b_spec = pl.BlockSpec((bn, K), ij(lambda i, j: (j, 0)))
    else:
        b_spec = pl.BlockSpec((K, bn), ij(lambda i, j: (0, j)))
    if out_blocks:
        out_spec = pl.BlockSpec((bn // co, bm, co), ij(lambda i, j: (j, i, 0)))
        out_shape = jax.ShapeDtypeStruct((N_DEV, M, co), out_dtype)
    else:
        out_spec = pl.BlockSpec((bm, bn), ij(lambda i, j: (i, j)))
        out_shape = jax.ShapeDtypeStruct((M, N), out_dtype)
    grid = (N // bn, M // bm) if j_outer else (M // bm, N // bn)
    return pl.pallas_call(
        body, grid=grid, in_specs=[a_spec, b_spec] * n, out_specs=out_spec, out_shape=out_shape, name=name,
        compiler_params=_params(("parallel", "parallel")),
    )(*[x for pair in pairs for x in pair])


def _mm_fused(pairs, mode, name, bm, bn, epilogue, extras, out_dtypes, j_outer=False, b_blocks=False,
              sum_shape=None, wide_first=None):
    a0, b0 = pairs[0]
    cb = b0.shape[2] if b_blocks else None
    b_shape = (b0.shape[1], N_DEV * cb) if b_blocks else b0.shape
    if mode == "nn":
        (M, K), (K2, N) = a0.shape, b_shape
    else:
        (M, K), (N, K2) = a0.shape, b_shape
    bm, bn = min(bm, M), min(bn, N)
    assert mode in ("nn", "nt") and K == K2 and M % bm == 0 and N % bn == 0, (name, a0.shape, b0.shape)
    dims = _DIMS[mode]
    n, ne, no = len(pairs), len(extras), len(out_dtypes)

    def body(*refs):
        prods = [_tile_product(refs[2 * t], refs[2 * t + 1], mode, b_blocks) for t in range(n)]
        results = epilogue(prods, [r[...] for r in refs[2 * n:2 * n + ne]])
        out_refs = refs[2 * n + ne:]
        for o_ref, val, dt in zip(out_refs, results, out_dtypes):
            o_ref[...] = val.astype(dt)
        if sum_shape is not None:
            s_ref = out_refs[no]

            @pl.when((pl.program_id(0) == 0) & (pl.program_id(1) == 0))
            def _():
                s_ref[...] = jnp.zeros_like(s_ref)

            s_ref[...] += results[no]

    def ij(f):
        return (lambda j, i: f(i, j)) if j_outer else f

    a_spec = pl.BlockSpec((bm, K), ij(lambda i, j: (i, 0)))
    once = dict(pipeline_mode=pl.Buffered(1)) if bn == N else {}
    if b_blocks and mode == "nt":
        b_spec = pl.BlockSpec((N_DEV, bn, cb), ij(lambda i, j: (0, j, 0)), **once)
    elif b_blocks:
        b_spec = pl.BlockSpec((bn // cb, K, cb), ij(lambda i, j: (j, 0, 0)), **once)
    elif mode == "nt":
        b_spec = pl.BlockSpec((bn, K), ij(lambda i, j: (j, 0)), **once)
    else:
        b_spec = pl.BlockSpec((K, bn), ij(lambda i, j: (0, j)), **once)
    e_specs = [pl.BlockSpec((1, bn), ij(lambda i, j: (0, j))) if first is None
               else pl.BlockSpec((bm, bn), ij(lambda i, j, first=first: (i, first + j))) for _, first in extras]
    tile = pl.BlockSpec((bm, bn), ij(lambda i, j: (i, j)))
    out_specs = [tile] * no
    out_shape = [jax.ShapeDtypeStruct((M, N), dt) for dt in out_dtypes]
    if wide_first is not None:
        assert bn == N
        out_specs[0] = pl.BlockSpec((bm, wide_first[1]), ij(lambda i, j: (i, 0)))
        out_shape[0] = jax.ShapeDtypeStruct((M, wide_first[0]), out_dtypes[0])
    if sum_shape is not None:
        assert sum_shape[1] in (1, bn) and (sum_shape[1] == 1 or bn == N)
        out_specs.append(_full(sum_shape))
        out_shape.append(jax.ShapeDtypeStruct(sum_shape, F32))
    grid = (N // bn, M // bm) if j_outer else (M // bm, N // bn)
    sem = ("arbitrary", "arbitrary") if sum_shape is not None else ("parallel", "parallel")
    return pl.pallas_call(
        body, grid=grid, in_specs=[a_spec, b_spec] * n + e_specs, out_specs=out_specs, out_shape=out_shape,
        name=name, compiler_params=_params(sem),
    )(*[x for pair in pairs for x in pair], *[arr for arr, _ in extras])


def _rms(x, gain):
    return x * lax.rsqrt(jnp.mean(x * x, axis=-1, keepdims=True) + EPS) * gain


def _silu(x):
    return x * jax.nn.sigmoid(x)


def _act(g, u):
    return _silu(g) * u


def _merge(g0, g1, a_dn, a_swa):
    return jax.nn.sigmoid(g0) * a_dn + jax.nn.sigmoid(g1) * a_swa


def _dn_post(c, is_v, q_scale):
    a = _silu(c)
    rs = lax.rsqrt(jnp.sum(a * a, axis=-1, keepdims=True) + EPS) * q_scale
    return a * jnp.where(is_v, 1.0, rs)


def _dn_post_bwd(c, d, is_v, q_scale):
    sig = jax.nn.sigmoid(c)
    a = c * sig
    u = lax.rsqrt(jnp.sum(a * a, axis=-1, keepdims=True) + EPS)
    f = jnp.where(is_v, 1.0, u * q_scale)
    k = jnp.where(is_v, 0.0, jnp.sum(d * a, axis=-1, keepdims=True) * (u * u * u) * q_scale)
    return (f * d - a * k) * (sig + a * (1.0 - sig))


def _dn_out(o, z, gain):
    return _rms(o, gain) * _silu(z)


def _dot(a, b, dims=_DIMS["nn"], hi=False):
    if a.ndim == 3 or b.ndim == 3:
        batch = a.shape[0] if a.ndim == 3 else b.shape[0]
        a = a if a.ndim == 3 else jnp.broadcast_to(a, (batch,) + a.shape)
        b = b if b.ndim == 3 else jnp.broadcast_to(b, (batch,) + b.shape)
        ((ca,), (cb,)), _ = dims
        dims = (((ca + 1,), (cb + 1,)), ((0,), (0,)))
    if hi:
        return lax.dot_general(a, b, dims, precision=HI, preferred_element_type=F32)
    return lax.dot_general(a.astype(BF16), b.astype(BF16), dims, preferred_element_type=F32)


def _pieces(x):
    hi = x.astype(BF16)
    r1 = x - hi.astype(F32)
    mid = r1.astype(BF16)
    return hi, mid, (r1 - mid.astype(F32)).astype(BF16)


def _sel_left_impl(m, x):
    mb = m.astype(BF16)
    hi, mid, lo = _pieces(x)
    return _dot(mb, hi) + (_dot(mb, mid) + _dot(mb, lo))


@jax.custom_vjp
def _sel_left(m, mt, x):
    return _sel_left_impl(m, x)


_sel_left.defvjp(lambda m, mt, x: (_sel_left_impl(m, x), (m, mt)),
                 lambda res, ct: (jnp.zeros_like(res[0]), jnp.zeros_like(res[1]), _sel_left_impl(res[1], ct)))


def _sel_right_impl(x, s):
    sb = s.astype(BF16)
    hi, mid, lo = _pieces(x)
    return _dot(hi, sb) + (_dot(mid, sb) + _dot(lo, sb))


@jax.custom_vjp
def _sel_right(x, s, st):
    return _sel_right_impl(x, s)


_sel_right.defvjp(lambda x, s, st: (_sel_right_impl(x, s), (s, st)),
                  lambda res, ct: (_sel_right_impl(ct, res[1]), jnp.zeros_like(res[0]), jnp.zeros_like(res[1])))


def _dot3_impl(a, b):
    a_hi, a_lo, _ = _pieces(a)
    b_hi, b_lo, _ = _pieces(b)
    return _dot(a_hi, b_hi) + (_dot(a_hi, b_lo) + _dot(a_lo, b_hi))


@jax.custom_vjp
def _dot3(a, b):
    return _dot3_impl(a, b)


_dot3.defvjp(lambda a, b: (_dot3_impl(a, b), (a, b)),
             lambda res, ct: (_dot(ct, res[1], _DIMS["nt"]), _dot(res[0], ct, _DIMS["tn"])))


def _inv_impl(a, eye, strict):
    t = eye - a
    p = _dot(a, a)
    for level in range(5):
        t = t + _dot(t, p)
        if level < 4:
            p = _dot(p, p)
    t = t + _dot(t, eye - t - _dot3_impl(a, t))
    return jnp.where(strict > 0.5, t, eye)


@jax.custom_vjp
def _inv_given(a, t):
    return t.astype(F32)


_inv_given.defvjp(lambda a, t: (t.astype(F32), t),
                  lambda t, ct: (-_dot(_dot(t, ct, _DIMS["tn"]), t, _DIMS["nt"]), jnp.zeros_like(t)))


@jax.custom_vjp
def _lanes_join(a, b):
    return jnp.concatenate([a, b], axis=-1)


_lanes_join.defvjp(lambda a, b: (jnp.concatenate([a, b], axis=-1), None),
                   lambda _, ct: (ct[..., :ct.shape[-1] // 2], ct[..., ct.shape[-1] // 2:]))


@jax.custom_vjp
def _lanes_halves(y):
    h = y.shape[-1] // 2
    return y[..., :h], y[..., h:]


_lanes_halves.defvjp(lambda y: ((y[..., :y.shape[-1] // 2], y[..., y.shape[-1] // 2:]), None),
                     lambda _, ct: (jnp.concatenate(ct, axis=-1),))

GROUP = 4
GROUP_ROWS = GROUP * CHUNK


def _block_consts(n):
    ii = lax.broadcasted_iota(jnp.int32, (n, n), 0)
    jj = lax.broadcasted_iota(jnp.int32, (n, n), 1)
    shift = CHUNK.bit_length() - 1
    same = jnp.right_shift(ii, shift) == jnp.right_shift(jj, shift)
    return same & (ii >= jj), same & (ii <= jj), same & (ii > jj), same, ii == jj


def _lane0(n):
    s = (lax.broadcasted_iota(jnp.int32, (LANES, n), 0) == 0).astype(F32)
    st = (lax.broadcasted_iota(jnp.int32, (n, LANES), 1) == 0).astype(F32)
    return s, st


def _dn_group(q, k, v, g, beta, t_saved=None):
    n = GROUP_ROWS
    low_b, upp_b, strict_b, _, eye_b = _block_consts(n)
    low, upp, eye = low_b.astype(F32), upp_b.astype(F32), eye_b.astype(F32)
    gc = _sel_left(low, upp, g)
    per_chunk = (g.shape[0], GROUP, CHUNK, LANES)
    g_last = jnp.sum(g.reshape(per_chunk), axis=2, keepdims=True)
    gl = jnp.broadcast_to(g_last, per_chunk).reshape(g.shape)
    s, st = _lane0(n)
    col = _sel_right(gc, s, st)
    row = jnp.swapaxes(col, 1, 2)
    decay = jnp.exp(jnp.where(low_b, col - row, -jnp.inf))
    kb = k * beta
    vb = v * beta
    a = jnp.where(strict_b, _dot(kb, k, _DIMS["nt"]) * decay, 0.0)
    t = _inv_impl(a, eye, strict_b.astype(F32)) if t_saved is None else _inv_given(a, t_saved)
    u, w = _lanes_halves(_dot3(t, _lanes_join(vb, kb * jnp.exp(gc))))
    fold = (jnp.bitwise_and(lax.broadcasted_iota(jnp.int32, (n, CHUNK), 0), CHUNK - 1)
            == lax.broadcasted_iota(jnp.int32, (n, CHUNK), 1)).astype(F32)
    fold_t = (jnp.bitwise_and(lax.broadcasted_iota(jnp.int32, (CHUNK, n), 1), CHUNK - 1)
              == lax.broadcasted_iota(jnp.int32, (CHUNK, n), 0)).astype(F32)
    qk = _sel_right(_dot(q, k, _DIMS["nt"]) * decay, fold, fold_t)
    return u, w, q * jnp.exp(gc), k * jnp.exp(gl - gc), qk, jnp.exp(g_last), t


def _dn_step(s, u, w, qe, kd, qk, egl):
    v_new = u - _dot(w, s)
    o = _dot(qe, s) + _dot(qk, v_new)
    s_new = s * egl + _dot(kd, v_new, _DIMS["tn"])
    return s_new, o


def _swa_block(q, kband, vband, qg, kg, sinks, band):
    kn = _rms(kband, kg)
    qn = _rms(q, qg) * (SWA_DIM ** -0.5)
    logits = _dot(qn, kn, _DIMS["nt"]) + band
    m = lax.stop_gradient(jnp.maximum(jnp.max(logits, axis=-1, keepdims=True), sinks))
    p = jnp.exp(logits - m)
    denom = jnp.sum(p, axis=-1, keepdims=True) + jnp.exp(sinks - m)
    return _dot(p * (1.0 / denom), vband)


def _adamw(w, g, m, v):
    m = ADAM_B1 * m + (1.0 - ADAM_B1) * g
    v = ADAM_B2 * v + (1.0 - ADAM_B2) * jnp.square(g)
    m_hat = m / (1.0 - ADAM_B1 ** ADAM_STEP)
    v_hat = v / (1.0 - ADAM_B2 ** ADAM_STEP)
    delta = -ADAM_LR * (m_hat / (jnp.sqrt(v_hat) + ADAM_EPS) + ADAM_WD * w)
    return delta, m, v


def _row(tm, c, cb=0):
    return pl.BlockSpec((tm, c), lambda i, cb=cb: (i, cb))


def _full(shape):
    nd = len(shape)
    return pl.BlockSpec(shape, lambda *_, nd=nd: (0,) * nd)


def _norm_fwd(x, gain, name, tm=1024):
    S = x.shape[0]

    def body(x_ref, g_ref, h_ref):
        h_ref[...] = _rms(x_ref[...], g_ref[...]).astype(BF16)

    return pl.pallas_call(
        body, grid=(S // tm,), in_specs=[_row(tm, D_MODEL), _full((1, D_MODEL))],
        out_specs=_row(tm, D_MODEL), out_shape=jax.ShapeDtypeStruct((S, D_MODEL), BF16),
        name=name, compiler_params=_params(("parallel",)))(x, gain)


def _shift_down(x, s):
    row = lax.broadcasted_iota(jnp.int32, x.shape, 0)
    return jnp.where(row >= s, pltpu.roll(x, s, axis=0), 0.0)


def _shift_up(x, s):
    n = x.shape[0]
    row = lax.broadcasted_iota(jnp.int32, x.shape, 0)
    return jnp.where(row < n - s, pltpu.roll(x, n - s, axis=0), 0.0)


def _conv(x, w):
    out = w[DN_CONV - 1:DN_CONV] * x
    for s in range(1, DN_CONV):
        out = out + w[DN_CONV - 1 - s:DN_CONV - s] * _shift_down(x, s)
    return out


def _dn_conv_fwd(proj, conv_w):
    S = proj.shape[0]
    nb = DN_QKV // LANES

    def body(x_ref, w_ref, o_ref, c_ref):
        j = pl.program_id(0)
        q_scale = jnp.where(j < DN_HEADS, DN_DIM ** -0.5, 1.0).astype(F32)
        c = _conv(x_ref[...], w_ref[...])
        c_ref[...] = c
        o_ref[...] = _dn_post(c, j >= 2 * DN_HEADS, q_scale)

    col = pl.BlockSpec((S, LANES), lambda j: (0, j))
    return pl.pallas_call(
        body, grid=(nb,),
        in_specs=[pl.BlockSpec((S, LANES), lambda j: (0, P_QKV // LANES + j)),
                  pl.BlockSpec((DN_CONV, LANES), lambda j: (0, j))],
        out_specs=[col, col], out_shape=[jax.ShapeDtypeStruct((S, DN_QKV), F32)] * 2, name="dn_conv_fwd",
        compiler_params=_params(("parallel",)))(proj, conv_w)


def _dn_conv_bwd(proj, conv_w, conv_out, dqkvn, dproj):
    S = proj.shape[0]
    nb = DN_QKV // LANES

    def body(x_ref, w_ref, c_ref, d_ref, _, dx_ref, dw_ref):
        j = pl.program_id(0)
        q_scale = jnp.where(j < DN_HEADS, DN_DIM ** -0.5, 1.0).astype(F32)
        x = x_ref[...]
        w = w_ref[...]
        dc = _dn_post_bwd(c_ref[...], d_ref[0], j >= 2 * DN_HEADS, q_scale)
        dx = w[DN_CONV - 1:DN_CONV] * dc
        dw_ref[DN_CONV - 1:DN_CONV, :] = jnp.sum(dc * x, axis=0, keepdims=True)
        for s in range(1, DN_CONV):
            up = _shift_up(dc, s)
            dx = dx + w[DN_CONV - 1 - s:DN_CONV - s] * up
            dw_ref[DN_CONV - 1 - s:DN_CONV - s, :] = jnp.sum(up * x, axis=0, keepdims=True)
        dx_ref[...] = dx.astype(BF16)

    return pl.pallas_call(
        body, grid=(nb,),
        in_specs=[pl.BlockSpec((S, LANES), lambda j: (0, P_QKV // LANES + j)),
                  pl.BlockSpec((DN_CONV, LANES), lambda j: (0, j)),
                  pl.BlockSpec((S, LANES), lambda j: (0, j)),
                  pl.BlockSpec((1, S, LANES), lambda j: (lax.div(j, DN_HEADS), 0, lax.rem(j, DN_HEADS))),
                  pl.BlockSpec(memory_space=pl.ANY)],
        out_specs=[pl.BlockSpec((S, LANES), lambda j: (0, P_QKV // LANES + j)),
                   pl.BlockSpec((DN_CONV, LANES), lambda j: (0, j))],
        out_shape=[jax.ShapeDtypeStruct(dproj.shape, dproj.dtype), jax.ShapeDtypeStruct((DN_CONV, DN_QKV), F32)],
        input_output_aliases={4: 0},
        name="dn_conv_bwd", compiler_params=_params(("parallel",)))(proj, conv_w, conv_out, dqkvn, dproj)


def _expanders():
    eb = np.zeros((LANES, DN_WIDTH), np.float32)
    ea = np.zeros((LANES, DN_WIDTH), np.float32)
    for h in range(DN_HEADS):
        eb[h, h * DN_DIM:(h + 1) * DN_DIM] = 1.0
        ea[DN_HEADS + h, h * DN_DIM:(h + 1) * DN_DIM] = 1.0
    return jnp.asarray(eb), jnp.asarray(ea), jnp.asarray(eb.T), jnp.asarray(ea.T)


def _dn_gate_args(a_log, dt_bias):
    alog = jnp.repeat(a_log.reshape(1, DN_HEADS), DN_DIM, axis=1)
    dtb = _pad_to(jnp.pad(dt_bias.reshape(1, DN_HEADS), ((0, 0), (DN_HEADS, 0))), (1, LANES))
    return _expanders() + (alog, dtb)


def _dn_gate_specs(tm):
    return [_row(tm, LANES, P_BA // LANES), _full((LANES, DN_WIDTH)), _full((LANES, DN_WIDTH)),
            _full((DN_WIDTH, LANES)), _full((DN_WIDTH, LANES)), _full((1, DN_WIDTH)), _full((1, LANES))]


def _dn_gate_fn(ba, eb, ea, ebt, eat, alog, dtb):
    beta = _sel_right(jax.nn.sigmoid(ba), eb, ebt)
    g = -jnp.exp(alog) * _sel_right(jax.nn.softplus(ba + dtb), ea, eat)
    return beta, g


def _dn_gate_fwd(proj, a_log, dt_bias, tm=1024):
    S = proj.shape[0]
    args = _dn_gate_args(a_log, dt_bias)

    def body(ba_ref, eb_ref, ea_ref, ebt_ref, eat_ref, al_ref, dt_ref, beta_ref, g_ref):
        beta, g = _dn_gate_fn(ba_ref[...], eb_ref[...], ea_ref[...], ebt_ref[...], eat_ref[...], al_ref[...],
                              dt_ref[...])
        beta_ref[...] = beta
        g_ref[...] = g

    return pl.pallas_call(
        body, grid=(S // tm,), in_specs=_dn_gate_specs(tm), out_specs=[_row(tm, DN_WIDTH), _row(tm, DN_WIDTH)],
        out_shape=[jax.ShapeDtypeStruct((S, DN_WIDTH), F32), jax.ShapeDtypeStruct((S, DN_WIDTH), F32)],
        name="dn_gate_fwd", compiler_params=_params(("parallel",)))(proj, *args)


def _dn_gate_bwd(proj, a_log, dt_bias, dbeta, dg, dproj, tm=1024):
    S = proj.shape[0]
    args = _dn_gate_args(a_log, dt_bias)

    def body(ba_ref, eb_ref, ea_ref, ebt_ref, eat_ref, al_ref, dt_ref, dbeta_ref, dg_ref, _, dba_ref, dal_ref,
             ddt_ref):
        eb, ea, ebt, eat = eb_ref[...], ea_ref[...], ebt_ref[...], eat_ref[...]
        _, vjp = jax.vjp(lambda ba, al, dt: _dn_gate_fn(ba, eb, ea, ebt, eat, al, dt), ba_ref[...], al_ref[...],
                         dt_ref[...])
        dba, dal, ddt = vjp((dbeta_ref[...], dg_ref[...]))
        dba_ref[...] = dba.astype(BF16)

        @pl.when(pl.program_id(0) == 0)
        def _():
            dal_ref[...] = jnp.zeros_like(dal_ref)
            ddt_ref[...] = jnp.zeros_like(ddt_ref)

        dal_ref[...] += dal
        ddt_ref[...] += ddt

    return pl.pallas_call(
        body, grid=(S // tm,),
        in_specs=_dn_gate_specs(tm) + [_row(tm, DN_WIDTH), _row(tm, DN_WIDTH), pl.BlockSpec(memory_space=pl.ANY)],
        out_specs=[_row(tm, LANES, P_BA // LANES), _full((1, DN_WIDTH)), _full((1, LANES))],
        out_shape=[jax.ShapeDtypeStruct(dproj.shape, dproj.dtype), jax.ShapeDtypeStruct((1, DN_WIDTH), F32),
                   jax.ShapeDtypeStruct((1, LANES), F32)],
        input_output_aliases={len(args) + 3: 0},
        name="dn_gate_bwd", compiler_params=_params(("arbitrary",)))(proj, *args, dbeta, dg, dproj)


PREP_GROUPS = 8
PREP_CHUNKS = GROUP * PREP_GROUPS


def _dn_prep_specs():
    rows = PREP_CHUNKS * CHUNK
    q = pl.BlockSpec((rows, LANES), lambda h, c: (c, h))
    k = pl.BlockSpec((rows, LANES), lambda h, c: (c, DN_HEADS + h))
    v = pl.BlockSpec((rows, LANES), lambda h, c: (c, 2 * DN_HEADS + h))
    qk = pl.BlockSpec((1, rows, CHUNK), lambda h, c: (h, c, 0))
    egl = pl.BlockSpec((1, PREP_CHUNKS, 1, LANES), lambda h, c: (h, c, 0, 0))
    return q, k, v, qk, egl


def _dn_prep_fwd(qkvn, g, beta):
    S = qkvn.shape[0]
    nc = S // CHUNK
    q, k, v, qks, egl = _dn_prep_specs()

    def body(q_ref, k_ref, v_ref, g_ref, b_ref, u_ref, w_ref, qe_ref, kd_ref, qk_ref, egl_ref, t_ref):
        rows = PREP_CHUNKS * CHUNK
        grp = (PREP_GROUPS, GROUP_ROWS, LANES)
        u, w, qe, kd, qk, e, t = _dn_group(q_ref[...].reshape(grp), k_ref[...].reshape(grp), v_ref[...].reshape(grp),
                                           g_ref[...].reshape(grp), b_ref[...].reshape(grp))
        u_ref[...] = u.reshape(rows, LANES)
        w_ref[...] = w.reshape(rows, LANES)
        qe_ref[...] = qe.reshape(rows, LANES)
        kd_ref[...] = kd.reshape(rows, LANES)
        t_ref[0] = t.reshape(rows, GROUP_ROWS).astype(BF16)
        qk_ref[0] = qk.reshape(rows, CHUNK)
        egl_ref[0] = e.reshape(PREP_CHUNKS, 1, LANES)

    wide = jax.ShapeDtypeStruct((S, DN_WIDTH), F32)
    return pl.pallas_call(
        body, grid=(DN_HEADS, nc // PREP_CHUNKS), in_specs=[q, k, v, q, q],
        out_specs=[q, q, q, q, qks, egl, _dn_tinv_spec()],
        out_shape=[wide, wide, wide, wide, jax.ShapeDtypeStruct((DN_HEADS, S, CHUNK), F32),
                   jax.ShapeDtypeStruct((DN_HEADS, nc, 1, LANES), F32),
                   jax.ShapeDtypeStruct((DN_HEADS, S, GROUP_ROWS), BF16)],
        name="dn_prep_fwd", compiler_params=_params(("parallel", "parallel")))(qkvn, qkvn, qkvn, g, beta)


def _dn_tinv_spec():
    return pl.BlockSpec((1, PREP_CHUNKS * CHUNK, GROUP_ROWS), lambda h, c: (h, c, 0))


def _dn_prep_bwd(qkvn, g, beta, tinv, du, dw, dqe, dkd, dqk, degl):
    S = qkvn.shape[0]
    nc = S // CHUNK
    q, k, v, qks, egl = _dn_prep_specs()

    def body(q_ref, k_ref, v_ref, g_ref, b_ref, t_ref, du_ref, dw_ref, dqe_ref, dkd_ref, dqk_ref, degl_ref,
             dqkv_ref, dg_ref, db_ref):
        rows = PREP_CHUNKS * CHUNK
        grp = (PREP_GROUPS, GROUP_ROWS, LANES)
        t_saved = t_ref[0].reshape(PREP_GROUPS, GROUP_ROWS, GROUP_ROWS)
        _, vjp = jax.vjp(lambda *x: _dn_group(*x, t_saved=t_saved)[:6], q_ref[...].reshape(grp),
                         k_ref[...].reshape(grp), v_ref[...].reshape(grp), g_ref[...].reshape(grp),
                         b_ref[...].reshape(grp))
        dq, dk, dv, dg, db = vjp((du_ref[...].reshape(grp), dw_ref[...].reshape(grp), dqe_ref[...].reshape(grp),
                                  dkd_ref[...].reshape(grp), dqk_ref[0].reshape(PREP_GROUPS, GROUP_ROWS, CHUNK),
                                  degl_ref[0].reshape(PREP_GROUPS, GROUP, 1, LANES)))
        dqkv_ref[0] = dq.reshape(rows, LANES)
        dqkv_ref[1] = dk.reshape(rows, LANES)
        dqkv_ref[2] = dv.reshape(rows, LANES)
        dg_ref[...] = dg.reshape(rows, LANES)
        db_ref[...] = db.reshape(rows, LANES)

    wide = jax.ShapeDtypeStruct((S, DN_WIDTH), F32)
    rows = PREP_CHUNKS * CHUNK
    return pl.pallas_call(
        body, grid=(DN_HEADS, nc // PREP_CHUNKS), in_specs=[q, k, v, q, q, _dn_tinv_spec(), q, q, q, q, qks, egl],
        out_specs=[pl.BlockSpec((3, rows, LANES), lambda h, c: (0, c, h)), q, q],
        out_shape=[jax.ShapeDtypeStruct((3, S, DN_WIDTH), F32), wide, wide],
        name="dn_prep_bwd", compiler_params=_params(("parallel", "parallel")),
    )(qkvn, qkvn, qkvn, g, beta, tinv, du, dw, dqe, dkd, dqk, degl)


SCAN_CHUNKS = 16


def _dn_scan_specs(nc, reverse):
    nb = nc // SCAN_CHUNKS

    def cidx(c):
        return nb - 1 - c if reverse else c

    hc = pl.BlockSpec((SCAN_CHUNKS * CHUNK, DN_WIDTH), lambda c: (cidx(c), 0))
    qk = pl.BlockSpec((DN_HEADS, SCAN_CHUNKS * CHUNK, CHUNK), lambda c: (0, cidx(c), 0))
    egl = pl.BlockSpec((DN_HEADS, SCAN_CHUNKS, 1, LANES), lambda c: (0, cidx(c), 0, 0))
    st = pl.BlockSpec((DN_HEADS, SCAN_CHUNKS, DN_DIM, DN_DIM), lambda c: (0, cidx(c), 0, 0))
    return hc, qk, egl, st


def _heads(ref, i):
    return jnp.stack([ref[pl.ds(i * CHUNK, CHUNK), pl.ds(h * DN_DIM, DN_DIM)] for h in range(DN_HEADS)])


def _dn_scan_fwd(u, w, qe, kd, qk, egl):
    S = u.shape[0]
    nc = S // CHUNK
    hc, qks, egls, st = _dn_scan_specs(nc, False)

    def body(u_ref, w_ref, qe_ref, kd_ref, qk_ref, egl_ref, o_ref, st_ref, s_scr):
        @pl.when(pl.program_id(0) == 0)
        def _():
            s_scr[...] = jnp.zeros_like(s_scr)

        s = s_scr[...]
        for i in range(SCAN_CHUNKS):
            rows = pl.ds(i * CHUNK, CHUNK)
            st_ref[:, i] = s
            s, o = _dn_step(s, _heads(u_ref, i), _heads(w_ref, i), _heads(qe_ref, i), _heads(kd_ref, i),
                            qk_ref[:, rows, :], egl_ref[:, i])
            for h in range(DN_HEADS):
                o_ref[rows, pl.ds(h * DN_DIM, DN_DIM)] = o[h]
        s_scr[...] = s

    return pl.pallas_call(
        body, grid=(nc // SCAN_CHUNKS,), in_specs=[hc, hc, hc, hc, qks, egls], out_specs=[hc, st],
        out_shape=[jax.ShapeDtypeStruct((S, DN_WIDTH), F32), jax.ShapeDtypeStruct((DN_HEADS, nc, DN_DIM, DN_DIM), F32)],
        scratch_shapes=[pltpu.VMEM((DN_HEADS, DN_DIM, DN_DIM), F32)], name="dn_scan_fwd",
        compiler_params=_params(("arbitrary",)))(u, w, qe, kd, qk, egl)


def _dn_scan_bwd(u, w, qe, kd, qk, egl, states, do):
    S = u.shape[0]
    nc = S // CHUNK
    hc, qks, egls, st = _dn_scan_specs(nc, True)

    def body(u_ref, w_ref, qe_ref, kd_ref, qk_ref, egl_ref, st_ref, do_ref,
             du_ref, dw_ref, dqe_ref, dkd_ref, dqk_ref, degl_ref, ds_scr):
        @pl.when(pl.program_id(0) == 0)
        def _():
            ds_scr[...] = jnp.zeros_like(ds_scr)

        ds = ds_scr[...]
        for i in reversed(range(SCAN_CHUNKS)):
            rows = pl.ds(i * CHUNK, CHUNK)
            _, vjp = jax.vjp(_dn_step, st_ref[:, i], _heads(u_ref, i), _heads(w_ref, i), _heads(qe_ref, i),
                             _heads(kd_ref, i), qk_ref[:, rows, :], egl_ref[:, i])
            ds, du, dw, dqe, dkd, dqk, degl = vjp((ds, _heads(do_ref, i)))
            dqk_ref[:, rows, :] = dqk
            degl_ref[:, i] = degl
            for h in range(DN_HEADS):
                cols = pl.ds(h * DN_DIM, DN_DIM)
                du_ref[rows, cols] = du[h]
                dw_ref[rows, cols] = dw[h]
                dqe_ref[rows, cols] = dqe[h]
                dkd_ref[rows, cols] = dkd[h]
        ds_scr[...] = ds

    wide = jax.ShapeDtypeStruct((S, DN_WIDTH), F32)
    return pl.pallas_call(
        body, grid=(nc // SCAN_CHUNKS,), in_specs=[hc, hc, hc, hc, qks, egls, st, hc],
        out_specs=[hc, hc, hc, hc, qks, egls],
        out_shape=[wide, wide, wide, wide, jax.ShapeDtypeStruct((DN_HEADS, S, CHUNK), F32),
                   jax.ShapeDtypeStruct((DN_HEADS, nc, 1, LANES), F32)],
        scratch_shapes=[pltpu.VMEM((DN_HEADS, DN_DIM, DN_DIM), F32)], name="dn_scan_bwd",
        compiler_params=_params(("arbitrary",)))(u, w, qe, kd, qk, egl, states, do)


def _dn_out_fwd(o, proj, gain, tm=1024):
    S = o.shape[0]

    def body(o_ref, z_ref, g_ref, y_ref):
        y_ref[...] = _dn_out(o_ref[...], z_ref[...], g_ref[...]).astype(BF16)

    hs = pl.BlockSpec((tm, LANES), lambda i, h: (i, h))
    zs = pl.BlockSpec((tm, LANES), lambda i, h: (i, P_Z // LANES + h))
    return pl.pallas_call(
        body, grid=(S // tm, DN_HEADS), in_specs=[hs, zs, _full((1, DN_DIM))], out_specs=hs,
        out_shape=jax.ShapeDtypeStruct((S, DN_WIDTH), BF16), name="dn_out_fwd",
        compiler_params=_params(("parallel", "parallel")))(o, proj, gain)


_ANY = pl.BlockSpec(memory_space=pl.ANY)


def _dn_out_bwd(o, proj, gain, dy, dproj, tm=1024):
    S = o.shape[0]

    def body(o_ref, z_ref, g_ref, dy_ref, _, do_ref, dz_ref, dg_ref):
        _, vjp = jax.vjp(_dn_out, o_ref[...], z_ref[...], g_ref[...])
        do, dz, dg = vjp(dy_ref[...])
        do_ref[...] = do
        dz_ref[...] = dz.astype(BF16)

        @pl.when((pl.program_id(0) == 0) & (pl.program_id(1) == 0))
        def _():
            dg_ref[...] = jnp.zeros_like(dg_ref)

        dg_ref[...] += dg

    hs = pl.BlockSpec((tm, LANES), lambda i, h: (i, h))
    zs = pl.BlockSpec((tm, LANES), lambda i, h: (i, P_Z // LANES + h))
    return pl.pallas_call(
        body, grid=(S // tm, DN_HEADS), in_specs=[hs, zs, _full((1, DN_DIM)), hs, _ANY],
        out_specs=[hs, zs, _full((1, DN_DIM))],
        out_shape=[jax.ShapeDtypeStruct((S, DN_WIDTH), F32), jax.ShapeDtypeStruct(dproj.shape, dproj.dtype),
                   jax.ShapeDtypeStruct((1, DN_DIM), F32)],
        input_output_aliases={4: 1},
        name="dn_out_bwd", compiler_params=_params(("arbitrary", "arbitrary")))(o, proj, gain, dy, dproj)


def _rel_buckets():
    qi = np.arange(BLOCK)[:, None]
    kj = np.arange(2 * BLOCK)[None, :]
    n = np.maximum(BLOCK + qi - kj, 0)
    max_exact = REL_BUCKETS // 2
    nf = np.maximum(n, 1).astype(np.float32)
    large = max_exact + (np.log(nf / np.float32(max_exact)) / np.float32(math.log(REL_MAX_DIST / max_exact))
                         * np.float32(REL_BUCKETS - max_exact)).astype(np.int32)
    large = np.minimum(large, REL_BUCKETS - 1)
    return np.where(n < max_exact, n, large).astype(np.int32)


def _bias_fwd(rel_bias):
    buckets = jnp.asarray(_rel_buckets())

    def body(rb_ref, bk_ref, o_ref):
        bk = bk_ref[...]
        for h in range(SWA_HEADS):
            acc = jnp.zeros((BLOCK, 2 * BLOCK), F32)
            for b in range(REL_BUCKETS):
                acc = jnp.where(bk == b, rb_ref[b, h], acc)
            for first in range(2):
                o_ref[first, h] = jnp.where(_swa_mask(1 - first), acc, -jnp.inf)

    return pl.pallas_call(
        body, in_specs=[pl.BlockSpec(memory_space=pltpu.SMEM), pl.BlockSpec(memory_space=pltpu.VMEM)],
        out_specs=pl.BlockSpec(memory_space=pltpu.VMEM),
        out_shape=jax.ShapeDtypeStruct((2, SWA_HEADS, BLOCK, 2 * BLOCK), F32), name="swa_bias_fwd",
        compiler_params=_params())(rel_bias, buckets)


def _bias_bwd(dbias):
    buckets = jnp.asarray(_rel_buckets())

    def body(d_ref, bk_ref, o_ref):
        bk = bk_ref[...]
        lane = lax.broadcasted_iota(jnp.int32, (1, LANES), 1)
        for h in range(SWA_HEADS):
            d = d_ref[h]
            row = jnp.zeros((1, LANES), F32)
            for b in range(REL_BUCKETS):
                part = jnp.sum(jnp.where(bk == b, d, 0.0), axis=1, keepdims=True)
                row = jnp.where(lane == b, jnp.sum(part, axis=0, keepdims=True), row)
            o_ref[h:h + 1, :] = row

    return pl.pallas_call(
        body, in_specs=[pl.BlockSpec(memory_space=pltpu.VMEM), pl.BlockSpec(memory_space=pltpu.VMEM)],
        out_specs=pl.BlockSpec(memory_space=pltpu.VMEM),
        out_shape=jax.ShapeDtypeStruct((SWA_HEADS, LANES), F32), name="swa_bias_bwd",
        compiler_params=_params())(dbias, buckets)


def _swa_mask(n):
    qi = lax.broadcasted_iota(jnp.int32, (BLOCK, 2 * BLOCK), 0)
    kj = lax.broadcasted_iota(jnp.int32, (BLOCK, 2 * BLOCK), 1)
    dist = BLOCK + qi - kj
    return (dist >= 0) & (dist < WINDOW) & ((n > 0) | (kj >= BLOCK))


def _swa_in_specs():
    q = pl.BlockSpec((BLOCK, SWA_WIDTH), lambda n: (n, P_SQ // SWA_WIDTH))
    kc = pl.BlockSpec((BLOCK, SWA_KVW), lambda n: (n, P_SK // SWA_KVW))
    kp = pl.BlockSpec((BLOCK, SWA_KVW), lambda n: (jnp.maximum(n - 1, 0), P_SK // SWA_KVW))
    vc = pl.BlockSpec((BLOCK, SWA_KVW), lambda n: (n, P_SV // SWA_KVW))
    vp = pl.BlockSpec((BLOCK, SWA_KVW), lambda n: (jnp.maximum(n - 1, 0), P_SV // SWA_KVW))
    band = pl.BlockSpec((None, SWA_HEADS, BLOCK, 2 * BLOCK), lambda n: (jnp.where(n == 0, 1, 0), 0, 0, 0))
    small = [_full((1, SWA_DIM)), _full((1, SWA_DIM)), _full((1, SWA_HEADS)), band]
    return [q, kp, kc, vp, vc] + small


def _swa_load(q_ref, kp_ref, kc_ref, vp_ref, vc_ref, s_ref):
    q = jnp.stack([q_ref[:, pl.ds(h * SWA_DIM, SWA_DIM)] for h in range(SWA_HEADS)])
    kbands, vbands = [], []
    for kv in range(SWA_KV):
        cols = pl.ds(kv * SWA_DIM, SWA_DIM)
        kbands += [jnp.concatenate([kp_ref[:, cols], kc_ref[:, cols]], axis=0)] * SWA_GROUP
        vbands += [jnp.concatenate([vp_ref[:, cols], vc_ref[:, cols]], axis=0)] * SWA_GROUP
    sinks = jnp.stack([s_ref[:, pl.ds(h, 1)] for h in range(SWA_HEADS)])
    return q, jnp.stack(kbands), jnp.stack(vbands), sinks


def _swa_fwd(proj, q_gain, k_gain, sinks, bias):
    S = proj.shape[0]

    def body(q_ref, kp_ref, kc_ref, vp_ref, vc_ref, qg_ref, kg_ref, s_ref, bias_ref, y_ref):
        q, kband, vband, sk = _swa_load(q_ref, kp_ref, kc_ref, vp_ref, vc_ref, s_ref)
        out = _swa_block(q, kband, vband, qg_ref[...], kg_ref[...], sk, bias_ref[...])
        for h in range(SWA_HEADS):
            y_ref[:, pl.ds(h * SWA_DIM, SWA_DIM)] = out[h].astype(BF16)

    return pl.pallas_call(
        body, grid=(S // BLOCK,), in_specs=_swa_in_specs(),
        out_specs=pl.BlockSpec((BLOCK, SWA_WIDTH), lambda n: (n, 0)),
        out_shape=jax.ShapeDtypeStruct((S, SWA_WIDTH), BF16), name="swa_fwd",
        compiler_params=_params(("parallel",)))(proj, proj, proj, proj, proj, q_gain, k_gain, sinks, bias)


def _swa_bwd(proj, q_gain, k_gain, sinks, bias, dy, dproj):
    S = proj.shape[0]

    def body(q_ref, kp_ref, kc_ref, vp_ref, vc_ref, qg_ref, kg_ref, s_ref, bias_ref, dy_ref, _,
             dq_ref, dk_ref, dv_ref, dqg_ref, dkg_ref, ds_ref, dbias_ref):
        n = pl.program_id(0)

        @pl.when(n == 0)
        def _():
            for r in (dk_ref, dv_ref, dqg_ref, dkg_ref, ds_ref, dbias_ref):
                r[...] = jnp.zeros_like(r)

        cur = pl.ds(pl.multiple_of(n * BLOCK, BLOCK), BLOCK)
        prev = pl.ds(pl.multiple_of(jnp.maximum(n - 1, 0) * BLOCK, BLOCK), BLOCK)
        q, kband, vband, sk = _swa_load(q_ref, kp_ref, kc_ref, vp_ref, vc_ref, s_ref)
        _, vjp = jax.vjp(_swa_block, q, kband, vband, qg_ref[...], kg_ref[...], sk, bias_ref[...])
        dy = jnp.stack([dy_ref[:, pl.ds(h * SWA_DIM, SWA_DIM)] for h in range(SWA_HEADS)])
        dq, dkb, dvb, dqg, dkg, dsk, dbs = vjp(dy)
        for h in range(SWA_HEADS):
            dq_ref[:, pl.ds(h * SWA_DIM, SWA_DIM)] = dq[h].astype(BF16)
            ds_ref[:, pl.ds(h, 1)] += dsk[h]
        dbias_ref[...] += dbs
        dqg_ref[...] += dqg
        dkg_ref[...] += dkg
        for kv in range(SWA_KV):
            cols = pl.ds(kv * SWA_DIM, SWA_DIM)
            group = range(kv * SWA_GROUP, (kv + 1) * SWA_GROUP)
            dk_kv = sum(dkb[h] for h in group)
            dv_kv = sum(dvb[h] for h in group)
            dk_ref[cur, cols] += dk_kv[BLOCK:]
            dv_ref[cur, cols] += dv_kv[BLOCK:]

            @pl.when(n > 0)
            def _(cols=cols, dk_kv=dk_kv, dv_kv=dv_kv):
                dk_ref[prev, cols] += dk_kv[:BLOCK]
                dv_ref[prev, cols] += dv_kv[:BLOCK]

    return pl.pallas_call(
        body, grid=(S // BLOCK,),
        in_specs=_swa_in_specs() + [pl.BlockSpec((BLOCK, SWA_WIDTH), lambda n: (n, 0)),
                                    pl.BlockSpec(memory_space=pl.ANY)],
        out_specs=[pl.BlockSpec((BLOCK, SWA_WIDTH), lambda n: (n, P_SQ // SWA_WIDTH)), _full((S, SWA_KVW)),
                   _full((S, SWA_KVW)), _full((1, SWA_DIM)), _full((1, SWA_DIM)), _full((1, SWA_HEADS)),
                   _full((SWA_HEADS, BLOCK, 2 * BLOCK))],
        out_shape=[jax.ShapeDtypeStruct(dproj.shape, dproj.dtype), jax.ShapeDtypeStruct((S, SWA_KVW), F32),
                   jax.ShapeDtypeStruct((S, SWA_KVW), F32), jax.ShapeDtypeStruct((1, SWA_DIM), F32),
                   jax.ShapeDtypeStruct((1, SWA_DIM), F32), jax.ShapeDtypeStruct((1, SWA_HEADS), F32),
                   jax.ShapeDtypeStruct((SWA_HEADS, BLOCK, 2 * BLOCK), F32)],
        input_output_aliases={10: 0},
        name="swa_bwd", compiler_params=_params(("arbitrary",)),
    )(proj, proj, proj, proj, proj, q_gain, k_gain, sinks, bias, dy, dproj)


def _kv_into(dproj, dk, dv, tm=1024):
    S = dk.shape[0]

    def body(dk_ref, dv_ref, _, o_ref):
        o_ref[:, :SWA_KVW] = dk_ref[...].astype(BF16)
        o_ref[:, SWA_KVW:] = dv_ref[...].astype(BF16)

    return pl.pallas_call(
        body, grid=(S // tm,), in_specs=[_row(tm, SWA_KVW), _row(tm, SWA_KVW), pl.BlockSpec(memory_space=pl.ANY)],
        out_specs=_row(tm, 2 * SWA_KVW, P_SK // (2 * SWA_KVW)),
        out_shape=jax.ShapeDtypeStruct(dproj.shape, dproj.dtype), input_output_aliases={2: 0},
        name="swa_kv_into", compiler_params=_params(("parallel",)))(dk, dv, dproj)


def _position():
    return lax.axis_index("x"), lax.axis_index("y"), lax.axis_index("c")


_HBM = pl.BlockSpec(memory_space=pltpu.HBM)
_SEM = pl.BlockSpec(memory_space=pltpu.SEMAPHORE)
_DATAFLOW = pltpu.SideEffectType.DATAFLOW_SIDE_EFFECTING


def _two_level_copies(x_refs, out_refs, send_sems, recv_sems):
    x, y, c = _position()
    me, sibling = (x, y, c), (x, y, 1 - c)
    chips = [(1 - x, y), (x, 1 - y), (1 - x, 1 - y)]

    def copy(a, k, block, to, own=False):
        px, py, pc = block
        slot = out_refs[a].at[4 * px + 2 * py + pc]
        return pltpu.make_async_remote_copy(
            src_ref=x_refs[a] if own else slot, dst_ref=slot, send_sem=send_sems.at[7 * a + k],
            recv_sem=recv_sems.at[7 * a + k], device_id=to, device_id_type=MESH_ID)

    return copy, me, sibling, chips


def _all_gather_start(shards, name):
    na = len(shards)
    lands = [lax.empty((N_DEV,) + s.shape, s.dtype) for s in shards]

    def body(*refs):
        x_refs, out_refs = refs[:na], refs[na:2 * na]
        send_sems, recv_sems = refs[2 * na], refs[2 * na + 1]
        token = refs[-1]
        copy, me, sibling, chips = _two_level_copies(x_refs, out_refs, send_sems, recv_sems)
        for a in range(na):
            copy(a, 0, me, sibling, own=True).start()
            for j, chip in enumerate(chips):
                copy(a, 1 + j, me, (*chip, me[2]), own=True).start()
        token[...] = jnp.zeros_like(token)

    hbm = lambda a: pltpu.HBM(a.shape, a.dtype)
    out = pl.pallas_call(
        body, name=name,
        out_shape=(pltpu.SemaphoreType.DMA((7 * na,)), pltpu.SemaphoreType.DMA((7 * na,)),
                   *[hbm(s) for s in shards], *[hbm(l) for l in lands], jax.ShapeDtypeStruct((1, D_MODEL), F32)),
        in_specs=[_HBM] * (2 * na),
        out_specs=(_SEM, _SEM, *[_HBM] * (2 * na), pl.BlockSpec(memory_space=pltpu.VMEM)),
        input_output_aliases={i: 2 + i for i in range(2 * na)},
        compiler_params=pltpu.CompilerParams(has_side_effects=_DATAFLOW),
    )(*[pltpu.with_memory_space_constraint(s, pltpu.HBM) for s in shards],
      *[pltpu.with_memory_space_constraint(l, pltpu.HBM) for l in lands])
    return (out[0], out[1], list(out[2:2 + na]), list(out[2 + na:2 + 2 * na])), out[-1]


def _all_gather_finish(handle, after, name):
    send_sems, recv_sems, srcs, lands = handle
    na = len(srcs)

    def body(*refs):
        x_refs, out_refs = refs[:na], refs[na:2 * na]
        copy, me, sibling, chips = _two_level_copies(x_refs, out_refs, refs[2 * na], refs[2 * na + 1])
        c = me[2]
        for a in range(na):
            copy(a, 0, sibling, me).wait_recv()
            copy(a, 0, me, sibling, own=True).wait_send()
            for j, chip in enumerate(chips):
                copy(a, 1 + j, (*chip, c), me).wait_recv()
                copy(a, 1 + j, me, (*chip, c), own=True).wait_send()

    hbm = lambda a: pltpu.HBM(a.shape, a.dtype)
    out = pl.pallas_call(
        body, name=name, out_shape=(*[hbm(s) for s in srcs], *[hbm(l) for l in lands]),
        in_specs=[_HBM] * (2 * na) + [_SEM, _SEM] + [pl.BlockSpec(memory_space=pl.ANY)] * len(after),
        out_specs=tuple([_HBM] * (2 * na)), input_output_aliases={i: i for i in range(2 * na)},
        compiler_params=pltpu.CompilerParams(has_side_effects=_DATAFLOW),
    )(*srcs, *lands, send_sems, recv_sems, *after)
    lands = _all_gather_relay(list(out[na:]), name + "_relay")
    return [_own_slot(land, src) for land, src in zip(lands, out[:na])]


def _all_gather_relay(lands, name):
    na = len(lands)

    def body(*refs):
        out_refs = refs[na:2 * na]
        send_sems, recv_sems = refs[2 * na:]
        x, y, c = _position()
        chips = [(1 - x, y), (x, 1 - y), (1 - x, 1 - y)]

        def copy(a, j, core):
            px, py = chips[j]
            slot = out_refs[a].at[4 * px + 2 * py + core]
            return pltpu.make_async_remote_copy(
                src_ref=slot, dst_ref=slot, send_sem=send_sems.at[3 * a + j], recv_sem=recv_sems.at[3 * a + j],
                device_id=(x, y, 1 - c), device_id_type=MESH_ID)

        sends = [copy(a, j, c) for a in range(na) for j in range(3)]
        for cp in sends:
            cp.start()
        for a in range(na):
            for j in range(3):
                copy(a, j, 1 - c).wait_recv()
        for cp in sends:
            cp.wait_send()

    return pl.pallas_call(
        body, in_specs=[pl.BlockSpec(memory_space=pl.ANY)] * na, out_specs=[pl.BlockSpec(memory_space=pl.ANY)] * na,
        out_shape=[jax.ShapeDtypeStruct(l.shape, l.dtype) for l in lands],
        input_output_aliases={i: i for i in range(na)},
        scratch_shapes=[pltpu.SemaphoreType.DMA((3 * na,)), pltpu.SemaphoreType.DMA((3 * na,))],
        name=name)(*lands)


def _peers(x, y, c):
    out = []
    for k in range(1, N_DEV):
        px, py, pc = x ^ (k >> 2), y ^ ((k >> 1) & 1), c ^ (k & 1)
        out.append(((px, py, pc), 4 * px + 2 * py + pc))
    return out


def _split_copies(src_refs, land_refs, send_sems, recv_sems, scatter):
    x, y, c = _position()
    me = 4 * x + 2 * y + c
    sends, recvs = [], []
    for k, (peer_id, peer) in enumerate(_peers(x, y, c)):
        for a, (src, land) in enumerate(zip(src_refs, land_refs)):
            sems = dict(send_sem=send_sems.at[7 * a + k], recv_sem=recv_sems.at[7 * a + k],
                        device_id=peer_id, device_id_type=MESH_ID)
            mine = src.at[peer] if scatter else src
            sends.append(pltpu.make_async_remote_copy(src_ref=mine, dst_ref=land.at[me], **sems))
            recvs.append(pltpu.make_async_remote_copy(src_ref=mine, dst_ref=land.at[peer], **sems))
    return sends, recvs


def _all_gather_direct(shards, name, after):
    na, nb = len(shards), len(after)

    def body(*refs):
        x_refs, out_refs = refs[:na], refs[na + nb:2 * na + nb]
        send_sems, recv_sems, local_sems = refs[2 * na + nb:]
        x, y, c = _position()
        me = 4 * x + 2 * y + c
        local = [pltpu.make_async_copy(x_refs[a], out_refs[a].at[me], local_sems.at[a]) for a in range(na)]
        sends, recvs = _split_copies(x_refs, out_refs, send_sems, recv_sems, False)
        for cp in local + sends:
            cp.start()
        for cp in recvs:
            cp.wait_recv()
        for cp in sends:
            cp.wait_send()
        for cp in local:
            cp.wait()

    return pl.pallas_call(
        body, in_specs=[pl.BlockSpec(memory_space=pl.ANY)] * (na + nb),
        out_specs=[pl.BlockSpec(memory_space=pl.ANY)] * na,
        out_shape=[jax.ShapeDtypeStruct((N_DEV,) + s.shape, s.dtype) for s in shards],
        scratch_shapes=[pltpu.SemaphoreType.DMA((7 * na,)), pltpu.SemaphoreType.DMA((7 * na,)),
                        pltpu.SemaphoreType.DMA((na,))],
        name=name)(*shards, *after)


def _exchange_start(srcs, scatter, name, after=None):
    na = len(srcs)
    lands = [lax.empty(s.shape if scatter else (N_DEV,) + s.shape, s.dtype) for s in srcs]
    extra = [] if after is None else [after]

    def body(*refs):
        src_refs, land_refs = refs[:na], refs[na:2 * na]
        send_sems, recv_sems = refs[2 * na + len(extra)], refs[2 * na + len(extra) + 1]
        token = refs[-1]
        sends, _ = _split_copies(src_refs, land_refs, send_sems, recv_sems, scatter)
        for cp in sends:
            cp.start()
        token[...] = jnp.zeros_like(token)

    hbm = lambda a: pltpu.HBM(a.shape, a.dtype)
    out = pl.pallas_call(
        body, name=name,
        out_shape=(pltpu.SemaphoreType.DMA((7 * na,)), pltpu.SemaphoreType.DMA((7 * na,)),
                   *[hbm(s) for s in srcs], *[hbm(l) for l in lands], jax.ShapeDtypeStruct((1, D_MODEL), F32)),
        in_specs=[_HBM] * (2 * na) + [pl.BlockSpec(memory_space=pl.ANY)] * len(extra),
        out_specs=(_SEM, _SEM, *[_HBM] * (2 * na), pl.BlockSpec(memory_space=pltpu.VMEM)),
        input_output_aliases={i: 2 + i for i in range(2 * na)},
        compiler_params=pltpu.CompilerParams(has_side_effects=_DATAFLOW),
    )(*[pltpu.with_memory_space_constraint(s, pltpu.HBM) for s in srcs],
      *[pltpu.with_memory_space_constraint(l, pltpu.HBM) for l in lands], *extra)
    return (out[0], out[1], list(out[2:2 + na]), list(out[2 + na:2 + 2 * na])), out[-1]


def _exchange_wait(handle, after, scatter, name):
    send_sems, recv_sems, srcs, lands = handle
    na = len(srcs)

    def body(*refs):
        src_refs, land_refs = refs[:na], refs[na:2 * na]
        s_sems, r_sems = refs[2 * na], refs[2 * na + 1]
        sends, recvs = _split_copies(src_refs, land_refs, s_sems, r_sems, scatter)
        for cp in sends:
            cp.wait_send()
        for cp in recvs:
            cp.wait_recv()

    hbm = lambda a: pltpu.HBM(a.shape, a.dtype)
    out = pl.pallas_call(
        body, name=name, out_shape=(*[hbm(s) for s in srcs], *[hbm(l) for l in lands]),
        in_specs=[_HBM] * (2 * na) + [_SEM, _SEM, pl.BlockSpec(memory_space=pl.ANY)],
        out_specs=tuple([_HBM] * (2 * na)), input_output_aliases={i: i for i in range(2 * na)},
        compiler_params=pltpu.CompilerParams(has_side_effects=_DATAFLOW),
    )(*srcs, *lands, send_sems, recv_sems, after)
    return list(out[:na]), list(out[na:])


def _own_slot(landed, own):
    me = 4 * lax.axis_index("x") + 2 * lax.axis_index("y") + lax.axis_index("c")
    return lax.dynamic_update_slice_in_dim(landed, own[None], me, axis=0)


def _adam_update(parts, w, m, v, name, tr=256, turned=False):
    _, r, c = w.shape
    tr = _pick_rows(r, tr)
    cp = parts.shape[2]
    flat = turned and c % 8 != 0
    at = (slice(None), 0) if flat else (0,)

    def body(p_ref, w_ref, m_ref, v_ref, g_ref, d_ref, nm_ref, nv_ref):
        cols = pl.ds(0, cp if turned else c)
        g = p_ref[0, :, cols].astype(F32)
        for i in range(1, N_DEV):
            g = g + p_ref[i, :, cols].astype(F32)
        if turned:
            g = g.T[:c]
        delta, nm, nv = _adamw(w_ref[at], g, m_ref[at], v_ref[at])
        g_ref[at] = g
        d_ref[at] = delta
        nm_ref[at] = nm
        nv_ref[at] = nv

    there, back = ((2, 0, 1), (1, 2, 0)) if flat else ((0, 2, 1), (0, 2, 1))
    if turned:
        w, m, v = (jnp.transpose(a, there) for a in (w, m, v))
        rs = pl.BlockSpec((c, 1, tr), lambda i: (0, 0, i)) if flat else pl.BlockSpec((1, c, tr), lambda i: (0, 0, i))
    else:
        rs = pl.BlockSpec((1, tr, c), lambda i: (0, i, 0))
    outs = pl.pallas_call(
        body, grid=(r // tr,), in_specs=[pl.BlockSpec((N_DEV, tr, cp), lambda i: (0, i, 0)), rs, rs, rs],
        out_specs=[rs] * 4, out_shape=[jax.ShapeDtypeStruct(w.shape, F32)] * 4, name=name,
        compiler_params=_params(("parallel",)))(parts, w, m, v)
    return [*([jnp.transpose(o, back) for o in outs] if turned else outs), outs[0]]


def _pick_rows(rows, target):
    if rows <= target:
        return rows
    t = target
    while t >= 16:
        if rows % t == 0:
            return t
        t -= 16
    return rows


BIG = ("w_in", "w_branch_dn", "w_branch_swa", "w_out", "w_gate", "w_up", "w_down")
IN_SHARD, IN_WIRE = D_IN // N_DEV, 640
FF_SHARD, FF_WIRE = D_FF // N_DEV, 384
D_FFP = N_DEV * FF_WIRE
BIG_SHAPES = {"w_in": ((D_MODEL, IN_SHARD), (D_MODEL, IN_WIRE)),
              "w_branch_dn": ((DN_WIDTH, LANES), (DN_WIDTH, LANES)),
              "w_branch_swa": ((SWA_WIDTH, LANES), (SWA_WIDTH, LANES)),
              "w_out": ((LANES, D_MODEL), (LANES, D_MODEL)),
              "w_gate": ((D_MODEL, FF_SHARD), (D_MODEL, FF_WIRE)),
              "w_up": ((D_MODEL, FF_SHARD), (D_MODEL, FF_WIRE)),
              "w_down": ((FF_SHARD, D_MODEL), (FF_WIRE, D_MODEL))}
CONV_SHARD, CONV_WIRE = (DN_CONV, DN_QKV // N_DEV), (8, 256)


def _pad_to(a, shape):
    return jnp.pad(a, [(0, t - s) for s, t in zip(a.shape, shape)])


IN_TILE_ROWS = 256
_IN_SEGS = ((R_GATE, 2048, P_GATE), (R_QKV, DN_QKV, P_QKV), (R_Z, DN_WIDTH, P_Z), (R_SQ, SWA_WIDTH, P_SQ),
            (R_SK, SWA_KVW, P_SK), (R_SV, SWA_KVW, P_SV), (R_B, 8, P_BA))


def _w_in_from_blocks(blocks, after):
    tm = IN_TILE_ROWS

    def body(b_ref, _, o_ref):
        parts = []
        for rs, n, _ in _IN_SEGS:
            for dev in range(N_DEV):
                lo, hi = max(rs, IN_SHARD * dev), min(rs + n, IN_SHARD * (dev + 1))
                if lo < hi:
                    parts.append(b_ref[dev][:, lo - IN_SHARD * dev:hi - IN_SHARD * dev])
        parts.append(jnp.zeros((tm, P_WIDTH - P_BA - 8), b_ref.dtype))
        o_ref[...] = jnp.concatenate(parts, axis=1)

    return pl.pallas_call(
        body, grid=(D_MODEL // tm,),
        in_specs=[pl.BlockSpec((N_DEV, tm, IN_WIRE), lambda i: (0, i, 0)), pl.BlockSpec(memory_space=pl.ANY)],
        out_specs=pl.BlockSpec((tm, P_WIDTH), lambda i: (i, 0)),
        out_shape=jax.ShapeDtypeStruct((D_MODEL, P_WIDTH), blocks.dtype), name="w_in_from_blocks",
        compiler_params=_params(("parallel",)))(blocks, after)


def _w_in_to_blocks(g):
    tm = IN_TILE_ROWS

    def body(g_ref, o_ref):
        for dev in range(N_DEV):
            parts = []
            for rs, n, ps in sorted(_IN_SEGS):
                lo, hi = max(rs, IN_SHARD * dev), min(rs + n, IN_SHARD * (dev + 1))
                if lo < hi:
                    parts.append(g_ref[:, ps + lo - rs:ps + hi - rs])
            parts.append(jnp.zeros((tm, IN_WIRE - IN_SHARD), g_ref.dtype))
            o_ref[dev] = jnp.concatenate(parts, axis=1)

    return pl.pallas_call(
        body, grid=(D_MODEL // tm,), in_specs=[pl.BlockSpec((tm, P_WIDTH), lambda i: (i, 0))],
        out_specs=pl.BlockSpec((N_DEV, tm, IN_WIRE), lambda i: (0, i, 0)),
        out_shape=jax.ShapeDtypeStruct((N_DEV, D_MODEL, IN_WIRE), g.dtype), name="w_in_to_blocks",
        compiler_params=_params(("parallel",)))(g)


SMALL = {"attn_norm": (0, (1, D_MODEL)), "ffn_norm": (1, (1, D_MODEL)), "dn_out_norm": (2, (1, DN_DIM)),
         "swa_q_norm": (3, (1, SWA_DIM)), "swa_k_norm": (4, (1, SWA_DIM)), "dn_a_log": (5, (1, DN_HEADS)),
         "dn_dt_bias": (6, (1, DN_HEADS)), "swa_sinks": (7, (1, SWA_HEADS)), "rel_bias": (8, (REL_BUCKETS, SWA_HEADS))}
SMALL_SHEET = (48, D_MODEL)


LOSS_ROW = 40


def _small_pack(grads, loss_local):
    names = list(SMALL)

    def body(*refs):
        o_ref = refs[-1]
        o_ref[...] = jnp.zeros_like(o_ref)
        for n, ref in zip(names, refs):
            r0, (nr, nc) = SMALL[n]
            o_ref[r0:r0 + nr, 0:nc] = ref[...]
        o_ref[LOSS_ROW:LOSS_ROW + 1, 0:1] = refs[len(names)][...]

    return pl.pallas_call(
        body, in_specs=[pl.BlockSpec(memory_space=pltpu.VMEM)] * (len(names) + 1),
        out_specs=pl.BlockSpec(memory_space=pltpu.VMEM), out_shape=jax.ShapeDtypeStruct(SMALL_SHEET, F32),
        name="small_pack", compiler_params=_params())(*[grads[n].reshape(SMALL[n][1]) for n in names], loss_local)


def _small_update(sheets, w, m, v):
    names = list(SMALL)
    k = len(names)

    def body(*refs):
        p_ref = refs[0]
        ins, outs = refs[1:1 + 3 * k], refs[1 + 3 * k:]
        loss = p_ref[0, LOSS_ROW:LOSS_ROW + 1, 0:1]
        for i in range(1, N_DEV):
            loss = loss + p_ref[i, LOSS_ROW:LOSS_ROW + 1, 0:1]
        outs[4 * k][...] = loss
        for t, n in enumerate(names):
            r0, (nr, nc) = SMALL[n]
            g = p_ref[0, r0:r0 + nr, 0:nc]
            for i in range(1, N_DEV):
                g = g + p_ref[i, r0:r0 + nr, 0:nc]
            delta, nm, nv = _adamw(ins[t][...], g, ins[k + t][...], ins[2 * k + t][...])
            for kind, val in enumerate((g, delta, nm, nv)):
                outs[kind * k + t][...] = val

    shapes = [jax.ShapeDtypeStruct(SMALL[n][1], F32) for n in names]
    vm = pl.BlockSpec(memory_space=pltpu.VMEM)
    res = pl.pallas_call(
        body, in_specs=[vm] * (1 + 3 * k), out_specs=[vm] * (4 * k + 1),
        out_shape=shapes * 4 + [jax.ShapeDtypeStruct((1, 1), F32)], name="adam_small", compiler_params=_params(),
    )(sheets, *[d[n].reshape(SMALL[n][1]) for d in (w, m, v) for n in names])
    return {n: tuple(res[kind * k + t] for kind in range(4)) for t, n in enumerate(names)}, res[4 * k]


def kernel(x, attn_norm, w_in, dn_conv, dn_a_log, dn_dt_bias, dn_out_norm, swa_q_norm, swa_k_norm, swa_sinks, rel_bias, w_branch_dn, w_branch_swa, w_out, ffn_norm, w_gate, w_up, w_down, loss_target, m_attn_norm, m_w_in, m_dn_conv, m_dn_a_log, m_dn_dt_bias, m_dn_out_norm, m_swa_q_norm, m_swa_k_norm, m_swa_sinks, m_rel_bias, m_w_branch_dn, m_w_branch_swa, m_w_out, m_ffn_norm, m_w_gate, m_w_up, m_w_down, v_attn_norm, v_w_in, v_dn_conv, v_dn_a_log, v_dn_dt_bias, v_dn_out_norm, v_swa_q_norm, v_swa_k_norm, v_swa_sinks, v_rel_bias, v_w_branch_dn, v_w_branch_swa, v_w_out, v_ffn_norm, v_w_gate, v_w_up, v_w_down):
    args = dict(locals())
    S = x.shape[1]
    xs = x.reshape(S, D_MODEL)
    target = loss_target.reshape(S, D_MODEL)

    w_loc = {n: args[n].reshape(BIG_SHAPES[n][0]) for n in BIG}
    conv_loc = dn_conv.reshape(CONV_SHARD)
    def on_wire(n, zero=0.0):
        return _pad_to(w_loc[n] + zero, BIG_SHAPES[n][1]).astype(BF16)

    first_handle, first_token = _all_gather_start([on_wire("w_in"), _pad_to(conv_loc, CONV_WIRE)],
                                                  "all_gather_weights_start")
    zero = first_token[0, 0]
    h = _norm_fwd(xs, attn_norm + first_token, "norm1_fwd")
    later = [n for n in BIG if n != "w_in"]
    wire = [on_wire(n, zero) for n in later]
    bias = _bias_fwd(rel_bias + zero)
    first = _all_gather_finish(first_handle, [h, bias] + wire, "all_gather_weights_finish")
    rest_handle, rest_token = _exchange_start(wire, False, "gather_rest_start", after=first[1])
    w_pad = _w_in_from_blocks(first[0], rest_token)
    conv_w = jnp.concatenate([first[1][d, :DN_CONV, :CONV_SHARD[1]] for d in range(N_DEV)], axis=1)

    proj = _mm([(h, w_pad)], "nn", F32, "mm_in", 1024, 1664, j_outer=True)
    qkvn, conv_out = _dn_conv_fwd(proj, conv_w)
    beta, g = _dn_gate_fwd(proj, dn_a_log, dn_dt_bias)
    u, w, qe, kd, qk, egl, tinv = _dn_prep_fwd(qkvn, g, beta)
    o, states = _dn_scan_fwd(u, w, qe, kd, qk, egl)
    y_dn = _dn_out_fwd(o, proj, dn_out_norm)
    y_swa = _swa_fwd(proj, swa_q_norm, swa_k_norm, swa_sinks, bias)
    rest_src, rest_land = _exchange_wait(rest_handle, y_swa, False, "gather_rest_wait")
    G = {n: _own_slot(land, src) for n, src, land in zip(later, rest_src, rest_land)}
    w_bdn, w_bswa, w_g, w_u = G["w_branch_dn"], G["w_branch_swa"], G["w_gate"], G["w_up"]
    w_o = G["w_out"].reshape(D_MODEL, D_MODEL)
    w_d = G["w_down"].reshape(D_FFP, D_MODEL)
    gates = [(proj, P_GATE // 512), (proj, (P_GATE + D_MODEL) // 512)]
    a_dn, a_swa, merged = _mm_fused(
        [(y_dn, w_bdn), (y_swa, w_bswa)], "nn", "mm_branch_merge", 1024, 512,
        lambda p, e: (p[0], p[1], _merge(e[0], e[1], p[0], p[1])), gates, (F32, F32, BF16), b_blocks=True)

    def resid_norm(p, e):
        x1 = e[0] + p[0]
        return x1, _rms(x1, e[1])

    x1, h2 = _mm_fused([(merged, w_o)], "nn", "mm_out_norm", 512, D_MODEL, resid_norm,
                       [(xs, 0), (ffn_norm, None)], (F32, BF16))
    gate, up, act = _mm_fused([(h2, w_g), (h2, w_u)], "nn", "mm_gate_up_act", 1024, 768,
                              lambda p, e: (p[0], p[1], _act(p[0], p[1])), [], (F32, F32, BF16),
                              j_outer=True, b_blocks=True)

    def loss_head(p, e):
        diff = e[0] + p[0] - e[1]
        dy = diff * (1.0 / D_MODEL)
        part = jnp.sum(jnp.mean(diff * diff, axis=-1, keepdims=True), axis=0, keepdims=True) * 0.5
        return dy, dy, part

    dy, dy_b, loss_local = _mm_fused([(act, w_d)], "nn", "mm_down_loss", 512, D_MODEL, loss_head,
                                     [(x1, 0), (target, 0)], (F32, BF16), sum_shape=(1, 1))

    def act_bwd(p, e):
        _, vjp = jax.vjp(_act, e[0], e[1])
        return vjp(p[0])

    dgate, dup = _mm_fused([(dy_b, w_d)], "nt", "mm_dact_act", 1024, 768, act_bwd, [(gate, 0), (up, 0)],
                           (BF16, BF16), j_outer=True)
    g_w_down = _mm([(act, dy_b)], "tn", BF16, "mm_dw_down", 768, D_MODEL, j_outer=True)
    g_w_down = g_w_down.reshape(N_DEV, FF_WIRE, D_MODEL)
    g_w_gate = _mm([(h2, dgate)], "tn", BF16, "mm_dw_gate", D_MODEL, 768, out_blocks=True)
    g_w_up = _mm([(h2, dup)], "tn", BF16, "mm_dw_up", D_MODEL, 768, out_blocks=True)
    ffn_handle, ffn_token = _exchange_start([g_w_down, g_w_gate, g_w_up], True, "scatter_ffn_start")

    def norm_bwd(p, e):
        _, vjp = jax.vjp(_rms, e[0], e[2])
        dx, dgain = vjp(sum(p))
        dx = dx + e[1]
        return dx, dx, dgain

    dx1, dx1_b, g_ffn_norm = _mm_fused(
        [(dgate, w_g), (dup, w_u)], "nt", "mm_dh2_norm", 256, D_MODEL, norm_bwd,
        [(x1, 0), (dy, 0), (ffn_norm, None), (ffn_token, None)], (F32, BF16), b_blocks=True, sum_shape=(1, D_MODEL))
    def merge_bwd(p, e):
        _, vjp = jax.vjp(_merge, *e)
        dg0, dg1, da_dn, da_swa = vjp(p[0])
        return jnp.concatenate([dg0, dg1], axis=1), da_dn, da_swa

    dproj, da_dn, da_swa = _mm_fused(
        [(dx1_b, w_o)], "nt", "mm_dmerged_merge", 512, D_MODEL, merge_bwd,
        [(proj, P_GATE // D_MODEL), (proj, P_GATE // D_MODEL + 1), (a_dn, 0), (a_swa, 0)], (BF16,) * 3,
        wide_first=(P_WIDTH, 2 * D_MODEL))
    g_w_out = _mm([(merged, dx1_b)], "tn", BF16, "mm_dw_out", 512, D_MODEL, j_outer=True)
    g_w_out = g_w_out.reshape(N_DEV, LANES, D_MODEL)
    dy_dn = _mm([(da_dn, w_bdn)], "nt", F32, "mm_dy_dn", 1024, DN_WIDTH, b_blocks=True)
    dy_swa = _mm([(da_swa, w_bswa)], "nt", F32, "mm_dy_swa", 1024, SWA_WIDTH, b_blocks=True)
    g_w_bdn = _mm([(y_dn, da_dn)], "tn", BF16, "mm_dw_branch_dn", DN_WIDTH, 512, out_blocks=True)
    g_w_bswa = _mm([(y_swa, da_swa)], "tn", BF16, "mm_dw_branch_swa", SWA_WIDTH, 512, out_blocks=True)
    dproj, dsk, dsv, g_q_norm, g_k_norm, g_sinks, dbias = _swa_bwd(proj, swa_q_norm, swa_k_norm, swa_sinks, bias,
                                                                   dy_swa, dproj)
    dproj = _kv_into(dproj, dsk, dsv)
    g_rel_bias = _bias_bwd(dbias)[:, :REL_BUCKETS].T
    mix_handle, mix_token = _exchange_start([g_w_out, g_w_bdn, g_w_bswa], True, "scatter_mix_start")
    do, dproj, g_out_norm = _dn_out_bwd(o, proj, dn_out_norm + mix_token[:, :DN_DIM], dy_dn, dproj)
    du, dw, dqe, dkd, dqk, degl = _dn_scan_bwd(u, w, qe, kd, qk, egl, states, do)
    dqkvn, dgd, dbeta = _dn_prep_bwd(qkvn, g, beta, tinv, du, dw, dqe, dkd, dqk, degl)
    dproj, dal, ddt = _dn_gate_bwd(proj, dn_a_log, dn_dt_bias, dbeta, dgd, dproj)
    g_a_log = dal.reshape(DN_HEADS, DN_DIM).sum(axis=1)
    g_dt_bias = ddt[0, DN_HEADS:2 * DN_HEADS]
    dproj, g_conv = _dn_conv_bwd(proj, conv_w, conv_out, dqkvn, dproj)
    g_w_in = _w_in_to_blocks(_mm([(h, dproj)], "tn", BF16, "mm_dw_in", 512, 1664, j_outer=True))
    in_handle, in_token = _exchange_start([g_w_in], True, "scatter_in_start")
    dx, g_attn_norm = _mm_fused(
        [(dproj, w_pad)], "nt", "mm_dh_norm", 512, D_MODEL, lambda p, e: norm_bwd(p, e)[1:],
        [(xs, 0), (dx1, 0), (attn_norm, None), (in_token, None)], (F32,), sum_shape=(1, D_MODEL))

    g_small = {"attn_norm": g_attn_norm, "ffn_norm": g_ffn_norm, "rel_bias": g_rel_bias, "dn_out_norm": g_out_norm,
               "swa_q_norm": g_q_norm, "swa_k_norm": g_k_norm, "dn_a_log": g_a_log, "dn_dt_bias": g_dt_bias,
               "swa_sinks": g_sinks}
    me = 4 * lax.axis_index("x") + 2 * lax.axis_index("y") + lax.axis_index("c")
    outs = {}

    def finish(handle, group, name, after):
        srcs, lands = _exchange_wait(handle, after, True, name)
        for n, src, land in zip(group, srcs, lands):
            parts = _own_slot(land, lax.dynamic_index_in_dim(src, me, 0, keepdims=False))
            outs[n] = _adam_update(parts, args[n], args["m_" + n], args["v_" + n], "adam_" + n,
                                   turned=args[n].shape[2] % LANES != 0)

    finish(ffn_handle, ("w_down", "w_gate", "w_up"), "scatter_ffn_wait", dx)
    finish(mix_handle, ("w_out", "w_branch_dn", "w_branch_swa"), "scatter_mix_wait", dx)
    sheets, conv_all = _all_gather_direct([_small_pack(g_small, loss_local), _pad_to(g_conv, (8, DN_QKV))],
                                          "all_gather_small",
                                          after=[outs[n][4] for n in sorted(outs)])
    finish(in_handle, ("w_in",), "scatter_in_wait", sheets)
    conv_parts = lax.dynamic_slice(conv_all, (0, 0, me * CONV_SHARD[1]), (N_DEV,) + CONV_SHARD)
    outs["dn_conv"] = _adam_update(conv_parts, dn_conv, m_dn_conv, v_dn_conv, "adam_dn_conv")
    small_outs, loss = _small_update(sheets, {n: args[n] for n in SMALL}, {n: args["m_" + n] for n in SMALL},
                                     {n: args["v_" + n] for n in SMALL})
    outs.update(small_outs)

    names = ("attn_norm", "w_in", "dn_conv", "dn_a_log", "dn_dt_bias", "dn_out_norm", "swa_q_norm", "swa_k_norm",
             "swa_sinks", "rel_bias", "w_branch_dn", "w_branch_swa", "w_out", "ffn_norm", "w_gate", "w_up", "w_down")
    results = []
    for kind in range(4):
        results += [outs[n][kind].reshape(args[n].shape) for n in names]

    return (loss.reshape(()), dx.reshape(x.shape), *results)
```
